```python
import jax, jax.numpy as jnp
from jax import lax
import numpy as np

D_MODEL = 1024
BATCH = 8
SEQ = 2048
DEPTH = 2

CTX_LEN = 256
GRID_W = 64
N_MIXERS = 2
HGRN_HEADS = 8
HGRN_HEAD_DIM = D_MODEL // HGRN_HEADS
HGRN_WIDTH = HGRN_HEADS * HGRN_HEAD_DIM
CHUNK = 64
POOL_WINDOWS = (2, 4, 8, 16)
POOL_WIDTH = D_MODEL
POOL_GROUPS = len(POOL_WINDOWS)
POOL_GROUP_DIM = POOL_WIDTH // POOL_GROUPS
N_HGRN_LAYERS = (DEPTH + 1) // 2
N_POOL_LAYERS = DEPTH // 2
EPS = 1e-6

kernel_name = 'hybrid_hgrn2_pool_prefix_dit'


def rmsnorm(x, w):
    xf = x.astype(jnp.float32)
    y = xf * lax.rsqrt(jnp.mean(xf * xf, axis=-1, keepdims=True) + EPS)
    return (y * w.astype(jnp.float32)).astype(x.dtype)


def _adaln(cond, w, b):
    mod = jax.nn.silu(cond) @ w + b
    return jnp.split(mod[:, None, :], 3, axis=-1)


def lower_bounds(logits):
    lg = jnp.concatenate([logits, jnp.zeros_like(logits[:1])], axis=0).astype(jnp.float32)
    p = jax.nn.softmax(lg, axis=0)
    return jnp.cumsum(p, axis=0)[: logits.shape[0]]


def _heads(a):
    b, t, _ = a.shape
    return a.reshape(b, t, HGRN_HEADS, HGRN_HEAD_DIM).transpose(0, 2, 1, 3)


def _chunk(a):
    b, h, t, d = a.shape
    return a.reshape(b, h, t // CHUNK, CHUNK, d)


def _inter_chunk_states(k_c, v_c, b_c, s0):
    b_last = b_c[:, :, :, -1]
    k_end = k_c * jnp.exp(b_last[:, :, :, None, :] - b_c)
    u = jnp.einsum('bhnck,bhncv->bhnkv', k_end, v_c)
    decay = jnp.exp(b_last)

    def step(s, inp):
        d_n, u_n = inp
        return d_n[..., None] * s + u_n, s

    s_final, s_prev = lax.scan(step, s0, (jnp.moveaxis(decay, 2, 0), jnp.moveaxis(u, 2, 0)))
    return jnp.moveaxis(s_prev, 0, 2), s_final


def gla_forward(q, k, v, log_f, s0):
    b, h, t, _ = q.shape
    q_c, k_c, v_c = _chunk(q), _chunk(k), _chunk(v)
    b_c = jnp.cumsum(_chunk(log_f), axis=3)
    s_prev, _ = _inter_chunk_states(k_c, v_c, b_c, s0)
    o_inter = jnp.einsum('bhnck,bhnkv->bhncv', q_c * jnp.exp(b_c), s_prev)
    b_mid = b_c[:, :, :, CHUNK // 2 - 1:CHUNK // 2]
    a = jnp.einsum('bhnck,bhnsk->bhncs', q_c * jnp.exp(b_c - b_mid), k_c * jnp.exp(b_mid - b_c))
    a = jnp.where(jnp.tril(jnp.ones((CHUNK, CHUNK), dtype=bool)), a, 0.0)
    o_intra = jnp.einsum('bhncs,bhnsv->bhncv', a, v_c)
    return (o_inter + o_intra).reshape(b, h, t, -1)


def gla_final_state(k, v, log_f, s0):
    k_c, v_c = _chunk(k), _chunk(v)
    b_c = jnp.cumsum(_chunk(log_f), axis=3)
    _, s_final = _inter_chunk_states(k_c, v_c, b_c, s0)
    return s_final


def _rev(a):
    return jnp.flip(a, axis=2)


def _hgrn2_gates(h, w_in, lb):
    f_f, f_b, i = jnp.split((h @ w_in[:, : 3 * HGRN_WIDTH]).astype(jnp.float32), 3, axis=-1)
    out = []
    for d, f_pre in enumerate((f_f, f_b)):
        f = lb[d] + (1.0 - lb[d]) * jax.nn.sigmoid(f_pre)
        out.append((_heads(1.0 - f), _heads(jnp.log(f))))
    return out[0], out[1], _heads(i)


def hgrn2_final_states(h, w_in, lb):
    (k_f, lf_f), (k_b, lf_b), v = _hgrn2_gates(h, w_in, lb)
    zeros = jnp.zeros(v.shape[:2] + (HGRN_HEAD_DIM, HGRN_HEAD_DIM), jnp.float32)
    s_f = gla_final_state(k_f, v, lf_f, zeros)
    s_b = gla_final_state(_rev(k_b), _rev(v), _rev(lf_b), zeros)
    return s_f, s_b


def hgrn2_mixer(h, w_in, lb, gnorm_w, w_out, s0_f, s0_b):
    bsz, t, _ = h.shape
    (k_f, lf_f), (k_b, lf_b), v = _hgrn2_gates(h, w_in, lb)
    q, z = jnp.split(h @ w_in[:, 3 * HGRN_WIDTH:], 2, axis=-1)
    q = _heads(jax.nn.silu(q).astype(jnp.float32)) * (HGRN_HEAD_DIM ** -0.5)
    o_f = gla_forward(q, k_f, v, lf_f, s0_f)
    o_b = _rev(gla_forward(_rev(q), _rev(k_b), _rev(v), _rev(lf_b), s0_b))
    o = (o_f + o_b).transpose(0, 2, 1, 3)
    o = rmsnorm(o, gnorm_w.reshape(HGRN_HEADS, HGRN_HEAD_DIM))
    o = o.reshape(bsz, t, HGRN_WIDTH).astype(h.dtype) * jax.nn.silu(z)
    return o @ w_out


def _centred_mean(u, w):
    length = u.shape[-2]
    csum = jnp.concatenate([jnp.zeros_like(u[..., :1, :]), jnp.cumsum(u, axis=-2)], axis=-2)
    t = jnp.arange(length)
    lo = jnp.clip(t - w // 2, 0, length)
    hi = jnp.clip(t - w // 2 + w, 0, length)
    s = jnp.take(csum, hi, axis=-2) - jnp.take(csum, lo, axis=-2)
    return s / (hi - lo).astype(u.dtype)[:, None]


def pool_mixer(h, rows, row_len, w_in, w_grp, scale, w_out):
    bsz, t, _ = h.shape
    u, z = jnp.split(h @ w_in, 2, axis=-1)
    ug = u.astype(jnp.float32).reshape(bsz, rows, row_len, POOL_GROUPS, POOL_GROUP_DIM)
    pooled = jnp.stack([_centred_mean(ug[..., g, :], w) for g, w in enumerate(POOL_WINDOWS)], axis=-2) - ug
    y = jnp.einsum('brwgi,gio->brwgo', pooled, w_grp).reshape(bsz, t, POOL_WIDTH) * scale
    return (y.astype(h.dtype) * jax.nn.silu(z)) @ w_out


def _fwd_setup_inputs(seed: int = 0) -> dict:
    key = jax.random.key(seed)
    ks = jax.random.split(key, 18)
    d, e, g = D_MODEL, HGRN_WIDTH, POOL_GROUP_DIM
    n = jax.random.normal
    return {
        'x': n(ks[0], (BATCH, SEQ, d), jnp.float32),
        'c': n(ks[1], (BATCH, d), jnp.float32),
        'ctx': n(ks[2], (BATCH, CTX_LEN, d), jnp.float32),
        'c_ctx': n(ks[3], (d,), jnp.float32),
        'ada_w': n(ks[4], (DEPTH, d, 3 * d), jnp.float32) * (0.5 * d ** -0.5),
        'ada_b': n(ks[5], (DEPTH, 3 * d), jnp.float32) * 0.02,
        'norm_w': 1.0 + 0.05 * n(ks[6], (DEPTH, d), jnp.float32),
        'hgrn_w_in': n(ks[7], (N_HGRN_LAYERS, d, 5 * e), jnp.float32) * d ** -0.5,
        'hgrn_lb_logits': 0.1 * n(ks[8], (N_HGRN_LAYERS, 2, e), jnp.float32),
        'hgrn_gnorm_w': 1.0 + 0.05 * n(ks[9], (N_HGRN_LAYERS, e), jnp.float32),
        'hgrn_w_out': n(ks[10], (N_HGRN_LAYERS, e, d), jnp.float32) * e ** -0.5,
        'pool_w_in': n(ks[11], (N_POOL_LAYERS, d, 2 * POOL_WIDTH), jnp.float32) * d ** -0.5,
        'pool_w_grp': n(ks[12], (N_POOL_LAYERS, POOL_GROUPS, g, g), jnp.float32) * g ** -0.5,
        'pool_scale': 1.0 + 0.05 * n(ks[13], (N_POOL_LAYERS, POOL_WIDTH), jnp.float32),
        'pool_w_out': n(ks[14], (N_POOL_LAYERS, POOL_WIDTH, d), jnp.float32) * POOL_WIDTH ** -0.5,
        'final_norm_w': 1.0 + 0.05 * n(ks[15], (d,), jnp.float32),
    }


def _fwd_reference(x, c, ctx, c_ctx, ada_w, ada_b, norm_w, hgrn_w_in, hgrn_lb_logits, hgrn_gnorm_w,
              hgrn_w_out, pool_w_in, pool_w_grp, pool_scale, pool_w_out, final_norm_w):
    rows = x.shape[1] // GRID_W
    lbs = lower_bounds(hgrn_lb_logits)
    h_ctx = ctx
    for i in range(DEPTH):
        is_hgrn = (i % N_MIXERS == 0)
        j = i // N_MIXERS
        ctx_update = any(l % N_MIXERS == 0 for l in range(i + 1, DEPTH))
        shift, scale, gate = _adaln(c, ada_w[i], ada_b[i])
        hx = rmsnorm(x, norm_w[i]) * (1.0 + scale) + shift
        if is_hgrn or ctx_update:
            c_shift, c_scale, c_gate = _adaln(c_ctx[None, :], ada_w[i], ada_b[i])
            hc = rmsnorm(h_ctx, norm_w[i]) * (1.0 + c_scale) + c_shift
        if is_hgrn:
            s_f, s_b = hgrn2_final_states(hc, hgrn_w_in[j], lbs[j])
            x = x + gate * hgrn2_mixer(hx, hgrn_w_in[j], lbs[j], hgrn_gnorm_w[j], hgrn_w_out[j], s_f, s_b)
            if ctx_update:
                zeros = jnp.zeros_like(s_f)
                h_ctx = h_ctx + c_gate * hgrn2_mixer(hc, hgrn_w_in[j], lbs[j], hgrn_gnorm_w[j], hgrn_w_out[j], zeros, zeros)
        else:
            x = x + gate * pool_mixer(hx, rows, GRID_W, pool_w_in[j], pool_w_grp[j], pool_scale[j], pool_w_out[j])
            if ctx_update:
                h_ctx = h_ctx + c_gate * pool_mixer(hc, 1, hc.shape[1], pool_w_in[j], pool_w_grp[j], pool_scale[j], pool_w_out[j])
    return rmsnorm(x, final_norm_w)


import jax as _jax
import jax.numpy as _jnp

TWIN_FORMAT = 'train_step'
FWD_PARAMS = ['x', 'c', 'ctx', 'c_ctx', 'ada_w', 'ada_b', 'norm_w', 'hgrn_w_in', 'hgrn_lb_logits', 'hgrn_gnorm_w', 'hgrn_w_out', 'pool_w_in', 'pool_w_grp', 'pool_scale', 'pool_w_out', 'final_norm_w']
TWIN_WEIGHTS = ['c_ctx', 'ada_w', 'ada_b', 'norm_w', 'hgrn_w_in', 'hgrn_lb_logits', 'hgrn_gnorm_w', 'hgrn_w_out', 'pool_w_in', 'pool_w_grp', 'pool_scale', 'pool_w_out', 'final_norm_w']
TWIN_DIFF_INPUT = 'x'
TWIN_INPUTS = ['x', 'c', 'ctx', 'c_ctx', 'ada_w', 'ada_b', 'norm_w', 'hgrn_w_in', 'hgrn_lb_logits', 'hgrn_gnorm_w', 'hgrn_w_out', 'pool_w_in', 'pool_w_grp', 'pool_scale', 'pool_w_out', 'final_norm_w', 'loss_target', 'm_c_ctx', 'm_ada_w', 'm_ada_b', 'm_norm_w', 'm_hgrn_w_in', 'm_hgrn_lb_logits', 'm_hgrn_gnorm_w', 'm_hgrn_w_out', 'm_pool_w_in', 'm_pool_w_grp', 'm_pool_scale', 'm_pool_w_out', 'm_final_norm_w', 'v_c_ctx', 'v_ada_w', 'v_ada_b', 'v_norm_w', 'v_hgrn_w_in', 'v_hgrn_lb_logits', 'v_hgrn_gnorm_w', 'v_hgrn_w_out', 'v_pool_w_in', 'v_pool_w_grp', 'v_pool_scale', 'v_pool_w_out', 'v_final_norm_w']
TWIN_OUTPUTS = ['loss', 'grad_x', 'grad_c_ctx', 'grad_ada_w', 'grad_ada_b', 'grad_norm_w', 'grad_hgrn_w_in', 'grad_hgrn_lb_logits', 'grad_hgrn_gnorm_w', 'grad_hgrn_w_out', 'grad_pool_w_in', 'grad_pool_w_grp', 'grad_pool_scale', 'grad_pool_w_out', 'grad_final_norm_w', 'delta_c_ctx', 'delta_ada_w', 'delta_ada_b', 'delta_norm_w', 'delta_hgrn_w_in', 'delta_hgrn_lb_logits', 'delta_hgrn_gnorm_w', 'delta_hgrn_w_out', 'delta_pool_w_in', 'delta_pool_w_grp', 'delta_pool_scale', 'delta_pool_w_out', 'delta_final_norm_w', 'new_m_c_ctx', 'new_m_ada_w', 'new_m_ada_b', 'new_m_norm_w', 'new_m_hgrn_w_in', 'new_m_hgrn_lb_logits', 'new_m_hgrn_gnorm_w', 'new_m_hgrn_w_out', 'new_m_pool_w_in', 'new_m_pool_w_grp', 'new_m_pool_scale', 'new_m_pool_w_out', 'new_m_final_norm_w', 'new_v_c_ctx', 'new_v_ada_w', 'new_v_ada_b', 'new_v_norm_w', 'new_v_hgrn_w_in', 'new_v_hgrn_lb_logits', 'new_v_hgrn_gnorm_w', 'new_v_hgrn_w_out', 'new_v_pool_w_in', 'new_v_pool_w_grp', 'new_v_pool_scale', 'new_v_pool_w_out', 'new_v_final_norm_w']
TWIN_LEAF_KINDS = {'loss': 'loss', 'grad_x': 'grad_x', 'grad_c_ctx': 'grad_w', 'grad_ada_w': 'grad_w', 'grad_ada_b': 'grad_w', 'grad_norm_w': 'grad_w', 'grad_hgrn_w_in': 'grad_w', 'grad_hgrn_lb_logits': 'grad_w', 'grad_hgrn_gnorm_w': 'grad_w', 'grad_hgrn_w_out': 'grad_w', 'grad_pool_w_in': 'grad_w', 'grad_pool_w_grp': 'grad_w', 'grad_pool_scale': 'grad_w', 'grad_pool_w_out': 'grad_w', 'grad_final_norm_w': 'grad_w', 'delta_c_ctx': 'delta_w', 'delta_ada_w': 'delta_w', 'delta_ada_b': 'delta_w', 'delta_norm_w': 'delta_w', 'delta_hgrn_w_in': 'delta_w', 'delta_hgrn_lb_logits': 'delta_w', 'delta_hgrn_gnorm_w': 'delta_w', 'delta_hgrn_w_out': 'delta_w', 'delta_pool_w_in': 'delta_w', 'delta_pool_w_grp': 'delta_w', 'delta_pool_scale': 'delta_w', 'delta_pool_w_out': 'delta_w', 'delta_final_norm_w': 'delta_w', 'new_m_c_ctx': 'new_m', 'new_m_ada_w': 'new_m', 'new_m_ada_b': 'new_m', 'new_m_norm_w': 'new_m', 'new_m_hgrn_w_in': 'new_m', 'new_m_hgrn_lb_logits': 'new_m', 'new_m_hgrn_gnorm_w': 'new_m', 'new_m_hgrn_w_out': 'new_m', 'new_m_pool_w_in': 'new_m', 'new_m_pool_w_grp': 'new_m', 'new_m_pool_scale': 'new_m', 'new_m_pool_w_out': 'new_m', 'new_m_final_norm_w': 'new_m', 'new_v_c_ctx': 'new_v', 'new_v_ada_w': 'new_v', 'new_v_ada_b': 'new_v', 'new_v_norm_w': 'new_v', 'new_v_hgrn_w_in': 'new_v', 'new_v_hgrn_lb_logits': 'new_v', 'new_v_hgrn_gnorm_w': 'new_v', 'new_v_hgrn_w_out': 'new_v', 'new_v_pool_w_in': 'new_v', 'new_v_pool_w_grp': 'new_v', 'new_v_pool_scale': 'new_v', 'new_v_pool_w_out': 'new_v', 'new_v_final_norm_w': 'new_v'}


def _forward(args):
    return _fwd_reference(*[args[k] for k in FWD_PARAMS])


def _output_shape():
    out = _jax.eval_shape(lambda: _forward(_fwd_setup_inputs(0)))
    return out.shape, out.dtype

N_MICROBATCH = 1
ADAM_LR = 0.001
ADAM_B1 = 0.9
ADAM_B2 = 0.999
ADAM_EPS = 1e-08
ADAM_WD = 0.01
ADAM_STEP = 10
PER_EXAMPLE_BATCH_AXIS = {'x': 0, 'c': 0, 'ctx': 0, 'loss_target': 0}
SHARED_INPUTS = []
_WEIGHT_DTYPES = {'c_ctx': _jnp.float32, 'ada_w': _jnp.float32, 'ada_b': _jnp.float32, 'norm_w': _jnp.float32, 'hgrn_w_in': _jnp.float32, 'hgrn_lb_logits': _jnp.float32, 'hgrn_gnorm_w': _jnp.float32, 'hgrn_w_out': _jnp.float32, 'pool_w_in': _jnp.float32, 'pool_w_grp': _jnp.float32, 'pool_scale': _jnp.float32, 'pool_w_out': _jnp.float32, 'final_norm_w': _jnp.float32}
MOMENT_SCALE = {'c_ctx': 9.910923e-04, 'ada_w': 4.181841e-02, 'ada_b': 7.009258e-02, 'norm_w': 3.223261e-02, 'hgrn_w_in': 1.681159e-02, 'hgrn_lb_logits': 1.136930e-03, 'hgrn_gnorm_w': 2.683819e-02, 'hgrn_w_out': 2.600322e-02, 'pool_w_in': 2.267997e-02, 'pool_w_grp': 2.239292e-02, 'pool_scale': 2.242062e-02, 'pool_w_out': 2.231490e-02, 'final_norm_w': 1.599932e+01}


def _to_microbatches(a, axis):
    t = _jnp.moveaxis(a, axis, 0)
    t = t.reshape((N_MICROBATCH, t.shape[0] // N_MICROBATCH) + t.shape[1:])
    return _jnp.moveaxis(t, 1, axis + 1)


def setup_inputs(seed: int = 0) -> dict:
    inp = _fwd_setup_inputs(seed)
    key = _jax.random.fold_in(_jax.random.key(seed), 7919)
    shape, _ = _output_shape()
    out = dict(inp)
    out["loss_target"] = _jax.random.normal(_jax.random.fold_in(key, 0), shape, _jnp.float32)
    for i, name in enumerate(TWIN_WEIGHTS):
        w = inp[name].astype(_jnp.float32)
        if MOMENT_SCALE is None:
            s = _jnp.sqrt(_jnp.mean(_jnp.square(w)) + 1e-30)
        else:
            s = MOMENT_SCALE[name]
        km, kv = _jax.random.split(_jax.random.fold_in(key, i + 1))
        out[name] = w
        out["m_" + name] = s * _jax.random.normal(km, w.shape, _jnp.float32)
        out["v_" + name] = (s * s) * _jax.random.uniform(kv, w.shape, _jnp.float32, 0.5, 1.5)
    if N_MICROBATCH > 1:
        for name, axis in PER_EXAMPLE_BATCH_AXIS.items():
            out[name] = _to_microbatches(out[name], axis)
    return {'x': out['x'], 'c': out['c'], 'ctx': out['ctx'], 'c_ctx': out['c_ctx'], 'ada_w': out['ada_w'], 'ada_b': out['ada_b'], 'norm_w': out['norm_w'], 'hgrn_w_in': out['hgrn_w_in'], 'hgrn_lb_logits': out['hgrn_lb_logits'], 'hgrn_gnorm_w': out['hgrn_gnorm_w'], 'hgrn_w_out': out['hgrn_w_out'], 'pool_w_in': out['pool_w_in'], 'pool_w_grp': out['pool_w_grp'], 'pool_scale': out['pool_scale'], 'pool_w_out': out['pool_w_out'], 'final_norm_w': out['final_norm_w'], 'loss_target': out['loss_target'], 'm_c_ctx': out['m_c_ctx'], 'm_ada_w': out['m_ada_w'], 'm_ada_b': out['m_ada_b'], 'm_norm_w': out['m_norm_w'], 'm_hgrn_w_in': out['m_hgrn_w_in'], 'm_hgrn_lb_logits': out['m_hgrn_lb_logits'], 'm_hgrn_gnorm_w': out['m_hgrn_gnorm_w'], 'm_hgrn_w_out': out['m_hgrn_w_out'], 'm_pool_w_in': out['m_pool_w_in'], 'm_pool_w_grp': out['m_pool_w_grp'], 'm_pool_scale': out['m_pool_scale'], 'm_pool_w_out': out['m_pool_w_out'], 'm_final_norm_w': out['m_final_norm_w'], 'v_c_ctx': out['v_c_ctx'], 'v_ada_w': out['v_ada_w'], 'v_ada_b': out['v_ada_b'], 'v_norm_w': out['v_norm_w'], 'v_hgrn_w_in': out['v_hgrn_w_in'], 'v_hgrn_lb_logits': out['v_hgrn_lb_logits'], 'v_hgrn_gnorm_w': out['v_hgrn_gnorm_w'], 'v_hgrn_w_out': out['v_hgrn_w_out'], 'v_pool_w_in': out['v_pool_w_in'], 'v_pool_w_grp': out['v_pool_w_grp'], 'v_pool_scale': out['v_pool_scale'], 'v_pool_w_out': out['v_pool_w_out'], 'v_final_norm_w': out['v_final_norm_w']}


def _loss(weights, diff, rest, loss_target):
    with _jax.named_scope("forward"):
        args = {**rest, TWIN_DIFF_INPUT: diff, **{k: w.astype(_WEIGHT_DTYPES[k]) for k, w in weights.items()}}
        y = _forward(args)
    with _jax.named_scope("loss_head"):
        err = _jnp.square(y.astype(_jnp.float32) - loss_target)
        return 0.5 * _jnp.sum(_jnp.mean(err, axis=-1)) if err.ndim else 0.5 * err


def _adamw(w, g, m, v):
    m = ADAM_B1 * m + (1.0 - ADAM_B1) * g
    v = ADAM_B2 * v + (1.0 - ADAM_B2) * _jnp.square(g)
    m_hat = m / (1.0 - ADAM_B1 ** ADAM_STEP)
    v_hat = v / (1.0 - ADAM_B2 ** ADAM_STEP)
    delta = -ADAM_LR * (m_hat / (_jnp.sqrt(v_hat) + ADAM_EPS) + ADAM_WD * w)
    return delta, m, v


def reference(x, c, ctx, c_ctx, ada_w, ada_b, norm_w, hgrn_w_in, hgrn_lb_logits, hgrn_gnorm_w, hgrn_w_out, pool_w_in, pool_w_grp, pool_scale, pool_w_out, final_norm_w, loss_target, m_c_ctx, m_ada_w, m_ada_b, m_norm_w, m_hgrn_w_in, m_hgrn_lb_logits, m_hgrn_gnorm_w, m_hgrn_w_out, m_pool_w_in, m_pool_w_grp, m_pool_scale, m_pool_w_out, m_final_norm_w, v_c_ctx, v_ada_w, v_ada_b, v_norm_w, v_hgrn_w_in, v_hgrn_lb_logits, v_hgrn_gnorm_w, v_hgrn_w_out, v_pool_w_in, v_pool_w_grp, v_pool_scale, v_pool_w_out, v_final_norm_w):
    given = dict(x=x, c=c, ctx=ctx, c_ctx=c_ctx, ada_w=ada_w, ada_b=ada_b, norm_w=norm_w, hgrn_w_in=hgrn_w_in, hgrn_lb_logits=hgrn_lb_logits, hgrn_gnorm_w=hgrn_gnorm_w, hgrn_w_out=hgrn_w_out, pool_w_in=pool_w_in, pool_w_grp=pool_w_grp, pool_scale=pool_scale, pool_w_out=pool_w_out, final_norm_w=final_norm_w, loss_target=loss_target, m_c_ctx=m_c_ctx, m_ada_w=m_ada_w, m_ada_b=m_ada_b, m_norm_w=m_norm_w, m_hgrn_w_in=m_hgrn_w_in, m_hgrn_lb_logits=m_hgrn_lb_logits, m_hgrn_gnorm_w=m_hgrn_gnorm_w, m_hgrn_w_out=m_hgrn_w_out, m_pool_w_in=m_pool_w_in, m_pool_w_grp=m_pool_w_grp, m_pool_scale=m_pool_scale, m_pool_w_out=m_pool_w_out, m_final_norm_w=m_final_norm_w, v_c_ctx=v_c_ctx, v_ada_w=v_ada_w, v_ada_b=v_ada_b, v_norm_w=v_norm_w, v_hgrn_w_in=v_hgrn_w_in, v_hgrn_lb_logits=v_hgrn_lb_logits, v_hgrn_gnorm_w=v_hgrn_gnorm_w, v_hgrn_w_out=v_hgrn_w_out, v_pool_w_in=v_pool_w_in, v_pool_w_grp=v_pool_w_grp, v_pool_scale=v_pool_scale, v_pool_w_out=v_pool_w_out, v_final_norm_w=v_final_norm_w)
    weights = {n: given[n] for n in TWIN_WEIGHTS}
    shared = {n: given[n] for n in SHARED_INPUTS}
    per_example = {n: given[n] for n in ['x', 'c', 'ctx']}
    grad_fn = _jax.value_and_grad(_loss, argnums=(0, 1))

    def one_microbatch(ex, loss_target):
        ex = dict(ex)
        diff = ex.pop(TWIN_DIFF_INPUT)
        return grad_fn(weights, diff, {**shared, **ex}, loss_target)

    if N_MICROBATCH == 1:
        loss, (grad_w, grad_x) = one_microbatch(per_example, given["loss_target"])
    else:
        def body(carry, xs):
            loss_sum, grad_sum = carry
            l_k, (gw_k, gx_k) = one_microbatch(xs[0], xs[1])
            with _jax.named_scope("update"):
                return (loss_sum + l_k, _jax.tree.map(_jnp.add, grad_sum, gw_k)), gx_k

        init = (_jnp.zeros((), _jnp.float32), _jax.tree.map(_jnp.zeros_like, weights))
        (loss, grad_w), grad_x = _jax.lax.scan(body, init, (per_example, given["loss_target"]))
    with _jax.named_scope("update"):
        delta_w, new_m, new_v = {}, {}, {}
        for n in TWIN_WEIGHTS:
            delta_w[n], new_m[n], new_v[n] = _adamw(weights[n], grad_w[n], given["m_" + n], given["v_" + n])
    return (loss, grad_x, *[grad_w[n] for n in TWIN_WEIGHTS], *[delta_w[n] for n in TWIN_WEIGHTS],
            *[new_m[n] for n in TWIN_WEIGHTS], *[new_v[n] for n in TWIN_WEIGHTS])
```

```python
import functools

import numpy as np
import jax
import jax.numpy as jnp
from jax import lax
from jax.experimental import pallas as pl
from jax.experimental.pallas import tpu as pltpu

F32 = jnp.float32
BF16 = jnp.bfloat16

D = 1024
E = 1024
HEADS = 8
DH = 128
CHUNK = 64
T = 2048
TC = 256
TT = T + TC
TM = 256
NT = TT // TM
NTX = T // TM
NDEV = 8
GRID_W = 64
POOL_WINDOWS = (2, 4, 8, 16)
PG = 256
EPS = 1e-6
WIN_COLS = 5 * E
SH_WIN = WIN_COLS // NDEV
SH_PWIN = 2 * E // NDEV
SH_ROWS = E // NDEV
SH_GRP = PG // NDEV
SH_ADA = 3 * D // NDEV
VMEM_LIMIT = 56 * 1024 * 1024

ADAM_LR, ADAM_B1, ADAM_B2, ADAM_EPS, ADAM_WD, ADAM_STEP = 0.001, 0.9, 0.999, 1e-08, 0.01, 10

MESH = pl.DeviceIdType.MESH
VMEM_SPEC = pl.BlockSpec(memory_space=pltpu.VMEM)
HBM_SPEC = pl.BlockSpec(memory_space=pltpu.HBM)
ANY_SPEC = pl.BlockSpec(memory_space=pl.ANY)


def _sds(shape, dtype):
    return jax.ShapeDtypeStruct(shape, dtype)


def _bf(a):
    return a if a.dtype == BF16 else a.astype(BF16)


def _dot(a, b):
    return lax.dot_general(_bf(a), _bf(b), (((1,), (0,)), ((), ())), preferred_element_type=F32)


def _dot_tb(a, b):
    return lax.dot_general(_bf(a), _bf(b), (((1,), (1,)), ((), ())), preferred_element_type=F32)


def _dot_ta(a, b):
    return lax.dot_general(_bf(a), _bf(b), (((0,), (0,)), ((), ())), preferred_element_type=F32)


def _dot01(m01, x):
    hi = x.astype(BF16)
    lo = (x - hi.astype(F32)).astype(BF16)
    return _dot(m01, hi) + _dot(m01, lo)


def _rstd(x):
    return lax.rsqrt(jnp.mean(x * x, axis=-1, keepdims=True) + EPS)


def _sigmoid(x):
    return jax.nn.sigmoid(x)


def _colsum(a):
    return jnp.sum(a, axis=0, keepdims=True)


def _stack_rows(rows):
    n = rows[0].shape[-1]
    rid = lax.broadcasted_iota(jnp.int32, (16, n), 0)
    out = jnp.zeros((16, n), F32)
    for i, r in enumerate(rows):
        out = jnp.where(rid == i, r, out)
    return out


def _head_map(fn, *arrs):
    outs = [fn(*[a[:, h * DH:(h + 1) * DH] for a in arrs]) for h in range(HEADS)]
    return jnp.concatenate(outs, axis=1)


def _gla_consts():
    r = np.arange(TM)[:, None]
    c = np.arange(TM)[None, :]
    same = (r // CHUNK) == (c // CHUNK)
    tril = same & (c <= r)
    triu = same & (c >= r)
    m = np.stack([tril, triu]).astype(np.float32)
    return jnp.asarray(m, BF16), jnp.asarray(m, F32)


def _pool_consts():
    r = np.arange(TM)[:, None]
    c = np.arange(TM)[None, :]
    same = (r // GRID_W) == (c // GRID_W)
    rp, cp = r % GRID_W, c % GRID_W
    bs, inv = [], []
    for w in POOL_WINDOWS:
        lo = np.clip(rp - w // 2, 0, GRID_W)
        hi = np.clip(rp - w // 2 + w, 0, GRID_W)
        bs.append(same & (cp >= lo) & (cp < hi))
        inv.append(1.0 / (hi - lo).astype(np.float32))
    b = np.stack(bs).astype(np.float32)
    bt = np.transpose(b, (0, 2, 1))
    return jnp.asarray(b, BF16), jnp.asarray(bt, BF16), jnp.asarray(np.stack(inv), F32)


def _mesh_pos():
    x, y, c = lax.axis_index("x"), lax.axis_index("y"), lax.axis_index("c")
    return x, y, c, 4 * x + 2 * y + c


def _peer(x, y, c, k):
    return (x ^ ((k >> 2) & 1), y ^ ((k >> 1) & 1), c ^ (k & 1))


def _gather_fwd(w_in, w_out, pw_in, pgrp, pw_out, lb_l, pscale, c, c_ctx, ada_w):
    n_arr = 9

    def body(win_r, wout_r, pwin_r, pgrp_r, pwout_r, lb_r, ps_r, c_r, cctx_r, ada_r,
             win_o, wout_o, pwin_o, pgrp_o, pwout_o, lb_o, ps_o, cg_o, mod_o,
             s_win, s_wout, s_pwin, s_pgrp, s_pwout, ssem, rsem, lsem):
        x, y, cc, idx = _mesh_pos()
        s_win[...] = win_r[...].astype(BF16)
        s_wout[...] = wout_r[...].astype(BF16)
        s_pwin[...] = pwin_r[...].astype(BF16)
        s_pgrp[...] = pgrp_r[...].astype(BF16)
        s_pwout[...] = pwout_r[...].astype(BF16)

        def dsts(i):
            return [
                win_o.at[:, pl.ds(pl.multiple_of(i * SH_WIN, 128), SH_WIN)],
                wout_o.at[pl.ds(pl.multiple_of(i * SH_ROWS, SH_ROWS), SH_ROWS), :],
                pwin_o.at[:, pl.ds(pl.multiple_of(i * SH_PWIN, 128), SH_PWIN)],
                pgrp_o.at[:, pl.ds(pl.multiple_of(i * SH_GRP, SH_GRP), SH_GRP), :],
                pwout_o.at[pl.ds(pl.multiple_of(i * SH_ROWS, SH_ROWS), SH_ROWS), :],
                lb_o.at[i], ps_o.at[i], cg_o.at[i], mod_o.at[i],
            ]

        srcs = [s_win, s_wout, s_pwin, s_pgrp, s_pwout, lb_r, ps_r, c_r, mod_o.at[idx]]
        mine = dsts(idx)

        def remote(a, k):
            return pltpu.make_async_remote_copy(src_ref=srcs[a], dst_ref=mine[a], send_sem=ssem.at[a, k], recv_sem=rsem.at[a, k],
                                                device_id=_peer(x, y, cc, k), device_id_type=MESH)

        first = [remote(a, k) for k in range(1, NDEV) for a in (7, 0, 2, 1, 3, 4, 5, 6)]
        for cp in first:
            cp.start()
        local = [pltpu.make_async_copy(srcs[a], mine[a], lsem.at[a]) for a in range(5)]
        for cp in local:
            cp.start()
        lb_o[idx] = lb_r[...]
        ps_o[idx] = ps_r[...]
        cg_o[idx] = c_r[...]
        for k in range(1, NDEV):
            remote(7, k).wait_recv()
        rows = _stack_rows([cg_o[i] for i in range(NDEV)] + [cctx_r[...]])
        sc = rows * _sigmoid(rows)
        for l in range(2):
            mod_o[idx, l] = _dot(sc, ada_r[l])
        second = [remote(8, k) for k in range(1, NDEV)]
        for cp in second:
            cp.start()
        for cp in first + second:
            cp.wait_send()
        for k in range(1, NDEV):
            for a in (0, 1, 2, 3, 4, 5, 6, 8):
                remote(a, k).wait_recv()
        for cp in local:
            cp.wait()

    outs = (
        _sds((D, WIN_COLS), BF16), _sds((E, D), BF16), _sds((D, 2 * E), BF16), _sds((4, PG, PG), BF16), _sds((E, D), BF16),
        _sds((NDEV, 2, DH), F32), _sds((NDEV, 1, DH), F32), _sds((NDEV, 1, D), F32), _sds((NDEV, 2, 16, SH_ADA), F32),
    )
    return pl.pallas_call(
        body, name="gather_fwd", out_shape=outs,
        in_specs=[VMEM_SPEC] * 10,
        out_specs=[HBM_SPEC] * 5 + [VMEM_SPEC] * 4,
        scratch_shapes=[
            pltpu.VMEM((D, SH_WIN), BF16), pltpu.VMEM((SH_ROWS, D), BF16), pltpu.VMEM((D, SH_PWIN), BF16),
            pltpu.VMEM((4, SH_GRP, PG), BF16), pltpu.VMEM((SH_ROWS, D), BF16),
            pltpu.SemaphoreType.DMA((n_arr, NDEV)), pltpu.SemaphoreType.DMA((n_arr, NDEV)), pltpu.SemaphoreType.DMA((5,)),
        ],
        compiler_params=pltpu.CompilerParams(vmem_limit_bytes=VMEM_LIMIT),
    )(w_in, w_out, pw_in, pgrp, pw_out, lb_l, pscale, c, c_ctx, ada_w)


def _modulated(x, nw, shift, scale):
    r = _rstd(x)
    xn = x * r
    a = xn * nw
    return a * (1.0 + scale) + shift, r, xn, a


def _f1_norm_matmul(xcat, nw, msel, win):
    def body(x_ref, nw_ref, m_ref, w_ref, g_ref, hx_sc):
        @pl.when(pl.program_id(1) == 0)
        def _():
            hx, _, _, _ = _modulated(x_ref[...], nw_ref[...], m_ref[0, 0:1, :], m_ref[0, 1:2, :])
            hx_sc[...] = hx.astype(BF16)
        g_ref[...] = jnp.dot(hx_sc[...], w_ref[...], preferred_element_type=F32)

    return pl.pallas_call(
        body, name="f1_norm_matmul", grid=(NT, 5),
        in_specs=[pl.BlockSpec((TM, D), lambda i, j: (i, 0)), pl.BlockSpec((1, D), lambda i, j: (0, 0)),
                  pl.BlockSpec((1, 2, D), lambda i, j: (jnp.minimum(i, 1), 0, 0)), pl.BlockSpec((D, E), lambda i, j: (0, j))],
        out_specs=pl.BlockSpec((TM, E), lambda i, j: (i, j)),
        out_shape=_sds((TT, WIN_COLS), F32),
        scratch_shapes=[pltpu.VMEM((TM, D), BF16)],
        compiler_params=pltpu.CompilerParams(dimension_semantics=("arbitrary", "arbitrary")),
    )(xcat, nw, msel, win)


def _gla_tile(pre, qpre, lbd, cum, rev):
    sig = _sigmoid(pre)
    f = lbd + (1.0 - lbd) * sig
    k = 1.0 - f
    g = _dot01(cum, jnp.log(f))
    g3 = g.reshape(TM // CHUNK, CHUNK, DH)
    last = 0 if rev else CHUNK - 1
    mid = CHUNK // 2 if rev else CHUNK // 2 - 1
    gl1 = [g3[ci, last:last + 1, :] for ci in range(TM // CHUNK)]
    gl = jnp.broadcast_to(g3[:, last:last + 1, :], g3.shape).reshape(TM, DH)
    gm = jnp.broadcast_to(g3[:, mid:mid + 1, :], g3.shape).reshape(TM, DH)
    qsig = _sigmoid(qpre)
    qs = qpre * qsig * (DH ** -0.5)
    e_q, e_k, e_in, e_end = jnp.exp(g - gm), jnp.exp(gm - g), jnp.exp(g), jnp.exp(gl - g)
    return dict(sig=sig, f=f, k=k, qsig=qsig, qs=qs, e_q=e_q, e_k=e_k, e_in=e_in, e_end=e_end,
                qg=qs * e_q, kg=k * e_k, q_in=qs * e_in, kend=k * e_end, decay=[jnp.exp(v) for v in gl1])


def _scan_tile(i, rev):
    t = jnp.where(i == 0, 0, NT - i) if rev else i
    return pl.ds(pl.multiple_of(t * TM, TM), TM)


def _chunk_order(rev):
    n = TM // CHUNK
    return tuple(range(n - 1, -1, -1)) if rev else tuple(range(n))


def _gla_fwd(g_all, lb, cum01, mask01):
    def body(gf_ref, gb_ref, gi_ref, gq_ref, lb_ref, cum_ref, msk_ref, o_ref):
        for d in (0, 1):
            rev = d == 1
            pre_ref = gb_ref if rev else gf_ref

            def tile_body(i, st, rev=rev, pre_ref=pre_ref, d=d):
                rows = _scan_tile(i, rev)
                tl = _gla_tile(pre_ref[rows, :], gq_ref[rows, :], lb_ref[d:d + 1, :], cum_ref[d], rev)
                v = gi_ref[rows, :].astype(BF16)
                a = _dot_tb(tl["qg"], tl["kg"]) * msk_ref[d]
                intra = _dot(a, v)
                q_in, kend = tl["q_in"].astype(BF16), tl["kend"].astype(BF16)
                outs = [None] * (TM // CHUNK)
                for ci in _chunk_order(rev):
                    r = slice(ci * CHUNK, (ci + 1) * CHUNK)
                    outs[ci] = _dot_tb(q_in[r], st) + intra[r]
                    st = st * tl["decay"][ci] + _dot_ta(v[r], kend[r])
                o_t = jnp.concatenate(outs, axis=0)
                if rev:
                    o_ref[rows, :] += o_t
                else:
                    o_ref[rows, :] = o_t
                return st

            lax.fori_loop(0, NT, tile_body, jnp.zeros((DH, DH), F32))

    def col(b):
        return pl.BlockSpec((TT, DH), lambda h, b=b: (0, b * HEADS + h))

    return pl.pallas_call(
        body, name="gla_fwd", grid=(HEADS,),
        in_specs=[col(0), col(1), col(2), col(3), pl.BlockSpec((2, DH), lambda h: (0, h)),
                  pl.BlockSpec((2, TM, TM), lambda h: (0, 0, 0)), pl.BlockSpec((2, TM, TM), lambda h: (0, 0, 0))],
        out_specs=pl.BlockSpec((TT, DH), lambda h: (0, h)),
        out_shape=_sds((TT, E), F32),
        compiler_params=pltpu.CompilerParams(dimension_semantics=("arbitrary",), vmem_limit_bytes=VMEM_LIMIT),
    )(g_all, g_all, g_all, g_all, lb, cum01, mask01)


def _gated_norm(o, z, gw):
    r = _head_map(lambda oh: jnp.broadcast_to(_rstd(oh), oh.shape), o)
    on = o * r
    zs = _sigmoid(z)
    sz = z * zs
    return on * gw * sz, r, on, zs, sz


def _f3_out(o, g_all, xcat, gate, gw, wout):
    def body(o_ref, z_ref, x_ref, gate_ref, gw_ref, w_ref, x1_ref):
        og, _, _, _, _ = _gated_norm(o_ref[...], z_ref[...], gw_ref[...])
        x1_ref[...] = x_ref[...] + gate_ref[...] * _dot(og, w_ref[...])

    return pl.pallas_call(
        body, name="f3_out", grid=(NTX,),
        in_specs=[pl.BlockSpec((TM, E), lambda i: (i + 1, 0)), pl.BlockSpec((TM, E), lambda i: (i + 1, 4)),
                  pl.BlockSpec((TM, D), lambda i: (i + 1, 0)), pl.BlockSpec((1, D), lambda i: (0, 0)),
                  pl.BlockSpec((1, E), lambda i: (0, 0)), pl.BlockSpec((E, D), lambda i: (0, 0))],
        out_specs=pl.BlockSpec((TM, D), lambda i: (i, 0)),
        out_shape=_sds((T, D), F32),
        compiler_params=pltpu.CompilerParams(dimension_semantics=("arbitrary",)),
    )(o, g_all, xcat, gate, gw, wout)


def _pool_layer(x1, tgt, mod1, nw1, fnw, pwin, pgrp, pscale, pwout, pb, pbt, pinv):
    def body(x_ref, t_ref, m_ref, nw_ref, fw_ref, pwin_ref, pgrp_ref, ps_ref, pwout_ref, pb_ref, pbt_ref, pinv_ref,
             dx_ref, gpwin_o, gpgrp_o, gpwout_o, dmod_o, gnw_o, gfw_o, gps_o, loss_o,
             a_pwin, a_pgrp, a_pwout):
        i = pl.program_id(0)

        @pl.when(i == 0)
        def _():
            for ref in (a_pwin, a_pgrp, a_pwout, dmod_o, gnw_o, gfw_o, gps_o, loss_o):
                ref[...] = jnp.zeros_like(ref)

        shift, scale, gate = m_ref[0:1, :], m_ref[1:2, :], m_ref[2:3, :]
        nw, fw, ps = nw_ref[...], fw_ref[...], ps_ref[...]
        x1 = x_ref[...]
        hx, r1, xn, a = _modulated(x1, nw, shift, scale)
        hxb = hx.astype(BF16)
        uz = jnp.dot(hxb, pwin_ref[...], preferred_element_type=F32)
        u, z = uz[:, :E], uz[:, E:]
        pooled, ys = [], []
        for g in range(4):
            ug = u[:, g * PG:(g + 1) * PG]
            pg = _dot01(pb_ref[g], ug) * pinv_ref[g] - ug
            pooled.append(pg.astype(BF16))
            ys.append(_dot(pooled[g], pgrp_ref[g]))
        ycat = jnp.concatenate(ys, axis=1)
        y = ycat * ps
        zs = _sigmoid(z)
        sz = z * zs
        p = (y * sz).astype(BF16)
        out = _dot(p, pwout_ref[...])
        x2 = x1 + gate * out
        r2 = _rstd(x2)
        xn2 = x2 * r2
        diff = xn2 * fw - t_ref[...]
        loss_o[...] += _colsum(diff * diff)
        dyf = diff * (1.0 / D)
        gfw_o[...] += _colsum(dyf * xn2)
        dxn2 = dyf * fw
        dx2 = r2 * (dxn2 - xn2 * jnp.mean(dxn2 * xn2, axis=-1, keepdims=True))
        dgate = _colsum(dx2 * out)
        dout = (dx2 * gate).astype(BF16)
        for j in range(4):
            cs = slice(j * PG, (j + 1) * PG)
            a_pwout[:, cs] += _dot_ta(p, dout[:, cs])
        dp = _dot_tb(dout, pwout_ref[...])
        dy = dp * sz
        dz = dp * y * (zs * (1.0 + z * (1.0 - zs)))
        gps_o[...] += _colsum(dy * ycat)
        dycat = dy * ps
        dus = []
        for g in range(4):
            dyg = dycat[:, g * PG:(g + 1) * PG].astype(BF16)
            a_pgrp[g] += _dot_ta(pooled[g], dyg)
            dpg = _dot_tb(dyg, pgrp_ref[g])
            dus.append(_dot01(pbt_ref[g], dpg * pinv_ref[g]) - dpg)
        duz = jnp.concatenate(dus + [dz], axis=1).astype(BF16)
        for j in range(2 * E // PG):
            cs = slice(j * PG, (j + 1) * PG)
            a_pwin[:, cs] += _dot_ta(hxb, duz[:, cs])
        dhx = _dot_tb(duz, pwin_ref[...])
        dmod_o[0:1, :] += _colsum(dhx)
        dmod_o[1:2, :] += _colsum(dhx * a)
        dmod_o[2:3, :] += dgate
        da = dhx * (1.0 + scale)
        gnw_o[...] += _colsum(da * xn)
        dxn = da * nw
        dx_ref[...] = dx2 + r1 * (dxn - xn * jnp.mean(dxn * xn, axis=-1, keepdims=True))

        @pl.when(i == NTX - 1)
        def _():
            gpwin_o[...] = a_pwin[...].astype(BF16)
            gpgrp_o[...] = a_pgrp[...].astype(BF16)
            gpwout_o[...] = a_pwout[...].astype(BF16)

    tile = pl.BlockSpec((TM, D), lambda i: (i, 0))
    outs = (_sds((T, D), F32), _sds((D, 2 * E), BF16), _sds((4, PG, PG), BF16), _sds((E, D), BF16),
            _sds((3, D), F32), _sds((1, D), F32), _sds((1, D), F32), _sds((1, E), F32), _sds((1, D), F32))
    return pl.pallas_call(
        body, name="pool_layer", grid=(NTX,),
        in_specs=[tile, tile] + [VMEM_SPEC] * 10,
        out_specs=[tile] + [VMEM_SPEC] * 8,
        out_shape=outs,
        scratch_shapes=[pltpu.VMEM((D, 2 * E), F32), pltpu.VMEM((4, PG, PG), F32), pltpu.VMEM((E, D), F32)],
        compiler_params=pltpu.CompilerParams(dimension_semantics=("arbitrary",), vmem_limit_bytes=VMEM_LIMIT),
    )(x1, tgt, mod1, nw1, fnw, pwin, pgrp, pscale, pwout, pb, pbt, pinv)


def _b3_out_bwd(dx1, o, g_all, gate, gw, wout):
    def body(dx_ref, o_ref, z_ref, gate_ref, gw_ref, w_ref, do_ref, dz_ref, gw_o, dgate_o, ggw_o, acc):
        i = pl.program_id(0)

        @pl.when(i == 0)
        def _():
            acc[...] = jnp.zeros_like(acc)
            dgate_o[...] = jnp.zeros_like(dgate_o)
            ggw_o[...] = jnp.zeros_like(ggw_o)
            do_ref[...] = jnp.zeros_like(do_ref)
            dz_ref[...] = jnp.zeros_like(dz_ref)

        @pl.when(i > 0)
        def _():
            gw = gw_ref[...]
            z = z_ref[...]
            og, r, on, zs, sz = _gated_norm(o_ref[...], z, gw)
            ogb = og.astype(BF16)
            dx = dx_ref[...]
            dgate_o[...] += _colsum(dx * _dot(ogb, w_ref[...]))
            dy = (dx * gate_ref[...]).astype(BF16)
            for j in range(4):
                cs = slice(j * PG, (j + 1) * PG)
                acc[:, cs] += _dot_ta(ogb, dy[:, cs])
            dog = _dot_tb(dy, w_ref[...])
            dz_ref[0] = (dog * (on * gw) * (zs * (1.0 + z * (1.0 - zs)))).astype(BF16)
            dong = dog * sz
            ggw_o[...] += _colsum(dong * on)
            don = dong * gw
            do = _head_map(lambda dh, nh, rh: rh * (dh - nh * jnp.mean(dh * nh, axis=-1, keepdims=True)), don, on, r)
            do_ref[...] = do.astype(BF16)

        @pl.when(i == NT - 1)
        def _():
            gw_o[...] = acc[...].astype(BF16)

    prev = lambda i: (jnp.maximum(i - 1, 0), 0)
    return pl.pallas_call(
        body, name="b3_out_bwd", grid=(NT,),
        in_specs=[pl.BlockSpec((TM, D), prev), pl.BlockSpec((TM, E), lambda i: (i, 0)), pl.BlockSpec((TM, E), lambda i: (i, 4)),
                  VMEM_SPEC, VMEM_SPEC, VMEM_SPEC],
        out_specs=[pl.BlockSpec((TM, E), lambda i: (i, 0)), pl.BlockSpec((1, TM, E), lambda i: (4, i, 0)),
                   VMEM_SPEC, VMEM_SPEC, VMEM_SPEC],
        out_shape=(_sds((TT, E), BF16), _sds((5, TT, E), BF16), _sds((E, D), BF16), _sds((1, D), F32), _sds((1, E), F32)),
        scratch_shapes=[pltpu.VMEM((E, D), F32)],
        compiler_params=pltpu.CompilerParams(dimension_semantics=("arbitrary",), vmem_limit_bytes=VMEM_LIMIT),
    )(dx1, o, g_all, gate, gw, wout)


def _gla_bwd(g_all, do, lb, cum01, mask01, dg5):
    nch = TM // CHUNK

    def body(gf_ref, gb_ref, gi_ref, gq_ref, do_ref, lb_ref, cum_ref, msk_ref, dg_in, dg_ref, dlb_ref, ss_sc, dv_sc, dq_sc):
        del dg_in
        for d in (0, 1):
            rev = d == 1
            pre_ref = gb_ref if rev else gf_ref
            order = _chunk_order(rev)
            lbd = lb_ref[d:d + 1, :]
            last = 0 if rev else CHUNK - 1

            def fwd_body(i, st, rev=rev, pre_ref=pre_ref, d=d, lbd=lbd, order=order):
                rows = _scan_tile(i, rev)
                tl = _gla_tile(pre_ref[rows, :], gq_ref[rows, :], lbd, cum_ref[d], rev)
                v = gi_ref[rows, :].astype(BF16)
                kend = tl["kend"].astype(BF16)
                for n, ci in enumerate(order):
                    r = slice(ci * CHUNK, (ci + 1) * CHUNK)
                    ss_sc[i * nch + n] = st
                    st = st * tl["decay"][ci] + _dot_ta(v[r], kend[r])
                return st

            lax.fori_loop(0, NT, fwd_body, jnp.zeros((DH, DH), F32))

            def bwd_body(ii, carry, rev=rev, pre_ref=pre_ref, d=d, lbd=lbd, order=order, last=last):
                dst, dlb = carry
                i = NT - 1 - ii
                rows = _scan_tile(i, rev)
                pre, qpre = pre_ref[rows, :], gq_ref[rows, :]
                tl = _gla_tile(pre, qpre, lbd, cum_ref[d], rev)
                v = gi_ref[rows, :].astype(BF16)
                dob = do_ref[rows, :]
                msk = msk_ref[d]
                qgb, kgb = tl["qg"].astype(BF16), tl["kg"].astype(BF16)
                q_in, kend = tl["q_in"].astype(BF16), tl["kend"].astype(BF16)
                a = (_dot_tb(qgb, kgb) * msk).astype(BF16)
                da = (_dot_tb(dob, v) * msk).astype(BF16)
                dqg = _dot(da, kgb)
                dkg = _dot_ta(da, qgb)
                dv_intra = _dot_ta(a, dob)
                dv_l, dkend_l, dqin_l, dgl_l = [None] * nch, [None] * nch, [None] * nch, [None] * nch
                for n in range(nch - 1, -1, -1):
                    ci = order[n]
                    r = slice(ci * CHUNK, (ci + 1) * CHUNK)
                    s_c = ss_sc[i * nch + n]
                    dstb = dst.astype(BF16)
                    dv_l[ci] = dv_intra[r] + _dot_tb(kend[r], dstb)
                    dkend_l[ci] = _dot(v[r], dstb)
                    dqin_l[ci] = _dot(dob[r], s_c)
                    ddecay = jnp.sum(s_c * dst, axis=0, keepdims=True)
                    dgl_l[ci] = ddecay * tl["decay"][ci]
                    dst = dst * tl["decay"][ci] + _dot_ta(dob[r], q_in[r])
                dv = jnp.concatenate(dv_l, axis=0)
                dkend = jnp.concatenate(dkend_l, axis=0)
                dqin = jnp.concatenate(dqin_l, axis=0)
                dqs = dqg * tl["e_q"] + dqin * tl["e_in"]
                qsig = tl["qsig"]
                dqpre = dqs * (DH ** -0.5) * (qsig * (1.0 + qpre * (1.0 - qsig)))
                dk = dkg * tl["e_k"] + dkend * tl["e_end"]
                dkk = dkend * tl["kend"]
                dg = dqg * tl["qg"] - dkg * tl["kg"] + dqin * tl["q_in"] - dkk
                dkk3 = dkk.reshape(nch, CHUNK, DH)
                dgl = jnp.concatenate([jnp.broadcast_to(dgl_l[ci] + jnp.sum(dkk3[ci], axis=0, keepdims=True), (CHUNK, DH))
                                       for ci in range(nch)], axis=0)
                pos = lax.broadcasted_iota(jnp.int32, (TM, DH), 0) & (CHUNK - 1)
                dg = dg + jnp.where(pos == last, dgl, 0.0)
                dlf = _dot01(cum_ref[1 - d], dg)
                df = dlf / tl["f"] - dk
                sig = tl["sig"]
                dg_ref[d, rows, :] = (df * (1.0 - lbd) * sig * (1.0 - sig)).astype(BF16)
                dlb = dlb + _colsum(df * (1.0 - sig))
                if rev:
                    dv_sc[rows, :] += dv
                    dq_sc[rows, :] += dqpre
                else:
                    dv_sc[rows, :] = dv
                    dq_sc[rows, :] = dqpre
                return dst, dlb

            _, dlb = lax.fori_loop(0, NT, bwd_body, (jnp.zeros((DH, DH), F32), jnp.zeros((1, DH), F32)))
            dlb_ref[d:d + 1, :] = dlb
        dg_ref[2] = dv_sc[...].astype(BF16)
        dg_ref[3] = dq_sc[...].astype(BF16)

    def col(b):
        return pl.BlockSpec((TT, DH), lambda h, b=b: (0, b * HEADS + h))

    return pl.pallas_call(
        body, name="gla_bwd", grid=(HEADS,),
        in_specs=[col(0), col(1), col(2), col(3), pl.BlockSpec((TT, DH), lambda h: (0, h)), pl.BlockSpec((2, DH), lambda h: (0, h)),
                  pl.BlockSpec((2, TM, TM), lambda h: (0, 0, 0)), pl.BlockSpec((2, TM, TM), lambda h: (0, 0, 0)), ANY_SPEC],
        out_specs=[pl.BlockSpec((4, TT, DH), lambda h: (0, 0, h)), pl.BlockSpec((2, DH), lambda h: (0, h))],
        out_shape=(_sds((5, TT, E), BF16), _sds((2, E), F32)),
        scratch_shapes=[pltpu.VMEM((NT * nch, DH, DH), F32), pltpu.VMEM((TT, DH), F32), pltpu.VMEM((TT, DH), F32)],
        input_output_aliases={8: 0},
        compiler_params=pltpu.CompilerParams(dimension_semantics=("arbitrary",), vmem_limit_bytes=VMEM_LIMIT),
    )(g_all, g_all, g_all, g_all, do, lb, cum01, mask01, dg5)


def _b1_in_bwd(xcat, dx1, dg5, nw, msel, win):
    def body(x_ref, dx1_ref, dg_ref, nw_ref, m_ref, w_ref, gx_ref, gw_ref, dmx_o, dmc_o, gnw_o, hx_sc, dhx_sc, acc):
        b, i = pl.program_id(0), pl.program_id(1)
        shift, scale = m_ref[0, 0:1, :], m_ref[0, 1:2, :]

        @pl.when((b == 0) & (i == 0))
        def _():
            for ref in (dmx_o, dmc_o, gnw_o):
                ref[...] = jnp.zeros_like(ref)

        @pl.when(b == 0)
        def _():
            hx, _, _, _ = _modulated(x_ref[...], nw_ref[...], shift, scale)
            hx_sc[i] = hx.astype(BF16)

        @pl.when(i == 0)
        def _():
            acc[...] = jnp.zeros_like(acc)

        dg = dg_ref[0]
        hxb = hx_sc[i]
        for j in range(4):
            cs = slice(j * PG, (j + 1) * PG)
            acc[:, cs] += _dot_ta(hxb, dg[:, cs])
        part = _dot_tb(dg, w_ref[...])

        @pl.when(b == 0)
        def _():
            dhx_sc[i] = part

        @pl.when(b > 0)
        def _():
            dhx_sc[i] += part

        @pl.when(i == NT - 1)
        def _():
            gw_ref[...] = acc[...].astype(BF16)

        @pl.when(b == 4)
        def _():
            nw = nw_ref[...]
            _, r, xn, a = _modulated(x_ref[...], nw, shift, scale)
            dhx = dhx_sc[i]
            dsh, dsc = _colsum(dhx), _colsum(dhx * a)
            da = dhx * (1.0 + scale)
            gnw_o[...] += _colsum(da * xn)
            dxn = da * nw
            gx_ref[...] = dx1_ref[...] + r * (dxn - xn * jnp.mean(dxn * xn, axis=-1, keepdims=True))

            @pl.when(i == 0)
            def _():
                dmc_o[0:1, :] += dsh
                dmc_o[1:2, :] += dsc

            @pl.when(i > 0)
            def _():
                dmx_o[0:1, :] += dsh
                dmx_o[1:2, :] += dsc

    prev = lambda b, i: (jnp.maximum(i - 1, 0), 0)
    return pl.pallas_call(
        body, name="b1_in_bwd", grid=(5, NT),
        in_specs=[pl.BlockSpec((TM, D), lambda b, i: (i, 0)), pl.BlockSpec((TM, D), prev),
                  pl.BlockSpec((1, TM, E), lambda b, i: (b, i, 0)), VMEM_SPEC,
                  pl.BlockSpec((1, 2, D), lambda b, i: (jnp.minimum(i, 1), 0, 0)), pl.BlockSpec((D, E), lambda b, i: (0, b))],
        out_specs=[pl.BlockSpec((TM, D), lambda b, i: (jnp.where(b == 4, jnp.maximum(i - 1, 0), 0), 0)),
                   pl.BlockSpec((D, E), lambda b, i: (0, b)), VMEM_SPEC, VMEM_SPEC, VMEM_SPEC],
        out_shape=(_sds((T, D), F32), _sds((D, WIN_COLS), BF16), _sds((2, D), F32), _sds((2, D), F32), _sds((1, D), F32)),
        scratch_shapes=[pltpu.VMEM((NT, TM, D), BF16), pltpu.VMEM((NT, TM, D), F32), pltpu.VMEM((D, E), F32)],
        compiler_params=pltpu.CompilerParams(dimension_semantics=("arbitrary", "arbitrary"), vmem_limit_bytes=VMEM_LIMIT),
    )(xcat, dx1, dg5, nw, msel, win)


def _reduce_bwd(gwin, gwout, gpwin, gpgrp, gpwout, pd, pv, cg, c_ctx, ada_w0):
    n_arr = 8

    def body(gwin_r, gwout_r, gpwin_r, gpgrp_r, gpwout_r, pd_r, pv_r, cg_r, cctx_r, ada_r,
             rwin_o, rwout_o, rpwin_o, rpgrp_o, rpwout_o, gada_o, gadab_o, gcctx_o, pvsum_o, loss_o,
             pd_all, pv_all, dsc_all, dsc_mine, ssem, rsem, lsem):
        x, y, cc, idx = _mesh_pos()

        def srcs(i):
            return [
                gwin_r.at[:, pl.ds(pl.multiple_of(i * SH_WIN, 128), SH_WIN)],
                gwout_r.at[pl.ds(pl.multiple_of(i * SH_ROWS, SH_ROWS), SH_ROWS), :],
                gpwin_r.at[:, pl.ds(pl.multiple_of(i * SH_PWIN, 128), SH_PWIN)],
                gpgrp_r.at[:, pl.ds(pl.multiple_of(i * SH_GRP, SH_GRP), SH_GRP), :],
                gpwout_r.at[pl.ds(pl.multiple_of(i * SH_ROWS, SH_ROWS), SH_ROWS), :],
                pd_r, pv_r, dsc_mine,
            ]

        dsts = [rwin_o.at[idx], rwout_o.at[idx], rpwin_o.at[idx], rpgrp_o.at[idx], rpwout_o.at[idx],
                pd_all.at[idx], pv_all.at[idx], dsc_all.at[idx]]

        def remote(a, k):
            px, py, pc = _peer(x, y, cc, k)
            return pltpu.make_async_remote_copy(src_ref=srcs(4 * px + 2 * py + pc)[a], dst_ref=dsts[a], send_sem=ssem.at[a, k],
                                                recv_sem=rsem.at[a, k], device_id=(px, py, pc), device_id_type=MESH)

        first = [remote(a, k) for k in range(1, NDEV) for a in (5, 6, 0, 1, 2, 3, 4)]
        for cp in first:
            cp.start()
        local = [pltpu.make_async_copy(srcs(idx)[a], dsts[a], lsem.at[a]) for a in range(5)]
        for cp in local:
            cp.start()
        pd_all[idx] = pd_r[...]
        pv_all[idx] = pv_r[...]
        for k in range(1, NDEV):
            remote(5, k).wait_recv()
            remote(6, k).wait_recv()
        mine = [pd_all[s, :, pl.ds(idx, 1), :] for s in range(NDEV)]
        dmc = functools.reduce(lambda u, v: u + v, [m[2] for m in mine])
        rows = _stack_rows([cg_r[i] for i in range(NDEV)] + [cctx_r[...]])
        sc = (rows * _sigmoid(rows)).astype(BF16)
        gada_o[0] = _dot_ta(sc, _stack_rows([m[0] for m in mine] + [dmc]))
        gada_o[1] = _dot_ta(sc, _stack_rows([m[1] for m in mine]))
        dsc_mine[...] = _dot_tb(jnp.broadcast_to(dmc, (8, SH_ADA)), ada_r[...])[0:1, :]
        dsc_all[idx] = dsc_mine[...]
        second = [remote(7, k) for k in range(1, NDEV)]
        for cp in second:
            cp.start()
        tot = [functools.reduce(lambda u, v: u + v, [pd_all[s, l] for s in range(NDEV)]) for l in range(3)]
        gadab_o[0] = tot[0] + tot[2]
        gadab_o[1] = tot[1]
        pvs = functools.reduce(lambda u, v: u + v, [pv_all[s] for s in range(NDEV)])
        pvsum_o[...] = pvs
        loss_o[...] = jnp.broadcast_to(jnp.sum(pvs[:, PV_LOSS:PV_LOSS + D], axis=-1, keepdims=True) * (0.5 / D), (1, 128))
        for k in range(1, NDEV):
            remote(7, k).wait_recv()
        dsc = functools.reduce(lambda u, v: u + v, [dsc_all[s] for s in range(NDEV)])
        cx = cctx_r[...]
        sx = _sigmoid(cx)
        gcctx_o[...] = dsc * (sx * (1.0 + cx * (1.0 - sx)))
        for cp in first + second:
            cp.wait_send()
        for k in range(1, NDEV):
            for a in range(5):
                remote(a, k).wait_recv()
        for cp in local:
            cp.wait()

    outs = (
        _sds((NDEV, D, SH_WIN), BF16), _sds((NDEV, SH_ROWS, D), BF16), _sds((NDEV, D, SH_PWIN), BF16),
        _sds((NDEV, 4, SH_GRP, PG), BF16), _sds((NDEV, SH_ROWS, D), BF16),
        _sds((2, D, SH_ADA), F32), _sds((2, NDEV, SH_ADA), F32), _sds((1, D), F32), _sds((1, PV_LEN), F32), _sds((1, 128), F32),
    )
    return pl.pallas_call(
        body, name="reduce_bwd", out_shape=outs,
        in_specs=[HBM_SPEC] * 5 + [VMEM_SPEC] * 5,
        out_specs=[HBM_SPEC] * 5 + [VMEM_SPEC] * 5,
        scratch_shapes=[
            pltpu.VMEM((NDEV, 3, NDEV, SH_ADA), F32), pltpu.VMEM((NDEV, 1, PV_LEN), F32), pltpu.VMEM((NDEV, 1, D), F32),
            pltpu.VMEM((1, D), F32),
            pltpu.SemaphoreType.DMA((n_arr, NDEV)), pltpu.SemaphoreType.DMA((n_arr, NDEV)), pltpu.SemaphoreType.DMA((5,)),
        ],
        compiler_params=pltpu.CompilerParams(vmem_limit_bytes=VMEM_LIMIT),
    )(gwin, gwout, gpwin, gpgrp, gpwout, pd, pv, cg, c_ctx, ada_w0)


PV_NW, PV_GNORM, PV_FINAL, PV_LB, PV_PSCALE, PV_LOSS, PV_LEN = 0, 2 * D, 3 * D, 4 * D, 6 * D, 7 * D, 8 * D


def _adamw(w, g, m, v):
    m = ADAM_B1 * m + (1.0 - ADAM_B1) * g
    v = ADAM_B2 * v + (1.0 - ADAM_B2) * (g * g)
    m_hat = m / (1.0 - ADAM_B1 ** ADAM_STEP)
    v_hat = v / (1.0 - ADAM_B2 ** ADAM_STEP)
    delta = -ADAM_LR * (m_hat / (jnp.sqrt(v_hat) + ADAM_EPS) + ADAM_WD * w)
    return delta, m, v


def _adam_sharded(name, parts, w, m, v, tr):
    rr, cc = w.shape

    def body(p_ref, w_ref, m_ref, v_ref, g_o, d_o, m_o, v_o):
        g = p_ref[0].astype(F32)
        for s in range(1, NDEV):
            g = g + p_ref[s].astype(F32)
        d, mn, vn = _adamw(w_ref[...], g, m_ref[...], v_ref[...])
        g_o[...], d_o[...], m_o[...], v_o[...] = g, d, mn, vn

    blk = pl.BlockSpec((tr, cc), lambda i: (i, 0))
    return pl.pallas_call(
        body, name=name, grid=(rr // tr,),
        in_specs=[pl.BlockSpec((NDEV, tr, cc), lambda i: (0, i, 0)), blk, blk, blk],
        out_specs=[blk] * 4, out_shape=(_sds((rr, cc), F32),) * 4,
        compiler_params=pltpu.CompilerParams(dimension_semantics=("arbitrary",)),
    )(parts, w, m, v)


def _adam_dense(name, g, w, m, v, tr):
    rr, cc = w.shape

    def body(g_ref, w_ref, m_ref, v_ref, d_o, m_o, v_o):
        d, mn, vn = _adamw(w_ref[...], g_ref[...], m_ref[...], v_ref[...])
        d_o[...], m_o[...], v_o[...] = d, mn, vn

    blk = pl.BlockSpec((tr, cc), lambda i: (i, 0))
    return pl.pallas_call(
        body, name=name, grid=(rr // tr,), in_specs=[blk] * 4, out_specs=[blk] * 3, out_shape=(_sds((rr, cc), F32),) * 3,
        compiler_params=pltpu.CompilerParams(dimension_semantics=("arbitrary",)),
    )(g, w, m, v)


def _adam_small(gs, ws, ms, vs, lb_idx, lbv):
    n = len(ws)

    def body(*refs):
        g_r, w_r, m_r, v_r = refs[:n], refs[n:2 * n], refs[2 * n:3 * n], refs[3 * n:4 * n]
        lb_r = refs[4 * n]
        outs = refs[4 * n + 1:]
        for j in range(n):
            g = g_r[j][...]
            if j == lb_idx:
                lbj = lb_r[...]
                g = g * lbj * (1.0 - lbj)
            d, mn, vn = _adamw(w_r[j][...], g, m_r[j][...], v_r[j][...])
            outs[j][...], outs[n + j][...], outs[2 * n + j][...], outs[3 * n + j][...] = g, d, mn, vn

    shapes = tuple(_sds(w.shape, F32) for w in ws)
    return pl.pallas_call(body, name="adam_small", out_shape=shapes * 4)(*gs, *ws, *ms, *vs, lbv)


def kernel(x, c, ctx, c_ctx, ada_w, ada_b, norm_w, hgrn_w_in, hgrn_lb_logits, hgrn_gnorm_w, hgrn_w_out, pool_w_in, pool_w_grp, pool_scale, pool_w_out, final_norm_w, loss_target, m_c_ctx, m_ada_w, m_ada_b, m_norm_w, m_hgrn_w_in, m_hgrn_lb_logits, m_hgrn_gnorm_w, m_hgrn_w_out, m_pool_w_in, m_pool_w_grp, m_pool_scale, m_pool_w_out, m_final_norm_w, v_c_ctx, v_ada_w, v_ada_b, v_norm_w, v_hgrn_w_in, v_hgrn_lb_logits, v_hgrn_gnorm_w, v_hgrn_w_out, v_pool_w_in, v_pool_w_grp, v_pool_scale, v_pool_w_out, v_final_norm_w):
    idx = 4 * lax.axis_index("x") + 2 * lax.axis_index("y") + lax.axis_index("c")
    cctx2 = c_ctx.reshape(1, D)
    cum01, mask01 = _gla_consts()
    pb, pbt, pinv = _pool_consts()

    win, wout, pwin, pgrp, pwout, lbl_g, ps_g, cg, mod_g = _gather_fwd(
        hgrn_w_in[0], hgrn_w_out[0], pool_w_in[0], pool_w_grp[0], pool_w_out[0], hgrn_lb_logits[0], pool_scale, c, cctx2, ada_w)
    lb = jax.nn.sigmoid(jnp.transpose(lbl_g, (1, 0, 2)).reshape(2, E))
    pscale = ps_g.reshape(1, E)
    mod_all = jnp.transpose(mod_g, (1, 2, 0, 3)).reshape(2, 16, 3 * D) + ada_b[:, None, :]
    mod_me = lax.dynamic_index_in_dim(mod_all, idx, axis=1, keepdims=False)
    mod0, mod1, modc = mod_me[0].reshape(3, D), mod_me[1].reshape(3, D), mod_all[0, NDEV].reshape(3, D)
    msel = jnp.stack([modc[:2], mod0[:2]])
    nw0, nw1 = norm_w[0:1], norm_w[1:2]
    fnw = final_norm_w.reshape(1, D)

    xcat = jnp.concatenate([ctx[0], x[0]], axis=0)
    g_all = _f1_norm_matmul(xcat, nw0, msel, win)
    o = _gla_fwd(g_all, lb, cum01, mask01)
    x1 = _f3_out(o, g_all, xcat, mod0[2:3], hgrn_gnorm_w, wout)
    dx1, gpwin, gpgrp, gpwout, dmod1, gnw1, gfw, gps, lossv = _pool_layer(
        x1, loss_target[0], mod1, nw1, fnw, pwin, pgrp, pscale, pwout, pb, pbt, pinv)
    do, dg5, gwout, dgate0, ggw = _b3_out_bwd(dx1, o, g_all, mod0[2:3], hgrn_gnorm_w, wout)
    dg5, dlb = _gla_bwd(g_all, do, lb, cum01, mask01, dg5)
    grad_x, gwin, dmx, dmc, gnw0 = _b1_in_bwd(xcat, dx1, dg5, nw0, msel, win)

    dmod0 = jnp.concatenate([dmx, dgate0], axis=0)
    dmodc = jnp.concatenate([dmc, jnp.zeros((1, D), F32)], axis=0)
    pd = jnp.stack([dmod0, dmod1, dmodc]).reshape(3, NDEV, SH_ADA)
    pv = jnp.concatenate([gnw0, gnw1, ggw, gfw, dlb.reshape(1, 2 * E), gps, lossv], axis=1)
    rwin, rwout, rpwin, rpgrp, rpwout, g_ada, g_adab, g_cctx, pvsum, loss128 = _reduce_bwd(
        gwin, gwout, gpwin, gpgrp, gpwout, pd, pv, cg, cctx2, ada_w[0])

    out = {}
    out["hgrn_w_in"] = _adam_sharded("adam_w_in", rwin, hgrn_w_in[0], m_hgrn_w_in[0], v_hgrn_w_in[0], 256)
    out["hgrn_w_out"] = _adam_sharded("adam_w_out", rwout, hgrn_w_out[0], m_hgrn_w_out[0], v_hgrn_w_out[0], SH_ROWS)
    out["pool_w_in"] = _adam_sharded("adam_pw_in", rpwin, pool_w_in[0], m_pool_w_in[0], v_pool_w_in[0], 512)
    out["pool_w_grp"] = _adam_sharded("adam_pgrp", rpgrp.reshape(NDEV, 4 * SH_GRP, PG), pool_w_grp[0].reshape(4 * SH_GRP, PG),
                                      m_pool_w_grp[0].reshape(4 * SH_GRP, PG), v_pool_w_grp[0].reshape(4 * SH_GRP, PG), 4 * SH_GRP)
    out["pool_w_out"] = _adam_sharded("adam_pw_out", rpwout, pool_w_out[0], m_pool_w_out[0], v_pool_w_out[0], SH_ROWS)
    g_ada2 = g_ada.reshape(2 * D, SH_ADA)
    out["ada_w"] = (g_ada2,) + _adam_dense("adam_ada_w", g_ada2, ada_w.reshape(2 * D, SH_ADA), m_ada_w.reshape(2 * D, SH_ADA),
                                           v_ada_w.reshape(2 * D, SH_ADA), 512)

    lb_me = lax.dynamic_slice_in_dim(lb, idx * DH, DH, axis=1)
    small = ["c_ctx", "ada_b", "norm_w", "hgrn_lb_logits", "hgrn_gnorm_w", "pool_scale", "final_norm_w"]
    gs = [g_cctx, g_adab.reshape(2, 3 * D), pvsum[:, PV_NW:PV_NW + 2 * D].reshape(2, D),
          lax.dynamic_slice_in_dim(pvsum[:, PV_LB:PV_LB + 2 * E].reshape(2, E), idx * DH, DH, axis=1),
          pvsum[:, PV_GNORM:PV_GNORM + E], lax.dynamic_slice_in_dim(pvsum[:, PV_PSCALE:PV_PSCALE + E], idx * DH, DH, axis=1),
          pvsum[:, PV_FINAL:PV_FINAL + D]]
    ws = [cctx2, ada_b, norm_w, hgrn_lb_logits[0], hgrn_gnorm_w, pool_scale, fnw]
    ms = [m_c_ctx.reshape(1, D), m_ada_b, m_norm_w, m_hgrn_lb_logits[0], m_hgrn_gnorm_w, m_pool_scale, m_final_norm_w.reshape(1, D)]
    vs = [v_c_ctx.reshape(1, D), v_ada_b, v_norm_w, v_hgrn_lb_logits[0], v_hgrn_gnorm_w, v_pool_scale, v_final_norm_w.reshape(1, D)]
    res = _adam_small(gs, ws, ms, vs, 3, lb_me)
    n = len(small)
    for j, name in enumerate(small):
        out[name] = tuple(res[q * n + j] for q in range(4))

    shapes = {"c_ctx": (D,), "ada_w": (2, D, SH_ADA), "ada_b": (2, 3 * D), "norm_w": (2, D), "hgrn_w_in": (1, D, SH_WIN),
              "hgrn_lb_logits": (1, 2, DH), "hgrn_gnorm_w": (1, E), "hgrn_w_out": (1, SH_ROWS, D), "pool_w_in": (1, D, SH_PWIN),
              "pool_w_grp": (1, 4, SH_GRP, PG), "pool_scale": (1, DH), "pool_w_out": (1, SH_ROWS, D), "final_norm_w": (D,)}
    order = ["c_ctx", "ada_w", "ada_b", "norm_w", "hgrn_w_in", "hgrn_lb_logits", "hgrn_gnorm_w", "hgrn_w_out", "pool_w_in",
             "pool_w_grp", "pool_scale", "pool_w_out", "final_norm_w"]
    flat = [out[name][q].reshape(shapes[name]) for q in range(4) for name in order]
    return (loss128[0, 0], grad_x[None], *flat)
```

```python
import functools

import numpy as np
import jax
import jax.numpy as jnp
from jax import lax
from jax.experimental import pallas as pl
from jax.experimental.pallas import tpu as pltpu

F32 = jnp.float32
BF16 = jnp.bfloat16

D = 1024
E = 1024
HEADS = 8
DH = 128
CHUNK = 64
T = 2048
TC = 256
TT = T + TC
TM = 256
NT = TT // TM
NTX = T // TM
NDEV = 8
GRID_W = 64
POOL_WINDOWS = (2, 4, 8, 16)
PG = 256
EPS = 1e-6
WIN_COLS = 5 * E
SH_WIN = WIN_COLS // NDEV
SH_PWIN = 2 * E // NDEV
SH_ROWS = E // NDEV
SH_GRP = PG // NDEV
SH_ADA = 3 * D // NDEV
VMEM_LIMIT = 56 * 1024 * 1024

ADAM_LR, ADAM_B1, ADAM_B2, ADAM_EPS, ADAM_WD, ADAM_STEP = 0.001, 0.9, 0.999, 1e-08, 0.01, 10

MESH = pl.DeviceIdType.MESH
VMEM_SPEC = pl.BlockSpec(memory_space=pltpu.VMEM)
HBM_SPEC = pl.BlockSpec(memory_space=pltpu.HBM)
ANY_SPEC = pl.BlockSpec(memory_space=pl.ANY)


def _sds(shape, dtype):
    return jax.ShapeDtypeStruct(shape, dtype)


def _bf(a):
    return a if a.dtype == BF16 else a.astype(BF16)


def _dot(a, b):
    return lax.dot_general(_bf(a), _bf(b), (((1,), (0,)), ((), ())), preferred_element_type=F32)


def _dot_tb(a, b):
    return lax.dot_general(_bf(a), _bf(b), (((1,), (1,)), ((), ())), preferred_element_type=F32)


def _dot_ta(a, b):
    return lax.dot_general(_bf(a), _bf(b), (((0,), (0,)), ((), ())), preferred_element_type=F32)


def _dot01(m01, x):
    hi = x.astype(BF16)
    lo = (x - hi.astype(F32)).astype(BF16)
    return _dot(m01, hi) + _dot(m01, lo)


def _rstd(x):
    return lax.rsqrt(jnp.mean(x * x, axis=-1, keepdims=True) + EPS)


def _sigmoid(x):
    return jax.nn.sigmoid(x)


def _colsum(a):
    return jnp.sum(a, axis=0, keepdims=True)


def _stack_rows(rows):
    n = rows[0].shape[-1]
    rid = lax.broadcasted_iota(jnp.int32, (16, n), 0)
    out = jnp.zeros((16, n), F32)
    for i, r in enumerate(rows):
        out = jnp.where(rid == i, r, out)
    return out


def _head_map(fn, *arrs):
    outs = [fn(*[a[:, h * DH:(h + 1) * DH] for a in arrs]) for h in range(HEADS)]
    return jnp.concatenate(outs, axis=1)


def _gla_consts():
    r = np.arange(TM)[:, None]
    c = np.arange(TM)[None, :]
    same = (r // CHUNK) == (c // CHUNK)
    tril = same & (c <= r)
    triu = same & (c >= r)
    m = np.stack([tril, triu]).astype(np.float32)
    return jnp.asarray(m, BF16), jnp.asarray(m, F32)


def _pool_consts():
    r = np.arange(TM)[:, None]
    c = np.arange(TM)[None, :]
    same = (r // GRID_W) == (c // GRID_W)
    rp, cp = r % GRID_W, c % GRID_W
    bs, inv = [], []
    for w in POOL_WINDOWS:
        lo = np.clip(rp - w // 2, 0, GRID_W)
        hi = np.clip(rp - w // 2 + w, 0, GRID_W)
        bs.append(same & (cp >= lo) & (cp < hi))
        inv.append(1.0 / (hi - lo).astype(np.float32))
    b = np.stack(bs).astype(np.float32)
    bt = np.transpose(b, (0, 2, 1))
    return jnp.asarray(b, BF16), jnp.asarray(bt, BF16), jnp.asarray(np.stack(inv), F32)


def _mesh_pos():
    x, y, c = lax.axis_index("x"), lax.axis_index("y"), lax.axis_index("c")
    return x, y, c, 4 * x + 2 * y + c


def _peer(x, y, c, k):
    return (x ^ ((k >> 2) & 1), y ^ ((k >> 1) & 1), c ^ (k & 1))


def _gather_small(w_in, w_out, pw_in, pgrp, pw_out, lb_l, pscale, c, c_ctx, ada_w):
    n_arr = 4

    def body(win_r, wout_r, pwin_r, pgrp_r, pwout_r, lb_r, ps_r, c_r, cctx_r, ada_r,
             s_win, s_wout, s_pwin, s_pgrp, s_pwout, lb_o, ps_o, cg_o, mod_o, ssem, rsem):
        x, y, cc, idx = _mesh_pos()
        srcs = [lb_r, ps_r, c_r, mod_o.at[idx]]
        mine = [lb_o.at[idx], ps_o.at[idx], cg_o.at[idx], mod_o.at[idx]]

        def remote(a, k):
            return pltpu.make_async_remote_copy(src_ref=srcs[a], dst_ref=mine[a], send_sem=ssem.at[a, k], recv_sem=rsem.at[a, k],
                                                device_id=_peer(x, y, cc, k), device_id_type=MESH)

        first = [remote(a, k) for k in range(1, NDEV) for a in (2, 0, 1)]
        for cp in first:
            cp.start()
        lb_o[idx] = lb_r[...]
        ps_o[idx] = ps_r[...]
        cg_o[idx] = c_r[...]
        s_win[...] = win_r[...].astype(BF16)
        s_wout[...] = wout_r[...].astype(BF16)
        s_pwin[...] = pwin_r[...].astype(BF16)
        s_pgrp[...] = pgrp_r[...].astype(BF16)
        s_pwout[...] = pwout_r[...].astype(BF16)
        for k in range(1, NDEV):
            remote(2, k).wait_recv()
        rows = _stack_rows([cg_o[i] for i in range(NDEV)] + [cctx_r[...]])
        sc = rows * _sigmoid(rows)
        for l in range(2):
            mod_o[idx, l] = _dot(sc, ada_r[l])
        second = [remote(3, k) for k in range(1, NDEV)]
        for cp in second:
            cp.start()
        for cp in first + second:
            cp.wait_send()
        for k in range(1, NDEV):
            for a in (0, 1, 3):
                remote(a, k).wait_recv()

    outs = (
        _sds((D, SH_WIN), BF16), _sds((SH_ROWS, D), BF16), _sds((D, SH_PWIN), BF16), _sds((4, SH_GRP, PG), BF16), _sds((SH_ROWS, D), BF16),
        _sds((NDEV, 2, DH), F32), _sds((NDEV, 1, DH), F32), _sds((NDEV, 1, D), F32), _sds((NDEV, 2, 16, SH_ADA), F32),
    )
    return pl.pallas_call(
        body, name="gather_small", out_shape=outs,
        in_specs=[VMEM_SPEC] * 10, out_specs=[VMEM_SPEC] * 9,
        scratch_shapes=[pltpu.SemaphoreType.DMA((n_arr, NDEV)), pltpu.SemaphoreType.DMA((n_arr, NDEV))],
        compiler_params=pltpu.CompilerParams(vmem_limit_bytes=VMEM_LIMIT),
    )(w_in, w_out, pw_in, pgrp, pw_out, lb_l, pscale, c, c_ctx, ada_w)


def _gather_order(s):
    if isinstance(s, int):
        return (0, 1, 2, 4, 3, 5, 6, 7)[s]
    return s + (s == 3).astype(jnp.int32) - (s == 4).astype(jnp.int32)


GATHER_ISSUE = (1, 2, 4, 3, 5, 6, 7)


def _weight_slices(refs, i):
    wout, pwin, pgrp, pwout = refs
    return [wout.at[pl.ds(pl.multiple_of(i * SH_ROWS, SH_ROWS), SH_ROWS), :],
            pwin.at[:, pl.ds(pl.multiple_of(i * SH_PWIN, 128), SH_PWIN)],
            pgrp.at[:, pl.ds(pl.multiple_of(i * SH_GRP, SH_GRP), SH_GRP), :],
            pwout.at[pl.ds(pl.multiple_of(i * SH_ROWS, SH_ROWS), SH_ROWS), :]]


def _modulated(x, nw, shift, scale):
    r = _rstd(x)
    xn = x * r
    a = xn * nw
    return a * (1.0 + scale) + shift, r, xn, a


def _f1_gather_matmul(idx1, xcat, nw, msel, s_win):
    def body(idx_ref, x_ref, nw_ref, m_ref, sw_ref, g_ref, win_o, wslot, hx_sc, ssem, rsem, lsem, osem):
        del idx_ref
        s, i = pl.program_id(0), pl.program_id(1)
        x, y, cc, idx = _mesh_pos()
        k = _gather_order(s)
        j = idx ^ k

        def remote(kk):
            return pltpu.make_async_remote_copy(src_ref=sw_ref, dst_ref=wslot.at[idx], send_sem=ssem.at[kk], recv_sem=rsem.at[kk],
                                                device_id=_peer(x, y, cc, kk), device_id_type=MESH)

        own = pltpu.make_async_copy(sw_ref, wslot.at[idx], lsem)

        def to_hbm(jj, kk):
            return pltpu.make_async_copy(wslot.at[jj], win_o.at[:, pl.ds(pl.multiple_of(jj * SH_WIN, 128), SH_WIN)], osem.at[kk])

        @pl.when((s == 0) & (i == 0))
        def _():
            own.start()
            for kk in GATHER_ISSUE:
                remote(kk).start()
            own.wait()

        @pl.when(s == 0)
        def _():
            hx, _, _, _ = _modulated(x_ref[...], nw_ref[...], m_ref[0, 0:1, :], m_ref[0, 1:2, :])
            hx_sc[i] = hx.astype(BF16)

        @pl.when((s > 0) & (i == 0))
        def _():
            remote(k).wait_recv()

        @pl.when(i == 0)
        def _():
            to_hbm(j, k).start()

        g_ref[...] = jnp.dot(hx_sc[i], wslot[j], preferred_element_type=F32)

        @pl.when((s == NDEV - 1) & (i == NT - 1))
        def _():
            for kk in GATHER_ISSUE:
                remote(kk).wait_send()
            for kk in range(NDEV):
                to_hbm(idx ^ kk, kk).wait()

    grid_spec = pltpu.PrefetchScalarGridSpec(
        num_scalar_prefetch=1, grid=(NDEV, NT),
        in_specs=[pl.BlockSpec((TM, D), lambda s, i, ix: (i, 0)), VMEM_SPEC,
                  pl.BlockSpec((1, 2, D), lambda s, i, ix: (jnp.minimum(i, 1), 0, 0)), HBM_SPEC],
        out_specs=[pl.BlockSpec((TM, SH_WIN), lambda s, i, ix: (i, ix[0] ^ _gather_order(s))), HBM_SPEC],
        scratch_shapes=[pltpu.VMEM((NDEV, D, SH_WIN), BF16), pltpu.VMEM((NT, TM, D), BF16),
                        pltpu.SemaphoreType.DMA((NDEV,)), pltpu.SemaphoreType.DMA((NDEV,)), pltpu.SemaphoreType.DMA,
                        pltpu.SemaphoreType.DMA((NDEV,))])
    return pl.pallas_call(
        body, name="f1_gather_matmul", grid_spec=grid_spec,
        out_shape=(_sds((TT, WIN_COLS), F32), _sds((D, WIN_COLS), BF16)),
        compiler_params=pltpu.CompilerParams(dimension_semantics=("arbitrary", "arbitrary"), vmem_limit_bytes=VMEM_LIMIT),
    )(idx1, xcat, nw, msel, s_win)


def _gla_tile(pre, qpre, lbd, cum, rev):
    sig = _sigmoid(pre)
    f = lbd + (1.0 - lbd) * sig
    k = 1.0 - f
    g = _dot01(cum, jnp.log(f))
    g3 = g.reshape(TM // CHUNK, CHUNK, DH)
    last = 0 if rev else CHUNK - 1
    mid = CHUNK // 2 if rev else CHUNK // 2 - 1
    gl1 = [g3[ci, last:last + 1, :] for ci in range(TM // CHUNK)]
    gl = jnp.broadcast_to(g3[:, last:last + 1, :], g3.shape).reshape(TM, DH)
    gm = jnp.broadcast_to(g3[:, mid:mid + 1, :], g3.shape).reshape(TM, DH)
    qsig = _sigmoid(qpre)
    qs = qpre * qsig * (DH ** -0.5)
    e_q, e_k, e_in, e_end = jnp.exp(g - gm), jnp.exp(gm - g), jnp.exp(g), jnp.exp(gl - g)
    return dict(sig=sig, f=f, k=k, qsig=qsig, qs=qs, e_q=e_q, e_k=e_k, e_in=e_in, e_end=e_end,
                qg=qs * e_q, kg=k * e_k, q_in=qs * e_in, kend=k * e_end, decay=[jnp.exp(v) for v in gl1])


def _scan_tile(i, rev):
    t = jnp.where(i == 0, 0, NT - i) if rev else i
    return pl.ds(pl.multiple_of(t * TM, TM), TM)


def _chunk_order(rev):
    n = TM // CHUNK
    return tuple(range(n - 1, -1, -1)) if rev else tuple(range(n))


def _gla_fwd(g_all, lb, cum01, mask01, s_wout, s_pwin, s_pgrp, s_pwout):
    def body(gf_ref, gb_ref, gi_ref, gq_ref, lb_ref, cum_ref, msk_ref, swout_r, spwin_r, spgrp_r, spwout_r,
             o_ref, wout_o, pwin_o, pgrp_o, pwout_o, ob_sc, ssem, rsem, lsem):
        h = pl.program_id(0)
        x, y, cc, idx = _mesh_pos()
        srcs = [swout_r, spwin_r, spgrp_r, spwout_r]
        mine = _weight_slices((wout_o, pwin_o, pgrp_o, pwout_o), idx)

        def remote(a, k):
            return pltpu.make_async_remote_copy(src_ref=srcs[a], dst_ref=mine[a], send_sem=ssem.at[a, k], recv_sem=rsem.at[a, k],
                                                device_id=_peer(x, y, cc, k), device_id_type=MESH)

        copies = [remote(a, k) for k in GATHER_ISSUE for a in range(4)]
        local = [pltpu.make_async_copy(srcs[a], mine[a], lsem.at[a]) for a in range(4)]

        @pl.when(h == 0)
        def _():
            for cp in copies + local:
                cp.start()

        def tile_body(i, sts):
            new = []
            for d in (0, 1):
                rev = d == 1
                st = sts[d]
                rows = _scan_tile(i, rev)
                tl = _gla_tile((gb_ref if rev else gf_ref)[rows, :], gq_ref[rows, :], lb_ref[d:d + 1, :], cum_ref[d], rev)
                v = gi_ref[rows, :].astype(BF16)
                a = _dot_tb(tl["qg"], tl["kg"]) * msk_ref[d]
                intra = _dot(a, v)
                q_in, kend = tl["q_in"].astype(BF16), tl["kend"].astype(BF16)
                outs = [None] * (TM // CHUNK)
                for ci in _chunk_order(rev):
                    r = slice(ci * CHUNK, (ci + 1) * CHUNK)
                    outs[ci] = _dot_tb(q_in[r], st) + intra[r]
                    st = st * tl["decay"][ci] + _dot_ta(v[r], kend[r])
                (ob_sc if rev else o_ref)[rows, :] = jnp.concatenate(outs, axis=0)
                new.append(st)
            return tuple(new)

        zero = jnp.zeros((DH, DH), F32)
        lax.fori_loop(0, NT, tile_body, (zero, zero))
        o_ref[...] += ob_sc[...]

        @pl.when(h == HEADS - 1)
        def _():
            for cp in copies:
                cp.wait_send()
            for cp in copies:
                cp.wait_recv()
            for cp in local:
                cp.wait()

    def col(b):
        return pl.BlockSpec((TT, DH), lambda h, b=b: (0, b * HEADS + h))

    return pl.pallas_call(
        body, name="gla_fwd", grid=(HEADS,),
        in_specs=[col(0), col(1), col(2), col(3), pl.BlockSpec((2, DH), lambda h: (0, h)),
                  pl.BlockSpec((2, TM, TM), lambda h: (0, 0, 0)), pl.BlockSpec((2, TM, TM), lambda h: (0, 0, 0))] + [HBM_SPEC] * 4,
        out_specs=[pl.BlockSpec((TT, DH), lambda h: (0, h))] + [HBM_SPEC] * 4,
        out_shape=(_sds((TT, E), F32), _sds((E, D), BF16), _sds((D, 2 * E), BF16), _sds((4, PG, PG), BF16), _sds((E, D), BF16)),
        scratch_shapes=[pltpu.VMEM((TT, DH), F32), pltpu.SemaphoreType.DMA((4, NDEV)), pltpu.SemaphoreType.DMA((4, NDEV)),
                        pltpu.SemaphoreType.DMA((4,))],
        compiler_params=pltpu.CompilerParams(dimension_semantics=("arbitrary",), vmem_limit_bytes=VMEM_LIMIT),
    )(g_all, g_all, g_all, g_all, lb, cum01, mask01, s_wout, s_pwin, s_pgrp, s_pwout)


def _gated_norm(o, z, gw):
    r = _head_map(lambda oh: jnp.broadcast_to(_rstd(oh), oh.shape), o)
    on = o * r
    zs = _sigmoid(z)
    sz = z * zs
    return on * gw * sz, r, on, zs, sz


def _f3_out(o, g_all, xcat, gate, gw, wout):
    def body(o_ref, z_ref, x_ref, gate_ref, gw_ref, w_ref, x1_ref):
        og, _, _, _, _ = _gated_norm(o_ref[...], z_ref[...], gw_ref[...])
        x1_ref[...] = x_ref[...] + gate_ref[...] * _dot(og, w_ref[...])

    return pl.pallas_call(
        body, name="f3_out", grid=(NTX,),
        in_specs=[pl.BlockSpec((TM, E), lambda i: (i + 1, 0)), pl.BlockSpec((TM, E), lambda i: (i + 1, 4)),
                  pl.BlockSpec((TM, D), lambda i: (i + 1, 0)), pl.BlockSpec((1, D), lambda i: (0, 0)),
                  pl.BlockSpec((1, E), lambda i: (0, 0)), pl.BlockSpec((E, D), lambda i: (0, 0))],
        out_specs=pl.BlockSpec((TM, D), lambda i: (i, 0)),
        out_shape=_sds((T, D), F32),
        compiler_params=pltpu.CompilerParams(dimension_semantics=("arbitrary",)),
    )(o, g_all, xcat, gate, gw, wout)


def _pool_layer(x1, tgt, mod1, nw1, fnw, pwin, pgrp, pscale, pwout, pb, pbt, pinv):
    def body(x_ref, t_ref, m_ref, nw_ref, fw_ref, pwin_ref, pgrp_ref, ps_ref, pwout_ref, pb_ref, pbt_ref, pinv_ref,
             dx_ref, gpwin_o, gpgrp_o, gpwout_o, dmod_o, gnw_o, gfw_o, gps_o, loss_o,
             a_pwin, a_pgrp, a_pwout):
        i = pl.program_id(0)

        @pl.when(i == 0)
        def _():
            for ref in (a_pwin, a_pgrp, a_pwout, dmod_o, gnw_o, gfw_o, gps_o, loss_o):
                ref[...] = jnp.zeros_like(ref)

        shift, scale, gate = m_ref[0:1, :], m_ref[1:2, :], m_ref[2:3, :]
        nw, fw, ps = nw_ref[...], fw_ref[...], ps_ref[...]
        x1 = x_ref[...]
        hx, r1, xn, a = _modulated(x1, nw, shift, scale)
        hxb = hx.astype(BF16)
        uz = jnp.dot(hxb, pwin_ref[...], preferred_element_type=F32)
        u, z = uz[:, :E], uz[:, E:]
        pooled, ys = [], []
        for g in range(4):
            ug = u[:, g * PG:(g + 1) * PG]
            pg = _dot01(pb_ref[g], ug) * pinv_ref[g] - ug
            pooled.append(pg.astype(BF16))
            ys.append(_dot(pooled[g], pgrp_ref[g]))
        ycat = jnp.concatenate(ys, axis=1)
        y = ycat * ps
        zs = _sigmoid(z)
        sz = z * zs
        p = (y * sz).astype(BF16)
        out = _dot(p, pwout_ref[...])
        x2 = x1 + gate * out
        r2 = _rstd(x2)
        xn2 = x2 * r2
        diff = xn2 * fw - t_ref[...]
        loss_o[...] += _colsum(diff * diff)
        dyf = diff * (1.0 / D)
        gfw_o[...] += _colsum(dyf * xn2)
        dxn2 = dyf * fw
        dx2 = r2 * (dxn2 - xn2 * jnp.mean(dxn2 * xn2, axis=-1, keepdims=True))
        dgate = _colsum(dx2 * out)
        dout = (dx2 * gate).astype(BF16)
        for j in range(4):
            cs = slice(j * PG, (j + 1) * PG)
            a_pwout[:, cs] += _dot_ta(p, dout[:, cs])
        dp = _dot_tb(dout, pwout_ref[...])
        dy = dp * sz
        dz = dp * y * (zs * (1.0 + z * (1.0 - zs)))
        gps_o[...] += _colsum(dy * ycat)
        dycat = dy * ps
        dus = []
        for g in range(4):
            dyg = dycat[:, g * PG:(g + 1) * PG].astype(BF16)
            a_pgrp[g] += _dot_ta(pooled[g], dyg)
            dpg = _dot_tb(dyg, pgrp_ref[g])
            dus.append(_dot01(pbt_ref[g], dpg * pinv_ref[g]) - dpg)
        duz = jnp.concatenate(dus + [dz], axis=1).astype(BF16)
        for j in range(2 * E // PG):
            cs = slice(j * PG, (j + 1) * PG)
            a_pwin[:, cs] += _dot_ta(hxb, duz[:, cs])
        dhx = _dot_tb(duz, pwin_ref[...])
        dmod_o[0:1, :] += _colsum(dhx)
        dmod_o[1:2, :] += _colsum(dhx * a)
        dmod_o[2:3, :] += dgate
        da = dhx * (1.0 + scale)
        gnw_o[...] += _colsum(da * xn)
        dxn = da * nw
        dx_ref[...] = dx2 + r1 * (dxn - xn * jnp.mean(dxn * xn, axis=-1, keepdims=True))

        @pl.when(i == NTX - 1)
        def _():
            gpwin_o[...] = a_pwin[...].astype(BF16)
            gpgrp_o[...] = a_pgrp[...].astype(BF16)
            gpwout_o[...] = a_pwout[...].astype(BF16)

    tile = pl.BlockSpec((TM, D), lambda i: (i, 0))
    outs = (_sds((T, D), F32), _sds((D, 2 * E), BF16), _sds((4, PG, PG), BF16), _sds((E, D), BF16),
            _sds((3, D), F32), _sds((1, D), F32), _sds((1, D), F32), _sds((1, E), F32), _sds((1, D), F32))
    return pl.pallas_call(
        body, name="pool_layer", grid=(NTX,),
        in_specs=[tile, tile] + [VMEM_SPEC] * 10,
        out_specs=[tile] + [VMEM_SPEC] * 8,
        out_shape=outs,
        scratch_shapes=[pltpu.VMEM((D, 2 * E), F32), pltpu.VMEM((4, PG, PG), F32), pltpu.VMEM((E, D), F32)],
        compiler_params=pltpu.CompilerParams(dimension_semantics=("arbitrary",), vmem_limit_bytes=VMEM_LIMIT),
    )(x1, tgt, mod1, nw1, fnw, pwin, pgrp, pscale, pwout, pb, pbt, pinv)


def _b3_out_bwd(dx1, o, g_all, gate, gw, wout):
    def body(dx_ref, o_ref, z_ref, gate_ref, gw_ref, w_ref, do_ref, dz_ref, gw_o, dgate_o, ggw_o, acc):
        i = pl.program_id(0)

        @pl.when(i == 0)
        def _():
            acc[...] = jnp.zeros_like(acc)
            dgate_o[...] = jnp.zeros_like(dgate_o)
            ggw_o[...] = jnp.zeros_like(ggw_o)
            do_ref[...] = jnp.zeros_like(do_ref)
            dz_ref[...] = jnp.zeros_like(dz_ref)

        @pl.when(i > 0)
        def _():
            gw = gw_ref[...]
            z = z_ref[...]
            og, r, on, zs, sz = _gated_norm(o_ref[...], z, gw)
            ogb = og.astype(BF16)
            dx = dx_ref[...]
            dgate_o[...] += _colsum(dx * _dot(ogb, w_ref[...]))
            dy = (dx * gate_ref[...]).astype(BF16)
            for j in range(4):
                cs = slice(j * PG, (j + 1) * PG)
                acc[:, cs] += _dot_ta(ogb, dy[:, cs])
            dog = _dot_tb(dy, w_ref[...])
            dz_ref[...] = (dog * (on * gw) * (zs * (1.0 + z * (1.0 - zs)))).astype(BF16)
            dong = dog * sz
            ggw_o[...] += _colsum(dong * on)
            don = dong * gw
            do = _head_map(lambda dh, nh, rh: rh * (dh - nh * jnp.mean(dh * nh, axis=-1, keepdims=True)), don, on, r)
            do_ref[...] = do.astype(BF16)

        @pl.when(i == NT - 1)
        def _():
            gw_o[...] = acc[...].astype(BF16)

    prev = lambda i: (jnp.maximum(i - 1, 0), 0)
    return pl.pallas_call(
        body, name="b3_out_bwd", grid=(NT,),
        in_specs=[pl.BlockSpec((TM, D), prev), pl.BlockSpec((TM, E), lambda i: (i, 0)), pl.BlockSpec((TM, E), lambda i: (i, 4)),
                  VMEM_SPEC, VMEM_SPEC, VMEM_SPEC],
        out_specs=[pl.BlockSpec((TM, E), lambda i: (i, 0)), pl.BlockSpec((TM, E), lambda i: (i, 4)),
                   VMEM_SPEC, VMEM_SPEC, VMEM_SPEC],
        out_shape=(_sds((TT, E), BF16), _sds((TT, WIN_COLS), BF16), _sds((E, D), BF16), _sds((1, D), F32), _sds((1, E), F32)),
        scratch_shapes=[pltpu.VMEM((E, D), F32)],
        compiler_params=pltpu.CompilerParams(dimension_semantics=("arbitrary",), vmem_limit_bytes=VMEM_LIMIT),
    )(dx1, o, g_all, gate, gw, wout)


def _gla_bwd(g_all, do, lb, cum01, mask01, dg, gwout, gpwin, gpgrp, gpwout):
    nch = TM // CHUNK

    def body(gf_ref, gb_ref, gi_ref, gq_ref, do_ref, lb_ref, cum_ref, msk_ref, dg_in, gwout_r, gpwin_r, gpgrp_r, gpwout_r,
             dg_o, dlb_ref, rwout_o, rpwin_o, rpgrp_o, rpwout_o,
             ss_sc, dv_sc, dq_sc, obuf, osem, ssem, rsem, lsem):
        del dg_in
        h = pl.program_id(0)
        x, y, cc, idx = _mesh_pos()
        grads = (gwout_r, gpwin_r, gpgrp_r, gpwout_r)
        dsts = [rwout_o.at[idx], rpwin_o.at[idx], rpgrp_o.at[idx], rpwout_o.at[idx]]

        def remote(a, k):
            px, py, pc = _peer(x, y, cc, k)
            return pltpu.make_async_remote_copy(src_ref=_weight_slices(grads, 4 * px + 2 * py + pc)[a], dst_ref=dsts[a],
                                                send_sem=ssem.at[a, k], recv_sem=rsem.at[a, k], device_id=(px, py, pc), device_id_type=MESH)

        copies = [remote(a, k) for k in GATHER_ISSUE for a in range(4)]
        local = [pltpu.make_async_copy(_weight_slices(grads, idx)[a], dsts[a], lsem.at[a]) for a in range(4)]

        @pl.when(h == 0)
        def _():
            for cp in copies + local:
                cp.start()

        slot = h & 1

        def out_copy(sl, t, hh):
            return pltpu.make_async_copy(obuf.at[sl, t], dg_o.at[:, pl.ds(pl.multiple_of(t * E + hh * DH, DH), DH)], osem.at[sl, t])

        @pl.when(h >= 2)
        def _():
            for t in range(4):
                out_copy(slot, t, h - 2).wait()

        dg_ref = obuf.at[slot]
        zero = jnp.zeros((DH, DH), F32)

        def fwd_body(i, sts):
            new = []
            for d in (0, 1):
                rev = d == 1
                st = sts[d]
                rows = _scan_tile(i, rev)
                tl = _gla_tile((gb_ref if rev else gf_ref)[rows, :], gq_ref[rows, :], lb_ref[d:d + 1, :], cum_ref[d], rev)
                v = gi_ref[rows, :].astype(BF16)
                kend = tl["kend"].astype(BF16)
                for n, ci in enumerate(_chunk_order(rev)):
                    r = slice(ci * CHUNK, (ci + 1) * CHUNK)
                    ss_sc[d, i * nch + n] = st
                    st = st * tl["decay"][ci] + _dot_ta(v[r], kend[r])
                new.append(st)
            return tuple(new)

        lax.fori_loop(0, NT, fwd_body, (zero, zero))

        def bwd_body(ii, carry):
            i = NT - 1 - ii
            new = []
            for d in (0, 1):
                rev = d == 1
                pre_ref = gb_ref if rev else gf_ref
                order = _chunk_order(rev)
                lbd = lb_ref[d:d + 1, :]
                last = 0 if rev else CHUNK - 1
                dst, dlb = carry[d]
                rows = _scan_tile(i, rev)
                pre, qpre = pre_ref[rows, :], gq_ref[rows, :]
                tl = _gla_tile(pre, qpre, lbd, cum_ref[d], rev)
                v = gi_ref[rows, :].astype(BF16)
                dob = do_ref[rows, :]
                msk = msk_ref[d]
                qgb, kgb = tl["qg"].astype(BF16), tl["kg"].astype(BF16)
                q_in, kend = tl["q_in"].astype(BF16), tl["kend"].astype(BF16)
                a = (_dot_tb(qgb, kgb) * msk).astype(BF16)
                da = (_dot_tb(dob, v) * msk).astype(BF16)
                dqg = _dot(da, kgb)
                dkg = _dot_ta(da, qgb)
                dv_intra = _dot_ta(a, dob)
                dv_l, dkend_l, dqin_l, dgl_l = [None] * nch, [None] * nch, [None] * nch, [None] * nch
                for n in range(nch - 1, -1, -1):
                    ci = order[n]
                    r = slice(ci * CHUNK, (ci + 1) * CHUNK)
                    s_c = ss_sc[d, i * nch + n]
                    dstb = dst.astype(BF16)
                    dv_l[ci] = dv_intra[r] + _dot_tb(kend[r], dstb)
                    dkend_l[ci] = _dot(v[r], dstb)
                    dqin_l[ci] = _dot(dob[r], s_c)
                    ddecay = jnp.sum(s_c * dst, axis=0, keepdims=True)
                    dgl_l[ci] = ddecay * tl["decay"][ci]
                    dst = dst * tl["decay"][ci] + _dot_ta(dob[r], q_in[r])
                dv = jnp.concatenate(dv_l, axis=0)
                dkend = jnp.concatenate(dkend_l, axis=0)
                dqin = jnp.concatenate(dqin_l, axis=0)
                dqs = dqg * tl["e_q"] + dqin * tl["e_in"]
                qsig = tl["qsig"]
                dqpre = dqs * (DH ** -0.5) * (qsig * (1.0 + qpre * (1.0 - qsig)))
                dk = dkg * tl["e_k"] + dkend * tl["e_end"]
                dkk = dkend * tl["kend"]
                dg = dqg * tl["qg"] - dkg * tl["kg"] + dqin * tl["q_in"] - dkk
                dkk3 = dkk.reshape(nch, CHUNK, DH)
                dgl = jnp.concatenate([jnp.broadcast_to(dgl_l[ci] + jnp.sum(dkk3[ci], axis=0, keepdims=True), (CHUNK, DH))
                                       for ci in range(nch)], axis=0)
                pos = lax.broadcasted_iota(jnp.int32, (TM, DH), 0) & (CHUNK - 1)
                dg = dg + jnp.where(pos == last, dgl, 0.0)
                dlf = _dot01(cum_ref[1 - d], dg)
                df = dlf / tl["f"] - dk
                sig = tl["sig"]
                dg_ref[d, rows, :] = (df * (1.0 - lbd) * sig * (1.0 - sig)).astype(BF16)
                dlb = dlb + _colsum(df * (1.0 - sig))
                dv_sc[d, rows, :] = dv
                dq_sc[d, rows, :] = dqpre
                new.append((dst, dlb))
            return tuple(new)

        zlb = jnp.zeros((1, DH), F32)
        res = lax.fori_loop(0, NT, bwd_body, ((zero, zlb), (zero, zlb)))
        for d in (0, 1):
            dlb_ref[d:d + 1, :] = res[d][1]
        dg_ref[2] = (dv_sc[0] + dv_sc[1]).astype(BF16)
        dg_ref[3] = (dq_sc[0] + dq_sc[1]).astype(BF16)
        for t in range(4):
            out_copy(slot, t, h).start()

        @pl.when(h == HEADS - 1)
        def _():
            for t in range(4):
                out_copy(1 - slot, t, h - 1).wait()
                out_copy(slot, t, h).wait()
            for cp in copies:
                cp.wait_send()
            for cp in copies:
                cp.wait_recv()
            for cp in local:
                cp.wait()

    def col(b):
        return pl.BlockSpec((TT, DH), lambda h, b=b: (0, b * HEADS + h))

    outs = (_sds((TT, WIN_COLS), BF16), _sds((2, E), F32), _sds((NDEV, SH_ROWS, D), BF16), _sds((NDEV, D, SH_PWIN), BF16),
            _sds((NDEV, 4, SH_GRP, PG), BF16), _sds((NDEV, SH_ROWS, D), BF16))
    return pl.pallas_call(
        body, name="gla_bwd", grid=(HEADS,),
        in_specs=[col(0), col(1), col(2), col(3), pl.BlockSpec((TT, DH), lambda h: (0, h)), pl.BlockSpec((2, DH), lambda h: (0, h)),
                  pl.BlockSpec((2, TM, TM), lambda h: (0, 0, 0)), pl.BlockSpec((2, TM, TM), lambda h: (0, 0, 0))] + [HBM_SPEC] * 5,
        out_specs=[HBM_SPEC, pl.BlockSpec((2, DH), lambda h: (0, h))] + [HBM_SPEC] * 4,
        out_shape=outs,
        scratch_shapes=[pltpu.VMEM((2, NT * nch, DH, DH), F32), pltpu.VMEM((2, TT, DH), F32), pltpu.VMEM((2, TT, DH), F32),
                        pltpu.VMEM((2, 4, TT, DH), BF16), pltpu.SemaphoreType.DMA((2, 4)),
                        pltpu.SemaphoreType.DMA((4, NDEV)), pltpu.SemaphoreType.DMA((4, NDEV)), pltpu.SemaphoreType.DMA((4,))],
        input_output_aliases={8: 0},
        compiler_params=pltpu.CompilerParams(dimension_semantics=("arbitrary",), vmem_limit_bytes=VMEM_LIMIT),
    )(g_all, g_all, g_all, g_all, do, lb, cum01, mask01, dg, gwout, gpwin, gpgrp, gpwout)


def _b1_in_bwd(idx1, xcat, dx1, dg, nw, msel, win):
    last_s = NDEV - 1

    def body(idx_ref, x_ref, dx1_ref, dg_ref, nw_ref, m_ref, w_ref, gx_ref, rwin_o, dmx_o, dmc_o, gnw_o,
             hx_sc, dhx_sc, acc, sbuf, ssem, rsem, lsem):
        del idx_ref
        s, i = pl.program_id(0), pl.program_id(1)
        x, y, cc, idx = _mesh_pos()
        k = _gather_order(last_s - s)
        shift, scale = m_ref[0, 0:1, :], m_ref[0, 1:2, :]

        def remote(kk, sl):
            return pltpu.make_async_remote_copy(src_ref=sbuf.at[sl], dst_ref=rwin_o.at[idx], send_sem=ssem.at[kk], recv_sem=rsem.at[kk],
                                                device_id=_peer(x, y, cc, kk), device_id_type=MESH)

        own = pltpu.make_async_copy(sbuf.at[last_s & 1], rwin_o.at[idx], lsem)

        @pl.when((s == 0) & (i == 0))
        def _():
            for ref in (dmx_o, dmc_o, gnw_o):
                ref[...] = jnp.zeros_like(ref)

        @pl.when(s == 0)
        def _():
            hx, _, _, _ = _modulated(x_ref[...], nw_ref[...], shift, scale)
            hx_sc[i] = hx.astype(BF16)

        @pl.when(i == 0)
        def _():
            acc[...] = jnp.zeros_like(acc)

        dgb = dg_ref[...]
        hxb = hx_sc[i]
        for lo, hi in ((0, 256), (256, 512), (512, SH_WIN)):
            acc[:, lo:hi] += _dot_ta(hxb, dgb[:, lo:hi])
        part = _dot_tb(dgb, w_ref[...])

        @pl.when(s == 0)
        def _():
            dhx_sc[i] = part

        @pl.when(s > 0)
        def _():
            dhx_sc[i] += part

        @pl.when((i == NT - 1) & (s < last_s))
        def _():
            sl = s & 1

            @pl.when(s >= 2)
            def _():
                remote(_gather_order(last_s - (s - 2)), sl).wait_send()
            sbuf[sl] = acc[...].astype(BF16)
            remote(k, sl).start()

        @pl.when((i == NT - 1) & (s == last_s))
        def _():
            remote(_gather_order(2), last_s & 1).wait_send()
            sbuf[last_s & 1] = acc[...].astype(BF16)
            own.start()

        @pl.when(s == last_s)
        def _():
            nw = nw_ref[...]
            _, r, xn, a = _modulated(x_ref[...], nw, shift, scale)
            dhx = dhx_sc[i]
            dsh, dsc = _colsum(dhx), _colsum(dhx * a)
            da = dhx * (1.0 + scale)
            gnw_o[...] += _colsum(da * xn)
            dxn = da * nw
            gx_ref[...] = dx1_ref[...] + r * (dxn - xn * jnp.mean(dxn * xn, axis=-1, keepdims=True))

            @pl.when(i == 0)
            def _():
                dmc_o[0:1, :] += dsh
                dmc_o[1:2, :] += dsc

            @pl.when(i > 0)
            def _():
                dmx_o[0:1, :] += dsh
                dmx_o[1:2, :] += dsc

        @pl.when((i == NT - 1) & (s == last_s))
        def _():
            remote(_gather_order(1), 1 - (last_s & 1)).wait_send()
            for kk in GATHER_ISSUE:
                remote(kk, 0).wait_recv()
            own.wait()

    grid_spec = pltpu.PrefetchScalarGridSpec(
        num_scalar_prefetch=1, grid=(NDEV, NT),
        in_specs=[pl.BlockSpec((TM, D), lambda s, i, ix: (i, 0)), pl.BlockSpec((TM, D), lambda s, i, ix: (jnp.maximum(i - 1, 0), 0)),
                  pl.BlockSpec((TM, SH_WIN), lambda s, i, ix: (i, ix[0] ^ _gather_order(last_s - s))), VMEM_SPEC,
                  pl.BlockSpec((1, 2, D), lambda s, i, ix: (jnp.minimum(i, 1), 0, 0)),
                  pl.BlockSpec((D, SH_WIN), lambda s, i, ix: (0, ix[0] ^ _gather_order(last_s - s)))],
        out_specs=[pl.BlockSpec((TM, D), lambda s, i, ix: (jnp.where(s == last_s, jnp.maximum(i - 1, 0), 0), 0)),
                   HBM_SPEC, VMEM_SPEC, VMEM_SPEC, VMEM_SPEC],
        scratch_shapes=[pltpu.VMEM((NT, TM, D), BF16), pltpu.VMEM((NT, TM, D), F32), pltpu.VMEM((D, SH_WIN), F32),
                        pltpu.VMEM((2, D, SH_WIN), BF16),
                        pltpu.SemaphoreType.DMA((NDEV,)), pltpu.SemaphoreType.DMA((NDEV,)), pltpu.SemaphoreType.DMA])
    return pl.pallas_call(
        body, name="b1_in_bwd", grid_spec=grid_spec,
        out_shape=(_sds((T, D), F32), _sds((NDEV, D, SH_WIN), BF16), _sds((2, D), F32), _sds((2, D), F32), _sds((1, D), F32)),
        compiler_params=pltpu.CompilerParams(dimension_semantics=("arbitrary", "arbitrary"), vmem_limit_bytes=VMEM_LIMIT),
    )(idx1, xcat, dx1, dg, nw, msel, win)


def _reduce_small(pd, pv, cg, c_ctx, ada_w0):
    n_arr = 3

    def body(pd_r, pv_r, cg_r, cctx_r, ada_r, gada_o, gadab_o, gcctx_o, pvsum_o, loss_o,
             pd_all, pv_all, dsc_all, dsc_mine, ssem, rsem):
        x, y, cc, idx = _mesh_pos()
        srcs = [pd_r, pv_r, dsc_mine]
        dsts = [pd_all.at[idx], pv_all.at[idx], dsc_all.at[idx]]

        def remote(a, k):
            return pltpu.make_async_remote_copy(src_ref=srcs[a], dst_ref=dsts[a], send_sem=ssem.at[a, k], recv_sem=rsem.at[a, k],
                                                device_id=_peer(x, y, cc, k), device_id_type=MESH)

        first = [remote(a, k) for k in range(1, NDEV) for a in (0, 1)]
        for cp in first:
            cp.start()
        pd_all[idx] = pd_r[...]
        pv_all[idx] = pv_r[...]
        for k in range(1, NDEV):
            remote(0, k).wait_recv()
            remote(1, k).wait_recv()
        mine = [pd_all[s, :, pl.ds(idx, 1), :] for s in range(NDEV)]
        dmc = functools.reduce(lambda u, v: u + v, [m[2] for m in mine])
        rows = _stack_rows([cg_r[i] for i in range(NDEV)] + [cctx_r[...]])
        sc = (rows * _sigmoid(rows)).astype(BF16)
        gada_o[0] = _dot_ta(sc, _stack_rows([m[0] for m in mine] + [dmc]))
        gada_o[1] = _dot_ta(sc, _stack_rows([m[1] for m in mine]))
        dsc_mine[...] = _dot_tb(jnp.broadcast_to(dmc, (8, SH_ADA)), ada_r[...])[0:1, :]
        dsc_all[idx] = dsc_mine[...]
        second = [remote(2, k) for k in range(1, NDEV)]
        for cp in second:
            cp.start()
        tot = [functools.reduce(lambda u, v: u + v, [pd_all[s, l] for s in range(NDEV)]) for l in range(3)]
        gadab_o[0] = tot[0] + tot[2]
        gadab_o[1] = tot[1]
        pvs = functools.reduce(lambda u, v: u + v, [pv_all[s] for s in range(NDEV)])
        pvsum_o[...] = pvs
        loss_o[...] = jnp.broadcast_to(jnp.sum(pvs[:, PV_LOSS:PV_LOSS + D], axis=-1, keepdims=True) * (0.5 / D), (1, 128))
        for k in range(1, NDEV):
            remote(2, k).wait_recv()
        dsc = functools.reduce(lambda u, v: u + v, [dsc_all[s] for s in range(NDEV)])
        cx = cctx_r[...]
        sx = _sigmoid(cx)
        gcctx_o[...] = dsc * (sx * (1.0 + cx * (1.0 - sx)))
        for cp in first + second:
            cp.wait_send()

    outs = (_sds((2, D, SH_ADA), F32), _sds((2, NDEV, SH_ADA), F32), _sds((1, D), F32), _sds((1, PV_LEN), F32), _sds((1, 128), F32))
    return pl.pallas_call(
        body, name="reduce_small", out_shape=outs,
        in_specs=[VMEM_SPEC] * 5, out_specs=[VMEM_SPEC] * 5,
        scratch_shapes=[
            pltpu.VMEM((NDEV, 3, NDEV, SH_ADA), F32), pltpu.VMEM((NDEV, 1, PV_LEN), F32), pltpu.VMEM((NDEV, 1, D), F32),
            pltpu.VMEM((1, D), F32),
            pltpu.SemaphoreType.DMA((n_arr, NDEV)), pltpu.SemaphoreType.DMA((n_arr, NDEV)),
        ],
        compiler_params=pltpu.CompilerParams(vmem_limit_bytes=VMEM_LIMIT),
    )(pd, pv, cg, c_ctx, ada_w0)


PV_NW, PV_GNORM, PV_FINAL, PV_LB, PV_PSCALE, PV_LOSS, PV_LEN = 0, 2 * D, 3 * D, 4 * D, 6 * D, 7 * D, 8 * D


def _adamw(w, g, m, v):
    m = ADAM_B1 * m + (1.0 - ADAM_B1) * g
    v = ADAM_B2 * v + (1.0 - ADAM_B2) * (g * g)
    m_hat = m / (1.0 - ADAM_B1 ** ADAM_STEP)
    v_hat = v / (1.0 - ADAM_B2 ** ADAM_STEP)
    delta = -ADAM_LR * (m_hat / (jnp.sqrt(v_hat) + ADAM_EPS) + ADAM_WD * w)
    return delta, m, v


def _adam_sharded(name, parts, w, m, v, tr):
    rr, cc = w.shape

    def body(p_ref, w_ref, m_ref, v_ref, g_o, d_o, m_o, v_o):
        g = p_ref[0].astype(F32)
        for s in range(1, NDEV):
            g = g + p_ref[s].astype(F32)
        d, mn, vn = _adamw(w_ref[...], g, m_ref[...], v_ref[...])
        g_o[...], d_o[...], m_o[...], v_o[...] = g, d, mn, vn

    blk = pl.BlockSpec((tr, cc), lambda i: (i, 0))
    return pl.pallas_call(
        body, name=name, grid=(rr // tr,),
        in_specs=[pl.BlockSpec((NDEV, tr, cc), lambda i: (0, i, 0)), blk, blk, blk],
        out_specs=[blk] * 4, out_shape=(_sds((rr, cc), F32),) * 4,
        compiler_params=pltpu.CompilerParams(dimension_semantics=("arbitrary",)),
    )(parts, w, m, v)


def _adam_dense(name, g, w, m, v, tr):
    rr, cc = w.shape

    def body(g_ref, w_ref, m_ref, v_ref, d_o, m_o, v_o):
        d, mn, vn = _adamw(w_ref[...], g_ref[...], m_ref[...], v_ref[...])
        d_o[...], m_o[...], v_o[...] = d, mn, vn

    blk = pl.BlockSpec((tr, cc), lambda i: (i, 0))
    return pl.pallas_call(
        body, name=name, grid=(rr // tr,), in_specs=[blk] * 4, out_specs=[blk] * 3, out_shape=(_sds((rr, cc), F32),) * 3,
        compiler_params=pltpu.CompilerParams(dimension_semantics=("arbitrary",)),
    )(g, w, m, v)


def _adam_small(gs, ws, ms, vs, lb_idx, lbv):
    n = len(ws)

    def body(*refs):
        g_r, w_r, m_r, v_r = refs[:n], refs[n:2 * n], refs[2 * n:3 * n], refs[3 * n:4 * n]
        lb_r = refs[4 * n]
        outs = refs[4 * n + 1:]
        for j in range(n):
            g = g_r[j][...]
            if j == lb_idx:
                lbj = lb_r[...]
                g = g * lbj * (1.0 - lbj)
            d, mn, vn = _adamw(w_r[j][...], g, m_r[j][...], v_r[j][...])
            outs[j][...], outs[n + j][...], outs[2 * n + j][...], outs[3 * n + j][...] = g, d, mn, vn

    shapes = tuple(_sds(w.shape, F32) for w in ws)
    return pl.pallas_call(body, name="adam_small", out_shape=shapes * 4)(*gs, *ws, *ms, *vs, lbv)


def kernel(x, c, ctx, c_ctx, ada_w, ada_b, norm_w, hgrn_w_in, hgrn_lb_logits, hgrn_gnorm_w, hgrn_w_out, pool_w_in, pool_w_grp, pool_scale, pool_w_out, final_norm_w, loss_target, m_c_ctx, m_ada_w, m_ada_b, m_norm_w, m_hgrn_w_in, m_hgrn_lb_logits, m_hgrn_gnorm_w, m_hgrn_w_out, m_pool_w_in, m_pool_w_grp, m_pool_scale, m_pool_w_out, m_final_norm_w, v_c_ctx, v_ada_w, v_ada_b, v_norm_w, v_hgrn_w_in, v_hgrn_lb_logits, v_hgrn_gnorm_w, v_hgrn_w_out, v_pool_w_in, v_pool_w_grp, v_pool_scale, v_pool_w_out, v_final_norm_w):
    idx = 4 * lax.axis_index("x") + 2 * lax.axis_index("y") + lax.axis_index("c")
    cctx2 = c_ctx.reshape(1, D)
    cum01, mask01 = _gla_consts()
    pb, pbt, pinv = _pool_consts()

    idx1 = idx.reshape(1).astype(jnp.int32)
    s_win, s_wout, s_pwin, s_pgrp, s_pwout, lbl_g, ps_g, cg, mod_g = _gather_small(
        hgrn_w_in[0], hgrn_w_out[0], pool_w_in[0], pool_w_grp[0], pool_w_out[0], hgrn_lb_logits[0], pool_scale, c, cctx2, ada_w)
    lb = jax.nn.sigmoid(jnp.transpose(lbl_g, (1, 0, 2)).reshape(2, E))
    pscale = ps_g.reshape(1, E)
    mod_all = jnp.transpose(mod_g, (1, 2, 0, 3)).reshape(2, 16, 3 * D) + ada_b[:, None, :]
    mod_me = lax.dynamic_index_in_dim(mod_all, idx, axis=1, keepdims=False)
    mod0, mod1, modc = mod_me[0].reshape(3, D), mod_me[1].reshape(3, D), mod_all[0, NDEV].reshape(3, D)
    msel = jnp.stack([modc[:2], mod0[:2]])
    nw0, nw1 = norm_w[0:1], norm_w[1:2]
    fnw = final_norm_w.reshape(1, D)

    xcat = jnp.concatenate([ctx[0], x[0]], axis=0)
    g_all, win = _f1_gather_matmul(idx1, xcat, nw0, msel, s_win)
    o, wout, pwin, pgrp, pwout = _gla_fwd(g_all, lb, cum01, mask01, s_wout, s_pwin, s_pgrp, s_pwout)
    x1 = _f3_out(o, g_all, xcat, mod0[2:3], hgrn_gnorm_w, wout)
    dx1, gpwin, gpgrp, gpwout, dmod1, gnw1, gfw, gps, lossv = _pool_layer(
        x1, loss_target[0], mod1, nw1, fnw, pwin, pgrp, pscale, pwout, pb, pbt, pinv)
    do, dg, gwout, dgate0, ggw = _b3_out_bwd(dx1, o, g_all, mod0[2:3], hgrn_gnorm_w, wout)
    dg, dlb, rwout, rpwin, rpgrp, rpwout = _gla_bwd(g_all, do, lb, cum01, mask01, dg, gwout, gpwin, gpgrp, gpwout)
    grad_x, rwin, dmx, dmc, gnw0 = _b1_in_bwd(idx1, xcat, dx1, dg, nw0, msel, win)

    dmod0 = jnp.concatenate([dmx, dgate0], axis=0)
    dmodc = jnp.concatenate([dmc, jnp.zeros((1, D), F32)], axis=0)
    pd = jnp.stack([dmod0, dmod1, dmodc]).reshape(3, NDEV, SH_ADA)
    pv = jnp.concatenate([gnw0, gnw1, ggw, gfw, dlb.reshape(1, 2 * E), gps, lossv], axis=1)
    g_ada, g_adab, g_cctx, pvsum, loss128 = _reduce_small(pd, pv, cg, cctx2, ada_w[0])

    out = {}
    out["hgrn_w_in"] = _adam_sharded("adam_w_in", rwin, hgrn_w_in[0], m_hgrn_w_in[0], v_hgrn_w_in[0], 256)
    out["hgrn_w_out"] = _adam_sharded("adam_w_out", rwout, hgrn_w_out[0], m_hgrn_w_out[0], v_hgrn_w_out[0], SH_ROWS)
    out["pool_w_in"] = _adam_sharded("adam_pw_in", rpwin, pool_w_in[0], m_pool_w_in[0], v_pool_w_in[0], 512)
    out["pool_w_grp"] = _adam_sharded("adam_pgrp", rpgrp.reshape(NDEV, 4 * SH_GRP, PG), pool_w_grp[0].reshape(4 * SH_GRP, PG),
                                      m_pool_w_grp[0].reshape(4 * SH_GRP, PG), v_pool_w_grp[0].reshape(4 * SH_GRP, PG), 4 * SH_GRP)
    out["pool_w_out"] = _adam_sharded("adam_pw_out", rpwout, pool_w_out[0], m_pool_w_out[0], v_pool_w_out[0], SH_ROWS)
    g_ada2 = g_ada.reshape(2 * D, SH_ADA)
    out["ada_w"] = (g_ada2,) + _adam_dense("adam_ada_w", g_ada2, ada_w.reshape(2 * D, SH_ADA), m_ada_w.reshape(2 * D, SH_ADA),
                                           v_ada_w.reshape(2 * D, SH_ADA), 512)

    lb_me = lax.dynamic_slice_in_dim(lb, idx * DH, DH, axis=1)
    small = ["c_ctx", "ada_b", "norm_w", "hgrn_lb_logits", "hgrn_gnorm_w", "pool_scale", "final_norm_w"]
    gs = [g_cctx, g_adab.reshape(2, 3 * D), pvsum[:, PV_NW:PV_NW + 2 * D].reshape(2, D),
          lax.dynamic_slice_in_dim(pvsum[:, PV_LB:PV_LB + 2 * E].reshape(2, E), idx * DH, DH, axis=1),
          pvsum[:, PV_GNORM:PV_GNORM + E], lax.dynamic_slice_in_dim(pvsum[:, PV_PSCALE:PV_PSCALE + E], idx * DH, DH, axis=1),
          pvsum[:, PV_FINAL:PV_FINAL + D]]
    ws = [cctx2, ada_b, norm_w, hgrn_lb_logits[0], hgrn_gnorm_w, pool_scale, fnw]
    ms = [m_c_ctx.reshape(1, D), m_ada_b, m_norm_w, m_hgrn_lb_logits[0], m_hgrn_gnorm_w, m_pool_scale, m_final_norm_w.reshape(1, D)]
    vs = [v_c_ctx.reshape(1, D), v_ada_b, v_norm_w, v_hgrn_lb_logits[0], v_hgrn_gnorm_w, v_pool_scale, v_final_norm_w.reshape(1, D)]
    res = _adam_small(gs, ws, ms, vs, 3, lb_me)
    n = len(small)
    for j, name in enumerate(small):
        out[name] = tuple(res[q * n + j] for q in range(4))

    shapes = {"c_ctx": (D,), "ada_w": (2, D, SH_ADA), "ada_b": (2, 3 * D), "norm_w": (2, D), "hgrn_w_in": (1, D, SH_WIN),
              "hgrn_lb_logits": (1, 2, DH), "hgrn_gnorm_w": (1, E), "hgrn_w_out": (1, SH_ROWS, D), "pool_w_in": (1, D, SH_PWIN),
              "pool_w_grp": (1, 4, SH_GRP, PG), "pool_scale": (1, DH), "pool_w_out": (1, SH_ROWS, D), "final_norm_w": (D,)}
    order = ["c_ctx", "ada_w", "ada_b", "norm_w", "hgrn_w_in", "hgrn_lb_logits", "hgrn_gnorm_w", "hgrn_w_out", "pool_w_in",
             "pool_w_grp", "pool_scale", "pool_w_out", "final_norm_w"]
    flat = [out[name][q].reshape(shapes[name]) for q in range(4) for name in order]
    return (loss128[0, 0], grad_x[None], *flat)
```

```python
import functools

import numpy as np
import jax
import jax.numpy as jnp
from jax import lax
from jax.experimental import pallas as pl
from jax.experimental.pallas import tpu as pltpu

F32 = jnp.float32
BF16 = jnp.bfloat16

D = 1024
E = 1024
HEADS = 8
DH = 128
CHUNK = 64
T = 2048
TC = 256
TT = T + TC
TM = 256
NT = TT // TM
NTX = T // TM
NDEV = 8
GRID_W = 64
POOL_WINDOWS = (2, 4, 8, 16)
PG = 256
EPS = 1e-6
WIN_COLS = 5 * E
SH_WIN = WIN_COLS // NDEV
SH_PWIN = 2 * E // NDEV
SH_ROWS = E // NDEV
SH_GRP = PG // NDEV
SH_ADA = 3 * D // NDEV
VMEM_LIMIT = 56 * 1024 * 1024

ADAM_LR, ADAM_B1, ADAM_B2, ADAM_EPS, ADAM_WD, ADAM_STEP = 0.001, 0.9, 0.999, 1e-08, 0.01, 10

MESH = pl.DeviceIdType.MESH
VMEM_SPEC = pl.BlockSpec(memory_space=pltpu.VMEM)
HBM_SPEC = pl.BlockSpec(memory_space=pltpu.HBM)
ANY_SPEC = pl.BlockSpec(memory_space=pl.ANY)


def _sds(shape, dtype):
    return jax.ShapeDtypeStruct(shape, dtype)


def _bf(a):
    return a if a.dtype == BF16 else a.astype(BF16)


def _dot(a, b):
    return lax.dot_general(_bf(a), _bf(b), (((1,), (0,)), ((), ())), preferred_element_type=F32)


def _dot_tb(a, b):
    return lax.dot_general(_bf(a), _bf(b), (((1,), (1,)), ((), ())), preferred_element_type=F32)


def _dot_ta(a, b):
    return lax.dot_general(_bf(a), _bf(b), (((0,), (0,)), ((), ())), preferred_element_type=F32)


def _dot01(m01, x):
    hi = x.astype(BF16)
    lo = (x - hi.astype(F32)).astype(BF16)
    return _dot(m01, hi) + _dot(m01, lo)


def _rstd(x):
    return lax.rsqrt(jnp.mean(x * x, axis=-1, keepdims=True) + EPS)


def _sigmoid(x):
    return jax.nn.sigmoid(x)


def _colsum(a):
    return jnp.sum(a, axis=0, keepdims=True)


def _stack_rows(rows):
    n = rows[0].shape[-1]
    rid = lax.broadcasted_iota(jnp.int32, (16, n), 0)
    out = jnp.zeros((16, n), F32)
    for i, r in enumerate(rows):
        out = jnp.where(rid == i, r, out)
    return out


def _head_map(fn, *arrs):
    outs = [fn(*[a[:, h * DH:(h + 1) * DH] for a in arrs]) for h in range(HEADS)]
    return jnp.concatenate(outs, axis=1)


def _gla_consts():
    r = np.arange(TM)[:, None]
    c = np.arange(TM)[None, :]
    same = (r // CHUNK) == (c // CHUNK)
    tril = same & (c <= r)
    triu = same & (c >= r)
    m = np.stack([tril, triu]).astype(np.float32)
    return jnp.asarray(m, BF16), jnp.asarray(m, F32)


def _pool_consts():
    r = np.arange(TM)[:, None]
    c = np.arange(TM)[None, :]
    same = (r // GRID_W) == (c // GRID_W)
    rp, cp = r % GRID_W, c % GRID_W
    bs, inv = [], []
    for w in POOL_WINDOWS:
        lo = np.clip(rp - w // 2, 0, GRID_W)
        hi = np.clip(rp - w // 2 + w, 0, GRID_W)
        bs.append(same & (cp >= lo) & (cp < hi))
        inv.append(1.0 / (hi - lo).astype(np.float32))
    b = np.stack(bs).astype(np.float32)
    bt = np.transpose(b, (0, 2, 1))
    return jnp.asarray(b, BF16), jnp.asarray(bt, BF16), jnp.asarray(np.stack(inv), F32)


def _mesh_pos():
    x, y, c = lax.axis_index("x"), lax.axis_index("y"), lax.axis_index("c")
    return x, y, c, 4 * x + 2 * y + c


def _peer(x, y, c, k):
    return (x ^ ((k >> 2) & 1), y ^ ((k >> 1) & 1), c ^ (k & 1))


def _gather_small(w_in, w_out, pw_in, pgrp, pw_out, lb_l, pscale, c, c_ctx, ada_w):
    n_arr = 4

    def body(win_r, wout_r, pwin_r, pgrp_r, pwout_r, lb_r, ps_r, c_r, cctx_r, ada_r,
             s_win, s_wout, s_pwin, s_pgrp, s_pwout, lb_o, ps_o, cg_o, mod_o, ssem, rsem):
        x, y, cc, idx = _mesh_pos()
        srcs = [lb_r, ps_r, c_r, mod_o.at[idx]]
        mine = [lb_o.at[idx], ps_o.at[idx], cg_o.at[idx], mod_o.at[idx]]

        def remote(a, k):
            return pltpu.make_async_remote_copy(src_ref=srcs[a], dst_ref=mine[a], send_sem=ssem.at[a, k], recv_sem=rsem.at[a, k],
                                                device_id=_peer(x, y, cc, k), device_id_type=MESH)

        first = [remote(a, k) for k in range(1, NDEV) for a in (2, 0, 1)]
        for cp in first:
            cp.start()
        lb_o[idx] = lb_r[...]
        ps_o[idx] = ps_r[...]
        cg_o[idx] = c_r[...]
        s_win[...] = win_r[...].astype(BF16)
        s_wout[...] = wout_r[...].astype(BF16)
        s_pwin[...] = pwin_r[...].astype(BF16)
        s_pgrp[...] = pgrp_r[...].astype(BF16)
        s_pwout[...] = pwout_r[...].astype(BF16)
        for k in range(1, NDEV):
            remote(2, k).wait_recv()
        rows = _stack_rows([cg_o[i] for i in range(NDEV)] + [cctx_r[...]])
        sc = rows * _sigmoid(rows)
        for l in range(2):
            mod_o[idx, l] = _dot(sc, ada_r[l])
        second = [remote(3, k) for k in range(1, NDEV)]
        for cp in second:
            cp.start()
        for cp in first + second:
            cp.wait_send()
        for k in range(1, NDEV):
            for a in (0, 1, 3):
                remote(a, k).wait_recv()

    outs = (
        _sds((D, SH_WIN), BF16), _sds((SH_ROWS, D), BF16), _sds((D, SH_PWIN), BF16), _sds((4, SH_GRP, PG), BF16), _sds((SH_ROWS, D), BF16),
        _sds((NDEV, 2, DH), F32), _sds((NDEV, 1, DH), F32), _sds((NDEV, 1, D), F32), _sds((NDEV, 2, 16, SH_ADA), F32),
    )
    return pl.pallas_call(
        body, name="gather_small", out_shape=outs,
        in_specs=[VMEM_SPEC] * 10, out_specs=[VMEM_SPEC] * 9,
        scratch_shapes=[pltpu.SemaphoreType.DMA((n_arr, NDEV)), pltpu.SemaphoreType.DMA((n_arr, NDEV))],
        compiler_params=pltpu.CompilerParams(vmem_limit_bytes=VMEM_LIMIT),
    )(w_in, w_out, pw_in, pgrp, pw_out, lb_l, pscale, c, c_ctx, ada_w)


def _gather_order(s):
    if isinstance(s, int):
        return (0, 1, 2, 4, 3, 5, 6, 7)[s]
    return s + (s == 3).astype(jnp.int32) - (s == 4).astype(jnp.int32)


GATHER_ISSUE = (1, 2, 4, 3, 5, 6, 7)


def _weight_slices(refs, i):
    wout, pwin, pgrp, pwout = refs
    return [wout.at[pl.ds(pl.multiple_of(i * SH_ROWS, SH_ROWS), SH_ROWS), :],
            pwin.at[:, pl.ds(pl.multiple_of(i * SH_PWIN, 128), SH_PWIN)],
            pgrp.at[:, pl.ds(pl.multiple_of(i * SH_GRP, SH_GRP), SH_GRP), :],
            pwout.at[pl.ds(pl.multiple_of(i * SH_ROWS, SH_ROWS), SH_ROWS), :]]


def _modulated(x, nw, shift, scale):
    r = _rstd(x)
    xn = x * r
    a = xn * nw
    return a * (1.0 + scale) + shift, r, xn, a


def _f1_gather_matmul(idx1, xcat, nw, msel, s_win):
    def body(idx_ref, x_ref, nw_ref, m_ref, sw_ref, g_ref, win_o, wslot, hx_sc, ssem, rsem, lsem, osem):
        del idx_ref
        s, i = pl.program_id(0), pl.program_id(1)
        x, y, cc, idx = _mesh_pos()
        k = _gather_order(s)
        j = idx ^ k

        def remote(kk):
            return pltpu.make_async_remote_copy(src_ref=sw_ref, dst_ref=wslot.at[idx], send_sem=ssem.at[kk], recv_sem=rsem.at[kk],
                                                device_id=_peer(x, y, cc, kk), device_id_type=MESH)

        own = pltpu.make_async_copy(sw_ref, wslot.at[idx], lsem)

        def to_hbm(jj, kk):
            return pltpu.make_async_copy(wslot.at[jj], win_o.at[:, pl.ds(pl.multiple_of(jj * SH_WIN, 128), SH_WIN)], osem.at[kk])

        @pl.when((s == 0) & (i == 0))
        def _():
            own.start()
            for kk in GATHER_ISSUE:
                remote(kk).start()
            own.wait()

        @pl.when(s == 0)
        def _():
            hx, _, _, _ = _modulated(x_ref[...], nw_ref[...], m_ref[0, 0:1, :], m_ref[0, 1:2, :])
            hx_sc[i] = hx.astype(BF16)

        @pl.when((s > 0) & (i == 0))
        def _():
            remote(k).wait_recv()

        @pl.when(i == 0)
        def _():
            to_hbm(j, k).start()

        g_ref[...] = jnp.dot(hx_sc[i], wslot[j], preferred_element_type=F32)

        @pl.when((s == NDEV - 1) & (i == NT - 1))
        def _():
            for kk in GATHER_ISSUE:
                remote(kk).wait_send()
            for kk in range(NDEV):
                to_hbm(idx ^ kk, kk).wait()

    grid_spec = pltpu.PrefetchScalarGridSpec(
        num_scalar_prefetch=1, grid=(NDEV, NT),
        in_specs=[pl.BlockSpec((TM, D), lambda s, i, ix: (i, 0)), VMEM_SPEC,
                  pl.BlockSpec((1, 2, D), lambda s, i, ix: (jnp.minimum(i, 1), 0, 0)), HBM_SPEC],
        out_specs=[pl.BlockSpec((TM, SH_WIN), lambda s, i, ix: (i, ix[0] ^ _gather_order(s))), HBM_SPEC],
        scratch_shapes=[pltpu.VMEM((NDEV, D, SH_WIN), BF16), pltpu.VMEM((NT, TM, D), BF16),
                        pltpu.SemaphoreType.DMA((NDEV,)), pltpu.SemaphoreType.DMA((NDEV,)), pltpu.SemaphoreType.DMA,
                        pltpu.SemaphoreType.DMA((NDEV,))])
    return pl.pallas_call(
        body, name="f1_gather_matmul", grid_spec=grid_spec,
        out_shape=(_sds((TT, WIN_COLS), F32), _sds((D, WIN_COLS), BF16)),
        compiler_params=pltpu.CompilerParams(dimension_semantics=("arbitrary", "arbitrary"), vmem_limit_bytes=VMEM_LIMIT),
    )(idx1, xcat, nw, msel, s_win)


def _gla_gates(pre, qpre, lbd, cum, rev):
    rows, n = pre.shape
    nch = rows // CHUNK
    sig = _sigmoid(pre)
    f = lbd + (1.0 - lbd) * sig
    k = 1.0 - f
    g = _dot01(cum, jnp.log(f))
    g3 = g.reshape(nch, CHUNK, n)
    last = 0 if rev else CHUNK - 1
    mid = CHUNK // 2 if rev else CHUNK // 2 - 1
    gl1, gm1 = g3[:, last:last + 1, :], g3[:, mid:mid + 1, :]

    def bc(a):
        return jnp.broadcast_to(a, g3.shape).reshape(rows, n)

    gm = bc(gm1)
    e_q, e_k = jnp.exp(g - gm), jnp.exp(gm - g)
    e_in, e_end = e_q * bc(jnp.exp(gm1)), e_k * bc(jnp.exp(gl1 - gm1))
    qsig = _sigmoid(qpre)
    qs = qpre * qsig * (DH ** -0.5)
    return dict(sig=sig, f=f, k=k, qsig=qsig, qs=qs, e_q=e_q, e_k=e_k, e_in=e_in, e_end=e_end,
                decay=[jnp.exp(g3[ci, last:last + 1, :]) for ci in range(nch)])


def _gla_prep(g_all, lb, cum01):
    def body(g_ref, lb_ref, cum_ref, p0_ref, p1_ref, v_ref, dec_ref):
        qpre = g_ref[:, 3 * E:4 * E]
        v_ref[...] = g_ref[:, 2 * E:3 * E].astype(BF16)
        dec_ref[...] = jnp.zeros_like(dec_ref)
        for d, p_ref in ((0, p0_ref), (1, p1_ref)):
            t = _gla_gates(g_ref[:, d * E:(d + 1) * E], qpre, lb_ref[d:d + 1, :], cum_ref[d], d == 1)
            p_ref[0] = (t["qs"] * t["e_q"]).astype(BF16)
            p_ref[1] = (t["k"] * t["e_k"]).astype(BF16)
            p_ref[2] = (t["qs"] * t["e_in"]).astype(BF16)
            p_ref[3] = (t["k"] * t["e_end"]).astype(BF16)
            for ci in range(TM // CHUNK):
                dec_ref[d, 0, ci:ci + 1, :] = t["decay"][ci]

    quad = pl.BlockSpec((4, TM, E), lambda i: (0, i, 0))
    return pl.pallas_call(
        body, name="gla_prep", grid=(NT,),
        in_specs=[pl.BlockSpec((TM, WIN_COLS), lambda i: (i, 0)), VMEM_SPEC, VMEM_SPEC],
        out_specs=[quad, quad, pl.BlockSpec((TM, E), lambda i: (i, 0)), pl.BlockSpec((2, 1, 8, E), lambda i: (0, i, 0, 0))],
        out_shape=(_sds((4, TT, E), BF16), _sds((4, TT, E), BF16), _sds((TT, E), BF16), _sds((2, NT, 8, E), F32)),
        compiler_params=pltpu.CompilerParams(dimension_semantics=("arbitrary",), vmem_limit_bytes=VMEM_LIMIT),
    )(g_all, lb, cum01)


def _scan_tile(i, rev):
    t = jnp.where(i == 0, 0, NT - i) if rev else i
    return t, pl.ds(pl.multiple_of(t * TM, TM), TM)


def _chunk_order(rev):
    n = TM // CHUNK
    return tuple(range(n - 1, -1, -1)) if rev else tuple(range(n))


def _gla_fwd(p0, p1, v_all, dec, mask01, s_wout, s_pwin, s_pgrp, s_pwout):
    def body(p0_ref, p1_ref, v_ref, dec_ref, msk_ref, swout_r, spwin_r, spgrp_r, spwout_r,
             o_ref, wout_o, pwin_o, pgrp_o, pwout_o, ob_sc, ssem, rsem, lsem):
        h = pl.program_id(0)
        x, y, cc, idx = _mesh_pos()
        srcs = [swout_r, spwin_r, spgrp_r, spwout_r]
        mine = _weight_slices((wout_o, pwin_o, pgrp_o, pwout_o), idx)

        def remote(a, k):
            return pltpu.make_async_remote_copy(src_ref=srcs[a], dst_ref=mine[a], send_sem=ssem.at[a, k], recv_sem=rsem.at[a, k],
                                                device_id=_peer(x, y, cc, k), device_id_type=MESH)

        copies = [remote(a, k) for k in GATHER_ISSUE for a in range(4)]
        local = [pltpu.make_async_copy(srcs[a], mine[a], lsem.at[a]) for a in range(4)]

        @pl.when(h == 0)
        def _():
            for cp in copies + local:
                cp.start()

        def tile_body(i, sts):
            new = []
            for d, p_ref in ((0, p0_ref), (1, p1_ref)):
                rev = d == 1
                st = sts[d]
                t, rows = _scan_tile(i, rev)
                v = v_ref[rows, :]
                a = _dot_tb(p_ref[0, rows, :], p_ref[1, rows, :]) * msk_ref[d]
                intra = _dot(a, v)
                q_in, kend = p_ref[2, rows, :], p_ref[3, rows, :]
                outs = [None] * (TM // CHUNK)
                for ci in _chunk_order(rev):
                    r = slice(ci * CHUNK, (ci + 1) * CHUNK)
                    outs[ci] = _dot_tb(q_in[r], st) + intra[r]
                    st = st * dec_ref[d, t, ci:ci + 1, :] + _dot_ta(v[r], kend[r])
                (ob_sc if rev else o_ref)[rows, :] = jnp.concatenate(outs, axis=0)
                new.append(st)
            return tuple(new)

        zero = jnp.zeros((DH, DH), F32)
        lax.fori_loop(0, NT, tile_body, (zero, zero))
        o_ref[...] += ob_sc[...]

        @pl.when(h == HEADS - 1)
        def _():
            for cp in copies:
                cp.wait_send()
            for cp in copies:
                cp.wait_recv()
            for cp in local:
                cp.wait()

    quad = pl.BlockSpec((4, TT, DH), lambda h: (0, 0, h))
    return pl.pallas_call(
        body, name="gla_fwd", grid=(HEADS,),
        in_specs=[quad, quad, pl.BlockSpec((TT, DH), lambda h: (0, h)), pl.BlockSpec((2, NT, 8, DH), lambda h: (0, 0, 0, h)),
                  pl.BlockSpec((2, TM, TM), lambda h: (0, 0, 0))] + [HBM_SPEC] * 4,
        out_specs=[pl.BlockSpec((TT, DH), lambda h: (0, h))] + [HBM_SPEC] * 4,
        out_shape=(_sds((TT, E), F32), _sds((E, D), BF16), _sds((D, 2 * E), BF16), _sds((4, PG, PG), BF16), _sds((E, D), BF16)),
        scratch_shapes=[pltpu.VMEM((TT, DH), F32), pltpu.SemaphoreType.DMA((4, NDEV)), pltpu.SemaphoreType.DMA((4, NDEV)),
                        pltpu.SemaphoreType.DMA((4,))],
        compiler_params=pltpu.CompilerParams(dimension_semantics=("arbitrary",), vmem_limit_bytes=VMEM_LIMIT),
    )(p0, p1, v_all, dec, mask01, s_wout, s_pwin, s_pgrp, s_pwout)


def _gated_norm(o, z, gw):
    r = _head_map(lambda oh: jnp.broadcast_to(_rstd(oh), oh.shape), o)
    on = o * r
    zs = _sigmoid(z)
    sz = z * zs
    return on * gw * sz, r, on, zs, sz


def _f3_out(o, g_all, xcat, gate, gw, wout):
    def body(o_ref, z_ref, x_ref, gate_ref, gw_ref, w_ref, x1_ref):
        og, _, _, _, _ = _gated_norm(o_ref[...], z_ref[...], gw_ref[...])
        x1_ref[...] = x_ref[...] + gate_ref[...] * _dot(og, w_ref[...])

    return pl.pallas_call(
        body, name="f3_out", grid=(NTX,),
        in_specs=[pl.BlockSpec((TM, E), lambda i: (i + 1, 0)), pl.BlockSpec((TM, E), lambda i: (i + 1, 4)),
                  pl.BlockSpec((TM, D), lambda i: (i + 1, 0)), pl.BlockSpec((1, D), lambda i: (0, 0)),
                  pl.BlockSpec((1, E), lambda i: (0, 0)), pl.BlockSpec((E, D), lambda i: (0, 0))],
        out_specs=pl.BlockSpec((TM, D), lambda i: (i, 0)),
        out_shape=_sds((T, D), F32),
        compiler_params=pltpu.CompilerParams(dimension_semantics=("arbitrary",)),
    )(o, g_all, xcat, gate, gw, wout)


def _pool_layer(x1, tgt, mod1, nw1, fnw, pwin, pgrp, pscale, pwout, pb, pbt, pinv):
    def body(x_ref, t_ref, m_ref, nw_ref, fw_ref, pwin_ref, pgrp_ref, ps_ref, pwout_ref, pb_ref, pbt_ref, pinv_ref,
             dx_ref, gpwin_o, gpgrp_o, gpwout_o, dmod_o, gnw_o, gfw_o, gps_o, loss_o,
             a_pwin, a_pgrp, a_pwout):
        i = pl.program_id(0)

        @pl.when(i == 0)
        def _():
            for ref in (a_pwin, a_pgrp, a_pwout, dmod_o, gnw_o, gfw_o, gps_o, loss_o):
                ref[...] = jnp.zeros_like(ref)

        shift, scale, gate = m_ref[0:1, :], m_ref[1:2, :], m_ref[2:3, :]
        nw, fw, ps = nw_ref[...], fw_ref[...], ps_ref[...]
        x1 = x_ref[...]
        hx, r1, xn, a = _modulated(x1, nw, shift, scale)
        hxb = hx.astype(BF16)
        uz = jnp.dot(hxb, pwin_ref[...], preferred_element_type=F32)
        u, z = uz[:, :E], uz[:, E:]
        pooled, ys = [], []
        for g in range(4):
            ug = u[:, g * PG:(g + 1) * PG]
            pg = _dot01(pb_ref[g], ug) * pinv_ref[g] - ug
            pooled.append(pg.astype(BF16))
            ys.append(_dot(pooled[g], pgrp_ref[g]))
        ycat = jnp.concatenate(ys, axis=1)
        y = ycat * ps
        zs = _sigmoid(z)
        sz = z * zs
        p = (y * sz).astype(BF16)
        out = _dot(p, pwout_ref[...])
        x2 = x1 + gate * out
        r2 = _rstd(x2)
        xn2 = x2 * r2
        diff = xn2 * fw - t_ref[...]
        loss_o[...] += _colsum(diff * diff)
        dyf = diff * (1.0 / D)
        gfw_o[...] += _colsum(dyf * xn2)
        dxn2 = dyf * fw
        dx2 = r2 * (dxn2 - xn2 * jnp.mean(dxn2 * xn2, axis=-1, keepdims=True))
        dgate = _colsum(dx2 * out)
        dout = (dx2 * gate).astype(BF16)
        for j in range(4):
            cs = slice(j * PG, (j + 1) * PG)
            a_pwout[:, cs] += _dot_ta(p, dout[:, cs])
        dp = _dot_tb(dout, pwout_ref[...])
        dy = dp * sz
        dz = dp * y * (zs * (1.0 + z * (1.0 - zs)))
        gps_o[...] += _colsum(dy * ycat)
        dycat = dy * ps
        dus = []
        for g in range(4):
            dyg = dycat[:, g * PG:(g + 1) * PG].astype(BF16)
            a_pgrp[g] += _dot_ta(pooled[g], dyg)
            dpg = _dot_tb(dyg, pgrp_ref[g])
            dus.append(_dot01(pbt_ref[g], dpg * pinv_ref[g]) - dpg)
        duz = jnp.concatenate(dus + [dz], axis=1).astype(BF16)
        for j in range(2 * E // PG):
            cs = slice(j * PG, (j + 1) * PG)
            a_pwin[:, cs] += _dot_ta(hxb, duz[:, cs])
        dhx = _dot_tb(duz, pwin_ref[...])
        dmod_o[0:1, :] += _colsum(dhx)
        dmod_o[1:2, :] += _colsum(dhx * a)
        dmod_o[2:3, :] += dgate
        da = dhx * (1.0 + scale)
        gnw_o[...] += _colsum(da * xn)
        dxn = da * nw
        dx_ref[...] = dx2 + r1 * (dxn - xn * jnp.mean(dxn * xn, axis=-1, keepdims=True))

        @pl.when(i == NTX - 1)
        def _():
            gpwin_o[...] = a_pwin[...].astype(BF16)
            gpgrp_o[...] = a_pgrp[...].astype(BF16)
            gpwout_o[...] = a_pwout[...].astype(BF16)

    tile = pl.BlockSpec((TM, D), lambda i: (i, 0))
    outs = (_sds((T, D), F32), _sds((D, 2 * E), BF16), _sds((4, PG, PG), BF16), _sds((E, D), BF16),
            _sds((3, D), F32), _sds((1, D), F32), _sds((1, D), F32), _sds((1, E), F32), _sds((1, D), F32))
    return pl.pallas_call(
        body, name="pool_layer", grid=(NTX,),
        in_specs=[tile, tile] + [VMEM_SPEC] * 10,
        out_specs=[tile] + [VMEM_SPEC] * 8,
        out_shape=outs,
        scratch_shapes=[pltpu.VMEM((D, 2 * E), F32), pltpu.VMEM((4, PG, PG), F32), pltpu.VMEM((E, D), F32)],
        compiler_params=pltpu.CompilerParams(dimension_semantics=("arbitrary",), vmem_limit_bytes=VMEM_LIMIT),
    )(x1, tgt, mod1, nw1, fnw, pwin, pgrp, pscale, pwout, pb, pbt, pinv)


def _b3_out_bwd(dx1, o, g_all, gate, gw, wout):
    def body(dx_ref, o_ref, z_ref, gate_ref, gw_ref, w_ref, do_ref, dz_ref, gw_o, dgate_o, ggw_o, acc):
        i = pl.program_id(0)

        @pl.when(i == 0)
        def _():
            acc[...] = jnp.zeros_like(acc)
            dgate_o[...] = jnp.zeros_like(dgate_o)
            ggw_o[...] = jnp.zeros_like(ggw_o)
            do_ref[...] = jnp.zeros_like(do_ref)
            dz_ref[...] = jnp.zeros_like(dz_ref)

        @pl.when(i > 0)
        def _():
            gw = gw_ref[...]
            z = z_ref[...]
            og, r, on, zs, sz = _gated_norm(o_ref[...], z, gw)
            ogb = og.astype(BF16)
            dx = dx_ref[...]
            dgate_o[...] += _colsum(dx * _dot(ogb, w_ref[...]))
            dy = (dx * gate_ref[...]).astype(BF16)
            for j in range(4):
                cs = slice(j * PG, (j + 1) * PG)
                acc[:, cs] += _dot_ta(ogb, dy[:, cs])
            dog = _dot_tb(dy, w_ref[...])
            dz_ref[...] = (dog * (on * gw) * (zs * (1.0 + z * (1.0 - zs)))).astype(BF16)
            dong = dog * sz
            ggw_o[...] += _colsum(dong * on)
            don = dong * gw
            do = _head_map(lambda dh, nh, rh: rh * (dh - nh * jnp.mean(dh * nh, axis=-1, keepdims=True)), don, on, r)
            do_ref[...] = do.astype(BF16)

        @pl.when(i == NT - 1)
        def _():
            gw_o[...] = acc[...].astype(BF16)

    prev = lambda i: (jnp.maximum(i - 1, 0), 0)
    return pl.pallas_call(
        body, name="b3_out_bwd", grid=(NT,),
        in_specs=[pl.BlockSpec((TM, D), prev), pl.BlockSpec((TM, E), lambda i: (i, 0)), pl.BlockSpec((TM, E), lambda i: (i, 4)),
                  VMEM_SPEC, VMEM_SPEC, VMEM_SPEC],
        out_specs=[pl.BlockSpec((TM, E), lambda i: (i, 0)), pl.BlockSpec((TM, E), lambda i: (i, 0)),
                   VMEM_SPEC, VMEM_SPEC, VMEM_SPEC],
        out_shape=(_sds((TT, E), BF16), _sds((TT, E), BF16), _sds((E, D), BF16), _sds((1, D), F32), _sds((1, E), F32)),
        scratch_shapes=[pltpu.VMEM((E, D), F32)],
        compiler_params=pltpu.CompilerParams(dimension_semantics=("arbitrary",), vmem_limit_bytes=VMEM_LIMIT),
    )(dx1, o, g_all, gate, gw, wout)


def _gla_bwd(p0, p1, v_all, dec, do, mask01, gwout, gpwin, gpgrp, gpwout):
    nch = TM // CHUNK

    def body(p0_ref, p1_ref, v_ref, dec_ref, do_ref, msk_ref, gwout_r, gpwin_r, gpgrp_r, gpwout_r,
             d0_ref, d1_ref, dv_ref, dgl_ref, rwout_o, rpwin_o, rpgrp_o, rpwout_o,
             ss_sc, dv_sc, ssem, rsem, lsem):
        h = pl.program_id(0)
        x, y, cc, idx = _mesh_pos()
        grads = (gwout_r, gpwin_r, gpgrp_r, gpwout_r)
        dsts = [rwout_o.at[idx], rpwin_o.at[idx], rpgrp_o.at[idx], rpwout_o.at[idx]]

        def remote(a, k):
            px, py, pc = _peer(x, y, cc, k)
            return pltpu.make_async_remote_copy(src_ref=_weight_slices(grads, 4 * px + 2 * py + pc)[a], dst_ref=dsts[a],
                                                send_sem=ssem.at[a, k], recv_sem=rsem.at[a, k], device_id=(px, py, pc), device_id_type=MESH)

        copies = [remote(a, k) for k in GATHER_ISSUE for a in range(4)]
        local = [pltpu.make_async_copy(_weight_slices(grads, idx)[a], dsts[a], lsem.at[a]) for a in range(4)]

        @pl.when(h == 0)
        def _():
            for cp in copies + local:
                cp.start()

        zero = jnp.zeros((DH, DH), F32)
        dgl_ref[...] = jnp.zeros_like(dgl_ref)

        def fwd_body(i, sts):
            new = []
            for d, p_ref in ((0, p0_ref), (1, p1_ref)):
                rev = d == 1
                st = sts[d]
                t, rows = _scan_tile(i, rev)
                v, kend = v_ref[rows, :], p_ref[3, rows, :]
                for n, ci in enumerate(_chunk_order(rev)):
                    r = slice(ci * CHUNK, (ci + 1) * CHUNK)
                    ss_sc[d, i * nch + n] = st
                    st = st * dec_ref[d, t, ci:ci + 1, :] + _dot_ta(v[r], kend[r])
                new.append(st)
            return tuple(new)

        lax.fori_loop(0, NT, fwd_body, (zero, zero))

        def bwd_body(ii, dsts_):
            i = NT - 1 - ii
            new = []
            for d, p_ref, d_ref in ((0, p0_ref, d0_ref), (1, p1_ref, d1_ref)):
                rev = d == 1
                order = _chunk_order(rev)
                dst = dsts_[d]
                t, rows = _scan_tile(i, rev)
                qg, kg, q_in, kend = p_ref[0, rows, :], p_ref[1, rows, :], p_ref[2, rows, :], p_ref[3, rows, :]
                v, dob, msk = v_ref[rows, :], do_ref[rows, :], msk_ref[d]
                a = (_dot_tb(qg, kg) * msk).astype(BF16)
                da = (_dot_tb(dob, v) * msk).astype(BF16)
                d_ref[0, rows, :] = _dot(da, kg)
                d_ref[1, rows, :] = _dot_ta(da, qg)
                dv_intra = _dot_ta(a, dob)
                dv_l, dkend_l, dqin_l = [None] * nch, [None] * nch, [None] * nch
                for n in range(nch - 1, -1, -1):
                    ci = order[n]
                    r = slice(ci * CHUNK, (ci + 1) * CHUNK)
                    s_c = ss_sc[d, i * nch + n]
                    dec = dec_ref[d, t, ci:ci + 1, :]
                    dstb = dst.astype(BF16)
                    dv_l[ci] = dv_intra[r] + _dot_tb(kend[r], dstb)
                    dkend_l[ci] = _dot(v[r], dstb)
                    dqin_l[ci] = _dot(dob[r], s_c)
                    dgl_ref[d, t, ci:ci + 1, :] = jnp.sum(s_c * dst, axis=0, keepdims=True) * dec
                    dst = dst * dec + _dot_ta(dob[r], q_in[r])
                d_ref[2, rows, :] = jnp.concatenate(dqin_l, axis=0)
                d_ref[3, rows, :] = jnp.concatenate(dkend_l, axis=0)
                dv_sc[d, rows, :] = jnp.concatenate(dv_l, axis=0)
                new.append(dst)
            return tuple(new)

        lax.fori_loop(0, NT, bwd_body, (zero, zero))
        dv_ref[...] = (dv_sc[0] + dv_sc[1]).astype(BF16)

        @pl.when(h == HEADS - 1)
        def _():
            for cp in copies:
                cp.wait_send()
            for cp in copies:
                cp.wait_recv()
            for cp in local:
                cp.wait()

    quad = pl.BlockSpec((4, TT, DH), lambda h: (0, 0, h))
    col = pl.BlockSpec((TT, DH), lambda h: (0, h))
    chunkv = pl.BlockSpec((2, NT, 8, DH), lambda h: (0, 0, 0, h))
    outs = (_sds((4, TT, E), F32), _sds((4, TT, E), F32), _sds((TT, E), BF16), _sds((2, NT, 8, E), F32),
            _sds((NDEV, SH_ROWS, D), BF16), _sds((NDEV, D, SH_PWIN), BF16), _sds((NDEV, 4, SH_GRP, PG), BF16), _sds((NDEV, SH_ROWS, D), BF16))
    return pl.pallas_call(
        body, name="gla_bwd", grid=(HEADS,),
        in_specs=[quad, quad, col, chunkv, col, pl.BlockSpec((2, TM, TM), lambda h: (0, 0, 0))] + [HBM_SPEC] * 4,
        out_specs=[quad, quad, col, chunkv] + [HBM_SPEC] * 4,
        out_shape=outs,
        scratch_shapes=[pltpu.VMEM((2, NT * nch, DH, DH), F32), pltpu.VMEM((2, TT, DH), F32),
                        pltpu.SemaphoreType.DMA((4, NDEV)), pltpu.SemaphoreType.DMA((4, NDEV)), pltpu.SemaphoreType.DMA((4,))],
        compiler_params=pltpu.CompilerParams(dimension_semantics=("arbitrary",), vmem_limit_bytes=VMEM_LIMIT),
    )(p0, p1, v_all, dec, do, mask01, gwout, gpwin, gpgrp, gpwout)


TMB = 128


def _gla_post_bwd(g_all, d0, d1, dgl, dv, dz, lb, cum01):
    nch = TMB // CHUNK

    def body(g_ref, d0_ref, d1_ref, dgl_ref, dv_ref, dz_ref, lb_ref, cum_ref, dg_ref, dlb_ref):
        i = pl.program_id(0)

        @pl.when(i == 0)
        def _():
            dlb_ref[...] = jnp.zeros_like(dlb_ref)

        half = i & 1
        qpre = g_ref[:, 3 * E:4 * E]
        dqs_sum = None
        dpre = []
        for d, d_ref in ((0, d0_ref), (1, d1_ref)):
            rev = d == 1
            lbd = lb_ref[d:d + 1, :]
            t = _gla_gates(g_ref[:, d * E:(d + 1) * E], qpre, lbd, cum_ref[d, :TMB, :TMB], rev)
            dqg, dkg, dqin, dkend = d_ref[0], d_ref[1], d_ref[2], d_ref[3]
            dqs = dqg * t["e_q"] + dqin * t["e_in"]
            dk = dkg * t["e_k"] + dkend * t["e_end"]
            dkk = dkend * (t["k"] * t["e_end"])
            dg = t["qs"] * dqs - t["k"] * dk
            dkk3 = dkk.reshape(nch, CHUNK, E)
            dgl8 = dgl_ref[d, 0]
            dgl_rows = [jnp.where(half == 0, dgl8[ci:ci + 1, :], dgl8[nch + ci:nch + ci + 1, :]) for ci in range(nch)]
            dgl_b = jnp.concatenate([jnp.broadcast_to(dgl_rows[ci] + jnp.sum(dkk3[ci], axis=0, keepdims=True), (CHUNK, E))
                                     for ci in range(nch)], axis=0)
            pos = lax.broadcasted_iota(jnp.int32, (TMB, E), 0) & (CHUNK - 1)
            dg = dg + jnp.where(pos == (0 if rev else CHUNK - 1), dgl_b, 0.0)
            dlf = _dot01(cum_ref[1 - d, :TMB, :TMB], dg)
            df = dlf / t["f"] - dk
            sig = t["sig"]
            dpre.append((df * (1.0 - lbd) * sig * (1.0 - sig)).astype(BF16))
            dlb_ref[d:d + 1, :] += _colsum(df * (1.0 - sig))
            dqs_sum = dqs if dqs_sum is None else dqs_sum + dqs
            qsig = t["qsig"]
        dqpre = dqs_sum * (DH ** -0.5) * (qsig * (1.0 + qpre * (1.0 - qsig)))
        dg_ref[...] = jnp.concatenate([dpre[0], dpre[1], dv_ref[...], dqpre.astype(BF16), dz_ref[...]], axis=1)

    quad = pl.BlockSpec((4, TMB, E), lambda i: (0, i, 0))
    tile = pl.BlockSpec((TMB, E), lambda i: (i, 0))
    return pl.pallas_call(
        body, name="gla_post_bwd", grid=(TT // TMB,),
        in_specs=[pl.BlockSpec((TMB, WIN_COLS), lambda i: (i, 0)), quad, quad,
                  pl.BlockSpec((2, 1, 8, E), lambda i: (0, i // 2, 0, 0)), tile, tile, VMEM_SPEC, VMEM_SPEC],
        out_specs=[pl.BlockSpec((TMB, WIN_COLS), lambda i: (i, 0)), VMEM_SPEC],
        out_shape=(_sds((TT, WIN_COLS), BF16), _sds((2, E), F32)),
        compiler_params=pltpu.CompilerParams(dimension_semantics=("arbitrary",), vmem_limit_bytes=VMEM_LIMIT),
    )(g_all, d0, d1, dgl, dv, dz, lb, cum01)


def _b1_in_bwd(idx1, xcat, dx1, dg, nw, msel, win):
    last_s = NDEV - 1

    def body(idx_ref, x_ref, dx1_ref, dg_ref, nw_ref, m_ref, w_ref, gx_ref, rwin_o, dmx_o, dmc_o, gnw_o,
             hx_sc, dhx_sc, acc, sbuf, ssem, rsem, lsem):
        del idx_ref
        s, i = pl.program_id(0), pl.program_id(1)
        x, y, cc, idx = _mesh_pos()
        k = _gather_order(last_s - s)
        shift, scale = m_ref[0, 0:1, :], m_ref[0, 1:2, :]

        def remote(kk, sl):
            return pltpu.make_async_remote_copy(src_ref=sbuf.at[sl], dst_ref=rwin_o.at[idx], send_sem=ssem.at[kk], recv_sem=rsem.at[kk],
                                                device_id=_peer(x, y, cc, kk), device_id_type=MESH)

        own = pltpu.make_async_copy(sbuf.at[last_s & 1], rwin_o.at[idx], lsem)

        @pl.when((s == 0) & (i == 0))
        def _():
            for ref in (dmx_o, dmc_o, gnw_o):
                ref[...] = jnp.zeros_like(ref)

        @pl.when(s == 0)
        def _():
            hx, _, _, _ = _modulated(x_ref[...], nw_ref[...], shift, scale)
            hx_sc[i] = hx.astype(BF16)

        @pl.when(i == 0)
        def _():
            acc[...] = jnp.zeros_like(acc)

        dgb = dg_ref[...]
        hxb = hx_sc[i]
        for lo, hi in ((0, 256), (256, 512), (512, SH_WIN)):
            acc[:, lo:hi] += _dot_ta(hxb, dgb[:, lo:hi])
        part = _dot_tb(dgb, w_ref[...])

        @pl.when(s == 0)
        def _():
            dhx_sc[i] = part

        @pl.when(s > 0)
        def _():
            dhx_sc[i] += part

        @pl.when((i == NT - 1) & (s < last_s))
        def _():
            sl = s & 1

            @pl.when(s >= 2)
            def _():
                remote(_gather_order(last_s - (s - 2)), sl).wait_send()
            sbuf[sl] = acc[...].astype(BF16)
            remote(k, sl).start()

        @pl.when((i == NT - 1) & (s == last_s))
        def _():
            remote(_gather_order(2), last_s & 1).wait_send()
            sbuf[last_s & 1] = acc[...].astype(BF16)
            own.start()

        @pl.when(s == last_s)
        def _():
            nw = nw_ref[...]
            _, r, xn, a = _modulated(x_ref[...], nw, shift, scale)
            dhx = dhx_sc[i]
            dsh, dsc = _colsum(dhx), _colsum(dhx * a)
            da = dhx * (1.0 + scale)
            gnw_o[...] += _colsum(da * xn)
            dxn = da * nw
            gx_ref[...] = dx1_ref[...] + r * (dxn - xn * jnp.mean(dxn * xn, axis=-1, keepdims=True))

            @pl.when(i == 0)
            def _():
                dmc_o[0:1, :] += dsh
                dmc_o[1:2, :] += dsc

            @pl.when(i > 0)
            def _():
                dmx_o[0:1, :] += dsh
                dmx_o[1:2, :] += dsc

        @pl.when((i == NT - 1) & (s == last_s))
        def _():
            remote(_gather_order(1), 1 - (last_s & 1)).wait_send()
            for kk in GATHER_ISSUE:
                remote(kk, 0).wait_recv()
            own.wait()

    grid_spec = pltpu.PrefetchScalarGridSpec(
        num_scalar_prefetch=1, grid=(NDEV, NT),
        in_specs=[pl.BlockSpec((TM, D), lambda s, i, ix: (i, 0)), pl.BlockSpec((TM, D), lambda s, i, ix: (jnp.maximum(i - 1, 0), 0)),
                  pl.BlockSpec((TM, SH_WIN), lambda s, i, ix: (i, ix[0] ^ _gather_order(last_s - s))), VMEM_SPEC,
                  pl.BlockSpec((1, 2, D), lambda s, i, ix: (jnp.minimum(i, 1), 0, 0)),
                  pl.BlockSpec((D, SH_WIN), lambda s, i, ix: (0, ix[0] ^ _gather_order(last_s - s)))],
        out_specs=[pl.BlockSpec((TM, D), lambda s, i, ix: (jnp.where(s == last_s, jnp.maximum(i - 1, 0), 0), 0)),
                   HBM_SPEC, VMEM_SPEC, VMEM_SPEC, VMEM_SPEC],
        scratch_shapes=[pltpu.VMEM((NT, TM, D), BF16), pltpu.VMEM((NT, TM, D), F32), pltpu.VMEM((D, SH_WIN), F32),
                        pltpu.VMEM((2, D, SH_WIN), BF16),
                        pltpu.SemaphoreType.DMA((NDEV,)), pltpu.SemaphoreType.DMA((NDEV,)), pltpu.SemaphoreType.DMA])
    return pl.pallas_call(
        body, name="b1_in_bwd", grid_spec=grid_spec,
        out_shape=(_sds((T, D), F32), _sds((NDEV, D, SH_WIN), BF16), _sds((2, D), F32), _sds((2, D), F32), _sds((1, D), F32)),
        compiler_params=pltpu.CompilerParams(dimension_semantics=("arbitrary", "arbitrary"), vmem_limit_bytes=VMEM_LIMIT),
    )(idx1, xcat, dx1, dg, nw, msel, win)


def _reduce_small(pd, pv, cg, c_ctx, ada_w0):
    n_arr = 3

    def body(pd_r, pv_r, cg_r, cctx_r, ada_r, gada_o, gadab_o, gcctx_o, pvsum_o, loss_o,
             pd_all, pv_all, dsc_all, dsc_mine, ssem, rsem):
        x, y, cc, idx = _mesh_pos()
        srcs = [pd_r, pv_r, dsc_mine]
        dsts = [pd_all.at[idx], pv_all.at[idx], dsc_all.at[idx]]

        def remote(a, k):
            return pltpu.make_async_remote_copy(src_ref=srcs[a], dst_ref=dsts[a], send_sem=ssem.at[a, k], recv_sem=rsem.at[a, k],
                                                device_id=_peer(x, y, cc, k), device_id_type=MESH)

        first = [remote(a, k) for k in range(1, NDEV) for a in (0, 1)]
        for cp in first:
            cp.start()
        pd_all[idx] = pd_r[...]
        pv_all[idx] = pv_r[...]
        for k in range(1, NDEV):
            remote(0, k).wait_recv()
            remote(1, k).wait_recv()
        mine = [pd_all[s, :, pl.ds(idx, 1), :] for s in range(NDEV)]
        dmc = functools.reduce(lambda u, v: u + v, [m[2] for m in mine])
        rows = _stack_rows([cg_r[i] for i in range(NDEV)] + [cctx_r[...]])
        sc = (rows * _sigmoid(rows)).astype(BF16)
        gada_o[0] = _dot_ta(sc, _stack_rows([m[0] for m in mine] + [dmc]))
        gada_o[1] = _dot_ta(sc, _stack_rows([m[1] for m in mine]))
        dsc_mine[...] = _dot_tb(jnp.broadcast_to(dmc, (8, SH_ADA)), ada_r[...])[0:1, :]
        dsc_all[idx] = dsc_mine[...]
        second = [remote(2, k) for k in range(1, NDEV)]
        for cp in second:
            cp.start()
        tot = [functools.reduce(lambda u, v: u + v, [pd_all[s, l] for s in range(NDEV)]) for l in range(3)]
        gadab_o[0] = tot[0] + tot[2]
        gadab_o[1] = tot[1]
        pvs = functools.reduce(lambda u, v: u + v, [pv_all[s] for s in range(NDEV)])
        pvsum_o[...] = pvs
        loss_o[...] = jnp.broadcast_to(jnp.sum(pvs[:, PV_LOSS:PV_LOSS + D], axis=-1, keepdims=True) * (0.5 / D), (1, 128))
        for k in range(1, NDEV):
            remote(2, k).wait_recv()
        dsc = functools.reduce(lambda u, v: u + v, [dsc_all[s] for s in range(NDEV)])
        cx = cctx_r[...]
        sx = _sigmoid(cx)
        gcctx_o[...] = dsc * (sx * (1.0 + cx * (1.0 - sx)))
        for cp in first + second:
            cp.wait_send()

    outs = (_sds((2, D, SH_ADA), F32), _sds((2, NDEV, SH_ADA), F32), _sds((1, D), F32), _sds((1, PV_LEN), F32), _sds((1, 128), F32))
    return pl.pallas_call(
        body, name="reduce_small", out_shape=outs,
        in_specs=[VMEM_SPEC] * 5, out_specs=[VMEM_SPEC] * 5,
        scratch_shapes=[
            pltpu.VMEM((NDEV, 3, NDEV, SH_ADA), F32), pltpu.VMEM((NDEV, 1, PV_LEN), F32), pltpu.VMEM((NDEV, 1, D), F32),
            pltpu.VMEM((1, D), F32),
            pltpu.SemaphoreType.DMA((n_arr, NDEV)), pltpu.SemaphoreType.DMA((n_arr, NDEV)),
        ],
        compiler_params=pltpu.CompilerParams(vmem_limit_bytes=VMEM_LIMIT),
    )(pd, pv, cg, c_ctx, ada_w0)


PV_NW, PV_GNORM, PV_FINAL, PV_LB, PV_PSCALE, PV_LOSS, PV_LEN = 0, 2 * D, 3 * D, 4 * D, 6 * D, 7 * D, 8 * D


def _adamw(w, g, m, v):
    m = ADAM_B1 * m + (1.0 - ADAM_B1) * g
    v = ADAM_B2 * v + (1.0 - ADAM_B2) * (g * g)
    m_hat = m / (1.0 - ADAM_B1 ** ADAM_STEP)
    v_hat = v / (1.0 - ADAM_B2 ** ADAM_STEP)
    delta = -ADAM_LR * (m_hat / (jnp.sqrt(v_hat) + ADAM_EPS) + ADAM_WD * w)
    return delta, m, v


def _adam_sharded(name, parts, w, m, v, tr):
    rr, cc = w.shape

    def body(p_ref, w_ref, m_ref, v_ref, g_o, d_o, m_o, v_o):
        g = p_ref[0].astype(F32)
        for s in range(1, NDEV):
            g = g + p_ref[s].astype(F32)
        d, mn, vn = _adamw(w_ref[...], g, m_ref[...], v_ref[...])
        g_o[...], d_o[...], m_o[...], v_o[...] = g, d, mn, vn

    blk = pl.BlockSpec((tr, cc), lambda i: (i, 0))
    return pl.pallas_call(
        body, name=name, grid=(rr // tr,),
        in_specs=[pl.BlockSpec((NDEV, tr, cc), lambda i: (0, i, 0)), blk, blk, blk],
        out_specs=[blk] * 4, out_shape=(_sds((rr, cc), F32),) * 4,
        compiler_params=pltpu.CompilerParams(dimension_semantics=("arbitrary",)),
    )(parts, w, m, v)


def _adam_dense(name, g, w, m, v, tr):
    rr, cc = w.shape

    def body(g_ref, w_ref, m_ref, v_ref, d_o, m_o, v_o):
        d, mn, vn = _adamw(w_ref[...], g_ref[...], m_ref[...], v_ref[...])
        d_o[...], m_o[...], v_o[...] = d, mn, vn

    blk = pl.BlockSpec((tr, cc), lambda i: (i, 0))
    return pl.pallas_call(
        body, name=name, grid=(rr // tr,), in_specs=[blk] * 4, out_specs=[blk] * 3, out_shape=(_sds((rr, cc), F32),) * 3,
        compiler_params=pltpu.CompilerParams(dimension_semantics=("arbitrary",)),
    )(g, w, m, v)


def _adam_small(gs, ws, ms, vs, lb_idx, lbv):
    n = len(ws)

    def body(*refs):
        g_r, w_r, m_r, v_r = refs[:n], refs[n:2 * n], refs[2 * n:3 * n], refs[3 * n:4 * n]
        lb_r = refs[4 * n]
        outs = refs[4 * n + 1:]
        for j in range(n):
            g = g_r[j][...]
            if j == lb_idx:
                lbj = lb_r[...]
                g = g * lbj * (1.0 - lbj)
            d, mn, vn = _adamw(w_r[j][...], g, m_r[j][...], v_r[j][...])
            outs[j][...], outs[n + j][...], outs[2 * n + j][...], outs[3 * n + j][...] = g, d, mn, vn

    shapes = tuple(_sds(w.shape, F32) for w in ws)
    return pl.pallas_call(body, name="adam_small", out_shape=shapes * 4)(*gs, *ws, *ms, *vs, lbv)


def kernel(x, c, ctx, c_ctx, ada_w, ada_b, norm_w, hgrn_w_in, hgrn_lb_logits, hgrn_gnorm_w, hgrn_w_out, pool_w_in, pool_w_grp, pool_scale, pool_w_out, final_norm_w, loss_target, m_c_ctx, m_ada_w, m_ada_b, m_norm_w, m_hgrn_w_in, m_hgrn_lb_logits, m_hgrn_gnorm_w, m_hgrn_w_out, m_pool_w_in, m_pool_w_grp, m_pool_scale, m_pool_w_out, m_final_norm_w, v_c_ctx, v_ada_w, v_ada_b, v_norm_w, v_hgrn_w_in, v_hgrn_lb_logits, v_hgrn_gnorm_w, v_hgrn_w_out, v_pool_w_in, v_pool_w_grp, v_pool_scale, v_pool_w_out, v_final_norm_w):
    idx = 4 * lax.axis_index("x") + 2 * lax.axis_index("y") + lax.axis_index("c")
    cctx2 = c_ctx.reshape(1, D)
    cum01, mask01 = _gla_consts()
    pb, pbt, pinv = _pool_consts()

    idx1 = idx.reshape(1).astype(jnp.int32)
    s_win, s_wout, s_pwin, s_pgrp, s_pwout, lbl_g, ps_g, cg, mod_g = _gather_small(
        hgrn_w_in[0], hgrn_w_out[0], pool_w_in[0], pool_w_grp[0], pool_w_out[0], hgrn_lb_logits[0], pool_scale, c, cctx2, ada_w)
    lb = jax.nn.sigmoid(jnp.transpose(lbl_g, (1, 0, 2)).reshape(2, E))
    pscale = ps_g.reshape(1, E)
    mod_all = jnp.transpose(mod_g, (1, 2, 0, 3)).reshape(2, 16, 3 * D) + ada_b[:, None, :]
    mod_me = lax.dynamic_index_in_dim(mod_all, idx, axis=1, keepdims=False)
    mod0, mod1, modc = mod_me[0].reshape(3, D), mod_me[1].reshape(3, D), mod_all[0, NDEV].reshape(3, D)
    msel = jnp.stack([modc[:2], mod0[:2]])
    nw0, nw1 = norm_w[0:1], norm_w[1:2]
    fnw = final_norm_w.reshape(1, D)

    xcat = jnp.concatenate([ctx[0], x[0]], axis=0)
    g_all, win = _f1_gather_matmul(idx1, xcat, nw0, msel, s_win)
    p0, p1, v_all, dec = _gla_prep(g_all, lb, cum01)
    o, wout, pwin, pgrp, pwout = _gla_fwd(p0, p1, v_all, dec, mask01, s_wout, s_pwin, s_pgrp, s_pwout)
    x1 = _f3_out(o, g_all, xcat, mod0[2:3], hgrn_gnorm_w, wout)
    dx1, gpwin, gpgrp, gpwout, dmod1, gnw1, gfw, gps, lossv = _pool_layer(
        x1, loss_target[0], mod1, nw1, fnw, pwin, pgrp, pscale, pwout, pb, pbt, pinv)
    do, dz, gwout, dgate0, ggw = _b3_out_bwd(dx1, o, g_all, mod0[2:3], hgrn_gnorm_w, wout)
    d0, d1, dv, dgl, rwout, rpwin, rpgrp, rpwout = _gla_bwd(p0, p1, v_all, dec, do, mask01, gwout, gpwin, gpgrp, gpwout)
    dg, dlb = _gla_post_bwd(g_all, d0, d1, dgl, dv, dz, lb, cum01)
    grad_x, rwin, dmx, dmc, gnw0 = _b1_in_bwd(idx1, xcat, dx1, dg, nw0, msel, win)

    dmod0 = jnp.concatenate([dmx, dgate0], axis=0)
    dmodc = jnp.concatenate([dmc, jnp.zeros((1, D), F32)], axis=0)
    pd = jnp.stack([dmod0, dmod1, dmodc]).reshape(3, NDEV, SH_ADA)
    pv = jnp.concatenate([gnw0, gnw1, ggw, gfw, dlb.reshape(1, 2 * E), gps, lossv], axis=1)
    g_ada, g_adab, g_cctx, pvsum, loss128 = _reduce_small(pd, pv, cg, cctx2, ada_w[0])

    out = {}
    out["hgrn_w_in"] = _adam_sharded("adam_w_in", rwin, hgrn_w_in[0], m_hgrn_w_in[0], v_hgrn_w_in[0], 256)
    out["hgrn_w_out"] = _adam_sharded("adam_w_out", rwout, hgrn_w_out[0], m_hgrn_w_out[0], v_hgrn_w_out[0], SH_ROWS)
    out["pool_w_in"] = _adam_sharded("adam_pw_in", rpwin, pool_w_in[0], m_pool_w_in[0], v_pool_w_in[0], 512)
    out["pool_w_grp"] = _adam_sharded("adam_pgrp", rpgrp.reshape(NDEV, 4 * SH_GRP, PG), pool_w_grp[0].reshape(4 * SH_GRP, PG),
                                      m_pool_w_grp[0].reshape(4 * SH_GRP, PG), v_pool_w_grp[0].reshape(4 * SH_GRP, PG), 4 * SH_GRP)
    out["pool_w_out"] = _adam_sharded("adam_pw_out", rpwout, pool_w_out[0], m_pool_w_out[0], v_pool_w_out[0], SH_ROWS)
    g_ada2 = g_ada.reshape(2 * D, SH_ADA)
    out["ada_w"] = (g_ada2,) + _adam_dense("adam_ada_w", g_ada2, ada_w.reshape(2 * D, SH_ADA), m_ada_w.reshape(2 * D, SH_ADA),
                                           v_ada_w.reshape(2 * D, SH_ADA), 512)

    lb_me = lax.dynamic_slice_in_dim(lb, idx * DH, DH, axis=1)
    small = ["c_ctx", "ada_b", "norm_w", "hgrn_lb_logits", "hgrn_gnorm_w", "pool_scale", "final_norm_w"]
    gs = [g_cctx, g_adab.reshape(2, 3 * D), pvsum[:, PV_NW:PV_NW + 2 * D].reshape(2, D),
          lax.dynamic_slice_in_dim(pvsum[:, PV_LB:PV_LB + 2 * E].reshape(2, E), idx * DH, DH, axis=1),
          pvsum[:, PV_GNORM:PV_GNORM + E], lax.dynamic_slice_in_dim(pvsum[:, PV_PSCALE:PV_PSCALE + E], idx * DH, DH, axis=1),
          pvsum[:, PV_FINAL:PV_FINAL + D]]
    ws = [cctx2, ada_b, norm_w, hgrn_lb_logits[0], hgrn_gnorm_w, pool_scale, fnw]
    ms = [m_c_ctx.reshape(1, D), m_ada_b, m_norm_w, m_hgrn_lb_logits[0], m_hgrn_gnorm_w, m_pool_scale, m_final_norm_w.reshape(1, D)]
    vs = [v_c_ctx.reshape(1, D), v_ada_b, v_norm_w, v_hgrn_lb_logits[0], v_hgrn_gnorm_w, v_pool_scale, v_final_norm_w.reshape(1, D)]
    res = _adam_small(gs, ws, ms, vs, 3, lb_me)
    n = len(small)
    for j, name in enumerate(small):
        out[name] = tuple(res[q * n + j] for q in range(4))

    shapes = {"c_ctx": (D,), "ada_w": (2, D, SH_ADA), "ada_b": (2, 3 * D), "norm_w": (2, D), "hgrn_w_in": (1, D, SH_WIN),
              "hgrn_lb_logits": (1, 2, DH), "hgrn_gnorm_w": (1, E), "hgrn_w_out": (1, SH_ROWS, D), "pool_w_in": (1, D, SH_PWIN),
              "pool_w_grp": (1, 4, SH_GRP, PG), "pool_scale": (1, DH), "pool_w_out": (1, SH_ROWS, D), "final_norm_w": (D,)}
    order = ["c_ctx", "ada_w", "ada_b", "norm_w", "hgrn_w_in", "hgrn_lb_logits", "hgrn_gnorm_w", "hgrn_w_out", "pool_w_in",
             "pool_w_grp", "pool_scale", "pool_w_out", "final_norm_w"]
    flat = [out[name][q].reshape(shapes[name]) for q in range(4) for name in order]
    return (loss128[0, 0], grad_x[None], *flat)
```

```python
import functools

import numpy as np
import jax
import jax.numpy as jnp
from jax import lax
from jax.experimental import pallas as pl
from jax.experimental.pallas import tpu as pltpu

F32 = jnp.float32
BF16 = jnp.bfloat16

D = 1024
E = 1024
HEADS = 8
DH = 128
CHUNK = 64
T = 2048
TC = 256
TT = T + TC
TM = 256
NT = TT // TM
NTX = T // TM
NDEV = 8
GRID_W = 64
POOL_WINDOWS = (2, 4, 8, 16)
PG = 256
EPS = 1e-6
WIN_COLS = 5 * E
SH_WIN = WIN_COLS // NDEV
SH_PWIN = 2 * E // NDEV
SH_ROWS = E // NDEV
SH_GRP = PG // NDEV
SH_ADA = 3 * D // NDEV
VMEM_LIMIT = 56 * 1024 * 1024

ADAM_LR, ADAM_B1, ADAM_B2, ADAM_EPS, ADAM_WD, ADAM_STEP = 0.001, 0.9, 0.999, 1e-08, 0.01, 10

MESH = pl.DeviceIdType.MESH
VMEM_SPEC = pl.BlockSpec(memory_space=pltpu.VMEM)
HBM_SPEC = pl.BlockSpec(memory_space=pltpu.HBM)
ANY_SPEC = pl.BlockSpec(memory_space=pl.ANY)


def _sds(shape, dtype):
    return jax.ShapeDtypeStruct(shape, dtype)


def _bf(a):
    return a if a.dtype == BF16 else a.astype(BF16)


def _dot(a, b):
    return lax.dot_general(_bf(a), _bf(b), (((1,), (0,)), ((), ())), preferred_element_type=F32)


def _dot_tb(a, b):
    return lax.dot_general(_bf(a), _bf(b), (((1,), (1,)), ((), ())), preferred_element_type=F32)


def _dot_ta(a, b):
    return lax.dot_general(_bf(a), _bf(b), (((0,), (0,)), ((), ())), preferred_element_type=F32)


def _dot01(m01, x):
    hi = x.astype(BF16)
    lo = (x - hi.astype(F32)).astype(BF16)
    return _dot(m01, hi) + _dot(m01, lo)


def _rstd(x):
    return lax.rsqrt(jnp.mean(x * x, axis=-1, keepdims=True) + EPS)


def _sigmoid(x):
    return jax.nn.sigmoid(x)


def _colsum(a):
    return jnp.sum(a, axis=0, keepdims=True)


def _stack_rows(rows):
    n = rows[0].shape[-1]
    rid = lax.broadcasted_iota(jnp.int32, (16, n), 0)
    out = jnp.zeros((16, n), F32)
    for i, r in enumerate(rows):
        out = jnp.where(rid == i, r, out)
    return out


def _head_map(fn, *arrs):
    outs = [fn(*[a[:, h * DH:(h + 1) * DH] for a in arrs]) for h in range(HEADS)]
    return jnp.concatenate(outs, axis=1)


def _gla_consts():
    r = np.arange(TM)[:, None]
    c = np.arange(TM)[None, :]
    same = (r // CHUNK) == (c // CHUNK)
    tril = same & (c <= r)
    triu = same & (c >= r)
    m = np.stack([tril, triu]).astype(np.float32)
    return jnp.asarray(m, BF16), jnp.asarray(m, F32)


def _pool_consts():
    r = np.arange(TM)[:, None]
    c = np.arange(TM)[None, :]
    same = (r // GRID_W) == (c // GRID_W)
    rp, cp = r % GRID_W, c % GRID_W
    bs, inv = [], []
    for w in POOL_WINDOWS:
        lo = np.clip(rp - w // 2, 0, GRID_W)
        hi = np.clip(rp - w // 2 + w, 0, GRID_W)
        bs.append(same & (cp >= lo) & (cp < hi))
        inv.append(1.0 / (hi - lo).astype(np.float32))
    b = np.stack(bs).astype(np.float32)
    bt = np.transpose(b, (0, 2, 1))
    return jnp.asarray(b, BF16), jnp.asarray(bt, BF16), jnp.asarray(np.stack(inv), F32)


def _mesh_pos():
    x, y, c = lax.axis_index("x"), lax.axis_index("y"), lax.axis_index("c")
    return x, y, c, 4 * x + 2 * y + c


def _peer(x, y, c, k):
    return (x ^ ((k >> 2) & 1), y ^ ((k >> 1) & 1), c ^ (k & 1))


def _gather_small(w_in, w_out, pw_in, pgrp, pw_out, lb_l, pscale, c, c_ctx, ada_w):
    n_arr = 4

    def body(win_r, wout_r, pwin_r, pgrp_r, pwout_r, lb_r, ps_r, c_r, cctx_r, ada_r,
             s_win, s_wout, s_pwin, s_pgrp, s_pwout, lb_o, ps_o, cg_o, mod_o, ssem, rsem):
        x, y, cc, idx = _mesh_pos()
        srcs = [lb_r, ps_r, c_r, mod_o.at[idx]]
        mine = [lb_o.at[idx], ps_o.at[idx], cg_o.at[idx], mod_o.at[idx]]

        def remote(a, k):
            return pltpu.make_async_remote_copy(src_ref=srcs[a], dst_ref=mine[a], send_sem=ssem.at[a, k], recv_sem=rsem.at[a, k],
                                                device_id=_peer(x, y, cc, k), device_id_type=MESH)

        first = [remote(a, k) for k in range(1, NDEV) for a in (2, 0, 1)]
        for cp in first:
            cp.start()
        lb_o[idx] = lb_r[...]
        ps_o[idx] = ps_r[...]
        cg_o[idx] = c_r[...]
        s_win[...] = win_r[...].astype(BF16)
        s_wout[...] = wout_r[...].astype(BF16)
        s_pwin[...] = pwin_r[...].astype(BF16)
        s_pgrp[...] = pgrp_r[...].astype(BF16)
        s_pwout[...] = pwout_r[...].astype(BF16)
        for k in range(1, NDEV):
            remote(2, k).wait_recv()
        rows = _stack_rows([cg_o[i] for i in range(NDEV)] + [cctx_r[...]])
        sc = rows * _sigmoid(rows)
        for l in range(2):
            mod_o[idx, l] = _dot(sc, ada_r[l])
        second = [remote(3, k) for k in range(1, NDEV)]
        for cp in second:
            cp.start()
        for cp in first + second:
            cp.wait_send()
        for k in range(1, NDEV):
            for a in (0, 1, 3):
                remote(a, k).wait_recv()

    outs = (
        _sds((D, SH_WIN), BF16), _sds((SH_ROWS, D), BF16), _sds((D, SH_PWIN), BF16), _sds((4, SH_GRP, PG), BF16), _sds((SH_ROWS, D), BF16),
        _sds((NDEV, 2, DH), F32), _sds((NDEV, 1, DH), F32), _sds((NDEV, 1, D), F32), _sds((NDEV, 2, 16, SH_ADA), F32),
    )
    return pl.pallas_call(
        body, name="gather_small", out_shape=outs,
        in_specs=[VMEM_SPEC] * 10, out_specs=[VMEM_SPEC] * 9,
        scratch_shapes=[pltpu.SemaphoreType.DMA((n_arr, NDEV)), pltpu.SemaphoreType.DMA((n_arr, NDEV))],
        compiler_params=pltpu.CompilerParams(vmem_limit_bytes=VMEM_LIMIT),
    )(w_in, w_out, pw_in, pgrp, pw_out, lb_l, pscale, c, c_ctx, ada_w)


def _gather_order(s):
    if isinstance(s, int):
        return (0, 1, 2, 4, 3, 5, 6, 7)[s]
    return s + (s == 3).astype(jnp.int32) - (s == 4).astype(jnp.int32)


GATHER_ISSUE = (1, 2, 4, 3, 5, 6, 7)
GATHER_ICI = (2, 4, 6)
GATHER_DIRECT = (1,) + GATHER_ICI
GATHER_FORWARD_AT = 3
RS_SLOTS = 5


def _weight_slices(refs, i):
    wout, pwin, pgrp, pwout = refs
    return [wout.at[pl.ds(pl.multiple_of(i * SH_ROWS, SH_ROWS), SH_ROWS), :],
            pwin.at[:, pl.ds(pl.multiple_of(i * SH_PWIN, 128), SH_PWIN)],
            pgrp.at[:, pl.ds(pl.multiple_of(i * SH_GRP, SH_GRP), SH_GRP), :],
            pwout.at[pl.ds(pl.multiple_of(i * SH_ROWS, SH_ROWS), SH_ROWS), :]]


def _modulated(x, nw, shift, scale):
    r = _rstd(x)
    xn = x * r
    a = xn * nw
    return a * (1.0 + scale) + shift, r, xn, a


def _f1_gather_matmul(idx1, xcat, nw, msel, s_win):
    def body(idx_ref, x_ref, nw_ref, m_ref, sw_ref, g_ref, win_o, wslot, hx_sc, ssem, rsem, lsem, osem):
        del idx_ref
        s, i = pl.program_id(0), pl.program_id(1)
        x, y, cc, idx = _mesh_pos()
        k = _gather_order(s)
        j = idx ^ k

        def remote(kk):
            return pltpu.make_async_remote_copy(src_ref=sw_ref, dst_ref=wslot.at[idx], send_sem=ssem.at[kk], recv_sem=rsem.at[kk],
                                                device_id=_peer(x, y, cc, kk), device_id_type=MESH)

        def forward(kk):
            jj = idx ^ kk
            return pltpu.make_async_remote_copy(src_ref=wslot.at[jj], dst_ref=wslot.at[jj], send_sem=ssem.at[kk ^ 1],
                                                recv_sem=rsem.at[kk ^ 1], device_id=(x, y, 1 - cc), device_id_type=MESH)

        own = pltpu.make_async_copy(sw_ref, wslot.at[idx], lsem)

        def to_hbm(jj, kk):
            return pltpu.make_async_copy(wslot.at[jj], win_o.at[:, pl.ds(pl.multiple_of(jj * SH_WIN, 128), SH_WIN)], osem.at[kk])

        @pl.when((s == 0) & (i == 0))
        def _():
            own.start()
            for kk in GATHER_DIRECT:
                remote(kk).start()
            own.wait()

        @pl.when(s == 0)
        def _():
            hx, _, _, _ = _modulated(x_ref[...], nw_ref[...], m_ref[0, 0:1, :], m_ref[0, 1:2, :])
            hx_sc[i] = hx.astype(BF16)

        @pl.when((s > 0) & (i == 0))
        def _():
            remote(k).wait_recv()

            @pl.when((k & 1) == 0)
            def _():
                forward(k).start()

        @pl.when(i == 0)
        def _():
            to_hbm(j, k).start()

        g_ref[...] = jnp.dot(hx_sc[i], wslot[j], preferred_element_type=F32)

        @pl.when((s == NDEV - 1) & (i == NT - 1))
        def _():
            for kk in GATHER_DIRECT:
                remote(kk).wait_send()
            for kk in GATHER_ICI:
                forward(kk).wait_send()
            for kk in range(NDEV):
                to_hbm(idx ^ kk, kk).wait()

    grid_spec = pltpu.PrefetchScalarGridSpec(
        num_scalar_prefetch=1, grid=(NDEV, NT),
        in_specs=[pl.BlockSpec((TM, D), lambda s, i, ix: (i, 0)), VMEM_SPEC,
                  pl.BlockSpec((1, 2, D), lambda s, i, ix: (jnp.minimum(i, 1), 0, 0)), HBM_SPEC],
        out_specs=[pl.BlockSpec((TM, SH_WIN), lambda s, i, ix: (i, ix[0] ^ _gather_order(s))), HBM_SPEC],
        scratch_shapes=[pltpu.VMEM((NDEV, D, SH_WIN), BF16), pltpu.VMEM((NT, TM, D), BF16),
                        pltpu.SemaphoreType.DMA((NDEV,)), pltpu.SemaphoreType.DMA((NDEV,)), pltpu.SemaphoreType.DMA,
                        pltpu.SemaphoreType.DMA((NDEV,))])
    return pl.pallas_call(
        body, name="f1_gather_matmul", grid_spec=grid_spec,
        out_shape=(_sds((TT, WIN_COLS), F32), _sds((D, WIN_COLS), BF16)),
        compiler_params=pltpu.CompilerParams(dimension_semantics=("arbitrary", "arbitrary"), vmem_limit_bytes=VMEM_LIMIT),
    )(idx1, xcat, nw, msel, s_win)


def _gla_gates(pre, qpre, lbd, cum, rev):
    rows, n = pre.shape
    nch = rows // CHUNK
    sig = _sigmoid(pre)
    f = lbd + (1.0 - lbd) * sig
    k = 1.0 - f
    g = _dot01(cum, jnp.log(f))
    g3 = g.reshape(nch, CHUNK, n)
    last = 0 if rev else CHUNK - 1
    mid = CHUNK // 2 if rev else CHUNK // 2 - 1
    gl1, gm1 = g3[:, last:last + 1, :], g3[:, mid:mid + 1, :]

    def bc(a):
        return jnp.broadcast_to(a, g3.shape).reshape(rows, n)

    gm = bc(gm1)
    e_q, e_k = jnp.exp(g - gm), jnp.exp(gm - g)
    e_in, e_end = e_q * bc(jnp.exp(gm1)), e_k * bc(jnp.exp(gl1 - gm1))
    qsig = _sigmoid(qpre)
    qs = qpre * qsig * (DH ** -0.5)
    return dict(sig=sig, f=f, k=k, qsig=qsig, qs=qs, e_q=e_q, e_k=e_k, e_in=e_in, e_end=e_end,
                decay=[jnp.exp(g3[ci, last:last + 1, :]) for ci in range(nch)])


def _gla_prep(g_all, lb, cum01):
    def body(g_ref, lb_ref, cum_ref, p0_ref, p1_ref, v_ref, dec_ref):
        qpre = g_ref[:, 3 * E:4 * E]
        v_ref[...] = g_ref[:, 2 * E:3 * E].astype(BF16)
        dec_ref[...] = jnp.zeros_like(dec_ref)
        for d, p_ref in ((0, p0_ref), (1, p1_ref)):
            t = _gla_gates(g_ref[:, d * E:(d + 1) * E], qpre, lb_ref[d:d + 1, :], cum_ref[d], d == 1)
            p_ref[0] = (t["qs"] * t["e_q"]).astype(BF16)
            p_ref[1] = (t["k"] * t["e_k"]).astype(BF16)
            p_ref[2] = (t["qs"] * t["e_in"]).astype(BF16)
            p_ref[3] = (t["k"] * t["e_end"]).astype(BF16)
            for ci in range(TM // CHUNK):
                dec_ref[d, 0, ci:ci + 1, :] = t["decay"][ci]

    quad = pl.BlockSpec((4, TM, E), lambda i: (0, i, 0))
    return pl.pallas_call(
        body, name="gla_prep", grid=(NT,),
        in_specs=[pl.BlockSpec((TM, WIN_COLS), lambda i: (i, 0)), VMEM_SPEC, VMEM_SPEC],
        out_specs=[quad, quad, pl.BlockSpec((TM, E), lambda i: (i, 0)), pl.BlockSpec((2, 1, 8, E), lambda i: (0, i, 0, 0))],
        out_shape=(_sds((4, TT, E), BF16), _sds((4, TT, E), BF16), _sds((TT, E), BF16), _sds((2, NT, 8, E), F32)),
        compiler_params=pltpu.CompilerParams(dimension_semantics=("arbitrary",), vmem_limit_bytes=VMEM_LIMIT),
    )(g_all, lb, cum01)


def _scan_tile(i, rev):
    t = jnp.where(i == 0, 0, NT - i) if rev else i
    return t, pl.ds(pl.multiple_of(t * TM, TM), TM)


def _chunk_order(rev):
    n = TM // CHUNK
    return tuple(range(n - 1, -1, -1)) if rev else tuple(range(n))


def _gla_fwd(p0, p1, v_all, dec, mask01, s_wout, s_pwin, s_pgrp, s_pwout):
    def body(p0_ref, p1_ref, v_ref, dec_ref, msk_ref, swout_r, spwin_r, spgrp_r, spwout_r,
             o_ref, wout_o, pwin_o, pgrp_o, pwout_o, ob_sc, ssem, rsem, lsem):
        h = pl.program_id(0)
        x, y, cc, idx = _mesh_pos()
        srcs = [swout_r, spwin_r, spgrp_r, spwout_r]
        gathered = (wout_o, pwin_o, pgrp_o, pwout_o)
        mine = _weight_slices(gathered, idx)

        def remote(a, k):
            return pltpu.make_async_remote_copy(src_ref=srcs[a], dst_ref=mine[a], send_sem=ssem.at[a, k], recv_sem=rsem.at[a, k],
                                                device_id=_peer(x, y, cc, k), device_id_type=MESH)

        def forward(a, k):
            blk = _weight_slices(gathered, idx ^ k)[a]
            return pltpu.make_async_remote_copy(src_ref=blk, dst_ref=blk, send_sem=ssem.at[a, k ^ 1], recv_sem=rsem.at[a, k ^ 1],
                                                device_id=(x, y, 1 - cc), device_id_type=MESH)

        copies = [remote(a, k) for k in GATHER_DIRECT for a in range(4)]
        passed = [forward(a, k) for k in GATHER_ICI for a in range(4)]
        local = [pltpu.make_async_copy(srcs[a], mine[a], lsem.at[a]) for a in range(4)]

        @pl.when(h == 0)
        def _():
            for cp in copies + local:
                cp.start()

        @pl.when(h == GATHER_FORWARD_AT)
        def _():
            for k in GATHER_ICI:
                for a in range(4):
                    remote(a, k).wait_recv()
                    forward(a, k).start()

        def tile_body(i, sts):
            new = []
            for d, p_ref in ((0, p0_ref), (1, p1_ref)):
                rev = d == 1
                st = sts[d]
                t, rows = _scan_tile(i, rev)
                v = v_ref[rows, :]
                a = _dot_tb(p_ref[0, rows, :], p_ref[1, rows, :]) * msk_ref[d]
                intra = _dot(a, v)
                q_in, kend = p_ref[2, rows, :], p_ref[3, rows, :]
                outs = [None] * (TM // CHUNK)
                for ci in _chunk_order(rev):
                    r = slice(ci * CHUNK, (ci + 1) * CHUNK)
                    outs[ci] = _dot_tb(q_in[r], st) + intra[r]
                    st = st * dec_ref[d, t, ci:ci + 1, :] + _dot_ta(v[r], kend[r])
                (ob_sc if rev else o_ref)[rows, :] = jnp.concatenate(outs, axis=0)
                new.append(st)
            return tuple(new)

        zero = jnp.zeros((DH, DH), F32)
        lax.fori_loop(0, NT, tile_body, (zero, zero))
        o_ref[...] += ob_sc[...]

        @pl.when(h == HEADS - 1)
        def _():
            for cp in copies + passed:
                cp.wait_send()
            for a in range(4):
                remote(a, 1).wait_recv()
            for cp in passed:
                cp.wait_recv()
            for cp in local:
                cp.wait()

    quad = pl.BlockSpec((4, TT, DH), lambda h: (0, 0, h))
    return pl.pallas_call(
        body, name="gla_fwd", grid=(HEADS,),
        in_specs=[quad, quad, pl.BlockSpec((TT, DH), lambda h: (0, h)), pl.BlockSpec((2, NT, 8, DH), lambda h: (0, 0, 0, h)),
                  pl.BlockSpec((2, TM, TM), lambda h: (0, 0, 0))] + [HBM_SPEC] * 4,
        out_specs=[pl.BlockSpec((TT, DH), lambda h: (0, h))] + [HBM_SPEC] * 4,
        out_shape=(_sds((TT, E), F32), _sds((E, D), BF16), _sds((D, 2 * E), BF16), _sds((4, PG, PG), BF16), _sds((E, D), BF16)),
        scratch_shapes=[pltpu.VMEM((TT, DH), F32), pltpu.SemaphoreType.DMA((4, NDEV)), pltpu.SemaphoreType.DMA((4, NDEV)),
                        pltpu.SemaphoreType.DMA((4,))],
        compiler_params=pltpu.CompilerParams(dimension_semantics=("arbitrary",), vmem_limit_bytes=VMEM_LIMIT),
    )(p0, p1, v_all, dec, mask01, s_wout, s_pwin, s_pgrp, s_pwout)


def _gated_norm(o, z, gw):
    r = _head_map(lambda oh: jnp.broadcast_to(_rstd(oh), oh.shape), o)
    on = o * r
    zs = _sigmoid(z)
    sz = z * zs
    return on * gw * sz, r, on, zs, sz


def _f3_out(o, g_all, xcat, gate, gw, wout):
    def body(o_ref, z_ref, x_ref, gate_ref, gw_ref, w_ref, x1_ref):
        og, _, _, _, _ = _gated_norm(o_ref[...], z_ref[...], gw_ref[...])
        x1_ref[...] = x_ref[...] + gate_ref[...] * _dot(og, w_ref[...])

    return pl.pallas_call(
        body, name="f3_out", grid=(NTX,),
        in_specs=[pl.BlockSpec((TM, E), lambda i: (i + 1, 0)), pl.BlockSpec((TM, E), lambda i: (i + 1, 4)),
                  pl.BlockSpec((TM, D), lambda i: (i + 1, 0)), pl.BlockSpec((1, D), lambda i: (0, 0)),
                  pl.BlockSpec((1, E), lambda i: (0, 0)), pl.BlockSpec((E, D), lambda i: (0, 0))],
        out_specs=pl.BlockSpec((TM, D), lambda i: (i, 0)),
        out_shape=_sds((T, D), F32),
        compiler_params=pltpu.CompilerParams(dimension_semantics=("arbitrary",)),
    )(o, g_all, xcat, gate, gw, wout)


def _pool_layer(x1, tgt, mod1, nw1, fnw, pwin, pgrp, pscale, pwout, pb, pbt, pinv):
    def body(x_ref, t_ref, m_ref, nw_ref, fw_ref, pwin_ref, pgrp_ref, ps_ref, pwout_ref, pb_ref, pbt_ref, pinv_ref,
             dx_ref, gpwin_o, gpgrp_o, gpwout_o, dmod_o, gnw_o, gfw_o, gps_o, loss_o,
             a_pwin, a_pgrp, a_pwout):
        i = pl.program_id(0)

        @pl.when(i == 0)
        def _():
            for ref in (a_pwin, a_pgrp, a_pwout, dmod_o, gnw_o, gfw_o, gps_o, loss_o):
                ref[...] = jnp.zeros_like(ref)

        shift, scale, gate = m_ref[0:1, :], m_ref[1:2, :], m_ref[2:3, :]
        nw, fw, ps = nw_ref[...], fw_ref[...], ps_ref[...]
        x1 = x_ref[...]
        hx, r1, xn, a = _modulated(x1, nw, shift, scale)
        hxb = hx.astype(BF16)
        uz = jnp.dot(hxb, pwin_ref[...], preferred_element_type=F32)
        u, z = uz[:, :E], uz[:, E:]
        pooled, ys = [], []
        for g in range(4):
            ug = u[:, g * PG:(g + 1) * PG]
            pg = _dot01(pb_ref[g], ug) * pinv_ref[g] - ug
            pooled.append(pg.astype(BF16))
            ys.append(_dot(pooled[g], pgrp_ref[g]))
        ycat = jnp.concatenate(ys, axis=1)
        y = ycat * ps
        zs = _sigmoid(z)
        sz = z * zs
        p = (y * sz).astype(BF16)
        out = _dot(p, pwout_ref[...])
        x2 = x1 + gate * out
        r2 = _rstd(x2)
        xn2 = x2 * r2
        diff = xn2 * fw - t_ref[...]
        loss_o[...] += _colsum(diff * diff)
        dyf = diff * (1.0 / D)
        gfw_o[...] += _colsum(dyf * xn2)
        dxn2 = dyf * fw
        dx2 = r2 * (dxn2 - xn2 * jnp.mean(dxn2 * xn2, axis=-1, keepdims=True))
        dgate = _colsum(dx2 * out)
        dout = (dx2 * gate).astype(BF16)
        for j in range(4):
            cs = slice(j * PG, (j + 1) * PG)
            a_pwout[:, cs] += _dot_ta(p, dout[:, cs])
        dp = _dot_tb(dout, pwout_ref[...])
        dy = dp * sz
        dz = dp * y * (zs * (1.0 + z * (1.0 - zs)))
        gps_o[...] += _colsum(dy * ycat)
        dycat = dy * ps
        dus = []
        for g in range(4):
            dyg = dycat[:, g * PG:(g + 1) * PG].astype(BF16)
            a_pgrp[g] += _dot_ta(pooled[g], dyg)
            dpg = _dot_tb(dyg, pgrp_ref[g])
            dus.append(_dot01(pbt_ref[g], dpg * pinv_ref[g]) - dpg)
        duz = jnp.concatenate(dus + [dz], axis=1).astype(BF16)
        for j in range(2 * E // PG):
            cs = slice(j * PG, (j + 1) * PG)
            a_pwin[:, cs] += _dot_ta(hxb, duz[:, cs])
        dhx = _dot_tb(duz, pwin_ref[...])
        dmod_o[0:1, :] += _colsum(dhx)
        dmod_o[1:2, :] += _colsum(dhx * a)
        dmod_o[2:3, :] += dgate
        da = dhx * (1.0 + scale)
        gnw_o[...] += _colsum(da * xn)
        dxn = da * nw
        dx_ref[...] = dx2 + r1 * (dxn - xn * jnp.mean(dxn * xn, axis=-1, keepdims=True))

        @pl.when(i == NTX - 1)
        def _():
            gpwin_o[...] = a_pwin[...].astype(BF16)
            gpgrp_o[...] = a_pgrp[...].astype(BF16)
            gpwout_o[...] = a_pwout[...].astype(BF16)

    tile = pl.BlockSpec((TM, D), lambda i: (i, 0))
    outs = (_sds((T, D), F32), _sds((D, 2 * E), BF16), _sds((4, PG, PG), BF16), _sds((E, D), BF16),
            _sds((3, D), F32), _sds((1, D), F32), _sds((1, D), F32), _sds((1, E), F32), _sds((1, D), F32))
    return pl.pallas_call(
        body, name="pool_layer", grid=(NTX,),
        in_specs=[tile, tile] + [VMEM_SPEC] * 10,
        out_specs=[tile] + [VMEM_SPEC] * 8,
        out_shape=outs,
        scratch_shapes=[pltpu.VMEM((D, 2 * E), F32), pltpu.VMEM((4, PG, PG), F32), pltpu.VMEM((E, D), F32)],
        compiler_params=pltpu.CompilerParams(dimension_semantics=("arbitrary",), vmem_limit_bytes=VMEM_LIMIT),
    )(x1, tgt, mod1, nw1, fnw, pwin, pgrp, pscale, pwout, pb, pbt, pinv)


def _b3_out_bwd(dx1, o, g_all, gate, gw, wout):
    def body(dx_ref, o_ref, z_ref, gate_ref, gw_ref, w_ref, do_ref, dz_ref, gw_o, dgate_o, ggw_o, acc):
        i = pl.program_id(0)

        @pl.when(i == 0)
        def _():
            acc[...] = jnp.zeros_like(acc)
            dgate_o[...] = jnp.zeros_like(dgate_o)
            ggw_o[...] = jnp.zeros_like(ggw_o)
            do_ref[...] = jnp.zeros_like(do_ref)
            dz_ref[...] = jnp.zeros_like(dz_ref)

        @pl.when(i > 0)
        def _():
            gw = gw_ref[...]
            z = z_ref[...]
            og, r, on, zs, sz = _gated_norm(o_ref[...], z, gw)
            ogb = og.astype(BF16)
            dx = dx_ref[...]
            dgate_o[...] += _colsum(dx * _dot(ogb, w_ref[...]))
            dy = (dx * gate_ref[...]).astype(BF16)
            for j in range(4):
                cs = slice(j * PG, (j + 1) * PG)
                acc[:, cs] += _dot_ta(ogb, dy[:, cs])
            dog = _dot_tb(dy, w_ref[...])
            dz_ref[...] = (dog * (on * gw) * (zs * (1.0 + z * (1.0 - zs)))).astype(BF16)
            dong = dog * sz
            ggw_o[...] += _colsum(dong * on)
            don = dong * gw
            do = _head_map(lambda dh, nh, rh: rh * (dh - nh * jnp.mean(dh * nh, axis=-1, keepdims=True)), don, on, r)
            do_ref[...] = do.astype(BF16)

        @pl.when(i == NT - 1)
        def _():
            gw_o[...] = acc[...].astype(BF16)

    prev = lambda i: (jnp.maximum(i - 1, 0), 0)
    return pl.pallas_call(
        body, name="b3_out_bwd", grid=(NT,),
        in_specs=[pl.BlockSpec((TM, D), prev), pl.BlockSpec((TM, E), lambda i: (i, 0)), pl.BlockSpec((TM, E), lambda i: (i, 4)),
                  VMEM_SPEC, VMEM_SPEC, VMEM_SPEC],
        out_specs=[pl.BlockSpec((TM, E), lambda i: (i, 0)), pl.BlockSpec((TM, E), lambda i: (i, 0)),
                   VMEM_SPEC, VMEM_SPEC, VMEM_SPEC],
        out_shape=(_sds((TT, E), BF16), _sds((TT, E), BF16), _sds((E, D), BF16), _sds((1, D), F32), _sds((1, E), F32)),
        scratch_shapes=[pltpu.VMEM((E, D), F32)],
        compiler_params=pltpu.CompilerParams(dimension_semantics=("arbitrary",), vmem_limit_bytes=VMEM_LIMIT),
    )(dx1, o, g_all, gate, gw, wout)


def _gla_bwd(p0, p1, v_all, dec, do, mask01, gwout, gpwin, gpgrp, gpwout):
    nch = TM // CHUNK

    def body(p0_ref, p1_ref, v_ref, dec_ref, do_ref, msk_ref, gwout_r, gpwin_r, gpgrp_r, gpwout_r,
             d0_ref, d1_ref, dv_ref, dgl_ref, rwout_o, rpwin_o, rpgrp_o, rpwout_o,
             ss_sc, dv_sc, ssem, rsem, lsem):
        h = pl.program_id(0)
        x, y, cc, idx = _mesh_pos()
        grads = (gwout_r, gpwin_r, gpgrp_r, gpwout_r)
        dsts = [rwout_o.at[idx], rpwin_o.at[idx], rpgrp_o.at[idx], rpwout_o.at[idx]]

        def remote(a, k):
            px, py, pc = _peer(x, y, cc, k)
            return pltpu.make_async_remote_copy(src_ref=_weight_slices(grads, 4 * px + 2 * py + pc)[a], dst_ref=dsts[a],
                                                send_sem=ssem.at[a, k], recv_sem=rsem.at[a, k], device_id=(px, py, pc), device_id_type=MESH)

        copies = [remote(a, k) for k in GATHER_ISSUE for a in range(4)]
        local = [pltpu.make_async_copy(_weight_slices(grads, idx)[a], dsts[a], lsem.at[a]) for a in range(4)]

        @pl.when(h == 0)
        def _():
            for cp in copies + local:
                cp.start()

        zero = jnp.zeros((DH, DH), F32)
        dgl_ref[...] = jnp.zeros_like(dgl_ref)

        def fwd_body(i, sts):
            new = []
            for d, p_ref in ((0, p0_ref), (1, p1_ref)):
                rev = d == 1
                st = sts[d]
                t, rows = _scan_tile(i, rev)
                v, kend = v_ref[rows, :], p_ref[3, rows, :]
                for n, ci in enumerate(_chunk_order(rev)):
                    r = slice(ci * CHUNK, (ci + 1) * CHUNK)
                    ss_sc[d, i * nch + n] = st
                    st = st * dec_ref[d, t, ci:ci + 1, :] + _dot_ta(v[r], kend[r])
                new.append(st)
            return tuple(new)

        lax.fori_loop(0, NT, fwd_body, (zero, zero))

        def bwd_body(ii, dsts_):
            i = NT - 1 - ii
            new = []
            for d, p_ref, d_ref in ((0, p0_ref, d0_ref), (1, p1_ref, d1_ref)):
                rev = d == 1
                order = _chunk_order(rev)
                dst = dsts_[d]
                t, rows = _scan_tile(i, rev)
                qg, kg, q_in, kend = p_ref[0, rows, :], p_ref[1, rows, :], p_ref[2, rows, :], p_ref[3, rows, :]
                v, dob, msk = v_ref[rows, :], do_ref[rows, :], msk_ref[d]
                a = (_dot_tb(qg, kg) * msk).astype(BF16)
                da = (_dot_tb(dob, v) * msk).astype(BF16)
                d_ref[0, rows, :] = _dot(da, kg)
                d_ref[1, rows, :] = _dot_ta(da, qg)
                dv_intra = _dot_ta(a, dob)
                dv_l, dkend_l, dqin_l = [None] * nch, [None] * nch, [None] * nch
                for n in range(nch - 1, -1, -1):
                    ci = order[n]
                    r = slice(ci * CHUNK, (ci + 1) * CHUNK)
                    s_c = ss_sc[d, i * nch + n]
                    dec = dec_ref[d, t, ci:ci + 1, :]
                    dstb = dst.astype(BF16)
                    dv_l[ci] = dv_intra[r] + _dot_tb(kend[r], dstb)
                    dkend_l[ci] = _dot(v[r], dstb)
                    dqin_l[ci] = _dot(dob[r], s_c)
                    dgl_ref[d, t, ci:ci + 1, :] = jnp.sum(s_c * dst, axis=0, keepdims=True) * dec
                    dst = dst * dec + _dot_ta(dob[r], q_in[r])
                d_ref[2, rows, :] = jnp.concatenate(dqin_l, axis=0)
                d_ref[3, rows, :] = jnp.concatenate(dkend_l, axis=0)
                dv_sc[d, rows, :] = jnp.concatenate(dv_l, axis=0)
                new.append(dst)
            return tuple(new)

        lax.fori_loop(0, NT, bwd_body, (zero, zero))
        dv_ref[...] = (dv_sc[0] + dv_sc[1]).astype(BF16)

        @pl.when(h == HEADS - 1)
        def _():
            for cp in copies:
                cp.wait_send()
            for cp in copies:
                cp.wait_recv()
            for cp in local:
                cp.wait()

    quad = pl.BlockSpec((4, TT, DH), lambda h: (0, 0, h))
    col = pl.BlockSpec((TT, DH), lambda h: (0, h))
    chunkv = pl.BlockSpec((2, NT, 8, DH), lambda h: (0, 0, 0, h))
    outs = (_sds((4, TT, E), F32), _sds((4, TT, E), F32), _sds((TT, E), BF16), _sds((2, NT, 8, E), F32),
            _sds((NDEV, SH_ROWS, D), BF16), _sds((NDEV, D, SH_PWIN), BF16), _sds((NDEV, 4, SH_GRP, PG), BF16), _sds((NDEV, SH_ROWS, D), BF16))
    return pl.pallas_call(
        body, name="gla_bwd", grid=(HEADS,),
        in_specs=[quad, quad, col, chunkv, col, pl.BlockSpec((2, TM, TM), lambda h: (0, 0, 0))] + [HBM_SPEC] * 4,
        out_specs=[quad, quad, col, chunkv] + [HBM_SPEC] * 4,
        out_shape=outs,
        scratch_shapes=[pltpu.VMEM((2, NT * nch, DH, DH), F32), pltpu.VMEM((2, TT, DH), F32),
                        pltpu.SemaphoreType.DMA((4, NDEV)), pltpu.SemaphoreType.DMA((4, NDEV)), pltpu.SemaphoreType.DMA((4,))],
        compiler_params=pltpu.CompilerParams(dimension_semantics=("arbitrary",), vmem_limit_bytes=VMEM_LIMIT),
    )(p0, p1, v_all, dec, do, mask01, gwout, gpwin, gpgrp, gpwout)


TMB = 128


def _gla_post_bwd(g_all, d0, d1, dgl, dv, dz, lb, cum01):
    nch = TMB // CHUNK

    def body(g_ref, d0_ref, d1_ref, dgl_ref, dv_ref, dz_ref, lb_ref, cum_ref, dg_ref, dlb_ref):
        i = pl.program_id(0)

        @pl.when(i == 0)
        def _():
            dlb_ref[...] = jnp.zeros_like(dlb_ref)

        half = i & 1
        qpre = g_ref[:, 3 * E:4 * E]
        dqs_sum = None
        dpre = []
        for d, d_ref in ((0, d0_ref), (1, d1_ref)):
            rev = d == 1
            lbd = lb_ref[d:d + 1, :]
            t = _gla_gates(g_ref[:, d * E:(d + 1) * E], qpre, lbd, cum_ref[d, :TMB, :TMB], rev)
            dqg, dkg, dqin, dkend = d_ref[0], d_ref[1], d_ref[2], d_ref[3]
            dqs = dqg * t["e_q"] + dqin * t["e_in"]
            dk = dkg * t["e_k"] + dkend * t["e_end"]
            dkk = dkend * (t["k"] * t["e_end"])
            dg = t["qs"] * dqs - t["k"] * dk
            dkk3 = dkk.reshape(nch, CHUNK, E)
            dgl8 = dgl_ref[d, 0]
            dgl_rows = [jnp.where(half == 0, dgl8[ci:ci + 1, :], dgl8[nch + ci:nch + ci + 1, :]) for ci in range(nch)]
            dgl_b = jnp.concatenate([jnp.broadcast_to(dgl_rows[ci] + jnp.sum(dkk3[ci], axis=0, keepdims=True), (CHUNK, E))
                                     for ci in range(nch)], axis=0)
            pos = lax.broadcasted_iota(jnp.int32, (TMB, E), 0) & (CHUNK - 1)
            dg = dg + jnp.where(pos == (0 if rev else CHUNK - 1), dgl_b, 0.0)
            dlf = _dot01(cum_ref[1 - d, :TMB, :TMB], dg)
            df = dlf / t["f"] - dk
            sig = t["sig"]
            dpre.append((df * (1.0 - lbd) * sig * (1.0 - sig)).astype(BF16))
            dlb_ref[d:d + 1, :] += _colsum(df * (1.0 - sig))
            dqs_sum = dqs if dqs_sum is None else dqs_sum + dqs
            qsig = t["qsig"]
        dqpre = dqs_sum * (DH ** -0.5) * (qsig * (1.0 + qpre * (1.0 - qsig)))
        dg_ref[...] = jnp.concatenate([dpre[0], dpre[1], dv_ref[...], dqpre.astype(BF16), dz_ref[...]], axis=1)

    quad = pl.BlockSpec((4, TMB, E), lambda i: (0, i, 0))
    tile = pl.BlockSpec((TMB, E), lambda i: (i, 0))
    return pl.pallas_call(
        body, name="gla_post_bwd", grid=(TT // TMB,),
        in_specs=[pl.BlockSpec((TMB, WIN_COLS), lambda i: (i, 0)), quad, quad,
                  pl.BlockSpec((2, 1, 8, E), lambda i: (0, i // 2, 0, 0)), tile, tile, VMEM_SPEC, VMEM_SPEC],
        out_specs=[pl.BlockSpec((TMB, WIN_COLS), lambda i: (i, 0)), VMEM_SPEC],
        out_shape=(_sds((TT, WIN_COLS), BF16), _sds((2, E), F32)),
        compiler_params=pltpu.CompilerParams(dimension_semantics=("arbitrary",), vmem_limit_bytes=VMEM_LIMIT),
    )(g_all, d0, d1, dgl, dv, dz, lb, cum01)


def _b1_in_bwd(idx1, xcat, dx1, dg, nw, msel, win):
    last_s = NDEV - 1

    def body(idx_ref, x_ref, dx1_ref, dg_ref, nw_ref, m_ref, w_ref, gx_ref, rwin_o, dmx_o, dmc_o, gnw_o,
             hx_sc, dhx_sc, acc, sbuf, pbuf, psend, precv, isend, irecv, sibsem, lsem):
        del idx_ref
        s, i = pl.program_id(0), pl.program_id(1)
        x, y, cc, idx = _mesh_pos()
        shift, scale = m_ref[0, 0:1, :], m_ref[0, 1:2, :]
        sibling = (x, y, 1 - cc)

        def partial(p):
            return pltpu.make_async_remote_copy(src_ref=sbuf.at[0], dst_ref=pbuf.at[p], send_sem=psend.at[p], recv_sem=precv.at[p],
                                                device_id=sibling, device_id_type=MESH)

        def chip_sum(p):
            return pltpu.make_async_remote_copy(src_ref=sbuf.at[1], dst_ref=rwin_o.at[2 + p], send_sem=isend.at[p], recv_sem=irecv.at[p],
                                                device_id=_peer(x, y, cc, 2 * (p + 1)), device_id_type=MESH)

        to_sibling = pltpu.make_async_remote_copy(src_ref=sbuf.at[0], dst_ref=rwin_o.at[1], send_sem=sibsem.at[0], recv_sem=sibsem.at[1],
                                                  device_id=sibling, device_id_type=MESH)
        own = pltpu.make_async_copy(sbuf.at[1], rwin_o.at[0], lsem)

        @pl.when((s == 0) & (i == 0))
        def _():
            for ref in (dmx_o, dmc_o, gnw_o):
                ref[...] = jnp.zeros_like(ref)

        @pl.when(s == 0)
        def _():
            hx, _, _, _ = _modulated(x_ref[...], nw_ref[...], shift, scale)
            hx_sc[i] = hx.astype(BF16)

        @pl.when(i == 0)
        def _():
            acc[...] = jnp.zeros_like(acc)

        dgb = dg_ref[...]
        hxb = hx_sc[i]
        for lo, hi in ((0, 256), (256, 512), (512, SH_WIN)):
            acc[:, lo:hi] += _dot_ta(hxb, dgb[:, lo:hi])
        part = _dot_tb(dgb, w_ref[...])

        @pl.when(s == 0)
        def _():
            dhx_sc[i] = part

        @pl.when(s > 0)
        def _():
            dhx_sc[i] += part

        for p in (2, 1, 0):
            @pl.when((i == NT - 1) & (s == 2 * (2 - p)))
            def _(p=p):
                if p < 2:
                    partial(p + 1).wait_send()
                sbuf[0] = acc[...].astype(BF16)
                partial(p).start()

            @pl.when((i == NT - 1) & (s == 2 * (2 - p) + 1))
            def _(p=p):
                if p < 2:
                    chip_sum(p + 1).wait_send()
                partial(p).wait_recv()
                sbuf[1] = (acc[...] + pbuf[p].astype(F32)).astype(BF16)
                chip_sum(p).start()

        @pl.when((i == NT - 1) & (s == last_s - 1))
        def _():
            partial(0).wait_send()
            sbuf[0] = acc[...].astype(BF16)
            to_sibling.start()

        @pl.when((i == NT - 1) & (s == last_s))
        def _():
            chip_sum(0).wait_send()
            sbuf[1] = acc[...].astype(BF16)
            own.start()

        @pl.when(s == last_s)
        def _():
            nw = nw_ref[...]
            _, r, xn, a = _modulated(x_ref[...], nw, shift, scale)
            dhx = dhx_sc[i]
            dsh, dsc = _colsum(dhx), _colsum(dhx * a)
            da = dhx * (1.0 + scale)
            gnw_o[...] += _colsum(da * xn)
            dxn = da * nw
            gx_ref[...] = dx1_ref[...] + r * (dxn - xn * jnp.mean(dxn * xn, axis=-1, keepdims=True))

            @pl.when(i == 0)
            def _():
                dmc_o[0:1, :] += dsh
                dmc_o[1:2, :] += dsc

            @pl.when(i > 0)
            def _():
                dmx_o[0:1, :] += dsh
                dmx_o[1:2, :] += dsc

        @pl.when((i == NT - 1) & (s == last_s))
        def _():
            to_sibling.wait_send()
            to_sibling.wait_recv()
            for p in range(3):
                chip_sum(p).wait_recv()
            own.wait()

    grid_spec = pltpu.PrefetchScalarGridSpec(
        num_scalar_prefetch=1, grid=(NDEV, NT),
        in_specs=[pl.BlockSpec((TM, D), lambda s, i, ix: (i, 0)), pl.BlockSpec((TM, D), lambda s, i, ix: (jnp.maximum(i - 1, 0), 0)),
                  pl.BlockSpec((TM, SH_WIN), lambda s, i, ix: (i, ix[0] ^ (last_s - s))), VMEM_SPEC,
                  pl.BlockSpec((1, 2, D), lambda s, i, ix: (jnp.minimum(i, 1), 0, 0)),
                  pl.BlockSpec((D, SH_WIN), lambda s, i, ix: (0, ix[0] ^ (last_s - s)))],
        out_specs=[pl.BlockSpec((TM, D), lambda s, i, ix: (jnp.where(s == last_s, jnp.maximum(i - 1, 0), 0), 0)),
                   HBM_SPEC, VMEM_SPEC, VMEM_SPEC, VMEM_SPEC],
        scratch_shapes=[pltpu.VMEM((NT, TM, D), BF16), pltpu.VMEM((NT, TM, D), F32), pltpu.VMEM((D, SH_WIN), F32),
                        pltpu.VMEM((2, D, SH_WIN), BF16), pltpu.VMEM((3, D, SH_WIN), BF16),
                        pltpu.SemaphoreType.DMA((3,)), pltpu.SemaphoreType.DMA((3,)), pltpu.SemaphoreType.DMA((3,)),
                        pltpu.SemaphoreType.DMA((3,)), pltpu.SemaphoreType.DMA((2,)), pltpu.SemaphoreType.DMA])
    return pl.pallas_call(
        body, name="b1_in_bwd", grid_spec=grid_spec,
        out_shape=(_sds((T, D), F32), _sds((RS_SLOTS, D, SH_WIN), BF16), _sds((2, D), F32), _sds((2, D), F32), _sds((1, D), F32)),
        compiler_params=pltpu.CompilerParams(dimension_semantics=("arbitrary", "arbitrary"), vmem_limit_bytes=VMEM_LIMIT),
    )(idx1, xcat, dx1, dg, nw, msel, win)


def _reduce_small(pd, pv, cg, c_ctx, ada_w0):
    n_arr = 3

    def body(pd_r, pv_r, cg_r, cctx_r, ada_r, gada_o, gadab_o, gcctx_o, pvsum_o, loss_o,
             pd_all, pv_all, dsc_all, dsc_mine, ssem, rsem):
        x, y, cc, idx = _mesh_pos()
        srcs = [pd_r, pv_r, dsc_mine]
        dsts = [pd_all.at[idx], pv_all.at[idx], dsc_all.at[idx]]

        def remote(a, k):
            return pltpu.make_async_remote_copy(src_ref=srcs[a], dst_ref=dsts[a], send_sem=ssem.at[a, k], recv_sem=rsem.at[a, k],
                                                device_id=_peer(x, y, cc, k), device_id_type=MESH)

        first = [remote(a, k) for k in range(1, NDEV) for a in (0, 1)]
        for cp in first:
            cp.start()
        pd_all[idx] = pd_r[...]
        pv_all[idx] = pv_r[...]
        for k in range(1, NDEV):
            remote(0, k).wait_recv()
            remote(1, k).wait_recv()
        mine = [pd_all[s, :, pl.ds(idx, 1), :] for s in range(NDEV)]
        dmc = functools.reduce(lambda u, v: u + v, [m[2] for m in mine])
        rows = _stack_rows([cg_r[i] for i in range(NDEV)] + [cctx_r[...]])
        sc = (rows * _sigmoid(rows)).astype(BF16)
        gada_o[0] = _dot_ta(sc, _stack_rows([m[0] for m in mine] + [dmc]))
        gada_o[1] = _dot_ta(sc, _stack_rows([m[1] for m in mine]))
        dsc_mine[...] = _dot_tb(jnp.broadcast_to(dmc, (8, SH_ADA)), ada_r[...])[0:1, :]
        dsc_all[idx] = dsc_mine[...]
        second = [remote(2, k) for k in range(1, NDEV)]
        for cp in second:
            cp.start()
        tot = [functools.reduce(lambda u, v: u + v, [pd_all[s, l] for s in range(NDEV)]) for l in range(3)]
        gadab_o[0] = tot[0] + tot[2]
        gadab_o[1] = tot[1]
        pvs = functools.reduce(lambda u, v: u + v, [pv_all[s] for s in range(NDEV)])
        pvsum_o[...] = pvs
        loss_o[...] = jnp.broadcast_to(jnp.sum(pvs[:, PV_LOSS:PV_LOSS + D], axis=-1, keepdims=True) * (0.5 / D), (1, 128))
        for k in range(1, NDEV):
            remote(2, k).wait_recv()
        dsc = functools.reduce(lambda u, v: u + v, [dsc_all[s] for s in range(NDEV)])
        cx = cctx_r[...]
        sx = _sigmoid(cx)
        gcctx_o[...] = dsc * (sx * (1.0 + cx * (1.0 - sx)))
        for cp in first + second:
            cp.wait_send()

    outs = (_sds((2, D, SH_ADA), F32), _sds((2, NDEV, SH_ADA), F32), _sds((1, D), F32), _sds((1, PV_LEN), F32), _sds((1, 128), F32))
    return pl.pallas_call(
        body, name="reduce_small", out_shape=outs,
        in_specs=[VMEM_SPEC] * 5, out_specs=[VMEM_SPEC] * 5,
        scratch_shapes=[
            pltpu.VMEM((NDEV, 3, NDEV, SH_ADA), F32), pltpu.VMEM((NDEV, 1, PV_LEN), F32), pltpu.VMEM((NDEV, 1, D), F32),
            pltpu.VMEM((1, D), F32),
            pltpu.SemaphoreType.DMA((n_arr, NDEV)), pltpu.SemaphoreType.DMA((n_arr, NDEV)),
        ],
        compiler_params=pltpu.CompilerParams(vmem_limit_bytes=VMEM_LIMIT),
    )(pd, pv, cg, c_ctx, ada_w0)


PV_NW, PV_GNORM, PV_FINAL, PV_LB, PV_PSCALE, PV_LOSS, PV_LEN = 0, 2 * D, 3 * D, 4 * D, 6 * D, 7 * D, 8 * D


def _adamw(w, g, m, v):
    m = ADAM_B1 * m + (1.0 - ADAM_B1) * g
    v = ADAM_B2 * v + (1.0 - ADAM_B2) * (g * g)
    m_hat = m / (1.0 - ADAM_B1 ** ADAM_STEP)
    v_hat = v / (1.0 - ADAM_B2 ** ADAM_STEP)
    delta = -ADAM_LR * (m_hat / (jnp.sqrt(v_hat) + ADAM_EPS) + ADAM_WD * w)
    return delta, m, v


def _adam_sharded(name, parts, w, m, v, tr):
    rr, cc = w.shape
    n = parts.shape[0]

    def body(p_ref, w_ref, m_ref, v_ref, g_o, d_o, m_o, v_o):
        g = p_ref[0].astype(F32)
        for s in range(1, n):
            g = g + p_ref[s].astype(F32)
        d, mn, vn = _adamw(w_ref[...], g, m_ref[...], v_ref[...])
        g_o[...], d_o[...], m_o[...], v_o[...] = g, d, mn, vn

    blk = pl.BlockSpec((tr, cc), lambda i: (i, 0))
    return pl.pallas_call(
        body, name=name, grid=(rr // tr,),
        in_specs=[pl.BlockSpec((n, tr, cc), lambda i: (0, i, 0)), blk, blk, blk],
        out_specs=[blk] * 4, out_shape=(_sds((rr, cc), F32),) * 4,
        compiler_params=pltpu.CompilerParams(dimension_semantics=("arbitrary",)),
    )(parts, w, m, v)


def _adam_dense(name, g, w, m, v, tr):
    rr, cc = w.shape

    def body(g_ref, w_ref, m_ref, v_ref, d_o, m_o, v_o):
        d, mn, vn = _adamw(w_ref[...], g_ref[...], m_ref[...], v_ref[...])
        d_o[...], m_o[...], v_o[...] = d, mn, vn

    blk = pl.BlockSpec((tr, cc), lambda i: (i, 0))
    return pl.pallas_call(
        body, name=name, grid=(rr // tr,), in_specs=[blk] * 4, out_specs=[blk] * 3, out_shape=(_sds((rr, cc), F32),) * 3,
        compiler_params=pltpu.CompilerParams(dimension_semantics=("arbitrary",)),
    )(g, w, m, v)


def _adam_small(gs, ws, ms, vs, lb_idx, lbv):
    n = len(ws)

    def body(*refs):
        g_r, w_r, m_r, v_r = refs[:n], refs[n:2 * n], refs[2 * n:3 * n], refs[3 * n:4 * n]
        lb_r = refs[4 * n]
        outs = refs[4 * n + 1:]
        for j in range(n):
            g = g_r[j][...]
            if j == lb_idx:
                lbj = lb_r[...]
                g = g * lbj * (1.0 - lbj)
            d, mn, vn = _adamw(w_r[j][...], g, m_r[j][...], v_r[j][...])
            outs[j][...], outs[n + j][...], outs[2 * n + j][...], outs[3 * n + j][...] = g, d, mn, vn

    shapes = tuple(_sds(w.shape, F32) for w in ws)
    return pl.pallas_call(body, name="adam_small", out_shape=shapes * 4)(*gs, *ws, *ms, *vs, lbv)


def kernel(x, c, ctx, c_ctx, ada_w, ada_b, norm_w, hgrn_w_in, hgrn_lb_logits, hgrn_gnorm_w, hgrn_w_out, pool_w_in, pool_w_grp, pool_scale, pool_w_out, final_norm_w, loss_target, m_c_ctx, m_ada_w, m_ada_b, m_norm_w, m_hgrn_w_in, m_hgrn_lb_logits, m_hgrn_gnorm_w, m_hgrn_w_out, m_pool_w_in, m_pool_w_grp, m_pool_scale, m_pool_w_out, m_final_norm_w, v_c_ctx, v_ada_w, v_ada_b, v_norm_w, v_hgrn_w_in, v_hgrn_lb_logits, v_hgrn_gnorm_w, v_hgrn_w_out, v_pool_w_in, v_pool_w_grp, v_pool_scale, v_pool_w_out, v_final_norm_w):
    idx = 4 * lax.axis_index("x") + 2 * lax.axis_index("y") + lax.axis_index("c")
    cctx2 = c_ctx.reshape(1, D)
    cum01, mask01 = _gla_consts()
    pb, pbt, pinv = _pool_consts()

    idx1 = idx.reshape(1).astype(jnp.int32)
    s_win, s_wout, s_pwin, s_pgrp, s_pwout, lbl_g, ps_g, cg, mod_g = _gather_small(
        hgrn_w_in[0], hgrn_w_out[0], pool_w_in[0], pool_w_grp[0], pool_w_out[0], hgrn_lb_logits[0], pool_scale, c, cctx2, ada_w)
    lb = jax.nn.sigmoid(jnp.transpose(lbl_g, (1, 0, 2)).reshape(2, E))
    pscale = ps_g.reshape(1, E)
    mod_all = jnp.transpose(mod_g, (1, 2, 0, 3)).reshape(2, 16, 3 * D) + ada_b[:, None, :]
    mod_me = lax.dynamic_index_in_dim(mod_all, idx, axis=1, keepdims=False)
    mod0, mod1, modc = mod_me[0].reshape(3, D), mod_me[1].reshape(3, D), mod_all[0, NDEV].reshape(3, D)
    msel = jnp.stack([modc[:2], mod0[:2]])
    nw0, nw1 = norm_w[0:1], norm_w[1:2]
    fnw = final_norm_w.reshape(1, D)

    xcat = jnp.concatenate([ctx[0], x[0]], axis=0)
    g_all, win = _f1_gather_matmul(idx1, xcat, nw0, msel, s_win)
    p0, p1, v_all, dec = _gla_prep(g_all, lb, cum01)
    o, wout, pwin, pgrp, pwout = _gla_fwd(p0, p1, v_all, dec, mask01, s_wout, s_pwin, s_pgrp, s_pwout)
    x1 = _f3_out(o, g_all, xcat, mod0[2:3], hgrn_gnorm_w, wout)
    dx1, gpwin, gpgrp, gpwout, dmod1, gnw1, gfw, gps, lossv = _pool_layer(
        x1, loss_target[0], mod1, nw1, fnw, pwin, pgrp, pscale, pwout, pb, pbt, pinv)
    do, dz, gwout, dgate0, ggw = _b3_out_bwd(dx1, o, g_all, mod0[2:3], hgrn_gnorm_w, wout)
    d0, d1, dv, dgl, rwout, rpwin, rpgrp, rpwout = _gla_bwd(p0, p1, v_all, dec, do, mask01, gwout, gpwin, gpgrp, gpwout)
    dg, dlb = _gla_post_bwd(g_all, d0, d1, dgl, dv, dz, lb, cum01)
    grad_x, rwin, dmx, dmc, gnw0 = _b1_in_bwd(idx1, xcat, dx1, dg, nw0, msel, win)

    dmod0 = jnp.concatenate([dmx, dgate0], axis=0)
    dmodc = jnp.concatenate([dmc, jnp.zeros((1, D), F32)], axis=0)
    pd = jnp.stack([dmod0, dmod1, dmodc]).reshape(3, NDEV, SH_ADA)
    pv = jnp.concatenate([gnw0, gnw1, ggw, gfw, dlb.reshape(1, 2 * E), gps, lossv], axis=1)
    g_ada, g_adab, g_cctx, pvsum, loss128 = _reduce_small(pd, pv, cg, cctx2, ada_w[0])

    out = {}
    out["hgrn_w_in"] = _adam_sharded("adam_w_in", rwin, hgrn_w_in[0], m_hgrn_w_in[0], v_hgrn_w_in[0], 256)
    out["hgrn_w_out"] = _adam_sharded("adam_w_out", rwout, hgrn_w_out[0], m_hgrn_w_out[0], v_hgrn_w_out[0], SH_ROWS)
    out["pool_w_in"] = _adam_sharded("adam_pw_in", rpwin, pool_w_in[0], m_pool_w_in[0], v_pool_w_in[0], 512)
    out["pool_w_grp"] = _adam_sharded("adam_pgrp", rpgrp.reshape(NDEV, 4 * SH_GRP, PG), pool_w_grp[0].reshape(4 * SH_GRP, PG),
                                      m_pool_w_grp[0].reshape(4 * SH_GRP, PG), v_pool_w_grp[0].reshape(4 * SH_GRP, PG), 4 * SH_GRP)
    out["pool_w_out"] = _adam_sharded("adam_pw_out", rpwout, pool_w_out[0], m_pool_w_out[0], v_pool_w_out[0], SH_ROWS)
    g_ada2 = g_ada.reshape(2 * D, SH_ADA)
    out["ada_w"] = (g_ada2,) + _adam_dense("adam_ada_w", g_ada2, ada_w.reshape(2 * D, SH_ADA), m_ada_w.reshape(2 * D, SH_ADA),
                                           v_ada_w.reshape(2 * D, SH_ADA), 512)

    lb_me = lax.dynamic_slice_in_dim(lb, idx * DH, DH, axis=1)
    small = ["c_ctx", "ada_b", "norm_w", "hgrn_lb_logits", "hgrn_gnorm_w", "pool_scale", "final_norm_w"]
    gs = [g_cctx, g_adab.reshape(2, 3 * D), pvsum[:, PV_NW:PV_NW + 2 * D].reshape(2, D),
          lax.dynamic_slice_in_dim(pvsum[:, PV_LB:PV_LB + 2 * E].reshape(2, E), idx * DH, DH, axis=1),
          pvsum[:, PV_GNORM:PV_GNORM + E], lax.dynamic_slice_in_dim(pvsum[:, PV_PSCALE:PV_PSCALE + E], idx * DH, DH, axis=1),
          pvsum[:, PV_FINAL:PV_FINAL + D]]
    ws = [cctx2, ada_b, norm_w, hgrn_lb_logits[0], hgrn_gnorm_w, pool_scale, fnw]
    ms = [m_c_ctx.reshape(1, D), m_ada_b, m_norm_w, m_hgrn_lb_logits[0], m_hgrn_gnorm_w, m_pool_scale, m_final_norm_w.reshape(1, D)]
    vs = [v_c_ctx.reshape(1, D), v_ada_b, v_norm_w, v_hgrn_lb_logits[0], v_hgrn_gnorm_w, v_pool_scale, v_final_norm_w.reshape(1, D)]
    res = _adam_small(gs, ws, ms, vs, 3, lb_me)
    n = len(small)
    for j, name in enumerate(small):
        out[name] = tuple(res[q * n + j] for q in range(4))

    shapes = {"c_ctx": (D,), "ada_w": (2, D, SH_ADA), "ada_b": (2, 3 * D), "norm_w": (2, D), "hgrn_w_in": (1, D, SH_WIN),
              "hgrn_lb_logits": (1, 2, DH), "hgrn_gnorm_w": (1, E), "hgrn_w_out": (1, SH_ROWS, D), "pool_w_in": (1, D, SH_PWIN),
              "pool_w_grp": (1, 4, SH_GRP, PG), "pool_scale": (1, DH), "pool_w_out": (1, SH_ROWS, D), "final_norm_w": (D,)}
    order = ["c_ctx", "ada_w", "ada_b", "norm_w", "hgrn_w_in", "hgrn_lb_logits", "hgrn_gnorm_w", "hgrn_w_out", "pool_w_in",
             "pool_w_grp", "pool_scale", "pool_w_out", "final_norm_w"]
    flat = [out[name][q].reshape(shapes[name]) for q in range(4) for name in order]
    return (loss128[0, 0], grad_x[None], *flat)
```

```python
import functools

import numpy as np
import jax
import jax.numpy as jnp
from jax import lax
from jax.experimental import pallas as pl
from jax.experimental.pallas import tpu as pltpu

F32 = jnp.float32
BF16 = jnp.bfloat16

D = 1024
E = 1024
HEADS = 8
DH = 128
CHUNK = 64
T = 2048
TC = 256
TT = T + TC
TM = 256
NT = TT // TM
NTX = T // TM
NDEV = 8
GRID_W = 64
POOL_WINDOWS = (2, 4, 8, 16)
PG = 256
EPS = 1e-6
WIN_COLS = 5 * E
SH_WIN = WIN_COLS // NDEV
SH_PWIN = 2 * E // NDEV
SH_ROWS = E // NDEV
SH_GRP = PG // NDEV
SH_ADA = 3 * D // NDEV
VMEM_LIMIT = 56 * 1024 * 1024

ADAM_LR, ADAM_B1, ADAM_B2, ADAM_EPS, ADAM_WD, ADAM_STEP = 0.001, 0.9, 0.999, 1e-08, 0.01, 10

MESH = pl.DeviceIdType.MESH
VMEM_SPEC = pl.BlockSpec(memory_space=pltpu.VMEM)
HBM_SPEC = pl.BlockSpec(memory_space=pltpu.HBM)
ANY_SPEC = pl.BlockSpec(memory_space=pl.ANY)


def _sds(shape, dtype):
    return jax.ShapeDtypeStruct(shape, dtype)


def _bf(a):
    return a if a.dtype == BF16 else a.astype(BF16)


def _dot(a, b):
    return lax.dot_general(_bf(a), _bf(b), (((1,), (0,)), ((), ())), preferred_element_type=F32)


def _dot_tb(a, b):
    return lax.dot_general(_bf(a), _bf(b), (((1,), (1,)), ((), ())), preferred_element_type=F32)


def _dot_ta(a, b):
    return lax.dot_general(_bf(a), _bf(b), (((0,), (0,)), ((), ())), preferred_element_type=F32)


def _dot01(m01, x):
    hi = x.astype(BF16)
    lo = (x - hi.astype(F32)).astype(BF16)
    return _dot(m01, hi) + _dot(m01, lo)


def _rstd(x):
    return lax.rsqrt(jnp.mean(x * x, axis=-1, keepdims=True) + EPS)


def _sigmoid(x):
    return jax.nn.sigmoid(x)


def _colsum(a):
    return jnp.sum(a, axis=0, keepdims=True)


def _stack_rows(rows):
    n = rows[0].shape[-1]
    rid = lax.broadcasted_iota(jnp.int32, (16, n), 0)
    out = jnp.zeros((16, n), F32)
    for i, r in enumerate(rows):
        out = jnp.where(rid == i, r, out)
    return out


def _head_map(fn, *arrs):
    outs = [fn(*[a[:, h * DH:(h + 1) * DH] for a in arrs]) for h in range(HEADS)]
    return jnp.concatenate(outs, axis=1)


def _gla_consts():
    r = np.arange(TM)[:, None]
    c = np.arange(TM)[None, :]
    same = (r // CHUNK) == (c // CHUNK)
    tril = same & (c <= r)
    triu = same & (c >= r)
    m = np.stack([tril, triu]).astype(np.float32)
    return jnp.asarray(m, BF16), jnp.asarray(m, F32)


def _pool_consts():
    r = np.arange(TM)[:, None]
    c = np.arange(TM)[None, :]
    same = (r // GRID_W) == (c // GRID_W)
    rp, cp = r % GRID_W, c % GRID_W
    bs, inv = [], []
    for w in POOL_WINDOWS:
        lo = np.clip(rp - w // 2, 0, GRID_W)
        hi = np.clip(rp - w // 2 + w, 0, GRID_W)
        bs.append(same & (cp >= lo) & (cp < hi))
        inv.append(1.0 / (hi - lo).astype(np.float32))
    b = np.stack(bs).astype(np.float32)
    bt = np.transpose(b, (0, 2, 1))
    return jnp.asarray(b, BF16), jnp.asarray(bt, BF16), jnp.asarray(np.stack(inv), F32)


def _mesh_pos():
    x, y, c = lax.axis_index("x"), lax.axis_index("y"), lax.axis_index("c")
    return x, y, c, 4 * x + 2 * y + c


def _peer(x, y, c, k):
    return (x ^ ((k >> 2) & 1), y ^ ((k >> 1) & 1), c ^ (k & 1))


def _gather_small(w_in, w_out, pw_in, pgrp, pw_out, lb_l, pscale, c, c_ctx, ada_w):
    n_arr = 4

    def body(win_r, wout_r, pwin_r, pgrp_r, pwout_r, lb_r, ps_r, c_r, cctx_r, ada_r,
             s_win, s_wout, s_pwin, s_pgrp, s_pwout, lb_o, ps_o, cg_o, mod_o, ssem, rsem):
        x, y, cc, idx = _mesh_pos()
        srcs = [lb_r, ps_r, c_r, mod_o.at[idx]]
        mine = [lb_o.at[idx], ps_o.at[idx], cg_o.at[idx], mod_o.at[idx]]

        def remote(a, k):
            return pltpu.make_async_remote_copy(src_ref=srcs[a], dst_ref=mine[a], send_sem=ssem.at[a, k], recv_sem=rsem.at[a, k],
                                                device_id=_peer(x, y, cc, k), device_id_type=MESH)

        first = [remote(a, k) for k in range(1, NDEV) for a in (2, 0, 1)]
        for cp in first:
            cp.start()
        lb_o[idx] = lb_r[...]
        ps_o[idx] = ps_r[...]
        cg_o[idx] = c_r[...]
        s_win[...] = win_r[...].astype(BF16)
        s_wout[...] = wout_r[...].astype(BF16)
        s_pwin[...] = pwin_r[...].astype(BF16)
        s_pgrp[...] = pgrp_r[...].astype(BF16)
        s_pwout[...] = pwout_r[...].astype(BF16)
        for k in range(1, NDEV):
            remote(2, k).wait_recv()
        rows = _stack_rows([cg_o[i] for i in range(NDEV)] + [cctx_r[...]])
        sc = rows * _sigmoid(rows)
        for l in range(2):
            mod_o[idx, l] = _dot(sc, ada_r[l])
        second = [remote(3, k) for k in range(1, NDEV)]
        for cp in second:
            cp.start()
        for cp in first + second:
            cp.wait_send()
        for k in range(1, NDEV):
            for a in (0, 1, 3):
                remote(a, k).wait_recv()

    outs = (
        _sds((D, SH_WIN), BF16), _sds((SH_ROWS, D), BF16), _sds((D, SH_PWIN), BF16), _sds((4, SH_GRP, PG), BF16), _sds((SH_ROWS, D), BF16),
        _sds((NDEV, 2, DH), F32), _sds((NDEV, 1, DH), F32), _sds((NDEV, 1, D), F32), _sds((NDEV, 2, 16, SH_ADA), F32),
    )
    return pl.pallas_call(
        body, name="gather_small", out_shape=outs,
        in_specs=[VMEM_SPEC] * 10, out_specs=[VMEM_SPEC] * 9,
        scratch_shapes=[pltpu.SemaphoreType.DMA((n_arr, NDEV)), pltpu.SemaphoreType.DMA((n_arr, NDEV))],
        compiler_params=pltpu.CompilerParams(vmem_limit_bytes=VMEM_LIMIT),
    )(w_in, w_out, pw_in, pgrp, pw_out, lb_l, pscale, c, c_ctx, ada_w)


def _gather_order(s):
    if isinstance(s, int):
        return (0, 1, 2, 4, 3, 5, 6, 7)[s]
    return s + (s == 3).astype(jnp.int32) - (s == 4).astype(jnp.int32)


GATHER_ISSUE = (1, 2, 4, 3, 5, 6, 7)
GATHER_ICI = (2, 4, 6)
GATHER_DIRECT = (1,) + GATHER_ICI
GATHER_FORWARD_AT = 3
RS_SLOTS = 5


def _weight_slices(refs, i):
    wout, pwin, pgrp, pwout = refs
    return [wout.at[pl.ds(pl.multiple_of(i * SH_ROWS, SH_ROWS), SH_ROWS), :],
            pwin.at[:, pl.ds(pl.multiple_of(i * SH_PWIN, 128), SH_PWIN)],
            pgrp.at[:, pl.ds(pl.multiple_of(i * SH_GRP, SH_GRP), SH_GRP), :],
            pwout.at[pl.ds(pl.multiple_of(i * SH_ROWS, SH_ROWS), SH_ROWS), :]]


def _modulated(x, nw, shift, scale):
    r = _rstd(x)
    xn = x * r
    a = xn * nw
    return a * (1.0 + scale) + shift, r, xn, a


def _f1_gather_matmul(idx1, xcat, nw, msel, s_win):
    def body(idx_ref, x_ref, nw_ref, m_ref, sw_ref, g_ref, win_o, wslot, hx_sc, ssem, rsem, lsem, osem):
        del idx_ref
        s, i = pl.program_id(0), pl.program_id(1)
        x, y, cc, idx = _mesh_pos()
        k = _gather_order(s)
        j = idx ^ k

        def remote(kk):
            return pltpu.make_async_remote_copy(src_ref=sw_ref, dst_ref=wslot.at[idx], send_sem=ssem.at[kk], recv_sem=rsem.at[kk],
                                                device_id=_peer(x, y, cc, kk), device_id_type=MESH)

        def forward(kk):
            jj = idx ^ kk
            return pltpu.make_async_remote_copy(src_ref=wslot.at[jj], dst_ref=wslot.at[jj], send_sem=ssem.at[kk ^ 1],
                                                recv_sem=rsem.at[kk ^ 1], device_id=(x, y, 1 - cc), device_id_type=MESH)

        own = pltpu.make_async_copy(sw_ref, wslot.at[idx], lsem)

        def to_hbm(jj, kk):
            return pltpu.make_async_copy(wslot.at[jj], win_o.at[:, pl.ds(pl.multiple_of(jj * SH_WIN, 128), SH_WIN)], osem.at[kk])

        @pl.when((s == 0) & (i == 0))
        def _():
            own.start()
            for kk in GATHER_DIRECT:
                remote(kk).start()
            own.wait()

        @pl.when(s == 0)
        def _():
            hx, _, _, _ = _modulated(x_ref[...], nw_ref[...], m_ref[0, 0:1, :], m_ref[0, 1:2, :])
            hx_sc[i] = hx.astype(BF16)

        @pl.when((s > 0) & (i == 0))
        def _():
            remote(k).wait_recv()

            @pl.when((k & 1) == 0)
            def _():
                forward(k).start()

        @pl.when(i == 0)
        def _():
            to_hbm(j, k).start()

        g_ref[...] = jnp.dot(hx_sc[i], wslot[j], preferred_element_type=F32)

        @pl.when((s == NDEV - 1) & (i == NT - 1))
        def _():
            for kk in GATHER_DIRECT:
                remote(kk).wait_send()
            for kk in GATHER_ICI:
                forward(kk).wait_send()
            for kk in range(NDEV):
                to_hbm(idx ^ kk, kk).wait()

    grid_spec = pltpu.PrefetchScalarGridSpec(
        num_scalar_prefetch=1, grid=(NDEV, NT),
        in_specs=[pl.BlockSpec((TM, D), lambda s, i, ix: (i, 0)), VMEM_SPEC,
                  pl.BlockSpec((1, 2, D), lambda s, i, ix: (jnp.minimum(i, 1), 0, 0)), HBM_SPEC],
        out_specs=[pl.BlockSpec((TM, SH_WIN), lambda s, i, ix: (i, ix[0] ^ _gather_order(s))), HBM_SPEC],
        scratch_shapes=[pltpu.VMEM((NDEV, D, SH_WIN), BF16), pltpu.VMEM((NT, TM, D), BF16),
                        pltpu.SemaphoreType.DMA((NDEV,)), pltpu.SemaphoreType.DMA((NDEV,)), pltpu.SemaphoreType.DMA,
                        pltpu.SemaphoreType.DMA((NDEV,))])
    return pl.pallas_call(
        body, name="f1_gather_matmul", grid_spec=grid_spec,
        out_shape=(_sds((TT, WIN_COLS), F32), _sds((D, WIN_COLS), BF16)),
        compiler_params=pltpu.CompilerParams(dimension_semantics=("arbitrary", "arbitrary"), vmem_limit_bytes=VMEM_LIMIT),
    )(idx1, xcat, nw, msel, s_win)


def _gla_gates(pre, qpre, lbd, cum, rev):
    rows, n = pre.shape
    nch = rows // CHUNK
    sig = _sigmoid(pre)
    f = lbd + (1.0 - lbd) * sig
    k = 1.0 - f
    g = _dot01(cum, jnp.log(f))
    g3 = g.reshape(nch, CHUNK, n)
    last = 0 if rev else CHUNK - 1
    mid = CHUNK // 2 if rev else CHUNK // 2 - 1
    gl1, gm1 = g3[:, last:last + 1, :], g3[:, mid:mid + 1, :]

    def bc(a):
        return jnp.broadcast_to(a, g3.shape).reshape(rows, n)

    gm = bc(gm1)
    e_q, e_k = jnp.exp(g - gm), jnp.exp(gm - g)
    e_in, e_end = e_q * bc(jnp.exp(gm1)), e_k * bc(jnp.exp(gl1 - gm1))
    qsig = _sigmoid(qpre)
    qs = qpre * qsig * (DH ** -0.5)
    return dict(sig=sig, f=f, k=k, qsig=qsig, qs=qs, e_q=e_q, e_k=e_k, e_in=e_in, e_end=e_end,
                decay=[jnp.exp(g3[ci, last:last + 1, :]) for ci in range(nch)])


def _put_heads(ref, lead, arr):
    for h in range(HEADS):
        ref[lead + (h,)] = arr[:, h * DH:(h + 1) * DH]


def _get_heads(ref, lead=()):
    return jnp.concatenate([ref[lead + (h,)] for h in range(HEADS)], axis=1)


def _gla_prep(g_all, lb, cum01):
    def body(g_ref, lb_ref, cum_ref, p0_ref, p1_ref, v_ref, dec_ref):
        qpre = g_ref[:, 3 * E:4 * E]
        _put_heads(v_ref, (), g_ref[:, 2 * E:3 * E].astype(BF16))
        dec_ref[...] = jnp.zeros_like(dec_ref)
        for d, p_ref in ((0, p0_ref), (1, p1_ref)):
            t = _gla_gates(g_ref[:, d * E:(d + 1) * E], qpre, lb_ref[d:d + 1, :], cum_ref[d], d == 1)
            _put_heads(p_ref, (0,), (t["qs"] * t["e_q"]).astype(BF16))
            _put_heads(p_ref, (1,), (t["k"] * t["e_k"]).astype(BF16))
            _put_heads(p_ref, (2,), (t["qs"] * t["e_in"]).astype(BF16))
            _put_heads(p_ref, (3,), (t["k"] * t["e_end"]).astype(BF16))
            for ci in range(TM // CHUNK):
                dec_ref[d, 0, ci:ci + 1, :] = t["decay"][ci]

    quad = pl.BlockSpec((4, HEADS, TM, DH), lambda i: (0, 0, i, 0))
    return pl.pallas_call(
        body, name="gla_prep", grid=(NT,),
        in_specs=[pl.BlockSpec((TM, WIN_COLS), lambda i: (i, 0)), VMEM_SPEC, VMEM_SPEC],
        out_specs=[quad, quad, pl.BlockSpec((HEADS, TM, DH), lambda i: (0, i, 0)), pl.BlockSpec((2, 1, 8, E), lambda i: (0, i, 0, 0))],
        out_shape=(_sds((4, HEADS, TT, DH), BF16), _sds((4, HEADS, TT, DH), BF16), _sds((HEADS, TT, DH), BF16), _sds((2, NT, 8, E), F32)),
        compiler_params=pltpu.CompilerParams(dimension_semantics=("arbitrary",), vmem_limit_bytes=VMEM_LIMIT),
    )(g_all, lb, cum01)


def _scan_tile(i, rev):
    t = jnp.where(i == 0, 0, NT - i) if rev else i
    return t, pl.ds(pl.multiple_of(t * TM, TM), TM)


def _chunk_order(rev):
    n = TM // CHUNK
    return tuple(range(n - 1, -1, -1)) if rev else tuple(range(n))


def _gla_fwd(p0, p1, v_all, dec, mask01, s_wout, s_pwin, s_pgrp, s_pwout):
    def body(p0_ref, p1_ref, v_ref, dec_ref, msk_ref, swout_r, spwin_r, spgrp_r, spwout_r,
             o_ref, wout_o, pwin_o, pgrp_o, pwout_o, ob_sc, ssem, rsem, lsem):
        h = pl.program_id(0)
        x, y, cc, idx = _mesh_pos()
        srcs = [swout_r, spwin_r, spgrp_r, spwout_r]
        gathered = (wout_o, pwin_o, pgrp_o, pwout_o)
        mine = _weight_slices(gathered, idx)

        def remote(a, k):
            return pltpu.make_async_remote_copy(src_ref=srcs[a], dst_ref=mine[a], send_sem=ssem.at[a, k], recv_sem=rsem.at[a, k],
                                                device_id=_peer(x, y, cc, k), device_id_type=MESH)

        def forward(a, k):
            blk = _weight_slices(gathered, idx ^ k)[a]
            return pltpu.make_async_remote_copy(src_ref=blk, dst_ref=blk, send_sem=ssem.at[a, k ^ 1], recv_sem=rsem.at[a, k ^ 1],
                                                device_id=(x, y, 1 - cc), device_id_type=MESH)

        copies = [remote(a, k) for k in GATHER_DIRECT for a in range(4)]
        passed = [forward(a, k) for k in GATHER_ICI for a in range(4)]
        local = [pltpu.make_async_copy(srcs[a], mine[a], lsem.at[a]) for a in range(4)]

        @pl.when(h == 0)
        def _():
            for cp in copies + local:
                cp.start()

        @pl.when(h == GATHER_FORWARD_AT)
        def _():
            for k in GATHER_ICI:
                for a in range(4):
                    remote(a, k).wait_recv()
                    forward(a, k).start()

        def tile_body(i, sts):
            new = []
            for d, p_ref in ((0, p0_ref), (1, p1_ref)):
                rev = d == 1
                st = sts[d]
                t, rows = _scan_tile(i, rev)
                v = v_ref[0, rows, :]
                a = _dot_tb(p_ref[0, 0, rows, :], p_ref[1, 0, rows, :]) * msk_ref[d]
                intra = _dot(a, v)
                q_in, kend = p_ref[2, 0, rows, :], p_ref[3, 0, rows, :]
                outs = [None] * (TM // CHUNK)
                for ci in _chunk_order(rev):
                    r = slice(ci * CHUNK, (ci + 1) * CHUNK)
                    outs[ci] = _dot_tb(q_in[r], st) + intra[r]
                    st = st * dec_ref[d, t, ci:ci + 1, :] + _dot_ta(v[r], kend[r])
                o_t = jnp.concatenate(outs, axis=0)
                if rev:
                    ob_sc[rows, :] = o_t
                else:
                    o_ref[0, rows, :] = o_t
                new.append(st)
            return tuple(new)

        zero = jnp.zeros((DH, DH), F32)
        lax.fori_loop(0, NT, tile_body, (zero, zero))
        o_ref[0] += ob_sc[...]

        @pl.when(h == HEADS - 1)
        def _():
            for cp in copies + passed:
                cp.wait_send()
            for a in range(4):
                remote(a, 1).wait_recv()
            for cp in passed:
                cp.wait_recv()
            for cp in local:
                cp.wait()

    quad = pl.BlockSpec((4, 1, TT, DH), lambda h: (0, h, 0, 0))
    head = pl.BlockSpec((1, TT, DH), lambda h: (h, 0, 0))
    return pl.pallas_call(
        body, name="gla_fwd", grid=(HEADS,),
        in_specs=[quad, quad, head, pl.BlockSpec((2, NT, 8, DH), lambda h: (0, 0, 0, h)),
                  pl.BlockSpec((2, TM, TM), lambda h: (0, 0, 0))] + [HBM_SPEC] * 4,
        out_specs=[head] + [HBM_SPEC] * 4,
        out_shape=(_sds((HEADS, TT, DH), F32), _sds((E, D), BF16), _sds((D, 2 * E), BF16), _sds((4, PG, PG), BF16), _sds((E, D), BF16)),
        scratch_shapes=[pltpu.VMEM((TT, DH), F32), pltpu.SemaphoreType.DMA((4, NDEV)), pltpu.SemaphoreType.DMA((4, NDEV)),
                        pltpu.SemaphoreType.DMA((4,))],
        compiler_params=pltpu.CompilerParams(dimension_semantics=("arbitrary",), vmem_limit_bytes=VMEM_LIMIT),
    )(p0, p1, v_all, dec, mask01, s_wout, s_pwin, s_pgrp, s_pwout)


def _gated_norm(o, z, gw):
    r = _head_map(lambda oh: jnp.broadcast_to(_rstd(oh), oh.shape), o)
    on = o * r
    zs = _sigmoid(z)
    sz = z * zs
    return on * gw * sz, r, on, zs, sz


def _f3_out(o, g_all, xcat, gate, gw, wout):
    def body(o_ref, z_ref, x_ref, gate_ref, gw_ref, w_ref, x1_ref):
        og, _, _, _, _ = _gated_norm(_get_heads(o_ref), z_ref[...], gw_ref[...])
        x1_ref[...] = x_ref[...] + gate_ref[...] * _dot(og, w_ref[...])

    return pl.pallas_call(
        body, name="f3_out", grid=(NTX,),
        in_specs=[pl.BlockSpec((HEADS, TM, DH), lambda i: (0, i + 1, 0)), pl.BlockSpec((TM, E), lambda i: (i + 1, 4)),
                  pl.BlockSpec((TM, D), lambda i: (i + 1, 0)), pl.BlockSpec((1, D), lambda i: (0, 0)),
                  pl.BlockSpec((1, E), lambda i: (0, 0)), pl.BlockSpec((E, D), lambda i: (0, 0))],
        out_specs=pl.BlockSpec((TM, D), lambda i: (i, 0)),
        out_shape=_sds((T, D), F32),
        compiler_params=pltpu.CompilerParams(dimension_semantics=("arbitrary",)),
    )(o, g_all, xcat, gate, gw, wout)


def _pool_layer(x1, tgt, mod1, nw1, fnw, pwin, pgrp, pscale, pwout, pb, pbt, pinv):
    def body(x_ref, t_ref, m_ref, nw_ref, fw_ref, pwin_ref, pgrp_ref, ps_ref, pwout_ref, pb_ref, pbt_ref, pinv_ref,
             dx_ref, gpwin_o, gpgrp_o, gpwout_o, dmod_o, gnw_o, gfw_o, gps_o, loss_o,
             a_pwin, a_pgrp, a_pwout):
        i = pl.program_id(0)

        @pl.when(i == 0)
        def _():
            for ref in (a_pwin, a_pgrp, a_pwout, dmod_o, gnw_o, gfw_o, gps_o, loss_o):
                ref[...] = jnp.zeros_like(ref)

        shift, scale, gate = m_ref[0:1, :], m_ref[1:2, :], m_ref[2:3, :]
        nw, fw, ps = nw_ref[...], fw_ref[...], ps_ref[...]
        x1 = x_ref[...]
        hx, r1, xn, a = _modulated(x1, nw, shift, scale)
        hxb = hx.astype(BF16)
        uz = jnp.dot(hxb, pwin_ref[...], preferred_element_type=F32)
        u, z = uz[:, :E], uz[:, E:]
        pooled, ys = [], []
        for g in range(4):
            ug = u[:, g * PG:(g + 1) * PG]
            pg = _dot01(pb_ref[g], ug) * pinv_ref[g] - ug
            pooled.append(pg.astype(BF16))
            ys.append(_dot(pooled[g], pgrp_ref[g]))
        ycat = jnp.concatenate(ys, axis=1)
        y = ycat * ps
        zs = _sigmoid(z)
        sz = z * zs
        p = (y * sz).astype(BF16)
        out = _dot(p, pwout_ref[...])
        x2 = x1 + gate * out
        r2 = _rstd(x2)
        xn2 = x2 * r2
        diff = xn2 * fw - t_ref[...]
        loss_o[...] += _colsum(diff * diff)
        dyf = diff * (1.0 / D)
        gfw_o[...] += _colsum(dyf * xn2)
        dxn2 = dyf * fw
        dx2 = r2 * (dxn2 - xn2 * jnp.mean(dxn2 * xn2, axis=-1, keepdims=True))
        dgate = _colsum(dx2 * out)
        dout = (dx2 * gate).astype(BF16)
        for j in range(4):
            cs = slice(j * PG, (j + 1) * PG)
            a_pwout[:, cs] += _dot_ta(p, dout[:, cs])
        dp = _dot_tb(dout, pwout_ref[...])
        dy = dp * sz
        dz = dp * y * (zs * (1.0 + z * (1.0 - zs)))
        gps_o[...] += _colsum(dy * ycat)
        dycat = dy * ps
        dus = []
        for g in range(4):
            dyg = dycat[:, g * PG:(g + 1) * PG].astype(BF16)
            a_pgrp[g] += _dot_ta(pooled[g], dyg)
            dpg = _dot_tb(dyg, pgrp_ref[g])
            dus.append(_dot01(pbt_ref[g], dpg * pinv_ref[g]) - dpg)
        duz = jnp.concatenate(dus + [dz], axis=1).astype(BF16)
        for j in range(2 * E // PG):
            cs = slice(j * PG, (j + 1) * PG)
            a_pwin[:, cs] += _dot_ta(hxb, duz[:, cs])
        dhx = _dot_tb(duz, pwin_ref[...])
        dmod_o[0:1, :] += _colsum(dhx)
        dmod_o[1:2, :] += _colsum(dhx * a)
        dmod_o[2:3, :] += dgate
        da = dhx * (1.0 + scale)
        gnw_o[...] += _colsum(da * xn)
        dxn = da * nw
        dx_ref[...] = dx2 + r1 * (dxn - xn * jnp.mean(dxn * xn, axis=-1, keepdims=True))

        @pl.when(i == NTX - 1)
        def _():
            gpwin_o[...] = a_pwin[...].astype(BF16)
            gpgrp_o[...] = a_pgrp[...].astype(BF16)
            gpwout_o[...] = a_pwout[...].astype(BF16)

    tile = pl.BlockSpec((TM, D), lambda i: (i, 0))
    outs = (_sds((T, D), F32), _sds((D, 2 * E), BF16), _sds((4, PG, PG), BF16), _sds((E, D), BF16),
            _sds((3, D), F32), _sds((1, D), F32), _sds((1, D), F32), _sds((1, E), F32), _sds((1, D), F32))
    return pl.pallas_call(
        body, name="pool_layer", grid=(NTX,),
        in_specs=[tile, tile] + [VMEM_SPEC] * 10,
        out_specs=[tile] + [VMEM_SPEC] * 8,
        out_shape=outs,
        scratch_shapes=[pltpu.VMEM((D, 2 * E), F32), pltpu.VMEM((4, PG, PG), F32), pltpu.VMEM((E, D), F32)],
        compiler_params=pltpu.CompilerParams(dimension_semantics=("arbitrary",), vmem_limit_bytes=VMEM_LIMIT),
    )(x1, tgt, mod1, nw1, fnw, pwin, pgrp, pscale, pwout, pb, pbt, pinv)


def _b3_out_bwd(dx1, o, g_all, gate, gw, wout):
    def body(dx_ref, o_ref, z_ref, gate_ref, gw_ref, w_ref, do_ref, dz_ref, gw_o, dgate_o, ggw_o, acc):
        i = pl.program_id(0)

        @pl.when(i == 0)
        def _():
            acc[...] = jnp.zeros_like(acc)
            dgate_o[...] = jnp.zeros_like(dgate_o)
            ggw_o[...] = jnp.zeros_like(ggw_o)
            do_ref[...] = jnp.zeros_like(do_ref)
            dz_ref[...] = jnp.zeros_like(dz_ref)

        @pl.when(i > 0)
        def _():
            gw = gw_ref[...]
            z = z_ref[...]
            og, r, on, zs, sz = _gated_norm(_get_heads(o_ref), z, gw)
            ogb = og.astype(BF16)
            dx = dx_ref[...]
            dgate_o[...] += _colsum(dx * _dot(ogb, w_ref[...]))
            dy = (dx * gate_ref[...]).astype(BF16)
            for j in range(4):
                cs = slice(j * PG, (j + 1) * PG)
                acc[:, cs] += _dot_ta(ogb, dy[:, cs])
            dog = _dot_tb(dy, w_ref[...])
            dz_ref[...] = (dog * (on * gw) * (zs * (1.0 + z * (1.0 - zs)))).astype(BF16)
            dong = dog * sz
            ggw_o[...] += _colsum(dong * on)
            don = dong * gw
            do = _head_map(lambda dh, nh, rh: rh * (dh - nh * jnp.mean(dh * nh, axis=-1, keepdims=True)), don, on, r)
            _put_heads(do_ref, (), do.astype(BF16))

        @pl.when(i == NT - 1)
        def _():
            gw_o[...] = acc[...].astype(BF16)

    prev = lambda i: (jnp.maximum(i - 1, 0), 0)
    heads = pl.BlockSpec((HEADS, TM, DH), lambda i: (0, i, 0))
    return pl.pallas_call(
        body, name="b3_out_bwd", grid=(NT,),
        in_specs=[pl.BlockSpec((TM, D), prev), heads, pl.BlockSpec((TM, E), lambda i: (i, 4)),
                  VMEM_SPEC, VMEM_SPEC, VMEM_SPEC],
        out_specs=[heads, pl.BlockSpec((TM, E), lambda i: (i, 0)), VMEM_SPEC, VMEM_SPEC, VMEM_SPEC],
        out_shape=(_sds((HEADS, TT, DH), BF16), _sds((TT, E), BF16), _sds((E, D), BF16), _sds((1, D), F32), _sds((1, E), F32)),
        scratch_shapes=[pltpu.VMEM((E, D), F32)],
        compiler_params=pltpu.CompilerParams(dimension_semantics=("arbitrary",), vmem_limit_bytes=VMEM_LIMIT),
    )(dx1, o, g_all, gate, gw, wout)


def _gla_bwd(p0, p1, v_all, dec, do, mask01, gwout, gpwin, gpgrp, gpwout):
    nch = TM // CHUNK

    def body(p0_ref, p1_ref, v_ref, dec_ref, do_ref, msk_ref, gwout_r, gpwin_r, gpgrp_r, gpwout_r,
             d0_ref, d1_ref, dv_ref, dgl_ref, rwout_o, rpwin_o, rpgrp_o, rpwout_o,
             ss_sc, dv_sc, ssem, rsem, lsem):
        h = pl.program_id(0)
        x, y, cc, idx = _mesh_pos()
        grads = (gwout_r, gpwin_r, gpgrp_r, gpwout_r)
        dsts = [rwout_o.at[idx], rpwin_o.at[idx], rpgrp_o.at[idx], rpwout_o.at[idx]]

        def remote(a, k):
            px, py, pc = _peer(x, y, cc, k)
            return pltpu.make_async_remote_copy(src_ref=_weight_slices(grads, 4 * px + 2 * py + pc)[a], dst_ref=dsts[a],
                                                send_sem=ssem.at[a, k], recv_sem=rsem.at[a, k], device_id=(px, py, pc), device_id_type=MESH)

        copies = [remote(a, k) for k in GATHER_ISSUE for a in range(4)]
        local = [pltpu.make_async_copy(_weight_slices(grads, idx)[a], dsts[a], lsem.at[a]) for a in range(4)]

        @pl.when(h == 0)
        def _():
            for cp in copies + local:
                cp.start()

        zero = jnp.zeros((DH, DH), F32)
        dgl_ref[...] = jnp.zeros_like(dgl_ref)

        def fwd_body(i, sts):
            new = []
            for d, p_ref in ((0, p0_ref), (1, p1_ref)):
                rev = d == 1
                st = sts[d]
                t, rows = _scan_tile(i, rev)
                v, kend = v_ref[0, rows, :], p_ref[3, 0, rows, :]
                for n, ci in enumerate(_chunk_order(rev)):
                    r = slice(ci * CHUNK, (ci + 1) * CHUNK)
                    ss_sc[d, i * nch + n] = st
                    st = st * dec_ref[d, t, ci:ci + 1, :] + _dot_ta(v[r], kend[r])
                new.append(st)
            return tuple(new)

        lax.fori_loop(0, NT, fwd_body, (zero, zero))

        def bwd_body(ii, dsts_):
            i = NT - 1 - ii
            new = []
            for d, p_ref, d_ref in ((0, p0_ref, d0_ref), (1, p1_ref, d1_ref)):
                rev = d == 1
                order = _chunk_order(rev)
                dst = dsts_[d]
                t, rows = _scan_tile(i, rev)
                qg, kg, q_in, kend = p_ref[0, 0, rows, :], p_ref[1, 0, rows, :], p_ref[2, 0, rows, :], p_ref[3, 0, rows, :]
                v, dob, msk = v_ref[0, rows, :], do_ref[0, rows, :], msk_ref[d]
                a = (_dot_tb(qg, kg) * msk).astype(BF16)
                da = (_dot_tb(dob, v) * msk).astype(BF16)
                d_ref[0, 0, rows, :] = _dot(da, kg).astype(BF16)
                d_ref[1, 0, rows, :] = _dot_ta(da, qg).astype(BF16)
                dv_intra = _dot_ta(a, dob)
                dv_l, dkend_l, dqin_l = [None] * nch, [None] * nch, [None] * nch
                for n in range(nch - 1, -1, -1):
                    ci = order[n]
                    r = slice(ci * CHUNK, (ci + 1) * CHUNK)
                    s_c = ss_sc[d, i * nch + n]
                    dec = dec_ref[d, t, ci:ci + 1, :]
                    dstb = dst.astype(BF16)
                    dv_l[ci] = dv_intra[r] + _dot_tb(kend[r], dstb)
                    dkend_l[ci] = _dot(v[r], dstb)
                    dqin_l[ci] = _dot(dob[r], s_c)
                    dgl_ref[d, t, ci:ci + 1, :] = jnp.sum(s_c * dst, axis=0, keepdims=True) * dec
                    dst = dst * dec + _dot_ta(dob[r], q_in[r])
                d_ref[2, 0, rows, :] = jnp.concatenate(dqin_l, axis=0).astype(BF16)
                d_ref[3, 0, rows, :] = jnp.concatenate(dkend_l, axis=0).astype(BF16)
                dv_sc[d, rows, :] = jnp.concatenate(dv_l, axis=0)
                new.append(dst)
            return tuple(new)

        lax.fori_loop(0, NT, bwd_body, (zero, zero))
        dv_ref[0] = (dv_sc[0] + dv_sc[1]).astype(BF16)

        @pl.when(h == HEADS - 1)
        def _():
            for cp in copies:
                cp.wait_send()
            for cp in copies:
                cp.wait_recv()
            for cp in local:
                cp.wait()

    quad = pl.BlockSpec((4, 1, TT, DH), lambda h: (0, h, 0, 0))
    col = pl.BlockSpec((1, TT, DH), lambda h: (h, 0, 0))
    chunkv = pl.BlockSpec((2, NT, 8, DH), lambda h: (0, 0, 0, h))
    outs = (_sds((4, HEADS, TT, DH), BF16), _sds((4, HEADS, TT, DH), BF16), _sds((HEADS, TT, DH), BF16), _sds((2, NT, 8, E), F32),
            _sds((NDEV, SH_ROWS, D), BF16), _sds((NDEV, D, SH_PWIN), BF16), _sds((NDEV, 4, SH_GRP, PG), BF16), _sds((NDEV, SH_ROWS, D), BF16))
    return pl.pallas_call(
        body, name="gla_bwd", grid=(HEADS,),
        in_specs=[quad, quad, col, chunkv, col, pl.BlockSpec((2, TM, TM), lambda h: (0, 0, 0))] + [HBM_SPEC] * 4,
        out_specs=[quad, quad, col, chunkv] + [HBM_SPEC] * 4,
        out_shape=outs,
        scratch_shapes=[pltpu.VMEM((2, NT * nch, DH, DH), F32), pltpu.VMEM((2, TT, DH), F32),
                        pltpu.SemaphoreType.DMA((4, NDEV)), pltpu.SemaphoreType.DMA((4, NDEV)), pltpu.SemaphoreType.DMA((4,))],
        compiler_params=pltpu.CompilerParams(dimension_semantics=("arbitrary",), vmem_limit_bytes=VMEM_LIMIT),
    )(p0, p1, v_all, dec, do, mask01, gwout, gpwin, gpgrp, gpwout)


TMB = 128


def _gla_post_bwd(g_all, d0, d1, dgl, dv, dz, lb, cum01):
    nch = TMB // CHUNK

    def body(g_ref, d0_ref, d1_ref, dgl_ref, dv_ref, dz_ref, lb_ref, cum_ref, dg_ref, dlb_ref):
        i = pl.program_id(0)

        @pl.when(i == 0)
        def _():
            dlb_ref[...] = jnp.zeros_like(dlb_ref)

        half = i & 1
        qpre = g_ref[:, 3 * E:4 * E]
        dqs_sum = None
        dpre = []
        for d, d_ref in ((0, d0_ref), (1, d1_ref)):
            rev = d == 1
            lbd = lb_ref[d:d + 1, :]
            t = _gla_gates(g_ref[:, d * E:(d + 1) * E], qpre, lbd, cum_ref[d, :TMB, :TMB], rev)
            dqg, dkg, dqin, dkend = [_get_heads(d_ref, (ty,)).astype(F32) for ty in range(4)]
            dqs = dqg * t["e_q"] + dqin * t["e_in"]
            dk = dkg * t["e_k"] + dkend * t["e_end"]
            dkk = dkend * (t["k"] * t["e_end"])
            dg = t["qs"] * dqs - t["k"] * dk
            dkk3 = dkk.reshape(nch, CHUNK, E)
            dgl8 = dgl_ref[d, 0]
            dgl_rows = [jnp.where(half == 0, dgl8[ci:ci + 1, :], dgl8[nch + ci:nch + ci + 1, :]) for ci in range(nch)]
            dgl_b = jnp.concatenate([jnp.broadcast_to(dgl_rows[ci] + jnp.sum(dkk3[ci], axis=0, keepdims=True), (CHUNK, E))
                                     for ci in range(nch)], axis=0)
            pos = lax.broadcasted_iota(jnp.int32, (TMB, E), 0) & (CHUNK - 1)
            dg = dg + jnp.where(pos == (0 if rev else CHUNK - 1), dgl_b, 0.0)
            dlf = _dot01(cum_ref[1 - d, :TMB, :TMB], dg)
            df = dlf / t["f"] - dk
            sig = t["sig"]
            dpre.append((df * (1.0 - lbd) * sig * (1.0 - sig)).astype(BF16))
            dlb_ref[d:d + 1, :] += _colsum(df * (1.0 - sig))
            dqs_sum = dqs if dqs_sum is None else dqs_sum + dqs
            qsig = t["qsig"]
        dqpre = dqs_sum * (DH ** -0.5) * (qsig * (1.0 + qpre * (1.0 - qsig)))
        dg_ref[...] = jnp.concatenate([dpre[0], dpre[1], _get_heads(dv_ref), dqpre.astype(BF16), dz_ref[...]], axis=1)

    quad = pl.BlockSpec((4, HEADS, TMB, DH), lambda i: (0, 0, i, 0))
    tile = pl.BlockSpec((TMB, E), lambda i: (i, 0))
    return pl.pallas_call(
        body, name="gla_post_bwd", grid=(TT // TMB,),
        in_specs=[pl.BlockSpec((TMB, WIN_COLS), lambda i: (i, 0)), quad, quad,
                  pl.BlockSpec((2, 1, 8, E), lambda i: (0, i // 2, 0, 0)), pl.BlockSpec((HEADS, TMB, DH), lambda i: (0, i, 0)), tile,
                  VMEM_SPEC, VMEM_SPEC],
        out_specs=[pl.BlockSpec((TMB, WIN_COLS), lambda i: (i, 0)), VMEM_SPEC],
        out_shape=(_sds((TT, WIN_COLS), BF16), _sds((2, E), F32)),
        compiler_params=pltpu.CompilerParams(dimension_semantics=("arbitrary",), vmem_limit_bytes=VMEM_LIMIT),
    )(g_all, d0, d1, dgl, dv, dz, lb, cum01)


def _b1_in_bwd(idx1, xcat, dx1, dg, nw, msel, win):
    last_s = NDEV - 1

    def body(idx_ref, x_ref, dx1_ref, dg_ref, nw_ref, m_ref, w_ref, gx_ref, rwin_o, dmx_o, dmc_o, gnw_o,
             hx_sc, dhx_sc, acc, sbuf, pbuf, psend, precv, isend, irecv, sibsem, lsem):
        del idx_ref
        s, i = pl.program_id(0), pl.program_id(1)
        x, y, cc, idx = _mesh_pos()
        shift, scale = m_ref[0, 0:1, :], m_ref[0, 1:2, :]
        sibling = (x, y, 1 - cc)

        def partial(p):
            return pltpu.make_async_remote_copy(src_ref=sbuf.at[0], dst_ref=pbuf.at[p], send_sem=psend.at[p], recv_sem=precv.at[p],
                                                device_id=sibling, device_id_type=MESH)

        def chip_sum(p):
            return pltpu.make_async_remote_copy(src_ref=sbuf.at[1], dst_ref=rwin_o.at[2 + p], send_sem=isend.at[p], recv_sem=irecv.at[p],
                                                device_id=_peer(x, y, cc, 2 * (p + 1)), device_id_type=MESH)

        to_sibling = pltpu.make_async_remote_copy(src_ref=sbuf.at[0], dst_ref=rwin_o.at[1], send_sem=sibsem.at[0], recv_sem=sibsem.at[1],
                                                  device_id=sibling, device_id_type=MESH)
        own = pltpu.make_async_copy(sbuf.at[1], rwin_o.at[0], lsem)

        @pl.when((s == 0) & (i == 0))
        def _():
            for ref in (dmx_o, dmc_o, gnw_o):
                ref[...] = jnp.zeros_like(ref)

        @pl.when(s == 0)
        def _():
            hx, _, _, _ = _modulated(x_ref[...], nw_ref[...], shift, scale)
            hx_sc[i] = hx.astype(BF16)

        @pl.when(i == 0)
        def _():
            acc[...] = jnp.zeros_like(acc)

        dgb = dg_ref[...]
        hxb = hx_sc[i]
        for lo, hi in ((0, 256), (256, 512), (512, SH_WIN)):
            acc[:, lo:hi] += _dot_ta(hxb, dgb[:, lo:hi])
        part = _dot_tb(dgb, w_ref[...])

        @pl.when(s == 0)
        def _():
            dhx_sc[i] = part

        @pl.when(s > 0)
        def _():
            dhx_sc[i] += part

        for p in (2, 1, 0):
            @pl.when((i == NT - 1) & (s == 2 * (2 - p)))
            def _(p=p):
                if p < 2:
                    partial(p + 1).wait_send()
                sbuf[0] = acc[...].astype(BF16)
                partial(p).start()

            @pl.when((i == NT - 1) & (s == 2 * (2 - p) + 1))
            def _(p=p):
                if p < 2:
                    chip_sum(p + 1).wait_send()
                partial(p).wait_recv()
                sbuf[1] = (acc[...] + pbuf[p].astype(F32)).astype(BF16)
                chip_sum(p).start()

        @pl.when((i == NT - 1) & (s == last_s - 1))
        def _():
            partial(0).wait_send()
            sbuf[0] = acc[...].astype(BF16)
            to_sibling.start()

        @pl.when((i == NT - 1) & (s == last_s))
        def _():
            chip_sum(0).wait_send()
            sbuf[1] = acc[...].astype(BF16)
            own.start()

        @pl.when(s == last_s)
        def _():
            nw = nw_ref[...]
            _, r, xn, a = _modulated(x_ref[...], nw, shift, scale)
            dhx = dhx_sc[i]
            dsh, dsc = _colsum(dhx), _colsum(dhx * a)
            da = dhx * (1.0 + scale)
            gnw_o[...] += _colsum(da * xn)
            dxn = da * nw
            gx_ref[...] = dx1_ref[...] + r * (dxn - xn * jnp.mean(dxn * xn, axis=-1, keepdims=True))

            @pl.when(i == 0)
            def _():
                dmc_o[0:1, :] += dsh
                dmc_o[1:2, :] += dsc

            @pl.when(i > 0)
            def _():
                dmx_o[0:1, :] += dsh
                dmx_o[1:2, :] += dsc

        @pl.when((i == NT - 1) & (s == last_s))
        def _():
            to_sibling.wait_send()
            to_sibling.wait_recv()
            for p in range(3):
                chip_sum(p).wait_recv()
            own.wait()

    grid_spec = pltpu.PrefetchScalarGridSpec(
        num_scalar_prefetch=1, grid=(NDEV, NT),
        in_specs=[pl.BlockSpec((TM, D), lambda s, i, ix: (i, 0)), pl.BlockSpec((TM, D), lambda s, i, ix: (jnp.maximum(i - 1, 0), 0)),
                  pl.BlockSpec((TM, SH_WIN), lambda s, i, ix: (i, ix[0] ^ (last_s - s))), VMEM_SPEC,
                  pl.BlockSpec((1, 2, D), lambda s, i, ix: (jnp.minimum(i, 1), 0, 0)),
                  pl.BlockSpec((D, SH_WIN), lambda s, i, ix: (0, ix[0] ^ (last_s - s)))],
        out_specs=[pl.BlockSpec((TM, D), lambda s, i, ix: (jnp.where(s == last_s, jnp.maximum(i - 1, 0), 0), 0)),
                   HBM_SPEC, VMEM_SPEC, VMEM_SPEC, VMEM_SPEC],
        scratch_shapes=[pltpu.VMEM((NT, TM, D), BF16), pltpu.VMEM((NT, TM, D), F32), pltpu.VMEM((D, SH_WIN), F32),
                        pltpu.VMEM((2, D, SH_WIN), BF16), pltpu.VMEM((3, D, SH_WIN), BF16),
                        pltpu.SemaphoreType.DMA((3,)), pltpu.SemaphoreType.DMA((3,)), pltpu.SemaphoreType.DMA((3,)),
                        pltpu.SemaphoreType.DMA((3,)), pltpu.SemaphoreType.DMA((2,)), pltpu.SemaphoreType.DMA])
    return pl.pallas_call(
        body, name="b1_in_bwd", grid_spec=grid_spec,
        out_shape=(_sds((T, D), F32), _sds((RS_SLOTS, D, SH_WIN), BF16), _sds((2, D), F32), _sds((2, D), F32), _sds((1, D), F32)),
        compiler_params=pltpu.CompilerParams(dimension_semantics=("arbitrary", "arbitrary"), vmem_limit_bytes=VMEM_LIMIT),
    )(idx1, xcat, dx1, dg, nw, msel, win)


def _reduce_small(pd, pv, cg, c_ctx, ada_w0):
    n_arr = 3

    def body(pd_r, pv_r, cg_r, cctx_r, ada_r, gada_o, gadab_o, gcctx_o, pvsum_o, loss_o,
             pd_all, pv_all, dsc_all, dsc_mine, ssem, rsem):
        x, y, cc, idx = _mesh_pos()
        srcs = [pd_r, pv_r, dsc_mine]
        dsts = [pd_all.at[idx], pv_all.at[idx], dsc_all.at[idx]]

        def remote(a, k):
            return pltpu.make_async_remote_copy(src_ref=srcs[a], dst_ref=dsts[a], send_sem=ssem.at[a, k], recv_sem=rsem.at[a, k],
                                                device_id=_peer(x, y, cc, k), device_id_type=MESH)

        first = [remote(a, k) for k in range(1, NDEV) for a in (0, 1)]
        for cp in first:
            cp.start()
        pd_all[idx] = pd_r[...]
        pv_all[idx] = pv_r[...]
        for k in range(1, NDEV):
            remote(0, k).wait_recv()
            remote(1, k).wait_recv()
        mine = [pd_all[s, :, pl.ds(idx, 1), :] for s in range(NDEV)]
        dmc = functools.reduce(lambda u, v: u + v, [m[2] for m in mine])
        rows = _stack_rows([cg_r[i] for i in range(NDEV)] + [cctx_r[...]])
        sc = (rows * _sigmoid(rows)).astype(BF16)
        gada_o[0] = _dot_ta(sc, _stack_rows([m[0] for m in mine] + [dmc]))
        gada_o[1] = _dot_ta(sc, _stack_rows([m[1] for m in mine]))
        dsc_mine[...] = _dot_tb(jnp.broadcast_to(dmc, (8, SH_ADA)), ada_r[...])[0:1, :]
        dsc_all[idx] = dsc_mine[...]
        second = [remote(2, k) for k in range(1, NDEV)]
        for cp in second:
            cp.start()
        tot = [functools.reduce(lambda u, v: u + v, [pd_all[s, l] for s in range(NDEV)]) for l in range(3)]
        gadab_o[0] = tot[0] + tot[2]
        gadab_o[1] = tot[1]
        pvs = functools.reduce(lambda u, v: u + v, [pv_all[s] for s in range(NDEV)])
        pvsum_o[...] = pvs
        loss_o[...] = jnp.broadcast_to(jnp.sum(pvs[:, PV_LOSS:PV_LOSS + D], axis=-1, keepdims=True) * (0.5 / D), (1, 128))
        for k in range(1, NDEV):
            remote(2, k).wait_recv()
        dsc = functools.reduce(lambda u, v: u + v, [dsc_all[s] for s in range(NDEV)])
        cx = cctx_r[...]
        sx = _sigmoid(cx)
        gcctx_o[...] = dsc * (sx * (1.0 + cx * (1.0 - sx)))
        for cp in first + second:
            cp.wait_send()

    outs = (_sds((2, D, SH_ADA), F32), _sds((2, NDEV, SH_ADA), F32), _sds((1, D), F32), _sds((1, PV_LEN), F32), _sds((1, 128), F32))
    return pl.pallas_call(
        body, name="reduce_small", out_shape=outs,
        in_specs=[VMEM_SPEC] * 5, out_specs=[VMEM_SPEC] * 5,
        scratch_shapes=[
            pltpu.VMEM((NDEV, 3, NDEV, SH_ADA), F32), pltpu.VMEM((NDEV, 1, PV_LEN), F32), pltpu.VMEM((NDEV, 1, D), F32),
            pltpu.VMEM((1, D), F32),
            pltpu.SemaphoreType.DMA((n_arr, NDEV)), pltpu.SemaphoreType.DMA((n_arr, NDEV)),
        ],
        compiler_params=pltpu.CompilerParams(vmem_limit_bytes=VMEM_LIMIT),
    )(pd, pv, cg, c_ctx, ada_w0)


PV_NW, PV_GNORM, PV_FINAL, PV_LB, PV_PSCALE, PV_LOSS, PV_LEN = 0, 2 * D, 3 * D, 4 * D, 6 * D, 7 * D, 8 * D


def _adamw(w, g, m, v):
    m = ADAM_B1 * m + (1.0 - ADAM_B1) * g
    v = ADAM_B2 * v + (1.0 - ADAM_B2) * (g * g)
    m_hat = m / (1.0 - ADAM_B1 ** ADAM_STEP)
    v_hat = v / (1.0 - ADAM_B2 ** ADAM_STEP)
    delta = -ADAM_LR * (m_hat / (jnp.sqrt(v_hat) + ADAM_EPS) + ADAM_WD * w)
    return delta, m, v


def _adam_sharded(name, parts, w, m, v, tr):
    rr, cc = w.shape
    n = parts.shape[0]

    def body(p_ref, w_ref, m_ref, v_ref, g_o, d_o, m_o, v_o):
        g = p_ref[0].astype(F32)
        for s in range(1, n):
            g = g + p_ref[s].astype(F32)
        d, mn, vn = _adamw(w_ref[...], g, m_ref[...], v_ref[...])
        g_o[...], d_o[...], m_o[...], v_o[...] = g, d, mn, vn

    blk = pl.BlockSpec((tr, cc), lambda i: (i, 0))
    return pl.pallas_call(
        body, name=name, grid=(rr // tr,),
        in_specs=[pl.BlockSpec((n, tr, cc), lambda i: (0, i, 0)), blk, blk, blk],
        out_specs=[blk] * 4, out_shape=(_sds((rr, cc), F32),) * 4,
        compiler_params=pltpu.CompilerParams(dimension_semantics=("arbitrary",)),
    )(parts, w, m, v)


def _adam_dense(name, g, w, m, v, tr):
    rr, cc = w.shape

    def body(g_ref, w_ref, m_ref, v_ref, d_o, m_o, v_o):
        d, mn, vn = _adamw(w_ref[...], g_ref[...], m_ref[...], v_ref[...])
        d_o[...], m_o[...], v_o[...] = d, mn, vn

    blk = pl.BlockSpec((tr, cc), lambda i: (i, 0))
    return pl.pallas_call(
        body, name=name, grid=(rr // tr,), in_specs=[blk] * 4, out_specs=[blk] * 3, out_shape=(_sds((rr, cc), F32),) * 3,
        compiler_params=pltpu.CompilerParams(dimension_semantics=("arbitrary",)),
    )(g, w, m, v)


def _adam_small(gs, ws, ms, vs, lb_idx, lbv):
    n = len(ws)

    def body(*refs):
        g_r, w_r, m_r, v_r = refs[:n], refs[n:2 * n], refs[2 * n:3 * n], refs[3 * n:4 * n]
        lb_r = refs[4 * n]
        outs = refs[4 * n + 1:]
        for j in range(n):
            g = g_r[j][...]
            if j == lb_idx:
                lbj = lb_r[...]
                g = g * lbj * (1.0 - lbj)
            d, mn, vn = _adamw(w_r[j][...], g, m_r[j][...], v_r[j][...])
            outs[j][...], outs[n + j][...], outs[2 * n + j][...], outs[3 * n + j][...] = g, d, mn, vn

    shapes = tuple(_sds(w.shape, F32) for w in ws)
    return pl.pallas_call(body, name="adam_small", out_shape=shapes * 4)(*gs, *ws, *ms, *vs, lbv)


def kernel(x, c, ctx, c_ctx, ada_w, ada_b, norm_w, hgrn_w_in, hgrn_lb_logits, hgrn_gnorm_w, hgrn_w_out, pool_w_in, pool_w_grp, pool_scale, pool_w_out, final_norm_w, loss_target, m_c_ctx, m_ada_w, m_ada_b, m_norm_w, m_hgrn_w_in, m_hgrn_lb_logits, m_hgrn_gnorm_w, m_hgrn_w_out, m_pool_w_in, m_pool_w_grp, m_pool_scale, m_pool_w_out, m_final_norm_w, v_c_ctx, v_ada_w, v_ada_b, v_norm_w, v_hgrn_w_in, v_hgrn_lb_logits, v_hgrn_gnorm_w, v_hgrn_w_out, v_pool_w_in, v_pool_w_grp, v_pool_scale, v_pool_w_out, v_final_norm_w):
    idx = 4 * lax.axis_index("x") + 2 * lax.axis_index("y") + lax.axis_index("c")
    cctx2 = c_ctx.reshape(1, D)
    cum01, mask01 = _gla_consts()
    pb, pbt, pinv = _pool_consts()

    idx1 = idx.reshape(1).astype(jnp.int32)
    s_win, s_wout, s_pwin, s_pgrp, s_pwout, lbl_g, ps_g, cg, mod_g = _gather_small(
        hgrn_w_in[0], hgrn_w_out[0], pool_w_in[0], pool_w_grp[0], pool_w_out[0], hgrn_lb_logits[0], pool_scale, c, cctx2, ada_w)
    lb = jax.nn.sigmoid(jnp.transpose(lbl_g, (1, 0, 2)).reshape(2, E))
    pscale = ps_g.reshape(1, E)
    mod_all = jnp.transpose(mod_g, (1, 2, 0, 3)).reshape(2, 16, 3 * D) + ada_b[:, None, :]
    mod_me = lax.dynamic_index_in_dim(mod_all, idx, axis=1, keepdims=False)
    mod0, mod1, modc = mod_me[0].reshape(3, D), mod_me[1].reshape(3, D), mod_all[0, NDEV].reshape(3, D)
    msel = jnp.stack([modc[:2], mod0[:2]])
    nw0, nw1 = norm_w[0:1], norm_w[1:2]
    fnw = final_norm_w.reshape(1, D)

    xcat = jnp.concatenate([ctx[0], x[0]], axis=0)
    g_all, win = _f1_gather_matmul(idx1, xcat, nw0, msel, s_win)
    p0, p1, v_all, dec = _gla_prep(g_all, lb, cum01)
    o, wout, pwin, pgrp, pwout = _gla_fwd(p0, p1, v_all, dec, mask01, s_wout, s_pwin, s_pgrp, s_pwout)
    x1 = _f3_out(o, g_all, xcat, mod0[2:3], hgrn_gnorm_w, wout)
    dx1, gpwin, gpgrp, gpwout, dmod1, gnw1, gfw, gps, lossv = _pool_layer(
        x1, loss_target[0], mod1, nw1, fnw, pwin, pgrp, pscale, pwout, pb, pbt, pinv)
    do, dz, gwout, dgate0, ggw = _b3_out_bwd(dx1, o, g_all, mod0[2:3], hgrn_gnorm_w, wout)
    d0, d1, dv, dgl, rwout, rpwin, rpgrp, rpwout = _gla_bwd(p0, p1, v_all, dec, do, mask01, gwout, gpwin, gpgrp, gpwout)
    dg, dlb = _gla_post_bwd(g_all, d0, d1, dgl, dv, dz, lb, cum01)
    grad_x, rwin, dmx, dmc, gnw0 = _b1_in_bwd(idx1, xcat, dx1, dg, nw0, msel, win)

    dmod0 = jnp.concatenate([dmx, dgate0], axis=0)
    dmodc = jnp.concatenate([dmc, jnp.zeros((1, D), F32)], axis=0)
    pd = jnp.stack([dmod0, dmod1, dmodc]).reshape(3, NDEV, SH_ADA)
    pv = jnp.concatenate([gnw0, gnw1, ggw, gfw, dlb.reshape(1, 2 * E), gps, lossv], axis=1)
    g_ada, g_adab, g_cctx, pvsum, loss128 = _reduce_small(pd, pv, cg, cctx2, ada_w[0])

    out = {}
    out["hgrn_w_in"] = _adam_sharded("adam_w_in", rwin, hgrn_w_in[0], m_hgrn_w_in[0], v_hgrn_w_in[0], 256)
    out["hgrn_w_out"] = _adam_sharded("adam_w_out", rwout, hgrn_w_out[0], m_hgrn_w_out[0], v_hgrn_w_out[0], SH_ROWS)
    out["pool_w_in"] = _adam_sharded("adam_pw_in", rpwin, pool_w_in[0], m_pool_w_in[0], v_pool_w_in[0], 512)
    out["pool_w_grp"] = _adam_sharded("adam_pgrp", rpgrp.reshape(NDEV, 4 * SH_GRP, PG), pool_w_grp[0].reshape(4 * SH_GRP, PG),
                                      m_pool_w_grp[0].reshape(4 * SH_GRP, PG), v_pool_w_grp[0].reshape(4 * SH_GRP, PG), 4 * SH_GRP)
    out["pool_w_out"] = _adam_sharded("adam_pw_out", rpwout, pool_w_out[0], m_pool_w_out[0], v_pool_w_out[0], SH_ROWS)
    g_ada2 = g_ada.reshape(2 * D, SH_ADA)
    out["ada_w"] = (g_ada2,) + _adam_dense("adam_ada_w", g_ada2, ada_w.reshape(2 * D, SH_ADA), m_ada_w.reshape(2 * D, SH_ADA),
                                           v_ada_w.reshape(2 * D, SH_ADA), 512)

    lb_me = lax.dynamic_slice_in_dim(lb, idx * DH, DH, axis=1)
    small = ["c_ctx", "ada_b", "norm_w", "hgrn_lb_logits", "hgrn_gnorm_w", "pool_scale", "final_norm_w"]
    gs = [g_cctx, g_adab.reshape(2, 3 * D), pvsum[:, PV_NW:PV_NW + 2 * D].reshape(2, D),
          lax.dynamic_slice_in_dim(pvsum[:, PV_LB:PV_LB + 2 * E].reshape(2, E), idx * DH, DH, axis=1),
          pvsum[:, PV_GNORM:PV_GNORM + E], lax.dynamic_slice_in_dim(pvsum[:, PV_PSCALE:PV_PSCALE + E], idx * DH, DH, axis=1),
          pvsum[:, PV_FINAL:PV_FINAL + D]]
    ws = [cctx2, ada_b, norm_w, hgrn_lb_logits[0], hgrn_gnorm_w, pool_scale, fnw]
    ms = [m_c_ctx.reshape(1, D), m_ada_b, m_norm_w, m_hgrn_lb_logits[0], m_hgrn_gnorm_w, m_pool_scale, m_final_norm_w.reshape(1, D)]
    vs = [v_c_ctx.reshape(1, D), v_ada_b, v_norm_w, v_hgrn_lb_logits[0], v_hgrn_gnorm_w, v_pool_scale, v_final_norm_w.reshape(1, D)]
    res = _adam_small(gs, ws, ms, vs, 3, lb_me)
    n = len(small)
    for j, name in enumerate(small):
        out[name] = tuple(res[q * n + j] for q in range(4))

    shapes = {"c_ctx": (D,), "ada_w": (2, D, SH_ADA), "ada_b": (2, 3 * D), "norm_w": (2, D), "hgrn_w_in": (1, D, SH_WIN),
              "hgrn_lb_logits": (1, 2, DH), "hgrn_gnorm_w": (1, E), "hgrn_w_out": (1, SH_ROWS, D), "pool_w_in": (1, D, SH_PWIN),
              "pool_w_grp": (1, 4, SH_GRP, PG), "pool_scale": (1, DH), "pool_w_out": (1, SH_ROWS, D), "final_norm_w": (D,)}
    order = ["c_ctx", "ada_w", "ada_b", "norm_w", "hgrn_w_in", "hgrn_lb_logits", "hgrn_gnorm_w", "hgrn_w_out", "pool_w_in",
             "pool_w_grp", "pool_scale", "pool_w_out", "final_norm_w"]
    flat = [out[name][q].reshape(shapes[name]) for q in range(4) for name in order]
    return (loss128[0, 0], grad_x[None], *flat)
```

```python
import functools

import numpy as np
import jax
import jax.numpy as jnp
from jax import lax
from jax.experimental import pallas as pl
from jax.experimental.pallas import tpu as pltpu

F32 = jnp.float32
BF16 = jnp.bfloat16

D = 1024
E = 1024
HEADS = 8
DH = 128
CHUNK = 64
T = 2048
TC = 256
TT = T + TC
TM = 256
NT = TT // TM
NTX = T // TM
NDEV = 8
GRID_W = 64
POOL_WINDOWS = (2, 4, 8, 16)
PG = 256
EPS = 1e-6
WIN_COLS = 5 * E
SH_WIN = WIN_COLS // NDEV
SH_PWIN = 2 * E // NDEV
SH_ROWS = E // NDEV
SH_GRP = PG // NDEV
SH_ADA = 3 * D // NDEV
VMEM_LIMIT = 56 * 1024 * 1024

ADAM_LR, ADAM_B1, ADAM_B2, ADAM_EPS, ADAM_WD, ADAM_STEP = 0.001, 0.9, 0.999, 1e-08, 0.01, 10

MESH = pl.DeviceIdType.MESH
VMEM_SPEC = pl.BlockSpec(memory_space=pltpu.VMEM)
HBM_SPEC = pl.BlockSpec(memory_space=pltpu.HBM)
ANY_SPEC = pl.BlockSpec(memory_space=pl.ANY)


def _sds(shape, dtype):
    return jax.ShapeDtypeStruct(shape, dtype)


def _bf(a):
    return a if a.dtype == BF16 else a.astype(BF16)


def _dot(a, b):
    return lax.dot_general(_bf(a), _bf(b), (((1,), (0,)), ((), ())), preferred_element_type=F32)


def _dot_tb(a, b):
    return lax.dot_general(_bf(a), _bf(b), (((1,), (1,)), ((), ())), preferred_element_type=F32)


def _dot_ta(a, b):
    return lax.dot_general(_bf(a), _bf(b), (((0,), (0,)), ((), ())), preferred_element_type=F32)


def _dot01(m01, x):
    hi = x.astype(BF16)
    lo = (x - hi.astype(F32)).astype(BF16)
    return _dot(m01, hi) + _dot(m01, lo)


def _rstd(x):
    return lax.rsqrt(jnp.mean(x * x, axis=-1, keepdims=True) + EPS)


def _sigmoid(x):
    return jax.nn.sigmoid(x)


def _colsum(a):
    return jnp.sum(a, axis=0, keepdims=True)


def _stack_rows(rows):
    n = rows[0].shape[-1]
    rid = lax.broadcasted_iota(jnp.int32, (16, n), 0)
    out = jnp.zeros((16, n), F32)
    for i, r in enumerate(rows):
        out = jnp.where(rid == i, r, out)
    return out


def _head_map(fn, *arrs):
    outs = [fn(*[a[:, h * DH:(h + 1) * DH] for a in arrs]) for h in range(HEADS)]
    return jnp.concatenate(outs, axis=1)


def _gla_consts():
    r = np.arange(TM)[:, None]
    c = np.arange(TM)[None, :]
    same = (r // CHUNK) == (c // CHUNK)
    tril = same & (c <= r)
    triu = same & (c >= r)
    m = np.stack([tril, triu]).astype(np.float32)
    return jnp.asarray(m, BF16), jnp.asarray(m, F32)


def _pool_consts():
    r = np.arange(TM)[:, None]
    c = np.arange(TM)[None, :]
    same = (r // GRID_W) == (c // GRID_W)
    rp, cp = r % GRID_W, c % GRID_W
    bs, inv = [], []
    for w in POOL_WINDOWS:
        lo = np.clip(rp - w // 2, 0, GRID_W)
        hi = np.clip(rp - w // 2 + w, 0, GRID_W)
        bs.append(same & (cp >= lo) & (cp < hi))
        inv.append(1.0 / (hi - lo).astype(np.float32))
    b = np.stack(bs).astype(np.float32)
    bt = np.transpose(b, (0, 2, 1))
    return jnp.asarray(b, BF16), jnp.asarray(bt, BF16), jnp.asarray(np.stack(inv), F32)


def _mesh_pos():
    x, y, c = lax.axis_index("x"), lax.axis_index("y"), lax.axis_index("c")
    return x, y, c, 4 * x + 2 * y + c


def _peer(x, y, c, k):
    return (x ^ ((k >> 2) & 1), y ^ ((k >> 1) & 1), c ^ (k & 1))


def _gather_small(w_in, w_out, pw_in, pgrp, pw_out, lb_l, pscale, c, c_ctx, ada_w):
    n_arr = 4

    def body(win_r, wout_r, pwin_r, pgrp_r, pwout_r, lb_r, ps_r, c_r, cctx_r, ada_r,
             s_win, s_wout, s_pwin, s_pgrp, s_pwout, lb_o, ps_o, cg_o, mod_o, ssem, rsem):
        x, y, cc, idx = _mesh_pos()
        srcs = [lb_r, ps_r, c_r, mod_o.at[idx]]
        mine = [lb_o.at[idx], ps_o.at[idx], cg_o.at[idx], mod_o.at[idx]]

        def remote(a, k):
            return pltpu.make_async_remote_copy(src_ref=srcs[a], dst_ref=mine[a], send_sem=ssem.at[a, k], recv_sem=rsem.at[a, k],
                                                device_id=_peer(x, y, cc, k), device_id_type=MESH)

        first = [remote(a, k) for k in range(1, NDEV) for a in (2, 0, 1)]
        for cp in first:
            cp.start()
        lb_o[idx] = lb_r[...]
        ps_o[idx] = ps_r[...]
        cg_o[idx] = c_r[...]
        s_win[...] = win_r[...].astype(BF16)
        s_wout[...] = wout_r[...].astype(BF16)
        s_pwin[...] = pwin_r[...].astype(BF16)
        s_pgrp[...] = pgrp_r[...].astype(BF16)
        s_pwout[...] = pwout_r[...].astype(BF16)
        for k in range(1, NDEV):
            remote(2, k).wait_recv()
        rows = _stack_rows([cg_o[i] for i in range(NDEV)] + [cctx_r[...]])
        sc = rows * _sigmoid(rows)
        for l in range(2):
            mod_o[idx, l] = _dot(sc, ada_r[l])
        second = [remote(3, k) for k in range(1, NDEV)]
        for cp in second:
            cp.start()
        for cp in first + second:
            cp.wait_send()
        for k in range(1, NDEV):
            for a in (0, 1, 3):
                remote(a, k).wait_recv()

    outs = (
        _sds((D, SH_WIN), BF16), _sds((SH_ROWS, D), BF16), _sds((D, SH_PWIN), BF16), _sds((4, SH_GRP, PG), BF16), _sds((SH_ROWS, D), BF16),
        _sds((NDEV, 2, DH), F32), _sds((NDEV, 1, DH), F32), _sds((NDEV, 1, D), F32), _sds((NDEV, 2, 16, SH_ADA), F32),
    )
    return pl.pallas_call(
        body, name="gather_small", out_shape=outs,
        in_specs=[VMEM_SPEC] * 10, out_specs=[VMEM_SPEC] * 9,
        scratch_shapes=[pltpu.SemaphoreType.DMA((n_arr, NDEV)), pltpu.SemaphoreType.DMA((n_arr, NDEV))],
        compiler_params=pltpu.CompilerParams(vmem_limit_bytes=VMEM_LIMIT),
    )(w_in, w_out, pw_in, pgrp, pw_out, lb_l, pscale, c, c_ctx, ada_w)


def _gather_order(s):
    if isinstance(s, int):
        return (0, 1, 2, 4, 3, 5, 6, 7)[s]
    return s + (s == 3).astype(jnp.int32) - (s == 4).astype(jnp.int32)


GATHER_ISSUE = (1, 2, 4, 3, 5, 6, 7)
GATHER_ICI = (2, 4, 6)
GATHER_DIRECT = (1,) + GATHER_ICI
GATHER_FORWARD_AT = 5
RS_SLOTS = 5


def _weight_slices(refs, i):
    wout, pwin, pgrp, pwout = refs
    return [wout.at[pl.ds(pl.multiple_of(i * SH_ROWS, SH_ROWS), SH_ROWS), :],
            pwin.at[:, pl.ds(pl.multiple_of(i * SH_PWIN, 128), SH_PWIN)],
            pgrp.at[:, pl.ds(pl.multiple_of(i * SH_GRP, SH_GRP), SH_GRP), :],
            pwout.at[pl.ds(pl.multiple_of(i * SH_ROWS, SH_ROWS), SH_ROWS), :]]


def _modulated(x, nw, shift, scale):
    r = _rstd(x)
    xn = x * r
    a = xn * nw
    return a * (1.0 + scale) + shift, r, xn, a


def _ctx_or_x(i, ctx_ref, x_ref):
    return jnp.where(i == 0, ctx_ref[...], x_ref[...])


def _f1_gather_matmul(idx1, ctx, x, nw, msel, s_win):
    def body(idx_ref, ctx_ref, x_ref, nw_ref, m_ref, sw_ref, g_ref, win_o, wslot, hx_sc, ssem, rsem, lsem, osem):
        del idx_ref
        s, i = pl.program_id(0), pl.program_id(1)
        x, y, cc, idx = _mesh_pos()
        k = _gather_order(s)
        j = idx ^ k

        def remote(kk):
            return pltpu.make_async_remote_copy(src_ref=sw_ref, dst_ref=wslot.at[idx], send_sem=ssem.at[kk], recv_sem=rsem.at[kk],
                                                device_id=_peer(x, y, cc, kk), device_id_type=MESH)

        def forward(kk):
            jj = idx ^ kk
            return pltpu.make_async_remote_copy(src_ref=wslot.at[jj], dst_ref=wslot.at[jj], send_sem=ssem.at[kk ^ 1],
                                                recv_sem=rsem.at[kk ^ 1], device_id=(x, y, 1 - cc), device_id_type=MESH)

        own = pltpu.make_async_copy(sw_ref, wslot.at[idx], lsem)

        def to_hbm(jj, kk):
            return pltpu.make_async_copy(wslot.at[jj], win_o.at[:, pl.ds(pl.multiple_of(jj * SH_WIN, 128), SH_WIN)], osem.at[kk])

        @pl.when((s == 0) & (i == 0))
        def _():
            own.start()
            for kk in GATHER_DIRECT:
                remote(kk).start()
            own.wait()

        @pl.when(s == 0)
        def _():
            hx, _, _, _ = _modulated(_ctx_or_x(i, ctx_ref, x_ref), nw_ref[...], m_ref[0, 0:1, :], m_ref[0, 1:2, :])
            hx_sc[i] = hx.astype(BF16)

        @pl.when((s > 0) & (i == 0))
        def _():
            remote(k).wait_recv()

            @pl.when((k & 1) == 0)
            def _():
                forward(k).start()

        @pl.when(i == 0)
        def _():
            to_hbm(j, k).start()

        g_ref[...] = jnp.dot(hx_sc[i], wslot[j], preferred_element_type=F32)

        @pl.when((s == NDEV - 1) & (i == NT - 1))
        def _():
            for kk in GATHER_DIRECT:
                remote(kk).wait_send()
            for kk in GATHER_ICI:
                forward(kk).wait_send()
            for kk in range(NDEV):
                to_hbm(idx ^ kk, kk).wait()

    grid_spec = pltpu.PrefetchScalarGridSpec(
        num_scalar_prefetch=1, grid=(NDEV, NT),
        in_specs=[VMEM_SPEC, pl.BlockSpec((TM, D), lambda s, i, ix: (jnp.maximum(i - 1, 0), 0)), VMEM_SPEC,
                  pl.BlockSpec((1, 2, D), lambda s, i, ix: (jnp.minimum(i, 1), 0, 0)), HBM_SPEC],
        out_specs=[pl.BlockSpec((TM, SH_WIN), lambda s, i, ix: (i, ix[0] ^ _gather_order(s))), HBM_SPEC],
        scratch_shapes=[pltpu.VMEM((NDEV, D, SH_WIN), BF16), pltpu.VMEM((NT, TM, D), BF16),
                        pltpu.SemaphoreType.DMA((NDEV,)), pltpu.SemaphoreType.DMA((NDEV,)), pltpu.SemaphoreType.DMA,
                        pltpu.SemaphoreType.DMA((NDEV,))])
    return pl.pallas_call(
        body, name="f1_gather_matmul", grid_spec=grid_spec,
        out_shape=(_sds((TT, WIN_COLS), F32), _sds((D, WIN_COLS), BF16)),
        compiler_params=pltpu.CompilerParams(dimension_semantics=("arbitrary", "arbitrary"), vmem_limit_bytes=VMEM_LIMIT),
    )(idx1, ctx, x, nw, msel, s_win)


def _gla_gates(pre, qpre, lbd, cum, rev):
    rows, n = pre.shape
    nch = rows // CHUNK
    sig = _sigmoid(pre)
    f = lbd + (1.0 - lbd) * sig
    k = 1.0 - f
    g = _dot01(cum, jnp.log(f))
    g3 = g.reshape(nch, CHUNK, n)
    last = 0 if rev else CHUNK - 1
    mid = CHUNK // 2 if rev else CHUNK // 2 - 1
    gl1, gm1 = g3[:, last:last + 1, :], g3[:, mid:mid + 1, :]

    def bc(a):
        return jnp.broadcast_to(a, g3.shape).reshape(rows, n)

    gm = bc(gm1)
    e_q, e_k = jnp.exp(g - gm), jnp.exp(gm - g)
    e_in, e_end = e_q * bc(jnp.exp(gm1)), e_k * bc(jnp.exp(gl1 - gm1))
    qsig = _sigmoid(qpre)
    qs = qpre * qsig * (DH ** -0.5)
    return dict(sig=sig, f=f, k=k, qsig=qsig, qs=qs, e_q=e_q, e_k=e_k, e_in=e_in, e_end=e_end,
                decay=[jnp.exp(g3[ci, last:last + 1, :]) for ci in range(nch)])


def _put_heads(ref, lead, arr):
    for h in range(HEADS):
        ref[lead + (h,)] = arr[:, h * DH:(h + 1) * DH]


def _get_heads(ref, lead=()):
    return jnp.concatenate([ref[lead + (h,)] for h in range(HEADS)], axis=1)


def _gla_prep(g_all, lb, cum01):
    def body(g_ref, lb_ref, cum_ref, p0_ref, p1_ref, v_ref, dec_ref):
        qpre = g_ref[:, 3 * E:4 * E]
        _put_heads(v_ref, (), g_ref[:, 2 * E:3 * E].astype(BF16))
        dec_ref[...] = jnp.zeros_like(dec_ref)
        for d, p_ref in ((0, p0_ref), (1, p1_ref)):
            t = _gla_gates(g_ref[:, d * E:(d + 1) * E], qpre, lb_ref[d:d + 1, :], cum_ref[d], d == 1)
            _put_heads(p_ref, (0,), (t["qs"] * t["e_q"]).astype(BF16))
            _put_heads(p_ref, (1,), (t["k"] * t["e_k"]).astype(BF16))
            _put_heads(p_ref, (2,), (t["qs"] * t["e_in"]).astype(BF16))
            _put_heads(p_ref, (3,), (t["k"] * t["e_end"]).astype(BF16))
            for ci in range(TM // CHUNK):
                dec_ref[d, 0, ci:ci + 1, :] = t["decay"][ci]

    quad = pl.BlockSpec((4, HEADS, TM, DH), lambda i: (0, 0, i, 0))
    return pl.pallas_call(
        body, name="gla_prep", grid=(NT,),
        in_specs=[pl.BlockSpec((TM, WIN_COLS), lambda i: (i, 0)), VMEM_SPEC, VMEM_SPEC],
        out_specs=[quad, quad, pl.BlockSpec((HEADS, TM, DH), lambda i: (0, i, 0)), pl.BlockSpec((2, 1, 8, E), lambda i: (0, i, 0, 0))],
        out_shape=(_sds((4, HEADS, TT, DH), BF16), _sds((4, HEADS, TT, DH), BF16), _sds((HEADS, TT, DH), BF16), _sds((2, NT, 8, E), F32)),
        compiler_params=pltpu.CompilerParams(dimension_semantics=("arbitrary",), vmem_limit_bytes=VMEM_LIMIT),
    )(g_all, lb, cum01)


def _scan_tile(i, rev):
    t = jnp.where(i == 0, 0, NT - i) if rev else i
    return t, pl.ds(pl.multiple_of(t * TM, TM), TM)


def _chunk_order(rev):
    n = TM // CHUNK
    return tuple(range(n - 1, -1, -1)) if rev else tuple(range(n))


def _gla_fwd(p0, p1, v_all, dec, mask01, s_wout, s_pwin, s_pgrp, s_pwout):
    def body(p0_ref, p1_ref, v_ref, dec_ref, msk_ref, swout_r, spwin_r, spgrp_r, spwout_r,
             o_ref, wout_o, pwin_o, pgrp_o, pwout_o, ob_sc, ssem, rsem, lsem):
        h = pl.program_id(0)
        x, y, cc, idx = _mesh_pos()
        srcs = [swout_r, spwin_r, spgrp_r, spwout_r]
        gathered = (wout_o, pwin_o, pgrp_o, pwout_o)
        mine = _weight_slices(gathered, idx)

        def remote(a, k):
            return pltpu.make_async_remote_copy(src_ref=srcs[a], dst_ref=mine[a], send_sem=ssem.at[a, k], recv_sem=rsem.at[a, k],
                                                device_id=_peer(x, y, cc, k), device_id_type=MESH)

        def forward(a, k):
            blk = _weight_slices(gathered, idx ^ k)[a]
            return pltpu.make_async_remote_copy(src_ref=blk, dst_ref=blk, send_sem=ssem.at[a, k ^ 1], recv_sem=rsem.at[a, k ^ 1],
                                                device_id=(x, y, 1 - cc), device_id_type=MESH)

        copies = [remote(a, k) for k in GATHER_DIRECT for a in range(4)]
        passed = [forward(a, k) for k in GATHER_ICI for a in range(4)]
        local = [pltpu.make_async_copy(srcs[a], mine[a], lsem.at[a]) for a in range(4)]

        @pl.when(h == 0)
        def _():
            for cp in copies + local:
                cp.start()

        @pl.when(h == GATHER_FORWARD_AT)
        def _():
            for k in GATHER_ICI:
                for a in range(4):
                    remote(a, k).wait_recv()
                    forward(a, k).start()

        def tile_body(i, sts):
            new = []
            for d, p_ref in ((0, p0_ref), (1, p1_ref)):
                rev = d == 1
                st = sts[d]
                t, rows = _scan_tile(i, rev)
                v = v_ref[0, rows, :]
                a = _dot_tb(p_ref[0, 0, rows, :], p_ref[1, 0, rows, :]) * msk_ref[d]
                intra = _dot(a, v)
                q_in, kend = p_ref[2, 0, rows, :], p_ref[3, 0, rows, :]
                outs = [None] * (TM // CHUNK)
                for ci in _chunk_order(rev):
                    r = slice(ci * CHUNK, (ci + 1) * CHUNK)
                    outs[ci] = _dot_tb(q_in[r], st) + intra[r]
                    st = st * dec_ref[d, t, ci:ci + 1, :] + _dot_ta(v[r], kend[r])
                o_t = jnp.concatenate(outs, axis=0)
                if rev:
                    ob_sc[rows, :] = o_t
                else:
                    o_ref[0, rows, :] = o_t
                new.append(st)
            return tuple(new)

        zero = jnp.zeros((DH, DH), F32)
        lax.fori_loop(0, NT, tile_body, (zero, zero))
        o_ref[0] += ob_sc[...]

        @pl.when(h == HEADS - 1)
        def _():
            for cp in copies + passed:
                cp.wait_send()
            for a in range(4):
                remote(a, 1).wait_recv()
            for cp in passed:
                cp.wait_recv()
            for cp in local:
                cp.wait()

    quad = pl.BlockSpec((4, 1, TT, DH), lambda h: (0, h, 0, 0))
    head = pl.BlockSpec((1, TT, DH), lambda h: (h, 0, 0))
    return pl.pallas_call(
        body, name="gla_fwd", grid=(HEADS,),
        in_specs=[quad, quad, head, pl.BlockSpec((2, NT, 8, DH), lambda h: (0, 0, 0, h)),
                  pl.BlockSpec((2, TM, TM), lambda h: (0, 0, 0))] + [HBM_SPEC] * 4,
        out_specs=[head] + [HBM_SPEC] * 4,
        out_shape=(_sds((HEADS, TT, DH), F32), _sds((E, D), BF16), _sds((D, 2 * E), BF16), _sds((4, PG, PG), BF16), _sds((E, D), BF16)),
        scratch_shapes=[pltpu.VMEM((TT, DH), F32), pltpu.SemaphoreType.DMA((4, NDEV)), pltpu.SemaphoreType.DMA((4, NDEV)),
                        pltpu.SemaphoreType.DMA((4,))],
        compiler_params=pltpu.CompilerParams(dimension_semantics=("arbitrary",), vmem_limit_bytes=VMEM_LIMIT),
    )(p0, p1, v_all, dec, mask01, s_wout, s_pwin, s_pgrp, s_pwout)


def _gated_norm(o, z, gw):
    r = _head_map(lambda oh: jnp.broadcast_to(_rstd(oh), oh.shape), o)
    on = o * r
    zs = _sigmoid(z)
    sz = z * zs
    return on * gw * sz, r, on, zs, sz


def _f3_out(o, g_all, x, gate, gw, wout):
    def body(o_ref, z_ref, x_ref, gate_ref, gw_ref, w_ref, x1_ref):
        og, _, _, _, _ = _gated_norm(_get_heads(o_ref), z_ref[...], gw_ref[...])
        x1_ref[...] = x_ref[...] + gate_ref[...] * _dot(og, w_ref[...])

    return pl.pallas_call(
        body, name="f3_out", grid=(NTX,),
        in_specs=[pl.BlockSpec((HEADS, TM, DH), lambda i: (0, i + 1, 0)), pl.BlockSpec((TM, E), lambda i: (i + 1, 4)),
                  pl.BlockSpec((TM, D), lambda i: (i, 0)), pl.BlockSpec((1, D), lambda i: (0, 0)),
                  pl.BlockSpec((1, E), lambda i: (0, 0)), pl.BlockSpec((E, D), lambda i: (0, 0))],
        out_specs=pl.BlockSpec((TM, D), lambda i: (i, 0)),
        out_shape=_sds((T, D), F32),
        compiler_params=pltpu.CompilerParams(dimension_semantics=("arbitrary",)),
    )(o, g_all, x, gate, gw, wout)


def _pool_layer(x1, tgt, mod1, nw1, fnw, pwin, pgrp, pscale, pwout, pb, pbt, pinv):
    def body(x_ref, t_ref, m_ref, nw_ref, fw_ref, pwin_ref, pgrp_ref, ps_ref, pwout_ref, pb_ref, pbt_ref, pinv_ref,
             dx_ref, gpwin_o, gpgrp_o, gpwout_o, dmod_o, gnw_o, gfw_o, gps_o, loss_o,
             a_pwin, a_pgrp, a_pwout):
        i = pl.program_id(0)

        @pl.when(i == 0)
        def _():
            for ref in (a_pwin, a_pgrp, a_pwout, dmod_o, gnw_o, gfw_o, gps_o, loss_o):
                ref[...] = jnp.zeros_like(ref)

        shift, scale, gate = m_ref[0:1, :], m_ref[1:2, :], m_ref[2:3, :]
        nw, fw, ps = nw_ref[...], fw_ref[...], ps_ref[...]
        x1 = x_ref[...]
        hx, r1, xn, a = _modulated(x1, nw, shift, scale)
        hxb = hx.astype(BF16)
        uz = jnp.dot(hxb, pwin_ref[...], preferred_element_type=F32)
        u, z = uz[:, :E], uz[:, E:]
        pooled, ys = [], []
        for g in range(4):
            ug = u[:, g * PG:(g + 1) * PG]
            pg = _dot01(pb_ref[g], ug) * pinv_ref[g] - ug
            pooled.append(pg.astype(BF16))
            ys.append(_dot(pooled[g], pgrp_ref[g]))
        ycat = jnp.concatenate(ys, axis=1)
        y = ycat * ps
        zs = _sigmoid(z)
        sz = z * zs
        p = (y * sz).astype(BF16)
        out = _dot(p, pwout_ref[...])
        x2 = x1 + gate * out
        r2 = _rstd(x2)
        xn2 = x2 * r2
        diff = xn2 * fw - t_ref[...]
        loss_o[...] += _colsum(diff * diff)
        dyf = diff * (1.0 / D)
        gfw_o[...] += _colsum(dyf * xn2)
        dxn2 = dyf * fw
        dx2 = r2 * (dxn2 - xn2 * jnp.mean(dxn2 * xn2, axis=-1, keepdims=True))
        dgate = _colsum(dx2 * out)
        dout = (dx2 * gate).astype(BF16)
        for j in range(4):
            cs = slice(j * PG, (j + 1) * PG)
            a_pwout[:, cs] += _dot_ta(p, dout[:, cs])
        dp = _dot_tb(dout, pwout_ref[...])
        dy = dp * sz
        dz = dp * y * (zs * (1.0 + z * (1.0 - zs)))
        gps_o[...] += _colsum(dy * ycat)
        dycat = dy * ps
        dus = []
        for g in range(4):
            dyg = dycat[:, g * PG:(g + 1) * PG].astype(BF16)
            a_pgrp[g] += _dot_ta(pooled[g], dyg)
            dpg = _dot_tb(dyg, pgrp_ref[g])
            dus.append(_dot01(pbt_ref[g], dpg * pinv_ref[g]) - dpg)
        duz = jnp.concatenate(dus + [dz], axis=1).astype(BF16)
        for j in range(2 * E // PG):
            cs = slice(j * PG, (j + 1) * PG)
            a_pwin[:, cs] += _dot_ta(hxb, duz[:, cs])
        dhx = _dot_tb(duz, pwin_ref[...])
        dmod_o[0:1, :] += _colsum(dhx)
        dmod_o[1:2, :] += _colsum(dhx * a)
        dmod_o[2:3, :] += dgate
        da = dhx * (1.0 + scale)
        gnw_o[...] += _colsum(da * xn)
        dxn = da * nw
        dx_ref[...] = dx2 + r1 * (dxn - xn * jnp.mean(dxn * xn, axis=-1, keepdims=True))

        @pl.when(i == NTX - 1)
        def _():
            gpwin_o[...] = a_pwin[...].astype(BF16)
            gpgrp_o[...] = a_pgrp[...].astype(BF16)
            gpwout_o[...] = a_pwout[...].astype(BF16)

    tile = pl.BlockSpec((TM, D), lambda i: (i, 0))
    outs = (_sds((T, D), F32), _sds((D, 2 * E), BF16), _sds((4, PG, PG), BF16), _sds((E, D), BF16),
            _sds((3, D), F32), _sds((1, D), F32), _sds((1, D), F32), _sds((1, E), F32), _sds((1, D), F32))
    return pl.pallas_call(
        body, name="pool_layer", grid=(NTX,),
        in_specs=[tile, tile] + [VMEM_SPEC] * 10,
        out_specs=[tile] + [VMEM_SPEC] * 8,
        out_shape=outs,
        scratch_shapes=[pltpu.VMEM((D, 2 * E), F32), pltpu.VMEM((4, PG, PG), F32), pltpu.VMEM((E, D), F32)],
        compiler_params=pltpu.CompilerParams(dimension_semantics=("arbitrary",), vmem_limit_bytes=VMEM_LIMIT),
    )(x1, tgt, mod1, nw1, fnw, pwin, pgrp, pscale, pwout, pb, pbt, pinv)


def _b3_out_bwd(dx1, o, g_all, gate, gw, wout):
    def body(dx_ref, o_ref, z_ref, gate_ref, gw_ref, w_ref, do_ref, dz_ref, gw_o, dgate_o, ggw_o, acc):
        i = pl.program_id(0)

        @pl.when(i == 0)
        def _():
            acc[...] = jnp.zeros_like(acc)
            dgate_o[...] = jnp.zeros_like(dgate_o)
            ggw_o[...] = jnp.zeros_like(ggw_o)
            do_ref[...] = jnp.zeros_like(do_ref)
            dz_ref[...] = jnp.zeros_like(dz_ref)

        @pl.when(i > 0)
        def _():
            gw = gw_ref[...]
            z = z_ref[...]
            og, r, on, zs, sz = _gated_norm(_get_heads(o_ref), z, gw)
            ogb = og.astype(BF16)
            dx = dx_ref[...]
            dgate_o[...] += _colsum(dx * _dot(ogb, w_ref[...]))
            dy = (dx * gate_ref[...]).astype(BF16)
            for j in range(4):
                cs = slice(j * PG, (j + 1) * PG)
                acc[:, cs] += _dot_ta(ogb, dy[:, cs])
            dog = _dot_tb(dy, w_ref[...])
            dz_ref[...] = (dog * (on * gw) * (zs * (1.0 + z * (1.0 - zs)))).astype(BF16)
            dong = dog * sz
            ggw_o[...] += _colsum(dong * on)
            don = dong * gw
            do = _head_map(lambda dh, nh, rh: rh * (dh - nh * jnp.mean(dh * nh, axis=-1, keepdims=True)), don, on, r)
            _put_heads(do_ref, (), do.astype(BF16))

        @pl.when(i == NT - 1)
        def _():
            gw_o[...] = acc[...].astype(BF16)

    prev = lambda i: (jnp.maximum(i - 1, 0), 0)
    heads = pl.BlockSpec((HEADS, TM, DH), lambda i: (0, i, 0))
    return pl.pallas_call(
        body, name="b3_out_bwd", grid=(NT,),
        in_specs=[pl.BlockSpec((TM, D), prev), heads, pl.BlockSpec((TM, E), lambda i: (i, 4)),
                  VMEM_SPEC, VMEM_SPEC, VMEM_SPEC],
        out_specs=[heads, pl.BlockSpec((TM, E), lambda i: (i, 0)), VMEM_SPEC, VMEM_SPEC, VMEM_SPEC],
        out_shape=(_sds((HEADS, TT, DH), BF16), _sds((TT, E), BF16), _sds((E, D), BF16), _sds((1, D), F32), _sds((1, E), F32)),
        scratch_shapes=[pltpu.VMEM((E, D), F32)],
        compiler_params=pltpu.CompilerParams(dimension_semantics=("arbitrary",), vmem_limit_bytes=VMEM_LIMIT),
    )(dx1, o, g_all, gate, gw, wout)


def _gla_bwd(p0, p1, v_all, dec, do, mask01, gwout, gpwin, gpgrp, gpwout):
    nch = TM // CHUNK

    def body(p0_ref, p1_ref, v_ref, dec_ref, do_ref, msk_ref, gwout_r, gpwin_r, gpgrp_r, gpwout_r,
             d0_ref, d1_ref, dv_ref, dgl_ref, rwout_o, rpwin_o, rpgrp_o, rpwout_o,
             ss_sc, dv_sc, ssem, rsem, lsem):
        h = pl.program_id(0)
        x, y, cc, idx = _mesh_pos()
        grads = (gwout_r, gpwin_r, gpgrp_r, gpwout_r)
        dsts = [rwout_o.at[idx], rpwin_o.at[idx], rpgrp_o.at[idx], rpwout_o.at[idx]]

        def remote(a, k):
            px, py, pc = _peer(x, y, cc, k)
            return pltpu.make_async_remote_copy(src_ref=_weight_slices(grads, 4 * px + 2 * py + pc)[a], dst_ref=dsts[a],
                                                send_sem=ssem.at[a, k], recv_sem=rsem.at[a, k], device_id=(px, py, pc), device_id_type=MESH)

        copies = [remote(a, k) for k in GATHER_ISSUE for a in range(4)]
        local = [pltpu.make_async_copy(_weight_slices(grads, idx)[a], dsts[a], lsem.at[a]) for a in range(4)]

        @pl.when(h == 0)
        def _():
            for cp in copies + local:
                cp.start()

        zero = jnp.zeros((DH, DH), F32)
        dgl_ref[...] = jnp.zeros_like(dgl_ref)

        def fwd_body(i, sts):
            new = []
            for d, p_ref in ((0, p0_ref), (1, p1_ref)):
                rev = d == 1
                st = sts[d]
                t, rows = _scan_tile(i, rev)
                v, kend = v_ref[0, rows, :], p_ref[3, 0, rows, :]
                for n, ci in enumerate(_chunk_order(rev)):
                    r = slice(ci * CHUNK, (ci + 1) * CHUNK)
                    ss_sc[d, i * nch + n] = st
                    st = st * dec_ref[d, t, ci:ci + 1, :] + _dot_ta(v[r], kend[r])
                new.append(st)
            return tuple(new)

        lax.fori_loop(0, NT, fwd_body, (zero, zero))

        def bwd_body(ii, dsts_):
            i = NT - 1 - ii
            new = []
            for d, p_ref, d_ref in ((0, p0_ref, d0_ref), (1, p1_ref, d1_ref)):
                rev = d == 1
                order = _chunk_order(rev)
                dst = dsts_[d]
                t, rows = _scan_tile(i, rev)
                qg, kg, q_in, kend = p_ref[0, 0, rows, :], p_ref[1, 0, rows, :], p_ref[2, 0, rows, :], p_ref[3, 0, rows, :]
                v, dob, msk = v_ref[0, rows, :], do_ref[0, rows, :], msk_ref[d]
                a = (_dot_tb(qg, kg) * msk).astype(BF16)
                da = (_dot_tb(dob, v) * msk).astype(BF16)
                d_ref[0, 0, rows, :] = _dot(da, kg).astype(BF16)
                d_ref[1, 0, rows, :] = _dot_ta(da, qg).astype(BF16)
                dv_intra = _dot_ta(a, dob)
                dv_l, dkend_l, dqin_l = [None] * nch, [None] * nch, [None] * nch
                for n in range(nch - 1, -1, -1):
                    ci = order[n]
                    r = slice(ci * CHUNK, (ci + 1) * CHUNK)
                    s_c = ss_sc[d, i * nch + n]
                    dec = dec_ref[d, t, ci:ci + 1, :]
                    dstb = dst.astype(BF16)
                    dv_l[ci] = dv_intra[r] + _dot_tb(kend[r], dstb)
                    dkend_l[ci] = _dot(v[r], dstb)
                    dqin_l[ci] = _dot(dob[r], s_c)
                    dgl_ref[d, t, ci:ci + 1, :] = jnp.sum(s_c * dst, axis=0, keepdims=True) * dec
                    dst = dst * dec + _dot_ta(dob[r], q_in[r])
                d_ref[2, 0, rows, :] = jnp.concatenate(dqin_l, axis=0).astype(BF16)
                d_ref[3, 0, rows, :] = jnp.concatenate(dkend_l, axis=0).astype(BF16)
                dv_sc[d, rows, :] = jnp.concatenate(dv_l, axis=0)
                new.append(dst)
            return tuple(new)

        lax.fori_loop(0, NT, bwd_body, (zero, zero))
        dv_ref[0] = (dv_sc[0] + dv_sc[1]).astype(BF16)

        @pl.when(h == HEADS - 1)
        def _():
            for cp in copies:
                cp.wait_send()
            for cp in copies:
                cp.wait_recv()
            for cp in local:
                cp.wait()

    quad = pl.BlockSpec((4, 1, TT, DH), lambda h: (0, h, 0, 0))
    col = pl.BlockSpec((1, TT, DH), lambda h: (h, 0, 0))
    chunkv = pl.BlockSpec((2, NT, 8, DH), lambda h: (0, 0, 0, h))
    outs = (_sds((4, HEADS, TT, DH), BF16), _sds((4, HEADS, TT, DH), BF16), _sds((HEADS, TT, DH), BF16), _sds((2, NT, 8, E), F32),
            _sds((NDEV, SH_ROWS, D), BF16), _sds((NDEV, D, SH_PWIN), BF16), _sds((NDEV, 4, SH_GRP, PG), BF16), _sds((NDEV, SH_ROWS, D), BF16))
    return pl.pallas_call(
        body, name="gla_bwd", grid=(HEADS,),
        in_specs=[quad, quad, col, chunkv, col, pl.BlockSpec((2, TM, TM), lambda h: (0, 0, 0))] + [HBM_SPEC] * 4,
        out_specs=[quad, quad, col, chunkv] + [HBM_SPEC] * 4,
        out_shape=outs,
        scratch_shapes=[pltpu.VMEM((2, NT * nch, DH, DH), F32), pltpu.VMEM((2, TT, DH), F32),
                        pltpu.SemaphoreType.DMA((4, NDEV)), pltpu.SemaphoreType.DMA((4, NDEV)), pltpu.SemaphoreType.DMA((4,))],
        compiler_params=pltpu.CompilerParams(dimension_semantics=("arbitrary",), vmem_limit_bytes=VMEM_LIMIT),
    )(p0, p1, v_all, dec, do, mask01, gwout, gpwin, gpgrp, gpwout)


TMB = 128


def _gla_post_bwd(g_all, d0, d1, dgl, dv, dz, lb, cum01):
    nch = TMB // CHUNK

    def body(g_ref, d0_ref, d1_ref, dgl_ref, dv_ref, dz_ref, lb_ref, cum_ref, dg_ref, dlb_ref):
        i = pl.program_id(0)

        @pl.when(i == 0)
        def _():
            dlb_ref[...] = jnp.zeros_like(dlb_ref)

        half = i & 1
        qpre = g_ref[:, 3 * E:4 * E]
        dqs_sum = None
        dpre = []
        for d, d_ref in ((0, d0_ref), (1, d1_ref)):
            rev = d == 1
            lbd = lb_ref[d:d + 1, :]
            t = _gla_gates(g_ref[:, d * E:(d + 1) * E], qpre, lbd, cum_ref[d, :TMB, :TMB], rev)
            dqg, dkg, dqin, dkend = [_get_heads(d_ref, (ty,)).astype(F32) for ty in range(4)]
            dqs = dqg * t["e_q"] + dqin * t["e_in"]
            dk = dkg * t["e_k"] + dkend * t["e_end"]
            dkk = dkend * (t["k"] * t["e_end"])
            dg = t["qs"] * dqs - t["k"] * dk
            dkk3 = dkk.reshape(nch, CHUNK, E)
            dgl8 = dgl_ref[d, 0]
            dgl_rows = [jnp.where(half == 0, dgl8[ci:ci + 1, :], dgl8[nch + ci:nch + ci + 1, :]) for ci in range(nch)]
            dgl_b = jnp.concatenate([jnp.broadcast_to(dgl_rows[ci] + jnp.sum(dkk3[ci], axis=0, keepdims=True), (CHUNK, E))
                                     for ci in range(nch)], axis=0)
            pos = lax.broadcasted_iota(jnp.int32, (TMB, E), 0) & (CHUNK - 1)
            dg = dg + jnp.where(pos == (0 if rev else CHUNK - 1), dgl_b, 0.0)
            dlf = _dot01(cum_ref[1 - d, :TMB, :TMB], dg)
            df = dlf / t["f"] - dk
            sig = t["sig"]
            dpre.append((df * (1.0 - lbd) * sig * (1.0 - sig)).astype(BF16))
            dlb_ref[d:d + 1, :] += _colsum(df * (1.0 - sig))
            dqs_sum = dqs if dqs_sum is None else dqs_sum + dqs
            qsig = t["qsig"]
        dqpre = dqs_sum * (DH ** -0.5) * (qsig * (1.0 + qpre * (1.0 - qsig)))
        dg_ref[...] = jnp.concatenate([dpre[0], dpre[1], _get_heads(dv_ref), dqpre.astype(BF16), dz_ref[...]], axis=1)

    quad = pl.BlockSpec((4, HEADS, TMB, DH), lambda i: (0, 0, i, 0))
    tile = pl.BlockSpec((TMB, E), lambda i: (i, 0))
    return pl.pallas_call(
        body, name="gla_post_bwd", grid=(TT // TMB,),
        in_specs=[pl.BlockSpec((TMB, WIN_COLS), lambda i: (i, 0)), quad, quad,
                  pl.BlockSpec((2, 1, 8, E), lambda i: (0, i // 2, 0, 0)), pl.BlockSpec((HEADS, TMB, DH), lambda i: (0, i, 0)), tile,
                  VMEM_SPEC, VMEM_SPEC],
        out_specs=[pl.BlockSpec((TMB, WIN_COLS), lambda i: (i, 0)), VMEM_SPEC],
        out_shape=(_sds((TT, WIN_COLS), BF16), _sds((2, E), F32)),
        compiler_params=pltpu.CompilerParams(dimension_semantics=("arbitrary",), vmem_limit_bytes=VMEM_LIMIT),
    )(g_all, d0, d1, dgl, dv, dz, lb, cum01)


def _b1_in_bwd(idx1, ctx, x, dx1, dg, nw, msel, win):
    last_s = NDEV - 1

    def body(idx_ref, ctx_ref, x_ref, dx1_ref, dg_ref, nw_ref, m_ref, w_ref, gx_ref, rwin_o, dmx_o, dmc_o, gnw_o,
             hx_sc, dhx_sc, acc, sbuf, pbuf, psend, precv, isend, irecv, sibsem, lsem):
        del idx_ref
        s, i = pl.program_id(0), pl.program_id(1)
        x, y, cc, idx = _mesh_pos()
        shift, scale = m_ref[0, 0:1, :], m_ref[0, 1:2, :]
        sibling = (x, y, 1 - cc)

        def partial(p):
            return pltpu.make_async_remote_copy(src_ref=sbuf.at[0], dst_ref=pbuf.at[p], send_sem=psend.at[p], recv_sem=precv.at[p],
                                                device_id=sibling, device_id_type=MESH)

        def chip_sum(p):
            return pltpu.make_async_remote_copy(src_ref=sbuf.at[1], dst_ref=rwin_o.at[2 + p], send_sem=isend.at[p], recv_sem=irecv.at[p],
                                                device_id=_peer(x, y, cc, 2 * (p + 1)), device_id_type=MESH)

        to_sibling = pltpu.make_async_remote_copy(src_ref=sbuf.at[0], dst_ref=rwin_o.at[1], send_sem=sibsem.at[0], recv_sem=sibsem.at[1],
                                                  device_id=sibling, device_id_type=MESH)
        own = pltpu.make_async_copy(sbuf.at[1], rwin_o.at[0], lsem)

        @pl.when((s == 0) & (i == 0))
        def _():
            for ref in (dmx_o, dmc_o, gnw_o):
                ref[...] = jnp.zeros_like(ref)

        @pl.when(s == 0)
        def _():
            hx, _, _, _ = _modulated(_ctx_or_x(i, ctx_ref, x_ref), nw_ref[...], shift, scale)
            hx_sc[i] = hx.astype(BF16)

        @pl.when(i == 0)
        def _():
            acc[...] = jnp.zeros_like(acc)

        dgb = dg_ref[...]
        hxb = hx_sc[i]
        for lo, hi in ((0, 256), (256, 512), (512, SH_WIN)):
            acc[:, lo:hi] += _dot_ta(hxb, dgb[:, lo:hi])
        part = _dot_tb(dgb, w_ref[...])

        @pl.when(s == 0)
        def _():
            dhx_sc[i] = part

        @pl.when(s > 0)
        def _():
            dhx_sc[i] += part

        for p in (2, 1, 0):
            @pl.when((i == NT - 1) & (s == 2 * (2 - p)))
            def _(p=p):
                if p < 2:
                    partial(p + 1).wait_send()
                sbuf[0] = acc[...].astype(BF16)
                partial(p).start()

            @pl.when((i == NT - 1) & (s == 2 * (2 - p) + 1))
            def _(p=p):
                if p < 2:
                    chip_sum(p + 1).wait_send()
                partial(p).wait_recv()
                sbuf[1] = (acc[...] + pbuf[p].astype(F32)).astype(BF16)
                chip_sum(p).start()

        @pl.when((i == NT - 1) & (s == last_s - 1))
        def _():
            partial(0).wait_send()
            sbuf[0] = acc[...].astype(BF16)
            to_sibling.start()

        @pl.when((i == NT - 1) & (s == last_s))
        def _():
            chip_sum(0).wait_send()
            sbuf[1] = acc[...].astype(BF16)
            own.start()

        @pl.when(s == last_s)
        def _():
            nw = nw_ref[...]
            _, r, xn, a = _modulated(_ctx_or_x(i, ctx_ref, x_ref), nw, shift, scale)
            dhx = dhx_sc[i]
            dsh, dsc = _colsum(dhx), _colsum(dhx * a)
            da = dhx * (1.0 + scale)
            gnw_o[...] += _colsum(da * xn)
            dxn = da * nw
            gx_ref[...] = dx1_ref[...] + r * (dxn - xn * jnp.mean(dxn * xn, axis=-1, keepdims=True))

            @pl.when(i == 0)
            def _():
                dmc_o[0:1, :] += dsh
                dmc_o[1:2, :] += dsc

            @pl.when(i > 0)
            def _():
                dmx_o[0:1, :] += dsh
                dmx_o[1:2, :] += dsc

        @pl.when((i == NT - 1) & (s == last_s))
        def _():
            to_sibling.wait_send()
            to_sibling.wait_recv()
            for p in range(3):
                chip_sum(p).wait_recv()
            own.wait()

    grid_spec = pltpu.PrefetchScalarGridSpec(
        num_scalar_prefetch=1, grid=(NDEV, NT),
        in_specs=[VMEM_SPEC, pl.BlockSpec((TM, D), lambda s, i, ix: (jnp.maximum(i - 1, 0), 0)),
                  pl.BlockSpec((TM, D), lambda s, i, ix: (jnp.maximum(i - 1, 0), 0)),
                  pl.BlockSpec((TM, SH_WIN), lambda s, i, ix: (i, ix[0] ^ (last_s - s))), VMEM_SPEC,
                  pl.BlockSpec((1, 2, D), lambda s, i, ix: (jnp.minimum(i, 1), 0, 0)),
                  pl.BlockSpec((D, SH_WIN), lambda s, i, ix: (0, ix[0] ^ (last_s - s)))],
        out_specs=[pl.BlockSpec((TM, D), lambda s, i, ix: (jnp.where(s == last_s, jnp.maximum(i - 1, 0), 0), 0)),
                   HBM_SPEC, VMEM_SPEC, VMEM_SPEC, VMEM_SPEC],
        scratch_shapes=[pltpu.VMEM((NT, TM, D), BF16), pltpu.VMEM((NT, TM, D), F32), pltpu.VMEM((D, SH_WIN), F32),
                        pltpu.VMEM((2, D, SH_WIN), BF16), pltpu.VMEM((3, D, SH_WIN), BF16),
                        pltpu.SemaphoreType.DMA((3,)), pltpu.SemaphoreType.DMA((3,)), pltpu.SemaphoreType.DMA((3,)),
                        pltpu.SemaphoreType.DMA((3,)), pltpu.SemaphoreType.DMA((2,)), pltpu.SemaphoreType.DMA])
    return pl.pallas_call(
        body, name="b1_in_bwd", grid_spec=grid_spec,
        out_shape=(_sds((T, D), F32), _sds((RS_SLOTS, D, SH_WIN), BF16), _sds((2, D), F32), _sds((2, D), F32), _sds((1, D), F32)),
        compiler_params=pltpu.CompilerParams(dimension_semantics=("arbitrary", "arbitrary"), vmem_limit_bytes=VMEM_LIMIT),
    )(idx1, ctx, x, dx1, dg, nw, msel, win)


def _reduce_small(pd, pv, cg, c_ctx, ada_w0):
    n_arr = 3

    def body(pd_r, pv_r, cg_r, cctx_r, ada_r, gada_o, gadab_o, gcctx_o, pvsum_o, loss_o,
             pd_all, pv_all, dsc_all, dsc_mine, ssem, rsem):
        x, y, cc, idx = _mesh_pos()
        srcs = [pd_r, pv_r, dsc_mine]
        dsts = [pd_all.at[idx], pv_all.at[idx], dsc_all.at[idx]]

        def remote(a, k):
            return pltpu.make_async_remote_copy(src_ref=srcs[a], dst_ref=dsts[a], send_sem=ssem.at[a, k], recv_sem=rsem.at[a, k],
                                                device_id=_peer(x, y, cc, k), device_id_type=MESH)

        first = [remote(a, k) for k in range(1, NDEV) for a in (0, 1)]
        for cp in first:
            cp.start()
        pd_all[idx] = pd_r[...]
        pv_all[idx] = pv_r[...]
        for k in range(1, NDEV):
            remote(0, k).wait_recv()
            remote(1, k).wait_recv()
        mine = [pd_all[s, :, pl.ds(idx, 1), :] for s in range(NDEV)]
        dmc = functools.reduce(lambda u, v: u + v, [m[2] for m in mine])
        rows = _stack_rows([cg_r[i] for i in range(NDEV)] + [cctx_r[...]])
        sc = (rows * _sigmoid(rows)).astype(BF16)
        gada_o[0] = _dot_ta(sc, _stack_rows([m[0] for m in mine] + [dmc]))
        gada_o[1] = _dot_ta(sc, _stack_rows([m[1] for m in mine]))
        dsc_mine[...] = _dot_tb(jnp.broadcast_to(dmc, (8, SH_ADA)), ada_r[...])[0:1, :]
        dsc_all[idx] = dsc_mine[...]
        second = [remote(2, k) for k in range(1, NDEV)]
        for cp in second:
            cp.start()
        tot = [functools.reduce(lambda u, v: u + v, [pd_all[s, l] for s in range(NDEV)]) for l in range(3)]
        gadab_o[0] = tot[0] + tot[2]
        gadab_o[1] = tot[1]
        pvs = functools.reduce(lambda u, v: u + v, [pv_all[s] for s in range(NDEV)])
        pvsum_o[...] = pvs
        loss_o[...] = jnp.broadcast_to(jnp.sum(pvs[:, PV_LOSS:PV_LOSS + D], axis=-1, keepdims=True) * (0.5 / D), (1, 128))
        for k in range(1, NDEV):
            remote(2, k).wait_recv()
        dsc = functools.reduce(lambda u, v: u + v, [dsc_all[s] for s in range(NDEV)])
        cx = cctx_r[...]
        sx = _sigmoid(cx)
        gcctx_o[...] = dsc * (sx * (1.0 + cx * (1.0 - sx)))
        for cp in first + second:
            cp.wait_send()

    outs = (_sds((2, D, SH_ADA), F32), _sds((2, NDEV, SH_ADA), F32), _sds((1, D), F32), _sds((1, PV_LEN), F32), _sds((1, 128), F32))
    return pl.pallas_call(
        body, name="reduce_small", out_shape=outs,
        in_specs=[VMEM_SPEC] * 5, out_specs=[VMEM_SPEC] * 5,
        scratch_shapes=[
            pltpu.VMEM((NDEV, 3, NDEV, SH_ADA), F32), pltpu.VMEM((NDEV, 1, PV_LEN), F32), pltpu.VMEM((NDEV, 1, D), F32),
            pltpu.VMEM((1, D), F32),
            pltpu.SemaphoreType.DMA((n_arr, NDEV)), pltpu.SemaphoreType.DMA((n_arr, NDEV)),
        ],
        compiler_params=pltpu.CompilerParams(vmem_limit_bytes=VMEM_LIMIT),
    )(pd, pv, cg, c_ctx, ada_w0)


PV_NW, PV_GNORM, PV_FINAL, PV_LB, PV_PSCALE, PV_LOSS, PV_LEN = 0, 2 * D, 3 * D, 4 * D, 6 * D, 7 * D, 8 * D


def _adamw(w, g, m, v):
    m = ADAM_B1 * m + (1.0 - ADAM_B1) * g
    v = ADAM_B2 * v + (1.0 - ADAM_B2) * (g * g)
    m_hat = m / (1.0 - ADAM_B1 ** ADAM_STEP)
    v_hat = v / (1.0 - ADAM_B2 ** ADAM_STEP)
    delta = -ADAM_LR * (m_hat / (jnp.sqrt(v_hat) + ADAM_EPS) + ADAM_WD * w)
    return delta, m, v


def _adam_sharded(name, parts, w, m, v, tr):
    rr, cc = w.shape
    n = parts.shape[0]

    def body(p_ref, w_ref, m_ref, v_ref, g_o, d_o, m_o, v_o):
        g = p_ref[0].astype(F32)
        for s in range(1, n):
            g = g + p_ref[s].astype(F32)
        d, mn, vn = _adamw(w_ref[...], g, m_ref[...], v_ref[...])
        g_o[...], d_o[...], m_o[...], v_o[...] = g, d, mn, vn

    blk = pl.BlockSpec((tr, cc), lambda i: (i, 0))
    return pl.pallas_call(
        body, name=name, grid=(rr // tr,),
        in_specs=[pl.BlockSpec((n, tr, cc), lambda i: (0, i, 0)), blk, blk, blk],
        out_specs=[blk] * 4, out_shape=(_sds((rr, cc), F32),) * 4,
        compiler_params=pltpu.CompilerParams(dimension_semantics=("arbitrary",)),
    )(parts, w, m, v)


def _adam_dense(name, g, w, m, v, tr):
    rr, cc = w.shape

    def body(g_ref, w_ref, m_ref, v_ref, d_o, m_o, v_o):
        d, mn, vn = _adamw(w_ref[...], g_ref[...], m_ref[...], v_ref[...])
        d_o[...], m_o[...], v_o[...] = d, mn, vn

    blk = pl.BlockSpec((tr, cc), lambda i: (i, 0))
    return pl.pallas_call(
        body, name=name, grid=(rr // tr,), in_specs=[blk] * 4, out_specs=[blk] * 3, out_shape=(_sds((rr, cc), F32),) * 3,
        compiler_params=pltpu.CompilerParams(dimension_semantics=("arbitrary",)),
    )(g, w, m, v)


def _adam_small(gs, ws, ms, vs, lb_idx, lbv):
    n = len(ws)

    def body(*refs):
        g_r, w_r, m_r, v_r = refs[:n], refs[n:2 * n], refs[2 * n:3 * n], refs[3 * n:4 * n]
        lb_r = refs[4 * n]
        outs = refs[4 * n + 1:]
        for j in range(n):
            g = g_r[j][...]
            if j == lb_idx:
                lbj = lb_r[...]
                g = g * lbj * (1.0 - lbj)
            d, mn, vn = _adamw(w_r[j][...], g, m_r[j][...], v_r[j][...])
            outs[j][...], outs[n + j][...], outs[2 * n + j][...], outs[3 * n + j][...] = g, d, mn, vn

    shapes = tuple(_sds(w.shape, F32) for w in ws)
    return pl.pallas_call(body, name="adam_small", out_shape=shapes * 4)(*gs, *ws, *ms, *vs, lbv)


def kernel(x, c, ctx, c_ctx, ada_w, ada_b, norm_w, hgrn_w_in, hgrn_lb_logits, hgrn_gnorm_w, hgrn_w_out, pool_w_in, pool_w_grp, pool_scale, pool_w_out, final_norm_w, loss_target, m_c_ctx, m_ada_w, m_ada_b, m_norm_w, m_hgrn_w_in, m_hgrn_lb_logits, m_hgrn_gnorm_w, m_hgrn_w_out, m_pool_w_in, m_pool_w_grp, m_pool_scale, m_pool_w_out, m_final_norm_w, v_c_ctx, v_ada_w, v_ada_b, v_norm_w, v_hgrn_w_in, v_hgrn_lb_logits, v_hgrn_gnorm_w, v_hgrn_w_out, v_pool_w_in, v_pool_w_grp, v_pool_scale, v_pool_w_out, v_final_norm_w):
    idx = 4 * lax.axis_index("x") + 2 * lax.axis_index("y") + lax.axis_index("c")
    cctx2 = c_ctx.reshape(1, D)
    cum01, mask01 = _gla_consts()
    pb, pbt, pinv = _pool_consts()

    idx1 = idx.reshape(1).astype(jnp.int32)
    s_win, s_wout, s_pwin, s_pgrp, s_pwout, lbl_g, ps_g, cg, mod_g = _gather_small(
        hgrn_w_in[0], hgrn_w_out[0], pool_w_in[0], pool_w_grp[0], pool_w_out[0], hgrn_lb_logits[0], pool_scale, c, cctx2, ada_w)
    lb = jax.nn.sigmoid(jnp.transpose(lbl_g, (1, 0, 2)).reshape(2, E))
    pscale = ps_g.reshape(1, E)
    mod_all = jnp.transpose(mod_g, (1, 2, 0, 3)).reshape(2, 16, 3 * D) + ada_b[:, None, :]
    mod_me = lax.dynamic_index_in_dim(mod_all, idx, axis=1, keepdims=False)
    mod0, mod1, modc = mod_me[0].reshape(3, D), mod_me[1].reshape(3, D), mod_all[0, NDEV].reshape(3, D)
    msel = jnp.stack([modc[:2], mod0[:2]])
    nw0, nw1 = norm_w[0:1], norm_w[1:2]
    fnw = final_norm_w.reshape(1, D)

    g_all, win = _f1_gather_matmul(idx1, ctx[0], x[0], nw0, msel, s_win)
    p0, p1, v_all, dec = _gla_prep(g_all, lb, cum01)
    o, wout, pwin, pgrp, pwout = _gla_fwd(p0, p1, v_all, dec, mask01, s_wout, s_pwin, s_pgrp, s_pwout)
    x1 = _f3_out(o, g_all, x[0], mod0[2:3], hgrn_gnorm_w, wout)
    dx1, gpwin, gpgrp, gpwout, dmod1, gnw1, gfw, gps, lossv = _pool_layer(
        x1, loss_target[0], mod1, nw1, fnw, pwin, pgrp, pscale, pwout, pb, pbt, pinv)
    do, dz, gwout, dgate0, ggw = _b3_out_bwd(dx1, o, g_all, mod0[2:3], hgrn_gnorm_w, wout)
    d0, d1, dv, dgl, rwout, rpwin, rpgrp, rpwout = _gla_bwd(p0, p1, v_all, dec, do, mask01, gwout, gpwin, gpgrp, gpwout)
    dg, dlb = _gla_post_bwd(g_all, d0, d1, dgl, dv, dz, lb, cum01)
    grad_x, rwin, dmx, dmc, gnw0 = _b1_in_bwd(idx1, ctx[0], x[0], dx1, dg, nw0, msel, win)

    dmod0 = jnp.concatenate([dmx, dgate0], axis=0)
    dmodc = jnp.concatenate([dmc, jnp.zeros((1, D), F32)], axis=0)
    pd = jnp.stack([dmod0, dmod1, dmodc]).reshape(3, NDEV, SH_ADA)
    pv = jnp.concatenate([gnw0, gnw1, ggw, gfw, dlb.reshape(1, 2 * E), gps, lossv], axis=1)
    g_ada, g_adab, g_cctx, pvsum, loss128 = _reduce_small(pd, pv, cg, cctx2, ada_w[0])

    out = {}
    out["hgrn_w_in"] = _adam_sharded("adam_w_in", rwin, hgrn_w_in[0], m_hgrn_w_in[0], v_hgrn_w_in[0], 256)
    out["hgrn_w_out"] = _adam_sharded("adam_w_out", rwout, hgrn_w_out[0], m_hgrn_w_out[0], v_hgrn_w_out[0], SH_ROWS)
    out["pool_w_in"] = _adam_sharded("adam_pw_in", rpwin, pool_w_in[0], m_pool_w_in[0], v_pool_w_in[0], 512)
    out["pool_w_grp"] = _adam_sharded("adam_pgrp", rpgrp.reshape(NDEV, 4 * SH_GRP, PG), pool_w_grp[0].reshape(4 * SH_GRP, PG),
                                      m_pool_w_grp[0].reshape(4 * SH_GRP, PG), v_pool_w_grp[0].reshape(4 * SH_GRP, PG), 4 * SH_GRP)
    out["pool_w_out"] = _adam_sharded("adam_pw_out", rpwout, pool_w_out[0], m_pool_w_out[0], v_pool_w_out[0], SH_ROWS)
    g_ada2 = g_ada.reshape(2 * D, SH_ADA)
    out["ada_w"] = (g_ada2,) + _adam_dense("adam_ada_w", g_ada2, ada_w.reshape(2 * D, SH_ADA), m_ada_w.reshape(2 * D, SH_ADA),
                                           v_ada_w.reshape(2 * D, SH_ADA), 512)

    lb_me = lax.dynamic_slice_in_dim(lb, idx * DH, DH, axis=1)
    small = ["c_ctx", "ada_b", "norm_w", "hgrn_lb_logits", "hgrn_gnorm_w", "pool_scale", "final_norm_w"]
    gs = [g_cctx, g_adab.reshape(2, 3 * D), pvsum[:, PV_NW:PV_NW + 2 * D].reshape(2, D),
          lax.dynamic_slice_in_dim(pvsum[:, PV_LB:PV_LB + 2 * E].reshape(2, E), idx * DH, DH, axis=1),
          pvsum[:, PV_GNORM:PV_GNORM + E], lax.dynamic_slice_in_dim(pvsum[:, PV_PSCALE:PV_PSCALE + E], idx * DH, DH, axis=1),
          pvsum[:, PV_FINAL:PV_FINAL + D]]
    ws = [cctx2, ada_b, norm_w, hgrn_lb_logits[0], hgrn_gnorm_w, pool_scale, fnw]
    ms = [m_c_ctx.reshape(1, D), m_ada_b, m_norm_w, m_hgrn_lb_logits[0], m_hgrn_gnorm_w, m_pool_scale, m_final_norm_w.reshape(1, D)]
    vs = [v_c_ctx.reshape(1, D), v_ada_b, v_norm_w, v_hgrn_lb_logits[0], v_hgrn_gnorm_w, v_pool_scale, v_final_norm_w.reshape(1, D)]
    res = _adam_small(gs, ws, ms, vs, 3, lb_me)
    n = len(small)
    for j, name in enumerate(small):
        out[name] = tuple(res[q * n + j] for q in range(4))

    shapes = {"c_ctx": (D,), "ada_w": (2, D, SH_ADA), "ada_b": (2, 3 * D), "norm_w": (2, D), "hgrn_w_in": (1, D, SH_WIN),
              "hgrn_lb_logits": (1, 2, DH), "hgrn_gnorm_w": (1, E), "hgrn_w_out": (1, SH_ROWS, D), "pool_w_in": (1, D, SH_PWIN),
              "pool_w_grp": (1, 4, SH_GRP, PG), "pool_scale": (1, DH), "pool_w_out": (1, SH_ROWS, D), "final_norm_w": (D,)}
    order = ["c_ctx", "ada_w", "ada_b", "norm_w", "hgrn_w_in", "hgrn_lb_logits", "hgrn_gnorm_w", "hgrn_w_out", "pool_w_in",
             "pool_w_grp", "pool_scale", "pool_w_out", "final_norm_w"]
    flat = [out[name][q].reshape(shapes[name]) for q in range(4) for name in order]
    return (loss128[0, 0], grad_x[None], *flat)
```

```python
import functools

import numpy as np
import jax
import jax.numpy as jnp
from jax import lax
from jax.experimental import pallas as pl
from jax.experimental.pallas import tpu as pltpu

F32 = jnp.float32
BF16 = jnp.bfloat16

D = 1024
E = 1024
HEADS = 8
DH = 128
CHUNK = 64
T = 2048
TC = 256
TT = T + TC
TM = 256
NT = TT // TM
NTX = T // TM
NDEV = 8
GRID_W = 64
POOL_WINDOWS = (2, 4, 8, 16)
PG = 256
EPS = 1e-6
WIN_COLS = 5 * E
SH_WIN = WIN_COLS // NDEV
SH_PWIN = 2 * E // NDEV
SH_ROWS = E // NDEV
SH_GRP = PG // NDEV
SH_ADA = 3 * D // NDEV
VMEM_LIMIT = 56 * 1024 * 1024
VMEM_LIMIT_SCAN = 60 * 1024 * 1024

ADAM_LR, ADAM_B1, ADAM_B2, ADAM_EPS, ADAM_WD, ADAM_STEP = 0.001, 0.9, 0.999, 1e-08, 0.01, 10

MESH = pl.DeviceIdType.MESH
VMEM_SPEC = pl.BlockSpec(memory_space=pltpu.VMEM)
HBM_SPEC = pl.BlockSpec(memory_space=pltpu.HBM)
ANY_SPEC = pl.BlockSpec(memory_space=pl.ANY)


def _sds(shape, dtype):
    return jax.ShapeDtypeStruct(shape, dtype)


def _bf(a):
    return a if a.dtype == BF16 else a.astype(BF16)


def _dot(a, b):
    return lax.dot_general(_bf(a), _bf(b), (((1,), (0,)), ((), ())), preferred_element_type=F32)


def _dot_tb(a, b):
    return lax.dot_general(_bf(a), _bf(b), (((1,), (1,)), ((), ())), preferred_element_type=F32)


def _dot_ta(a, b):
    return lax.dot_general(_bf(a), _bf(b), (((0,), (0,)), ((), ())), preferred_element_type=F32)


def _bdot(a, b):
    return lax.dot_general(_bf(a), _bf(b), (((2,), (1,)), ((0,), (0,))), preferred_element_type=F32)


def _bdot_nt(a, b):
    return lax.dot_general(_bf(a), _bf(b), (((2,), (2,)), ((0,), (0,))), preferred_element_type=F32)


def _bdot_tn(a, b):
    return lax.dot_general(_bf(a), _bf(b), (((1,), (1,)), ((0,), (0,))), preferred_element_type=F32)


def _dot01(m01, x):
    hi = x.astype(BF16)
    lo = (x - hi.astype(F32)).astype(BF16)
    return _dot(m01, hi) + _dot(m01, lo)


def _rstd(x):
    return lax.rsqrt(jnp.mean(x * x, axis=-1, keepdims=True) + EPS)


def _sigmoid(x):
    return jax.nn.sigmoid(x)


def _colsum(a):
    return jnp.sum(a, axis=0, keepdims=True)


def _stack_rows(rows):
    n = rows[0].shape[-1]
    rid = lax.broadcasted_iota(jnp.int32, (16, n), 0)
    out = jnp.zeros((16, n), F32)
    for i, r in enumerate(rows):
        out = jnp.where(rid == i, r, out)
    return out


def _head_map(fn, *arrs):
    outs = [fn(*[a[:, h * DH:(h + 1) * DH] for a in arrs]) for h in range(HEADS)]
    return jnp.concatenate(outs, axis=1)


def _gla_consts():
    r = np.arange(TM)[:, None]
    c = np.arange(TM)[None, :]
    same = (r // CHUNK) == (c // CHUNK)
    tril = same & (c <= r)
    triu = same & (c >= r)
    m = np.stack([tril, triu]).astype(np.float32)
    return jnp.asarray(m, BF16), jnp.asarray(m, F32)


def _pool_consts():
    r = np.arange(TM)[:, None]
    c = np.arange(TM)[None, :]
    same = (r // GRID_W) == (c // GRID_W)
    rp, cp = r % GRID_W, c % GRID_W
    bs, inv = [], []
    for w in POOL_WINDOWS:
        lo = np.clip(rp - w // 2, 0, GRID_W)
        hi = np.clip(rp - w // 2 + w, 0, GRID_W)
        bs.append(same & (cp >= lo) & (cp < hi))
        inv.append(1.0 / (hi - lo).astype(np.float32))
    b = np.stack(bs).astype(np.float32)
    bt = np.transpose(b, (0, 2, 1))
    return jnp.asarray(b, BF16), jnp.asarray(bt, BF16), jnp.asarray(np.stack(inv), F32)


def _mesh_pos():
    x, y, c = lax.axis_index("x"), lax.axis_index("y"), lax.axis_index("c")
    return x, y, c, 4 * x + 2 * y + c


def _peer(x, y, c, k):
    return (x ^ ((k >> 2) & 1), y ^ ((k >> 1) & 1), c ^ (k & 1))


def _gather_small(w_in, w_out, pw_in, pgrp, pw_out, lb_l, pscale, c, c_ctx, ada_w):
    n_arr = 4

    def body(win_r, wout_r, pwin_r, pgrp_r, pwout_r, lb_r, ps_r, c_r, cctx_r, ada_r,
             s_win, s_wout, s_pwin, s_pgrp, s_pwout, lb_o, ps_o, cg_o, mod_o, ssem, rsem):
        x, y, cc, idx = _mesh_pos()
        srcs = [lb_r, ps_r, c_r, mod_o.at[idx]]
        mine = [lb_o.at[idx], ps_o.at[idx], cg_o.at[idx], mod_o.at[idx]]

        def remote(a, k):
            return pltpu.make_async_remote_copy(src_ref=srcs[a], dst_ref=mine[a], send_sem=ssem.at[a, k], recv_sem=rsem.at[a, k],
                                                device_id=_peer(x, y, cc, k), device_id_type=MESH)

        first = [remote(a, k) for k in range(1, NDEV) for a in (2, 0, 1)]
        for cp in first:
            cp.start()
        lb_o[idx] = lb_r[...]
        ps_o[idx] = ps_r[...]
        cg_o[idx] = c_r[...]
        s_win[...] = win_r[...].astype(BF16)
        s_wout[...] = wout_r[...].astype(BF16)
        s_pwin[...] = pwin_r[...].astype(BF16)
        s_pgrp[...] = pgrp_r[...].astype(BF16)
        s_pwout[...] = pwout_r[...].astype(BF16)
        for k in range(1, NDEV):
            remote(2, k).wait_recv()
        rows = _stack_rows([cg_o[i] for i in range(NDEV)] + [cctx_r[...]])
        sc = rows * _sigmoid(rows)
        for l in range(2):
            mod_o[idx, l] = _dot(sc, ada_r[l])
        second = [remote(3, k) for k in range(1, NDEV)]
        for cp in second:
            cp.start()
        for cp in first + second:
            cp.wait_send()
        for k in range(1, NDEV):
            for a in (0, 1, 3):
                remote(a, k).wait_recv()

    outs = (
        _sds((D, SH_WIN), BF16), _sds((SH_ROWS, D), BF16), _sds((D, SH_PWIN), BF16), _sds((4, SH_GRP, PG), BF16), _sds((SH_ROWS, D), BF16),
        _sds((NDEV, 2, DH), F32), _sds((NDEV, 1, DH), F32), _sds((NDEV, 1, D), F32), _sds((NDEV, 2, 16, SH_ADA), F32),
    )
    return pl.pallas_call(
        body, name="gather_small", out_shape=outs,
        in_specs=[VMEM_SPEC] * 10, out_specs=[VMEM_SPEC] * 9,
        scratch_shapes=[pltpu.SemaphoreType.DMA((n_arr, NDEV)), pltpu.SemaphoreType.DMA((n_arr, NDEV))],
        compiler_params=pltpu.CompilerParams(vmem_limit_bytes=VMEM_LIMIT),
    )(w_in, w_out, pw_in, pgrp, pw_out, lb_l, pscale, c, c_ctx, ada_w)


def _gather_order(s):
    if isinstance(s, int):
        return (0, 1, 2, 4, 3, 5, 6, 7)[s]
    return s + (s == 3).astype(jnp.int32) - (s == 4).astype(jnp.int32)


GATHER_ISSUE = (1, 2, 4, 3, 5, 6, 7)
GATHER_ICI = (2, 4, 6)
GATHER_DIRECT = (1,) + GATHER_ICI
GLA_HB = 2
GATHER_FORWARD_AT = 2
RS_SLOTS = 5


def _weight_slices(refs, i):
    wout, pwin, pgrp, pwout = refs
    return [wout.at[pl.ds(pl.multiple_of(i * SH_ROWS, SH_ROWS), SH_ROWS), :],
            pwin.at[:, pl.ds(pl.multiple_of(i * SH_PWIN, 128), SH_PWIN)],
            pgrp.at[:, pl.ds(pl.multiple_of(i * SH_GRP, SH_GRP), SH_GRP), :],
            pwout.at[pl.ds(pl.multiple_of(i * SH_ROWS, SH_ROWS), SH_ROWS), :]]


def _modulated(x, nw, shift, scale):
    r = _rstd(x)
    xn = x * r
    a = xn * nw
    return a * (1.0 + scale) + shift, r, xn, a


def _ctx_or_x(i, ctx_ref, x_ref):
    return jnp.where(i == 0, ctx_ref[...], x_ref[...])


def _f1_gather_matmul(idx1, ctx, x, nw, msel, s_win):
    def body(idx_ref, ctx_ref, x_ref, nw_ref, m_ref, sw_ref, g_ref, win_o, wslot, hx_sc, ssem, rsem, lsem, osem):
        del idx_ref
        s, i = pl.program_id(0), pl.program_id(1)
        x, y, cc, idx = _mesh_pos()
        k = _gather_order(s)
        j = idx ^ k

        def remote(kk):
            return pltpu.make_async_remote_copy(src_ref=sw_ref, dst_ref=wslot.at[idx], send_sem=ssem.at[kk], recv_sem=rsem.at[kk],
                                                device_id=_peer(x, y, cc, kk), device_id_type=MESH)

        def forward(kk):
            jj = idx ^ kk
            return pltpu.make_async_remote_copy(src_ref=wslot.at[jj], dst_ref=wslot.at[jj], send_sem=ssem.at[kk ^ 1],
                                                recv_sem=rsem.at[kk ^ 1], device_id=(x, y, 1 - cc), device_id_type=MESH)

        own = pltpu.make_async_copy(sw_ref, wslot.at[idx], lsem)

        def to_hbm(jj, kk):
            return pltpu.make_async_copy(wslot.at[jj], win_o.at[:, pl.ds(pl.multiple_of(jj * SH_WIN, 128), SH_WIN)], osem.at[kk])

        @pl.when((s == 0) & (i == 0))
        def _():
            own.start()
            for kk in GATHER_DIRECT:
                remote(kk).start()
            own.wait()

        @pl.when(s == 0)
        def _():
            hx, _, _, _ = _modulated(_ctx_or_x(i, ctx_ref, x_ref), nw_ref[...], m_ref[0, 0:1, :], m_ref[0, 1:2, :])
            hx_sc[i] = hx.astype(BF16)

        @pl.when((s > 0) & (i == 0))
        def _():
            remote(k).wait_recv()

            @pl.when((k & 1) == 0)
            def _():
                forward(k).start()

        @pl.when(i == 0)
        def _():
            to_hbm(j, k).start()

        g_ref[...] = jnp.dot(hx_sc[i], wslot[j], preferred_element_type=F32)

        @pl.when((s == NDEV - 1) & (i == NT - 1))
        def _():
            for kk in GATHER_DIRECT:
                remote(kk).wait_send()
            for kk in GATHER_ICI:
                forward(kk).wait_send()
            for kk in range(NDEV):
                to_hbm(idx ^ kk, kk).wait()

    grid_spec = pltpu.PrefetchScalarGridSpec(
        num_scalar_prefetch=1, grid=(NDEV, NT),
        in_specs=[VMEM_SPEC, pl.BlockSpec((TM, D), lambda s, i, ix: (jnp.maximum(i - 1, 0), 0)), VMEM_SPEC,
                  pl.BlockSpec((1, 2, D), lambda s, i, ix: (jnp.minimum(i, 1), 0, 0)), HBM_SPEC],
        out_specs=[pl.BlockSpec((TM, SH_WIN), lambda s, i, ix: (i, ix[0] ^ _gather_order(s))), HBM_SPEC],
        scratch_shapes=[pltpu.VMEM((NDEV, D, SH_WIN), BF16), pltpu.VMEM((NT, TM, D), BF16),
                        pltpu.SemaphoreType.DMA((NDEV,)), pltpu.SemaphoreType.DMA((NDEV,)), pltpu.SemaphoreType.DMA,
                        pltpu.SemaphoreType.DMA((NDEV,))])
    return pl.pallas_call(
        body, name="f1_gather_matmul", grid_spec=grid_spec,
        out_shape=(_sds((TT, WIN_COLS), F32), _sds((D, WIN_COLS), BF16)),
        compiler_params=pltpu.CompilerParams(dimension_semantics=("arbitrary", "arbitrary"), vmem_limit_bytes=VMEM_LIMIT),
    )(idx1, ctx, x, nw, msel, s_win)


def _gla_gates(pre, qpre, lbd, cum, rev):
    rows, n = pre.shape
    nch = rows // CHUNK
    sig = _sigmoid(pre)
    f = lbd + (1.0 - lbd) * sig
    k = 1.0 - f
    g = _dot01(cum, jnp.log(f))
    g3 = g.reshape(nch, CHUNK, n)
    last = 0 if rev else CHUNK - 1
    mid = CHUNK // 2 if rev else CHUNK // 2 - 1
    gl1, gm1 = g3[:, last:last + 1, :], g3[:, mid:mid + 1, :]

    def bc(a):
        return jnp.broadcast_to(a, g3.shape).reshape(rows, n)

    gm = bc(gm1)
    e_q, e_k = jnp.exp(g - gm), jnp.exp(gm - g)
    e_in, e_end = e_q * bc(jnp.exp(gm1)), e_k * bc(jnp.exp(gl1 - gm1))
    qsig = _sigmoid(qpre)
    qs = qpre * qsig * (DH ** -0.5)
    return dict(sig=sig, f=f, k=k, qsig=qsig, qs=qs, e_q=e_q, e_k=e_k, e_in=e_in, e_end=e_end,
                decay=[jnp.exp(g3[ci, last:last + 1, :]) for ci in range(nch)])


def _put_heads(ref, lead, arr):
    for h in range(HEADS):
        ref[lead + (h,)] = arr[:, h * DH:(h + 1) * DH]


def _get_heads(ref, lead=()):
    return jnp.concatenate([ref[lead + (h,)] for h in range(HEADS)], axis=1)


def _gla_prep(g_all, lb, cum01):
    def body(g_ref, lb_ref, cum_ref, p0_ref, p1_ref, v_ref, dec_ref):
        qpre = g_ref[:, 3 * E:4 * E]
        _put_heads(v_ref, (), g_ref[:, 2 * E:3 * E].astype(BF16))
        dec_ref[...] = jnp.zeros_like(dec_ref)
        for d, p_ref in ((0, p0_ref), (1, p1_ref)):
            t = _gla_gates(g_ref[:, d * E:(d + 1) * E], qpre, lb_ref[d:d + 1, :], cum_ref[d], d == 1)
            _put_heads(p_ref, (0,), (t["qs"] * t["e_q"]).astype(BF16))
            _put_heads(p_ref, (1,), (t["k"] * t["e_k"]).astype(BF16))
            _put_heads(p_ref, (2,), (t["qs"] * t["e_in"]).astype(BF16))
            _put_heads(p_ref, (3,), (t["k"] * t["e_end"]).astype(BF16))
            for ci in range(TM // CHUNK):
                dec_ref[d, 0, ci:ci + 1, :] = t["decay"][ci]

    quad = pl.BlockSpec((4, HEADS, TM, DH), lambda i: (0, 0, i, 0))
    return pl.pallas_call(
        body, name="gla_prep", grid=(NT,),
        in_specs=[pl.BlockSpec((TM, WIN_COLS), lambda i: (i, 0)), VMEM_SPEC, VMEM_SPEC],
        out_specs=[quad, quad, pl.BlockSpec((HEADS, TM, DH), lambda i: (0, i, 0)), pl.BlockSpec((2, 1, 8, E), lambda i: (0, i, 0, 0))],
        out_shape=(_sds((4, HEADS, TT, DH), BF16), _sds((4, HEADS, TT, DH), BF16), _sds((HEADS, TT, DH), BF16), _sds((2, NT, 8, E), F32)),
        compiler_params=pltpu.CompilerParams(dimension_semantics=("arbitrary",), vmem_limit_bytes=VMEM_LIMIT),
    )(g_all, lb, cum01)


def _scan_tile(i, rev):
    t = jnp.where(i == 0, 0, NT - i) if rev else i
    return t, pl.ds(pl.multiple_of(t * TM, TM), TM)


def _chunk_order(rev):
    n = TM // CHUNK
    return tuple(range(n - 1, -1, -1)) if rev else tuple(range(n))


def _gla_fwd(p0, p1, v_all, dec, mask01, s_wout, s_pwin, s_pgrp, s_pwout):
    def body(p0_ref, p1_ref, v_ref, dec_ref, msk_ref, swout_r, spwin_r, spgrp_r, spwout_r,
             o_ref, wout_o, pwin_o, pgrp_o, pwout_o, ob_sc, ssem, rsem, lsem):
        h = pl.program_id(0)
        x, y, cc, idx = _mesh_pos()
        srcs = [swout_r, spwin_r, spgrp_r, spwout_r]
        gathered = (wout_o, pwin_o, pgrp_o, pwout_o)
        mine = _weight_slices(gathered, idx)

        def remote(a, k):
            return pltpu.make_async_remote_copy(src_ref=srcs[a], dst_ref=mine[a], send_sem=ssem.at[a, k], recv_sem=rsem.at[a, k],
                                                device_id=_peer(x, y, cc, k), device_id_type=MESH)

        def forward(a, k):
            blk = _weight_slices(gathered, idx ^ k)[a]
            return pltpu.make_async_remote_copy(src_ref=blk, dst_ref=blk, send_sem=ssem.at[a, k ^ 1], recv_sem=rsem.at[a, k ^ 1],
                                                device_id=(x, y, 1 - cc), device_id_type=MESH)

        copies = [remote(a, k) for k in GATHER_DIRECT for a in range(4)]
        passed = [forward(a, k) for k in GATHER_ICI for a in range(4)]
        local = [pltpu.make_async_copy(srcs[a], mine[a], lsem.at[a]) for a in range(4)]

        @pl.when(h == 0)
        def _():
            for cp in copies + local:
                cp.start()

        @pl.when(h == GATHER_FORWARD_AT)
        def _():
            for k in GATHER_ICI:
                for a in range(4):
                    remote(a, k).wait_recv()
                    forward(a, k).start()

        lanes = [(d, hh) for d in (0, 1) for hh in range(GLA_HB)]
        nch = TM // CHUNK

        def tile_body(i, st):
            where = [_scan_tile(i, d == 1) for d in (0, 1)]

            def stacked(fn):
                return jnp.stack([fn(d, hh, where[d][1]) for d, hh in lanes])

            qg, kg, q_in, kend = [stacked(lambda d, hh, rows, ty=ty: (p1_ref if d else p0_ref)[ty, hh, rows, :]) for ty in range(4)]
            v = stacked(lambda d, hh, rows: v_ref[hh, rows, :])
            a = _bdot_nt(qg, kg) * jnp.stack([msk_ref[d] for d, _ in lanes])
            intra = _bdot(a, v)
            outs = [[None] * nch for _ in lanes]
            for n in range(nch):
                cis = [nch - 1 - n if d else n for d, _ in lanes]

                def chunk(arr):
                    return jnp.stack([arr[l, ci * CHUNK:(ci + 1) * CHUNK] for l, ci in enumerate(cis)])

                dec = jnp.stack([dec_ref[d, where[d][0], ci:ci + 1, hh * DH:(hh + 1) * DH] for (d, hh), ci in zip(lanes, cis)])
                inter = _bdot_nt(chunk(q_in), st)
                for l, ci in enumerate(cis):
                    outs[l][ci] = inter[l] + intra[l, ci * CHUNK:(ci + 1) * CHUNK]
                st = st * dec + _bdot_tn(chunk(v), chunk(kend))
            for l, (d, hh) in enumerate(lanes):
                (ob_sc if d else o_ref)[hh, where[d][1], :] = jnp.concatenate(outs[l], axis=0)
            return st

        lax.fori_loop(0, NT, tile_body, jnp.zeros((len(lanes), DH, DH), F32))
        o_ref[...] += ob_sc[...]

        @pl.when(h == HEADS // GLA_HB - 1)
        def _():
            for cp in copies + passed:
                cp.wait_send()
            for a in range(4):
                remote(a, 1).wait_recv()
            for cp in passed:
                cp.wait_recv()
            for cp in local:
                cp.wait()

    quad = pl.BlockSpec((4, GLA_HB, TT, DH), lambda h: (0, h, 0, 0))
    head = pl.BlockSpec((GLA_HB, TT, DH), lambda h: (h, 0, 0))
    return pl.pallas_call(
        body, name="gla_fwd", grid=(HEADS // GLA_HB,),
        in_specs=[quad, quad, head, pl.BlockSpec((2, NT, 8, GLA_HB * DH), lambda h: (0, 0, 0, h)),
                  pl.BlockSpec((2, TM, TM), lambda h: (0, 0, 0))] + [HBM_SPEC] * 4,
        out_specs=[head] + [HBM_SPEC] * 4,
        out_shape=(_sds((HEADS, TT, DH), F32), _sds((E, D), BF16), _sds((D, 2 * E), BF16), _sds((4, PG, PG), BF16), _sds((E, D), BF16)),
        scratch_shapes=[pltpu.VMEM((GLA_HB, TT, DH), F32), pltpu.SemaphoreType.DMA((4, NDEV)), pltpu.SemaphoreType.DMA((4, NDEV)),
                        pltpu.SemaphoreType.DMA((4,))],
        compiler_params=pltpu.CompilerParams(dimension_semantics=("arbitrary",), vmem_limit_bytes=VMEM_LIMIT),
    )(p0, p1, v_all, dec, mask01, s_wout, s_pwin, s_pgrp, s_pwout)


def _gated_norm(o, z, gw):
    r = _head_map(lambda oh: jnp.broadcast_to(_rstd(oh), oh.shape), o)
    on = o * r
    zs = _sigmoid(z)
    sz = z * zs
    return on * gw * sz, r, on, zs, sz


def _f3_out(o, g_all, x, gate, gw, wout):
    def body(o_ref, z_ref, x_ref, gate_ref, gw_ref, w_ref, x1_ref):
        og, _, _, _, _ = _gated_norm(_get_heads(o_ref), z_ref[...], gw_ref[...])
        x1_ref[...] = x_ref[...] + gate_ref[...] * _dot(og, w_ref[...])

    return pl.pallas_call(
        body, name="f3_out", grid=(NTX,),
        in_specs=[pl.BlockSpec((HEADS, TM, DH), lambda i: (0, i + 1, 0)), pl.BlockSpec((TM, E), lambda i: (i + 1, 4)),
                  pl.BlockSpec((TM, D), lambda i: (i, 0)), pl.BlockSpec((1, D), lambda i: (0, 0)),
                  pl.BlockSpec((1, E), lambda i: (0, 0)), pl.BlockSpec((E, D), lambda i: (0, 0))],
        out_specs=pl.BlockSpec((TM, D), lambda i: (i, 0)),
        out_shape=_sds((T, D), F32),
        compiler_params=pltpu.CompilerParams(dimension_semantics=("arbitrary",)),
    )(o, g_all, x, gate, gw, wout)


def _pool_layer(x1, tgt, mod1, nw1, fnw, pwin, pgrp, pscale, pwout, pb, pbt, pinv):
    def body(x_ref, t_ref, m_ref, nw_ref, fw_ref, pwin_ref, pgrp_ref, ps_ref, pwout_ref, pb_ref, pbt_ref, pinv_ref,
             dx_ref, gpwin_o, gpgrp_o, gpwout_o, dmod_o, gnw_o, gfw_o, gps_o, loss_o,
             a_pwin, a_pgrp, a_pwout):
        i = pl.program_id(0)

        @pl.when(i == 0)
        def _():
            for ref in (a_pwin, a_pgrp, a_pwout, dmod_o, gnw_o, gfw_o, gps_o, loss_o):
                ref[...] = jnp.zeros_like(ref)

        shift, scale, gate = m_ref[0:1, :], m_ref[1:2, :], m_ref[2:3, :]
        nw, fw, ps = nw_ref[...], fw_ref[...], ps_ref[...]
        x1 = x_ref[...]
        hx, r1, xn, a = _modulated(x1, nw, shift, scale)
        hxb = hx.astype(BF16)
        uz = jnp.dot(hxb, pwin_ref[...], preferred_element_type=F32)
        u, z = uz[:, :E], uz[:, E:]
        pooled, ys = [], []
        for g in range(4):
            ug = u[:, g * PG:(g + 1) * PG]
            pg = _dot01(pb_ref[g], ug) * pinv_ref[g] - ug
            pooled.append(pg.astype(BF16))
            ys.append(_dot(pooled[g], pgrp_ref[g]))
        ycat = jnp.concatenate(ys, axis=1)
        y = ycat * ps
        zs = _sigmoid(z)
        sz = z * zs
        p = (y * sz).astype(BF16)
        out = _dot(p, pwout_ref[...])
        x2 = x1 + gate * out
        r2 = _rstd(x2)
        xn2 = x2 * r2
        diff = xn2 * fw - t_ref[...]
        loss_o[...] += _colsum(diff * diff)
        dyf = diff * (1.0 / D)
        gfw_o[...] += _colsum(dyf * xn2)
        dxn2 = dyf * fw
        dx2 = r2 * (dxn2 - xn2 * jnp.mean(dxn2 * xn2, axis=-1, keepdims=True))
        dgate = _colsum(dx2 * out)
        dout = (dx2 * gate).astype(BF16)
        for j in range(4):
            cs = slice(j * PG, (j + 1) * PG)
            a_pwout[:, cs] += _dot_ta(p, dout[:, cs])
        dp = _dot_tb(dout, pwout_ref[...])
        dy = dp * sz
        dz = dp * y * (zs * (1.0 + z * (1.0 - zs)))
        gps_o[...] += _colsum(dy * ycat)
        dycat = dy * ps
        dus = []
        for g in range(4):
            dyg = dycat[:, g * PG:(g + 1) * PG].astype(BF16)
            a_pgrp[g] += _dot_ta(pooled[g], dyg)
            dpg = _dot_tb(dyg, pgrp_ref[g])
            dus.append(_dot01(pbt_ref[g], dpg * pinv_ref[g]) - dpg)
        duz = jnp.concatenate(dus + [dz], axis=1).astype(BF16)
        for j in range(2 * E // PG):
            cs = slice(j * PG, (j + 1) * PG)
            a_pwin[:, cs] += _dot_ta(hxb, duz[:, cs])
        dhx = _dot_tb(duz, pwin_ref[...])
        dmod_o[0:1, :] += _colsum(dhx)
        dmod_o[1:2, :] += _colsum(dhx * a)
        dmod_o[2:3, :] += dgate
        da = dhx * (1.0 + scale)
        gnw_o[...] += _colsum(da * xn)
        dxn = da * nw
        dx_ref[...] = dx2 + r1 * (dxn - xn * jnp.mean(dxn * xn, axis=-1, keepdims=True))

        @pl.when(i == NTX - 1)
        def _():
            gpwin_o[...] = a_pwin[...].astype(BF16)
            gpgrp_o[...] = a_pgrp[...].astype(BF16)
            gpwout_o[...] = a_pwout[...].astype(BF16)

    tile = pl.BlockSpec((TM, D), lambda i: (i, 0))
    outs = (_sds((T, D), F32), _sds((D, 2 * E), BF16), _sds((4, PG, PG), BF16), _sds((E, D), BF16),
            _sds((3, D), F32), _sds((1, D), F32), _sds((1, D), F32), _sds((1, E), F32), _sds((1, D), F32))
    return pl.pallas_call(
        body, name="pool_layer", grid=(NTX,),
        in_specs=[tile, tile] + [VMEM_SPEC] * 10,
        out_specs=[tile] + [VMEM_SPEC] * 8,
        out_shape=outs,
        scratch_shapes=[pltpu.VMEM((D, 2 * E), F32), pltpu.VMEM((4, PG, PG), F32), pltpu.VMEM((E, D), F32)],
        compiler_params=pltpu.CompilerParams(dimension_semantics=("arbitrary",), vmem_limit_bytes=VMEM_LIMIT),
    )(x1, tgt, mod1, nw1, fnw, pwin, pgrp, pscale, pwout, pb, pbt, pinv)


def _b3_out_bwd(dx1, o, g_all, gate, gw, wout):
    def body(dx_ref, o_ref, z_ref, gate_ref, gw_ref, w_ref, do_ref, dz_ref, gw_o, dgate_o, ggw_o, acc):
        i = pl.program_id(0)

        @pl.when(i == 0)
        def _():
            acc[...] = jnp.zeros_like(acc)
            dgate_o[...] = jnp.zeros_like(dgate_o)
            ggw_o[...] = jnp.zeros_like(ggw_o)
            do_ref[...] = jnp.zeros_like(do_ref)
            dz_ref[...] = jnp.zeros_like(dz_ref)

        @pl.when(i > 0)
        def _():
            gw = gw_ref[...]
            z = z_ref[...]
            og, r, on, zs, sz = _gated_norm(_get_heads(o_ref), z, gw)
            ogb = og.astype(BF16)
            dx = dx_ref[...]
            dgate_o[...] += _colsum(dx * _dot(ogb, w_ref[...]))
            dy = (dx * gate_ref[...]).astype(BF16)
            for j in range(4):
                cs = slice(j * PG, (j + 1) * PG)
                acc[:, cs] += _dot_ta(ogb, dy[:, cs])
            dog = _dot_tb(dy, w_ref[...])
            dz_ref[...] = (dog * (on * gw) * (zs * (1.0 + z * (1.0 - zs)))).astype(BF16)
            dong = dog * sz
            ggw_o[...] += _colsum(dong * on)
            don = dong * gw
            do = _head_map(lambda dh, nh, rh: rh * (dh - nh * jnp.mean(dh * nh, axis=-1, keepdims=True)), don, on, r)
            _put_heads(do_ref, (), do.astype(BF16))

        @pl.when(i == NT - 1)
        def _():
            gw_o[...] = acc[...].astype(BF16)

    prev = lambda i: (jnp.maximum(i - 1, 0), 0)
    heads = pl.BlockSpec((HEADS, TM, DH), lambda i: (0, i, 0))
    return pl.pallas_call(
        body, name="b3_out_bwd", grid=(NT,),
        in_specs=[pl.BlockSpec((TM, D), prev), heads, pl.BlockSpec((TM, E), lambda i: (i, 4)),
                  VMEM_SPEC, VMEM_SPEC, VMEM_SPEC],
        out_specs=[heads, pl.BlockSpec((TM, E), lambda i: (i, 0)), VMEM_SPEC, VMEM_SPEC, VMEM_SPEC],
        out_shape=(_sds((HEADS, TT, DH), BF16), _sds((TT, E), BF16), _sds((E, D), BF16), _sds((1, D), F32), _sds((1, E), F32)),
        scratch_shapes=[pltpu.VMEM((E, D), F32)],
        compiler_params=pltpu.CompilerParams(dimension_semantics=("arbitrary",), vmem_limit_bytes=VMEM_LIMIT),
    )(dx1, o, g_all, gate, gw, wout)


def _gla_bwd(p0, p1, v_all, dec, do, mask01, gwout, gpwin, gpgrp, gpwout):
    nch = TM // CHUNK

    def body(p0_ref, p1_ref, v_ref, dec_ref, do_ref, msk_ref, gwout_r, gpwin_r, gpgrp_r, gpwout_r,
             d0_ref, d1_ref, dv_ref, dgl_ref, rwout_o, rpwin_o, rpgrp_o, rpwout_o,
             ss_sc, dv_sc, ssem, rsem, lsem):
        h = pl.program_id(0)
        x, y, cc, idx = _mesh_pos()
        grads = (gwout_r, gpwin_r, gpgrp_r, gpwout_r)
        dsts = [rwout_o.at[idx], rpwin_o.at[idx], rpgrp_o.at[idx], rpwout_o.at[idx]]

        def remote(a, k):
            px, py, pc = _peer(x, y, cc, k)
            return pltpu.make_async_remote_copy(src_ref=_weight_slices(grads, 4 * px + 2 * py + pc)[a], dst_ref=dsts[a],
                                                send_sem=ssem.at[a, k], recv_sem=rsem.at[a, k], device_id=(px, py, pc), device_id_type=MESH)

        copies = [remote(a, k) for k in GATHER_ISSUE for a in range(4)]
        local = [pltpu.make_async_copy(_weight_slices(grads, idx)[a], dsts[a], lsem.at[a]) for a in range(4)]

        @pl.when(h == 0)
        def _():
            for cp in copies + local:
                cp.start()

        lanes = [(d, hh) for d in (0, 1) for hh in range(GLA_HB)]
        zero = jnp.zeros((len(lanes), DH, DH), F32)
        dgl_ref[...] = jnp.zeros_like(dgl_ref)

        def p_of(d):
            return p1_ref if d else p0_ref

        def scan_step(i, n):
            where = [_scan_tile(i, d == 1) for d in (0, 1)]
            cis = [nch - 1 - n if d else n for d, _ in lanes]
            dec = jnp.stack([dec_ref[d, where[d][0], ci:ci + 1, hh * DH:(hh + 1) * DH] for (d, hh), ci in zip(lanes, cis)])

            def chunk(arr):
                return jnp.stack([arr[l, ci * CHUNK:(ci + 1) * CHUNK] for l, ci in enumerate(cis)])

            return where, cis, dec, chunk

        def stacked(i, fn):
            where = [_scan_tile(i, d == 1) for d in (0, 1)]
            return jnp.stack([fn(d, hh, where[d][1]) for d, hh in lanes])

        def fwd_body(i, st):
            v = stacked(i, lambda d, hh, rows: v_ref[hh, rows, :])
            kend = stacked(i, lambda d, hh, rows: p_of(d)[3, hh, rows, :])
            for n in range(nch):
                _, _, dec, chunk = scan_step(i, n)
                ss_sc[i * nch + n] = st
                st = st * dec + _bdot_tn(chunk(v), chunk(kend))
            return st

        lax.fori_loop(0, NT, fwd_body, zero)

        def bwd_body(ii, dst):
            i = NT - 1 - ii
            qg, kg, q_in, kend = [stacked(i, lambda d, hh, rows, ty=ty: p_of(d)[ty, hh, rows, :]) for ty in range(4)]
            v = stacked(i, lambda d, hh, rows: v_ref[hh, rows, :])
            dob = stacked(i, lambda d, hh, rows: do_ref[hh, rows, :])
            msk = jnp.stack([msk_ref[d] for d, _ in lanes])
            a = (_bdot_nt(qg, kg) * msk).astype(BF16)
            da = (_bdot_nt(dob, v) * msk).astype(BF16)
            dqg = _bdot(da, kg)
            dkg = _bdot_tn(da, qg)
            dv_intra = _bdot_tn(a, dob)
            dv_l, dkend_l, dqin_l = ([[None] * nch for _ in lanes] for _ in range(3))
            for n in range(nch - 1, -1, -1):
                where, cis, dec, chunk = scan_step(i, n)
                s_c = ss_sc[i * nch + n]
                dstb = dst.astype(BF16)
                kend_c, v_c, dob_c = chunk(kend), chunk(v), chunk(dob)
                dv_c = chunk(dv_intra) + _bdot_nt(kend_c, dstb)
                dkend_c = _bdot(v_c, dstb)
                dqin_c = _bdot(dob_c, s_c)
                dgl = jnp.sum(s_c * dst, axis=1, keepdims=True) * dec
                for l, ((d, hh), ci) in enumerate(zip(lanes, cis)):
                    dv_l[l][ci], dkend_l[l][ci], dqin_l[l][ci] = dv_c[l], dkend_c[l], dqin_c[l]
                    dgl_ref[d, where[d][0], ci:ci + 1, hh * DH:(hh + 1) * DH] = dgl[l]
                dst = dst * dec + _bdot_tn(dob_c, chunk(q_in))
            where = [_scan_tile(i, d == 1) for d in (0, 1)]
            for l, (d, hh) in enumerate(lanes):
                rows = where[d][1]
                d_ref = d1_ref if d else d0_ref
                d_ref[0, hh, rows, :] = dqg[l].astype(BF16)
                d_ref[1, hh, rows, :] = dkg[l].astype(BF16)
                d_ref[2, hh, rows, :] = jnp.concatenate(dqin_l[l], axis=0).astype(BF16)
                d_ref[3, hh, rows, :] = jnp.concatenate(dkend_l[l], axis=0).astype(BF16)
                dv_sc[d, hh, rows, :] = jnp.concatenate(dv_l[l], axis=0).astype(BF16)
            return dst

        lax.fori_loop(0, NT, bwd_body, zero)
        dv_ref[...] = (dv_sc[0].astype(F32) + dv_sc[1].astype(F32)).astype(BF16)

        @pl.when(h == HEADS // GLA_HB - 1)
        def _():
            for cp in copies:
                cp.wait_send()
            for cp in copies:
                cp.wait_recv()
            for cp in local:
                cp.wait()

    quad = pl.BlockSpec((4, GLA_HB, TT, DH), lambda h: (0, h, 0, 0))
    col = pl.BlockSpec((GLA_HB, TT, DH), lambda h: (h, 0, 0))
    chunkv = pl.BlockSpec((2, NT, 8, GLA_HB * DH), lambda h: (0, 0, 0, h))
    outs = (_sds((4, HEADS, TT, DH), BF16), _sds((4, HEADS, TT, DH), BF16), _sds((HEADS, TT, DH), BF16), _sds((2, NT, 8, E), F32),
            _sds((NDEV, SH_ROWS, D), BF16), _sds((NDEV, D, SH_PWIN), BF16), _sds((NDEV, 4, SH_GRP, PG), BF16), _sds((NDEV, SH_ROWS, D), BF16))
    return pl.pallas_call(
        body, name="gla_bwd", grid=(HEADS // GLA_HB,),
        in_specs=[quad, quad, col, chunkv, col, pl.BlockSpec((2, TM, TM), lambda h: (0, 0, 0))] + [HBM_SPEC] * 4,
        out_specs=[quad, quad, col, chunkv] + [HBM_SPEC] * 4,
        out_shape=outs,
        scratch_shapes=[pltpu.VMEM((NT * nch, 2 * GLA_HB, DH, DH), F32), pltpu.VMEM((2, GLA_HB, TT, DH), BF16),
                        pltpu.SemaphoreType.DMA((4, NDEV)), pltpu.SemaphoreType.DMA((4, NDEV)), pltpu.SemaphoreType.DMA((4,))],
        compiler_params=pltpu.CompilerParams(dimension_semantics=("arbitrary",), vmem_limit_bytes=VMEM_LIMIT_SCAN),
    )(p0, p1, v_all, dec, do, mask01, gwout, gpwin, gpgrp, gpwout)


TMB = 128


def _gla_post_bwd(g_all, d0, d1, dgl, dv, dz, lb, cum01):
    nch = TMB // CHUNK

    def body(g_ref, d0_ref, d1_ref, dgl_ref, dv_ref, dz_ref, lb_ref, cum_ref, dg_ref, dlb_ref):
        i = pl.program_id(0)

        @pl.when(i == 0)
        def _():
            dlb_ref[...] = jnp.zeros_like(dlb_ref)

        half = i & 1
        qpre = g_ref[:, 3 * E:4 * E]
        dqs_sum = None
        dpre = []
        for d, d_ref in ((0, d0_ref), (1, d1_ref)):
            rev = d == 1
            lbd = lb_ref[d:d + 1, :]
            t = _gla_gates(g_ref[:, d * E:(d + 1) * E], qpre, lbd, cum_ref[d, :TMB, :TMB], rev)
            dqg, dkg, dqin, dkend = [_get_heads(d_ref, (ty,)).astype(F32) for ty in range(4)]
            dqs = dqg * t["e_q"] + dqin * t["e_in"]
            dk = dkg * t["e_k"] + dkend * t["e_end"]
            dkk = dkend * (t["k"] * t["e_end"])
            dg = t["qs"] * dqs - t["k"] * dk
            dkk3 = dkk.reshape(nch, CHUNK, E)
            dgl8 = dgl_ref[d, 0]
            dgl_rows = [jnp.where(half == 0, dgl8[ci:ci + 1, :], dgl8[nch + ci:nch + ci + 1, :]) for ci in range(nch)]
            dgl_b = jnp.concatenate([jnp.broadcast_to(dgl_rows[ci] + jnp.sum(dkk3[ci], axis=0, keepdims=True), (CHUNK, E))
                                     for ci in range(nch)], axis=0)
            pos = lax.broadcasted_iota(jnp.int32, (TMB, E), 0) & (CHUNK - 1)
            dg = dg + jnp.where(pos == (0 if rev else CHUNK - 1), dgl_b, 0.0)
            dlf = _dot01(cum_ref[1 - d, :TMB, :TMB], dg)
            df = dlf / t["f"] - dk
            sig = t["sig"]
            dpre.append((df * (1.0 - lbd) * sig * (1.0 - sig)).astype(BF16))
            dlb_ref[d:d + 1, :] += _colsum(df * (1.0 - sig))
            dqs_sum = dqs if dqs_sum is None else dqs_sum + dqs
            qsig = t["qsig"]
        dqpre = dqs_sum * (DH ** -0.5) * (qsig * (1.0 + qpre * (1.0 - qsig)))
        dg_ref[...] = jnp.concatenate([dpre[0], dpre[1], _get_heads(dv_ref), dqpre.astype(BF16), dz_ref[...]], axis=1)

    quad = pl.BlockSpec((4, HEADS, TMB, DH), lambda i: (0, 0, i, 0))
    tile = pl.BlockSpec((TMB, E), lambda i: (i, 0))
    return pl.pallas_call(
        body, name="gla_post_bwd", grid=(TT // TMB,),
        in_specs=[pl.BlockSpec((TMB, WIN_COLS), lambda i: (i, 0)), quad, quad,
                  pl.BlockSpec((2, 1, 8, E), lambda i: (0, i // 2, 0, 0)), pl.BlockSpec((HEADS, TMB, DH), lambda i: (0, i, 0)), tile,
                  VMEM_SPEC, VMEM_SPEC],
        out_specs=[pl.BlockSpec((TMB, WIN_COLS), lambda i: (i, 0)), VMEM_SPEC],
        out_shape=(_sds((TT, WIN_COLS), BF16), _sds((2, E), F32)),
        compiler_params=pltpu.CompilerParams(dimension_semantics=("arbitrary",), vmem_limit_bytes=VMEM_LIMIT),
    )(g_all, d0, d1, dgl, dv, dz, lb, cum01)


def _b1_in_bwd(idx1, ctx, x, dx1, dg, nw, msel, win):
    last_s = NDEV - 1

    def body(idx_ref, ctx_ref, x_ref, dx1_ref, dg_ref, nw_ref, m_ref, w_ref, gx_ref, rwin_o, dmx_o, dmc_o, gnw_o,
             hx_sc, dhx_sc, acc, sbuf, pbuf, psend, precv, isend, irecv, sibsem, lsem):
        del idx_ref
        s, i = pl.program_id(0), pl.program_id(1)
        x, y, cc, idx = _mesh_pos()
        shift, scale = m_ref[0, 0:1, :], m_ref[0, 1:2, :]
        sibling = (x, y, 1 - cc)

        def partial(p):
            return pltpu.make_async_remote_copy(src_ref=sbuf.at[0], dst_ref=pbuf.at[p], send_sem=psend.at[p], recv_sem=precv.at[p],
                                                device_id=sibling, device_id_type=MESH)

        def chip_sum(p):
            return pltpu.make_async_remote_copy(src_ref=sbuf.at[1], dst_ref=rwin_o.at[2 + p], send_sem=isend.at[p], recv_sem=irecv.at[p],
                                                device_id=_peer(x, y, cc, 2 * (p + 1)), device_id_type=MESH)

        to_sibling = pltpu.make_async_remote_copy(src_ref=sbuf.at[0], dst_ref=rwin_o.at[1], send_sem=sibsem.at[0], recv_sem=sibsem.at[1],
                                                  device_id=sibling, device_id_type=MESH)
        own = pltpu.make_async_copy(sbuf.at[1], rwin_o.at[0], lsem)

        @pl.when((s == 0) & (i == 0))
        def _():
            for ref in (dmx_o, dmc_o, gnw_o):
                ref[...] = jnp.zeros_like(ref)

        @pl.when(s == 0)
        def _():
            hx, _, _, _ = _modulated(_ctx_or_x(i, ctx_ref, x_ref), nw_ref[...], shift, scale)
            hx_sc[i] = hx.astype(BF16)

        @pl.when(i == 0)
        def _():
            acc[...] = jnp.zeros_like(acc)

        dgb = dg_ref[...]
        hxb = hx_sc[i]
        for lo, hi in ((0, 256), (256, 512), (512, SH_WIN)):
            acc[:, lo:hi] += _dot_ta(hxb, dgb[:, lo:hi])
        part = _dot_tb(dgb, w_ref[...])

        @pl.when(s == 0)
        def _():
            dhx_sc[i] = part

        @pl.when(s > 0)
        def _():
            dhx_sc[i] += part

        for p in (2, 1, 0):
            @pl.when((i == NT - 1) & (s == 2 * (2 - p)))
            def _(p=p):
                if p < 2:
                    partial(p + 1).wait_send()
                sbuf[0] = acc[...].astype(BF16)
                partial(p).start()

            @pl.when((i == NT - 1) & (s == 2 * (2 - p) + 1))
            def _(p=p):
                if p < 2:
                    chip_sum(p + 1).wait_send()
                partial(p).wait_recv()
                sbuf[1] = (acc[...] + pbuf[p].astype(F32)).astype(BF16)
                chip_sum(p).start()

        @pl.when((i == NT - 1) & (s == last_s - 1))
        def _():
            partial(0).wait_send()
            sbuf[0] = acc[...].astype(BF16)
            to_sibling.start()

        @pl.when((i == NT - 1) & (s == last_s))
        def _():
            chip_sum(0).wait_send()
            sbuf[1] = acc[...].astype(BF16)
            own.start()

        @pl.when(s == last_s)
        def _():
            nw = nw_ref[...]
            _, r, xn, a = _modulated(_ctx_or_x(i, ctx_ref, x_ref), nw, shift, scale)
            dhx = dhx_sc[i]
            dsh, dsc = _colsum(dhx), _colsum(dhx * a)
            da = dhx * (1.0 + scale)
            gnw_o[...] += _colsum(da * xn)
            dxn = da * nw
            gx_ref[...] = dx1_ref[...] + r * (dxn - xn * jnp.mean(dxn * xn, axis=-1, keepdims=True))

            @pl.when(i == 0)
            def _():
                dmc_o[0:1, :] += dsh
                dmc_o[1:2, :] += dsc

            @pl.when(i > 0)
            def _():
                dmx_o[0:1, :] += dsh
                dmx_o[1:2, :] += dsc

        @pl.when((i == NT - 1) & (s == last_s))
        def _():
            to_sibling.wait_send()
            to_sibling.wait_recv()
            for p in range(3):
                chip_sum(p).wait_recv()
            own.wait()

    grid_spec = pltpu.PrefetchScalarGridSpec(
        num_scalar_prefetch=1, grid=(NDEV, NT),
        in_specs=[VMEM_SPEC, pl.BlockSpec((TM, D), lambda s, i, ix: (jnp.maximum(i - 1, 0), 0)),
                  pl.BlockSpec((TM, D), lambda s, i, ix: (jnp.maximum(i - 1, 0), 0)),
                  pl.BlockSpec((TM, SH_WIN), lambda s, i, ix: (i, ix[0] ^ (last_s - s))), VMEM_SPEC,
                  pl.BlockSpec((1, 2, D), lambda s, i, ix: (jnp.minimum(i, 1), 0, 0)),
                  pl.BlockSpec((D, SH_WIN), lambda s, i, ix: (0, ix[0] ^ (last_s - s)))],
        out_specs=[pl.BlockSpec((TM, D), lambda s, i, ix: (jnp.where(s == last_s, jnp.maximum(i - 1, 0), 0), 0)),
                   HBM_SPEC, VMEM_SPEC, VMEM_SPEC, VMEM_SPEC],
        scratch_shapes=[pltpu.VMEM((NT, TM, D), BF16), pltpu.VMEM((NT, TM, D), F32), pltpu.VMEM((D, SH_WIN), F32),
                        pltpu.VMEM((2, D, SH_WIN), BF16), pltpu.VMEM((3, D, SH_WIN), BF16),
                        pltpu.SemaphoreType.DMA((3,)), pltpu.SemaphoreType.DMA((3,)), pltpu.SemaphoreType.DMA((3,)),
                        pltpu.SemaphoreType.DMA((3,)), pltpu.SemaphoreType.DMA((2,)), pltpu.SemaphoreType.DMA])
    return pl.pallas_call(
        body, name="b1_in_bwd", grid_spec=grid_spec,
        out_shape=(_sds((T, D), F32), _sds((RS_SLOTS, D, SH_WIN), BF16), _sds((2, D), F32), _sds((2, D), F32), _sds((1, D), F32)),
        compiler_params=pltpu.CompilerParams(dimension_semantics=("arbitrary", "arbitrary"), vmem_limit_bytes=VMEM_LIMIT),
    )(idx1, ctx, x, dx1, dg, nw, msel, win)


def _reduce_small(pd, pv, cg, c_ctx, ada_w0):
    n_arr = 3

    def body(pd_r, pv_r, cg_r, cctx_r, ada_r, gada_o, gadab_o, gcctx_o, pvsum_o, loss_o,
             pd_all, pv_all, dsc_all, dsc_mine, ssem, rsem):
        x, y, cc, idx = _mesh_pos()
        srcs = [pd_r, pv_r, dsc_mine]
        dsts = [pd_all.at[idx], pv_all.at[idx], dsc_all.at[idx]]

        def remote(a, k):
            return pltpu.make_async_remote_copy(src_ref=srcs[a], dst_ref=dsts[a], send_sem=ssem.at[a, k], recv_sem=rsem.at[a, k],
                                                device_id=_peer(x, y, cc, k), device_id_type=MESH)

        first = [remote(a, k) for k in range(1, NDEV) for a in (0, 1)]
        for cp in first:
            cp.start()
        pd_all[idx] = pd_r[...]
        pv_all[idx] = pv_r[...]
        for k in range(1, NDEV):
            remote(0, k).wait_recv()
            remote(1, k).wait_recv()
        mine = [pd_all[s, :, pl.ds(idx, 1), :] for s in range(NDEV)]
        dmc = functools.reduce(lambda u, v: u + v, [m[2] for m in mine])
        rows = _stack_rows([cg_r[i] for i in range(NDEV)] + [cctx_r[...]])
        sc = (rows * _sigmoid(rows)).astype(BF16)
        gada_o[0] = _dot_ta(sc, _stack_rows([m[0] for m in mine] + [dmc]))
        gada_o[1] = _dot_ta(sc, _stack_rows([m[1] for m in mine]))
        dsc_mine[...] = _dot_tb(jnp.broadcast_to(dmc, (8, SH_ADA)), ada_r[...])[0:1, :]
        dsc_all[idx] = dsc_mine[...]
        second = [remote(2, k) for k in range(1, NDEV)]
        for cp in second:
            cp.start()
        tot = [functools.reduce(lambda u, v: u + v, [pd_all[s, l] for s in range(NDEV)]) for l in range(3)]
        gadab_o[0] = tot[0] + tot[2]
        gadab_o[1] = tot[1]
        pvs = functools.reduce(lambda u, v: u + v, [pv_all[s] for s in range(NDEV)])
        pvsum_o[...] = pvs
        loss_o[...] = jnp.broadcast_to(jnp.sum(pvs[:, PV_LOSS:PV_LOSS + D], axis=-1, keepdims=True) * (0.5 / D), (1, 128))
        for k in range(1, NDEV):
            remote(2, k).wait_recv()
        dsc = functools.reduce(lambda u, v: u + v, [dsc_all[s] for s in range(NDEV)])
        cx = cctx_r[...]
        sx = _sigmoid(cx)
        gcctx_o[...] = dsc * (sx * (1.0 + cx * (1.0 - sx)))
        for cp in first + second:
            cp.wait_send()

    outs = (_sds((2, D, SH_ADA), F32), _sds((2, NDEV, SH_ADA), F32), _sds((1, D), F32), _sds((1, PV_LEN), F32), _sds((1, 128), F32))
    return pl.pallas_call(
        body, name="reduce_small", out_shape=outs,
        in_specs=[VMEM_SPEC] * 5, out_specs=[VMEM_SPEC] * 5,
        scratch_shapes=[
            pltpu.VMEM((NDEV, 3, NDEV, SH_ADA), F32), pltpu.VMEM((NDEV, 1, PV_LEN), F32), pltpu.VMEM((NDEV, 1, D), F32),
            pltpu.VMEM((1, D), F32),
            pltpu.SemaphoreType.DMA((n_arr, NDEV)), pltpu.SemaphoreType.DMA((n_arr, NDEV)),
        ],
        compiler_params=pltpu.CompilerParams(vmem_limit_bytes=VMEM_LIMIT),
    )(pd, pv, cg, c_ctx, ada_w0)


PV_NW, PV_GNORM, PV_FINAL, PV_LB, PV_PSCALE, PV_LOSS, PV_LEN = 0, 2 * D, 3 * D, 4 * D, 6 * D, 7 * D, 8 * D


def _adamw(w, g, m, v):
    m = ADAM_B1 * m + (1.0 - ADAM_B1) * g
    v = ADAM_B2 * v + (1.0 - ADAM_B2) * (g * g)
    m_hat = m / (1.0 - ADAM_B1 ** ADAM_STEP)
    v_hat = v / (1.0 - ADAM_B2 ** ADAM_STEP)
    delta = -ADAM_LR * (m_hat / (jnp.sqrt(v_hat) + ADAM_EPS) + ADAM_WD * w)
    return delta, m, v


def _adam_sharded(name, parts, w, m, v, tr):
    rr, cc = w.shape
    n = parts.shape[0]

    def body(p_ref, w_ref, m_ref, v_ref, g_o, d_o, m_o, v_o):
        g = p_ref[0].astype(F32)
        for s in range(1, n):
            g = g + p_ref[s].astype(F32)
        d, mn, vn = _adamw(w_ref[...], g, m_ref[...], v_ref[...])
        g_o[...], d_o[...], m_o[...], v_o[...] = g, d, mn, vn

    blk = pl.BlockSpec((tr, cc), lambda i: (i, 0))
    return pl.pallas_call(
        body, name=name, grid=(rr // tr,),
        in_specs=[pl.BlockSpec((n, tr, cc), lambda i: (0, i, 0)), blk, blk, blk],
        out_specs=[blk] * 4, out_shape=(_sds((rr, cc), F32),) * 4,
        compiler_params=pltpu.CompilerParams(dimension_semantics=("arbitrary",)),
    )(parts, w, m, v)


def _adam_dense(name, g, w, m, v, tr):
    rr, cc = w.shape

    def body(g_ref, w_ref, m_ref, v_ref, d_o, m_o, v_o):
        d, mn, vn = _adamw(w_ref[...], g_ref[...], m_ref[...], v_ref[...])
        d_o[...], m_o[...], v_o[...] = d, mn, vn

    blk = pl.BlockSpec((tr, cc), lambda i: (i, 0))
    return pl.pallas_call(
        body, name=name, grid=(rr // tr,), in_specs=[blk] * 4, out_specs=[blk] * 3, out_shape=(_sds((rr, cc), F32),) * 3,
        compiler_params=pltpu.CompilerParams(dimension_semantics=("arbitrary",)),
    )(g, w, m, v)


def _adam_small(gs, ws, ms, vs, lb_idx, lbv):
    n = len(ws)

    def body(*refs):
        g_r, w_r, m_r, v_r = refs[:n], refs[n:2 * n], refs[2 * n:3 * n], refs[3 * n:4 * n]
        lb_r = refs[4 * n]
        outs = refs[4 * n + 1:]
        for j in range(n):
            g = g_r[j][...]
            if j == lb_idx:
                lbj = lb_r[...]
                g = g * lbj * (1.0 - lbj)
            d, mn, vn = _adamw(w_r[j][...], g, m_r[j][...], v_r[j][...])
            outs[j][...], outs[n + j][...], outs[2 * n + j][...], outs[3 * n + j][...] = g, d, mn, vn

    shapes = tuple(_sds(w.shape, F32) for w in ws)
    return pl.pallas_call(body, name="adam_small", out_shape=shapes * 4)(*gs, *ws, *ms, *vs, lbv)


def kernel(x, c, ctx, c_ctx, ada_w, ada_b, norm_w, hgrn_w_in, hgrn_lb_logits, hgrn_gnorm_w, hgrn_w_out, pool_w_in, pool_w_grp, pool_scale, pool_w_out, final_norm_w, loss_target, m_c_ctx, m_ada_w, m_ada_b, m_norm_w, m_hgrn_w_in, m_hgrn_lb_logits, m_hgrn_gnorm_w, m_hgrn_w_out, m_pool_w_in, m_pool_w_grp, m_pool_scale, m_pool_w_out, m_final_norm_w, v_c_ctx, v_ada_w, v_ada_b, v_norm_w, v_hgrn_w_in, v_hgrn_lb_logits, v_hgrn_gnorm_w, v_hgrn_w_out, v_pool_w_in, v_pool_w_grp, v_pool_scale, v_pool_w_out, v_final_norm_w):
    idx = 4 * lax.axis_index("x") + 2 * lax.axis_index("y") + lax.axis_index("c")
    cctx2 = c_ctx.reshape(1, D)
    cum01, mask01 = _gla_consts()
    pb, pbt, pinv = _pool_consts()

    idx1 = idx.reshape(1).astype(jnp.int32)
    s_win, s_wout, s_pwin, s_pgrp, s_pwout, lbl_g, ps_g, cg, mod_g = _gather_small(
        hgrn_w_in[0], hgrn_w_out[0], pool_w_in[0], pool_w_grp[0], pool_w_out[0], hgrn_lb_logits[0], pool_scale, c, cctx2, ada_w)
    lb = jax.nn.sigmoid(jnp.transpose(lbl_g, (1, 0, 2)).reshape(2, E))
    pscale = ps_g.reshape(1, E)
    mod_all = jnp.transpose(mod_g, (1, 2, 0, 3)).reshape(2, 16, 3 * D) + ada_b[:, None, :]
    mod_me = lax.dynamic_index_in_dim(mod_all, idx, axis=1, keepdims=False)
    mod0, mod1, modc = mod_me[0].reshape(3, D), mod_me[1].reshape(3, D), mod_all[0, NDEV].reshape(3, D)
    msel = jnp.stack([modc[:2], mod0[:2]])
    nw0, nw1 = norm_w[0:1], norm_w[1:2]
    fnw = final_norm_w.reshape(1, D)

    g_all, win = _f1_gather_matmul(idx1, ctx[0], x[0], nw0, msel, s_win)
    p0, p1, v_all, dec = _gla_prep(g_all, lb, cum01)
    o, wout, pwin, pgrp, pwout = _gla_fwd(p0, p1, v_all, dec, mask01, s_wout, s_pwin, s_pgrp, s_pwout)
    x1 = _f3_out(o, g_all, x[0], mod0[2:3], hgrn_gnorm_w, wout)
    dx1, gpwin, gpgrp, gpwout, dmod1, gnw1, gfw, gps, lossv = _pool_layer(
        x1, loss_target[0], mod1, nw1, fnw, pwin, pgrp, pscale, pwout, pb, pbt, pinv)
    do, dz, gwout, dgate0, ggw = _b3_out_bwd(dx1, o, g_all, mod0[2:3], hgrn_gnorm_w, wout)
    d0, d1, dv, dgl, rwout, rpwin, rpgrp, rpwout = _gla_bwd(p0, p1, v_all, dec, do, mask01, gwout, gpwin, gpgrp, gpwout)
    dg, dlb = _gla_post_bwd(g_all, d0, d1, dgl, dv, dz, lb, cum01)
    grad_x, rwin, dmx, dmc, gnw0 = _b1_in_bwd(idx1, ctx[0], x[0], dx1, dg, nw0, msel, win)

    dmod0 = jnp.concatenate([dmx, dgate0], axis=0)
    dmodc = jnp.concatenate([dmc, jnp.zeros((1, D), F32)], axis=0)
    pd = jnp.stack([dmod0, dmod1, dmodc]).reshape(3, NDEV, SH_ADA)
    pv = jnp.concatenate([gnw0, gnw1, ggw, gfw, dlb.reshape(1, 2 * E), gps, lossv], axis=1)
    g_ada, g_adab, g_cctx, pvsum, loss128 = _reduce_small(pd, pv, cg, cctx2, ada_w[0])

    out = {}
    out["hgrn_w_in"] = _adam_sharded("adam_w_in", rwin, hgrn_w_in[0], m_hgrn_w_in[0], v_hgrn_w_in[0], 256)
    out["hgrn_w_out"] = _adam_sharded("adam_w_out", rwout, hgrn_w_out[0], m_hgrn_w_out[0], v_hgrn_w_out[0], SH_ROWS)
    out["pool_w_in"] = _adam_sharded("adam_pw_in", rpwin, pool_w_in[0], m_pool_w_in[0], v_pool_w_in[0], 512)
    out["pool_w_grp"] = _adam_sharded("adam_pgrp", rpgrp.reshape(NDEV, 4 * SH_GRP, PG), pool_w_grp[0].reshape(4 * SH_GRP, PG),
                                      m_pool_w_grp[0].reshape(4 * SH_GRP, PG), v_pool_w_grp[0].reshape(4 * SH_GRP, PG), 4 * SH_GRP)
    out["pool_w_out"] = _adam_sharded("adam_pw_out", rpwout, pool_w_out[0], m_pool_w_out[0], v_pool_w_out[0], SH_ROWS)
    g_ada2 = g_ada.reshape(2 * D, SH_ADA)
    out["ada_w"] = (g_ada2,) + _adam_dense("adam_ada_w", g_ada2, ada_w.reshape(2 * D, SH_ADA), m_ada_w.reshape(2 * D, SH_ADA),
                                           v_ada_w.reshape(2 * D, SH_ADA), 512)

    lb_me = lax.dynamic_slice_in_dim(lb, idx * DH, DH, axis=1)
    small = ["c_ctx", "ada_b", "norm_w", "hgrn_lb_logits", "hgrn_gnorm_w", "pool_scale", "final_norm_w"]
    gs = [g_cctx, g_adab.reshape(2, 3 * D), pvsum[:, PV_NW:PV_NW + 2 * D].reshape(2, D),
          lax.dynamic_slice_in_dim(pvsum[:, PV_LB:PV_LB + 2 * E].reshape(2, E), idx * DH, DH, axis=1),
          pvsum[:, PV_GNORM:PV_GNORM + E], lax.dynamic_slice_in_dim(pvsum[:, PV_PSCALE:PV_PSCALE + E], idx * DH, DH, axis=1),
          pvsum[:, PV_FINAL:PV_FINAL + D]]
    ws = [cctx2, ada_b, norm_w, hgrn_lb_logits[0], hgrn_gnorm_w, pool_scale, fnw]
    ms = [m_c_ctx.reshape(1, D), m_ada_b, m_norm_w, m_hgrn_lb_logits[0], m_hgrn_gnorm_w, m_pool_scale, m_final_norm_w.reshape(1, D)]
    vs = [v_c_ctx.reshape(1, D), v_ada_b, v_norm_w, v_hgrn_lb_logits[0], v_hgrn_gnorm_w, v_pool_scale, v_final_norm_w.reshape(1, D)]
    res = _adam_small(gs, ws, ms, vs, 3, lb_me)
    n = len(small)
    for j, name in enumerate(small):
        out[name] = tuple(res[q * n + j] for q in range(4))

    shapes = {"c_ctx": (D,), "ada_w": (2, D, SH_ADA), "ada_b": (2, 3 * D), "norm_w": (2, D), "hgrn_w_in": (1, D, SH_WIN),
              "hgrn_lb_logits": (1, 2, DH), "hgrn_gnorm_w": (1, E), "hgrn_w_out": (1, SH_ROWS, D), "pool_w_in": (1, D, SH_PWIN),
              "pool_w_grp": (1, 4, SH_GRP, PG), "pool_scale": (1, DH), "pool_w_out": (1, SH_ROWS, D), "final_norm_w": (D,)}
    order = ["c_ctx", "ada_w", "ada_b", "norm_w", "hgrn_w_in", "hgrn_lb_logits", "hgrn_gnorm_w", "hgrn_w_out", "pool_w_in",
             "pool_w_grp", "pool_scale", "pool_w_out", "final_norm_w"]
    flat = [out[name][q].reshape(shapes[name]) for q in range(4) for name in order]
    return (loss128[0, 0], grad_x[None], *flat)
```

```python
import functools

import numpy as np
import jax
import jax.numpy as jnp
from jax import lax
from jax.experimental import pallas as pl
from jax.experimental.pallas import tpu as pltpu

F32 = jnp.float32
BF16 = jnp.bfloat16

D = 1024
E = 1024
HEADS = 8
DH = 128
CHUNK = 64
T = 2048
TC = 256
TT = T + TC
TM = 256
NT = TT // TM
NTX = T // TM
NDEV = 8
GRID_W = 64
POOL_WINDOWS = (2, 4, 8, 16)
PG = 256
EPS = 1e-6
WIN_COLS = 5 * E
SH_WIN = WIN_COLS // NDEV
SH_PWIN = 2 * E // NDEV
SH_ROWS = E // NDEV
SH_GRP = PG // NDEV
SH_ADA = 3 * D // NDEV
VMEM_LIMIT = 56 * 1024 * 1024
VMEM_LIMIT_SCAN = 60 * 1024 * 1024

ADAM_LR, ADAM_B1, ADAM_B2, ADAM_EPS, ADAM_WD, ADAM_STEP = 0.001, 0.9, 0.999, 1e-08, 0.01, 10

MESH = pl.DeviceIdType.MESH
VMEM_SPEC = pl.BlockSpec(memory_space=pltpu.VMEM)
HBM_SPEC = pl.BlockSpec(memory_space=pltpu.HBM)
ANY_SPEC = pl.BlockSpec(memory_space=pl.ANY)


def _sds(shape, dtype):
    return jax.ShapeDtypeStruct(shape, dtype)


def _bf(a):
    return a if a.dtype == BF16 else a.astype(BF16)


def _dot(a, b):
    return lax.dot_general(_bf(a), _bf(b), (((1,), (0,)), ((), ())), preferred_element_type=F32)


def _dot_tb(a, b):
    return lax.dot_general(_bf(a), _bf(b), (((1,), (1,)), ((), ())), preferred_element_type=F32)


def _dot_ta(a, b):
    return lax.dot_general(_bf(a), _bf(b), (((0,), (0,)), ((), ())), preferred_element_type=F32)


def _bdot(a, b):
    return lax.dot_general(_bf(a), _bf(b), (((2,), (1,)), ((0,), (0,))), preferred_element_type=F32)


def _bdot_nt(a, b):
    return lax.dot_general(_bf(a), _bf(b), (((2,), (2,)), ((0,), (0,))), preferred_element_type=F32)


def _bdot_tn(a, b):
    return lax.dot_general(_bf(a), _bf(b), (((1,), (1,)), ((0,), (0,))), preferred_element_type=F32)


def _dot01(m01, x):
    hi = x.astype(BF16)
    lo = (x - hi.astype(F32)).astype(BF16)
    return _dot(m01, hi) + _dot(m01, lo)


def _rstd(x):
    return lax.rsqrt(jnp.mean(x * x, axis=-1, keepdims=True) + EPS)


def _sigmoid(x):
    return jax.nn.sigmoid(x)


def _colsum(a):
    return jnp.sum(a, axis=0, keepdims=True)


def _stack_rows(rows):
    n = rows[0].shape[-1]
    rid = lax.broadcasted_iota(jnp.int32, (16, n), 0)
    out = jnp.zeros((16, n), F32)
    for i, r in enumerate(rows):
        out = jnp.where(rid == i, r, out)
    return out


def _head_map(fn, *arrs):
    outs = [fn(*[a[:, h * DH:(h + 1) * DH] for a in arrs]) for h in range(HEADS)]
    return jnp.concatenate(outs, axis=1)


def _gla_consts():
    r = np.arange(TM)[:, None]
    c = np.arange(TM)[None, :]
    same = (r // CHUNK) == (c // CHUNK)
    tril = same & (c <= r)
    triu = same & (c >= r)
    m = np.stack([tril, triu]).astype(np.float32)
    return jnp.asarray(m, BF16), jnp.asarray(m, F32)


def _pool_consts():
    r = np.arange(TM)[:, None]
    c = np.arange(TM)[None, :]
    same = (r // GRID_W) == (c // GRID_W)
    rp, cp = r % GRID_W, c % GRID_W
    bs, inv = [], []
    for w in POOL_WINDOWS:
        lo = np.clip(rp - w // 2, 0, GRID_W)
        hi = np.clip(rp - w // 2 + w, 0, GRID_W)
        bs.append(same & (cp >= lo) & (cp < hi))
        inv.append(1.0 / (hi - lo).astype(np.float32))
    b = np.stack(bs).astype(np.float32)
    bt = np.transpose(b, (0, 2, 1))
    return jnp.asarray(b, BF16), jnp.asarray(bt, BF16), jnp.asarray(np.stack(inv), F32)


def _mesh_pos():
    x, y, c = lax.axis_index("x"), lax.axis_index("y"), lax.axis_index("c")
    return x, y, c, 4 * x + 2 * y + c


def _peer(x, y, c, k):
    return (x ^ ((k >> 2) & 1), y ^ ((k >> 1) & 1), c ^ (k & 1))


def _gather_small(w_in, w_out, pw_in, pgrp, pw_out, lb_l, pscale, c, c_ctx, ada_w):
    n_arr = 4

    def body(win_r, wout_r, pwin_r, pgrp_r, pwout_r, lb_r, ps_r, c_r, cctx_r, ada_r,
             s_win, s_wout, s_pwin, s_pgrp, s_pwout, lb_o, ps_o, cg_o, mod_o, ssem, rsem):
        x, y, cc, idx = _mesh_pos()
        srcs = [lb_r, ps_r, c_r, mod_o.at[idx]]
        mine = [lb_o.at[idx], ps_o.at[idx], cg_o.at[idx], mod_o.at[idx]]

        def remote(a, k):
            return pltpu.make_async_remote_copy(src_ref=srcs[a], dst_ref=mine[a], send_sem=ssem.at[a, k], recv_sem=rsem.at[a, k],
                                                device_id=_peer(x, y, cc, k), device_id_type=MESH)

        first = [remote(a, k) for k in range(1, NDEV) for a in (2, 0, 1)]
        for cp in first:
            cp.start()
        lb_o[idx] = lb_r[...]
        ps_o[idx] = ps_r[...]
        cg_o[idx] = c_r[...]
        s_win[...] = win_r[...].astype(BF16)
        s_wout[...] = wout_r[...].astype(BF16)
        s_pwin[...] = pwin_r[...].astype(BF16)
        s_pgrp[...] = pgrp_r[...].astype(BF16)
        s_pwout[...] = pwout_r[...].astype(BF16)
        for k in range(1, NDEV):
            remote(2, k).wait_recv()
        rows = _stack_rows([cg_o[i] for i in range(NDEV)] + [cctx_r[...]])
        sc = rows * _sigmoid(rows)
        for l in range(2):
            mod_o[idx, l] = _dot(sc, ada_r[l])
        second = [remote(3, k) for k in range(1, NDEV)]
        for cp in second:
            cp.start()
        for cp in first + second:
            cp.wait_send()
        for k in range(1, NDEV):
            for a in (0, 1, 3):
                remote(a, k).wait_recv()

    outs = (
        _sds((D, SH_WIN), BF16), _sds((SH_ROWS, D), BF16), _sds((D, SH_PWIN), BF16), _sds((4, SH_GRP, PG), BF16), _sds((SH_ROWS, D), BF16),
        _sds((NDEV, 2, DH), F32), _sds((NDEV, 1, DH), F32), _sds((NDEV, 1, D), F32), _sds((NDEV, 2, 16, SH_ADA), F32),
    )
    return pl.pallas_call(
        body, name="gather_small", out_shape=outs,
        in_specs=[VMEM_SPEC] * 10, out_specs=[VMEM_SPEC] * 9,
        scratch_shapes=[pltpu.SemaphoreType.DMA((n_arr, NDEV)), pltpu.SemaphoreType.DMA((n_arr, NDEV))],
        compiler_params=pltpu.CompilerParams(vmem_limit_bytes=VMEM_LIMIT),
    )(w_in, w_out, pw_in, pgrp, pw_out, lb_l, pscale, c, c_ctx, ada_w)


def _gather_order(s):
    if isinstance(s, int):
        return (0, 1, 2, 4, 3, 5, 6, 7)[s]
    return s + (s == 3).astype(jnp.int32) - (s == 4).astype(jnp.int32)


GATHER_ISSUE = (1, 2, 4, 3, 5, 6, 7)
GATHER_ICI = (2, 4, 6)
GATHER_DIRECT = (1,) + GATHER_ICI
GLA_HB = 2
RS_SLOTS = 5


def _shard_of(kind, ref, i):
    if kind == "rows":
        return ref.at[pl.ds(pl.multiple_of(i * SH_ROWS, SH_ROWS), SH_ROWS), :]
    if kind == "cols":
        return ref.at[:, pl.ds(pl.multiple_of(i * SH_PWIN, 128), SH_PWIN)]
    assert kind == "grp"
    return ref.at[:, pl.ds(pl.multiple_of(i * SH_GRP, SH_GRP), SH_GRP), :]


def _gather_rider(step, n_steps, forward_at, kinds, srcs, outs, ssem, rsem, lsem):
    x, y, cc, idx = _mesh_pos()
    arrays = range(len(kinds))
    mine = [_shard_of(kinds[a], outs[a], idx) for a in arrays]

    def remote(a, k):
        return pltpu.make_async_remote_copy(src_ref=srcs[a], dst_ref=mine[a], send_sem=ssem.at[a, k], recv_sem=rsem.at[a, k],
                                            device_id=_peer(x, y, cc, k), device_id_type=MESH)

    def forward(a, k):
        blk = _shard_of(kinds[a], outs[a], idx ^ k)
        return pltpu.make_async_remote_copy(src_ref=blk, dst_ref=blk, send_sem=ssem.at[a, k ^ 1], recv_sem=rsem.at[a, k ^ 1],
                                            device_id=(x, y, 1 - cc), device_id_type=MESH)

    copies = [remote(a, k) for k in GATHER_DIRECT for a in arrays]
    passed = [forward(a, k) for k in GATHER_ICI for a in arrays]
    local = [pltpu.make_async_copy(srcs[a], mine[a], lsem.at[a]) for a in arrays]

    @pl.when(step == 0)
    def _():
        for cp in copies + local:
            cp.start()

    @pl.when(step == forward_at)
    def _():
        for k in GATHER_ICI:
            for a in arrays:
                remote(a, k).wait_recv()
                forward(a, k).start()

    @pl.when(step == n_steps - 1)
    def _():
        for cp in copies + passed:
            cp.wait_send()
        for a in arrays:
            remote(a, 1).wait_recv()
        for cp in passed:
            cp.wait_recv()
        for cp in local:
            cp.wait()


def _scatter_rider(step, n_steps, kinds, grads, slots, ssem, rsem, lsem):
    x, y, cc, idx = _mesh_pos()
    arrays = range(len(kinds))
    dsts = [slots[a].at[idx] for a in arrays]

    def remote(a, k):
        px, py, pc = _peer(x, y, cc, k)
        return pltpu.make_async_remote_copy(src_ref=_shard_of(kinds[a], grads[a], 4 * px + 2 * py + pc), dst_ref=dsts[a],
                                            send_sem=ssem.at[a, k], recv_sem=rsem.at[a, k], device_id=(px, py, pc), device_id_type=MESH)

    copies = [remote(a, k) for k in GATHER_ISSUE for a in arrays]
    local = [pltpu.make_async_copy(_shard_of(kinds[a], grads[a], idx), dsts[a], lsem.at[a]) for a in arrays]

    @pl.when(step == 0)
    def _():
        for cp in copies + local:
            cp.start()

    @pl.when(step == n_steps - 1)
    def _():
        for cp in copies:
            cp.wait_send()
        for cp in copies:
            cp.wait_recv()
        for cp in local:
            cp.wait()


def _rider_sems(n):
    return [pltpu.SemaphoreType.DMA((n, NDEV)), pltpu.SemaphoreType.DMA((n, NDEV)), pltpu.SemaphoreType.DMA((n,))]


def _modulated(x, nw, shift, scale):
    r = _rstd(x)
    xn = x * r
    a = xn * nw
    return a * (1.0 + scale) + shift, r, xn, a


def _ctx_or_x(i, ctx_ref, x_ref):
    return jnp.where(i == 0, ctx_ref[...], x_ref[...])


def _f1_gather_matmul(idx1, ctx, x, nw, msel, s_win):
    def body(idx_ref, ctx_ref, x_ref, nw_ref, m_ref, sw_ref, g_ref, win_o, wslot, hx_sc, ssem, rsem, lsem, osem):
        del idx_ref
        s, i = pl.program_id(0), pl.program_id(1)
        x, y, cc, idx = _mesh_pos()
        k = _gather_order(s)
        j = idx ^ k

        def remote(kk):
            return pltpu.make_async_remote_copy(src_ref=sw_ref, dst_ref=wslot.at[idx], send_sem=ssem.at[kk], recv_sem=rsem.at[kk],
                                                device_id=_peer(x, y, cc, kk), device_id_type=MESH)

        def forward(kk):
            jj = idx ^ kk
            return pltpu.make_async_remote_copy(src_ref=wslot.at[jj], dst_ref=wslot.at[jj], send_sem=ssem.at[kk ^ 1],
                                                recv_sem=rsem.at[kk ^ 1], device_id=(x, y, 1 - cc), device_id_type=MESH)

        own = pltpu.make_async_copy(sw_ref, wslot.at[idx], lsem)

        def to_hbm(jj, kk):
            return pltpu.make_async_copy(wslot.at[jj], win_o.at[:, pl.ds(pl.multiple_of(jj * SH_WIN, 128), SH_WIN)], osem.at[kk])

        @pl.when((s == 0) & (i == 0))
        def _():
            own.start()
            for kk in GATHER_DIRECT:
                remote(kk).start()
            own.wait()

        @pl.when(s == 0)
        def _():
            hx, _, _, _ = _modulated(_ctx_or_x(i, ctx_ref, x_ref), nw_ref[...], m_ref[0, 0:1, :], m_ref[0, 1:2, :])
            hx_sc[i] = hx.astype(BF16)

        @pl.when((s > 0) & (i == 0))
        def _():
            remote(k).wait_recv()

            @pl.when((k & 1) == 0)
            def _():
                forward(k).start()

        @pl.when(i == 0)
        def _():
            to_hbm(j, k).start()

        g_ref[...] = jnp.dot(hx_sc[i], wslot[j], preferred_element_type=F32)

        @pl.when((s == NDEV - 1) & (i == NT - 1))
        def _():
            for kk in GATHER_DIRECT:
                remote(kk).wait_send()
            for kk in GATHER_ICI:
                forward(kk).wait_send()
            for kk in range(NDEV):
                to_hbm(idx ^ kk, kk).wait()

    grid_spec = pltpu.PrefetchScalarGridSpec(
        num_scalar_prefetch=1, grid=(NDEV, NT),
        in_specs=[VMEM_SPEC, pl.BlockSpec((TM, D), lambda s, i, ix: (jnp.maximum(i - 1, 0), 0)), VMEM_SPEC,
                  pl.BlockSpec((1, 2, D), lambda s, i, ix: (jnp.minimum(i, 1), 0, 0)), HBM_SPEC],
        out_specs=[pl.BlockSpec((TM, SH_WIN), lambda s, i, ix: (i, ix[0] ^ _gather_order(s))), HBM_SPEC],
        scratch_shapes=[pltpu.VMEM((NDEV, D, SH_WIN), BF16), pltpu.VMEM((NT, TM, D), BF16),
                        pltpu.SemaphoreType.DMA((NDEV,)), pltpu.SemaphoreType.DMA((NDEV,)), pltpu.SemaphoreType.DMA,
                        pltpu.SemaphoreType.DMA((NDEV,))])
    return pl.pallas_call(
        body, name="f1_gather_matmul", grid_spec=grid_spec,
        out_shape=(_sds((TT, WIN_COLS), F32), _sds((D, WIN_COLS), BF16)),
        compiler_params=pltpu.CompilerParams(dimension_semantics=("arbitrary", "arbitrary"), vmem_limit_bytes=VMEM_LIMIT),
    )(idx1, ctx, x, nw, msel, s_win)


def _gla_gates(pre, qpre, lbd, cum, rev):
    rows, n = pre.shape
    nch = rows // CHUNK
    sig = _sigmoid(pre)
    f = lbd + (1.0 - lbd) * sig
    k = 1.0 - f
    g = _dot01(cum, jnp.log(f))
    g3 = g.reshape(nch, CHUNK, n)
    last = 0 if rev else CHUNK - 1
    mid = CHUNK // 2 if rev else CHUNK // 2 - 1
    gl1, gm1 = g3[:, last:last + 1, :], g3[:, mid:mid + 1, :]

    def bc(a):
        return jnp.broadcast_to(a, g3.shape).reshape(rows, n)

    gm = bc(gm1)
    e_q, e_k = jnp.exp(g - gm), jnp.exp(gm - g)
    e_in, e_end = e_q * bc(jnp.exp(gm1)), e_k * bc(jnp.exp(gl1 - gm1))
    qsig = _sigmoid(qpre)
    qs = qpre * qsig * (DH ** -0.5)
    return dict(sig=sig, f=f, k=k, qsig=qsig, qs=qs, e_q=e_q, e_k=e_k, e_in=e_in, e_end=e_end,
                decay=[jnp.exp(g3[ci, last:last + 1, :]) for ci in range(nch)])


def _put_heads(ref, lead, arr):
    for h in range(HEADS):
        ref[lead + (h,)] = arr[:, h * DH:(h + 1) * DH]


def _get_heads(ref, lead=()):
    return jnp.concatenate([ref[lead + (h,)] for h in range(HEADS)], axis=1)


def _gla_prep(g_all, lb, cum01, s_pwin, s_pgrp):
    def body(g_ref, lb_ref, cum_ref, spwin_r, spgrp_r, p0_ref, p1_ref, v_ref, dec_ref, pwin_o, pgrp_o, ssem, rsem, lsem):
        _gather_rider(pl.program_id(0), NT, NT // 2 + 1, ("cols", "grp"), (spwin_r, spgrp_r), (pwin_o, pgrp_o), ssem, rsem, lsem)
        qpre = g_ref[:, 3 * E:4 * E]
        _put_heads(v_ref, (), g_ref[:, 2 * E:3 * E].astype(BF16))
        dec_ref[...] = jnp.zeros_like(dec_ref)
        for d, p_ref in ((0, p0_ref), (1, p1_ref)):
            t = _gla_gates(g_ref[:, d * E:(d + 1) * E], qpre, lb_ref[d:d + 1, :], cum_ref[d], d == 1)
            _put_heads(p_ref, (0,), (t["qs"] * t["e_q"]).astype(BF16))
            _put_heads(p_ref, (1,), (t["k"] * t["e_k"]).astype(BF16))
            _put_heads(p_ref, (2,), (t["qs"] * t["e_in"]).astype(BF16))
            _put_heads(p_ref, (3,), (t["k"] * t["e_end"]).astype(BF16))
            for ci in range(TM // CHUNK):
                dec_ref[d, 0, ci:ci + 1, :] = t["decay"][ci]

    quad = pl.BlockSpec((4, HEADS, TM, DH), lambda i: (0, 0, i, 0))
    return pl.pallas_call(
        body, name="gla_prep", grid=(NT,),
        in_specs=[pl.BlockSpec((TM, WIN_COLS), lambda i: (i, 0)), VMEM_SPEC, VMEM_SPEC, HBM_SPEC, HBM_SPEC],
        out_specs=[quad, quad, pl.BlockSpec((HEADS, TM, DH), lambda i: (0, i, 0)), pl.BlockSpec((2, 1, 8, E), lambda i: (0, i, 0, 0)),
                   HBM_SPEC, HBM_SPEC],
        out_shape=(_sds((4, HEADS, TT, DH), BF16), _sds((4, HEADS, TT, DH), BF16), _sds((HEADS, TT, DH), BF16), _sds((2, NT, 8, E), F32),
                   _sds((D, 2 * E), BF16), _sds((4, PG, PG), BF16)),
        scratch_shapes=_rider_sems(2),
        compiler_params=pltpu.CompilerParams(dimension_semantics=("arbitrary",), vmem_limit_bytes=VMEM_LIMIT),
    )(g_all, lb, cum01, s_pwin, s_pgrp)


def _scan_tile(i, rev):
    t = jnp.where(i == 0, 0, NT - i) if rev else i
    return t, pl.ds(pl.multiple_of(t * TM, TM), TM)


def _chunk_order(rev):
    n = TM // CHUNK
    return tuple(range(n - 1, -1, -1)) if rev else tuple(range(n))


def _gla_fwd(p0, p1, v_all, dec, mask01, s_wout, s_pwout):
    n_steps = HEADS // GLA_HB

    def body(p0_ref, p1_ref, v_ref, dec_ref, msk_ref, swout_r, spwout_r, o_ref, wout_o, pwout_o, ob_sc, ssem, rsem, lsem):
        _gather_rider(pl.program_id(0), n_steps, n_steps // 2, ("rows", "rows"), (swout_r, spwout_r), (wout_o, pwout_o), ssem, rsem, lsem)

        lanes = [(d, hh) for d in (0, 1) for hh in range(GLA_HB)]
        nch = TM // CHUNK

        def tile_body(i, st):
            where = [_scan_tile(i, d == 1) for d in (0, 1)]

            def stacked(fn):
                return jnp.stack([fn(d, hh, where[d][1]) for d, hh in lanes])

            qg, kg, q_in, kend = [stacked(lambda d, hh, rows, ty=ty: (p1_ref if d else p0_ref)[ty, hh, rows, :]) for ty in range(4)]
            v = stacked(lambda d, hh, rows: v_ref[hh, rows, :])
            a = _bdot_nt(qg, kg) * jnp.stack([msk_ref[d] for d, _ in lanes])
            intra = _bdot(a, v)
            outs = [[None] * nch for _ in lanes]
            for n in range(nch):
                cis = [nch - 1 - n if d else n for d, _ in lanes]

                def chunk(arr):
                    return jnp.stack([arr[l, ci * CHUNK:(ci + 1) * CHUNK] for l, ci in enumerate(cis)])

                dec = jnp.stack([dec_ref[d, where[d][0], ci:ci + 1, hh * DH:(hh + 1) * DH] for (d, hh), ci in zip(lanes, cis)])
                inter = _bdot_nt(chunk(q_in), st)
                for l, ci in enumerate(cis):
                    outs[l][ci] = inter[l] + intra[l, ci * CHUNK:(ci + 1) * CHUNK]
                st = st * dec + _bdot_tn(chunk(v), chunk(kend))
            for l, (d, hh) in enumerate(lanes):
                (ob_sc if d else o_ref)[hh, where[d][1], :] = jnp.concatenate(outs[l], axis=0)
            return st

        lax.fori_loop(0, NT, tile_body, jnp.zeros((len(lanes), DH, DH), F32))
        o_ref[...] += ob_sc[...]

    quad = pl.BlockSpec((4, GLA_HB, TT, DH), lambda h: (0, h, 0, 0))
    head = pl.BlockSpec((GLA_HB, TT, DH), lambda h: (h, 0, 0))
    return pl.pallas_call(
        body, name="gla_fwd", grid=(n_steps,),
        in_specs=[quad, quad, head, pl.BlockSpec((2, NT, 8, GLA_HB * DH), lambda h: (0, 0, 0, h)),
                  pl.BlockSpec((2, TM, TM), lambda h: (0, 0, 0)), HBM_SPEC, HBM_SPEC],
        out_specs=[head, HBM_SPEC, HBM_SPEC],
        out_shape=(_sds((HEADS, TT, DH), F32), _sds((E, D), BF16), _sds((E, D), BF16)),
        scratch_shapes=[pltpu.VMEM((GLA_HB, TT, DH), F32)] + _rider_sems(2),
        compiler_params=pltpu.CompilerParams(dimension_semantics=("arbitrary",), vmem_limit_bytes=VMEM_LIMIT),
    )(p0, p1, v_all, dec, mask01, s_wout, s_pwout)


def _gated_norm(o, z, gw):
    r = _head_map(lambda oh: jnp.broadcast_to(_rstd(oh), oh.shape), o)
    on = o * r
    zs = _sigmoid(z)
    sz = z * zs
    return on * gw * sz, r, on, zs, sz


def _f3_out(o, g_all, x, gate, gw, wout):
    def body(o_ref, z_ref, x_ref, gate_ref, gw_ref, w_ref, x1_ref):
        og, _, _, _, _ = _gated_norm(_get_heads(o_ref), z_ref[...], gw_ref[...])
        x1_ref[...] = x_ref[...] + gate_ref[...] * _dot(og, w_ref[...])

    return pl.pallas_call(
        body, name="f3_out", grid=(NTX,),
        in_specs=[pl.BlockSpec((HEADS, TM, DH), lambda i: (0, i + 1, 0)), pl.BlockSpec((TM, E), lambda i: (i + 1, 4)),
                  pl.BlockSpec((TM, D), lambda i: (i, 0)), pl.BlockSpec((1, D), lambda i: (0, 0)),
                  pl.BlockSpec((1, E), lambda i: (0, 0)), pl.BlockSpec((E, D), lambda i: (0, 0))],
        out_specs=pl.BlockSpec((TM, D), lambda i: (i, 0)),
        out_shape=_sds((T, D), F32),
        compiler_params=pltpu.CompilerParams(dimension_semantics=("arbitrary",)),
    )(o, g_all, x, gate, gw, wout)


def _pool_layer(x1, tgt, mod1, nw1, fnw, pwin, pgrp, pscale, pwout, pb, pbt, pinv):
    def body(x_ref, t_ref, m_ref, nw_ref, fw_ref, pwin_ref, pgrp_ref, ps_ref, pwout_ref, pb_ref, pbt_ref, pinv_ref,
             dx_ref, gpwin_o, gpgrp_o, gpwout_o, dmod_o, gnw_o, gfw_o, gps_o, loss_o,
             a_pwin, a_pgrp, a_pwout):
        i = pl.program_id(0)

        @pl.when(i == 0)
        def _():
            for ref in (a_pwin, a_pgrp, a_pwout, dmod_o, gnw_o, gfw_o, gps_o, loss_o):
                ref[...] = jnp.zeros_like(ref)

        shift, scale, gate = m_ref[0:1, :], m_ref[1:2, :], m_ref[2:3, :]
        nw, fw, ps = nw_ref[...], fw_ref[...], ps_ref[...]
        x1 = x_ref[...]
        hx, r1, xn, a = _modulated(x1, nw, shift, scale)
        hxb = hx.astype(BF16)
        uz = jnp.dot(hxb, pwin_ref[...], preferred_element_type=F32)
        u, z = uz[:, :E], uz[:, E:]
        pooled, ys = [], []
        for g in range(4):
            ug = u[:, g * PG:(g + 1) * PG]
            pg = _dot01(pb_ref[g], ug) * pinv_ref[g] - ug
            pooled.append(pg.astype(BF16))
            ys.append(_dot(pooled[g], pgrp_ref[g]))
        ycat = jnp.concatenate(ys, axis=1)
        y = ycat * ps
        zs = _sigmoid(z)
        sz = z * zs
        p = (y * sz).astype(BF16)
        out = _dot(p, pwout_ref[...])
        x2 = x1 + gate * out
        r2 = _rstd(x2)
        xn2 = x2 * r2
        diff = xn2 * fw - t_ref[...]
        loss_o[...] += _colsum(diff * diff)
        dyf = diff * (1.0 / D)
        gfw_o[...] += _colsum(dyf * xn2)
        dxn2 = dyf * fw
        dx2 = r2 * (dxn2 - xn2 * jnp.mean(dxn2 * xn2, axis=-1, keepdims=True))
        dgate = _colsum(dx2 * out)
        dout = (dx2 * gate).astype(BF16)
        for j in range(4):
            cs = slice(j * PG, (j + 1) * PG)
            a_pwout[:, cs] += _dot_ta(p, dout[:, cs])
        dp = _dot_tb(dout, pwout_ref[...])
        dy = dp * sz
        dz = dp * y * (zs * (1.0 + z * (1.0 - zs)))
        gps_o[...] += _colsum(dy * ycat)
        dycat = dy * ps
        dus = []
        for g in range(4):
            dyg = dycat[:, g * PG:(g + 1) * PG].astype(BF16)
            a_pgrp[g] += _dot_ta(pooled[g], dyg)
            dpg = _dot_tb(dyg, pgrp_ref[g])
            dus.append(_dot01(pbt_ref[g], dpg * pinv_ref[g]) - dpg)
        duz = jnp.concatenate(dus + [dz], axis=1).astype(BF16)
        for j in range(2 * E // PG):
            cs = slice(j * PG, (j + 1) * PG)
            a_pwin[:, cs] += _dot_ta(hxb, duz[:, cs])
        dhx = _dot_tb(duz, pwin_ref[...])
        dmod_o[0:1, :] += _colsum(dhx)
        dmod_o[1:2, :] += _colsum(dhx * a)
        dmod_o[2:3, :] += dgate
        da = dhx * (1.0 + scale)
        gnw_o[...] += _colsum(da * xn)
        dxn = da * nw
        dx_ref[...] = dx2 + r1 * (dxn - xn * jnp.mean(dxn * xn, axis=-1, keepdims=True))

        @pl.when(i == NTX - 1)
        def _():
            gpwin_o[...] = a_pwin[...].astype(BF16)
            gpgrp_o[...] = a_pgrp[...].astype(BF16)
            gpwout_o[...] = a_pwout[...].astype(BF16)

    tile = pl.BlockSpec((TM, D), lambda i: (i, 0))
    outs = (_sds((T, D), F32), _sds((D, 2 * E), BF16), _sds((4, PG, PG), BF16), _sds((E, D), BF16),
            _sds((3, D), F32), _sds((1, D), F32), _sds((1, D), F32), _sds((1, E), F32), _sds((1, D), F32))
    return pl.pallas_call(
        body, name="pool_layer", grid=(NTX,),
        in_specs=[tile, tile] + [VMEM_SPEC] * 10,
        out_specs=[tile] + [VMEM_SPEC] * 8,
        out_shape=outs,
        scratch_shapes=[pltpu.VMEM((D, 2 * E), F32), pltpu.VMEM((4, PG, PG), F32), pltpu.VMEM((E, D), F32)],
        compiler_params=pltpu.CompilerParams(dimension_semantics=("arbitrary",), vmem_limit_bytes=VMEM_LIMIT),
    )(x1, tgt, mod1, nw1, fnw, pwin, pgrp, pscale, pwout, pb, pbt, pinv)


def _b3_out_bwd(dx1, o, g_all, gate, gw, wout, gpwout):
    def body(dx_ref, o_ref, z_ref, gate_ref, gw_ref, w_ref, gpwout_r, do_ref, dz_ref, gw_o, dgate_o, ggw_o, rpwout_o,
             acc, ssem, rsem, lsem):
        i = pl.program_id(0)
        _scatter_rider(i, NT, ("rows",), (gpwout_r,), (rpwout_o,), ssem, rsem, lsem)

        @pl.when(i == 0)
        def _():
            acc[...] = jnp.zeros_like(acc)
            dgate_o[...] = jnp.zeros_like(dgate_o)
            ggw_o[...] = jnp.zeros_like(ggw_o)
            do_ref[...] = jnp.zeros_like(do_ref)
            dz_ref[...] = jnp.zeros_like(dz_ref)

        @pl.when(i > 0)
        def _():
            gw = gw_ref[...]
            z = z_ref[...]
            og, r, on, zs, sz = _gated_norm(_get_heads(o_ref), z, gw)
            ogb = og.astype(BF16)
            dx = dx_ref[...]
            dgate_o[...] += _colsum(dx * _dot(ogb, w_ref[...]))
            dy = (dx * gate_ref[...]).astype(BF16)
            for j in range(4):
                cs = slice(j * PG, (j + 1) * PG)
                acc[:, cs] += _dot_ta(ogb, dy[:, cs])
            dog = _dot_tb(dy, w_ref[...])
            dz_ref[...] = (dog * (on * gw) * (zs * (1.0 + z * (1.0 - zs)))).astype(BF16)
            dong = dog * sz
            ggw_o[...] += _colsum(dong * on)
            don = dong * gw
            do = _head_map(lambda dh, nh, rh: rh * (dh - nh * jnp.mean(dh * nh, axis=-1, keepdims=True)), don, on, r)
            _put_heads(do_ref, (), do.astype(BF16))

        @pl.when(i == NT - 1)
        def _():
            gw_o[...] = acc[...].astype(BF16)

    prev = lambda i: (jnp.maximum(i - 1, 0), 0)
    heads = pl.BlockSpec((HEADS, TM, DH), lambda i: (0, i, 0))
    return pl.pallas_call(
        body, name="b3_out_bwd", grid=(NT,),
        in_specs=[pl.BlockSpec((TM, D), prev), heads, pl.BlockSpec((TM, E), lambda i: (i, 4)),
                  VMEM_SPEC, VMEM_SPEC, VMEM_SPEC, HBM_SPEC],
        out_specs=[heads, pl.BlockSpec((TM, E), lambda i: (i, 0)), VMEM_SPEC, VMEM_SPEC, VMEM_SPEC, HBM_SPEC],
        out_shape=(_sds((HEADS, TT, DH), BF16), _sds((TT, E), BF16), _sds((E, D), BF16), _sds((1, D), F32), _sds((1, E), F32),
                   _sds((NDEV, SH_ROWS, D), BF16)),
        scratch_shapes=[pltpu.VMEM((E, D), F32)] + _rider_sems(1),
        compiler_params=pltpu.CompilerParams(dimension_semantics=("arbitrary",), vmem_limit_bytes=VMEM_LIMIT),
    )(dx1, o, g_all, gate, gw, wout, gpwout)


def _gla_bwd(p0, p1, v_all, dec, do, mask01, gpwin):
    nch = TM // CHUNK
    n_steps = HEADS // GLA_HB

    def body(p0_ref, p1_ref, v_ref, dec_ref, do_ref, msk_ref, gpwin_r, d0_ref, d1_ref, dv_ref, dgl_ref, rpwin_o,
             ss_sc, dv_sc, ssem, rsem, lsem):
        _scatter_rider(pl.program_id(0), n_steps, ("cols",), (gpwin_r,), (rpwin_o,), ssem, rsem, lsem)

        lanes = [(d, hh) for d in (0, 1) for hh in range(GLA_HB)]
        zero = jnp.zeros((len(lanes), DH, DH), F32)
        dgl_ref[...] = jnp.zeros_like(dgl_ref)

        def p_of(d):
            return p1_ref if d else p0_ref

        def scan_step(i, n):
            where = [_scan_tile(i, d == 1) for d in (0, 1)]
            cis = [nch - 1 - n if d else n for d, _ in lanes]
            dec = jnp.stack([dec_ref[d, where[d][0], ci:ci + 1, hh * DH:(hh + 1) * DH] for (d, hh), ci in zip(lanes, cis)])

            def chunk(arr):
                return jnp.stack([arr[l, ci * CHUNK:(ci + 1) * CHUNK] for l, ci in enumerate(cis)])

            return where, cis, dec, chunk

        def stacked(i, fn):
            where = [_scan_tile(i, d == 1) for d in (0, 1)]
            return jnp.stack([fn(d, hh, where[d][1]) for d, hh in lanes])

        def fwd_body(i, st):
            v = stacked(i, lambda d, hh, rows: v_ref[hh, rows, :])
            kend = stacked(i, lambda d, hh, rows: p_of(d)[3, hh, rows, :])
            for n in range(nch):
                _, _, dec, chunk = scan_step(i, n)
                ss_sc[i * nch + n] = st
                st = st * dec + _bdot_tn(chunk(v), chunk(kend))
            return st

        lax.fori_loop(0, NT, fwd_body, zero)

        def bwd_body(ii, dst):
            i = NT - 1 - ii
            qg, kg, q_in, kend = [stacked(i, lambda d, hh, rows, ty=ty: p_of(d)[ty, hh, rows, :]) for ty in range(4)]
            v = stacked(i, lambda d, hh, rows: v_ref[hh, rows, :])
            dob = stacked(i, lambda d, hh, rows: do_ref[hh, rows, :])
            msk = jnp.stack([msk_ref[d] for d, _ in lanes])
            a = (_bdot_nt(qg, kg) * msk).astype(BF16)
            da = (_bdot_nt(dob, v) * msk).astype(BF16)
            dqg = _bdot(da, kg)
            dkg = _bdot_tn(da, qg)
            dv_intra = _bdot_tn(a, dob)
            dv_l, dkend_l, dqin_l = ([[None] * nch for _ in lanes] for _ in range(3))
            for n in range(nch - 1, -1, -1):
                where, cis, dec, chunk = scan_step(i, n)
                s_c = ss_sc[i * nch + n]
                dstb = dst.astype(BF16)
                kend_c, v_c, dob_c = chunk(kend), chunk(v), chunk(dob)
                dv_c = chunk(dv_intra) + _bdot_nt(kend_c, dstb)
                dkend_c = _bdot(v_c, dstb)
                dqin_c = _bdot(dob_c, s_c)
                dgl = jnp.sum(s_c * dst, axis=1, keepdims=True) * dec
                for l, ((d, hh), ci) in enumerate(zip(lanes, cis)):
                    dv_l[l][ci], dkend_l[l][ci], dqin_l[l][ci] = dv_c[l], dkend_c[l], dqin_c[l]
                    dgl_ref[d, where[d][0], ci:ci + 1, hh * DH:(hh + 1) * DH] = dgl[l]
                dst = dst * dec + _bdot_tn(dob_c, chunk(q_in))
            where = [_scan_tile(i, d == 1) for d in (0, 1)]
            for l, (d, hh) in enumerate(lanes):
                rows = where[d][1]
                d_ref = d1_ref if d else d0_ref
                d_ref[0, hh, rows, :] = dqg[l].astype(BF16)
                d_ref[1, hh, rows, :] = dkg[l].astype(BF16)
                d_ref[2, hh, rows, :] = jnp.concatenate(dqin_l[l], axis=0).astype(BF16)
                d_ref[3, hh, rows, :] = jnp.concatenate(dkend_l[l], axis=0).astype(BF16)
                dv_sc[d, hh, rows, :] = jnp.concatenate(dv_l[l], axis=0).astype(BF16)
            return dst

        lax.fori_loop(0, NT, bwd_body, zero)
        dv_ref[...] = (dv_sc[0].astype(F32) + dv_sc[1].astype(F32)).astype(BF16)

    quad = pl.BlockSpec((4, GLA_HB, TT, DH), lambda h: (0, h, 0, 0))
    col = pl.BlockSpec((GLA_HB, TT, DH), lambda h: (h, 0, 0))
    chunkv = pl.BlockSpec((2, NT, 8, GLA_HB * DH), lambda h: (0, 0, 0, h))
    outs = (_sds((4, HEADS, TT, DH), BF16), _sds((4, HEADS, TT, DH), BF16), _sds((HEADS, TT, DH), BF16), _sds((2, NT, 8, E), F32),
            _sds((NDEV, D, SH_PWIN), BF16))
    return pl.pallas_call(
        body, name="gla_bwd", grid=(n_steps,),
        in_specs=[quad, quad, col, chunkv, col, pl.BlockSpec((2, TM, TM), lambda h: (0, 0, 0)), HBM_SPEC],
        out_specs=[quad, quad, col, chunkv, HBM_SPEC],
        out_shape=outs,
        scratch_shapes=[pltpu.VMEM((NT * nch, 2 * GLA_HB, DH, DH), F32), pltpu.VMEM((2, GLA_HB, TT, DH), BF16)] + _rider_sems(1),
        compiler_params=pltpu.CompilerParams(dimension_semantics=("arbitrary",), vmem_limit_bytes=VMEM_LIMIT_SCAN),
    )(p0, p1, v_all, dec, do, mask01, gpwin)


TMB = 128


def _gla_post_bwd(g_all, d0, d1, dgl, dv, dz, lb, cum01, gwout, gpgrp):
    nch = TMB // CHUNK

    def body(g_ref, d0_ref, d1_ref, dgl_ref, dv_ref, dz_ref, lb_ref, cum_ref, gwout_r, gpgrp_r, dg_ref, dlb_ref, rwout_o, rpgrp_o,
             ssem, rsem, lsem):
        i = pl.program_id(0)
        _scatter_rider(i, TT // TMB, ("rows", "grp"), (gwout_r, gpgrp_r), (rwout_o, rpgrp_o), ssem, rsem, lsem)

        @pl.when(i == 0)
        def _():
            dlb_ref[...] = jnp.zeros_like(dlb_ref)

        half = i & 1
        qpre = g_ref[:, 3 * E:4 * E]
        dqs_sum = None
        dpre = []
        for d, d_ref in ((0, d0_ref), (1, d1_ref)):
            rev = d == 1
            lbd = lb_ref[d:d + 1, :]
            t = _gla_gates(g_ref[:, d * E:(d + 1) * E], qpre, lbd, cum_ref[d, :TMB, :TMB], rev)
            dqg, dkg, dqin, dkend = [_get_heads(d_ref, (ty,)).astype(F32) for ty in range(4)]
            dqs = dqg * t["e_q"] + dqin * t["e_in"]
            dk = dkg * t["e_k"] + dkend * t["e_end"]
            dkk = dkend * (t["k"] * t["e_end"])
            dg = t["qs"] * dqs - t["k"] * dk
            dkk3 = dkk.reshape(nch, CHUNK, E)
            dgl8 = dgl_ref[d, 0]
            dgl_rows = [jnp.where(half == 0, dgl8[ci:ci + 1, :], dgl8[nch + ci:nch + ci + 1, :]) for ci in range(nch)]
            dgl_b = jnp.concatenate([jnp.broadcast_to(dgl_rows[ci] + jnp.sum(dkk3[ci], axis=0, keepdims=True), (CHUNK, E))
                                     for ci in range(nch)], axis=0)
            pos = lax.broadcasted_iota(jnp.int32, (TMB, E), 0) & (CHUNK - 1)
            dg = dg + jnp.where(pos == (0 if rev else CHUNK - 1), dgl_b, 0.0)
            dlf = _dot01(cum_ref[1 - d, :TMB, :TMB], dg)
            df = dlf / t["f"] - dk
            sig = t["sig"]
            dpre.append((df * (1.0 - lbd) * sig * (1.0 - sig)).astype(BF16))
            dlb_ref[d:d + 1, :] += _colsum(df * (1.0 - sig))
            dqs_sum = dqs if dqs_sum is None else dqs_sum + dqs
            qsig = t["qsig"]
        dqpre = dqs_sum * (DH ** -0.5) * (qsig * (1.0 + qpre * (1.0 - qsig)))
        dg_ref[...] = jnp.concatenate([dpre[0], dpre[1], _get_heads(dv_ref), dqpre.astype(BF16), dz_ref[...]], axis=1)

    quad = pl.BlockSpec((4, HEADS, TMB, DH), lambda i: (0, 0, i, 0))
    tile = pl.BlockSpec((TMB, E), lambda i: (i, 0))
    return pl.pallas_call(
        body, name="gla_post_bwd", grid=(TT // TMB,),
        in_specs=[pl.BlockSpec((TMB, WIN_COLS), lambda i: (i, 0)), quad, quad,
                  pl.BlockSpec((2, 1, 8, E), lambda i: (0, i // 2, 0, 0)), pl.BlockSpec((HEADS, TMB, DH), lambda i: (0, i, 0)), tile,
                  VMEM_SPEC, VMEM_SPEC, HBM_SPEC, HBM_SPEC],
        out_specs=[pl.BlockSpec((TMB, WIN_COLS), lambda i: (i, 0)), VMEM_SPEC, HBM_SPEC, HBM_SPEC],
        out_shape=(_sds((TT, WIN_COLS), BF16), _sds((2, E), F32), _sds((NDEV, SH_ROWS, D), BF16), _sds((NDEV, 4, SH_GRP, PG), BF16)),
        scratch_shapes=_rider_sems(2),
        compiler_params=pltpu.CompilerParams(dimension_semantics=("arbitrary",), vmem_limit_bytes=VMEM_LIMIT),
    )(g_all, d0, d1, dgl, dv, dz, lb, cum01, gwout, gpgrp)


def _b1_in_bwd(idx1, ctx, x, dx1, dg, nw, msel, win):
    last_s = NDEV - 1

    def body(idx_ref, ctx_ref, x_ref, dx1_ref, dg_ref, nw_ref, m_ref, w_ref, gx_ref, rwin_o, dmx_o, dmc_o, gnw_o,
             hx_sc, dhx_sc, acc, sbuf, pbuf, psend, precv, isend, irecv, sibsem, lsem):
        del idx_ref
        s, i = pl.program_id(0), pl.program_id(1)
        x, y, cc, idx = _mesh_pos()
        shift, scale = m_ref[0, 0:1, :], m_ref[0, 1:2, :]
        sibling = (x, y, 1 - cc)

        def partial(p):
            return pltpu.make_async_remote_copy(src_ref=sbuf.at[0], dst_ref=pbuf.at[p], send_sem=psend.at[p], recv_sem=precv.at[p],
                                                device_id=sibling, device_id_type=MESH)

        def chip_sum(p):
            return pltpu.make_async_remote_copy(src_ref=sbuf.at[1], dst_ref=rwin_o.at[2 + p], send_sem=isend.at[p], recv_sem=irecv.at[p],
                                                device_id=_peer(x, y, cc, 2 * (p + 1)), device_id_type=MESH)

        to_sibling = pltpu.make_async_remote_copy(src_ref=sbuf.at[0], dst_ref=rwin_o.at[1], send_sem=sibsem.at[0], recv_sem=sibsem.at[1],
                                                  device_id=sibling, device_id_type=MESH)
        own = pltpu.make_async_copy(sbuf.at[1], rwin_o.at[0], lsem)

        @pl.when((s == 0) & (i == 0))
        def _():
            for ref in (dmx_o, dmc_o, gnw_o):
                ref[...] = jnp.zeros_like(ref)

        @pl.when(s == 0)
        def _():
            hx, _, _, _ = _modulated(_ctx_or_x(i, ctx_ref, x_ref), nw_ref[...], shift, scale)
            hx_sc[i] = hx.astype(BF16)

        @pl.when(i == 0)
        def _():
            acc[...] = jnp.zeros_like(acc)

        dgb = dg_ref[...]
        hxb = hx_sc[i]
        for lo, hi in ((0, 256), (256, 512), (512, SH_WIN)):
            acc[:, lo:hi] += _dot_ta(hxb, dgb[:, lo:hi])
        part = _dot_tb(dgb, w_ref[...])

        @pl.when(s == 0)
        def _():
            dhx_sc[i] = part

        @pl.when(s > 0)
        def _():
            dhx_sc[i] += part

        for p in (2, 1, 0):
            @pl.when((i == NT - 1) & (s == 2 * (2 - p)))
            def _(p=p):
                if p < 2:
                    partial(p + 1).wait_send()
                sbuf[0] = acc[...].astype(BF16)
                partial(p).start()

            @pl.when((i == NT - 1) & (s == 2 * (2 - p) + 1))
            def _(p=p):
                if p < 2:
                    chip_sum(p + 1).wait_send()
                partial(p).wait_recv()
                sbuf[1] = (acc[...] + pbuf[p].astype(F32)).astype(BF16)
                chip_sum(p).start()

        @pl.when((i == NT - 1) & (s == last_s - 1))
        def _():
            partial(0).wait_send()
            sbuf[0] = acc[...].astype(BF16)
            to_sibling.start()

        @pl.when((i == NT - 1) & (s == last_s))
        def _():
            chip_sum(0).wait_send()
            sbuf[1] = acc[...].astype(BF16)
            own.start()

        @pl.when(s == last_s)
        def _():
            nw = nw_ref[...]
            _, r, xn, a = _modulated(_ctx_or_x(i, ctx_ref, x_ref), nw, shift, scale)
            dhx = dhx_sc[i]
            dsh, dsc = _colsum(dhx), _colsum(dhx * a)
            da = dhx * (1.0 + scale)
            gnw_o[...] += _colsum(da * xn)
            dxn = da * nw
            gx_ref[...] = dx1_ref[...] + r * (dxn - xn * jnp.mean(dxn * xn, axis=-1, keepdims=True))

            @pl.when(i == 0)
            def _():
                dmc_o[0:1, :] += dsh
                dmc_o[1:2, :] += dsc

            @pl.when(i > 0)
            def _():
                dmx_o[0:1, :] += dsh
                dmx_o[1:2, :] += dsc

        @pl.when((i == NT - 1) & (s == last_s))
        def _():
            to_sibling.wait_send()
            to_sibling.wait_recv()
            for p in range(3):
                chip_sum(p).wait_recv()
            own.wait()

    grid_spec = pltpu.PrefetchScalarGridSpec(
        num_scalar_prefetch=1, grid=(NDEV, NT),
        in_specs=[VMEM_SPEC, pl.BlockSpec((TM, D), lambda s, i, ix: (jnp.maximum(i - 1, 0), 0)),
                  pl.BlockSpec((TM, D), lambda s, i, ix: (jnp.maximum(i - 1, 0), 0)),
                  pl.BlockSpec((TM, SH_WIN), lambda s, i, ix: (i, ix[0] ^ (last_s - s))), VMEM_SPEC,
                  pl.BlockSpec((1, 2, D), lambda s, i, ix: (jnp.minimum(i, 1), 0, 0)),
                  pl.BlockSpec((D, SH_WIN), lambda s, i, ix: (0, ix[0] ^ (last_s - s)))],
        out_specs=[pl.BlockSpec((TM, D), lambda s, i, ix: (jnp.where(s == last_s, jnp.maximum(i - 1, 0), 0), 0)),
                   HBM_SPEC, VMEM_SPEC, VMEM_SPEC, VMEM_SPEC],
        scratch_shapes=[pltpu.VMEM((NT, TM, D), BF16), pltpu.VMEM((NT, TM, D), F32), pltpu.VMEM((D, SH_WIN), F32),
                        pltpu.VMEM((2, D, SH_WIN), BF16), pltpu.VMEM((3, D, SH_WIN), BF16),
                        pltpu.SemaphoreType.DMA((3,)), pltpu.SemaphoreType.DMA((3,)), pltpu.SemaphoreType.DMA((3,)),
                        pltpu.SemaphoreType.DMA((3,)), pltpu.SemaphoreType.DMA((2,)), pltpu.SemaphoreType.DMA])
    return pl.pallas_call(
        body, name="b1_in_bwd", grid_spec=grid_spec,
        out_shape=(_sds((T, D), F32), _sds((RS_SLOTS, D, SH_WIN), BF16), _sds((2, D), F32), _sds((2, D), F32), _sds((1, D), F32)),
        compiler_params=pltpu.CompilerParams(dimension_semantics=("arbitrary", "arbitrary"), vmem_limit_bytes=VMEM_LIMIT),
    )(idx1, ctx, x, dx1, dg, nw, msel, win)


def _reduce_small(pd, pv, cg, c_ctx, ada_w0):
    n_arr = 3

    def body(pd_r, pv_r, cg_r, cctx_r, ada_r, gada_o, gadab_o, gcctx_o, pvsum_o, loss_o,
             pd_all, pv_all, dsc_all, dsc_mine, ssem, rsem):
        x, y, cc, idx = _mesh_pos()
        srcs = [pd_r, pv_r, dsc_mine]
        dsts = [pd_all.at[idx], pv_all.at[idx], dsc_all.at[idx]]

        def remote(a, k):
            return pltpu.make_async_remote_copy(src_ref=srcs[a], dst_ref=dsts[a], send_sem=ssem.at[a, k], recv_sem=rsem.at[a, k],
                                                device_id=_peer(x, y, cc, k), device_id_type=MESH)

        first = [remote(a, k) for k in range(1, NDEV) for a in (0, 1)]
        for cp in first:
            cp.start()
        pd_all[idx] = pd_r[...]
        pv_all[idx] = pv_r[...]
        for k in range(1, NDEV):
            remote(0, k).wait_recv()
            remote(1, k).wait_recv()
        mine = [pd_all[s, :, pl.ds(idx, 1), :] for s in range(NDEV)]
        dmc = functools.reduce(lambda u, v: u + v, [m[2] for m in mine])
        rows = _stack_rows([cg_r[i] for i in range(NDEV)] + [cctx_r[...]])
        sc = (rows * _sigmoid(rows)).astype(BF16)
        gada_o[0] = _dot_ta(sc, _stack_rows([m[0] for m in mine] + [dmc]))
        gada_o[1] = _dot_ta(sc, _stack_rows([m[1] for m in mine]))
        dsc_mine[...] = _dot_tb(jnp.broadcast_to(dmc, (8, SH_ADA)), ada_r[...])[0:1, :]
        dsc_all[idx] = dsc_mine[...]
        second = [remote(2, k) for k in range(1, NDEV)]
        for cp in second:
            cp.start()
        tot = [functools.reduce(lambda u, v: u + v, [pd_all[s, l] for s in range(NDEV)]) for l in range(3)]
        gadab_o[0] = tot[0] + tot[2]
        gadab_o[1] = tot[1]
        pvs = functools.reduce(lambda u, v: u + v, [pv_all[s] for s in range(NDEV)])
        pvsum_o[...] = pvs
        loss_o[...] = jnp.broadcast_to(jnp.sum(pvs[:, PV_LOSS:PV_LOSS + D], axis=-1, keepdims=True) * (0.5 / D), (1, 128))
        for k in range(1, NDEV):
            remote(2, k).wait_recv()
        dsc = functools.reduce(lambda u, v: u + v, [dsc_all[s] for s in range(NDEV)])
        cx = cctx_r[...]
        sx = _sigmoid(cx)
        gcctx_o[...] = dsc * (sx * (1.0 + cx * (1.0 - sx)))
        for cp in first + second:
            cp.wait_send()

    outs = (_sds((2, D, SH_ADA), F32), _sds((2, NDEV, SH_ADA), F32), _sds((1, D), F32), _sds((1, PV_LEN), F32), _sds((1, 128), F32))
    return pl.pallas_call(
        body, name="reduce_small", out_shape=outs,
        in_specs=[VMEM_SPEC] * 5, out_specs=[VMEM_SPEC] * 5,
        scratch_shapes=[
            pltpu.VMEM((NDEV, 3, NDEV, SH_ADA), F32), pltpu.VMEM((NDEV, 1, PV_LEN), F32), pltpu.VMEM((NDEV, 1, D), F32),
            pltpu.VMEM((1, D), F32),
            pltpu.SemaphoreType.DMA((n_arr, NDEV)), pltpu.SemaphoreType.DMA((n_arr, NDEV)),
        ],
        compiler_params=pltpu.CompilerParams(vmem_limit_bytes=VMEM_LIMIT),
    )(pd, pv, cg, c_ctx, ada_w0)


PV_NW, PV_GNORM, PV_FINAL, PV_LB, PV_PSCALE, PV_LOSS, PV_LEN = 0, 2 * D, 3 * D, 4 * D, 6 * D, 7 * D, 8 * D


def _adamw(w, g, m, v):
    m = ADAM_B1 * m + (1.0 - ADAM_B1) * g
    v = ADAM_B2 * v + (1.0 - ADAM_B2) * (g * g)
    m_hat = m / (1.0 - ADAM_B1 ** ADAM_STEP)
    v_hat = v / (1.0 - ADAM_B2 ** ADAM_STEP)
    delta = -ADAM_LR * (m_hat / (jnp.sqrt(v_hat) + ADAM_EPS) + ADAM_WD * w)
    return delta, m, v


def _adam_sharded(name, parts, w, m, v, tr):
    rr, cc = w.shape
    n = parts.shape[0]

    def body(p_ref, w_ref, m_ref, v_ref, g_o, d_o, m_o, v_o):
        g = p_ref[0].astype(F32)
        for s in range(1, n):
            g = g + p_ref[s].astype(F32)
        d, mn, vn = _adamw(w_ref[...], g, m_ref[...], v_ref[...])
        g_o[...], d_o[...], m_o[...], v_o[...] = g, d, mn, vn

    blk = pl.BlockSpec((tr, cc), lambda i: (i, 0))
    return pl.pallas_call(
        body, name=name, grid=(rr // tr,),
        in_specs=[pl.BlockSpec((n, tr, cc), lambda i: (0, i, 0)), blk, blk, blk],
        out_specs=[blk] * 4, out_shape=(_sds((rr, cc), F32),) * 4,
        compiler_params=pltpu.CompilerParams(dimension_semantics=("arbitrary",)),
    )(parts, w, m, v)


def _adam_dense(name, g, w, m, v, tr):
    rr, cc = w.shape

    def body(g_ref, w_ref, m_ref, v_ref, d_o, m_o, v_o):
        d, mn, vn = _adamw(w_ref[...], g_ref[...], m_ref[...], v_ref[...])
        d_o[...], m_o[...], v_o[...] = d, mn, vn

    blk = pl.BlockSpec((tr, cc), lambda i: (i, 0))
    return pl.pallas_call(
        body, name=name, grid=(rr // tr,), in_specs=[blk] * 4, out_specs=[blk] * 3, out_shape=(_sds((rr, cc), F32),) * 3,
        compiler_params=pltpu.CompilerParams(dimension_semantics=("arbitrary",)),
    )(g, w, m, v)


def _adam_small(gs, ws, ms, vs, lb_idx, lbv):
    n = len(ws)

    def body(*refs):
        g_r, w_r, m_r, v_r = refs[:n], refs[n:2 * n], refs[2 * n:3 * n], refs[3 * n:4 * n]
        lb_r = refs[4 * n]
        outs = refs[4 * n + 1:]
        for j in range(n):
            g = g_r[j][...]
            if j == lb_idx:
                lbj = lb_r[...]
                g = g * lbj * (1.0 - lbj)
            d, mn, vn = _adamw(w_r[j][...], g, m_r[j][...], v_r[j][...])
            outs[j][...], outs[n + j][...], outs[2 * n + j][...], outs[3 * n + j][...] = g, d, mn, vn

    shapes = tuple(_sds(w.shape, F32) for w in ws)
    return pl.pallas_call(body, name="adam_small", out_shape=shapes * 4)(*gs, *ws, *ms, *vs, lbv)


def kernel(x, c, ctx, c_ctx, ada_w, ada_b, norm_w, hgrn_w_in, hgrn_lb_logits, hgrn_gnorm_w, hgrn_w_out, pool_w_in, pool_w_grp, pool_scale, pool_w_out, final_norm_w, loss_target, m_c_ctx, m_ada_w, m_ada_b, m_norm_w, m_hgrn_w_in, m_hgrn_lb_logits, m_hgrn_gnorm_w, m_hgrn_w_out, m_pool_w_in, m_pool_w_grp, m_pool_scale, m_pool_w_out, m_final_norm_w, v_c_ctx, v_ada_w, v_ada_b, v_norm_w, v_hgrn_w_in, v_hgrn_lb_logits, v_hgrn_gnorm_w, v_hgrn_w_out, v_pool_w_in, v_pool_w_grp, v_pool_scale, v_pool_w_out, v_final_norm_w):
    idx = 4 * lax.axis_index("x") + 2 * lax.axis_index("y") + lax.axis_index("c")
    cctx2 = c_ctx.reshape(1, D)
    cum01, mask01 = _gla_consts()
    pb, pbt, pinv = _pool_consts()

    idx1 = idx.reshape(1).astype(jnp.int32)
    s_win, s_wout, s_pwin, s_pgrp, s_pwout, lbl_g, ps_g, cg, mod_g = _gather_small(
        hgrn_w_in[0], hgrn_w_out[0], pool_w_in[0], pool_w_grp[0], pool_w_out[0], hgrn_lb_logits[0], pool_scale, c, cctx2, ada_w)
    lb = jax.nn.sigmoid(jnp.transpose(lbl_g, (1, 0, 2)).reshape(2, E))
    pscale = ps_g.reshape(1, E)
    mod_all = jnp.transpose(mod_g, (1, 2, 0, 3)).reshape(2, 16, 3 * D) + ada_b[:, None, :]
    mod_me = lax.dynamic_index_in_dim(mod_all, idx, axis=1, keepdims=False)
    mod0, mod1, modc = mod_me[0].reshape(3, D), mod_me[1].reshape(3, D), mod_all[0, NDEV].reshape(3, D)
    msel = jnp.stack([modc[:2], mod0[:2]])
    nw0, nw1 = norm_w[0:1], norm_w[1:2]
    fnw = final_norm_w.reshape(1, D)

    g_all, win = _f1_gather_matmul(idx1, ctx[0], x[0], nw0, msel, s_win)
    p0, p1, v_all, dec, pwin, pgrp = _gla_prep(g_all, lb, cum01, s_pwin, s_pgrp)
    o, wout, pwout = _gla_fwd(p0, p1, v_all, dec, mask01, s_wout, s_pwout)
    x1 = _f3_out(o, g_all, x[0], mod0[2:3], hgrn_gnorm_w, wout)
    dx1, gpwin, gpgrp, gpwout, dmod1, gnw1, gfw, gps, lossv = _pool_layer(
        x1, loss_target[0], mod1, nw1, fnw, pwin, pgrp, pscale, pwout, pb, pbt, pinv)
    do, dz, gwout, dgate0, ggw, rpwout = _b3_out_bwd(dx1, o, g_all, mod0[2:3], hgrn_gnorm_w, wout, gpwout)
    d0, d1, dv, dgl, rpwin = _gla_bwd(p0, p1, v_all, dec, do, mask01, gpwin)
    dg, dlb, rwout, rpgrp = _gla_post_bwd(g_all, d0, d1, dgl, dv, dz, lb, cum01, gwout, gpgrp)
    grad_x, rwin, dmx, dmc, gnw0 = _b1_in_bwd(idx1, ctx[0], x[0], dx1, dg, nw0, msel, win)

    dmod0 = jnp.concatenate([dmx, dgate0], axis=0)
    dmodc = jnp.concatenate([dmc, jnp.zeros((1, D), F32)], axis=0)
    pd = jnp.stack([dmod0, dmod1, dmodc]).reshape(3, NDEV, SH_ADA)
    pv = jnp.concatenate([gnw0, gnw1, ggw, gfw, dlb.reshape(1, 2 * E), gps, lossv], axis=1)
    g_ada, g_adab, g_cctx, pvsum, loss128 = _reduce_small(pd, pv, cg, cctx2, ada_w[0])

    out = {}
    out["hgrn_w_in"] = _adam_sharded("adam_w_in", rwin, hgrn_w_in[0], m_hgrn_w_in[0], v_hgrn_w_in[0], 256)
    out["hgrn_w_out"] = _adam_sharded("adam_w_out", rwout, hgrn_w_out[0], m_hgrn_w_out[0], v_hgrn_w_out[0], SH_ROWS)
    out["pool_w_in"] = _adam_sharded("adam_pw_in", rpwin, pool_w_in[0], m_pool_w_in[0], v_pool_w_in[0], 512)
    out["pool_w_grp"] = _adam_sharded("adam_pgrp", rpgrp.reshape(NDEV, 4 * SH_GRP, PG), pool_w_grp[0].reshape(4 * SH_GRP, PG),
                                      m_pool_w_grp[0].reshape(4 * SH_GRP, PG), v_pool_w_grp[0].reshape(4 * SH_GRP, PG), 4 * SH_GRP)
    out["pool_w_out"] = _adam_sharded("adam_pw_out", rpwout, pool_w_out[0], m_pool_w_out[0], v_pool_w_out[0], SH_ROWS)
    g_ada2 = g_ada.reshape(2 * D, SH_ADA)
    out["ada_w"] = (g_ada2,) + _adam_dense("adam_ada_w", g_ada2, ada_w.reshape(2 * D, SH_ADA), m_ada_w.reshape(2 * D, SH_ADA),
                                           v_ada_w.reshape(2 * D, SH_ADA), 512)

    lb_me = lax.dynamic_slice_in_dim(lb, idx * DH, DH, axis=1)
    small = ["c_ctx", "ada_b", "norm_w", "hgrn_lb_logits", "hgrn_gnorm_w", "pool_scale", "final_norm_w"]
    gs = [g_cctx, g_adab.reshape(2, 3 * D), pvsum[:, PV_NW:PV_NW + 2 * D].reshape(2, D),
          lax.dynamic_slice_in_dim(pvsum[:, PV_LB:PV_LB + 2 * E].reshape(2, E), idx * DH, DH, axis=1),
          pvsum[:, PV_GNORM:PV_GNORM + E], lax.dynamic_slice_in_dim(pvsum[:, PV_PSCALE:PV_PSCALE + E], idx * DH, DH, axis=1),
          pvsum[:, PV_FINAL:PV_FINAL + D]]
    ws = [cctx2, ada_b, norm_w, hgrn_lb_logits[0], hgrn_gnorm_w, pool_scale, fnw]
    ms = [m_c_ctx.reshape(1, D), m_ada_b, m_norm_w, m_hgrn_lb_logits[0], m_hgrn_gnorm_w, m_pool_scale, m_final_norm_w.reshape(1, D)]
    vs = [v_c_ctx.reshape(1, D), v_ada_b, v_norm_w, v_hgrn_lb_logits[0], v_hgrn_gnorm_w, v_pool_scale, v_final_norm_w.reshape(1, D)]
    res = _adam_small(gs, ws, ms, vs, 3, lb_me)
    n = len(small)
    for j, name in enumerate(small):
        out[name] = tuple(res[q * n + j] for q in range(4))

    shapes = {"c_ctx": (D,), "ada_w": (2, D, SH_ADA), "ada_b": (2, 3 * D), "norm_w": (2, D), "hgrn_w_in": (1, D, SH_WIN),
              "hgrn_lb_logits": (1, 2, DH), "hgrn_gnorm_w": (1, E), "hgrn_w_out": (1, SH_ROWS, D), "pool_w_in": (1, D, SH_PWIN),
              "pool_w_grp": (1, 4, SH_GRP, PG), "pool_scale": (1, DH), "pool_w_out": (1, SH_ROWS, D), "final_norm_w": (D,)}
    order = ["c_ctx", "ada_w", "ada_b", "norm_w", "hgrn_w_in", "hgrn_lb_logits", "hgrn_gnorm_w", "hgrn_w_out", "pool_w_in",
             "pool_w_grp", "pool_scale", "pool_w_out", "final_norm_w"]
    flat = [out[name][q].reshape(shapes[name]) for q in range(4) for name in order]
    return (loss128[0, 0], grad_x[None], *flat)
```

```python
import functools

import numpy as np
import jax
import jax.numpy as jnp
from jax import lax
from jax.experimental import pallas as pl
from jax.experimental.pallas import tpu as pltpu

F32 = jnp.float32
BF16 = jnp.bfloat16

D = 1024
E = 1024
HEADS = 8
DH = 128
CHUNK = 64
T = 2048
TC = 256
TT = T + TC
TM = 256
NT = TT // TM
NTX = T // TM
NDEV = 8
GRID_W = 64
POOL_WINDOWS = (2, 4, 8, 16)
PG = 256
EPS = 1e-6
WIN_COLS = 5 * E
SH_WIN = WIN_COLS // NDEV
SH_PWIN = 2 * E // NDEV
SH_ROWS = E // NDEV
SH_GRP = PG // NDEV
SH_ADA = 3 * D // NDEV
VMEM_LIMIT = 56 * 1024 * 1024
VMEM_LIMIT_SCAN = 60 * 1024 * 1024

ADAM_LR, ADAM_B1, ADAM_B2, ADAM_EPS, ADAM_WD, ADAM_STEP = 0.001, 0.9, 0.999, 1e-08, 0.01, 10

MESH = pl.DeviceIdType.MESH
VMEM_SPEC = pl.BlockSpec(memory_space=pltpu.VMEM)
HBM_SPEC = pl.BlockSpec(memory_space=pltpu.HBM)
ANY_SPEC = pl.BlockSpec(memory_space=pl.ANY)


def _sds(shape, dtype):
    return jax.ShapeDtypeStruct(shape, dtype)


def _bf(a):
    return a if a.dtype == BF16 else a.astype(BF16)


def _dot(a, b):
    return lax.dot_general(_bf(a), _bf(b), (((1,), (0,)), ((), ())), preferred_element_type=F32)


def _dot_tb(a, b):
    return lax.dot_general(_bf(a), _bf(b), (((1,), (1,)), ((), ())), preferred_element_type=F32)


def _dot_ta(a, b):
    return lax.dot_general(_bf(a), _bf(b), (((0,), (0,)), ((), ())), preferred_element_type=F32)


def _bdot(a, b):
    return lax.dot_general(_bf(a), _bf(b), (((2,), (1,)), ((0,), (0,))), preferred_element_type=F32)


def _bdot_nt(a, b):
    return lax.dot_general(_bf(a), _bf(b), (((2,), (2,)), ((0,), (0,))), preferred_element_type=F32)


def _bdot_tn(a, b):
    return lax.dot_general(_bf(a), _bf(b), (((1,), (1,)), ((0,), (0,))), preferred_element_type=F32)


def _dot01(m01, x):
    hi = x.astype(BF16)
    lo = (x - hi.astype(F32)).astype(BF16)
    return _dot(m01, hi) + _dot(m01, lo)


def _rstd(x):
    return lax.rsqrt(jnp.mean(x * x, axis=-1, keepdims=True) + EPS)


def _sigmoid(x):
    return jax.nn.sigmoid(x)


def _colsum(a):
    return jnp.sum(a, axis=0, keepdims=True)


def _stack_rows(rows):
    n = rows[0].shape[-1]
    rid = lax.broadcasted_iota(jnp.int32, (16, n), 0)
    out = jnp.zeros((16, n), F32)
    for i, r in enumerate(rows):
        out = jnp.where(rid == i, r, out)
    return out


def _head_map(fn, *arrs):
    outs = [fn(*[a[:, h * DH:(h + 1) * DH] for a in arrs]) for h in range(HEADS)]
    return jnp.concatenate(outs, axis=1)


def _gla_consts():
    r = np.arange(TM)[:, None]
    c = np.arange(TM)[None, :]
    same = (r // CHUNK) == (c // CHUNK)
    tril = same & (c <= r)
    triu = same & (c >= r)
    m = np.stack([tril, triu]).astype(np.float32)
    return jnp.asarray(m, BF16), jnp.asarray(m, F32)


def _pool_consts():
    r = np.arange(TM)[:, None]
    c = np.arange(TM)[None, :]
    same = (r // GRID_W) == (c // GRID_W)
    rp, cp = r % GRID_W, c % GRID_W
    bs, inv = [], []
    for w in POOL_WINDOWS:
        lo = np.clip(rp - w // 2, 0, GRID_W)
        hi = np.clip(rp - w // 2 + w, 0, GRID_W)
        bs.append(same & (cp >= lo) & (cp < hi))
        inv.append(1.0 / (hi - lo).astype(np.float32))
    b = np.stack(bs).astype(np.float32)
    bt = np.transpose(b, (0, 2, 1))
    return jnp.asarray(b, BF16), jnp.asarray(bt, BF16), jnp.asarray(np.stack(inv), F32)


def _mesh_pos():
    x, y, c = lax.axis_index("x"), lax.axis_index("y"), lax.axis_index("c")
    return x, y, c, 4 * x + 2 * y + c


def _peer(x, y, c, k):
    return (x ^ ((k >> 2) & 1), y ^ ((k >> 1) & 1), c ^ (k & 1))


def _gather_small(w_in, w_out, pw_in, pgrp, pw_out, lb_l, pscale, c, c_ctx, ada_w):
    n_arr = 4

    def body(win_r, wout_r, pwin_r, pgrp_r, pwout_r, lb_r, ps_r, c_r, cctx_r, ada_r,
             s_win, s_wout, s_pwin, s_pgrp, s_pwout, lb_o, ps_o, cg_o, mod_o, ssem, rsem):
        x, y, cc, idx = _mesh_pos()
        srcs = [lb_r, ps_r, c_r, mod_o.at[idx]]
        mine = [lb_o.at[idx], ps_o.at[idx], cg_o.at[idx], mod_o.at[idx]]

        def remote(a, k):
            return pltpu.make_async_remote_copy(src_ref=srcs[a], dst_ref=mine[a], send_sem=ssem.at[a, k], recv_sem=rsem.at[a, k],
                                                device_id=_peer(x, y, cc, k), device_id_type=MESH)

        first = [remote(a, k) for k in range(1, NDEV) for a in (2, 0, 1)]
        for cp in first:
            cp.start()
        lb_o[idx] = lb_r[...]
        ps_o[idx] = ps_r[...]
        cg_o[idx] = c_r[...]
        s_win[...] = win_r[...].astype(BF16)
        s_wout[...] = wout_r[...].astype(BF16)
        s_pwin[...] = pwin_r[...].astype(BF16)
        s_pgrp[...] = pgrp_r[...].astype(BF16)
        s_pwout[...] = pwout_r[...].astype(BF16)
        for k in range(1, NDEV):
            remote(2, k).wait_recv()
        rows = _stack_rows([cg_o[i] for i in range(NDEV)] + [cctx_r[...]])
        sc = rows * _sigmoid(rows)
        for l in range(2):
            mod_o[idx, l] = _dot(sc, ada_r[l])
        second = [remote(3, k) for k in range(1, NDEV)]
        for cp in second:
            cp.start()
        for cp in first + second:
            cp.wait_send()
        for k in range(1, NDEV):
            for a in (0, 1, 3):
                remote(a, k).wait_recv()

    outs = (
        _sds((D, SH_WIN), BF16), _sds((SH_ROWS, D), BF16), _sds((D, SH_PWIN), BF16), _sds((4, SH_GRP, PG), BF16), _sds((SH_ROWS, D), BF16),
        _sds((NDEV, 2, DH), F32), _sds((NDEV, 1, DH), F32), _sds((NDEV, 1, D), F32), _sds((NDEV, 2, 16, SH_ADA), F32),
    )
    return pl.pallas_call(
        body, name="gather_small", out_shape=outs,
        in_specs=[VMEM_SPEC] * 10, out_specs=[VMEM_SPEC] * 9,
        scratch_shapes=[pltpu.SemaphoreType.DMA((n_arr, NDEV)), pltpu.SemaphoreType.DMA((n_arr, NDEV))],
        compiler_params=pltpu.CompilerParams(vmem_limit_bytes=VMEM_LIMIT),
    )(w_in, w_out, pw_in, pgrp, pw_out, lb_l, pscale, c, c_ctx, ada_w)


def _gather_order(s):
    if isinstance(s, int):
        return (0, 1, 2, 4, 3, 5, 6, 7)[s]
    return s + (s == 3).astype(jnp.int32) - (s == 4).astype(jnp.int32)


GATHER_ISSUE = (1, 2, 4, 3, 5, 6, 7)
GATHER_ICI = (2, 4, 6)
GATHER_DIRECT = (1,) + GATHER_ICI
GLA_HB = 2
RS_SLOTS = 5


def _shard_of(kind, ref, i):
    if kind == "rows":
        return ref.at[pl.ds(pl.multiple_of(i * SH_ROWS, SH_ROWS), SH_ROWS), :]
    if kind == "major":
        return ref.at[i]
    assert kind == "grp"
    return ref.at[:, pl.ds(pl.multiple_of(i * SH_GRP, SH_GRP), SH_GRP), :]


def _gather_rider(step, n_steps, forward_at, kinds, srcs, outs, ssem, rsem, lsem):
    x, y, cc, idx = _mesh_pos()
    arrays = range(len(kinds))
    mine = [_shard_of(kinds[a], outs[a], idx) for a in arrays]

    def remote(a, k):
        return pltpu.make_async_remote_copy(src_ref=srcs[a], dst_ref=mine[a], send_sem=ssem.at[a, k], recv_sem=rsem.at[a, k],
                                            device_id=_peer(x, y, cc, k), device_id_type=MESH)

    def forward(a, k):
        blk = _shard_of(kinds[a], outs[a], idx ^ k)
        return pltpu.make_async_remote_copy(src_ref=blk, dst_ref=blk, send_sem=ssem.at[a, k ^ 1], recv_sem=rsem.at[a, k ^ 1],
                                            device_id=(x, y, 1 - cc), device_id_type=MESH)

    copies = [remote(a, k) for k in GATHER_DIRECT for a in arrays]
    passed = [forward(a, k) for k in GATHER_ICI for a in arrays]
    local = [pltpu.make_async_copy(srcs[a], mine[a], lsem.at[a]) for a in arrays]

    @pl.when(step == 0)
    def _():
        for cp in copies + local:
            cp.start()

    @pl.when(step == forward_at)
    def _():
        for k in GATHER_ICI:
            for a in arrays:
                remote(a, k).wait_recv()
                forward(a, k).start()

    @pl.when(step == n_steps - 1)
    def _():
        for cp in copies + passed:
            cp.wait_send()
        for a in arrays:
            remote(a, 1).wait_recv()
        for cp in passed:
            cp.wait_recv()
        for cp in local:
            cp.wait()


def _scatter_rider(step, n_steps, kinds, grads, slots, ssem, rsem, lsem):
    x, y, cc, idx = _mesh_pos()
    arrays = range(len(kinds))
    dsts = [slots[a].at[idx] for a in arrays]

    def remote(a, k):
        px, py, pc = _peer(x, y, cc, k)
        return pltpu.make_async_remote_copy(src_ref=_shard_of(kinds[a], grads[a], 4 * px + 2 * py + pc), dst_ref=dsts[a],
                                            send_sem=ssem.at[a, k], recv_sem=rsem.at[a, k], device_id=(px, py, pc), device_id_type=MESH)

    copies = [remote(a, k) for k in GATHER_ISSUE for a in arrays]
    local = [pltpu.make_async_copy(_shard_of(kinds[a], grads[a], idx), dsts[a], lsem.at[a]) for a in arrays]

    @pl.when(step == 0)
    def _():
        for cp in copies + local:
            cp.start()

    @pl.when(step == n_steps - 1)
    def _():
        for cp in copies:
            cp.wait_send()
        for cp in copies:
            cp.wait_recv()
        for cp in local:
            cp.wait()


def _rider_sems(n):
    return [pltpu.SemaphoreType.DMA((n, NDEV)), pltpu.SemaphoreType.DMA((n, NDEV)), pltpu.SemaphoreType.DMA((n,))]


def _modulated(x, nw, shift, scale):
    r = _rstd(x)
    xn = x * r
    a = xn * nw
    return a * (1.0 + scale) + shift, r, xn, a


def _ctx_or_x(i, ctx_ref, x_ref):
    return jnp.where(i == 0, ctx_ref[...], x_ref[...])


def _f1_gather_matmul(idx1, ctx, x, nw, msel, s_win):
    def body(idx_ref, ctx_ref, x_ref, nw_ref, m_ref, sw_ref, g_ref, win_o, wslot, hx_sc, ssem, rsem, lsem, osem):
        del idx_ref
        s, i = pl.program_id(0), pl.program_id(1)
        x, y, cc, idx = _mesh_pos()
        k = _gather_order(s)
        j = idx ^ k

        def remote(kk):
            return pltpu.make_async_remote_copy(src_ref=sw_ref, dst_ref=wslot.at[idx], send_sem=ssem.at[kk], recv_sem=rsem.at[kk],
                                                device_id=_peer(x, y, cc, kk), device_id_type=MESH)

        def forward(kk):
            jj = idx ^ kk
            return pltpu.make_async_remote_copy(src_ref=wslot.at[jj], dst_ref=wslot.at[jj], send_sem=ssem.at[kk ^ 1],
                                                recv_sem=rsem.at[kk ^ 1], device_id=(x, y, 1 - cc), device_id_type=MESH)

        own = pltpu.make_async_copy(sw_ref, wslot.at[idx], lsem)

        def to_hbm(jj, kk):
            return pltpu.make_async_copy(wslot.at[jj], win_o.at[:, pl.ds(pl.multiple_of(jj * SH_WIN, 128), SH_WIN)], osem.at[kk])

        @pl.when((s == 0) & (i == 0))
        def _():
            own.start()
            for kk in GATHER_DIRECT:
                remote(kk).start()
            own.wait()

        @pl.when(s == 0)
        def _():
            hx, _, _, _ = _modulated(_ctx_or_x(i, ctx_ref, x_ref), nw_ref[...], m_ref[0, 0:1, :], m_ref[0, 1:2, :])
            hx_sc[i] = hx.astype(BF16)

        @pl.when((s > 0) & (i == 0))
        def _():
            remote(k).wait_recv()

            @pl.when((k & 1) == 0)
            def _():
                forward(k).start()

        @pl.when(i == 0)
        def _():
            to_hbm(j, k).start()

        g_ref[...] = jnp.dot(hx_sc[i], wslot[j], preferred_element_type=F32)

        @pl.when((s == NDEV - 1) & (i == NT - 1))
        def _():
            for kk in GATHER_DIRECT:
                remote(kk).wait_send()
            for kk in GATHER_ICI:
                forward(kk).wait_send()
            for kk in range(NDEV):
                to_hbm(idx ^ kk, kk).wait()

    grid_spec = pltpu.PrefetchScalarGridSpec(
        num_scalar_prefetch=1, grid=(NDEV, NT),
        in_specs=[VMEM_SPEC, pl.BlockSpec((TM, D), lambda s, i, ix: (jnp.maximum(i - 1, 0), 0)), VMEM_SPEC,
                  pl.BlockSpec((1, 2, D), lambda s, i, ix: (jnp.minimum(i, 1), 0, 0)), HBM_SPEC],
        out_specs=[pl.BlockSpec((TM, SH_WIN), lambda s, i, ix: (i, ix[0] ^ _gather_order(s))), HBM_SPEC],
        scratch_shapes=[pltpu.VMEM((NDEV, D, SH_WIN), BF16), pltpu.VMEM((NT, TM, D), BF16),
                        pltpu.SemaphoreType.DMA((NDEV,)), pltpu.SemaphoreType.DMA((NDEV,)), pltpu.SemaphoreType.DMA,
                        pltpu.SemaphoreType.DMA((NDEV,))])
    return pl.pallas_call(
        body, name="f1_gather_matmul", grid_spec=grid_spec,
        out_shape=(_sds((TT, WIN_COLS), F32), _sds((D, WIN_COLS), BF16)),
        compiler_params=pltpu.CompilerParams(dimension_semantics=("arbitrary", "arbitrary"), vmem_limit_bytes=VMEM_LIMIT),
    )(idx1, ctx, x, nw, msel, s_win)


def _gla_gates(pre, qpre, lbd, cum, rev):
    rows, n = pre.shape
    nch = rows // CHUNK
    sig = _sigmoid(pre)
    f = lbd + (1.0 - lbd) * sig
    k = 1.0 - f
    g = _dot01(cum, jnp.log(f))
    g3 = g.reshape(nch, CHUNK, n)
    last = 0 if rev else CHUNK - 1
    mid = CHUNK // 2 if rev else CHUNK // 2 - 1
    gl1, gm1 = g3[:, last:last + 1, :], g3[:, mid:mid + 1, :]

    def bc(a):
        return jnp.broadcast_to(a, g3.shape).reshape(rows, n)

    gm = bc(gm1)
    e_q, e_k = jnp.exp(g - gm), jnp.exp(gm - g)
    e_in, e_end = e_q * bc(jnp.exp(gm1)), e_k * bc(jnp.exp(gl1 - gm1))
    qsig = _sigmoid(qpre)
    qs = qpre * qsig * (DH ** -0.5)
    return dict(sig=sig, f=f, k=k, qsig=qsig, qs=qs, e_q=e_q, e_k=e_k, e_in=e_in, e_end=e_end,
                decay=[jnp.exp(g3[ci, last:last + 1, :]) for ci in range(nch)])


def _put_heads(ref, lead, arr):
    for h in range(HEADS):
        ref[lead + (h,)] = arr[:, h * DH:(h + 1) * DH]


def _get_heads(ref, lead=()):
    return jnp.concatenate([ref[lead + (h,)] for h in range(HEADS)], axis=1)


def _gla_prep(g_all, lb, cum01, s_pwin, s_pgrp):
    def body(g_ref, lb_ref, cum_ref, spwin_r, spgrp_r, p0_ref, p1_ref, v_ref, dec_ref, pwin_o, pgrp_o, ssem, rsem, lsem):
        _gather_rider(pl.program_id(0), NT, NT // 2 + 1, ("major", "grp"), (spwin_r, spgrp_r), (pwin_o, pgrp_o), ssem, rsem, lsem)
        qpre = g_ref[:, 3 * E:4 * E]
        _put_heads(v_ref, (), g_ref[:, 2 * E:3 * E].astype(BF16))
        dec_ref[...] = jnp.zeros_like(dec_ref)
        for d, p_ref in ((0, p0_ref), (1, p1_ref)):
            t = _gla_gates(g_ref[:, d * E:(d + 1) * E], qpre, lb_ref[d:d + 1, :], cum_ref[d], d == 1)
            _put_heads(p_ref, (0,), (t["qs"] * t["e_q"]).astype(BF16))
            _put_heads(p_ref, (1,), (t["k"] * t["e_k"]).astype(BF16))
            _put_heads(p_ref, (2,), (t["qs"] * t["e_in"]).astype(BF16))
            _put_heads(p_ref, (3,), (t["k"] * t["e_end"]).astype(BF16))
            for ci in range(TM // CHUNK):
                dec_ref[d, 0, ci:ci + 1, :] = t["decay"][ci]

    quad = pl.BlockSpec((4, HEADS, TM, DH), lambda i: (0, 0, i, 0))
    return pl.pallas_call(
        body, name="gla_prep", grid=(NT,),
        in_specs=[pl.BlockSpec((TM, WIN_COLS), lambda i: (i, 0)), VMEM_SPEC, VMEM_SPEC, HBM_SPEC, HBM_SPEC],
        out_specs=[quad, quad, pl.BlockSpec((HEADS, TM, DH), lambda i: (0, i, 0)), pl.BlockSpec((2, 1, 8, E), lambda i: (0, i, 0, 0)),
                   HBM_SPEC, HBM_SPEC],
        out_shape=(_sds((4, HEADS, TT, DH), BF16), _sds((4, HEADS, TT, DH), BF16), _sds((HEADS, TT, DH), BF16), _sds((2, NT, 8, E), F32),
                   _sds((NDEV, D, SH_PWIN), BF16), _sds((4, PG, PG), BF16)),
        scratch_shapes=_rider_sems(2),
        compiler_params=pltpu.CompilerParams(dimension_semantics=("arbitrary",), vmem_limit_bytes=VMEM_LIMIT),
    )(g_all, lb, cum01, s_pwin, s_pgrp)


def _scan_tile(i, rev):
    t = jnp.where(i == 0, 0, NT - i) if rev else i
    return t, pl.ds(pl.multiple_of(t * TM, TM), TM)


def _chunk_order(rev):
    n = TM // CHUNK
    return tuple(range(n - 1, -1, -1)) if rev else tuple(range(n))


def _gla_fwd(p0, p1, v_all, dec, mask01, s_wout, s_pwout):
    n_steps = HEADS // GLA_HB

    def body(p0_ref, p1_ref, v_ref, dec_ref, msk_ref, swout_r, spwout_r, o_ref, wout_o, pwout_o, ob_sc, ssem, rsem, lsem):
        _gather_rider(pl.program_id(0), n_steps, n_steps // 2, ("rows", "rows"), (swout_r, spwout_r), (wout_o, pwout_o), ssem, rsem, lsem)

        lanes = [(d, hh) for d in (0, 1) for hh in range(GLA_HB)]
        nch = TM // CHUNK

        def tile_body(i, st):
            where = [_scan_tile(i, d == 1) for d in (0, 1)]

            def stacked(fn):
                return jnp.stack([fn(d, hh, where[d][1]) for d, hh in lanes])

            qg, kg, q_in, kend = [stacked(lambda d, hh, rows, ty=ty: (p1_ref if d else p0_ref)[ty, hh, rows, :]) for ty in range(4)]
            v = stacked(lambda d, hh, rows: v_ref[hh, rows, :])
            a = _bdot_nt(qg, kg) * jnp.stack([msk_ref[d] for d, _ in lanes])
            intra = _bdot(a, v)
            outs = [[None] * nch for _ in lanes]
            for n in range(nch):
                cis = [nch - 1 - n if d else n for d, _ in lanes]

                def chunk(arr):
                    return jnp.stack([arr[l, ci * CHUNK:(ci + 1) * CHUNK] for l, ci in enumerate(cis)])

                dec = jnp.stack([dec_ref[d, where[d][0], ci:ci + 1, hh * DH:(hh + 1) * DH] for (d, hh), ci in zip(lanes, cis)])
                inter = _bdot_nt(chunk(q_in), st)
                for l, ci in enumerate(cis):
                    outs[l][ci] = inter[l] + intra[l, ci * CHUNK:(ci + 1) * CHUNK]
                st = st * dec + _bdot_tn(chunk(v), chunk(kend))
            for l, (d, hh) in enumerate(lanes):
                (ob_sc if d else o_ref)[hh, where[d][1], :] = jnp.concatenate(outs[l], axis=0)
            return st

        lax.fori_loop(0, NT, tile_body, jnp.zeros((len(lanes), DH, DH), F32))
        o_ref[...] += ob_sc[...]

    quad = pl.BlockSpec((4, GLA_HB, TT, DH), lambda h: (0, h, 0, 0))
    head = pl.BlockSpec((GLA_HB, TT, DH), lambda h: (h, 0, 0))
    return pl.pallas_call(
        body, name="gla_fwd", grid=(n_steps,),
        in_specs=[quad, quad, head, pl.BlockSpec((2, NT, 8, GLA_HB * DH), lambda h: (0, 0, 0, h)),
                  pl.BlockSpec((2, TM, TM), lambda h: (0, 0, 0)), HBM_SPEC, HBM_SPEC],
        out_specs=[head, HBM_SPEC, HBM_SPEC],
        out_shape=(_sds((HEADS, TT, DH), F32), _sds((E, D), BF16), _sds((E, D), BF16)),
        scratch_shapes=[pltpu.VMEM((GLA_HB, TT, DH), F32)] + _rider_sems(2),
        compiler_params=pltpu.CompilerParams(dimension_semantics=("arbitrary",), vmem_limit_bytes=VMEM_LIMIT),
    )(p0, p1, v_all, dec, mask01, s_wout, s_pwout)


def _gated_norm(o, z, gw):
    r = _head_map(lambda oh: jnp.broadcast_to(_rstd(oh), oh.shape), o)
    on = o * r
    zs = _sigmoid(z)
    sz = z * zs
    return on * gw * sz, r, on, zs, sz


def _f3_out(o, g_all, x, gate, gw, wout):
    def body(o_ref, z_ref, x_ref, gate_ref, gw_ref, w_ref, x1_ref):
        og, _, _, _, _ = _gated_norm(_get_heads(o_ref), z_ref[...], gw_ref[...])
        x1_ref[...] = x_ref[...] + gate_ref[...] * _dot(og, w_ref[...])

    return pl.pallas_call(
        body, name="f3_out", grid=(NTX,),
        in_specs=[pl.BlockSpec((HEADS, TM, DH), lambda i: (0, i + 1, 0)), pl.BlockSpec((TM, E), lambda i: (i + 1, 4)),
                  pl.BlockSpec((TM, D), lambda i: (i, 0)), pl.BlockSpec((1, D), lambda i: (0, 0)),
                  pl.BlockSpec((1, E), lambda i: (0, 0)), pl.BlockSpec((E, D), lambda i: (0, 0))],
        out_specs=pl.BlockSpec((TM, D), lambda i: (i, 0)),
        out_shape=_sds((T, D), F32),
        compiler_params=pltpu.CompilerParams(dimension_semantics=("arbitrary",)),
    )(o, g_all, x, gate, gw, wout)


def _pool_layer(x1, tgt, mod1, nw1, fnw, pwin, pgrp, pscale, pwout, pb, pbt, pinv):
    def body(x_ref, t_ref, m_ref, nw_ref, fw_ref, pwin_ref, pgrp_ref, ps_ref, pwout_ref, pb_ref, pbt_ref, pinv_ref,
             dx_ref, gpwin_o, gpgrp_o, gpwout_o, dmod_o, gnw_o, gfw_o, gps_o, loss_o,
             a_pwin, a_pgrp, a_pwout):
        i = pl.program_id(0)

        @pl.when(i == 0)
        def _():
            for ref in (a_pwin, a_pgrp, a_pwout, dmod_o, gnw_o, gfw_o, gps_o, loss_o):
                ref[...] = jnp.zeros_like(ref)

        shift, scale, gate = m_ref[0:1, :], m_ref[1:2, :], m_ref[2:3, :]
        nw, fw, ps = nw_ref[...], fw_ref[...], ps_ref[...]
        x1 = x_ref[...]
        hx, r1, xn, a = _modulated(x1, nw, shift, scale)
        hxb = hx.astype(BF16)
        uz = jnp.concatenate([_dot(hxb, pwin_ref[j]) for j in range(NDEV)], axis=1)
        u, z = uz[:, :E], uz[:, E:]
        pooled, ys = [], []
        for g in range(4):
            ug = u[:, g * PG:(g + 1) * PG]
            pg = _dot01(pb_ref[g], ug) * pinv_ref[g] - ug
            pooled.append(pg.astype(BF16))
            ys.append(_dot(pooled[g], pgrp_ref[g]))
        ycat = jnp.concatenate(ys, axis=1)
        y = ycat * ps
        zs = _sigmoid(z)
        sz = z * zs
        p = (y * sz).astype(BF16)
        out = _dot(p, pwout_ref[...])
        x2 = x1 + gate * out
        r2 = _rstd(x2)
        xn2 = x2 * r2
        diff = xn2 * fw - t_ref[...]
        loss_o[...] += _colsum(diff * diff)
        dyf = diff * (1.0 / D)
        gfw_o[...] += _colsum(dyf * xn2)
        dxn2 = dyf * fw
        dx2 = r2 * (dxn2 - xn2 * jnp.mean(dxn2 * xn2, axis=-1, keepdims=True))
        dgate = _colsum(dx2 * out)
        dout = (dx2 * gate).astype(BF16)
        for j in range(4):
            cs = slice(j * PG, (j + 1) * PG)
            a_pwout[:, cs] += _dot_ta(p, dout[:, cs])
        dp = _dot_tb(dout, pwout_ref[...])
        dy = dp * sz
        dz = dp * y * (zs * (1.0 + z * (1.0 - zs)))
        gps_o[...] += _colsum(dy * ycat)
        dycat = dy * ps
        dus = []
        for g in range(4):
            dyg = dycat[:, g * PG:(g + 1) * PG].astype(BF16)
            a_pgrp[g] += _dot_ta(pooled[g], dyg)
            dpg = _dot_tb(dyg, pgrp_ref[g])
            dus.append(_dot01(pbt_ref[g], dpg * pinv_ref[g]) - dpg)
        duz = jnp.concatenate(dus + [dz], axis=1).astype(BF16)
        dhx = None
        for j in range(NDEV):
            dj = duz[:, j * SH_PWIN:(j + 1) * SH_PWIN]
            a_pwin[j] += _dot_ta(hxb, dj)
            part = _dot_tb(dj, pwin_ref[j])
            dhx = part if dhx is None else dhx + part
        dmod_o[0:1, :] += _colsum(dhx)
        dmod_o[1:2, :] += _colsum(dhx * a)
        dmod_o[2:3, :] += dgate
        da = dhx * (1.0 + scale)
        gnw_o[...] += _colsum(da * xn)
        dxn = da * nw
        dx_ref[...] = dx2 + r1 * (dxn - xn * jnp.mean(dxn * xn, axis=-1, keepdims=True))

        @pl.when(i == NTX - 1)
        def _():
            gpwin_o[...] = a_pwin[...].astype(BF16)
            gpgrp_o[...] = a_pgrp[...].astype(BF16)
            gpwout_o[...] = a_pwout[...].astype(BF16)

    tile = pl.BlockSpec((TM, D), lambda i: (i, 0))
    outs = (_sds((T, D), F32), _sds((NDEV, D, SH_PWIN), BF16), _sds((4, PG, PG), BF16), _sds((E, D), BF16),
            _sds((3, D), F32), _sds((1, D), F32), _sds((1, D), F32), _sds((1, E), F32), _sds((1, D), F32))
    return pl.pallas_call(
        body, name="pool_layer", grid=(NTX,),
        in_specs=[tile, tile] + [VMEM_SPEC] * 10,
        out_specs=[tile] + [VMEM_SPEC] * 8,
        out_shape=outs,
        scratch_shapes=[pltpu.VMEM((NDEV, D, SH_PWIN), F32), pltpu.VMEM((4, PG, PG), F32), pltpu.VMEM((E, D), F32)],
        compiler_params=pltpu.CompilerParams(dimension_semantics=("arbitrary",), vmem_limit_bytes=VMEM_LIMIT),
    )(x1, tgt, mod1, nw1, fnw, pwin, pgrp, pscale, pwout, pb, pbt, pinv)


def _b3_out_bwd(dx1, o, g_all, gate, gw, wout, gpwout):
    def body(dx_ref, o_ref, z_ref, gate_ref, gw_ref, w_ref, gpwout_r, do_ref, dz_ref, gw_o, dgate_o, ggw_o, rpwout_o,
             acc, ssem, rsem, lsem):
        i = pl.program_id(0)
        _scatter_rider(i, NT, ("rows",), (gpwout_r,), (rpwout_o,), ssem, rsem, lsem)

        @pl.when(i == 0)
        def _():
            acc[...] = jnp.zeros_like(acc)
            dgate_o[...] = jnp.zeros_like(dgate_o)
            ggw_o[...] = jnp.zeros_like(ggw_o)
            do_ref[...] = jnp.zeros_like(do_ref)
            dz_ref[...] = jnp.zeros_like(dz_ref)

        @pl.when(i > 0)
        def _():
            gw = gw_ref[...]
            z = z_ref[...]
            og, r, on, zs, sz = _gated_norm(_get_heads(o_ref), z, gw)
            ogb = og.astype(BF16)
            dx = dx_ref[...]
            dgate_o[...] += _colsum(dx * _dot(ogb, w_ref[...]))
            dy = (dx * gate_ref[...]).astype(BF16)
            for j in range(4):
                cs = slice(j * PG, (j + 1) * PG)
                acc[:, cs] += _dot_ta(ogb, dy[:, cs])
            dog = _dot_tb(dy, w_ref[...])
            dz_ref[...] = (dog * (on * gw) * (zs * (1.0 + z * (1.0 - zs)))).astype(BF16)
            dong = dog * sz
            ggw_o[...] += _colsum(dong * on)
            don = dong * gw
            do = _head_map(lambda dh, nh, rh: rh * (dh - nh * jnp.mean(dh * nh, axis=-1, keepdims=True)), don, on, r)
            _put_heads(do_ref, (), do.astype(BF16))

        @pl.when(i == NT - 1)
        def _():
            gw_o[...] = acc[...].astype(BF16)

    prev = lambda i: (jnp.maximum(i - 1, 0), 0)
    heads = pl.BlockSpec((HEADS, TM, DH), lambda i: (0, i, 0))
    return pl.pallas_call(
        body, name="b3_out_bwd", grid=(NT,),
        in_specs=[pl.BlockSpec((TM, D), prev), heads, pl.BlockSpec((TM, E), lambda i: (i, 4)),
                  VMEM_SPEC, VMEM_SPEC, VMEM_SPEC, HBM_SPEC],
        out_specs=[heads, pl.BlockSpec((TM, E), lambda i: (i, 0)), VMEM_SPEC, VMEM_SPEC, VMEM_SPEC, HBM_SPEC],
        out_shape=(_sds((HEADS, TT, DH), BF16), _sds((TT, E), BF16), _sds((E, D), BF16), _sds((1, D), F32), _sds((1, E), F32),
                   _sds((NDEV, SH_ROWS, D), BF16)),
        scratch_shapes=[pltpu.VMEM((E, D), F32)] + _rider_sems(1),
        compiler_params=pltpu.CompilerParams(dimension_semantics=("arbitrary",), vmem_limit_bytes=VMEM_LIMIT),
    )(dx1, o, g_all, gate, gw, wout, gpwout)


def _gla_bwd(p0, p1, v_all, dec, do, mask01, gpwin):
    nch = TM // CHUNK
    n_steps = HEADS // GLA_HB

    def body(p0_ref, p1_ref, v_ref, dec_ref, do_ref, msk_ref, gpwin_r, d0_ref, d1_ref, dv_ref, dgl_ref, rpwin_o,
             ss_sc, dv_sc, ssem, rsem, lsem):
        _scatter_rider(pl.program_id(0), n_steps, ("major",), (gpwin_r,), (rpwin_o,), ssem, rsem, lsem)

        lanes = [(d, hh) for d in (0, 1) for hh in range(GLA_HB)]
        zero = jnp.zeros((len(lanes), DH, DH), F32)
        dgl_ref[...] = jnp.zeros_like(dgl_ref)

        def p_of(d):
            return p1_ref if d else p0_ref

        def scan_step(i, n):
            where = [_scan_tile(i, d == 1) for d in (0, 1)]
            cis = [nch - 1 - n if d else n for d, _ in lanes]
            dec = jnp.stack([dec_ref[d, where[d][0], ci:ci + 1, hh * DH:(hh + 1) * DH] for (d, hh), ci in zip(lanes, cis)])

            def chunk(arr):
                return jnp.stack([arr[l, ci * CHUNK:(ci + 1) * CHUNK] for l, ci in enumerate(cis)])

            return where, cis, dec, chunk

        def stacked(i, fn):
            where = [_scan_tile(i, d == 1) for d in (0, 1)]
            return jnp.stack([fn(d, hh, where[d][1]) for d, hh in lanes])

        def fwd_body(i, st):
            v = stacked(i, lambda d, hh, rows: v_ref[hh, rows, :])
            kend = stacked(i, lambda d, hh, rows: p_of(d)[3, hh, rows, :])
            for n in range(nch):
                _, _, dec, chunk = scan_step(i, n)
                ss_sc[i * nch + n] = st
                st = st * dec + _bdot_tn(chunk(v), chunk(kend))
            return st

        lax.fori_loop(0, NT, fwd_body, zero)

        def bwd_body(ii, dst):
            i = NT - 1 - ii
            qg, kg, q_in, kend = [stacked(i, lambda d, hh, rows, ty=ty: p_of(d)[ty, hh, rows, :]) for ty in range(4)]
            v = stacked(i, lambda d, hh, rows: v_ref[hh, rows, :])
            dob = stacked(i, lambda d, hh, rows: do_ref[hh, rows, :])
            msk = jnp.stack([msk_ref[d] for d, _ in lanes])
            a = (_bdot_nt(qg, kg) * msk).astype(BF16)
            da = (_bdot_nt(dob, v) * msk).astype(BF16)
            dqg = _bdot(da, kg)
            dkg = _bdot_tn(da, qg)
            dv_intra = _bdot_tn(a, dob)
            dv_l, dkend_l, dqin_l = ([[None] * nch for _ in lanes] for _ in range(3))
            for n in range(nch - 1, -1, -1):
                where, cis, dec, chunk = scan_step(i, n)
                s_c = ss_sc[i * nch + n]
                dstb = dst.astype(BF16)
                kend_c, v_c, dob_c = chunk(kend), chunk(v), chunk(dob)
                dv_c = chunk(dv_intra) + _bdot_nt(kend_c, dstb)
                dkend_c = _bdot(v_c, dstb)
                dqin_c = _bdot(dob_c, s_c)
                dgl = jnp.sum(s_c * dst, axis=1, keepdims=True) * dec
                for l, ((d, hh), ci) in enumerate(zip(lanes, cis)):
                    dv_l[l][ci], dkend_l[l][ci], dqin_l[l][ci] = dv_c[l], dkend_c[l], dqin_c[l]
                    dgl_ref[d, where[d][0], ci:ci + 1, hh * DH:(hh + 1) * DH] = dgl[l]
                dst = dst * dec + _bdot_tn(dob_c, chunk(q_in))
            where = [_scan_tile(i, d == 1) for d in (0, 1)]
            for l, (d, hh) in enumerate(lanes):
                rows = where[d][1]
                d_ref = d1_ref if d else d0_ref
                d_ref[0, hh, rows, :] = dqg[l].astype(BF16)
                d_ref[1, hh, rows, :] = dkg[l].astype(BF16)
                d_ref[2, hh, rows, :] = jnp.concatenate(dqin_l[l], axis=0).astype(BF16)
                d_ref[3, hh, rows, :] = jnp.concatenate(dkend_l[l], axis=0).astype(BF16)
                dv_sc[d, hh, rows, :] = jnp.concatenate(dv_l[l], axis=0).astype(BF16)
            return dst

        lax.fori_loop(0, NT, bwd_body, zero)
        dv_ref[...] = (dv_sc[0].astype(F32) + dv_sc[1].astype(F32)).astype(BF16)

    quad = pl.BlockSpec((4, GLA_HB, TT, DH), lambda h: (0, h, 0, 0))
    col = pl.BlockSpec((GLA_HB, TT, DH), lambda h: (h, 0, 0))
    chunkv = pl.BlockSpec((2, NT, 8, GLA_HB * DH), lambda h: (0, 0, 0, h))
    outs = (_sds((4, HEADS, TT, DH), BF16), _sds((4, HEADS, TT, DH), BF16), _sds((HEADS, TT, DH), BF16), _sds((2, NT, 8, E), F32),
            _sds((NDEV, D, SH_PWIN), BF16))
    return pl.pallas_call(
        body, name="gla_bwd", grid=(n_steps,),
        in_specs=[quad, quad, col, chunkv, col, pl.BlockSpec((2, TM, TM), lambda h: (0, 0, 0)), HBM_SPEC],
        out_specs=[quad, quad, col, chunkv, HBM_SPEC],
        out_shape=outs,
        scratch_shapes=[pltpu.VMEM((NT * nch, 2 * GLA_HB, DH, DH), F32), pltpu.VMEM((2, GLA_HB, TT, DH), BF16)] + _rider_sems(1),
        compiler_params=pltpu.CompilerParams(dimension_semantics=("arbitrary",), vmem_limit_bytes=VMEM_LIMIT_SCAN),
    )(p0, p1, v_all, dec, do, mask01, gpwin)


TMB = 128


def _gla_post_bwd(g_all, d0, d1, dgl, dv, dz, lb, cum01, gwout, gpgrp):
    nch = TMB // CHUNK

    def body(g_ref, d0_ref, d1_ref, dgl_ref, dv_ref, dz_ref, lb_ref, cum_ref, gwout_r, gpgrp_r, dg_ref, dlb_ref, rwout_o, rpgrp_o,
             ssem, rsem, lsem):
        i = pl.program_id(0)
        _scatter_rider(i, TT // TMB, ("rows", "grp"), (gwout_r, gpgrp_r), (rwout_o, rpgrp_o), ssem, rsem, lsem)

        @pl.when(i == 0)
        def _():
            dlb_ref[...] = jnp.zeros_like(dlb_ref)

        half = i & 1
        qpre = g_ref[:, 3 * E:4 * E]
        dqs_sum = None
        dpre = []
        for d, d_ref in ((0, d0_ref), (1, d1_ref)):
            rev = d == 1
            lbd = lb_ref[d:d + 1, :]
            t = _gla_gates(g_ref[:, d * E:(d + 1) * E], qpre, lbd, cum_ref[d, :TMB, :TMB], rev)
            dqg, dkg, dqin, dkend = [_get_heads(d_ref, (ty,)).astype(F32) for ty in range(4)]
            dqs = dqg * t["e_q"] + dqin * t["e_in"]
            dk = dkg * t["e_k"] + dkend * t["e_end"]
            dkk = dkend * (t["k"] * t["e_end"])
            dg = t["qs"] * dqs - t["k"] * dk
            dkk3 = dkk.reshape(nch, CHUNK, E)
            dgl8 = dgl_ref[d, 0]
            dgl_rows = [jnp.where(half == 0, dgl8[ci:ci + 1, :], dgl8[nch + ci:nch + ci + 1, :]) for ci in range(nch)]
            dgl_b = jnp.concatenate([jnp.broadcast_to(dgl_rows[ci] + jnp.sum(dkk3[ci], axis=0, keepdims=True), (CHUNK, E))
                                     for ci in range(nch)], axis=0)
            pos = lax.broadcasted_iota(jnp.int32, (TMB, E), 0) & (CHUNK - 1)
            dg = dg + jnp.where(pos == (0 if rev else CHUNK - 1), dgl_b, 0.0)
            dlf = _dot01(cum_ref[1 - d, :TMB, :TMB], dg)
            df = dlf / t["f"] - dk
            sig = t["sig"]
            dpre.append((df * (1.0 - lbd) * sig * (1.0 - sig)).astype(BF16))
            dlb_ref[d:d + 1, :] += _colsum(df * (1.0 - sig))
            dqs_sum = dqs if dqs_sum is None else dqs_sum + dqs
            qsig = t["qsig"]
        dqpre = dqs_sum * (DH ** -0.5) * (qsig * (1.0 + qpre * (1.0 - qsig)))
        dg_ref[...] = jnp.concatenate([dpre[0], dpre[1], _get_heads(dv_ref), dqpre.astype(BF16), dz_ref[...]], axis=1)

    quad = pl.BlockSpec((4, HEADS, TMB, DH), lambda i: (0, 0, i, 0))
    tile = pl.BlockSpec((TMB, E), lambda i: (i, 0))
    return pl.pallas_call(
        body, name="gla_post_bwd", grid=(TT // TMB,),
        in_specs=[pl.BlockSpec((TMB, WIN_COLS), lambda i: (i, 0)), quad, quad,
                  pl.BlockSpec((2, 1, 8, E), lambda i: (0, i // 2, 0, 0)), pl.BlockSpec((HEADS, TMB, DH), lambda i: (0, i, 0)), tile,
                  VMEM_SPEC, VMEM_SPEC, HBM_SPEC, HBM_SPEC],
        out_specs=[pl.BlockSpec((TMB, WIN_COLS), lambda i: (i, 0)), VMEM_SPEC, HBM_SPEC, HBM_SPEC],
        out_shape=(_sds((TT, WIN_COLS), BF16), _sds((2, E), F32), _sds((NDEV, SH_ROWS, D), BF16), _sds((NDEV, 4, SH_GRP, PG), BF16)),
        scratch_shapes=_rider_sems(2),
        compiler_params=pltpu.CompilerParams(dimension_semantics=("arbitrary",), vmem_limit_bytes=VMEM_LIMIT),
    )(g_all, d0, d1, dgl, dv, dz, lb, cum01, gwout, gpgrp)


def _b1_in_bwd(idx1, ctx, x, dx1, dg, nw, msel, win):
    last_s = NDEV - 1

    def body(idx_ref, ctx_ref, x_ref, dx1_ref, dg_ref, nw_ref, m_ref, w_ref, gx_ref, rwin_o, dmx_o, dmc_o, gnw_o,
             hx_sc, dhx_sc, acc, sbuf, pbuf, psend, precv, isend, irecv, sibsem, lsem):
        del idx_ref
        s, i = pl.program_id(0), pl.program_id(1)
        x, y, cc, idx = _mesh_pos()
        shift, scale = m_ref[0, 0:1, :], m_ref[0, 1:2, :]
        sibling = (x, y, 1 - cc)

        def partial(p):
            return pltpu.make_async_remote_copy(src_ref=sbuf.at[0], dst_ref=pbuf.at[p], send_sem=psend.at[p], recv_sem=precv.at[p],
                                                device_id=sibling, device_id_type=MESH)

        def chip_sum(p):
            return pltpu.make_async_remote_copy(src_ref=sbuf.at[1], dst_ref=rwin_o.at[2 + p], send_sem=isend.at[p], recv_sem=irecv.at[p],
                                                device_id=_peer(x, y, cc, 2 * (p + 1)), device_id_type=MESH)

        to_sibling = pltpu.make_async_remote_copy(src_ref=sbuf.at[0], dst_ref=rwin_o.at[1], send_sem=sibsem.at[0], recv_sem=sibsem.at[1],
                                                  device_id=sibling, device_id_type=MESH)
        own = pltpu.make_async_copy(sbuf.at[1], rwin_o.at[0], lsem)

        @pl.when((s == 0) & (i == 0))
        def _():
            for ref in (dmx_o, dmc_o, gnw_o):
                ref[...] = jnp.zeros_like(ref)

        @pl.when(s == 0)
        def _():
            hx, _, _, _ = _modulated(_ctx_or_x(i, ctx_ref, x_ref), nw_ref[...], shift, scale)
            hx_sc[i] = hx.astype(BF16)

        @pl.when(i == 0)
        def _():
            acc[...] = jnp.zeros_like(acc)

        dgb = dg_ref[...]
        hxb = hx_sc[i]
        for lo, hi in ((0, 256), (256, 512), (512, SH_WIN)):
            acc[:, lo:hi] += _dot_ta(hxb, dgb[:, lo:hi])
        part = _dot_tb(dgb, w_ref[...])

        @pl.when(s == 0)
        def _():
            dhx_sc[i] = part

        @pl.when(s > 0)
        def _():
            dhx_sc[i] += part

        for p in (2, 1, 0):
            @pl.when((i == NT - 1) & (s == 2 * (2 - p)))
            def _(p=p):
                if p < 2:
                    partial(p + 1).wait_send()
                sbuf[0] = acc[...].astype(BF16)
                partial(p).start()

            @pl.when((i == NT - 1) & (s == 2 * (2 - p) + 1))
            def _(p=p):
                if p < 2:
                    chip_sum(p + 1).wait_send()
                partial(p).wait_recv()
                sbuf[1] = (acc[...] + pbuf[p].astype(F32)).astype(BF16)
                chip_sum(p).start()

        @pl.when((i == NT - 1) & (s == last_s - 1))
        def _():
            partial(0).wait_send()
            sbuf[0] = acc[...].astype(BF16)
            to_sibling.start()

        @pl.when((i == NT - 1) & (s == last_s))
        def _():
            chip_sum(0).wait_send()
            sbuf[1] = acc[...].astype(BF16)
            own.start()

        @pl.when(s == last_s)
        def _():
            nw = nw_ref[...]
            _, r, xn, a = _modulated(_ctx_or_x(i, ctx_ref, x_ref), nw, shift, scale)
            dhx = dhx_sc[i]
            dsh, dsc = _colsum(dhx), _colsum(dhx * a)
            da = dhx * (1.0 + scale)
            gnw_o[...] += _colsum(da * xn)
            dxn = da * nw
            gx_ref[...] = dx1_ref[...] + r * (dxn - xn * jnp.mean(dxn * xn, axis=-1, keepdims=True))

            @pl.when(i == 0)
            def _():
                dmc_o[0:1, :] += dsh
                dmc_o[1:2, :] += dsc

            @pl.when(i > 0)
            def _():
                dmx_o[0:1, :] += dsh
                dmx_o[1:2, :] += dsc

        @pl.when((i == NT - 1) & (s == last_s))
        def _():
            to_sibling.wait_send()
            to_sibling.wait_recv()
            for p in range(3):
                chip_sum(p).wait_recv()
            own.wait()

    grid_spec = pltpu.PrefetchScalarGridSpec(
        num_scalar_prefetch=1, grid=(NDEV, NT),
        in_specs=[VMEM_SPEC, pl.BlockSpec((TM, D), lambda s, i, ix: (jnp.maximum(i - 1, 0), 0)),
                  pl.BlockSpec((TM, D), lambda s, i, ix: (jnp.maximum(i - 1, 0), 0)),
                  pl.BlockSpec((TM, SH_WIN), lambda s, i, ix: (i, ix[0] ^ (last_s - s))), VMEM_SPEC,
                  pl.BlockSpec((1, 2, D), lambda s, i, ix: (jnp.minimum(i, 1), 0, 0)),
                  pl.BlockSpec((D, SH_WIN), lambda s, i, ix: (0, ix[0] ^ (last_s - s)))],
        out_specs=[pl.BlockSpec((TM, D), lambda s, i, ix: (jnp.where(s == last_s, jnp.maximum(i - 1, 0), 0), 0)),
                   HBM_SPEC, VMEM_SPEC, VMEM_SPEC, VMEM_SPEC],
        scratch_shapes=[pltpu.VMEM((NT, TM, D), BF16), pltpu.VMEM((NT, TM, D), F32), pltpu.VMEM((D, SH_WIN), F32),
                        pltpu.VMEM((2, D, SH_WIN), BF16), pltpu.VMEM((3, D, SH_WIN), BF16),
                        pltpu.SemaphoreType.DMA((3,)), pltpu.SemaphoreType.DMA((3,)), pltpu.SemaphoreType.DMA((3,)),
                        pltpu.SemaphoreType.DMA((3,)), pltpu.SemaphoreType.DMA((2,)), pltpu.SemaphoreType.DMA])
    return pl.pallas_call(
        body, name="b1_in_bwd", grid_spec=grid_spec,
        out_shape=(_sds((T, D), F32), _sds((RS_SLOTS, D, SH_WIN), BF16), _sds((2, D), F32), _sds((2, D), F32), _sds((1, D), F32)),
        compiler_params=pltpu.CompilerParams(dimension_semantics=("arbitrary", "arbitrary"), vmem_limit_bytes=VMEM_LIMIT),
    )(idx1, ctx, x, dx1, dg, nw, msel, win)


def _reduce_small(pd, pv, cg, c_ctx, ada_w0):
    n_arr = 3

    def body(pd_r, pv_r, cg_r, cctx_r, ada_r, gada_o, gadab_o, gcctx_o, pvsum_o, loss_o,
             pd_all, pv_all, dsc_all, dsc_mine, ssem, rsem):
        x, y, cc, idx = _mesh_pos()
        srcs = [pd_r, pv_r, dsc_mine]
        dsts = [pd_all.at[idx], pv_all.at[idx], dsc_all.at[idx]]

        def remote(a, k):
            return pltpu.make_async_remote_copy(src_ref=srcs[a], dst_ref=dsts[a], send_sem=ssem.at[a, k], recv_sem=rsem.at[a, k],
                                                device_id=_peer(x, y, cc, k), device_id_type=MESH)

        first = [remote(a, k) for k in range(1, NDEV) for a in (0, 1)]
        for cp in first:
            cp.start()
        pd_all[idx] = pd_r[...]
        pv_all[idx] = pv_r[...]
        for k in range(1, NDEV):
            remote(0, k).wait_recv()
            remote(1, k).wait_recv()
        mine = [pd_all[s, :, pl.ds(idx, 1), :] for s in range(NDEV)]
        dmc = functools.reduce(lambda u, v: u + v, [m[2] for m in mine])
        rows = _stack_rows([cg_r[i] for i in range(NDEV)] + [cctx_r[...]])
        sc = (rows * _sigmoid(rows)).astype(BF16)
        gada_o[0] = _dot_ta(sc, _stack_rows([m[0] for m in mine] + [dmc]))
        gada_o[1] = _dot_ta(sc, _stack_rows([m[1] for m in mine]))
        dsc_mine[...] = _dot_tb(jnp.broadcast_to(dmc, (8, SH_ADA)), ada_r[...])[0:1, :]
        dsc_all[idx] = dsc_mine[...]
        second = [remote(2, k) for k in range(1, NDEV)]
        for cp in second:
            cp.start()
        tot = [functools.reduce(lambda u, v: u + v, [pd_all[s, l] for s in range(NDEV)]) for l in range(3)]
        gadab_o[0] = tot[0] + tot[2]
        gadab_o[1] = tot[1]
        pvs = functools.reduce(lambda u, v: u + v, [pv_all[s] for s in range(NDEV)])
        pvsum_o[...] = pvs
        loss_o[...] = jnp.broadcast_to(jnp.sum(pvs[:, PV_LOSS:PV_LOSS + D], axis=-1, keepdims=True) * (0.5 / D), (1, 128))
        for k in range(1, NDEV):
            remote(2, k).wait_recv()
        dsc = functools.reduce(lambda u, v: u + v, [dsc_all[s] for s in range(NDEV)])
        cx = cctx_r[...]
        sx = _sigmoid(cx)
        gcctx_o[...] = dsc * (sx * (1.0 + cx * (1.0 - sx)))
        for cp in first + second:
            cp.wait_send()

    outs = (_sds((2, D, SH_ADA), F32), _sds((2, NDEV, SH_ADA), F32), _sds((1, D), F32), _sds((1, PV_LEN), F32), _sds((1, 128), F32))
    return pl.pallas_call(
        body, name="reduce_small", out_shape=outs,
        in_specs=[VMEM_SPEC] * 5, out_specs=[VMEM_SPEC] * 5,
        scratch_shapes=[
            pltpu.VMEM((NDEV, 3, NDEV, SH_ADA), F32), pltpu.VMEM((NDEV, 1, PV_LEN), F32), pltpu.VMEM((NDEV, 1, D), F32),
            pltpu.VMEM((1, D), F32),
            pltpu.SemaphoreType.DMA((n_arr, NDEV)), pltpu.SemaphoreType.DMA((n_arr, NDEV)),
        ],
        compiler_params=pltpu.CompilerParams(vmem_limit_bytes=VMEM_LIMIT),
    )(pd, pv, cg, c_ctx, ada_w0)


PV_NW, PV_GNORM, PV_FINAL, PV_LB, PV_PSCALE, PV_LOSS, PV_LEN = 0, 2 * D, 3 * D, 4 * D, 6 * D, 7 * D, 8 * D


def _adamw(w, g, m, v):
    m = ADAM_B1 * m + (1.0 - ADAM_B1) * g
    v = ADAM_B2 * v + (1.0 - ADAM_B2) * (g * g)
    m_hat = m / (1.0 - ADAM_B1 ** ADAM_STEP)
    v_hat = v / (1.0 - ADAM_B2 ** ADAM_STEP)
    delta = -ADAM_LR * (m_hat / (jnp.sqrt(v_hat) + ADAM_EPS) + ADAM_WD * w)
    return delta, m, v


def _adam_sharded(name, parts, w, m, v, tr):
    rr, cc = w.shape
    n = parts.shape[0]

    def body(p_ref, w_ref, m_ref, v_ref, g_o, d_o, m_o, v_o):
        g = p_ref[0].astype(F32)
        for s in range(1, n):
            g = g + p_ref[s].astype(F32)
        d, mn, vn = _adamw(w_ref[...], g, m_ref[...], v_ref[...])
        g_o[...], d_o[...], m_o[...], v_o[...] = g, d, mn, vn

    blk = pl.BlockSpec((tr, cc), lambda i: (i, 0))
    return pl.pallas_call(
        body, name=name, grid=(rr // tr,),
        in_specs=[pl.BlockSpec((n, tr, cc), lambda i: (0, i, 0)), blk, blk, blk],
        out_specs=[blk] * 4, out_shape=(_sds((rr, cc), F32),) * 4,
        compiler_params=pltpu.CompilerParams(dimension_semantics=("arbitrary",)),
    )(parts, w, m, v)


def _adam_dense(name, g, w, m, v, tr):
    rr, cc = w.shape

    def body(g_ref, w_ref, m_ref, v_ref, d_o, m_o, v_o):
        d, mn, vn = _adamw(w_ref[...], g_ref[...], m_ref[...], v_ref[...])
        d_o[...], m_o[...], v_o[...] = d, mn, vn

    blk = pl.BlockSpec((tr, cc), lambda i: (i, 0))
    return pl.pallas_call(
        body, name=name, grid=(rr // tr,), in_specs=[blk] * 4, out_specs=[blk] * 3, out_shape=(_sds((rr, cc), F32),) * 3,
        compiler_params=pltpu.CompilerParams(dimension_semantics=("arbitrary",)),
    )(g, w, m, v)


def _adam_small(gs, ws, ms, vs, lb_idx, lbv):
    n = len(ws)

    def body(*refs):
        g_r, w_r, m_r, v_r = refs[:n], refs[n:2 * n], refs[2 * n:3 * n], refs[3 * n:4 * n]
        lb_r = refs[4 * n]
        outs = refs[4 * n + 1:]
        for j in range(n):
            g = g_r[j][...]
            if j == lb_idx:
                lbj = lb_r[...]
                g = g * lbj * (1.0 - lbj)
            d, mn, vn = _adamw(w_r[j][...], g, m_r[j][...], v_r[j][...])
            outs[j][...], outs[n + j][...], outs[2 * n + j][...], outs[3 * n + j][...] = g, d, mn, vn

    shapes = tuple(_sds(w.shape, F32) for w in ws)
    return pl.pallas_call(body, name="adam_small", out_shape=shapes * 4)(*gs, *ws, *ms, *vs, lbv)


def kernel(x, c, ctx, c_ctx, ada_w, ada_b, norm_w, hgrn_w_in, hgrn_lb_logits, hgrn_gnorm_w, hgrn_w_out, pool_w_in, pool_w_grp, pool_scale, pool_w_out, final_norm_w, loss_target, m_c_ctx, m_ada_w, m_ada_b, m_norm_w, m_hgrn_w_in, m_hgrn_lb_logits, m_hgrn_gnorm_w, m_hgrn_w_out, m_pool_w_in, m_pool_w_grp, m_pool_scale, m_pool_w_out, m_final_norm_w, v_c_ctx, v_ada_w, v_ada_b, v_norm_w, v_hgrn_w_in, v_hgrn_lb_logits, v_hgrn_gnorm_w, v_hgrn_w_out, v_pool_w_in, v_pool_w_grp, v_pool_scale, v_pool_w_out, v_final_norm_w):
    idx = 4 * lax.axis_index("x") + 2 * lax.axis_index("y") + lax.axis_index("c")
    cctx2 = c_ctx.reshape(1, D)
    cum01, mask01 = _gla_consts()
    pb, pbt, pinv = _pool_consts()

    idx1 = idx.reshape(1).astype(jnp.int32)
    s_win, s_wout, s_pwin, s_pgrp, s_pwout, lbl_g, ps_g, cg, mod_g = _gather_small(
        hgrn_w_in[0], hgrn_w_out[0], pool_w_in[0], pool_w_grp[0], pool_w_out[0], hgrn_lb_logits[0], pool_scale, c, cctx2, ada_w)
    lb = jax.nn.sigmoid(jnp.transpose(lbl_g, (1, 0, 2)).reshape(2, E))
    pscale = ps_g.reshape(1, E)
    mod_all = jnp.transpose(mod_g, (1, 2, 0, 3)).reshape(2, 16, 3 * D) + ada_b[:, None, :]
    mod_me = lax.dynamic_index_in_dim(mod_all, idx, axis=1, keepdims=False)
    mod0, mod1, modc = mod_me[0].reshape(3, D), mod_me[1].reshape(3, D), mod_all[0, NDEV].reshape(3, D)
    msel = jnp.stack([modc[:2], mod0[:2]])
    nw0, nw1 = norm_w[0:1], norm_w[1:2]
    fnw = final_norm_w.reshape(1, D)

    g_all, win = _f1_gather_matmul(idx1, ctx[0], x[0], nw0, msel, s_win)
    p0, p1, v_all, dec, pwin, pgrp = _gla_prep(g_all, lb, cum01, s_pwin, s_pgrp)
    o, wout, pwout = _gla_fwd(p0, p1, v_all, dec, mask01, s_wout, s_pwout)
    x1 = _f3_out(o, g_all, x[0], mod0[2:3], hgrn_gnorm_w, wout)
    dx1, gpwin, gpgrp, gpwout, dmod1, gnw1, gfw, gps, lossv = _pool_layer(
        x1, loss_target[0], mod1, nw1, fnw, pwin, pgrp, pscale, pwout, pb, pbt, pinv)
    do, dz, gwout, dgate0, ggw, rpwout = _b3_out_bwd(dx1, o, g_all, mod0[2:3], hgrn_gnorm_w, wout, gpwout)
    d0, d1, dv, dgl, rpwin = _gla_bwd(p0, p1, v_all, dec, do, mask01, gpwin)
    dg, dlb, rwout, rpgrp = _gla_post_bwd(g_all, d0, d1, dgl, dv, dz, lb, cum01, gwout, gpgrp)
    grad_x, rwin, dmx, dmc, gnw0 = _b1_in_bwd(idx1, ctx[0], x[0], dx1, dg, nw0, msel, win)

    dmod0 = jnp.concatenate([dmx, dgate0], axis=0)
    dmodc = jnp.concatenate([dmc, jnp.zeros((1, D), F32)], axis=0)
    pd = jnp.stack([dmod0, dmod1, dmodc]).reshape(3, NDEV, SH_ADA)
    pv = jnp.concatenate([gnw0, gnw1, ggw, gfw, dlb.reshape(1, 2 * E), gps, lossv], axis=1)
    g_ada, g_adab, g_cctx, pvsum, loss128 = _reduce_small(pd, pv, cg, cctx2, ada_w[0])

    out = {}
    out["hgrn_w_in"] = _adam_sharded("adam_w_in", rwin, hgrn_w_in[0], m_hgrn_w_in[0], v_hgrn_w_in[0], 256)
    out["hgrn_w_out"] = _adam_sharded("adam_w_out", rwout, hgrn_w_out[0], m_hgrn_w_out[0], v_hgrn_w_out[0], SH_ROWS)
    out["pool_w_in"] = _adam_sharded("adam_pw_in", rpwin, pool_w_in[0], m_pool_w_in[0], v_pool_w_in[0], 512)
    out["pool_w_grp"] = _adam_sharded("adam_pgrp", rpgrp.reshape(NDEV, 4 * SH_GRP, PG), pool_w_grp[0].reshape(4 * SH_GRP, PG),
                                      m_pool_w_grp[0].reshape(4 * SH_GRP, PG), v_pool_w_grp[0].reshape(4 * SH_GRP, PG), 4 * SH_GRP)
    out["pool_w_out"] = _adam_sharded("adam_pw_out", rpwout, pool_w_out[0], m_pool_w_out[0], v_pool_w_out[0], SH_ROWS)
    g_ada2 = g_ada.reshape(2 * D, SH_ADA)
    out["ada_w"] = (g_ada2,) + _adam_dense("adam_ada_w", g_ada2, ada_w.reshape(2 * D, SH_ADA), m_ada_w.reshape(2 * D, SH_ADA),
                                           v_ada_w.reshape(2 * D, SH_ADA), 512)

    lb_me = lax.dynamic_slice_in_dim(lb, idx * DH, DH, axis=1)
    small = ["c_ctx", "ada_b", "norm_w", "hgrn_lb_logits", "hgrn_gnorm_w", "pool_scale", "final_norm_w"]
    gs = [g_cctx, g_adab.reshape(2, 3 * D), pvsum[:, PV_NW:PV_NW + 2 * D].reshape(2, D),
          lax.dynamic_slice_in_dim(pvsum[:, PV_LB:PV_LB + 2 * E].reshape(2, E), idx * DH, DH, axis=1),
          pvsum[:, PV_GNORM:PV_GNORM + E], lax.dynamic_slice_in_dim(pvsum[:, PV_PSCALE:PV_PSCALE + E], idx * DH, DH, axis=1),
          pvsum[:, PV_FINAL:PV_FINAL + D]]
    ws = [cctx2, ada_b, norm_w, hgrn_lb_logits[0], hgrn_gnorm_w, pool_scale, fnw]
    ms = [m_c_ctx.reshape(1, D), m_ada_b, m_norm_w, m_hgrn_lb_logits[0], m_hgrn_gnorm_w, m_pool_scale, m_final_norm_w.reshape(1, D)]
    vs = [v_c_ctx.reshape(1, D), v_ada_b, v_norm_w, v_hgrn_lb_logits[0], v_hgrn_gnorm_w, v_pool_scale, v_final_norm_w.reshape(1, D)]
    res = _adam_small(gs, ws, ms, vs, 3, lb_me)
    n = len(small)
    for j, name in enumerate(small):
        out[name] = tuple(res[q * n + j] for q in range(4))

    shapes = {"c_ctx": (D,), "ada_w": (2, D, SH_ADA), "ada_b": (2, 3 * D), "norm_w": (2, D), "hgrn_w_in": (1, D, SH_WIN),
              "hgrn_lb_logits": (1, 2, DH), "hgrn_gnorm_w": (1, E), "hgrn_w_out": (1, SH_ROWS, D), "pool_w_in": (1, D, SH_PWIN),
              "pool_w_grp": (1, 4, SH_GRP, PG), "pool_scale": (1, DH), "pool_w_out": (1, SH_ROWS, D), "final_norm_w": (D,)}
    order = ["c_ctx", "ada_w", "ada_b", "norm_w", "hgrn_w_in", "hgrn_lb_logits", "hgrn_gnorm_w", "hgrn_w_out", "pool_w_in",
             "pool_w_grp", "pool_scale", "pool_w_out", "final_norm_w"]
    flat = [out[name][q].reshape(shapes[name]) for q in range(4) for name in order]
    return (loss128[0, 0], grad_x[None], *flat)
```

```python
import functools

import numpy as np
import jax
import jax.numpy as jnp
from jax import lax
from jax.experimental import pallas as pl
from jax.experimental.pallas import tpu as pltpu

F32 = jnp.float32
BF16 = jnp.bfloat16

D = 1024
E = 1024
HEADS = 8
DH = 128
CHUNK = 64
T = 2048
TC = 256
TT = T + TC
TM = 256
NT = TT // TM
NTX = T // TM
NDEV = 8
GRID_W = 64
POOL_WINDOWS = (2, 4, 8, 16)
PG = 256
EPS = 1e-6
WIN_COLS = 5 * E
SH_WIN = WIN_COLS // NDEV
SH_PWIN = 2 * E // NDEV
SH_ROWS = E // NDEV
SH_GRP = PG // NDEV
SH_ADA = 3 * D // NDEV
VMEM_LIMIT = 56 * 1024 * 1024
VMEM_LIMIT_SCAN = 60 * 1024 * 1024

ADAM_LR, ADAM_B1, ADAM_B2, ADAM_EPS, ADAM_WD, ADAM_STEP = 0.001, 0.9, 0.999, 1e-08, 0.01, 10

MESH = pl.DeviceIdType.MESH
VMEM_SPEC = pl.BlockSpec(memory_space=pltpu.VMEM)
HBM_SPEC = pl.BlockSpec(memory_space=pltpu.HBM)
ANY_SPEC = pl.BlockSpec(memory_space=pl.ANY)


def _sds(shape, dtype):
    return jax.ShapeDtypeStruct(shape, dtype)


def _bf(a):
    return a if a.dtype == BF16 else a.astype(BF16)


def _dot(a, b):
    return lax.dot_general(_bf(a), _bf(b), (((1,), (0,)), ((), ())), preferred_element_type=F32)


def _dot_tb(a, b):
    return lax.dot_general(_bf(a), _bf(b), (((1,), (1,)), ((), ())), preferred_element_type=F32)


def _dot_ta(a, b):
    return lax.dot_general(_bf(a), _bf(b), (((0,), (0,)), ((), ())), preferred_element_type=F32)


def _bdot(a, b):
    return lax.dot_general(_bf(a), _bf(b), (((2,), (1,)), ((0,), (0,))), preferred_element_type=F32)


def _bdot_nt(a, b):
    return lax.dot_general(_bf(a), _bf(b), (((2,), (2,)), ((0,), (0,))), preferred_element_type=F32)


def _bdot_tn(a, b):
    return lax.dot_general(_bf(a), _bf(b), (((1,), (1,)), ((0,), (0,))), preferred_element_type=F32)


def _dot01(m01, x):
    hi = x.astype(BF16)
    lo = (x - hi.astype(F32)).astype(BF16)
    return _dot(m01, hi) + _dot(m01, lo)


def _rstd(x):
    return lax.rsqrt(jnp.mean(x * x, axis=-1, keepdims=True) + EPS)


def _sigmoid(x):
    return jax.nn.sigmoid(x)


def _colsum(a):
    return jnp.sum(a, axis=0, keepdims=True)


def _stack_rows(rows):
    n = rows[0].shape[-1]
    rid = lax.broadcasted_iota(jnp.int32, (16, n), 0)
    out = jnp.zeros((16, n), F32)
    for i, r in enumerate(rows):
        out = jnp.where(rid == i, r, out)
    return out


def _head_map(fn, *arrs):
    outs = [fn(*[a[:, h * DH:(h + 1) * DH] for a in arrs]) for h in range(HEADS)]
    return jnp.concatenate(outs, axis=1)


def _gla_consts():
    r = np.arange(TM)[:, None]
    c = np.arange(TM)[None, :]
    same = (r // CHUNK) == (c // CHUNK)
    tril = same & (c <= r)
    triu = same & (c >= r)
    m = np.stack([tril, triu]).astype(np.float32)
    return jnp.asarray(m, BF16), jnp.asarray(m, F32)


def _pool_consts():
    r = np.arange(TM)[:, None]
    c = np.arange(TM)[None, :]
    same = (r // GRID_W) == (c // GRID_W)
    rp, cp = r % GRID_W, c % GRID_W
    bs, inv = [], []
    for w in POOL_WINDOWS:
        lo = np.clip(rp - w // 2, 0, GRID_W)
        hi = np.clip(rp - w // 2 + w, 0, GRID_W)
        bs.append(same & (cp >= lo) & (cp < hi))
        inv.append(1.0 / (hi - lo).astype(np.float32))
    b = np.stack(bs).astype(np.float32)
    bt = np.transpose(b, (0, 2, 1))
    return jnp.asarray(b, BF16), jnp.asarray(bt, BF16), jnp.asarray(np.stack(inv), F32)


def _mesh_pos():
    x, y, c = lax.axis_index("x"), lax.axis_index("y"), lax.axis_index("c")
    return x, y, c, 4 * x + 2 * y + c


def _peer(x, y, c, k):
    return (x ^ ((k >> 2) & 1), y ^ ((k >> 1) & 1), c ^ (k & 1))


def _small_gathers(refs, ssem, rsem):
    lb_r, ps_r, c_r, cctx_r, ada_r, adab_r, lb_o, ps_o, cg_o, mod_o, lb_out, ps_out, cg_out, mod0_o, mod1_o, modc_o = refs
    x, y, cc, idx = _mesh_pos()
    srcs = [lb_r, ps_r, c_r, mod_o.at[idx]]
    mine = [lb_o.at[idx], ps_o.at[idx], cg_o.at[idx], mod_o.at[idx]]

    def remote(a, k):
        return pltpu.make_async_remote_copy(src_ref=srcs[a], dst_ref=mine[a], send_sem=ssem.at[a, k], recv_sem=rsem.at[a, k],
                                            device_id=_peer(x, y, cc, k), device_id_type=MESH)

    first = [remote(a, k) for k in range(1, NDEV) for a in (2, 0, 1)]
    for cp in first:
        cp.start()
    lb_o[idx] = lb_r[...]
    ps_o[idx] = ps_r[...]
    cg_o[idx] = c_r[...]
    for k in range(1, NDEV):
        remote(2, k).wait_recv()
    rows = _stack_rows([cg_o[i] for i in range(NDEV)] + [cctx_r[...]])
    sc = rows * _sigmoid(rows)
    for l in range(2):
        mod_o[idx, l] = _dot(sc, ada_r[l])
    second = [remote(3, k) for k in range(1, NDEV)]
    for cp in second:
        cp.start()
    for k in range(1, NDEV):
        remote(3, k).wait_recv()

    def mod_rows(l, row):
        full = jnp.concatenate([mod_o[s, l, row, :] for s in range(NDEV)], axis=1) + adab_r[l:l + 1, :]
        return [full[:, j * D:(j + 1) * D] for j in range(3)]

    me = pl.ds(idx, 1)
    for out, parts in ((mod0_o, mod_rows(0, me)), (mod1_o, mod_rows(1, me)), (modc_o, mod_rows(0, slice(NDEV, NDEV + 1)))):
        for j in range(3):
            out[j:j + 1, :] = parts[j]
    for cp in first + second:
        cp.wait_send()
    for k in range(1, NDEV):
        for a in (0, 1):
            remote(a, k).wait_recv()
    lb_out[...] = lb_o[...]
    ps_out[...] = ps_o[...]
    cg_out[...] = cg_o[...]


def _gather_order(s):
    if isinstance(s, int):
        return (0, 1, 2, 4, 3, 5, 6, 7)[s]
    return s + (s == 3).astype(jnp.int32) - (s == 4).astype(jnp.int32)


GATHER_ISSUE = (1, 2, 4, 3, 5, 6, 7)
GATHER_ICI = (2, 4, 6)
GATHER_DIRECT = (1,) + GATHER_ICI
GLA_HB = 2
RS_SLOTS = 5


def _shard_of(kind, ref, i):
    if kind == "rows":
        return ref.at[pl.ds(pl.multiple_of(i * SH_ROWS, SH_ROWS), SH_ROWS), :]
    if kind == "major":
        return ref.at[i]
    assert kind == "grp"
    return ref.at[:, pl.ds(pl.multiple_of(i * SH_GRP, SH_GRP), SH_GRP), :]


def _gather_rider(step, n_steps, forward_at, kinds, srcs, outs, ssem, rsem, lsem):
    x, y, cc, idx = _mesh_pos()
    arrays = range(len(kinds))
    mine = [_shard_of(kinds[a], outs[a], idx) for a in arrays]

    def remote(a, k):
        return pltpu.make_async_remote_copy(src_ref=srcs[a], dst_ref=mine[a], send_sem=ssem.at[a, k], recv_sem=rsem.at[a, k],
                                            device_id=_peer(x, y, cc, k), device_id_type=MESH)

    def forward(a, k):
        blk = _shard_of(kinds[a], outs[a], idx ^ k)
        return pltpu.make_async_remote_copy(src_ref=blk, dst_ref=blk, send_sem=ssem.at[a, k ^ 1], recv_sem=rsem.at[a, k ^ 1],
                                            device_id=(x, y, 1 - cc), device_id_type=MESH)

    copies = [remote(a, k) for k in GATHER_DIRECT for a in arrays]
    passed = [forward(a, k) for k in GATHER_ICI for a in arrays]
    local = [pltpu.make_async_copy(srcs[a], mine[a], lsem.at[a]) for a in arrays]

    @pl.when(step == 0)
    def _():
        for cp in copies + local:
            cp.start()

    @pl.when(step == forward_at)
    def _():
        for k in GATHER_ICI:
            for a in arrays:
                remote(a, k).wait_recv()
                forward(a, k).start()

    @pl.when(step == n_steps - 1)
    def _():
        for cp in copies + passed:
            cp.wait_send()
        for a in arrays:
            remote(a, 1).wait_recv()
        for cp in passed:
            cp.wait_recv()
        for cp in local:
            cp.wait()


def _scatter_rider(step, n_steps, kinds, grads, slots, ssem, rsem, lsem):
    x, y, cc, idx = _mesh_pos()
    arrays = range(len(kinds))
    dsts = [slots[a].at[idx] for a in arrays]

    def remote(a, k):
        px, py, pc = _peer(x, y, cc, k)
        return pltpu.make_async_remote_copy(src_ref=_shard_of(kinds[a], grads[a], 4 * px + 2 * py + pc), dst_ref=dsts[a],
                                            send_sem=ssem.at[a, k], recv_sem=rsem.at[a, k], device_id=(px, py, pc), device_id_type=MESH)

    copies = [remote(a, k) for k in GATHER_ISSUE for a in arrays]
    local = [pltpu.make_async_copy(_shard_of(kinds[a], grads[a], idx), dsts[a], lsem.at[a]) for a in arrays]

    @pl.when(step == 0)
    def _():
        for cp in copies + local:
            cp.start()

    @pl.when(step == n_steps - 1)
    def _():
        for cp in copies:
            cp.wait_send()
        for cp in copies:
            cp.wait_recv()
        for cp in local:
            cp.wait()


def _rider_sems(n):
    return [pltpu.SemaphoreType.DMA((n, NDEV)), pltpu.SemaphoreType.DMA((n, NDEV)), pltpu.SemaphoreType.DMA((n,))]


def _modulated(x, nw, shift, scale):
    r = _rstd(x)
    xn = x * r
    a = xn * nw
    return a * (1.0 + scale) + shift, r, xn, a


def _ctx_or_x(i, ctx_ref, x_ref):
    return jnp.where(i == 0, ctx_ref[...], x_ref[...])


def _f1_gather_matmul(idx1, ctx, x, nw, w_in, w_out, pw_in, pgrp, pw_out, lb_l, pscale, c, c_ctx, ada_w, ada_b):
    def body(idx_ref, ctx_ref, x_ref, nw_ref, win_r, wout_r, pwin_r, pgrp_r, pwout_r, lb_r, ps_r, c_r, cctx_r, ada_r, adab_r,
             g_ref, win_o, s_wout, s_pwin, s_pgrp, s_pwout, lb_o, ps_o, cg_o, mod0_o, mod1_o, modc_o,
             wslot, hx_sc, lb_g, ps_g, cg_g, mod_g, ssem, rsem, osem, sm_ssem, sm_rsem):
        del idx_ref
        s, i = pl.program_id(0), pl.program_id(1)
        x, y, cc, idx = _mesh_pos()
        k = _gather_order(s)
        j = idx ^ k

        def remote(kk):
            return pltpu.make_async_remote_copy(src_ref=wslot.at[idx], dst_ref=wslot.at[idx], send_sem=ssem.at[kk], recv_sem=rsem.at[kk],
                                                device_id=_peer(x, y, cc, kk), device_id_type=MESH)

        def forward(kk):
            jj = idx ^ kk
            return pltpu.make_async_remote_copy(src_ref=wslot.at[jj], dst_ref=wslot.at[jj], send_sem=ssem.at[kk ^ 1],
                                                recv_sem=rsem.at[kk ^ 1], device_id=(x, y, 1 - cc), device_id_type=MESH)

        def to_hbm(jj, kk):
            return pltpu.make_async_copy(wslot.at[jj], win_o.at[:, pl.ds(pl.multiple_of(jj * SH_WIN, 128), SH_WIN)], osem.at[kk])

        @pl.when((s == 0) & (i == 0))
        def _():
            wslot[idx] = win_r[...].astype(BF16)
            for kk in GATHER_DIRECT:
                remote(kk).start()
            s_wout[...] = wout_r[...].astype(BF16)
            s_pwin[...] = pwin_r[...].astype(BF16)
            s_pgrp[...] = pgrp_r[...].astype(BF16)
            s_pwout[...] = pwout_r[...].astype(BF16)
            _small_gathers((lb_r, ps_r, c_r, cctx_r, ada_r, adab_r, lb_g, ps_g, cg_g, mod_g, lb_o, ps_o, cg_o, mod0_o, mod1_o, modc_o),
                           sm_ssem, sm_rsem)

        @pl.when(s == 0)
        def _():
            shift = jnp.where(i == 0, modc_o[0:1, :], mod0_o[0:1, :])
            scale = jnp.where(i == 0, modc_o[1:2, :], mod0_o[1:2, :])
            hx, _, _, _ = _modulated(_ctx_or_x(i, ctx_ref, x_ref), nw_ref[...], shift, scale)
            hx_sc[i] = hx.astype(BF16)

        @pl.when((s > 0) & (i == 0))
        def _():
            remote(k).wait_recv()

            @pl.when((k & 1) == 0)
            def _():
                forward(k).start()

        @pl.when(i == 0)
        def _():
            to_hbm(j, k).start()

        g_ref[...] = jnp.dot(hx_sc[i], wslot[j], preferred_element_type=F32)

        @pl.when((s == NDEV - 1) & (i == NT - 1))
        def _():
            for kk in GATHER_DIRECT:
                remote(kk).wait_send()
            for kk in GATHER_ICI:
                forward(kk).wait_send()
            for kk in range(NDEV):
                to_hbm(idx ^ kk, kk).wait()

    grid_spec = pltpu.PrefetchScalarGridSpec(
        num_scalar_prefetch=1, grid=(NDEV, NT),
        in_specs=[VMEM_SPEC, pl.BlockSpec((TM, D), lambda s, i, ix: (jnp.maximum(i - 1, 0), 0))] + [VMEM_SPEC] * 12,
        out_specs=[pl.BlockSpec((TM, SH_WIN), lambda s, i, ix: (i, ix[0] ^ _gather_order(s))), HBM_SPEC] + [VMEM_SPEC] * 10,
        scratch_shapes=[pltpu.VMEM((NDEV, D, SH_WIN), BF16), pltpu.VMEM((NT, TM, D), BF16),
                        pltpu.VMEM((NDEV, 2, DH), F32), pltpu.VMEM((NDEV, 1, DH), F32), pltpu.VMEM((NDEV, 1, D), F32),
                        pltpu.VMEM((NDEV, 2, 16, SH_ADA), F32),
                        pltpu.SemaphoreType.DMA((NDEV,)), pltpu.SemaphoreType.DMA((NDEV,)), pltpu.SemaphoreType.DMA((NDEV,)),
                        pltpu.SemaphoreType.DMA((4, NDEV)), pltpu.SemaphoreType.DMA((4, NDEV))])
    outs = (_sds((TT, WIN_COLS), F32), _sds((D, WIN_COLS), BF16),
            _sds((SH_ROWS, D), BF16), _sds((D, SH_PWIN), BF16), _sds((4, SH_GRP, PG), BF16), _sds((SH_ROWS, D), BF16),
            _sds((NDEV, 2, DH), F32), _sds((NDEV, 1, DH), F32), _sds((NDEV, 1, D), F32),
            _sds((3, D), F32), _sds((3, D), F32), _sds((3, D), F32))
    return pl.pallas_call(
        body, name="f1_gather_matmul", grid_spec=grid_spec, out_shape=outs,
        compiler_params=pltpu.CompilerParams(dimension_semantics=("arbitrary", "arbitrary"), vmem_limit_bytes=VMEM_LIMIT),
    )(idx1, ctx, x, nw, w_in, w_out, pw_in, pgrp, pw_out, lb_l, pscale, c, c_ctx, ada_w, ada_b)


def _gla_gates(pre, qpre, lbd, cum, rev):
    rows, n = pre.shape
    nch = rows // CHUNK
    sig = _sigmoid(pre)
    f = lbd + (1.0 - lbd) * sig
    k = 1.0 - f
    g = _dot01(cum, jnp.log(f))
    g3 = g.reshape(nch, CHUNK, n)
    last = 0 if rev else CHUNK - 1
    mid = CHUNK // 2 if rev else CHUNK // 2 - 1
    gl1, gm1 = g3[:, last:last + 1, :], g3[:, mid:mid + 1, :]

    def bc(a):
        return jnp.broadcast_to(a, g3.shape).reshape(rows, n)

    gm = bc(gm1)
    e_q, e_k = jnp.exp(g - gm), jnp.exp(gm - g)
    e_in, e_end = e_q * bc(jnp.exp(gm1)), e_k * bc(jnp.exp(gl1 - gm1))
    qsig = _sigmoid(qpre)
    qs = qpre * qsig * (DH ** -0.5)
    return dict(sig=sig, f=f, k=k, qsig=qsig, qs=qs, e_q=e_q, e_k=e_k, e_in=e_in, e_end=e_end,
                decay=[jnp.exp(g3[ci, last:last + 1, :]) for ci in range(nch)])


def _put_heads(ref, lead, arr):
    for h in range(HEADS):
        ref[lead + (h,)] = arr[:, h * DH:(h + 1) * DH]


def _get_heads(ref, lead=()):
    return jnp.concatenate([ref[lead + (h,)] for h in range(HEADS)], axis=1)


def _gla_prep(g_all, lb, cum01, s_pwin, s_pgrp):
    def body(g_ref, lb_ref, cum_ref, spwin_r, spgrp_r, p0_ref, p1_ref, v_ref, dec_ref, pwin_o, pgrp_o, ssem, rsem, lsem):
        _gather_rider(pl.program_id(0), NT, NT // 2 + 1, ("major", "grp"), (spwin_r, spgrp_r), (pwin_o, pgrp_o), ssem, rsem, lsem)
        qpre = g_ref[:, 3 * E:4 * E]
        _put_heads(v_ref, (), g_ref[:, 2 * E:3 * E].astype(BF16))
        dec_ref[...] = jnp.zeros_like(dec_ref)
        for d, p_ref in ((0, p0_ref), (1, p1_ref)):
            t = _gla_gates(g_ref[:, d * E:(d + 1) * E], qpre, lb_ref[d:d + 1, :], cum_ref[d], d == 1)
            _put_heads(p_ref, (0,), (t["qs"] * t["e_q"]).astype(BF16))
            _put_heads(p_ref, (1,), (t["k"] * t["e_k"]).astype(BF16))
            _put_heads(p_ref, (2,), (t["qs"] * t["e_in"]).astype(BF16))
            _put_heads(p_ref, (3,), (t["k"] * t["e_end"]).astype(BF16))
            for ci in range(TM // CHUNK):
                dec_ref[d, 0, ci:ci + 1, :] = t["decay"][ci]

    quad = pl.BlockSpec((4, HEADS, TM, DH), lambda i: (0, 0, i, 0))
    return pl.pallas_call(
        body, name="gla_prep", grid=(NT,),
        in_specs=[pl.BlockSpec((TM, WIN_COLS), lambda i: (i, 0)), VMEM_SPEC, VMEM_SPEC, HBM_SPEC, HBM_SPEC],
        out_specs=[quad, quad, pl.BlockSpec((HEADS, TM, DH), lambda i: (0, i, 0)), pl.BlockSpec((2, 1, 8, E), lambda i: (0, i, 0, 0)),
                   HBM_SPEC, HBM_SPEC],
        out_shape=(_sds((4, HEADS, TT, DH), BF16), _sds((4, HEADS, TT, DH), BF16), _sds((HEADS, TT, DH), BF16), _sds((2, NT, 8, E), F32),
                   _sds((NDEV, D, SH_PWIN), BF16), _sds((4, PG, PG), BF16)),
        scratch_shapes=_rider_sems(2),
        compiler_params=pltpu.CompilerParams(dimension_semantics=("arbitrary",), vmem_limit_bytes=VMEM_LIMIT),
    )(g_all, lb, cum01, s_pwin, s_pgrp)


def _scan_tile(i, rev):
    t = jnp.where(i == 0, 0, NT - i) if rev else i
    return t, pl.ds(pl.multiple_of(t * TM, TM), TM)


def _chunk_order(rev):
    n = TM // CHUNK
    return tuple(range(n - 1, -1, -1)) if rev else tuple(range(n))


def _gla_fwd(p0, p1, v_all, dec, mask01, s_wout, s_pwout):
    n_steps = HEADS // GLA_HB

    def body(p0_ref, p1_ref, v_ref, dec_ref, msk_ref, swout_r, spwout_r, o_ref, wout_o, pwout_o, ob_sc, ssem, rsem, lsem):
        _gather_rider(pl.program_id(0), n_steps, n_steps // 2, ("rows", "rows"), (swout_r, spwout_r), (wout_o, pwout_o), ssem, rsem, lsem)

        lanes = [(d, hh) for d in (0, 1) for hh in range(GLA_HB)]
        nch = TM // CHUNK

        def tile_body(i, st):
            where = [_scan_tile(i, d == 1) for d in (0, 1)]

            def stacked(fn):
                return jnp.stack([fn(d, hh, where[d][1]) for d, hh in lanes])

            qg, kg, q_in, kend = [stacked(lambda d, hh, rows, ty=ty: (p1_ref if d else p0_ref)[ty, hh, rows, :]) for ty in range(4)]
            v = stacked(lambda d, hh, rows: v_ref[hh, rows, :])
            a = _bdot_nt(qg, kg) * jnp.stack([msk_ref[d] for d, _ in lanes])
            intra = _bdot(a, v)
            outs = [[None] * nch for _ in lanes]
            for n in range(nch):
                cis = [nch - 1 - n if d else n for d, _ in lanes]

                def chunk(arr):
                    return jnp.stack([arr[l, ci * CHUNK:(ci + 1) * CHUNK] for l, ci in enumerate(cis)])

                dec = jnp.stack([dec_ref[d, where[d][0], ci:ci + 1, hh * DH:(hh + 1) * DH] for (d, hh), ci in zip(lanes, cis)])
                inter = _bdot_nt(chunk(q_in), st)
                for l, ci in enumerate(cis):
                    outs[l][ci] = inter[l] + intra[l, ci * CHUNK:(ci + 1) * CHUNK]
                st = st * dec + _bdot_tn(chunk(v), chunk(kend))
            for l, (d, hh) in enumerate(lanes):
                (ob_sc if d else o_ref)[hh, where[d][1], :] = jnp.concatenate(outs[l], axis=0)
            return st

        lax.fori_loop(0, NT, tile_body, jnp.zeros((len(lanes), DH, DH), F32))
        o_ref[...] += ob_sc[...]

    quad = pl.BlockSpec((4, GLA_HB, TT, DH), lambda h: (0, h, 0, 0))
    head = pl.BlockSpec((GLA_HB, TT, DH), lambda h: (h, 0, 0))
    return pl.pallas_call(
        body, name="gla_fwd", grid=(n_steps,),
        in_specs=[quad, quad, head, pl.BlockSpec((2, NT, 8, GLA_HB * DH), lambda h: (0, 0, 0, h)),
                  pl.BlockSpec((2, TM, TM), lambda h: (0, 0, 0)), HBM_SPEC, HBM_SPEC],
        out_specs=[head, HBM_SPEC, HBM_SPEC],
        out_shape=(_sds((HEADS, TT, DH), F32), _sds((E, D), BF16), _sds((E, D), BF16)),
        scratch_shapes=[pltpu.VMEM((GLA_HB, TT, DH), F32)] + _rider_sems(2),
        compiler_params=pltpu.CompilerParams(dimension_semantics=("arbitrary",), vmem_limit_bytes=VMEM_LIMIT),
    )(p0, p1, v_all, dec, mask01, s_wout, s_pwout)


def _gated_norm(o, z, gw):
    r = _head_map(lambda oh: jnp.broadcast_to(_rstd(oh), oh.shape), o)
    on = o * r
    zs = _sigmoid(z)
    sz = z * zs
    return on * gw * sz, r, on, zs, sz


def _f3_out(o, g_all, x, gate, gw, wout):
    def body(o_ref, z_ref, x_ref, gate_ref, gw_ref, w_ref, x1_ref):
        og, _, _, _, _ = _gated_norm(_get_heads(o_ref), z_ref[...], gw_ref[...])
        x1_ref[...] = x_ref[...] + gate_ref[...] * _dot(og, w_ref[...])

    return pl.pallas_call(
        body, name="f3_out", grid=(NTX,),
        in_specs=[pl.BlockSpec((HEADS, TM, DH), lambda i: (0, i + 1, 0)), pl.BlockSpec((TM, E), lambda i: (i + 1, 4)),
                  pl.BlockSpec((TM, D), lambda i: (i, 0)), pl.BlockSpec((1, D), lambda i: (0, 0)),
                  pl.BlockSpec((1, E), lambda i: (0, 0)), pl.BlockSpec((E, D), lambda i: (0, 0))],
        out_specs=pl.BlockSpec((TM, D), lambda i: (i, 0)),
        out_shape=_sds((T, D), F32),
        compiler_params=pltpu.CompilerParams(dimension_semantics=("arbitrary",)),
    )(o, g_all, x, gate, gw, wout)


def _pool_layer(x1, tgt, mod1, nw1, fnw, pwin, pgrp, pscale, pwout, pb, pbt, pinv):
    def body(x_ref, t_ref, m_ref, nw_ref, fw_ref, pwin_ref, pgrp_ref, ps_ref, pwout_ref, pb_ref, pbt_ref, pinv_ref,
             dx_ref, gpwin_o, gpgrp_o, gpwout_o, dmod_o, gnw_o, gfw_o, gps_o, loss_o,
             a_pwin, a_pgrp, a_pwout):
        i = pl.program_id(0)

        @pl.when(i == 0)
        def _():
            for ref in (a_pwin, a_pgrp, a_pwout, dmod_o, gnw_o, gfw_o, gps_o, loss_o):
                ref[...] = jnp.zeros_like(ref)

        shift, scale, gate = m_ref[0:1, :], m_ref[1:2, :], m_ref[2:3, :]
        nw, fw, ps = nw_ref[...], fw_ref[...], ps_ref[...]
        x1 = x_ref[...]
        hx, r1, xn, a = _modulated(x1, nw, shift, scale)
        hxb = hx.astype(BF16)
        uz = jnp.concatenate([_dot(hxb, pwin_ref[j]) for j in range(NDEV)], axis=1)
        u, z = uz[:, :E], uz[:, E:]
        pooled, ys = [], []
        for g in range(4):
            ug = u[:, g * PG:(g + 1) * PG]
            pg = _dot01(pb_ref[g], ug) * pinv_ref[g] - ug
            pooled.append(pg.astype(BF16))
            ys.append(_dot(pooled[g], pgrp_ref[g]))
        ycat = jnp.concatenate(ys, axis=1)
        y = ycat * ps
        zs = _sigmoid(z)
        sz = z * zs
        p = (y * sz).astype(BF16)
        out = _dot(p, pwout_ref[...])
        x2 = x1 + gate * out
        r2 = _rstd(x2)
        xn2 = x2 * r2
        diff = xn2 * fw - t_ref[...]
        loss_o[...] += _colsum(diff * diff)
        dyf = diff * (1.0 / D)
        gfw_o[...] += _colsum(dyf * xn2)
        dxn2 = dyf * fw
        dx2 = r2 * (dxn2 - xn2 * jnp.mean(dxn2 * xn2, axis=-1, keepdims=True))
        dgate = _colsum(dx2 * out)
        dout = (dx2 * gate).astype(BF16)
        for j in range(4):
            cs = slice(j * PG, (j + 1) * PG)
            a_pwout[:, cs] += _dot_ta(p, dout[:, cs])
        dp = _dot_tb(dout, pwout_ref[...])
        dy = dp * sz
        dz = dp * y * (zs * (1.0 + z * (1.0 - zs)))
        gps_o[...] += _colsum(dy * ycat)
        dycat = dy * ps
        dus = []
        for g in range(4):
            dyg = dycat[:, g * PG:(g + 1) * PG].astype(BF16)
            a_pgrp[g] += _dot_ta(pooled[g], dyg)
            dpg = _dot_tb(dyg, pgrp_ref[g])
            dus.append(_dot01(pbt_ref[g], dpg * pinv_ref[g]) - dpg)
        duz = jnp.concatenate(dus + [dz], axis=1).astype(BF16)
        dhx = None
        for j in range(NDEV):
            dj = duz[:, j * SH_PWIN:(j + 1) * SH_PWIN]
            a_pwin[j] += _dot_ta(hxb, dj)
            part = _dot_tb(dj, pwin_ref[j])
            dhx = part if dhx is None else dhx + part
        dmod_o[0:1, :] += _colsum(dhx)
        dmod_o[1:2, :] += _colsum(dhx * a)
        dmod_o[2:3, :] += dgate
        da = dhx * (1.0 + scale)
        gnw_o[...] += _colsum(da * xn)
        dxn = da * nw
        dx_ref[...] = dx2 + r1 * (dxn - xn * jnp.mean(dxn * xn, axis=-1, keepdims=True))

        @pl.when(i == NTX - 1)
        def _():
            gpwin_o[...] = a_pwin[...].astype(BF16)
            gpgrp_o[...] = a_pgrp[...].astype(BF16)
            gpwout_o[...] = a_pwout[...].astype(BF16)

    tile = pl.BlockSpec((TM, D), lambda i: (i, 0))
    outs = (_sds((T, D), F32), _sds((NDEV, D, SH_PWIN), BF16), _sds((4, PG, PG), BF16), _sds((E, D), BF16),
            _sds((3, D), F32), _sds((1, D), F32), _sds((1, D), F32), _sds((1, E), F32), _sds((1, D), F32))
    return pl.pallas_call(
        body, name="pool_layer", grid=(NTX,),
        in_specs=[tile, tile] + [VMEM_SPEC] * 10,
        out_specs=[tile] + [VMEM_SPEC] * 8,
        out_shape=outs,
        scratch_shapes=[pltpu.VMEM((NDEV, D, SH_PWIN), F32), pltpu.VMEM((4, PG, PG), F32), pltpu.VMEM((E, D), F32)],
        compiler_params=pltpu.CompilerParams(dimension_semantics=("arbitrary",), vmem_limit_bytes=VMEM_LIMIT),
    )(x1, tgt, mod1, nw1, fnw, pwin, pgrp, pscale, pwout, pb, pbt, pinv)


def _b3_out_bwd(dx1, o, g_all, gate, gw, wout, gpwout):
    def body(dx_ref, o_ref, z_ref, gate_ref, gw_ref, w_ref, gpwout_r, do_ref, dz_ref, gw_o, dgate_o, ggw_o, rpwout_o,
             acc, ssem, rsem, lsem):
        i = pl.program_id(0)
        _scatter_rider(i, NT, ("rows",), (gpwout_r,), (rpwout_o,), ssem, rsem, lsem)

        @pl.when(i == 0)
        def _():
            acc[...] = jnp.zeros_like(acc)
            dgate_o[...] = jnp.zeros_like(dgate_o)
            ggw_o[...] = jnp.zeros_like(ggw_o)
            do_ref[...] = jnp.zeros_like(do_ref)
            dz_ref[...] = jnp.zeros_like(dz_ref)

        @pl.when(i > 0)
        def _():
            gw = gw_ref[...]
            z = z_ref[...]
            og, r, on, zs, sz = _gated_norm(_get_heads(o_ref), z, gw)
            ogb = og.astype(BF16)
            dx = dx_ref[...]
            dgate_o[...] += _colsum(dx * _dot(ogb, w_ref[...]))
            dy = (dx * gate_ref[...]).astype(BF16)
            for j in range(4):
                cs = slice(j * PG, (j + 1) * PG)
                acc[:, cs] += _dot_ta(ogb, dy[:, cs])
            dog = _dot_tb(dy, w_ref[...])
            dz_ref[...] = (dog * (on * gw) * (zs * (1.0 + z * (1.0 - zs)))).astype(BF16)
            dong = dog * sz
            ggw_o[...] += _colsum(dong * on)
            don = dong * gw
            do = _head_map(lambda dh, nh, rh: rh * (dh - nh * jnp.mean(dh * nh, axis=-1, keepdims=True)), don, on, r)
            _put_heads(do_ref, (), do.astype(BF16))

        @pl.when(i == NT - 1)
        def _():
            gw_o[...] = acc[...].astype(BF16)

    prev = lambda i: (jnp.maximum(i - 1, 0), 0)
    heads = pl.BlockSpec((HEADS, TM, DH), lambda i: (0, i, 0))
    return pl.pallas_call(
        body, name="b3_out_bwd", grid=(NT,),
        in_specs=[pl.BlockSpec((TM, D), prev), heads, pl.BlockSpec((TM, E), lambda i: (i, 4)),
                  VMEM_SPEC, VMEM_SPEC, VMEM_SPEC, HBM_SPEC],
        out_specs=[heads, pl.BlockSpec((TM, E), lambda i: (i, 0)), VMEM_SPEC, VMEM_SPEC, VMEM_SPEC, HBM_SPEC],
        out_shape=(_sds((HEADS, TT, DH), BF16), _sds((TT, E), BF16), _sds((E, D), BF16), _sds((1, D), F32), _sds((1, E), F32),
                   _sds((NDEV, SH_ROWS, D), BF16)),
        scratch_shapes=[pltpu.VMEM((E, D), F32)] + _rider_sems(1),
        compiler_params=pltpu.CompilerParams(dimension_semantics=("arbitrary",), vmem_limit_bytes=VMEM_LIMIT),
    )(dx1, o, g_all, gate, gw, wout, gpwout)


def _gla_bwd(p0, p1, v_all, dec, do, mask01, gpwin):
    nch = TM // CHUNK
    n_steps = HEADS // GLA_HB

    def body(p0_ref, p1_ref, v_ref, dec_ref, do_ref, msk_ref, gpwin_r, d0_ref, d1_ref, dv_ref, dgl_ref, rpwin_o,
             ss_sc, dv_sc, ssem, rsem, lsem):
        _scatter_rider(pl.program_id(0), n_steps, ("major",), (gpwin_r,), (rpwin_o,), ssem, rsem, lsem)

        lanes = [(d, hh) for d in (0, 1) for hh in range(GLA_HB)]
        zero = jnp.zeros((len(lanes), DH, DH), F32)
        dgl_ref[...] = jnp.zeros_like(dgl_ref)

        def p_of(d):
            return p1_ref if d else p0_ref

        def scan_step(i, n):
            where = [_scan_tile(i, d == 1) for d in (0, 1)]
            cis = [nch - 1 - n if d else n for d, _ in lanes]
            dec = jnp.stack([dec_ref[d, where[d][0], ci:ci + 1, hh * DH:(hh + 1) * DH] for (d, hh), ci in zip(lanes, cis)])

            def chunk(arr):
                return jnp.stack([arr[l, ci * CHUNK:(ci + 1) * CHUNK] for l, ci in enumerate(cis)])

            return where, cis, dec, chunk

        def stacked(i, fn):
            where = [_scan_tile(i, d == 1) for d in (0, 1)]
            return jnp.stack([fn(d, hh, where[d][1]) for d, hh in lanes])

        def fwd_body(i, st):
            v = stacked(i, lambda d, hh, rows: v_ref[hh, rows, :])
            kend = stacked(i, lambda d, hh, rows: p_of(d)[3, hh, rows, :])
            for n in range(nch):
                _, _, dec, chunk = scan_step(i, n)
                ss_sc[i * nch + n] = st
                st = st * dec + _bdot_tn(chunk(v), chunk(kend))
            return st

        lax.fori_loop(0, NT, fwd_body, zero)

        def bwd_body(ii, dst):
            i = NT - 1 - ii
            qg, kg, q_in, kend = [stacked(i, lambda d, hh, rows, ty=ty: p_of(d)[ty, hh, rows, :]) for ty in range(4)]
            v = stacked(i, lambda d, hh, rows: v_ref[hh, rows, :])
            dob = stacked(i, lambda d, hh, rows: do_ref[hh, rows, :])
            msk = jnp.stack([msk_ref[d] for d, _ in lanes])
            a = (_bdot_nt(qg, kg) * msk).astype(BF16)
            da = (_bdot_nt(dob, v) * msk).astype(BF16)
            dqg = _bdot(da, kg)
            dkg = _bdot_tn(da, qg)
            dv_intra = _bdot_tn(a, dob)
            dv_l, dkend_l, dqin_l = ([[None] * nch for _ in lanes] for _ in range(3))
            for n in range(nch - 1, -1, -1):
                where, cis, dec, chunk = scan_step(i, n)
                s_c = ss_sc[i * nch + n]
                dstb = dst.astype(BF16)
                kend_c, v_c, dob_c = chunk(kend), chunk(v), chunk(dob)
                dv_c = chunk(dv_intra) + _bdot_nt(kend_c, dstb)
                dkend_c = _bdot(v_c, dstb)
                dqin_c = _bdot(dob_c, s_c)
                dgl = jnp.sum(s_c * dst, axis=1, keepdims=True) * dec
                for l, ((d, hh), ci) in enumerate(zip(lanes, cis)):
                    dv_l[l][ci], dkend_l[l][ci], dqin_l[l][ci] = dv_c[l], dkend_c[l], dqin_c[l]
                    dgl_ref[d, where[d][0], ci:ci + 1, hh * DH:(hh + 1) * DH] = dgl[l]
                dst = dst * dec + _bdot_tn(dob_c, chunk(q_in))
            where = [_scan_tile(i, d == 1) for d in (0, 1)]
            for l, (d, hh) in enumerate(lanes):
                rows = where[d][1]
                d_ref = d1_ref if d else d0_ref
                d_ref[0, hh, rows, :] = dqg[l].astype(BF16)
                d_ref[1, hh, rows, :] = dkg[l].astype(BF16)
                d_ref[2, hh, rows, :] = jnp.concatenate(dqin_l[l], axis=0).astype(BF16)
                d_ref[3, hh, rows, :] = jnp.concatenate(dkend_l[l], axis=0).astype(BF16)
                dv_sc[d, hh, rows, :] = jnp.concatenate(dv_l[l], axis=0).astype(BF16)
            return dst

        lax.fori_loop(0, NT, bwd_body, zero)
        dv_ref[...] = (dv_sc[0].astype(F32) + dv_sc[1].astype(F32)).astype(BF16)

    quad = pl.BlockSpec((4, GLA_HB, TT, DH), lambda h: (0, h, 0, 0))
    col = pl.BlockSpec((GLA_HB, TT, DH), lambda h: (h, 0, 0))
    chunkv = pl.BlockSpec((2, NT, 8, GLA_HB * DH), lambda h: (0, 0, 0, h))
    outs = (_sds((4, HEADS, TT, DH), BF16), _sds((4, HEADS, TT, DH), BF16), _sds((HEADS, TT, DH), BF16), _sds((2, NT, 8, E), F32),
            _sds((NDEV, D, SH_PWIN), BF16))
    return pl.pallas_call(
        body, name="gla_bwd", grid=(n_steps,),
        in_specs=[quad, quad, col, chunkv, col, pl.BlockSpec((2, TM, TM), lambda h: (0, 0, 0)), HBM_SPEC],
        out_specs=[quad, quad, col, chunkv, HBM_SPEC],
        out_shape=outs,
        scratch_shapes=[pltpu.VMEM((NT * nch, 2 * GLA_HB, DH, DH), F32), pltpu.VMEM((2, GLA_HB, TT, DH), BF16)] + _rider_sems(1),
        compiler_params=pltpu.CompilerParams(dimension_semantics=("arbitrary",), vmem_limit_bytes=VMEM_LIMIT_SCAN),
    )(p0, p1, v_all, dec, do, mask01, gpwin)


TMB = 128


def _gla_post_bwd(g_all, d0, d1, dgl, dv, dz, lb, cum01, gwout, gpgrp):
    nch = TMB // CHUNK

    def body(g_ref, d0_ref, d1_ref, dgl_ref, dv_ref, dz_ref, lb_ref, cum_ref, gwout_r, gpgrp_r, dg_ref, dlb_ref, rwout_o, rpgrp_o,
             ssem, rsem, lsem):
        i = pl.program_id(0)
        _scatter_rider(i, TT // TMB, ("rows", "grp"), (gwout_r, gpgrp_r), (rwout_o, rpgrp_o), ssem, rsem, lsem)

        @pl.when(i == 0)
        def _():
            dlb_ref[...] = jnp.zeros_like(dlb_ref)

        half = i & 1
        qpre = g_ref[:, 3 * E:4 * E]
        dqs_sum = None
        dpre = []
        for d, d_ref in ((0, d0_ref), (1, d1_ref)):
            rev = d == 1
            lbd = lb_ref[d:d + 1, :]
            t = _gla_gates(g_ref[:, d * E:(d + 1) * E], qpre, lbd, cum_ref[d, :TMB, :TMB], rev)
            dqg, dkg, dqin, dkend = [_get_heads(d_ref, (ty,)).astype(F32) for ty in range(4)]
            dqs = dqg * t["e_q"] + dqin * t["e_in"]
            dk = dkg * t["e_k"] + dkend * t["e_end"]
            dkk = dkend * (t["k"] * t["e_end"])
            dg = t["qs"] * dqs - t["k"] * dk
            dkk3 = dkk.reshape(nch, CHUNK, E)
            dgl8 = dgl_ref[d, 0]
            dgl_rows = [jnp.where(half == 0, dgl8[ci:ci + 1, :], dgl8[nch + ci:nch + ci + 1, :]) for ci in range(nch)]
            dgl_b = jnp.concatenate([jnp.broadcast_to(dgl_rows[ci] + jnp.sum(dkk3[ci], axis=0, keepdims=True), (CHUNK, E))
                                     for ci in range(nch)], axis=0)
            pos = lax.broadcasted_iota(jnp.int32, (TMB, E), 0) & (CHUNK - 1)
            dg = dg + jnp.where(pos == (0 if rev else CHUNK - 1), dgl_b, 0.0)
            dlf = _dot01(cum_ref[1 - d, :TMB, :TMB], dg)
            df = dlf / t["f"] - dk
            sig = t["sig"]
            dpre.append((df * (1.0 - lbd) * sig * (1.0 - sig)).astype(BF16))
            dlb_ref[d:d + 1, :] += _colsum(df * (1.0 - sig))
            dqs_sum = dqs if dqs_sum is None else dqs_sum + dqs
            qsig = t["qsig"]
        dqpre = dqs_sum * (DH ** -0.5) * (qsig * (1.0 + qpre * (1.0 - qsig)))
        dg_ref[...] = jnp.concatenate([dpre[0], dpre[1], _get_heads(dv_ref), dqpre.astype(BF16), dz_ref[...]], axis=1)

    quad = pl.BlockSpec((4, HEADS, TMB, DH), lambda i: (0, 0, i, 0))
    tile = pl.BlockSpec((TMB, E), lambda i: (i, 0))
    return pl.pallas_call(
        body, name="gla_post_bwd", grid=(TT // TMB,),
        in_specs=[pl.BlockSpec((TMB, WIN_COLS), lambda i: (i, 0)), quad, quad,
                  pl.BlockSpec((2, 1, 8, E), lambda i: (0, i // 2, 0, 0)), pl.BlockSpec((HEADS, TMB, DH), lambda i: (0, i, 0)), tile,
                  VMEM_SPEC, VMEM_SPEC, HBM_SPEC, HBM_SPEC],
        out_specs=[pl.BlockSpec((TMB, WIN_COLS), lambda i: (i, 0)), VMEM_SPEC, HBM_SPEC, HBM_SPEC],
        out_shape=(_sds((TT, WIN_COLS), BF16), _sds((2, E), F32), _sds((NDEV, SH_ROWS, D), BF16), _sds((NDEV, 4, SH_GRP, PG), BF16)),
        scratch_shapes=_rider_sems(2),
        compiler_params=pltpu.CompilerParams(dimension_semantics=("arbitrary",), vmem_limit_bytes=VMEM_LIMIT),
    )(g_all, d0, d1, dgl, dv, dz, lb, cum01, gwout, gpgrp)


def _b1_in_bwd(idx1, ctx, x, dx1, dg, nw, msel, win):
    last_s = NDEV - 1

    def body(idx_ref, ctx_ref, x_ref, dx1_ref, dg_ref, nw_ref, m_ref, w_ref, gx_ref, rwin_o, dmx_o, dmc_o, gnw_o,
             hx_sc, dhx_sc, acc, sbuf, pbuf, psend, precv, isend, irecv, sibsem, lsem):
        del idx_ref
        s, i = pl.program_id(0), pl.program_id(1)
        x, y, cc, idx = _mesh_pos()
        shift, scale = m_ref[0, 0:1, :], m_ref[0, 1:2, :]
        sibling = (x, y, 1 - cc)

        def partial(p):
            return pltpu.make_async_remote_copy(src_ref=sbuf.at[0], dst_ref=pbuf.at[p], send_sem=psend.at[p], recv_sem=precv.at[p],
                                                device_id=sibling, device_id_type=MESH)

        def chip_sum(p):
            return pltpu.make_async_remote_copy(src_ref=sbuf.at[1], dst_ref=rwin_o.at[2 + p], send_sem=isend.at[p], recv_sem=irecv.at[p],
                                                device_id=_peer(x, y, cc, 2 * (p + 1)), device_id_type=MESH)

        to_sibling = pltpu.make_async_remote_copy(src_ref=sbuf.at[0], dst_ref=rwin_o.at[1], send_sem=sibsem.at[0], recv_sem=sibsem.at[1],
                                                  device_id=sibling, device_id_type=MESH)
        own = pltpu.make_async_copy(sbuf.at[1], rwin_o.at[0], lsem)

        @pl.when((s == 0) & (i == 0))
        def _():
            for ref in (dmx_o, dmc_o, gnw_o):
                ref[...] = jnp.zeros_like(ref)

        @pl.when(s == 0)
        def _():
            hx, _, _, _ = _modulated(_ctx_or_x(i, ctx_ref, x_ref), nw_ref[...], shift, scale)
            hx_sc[i] = hx.astype(BF16)

        @pl.when(i == 0)
        def _():
            acc[...] = jnp.zeros_like(acc)

        dgb = dg_ref[...]
        hxb = hx_sc[i]
        for lo, hi in ((0, 256), (256, 512), (512, SH_WIN)):
            acc[:, lo:hi] += _dot_ta(hxb, dgb[:, lo:hi])
        part = _dot_tb(dgb, w_ref[...])

        @pl.when(s == 0)
        def _():
            dhx_sc[i] = part

        @pl.when(s > 0)
        def _():
            dhx_sc[i] += part

        for p in (2, 1, 0):
            @pl.when((i == NT - 1) & (s == 2 * (2 - p)))
            def _(p=p):
                if p < 2:
                    partial(p + 1).wait_send()
                sbuf[0] = acc[...].astype(BF16)
                partial(p).start()

            @pl.when((i == NT - 1) & (s == 2 * (2 - p) + 1))
            def _(p=p):
                if p < 2:
                    chip_sum(p + 1).wait_send()
                partial(p).wait_recv()
                sbuf[1] = (acc[...] + pbuf[p].astype(F32)).astype(BF16)
                chip_sum(p).start()

        @pl.when((i == NT - 1) & (s == last_s - 1))
        def _():
            partial(0).wait_send()
            sbuf[0] = acc[...].astype(BF16)
            to_sibling.start()

        @pl.when((i == NT - 1) & (s == last_s))
        def _():
            chip_sum(0).wait_send()
            sbuf[1] = acc[...].astype(BF16)
            own.start()

        @pl.when(s == last_s)
        def _():
            nw = nw_ref[...]
            _, r, xn, a = _modulated(_ctx_or_x(i, ctx_ref, x_ref), nw, shift, scale)
            dhx = dhx_sc[i]
            dsh, dsc = _colsum(dhx), _colsum(dhx * a)
            da = dhx * (1.0 + scale)
            gnw_o[...] += _colsum(da * xn)
            dxn = da * nw
            gx_ref[...] = dx1_ref[...] + r * (dxn - xn * jnp.mean(dxn * xn, axis=-1, keepdims=True))

            @pl.when(i == 0)
            def _():
                dmc_o[0:1, :] += dsh
                dmc_o[1:2, :] += dsc

            @pl.when(i > 0)
            def _():
                dmx_o[0:1, :] += dsh
                dmx_o[1:2, :] += dsc

        @pl.when((i == NT - 1) & (s == last_s))
        def _():
            to_sibling.wait_send()
            to_sibling.wait_recv()
            for p in range(3):
                chip_sum(p).wait_recv()
            own.wait()

    grid_spec = pltpu.PrefetchScalarGridSpec(
        num_scalar_prefetch=1, grid=(NDEV, NT),
        in_specs=[VMEM_SPEC, pl.BlockSpec((TM, D), lambda s, i, ix: (jnp.maximum(i - 1, 0), 0)),
                  pl.BlockSpec((TM, D), lambda s, i, ix: (jnp.maximum(i - 1, 0), 0)),
                  pl.BlockSpec((TM, SH_WIN), lambda s, i, ix: (i, ix[0] ^ (last_s - s))), VMEM_SPEC,
                  pl.BlockSpec((1, 2, D), lambda s, i, ix: (jnp.minimum(i, 1), 0, 0)),
                  pl.BlockSpec((D, SH_WIN), lambda s, i, ix: (0, ix[0] ^ (last_s - s)))],
        out_specs=[pl.BlockSpec((TM, D), lambda s, i, ix: (jnp.where(s == last_s, jnp.maximum(i - 1, 0), 0), 0)),
                   HBM_SPEC, VMEM_SPEC, VMEM_SPEC, VMEM_SPEC],
        scratch_shapes=[pltpu.VMEM((NT, TM, D), BF16), pltpu.VMEM((NT, TM, D), F32), pltpu.VMEM((D, SH_WIN), F32),
                        pltpu.VMEM((2, D, SH_WIN), BF16), pltpu.VMEM((3, D, SH_WIN), BF16),
                        pltpu.SemaphoreType.DMA((3,)), pltpu.SemaphoreType.DMA((3,)), pltpu.SemaphoreType.DMA((3,)),
                        pltpu.SemaphoreType.DMA((3,)), pltpu.SemaphoreType.DMA((2,)), pltpu.SemaphoreType.DMA])
    return pl.pallas_call(
        body, name="b1_in_bwd", grid_spec=grid_spec,
        out_shape=(_sds((T, D), F32), _sds((RS_SLOTS, D, SH_WIN), BF16), _sds((2, D), F32), _sds((2, D), F32), _sds((1, D), F32)),
        compiler_params=pltpu.CompilerParams(dimension_semantics=("arbitrary", "arbitrary"), vmem_limit_bytes=VMEM_LIMIT),
    )(idx1, ctx, x, dx1, dg, nw, msel, win)


def _reduce_small(pd, pv, cg, c_ctx, ada_w0):
    n_arr = 3

    def body(pd_r, pv_r, cg_r, cctx_r, ada_r, gada_o, gadab_o, gcctx_o, pvsum_o, loss_o,
             pd_all, pv_all, dsc_all, dsc_mine, ssem, rsem):
        x, y, cc, idx = _mesh_pos()
        srcs = [pd_r, pv_r, dsc_mine]
        dsts = [pd_all.at[idx], pv_all.at[idx], dsc_all.at[idx]]

        def remote(a, k):
            return pltpu.make_async_remote_copy(src_ref=srcs[a], dst_ref=dsts[a], send_sem=ssem.at[a, k], recv_sem=rsem.at[a, k],
                                                device_id=_peer(x, y, cc, k), device_id_type=MESH)

        first = [remote(a, k) for k in range(1, NDEV) for a in (0, 1)]
        for cp in first:
            cp.start()
        pd_all[idx] = pd_r[...]
        pv_all[idx] = pv_r[...]
        for k in range(1, NDEV):
            remote(0, k).wait_recv()
            remote(1, k).wait_recv()
        mine = [pd_all[s, :, pl.ds(idx, 1), :] for s in range(NDEV)]
        dmc = functools.reduce(lambda u, v: u + v, [m[2] for m in mine])
        rows = _stack_rows([cg_r[i] for i in range(NDEV)] + [cctx_r[...]])
        sc = (rows * _sigmoid(rows)).astype(BF16)
        gada_o[0] = _dot_ta(sc, _stack_rows([m[0] for m in mine] + [dmc]))
        gada_o[1] = _dot_ta(sc, _stack_rows([m[1] for m in mine]))
        dsc_mine[...] = _dot_tb(jnp.broadcast_to(dmc, (8, SH_ADA)), ada_r[...])[0:1, :]
        dsc_all[idx] = dsc_mine[...]
        second = [remote(2, k) for k in range(1, NDEV)]
        for cp in second:
            cp.start()
        tot = [functools.reduce(lambda u, v: u + v, [pd_all[s, l] for s in range(NDEV)]) for l in range(3)]
        gadab_o[0] = tot[0] + tot[2]
        gadab_o[1] = tot[1]
        pvs = functools.reduce(lambda u, v: u + v, [pv_all[s] for s in range(NDEV)])
        pvsum_o[...] = pvs
        loss_o[...] = jnp.broadcast_to(jnp.sum(pvs[:, PV_LOSS:PV_LOSS + D], axis=-1, keepdims=True) * (0.5 / D), (1, 128))
        for k in range(1, NDEV):
            remote(2, k).wait_recv()
        dsc = functools.reduce(lambda u, v: u + v, [dsc_all[s] for s in range(NDEV)])
        cx = cctx_r[...]
        sx = _sigmoid(cx)
        gcctx_o[...] = dsc * (sx * (1.0 + cx * (1.0 - sx)))
        for cp in first + second:
            cp.wait_send()

    outs = (_sds((2, D, SH_ADA), F32), _sds((2, NDEV, SH_ADA), F32), _sds((1, D), F32), _sds((1, PV_LEN), F32), _sds((1, 128), F32))
    return pl.pallas_call(
        body, name="reduce_small", out_shape=outs,
        in_specs=[VMEM_SPEC] * 5, out_specs=[VMEM_SPEC] * 5,
        scratch_shapes=[
            pltpu.VMEM((NDEV, 3, NDEV, SH_ADA), F32), pltpu.VMEM((NDEV, 1, PV_LEN), F32), pltpu.VMEM((NDEV, 1, D), F32),
            pltpu.VMEM((1, D), F32),
            pltpu.SemaphoreType.DMA((n_arr, NDEV)), pltpu.SemaphoreType.DMA((n_arr, NDEV)),
        ],
        compiler_params=pltpu.CompilerParams(vmem_limit_bytes=VMEM_LIMIT),
    )(pd, pv, cg, c_ctx, ada_w0)


PV_NW, PV_GNORM, PV_FINAL, PV_LB, PV_PSCALE, PV_LOSS, PV_LEN = 0, 2 * D, 3 * D, 4 * D, 6 * D, 7 * D, 8 * D


def _adamw(w, g, m, v):
    m = ADAM_B1 * m + (1.0 - ADAM_B1) * g
    v = ADAM_B2 * v + (1.0 - ADAM_B2) * (g * g)
    m_hat = m / (1.0 - ADAM_B1 ** ADAM_STEP)
    v_hat = v / (1.0 - ADAM_B2 ** ADAM_STEP)
    delta = -ADAM_LR * (m_hat / (jnp.sqrt(v_hat) + ADAM_EPS) + ADAM_WD * w)
    return delta, m, v


def _adam_sharded(name, parts, w, m, v, tr):
    rr, cc = w.shape
    n = parts.shape[0]

    def body(p_ref, w_ref, m_ref, v_ref, g_o, d_o, m_o, v_o):
        g = p_ref[0].astype(F32)
        for s in range(1, n):
            g = g + p_ref[s].astype(F32)
        d, mn, vn = _adamw(w_ref[...], g, m_ref[...], v_ref[...])
        g_o[...], d_o[...], m_o[...], v_o[...] = g, d, mn, vn

    blk = pl.BlockSpec((tr, cc), lambda i: (i, 0))
    return pl.pallas_call(
        body, name=name, grid=(rr // tr,),
        in_specs=[pl.BlockSpec((n, tr, cc), lambda i: (0, i, 0)), blk, blk, blk],
        out_specs=[blk] * 4, out_shape=(_sds((rr, cc), F32),) * 4,
        compiler_params=pltpu.CompilerParams(dimension_semantics=("arbitrary",)),
    )(parts, w, m, v)


def _adam_dense(name, g, w, m, v, tr):
    rr, cc = w.shape

    def body(g_ref, w_ref, m_ref, v_ref, d_o, m_o, v_o):
        d, mn, vn = _adamw(w_ref[...], g_ref[...], m_ref[...], v_ref[...])
        d_o[...], m_o[...], v_o[...] = d, mn, vn

    blk = pl.BlockSpec((tr, cc), lambda i: (i, 0))
    return pl.pallas_call(
        body, name=name, grid=(rr // tr,), in_specs=[blk] * 4, out_specs=[blk] * 3, out_shape=(_sds((rr, cc), F32),) * 3,
        compiler_params=pltpu.CompilerParams(dimension_semantics=("arbitrary",)),
    )(g, w, m, v)


def _adam_small(gs, ws, ms, vs, lb_idx, lbv):
    n = len(ws)

    def body(*refs):
        g_r, w_r, m_r, v_r = refs[:n], refs[n:2 * n], refs[2 * n:3 * n], refs[3 * n:4 * n]
        lb_r = refs[4 * n]
        outs = refs[4 * n + 1:]
        for j in range(n):
            g = g_r[j][...]
            if j == lb_idx:
                lbj = lb_r[...]
                g = g * lbj * (1.0 - lbj)
            d, mn, vn = _adamw(w_r[j][...], g, m_r[j][...], v_r[j][...])
            outs[j][...], outs[n + j][...], outs[2 * n + j][...], outs[3 * n + j][...] = g, d, mn, vn

    shapes = tuple(_sds(w.shape, F32) for w in ws)
    return pl.pallas_call(body, name="adam_small", out_shape=shapes * 4)(*gs, *ws, *ms, *vs, lbv)


def kernel(x, c, ctx, c_ctx, ada_w, ada_b, norm_w, hgrn_w_in, hgrn_lb_logits, hgrn_gnorm_w, hgrn_w_out, pool_w_in, pool_w_grp, pool_scale, pool_w_out, final_norm_w, loss_target, m_c_ctx, m_ada_w, m_ada_b, m_norm_w, m_hgrn_w_in, m_hgrn_lb_logits, m_hgrn_gnorm_w, m_hgrn_w_out, m_pool_w_in, m_pool_w_grp, m_pool_scale, m_pool_w_out, m_final_norm_w, v_c_ctx, v_ada_w, v_ada_b, v_norm_w, v_hgrn_w_in, v_hgrn_lb_logits, v_hgrn_gnorm_w, v_hgrn_w_out, v_pool_w_in, v_pool_w_grp, v_pool_scale, v_pool_w_out, v_final_norm_w):
    idx = 4 * lax.axis_index("x") + 2 * lax.axis_index("y") + lax.axis_index("c")
    cctx2 = c_ctx.reshape(1, D)
    cum01, mask01 = _gla_consts()
    pb, pbt, pinv = _pool_consts()

    idx1 = idx.reshape(1).astype(jnp.int32)
    nw0, nw1 = norm_w[0:1], norm_w[1:2]
    fnw = final_norm_w.reshape(1, D)
    g_all, win, s_wout, s_pwin, s_pgrp, s_pwout, lbl_g, ps_g, cg, mod0, mod1, modc = _f1_gather_matmul(
        idx1, ctx[0], x[0], nw0, hgrn_w_in[0], hgrn_w_out[0], pool_w_in[0], pool_w_grp[0], pool_w_out[0], hgrn_lb_logits[0],
        pool_scale, c, cctx2, ada_w, ada_b)
    lb = jax.nn.sigmoid(jnp.transpose(lbl_g, (1, 0, 2)).reshape(2, E))
    pscale = ps_g.reshape(1, E)
    msel = jnp.stack([modc[:2], mod0[:2]])
    p0, p1, v_all, dec, pwin, pgrp = _gla_prep(g_all, lb, cum01, s_pwin, s_pgrp)
    o, wout, pwout = _gla_fwd(p0, p1, v_all, dec, mask01, s_wout, s_pwout)
    x1 = _f3_out(o, g_all, x[0], mod0[2:3], hgrn_gnorm_w, wout)
    dx1, gpwin, gpgrp, gpwout, dmod1, gnw1, gfw, gps, lossv = _pool_layer(
        x1, loss_target[0], mod1, nw1, fnw, pwin, pgrp, pscale, pwout, pb, pbt, pinv)
    do, dz, gwout, dgate0, ggw, rpwout = _b3_out_bwd(dx1, o, g_all, mod0[2:3], hgrn_gnorm_w, wout, gpwout)
    d0, d1, dv, dgl, rpwin = _gla_bwd(p0, p1, v_all, dec, do, mask01, gpwin)
    dg, dlb, rwout, rpgrp = _gla_post_bwd(g_all, d0, d1, dgl, dv, dz, lb, cum01, gwout, gpgrp)
    grad_x, rwin, dmx, dmc, gnw0 = _b1_in_bwd(idx1, ctx[0], x[0], dx1, dg, nw0, msel, win)

    dmod0 = jnp.concatenate([dmx, dgate0], axis=0)
    dmodc = jnp.concatenate([dmc, jnp.zeros((1, D), F32)], axis=0)
    pd = jnp.stack([dmod0, dmod1, dmodc]).reshape(3, NDEV, SH_ADA)
    pv = jnp.concatenate([gnw0, gnw1, ggw, gfw, dlb.reshape(1, 2 * E), gps, lossv], axis=1)
    g_ada, g_adab, g_cctx, pvsum, loss128 = _reduce_small(pd, pv, cg, cctx2, ada_w[0])

    out = {}
    out["hgrn_w_in"] = _adam_sharded("adam_w_in", rwin, hgrn_w_in[0], m_hgrn_w_in[0], v_hgrn_w_in[0], 256)
    out["hgrn_w_out"] = _adam_sharded("adam_w_out", rwout, hgrn_w_out[0], m_hgrn_w_out[0], v_hgrn_w_out[0], SH_ROWS)
    out["pool_w_in"] = _adam_sharded("adam_pw_in", rpwin, pool_w_in[0], m_pool_w_in[0], v_pool_w_in[0], 512)
    out["pool_w_grp"] = _adam_sharded("adam_pgrp", rpgrp.reshape(NDEV, 4 * SH_GRP, PG), pool_w_grp[0].reshape(4 * SH_GRP, PG),
                                      m_pool_w_grp[0].reshape(4 * SH_GRP, PG), v_pool_w_grp[0].reshape(4 * SH_GRP, PG), 4 * SH_GRP)
    out["pool_w_out"] = _adam_sharded("adam_pw_out", rpwout, pool_w_out[0], m_pool_w_out[0], v_pool_w_out[0], SH_ROWS)
    g_ada2 = g_ada.reshape(2 * D, SH_ADA)
    out["ada_w"] = (g_ada2,) + _adam_dense("adam_ada_w", g_ada2, ada_w.reshape(2 * D, SH_ADA), m_ada_w.reshape(2 * D, SH_ADA),
                                           v_ada_w.reshape(2 * D, SH_ADA), 512)

    lb_me = lax.dynamic_slice_in_dim(lb, idx * DH, DH, axis=1)
    small = ["c_ctx", "ada_b", "norm_w", "hgrn_lb_logits", "hgrn_gnorm_w", "pool_scale", "final_norm_w"]
    gs = [g_cctx, g_adab.reshape(2, 3 * D), pvsum[:, PV_NW:PV_NW + 2 * D].reshape(2, D),
          lax.dynamic_slice_in_dim(pvsum[:, PV_LB:PV_LB + 2 * E].reshape(2, E), idx * DH, DH, axis=1),
          pvsum[:, PV_GNORM:PV_GNORM + E], lax.dynamic_slice_in_dim(pvsum[:, PV_PSCALE:PV_PSCALE + E], idx * DH, DH, axis=1),
          pvsum[:, PV_FINAL:PV_FINAL + D]]
    ws = [cctx2, ada_b, norm_w, hgrn_lb_logits[0], hgrn_gnorm_w, pool_scale, fnw]
    ms = [m_c_ctx.reshape(1, D), m_ada_b, m_norm_w, m_hgrn_lb_logits[0], m_hgrn_gnorm_w, m_pool_scale, m_final_norm_w.reshape(1, D)]
    vs = [v_c_ctx.reshape(1, D), v_ada_b, v_norm_w, v_hgrn_lb_logits[0], v_hgrn_gnorm_w, v_pool_scale, v_final_norm_w.reshape(1, D)]
    res = _adam_small(gs, ws, ms, vs, 3, lb_me)
    n = len(small)
    for j, name in enumerate(small):
        out[name] = tuple(res[q * n + j] for q in range(4))

    shapes = {"c_ctx": (D,), "ada_w": (2, D, SH_ADA), "ada_b": (2, 3 * D), "norm_w": (2, D), "hgrn_w_in": (1, D, SH_WIN),
              "hgrn_lb_logits": (1, 2, DH), "hgrn_gnorm_w": (1, E), "hgrn_w_out": (1, SH_ROWS, D), "pool_w_in": (1, D, SH_PWIN),
              "pool_w_grp": (1, 4, SH_GRP, PG), "pool_scale": (1, DH), "pool_w_out": (1, SH_ROWS, D), "final_norm_w": (D,)}
    order = ["c_ctx", "ada_w", "ada_b", "norm_w", "hgrn_w_in", "hgrn_lb_logits", "hgrn_gnorm_w", "hgrn_w_out", "pool_w_in",
             "pool_w_grp", "pool_scale", "pool_w_out", "final_norm_w"]
    flat = [out[name][q].reshape(shapes[name]) for q in range(4) for name in order]
    return (loss128[0, 0], grad_x[None], *flat)
```

```python
import functools

import numpy as np
import jax
import jax.numpy as jnp
from jax import lax
from jax.experimental import pallas as pl
from jax.experimental.pallas import tpu as pltpu

F32 = jnp.float32
BF16 = jnp.bfloat16

D = 1024
E = 1024
HEADS = 8
DH = 128
CHUNK = 64
T = 2048
TC = 256
TT = T + TC
TM = 256
NT = TT // TM
NTX = T // TM
NDEV = 8
GRID_W = 64
POOL_WINDOWS = (2, 4, 8, 16)
PG = 256
EPS = 1e-6
WIN_COLS = 5 * E
SH_WIN = WIN_COLS // NDEV
SH_PWIN = 2 * E // NDEV
SH_ROWS = E // NDEV
SH_GRP = PG // NDEV
SH_ADA = 3 * D // NDEV
VMEM_LIMIT = 56 * 1024 * 1024
VMEM_LIMIT_SCAN = 60 * 1024 * 1024

ADAM_LR, ADAM_B1, ADAM_B2, ADAM_EPS, ADAM_WD, ADAM_STEP = 0.001, 0.9, 0.999, 1e-08, 0.01, 10

MESH = pl.DeviceIdType.MESH
VMEM_SPEC = pl.BlockSpec(memory_space=pltpu.VMEM)
HBM_SPEC = pl.BlockSpec(memory_space=pltpu.HBM)
ANY_SPEC = pl.BlockSpec(memory_space=pl.ANY)


def _sds(shape, dtype):
    return jax.ShapeDtypeStruct(shape, dtype)


def _bf(a):
    return a if a.dtype == BF16 else a.astype(BF16)


def _dot(a, b):
    return lax.dot_general(_bf(a), _bf(b), (((1,), (0,)), ((), ())), preferred_element_type=F32)


def _dot_tb(a, b):
    return lax.dot_general(_bf(a), _bf(b), (((1,), (1,)), ((), ())), preferred_element_type=F32)


def _dot_ta(a, b):
    return lax.dot_general(_bf(a), _bf(b), (((0,), (0,)), ((), ())), preferred_element_type=F32)


def _bdot(a, b):
    return lax.dot_general(_bf(a), _bf(b), (((2,), (1,)), ((0,), (0,))), preferred_element_type=F32)


def _bdot_nt(a, b):
    return lax.dot_general(_bf(a), _bf(b), (((2,), (2,)), ((0,), (0,))), preferred_element_type=F32)


def _bdot_tn(a, b):
    return lax.dot_general(_bf(a), _bf(b), (((1,), (1,)), ((0,), (0,))), preferred_element_type=F32)


def _dot01(m01, x):
    hi = x.astype(BF16)
    lo = (x - hi.astype(F32)).astype(BF16)
    return _dot(m01, hi) + _dot(m01, lo)


def _rstd(x):
    return lax.rsqrt(jnp.mean(x * x, axis=-1, keepdims=True) + EPS)


def _sigmoid(x):
    return jax.nn.sigmoid(x)


def _colsum(a):
    return jnp.sum(a, axis=0, keepdims=True)


def _stack_rows(rows):
    n = rows[0].shape[-1]
    rid = lax.broadcasted_iota(jnp.int32, (16, n), 0)
    out = jnp.zeros((16, n), F32)
    for i, r in enumerate(rows):
        out = jnp.where(rid == i, r, out)
    return out


def _head_map(fn, *arrs):
    outs = [fn(*[a[:, h * DH:(h + 1) * DH] for a in arrs]) for h in range(HEADS)]
    return jnp.concatenate(outs, axis=1)


def _gla_consts():
    r = np.arange(TM)[:, None]
    c = np.arange(TM)[None, :]
    same = (r // CHUNK) == (c // CHUNK)
    tril = same & (c <= r)
    triu = same & (c >= r)
    m = np.stack([tril, triu]).astype(np.float32)
    return jnp.asarray(m, BF16), jnp.asarray(m, F32)


def _pool_consts():
    r = np.arange(TM)[:, None]
    c = np.arange(TM)[None, :]
    same = (r // GRID_W) == (c // GRID_W)
    rp, cp = r % GRID_W, c % GRID_W
    bs, inv = [], []
    for w in POOL_WINDOWS:
        lo = np.clip(rp - w // 2, 0, GRID_W)
        hi = np.clip(rp - w // 2 + w, 0, GRID_W)
        bs.append(same & (cp >= lo) & (cp < hi))
        inv.append(1.0 / (hi - lo).astype(np.float32))
    b = np.stack(bs).astype(np.float32)
    bt = np.transpose(b, (0, 2, 1))
    return jnp.asarray(b, BF16), jnp.asarray(bt, BF16), jnp.asarray(np.stack(inv), F32)


def _mesh_pos():
    x, y, c = lax.axis_index("x"), lax.axis_index("y"), lax.axis_index("c")
    return x, y, c, 4 * x + 2 * y + c


def _peer(x, y, c, k):
    return (x ^ ((k >> 2) & 1), y ^ ((k >> 1) & 1), c ^ (k & 1))


def _small_gathers(refs, ssem, rsem):
    lb_r, ps_r, c_r, cctx_r, ada_r, adab_r, lb_o, ps_o, cg_o, mod_o, lb_out, ps_out, cg_out, mod0_o, mod1_o, modc_o = refs
    x, y, cc, idx = _mesh_pos()
    srcs = [lb_r, ps_r, c_r, mod_o.at[idx]]
    mine = [lb_o.at[idx], ps_o.at[idx], cg_o.at[idx], mod_o.at[idx]]

    def remote(a, k):
        return pltpu.make_async_remote_copy(src_ref=srcs[a], dst_ref=mine[a], send_sem=ssem.at[a, k], recv_sem=rsem.at[a, k],
                                            device_id=_peer(x, y, cc, k), device_id_type=MESH)

    first = [remote(a, k) for k in range(1, NDEV) for a in (2, 0, 1)]
    for cp in first:
        cp.start()
    lb_o[idx] = lb_r[...]
    ps_o[idx] = ps_r[...]
    cg_o[idx] = c_r[...]
    for k in range(1, NDEV):
        remote(2, k).wait_recv()
    rows = _stack_rows([cg_o[i] for i in range(NDEV)] + [cctx_r[...]])
    sc = rows * _sigmoid(rows)
    for l in range(2):
        mod_o[idx, l] = _dot(sc, ada_r[l])
    second = [remote(3, k) for k in range(1, NDEV)]
    for cp in second:
        cp.start()
    for k in range(1, NDEV):
        remote(3, k).wait_recv()

    def mod_rows(l, row):
        full = jnp.concatenate([mod_o[s, l, row, :] for s in range(NDEV)], axis=1) + adab_r[l:l + 1, :]
        return [full[:, j * D:(j + 1) * D] for j in range(3)]

    me = pl.ds(idx, 1)
    for out, parts in ((mod0_o, mod_rows(0, me)), (mod1_o, mod_rows(1, me)), (modc_o, mod_rows(0, slice(NDEV, NDEV + 1)))):
        for j in range(3):
            out[j:j + 1, :] = parts[j]
    for cp in first + second:
        cp.wait_send()
    for k in range(1, NDEV):
        for a in (0, 1):
            remote(a, k).wait_recv()
    lb_out[...] = lb_o[...]
    ps_out[...] = ps_o[...]
    cg_out[...] = cg_o[...]


def _gather_order(s):
    if isinstance(s, int):
        return (0, 1, 2, 4, 3, 5, 6, 7)[s]
    return s + (s == 3).astype(jnp.int32) - (s == 4).astype(jnp.int32)


GATHER_ISSUE = (1, 2, 4, 3, 5, 6, 7)
GATHER_ICI = (2, 4, 6)
GATHER_DIRECT = (1,) + GATHER_ICI
GLA_HB = 2
RS_SLOTS = 5


def _shard_of(kind, ref, i):
    if kind == "rows":
        return ref.at[pl.ds(pl.multiple_of(i * SH_ROWS, SH_ROWS), SH_ROWS), :]
    if kind == "major":
        return ref.at[i]
    assert kind == "grp"
    return ref.at[:, pl.ds(pl.multiple_of(i * SH_GRP, SH_GRP), SH_GRP), :]


def _gather_rider(step, n_steps, forward_at, kinds, srcs, outs, ssem, rsem, lsem):
    x, y, cc, idx = _mesh_pos()
    arrays = range(len(kinds))
    mine = [_shard_of(kinds[a], outs[a], idx) for a in arrays]

    def remote(a, k):
        return pltpu.make_async_remote_copy(src_ref=srcs[a], dst_ref=mine[a], send_sem=ssem.at[a, k], recv_sem=rsem.at[a, k],
                                            device_id=_peer(x, y, cc, k), device_id_type=MESH)

    def forward(a, k):
        blk = _shard_of(kinds[a], outs[a], idx ^ k)
        return pltpu.make_async_remote_copy(src_ref=blk, dst_ref=blk, send_sem=ssem.at[a, k ^ 1], recv_sem=rsem.at[a, k ^ 1],
                                            device_id=(x, y, 1 - cc), device_id_type=MESH)

    copies = [remote(a, k) for k in GATHER_DIRECT for a in arrays]
    passed = [forward(a, k) for k in GATHER_ICI for a in arrays]
    local = [pltpu.make_async_copy(srcs[a], mine[a], lsem.at[a]) for a in arrays]

    @pl.when(step == 0)
    def _():
        for cp in copies + local:
            cp.start()

    @pl.when(step == forward_at)
    def _():
        for k in GATHER_ICI:
            for a in arrays:
                remote(a, k).wait_recv()
                forward(a, k).start()

    @pl.when(step == n_steps - 1)
    def _():
        for cp in copies + passed:
            cp.wait_send()
        for a in arrays:
            remote(a, 1).wait_recv()
        for cp in passed:
            cp.wait_recv()
        for cp in local:
            cp.wait()


def _scatter_rider(step, n_steps, kinds, grads, slots, ssem, rsem, lsem):
    x, y, cc, idx = _mesh_pos()
    arrays = range(len(kinds))
    dsts = [slots[a].at[idx] for a in arrays]

    def remote(a, k):
        px, py, pc = _peer(x, y, cc, k)
        return pltpu.make_async_remote_copy(src_ref=_shard_of(kinds[a], grads[a], 4 * px + 2 * py + pc), dst_ref=dsts[a],
                                            send_sem=ssem.at[a, k], recv_sem=rsem.at[a, k], device_id=(px, py, pc), device_id_type=MESH)

    copies = [remote(a, k) for k in GATHER_ISSUE for a in arrays]
    local = [pltpu.make_async_copy(_shard_of(kinds[a], grads[a], idx), dsts[a], lsem.at[a]) for a in arrays]

    @pl.when(step == 0)
    def _():
        for cp in copies + local:
            cp.start()

    @pl.when(step == n_steps - 1)
    def _():
        for cp in copies:
            cp.wait_send()
        for cp in copies:
            cp.wait_recv()
        for cp in local:
            cp.wait()


def _rider_sems(n):
    return [pltpu.SemaphoreType.DMA((n, NDEV)), pltpu.SemaphoreType.DMA((n, NDEV)), pltpu.SemaphoreType.DMA((n,))]


def _modulated(x, nw, shift, scale):
    r = _rstd(x)
    xn = x * r
    a = xn * nw
    return a * (1.0 + scale) + shift, r, xn, a


def _ctx_or_x(i, ctx_ref, x_ref):
    return jnp.where(i == 0, ctx_ref[...], x_ref[...])


def _f1_gather_matmul(idx1, ctx, x, nw, w_in, w_out, pw_in, pgrp, pw_out, lb_l, pscale, c, c_ctx, ada_w, ada_b):
    def body(idx_ref, ctx_ref, x_ref, nw_ref, win_r, wout_r, pwin_r, pgrp_r, pwout_r, lb_r, ps_r, c_r, cctx_r, ada_r, adab_r,
             g_ref, win_o, s_wout, s_pwin, s_pgrp, s_pwout, lb_o, ps_o, cg_o, mod0_o, mod1_o, modc_o,
             wslot, hx_sc, lb_g, ps_g, cg_g, mod_g, ssem, rsem, osem, sm_ssem, sm_rsem):
        del idx_ref
        s, i = pl.program_id(0), pl.program_id(1)
        x, y, cc, idx = _mesh_pos()
        k = _gather_order(s)
        j = idx ^ k

        def remote(kk):
            return pltpu.make_async_remote_copy(src_ref=wslot.at[idx], dst_ref=wslot.at[idx], send_sem=ssem.at[kk], recv_sem=rsem.at[kk],
                                                device_id=_peer(x, y, cc, kk), device_id_type=MESH)

        def forward(kk):
            jj = idx ^ kk
            return pltpu.make_async_remote_copy(src_ref=wslot.at[jj], dst_ref=wslot.at[jj], send_sem=ssem.at[kk ^ 1],
                                                recv_sem=rsem.at[kk ^ 1], device_id=(x, y, 1 - cc), device_id_type=MESH)

        def to_hbm(jj, kk):
            return pltpu.make_async_copy(wslot.at[jj], win_o.at[:, pl.ds(pl.multiple_of(jj * SH_WIN, 128), SH_WIN)], osem.at[kk])

        @pl.when((s == 0) & (i == 0))
        def _():
            _small_gathers((lb_r, ps_r, c_r, cctx_r, ada_r, adab_r, lb_g, ps_g, cg_g, mod_g, lb_o, ps_o, cg_o, mod0_o, mod1_o, modc_o),
                           sm_ssem, sm_rsem)
            wslot[idx] = win_r[...].astype(BF16)
            for kk in GATHER_DIRECT:
                remote(kk).start()
            s_wout[...] = wout_r[...].astype(BF16)
            s_pwin[...] = pwin_r[...].astype(BF16)
            s_pgrp[...] = pgrp_r[...].astype(BF16)
            s_pwout[...] = pwout_r[...].astype(BF16)

        @pl.when(s == 0)
        def _():
            shift = jnp.where(i == 0, modc_o[0:1, :], mod0_o[0:1, :])
            scale = jnp.where(i == 0, modc_o[1:2, :], mod0_o[1:2, :])
            hx, _, _, _ = _modulated(_ctx_or_x(i, ctx_ref, x_ref), nw_ref[...], shift, scale)
            hx_sc[i] = hx.astype(BF16)

        @pl.when((s > 0) & (i == 0))
        def _():
            remote(k).wait_recv()

            @pl.when((k & 1) == 0)
            def _():
                forward(k).start()

        @pl.when(i == 0)
        def _():
            to_hbm(j, k).start()

        g_ref[...] = jnp.dot(hx_sc[i], wslot[j], preferred_element_type=F32)

        @pl.when((s == NDEV - 1) & (i == NT - 1))
        def _():
            for kk in GATHER_DIRECT:
                remote(kk).wait_send()
            for kk in GATHER_ICI:
                forward(kk).wait_send()
            for kk in range(NDEV):
                to_hbm(idx ^ kk, kk).wait()

    grid_spec = pltpu.PrefetchScalarGridSpec(
        num_scalar_prefetch=1, grid=(NDEV, NT),
        in_specs=[VMEM_SPEC, pl.BlockSpec((TM, D), lambda s, i, ix: (jnp.maximum(i - 1, 0), 0))] + [VMEM_SPEC] * 12,
        out_specs=[pl.BlockSpec((TM, SH_WIN), lambda s, i, ix: (i, ix[0] ^ _gather_order(s))), HBM_SPEC] + [VMEM_SPEC] * 10,
        scratch_shapes=[pltpu.VMEM((NDEV, D, SH_WIN), BF16), pltpu.VMEM((NT, TM, D), BF16),
                        pltpu.VMEM((NDEV, 2, DH), F32), pltpu.VMEM((NDEV, 1, DH), F32), pltpu.VMEM((NDEV, 1, D), F32),
                        pltpu.VMEM((NDEV, 2, 16, SH_ADA), F32),
                        pltpu.SemaphoreType.DMA((NDEV,)), pltpu.SemaphoreType.DMA((NDEV,)), pltpu.SemaphoreType.DMA((NDEV,)),
                        pltpu.SemaphoreType.DMA((4, NDEV)), pltpu.SemaphoreType.DMA((4, NDEV))])
    outs = (_sds((TT, WIN_COLS), F32), _sds((D, WIN_COLS), BF16),
            _sds((SH_ROWS, D), BF16), _sds((D, SH_PWIN), BF16), _sds((4, SH_GRP, PG), BF16), _sds((SH_ROWS, D), BF16),
            _sds((NDEV, 2, DH), F32), _sds((NDEV, 1, DH), F32), _sds((NDEV, 1, D), F32),
            _sds((3, D), F32), _sds((3, D), F32), _sds((3, D), F32))
    return pl.pallas_call(
        body, name="f1_gather_matmul", grid_spec=grid_spec, out_shape=outs,
        compiler_params=pltpu.CompilerParams(dimension_semantics=("arbitrary", "arbitrary"), vmem_limit_bytes=VMEM_LIMIT),
    )(idx1, ctx, x, nw, w_in, w_out, pw_in, pgrp, pw_out, lb_l, pscale, c, c_ctx, ada_w, ada_b)


def _gla_gates(pre, qpre, lbd, cum, rev):
    rows, n = pre.shape
    nch = rows // CHUNK
    sig = _sigmoid(pre)
    f = lbd + (1.0 - lbd) * sig
    k = 1.0 - f
    g = _dot01(cum, jnp.log(f))
    g3 = g.reshape(nch, CHUNK, n)
    last = 0 if rev else CHUNK - 1
    mid = CHUNK // 2 if rev else CHUNK // 2 - 1
    gl1, gm1 = g3[:, last:last + 1, :], g3[:, mid:mid + 1, :]

    def bc(a):
        return jnp.broadcast_to(a, g3.shape).reshape(rows, n)

    gm = bc(gm1)
    e_q, e_k = jnp.exp(g - gm), jnp.exp(gm - g)
    e_in, e_end = e_q * bc(jnp.exp(gm1)), e_k * bc(jnp.exp(gl1 - gm1))
    qsig = _sigmoid(qpre)
    qs = qpre * qsig * (DH ** -0.5)
    return dict(sig=sig, f=f, k=k, qsig=qsig, qs=qs, e_q=e_q, e_k=e_k, e_in=e_in, e_end=e_end,
                decay=[jnp.exp(g3[ci, last:last + 1, :]) for ci in range(nch)])


def _put_heads(ref, lead, arr):
    for h in range(HEADS):
        ref[lead + (h,)] = arr[:, h * DH:(h + 1) * DH]


def _get_heads(ref, lead=()):
    return jnp.concatenate([ref[lead + (h,)] for h in range(HEADS)], axis=1)


def _gla_prep(g_all, lb, cum01, s_pwin, s_pgrp):
    def body(g_ref, lb_ref, cum_ref, spwin_r, spgrp_r, p0_ref, p1_ref, v_ref, dec_ref, pwin_o, pgrp_o, ssem, rsem, lsem):
        _gather_rider(pl.program_id(0), NT, NT // 2 + 1, ("major", "grp"), (spwin_r, spgrp_r), (pwin_o, pgrp_o), ssem, rsem, lsem)
        qpre = g_ref[:, 3 * E:4 * E]
        _put_heads(v_ref, (), g_ref[:, 2 * E:3 * E].astype(BF16))
        dec_ref[...] = jnp.zeros_like(dec_ref)
        for d, p_ref in ((0, p0_ref), (1, p1_ref)):
            t = _gla_gates(g_ref[:, d * E:(d + 1) * E], qpre, lb_ref[d:d + 1, :], cum_ref[d], d == 1)
            _put_heads(p_ref, (0,), (t["qs"] * t["e_q"]).astype(BF16))
            _put_heads(p_ref, (1,), (t["k"] * t["e_k"]).astype(BF16))
            _put_heads(p_ref, (2,), (t["qs"] * t["e_in"]).astype(BF16))
            _put_heads(p_ref, (3,), (t["k"] * t["e_end"]).astype(BF16))
            for ci in range(TM // CHUNK):
                dec_ref[d, 0, ci:ci + 1, :] = t["decay"][ci]

    quad = pl.BlockSpec((4, HEADS, TM, DH), lambda i: (0, 0, i, 0))
    return pl.pallas_call(
        body, name="gla_prep", grid=(NT,),
        in_specs=[pl.BlockSpec((TM, WIN_COLS), lambda i: (i, 0)), VMEM_SPEC, VMEM_SPEC, HBM_SPEC, HBM_SPEC],
        out_specs=[quad, quad, pl.BlockSpec((HEADS, TM, DH), lambda i: (0, i, 0)), pl.BlockSpec((2, 1, 8, E), lambda i: (0, i, 0, 0)),
                   HBM_SPEC, HBM_SPEC],
        out_shape=(_sds((4, HEADS, TT, DH), BF16), _sds((4, HEADS, TT, DH), BF16), _sds((HEADS, TT, DH), BF16), _sds((2, NT, 8, E), F32),
                   _sds((NDEV, D, SH_PWIN), BF16), _sds((4, PG, PG), BF16)),
        scratch_shapes=_rider_sems(2),
        compiler_params=pltpu.CompilerParams(dimension_semantics=("arbitrary",), vmem_limit_bytes=VMEM_LIMIT),
    )(g_all, lb, cum01, s_pwin, s_pgrp)


def _scan_tile(i, rev):
    t = jnp.where(i == 0, 0, NT - i) if rev else i
    return t, pl.ds(pl.multiple_of(t * TM, TM), TM)


def _chunk_order(rev):
    n = TM // CHUNK
    return tuple(range(n - 1, -1, -1)) if rev else tuple(range(n))


def _gla_fwd(p0, p1, v_all, dec, mask01, s_wout, s_pwout):
    n_steps = HEADS // GLA_HB

    def body(p0_ref, p1_ref, v_ref, dec_ref, msk_ref, swout_r, spwout_r, o_ref, wout_o, pwout_o, ob_sc, ssem, rsem, lsem):
        _gather_rider(pl.program_id(0), n_steps, n_steps // 2, ("rows", "rows"), (swout_r, spwout_r), (wout_o, pwout_o), ssem, rsem, lsem)

        lanes = [(d, hh) for d in (0, 1) for hh in range(GLA_HB)]
        nch = TM // CHUNK

        def tile_body(i, st):
            where = [_scan_tile(i, d == 1) for d in (0, 1)]

            def stacked(fn):
                return jnp.stack([fn(d, hh, where[d][1]) for d, hh in lanes])

            qg, kg, q_in, kend = [stacked(lambda d, hh, rows, ty=ty: (p1_ref if d else p0_ref)[ty, hh, rows, :]) for ty in range(4)]
            v = stacked(lambda d, hh, rows: v_ref[hh, rows, :])
            a = _bdot_nt(qg, kg) * jnp.stack([msk_ref[d] for d, _ in lanes])
            intra = _bdot(a, v)
            outs = [[None] * nch for _ in lanes]
            for n in range(nch):
                cis = [nch - 1 - n if d else n for d, _ in lanes]

                def chunk(arr):
                    return jnp.stack([arr[l, ci * CHUNK:(ci + 1) * CHUNK] for l, ci in enumerate(cis)])

                dec = jnp.stack([dec_ref[d, where[d][0], ci:ci + 1, hh * DH:(hh + 1) * DH] for (d, hh), ci in zip(lanes, cis)])
                inter = _bdot_nt(chunk(q_in), st)
                for l, ci in enumerate(cis):
                    outs[l][ci] = inter[l] + intra[l, ci * CHUNK:(ci + 1) * CHUNK]
                st = st * dec + _bdot_tn(chunk(v), chunk(kend))
            for l, (d, hh) in enumerate(lanes):
                (ob_sc if d else o_ref)[hh, where[d][1], :] = jnp.concatenate(outs[l], axis=0)
            return st

        lax.fori_loop(0, NT, tile_body, jnp.zeros((len(lanes), DH, DH), F32))
        o_ref[...] += ob_sc[...]

    quad = pl.BlockSpec((4, GLA_HB, TT, DH), lambda h: (0, h, 0, 0))
    head = pl.BlockSpec((GLA_HB, TT, DH), lambda h: (h, 0, 0))
    return pl.pallas_call(
        body, name="gla_fwd", grid=(n_steps,),
        in_specs=[quad, quad, head, pl.BlockSpec((2, NT, 8, GLA_HB * DH), lambda h: (0, 0, 0, h)),
                  pl.BlockSpec((2, TM, TM), lambda h: (0, 0, 0)), HBM_SPEC, HBM_SPEC],
        out_specs=[head, HBM_SPEC, HBM_SPEC],
        out_shape=(_sds((HEADS, TT, DH), F32), _sds((E, D), BF16), _sds((E, D), BF16)),
        scratch_shapes=[pltpu.VMEM((GLA_HB, TT, DH), F32)] + _rider_sems(2),
        compiler_params=pltpu.CompilerParams(dimension_semantics=("arbitrary",), vmem_limit_bytes=VMEM_LIMIT),
    )(p0, p1, v_all, dec, mask01, s_wout, s_pwout)


def _gated_norm(o, z, gw):
    r = _head_map(lambda oh: jnp.broadcast_to(_rstd(oh), oh.shape), o)
    on = o * r
    zs = _sigmoid(z)
    sz = z * zs
    return on * gw * sz, r, on, zs, sz


def _f3_out(o, g_all, x, gate, gw, wout):
    def body(o_ref, z_ref, x_ref, gate_ref, gw_ref, w_ref, x1_ref):
        og, _, _, _, _ = _gated_norm(_get_heads(o_ref), z_ref[...], gw_ref[...])
        x1_ref[...] = x_ref[...] + gate_ref[...] * _dot(og, w_ref[...])

    return pl.pallas_call(
        body, name="f3_out", grid=(NTX,),
        in_specs=[pl.BlockSpec((HEADS, TM, DH), lambda i: (0, i + 1, 0)), pl.BlockSpec((TM, E), lambda i: (i + 1, 4)),
                  pl.BlockSpec((TM, D), lambda i: (i, 0)), pl.BlockSpec((1, D), lambda i: (0, 0)),
                  pl.BlockSpec((1, E), lambda i: (0, 0)), pl.BlockSpec((E, D), lambda i: (0, 0))],
        out_specs=pl.BlockSpec((TM, D), lambda i: (i, 0)),
        out_shape=_sds((T, D), F32),
        compiler_params=pltpu.CompilerParams(dimension_semantics=("arbitrary",)),
    )(o, g_all, x, gate, gw, wout)


def _pool_layer(x1, tgt, mod1, nw1, fnw, pwin, pgrp, pscale, pwout, pb, pbt, pinv):
    def body(x_ref, t_ref, m_ref, nw_ref, fw_ref, pwin_ref, pgrp_ref, ps_ref, pwout_ref, pb_ref, pbt_ref, pinv_ref,
             dx_ref, gpwin_o, gpgrp_o, gpwout_o, dmod_o, gnw_o, gfw_o, gps_o, loss_o,
             a_pwin, a_pgrp, a_pwout):
        i = pl.program_id(0)

        @pl.when(i == 0)
        def _():
            for ref in (a_pwin, a_pgrp, a_pwout, dmod_o, gnw_o, gfw_o, gps_o, loss_o):
                ref[...] = jnp.zeros_like(ref)

        shift, scale, gate = m_ref[0:1, :], m_ref[1:2, :], m_ref[2:3, :]
        nw, fw, ps = nw_ref[...], fw_ref[...], ps_ref[...]
        x1 = x_ref[...]
        hx, r1, xn, a = _modulated(x1, nw, shift, scale)
        hxb = hx.astype(BF16)
        uz = jnp.concatenate([_dot(hxb, pwin_ref[j]) for j in range(NDEV)], axis=1)
        u, z = uz[:, :E], uz[:, E:]
        pooled, ys = [], []
        for g in range(4):
            ug = u[:, g * PG:(g + 1) * PG]
            pg = _dot01(pb_ref[g], ug) * pinv_ref[g] - ug
            pooled.append(pg.astype(BF16))
            ys.append(_dot(pooled[g], pgrp_ref[g]))
        ycat = jnp.concatenate(ys, axis=1)
        y = ycat * ps
        zs = _sigmoid(z)
        sz = z * zs
        p = (y * sz).astype(BF16)
        out = _dot(p, pwout_ref[...])
        x2 = x1 + gate * out
        r2 = _rstd(x2)
        xn2 = x2 * r2
        diff = xn2 * fw - t_ref[...]
        loss_o[...] += _colsum(diff * diff)
        dyf = diff * (1.0 / D)
        gfw_o[...] += _colsum(dyf * xn2)
        dxn2 = dyf * fw
        dx2 = r2 * (dxn2 - xn2 * jnp.mean(dxn2 * xn2, axis=-1, keepdims=True))
        dgate = _colsum(dx2 * out)
        dout = (dx2 * gate).astype(BF16)
        for j in range(4):
            cs = slice(j * PG, (j + 1) * PG)
            a_pwout[:, cs] += _dot_ta(p, dout[:, cs])
        dp = _dot_tb(dout, pwout_ref[...])
        dy = dp * sz
        dz = dp * y * (zs * (1.0 + z * (1.0 - zs)))
        gps_o[...] += _colsum(dy * ycat)
        dycat = dy * ps
        dus = []
        for g in range(4):
            dyg = dycat[:, g * PG:(g + 1) * PG].astype(BF16)
            a_pgrp[g] += _dot_ta(pooled[g], dyg)
            dpg = _dot_tb(dyg, pgrp_ref[g])
            dus.append(_dot01(pbt_ref[g], dpg * pinv_ref[g]) - dpg)
        duz = jnp.concatenate(dus + [dz], axis=1).astype(BF16)
        dhx = None
        for j in range(NDEV):
            dj = duz[:, j * SH_PWIN:(j + 1) * SH_PWIN]
            a_pwin[j] += _dot_ta(hxb, dj)
            part = _dot_tb(dj, pwin_ref[j])
            dhx = part if dhx is None else dhx + part
        dmod_o[0:1, :] += _colsum(dhx)
        dmod_o[1:2, :] += _colsum(dhx * a)
        dmod_o[2:3, :] += dgate
        da = dhx * (1.0 + scale)
        gnw_o[...] += _colsum(da * xn)
        dxn = da * nw
        dx_ref[...] = dx2 + r1 * (dxn - xn * jnp.mean(dxn * xn, axis=-1, keepdims=True))

        @pl.when(i == NTX - 1)
        def _():
            gpwin_o[...] = a_pwin[...].astype(BF16)
            gpgrp_o[...] = a_pgrp[...].astype(BF16)
            gpwout_o[...] = a_pwout[...].astype(BF16)

    tile = pl.BlockSpec((TM, D), lambda i: (i, 0))
    outs = (_sds((T, D), F32), _sds((NDEV, D, SH_PWIN), BF16), _sds((4, PG, PG), BF16), _sds((E, D), BF16),
            _sds((3, D), F32), _sds((1, D), F32), _sds((1, D), F32), _sds((1, E), F32), _sds((1, D), F32))
    return pl.pallas_call(
        body, name="pool_layer", grid=(NTX,),
        in_specs=[tile, tile] + [VMEM_SPEC] * 10,
        out_specs=[tile] + [VMEM_SPEC] * 8,
        out_shape=outs,
        scratch_shapes=[pltpu.VMEM((NDEV, D, SH_PWIN), F32), pltpu.VMEM((4, PG, PG), F32), pltpu.VMEM((E, D), F32)],
        compiler_params=pltpu.CompilerParams(dimension_semantics=("arbitrary",), vmem_limit_bytes=VMEM_LIMIT),
    )(x1, tgt, mod1, nw1, fnw, pwin, pgrp, pscale, pwout, pb, pbt, pinv)


def _b3_out_bwd(dx1, o, g_all, gate, gw, wout, gpwout):
    def body(dx_ref, o_ref, z_ref, gate_ref, gw_ref, w_ref, gpwout_r, do_ref, dz_ref, gw_o, dgate_o, ggw_o, rpwout_o,
             acc, ssem, rsem, lsem):
        i = pl.program_id(0)
        _scatter_rider(i, NT, ("rows",), (gpwout_r,), (rpwout_o,), ssem, rsem, lsem)

        @pl.when(i == 0)
        def _():
            acc[...] = jnp.zeros_like(acc)
            dgate_o[...] = jnp.zeros_like(dgate_o)
            ggw_o[...] = jnp.zeros_like(ggw_o)
            do_ref[...] = jnp.zeros_like(do_ref)
            dz_ref[...] = jnp.zeros_like(dz_ref)

        @pl.when(i > 0)
        def _():
            gw = gw_ref[...]
            z = z_ref[...]
            og, r, on, zs, sz = _gated_norm(_get_heads(o_ref), z, gw)
            ogb = og.astype(BF16)
            dx = dx_ref[...]
            dgate_o[...] += _colsum(dx * _dot(ogb, w_ref[...]))
            dy = (dx * gate_ref[...]).astype(BF16)
            for j in range(4):
                cs = slice(j * PG, (j + 1) * PG)
                acc[:, cs] += _dot_ta(ogb, dy[:, cs])
            dog = _dot_tb(dy, w_ref[...])
            dz_ref[...] = (dog * (on * gw) * (zs * (1.0 + z * (1.0 - zs)))).astype(BF16)
            dong = dog * sz
            ggw_o[...] += _colsum(dong * on)
            don = dong * gw
            do = _head_map(lambda dh, nh, rh: rh * (dh - nh * jnp.mean(dh * nh, axis=-1, keepdims=True)), don, on, r)
            _put_heads(do_ref, (), do.astype(BF16))

        @pl.when(i == NT - 1)
        def _():
            gw_o[...] = acc[...].astype(BF16)

    prev = lambda i: (jnp.maximum(i - 1, 0), 0)
    heads = pl.BlockSpec((HEADS, TM, DH), lambda i: (0, i, 0))
    return pl.pallas_call(
        body, name="b3_out_bwd", grid=(NT,),
        in_specs=[pl.BlockSpec((TM, D), prev), heads, pl.BlockSpec((TM, E), lambda i: (i, 4)),
                  VMEM_SPEC, VMEM_SPEC, VMEM_SPEC, HBM_SPEC],
        out_specs=[heads, pl.BlockSpec((TM, E), lambda i: (i, 0)), VMEM_SPEC, VMEM_SPEC, VMEM_SPEC, HBM_SPEC],
        out_shape=(_sds((HEADS, TT, DH), BF16), _sds((TT, E), BF16), _sds((E, D), BF16), _sds((1, D), F32), _sds((1, E), F32),
                   _sds((NDEV, SH_ROWS, D), BF16)),
        scratch_shapes=[pltpu.VMEM((E, D), F32)] + _rider_sems(1),
        compiler_params=pltpu.CompilerParams(dimension_semantics=("arbitrary",), vmem_limit_bytes=VMEM_LIMIT),
    )(dx1, o, g_all, gate, gw, wout, gpwout)


def _gla_bwd(p0, p1, v_all, dec, do, mask01, gpwin):
    nch = TM // CHUNK
    n_steps = HEADS // GLA_HB

    def body(p0_ref, p1_ref, v_ref, dec_ref, do_ref, msk_ref, gpwin_r, d0_ref, d1_ref, dv_ref, dgl_ref, rpwin_o,
             ss_sc, dv_sc, ssem, rsem, lsem):
        _scatter_rider(pl.program_id(0), n_steps, ("major",), (gpwin_r,), (rpwin_o,), ssem, rsem, lsem)

        lanes = [(d, hh) for d in (0, 1) for hh in range(GLA_HB)]
        zero = jnp.zeros((len(lanes), DH, DH), F32)
        dgl_ref[...] = jnp.zeros_like(dgl_ref)

        def p_of(d):
            return p1_ref if d else p0_ref

        def scan_step(i, n):
            where = [_scan_tile(i, d == 1) for d in (0, 1)]
            cis = [nch - 1 - n if d else n for d, _ in lanes]
            dec = jnp.stack([dec_ref[d, where[d][0], ci:ci + 1, hh * DH:(hh + 1) * DH] for (d, hh), ci in zip(lanes, cis)])

            def chunk(arr):
                return jnp.stack([arr[l, ci * CHUNK:(ci + 1) * CHUNK] for l, ci in enumerate(cis)])

            return where, cis, dec, chunk

        def stacked(i, fn):
            where = [_scan_tile(i, d == 1) for d in (0, 1)]
            return jnp.stack([fn(d, hh, where[d][1]) for d, hh in lanes])

        def fwd_body(i, st):
            v = stacked(i, lambda d, hh, rows: v_ref[hh, rows, :])
            kend = stacked(i, lambda d, hh, rows: p_of(d)[3, hh, rows, :])
            for n in range(nch):
                _, _, dec, chunk = scan_step(i, n)
                ss_sc[i * nch + n] = st
                st = st * dec + _bdot_tn(chunk(v), chunk(kend))
            return st

        lax.fori_loop(0, NT, fwd_body, zero)

        def bwd_body(ii, dst):
            i = NT - 1 - ii
            qg, kg, q_in, kend = [stacked(i, lambda d, hh, rows, ty=ty: p_of(d)[ty, hh, rows, :]) for ty in range(4)]
            v = stacked(i, lambda d, hh, rows: v_ref[hh, rows, :])
            dob = stacked(i, lambda d, hh, rows: do_ref[hh, rows, :])
            msk = jnp.stack([msk_ref[d] for d, _ in lanes])
            a = (_bdot_nt(qg, kg) * msk).astype(BF16)
            da = (_bdot_nt(dob, v) * msk).astype(BF16)
            dqg = _bdot(da, kg)
            dkg = _bdot_tn(da, qg)
            dv_intra = _bdot_tn(a, dob)
            dv_l, dkend_l, dqin_l = ([[None] * nch for _ in lanes] for _ in range(3))
            for n in range(nch - 1, -1, -1):
                where, cis, dec, chunk = scan_step(i, n)
                s_c = ss_sc[i * nch + n]
                dstb = dst.astype(BF16)
                kend_c, v_c, dob_c = chunk(kend), chunk(v), chunk(dob)
                dv_c = chunk(dv_intra) + _bdot_nt(kend_c, dstb)
                dkend_c = _bdot(v_c, dstb)
                dqin_c = _bdot(dob_c, s_c)
                dgl = jnp.sum(s_c * dst, axis=1, keepdims=True) * dec
                for l, ((d, hh), ci) in enumerate(zip(lanes, cis)):
                    dv_l[l][ci], dkend_l[l][ci], dqin_l[l][ci] = dv_c[l], dkend_c[l], dqin_c[l]
                    dgl_ref[d, where[d][0], ci:ci + 1, hh * DH:(hh + 1) * DH] = dgl[l]
                dst = dst * dec + _bdot_tn(dob_c, chunk(q_in))
            where = [_scan_tile(i, d == 1) for d in (0, 1)]
            for l, (d, hh) in enumerate(lanes):
                rows = where[d][1]
                d_ref = d1_ref if d else d0_ref
                d_ref[0, hh, rows, :] = dqg[l].astype(BF16)
                d_ref[1, hh, rows, :] = dkg[l].astype(BF16)
                d_ref[2, hh, rows, :] = jnp.concatenate(dqin_l[l], axis=0).astype(BF16)
                d_ref[3, hh, rows, :] = jnp.concatenate(dkend_l[l], axis=0).astype(BF16)
                dv_sc[d, hh, rows, :] = jnp.concatenate(dv_l[l], axis=0).astype(BF16)
            return dst

        lax.fori_loop(0, NT, bwd_body, zero)
        dv_ref[...] = (dv_sc[0].astype(F32) + dv_sc[1].astype(F32)).astype(BF16)

    quad = pl.BlockSpec((4, GLA_HB, TT, DH), lambda h: (0, h, 0, 0))
    col = pl.BlockSpec((GLA_HB, TT, DH), lambda h: (h, 0, 0))
    chunkv = pl.BlockSpec((2, NT, 8, GLA_HB * DH), lambda h: (0, 0, 0, h))
    outs = (_sds((4, HEADS, TT, DH), BF16), _sds((4, HEADS, TT, DH), BF16), _sds((HEADS, TT, DH), BF16), _sds((2, NT, 8, E), F32),
            _sds((NDEV, D, SH_PWIN), BF16))
    return pl.pallas_call(
        body, name="gla_bwd", grid=(n_steps,),
        in_specs=[quad, quad, col, chunkv, col, pl.BlockSpec((2, TM, TM), lambda h: (0, 0, 0)), HBM_SPEC],
        out_specs=[quad, quad, col, chunkv, HBM_SPEC],
        out_shape=outs,
        scratch_shapes=[pltpu.VMEM((NT * nch, 2 * GLA_HB, DH, DH), F32), pltpu.VMEM((2, GLA_HB, TT, DH), BF16)] + _rider_sems(1),
        compiler_params=pltpu.CompilerParams(dimension_semantics=("arbitrary",), vmem_limit_bytes=VMEM_LIMIT_SCAN),
    )(p0, p1, v_all, dec, do, mask01, gpwin)


TMB = 128


def _gla_post_bwd(g_all, d0, d1, dgl, dv, dz, lb, cum01, gwout, gpgrp):
    nch = TMB // CHUNK

    def body(g_ref, d0_ref, d1_ref, dgl_ref, dv_ref, dz_ref, lb_ref, cum_ref, gwout_r, gpgrp_r, dg_ref, dlb_ref, rwout_o, rpgrp_o,
             ssem, rsem, lsem):
        i = pl.program_id(0)
        _scatter_rider(i, TT // TMB, ("rows", "grp"), (gwout_r, gpgrp_r), (rwout_o, rpgrp_o), ssem, rsem, lsem)

        @pl.when(i == 0)
        def _():
            dlb_ref[...] = jnp.zeros_like(dlb_ref)

        half = i & 1
        qpre = g_ref[:, 3 * E:4 * E]
        dqs_sum = None
        dpre = []
        for d, d_ref in ((0, d0_ref), (1, d1_ref)):
            rev = d == 1
            lbd = lb_ref[d:d + 1, :]
            t = _gla_gates(g_ref[:, d * E:(d + 1) * E], qpre, lbd, cum_ref[d, :TMB, :TMB], rev)
            dqg, dkg, dqin, dkend = [_get_heads(d_ref, (ty,)).astype(F32) for ty in range(4)]
            dqs = dqg * t["e_q"] + dqin * t["e_in"]
            dk = dkg * t["e_k"] + dkend * t["e_end"]
            dkk = dkend * (t["k"] * t["e_end"])
            dg = t["qs"] * dqs - t["k"] * dk
            dkk3 = dkk.reshape(nch, CHUNK, E)
            dgl8 = dgl_ref[d, 0]
            dgl_rows = [jnp.where(half == 0, dgl8[ci:ci + 1, :], dgl8[nch + ci:nch + ci + 1, :]) for ci in range(nch)]
            dgl_b = jnp.concatenate([jnp.broadcast_to(dgl_rows[ci] + jnp.sum(dkk3[ci], axis=0, keepdims=True), (CHUNK, E))
                                     for ci in range(nch)], axis=0)
            pos = lax.broadcasted_iota(jnp.int32, (TMB, E), 0) & (CHUNK - 1)
            dg = dg + jnp.where(pos == (0 if rev else CHUNK - 1), dgl_b, 0.0)
            dlf = _dot01(cum_ref[1 - d, :TMB, :TMB], dg)
            df = dlf / t["f"] - dk
            sig = t["sig"]
            dpre.append((df * (1.0 - lbd) * sig * (1.0 - sig)).astype(BF16))
            dlb_ref[d:d + 1, :] += _colsum(df * (1.0 - sig))
            dqs_sum = dqs if dqs_sum is None else dqs_sum + dqs
            qsig = t["qsig"]
        dqpre = dqs_sum * (DH ** -0.5) * (qsig * (1.0 + qpre * (1.0 - qsig)))
        dg_ref[...] = jnp.concatenate([dpre[0], dpre[1], _get_heads(dv_ref), dqpre.astype(BF16), dz_ref[...]], axis=1)

    quad = pl.BlockSpec((4, HEADS, TMB, DH), lambda i: (0, 0, i, 0))
    tile = pl.BlockSpec((TMB, E), lambda i: (i, 0))
    return pl.pallas_call(
        body, name="gla_post_bwd", grid=(TT // TMB,),
        in_specs=[pl.BlockSpec((TMB, WIN_COLS), lambda i: (i, 0)), quad, quad,
                  pl.BlockSpec((2, 1, 8, E), lambda i: (0, i // 2, 0, 0)), pl.BlockSpec((HEADS, TMB, DH), lambda i: (0, i, 0)), tile,
                  VMEM_SPEC, VMEM_SPEC, HBM_SPEC, HBM_SPEC],
        out_specs=[pl.BlockSpec((TMB, WIN_COLS), lambda i: (i, 0)), VMEM_SPEC, HBM_SPEC, HBM_SPEC],
        out_shape=(_sds((TT, WIN_COLS), BF16), _sds((2, E), F32), _sds((NDEV, SH_ROWS, D), BF16), _sds((NDEV, 4, SH_GRP, PG), BF16)),
        scratch_shapes=_rider_sems(2),
        compiler_params=pltpu.CompilerParams(dimension_semantics=("arbitrary",), vmem_limit_bytes=VMEM_LIMIT),
    )(g_all, d0, d1, dgl, dv, dz, lb, cum01, gwout, gpgrp)


def _b1_in_bwd(idx1, ctx, x, dx1, dg, nw, msel, win):
    last_s = NDEV - 1

    def body(idx_ref, ctx_ref, x_ref, dx1_ref, dg_ref, nw_ref, m_ref, w_ref, gx_ref, rwin_o, dmx_o, dmc_o, gnw_o,
             hx_sc, dhx_sc, acc, sbuf, pbuf, psend, precv, isend, irecv, sibsem, lsem):
        del idx_ref
        s, i = pl.program_id(0), pl.program_id(1)
        x, y, cc, idx = _mesh_pos()
        shift, scale = m_ref[0, 0:1, :], m_ref[0, 1:2, :]
        sibling = (x, y, 1 - cc)

        def partial(p):
            return pltpu.make_async_remote_copy(src_ref=sbuf.at[0], dst_ref=pbuf.at[p], send_sem=psend.at[p], recv_sem=precv.at[p],
                                                device_id=sibling, device_id_type=MESH)

        def chip_sum(p):
            return pltpu.make_async_remote_copy(src_ref=sbuf.at[1], dst_ref=rwin_o.at[2 + p], send_sem=isend.at[p], recv_sem=irecv.at[p],
                                                device_id=_peer(x, y, cc, 2 * (p + 1)), device_id_type=MESH)

        to_sibling = pltpu.make_async_remote_copy(src_ref=sbuf.at[0], dst_ref=rwin_o.at[1], send_sem=sibsem.at[0], recv_sem=sibsem.at[1],
                                                  device_id=sibling, device_id_type=MESH)
        own = pltpu.make_async_copy(sbuf.at[1], rwin_o.at[0], lsem)

        @pl.when((s == 0) & (i == 0))
        def _():
            for ref in (dmx_o, dmc_o, gnw_o):
                ref[...] = jnp.zeros_like(ref)

        @pl.when(s == 0)
        def _():
            hx, _, _, _ = _modulated(_ctx_or_x(i, ctx_ref, x_ref), nw_ref[...], shift, scale)
            hx_sc[i] = hx.astype(BF16)

        @pl.when(i == 0)
        def _():
            acc[...] = jnp.zeros_like(acc)

        dgb = dg_ref[...]
        hxb = hx_sc[i]
        for lo, hi in ((0, 256), (256, 512), (512, SH_WIN)):
            acc[:, lo:hi] += _dot_ta(hxb, dgb[:, lo:hi])
        part = _dot_tb(dgb, w_ref[...])

        @pl.when(s == 0)
        def _():
            dhx_sc[i] = part

        @pl.when(s > 0)
        def _():
            dhx_sc[i] += part

        for p in (2, 1, 0):
            @pl.when((i == NT - 1) & (s == 2 * (2 - p)))
            def _(p=p):
                if p < 2:
                    partial(p + 1).wait_send()
                sbuf[0] = acc[...].astype(BF16)
                partial(p).start()

            @pl.when((i == NT - 1) & (s == 2 * (2 - p) + 1))
            def _(p=p):
                if p < 2:
                    chip_sum(p + 1).wait_send()
                partial(p).wait_recv()
                sbuf[1] = (acc[...] + pbuf[p].astype(F32)).astype(BF16)
                chip_sum(p).start()

        @pl.when((i == NT - 1) & (s == last_s - 1))
        def _():
            partial(0).wait_send()
            sbuf[0] = acc[...].astype(BF16)
            to_sibling.start()

        @pl.when((i == NT - 1) & (s == last_s))
        def _():
            chip_sum(0).wait_send()
            sbuf[1] = acc[...].astype(BF16)
            own.start()

        @pl.when(s == last_s)
        def _():
            nw = nw_ref[...]
            _, r, xn, a = _modulated(_ctx_or_x(i, ctx_ref, x_ref), nw, shift, scale)
            dhx = dhx_sc[i]
            dsh, dsc = _colsum(dhx), _colsum(dhx * a)
            da = dhx * (1.0 + scale)
            gnw_o[...] += _colsum(da * xn)
            dxn = da * nw
            gx_ref[...] = dx1_ref[...] + r * (dxn - xn * jnp.mean(dxn * xn, axis=-1, keepdims=True))

            @pl.when(i == 0)
            def _():
                dmc_o[0:1, :] += dsh
                dmc_o[1:2, :] += dsc

            @pl.when(i > 0)
            def _():
                dmx_o[0:1, :] += dsh
                dmx_o[1:2, :] += dsc

        @pl.when((i == NT - 1) & (s == last_s))
        def _():
            to_sibling.wait_send()
            to_sibling.wait_recv()
            for p in range(3):
                chip_sum(p).wait_recv()
            own.wait()

    grid_spec = pltpu.PrefetchScalarGridSpec(
        num_scalar_prefetch=1, grid=(NDEV, NT),
        in_specs=[VMEM_SPEC, pl.BlockSpec((TM, D), lambda s, i, ix: (jnp.maximum(i - 1, 0), 0)),
                  pl.BlockSpec((TM, D), lambda s, i, ix: (jnp.maximum(i - 1, 0), 0)),
                  pl.BlockSpec((TM, SH_WIN), lambda s, i, ix: (i, ix[0] ^ (last_s - s))), VMEM_SPEC,
                  pl.BlockSpec((1, 2, D), lambda s, i, ix: (jnp.minimum(i, 1), 0, 0)),
                  pl.BlockSpec((D, SH_WIN), lambda s, i, ix: (0, ix[0] ^ (last_s - s)))],
        out_specs=[pl.BlockSpec((TM, D), lambda s, i, ix: (jnp.where(s == last_s, jnp.maximum(i - 1, 0), 0), 0)),
                   HBM_SPEC, VMEM_SPEC, VMEM_SPEC, VMEM_SPEC],
        scratch_shapes=[pltpu.VMEM((NT, TM, D), BF16), pltpu.VMEM((NT, TM, D), F32), pltpu.VMEM((D, SH_WIN), F32),
                        pltpu.VMEM((2, D, SH_WIN), BF16), pltpu.VMEM((3, D, SH_WIN), BF16),
                        pltpu.SemaphoreType.DMA((3,)), pltpu.SemaphoreType.DMA((3,)), pltpu.SemaphoreType.DMA((3,)),
                        pltpu.SemaphoreType.DMA((3,)), pltpu.SemaphoreType.DMA((2,)), pltpu.SemaphoreType.DMA])
    return pl.pallas_call(
        body, name="b1_in_bwd", grid_spec=grid_spec,
        out_shape=(_sds((T, D), F32), _sds((RS_SLOTS, D, SH_WIN), BF16), _sds((2, D), F32), _sds((2, D), F32), _sds((1, D), F32)),
        compiler_params=pltpu.CompilerParams(dimension_semantics=("arbitrary", "arbitrary"), vmem_limit_bytes=VMEM_LIMIT),
    )(idx1, ctx, x, dx1, dg, nw, msel, win)


def _reduce_small(pd, pv, cg, c_ctx, ada_w0):
    n_arr = 3

    def body(pd_r, pv_r, cg_r, cctx_r, ada_r, gada_o, gadab_o, gcctx_o, pvsum_o, loss_o,
             pd_all, pv_all, dsc_all, dsc_mine, ssem, rsem):
        x, y, cc, idx = _mesh_pos()
        srcs = [pd_r, pv_r, dsc_mine]
        dsts = [pd_all.at[idx], pv_all.at[idx], dsc_all.at[idx]]

        def remote(a, k):
            return pltpu.make_async_remote_copy(src_ref=srcs[a], dst_ref=dsts[a], send_sem=ssem.at[a, k], recv_sem=rsem.at[a, k],
                                                device_id=_peer(x, y, cc, k), device_id_type=MESH)

        first = [remote(a, k) for k in range(1, NDEV) for a in (0, 1)]
        for cp in first:
            cp.start()
        pd_all[idx] = pd_r[...]
        pv_all[idx] = pv_r[...]
        for k in range(1, NDEV):
            remote(0, k).wait_recv()
            remote(1, k).wait_recv()
        mine = [pd_all[s, :, pl.ds(idx, 1), :] for s in range(NDEV)]
        dmc = functools.reduce(lambda u, v: u + v, [m[2] for m in mine])
        rows = _stack_rows([cg_r[i] for i in range(NDEV)] + [cctx_r[...]])
        sc = (rows * _sigmoid(rows)).astype(BF16)
        gada_o[0] = _dot_ta(sc, _stack_rows([m[0] for m in mine] + [dmc]))
        gada_o[1] = _dot_ta(sc, _stack_rows([m[1] for m in mine]))
        dsc_mine[...] = _dot_tb(jnp.broadcast_to(dmc, (8, SH_ADA)), ada_r[...])[0:1, :]
        dsc_all[idx] = dsc_mine[...]
        second = [remote(2, k) for k in range(1, NDEV)]
        for cp in second:
            cp.start()
        tot = [functools.reduce(lambda u, v: u + v, [pd_all[s, l] for s in range(NDEV)]) for l in range(3)]
        gadab_o[0] = tot[0] + tot[2]
        gadab_o[1] = tot[1]
        pvs = functools.reduce(lambda u, v: u + v, [pv_all[s] for s in range(NDEV)])
        pvsum_o[...] = pvs
        loss_o[...] = jnp.broadcast_to(jnp.sum(pvs[:, PV_LOSS:PV_LOSS + D], axis=-1, keepdims=True) * (0.5 / D), (1, 128))
        for k in range(1, NDEV):
            remote(2, k).wait_recv()
        dsc = functools.reduce(lambda u, v: u + v, [dsc_all[s] for s in range(NDEV)])
        cx = cctx_r[...]
        sx = _sigmoid(cx)
        gcctx_o[...] = dsc * (sx * (1.0 + cx * (1.0 - sx)))
        for cp in first + second:
            cp.wait_send()

    outs = (_sds((2, D, SH_ADA), F32), _sds((2, NDEV, SH_ADA), F32), _sds((1, D), F32), _sds((1, PV_LEN), F32), _sds((1, 128), F32))
    return pl.pallas_call(
        body, name="reduce_small", out_shape=outs,
        in_specs=[VMEM_SPEC] * 5, out_specs=[VMEM_SPEC] * 5,
        scratch_shapes=[
            pltpu.VMEM((NDEV, 3, NDEV, SH_ADA), F32), pltpu.VMEM((NDEV, 1, PV_LEN), F32), pltpu.VMEM((NDEV, 1, D), F32),
            pltpu.VMEM((1, D), F32),
            pltpu.SemaphoreType.DMA((n_arr, NDEV)), pltpu.SemaphoreType.DMA((n_arr, NDEV)),
        ],
        compiler_params=pltpu.CompilerParams(vmem_limit_bytes=VMEM_LIMIT),
    )(pd, pv, cg, c_ctx, ada_w0)


PV_NW, PV_GNORM, PV_FINAL, PV_LB, PV_PSCALE, PV_LOSS, PV_LEN = 0, 2 * D, 3 * D, 4 * D, 6 * D, 7 * D, 8 * D


def _adamw(w, g, m, v):
    m = ADAM_B1 * m + (1.0 - ADAM_B1) * g
    v = ADAM_B2 * v + (1.0 - ADAM_B2) * (g * g)
    m_hat = m / (1.0 - ADAM_B1 ** ADAM_STEP)
    v_hat = v / (1.0 - ADAM_B2 ** ADAM_STEP)
    delta = -ADAM_LR * (m_hat / (jnp.sqrt(v_hat) + ADAM_EPS) + ADAM_WD * w)
    return delta, m, v


def _adam_sharded(name, parts, w, m, v, tr):
    rr, cc = w.shape
    n = parts.shape[0]

    def body(p_ref, w_ref, m_ref, v_ref, g_o, d_o, m_o, v_o):
        g = p_ref[0].astype(F32)
        for s in range(1, n):
            g = g + p_ref[s].astype(F32)
        d, mn, vn = _adamw(w_ref[...], g, m_ref[...], v_ref[...])
        g_o[...], d_o[...], m_o[...], v_o[...] = g, d, mn, vn

    blk = pl.BlockSpec((tr, cc), lambda i: (i, 0))
    return pl.pallas_call(
        body, name=name, grid=(rr // tr,),
        in_specs=[pl.BlockSpec((n, tr, cc), lambda i: (0, i, 0)), blk, blk, blk],
        out_specs=[blk] * 4, out_shape=(_sds((rr, cc), F32),) * 4,
        compiler_params=pltpu.CompilerParams(dimension_semantics=("arbitrary",)),
    )(parts, w, m, v)


def _adam_dense(name, g, w, m, v, tr):
    rr, cc = w.shape

    def body(g_ref, w_ref, m_ref, v_ref, d_o, m_o, v_o):
        d, mn, vn = _adamw(w_ref[...], g_ref[...], m_ref[...], v_ref[...])
        d_o[...], m_o[...], v_o[...] = d, mn, vn

    blk = pl.BlockSpec((tr, cc), lambda i: (i, 0))
    return pl.pallas_call(
        body, name=name, grid=(rr // tr,), in_specs=[blk] * 4, out_specs=[blk] * 3, out_shape=(_sds((rr, cc), F32),) * 3,
        compiler_params=pltpu.CompilerParams(dimension_semantics=("arbitrary",)),
    )(g, w, m, v)


def _adam_small(gs, ws, ms, vs, lb_idx, lbv):
    n = len(ws)

    def body(*refs):
        g_r, w_r, m_r, v_r = refs[:n], refs[n:2 * n], refs[2 * n:3 * n], refs[3 * n:4 * n]
        lb_r = refs[4 * n]
        outs = refs[4 * n + 1:]
        for j in range(n):
            g = g_r[j][...]
            if j == lb_idx:
                lbj = lb_r[...]
                g = g * lbj * (1.0 - lbj)
            d, mn, vn = _adamw(w_r[j][...], g, m_r[j][...], v_r[j][...])
            outs[j][...], outs[n + j][...], outs[2 * n + j][...], outs[3 * n + j][...] = g, d, mn, vn

    shapes = tuple(_sds(w.shape, F32) for w in ws)
    return pl.pallas_call(body, name="adam_small", out_shape=shapes * 4)(*gs, *ws, *ms, *vs, lbv)


def kernel(x, c, ctx, c_ctx, ada_w, ada_b, norm_w, hgrn_w_in, hgrn_lb_logits, hgrn_gnorm_w, hgrn_w_out, pool_w_in, pool_w_grp, pool_scale, pool_w_out, final_norm_w, loss_target, m_c_ctx, m_ada_w, m_ada_b, m_norm_w, m_hgrn_w_in, m_hgrn_lb_logits, m_hgrn_gnorm_w, m_hgrn_w_out, m_pool_w_in, m_pool_w_grp, m_pool_scale, m_pool_w_out, m_final_norm_w, v_c_ctx, v_ada_w, v_ada_b, v_norm_w, v_hgrn_w_in, v_hgrn_lb_logits, v_hgrn_gnorm_w, v_hgrn_w_out, v_pool_w_in, v_pool_w_grp, v_pool_scale, v_pool_w_out, v_final_norm_w):
    idx = 4 * lax.axis_index("x") + 2 * lax.axis_index("y") + lax.axis_index("c")
    cctx2 = c_ctx.reshape(1, D)
    cum01, mask01 = _gla_consts()
    pb, pbt, pinv = _pool_consts()

    idx1 = idx.reshape(1).astype(jnp.int32)
    nw0, nw1 = norm_w[0:1], norm_w[1:2]
    fnw = final_norm_w.reshape(1, D)
    g_all, win, s_wout, s_pwin, s_pgrp, s_pwout, lbl_g, ps_g, cg, mod0, mod1, modc = _f1_gather_matmul(
        idx1, ctx[0], x[0], nw0, hgrn_w_in[0], hgrn_w_out[0], pool_w_in[0], pool_w_grp[0], pool_w_out[0], hgrn_lb_logits[0],
        pool_scale, c, cctx2, ada_w, ada_b)
    lb = jax.nn.sigmoid(jnp.transpose(lbl_g, (1, 0, 2)).reshape(2, E))
    pscale = ps_g.reshape(1, E)
    msel = jnp.stack([modc[:2], mod0[:2]])
    p0, p1, v_all, dec, pwin, pgrp = _gla_prep(g_all, lb, cum01, s_pwin, s_pgrp)
    o, wout, pwout = _gla_fwd(p0, p1, v_all, dec, mask01, s_wout, s_pwout)
    x1 = _f3_out(o, g_all, x[0], mod0[2:3], hgrn_gnorm_w, wout)
    dx1, gpwin, gpgrp, gpwout, dmod1, gnw1, gfw, gps, lossv = _pool_layer(
        x1, loss_target[0], mod1, nw1, fnw, pwin, pgrp, pscale, pwout, pb, pbt, pinv)
    do, dz, gwout, dgate0, ggw, rpwout = _b3_out_bwd(dx1, o, g_all, mod0[2:3], hgrn_gnorm_w, wout, gpwout)
    d0, d1, dv, dgl, rpwin = _gla_bwd(p0, p1, v_all, dec, do, mask01, gpwin)
    dg, dlb, rwout, rpgrp = _gla_post_bwd(g_all, d0, d1, dgl, dv, dz, lb, cum01, gwout, gpgrp)
    grad_x, rwin, dmx, dmc, gnw0 = _b1_in_bwd(idx1, ctx[0], x[0], dx1, dg, nw0, msel, win)

    dmod0 = jnp.concatenate([dmx, dgate0], axis=0)
    dmodc = jnp.concatenate([dmc, jnp.zeros((1, D), F32)], axis=0)
    pd = jnp.stack([dmod0, dmod1, dmodc]).reshape(3, NDEV, SH_ADA)
    pv = jnp.concatenate([gnw0, gnw1, ggw, gfw, dlb.reshape(1, 2 * E), gps, lossv], axis=1)
    g_ada, g_adab, g_cctx, pvsum, loss128 = _reduce_small(pd, pv, cg, cctx2, ada_w[0])

    out = {}
    out["hgrn_w_in"] = _adam_sharded("adam_w_in", rwin, hgrn_w_in[0], m_hgrn_w_in[0], v_hgrn_w_in[0], 256)
    out["hgrn_w_out"] = _adam_sharded("adam_w_out", rwout, hgrn_w_out[0], m_hgrn_w_out[0], v_hgrn_w_out[0], SH_ROWS)
    out["pool_w_in"] = _adam_sharded("adam_pw_in", rpwin, pool_w_in[0], m_pool_w_in[0], v_pool_w_in[0], 512)
    out["pool_w_grp"] = _adam_sharded("adam_pgrp", rpgrp.reshape(NDEV, 4 * SH_GRP, PG), pool_w_grp[0].reshape(4 * SH_GRP, PG),
                                      m_pool_w_grp[0].reshape(4 * SH_GRP, PG), v_pool_w_grp[0].reshape(4 * SH_GRP, PG), 4 * SH_GRP)
    out["pool_w_out"] = _adam_sharded("adam_pw_out", rpwout, pool_w_out[0], m_pool_w_out[0], v_pool_w_out[0], SH_ROWS)
    g_ada2 = g_ada.reshape(2 * D, SH_ADA)
    out["ada_w"] = (g_ada2,) + _adam_dense("adam_ada_w", g_ada2, ada_w.reshape(2 * D, SH_ADA), m_ada_w.reshape(2 * D, SH_ADA),
                                           v_ada_w.reshape(2 * D, SH_ADA), 512)

    lb_me = lax.dynamic_slice_in_dim(lb, idx * DH, DH, axis=1)
    small = ["c_ctx", "ada_b", "norm_w", "hgrn_lb_logits", "hgrn_gnorm_w", "pool_scale", "final_norm_w"]
    gs = [g_cctx, g_adab.reshape(2, 3 * D), pvsum[:, PV_NW:PV_NW + 2 * D].reshape(2, D),
          lax.dynamic_slice_in_dim(pvsum[:, PV_LB:PV_LB + 2 * E].reshape(2, E), idx * DH, DH, axis=1),
          pvsum[:, PV_GNORM:PV_GNORM + E], lax.dynamic_slice_in_dim(pvsum[:, PV_PSCALE:PV_PSCALE + E], idx * DH, DH, axis=1),
          pvsum[:, PV_FINAL:PV_FINAL + D]]
    ws = [cctx2, ada_b, norm_w, hgrn_lb_logits[0], hgrn_gnorm_w, pool_scale, fnw]
    ms = [m_c_ctx.reshape(1, D), m_ada_b, m_norm_w, m_hgrn_lb_logits[0], m_hgrn_gnorm_w, m_pool_scale, m_final_norm_w.reshape(1, D)]
    vs = [v_c_ctx.reshape(1, D), v_ada_b, v_norm_w, v_hgrn_lb_logits[0], v_hgrn_gnorm_w, v_pool_scale, v_final_norm_w.reshape(1, D)]
    res = _adam_small(gs, ws, ms, vs, 3, lb_me)
    n = len(small)
    for j, name in enumerate(small):
        out[name] = tuple(res[q * n + j] for q in range(4))

    shapes = {"c_ctx": (D,), "ada_w": (2, D, SH_ADA), "ada_b": (2, 3 * D), "norm_w": (2, D), "hgrn_w_in": (1, D, SH_WIN),
              "hgrn_lb_logits": (1, 2, DH), "hgrn_gnorm_w": (1, E), "hgrn_w_out": (1, SH_ROWS, D), "pool_w_in": (1, D, SH_PWIN),
              "pool_w_grp": (1, 4, SH_GRP, PG), "pool_scale": (1, DH), "pool_w_out": (1, SH_ROWS, D), "final_norm_w": (D,)}
    order = ["c_ctx", "ada_w", "ada_b", "norm_w", "hgrn_w_in", "hgrn_lb_logits", "hgrn_gnorm_w", "hgrn_w_out", "pool_w_in",
             "pool_w_grp", "pool_scale", "pool_w_out", "final_norm_w"]
    flat = [out[name][q].reshape(shapes[name]) for q in range(4) for name in order]
    return (loss128[0, 0], grad_x[None], *flat)
```

```python
import functools

import numpy as np
import jax
import jax.numpy as jnp
from jax import lax
from jax.experimental import pallas as pl
from jax.experimental.pallas import tpu as pltpu

F32 = jnp.float32
BF16 = jnp.bfloat16

D = 1024
E = 1024
HEADS = 8
DH = 128
CHUNK = 64
T = 2048
TC = 256
TT = T + TC
TM = 256
NT = TT // TM
NTX = T // TM
NDEV = 8
GRID_W = 64
POOL_WINDOWS = (2, 4, 8, 16)
PG = 256
EPS = 1e-6
WIN_COLS = 5 * E
SH_WIN = WIN_COLS // NDEV
SH_PWIN = 2 * E // NDEV
SH_ROWS = E // NDEV
SH_GRP = PG // NDEV
SH_ADA = 3 * D // NDEV
VMEM_LIMIT = 56 * 1024 * 1024
VMEM_LIMIT_SCAN = 60 * 1024 * 1024

ADAM_LR, ADAM_B1, ADAM_B2, ADAM_EPS, ADAM_WD, ADAM_STEP = 0.001, 0.9, 0.999, 1e-08, 0.01, 10

MESH = pl.DeviceIdType.MESH
VMEM_SPEC = pl.BlockSpec(memory_space=pltpu.VMEM)
HBM_SPEC = pl.BlockSpec(memory_space=pltpu.HBM)
ANY_SPEC = pl.BlockSpec(memory_space=pl.ANY)


def _sds(shape, dtype):
    return jax.ShapeDtypeStruct(shape, dtype)


def _bf(a):
    return a if a.dtype == BF16 else a.astype(BF16)


def _dot(a, b):
    return lax.dot_general(_bf(a), _bf(b), (((1,), (0,)), ((), ())), preferred_element_type=F32)


def _dot_tb(a, b):
    return lax.dot_general(_bf(a), _bf(b), (((1,), (1,)), ((), ())), preferred_element_type=F32)


def _dot_ta(a, b):
    return lax.dot_general(_bf(a), _bf(b), (((0,), (0,)), ((), ())), preferred_element_type=F32)


def _bdot(a, b):
    return lax.dot_general(_bf(a), _bf(b), (((2,), (1,)), ((0,), (0,))), preferred_element_type=F32)


def _bdot_nt(a, b):
    return lax.dot_general(_bf(a), _bf(b), (((2,), (2,)), ((0,), (0,))), preferred_element_type=F32)


def _bdot_tn(a, b):
    return lax.dot_general(_bf(a), _bf(b), (((1,), (1,)), ((0,), (0,))), preferred_element_type=F32)


def _dot01(m01, x):
    hi = x.astype(BF16)
    lo = (x - hi.astype(F32)).astype(BF16)
    return _dot(m01, hi) + _dot(m01, lo)


def _rstd(x):
    return lax.rsqrt(jnp.mean(x * x, axis=-1, keepdims=True) + EPS)


def _sigmoid(x):
    return jax.nn.sigmoid(x)


def _colsum(a):
    return jnp.sum(a, axis=0, keepdims=True)


def _stack_rows(rows):
    n = rows[0].shape[-1]
    rid = lax.broadcasted_iota(jnp.int32, (16, n), 0)
    out = jnp.zeros((16, n), F32)
    for i, r in enumerate(rows):
        out = jnp.where(rid == i, r, out)
    return out


def _head_map(fn, *arrs):
    outs = [fn(*[a[:, h * DH:(h + 1) * DH] for a in arrs]) for h in range(HEADS)]
    return jnp.concatenate(outs, axis=1)


def _gla_consts():
    r = np.arange(TM)[:, None]
    c = np.arange(TM)[None, :]
    same = (r // CHUNK) == (c // CHUNK)
    tril = same & (c <= r)
    triu = same & (c >= r)
    m = np.stack([tril, triu]).astype(np.float32)
    return jnp.asarray(m, BF16), jnp.asarray(m, F32)


def _pool_consts():
    r = np.arange(TM)[:, None]
    c = np.arange(TM)[None, :]
    same = (r // GRID_W) == (c // GRID_W)
    rp, cp = r % GRID_W, c % GRID_W
    bs, inv = [], []
    for w in POOL_WINDOWS:
        lo = np.clip(rp - w // 2, 0, GRID_W)
        hi = np.clip(rp - w // 2 + w, 0, GRID_W)
        bs.append(same & (cp >= lo) & (cp < hi))
        inv.append(1.0 / (hi - lo).astype(np.float32))
    b = np.stack(bs).astype(np.float32)
    bt = np.transpose(b, (0, 2, 1))
    return jnp.asarray(b, BF16), jnp.asarray(bt, BF16), jnp.asarray(np.stack(inv), F32)


def _mesh_pos():
    x, y, c = lax.axis_index("x"), lax.axis_index("y"), lax.axis_index("c")
    return x, y, c, 4 * x + 2 * y + c


def _peer(x, y, c, k):
    return (x ^ ((k >> 2) & 1), y ^ ((k >> 1) & 1), c ^ (k & 1))


def _small_gathers(refs, ssem, rsem):
    lb_r, ps_r, c_r, cctx_r, ada_r, adab_r, lb_o, ps_o, cg_o, mod_o, lb_out, ps_out, cg_out, mod0_o, mod1_o, modc_o = refs
    x, y, cc, idx = _mesh_pos()
    srcs = [lb_r, ps_r, c_r, mod_o.at[idx]]
    mine = [lb_o.at[idx], ps_o.at[idx], cg_o.at[idx], mod_o.at[idx]]

    def remote(a, k):
        return pltpu.make_async_remote_copy(src_ref=srcs[a], dst_ref=mine[a], send_sem=ssem.at[a, k], recv_sem=rsem.at[a, k],
                                            device_id=_peer(x, y, cc, k), device_id_type=MESH)

    first = [remote(a, k) for k in range(1, NDEV) for a in (2, 0, 1)]
    for cp in first:
        cp.start()
    lb_o[idx] = lb_r[...]
    ps_o[idx] = ps_r[...]
    cg_o[idx] = c_r[...]
    for k in range(1, NDEV):
        remote(2, k).wait_recv()
    rows = _stack_rows([cg_o[i] for i in range(NDEV)] + [cctx_r[...]])
    sc = rows * _sigmoid(rows)
    for l in range(2):
        mod_o[idx, l] = _dot(sc, ada_r[l])
    second = [remote(3, k) for k in range(1, NDEV)]
    for cp in second:
        cp.start()
    for k in range(1, NDEV):
        remote(3, k).wait_recv()

    def mod_rows(l, row):
        full = jnp.concatenate([mod_o[s, l, row, :] for s in range(NDEV)], axis=1) + adab_r[l:l + 1, :]
        return [full[:, j * D:(j + 1) * D] for j in range(3)]

    me = pl.ds(idx, 1)
    for out, parts in ((mod0_o, mod_rows(0, me)), (mod1_o, mod_rows(1, me)), (modc_o, mod_rows(0, slice(NDEV, NDEV + 1)))):
        for j in range(3):
            out[j:j + 1, :] = parts[j]
    for cp in first + second:
        cp.wait_send()
    for k in range(1, NDEV):
        for a in (0, 1):
            remote(a, k).wait_recv()
    lb_out[...] = lb_o[...]
    ps_out[...] = ps_o[...]
    cg_out[...] = cg_o[...]


def _gather_order(s):
    if isinstance(s, int):
        return (0, 1, 2, 4, 3, 5, 6, 7)[s]
    return s + (s == 3).astype(jnp.int32) - (s == 4).astype(jnp.int32)


GATHER_ISSUE = (1, 2, 4, 3, 5, 6, 7)
GATHER_ICI = (2, 4, 6)
GATHER_DIRECT = (1,) + GATHER_ICI
GLA_HB = 2
RS_SLOTS = 5


def _shard_of(kind, ref, i):
    if kind == "rows":
        return ref.at[pl.ds(pl.multiple_of(i * SH_ROWS, SH_ROWS), SH_ROWS), :]
    if kind == "major":
        return ref.at[i]
    assert kind == "grp"
    return ref.at[:, pl.ds(pl.multiple_of(i * SH_GRP, SH_GRP), SH_GRP), :]


def _gather_rider(step, n_steps, forward_at, kinds, srcs, outs, ssem, rsem, lsem):
    x, y, cc, idx = _mesh_pos()
    arrays = range(len(kinds))
    mine = [_shard_of(kinds[a], outs[a], idx) for a in arrays]

    def remote(a, k):
        return pltpu.make_async_remote_copy(src_ref=srcs[a], dst_ref=mine[a], send_sem=ssem.at[a, k], recv_sem=rsem.at[a, k],
                                            device_id=_peer(x, y, cc, k), device_id_type=MESH)

    def forward(a, k):
        blk = _shard_of(kinds[a], outs[a], idx ^ k)
        return pltpu.make_async_remote_copy(src_ref=blk, dst_ref=blk, send_sem=ssem.at[a, k ^ 1], recv_sem=rsem.at[a, k ^ 1],
                                            device_id=(x, y, 1 - cc), device_id_type=MESH)

    copies = [remote(a, k) for k in GATHER_DIRECT for a in arrays]
    passed = [forward(a, k) for k in GATHER_ICI for a in arrays]
    local = [pltpu.make_async_copy(srcs[a], mine[a], lsem.at[a]) for a in arrays]

    @pl.when(step == 0)
    def _():
        for cp in copies + local:
            cp.start()

    @pl.when(step == forward_at)
    def _():
        for k in GATHER_ICI:
            for a in arrays:
                remote(a, k).wait_recv()
                forward(a, k).start()

    @pl.when(step == n_steps - 1)
    def _():
        for cp in copies + passed:
            cp.wait_send()
        for a in arrays:
            remote(a, 1).wait_recv()
        for cp in passed:
            cp.wait_recv()
        for cp in local:
            cp.wait()


def _scatter_rider(step, n_steps, kinds, grads, slots, ssem, rsem, lsem):
    x, y, cc, idx = _mesh_pos()
    arrays = range(len(kinds))
    dsts = [slots[a].at[idx] for a in arrays]

    def remote(a, k):
        px, py, pc = _peer(x, y, cc, k)
        return pltpu.make_async_remote_copy(src_ref=_shard_of(kinds[a], grads[a], 4 * px + 2 * py + pc), dst_ref=dsts[a],
                                            send_sem=ssem.at[a, k], recv_sem=rsem.at[a, k], device_id=(px, py, pc), device_id_type=MESH)

    copies = [remote(a, k) for k in GATHER_ISSUE for a in arrays]
    local = [pltpu.make_async_copy(_shard_of(kinds[a], grads[a], idx), dsts[a], lsem.at[a]) for a in arrays]

    @pl.when(step == 0)
    def _():
        for cp in copies + local:
            cp.start()

    @pl.when(step == n_steps - 1)
    def _():
        for cp in copies:
            cp.wait_send()
        for cp in copies:
            cp.wait_recv()
        for cp in local:
            cp.wait()


def _rider_sems(n):
    return [pltpu.SemaphoreType.DMA((n, NDEV)), pltpu.SemaphoreType.DMA((n, NDEV)), pltpu.SemaphoreType.DMA((n,))]


def _modulated(x, nw, shift, scale):
    r = _rstd(x)
    xn = x * r
    a = xn * nw
    return a * (1.0 + scale) + shift, r, xn, a


def _ctx_or_x(i, ctx_ref, x_ref):
    return jnp.where(i == 0, ctx_ref[...], x_ref[...])


def _f1_gather_matmul(idx1, ctx, x, nw, w_in, w_out, pw_in, pgrp, pw_out, lb_l, pscale, c, c_ctx, ada_w, ada_b):
    def body(idx_ref, ctx_ref, x_ref, nw_ref, win_r, wout_r, pwin_r, pgrp_r, pwout_r, lb_r, ps_r, c_r, cctx_r, ada_r, adab_r,
             g_ref, win_o, s_wout, s_pwin, s_pgrp, s_pwout, lb_o, ps_o, cg_o, mod0_o, mod1_o, modc_o,
             wslot, hx_sc, lb_g, ps_g, cg_g, mod_g, ssem, rsem, osem, sm_ssem, sm_rsem):
        del idx_ref
        s, i = pl.program_id(0), pl.program_id(1)
        x, y, cc, idx = _mesh_pos()
        k = _gather_order(s)
        j = idx ^ k

        def remote(kk):
            return pltpu.make_async_remote_copy(src_ref=wslot.at[idx], dst_ref=wslot.at[idx], send_sem=ssem.at[kk], recv_sem=rsem.at[kk],
                                                device_id=_peer(x, y, cc, kk), device_id_type=MESH)

        def forward(kk):
            jj = idx ^ kk
            return pltpu.make_async_remote_copy(src_ref=wslot.at[jj], dst_ref=wslot.at[jj], send_sem=ssem.at[kk ^ 1],
                                                recv_sem=rsem.at[kk ^ 1], device_id=(x, y, 1 - cc), device_id_type=MESH)

        def to_hbm(jj, kk):
            return pltpu.make_async_copy(wslot.at[jj], win_o.at[:, pl.ds(pl.multiple_of(jj * SH_WIN, 128), SH_WIN)], osem.at[kk])

        @pl.when((s == 0) & (i == 0))
        def _():
            _small_gathers((lb_r, ps_r, c_r, cctx_r, ada_r, adab_r, lb_g, ps_g, cg_g, mod_g, lb_o, ps_o, cg_o, mod0_o, mod1_o, modc_o),
                           sm_ssem, sm_rsem)
            wslot[idx] = win_r[...].astype(BF16)
            for kk in GATHER_DIRECT:
                remote(kk).start()
            s_wout[...] = wout_r[...].astype(BF16)
            s_pwin[...] = pwin_r[...].astype(BF16)
            s_pgrp[...] = pgrp_r[...].astype(BF16)
            s_pwout[...] = pwout_r[...].astype(BF16)

        @pl.when(s == 0)
        def _():
            shift = jnp.where(i == 0, modc_o[0:1, :], mod0_o[0:1, :])
            scale = jnp.where(i == 0, modc_o[1:2, :], mod0_o[1:2, :])
            hx, _, _, _ = _modulated(_ctx_or_x(i, ctx_ref, x_ref), nw_ref[...], shift, scale)
            hx_sc[i] = hx.astype(BF16)

        @pl.when((s > 0) & (i == 0))
        def _():
            remote(k).wait_recv()

            @pl.when((k & 1) == 0)
            def _():
                forward(k).start()

        @pl.when(i == 0)
        def _():
            to_hbm(j, k).start()

        g_ref[...] = jnp.dot(hx_sc[i], wslot[j], preferred_element_type=F32)

        @pl.when((s == NDEV - 1) & (i == NT - 1))
        def _():
            for kk in GATHER_DIRECT:
                remote(kk).wait_send()
            for kk in GATHER_ICI:
                forward(kk).wait_send()
            for kk in range(NDEV):
                to_hbm(idx ^ kk, kk).wait()

    grid_spec = pltpu.PrefetchScalarGridSpec(
        num_scalar_prefetch=1, grid=(NDEV, NT),
        in_specs=[VMEM_SPEC, pl.BlockSpec((TM, D), lambda s, i, ix: (jnp.maximum(i - 1, 0), 0))] + [VMEM_SPEC] * 12,
        out_specs=[pl.BlockSpec((TM, SH_WIN), lambda s, i, ix: (i, ix[0] ^ _gather_order(s))), HBM_SPEC] + [VMEM_SPEC] * 10,
        scratch_shapes=[pltpu.VMEM((NDEV, D, SH_WIN), BF16), pltpu.VMEM((NT, TM, D), BF16),
                        pltpu.VMEM((NDEV, 2, DH), F32), pltpu.VMEM((NDEV, 1, DH), F32), pltpu.VMEM((NDEV, 1, D), F32),
                        pltpu.VMEM((NDEV, 2, 16, SH_ADA), F32),
                        pltpu.SemaphoreType.DMA((NDEV,)), pltpu.SemaphoreType.DMA((NDEV,)), pltpu.SemaphoreType.DMA((NDEV,)),
                        pltpu.SemaphoreType.DMA((4, NDEV)), pltpu.SemaphoreType.DMA((4, NDEV))])
    outs = (_sds((TT, WIN_COLS), F32), _sds((D, WIN_COLS), BF16),
            _sds((SH_ROWS, D), BF16), _sds((D, SH_PWIN), BF16), _sds((4, SH_GRP, PG), BF16), _sds((SH_ROWS, D), BF16),
            _sds((NDEV, 2, DH), F32), _sds((NDEV, 1, DH), F32), _sds((NDEV, 1, D), F32),
            _sds((3, D), F32), _sds((3, D), F32), _sds((3, D), F32))
    return pl.pallas_call(
        body, name="f1_gather_matmul", grid_spec=grid_spec, out_shape=outs,
        compiler_params=pltpu.CompilerParams(dimension_semantics=("arbitrary", "arbitrary"), vmem_limit_bytes=VMEM_LIMIT),
    )(idx1, ctx, x, nw, w_in, w_out, pw_in, pgrp, pw_out, lb_l, pscale, c, c_ctx, ada_w, ada_b)


def _gla_gates(pre, qpre, lbd, cum, rev):
    rows, n = pre.shape
    nch = rows // CHUNK
    sig = _sigmoid(pre)
    f = lbd + (1.0 - lbd) * sig
    k = 1.0 - f
    g = _dot01(cum, jnp.log(f))
    g3 = g.reshape(nch, CHUNK, n)
    last = 0 if rev else CHUNK - 1
    mid = CHUNK // 2 if rev else CHUNK // 2 - 1
    gl1, gm1 = g3[:, last:last + 1, :], g3[:, mid:mid + 1, :]

    def bc(a):
        return jnp.broadcast_to(a, g3.shape).reshape(rows, n)

    gm = bc(gm1)
    e_q, e_k = jnp.exp(g - gm), jnp.exp(gm - g)
    e_in, e_end = e_q * bc(jnp.exp(gm1)), e_k * bc(jnp.exp(gl1 - gm1))
    qsig = _sigmoid(qpre)
    qs = qpre * qsig * (DH ** -0.5)
    return dict(sig=sig, f=f, k=k, qsig=qsig, qs=qs, e_q=e_q, e_k=e_k, e_in=e_in, e_end=e_end,
                decay=[jnp.exp(g3[ci, last:last + 1, :]) for ci in range(nch)])


def _put_heads(ref, lead, arr):
    for h in range(HEADS):
        ref[lead + (h,)] = arr[:, h * DH:(h + 1) * DH]


def _get_heads(ref, lead=()):
    return jnp.concatenate([ref[lead + (h,)] for h in range(HEADS)], axis=1)


def _gla_prep(g_all, lb, cum01, s_pwin, s_pgrp):
    def body(g_ref, lb_ref, cum_ref, spwin_r, spgrp_r, p0_ref, p1_ref, v_ref, dec_ref, pwin_o, pgrp_o, ssem, rsem, lsem):
        _gather_rider(pl.program_id(0), NT, NT // 2 + 1, ("major", "grp"), (spwin_r, spgrp_r), (pwin_o, pgrp_o), ssem, rsem, lsem)
        qpre = g_ref[:, 3 * E:4 * E]
        _put_heads(v_ref, (), g_ref[:, 2 * E:3 * E].astype(BF16))
        dec_ref[...] = jnp.zeros_like(dec_ref)
        for d, p_ref in ((0, p0_ref), (1, p1_ref)):
            t = _gla_gates(g_ref[:, d * E:(d + 1) * E], qpre, lb_ref[d:d + 1, :], cum_ref[d], d == 1)
            _put_heads(p_ref, (0,), (t["qs"] * t["e_q"]).astype(BF16))
            _put_heads(p_ref, (1,), (t["k"] * t["e_k"]).astype(BF16))
            _put_heads(p_ref, (2,), (t["qs"] * t["e_in"]).astype(BF16))
            _put_heads(p_ref, (3,), (t["k"] * t["e_end"]).astype(BF16))
            for ci in range(TM // CHUNK):
                dec_ref[d, 0, ci:ci + 1, :] = t["decay"][ci]

    quad = pl.BlockSpec((4, HEADS, TM, DH), lambda i: (0, 0, i, 0))
    return pl.pallas_call(
        body, name="gla_prep", grid=(NT,),
        in_specs=[pl.BlockSpec((TM, 4 * E), lambda i: (i, 0)), VMEM_SPEC, VMEM_SPEC, HBM_SPEC, HBM_SPEC],
        out_specs=[quad, quad, pl.BlockSpec((HEADS, TM, DH), lambda i: (0, i, 0)), pl.BlockSpec((2, 1, 8, E), lambda i: (0, i, 0, 0)),
                   HBM_SPEC, HBM_SPEC],
        out_shape=(_sds((4, HEADS, TT, DH), BF16), _sds((4, HEADS, TT, DH), BF16), _sds((HEADS, TT, DH), BF16), _sds((2, NT, 8, E), F32),
                   _sds((NDEV, D, SH_PWIN), BF16), _sds((4, PG, PG), BF16)),
        scratch_shapes=_rider_sems(2),
        compiler_params=pltpu.CompilerParams(dimension_semantics=("arbitrary",), vmem_limit_bytes=VMEM_LIMIT),
    )(g_all, lb, cum01, s_pwin, s_pgrp)


def _scan_tile(i, rev):
    t = jnp.where(i == 0, 0, NT - i) if rev else i
    return t, pl.ds(pl.multiple_of(t * TM, TM), TM)


def _chunk_order(rev):
    n = TM // CHUNK
    return tuple(range(n - 1, -1, -1)) if rev else tuple(range(n))


def _gla_fwd(p0, p1, v_all, dec, mask01, s_wout, s_pwout):
    n_steps = HEADS // GLA_HB

    def body(p0_ref, p1_ref, v_ref, dec_ref, msk_ref, swout_r, spwout_r, o_ref, wout_o, pwout_o, ob_sc, ssem, rsem, lsem):
        _gather_rider(pl.program_id(0), n_steps, n_steps // 2, ("rows", "rows"), (swout_r, spwout_r), (wout_o, pwout_o), ssem, rsem, lsem)

        lanes = [(d, hh) for d in (0, 1) for hh in range(GLA_HB)]
        nch = TM // CHUNK

        def tile_body(i, st):
            where = [_scan_tile(i, d == 1) for d in (0, 1)]

            def stacked(fn):
                return jnp.stack([fn(d, hh, where[d][1]) for d, hh in lanes])

            qg, kg, q_in, kend = [stacked(lambda d, hh, rows, ty=ty: (p1_ref if d else p0_ref)[ty, hh, rows, :]) for ty in range(4)]
            v = stacked(lambda d, hh, rows: v_ref[hh, rows, :])
            a = _bdot_nt(qg, kg) * jnp.stack([msk_ref[d] for d, _ in lanes])
            intra = _bdot(a, v)
            outs = [[None] * nch for _ in lanes]
            for n in range(nch):
                cis = [nch - 1 - n if d else n for d, _ in lanes]

                def chunk(arr):
                    return jnp.stack([arr[l, ci * CHUNK:(ci + 1) * CHUNK] for l, ci in enumerate(cis)])

                dec = jnp.stack([dec_ref[d, where[d][0], ci:ci + 1, hh * DH:(hh + 1) * DH] for (d, hh), ci in zip(lanes, cis)])
                inter = _bdot_nt(chunk(q_in), st)
                for l, ci in enumerate(cis):
                    outs[l][ci] = inter[l] + intra[l, ci * CHUNK:(ci + 1) * CHUNK]
                st = st * dec + _bdot_tn(chunk(v), chunk(kend))
            for l, (d, hh) in enumerate(lanes):
                (ob_sc if d else o_ref)[hh, where[d][1], :] = jnp.concatenate(outs[l], axis=0)
            return st

        lax.fori_loop(0, NT, tile_body, jnp.zeros((len(lanes), DH, DH), F32))
        o_ref[...] += ob_sc[...]

    quad = pl.BlockSpec((4, GLA_HB, TT, DH), lambda h: (0, h, 0, 0))
    head = pl.BlockSpec((GLA_HB, TT, DH), lambda h: (h, 0, 0))
    return pl.pallas_call(
        body, name="gla_fwd", grid=(n_steps,),
        in_specs=[quad, quad, head, pl.BlockSpec((2, NT, 8, GLA_HB * DH), lambda h: (0, 0, 0, h)),
                  pl.BlockSpec((2, TM, TM), lambda h: (0, 0, 0)), HBM_SPEC, HBM_SPEC],
        out_specs=[head, HBM_SPEC, HBM_SPEC],
        out_shape=(_sds((HEADS, TT, DH), F32), _sds((E, D), BF16), _sds((E, D), BF16)),
        scratch_shapes=[pltpu.VMEM((GLA_HB, TT, DH), F32)] + _rider_sems(2),
        compiler_params=pltpu.CompilerParams(dimension_semantics=("arbitrary",), vmem_limit_bytes=VMEM_LIMIT),
    )(p0, p1, v_all, dec, mask01, s_wout, s_pwout)


def _gated_norm(o, z, gw):
    r = _head_map(lambda oh: jnp.broadcast_to(_rstd(oh), oh.shape), o)
    on = o * r
    zs = _sigmoid(z)
    sz = z * zs
    return on * gw * sz, r, on, zs, sz


def _f3_out(o, g_all, x, gate, gw, wout):
    def body(o_ref, z_ref, x_ref, gate_ref, gw_ref, w_ref, x1_ref):
        og, _, _, _, _ = _gated_norm(_get_heads(o_ref), z_ref[...], gw_ref[...])
        x1_ref[...] = x_ref[...] + gate_ref[...] * _dot(og, w_ref[...])

    return pl.pallas_call(
        body, name="f3_out", grid=(NTX,),
        in_specs=[pl.BlockSpec((HEADS, TM, DH), lambda i: (0, i + 1, 0)), pl.BlockSpec((TM, E), lambda i: (i + 1, 4)),
                  pl.BlockSpec((TM, D), lambda i: (i, 0)), pl.BlockSpec((1, D), lambda i: (0, 0)),
                  pl.BlockSpec((1, E), lambda i: (0, 0)), pl.BlockSpec((E, D), lambda i: (0, 0))],
        out_specs=pl.BlockSpec((TM, D), lambda i: (i, 0)),
        out_shape=_sds((T, D), F32),
        compiler_params=pltpu.CompilerParams(dimension_semantics=("arbitrary",)),
    )(o, g_all, x, gate, gw, wout)


def _pool_layer(x1, tgt, mod1, nw1, fnw, pwin, pgrp, pscale, pwout, pb, pbt, pinv):
    def body(x_ref, t_ref, m_ref, nw_ref, fw_ref, pwin_ref, pgrp_ref, ps_ref, pwout_ref, pb_ref, pbt_ref, pinv_ref,
             dx_ref, gpwin_o, gpgrp_o, gpwout_o, dmod_o, gnw_o, gfw_o, gps_o, loss_o,
             a_pwin, a_pgrp, a_pwout):
        i = pl.program_id(0)

        @pl.when(i == 0)
        def _():
            for ref in (a_pwin, a_pgrp, a_pwout, dmod_o, gnw_o, gfw_o, gps_o, loss_o):
                ref[...] = jnp.zeros_like(ref)

        shift, scale, gate = m_ref[0:1, :], m_ref[1:2, :], m_ref[2:3, :]
        nw, fw, ps = nw_ref[...], fw_ref[...], ps_ref[...]
        x1 = x_ref[...]
        hx, r1, xn, a = _modulated(x1, nw, shift, scale)
        hxb = hx.astype(BF16)
        uz = jnp.concatenate([_dot(hxb, pwin_ref[j]) for j in range(NDEV)], axis=1)
        u, z = uz[:, :E], uz[:, E:]
        pooled, ys = [], []
        for g in range(4):
            ug = u[:, g * PG:(g + 1) * PG]
            pg = _dot01(pb_ref[g], ug) * pinv_ref[g] - ug
            pooled.append(pg.astype(BF16))
            ys.append(_dot(pooled[g], pgrp_ref[g]))
        ycat = jnp.concatenate(ys, axis=1)
        y = ycat * ps
        zs = _sigmoid(z)
        sz = z * zs
        p = (y * sz).astype(BF16)
        out = _dot(p, pwout_ref[...])
        x2 = x1 + gate * out
        r2 = _rstd(x2)
        xn2 = x2 * r2
        diff = xn2 * fw - t_ref[...]
        loss_o[...] += _colsum(diff * diff)
        dyf = diff * (1.0 / D)
        gfw_o[...] += _colsum(dyf * xn2)
        dxn2 = dyf * fw
        dx2 = r2 * (dxn2 - xn2 * jnp.mean(dxn2 * xn2, axis=-1, keepdims=True))
        dgate = _colsum(dx2 * out)
        dout = (dx2 * gate).astype(BF16)
        for j in range(4):
            cs = slice(j * PG, (j + 1) * PG)
            a_pwout[:, cs] += _dot_ta(p, dout[:, cs])
        dp = _dot_tb(dout, pwout_ref[...])
        dy = dp * sz
        dz = dp * y * (zs * (1.0 + z * (1.0 - zs)))
        gps_o[...] += _colsum(dy * ycat)
        dycat = dy * ps
        dus = []
        for g in range(4):
            dyg = dycat[:, g * PG:(g + 1) * PG].astype(BF16)
            a_pgrp[g] += _dot_ta(pooled[g], dyg)
            dpg = _dot_tb(dyg, pgrp_ref[g])
            dus.append(_dot01(pbt_ref[g], dpg * pinv_ref[g]) - dpg)
        duz = jnp.concatenate(dus + [dz], axis=1).astype(BF16)
        dhx = None
        for j in range(NDEV):
            dj = duz[:, j * SH_PWIN:(j + 1) * SH_PWIN]
            a_pwin[j] += _dot_ta(hxb, dj)
            part = _dot_tb(dj, pwin_ref[j])
            dhx = part if dhx is None else dhx + part
        dmod_o[0:1, :] += _colsum(dhx)
        dmod_o[1:2, :] += _colsum(dhx * a)
        dmod_o[2:3, :] += dgate
        da = dhx * (1.0 + scale)
        gnw_o[...] += _colsum(da * xn)
        dxn = da * nw
        dx_ref[...] = dx2 + r1 * (dxn - xn * jnp.mean(dxn * xn, axis=-1, keepdims=True))

        @pl.when(i == NTX - 1)
        def _():
            gpwin_o[...] = a_pwin[...].astype(BF16)
            gpgrp_o[...] = a_pgrp[...].astype(BF16)
            gpwout_o[...] = a_pwout[...].astype(BF16)

    tile = pl.BlockSpec((TM, D), lambda i: (i, 0))
    outs = (_sds((T, D), F32), _sds((NDEV, D, SH_PWIN), BF16), _sds((4, PG, PG), BF16), _sds((E, D), BF16),
            _sds((3, D), F32), _sds((1, D), F32), _sds((1, D), F32), _sds((1, E), F32), _sds((1, D), F32))
    return pl.pallas_call(
        body, name="pool_layer", grid=(NTX,),
        in_specs=[tile, tile] + [VMEM_SPEC] * 10,
        out_specs=[tile] + [VMEM_SPEC] * 8,
        out_shape=outs,
        scratch_shapes=[pltpu.VMEM((NDEV, D, SH_PWIN), F32), pltpu.VMEM((4, PG, PG), F32), pltpu.VMEM((E, D), F32)],
        compiler_params=pltpu.CompilerParams(dimension_semantics=("arbitrary",), vmem_limit_bytes=VMEM_LIMIT),
    )(x1, tgt, mod1, nw1, fnw, pwin, pgrp, pscale, pwout, pb, pbt, pinv)


def _b3_out_bwd(dx1, o, g_all, gate, gw, wout, gpwout):
    def body(dx_ref, o_ref, z_ref, gate_ref, gw_ref, w_ref, gpwout_r, do_ref, dz_ref, gw_o, dgate_o, ggw_o, rpwout_o,
             acc, ssem, rsem, lsem):
        i = pl.program_id(0)
        _scatter_rider(i, NT, ("rows",), (gpwout_r,), (rpwout_o,), ssem, rsem, lsem)

        @pl.when(i == 0)
        def _():
            acc[...] = jnp.zeros_like(acc)
            dgate_o[...] = jnp.zeros_like(dgate_o)
            ggw_o[...] = jnp.zeros_like(ggw_o)
            do_ref[...] = jnp.zeros_like(do_ref)
            dz_ref[...] = jnp.zeros_like(dz_ref)

        @pl.when(i > 0)
        def _():
            gw = gw_ref[...]
            z = z_ref[...]
            og, r, on, zs, sz = _gated_norm(_get_heads(o_ref), z, gw)
            ogb = og.astype(BF16)
            dx = dx_ref[...]
            dgate_o[...] += _colsum(dx * _dot(ogb, w_ref[...]))
            dy = (dx * gate_ref[...]).astype(BF16)
            for j in range(4):
                cs = slice(j * PG, (j + 1) * PG)
                acc[:, cs] += _dot_ta(ogb, dy[:, cs])
            dog = _dot_tb(dy, w_ref[...])
            dz_ref[...] = (dog * (on * gw) * (zs * (1.0 + z * (1.0 - zs)))).astype(BF16)
            dong = dog * sz
            ggw_o[...] += _colsum(dong * on)
            don = dong * gw
            do = _head_map(lambda dh, nh, rh: rh * (dh - nh * jnp.mean(dh * nh, axis=-1, keepdims=True)), don, on, r)
            _put_heads(do_ref, (), do.astype(BF16))

        @pl.when(i == NT - 1)
        def _():
            gw_o[...] = acc[...].astype(BF16)

    prev = lambda i: (jnp.maximum(i - 1, 0), 0)
    heads = pl.BlockSpec((HEADS, TM, DH), lambda i: (0, i, 0))
    return pl.pallas_call(
        body, name="b3_out_bwd", grid=(NT,),
        in_specs=[pl.BlockSpec((TM, D), prev), heads, pl.BlockSpec((TM, E), lambda i: (i, 4)),
                  VMEM_SPEC, VMEM_SPEC, VMEM_SPEC, HBM_SPEC],
        out_specs=[heads, pl.BlockSpec((TM, E), lambda i: (i, 0)), VMEM_SPEC, VMEM_SPEC, VMEM_SPEC, HBM_SPEC],
        out_shape=(_sds((HEADS, TT, DH), BF16), _sds((TT, E), BF16), _sds((E, D), BF16), _sds((1, D), F32), _sds((1, E), F32),
                   _sds((NDEV, SH_ROWS, D), BF16)),
        scratch_shapes=[pltpu.VMEM((E, D), F32)] + _rider_sems(1),
        compiler_params=pltpu.CompilerParams(dimension_semantics=("arbitrary",), vmem_limit_bytes=VMEM_LIMIT),
    )(dx1, o, g_all, gate, gw, wout, gpwout)


def _gla_bwd(p0, p1, v_all, dec, do, mask01, gpwin):
    nch = TM // CHUNK
    n_steps = HEADS // GLA_HB

    def body(p0_ref, p1_ref, v_ref, dec_ref, do_ref, msk_ref, gpwin_r, d0_ref, d1_ref, dv_ref, dgl_ref, rpwin_o,
             ss_sc, dv_sc, ssem, rsem, lsem):
        _scatter_rider(pl.program_id(0), n_steps, ("major",), (gpwin_r,), (rpwin_o,), ssem, rsem, lsem)

        lanes = [(d, hh) for d in (0, 1) for hh in range(GLA_HB)]
        zero = jnp.zeros((len(lanes), DH, DH), F32)
        dgl_ref[...] = jnp.zeros_like(dgl_ref)

        def p_of(d):
            return p1_ref if d else p0_ref

        def scan_step(i, n):
            where = [_scan_tile(i, d == 1) for d in (0, 1)]
            cis = [nch - 1 - n if d else n for d, _ in lanes]
            dec = jnp.stack([dec_ref[d, where[d][0], ci:ci + 1, hh * DH:(hh + 1) * DH] for (d, hh), ci in zip(lanes, cis)])

            def chunk(arr):
                return jnp.stack([arr[l, ci * CHUNK:(ci + 1) * CHUNK] for l, ci in enumerate(cis)])

            return where, cis, dec, chunk

        def stacked(i, fn):
            where = [_scan_tile(i, d == 1) for d in (0, 1)]
            return jnp.stack([fn(d, hh, where[d][1]) for d, hh in lanes])

        def fwd_body(i, st):
            v = stacked(i, lambda d, hh, rows: v_ref[hh, rows, :])
            kend = stacked(i, lambda d, hh, rows: p_of(d)[3, hh, rows, :])
            for n in range(nch):
                _, _, dec, chunk = scan_step(i, n)
                ss_sc[i * nch + n] = st
                st = st * dec + _bdot_tn(chunk(v), chunk(kend))
            return st

        lax.fori_loop(0, NT, fwd_body, zero)

        def bwd_body(ii, dst):
            i = NT - 1 - ii
            qg, kg, q_in, kend = [stacked(i, lambda d, hh, rows, ty=ty: p_of(d)[ty, hh, rows, :]) for ty in range(4)]
            v = stacked(i, lambda d, hh, rows: v_ref[hh, rows, :])
            dob = stacked(i, lambda d, hh, rows: do_ref[hh, rows, :])
            msk = jnp.stack([msk_ref[d] for d, _ in lanes])
            a = (_bdot_nt(qg, kg) * msk).astype(BF16)
            da = (_bdot_nt(dob, v) * msk).astype(BF16)
            dqg = _bdot(da, kg)
            dkg = _bdot_tn(da, qg)
            dv_intra = _bdot_tn(a, dob)
            dv_l, dkend_l, dqin_l = ([[None] * nch for _ in lanes] for _ in range(3))
            for n in range(nch - 1, -1, -1):
                where, cis, dec, chunk = scan_step(i, n)
                s_c = ss_sc[i * nch + n]
                dstb = dst.astype(BF16)
                kend_c, v_c, dob_c = chunk(kend), chunk(v), chunk(dob)
                dv_c = chunk(dv_intra) + _bdot_nt(kend_c, dstb)
                dkend_c = _bdot(v_c, dstb)
                dqin_c = _bdot(dob_c, s_c)
                dgl = jnp.sum(s_c * dst, axis=1, keepdims=True) * dec
                for l, ((d, hh), ci) in enumerate(zip(lanes, cis)):
                    dv_l[l][ci], dkend_l[l][ci], dqin_l[l][ci] = dv_c[l], dkend_c[l], dqin_c[l]
                    dgl_ref[d, where[d][0], ci:ci + 1, hh * DH:(hh + 1) * DH] = dgl[l]
                dst = dst * dec + _bdot_tn(dob_c, chunk(q_in))
            where = [_scan_tile(i, d == 1) for d in (0, 1)]
            for l, (d, hh) in enumerate(lanes):
                rows = where[d][1]
                d_ref = d1_ref if d else d0_ref
                d_ref[0, hh, rows, :] = dqg[l].astype(BF16)
                d_ref[1, hh, rows, :] = dkg[l].astype(BF16)
                d_ref[2, hh, rows, :] = jnp.concatenate(dqin_l[l], axis=0).astype(BF16)
                d_ref[3, hh, rows, :] = jnp.concatenate(dkend_l[l], axis=0).astype(BF16)
                dv_sc[d, hh, rows, :] = jnp.concatenate(dv_l[l], axis=0).astype(BF16)
            return dst

        lax.fori_loop(0, NT, bwd_body, zero)
        dv_ref[...] = (dv_sc[0].astype(F32) + dv_sc[1].astype(F32)).astype(BF16)

    quad = pl.BlockSpec((4, GLA_HB, TT, DH), lambda h: (0, h, 0, 0))
    col = pl.BlockSpec((GLA_HB, TT, DH), lambda h: (h, 0, 0))
    chunkv = pl.BlockSpec((2, NT, 8, GLA_HB * DH), lambda h: (0, 0, 0, h))
    outs = (_sds((4, HEADS, TT, DH), BF16), _sds((4, HEADS, TT, DH), BF16), _sds((HEADS, TT, DH), BF16), _sds((2, NT, 8, E), F32),
            _sds((NDEV, D, SH_PWIN), BF16))
    return pl.pallas_call(
        body, name="gla_bwd", grid=(n_steps,),
        in_specs=[quad, quad, col, chunkv, col, pl.BlockSpec((2, TM, TM), lambda h: (0, 0, 0)), HBM_SPEC],
        out_specs=[quad, quad, col, chunkv, HBM_SPEC],
        out_shape=outs,
        scratch_shapes=[pltpu.VMEM((NT * nch, 2 * GLA_HB, DH, DH), F32), pltpu.VMEM((2, GLA_HB, TT, DH), BF16)] + _rider_sems(1),
        compiler_params=pltpu.CompilerParams(dimension_semantics=("arbitrary",), vmem_limit_bytes=VMEM_LIMIT_SCAN),
    )(p0, p1, v_all, dec, do, mask01, gpwin)


TMB = 128


def _gla_post_bwd(g_all, d0, d1, dgl, dv, dz, lb, cum01, gwout, gpgrp):
    nch = TMB // CHUNK

    def body(g_ref, d0_ref, d1_ref, dgl_ref, dv_ref, dz_ref, lb_ref, cum_ref, gwout_r, gpgrp_r, dg_ref, dlb_ref, rwout_o, rpgrp_o,
             ssem, rsem, lsem):
        i = pl.program_id(0)
        _scatter_rider(i, TT // TMB, ("rows", "grp"), (gwout_r, gpgrp_r), (rwout_o, rpgrp_o), ssem, rsem, lsem)

        @pl.when(i == 0)
        def _():
            dlb_ref[...] = jnp.zeros_like(dlb_ref)

        half = i & 1
        qpre = g_ref[:, 3 * E:4 * E]
        dqs_sum = None
        dpre = []
        for d, d_ref in ((0, d0_ref), (1, d1_ref)):
            rev = d == 1
            lbd = lb_ref[d:d + 1, :]
            t = _gla_gates(g_ref[:, d * E:(d + 1) * E], qpre, lbd, cum_ref[d, :TMB, :TMB], rev)
            dqg, dkg, dqin, dkend = [_get_heads(d_ref, (ty,)).astype(F32) for ty in range(4)]
            dqs = dqg * t["e_q"] + dqin * t["e_in"]
            dk = dkg * t["e_k"] + dkend * t["e_end"]
            dkk = dkend * (t["k"] * t["e_end"])
            dg = t["qs"] * dqs - t["k"] * dk
            dkk3 = dkk.reshape(nch, CHUNK, E)
            dgl8 = dgl_ref[d, 0]
            dgl_rows = [jnp.where(half == 0, dgl8[ci:ci + 1, :], dgl8[nch + ci:nch + ci + 1, :]) for ci in range(nch)]
            dgl_b = jnp.concatenate([jnp.broadcast_to(dgl_rows[ci] + jnp.sum(dkk3[ci], axis=0, keepdims=True), (CHUNK, E))
                                     for ci in range(nch)], axis=0)
            pos = lax.broadcasted_iota(jnp.int32, (TMB, E), 0) & (CHUNK - 1)
            dg = dg + jnp.where(pos == (0 if rev else CHUNK - 1), dgl_b, 0.0)
            dlf = _dot01(cum_ref[1 - d, :TMB, :TMB], dg)
            df = dlf / t["f"] - dk
            sig = t["sig"]
            dpre.append((df * (1.0 - lbd) * sig * (1.0 - sig)).astype(BF16))
            dlb_ref[d:d + 1, :] += _colsum(df * (1.0 - sig))
            dqs_sum = dqs if dqs_sum is None else dqs_sum + dqs
            qsig = t["qsig"]
        dqpre = dqs_sum * (DH ** -0.5) * (qsig * (1.0 + qpre * (1.0 - qsig)))
        dg_ref[...] = jnp.concatenate([dpre[0], dpre[1], _get_heads(dv_ref), dqpre.astype(BF16), dz_ref[...]], axis=1)

    quad = pl.BlockSpec((4, HEADS, TMB, DH), lambda i: (0, 0, i, 0))
    tile = pl.BlockSpec((TMB, E), lambda i: (i, 0))
    return pl.pallas_call(
        body, name="gla_post_bwd", grid=(TT // TMB,),
        in_specs=[pl.BlockSpec((TMB, 4 * E), lambda i: (i, 0)), quad, quad,
                  pl.BlockSpec((2, 1, 8, E), lambda i: (0, i // 2, 0, 0)), pl.BlockSpec((HEADS, TMB, DH), lambda i: (0, i, 0)), tile,
                  VMEM_SPEC, VMEM_SPEC, HBM_SPEC, HBM_SPEC],
        out_specs=[pl.BlockSpec((TMB, WIN_COLS), lambda i: (i, 0)), VMEM_SPEC, HBM_SPEC, HBM_SPEC],
        out_shape=(_sds((TT, WIN_COLS), BF16), _sds((2, E), F32), _sds((NDEV, SH_ROWS, D), BF16), _sds((NDEV, 4, SH_GRP, PG), BF16)),
        scratch_shapes=_rider_sems(2),
        compiler_params=pltpu.CompilerParams(dimension_semantics=("arbitrary",), vmem_limit_bytes=VMEM_LIMIT),
    )(g_all, d0, d1, dgl, dv, dz, lb, cum01, gwout, gpgrp)


def _b1_in_bwd(idx1, ctx, x, dx1, dg, nw, msel, win):
    last_s = NDEV - 1

    def body(idx_ref, ctx_ref, x_ref, dx1_ref, dg_ref, nw_ref, m_ref, w_ref, gx_ref, rwin_o, dmx_o, dmc_o, gnw_o,
             hx_sc, dhx_sc, acc, sbuf, pbuf, psend, precv, isend, irecv, sibsem, lsem):
        del idx_ref
        s, i = pl.program_id(0), pl.program_id(1)
        x, y, cc, idx = _mesh_pos()
        shift, scale = m_ref[0, 0:1, :], m_ref[0, 1:2, :]
        sibling = (x, y, 1 - cc)

        def partial(p):
            return pltpu.make_async_remote_copy(src_ref=sbuf.at[0], dst_ref=pbuf.at[p], send_sem=psend.at[p], recv_sem=precv.at[p],
                                                device_id=sibling, device_id_type=MESH)

        def chip_sum(p):
            return pltpu.make_async_remote_copy(src_ref=sbuf.at[1], dst_ref=rwin_o.at[2 + p], send_sem=isend.at[p], recv_sem=irecv.at[p],
                                                device_id=_peer(x, y, cc, 2 * (p + 1)), device_id_type=MESH)

        to_sibling = pltpu.make_async_remote_copy(src_ref=sbuf.at[0], dst_ref=rwin_o.at[1], send_sem=sibsem.at[0], recv_sem=sibsem.at[1],
                                                  device_id=sibling, device_id_type=MESH)
        own = pltpu.make_async_copy(sbuf.at[1], rwin_o.at[0], lsem)

        @pl.when((s == 0) & (i == 0))
        def _():
            for ref in (dmx_o, dmc_o, gnw_o):
                ref[...] = jnp.zeros_like(ref)

        @pl.when(s == 0)
        def _():
            hx, _, _, _ = _modulated(_ctx_or_x(i, ctx_ref, x_ref), nw_ref[...], shift, scale)
            hx_sc[i] = hx.astype(BF16)

        @pl.when(i == 0)
        def _():
            acc[...] = jnp.zeros_like(acc)

        dgb = dg_ref[...]
        hxb = hx_sc[i]
        for lo, hi in ((0, 256), (256, 512), (512, SH_WIN)):
            acc[:, lo:hi] += _dot_ta(hxb, dgb[:, lo:hi])
        part = _dot_tb(dgb, w_ref[...])

        @pl.when(s == 0)
        def _():
            dhx_sc[i] = part

        @pl.when(s > 0)
        def _():
            dhx_sc[i] += part

        for p in (2, 1, 0):
            @pl.when((i == NT - 1) & (s == 2 * (2 - p)))
            def _(p=p):
                if p < 2:
                    partial(p + 1).wait_send()
                sbuf[0] = acc[...].astype(BF16)
                partial(p).start()

            @pl.when((i == NT - 1) & (s == 2 * (2 - p) + 1))
            def _(p=p):
                if p < 2:
                    chip_sum(p + 1).wait_send()
                partial(p).wait_recv()
                sbuf[1] = (acc[...] + pbuf[p].astype(F32)).astype(BF16)
                chip_sum(p).start()

        @pl.when((i == NT - 1) & (s == last_s - 1))
        def _():
            partial(0).wait_send()
            sbuf[0] = acc[...].astype(BF16)
            to_sibling.start()

        @pl.when((i == NT - 1) & (s == last_s))
        def _():
            chip_sum(0).wait_send()
            sbuf[1] = acc[...].astype(BF16)
            own.start()

        @pl.when(s == last_s)
        def _():
            nw = nw_ref[...]
            _, r, xn, a = _modulated(_ctx_or_x(i, ctx_ref, x_ref), nw, shift, scale)
            dhx = dhx_sc[i]
            dsh, dsc = _colsum(dhx), _colsum(dhx * a)
            da = dhx * (1.0 + scale)
            gnw_o[...] += _colsum(da * xn)
            dxn = da * nw
            gx_ref[...] = dx1_ref[...] + r * (dxn - xn * jnp.mean(dxn * xn, axis=-1, keepdims=True))

            @pl.when(i == 0)
            def _():
                dmc_o[0:1, :] += dsh
                dmc_o[1:2, :] += dsc

            @pl.when(i > 0)
            def _():
                dmx_o[0:1, :] += dsh
                dmx_o[1:2, :] += dsc

        @pl.when((i == NT - 1) & (s == last_s))
        def _():
            to_sibling.wait_send()
            to_sibling.wait_recv()
            for p in range(3):
                chip_sum(p).wait_recv()
            own.wait()

    grid_spec = pltpu.PrefetchScalarGridSpec(
        num_scalar_prefetch=1, grid=(NDEV, NT),
        in_specs=[VMEM_SPEC, pl.BlockSpec((TM, D), lambda s, i, ix: (jnp.maximum(i - 1, 0), 0)),
                  pl.BlockSpec((TM, D), lambda s, i, ix: (jnp.maximum(i - 1, 0), 0)),
                  pl.BlockSpec((TM, SH_WIN), lambda s, i, ix: (i, ix[0] ^ (last_s - s))), VMEM_SPEC,
                  pl.BlockSpec((1, 2, D), lambda s, i, ix: (jnp.minimum(i, 1), 0, 0)),
                  pl.BlockSpec((D, SH_WIN), lambda s, i, ix: (0, ix[0] ^ (last_s - s)))],
        out_specs=[pl.BlockSpec((TM, D), lambda s, i, ix: (jnp.where(s == last_s, jnp.maximum(i - 1, 0), 0), 0)),
                   HBM_SPEC, VMEM_SPEC, VMEM_SPEC, VMEM_SPEC],
        scratch_shapes=[pltpu.VMEM((NT, TM, D), BF16), pltpu.VMEM((NT, TM, D), F32), pltpu.VMEM((D, SH_WIN), F32),
                        pltpu.VMEM((2, D, SH_WIN), BF16), pltpu.VMEM((3, D, SH_WIN), BF16),
                        pltpu.SemaphoreType.DMA((3,)), pltpu.SemaphoreType.DMA((3,)), pltpu.SemaphoreType.DMA((3,)),
                        pltpu.SemaphoreType.DMA((3,)), pltpu.SemaphoreType.DMA((2,)), pltpu.SemaphoreType.DMA])
    return pl.pallas_call(
        body, name="b1_in_bwd", grid_spec=grid_spec,
        out_shape=(_sds((T, D), F32), _sds((RS_SLOTS, D, SH_WIN), BF16), _sds((2, D), F32), _sds((2, D), F32), _sds((1, D), F32)),
        compiler_params=pltpu.CompilerParams(dimension_semantics=("arbitrary", "arbitrary"), vmem_limit_bytes=VMEM_LIMIT),
    )(idx1, ctx, x, dx1, dg, nw, msel, win)


def _reduce_small(pd, pv, cg, c_ctx, ada_w0):
    n_arr = 3

    def body(pd_r, pv_r, cg_r, cctx_r, ada_r, gada_o, gadab_o, gcctx_o, pvsum_o, loss_o,
             pd_all, pv_all, dsc_all, dsc_mine, ssem, rsem):
        x, y, cc, idx = _mesh_pos()
        srcs = [pd_r, pv_r, dsc_mine]
        dsts = [pd_all.at[idx], pv_all.at[idx], dsc_all.at[idx]]

        def remote(a, k):
            return pltpu.make_async_remote_copy(src_ref=srcs[a], dst_ref=dsts[a], send_sem=ssem.at[a, k], recv_sem=rsem.at[a, k],
                                                device_id=_peer(x, y, cc, k), device_id_type=MESH)

        first = [remote(a, k) for k in range(1, NDEV) for a in (0, 1)]
        for cp in first:
            cp.start()
        pd_all[idx] = pd_r[...]
        pv_all[idx] = pv_r[...]
        for k in range(1, NDEV):
            remote(0, k).wait_recv()
            remote(1, k).wait_recv()
        mine = [pd_all[s, :, pl.ds(idx, 1), :] for s in range(NDEV)]
        dmc = functools.reduce(lambda u, v: u + v, [m[2] for m in mine])
        rows = _stack_rows([cg_r[i] for i in range(NDEV)] + [cctx_r[...]])
        sc = (rows * _sigmoid(rows)).astype(BF16)
        gada_o[0] = _dot_ta(sc, _stack_rows([m[0] for m in mine] + [dmc]))
        gada_o[1] = _dot_ta(sc, _stack_rows([m[1] for m in mine]))
        dsc_mine[...] = _dot_tb(jnp.broadcast_to(dmc, (8, SH_ADA)), ada_r[...])[0:1, :]
        dsc_all[idx] = dsc_mine[...]
        second = [remote(2, k) for k in range(1, NDEV)]
        for cp in second:
            cp.start()
        tot = [functools.reduce(lambda u, v: u + v, [pd_all[s, l] for s in range(NDEV)]) for l in range(3)]
        gadab_o[0] = tot[0] + tot[2]
        gadab_o[1] = tot[1]
        pvs = functools.reduce(lambda u, v: u + v, [pv_all[s] for s in range(NDEV)])
        pvsum_o[...] = pvs
        loss_o[...] = jnp.broadcast_to(jnp.sum(pvs[:, PV_LOSS:PV_LOSS + D], axis=-1, keepdims=True) * (0.5 / D), (1, 128))
        for k in range(1, NDEV):
            remote(2, k).wait_recv()
        dsc = functools.reduce(lambda u, v: u + v, [dsc_all[s] for s in range(NDEV)])
        cx = cctx_r[...]
        sx = _sigmoid(cx)
        gcctx_o[...] = dsc * (sx * (1.0 + cx * (1.0 - sx)))
        for cp in first + second:
            cp.wait_send()

    outs = (_sds((2, D, SH_ADA), F32), _sds((2, NDEV, SH_ADA), F32), _sds((1, D), F32), _sds((1, PV_LEN), F32), _sds((1, 128), F32))
    return pl.pallas_call(
        body, name="reduce_small", out_shape=outs,
        in_specs=[VMEM_SPEC] * 5, out_specs=[VMEM_SPEC] * 5,
        scratch_shapes=[
            pltpu.VMEM((NDEV, 3, NDEV, SH_ADA), F32), pltpu.VMEM((NDEV, 1, PV_LEN), F32), pltpu.VMEM((NDEV, 1, D), F32),
            pltpu.VMEM((1, D), F32),
            pltpu.SemaphoreType.DMA((n_arr, NDEV)), pltpu.SemaphoreType.DMA((n_arr, NDEV)),
        ],
        compiler_params=pltpu.CompilerParams(vmem_limit_bytes=VMEM_LIMIT),
    )(pd, pv, cg, c_ctx, ada_w0)


PV_NW, PV_GNORM, PV_FINAL, PV_LB, PV_PSCALE, PV_LOSS, PV_LEN = 0, 2 * D, 3 * D, 4 * D, 6 * D, 7 * D, 8 * D


def _adamw(w, g, m, v):
    m = ADAM_B1 * m + (1.0 - ADAM_B1) * g
    v = ADAM_B2 * v + (1.0 - ADAM_B2) * (g * g)
    m_hat = m / (1.0 - ADAM_B1 ** ADAM_STEP)
    v_hat = v / (1.0 - ADAM_B2 ** ADAM_STEP)
    delta = -ADAM_LR * (m_hat / (jnp.sqrt(v_hat) + ADAM_EPS) + ADAM_WD * w)
    return delta, m, v


ADAM_STEPS = 8


def _adam_all(sharded, dense, small, lb_idx, lbv):
    ns, nd, nsm = len(sharded), len(dense), len(small)

    def body(*refs):
        it = iter(refs)
        sh_in = [[next(it) for _ in range(4)] for _ in range(ns)]
        de_in = [[next(it) for _ in range(4)] for _ in range(nd)]
        sm_in = [[next(it) for _ in range(4)] for _ in range(nsm)]
        lb_r = next(it)
        sh_out = [[next(it) for _ in range(4)] for _ in range(ns)]
        de_out = [[next(it) for _ in range(3)] for _ in range(nd)]
        sm_out = [[next(it) for _ in range(4)] for _ in range(nsm)]
        for (p, w, m, v), outs in zip(sh_in, sh_out):
            g = p[0].astype(F32)
            for s in range(1, p.shape[0]):
                g = g + p[s].astype(F32)
            d, mn, vn = _adamw(w[...], g, m[...], v[...])
            outs[0][...], outs[1][...], outs[2][...], outs[3][...] = g, d, mn, vn
        for (g, w, m, v), outs in zip(de_in, de_out):
            d, mn, vn = _adamw(w[...], g[...], m[...], v[...])
            outs[0][...], outs[1][...], outs[2][...] = d, mn, vn

        @pl.when(pl.program_id(0) == 0)
        def _():
            for j, ((g, w, m, v), outs) in enumerate(zip(sm_in, sm_out)):
                gj = g[...]
                if j == lb_idx:
                    gj = gj * lb_r[...] * (1.0 - lb_r[...])
                d, mn, vn = _adamw(w[...], gj, m[...], v[...])
                outs[0][...], outs[1][...], outs[2][...], outs[3][...] = gj, d, mn, vn

    def tile(a):
        return pl.BlockSpec((a.shape[0] // ADAM_STEPS, a.shape[1]), lambda i: (i, 0))

    in_specs, out_specs, out_shape, args = [], [], [], []
    for p, w, m, v in sharded:
        in_specs += [pl.BlockSpec((p.shape[0], p.shape[1] // ADAM_STEPS, p.shape[2]), lambda i: (0, i, 0))] + [tile(w)] * 3
        args += [p, w, m, v]
    for g, w, m, v in dense:
        in_specs += [tile(w)] * 4
        args += [g, w, m, v]
    for g, w, m, v in small:
        in_specs += [VMEM_SPEC] * 4
        args += [g, w, m, v]
    in_specs.append(VMEM_SPEC)
    args.append(lbv)
    for _, w, _, _ in sharded:
        out_specs += [tile(w)] * 4
        out_shape += [_sds(w.shape, F32)] * 4
    for _, w, _, _ in dense:
        out_specs += [tile(w)] * 3
        out_shape += [_sds(w.shape, F32)] * 3
    for _, w, _, _ in small:
        out_specs += [VMEM_SPEC] * 4
        out_shape += [_sds(w.shape, F32)] * 4
    res = pl.pallas_call(body, name="adam_all", grid=(ADAM_STEPS,), in_specs=in_specs, out_specs=out_specs, out_shape=tuple(out_shape),
                         compiler_params=pltpu.CompilerParams(dimension_semantics=("arbitrary",), vmem_limit_bytes=VMEM_LIMIT))(*args)
    it = iter(res)
    return ([tuple(next(it) for _ in range(4)) for _ in range(ns)], [tuple(next(it) for _ in range(3)) for _ in range(nd)],
            [tuple(next(it) for _ in range(4)) for _ in range(nsm)])


def kernel(x, c, ctx, c_ctx, ada_w, ada_b, norm_w, hgrn_w_in, hgrn_lb_logits, hgrn_gnorm_w, hgrn_w_out, pool_w_in, pool_w_grp, pool_scale, pool_w_out, final_norm_w, loss_target, m_c_ctx, m_ada_w, m_ada_b, m_norm_w, m_hgrn_w_in, m_hgrn_lb_logits, m_hgrn_gnorm_w, m_hgrn_w_out, m_pool_w_in, m_pool_w_grp, m_pool_scale, m_pool_w_out, m_final_norm_w, v_c_ctx, v_ada_w, v_ada_b, v_norm_w, v_hgrn_w_in, v_hgrn_lb_logits, v_hgrn_gnorm_w, v_hgrn_w_out, v_pool_w_in, v_pool_w_grp, v_pool_scale, v_pool_w_out, v_final_norm_w):
    idx = 4 * lax.axis_index("x") + 2 * lax.axis_index("y") + lax.axis_index("c")
    cctx2 = c_ctx.reshape(1, D)
    cum01, mask01 = _gla_consts()
    pb, pbt, pinv = _pool_consts()

    idx1 = idx.reshape(1).astype(jnp.int32)
    nw0, nw1 = norm_w[0:1], norm_w[1:2]
    fnw = final_norm_w.reshape(1, D)
    g_all, win, s_wout, s_pwin, s_pgrp, s_pwout, lbl_g, ps_g, cg, mod0, mod1, modc = _f1_gather_matmul(
        idx1, ctx[0], x[0], nw0, hgrn_w_in[0], hgrn_w_out[0], pool_w_in[0], pool_w_grp[0], pool_w_out[0], hgrn_lb_logits[0],
        pool_scale, c, cctx2, ada_w, ada_b)
    lb = jax.nn.sigmoid(jnp.transpose(lbl_g, (1, 0, 2)).reshape(2, E))
    pscale = ps_g.reshape(1, E)
    msel = jnp.stack([modc[:2], mod0[:2]])
    p0, p1, v_all, dec, pwin, pgrp = _gla_prep(g_all, lb, cum01, s_pwin, s_pgrp)
    o, wout, pwout = _gla_fwd(p0, p1, v_all, dec, mask01, s_wout, s_pwout)
    x1 = _f3_out(o, g_all, x[0], mod0[2:3], hgrn_gnorm_w, wout)
    dx1, gpwin, gpgrp, gpwout, dmod1, gnw1, gfw, gps, lossv = _pool_layer(
        x1, loss_target[0], mod1, nw1, fnw, pwin, pgrp, pscale, pwout, pb, pbt, pinv)
    do, dz, gwout, dgate0, ggw, rpwout = _b3_out_bwd(dx1, o, g_all, mod0[2:3], hgrn_gnorm_w, wout, gpwout)
    d0, d1, dv, dgl, rpwin = _gla_bwd(p0, p1, v_all, dec, do, mask01, gpwin)
    dg, dlb, rwout, rpgrp = _gla_post_bwd(g_all, d0, d1, dgl, dv, dz, lb, cum01, gwout, gpgrp)
    grad_x, rwin, dmx, dmc, gnw0 = _b1_in_bwd(idx1, ctx[0], x[0], dx1, dg, nw0, msel, win)

    dmod0 = jnp.concatenate([dmx, dgate0], axis=0)
    dmodc = jnp.concatenate([dmc, jnp.zeros((1, D), F32)], axis=0)
    pd = jnp.stack([dmod0, dmod1, dmodc]).reshape(3, NDEV, SH_ADA)
    pv = jnp.concatenate([gnw0, gnw1, ggw, gfw, dlb.reshape(1, 2 * E), gps, lossv], axis=1)
    g_ada, g_adab, g_cctx, pvsum, loss128 = _reduce_small(pd, pv, cg, cctx2, ada_w[0])

    g2 = (4 * SH_GRP, PG)
    sharded_names = ["hgrn_w_in", "hgrn_w_out", "pool_w_in", "pool_w_grp", "pool_w_out"]
    sharded = [(rwin, hgrn_w_in[0], m_hgrn_w_in[0], v_hgrn_w_in[0]),
               (rwout, hgrn_w_out[0], m_hgrn_w_out[0], v_hgrn_w_out[0]),
               (rpwin, pool_w_in[0], m_pool_w_in[0], v_pool_w_in[0]),
               (rpgrp.reshape((NDEV,) + g2), pool_w_grp[0].reshape(g2), m_pool_w_grp[0].reshape(g2), v_pool_w_grp[0].reshape(g2)),
               (rpwout, pool_w_out[0], m_pool_w_out[0], v_pool_w_out[0])]
    a2 = (2 * D, SH_ADA)
    g_ada2 = g_ada.reshape(a2)
    dense = [(g_ada2, ada_w.reshape(a2), m_ada_w.reshape(a2), v_ada_w.reshape(a2))]
    lb_me = lax.dynamic_slice_in_dim(lb, idx * DH, DH, axis=1)
    small_names = ["c_ctx", "ada_b", "norm_w", "hgrn_lb_logits", "hgrn_gnorm_w", "pool_scale", "final_norm_w"]
    small = [(g_cctx, cctx2, m_c_ctx.reshape(1, D), v_c_ctx.reshape(1, D)),
             (g_adab.reshape(2, 3 * D), ada_b, m_ada_b, v_ada_b),
             (pvsum[:, PV_NW:PV_NW + 2 * D].reshape(2, D), norm_w, m_norm_w, v_norm_w),
             (lax.dynamic_slice_in_dim(pvsum[:, PV_LB:PV_LB + 2 * E].reshape(2, E), idx * DH, DH, axis=1),
              hgrn_lb_logits[0], m_hgrn_lb_logits[0], v_hgrn_lb_logits[0]),
             (pvsum[:, PV_GNORM:PV_GNORM + E], hgrn_gnorm_w, m_hgrn_gnorm_w, v_hgrn_gnorm_w),
             (lax.dynamic_slice_in_dim(pvsum[:, PV_PSCALE:PV_PSCALE + E], idx * DH, DH, axis=1), pool_scale, m_pool_scale, v_pool_scale),
             (pvsum[:, PV_FINAL:PV_FINAL + D], fnw, m_final_norm_w.reshape(1, D), v_final_norm_w.reshape(1, D))]
    r_sharded, r_dense, r_small = _adam_all(sharded, dense, small, 3, lb_me)
    out = dict(zip(sharded_names, r_sharded))
    out["ada_w"] = (g_ada2,) + r_dense[0]
    out.update(zip(small_names, r_small))

    shapes = {"c_ctx": (D,), "ada_w": (2, D, SH_ADA), "ada_b": (2, 3 * D), "norm_w": (2, D), "hgrn_w_in": (1, D, SH_WIN),
              "hgrn_lb_logits": (1, 2, DH), "hgrn_gnorm_w": (1, E), "hgrn_w_out": (1, SH_ROWS, D), "pool_w_in": (1, D, SH_PWIN),
              "pool_w_grp": (1, 4, SH_GRP, PG), "pool_scale": (1, DH), "pool_w_out": (1, SH_ROWS, D), "final_norm_w": (D,)}
    order = ["c_ctx", "ada_w", "ada_b", "norm_w", "hgrn_w_in", "hgrn_lb_logits", "hgrn_gnorm_w", "hgrn_w_out", "pool_w_in",
             "pool_w_grp", "pool_scale", "pool_w_out", "final_norm_w"]
    flat = [out[name][q].reshape(shapes[name]) for q in range(4) for name in order]
    return (loss128[0, 0], grad_x[None], *flat)
```

```python
import functools

import numpy as np
import jax
import jax.numpy as jnp
from jax import lax
from jax.experimental import pallas as pl
from jax.experimental.pallas import tpu as pltpu

F32 = jnp.float32
BF16 = jnp.bfloat16

D = 1024
E = 1024
HEADS = 8
DH = 128
CHUNK = 64
T = 2048
TC = 256
TT = T + TC
TM = 256
NT = TT // TM
NTX = T // TM
NDEV = 8
GRID_W = 64
POOL_WINDOWS = (2, 4, 8, 16)
PG = 256
EPS = 1e-6
WIN_COLS = 5 * E
SH_WIN = WIN_COLS // NDEV
SH_PWIN = 2 * E // NDEV
SH_ROWS = E // NDEV
SH_GRP = PG // NDEV
SH_ADA = 3 * D // NDEV
VMEM_LIMIT = 56 * 1024 * 1024
VMEM_LIMIT_SCAN = 60 * 1024 * 1024

ADAM_LR, ADAM_B1, ADAM_B2, ADAM_EPS, ADAM_WD, ADAM_STEP = 0.001, 0.9, 0.999, 1e-08, 0.01, 10

MESH = pl.DeviceIdType.MESH
VMEM_SPEC = pl.BlockSpec(memory_space=pltpu.VMEM)
HBM_SPEC = pl.BlockSpec(memory_space=pltpu.HBM)
ANY_SPEC = pl.BlockSpec(memory_space=pl.ANY)


def _sds(shape, dtype):
    return jax.ShapeDtypeStruct(shape, dtype)


def _bf(a):
    return a if a.dtype == BF16 else a.astype(BF16)


def _dot(a, b):
    return lax.dot_general(_bf(a), _bf(b), (((1,), (0,)), ((), ())), preferred_element_type=F32)


def _dot_tb(a, b):
    return lax.dot_general(_bf(a), _bf(b), (((1,), (1,)), ((), ())), preferred_element_type=F32)


def _dot_ta(a, b):
    return lax.dot_general(_bf(a), _bf(b), (((0,), (0,)), ((), ())), preferred_element_type=F32)


def _bdot(a, b):
    return lax.dot_general(_bf(a), _bf(b), (((2,), (1,)), ((0,), (0,))), preferred_element_type=F32)


def _bdot_nt(a, b):
    return lax.dot_general(_bf(a), _bf(b), (((2,), (2,)), ((0,), (0,))), preferred_element_type=F32)


def _bdot_tn(a, b):
    return lax.dot_general(_bf(a), _bf(b), (((1,), (1,)), ((0,), (0,))), preferred_element_type=F32)


def _dot01(m01, x):
    hi = x.astype(BF16)
    lo = (x - hi.astype(F32)).astype(BF16)
    return _dot(m01, hi) + _dot(m01, lo)


def _rstd(x):
    return lax.rsqrt(jnp.mean(x * x, axis=-1, keepdims=True) + EPS)


def _sigmoid(x):
    return jax.nn.sigmoid(x)


def _colsum(a):
    return jnp.sum(a, axis=0, keepdims=True)


def _stack_rows(rows):
    n = rows[0].shape[-1]
    rid = lax.broadcasted_iota(jnp.int32, (16, n), 0)
    out = jnp.zeros((16, n), F32)
    for i, r in enumerate(rows):
        out = jnp.where(rid == i, r, out)
    return out


def _head_map(fn, *arrs):
    outs = [fn(*[a[:, h * DH:(h + 1) * DH] for a in arrs]) for h in range(HEADS)]
    return jnp.concatenate(outs, axis=1)


def _gla_consts():
    r = np.arange(TM)[:, None]
    c = np.arange(TM)[None, :]
    same = (r // CHUNK) == (c // CHUNK)
    tril = same & (c <= r)
    triu = same & (c >= r)
    m = np.stack([tril, triu]).astype(np.float32)
    return jnp.asarray(m, BF16), jnp.asarray(m, F32)


def _pool_consts():
    r = np.arange(TM)[:, None]
    c = np.arange(TM)[None, :]
    same = (r // GRID_W) == (c // GRID_W)
    rp, cp = r % GRID_W, c % GRID_W
    bs, inv = [], []
    for w in POOL_WINDOWS:
        lo = np.clip(rp - w // 2, 0, GRID_W)
        hi = np.clip(rp - w // 2 + w, 0, GRID_W)
        bs.append(same & (cp >= lo) & (cp < hi))
        inv.append(1.0 / (hi - lo).astype(np.float32))
    b = np.stack(bs).astype(np.float32)
    bt = np.transpose(b, (0, 2, 1))
    return jnp.asarray(b, BF16), jnp.asarray(bt, BF16), jnp.asarray(np.stack(inv), F32)


def _mesh_pos():
    x, y, c = lax.axis_index("x"), lax.axis_index("y"), lax.axis_index("c")
    return x, y, c, 4 * x + 2 * y + c


def _peer(x, y, c, k):
    return (x ^ ((k >> 2) & 1), y ^ ((k >> 1) & 1), c ^ (k & 1))


def _small_gathers(refs, ssem, rsem):
    lb_r, ps_r, c_r, cctx_r, ada_r, adab_r, lb_o, ps_o, cg_o, mod_o, lb_out, ps_out, cg_out, mod0_o, mod1_o, modc_o = refs
    x, y, cc, idx = _mesh_pos()
    srcs = [lb_r, ps_r, c_r, mod_o.at[idx]]
    mine = [lb_o.at[idx], ps_o.at[idx], cg_o.at[idx], mod_o.at[idx]]

    def remote(a, k):
        return pltpu.make_async_remote_copy(src_ref=srcs[a], dst_ref=mine[a], send_sem=ssem.at[a, k], recv_sem=rsem.at[a, k],
                                            device_id=_peer(x, y, cc, k), device_id_type=MESH)

    first = [remote(a, k) for k in range(1, NDEV) for a in (2, 0, 1)]
    for cp in first:
        cp.start()
    lb_o[idx] = lb_r[...]
    ps_o[idx] = ps_r[...]
    cg_o[idx] = c_r[...]
    for k in range(1, NDEV):
        remote(2, k).wait_recv()
    rows = _stack_rows([cg_o[i] for i in range(NDEV)] + [cctx_r[...]])
    sc = rows * _sigmoid(rows)
    for l in range(2):
        mod_o[idx, l] = _dot(sc, ada_r[l])
    second = [remote(3, k) for k in range(1, NDEV)]
    for cp in second:
        cp.start()
    for k in range(1, NDEV):
        remote(3, k).wait_recv()

    def mod_rows(l, row):
        full = jnp.concatenate([mod_o[s, l, row, :] for s in range(NDEV)], axis=1) + adab_r[l:l + 1, :]
        return [full[:, j * D:(j + 1) * D] for j in range(3)]

    me = pl.ds(idx, 1)
    for out, parts in ((mod0_o, mod_rows(0, me)), (mod1_o, mod_rows(1, me)), (modc_o, mod_rows(0, slice(NDEV, NDEV + 1)))):
        for j in range(3):
            out[j:j + 1, :] = parts[j]
    for cp in first + second:
        cp.wait_send()
    for k in range(1, NDEV):
        for a in (0, 1):
            remote(a, k).wait_recv()
    lb_out[...] = lb_o[...]
    ps_out[...] = ps_o[...]
    cg_out[...] = cg_o[...]


def _gather_order(s):
    if isinstance(s, int):
        return (0, 1, 2, 4, 3, 5, 6, 7)[s]
    return s + (s == 3).astype(jnp.int32) - (s == 4).astype(jnp.int32)


GATHER_ISSUE = (1, 2, 4, 3, 5, 6, 7)
GATHER_ICI = (2, 4, 6)
GATHER_DIRECT = (1,) + GATHER_ICI
GLA_HB = 2
RS_SLOTS = 5


def _shard_of(kind, ref, i):
    if kind == "rows":
        return ref.at[pl.ds(pl.multiple_of(i * SH_ROWS, SH_ROWS), SH_ROWS), :]
    if kind == "major":
        return ref.at[i]
    assert kind == "grp"
    return ref.at[:, pl.ds(pl.multiple_of(i * SH_GRP, SH_GRP), SH_GRP), :]


def _gather_rider(step, n_steps, forward_at, kinds, srcs, outs, ssem, rsem, lsem):
    x, y, cc, idx = _mesh_pos()
    arrays = range(len(kinds))
    mine = [_shard_of(kinds[a], outs[a], idx) for a in arrays]

    def remote(a, k):
        return pltpu.make_async_remote_copy(src_ref=srcs[a], dst_ref=mine[a], send_sem=ssem.at[a, k], recv_sem=rsem.at[a, k],
                                            device_id=_peer(x, y, cc, k), device_id_type=MESH)

    def forward(a, k):
        blk = _shard_of(kinds[a], outs[a], idx ^ k)
        return pltpu.make_async_remote_copy(src_ref=blk, dst_ref=blk, send_sem=ssem.at[a, k ^ 1], recv_sem=rsem.at[a, k ^ 1],
                                            device_id=(x, y, 1 - cc), device_id_type=MESH)

    copies = [remote(a, k) for k in GATHER_DIRECT for a in arrays]
    passed = [forward(a, k) for k in GATHER_ICI for a in arrays]
    local = [pltpu.make_async_copy(srcs[a], mine[a], lsem.at[a]) for a in arrays]

    @pl.when(step == 0)
    def _():
        for cp in copies + local:
            cp.start()

    @pl.when(step == forward_at)
    def _():
        for k in GATHER_ICI:
            for a in arrays:
                remote(a, k).wait_recv()
                forward(a, k).start()

    @pl.when(step == n_steps - 1)
    def _():
        for cp in copies + passed:
            cp.wait_send()
        for a in arrays:
            remote(a, 1).wait_recv()
        for cp in passed:
            cp.wait_recv()
        for cp in local:
            cp.wait()


def _scatter_rider(step, n_steps, kinds, grads, slots, ssem, rsem, lsem):
    x, y, cc, idx = _mesh_pos()
    arrays = range(len(kinds))
    dsts = [slots[a].at[idx] for a in arrays]

    def remote(a, k):
        px, py, pc = _peer(x, y, cc, k)
        return pltpu.make_async_remote_copy(src_ref=_shard_of(kinds[a], grads[a], 4 * px + 2 * py + pc), dst_ref=dsts[a],
                                            send_sem=ssem.at[a, k], recv_sem=rsem.at[a, k], device_id=(px, py, pc), device_id_type=MESH)

    copies = [remote(a, k) for k in GATHER_ISSUE for a in arrays]
    local = [pltpu.make_async_copy(_shard_of(kinds[a], grads[a], idx), dsts[a], lsem.at[a]) for a in arrays]

    @pl.when(step == 0)
    def _():
        for cp in copies + local:
            cp.start()

    @pl.when(step == n_steps - 1)
    def _():
        for cp in copies:
            cp.wait_send()
        for cp in copies:
            cp.wait_recv()
        for cp in local:
            cp.wait()


def _rider_sems(n):
    return [pltpu.SemaphoreType.DMA((n, NDEV)), pltpu.SemaphoreType.DMA((n, NDEV)), pltpu.SemaphoreType.DMA((n,))]


def _scatter_rider2(step, n_steps, add_at, kinds, grads, slots, bufs, sems):
    x, y, cc, idx = _mesh_pos()
    sibling = (x, y, 1 - cc)
    arrays = range(len(kinds))
    psend, precv, isend, irecv, lown, sibsem, lself = sems

    def mine(a, i):
        return _shard_of(kinds[a], grads[a], i)

    def partial(a, p):
        return pltpu.make_async_remote_copy(src_ref=mine(a, idx ^ (2 * (p + 1)) ^ 1), dst_ref=bufs[a][1].at[p], send_sem=psend.at[a, p],
                                            recv_sem=precv.at[a, p], device_id=sibling, device_id_type=MESH)

    def load(a, p):
        return pltpu.make_async_copy(mine(a, idx ^ (2 * (p + 1))), bufs[a][0].at[p], lown.at[a, p])

    def chip_sum(a, p):
        return pltpu.make_async_remote_copy(src_ref=bufs[a][0].at[p], dst_ref=slots[a].at[2 + p], send_sem=isend.at[a, p],
                                            recv_sem=irecv.at[a, p], device_id=_peer(x, y, cc, 2 * (p + 1)), device_id_type=MESH)

    def to_sibling(a):
        return pltpu.make_async_remote_copy(src_ref=mine(a, idx ^ 1), dst_ref=slots[a].at[1], send_sem=sibsem.at[a, 0],
                                            recv_sem=sibsem.at[a, 1], device_id=sibling, device_id_type=MESH)

    def own(a):
        return pltpu.make_async_copy(mine(a, idx), slots[a].at[0], lself.at[a, 0])

    @pl.when(step == 0)
    def _():
        for a in arrays:
            for p in range(3):
                partial(a, p).start()
                load(a, p).start()
            to_sibling(a).start()
            own(a).start()

    @pl.when(step == add_at)
    def _():
        for a in arrays:
            for p in range(3):
                partial(a, p).wait_recv()
                load(a, p).wait()
                bufs[a][0][p] = (bufs[a][0][p].astype(F32) + bufs[a][1][p].astype(F32)).astype(BF16)
                chip_sum(a, p).start()

    @pl.when(step == n_steps - 1)
    def _():
        for a in arrays:
            for p in range(3):
                partial(a, p).wait_send()
                chip_sum(a, p).wait_send()
                chip_sum(a, p).wait_recv()
            to_sibling(a).wait_send()
            to_sibling(a).wait_recv()
            own(a).wait()


def _rider2_scratch(blocks):
    n = len(blocks)
    bufs = [pltpu.VMEM((3,) + tuple(b), BF16) for b in blocks for _ in range(2)]
    return bufs + [pltpu.SemaphoreType.DMA((n, 3)) for _ in range(5)] + [pltpu.SemaphoreType.DMA((n, 2)), pltpu.SemaphoreType.DMA((n, 1))]


def _rider2_split(refs, n):
    refs = list(refs)
    return [tuple(refs[2 * a:2 * a + 2]) for a in range(n)], tuple(refs[2 * n:2 * n + 7])


def _modulated(x, nw, shift, scale):
    r = _rstd(x)
    xn = x * r
    a = xn * nw
    return a * (1.0 + scale) + shift, r, xn, a


def _ctx_or_x(i, ctx_ref, x_ref):
    return jnp.where(i == 0, ctx_ref[...], x_ref[...])


def _f1_gather_matmul(idx1, ctx, x, nw, w_in, w_out, pw_in, pgrp, pw_out, lb_l, pscale, c, c_ctx, ada_w, ada_b):
    def body(idx_ref, ctx_ref, x_ref, nw_ref, win_r, wout_r, pwin_r, pgrp_r, pwout_r, lb_r, ps_r, c_r, cctx_r, ada_r, adab_r,
             g_ref, win_o, s_wout, s_pwin, s_pgrp, s_pwout, lb_o, ps_o, cg_o, mod0_o, mod1_o, modc_o,
             wslot, hx_sc, lb_g, ps_g, cg_g, mod_g, ssem, rsem, osem, sm_ssem, sm_rsem):
        del idx_ref
        s, i = pl.program_id(0), pl.program_id(1)
        x, y, cc, idx = _mesh_pos()
        k = _gather_order(s)
        j = idx ^ k

        def remote(kk):
            return pltpu.make_async_remote_copy(src_ref=wslot.at[idx], dst_ref=wslot.at[idx], send_sem=ssem.at[kk], recv_sem=rsem.at[kk],
                                                device_id=_peer(x, y, cc, kk), device_id_type=MESH)

        def forward(kk):
            jj = idx ^ kk
            return pltpu.make_async_remote_copy(src_ref=wslot.at[jj], dst_ref=wslot.at[jj], send_sem=ssem.at[kk ^ 1],
                                                recv_sem=rsem.at[kk ^ 1], device_id=(x, y, 1 - cc), device_id_type=MESH)

        def to_hbm(jj, kk):
            return pltpu.make_async_copy(wslot.at[jj], win_o.at[:, pl.ds(pl.multiple_of(jj * SH_WIN, 128), SH_WIN)], osem.at[kk])

        @pl.when((s == 0) & (i == 0))
        def _():
            _small_gathers((lb_r, ps_r, c_r, cctx_r, ada_r, adab_r, lb_g, ps_g, cg_g, mod_g, lb_o, ps_o, cg_o, mod0_o, mod1_o, modc_o),
                           sm_ssem, sm_rsem)
            wslot[idx] = win_r[...].astype(BF16)
            for kk in GATHER_DIRECT:
                remote(kk).start()
            s_wout[...] = wout_r[...].astype(BF16)
            s_pwin[...] = pwin_r[...].astype(BF16)
            s_pgrp[...] = pgrp_r[...].astype(BF16)
            s_pwout[...] = pwout_r[...].astype(BF16)

        @pl.when(s == 0)
        def _():
            shift = jnp.where(i == 0, modc_o[0:1, :], mod0_o[0:1, :])
            scale = jnp.where(i == 0, modc_o[1:2, :], mod0_o[1:2, :])
            hx, _, _, _ = _modulated(_ctx_or_x(i, ctx_ref, x_ref), nw_ref[...], shift, scale)
            hx_sc[i] = hx.astype(BF16)

        @pl.when((s > 0) & (i == 0))
        def _():
            remote(k).wait_recv()

            @pl.when((k & 1) == 0)
            def _():
                forward(k).start()

        @pl.when(i == 0)
        def _():
            to_hbm(j, k).start()

        g_ref[...] = jnp.dot(hx_sc[i], wslot[j], preferred_element_type=F32)

        @pl.when((s == NDEV - 1) & (i == NT - 1))
        def _():
            for kk in GATHER_DIRECT:
                remote(kk).wait_send()
            for kk in GATHER_ICI:
                forward(kk).wait_send()
            for kk in range(NDEV):
                to_hbm(idx ^ kk, kk).wait()

    grid_spec = pltpu.PrefetchScalarGridSpec(
        num_scalar_prefetch=1, grid=(NDEV, NT),
        in_specs=[VMEM_SPEC, pl.BlockSpec((TM, D), lambda s, i, ix: (jnp.maximum(i - 1, 0), 0))] + [VMEM_SPEC] * 12,
        out_specs=[pl.BlockSpec((TM, SH_WIN), lambda s, i, ix: (i, ix[0] ^ _gather_order(s))), HBM_SPEC] + [VMEM_SPEC] * 10,
        scratch_shapes=[pltpu.VMEM((NDEV, D, SH_WIN), BF16), pltpu.VMEM((NT, TM, D), BF16),
                        pltpu.VMEM((NDEV, 2, DH), F32), pltpu.VMEM((NDEV, 1, DH), F32), pltpu.VMEM((NDEV, 1, D), F32),
                        pltpu.VMEM((NDEV, 2, 16, SH_ADA), F32),
                        pltpu.SemaphoreType.DMA((NDEV,)), pltpu.SemaphoreType.DMA((NDEV,)), pltpu.SemaphoreType.DMA((NDEV,)),
                        pltpu.SemaphoreType.DMA((4, NDEV)), pltpu.SemaphoreType.DMA((4, NDEV))])
    outs = (_sds((TT, WIN_COLS), F32), _sds((D, WIN_COLS), BF16),
            _sds((SH_ROWS, D), BF16), _sds((D, SH_PWIN), BF16), _sds((4, SH_GRP, PG), BF16), _sds((SH_ROWS, D), BF16),
            _sds((NDEV, 2, DH), F32), _sds((NDEV, 1, DH), F32), _sds((NDEV, 1, D), F32),
            _sds((3, D), F32), _sds((3, D), F32), _sds((3, D), F32))
    return pl.pallas_call(
        body, name="f1_gather_matmul", grid_spec=grid_spec, out_shape=outs,
        compiler_params=pltpu.CompilerParams(dimension_semantics=("arbitrary", "arbitrary"), vmem_limit_bytes=VMEM_LIMIT),
    )(idx1, ctx, x, nw, w_in, w_out, pw_in, pgrp, pw_out, lb_l, pscale, c, c_ctx, ada_w, ada_b)


def _gla_gates(pre, qpre, lbd, cum, rev):
    rows, n = pre.shape
    nch = rows // CHUNK
    sig = _sigmoid(pre)
    f = lbd + (1.0 - lbd) * sig
    k = 1.0 - f
    g = _dot01(cum, jnp.log(f))
    g3 = g.reshape(nch, CHUNK, n)
    last = 0 if rev else CHUNK - 1
    mid = CHUNK // 2 if rev else CHUNK // 2 - 1
    gl1, gm1 = g3[:, last:last + 1, :], g3[:, mid:mid + 1, :]

    def bc(a):
        return jnp.broadcast_to(a, g3.shape).reshape(rows, n)

    gm = bc(gm1)
    e_q, e_k = jnp.exp(g - gm), jnp.exp(gm - g)
    e_in, e_end = e_q * bc(jnp.exp(gm1)), e_k * bc(jnp.exp(gl1 - gm1))
    qsig = _sigmoid(qpre)
    qs = qpre * qsig * (DH ** -0.5)
    return dict(sig=sig, f=f, k=k, qsig=qsig, qs=qs, e_q=e_q, e_k=e_k, e_in=e_in, e_end=e_end,
                decay=[jnp.exp(g3[ci, last:last + 1, :]) for ci in range(nch)])


def _put_heads(ref, lead, arr):
    for h in range(HEADS):
        ref[lead + (h,)] = arr[:, h * DH:(h + 1) * DH]


def _get_heads(ref, lead=()):
    return jnp.concatenate([ref[lead + (h,)] for h in range(HEADS)], axis=1)


def _gla_prep(g_all, lb, cum01, s_pwin, s_pgrp):
    def body(g_ref, lb_ref, cum_ref, spwin_r, spgrp_r, p0_ref, p1_ref, v_ref, dec_ref, pwin_o, pgrp_o, ssem, rsem, lsem):
        _gather_rider(pl.program_id(0), NT, NT // 2 + 1, ("major", "grp"), (spwin_r, spgrp_r), (pwin_o, pgrp_o), ssem, rsem, lsem)
        qpre = g_ref[:, 3 * E:4 * E]
        _put_heads(v_ref, (), g_ref[:, 2 * E:3 * E].astype(BF16))
        dec_ref[...] = jnp.zeros_like(dec_ref)
        for d, p_ref in ((0, p0_ref), (1, p1_ref)):
            t = _gla_gates(g_ref[:, d * E:(d + 1) * E], qpre, lb_ref[d:d + 1, :], cum_ref[d], d == 1)
            _put_heads(p_ref, (0,), (t["qs"] * t["e_q"]).astype(BF16))
            _put_heads(p_ref, (1,), (t["k"] * t["e_k"]).astype(BF16))
            _put_heads(p_ref, (2,), (t["qs"] * t["e_in"]).astype(BF16))
            _put_heads(p_ref, (3,), (t["k"] * t["e_end"]).astype(BF16))
            for ci in range(TM // CHUNK):
                dec_ref[d, 0, ci:ci + 1, :] = t["decay"][ci]

    quad = pl.BlockSpec((4, HEADS, TM, DH), lambda i: (0, 0, i, 0))
    return pl.pallas_call(
        body, name="gla_prep", grid=(NT,),
        in_specs=[pl.BlockSpec((TM, 4 * E), lambda i: (i, 0)), VMEM_SPEC, VMEM_SPEC, HBM_SPEC, HBM_SPEC],
        out_specs=[quad, quad, pl.BlockSpec((HEADS, TM, DH), lambda i: (0, i, 0)), pl.BlockSpec((2, 1, 8, E), lambda i: (0, i, 0, 0)),
                   HBM_SPEC, HBM_SPEC],
        out_shape=(_sds((4, HEADS, TT, DH), BF16), _sds((4, HEADS, TT, DH), BF16), _sds((HEADS, TT, DH), BF16), _sds((2, NT, 8, E), F32),
                   _sds((NDEV, D, SH_PWIN), BF16), _sds((4, PG, PG), BF16)),
        scratch_shapes=_rider_sems(2),
        compiler_params=pltpu.CompilerParams(dimension_semantics=("arbitrary",), vmem_limit_bytes=VMEM_LIMIT),
    )(g_all, lb, cum01, s_pwin, s_pgrp)


def _scan_tile(i, rev):
    t = jnp.where(i == 0, 0, NT - i) if rev else i
    return t, pl.ds(pl.multiple_of(t * TM, TM), TM)


def _chunk_order(rev):
    n = TM // CHUNK
    return tuple(range(n - 1, -1, -1)) if rev else tuple(range(n))


def _gla_fwd(p0, p1, v_all, dec, mask01, s_wout, s_pwout):
    n_steps = HEADS // GLA_HB

    def body(p0_ref, p1_ref, v_ref, dec_ref, msk_ref, swout_r, spwout_r, o_ref, wout_o, pwout_o, ob_sc, ssem, rsem, lsem):
        _gather_rider(pl.program_id(0), n_steps, n_steps // 2, ("rows", "rows"), (swout_r, spwout_r), (wout_o, pwout_o), ssem, rsem, lsem)

        lanes = [(d, hh) for d in (0, 1) for hh in range(GLA_HB)]
        nch = TM // CHUNK

        def tile_body(i, st):
            where = [_scan_tile(i, d == 1) for d in (0, 1)]

            def stacked(fn):
                return jnp.stack([fn(d, hh, where[d][1]) for d, hh in lanes])

            qg, kg, q_in, kend = [stacked(lambda d, hh, rows, ty=ty: (p1_ref if d else p0_ref)[ty, hh, rows, :]) for ty in range(4)]
            v = stacked(lambda d, hh, rows: v_ref[hh, rows, :])
            a = _bdot_nt(qg, kg) * jnp.stack([msk_ref[d] for d, _ in lanes])
            intra = _bdot(a, v)
            outs = [[None] * nch for _ in lanes]
            for n in range(nch):
                cis = [nch - 1 - n if d else n for d, _ in lanes]

                def chunk(arr):
                    return jnp.stack([arr[l, ci * CHUNK:(ci + 1) * CHUNK] for l, ci in enumerate(cis)])

                dec = jnp.stack([dec_ref[d, where[d][0], ci:ci + 1, hh * DH:(hh + 1) * DH] for (d, hh), ci in zip(lanes, cis)])
                inter = _bdot_nt(chunk(q_in), st)
                for l, ci in enumerate(cis):
                    outs[l][ci] = inter[l] + intra[l, ci * CHUNK:(ci + 1) * CHUNK]
                st = st * dec + _bdot_tn(chunk(v), chunk(kend))
            for l, (d, hh) in enumerate(lanes):
                (ob_sc if d else o_ref)[hh, where[d][1], :] = jnp.concatenate(outs[l], axis=0)
            return st

        lax.fori_loop(0, NT, tile_body, jnp.zeros((len(lanes), DH, DH), F32))
        o_ref[...] += ob_sc[...]

    quad = pl.BlockSpec((4, GLA_HB, TT, DH), lambda h: (0, h, 0, 0))
    head = pl.BlockSpec((GLA_HB, TT, DH), lambda h: (h, 0, 0))
    return pl.pallas_call(
        body, name="gla_fwd", grid=(n_steps,),
        in_specs=[quad, quad, head, pl.BlockSpec((2, NT, 8, GLA_HB * DH), lambda h: (0, 0, 0, h)),
                  pl.BlockSpec((2, TM, TM), lambda h: (0, 0, 0)), HBM_SPEC, HBM_SPEC],
        out_specs=[head, HBM_SPEC, HBM_SPEC],
        out_shape=(_sds((HEADS, TT, DH), F32), _sds((E, D), BF16), _sds((E, D), BF16)),
        scratch_shapes=[pltpu.VMEM((GLA_HB, TT, DH), F32)] + _rider_sems(2),
        compiler_params=pltpu.CompilerParams(dimension_semantics=("arbitrary",), vmem_limit_bytes=VMEM_LIMIT),
    )(p0, p1, v_all, dec, mask01, s_wout, s_pwout)


def _gated_norm(o, z, gw):
    r = _head_map(lambda oh: jnp.broadcast_to(_rstd(oh), oh.shape), o)
    on = o * r
    zs = _sigmoid(z)
    sz = z * zs
    return on * gw * sz, r, on, zs, sz


def _f3_out(o, g_all, x, gate, gw, wout):
    def body(o_ref, z_ref, x_ref, gate_ref, gw_ref, w_ref, x1_ref):
        og, _, _, _, _ = _gated_norm(_get_heads(o_ref), z_ref[...], gw_ref[...])
        x1_ref[...] = x_ref[...] + gate_ref[...] * _dot(og, w_ref[...])

    return pl.pallas_call(
        body, name="f3_out", grid=(NTX,),
        in_specs=[pl.BlockSpec((HEADS, TM, DH), lambda i: (0, i + 1, 0)), pl.BlockSpec((TM, E), lambda i: (i + 1, 4)),
                  pl.BlockSpec((TM, D), lambda i: (i, 0)), pl.BlockSpec((1, D), lambda i: (0, 0)),
                  pl.BlockSpec((1, E), lambda i: (0, 0)), pl.BlockSpec((E, D), lambda i: (0, 0))],
        out_specs=pl.BlockSpec((TM, D), lambda i: (i, 0)),
        out_shape=_sds((T, D), F32),
        compiler_params=pltpu.CompilerParams(dimension_semantics=("arbitrary",)),
    )(o, g_all, x, gate, gw, wout)


def _pool_layer(x1, tgt, mod1, nw1, fnw, pwin, pgrp, pscale, pwout, pb, pbt, pinv):
    def body(x_ref, t_ref, m_ref, nw_ref, fw_ref, pwin_ref, pgrp_ref, ps_ref, pwout_ref, pb_ref, pbt_ref, pinv_ref,
             dx_ref, gpwin_o, gpgrp_o, gpwout_o, dmod_o, gnw_o, gfw_o, gps_o, loss_o,
             a_pwin, a_pgrp, a_pwout):
        i = pl.program_id(0)

        @pl.when(i == 0)
        def _():
            for ref in (a_pwin, a_pgrp, a_pwout, dmod_o, gnw_o, gfw_o, gps_o, loss_o):
                ref[...] = jnp.zeros_like(ref)

        shift, scale, gate = m_ref[0:1, :], m_ref[1:2, :], m_ref[2:3, :]
        nw, fw, ps = nw_ref[...], fw_ref[...], ps_ref[...]
        x1 = x_ref[...]
        hx, r1, xn, a = _modulated(x1, nw, shift, scale)
        hxb = hx.astype(BF16)
        uz = jnp.concatenate([_dot(hxb, pwin_ref[j]) for j in range(NDEV)], axis=1)
        u, z = uz[:, :E], uz[:, E:]
        pooled, ys = [], []
        for g in range(4):
            ug = u[:, g * PG:(g + 1) * PG]
            pg = _dot01(pb_ref[g], ug) * pinv_ref[g] - ug
            pooled.append(pg.astype(BF16))
            ys.append(_dot(pooled[g], pgrp_ref[g]))
        ycat = jnp.concatenate(ys, axis=1)
        y = ycat * ps
        zs = _sigmoid(z)
        sz = z * zs
        p = (y * sz).astype(BF16)
        out = _dot(p, pwout_ref[...])
        x2 = x1 + gate * out
        r2 = _rstd(x2)
        xn2 = x2 * r2
        diff = xn2 * fw - t_ref[...]
        loss_o[...] += _colsum(diff * diff)
        dyf = diff * (1.0 / D)
        gfw_o[...] += _colsum(dyf * xn2)
        dxn2 = dyf * fw
        dx2 = r2 * (dxn2 - xn2 * jnp.mean(dxn2 * xn2, axis=-1, keepdims=True))
        dgate = _colsum(dx2 * out)
        dout = (dx2 * gate).astype(BF16)
        for j in range(4):
            cs = slice(j * PG, (j + 1) * PG)
            a_pwout[:, cs] += _dot_ta(p, dout[:, cs])
        dp = _dot_tb(dout, pwout_ref[...])
        dy = dp * sz
        dz = dp * y * (zs * (1.0 + z * (1.0 - zs)))
        gps_o[...] += _colsum(dy * ycat)
        dycat = dy * ps
        dus = []
        for g in range(4):
            dyg = dycat[:, g * PG:(g + 1) * PG].astype(BF16)
            a_pgrp[g] += _dot_ta(pooled[g], dyg)
            dpg = _dot_tb(dyg, pgrp_ref[g])
            dus.append(_dot01(pbt_ref[g], dpg * pinv_ref[g]) - dpg)
        duz = jnp.concatenate(dus + [dz], axis=1).astype(BF16)
        dhx = None
        for j in range(NDEV):
            dj = duz[:, j * SH_PWIN:(j + 1) * SH_PWIN]
            a_pwin[j] += _dot_ta(hxb, dj)
            part = _dot_tb(dj, pwin_ref[j])
            dhx = part if dhx is None else dhx + part
        dmod_o[0:1, :] += _colsum(dhx)
        dmod_o[1:2, :] += _colsum(dhx * a)
        dmod_o[2:3, :] += dgate
        da = dhx * (1.0 + scale)
        gnw_o[...] += _colsum(da * xn)
        dxn = da * nw
        dx_ref[...] = dx2 + r1 * (dxn - xn * jnp.mean(dxn * xn, axis=-1, keepdims=True))

        @pl.when(i == NTX - 1)
        def _():
            gpwin_o[...] = a_pwin[...].astype(BF16)
            gpgrp_o[...] = a_pgrp[...].astype(BF16)
            gpwout_o[...] = a_pwout[...].astype(BF16)

    tile = pl.BlockSpec((TM, D), lambda i: (i, 0))
    outs = (_sds((T, D), F32), _sds((NDEV, D, SH_PWIN), BF16), _sds((4, PG, PG), BF16), _sds((E, D), BF16),
            _sds((3, D), F32), _sds((1, D), F32), _sds((1, D), F32), _sds((1, E), F32), _sds((1, D), F32))
    return pl.pallas_call(
        body, name="pool_layer", grid=(NTX,),
        in_specs=[tile, tile] + [VMEM_SPEC] * 10,
        out_specs=[tile] + [VMEM_SPEC] * 8,
        out_shape=outs,
        scratch_shapes=[pltpu.VMEM((NDEV, D, SH_PWIN), F32), pltpu.VMEM((4, PG, PG), F32), pltpu.VMEM((E, D), F32)],
        compiler_params=pltpu.CompilerParams(dimension_semantics=("arbitrary",), vmem_limit_bytes=VMEM_LIMIT),
    )(x1, tgt, mod1, nw1, fnw, pwin, pgrp, pscale, pwout, pb, pbt, pinv)


def _b3_out_bwd(dx1, o, g_all, gate, gw, wout, gpwout):
    def body(dx_ref, o_ref, z_ref, gate_ref, gw_ref, w_ref, gpwout_r, do_ref, dz_ref, gw_o, dgate_o, ggw_o, rpwout_o,
             acc, *rider):
        i = pl.program_id(0)
        bufs, sems = _rider2_split(rider, 1)
        _scatter_rider2(i, NT, 2, ("rows",), (gpwout_r,), (rpwout_o,), bufs, sems)

        @pl.when(i == 0)
        def _():
            acc[...] = jnp.zeros_like(acc)
            dgate_o[...] = jnp.zeros_like(dgate_o)
            ggw_o[...] = jnp.zeros_like(ggw_o)
            do_ref[...] = jnp.zeros_like(do_ref)
            dz_ref[...] = jnp.zeros_like(dz_ref)

        @pl.when(i > 0)
        def _():
            gw = gw_ref[...]
            z = z_ref[...]
            og, r, on, zs, sz = _gated_norm(_get_heads(o_ref), z, gw)
            ogb = og.astype(BF16)
            dx = dx_ref[...]
            dgate_o[...] += _colsum(dx * _dot(ogb, w_ref[...]))
            dy = (dx * gate_ref[...]).astype(BF16)
            for j in range(4):
                cs = slice(j * PG, (j + 1) * PG)
                acc[:, cs] += _dot_ta(ogb, dy[:, cs])
            dog = _dot_tb(dy, w_ref[...])
            dz_ref[...] = (dog * (on * gw) * (zs * (1.0 + z * (1.0 - zs)))).astype(BF16)
            dong = dog * sz
            ggw_o[...] += _colsum(dong * on)
            don = dong * gw
            do = _head_map(lambda dh, nh, rh: rh * (dh - nh * jnp.mean(dh * nh, axis=-1, keepdims=True)), don, on, r)
            _put_heads(do_ref, (), do.astype(BF16))

        @pl.when(i == NT - 1)
        def _():
            gw_o[...] = acc[...].astype(BF16)

    prev = lambda i: (jnp.maximum(i - 1, 0), 0)
    heads = pl.BlockSpec((HEADS, TM, DH), lambda i: (0, i, 0))
    return pl.pallas_call(
        body, name="b3_out_bwd", grid=(NT,),
        in_specs=[pl.BlockSpec((TM, D), prev), heads, pl.BlockSpec((TM, E), lambda i: (i, 4)),
                  VMEM_SPEC, VMEM_SPEC, VMEM_SPEC, HBM_SPEC],
        out_specs=[heads, pl.BlockSpec((TM, E), lambda i: (i, 0)), VMEM_SPEC, VMEM_SPEC, VMEM_SPEC, HBM_SPEC],
        out_shape=(_sds((HEADS, TT, DH), BF16), _sds((TT, E), BF16), _sds((E, D), BF16), _sds((1, D), F32), _sds((1, E), F32),
                   _sds((RS_SLOTS, SH_ROWS, D), BF16)),
        scratch_shapes=[pltpu.VMEM((E, D), F32)] + _rider2_scratch([(SH_ROWS, D)]),
        compiler_params=pltpu.CompilerParams(dimension_semantics=("arbitrary",), vmem_limit_bytes=VMEM_LIMIT),
    )(dx1, o, g_all, gate, gw, wout, gpwout)


def _gla_bwd(p0, p1, v_all, dec, do, mask01, gpgrp):
    nch = TM // CHUNK
    n_steps = HEADS // GLA_HB

    def body(p0_ref, p1_ref, v_ref, dec_ref, do_ref, msk_ref, gpgrp_r, d0_ref, d1_ref, dv_ref, dgl_ref, rpgrp_o,
             ss_sc, dv_sc, ssem, rsem, lsem):
        _scatter_rider(pl.program_id(0), n_steps, ("grp",), (gpgrp_r,), (rpgrp_o,), ssem, rsem, lsem)

        lanes = [(d, hh) for d in (0, 1) for hh in range(GLA_HB)]
        zero = jnp.zeros((len(lanes), DH, DH), F32)
        dgl_ref[...] = jnp.zeros_like(dgl_ref)

        def p_of(d):
            return p1_ref if d else p0_ref

        def scan_step(i, n):
            where = [_scan_tile(i, d == 1) for d in (0, 1)]
            cis = [nch - 1 - n if d else n for d, _ in lanes]
            dec = jnp.stack([dec_ref[d, where[d][0], ci:ci + 1, hh * DH:(hh + 1) * DH] for (d, hh), ci in zip(lanes, cis)])

            def chunk(arr):
                return jnp.stack([arr[l, ci * CHUNK:(ci + 1) * CHUNK] for l, ci in enumerate(cis)])

            return where, cis, dec, chunk

        def stacked(i, fn):
            where = [_scan_tile(i, d == 1) for d in (0, 1)]
            return jnp.stack([fn(d, hh, where[d][1]) for d, hh in lanes])

        def fwd_body(i, st):
            v = stacked(i, lambda d, hh, rows: v_ref[hh, rows, :])
            kend = stacked(i, lambda d, hh, rows: p_of(d)[3, hh, rows, :])
            for n in range(nch):
                _, _, dec, chunk = scan_step(i, n)
                ss_sc[i * nch + n] = st
                st = st * dec + _bdot_tn(chunk(v), chunk(kend))
            return st

        lax.fori_loop(0, NT, fwd_body, zero)

        def bwd_body(ii, dst):
            i = NT - 1 - ii
            qg, kg, q_in, kend = [stacked(i, lambda d, hh, rows, ty=ty: p_of(d)[ty, hh, rows, :]) for ty in range(4)]
            v = stacked(i, lambda d, hh, rows: v_ref[hh, rows, :])
            dob = stacked(i, lambda d, hh, rows: do_ref[hh, rows, :])
            msk = jnp.stack([msk_ref[d] for d, _ in lanes])
            a = (_bdot_nt(qg, kg) * msk).astype(BF16)
            da = (_bdot_nt(dob, v) * msk).astype(BF16)
            dqg = _bdot(da, kg)
            dkg = _bdot_tn(da, qg)
            dv_intra = _bdot_tn(a, dob)
            dv_l, dkend_l, dqin_l = ([[None] * nch for _ in lanes] for _ in range(3))
            for n in range(nch - 1, -1, -1):
                where, cis, dec, chunk = scan_step(i, n)
                s_c = ss_sc[i * nch + n]
                dstb = dst.astype(BF16)
                kend_c, v_c, dob_c = chunk(kend), chunk(v), chunk(dob)
                dv_c = chunk(dv_intra) + _bdot_nt(kend_c, dstb)
                dkend_c = _bdot(v_c, dstb)
                dqin_c = _bdot(dob_c, s_c)
                dgl = jnp.sum(s_c * dst, axis=1, keepdims=True) * dec
                for l, ((d, hh), ci) in enumerate(zip(lanes, cis)):
                    dv_l[l][ci], dkend_l[l][ci], dqin_l[l][ci] = dv_c[l], dkend_c[l], dqin_c[l]
                    dgl_ref[d, where[d][0], ci:ci + 1, hh * DH:(hh + 1) * DH] = dgl[l]
                dst = dst * dec + _bdot_tn(dob_c, chunk(q_in))
            where = [_scan_tile(i, d == 1) for d in (0, 1)]
            for l, (d, hh) in enumerate(lanes):
                rows = where[d][1]
                d_ref = d1_ref if d else d0_ref
                d_ref[0, hh, rows, :] = dqg[l].astype(BF16)
                d_ref[1, hh, rows, :] = dkg[l].astype(BF16)
                d_ref[2, hh, rows, :] = jnp.concatenate(dqin_l[l], axis=0).astype(BF16)
                d_ref[3, hh, rows, :] = jnp.concatenate(dkend_l[l], axis=0).astype(BF16)
                dv_sc[d, hh, rows, :] = jnp.concatenate(dv_l[l], axis=0).astype(BF16)
            return dst

        lax.fori_loop(0, NT, bwd_body, zero)
        dv_ref[...] = (dv_sc[0].astype(F32) + dv_sc[1].astype(F32)).astype(BF16)

    quad = pl.BlockSpec((4, GLA_HB, TT, DH), lambda h: (0, h, 0, 0))
    col = pl.BlockSpec((GLA_HB, TT, DH), lambda h: (h, 0, 0))
    chunkv = pl.BlockSpec((2, NT, 8, GLA_HB * DH), lambda h: (0, 0, 0, h))
    outs = (_sds((4, HEADS, TT, DH), BF16), _sds((4, HEADS, TT, DH), BF16), _sds((HEADS, TT, DH), BF16), _sds((2, NT, 8, E), F32),
            _sds((NDEV, 4, SH_GRP, PG), BF16))
    return pl.pallas_call(
        body, name="gla_bwd", grid=(n_steps,),
        in_specs=[quad, quad, col, chunkv, col, pl.BlockSpec((2, TM, TM), lambda h: (0, 0, 0)), HBM_SPEC],
        out_specs=[quad, quad, col, chunkv, HBM_SPEC],
        out_shape=outs,
        scratch_shapes=[pltpu.VMEM((NT * nch, 2 * GLA_HB, DH, DH), F32), pltpu.VMEM((2, GLA_HB, TT, DH), BF16)] + _rider_sems(1),
        compiler_params=pltpu.CompilerParams(dimension_semantics=("arbitrary",), vmem_limit_bytes=VMEM_LIMIT_SCAN),
    )(p0, p1, v_all, dec, do, mask01, gpgrp)


TMB = 128


def _gla_post_bwd(g_all, d0, d1, dgl, dv, dz, lb, cum01, gwout, gpwin):
    nch = TMB // CHUNK

    def body(g_ref, d0_ref, d1_ref, dgl_ref, dv_ref, dz_ref, lb_ref, cum_ref, gwout_r, gpwin_r, dg_ref, dlb_ref, rwout_o, rpwin_o,
             *rider):
        i = pl.program_id(0)
        bufs, sems = _rider2_split(rider, 2)
        _scatter_rider2(i, TT // TMB, 4, ("major", "rows"), (gpwin_r, gwout_r), (rpwin_o, rwout_o), bufs, sems)

        @pl.when(i == 0)
        def _():
            dlb_ref[...] = jnp.zeros_like(dlb_ref)

        half = i & 1
        qpre = g_ref[:, 3 * E:4 * E]
        dqs_sum = None
        dpre = []
        for d, d_ref in ((0, d0_ref), (1, d1_ref)):
            rev = d == 1
            lbd = lb_ref[d:d + 1, :]
            t = _gla_gates(g_ref[:, d * E:(d + 1) * E], qpre, lbd, cum_ref[d, :TMB, :TMB], rev)
            dqg, dkg, dqin, dkend = [_get_heads(d_ref, (ty,)).astype(F32) for ty in range(4)]
            dqs = dqg * t["e_q"] + dqin * t["e_in"]
            dk = dkg * t["e_k"] + dkend * t["e_end"]
            dkk = dkend * (t["k"] * t["e_end"])
            dg = t["qs"] * dqs - t["k"] * dk
            dkk3 = dkk.reshape(nch, CHUNK, E)
            dgl8 = dgl_ref[d, 0]
            dgl_rows = [jnp.where(half == 0, dgl8[ci:ci + 1, :], dgl8[nch + ci:nch + ci + 1, :]) for ci in range(nch)]
            dgl_b = jnp.concatenate([jnp.broadcast_to(dgl_rows[ci] + jnp.sum(dkk3[ci], axis=0, keepdims=True), (CHUNK, E))
                                     for ci in range(nch)], axis=0)
            pos = lax.broadcasted_iota(jnp.int32, (TMB, E), 0) & (CHUNK - 1)
            dg = dg + jnp.where(pos == (0 if rev else CHUNK - 1), dgl_b, 0.0)
            dlf = _dot01(cum_ref[1 - d, :TMB, :TMB], dg)
            df = dlf / t["f"] - dk
            sig = t["sig"]
            dpre.append((df * (1.0 - lbd) * sig * (1.0 - sig)).astype(BF16))
            dlb_ref[d:d + 1, :] += _colsum(df * (1.0 - sig))
            dqs_sum = dqs if dqs_sum is None else dqs_sum + dqs
            qsig = t["qsig"]
        dqpre = dqs_sum * (DH ** -0.5) * (qsig * (1.0 + qpre * (1.0 - qsig)))
        dg_ref[...] = jnp.concatenate([dpre[0], dpre[1], _get_heads(dv_ref), dqpre.astype(BF16), dz_ref[...]], axis=1)

    quad = pl.BlockSpec((4, HEADS, TMB, DH), lambda i: (0, 0, i, 0))
    tile = pl.BlockSpec((TMB, E), lambda i: (i, 0))
    return pl.pallas_call(
        body, name="gla_post_bwd", grid=(TT // TMB,),
        in_specs=[pl.BlockSpec((TMB, 4 * E), lambda i: (i, 0)), quad, quad,
                  pl.BlockSpec((2, 1, 8, E), lambda i: (0, i // 2, 0, 0)), pl.BlockSpec((HEADS, TMB, DH), lambda i: (0, i, 0)), tile,
                  VMEM_SPEC, VMEM_SPEC, HBM_SPEC, HBM_SPEC],
        out_specs=[pl.BlockSpec((TMB, WIN_COLS), lambda i: (i, 0)), VMEM_SPEC, HBM_SPEC, HBM_SPEC],
        out_shape=(_sds((TT, WIN_COLS), BF16), _sds((2, E), F32), _sds((RS_SLOTS, SH_ROWS, D), BF16), _sds((RS_SLOTS, D, SH_PWIN), BF16)),
        scratch_shapes=_rider2_scratch([(D, SH_PWIN), (SH_ROWS, D)]),
        compiler_params=pltpu.CompilerParams(dimension_semantics=("arbitrary",), vmem_limit_bytes=VMEM_LIMIT),
    )(g_all, d0, d1, dgl, dv, dz, lb, cum01, gwout, gpwin)


def _b1_in_bwd(idx1, ctx, x, dx1, dg, nw, msel, win):
    last_s = NDEV - 1

    def body(idx_ref, ctx_ref, x_ref, dx1_ref, dg_ref, nw_ref, m_ref, w_ref, gx_ref, rwin_o, dmx_o, dmc_o, gnw_o,
             hx_sc, dhx_sc, acc, sbuf, pbuf, psend, precv, isend, irecv, sibsem, lsem):
        del idx_ref
        s, i = pl.program_id(0), pl.program_id(1)
        x, y, cc, idx = _mesh_pos()
        shift, scale = m_ref[0, 0:1, :], m_ref[0, 1:2, :]
        sibling = (x, y, 1 - cc)

        def partial(p):
            return pltpu.make_async_remote_copy(src_ref=sbuf.at[0], dst_ref=pbuf.at[p], send_sem=psend.at[p], recv_sem=precv.at[p],
                                                device_id=sibling, device_id_type=MESH)

        def chip_sum(p):
            return pltpu.make_async_remote_copy(src_ref=sbuf.at[1], dst_ref=rwin_o.at[2 + p], send_sem=isend.at[p], recv_sem=irecv.at[p],
                                                device_id=_peer(x, y, cc, 2 * (p + 1)), device_id_type=MESH)

        to_sibling = pltpu.make_async_remote_copy(src_ref=sbuf.at[0], dst_ref=rwin_o.at[1], send_sem=sibsem.at[0], recv_sem=sibsem.at[1],
                                                  device_id=sibling, device_id_type=MESH)
        own = pltpu.make_async_copy(sbuf.at[1], rwin_o.at[0], lsem)

        @pl.when((s == 0) & (i == 0))
        def _():
            for ref in (dmx_o, dmc_o, gnw_o):
                ref[...] = jnp.zeros_like(ref)

        @pl.when(s == 0)
        def _():
            hx, _, _, _ = _modulated(_ctx_or_x(i, ctx_ref, x_ref), nw_ref[...], shift, scale)
            hx_sc[i] = hx.astype(BF16)

        @pl.when(i == 0)
        def _():
            acc[...] = jnp.zeros_like(acc)

        dgb = dg_ref[...]
        hxb = hx_sc[i]
        for lo, hi in ((0, 256), (256, 512), (512, SH_WIN)):
            acc[:, lo:hi] += _dot_ta(hxb, dgb[:, lo:hi])
        part = _dot_tb(dgb, w_ref[...])

        @pl.when(s == 0)
        def _():
            dhx_sc[i] = part

        @pl.when(s > 0)
        def _():
            dhx_sc[i] += part

        for p in (2, 1, 0):
            @pl.when((i == NT - 1) & (s == 2 * (2 - p)))
            def _(p=p):
                if p < 2:
                    partial(p + 1).wait_send()
                sbuf[0] = acc[...].astype(BF16)
                partial(p).start()

            @pl.when((i == NT - 1) & (s == 2 * (2 - p) + 1))
            def _(p=p):
                if p < 2:
                    chip_sum(p + 1).wait_send()
                partial(p).wait_recv()
                sbuf[1] = (acc[...] + pbuf[p].astype(F32)).astype(BF16)
                chip_sum(p).start()

        @pl.when((i == NT - 1) & (s == last_s - 1))
        def _():
            partial(0).wait_send()
            sbuf[0] = acc[...].astype(BF16)
            to_sibling.start()

        @pl.when((i == NT - 1) & (s == last_s))
        def _():
            chip_sum(0).wait_send()
            sbuf[1] = acc[...].astype(BF16)
            own.start()

        @pl.when(s == last_s)
        def _():
            nw = nw_ref[...]
            _, r, xn, a = _modulated(_ctx_or_x(i, ctx_ref, x_ref), nw, shift, scale)
            dhx = dhx_sc[i]
            dsh, dsc = _colsum(dhx), _colsum(dhx * a)
            da = dhx * (1.0 + scale)
            gnw_o[...] += _colsum(da * xn)
            dxn = da * nw
            gx_ref[...] = dx1_ref[...] + r * (dxn - xn * jnp.mean(dxn * xn, axis=-1, keepdims=True))

            @pl.when(i == 0)
            def _():
                dmc_o[0:1, :] += dsh
                dmc_o[1:2, :] += dsc

            @pl.when(i > 0)
            def _():
                dmx_o[0:1, :] += dsh
                dmx_o[1:2, :] += dsc

        @pl.when((i == NT - 1) & (s == last_s))
        def _():
            to_sibling.wait_send()
            to_sibling.wait_recv()
            for p in range(3):
                chip_sum(p).wait_recv()
            own.wait()

    grid_spec = pltpu.PrefetchScalarGridSpec(
        num_scalar_prefetch=1, grid=(NDEV, NT),
        in_specs=[VMEM_SPEC, pl.BlockSpec((TM, D), lambda s, i, ix: (jnp.maximum(i - 1, 0), 0)),
                  pl.BlockSpec((TM, D), lambda s, i, ix: (jnp.maximum(i - 1, 0), 0)),
                  pl.BlockSpec((TM, SH_WIN), lambda s, i, ix: (i, ix[0] ^ (last_s - s))), VMEM_SPEC,
                  pl.BlockSpec((1, 2, D), lambda s, i, ix: (jnp.minimum(i, 1), 0, 0)),
                  pl.BlockSpec((D, SH_WIN), lambda s, i, ix: (0, ix[0] ^ (last_s - s)))],
        out_specs=[pl.BlockSpec((TM, D), lambda s, i, ix: (jnp.where(s == last_s, jnp.maximum(i - 1, 0), 0), 0)),
                   HBM_SPEC, VMEM_SPEC, VMEM_SPEC, VMEM_SPEC],
        scratch_shapes=[pltpu.VMEM((NT, TM, D), BF16), pltpu.VMEM((NT, TM, D), F32), pltpu.VMEM((D, SH_WIN), F32),
                        pltpu.VMEM((2, D, SH_WIN), BF16), pltpu.VMEM((3, D, SH_WIN), BF16),
                        pltpu.SemaphoreType.DMA((3,)), pltpu.SemaphoreType.DMA((3,)), pltpu.SemaphoreType.DMA((3,)),
                        pltpu.SemaphoreType.DMA((3,)), pltpu.SemaphoreType.DMA((2,)), pltpu.SemaphoreType.DMA])
    return pl.pallas_call(
        body, name="b1_in_bwd", grid_spec=grid_spec,
        out_shape=(_sds((T, D), F32), _sds((RS_SLOTS, D, SH_WIN), BF16), _sds((2, D), F32), _sds((2, D), F32), _sds((1, D), F32)),
        compiler_params=pltpu.CompilerParams(dimension_semantics=("arbitrary", "arbitrary"), vmem_limit_bytes=VMEM_LIMIT),
    )(idx1, ctx, x, dx1, dg, nw, msel, win)


def _reduce_small(pd, pv, cg, c_ctx, ada_w0):
    n_arr = 3

    def body(pd_r, pv_r, cg_r, cctx_r, ada_r, gada_o, gadab_o, gcctx_o, pvsum_o, loss_o,
             pd_all, pv_all, dsc_all, dsc_mine, ssem, rsem):
        x, y, cc, idx = _mesh_pos()
        srcs = [pd_r, pv_r, dsc_mine]
        dsts = [pd_all.at[idx], pv_all.at[idx], dsc_all.at[idx]]

        def remote(a, k):
            return pltpu.make_async_remote_copy(src_ref=srcs[a], dst_ref=dsts[a], send_sem=ssem.at[a, k], recv_sem=rsem.at[a, k],
                                                device_id=_peer(x, y, cc, k), device_id_type=MESH)

        first = [remote(a, k) for k in range(1, NDEV) for a in (0, 1)]
        for cp in first:
            cp.start()
        pd_all[idx] = pd_r[...]
        pv_all[idx] = pv_r[...]
        for k in range(1, NDEV):
            remote(0, k).wait_recv()
            remote(1, k).wait_recv()
        mine = [pd_all[s, :, pl.ds(idx, 1), :] for s in range(NDEV)]
        dmc = functools.reduce(lambda u, v: u + v, [m[2] for m in mine])
        rows = _stack_rows([cg_r[i] for i in range(NDEV)] + [cctx_r[...]])
        sc = (rows * _sigmoid(rows)).astype(BF16)
        gada_o[0] = _dot_ta(sc, _stack_rows([m[0] for m in mine] + [dmc]))
        gada_o[1] = _dot_ta(sc, _stack_rows([m[1] for m in mine]))
        dsc_mine[...] = _dot_tb(jnp.broadcast_to(dmc, (8, SH_ADA)), ada_r[...])[0:1, :]
        dsc_all[idx] = dsc_mine[...]
        second = [remote(2, k) for k in range(1, NDEV)]
        for cp in second:
            cp.start()
        tot = [functools.reduce(lambda u, v: u + v, [pd_all[s, l] for s in range(NDEV)]) for l in range(3)]
        gadab_o[0] = tot[0] + tot[2]
        gadab_o[1] = tot[1]
        pvs = functools.reduce(lambda u, v: u + v, [pv_all[s] for s in range(NDEV)])
        pvsum_o[...] = pvs
        loss_o[...] = jnp.broadcast_to(jnp.sum(pvs[:, PV_LOSS:PV_LOSS + D], axis=-1, keepdims=True) * (0.5 / D), (1, 128))
        for k in range(1, NDEV):
            remote(2, k).wait_recv()
        dsc = functools.reduce(lambda u, v: u + v, [dsc_all[s] for s in range(NDEV)])
        cx = cctx_r[...]
        sx = _sigmoid(cx)
        gcctx_o[...] = dsc * (sx * (1.0 + cx * (1.0 - sx)))
        for cp in first + second:
            cp.wait_send()

    outs = (_sds((2, D, SH_ADA), F32), _sds((2, NDEV, SH_ADA), F32), _sds((1, D), F32), _sds((1, PV_LEN), F32), _sds((1, 128), F32))
    return pl.pallas_call(
        body, name="reduce_small", out_shape=outs,
        in_specs=[VMEM_SPEC] * 5, out_specs=[VMEM_SPEC] * 5,
        scratch_shapes=[
            pltpu.VMEM((NDEV, 3, NDEV, SH_ADA), F32), pltpu.VMEM((NDEV, 1, PV_LEN), F32), pltpu.VMEM((NDEV, 1, D), F32),
            pltpu.VMEM((1, D), F32),
            pltpu.SemaphoreType.DMA((n_arr, NDEV)), pltpu.SemaphoreType.DMA((n_arr, NDEV)),
        ],
        compiler_params=pltpu.CompilerParams(vmem_limit_bytes=VMEM_LIMIT),
    )(pd, pv, cg, c_ctx, ada_w0)


PV_NW, PV_GNORM, PV_FINAL, PV_LB, PV_PSCALE, PV_LOSS, PV_LEN = 0, 2 * D, 3 * D, 4 * D, 6 * D, 7 * D, 8 * D


def _adamw(w, g, m, v):
    m = ADAM_B1 * m + (1.0 - ADAM_B1) * g
    v = ADAM_B2 * v + (1.0 - ADAM_B2) * (g * g)
    m_hat = m / (1.0 - ADAM_B1 ** ADAM_STEP)
    v_hat = v / (1.0 - ADAM_B2 ** ADAM_STEP)
    delta = -ADAM_LR * (m_hat / (jnp.sqrt(v_hat) + ADAM_EPS) + ADAM_WD * w)
    return delta, m, v


ADAM_STEPS = 8


def _adam_all(sharded, dense, small, lb_idx, lbv):
    ns, nd, nsm = len(sharded), len(dense), len(small)

    def body(*refs):
        it = iter(refs)
        sh_in = [[next(it) for _ in range(4)] for _ in range(ns)]
        de_in = [[next(it) for _ in range(4)] for _ in range(nd)]
        sm_in = [[next(it) for _ in range(4)] for _ in range(nsm)]
        lb_r = next(it)
        sh_out = [[next(it) for _ in range(4)] for _ in range(ns)]
        de_out = [[next(it) for _ in range(3)] for _ in range(nd)]
        sm_out = [[next(it) for _ in range(4)] for _ in range(nsm)]
        for (p, w, m, v), outs in zip(sh_in, sh_out):
            g = p[0].astype(F32)
            for s in range(1, p.shape[0]):
                g = g + p[s].astype(F32)
            d, mn, vn = _adamw(w[...], g, m[...], v[...])
            outs[0][...], outs[1][...], outs[2][...], outs[3][...] = g, d, mn, vn
        for (g, w, m, v), outs in zip(de_in, de_out):
            d, mn, vn = _adamw(w[...], g[...], m[...], v[...])
            outs[0][...], outs[1][...], outs[2][...] = d, mn, vn

        @pl.when(pl.program_id(0) == 0)
        def _():
            for j, ((g, w, m, v), outs) in enumerate(zip(sm_in, sm_out)):
                gj = g[...]
                if j == lb_idx:
                    gj = gj * lb_r[...] * (1.0 - lb_r[...])
                d, mn, vn = _adamw(w[...], gj, m[...], v[...])
                outs[0][...], outs[1][...], outs[2][...], outs[3][...] = gj, d, mn, vn

    def tile(a):
        return pl.BlockSpec((a.shape[0] // ADAM_STEPS, a.shape[1]), lambda i: (i, 0))

    in_specs, out_specs, out_shape, args = [], [], [], []
    for p, w, m, v in sharded:
        in_specs += [pl.BlockSpec((p.shape[0], p.shape[1] // ADAM_STEPS, p.shape[2]), lambda i: (0, i, 0))] + [tile(w)] * 3
        args += [p, w, m, v]
    for g, w, m, v in dense:
        in_specs += [tile(w)] * 4
        args += [g, w, m, v]
    for g, w, m, v in small:
        in_specs += [VMEM_SPEC] * 4
        args += [g, w, m, v]
    in_specs.append(VMEM_SPEC)
    args.append(lbv)
    for _, w, _, _ in sharded:
        out_specs += [tile(w)] * 4
        out_shape += [_sds(w.shape, F32)] * 4
    for _, w, _, _ in dense:
        out_specs += [tile(w)] * 3
        out_shape += [_sds(w.shape, F32)] * 3
    for _, w, _, _ in small:
        out_specs += [VMEM_SPEC] * 4
        out_shape += [_sds(w.shape, F32)] * 4
    res = pl.pallas_call(body, name="adam_all", grid=(ADAM_STEPS,), in_specs=in_specs, out_specs=out_specs, out_shape=tuple(out_shape),
                         compiler_params=pltpu.CompilerParams(dimension_semantics=("arbitrary",), vmem_limit_bytes=VMEM_LIMIT))(*args)
    it = iter(res)
    return ([tuple(next(it) for _ in range(4)) for _ in range(ns)], [tuple(next(it) for _ in range(3)) for _ in range(nd)],
            [tuple(next(it) for _ in range(4)) for _ in range(nsm)])


def kernel(x, c, ctx, c_ctx, ada_w, ada_b, norm_w, hgrn_w_in, hgrn_lb_logits, hgrn_gnorm_w, hgrn_w_out, pool_w_in, pool_w_grp, pool_scale, pool_w_out, final_norm_w, loss_target, m_c_ctx, m_ada_w, m_ada_b, m_norm_w, m_hgrn_w_in, m_hgrn_lb_logits, m_hgrn_gnorm_w, m_hgrn_w_out, m_pool_w_in, m_pool_w_grp, m_pool_scale, m_pool_w_out, m_final_norm_w, v_c_ctx, v_ada_w, v_ada_b, v_norm_w, v_hgrn_w_in, v_hgrn_lb_logits, v_hgrn_gnorm_w, v_hgrn_w_out, v_pool_w_in, v_pool_w_grp, v_pool_scale, v_pool_w_out, v_final_norm_w):
    idx = 4 * lax.axis_index("x") + 2 * lax.axis_index("y") + lax.axis_index("c")
    cctx2 = c_ctx.reshape(1, D)
    cum01, mask01 = _gla_consts()
    pb, pbt, pinv = _pool_consts()

    idx1 = idx.reshape(1).astype(jnp.int32)
    nw0, nw1 = norm_w[0:1], norm_w[1:2]
    fnw = final_norm_w.reshape(1, D)
    g_all, win, s_wout, s_pwin, s_pgrp, s_pwout, lbl_g, ps_g, cg, mod0, mod1, modc = _f1_gather_matmul(
        idx1, ctx[0], x[0], nw0, hgrn_w_in[0], hgrn_w_out[0], pool_w_in[0], pool_w_grp[0], pool_w_out[0], hgrn_lb_logits[0],
        pool_scale, c, cctx2, ada_w, ada_b)
    lb = jax.nn.sigmoid(jnp.transpose(lbl_g, (1, 0, 2)).reshape(2, E))
    pscale = ps_g.reshape(1, E)
    msel = jnp.stack([modc[:2], mod0[:2]])
    p0, p1, v_all, dec, pwin, pgrp = _gla_prep(g_all, lb, cum01, s_pwin, s_pgrp)
    o, wout, pwout = _gla_fwd(p0, p1, v_all, dec, mask01, s_wout, s_pwout)
    x1 = _f3_out(o, g_all, x[0], mod0[2:3], hgrn_gnorm_w, wout)
    dx1, gpwin, gpgrp, gpwout, dmod1, gnw1, gfw, gps, lossv = _pool_layer(
        x1, loss_target[0], mod1, nw1, fnw, pwin, pgrp, pscale, pwout, pb, pbt, pinv)
    do, dz, gwout, dgate0, ggw, rpwout = _b3_out_bwd(dx1, o, g_all, mod0[2:3], hgrn_gnorm_w, wout, gpwout)
    d0, d1, dv, dgl, rpgrp = _gla_bwd(p0, p1, v_all, dec, do, mask01, gpgrp)
    dg, dlb, rwout, rpwin = _gla_post_bwd(g_all, d0, d1, dgl, dv, dz, lb, cum01, gwout, gpwin)
    grad_x, rwin, dmx, dmc, gnw0 = _b1_in_bwd(idx1, ctx[0], x[0], dx1, dg, nw0, msel, win)

    dmod0 = jnp.concatenate([dmx, dgate0], axis=0)
    dmodc = jnp.concatenate([dmc, jnp.zeros((1, D), F32)], axis=0)
    pd = jnp.stack([dmod0, dmod1, dmodc]).reshape(3, NDEV, SH_ADA)
    pv = jnp.concatenate([gnw0, gnw1, ggw, gfw, dlb.reshape(1, 2 * E), gps, lossv], axis=1)
    g_ada, g_adab, g_cctx, pvsum, loss128 = _reduce_small(pd, pv, cg, cctx2, ada_w[0])

    g2 = (4 * SH_GRP, PG)
    sharded_names = ["hgrn_w_in", "hgrn_w_out", "pool_w_in", "pool_w_grp", "pool_w_out"]
    sharded = [(rwin, hgrn_w_in[0], m_hgrn_w_in[0], v_hgrn_w_in[0]),
               (rwout, hgrn_w_out[0], m_hgrn_w_out[0], v_hgrn_w_out[0]),
               (rpwin, pool_w_in[0], m_pool_w_in[0], v_pool_w_in[0]),
               (rpgrp.reshape((NDEV,) + g2), pool_w_grp[0].reshape(g2), m_pool_w_grp[0].reshape(g2), v_pool_w_grp[0].reshape(g2)),
               (rpwout, pool_w_out[0], m_pool_w_out[0], v_pool_w_out[0])]
    a2 = (2 * D, SH_ADA)
    g_ada2 = g_ada.reshape(a2)
    dense = [(g_ada2, ada_w.reshape(a2), m_ada_w.reshape(a2), v_ada_w.reshape(a2))]
    lb_me = lax.dynamic_slice_in_dim(lb, idx * DH, DH, axis=1)
    small_names = ["c_ctx", "ada_b", "norm_w", "hgrn_lb_logits", "hgrn_gnorm_w", "pool_scale", "final_norm_w"]
    small = [(g_cctx, cctx2, m_c_ctx.reshape(1, D), v_c_ctx.reshape(1, D)),
             (g_adab.reshape(2, 3 * D), ada_b, m_ada_b, v_ada_b),
             (pvsum[:, PV_NW:PV_NW + 2 * D].reshape(2, D), norm_w, m_norm_w, v_norm_w),
             (lax.dynamic_slice_in_dim(pvsum[:, PV_LB:PV_LB + 2 * E].reshape(2, E), idx * DH, DH, axis=1),
              hgrn_lb_logits[0], m_hgrn_lb_logits[0], v_hgrn_lb_logits[0]),
             (pvsum[:, PV_GNORM:PV_GNORM + E], hgrn_gnorm_w, m_hgrn_gnorm_w, v_hgrn_gnorm_w),
             (lax.dynamic_slice_in_dim(pvsum[:, PV_PSCALE:PV_PSCALE + E], idx * DH, DH, axis=1), pool_scale, m_pool_scale, v_pool_scale),
             (pvsum[:, PV_FINAL:PV_FINAL + D], fnw, m_final_norm_w.reshape(1, D), v_final_norm_w.reshape(1, D))]
    r_sharded, r_dense, r_small = _adam_all(sharded, dense, small, 3, lb_me)
    out = dict(zip(sharded_names, r_sharded))
    out["ada_w"] = (g_ada2,) + r_dense[0]
    out.update(zip(small_names, r_small))

    shapes = {"c_ctx": (D,), "ada_w": (2, D, SH_ADA), "ada_b": (2, 3 * D), "norm_w": (2, D), "hgrn_w_in": (1, D, SH_WIN),
              "hgrn_lb_logits": (1, 2, DH), "hgrn_gnorm_w": (1, E), "hgrn_w_out": (1, SH_ROWS, D), "pool_w_in": (1, D, SH_PWIN),
              "pool_w_grp": (1, 4, SH_GRP, PG), "pool_scale": (1, DH), "pool_w_out": (1, SH_ROWS, D), "final_norm_w": (D,)}
    order = ["c_ctx", "ada_w", "ada_b", "norm_w", "hgrn_w_in", "hgrn_lb_logits", "hgrn_gnorm_w", "hgrn_w_out", "pool_w_in",
             "pool_w_grp", "pool_scale", "pool_w_out", "final_norm_w"]
    flat = [out[name][q].reshape(shapes[name]) for q in range(4) for name in order]
    return (loss128[0, 0], grad_x[None], *flat)
```

```python
import functools

import numpy as np
import jax
import jax.numpy as jnp
from jax import lax
from jax.experimental import pallas as pl
from jax.experimental.pallas import tpu as pltpu

F32 = jnp.float32
BF16 = jnp.bfloat16

D = 1024
E = 1024
HEADS = 8
DH = 128
CHUNK = 64
T = 2048
TC = 256
TT = T + TC
TM = 256
NT = TT // TM
NTX = T // TM
NDEV = 8
GRID_W = 64
POOL_WINDOWS = (2, 4, 8, 16)
PG = 256
EPS = 1e-6
WIN_COLS = 5 * E
SH_WIN = WIN_COLS // NDEV
SH_PWIN = 2 * E // NDEV
SH_ROWS = E // NDEV
SH_GRP = PG // NDEV
SH_ADA = 3 * D // NDEV
VMEM_LIMIT = 56 * 1024 * 1024
VMEM_LIMIT_SCAN = 60 * 1024 * 1024

ADAM_LR, ADAM_B1, ADAM_B2, ADAM_EPS, ADAM_WD, ADAM_STEP = 0.001, 0.9, 0.999, 1e-08, 0.01, 10

MESH = pl.DeviceIdType.MESH
VMEM_SPEC = pl.BlockSpec(memory_space=pltpu.VMEM)
HBM_SPEC = pl.BlockSpec(memory_space=pltpu.HBM)
ANY_SPEC = pl.BlockSpec(memory_space=pl.ANY)


def _sds(shape, dtype):
    return jax.ShapeDtypeStruct(shape, dtype)


def _bf(a):
    return a if a.dtype == BF16 else a.astype(BF16)


def _dot(a, b):
    return lax.dot_general(_bf(a), _bf(b), (((1,), (0,)), ((), ())), preferred_element_type=F32)


def _dot_tb(a, b):
    return lax.dot_general(_bf(a), _bf(b), (((1,), (1,)), ((), ())), preferred_element_type=F32)


def _dot_ta(a, b):
    return lax.dot_general(_bf(a), _bf(b), (((0,), (0,)), ((), ())), preferred_element_type=F32)


def _bdot(a, b):
    return lax.dot_general(_bf(a), _bf(b), (((2,), (1,)), ((0,), (0,))), preferred_element_type=F32)


def _bdot_nt(a, b):
    return lax.dot_general(_bf(a), _bf(b), (((2,), (2,)), ((0,), (0,))), preferred_element_type=F32)


def _bdot_tn(a, b):
    return lax.dot_general(_bf(a), _bf(b), (((1,), (1,)), ((0,), (0,))), preferred_element_type=F32)


def _dot01(m01, x):
    hi = x.astype(BF16)
    lo = (x - hi.astype(F32)).astype(BF16)
    return _dot(m01, hi) + _dot(m01, lo)


def _rstd(x):
    return lax.rsqrt(jnp.mean(x * x, axis=-1, keepdims=True) + EPS)


def _sigmoid(x):
    return jax.nn.sigmoid(x)


def _colsum(a):
    return jnp.sum(a, axis=0, keepdims=True)


def _stack_rows(rows):
    n = rows[0].shape[-1]
    rid = lax.broadcasted_iota(jnp.int32, (16, n), 0)
    out = jnp.zeros((16, n), F32)
    for i, r in enumerate(rows):
        out = jnp.where(rid == i, r, out)
    return out


def _head_map(fn, *arrs):
    outs = [fn(*[a[:, h * DH:(h + 1) * DH] for a in arrs]) for h in range(HEADS)]
    return jnp.concatenate(outs, axis=1)


def _gla_consts():
    r = np.arange(TM)[:, None]
    c = np.arange(TM)[None, :]
    same = (r // CHUNK) == (c // CHUNK)
    tril = same & (c <= r)
    triu = same & (c >= r)
    m = np.stack([tril, triu]).astype(np.float32)
    return jnp.asarray(m, BF16), jnp.asarray(m, F32)


def _pool_consts():
    r = np.arange(TM)[:, None]
    c = np.arange(TM)[None, :]
    same = (r // GRID_W) == (c // GRID_W)
    rp, cp = r % GRID_W, c % GRID_W
    bs, inv = [], []
    for w in POOL_WINDOWS:
        lo = np.clip(rp - w // 2, 0, GRID_W)
        hi = np.clip(rp - w // 2 + w, 0, GRID_W)
        bs.append(same & (cp >= lo) & (cp < hi))
        inv.append(1.0 / (hi - lo).astype(np.float32))
    b = np.stack(bs).astype(np.float32)
    bt = np.transpose(b, (0, 2, 1))
    return jnp.asarray(b, BF16), jnp.asarray(bt, BF16), jnp.asarray(np.stack(inv), F32)


def _mesh_pos():
    x, y, c = lax.axis_index("x"), lax.axis_index("y"), lax.axis_index("c")
    return x, y, c, 4 * x + 2 * y + c


def _peer(x, y, c, k):
    return (x ^ ((k >> 2) & 1), y ^ ((k >> 1) & 1), c ^ (k & 1))


def _small_gathers(refs, ssem, rsem):
    lb_r, ps_r, c_r, cctx_r, ada_r, adab_r, lb_o, ps_o, cg_o, mod_o, lb_out, ps_out, cg_out, mod0_o, mod1_o, modc_o = refs
    x, y, cc, idx = _mesh_pos()
    srcs = [lb_r, ps_r, c_r, mod_o.at[idx]]
    mine = [lb_o.at[idx], ps_o.at[idx], cg_o.at[idx], mod_o.at[idx]]

    def remote(a, k):
        return pltpu.make_async_remote_copy(src_ref=srcs[a], dst_ref=mine[a], send_sem=ssem.at[a, k], recv_sem=rsem.at[a, k],
                                            device_id=_peer(x, y, cc, k), device_id_type=MESH)

    first = [remote(a, k) for k in range(1, NDEV) for a in (2, 0, 1)]
    for cp in first:
        cp.start()
    lb_o[idx] = lb_r[...]
    ps_o[idx] = ps_r[...]
    cg_o[idx] = c_r[...]
    for k in range(1, NDEV):
        remote(2, k).wait_recv()
    rows = _stack_rows([cg_o[i] for i in range(NDEV)] + [cctx_r[...]])
    sc = rows * _sigmoid(rows)
    for l in range(2):
        mod_o[idx, l] = _dot(sc, ada_r[l])
    second = [remote(3, k) for k in range(1, NDEV)]
    for cp in second:
        cp.start()
    for k in range(1, NDEV):
        remote(3, k).wait_recv()

    def mod_rows(l, row):
        full = jnp.concatenate([mod_o[s, l, row, :] for s in range(NDEV)], axis=1) + adab_r[l:l + 1, :]
        return [full[:, j * D:(j + 1) * D] for j in range(3)]

    me = pl.ds(idx, 1)
    for out, parts in ((mod0_o, mod_rows(0, me)), (mod1_o, mod_rows(1, me)), (modc_o, mod_rows(0, slice(NDEV, NDEV + 1)))):
        for j in range(3):
            out[j:j + 1, :] = parts[j]
    for cp in first + second:
        cp.wait_send()
    for k in range(1, NDEV):
        for a in (0, 1):
            remote(a, k).wait_recv()
    lb_out[...] = lb_o[...]
    ps_out[...] = ps_o[...]
    cg_out[...] = cg_o[...]


def _gather_order(s):
    if isinstance(s, int):
        return (0, 1, 2, 4, 3, 5, 6, 7)[s]
    return s + (s == 3).astype(jnp.int32) - (s == 4).astype(jnp.int32)


GATHER_ISSUE = (1, 2, 4, 3, 5, 6, 7)
GATHER_ICI = (2, 4, 6)
GATHER_DIRECT = (1,) + GATHER_ICI
GLA_HB = 2
RS_SLOTS = 5


def _shard_of(kind, ref, i):
    if kind == "rows":
        return ref.at[pl.ds(pl.multiple_of(i * SH_ROWS, SH_ROWS), SH_ROWS), :]
    if kind == "major":
        return ref.at[i]
    assert kind == "grp"
    return ref.at[:, pl.ds(pl.multiple_of(i * SH_GRP, SH_GRP), SH_GRP), :]


def _gather_rider(step, n_steps, forward_at, kinds, srcs, outs, ssem, rsem, lsem):
    x, y, cc, idx = _mesh_pos()
    arrays = range(len(kinds))
    mine = [_shard_of(kinds[a], outs[a], idx) for a in arrays]

    def remote(a, k):
        return pltpu.make_async_remote_copy(src_ref=srcs[a], dst_ref=mine[a], send_sem=ssem.at[a, k], recv_sem=rsem.at[a, k],
                                            device_id=_peer(x, y, cc, k), device_id_type=MESH)

    def forward(a, k):
        blk = _shard_of(kinds[a], outs[a], idx ^ k)
        return pltpu.make_async_remote_copy(src_ref=blk, dst_ref=blk, send_sem=ssem.at[a, k ^ 1], recv_sem=rsem.at[a, k ^ 1],
                                            device_id=(x, y, 1 - cc), device_id_type=MESH)

    copies = [remote(a, k) for k in GATHER_DIRECT for a in arrays]
    passed = [forward(a, k) for k in GATHER_ICI for a in arrays]
    local = [pltpu.make_async_copy(srcs[a], mine[a], lsem.at[a]) for a in arrays]

    @pl.when(step == 0)
    def _():
        for cp in copies + local:
            cp.start()

    @pl.when(step == forward_at)
    def _():
        for k in GATHER_ICI:
            for a in arrays:
                remote(a, k).wait_recv()
                forward(a, k).start()

    @pl.when(step == n_steps - 1)
    def _():
        for cp in copies + passed:
            cp.wait_send()
        for a in arrays:
            remote(a, 1).wait_recv()
        for cp in passed:
            cp.wait_recv()
        for cp in local:
            cp.wait()


def _scatter_rider(step, n_steps, kinds, grads, slots, ssem, rsem, lsem):
    x, y, cc, idx = _mesh_pos()
    arrays = range(len(kinds))
    dsts = [slots[a].at[idx] for a in arrays]

    def remote(a, k):
        px, py, pc = _peer(x, y, cc, k)
        return pltpu.make_async_remote_copy(src_ref=_shard_of(kinds[a], grads[a], 4 * px + 2 * py + pc), dst_ref=dsts[a],
                                            send_sem=ssem.at[a, k], recv_sem=rsem.at[a, k], device_id=(px, py, pc), device_id_type=MESH)

    copies = [remote(a, k) for k in GATHER_ISSUE for a in arrays]
    local = [pltpu.make_async_copy(_shard_of(kinds[a], grads[a], idx), dsts[a], lsem.at[a]) for a in arrays]

    @pl.when(step == 0)
    def _():
        for cp in copies + local:
            cp.start()

    @pl.when(step == n_steps - 1)
    def _():
        for cp in copies:
            cp.wait_send()
        for cp in copies:
            cp.wait_recv()
        for cp in local:
            cp.wait()


def _rider_sems(n):
    return [pltpu.SemaphoreType.DMA((n, NDEV)), pltpu.SemaphoreType.DMA((n, NDEV)), pltpu.SemaphoreType.DMA((n,))]


def _scatter_rider2(step, n_steps, add_at, kinds, grads, slots, bufs, sems):
    x, y, cc, idx = _mesh_pos()
    sibling = (x, y, 1 - cc)
    arrays = range(len(kinds))
    psend, precv, isend, irecv, lown, sibsem, lself = sems

    def mine(a, i):
        return _shard_of(kinds[a], grads[a], i)

    def partial(a, p):
        return pltpu.make_async_remote_copy(src_ref=mine(a, idx ^ (2 * (p + 1)) ^ 1), dst_ref=bufs[a][1].at[p], send_sem=psend.at[a, p],
                                            recv_sem=precv.at[a, p], device_id=sibling, device_id_type=MESH)

    def load(a, p):
        return pltpu.make_async_copy(mine(a, idx ^ (2 * (p + 1))), bufs[a][0].at[p], lown.at[a, p])

    def chip_sum(a, p):
        return pltpu.make_async_remote_copy(src_ref=bufs[a][0].at[p], dst_ref=slots[a].at[2 + p], send_sem=isend.at[a, p],
                                            recv_sem=irecv.at[a, p], device_id=_peer(x, y, cc, 2 * (p + 1)), device_id_type=MESH)

    def to_sibling(a):
        return pltpu.make_async_remote_copy(src_ref=mine(a, idx ^ 1), dst_ref=slots[a].at[1], send_sem=sibsem.at[a, 0],
                                            recv_sem=sibsem.at[a, 1], device_id=sibling, device_id_type=MESH)

    def own(a):
        return pltpu.make_async_copy(mine(a, idx), slots[a].at[0], lself.at[a, 0])

    @pl.when(step == 0)
    def _():
        for a in arrays:
            for p in range(3):
                partial(a, p).start()
                load(a, p).start()
            to_sibling(a).start()
            own(a).start()

    @pl.when(step == add_at)
    def _():
        for a in arrays:
            for p in range(3):
                partial(a, p).wait_recv()
                load(a, p).wait()
                bufs[a][0][p] = (bufs[a][0][p].astype(F32) + bufs[a][1][p].astype(F32)).astype(BF16)
                chip_sum(a, p).start()

    @pl.when(step == n_steps - 1)
    def _():
        for a in arrays:
            for p in range(3):
                partial(a, p).wait_send()
                chip_sum(a, p).wait_send()
                chip_sum(a, p).wait_recv()
            to_sibling(a).wait_send()
            to_sibling(a).wait_recv()
            own(a).wait()


def _rider2_scratch(blocks):
    n = len(blocks)
    bufs = [pltpu.VMEM((3,) + tuple(b), BF16) for b in blocks for _ in range(2)]
    return bufs + [pltpu.SemaphoreType.DMA((n, 3)) for _ in range(5)] + [pltpu.SemaphoreType.DMA((n, 2)), pltpu.SemaphoreType.DMA((n, 1))]


def _rider2_split(refs, n):
    refs = list(refs)
    return [tuple(refs[2 * a:2 * a + 2]) for a in range(n)], tuple(refs[2 * n:2 * n + 7])


def _modulated(x, nw, shift, scale):
    r = _rstd(x)
    xn = x * r
    a = xn * nw
    return a * (1.0 + scale) + shift, r, xn, a


def _ctx_or_x(i, ctx_ref, x_ref):
    return jnp.where(i == 0, ctx_ref[...], x_ref[...])


def _f1_gather_matmul(idx1, ctx, x, nw, w_in, w_out, pw_in, pgrp, pw_out, lb_l, pscale, c, c_ctx, ada_w, ada_b):
    def body(idx_ref, ctx_ref, x_ref, nw_ref, win_r, wout_r, pwin_r, pgrp_r, pwout_r, lb_r, ps_r, c_r, cctx_r, ada_r, adab_r,
             g_ref, win_o, s_wout, s_pwin, s_pgrp, s_pwout, lb_o, ps_o, cg_o, mod0_o, mod1_o, modc_o,
             wslot, hx_sc, lb_g, ps_g, cg_g, mod_g, ssem, rsem, osem, sm_ssem, sm_rsem):
        del idx_ref
        s, i = pl.program_id(0), pl.program_id(1)
        x, y, cc, idx = _mesh_pos()
        k = _gather_order(s)
        j = idx ^ k

        def remote(kk):
            return pltpu.make_async_remote_copy(src_ref=wslot.at[idx], dst_ref=wslot.at[idx], send_sem=ssem.at[kk], recv_sem=rsem.at[kk],
                                                device_id=_peer(x, y, cc, kk), device_id_type=MESH)

        def forward(kk):
            jj = idx ^ kk
            return pltpu.make_async_remote_copy(src_ref=wslot.at[jj], dst_ref=wslot.at[jj], send_sem=ssem.at[kk ^ 1],
                                                recv_sem=rsem.at[kk ^ 1], device_id=(x, y, 1 - cc), device_id_type=MESH)

        def to_hbm(jj, kk):
            return pltpu.make_async_copy(wslot.at[jj], win_o.at[:, pl.ds(pl.multiple_of(jj * SH_WIN, 128), SH_WIN)], osem.at[kk])

        @pl.when((s == 0) & (i == 0))
        def _():
            _small_gathers((lb_r, ps_r, c_r, cctx_r, ada_r, adab_r, lb_g, ps_g, cg_g, mod_g, lb_o, ps_o, cg_o, mod0_o, mod1_o, modc_o),
                           sm_ssem, sm_rsem)
            wslot[idx] = win_r[...].astype(BF16)
            for kk in GATHER_DIRECT:
                remote(kk).start()
            s_wout[...] = wout_r[...].astype(BF16)
            s_pwin[...] = pwin_r[...].astype(BF16)
            s_pgrp[...] = pgrp_r[...].astype(BF16)
            s_pwout[...] = pwout_r[...].astype(BF16)

        @pl.when(s == 0)
        def _():
            shift = jnp.where(i == 0, modc_o[0:1, :], mod0_o[0:1, :])
            scale = jnp.where(i == 0, modc_o[1:2, :], mod0_o[1:2, :])
            hx, _, _, _ = _modulated(_ctx_or_x(i, ctx_ref, x_ref), nw_ref[...], shift, scale)
            hx_sc[i] = hx.astype(BF16)

        @pl.when((s > 0) & (i == 0))
        def _():
            remote(k).wait_recv()

            @pl.when((k & 1) == 0)
            def _():
                forward(k).start()

        @pl.when(i == 0)
        def _():
            to_hbm(j, k).start()

        g_ref[...] = jnp.dot(hx_sc[i], wslot[j], preferred_element_type=F32)

        @pl.when((s == NDEV - 1) & (i == NT - 1))
        def _():
            for kk in GATHER_DIRECT:
                remote(kk).wait_send()
            for kk in GATHER_ICI:
                forward(kk).wait_send()
            for kk in range(NDEV):
                to_hbm(idx ^ kk, kk).wait()

    grid_spec = pltpu.PrefetchScalarGridSpec(
        num_scalar_prefetch=1, grid=(NDEV, NT),
        in_specs=[VMEM_SPEC, pl.BlockSpec((TM, D), lambda s, i, ix: (jnp.maximum(i - 1, 0), 0))] + [VMEM_SPEC] * 12,
        out_specs=[pl.BlockSpec((TM, SH_WIN), lambda s, i, ix: (i, ix[0] ^ _gather_order(s))), HBM_SPEC] + [VMEM_SPEC] * 10,
        scratch_shapes=[pltpu.VMEM((NDEV, D, SH_WIN), BF16), pltpu.VMEM((NT, TM, D), BF16),
                        pltpu.VMEM((NDEV, 2, DH), F32), pltpu.VMEM((NDEV, 1, DH), F32), pltpu.VMEM((NDEV, 1, D), F32),
                        pltpu.VMEM((NDEV, 2, 16, SH_ADA), F32),
                        pltpu.SemaphoreType.DMA((NDEV,)), pltpu.SemaphoreType.DMA((NDEV,)), pltpu.SemaphoreType.DMA((NDEV,)),
                        pltpu.SemaphoreType.DMA((4, NDEV)), pltpu.SemaphoreType.DMA((4, NDEV))])
    outs = (_sds((TT, WIN_COLS), F32), _sds((D, WIN_COLS), BF16),
            _sds((SH_ROWS, D), BF16), _sds((D, SH_PWIN), BF16), _sds((4, SH_GRP, PG), BF16), _sds((SH_ROWS, D), BF16),
            _sds((NDEV, 2, DH), F32), _sds((NDEV, 1, DH), F32), _sds((NDEV, 1, D), F32),
            _sds((3, D), F32), _sds((3, D), F32), _sds((3, D), F32))
    return pl.pallas_call(
        body, name="f1_gather_matmul", grid_spec=grid_spec, out_shape=outs,
        compiler_params=pltpu.CompilerParams(dimension_semantics=("arbitrary", "arbitrary"), vmem_limit_bytes=VMEM_LIMIT),
    )(idx1, ctx, x, nw, w_in, w_out, pw_in, pgrp, pw_out, lb_l, pscale, c, c_ctx, ada_w, ada_b)


def _gla_gates(pre, qpre, lbd, cum, rev):
    rows, n = pre.shape
    nch = rows // CHUNK
    sig = _sigmoid(pre)
    f = lbd + (1.0 - lbd) * sig
    k = 1.0 - f
    g = _dot01(cum, jnp.log(f))
    g3 = g.reshape(nch, CHUNK, n)
    last = 0 if rev else CHUNK - 1
    mid = CHUNK // 2 if rev else CHUNK // 2 - 1
    gl1, gm1 = g3[:, last:last + 1, :], g3[:, mid:mid + 1, :]

    def bc(a):
        return jnp.broadcast_to(a, g3.shape).reshape(rows, n)

    gm = bc(gm1)
    e_q, e_k = jnp.exp(g - gm), jnp.exp(gm - g)
    qsig = _sigmoid(qpre)
    qs = qpre * qsig * (DH ** -0.5)
    return dict(sig=sig, f=f, k=k, qsig=qsig, qs=qs, e_q=e_q, e_k=e_k,
                e_mid=[jnp.exp(gm1[ci]) for ci in range(nch)], e_rest=[jnp.exp(gl1[ci] - gm1[ci]) for ci in range(nch)])


def _put_heads(ref, lead, arr):
    for h in range(HEADS):
        ref[lead + (h,)] = arr[:, h * DH:(h + 1) * DH]


def _get_heads(ref, lead=()):
    return jnp.concatenate([ref[lead + (h,)] for h in range(HEADS)], axis=1)


def _gla_prep(g_all, lb, cum01, s_pwin, s_pgrp):
    nch = TM // CHUNK

    def body(g_ref, lb_ref, cum_ref, spwin_r, spgrp_r, p0_ref, p1_ref, v_ref, dec_ref, pwin_o, pgrp_o, ssem, rsem, lsem):
        _gather_rider(pl.program_id(0), NT, NT // 2 + 1, ("major", "grp"), (spwin_r, spgrp_r), (pwin_o, pgrp_o), ssem, rsem, lsem)
        qpre = g_ref[:, 3 * E:4 * E]
        _put_heads(v_ref, (), g_ref[:, 2 * E:3 * E].astype(BF16))
        for d, p_ref in ((0, p0_ref), (1, p1_ref)):
            t = _gla_gates(g_ref[:, d * E:(d + 1) * E], qpre, lb_ref[d:d + 1, :], cum_ref[d], d == 1)
            _put_heads(p_ref, (0,), (t["qs"] * t["e_q"]).astype(BF16))
            _put_heads(p_ref, (1,), (t["k"] * t["e_k"]).astype(BF16))
            for ci in range(nch):
                dec_ref[d, 0, ci:ci + 1, :] = t["e_mid"][ci]
                dec_ref[d, 0, nch + ci:nch + ci + 1, :] = t["e_rest"][ci]

    quad = pl.BlockSpec((2, HEADS, TM, DH), lambda i: (0, 0, i, 0))
    return pl.pallas_call(
        body, name="gla_prep", grid=(NT,),
        in_specs=[pl.BlockSpec((TM, 4 * E), lambda i: (i, 0)), VMEM_SPEC, VMEM_SPEC, HBM_SPEC, HBM_SPEC],
        out_specs=[quad, quad, pl.BlockSpec((HEADS, TM, DH), lambda i: (0, i, 0)), pl.BlockSpec((2, 1, 2 * nch, E), lambda i: (0, i, 0, 0)),
                   HBM_SPEC, HBM_SPEC],
        out_shape=(_sds((2, HEADS, TT, DH), BF16), _sds((2, HEADS, TT, DH), BF16), _sds((HEADS, TT, DH), BF16), _sds((2, NT, 2 * nch, E), F32),
                   _sds((NDEV, D, SH_PWIN), BF16), _sds((4, PG, PG), BF16)),
        scratch_shapes=_rider_sems(2),
        compiler_params=pltpu.CompilerParams(dimension_semantics=("arbitrary",), vmem_limit_bytes=VMEM_LIMIT),
    )(g_all, lb, cum01, s_pwin, s_pgrp)


def _scan_tile(i, rev):
    t = jnp.where(i == 0, 0, NT - i) if rev else i
    return t, pl.ds(pl.multiple_of(t * TM, TM), TM)


def _chunk_order(rev):
    n = TM // CHUNK
    return tuple(range(n - 1, -1, -1)) if rev else tuple(range(n))


def _chunk_rows(dec_ref, lanes, cis, where):
    nch = TM // CHUNK

    def rows(off):
        return jnp.stack([dec_ref[d, where[d][0], off + ci:off + ci + 1, hh * DH:(hh + 1) * DH] for (d, hh), ci in zip(lanes, cis)])

    return rows(0), rows(nch)


def _gla_fwd(p0, p1, v_all, dec, mask01, s_wout, s_pwout):
    n_steps = HEADS // GLA_HB

    def body(p0_ref, p1_ref, v_ref, dec_ref, msk_ref, swout_r, spwout_r, o_ref, wout_o, pwout_o, ob_sc, ssem, rsem, lsem):
        _gather_rider(pl.program_id(0), n_steps, n_steps // 2, ("rows", "rows"), (swout_r, spwout_r), (wout_o, pwout_o), ssem, rsem, lsem)

        lanes = [(d, hh) for d in (0, 1) for hh in range(GLA_HB)]
        nch = TM // CHUNK

        def tile_body(i, st):
            where = [_scan_tile(i, d == 1) for d in (0, 1)]

            def stacked(fn):
                return jnp.stack([fn(d, hh, where[d][1]) for d, hh in lanes])

            qg, kg = [stacked(lambda d, hh, rows, ty=ty: (p1_ref if d else p0_ref)[ty, hh, rows, :]) for ty in range(2)]
            v = stacked(lambda d, hh, rows: v_ref[hh, rows, :])
            a = _bdot_nt(qg, kg) * jnp.stack([msk_ref[d] for d, _ in lanes])
            intra = _bdot(a, v)
            outs = [[None] * nch for _ in lanes]
            for n in range(nch):
                cis = [nch - 1 - n if d else n for d, _ in lanes]

                def chunk(arr):
                    return jnp.stack([arr[l, ci * CHUNK:(ci + 1) * CHUNK] for l, ci in enumerate(cis)])

                e_mid, e_rest = _chunk_rows(dec_ref, lanes, cis, where)
                inter = _bdot_nt(chunk(qg), st * e_mid)
                for l, ci in enumerate(cis):
                    outs[l][ci] = inter[l] + intra[l, ci * CHUNK:(ci + 1) * CHUNK]
                st = st * (e_mid * e_rest) + _bdot_tn(chunk(v), chunk(kg)) * e_rest
            for l, (d, hh) in enumerate(lanes):
                (ob_sc if d else o_ref)[hh, where[d][1], :] = jnp.concatenate(outs[l], axis=0)
            return st

        lax.fori_loop(0, NT, tile_body, jnp.zeros((len(lanes), DH, DH), F32))
        o_ref[...] += ob_sc[...]

    quad = pl.BlockSpec((2, GLA_HB, TT, DH), lambda h: (0, h, 0, 0))
    head = pl.BlockSpec((GLA_HB, TT, DH), lambda h: (h, 0, 0))
    return pl.pallas_call(
        body, name="gla_fwd", grid=(n_steps,),
        in_specs=[quad, quad, head, pl.BlockSpec((2, NT, 8, GLA_HB * DH), lambda h: (0, 0, 0, h)),
                  pl.BlockSpec((2, TM, TM), lambda h: (0, 0, 0)), HBM_SPEC, HBM_SPEC],
        out_specs=[head, HBM_SPEC, HBM_SPEC],
        out_shape=(_sds((HEADS, TT, DH), F32), _sds((E, D), BF16), _sds((E, D), BF16)),
        scratch_shapes=[pltpu.VMEM((GLA_HB, TT, DH), F32)] + _rider_sems(2),
        compiler_params=pltpu.CompilerParams(dimension_semantics=("arbitrary",), vmem_limit_bytes=VMEM_LIMIT),
    )(p0, p1, v_all, dec, mask01, s_wout, s_pwout)


def _gated_norm(o, z, gw):
    r = _head_map(lambda oh: jnp.broadcast_to(_rstd(oh), oh.shape), o)
    on = o * r
    zs = _sigmoid(z)
    sz = z * zs
    return on * gw * sz, r, on, zs, sz


def _f3_out(o, g_all, x, gate, gw, wout):
    def body(o_ref, z_ref, x_ref, gate_ref, gw_ref, w_ref, x1_ref):
        og, _, _, _, _ = _gated_norm(_get_heads(o_ref), z_ref[...], gw_ref[...])
        x1_ref[...] = x_ref[...] + gate_ref[...] * _dot(og, w_ref[...])

    return pl.pallas_call(
        body, name="f3_out", grid=(NTX,),
        in_specs=[pl.BlockSpec((HEADS, TM, DH), lambda i: (0, i + 1, 0)), pl.BlockSpec((TM, E), lambda i: (i + 1, 4)),
                  pl.BlockSpec((TM, D), lambda i: (i, 0)), pl.BlockSpec((1, D), lambda i: (0, 0)),
                  pl.BlockSpec((1, E), lambda i: (0, 0)), pl.BlockSpec((E, D), lambda i: (0, 0))],
        out_specs=pl.BlockSpec((TM, D), lambda i: (i, 0)),
        out_shape=_sds((T, D), F32),
        compiler_params=pltpu.CompilerParams(dimension_semantics=("arbitrary",)),
    )(o, g_all, x, gate, gw, wout)


def _pool_layer(x1, tgt, mod1, nw1, fnw, pwin, pgrp, pscale, pwout, pb, pbt, pinv):
    def body(x_ref, t_ref, m_ref, nw_ref, fw_ref, pwin_ref, pgrp_ref, ps_ref, pwout_ref, pb_ref, pbt_ref, pinv_ref,
             dx_ref, gpwin_o, gpgrp_o, gpwout_o, dmod_o, gnw_o, gfw_o, gps_o, loss_o,
             a_pwin, a_pgrp, a_pwout):
        i = pl.program_id(0)

        @pl.when(i == 0)
        def _():
            for ref in (a_pwin, a_pgrp, a_pwout, dmod_o, gnw_o, gfw_o, gps_o, loss_o):
                ref[...] = jnp.zeros_like(ref)

        shift, scale, gate = m_ref[0:1, :], m_ref[1:2, :], m_ref[2:3, :]
        nw, fw, ps = nw_ref[...], fw_ref[...], ps_ref[...]
        x1 = x_ref[...]
        hx, r1, xn, a = _modulated(x1, nw, shift, scale)
        hxb = hx.astype(BF16)
        uz = jnp.concatenate([_dot(hxb, pwin_ref[j]) for j in range(NDEV)], axis=1)
        u, z = uz[:, :E], uz[:, E:]
        pooled, ys = [], []
        for g in range(4):
            ug = u[:, g * PG:(g + 1) * PG]
            pg = _dot01(pb_ref[g], ug) * pinv_ref[g] - ug
            pooled.append(pg.astype(BF16))
            ys.append(_dot(pooled[g], pgrp_ref[g]))
        ycat = jnp.concatenate(ys, axis=1)
        y = ycat * ps
        zs = _sigmoid(z)
        sz = z * zs
        p = (y * sz).astype(BF16)
        out = _dot(p, pwout_ref[...])
        x2 = x1 + gate * out
        r2 = _rstd(x2)
        xn2 = x2 * r2
        diff = xn2 * fw - t_ref[...]
        loss_o[...] += _colsum(diff * diff)
        dyf = diff * (1.0 / D)
        gfw_o[...] += _colsum(dyf * xn2)
        dxn2 = dyf * fw
        dx2 = r2 * (dxn2 - xn2 * jnp.mean(dxn2 * xn2, axis=-1, keepdims=True))
        dgate = _colsum(dx2 * out)
        dout = (dx2 * gate).astype(BF16)
        for j in range(4):
            cs = slice(j * PG, (j + 1) * PG)
            a_pwout[:, cs] += _dot_ta(p, dout[:, cs])
        dp = _dot_tb(dout, pwout_ref[...])
        dy = dp * sz
        dz = dp * y * (zs * (1.0 + z * (1.0 - zs)))
        gps_o[...] += _colsum(dy * ycat)
        dycat = dy * ps
        dus = []
        for g in range(4):
            dyg = dycat[:, g * PG:(g + 1) * PG].astype(BF16)
            a_pgrp[g] += _dot_ta(pooled[g], dyg)
            dpg = _dot_tb(dyg, pgrp_ref[g])
            dus.append(_dot01(pbt_ref[g], dpg * pinv_ref[g]) - dpg)
        duz = jnp.concatenate(dus + [dz], axis=1).astype(BF16)
        dhx = None
        for j in range(NDEV):
            dj = duz[:, j * SH_PWIN:(j + 1) * SH_PWIN]
            a_pwin[j] += _dot_ta(hxb, dj)
            part = _dot_tb(dj, pwin_ref[j])
            dhx = part if dhx is None else dhx + part
        dmod_o[0:1, :] += _colsum(dhx)
        dmod_o[1:2, :] += _colsum(dhx * a)
        dmod_o[2:3, :] += dgate
        da = dhx * (1.0 + scale)
        gnw_o[...] += _colsum(da * xn)
        dxn = da * nw
        dx_ref[...] = dx2 + r1 * (dxn - xn * jnp.mean(dxn * xn, axis=-1, keepdims=True))

        @pl.when(i == NTX - 1)
        def _():
            gpwin_o[...] = a_pwin[...].astype(BF16)
            gpgrp_o[...] = a_pgrp[...].astype(BF16)
            gpwout_o[...] = a_pwout[...].astype(BF16)

    tile = pl.BlockSpec((TM, D), lambda i: (i, 0))
    outs = (_sds((T, D), F32), _sds((NDEV, D, SH_PWIN), BF16), _sds((4, PG, PG), BF16), _sds((E, D), BF16),
            _sds((3, D), F32), _sds((1, D), F32), _sds((1, D), F32), _sds((1, E), F32), _sds((1, D), F32))
    return pl.pallas_call(
        body, name="pool_layer", grid=(NTX,),
        in_specs=[tile, tile] + [VMEM_SPEC] * 10,
        out_specs=[tile] + [VMEM_SPEC] * 8,
        out_shape=outs,
        scratch_shapes=[pltpu.VMEM((NDEV, D, SH_PWIN), F32), pltpu.VMEM((4, PG, PG), F32), pltpu.VMEM((E, D), F32)],
        compiler_params=pltpu.CompilerParams(dimension_semantics=("arbitrary",), vmem_limit_bytes=VMEM_LIMIT),
    )(x1, tgt, mod1, nw1, fnw, pwin, pgrp, pscale, pwout, pb, pbt, pinv)


def _b3_out_bwd(dx1, o, g_all, gate, gw, wout, gpwout):
    def body(dx_ref, o_ref, z_ref, gate_ref, gw_ref, w_ref, gpwout_r, do_ref, dz_ref, gw_o, dgate_o, ggw_o, rpwout_o,
             acc, *rider):
        i = pl.program_id(0)
        bufs, sems = _rider2_split(rider, 1)
        _scatter_rider2(i, NT, 2, ("rows",), (gpwout_r,), (rpwout_o,), bufs, sems)

        @pl.when(i == 0)
        def _():
            acc[...] = jnp.zeros_like(acc)
            dgate_o[...] = jnp.zeros_like(dgate_o)
            ggw_o[...] = jnp.zeros_like(ggw_o)
            do_ref[...] = jnp.zeros_like(do_ref)
            dz_ref[...] = jnp.zeros_like(dz_ref)

        @pl.when(i > 0)
        def _():
            gw = gw_ref[...]
            z = z_ref[...]
            og, r, on, zs, sz = _gated_norm(_get_heads(o_ref), z, gw)
            ogb = og.astype(BF16)
            dx = dx_ref[...]
            dgate_o[...] += _colsum(dx * _dot(ogb, w_ref[...]))
            dy = (dx * gate_ref[...]).astype(BF16)
            for j in range(4):
                cs = slice(j * PG, (j + 1) * PG)
                acc[:, cs] += _dot_ta(ogb, dy[:, cs])
            dog = _dot_tb(dy, w_ref[...])
            dz_ref[...] = (dog * (on * gw) * (zs * (1.0 + z * (1.0 - zs)))).astype(BF16)
            dong = dog * sz
            ggw_o[...] += _colsum(dong * on)
            don = dong * gw
            do = _head_map(lambda dh, nh, rh: rh * (dh - nh * jnp.mean(dh * nh, axis=-1, keepdims=True)), don, on, r)
            _put_heads(do_ref, (), do.astype(BF16))

        @pl.when(i == NT - 1)
        def _():
            gw_o[...] = acc[...].astype(BF16)

    prev = lambda i: (jnp.maximum(i - 1, 0), 0)
    heads = pl.BlockSpec((HEADS, TM, DH), lambda i: (0, i, 0))
    return pl.pallas_call(
        body, name="b3_out_bwd", grid=(NT,),
        in_specs=[pl.BlockSpec((TM, D), prev), heads, pl.BlockSpec((TM, E), lambda i: (i, 4)),
                  VMEM_SPEC, VMEM_SPEC, VMEM_SPEC, HBM_SPEC],
        out_specs=[heads, pl.BlockSpec((TM, E), lambda i: (i, 0)), VMEM_SPEC, VMEM_SPEC, VMEM_SPEC, HBM_SPEC],
        out_shape=(_sds((HEADS, TT, DH), BF16), _sds((TT, E), BF16), _sds((E, D), BF16), _sds((1, D), F32), _sds((1, E), F32),
                   _sds((RS_SLOTS, SH_ROWS, D), BF16)),
        scratch_shapes=[pltpu.VMEM((E, D), F32)] + _rider2_scratch([(SH_ROWS, D)]),
        compiler_params=pltpu.CompilerParams(dimension_semantics=("arbitrary",), vmem_limit_bytes=VMEM_LIMIT),
    )(dx1, o, g_all, gate, gw, wout, gpwout)


def _gla_bwd(p0, p1, v_all, dec, do, mask01, gpgrp):
    nch = TM // CHUNK
    n_steps = HEADS // GLA_HB

    def body(p0_ref, p1_ref, v_ref, dec_ref, do_ref, msk_ref, gpgrp_r, d0_ref, d1_ref, dv_ref, dgl_ref, rpgrp_o,
             ss_sc, dv_sc, ssem, rsem, lsem):
        _scatter_rider(pl.program_id(0), n_steps, ("grp",), (gpgrp_r,), (rpgrp_o,), ssem, rsem, lsem)

        lanes = [(d, hh) for d in (0, 1) for hh in range(GLA_HB)]
        zero = jnp.zeros((len(lanes), DH, DH), F32)
        dgl_ref[...] = jnp.zeros_like(dgl_ref)

        def p_of(d):
            return p1_ref if d else p0_ref

        def scan_step(i, n):
            where = [_scan_tile(i, d == 1) for d in (0, 1)]
            cis = [nch - 1 - n if d else n for d, _ in lanes]
            e_mid, e_rest = _chunk_rows(dec_ref, lanes, cis, where)

            def chunk(arr):
                return jnp.stack([arr[l, ci * CHUNK:(ci + 1) * CHUNK] for l, ci in enumerate(cis)])

            return where, cis, e_mid, e_rest, chunk

        def stacked(i, fn):
            where = [_scan_tile(i, d == 1) for d in (0, 1)]
            return jnp.stack([fn(d, hh, where[d][1]) for d, hh in lanes])

        def fwd_body(i, st):
            v = stacked(i, lambda d, hh, rows: v_ref[hh, rows, :])
            kg = stacked(i, lambda d, hh, rows: p_of(d)[1, hh, rows, :])
            for n in range(nch):
                _, _, e_mid, e_rest, chunk = scan_step(i, n)
                ss_sc[i * nch + n] = st
                st = st * (e_mid * e_rest) + _bdot_tn(chunk(v), chunk(kg)) * e_rest
            return st

        ss_sc[NT * nch] = lax.fori_loop(0, NT, fwd_body, zero)

        def bwd_body(ii, dst):
            i = NT - 1 - ii
            qg, kg = [stacked(i, lambda d, hh, rows, ty=ty: p_of(d)[ty, hh, rows, :]) for ty in range(2)]
            v = stacked(i, lambda d, hh, rows: v_ref[hh, rows, :])
            dob = stacked(i, lambda d, hh, rows: do_ref[hh, rows, :])
            msk = jnp.stack([msk_ref[d] for d, _ in lanes])
            a = (_bdot_nt(qg, kg) * msk).astype(BF16)
            da = (_bdot_nt(dob, v) * msk).astype(BF16)
            dqg = _bdot(da, kg)
            dkg = _bdot_tn(da, qg)
            dv_intra = _bdot_tn(a, dob)
            dv_l, dkg_l, dqg_l = ([[None] * nch for _ in lanes] for _ in range(3))
            for n in range(nch - 1, -1, -1):
                where, cis, e_mid, e_rest, chunk = scan_step(i, n)
                s_c, s_end = ss_sc[i * nch + n], ss_sc[i * nch + n + 1]
                dste = (dst * e_rest).astype(BF16)
                kg_c, v_c, dob_c = chunk(kg), chunk(v), chunk(dob)
                dv_c = chunk(dv_intra) + _bdot_nt(kg_c, dste)
                dkg_c = chunk(dkg) + _bdot(v_c, dste)
                dqg_c = chunk(dqg) + _bdot(dob_c, s_c * e_mid)
                dgl = jnp.sum(s_end * dst, axis=1, keepdims=True)
                for l, ((d, hh), ci) in enumerate(zip(lanes, cis)):
                    dv_l[l][ci], dkg_l[l][ci], dqg_l[l][ci] = dv_c[l], dkg_c[l], dqg_c[l]
                    dgl_ref[d, where[d][0], ci:ci + 1, hh * DH:(hh + 1) * DH] = dgl[l]
                dst = dst * (e_mid * e_rest) + _bdot_tn(dob_c, chunk(qg)) * e_mid
            where = [_scan_tile(i, d == 1) for d in (0, 1)]
            for l, (d, hh) in enumerate(lanes):
                rows = where[d][1]
                d_ref = d1_ref if d else d0_ref
                d_ref[0, hh, rows, :] = jnp.concatenate(dqg_l[l], axis=0).astype(BF16)
                d_ref[1, hh, rows, :] = jnp.concatenate(dkg_l[l], axis=0).astype(BF16)
                dv_sc[d, hh, rows, :] = jnp.concatenate(dv_l[l], axis=0).astype(BF16)
            return dst

        lax.fori_loop(0, NT, bwd_body, zero)
        dv_ref[...] = (dv_sc[0].astype(F32) + dv_sc[1].astype(F32)).astype(BF16)

    quad = pl.BlockSpec((2, GLA_HB, TT, DH), lambda h: (0, h, 0, 0))
    col = pl.BlockSpec((GLA_HB, TT, DH), lambda h: (h, 0, 0))
    chunkv = pl.BlockSpec((2, NT, 8, GLA_HB * DH), lambda h: (0, 0, 0, h))
    outs = (_sds((2, HEADS, TT, DH), BF16), _sds((2, HEADS, TT, DH), BF16), _sds((HEADS, TT, DH), BF16), _sds((2, NT, 8, E), F32),
            _sds((NDEV, 4, SH_GRP, PG), BF16))
    return pl.pallas_call(
        body, name="gla_bwd", grid=(n_steps,),
        in_specs=[quad, quad, col, chunkv, col, pl.BlockSpec((2, TM, TM), lambda h: (0, 0, 0)), HBM_SPEC],
        out_specs=[quad, quad, col, chunkv, HBM_SPEC],
        out_shape=outs,
        scratch_shapes=[pltpu.VMEM((NT * nch + 1, 2 * GLA_HB, DH, DH), F32), pltpu.VMEM((2, GLA_HB, TT, DH), BF16)] + _rider_sems(1),
        compiler_params=pltpu.CompilerParams(dimension_semantics=("arbitrary",), vmem_limit_bytes=VMEM_LIMIT_SCAN),
    )(p0, p1, v_all, dec, do, mask01, gpgrp)


TMB = 128


def _gla_post_bwd(g_all, d0, d1, dgl, dv, dz, lb, cum01, gwout, gpwin):
    nch = TMB // CHUNK

    def body(g_ref, d0_ref, d1_ref, dgl_ref, dv_ref, dz_ref, lb_ref, cum_ref, gwout_r, gpwin_r, dg_ref, dlb_ref, rwout_o, rpwin_o,
             *rider):
        i = pl.program_id(0)
        bufs, sems = _rider2_split(rider, 2)
        _scatter_rider2(i, TT // TMB, 4, ("major", "rows"), (gpwin_r, gwout_r), (rpwin_o, rwout_o), bufs, sems)

        @pl.when(i == 0)
        def _():
            dlb_ref[...] = jnp.zeros_like(dlb_ref)

        half = i & 1
        qpre = g_ref[:, 3 * E:4 * E]
        dqs_sum = None
        dpre = []
        for d, d_ref in ((0, d0_ref), (1, d1_ref)):
            rev = d == 1
            lbd = lb_ref[d:d + 1, :]
            t = _gla_gates(g_ref[:, d * E:(d + 1) * E], qpre, lbd, cum_ref[d, :TMB, :TMB], rev)
            dqs = _get_heads(d_ref, (0,)).astype(F32) * t["e_q"]
            dk = _get_heads(d_ref, (1,)).astype(F32) * t["e_k"]
            dg = t["qs"] * dqs - t["k"] * dk
            dgl8 = dgl_ref[d, 0]
            dgl_rows = [jnp.where(half == 0, dgl8[ci:ci + 1, :], dgl8[nch + ci:nch + ci + 1, :]) for ci in range(nch)]
            dgl_b = jnp.concatenate([jnp.broadcast_to(dgl_rows[ci], (CHUNK, E)) for ci in range(nch)], axis=0)
            pos = lax.broadcasted_iota(jnp.int32, (TMB, E), 0) & (CHUNK - 1)
            dg = dg + jnp.where(pos == (0 if rev else CHUNK - 1), dgl_b, 0.0)
            dlf = _dot01(cum_ref[1 - d, :TMB, :TMB], dg)
            df = dlf / t["f"] - dk
            sig = t["sig"]
            dpre.append((df * (1.0 - lbd) * sig * (1.0 - sig)).astype(BF16))
            dlb_ref[d:d + 1, :] += _colsum(df * (1.0 - sig))
            dqs_sum = dqs if dqs_sum is None else dqs_sum + dqs
            qsig = t["qsig"]
        dqpre = dqs_sum * (DH ** -0.5) * (qsig * (1.0 + qpre * (1.0 - qsig)))
        dg_ref[...] = jnp.concatenate([dpre[0], dpre[1], _get_heads(dv_ref), dqpre.astype(BF16), dz_ref[...]], axis=1)

    quad = pl.BlockSpec((2, HEADS, TMB, DH), lambda i: (0, 0, i, 0))
    tile = pl.BlockSpec((TMB, E), lambda i: (i, 0))
    return pl.pallas_call(
        body, name="gla_post_bwd", grid=(TT // TMB,),
        in_specs=[pl.BlockSpec((TMB, 4 * E), lambda i: (i, 0)), quad, quad,
                  pl.BlockSpec((2, 1, 8, E), lambda i: (0, i // 2, 0, 0)), pl.BlockSpec((HEADS, TMB, DH), lambda i: (0, i, 0)), tile,
                  VMEM_SPEC, VMEM_SPEC, HBM_SPEC, HBM_SPEC],
        out_specs=[pl.BlockSpec((TMB, WIN_COLS), lambda i: (i, 0)), VMEM_SPEC, HBM_SPEC, HBM_SPEC],
        out_shape=(_sds((TT, WIN_COLS), BF16), _sds((2, E), F32), _sds((RS_SLOTS, SH_ROWS, D), BF16), _sds((RS_SLOTS, D, SH_PWIN), BF16)),
        scratch_shapes=_rider2_scratch([(D, SH_PWIN), (SH_ROWS, D)]),
        compiler_params=pltpu.CompilerParams(dimension_semantics=("arbitrary",), vmem_limit_bytes=VMEM_LIMIT),
    )(g_all, d0, d1, dgl, dv, dz, lb, cum01, gwout, gpwin)


def _b1_in_bwd(idx1, ctx, x, dx1, dg, nw, msel, win):
    last_s = NDEV - 1

    def body(idx_ref, ctx_ref, x_ref, dx1_ref, dg_ref, nw_ref, m_ref, w_ref, gx_ref, rwin_o, dmx_o, dmc_o, gnw_o,
             hx_sc, dhx_sc, acc, sbuf, pbuf, psend, precv, isend, irecv, sibsem, lsem):
        del idx_ref
        s, i = pl.program_id(0), pl.program_id(1)
        x, y, cc, idx = _mesh_pos()
        shift, scale = m_ref[0, 0:1, :], m_ref[0, 1:2, :]
        sibling = (x, y, 1 - cc)

        def partial(p):
            return pltpu.make_async_remote_copy(src_ref=sbuf.at[0], dst_ref=pbuf.at[p], send_sem=psend.at[p], recv_sem=precv.at[p],
                                                device_id=sibling, device_id_type=MESH)

        def chip_sum(p):
            return pltpu.make_async_remote_copy(src_ref=sbuf.at[1], dst_ref=rwin_o.at[2 + p], send_sem=isend.at[p], recv_sem=irecv.at[p],
                                                device_id=_peer(x, y, cc, 2 * (p + 1)), device_id_type=MESH)

        to_sibling = pltpu.make_async_remote_copy(src_ref=sbuf.at[0], dst_ref=rwin_o.at[1], send_sem=sibsem.at[0], recv_sem=sibsem.at[1],
                                                  device_id=sibling, device_id_type=MESH)
        own = pltpu.make_async_copy(sbuf.at[1], rwin_o.at[0], lsem)

        @pl.when((s == 0) & (i == 0))
        def _():
            for ref in (dmx_o, dmc_o, gnw_o):
                ref[...] = jnp.zeros_like(ref)

        @pl.when(s == 0)
        def _():
            hx, _, _, _ = _modulated(_ctx_or_x(i, ctx_ref, x_ref), nw_ref[...], shift, scale)
            hx_sc[i] = hx.astype(BF16)

        @pl.when(i == 0)
        def _():
            acc[...] = jnp.zeros_like(acc)

        dgb = dg_ref[...]
        hxb = hx_sc[i]
        for lo, hi in ((0, 256), (256, 512), (512, SH_WIN)):
            acc[:, lo:hi] += _dot_ta(hxb, dgb[:, lo:hi])
        part = _dot_tb(dgb, w_ref[...])

        @pl.when(s == 0)
        def _():
            dhx_sc[i] = part

        @pl.when(s > 0)
        def _():
            dhx_sc[i] += part

        for p in (2, 1, 0):
            @pl.when((i == NT - 1) & (s == 2 * (2 - p)))
            def _(p=p):
                if p < 2:
                    partial(p + 1).wait_send()
                sbuf[0] = acc[...].astype(BF16)
                partial(p).start()

            @pl.when((i == NT - 1) & (s == 2 * (2 - p) + 1))
            def _(p=p):
                if p < 2:
                    chip_sum(p + 1).wait_send()
                partial(p).wait_recv()
                sbuf[1] = (acc[...] + pbuf[p].astype(F32)).astype(BF16)
                chip_sum(p).start()

        @pl.when((i == NT - 1) & (s == last_s - 1))
        def _():
            partial(0).wait_send()
            sbuf[0] = acc[...].astype(BF16)
            to_sibling.start()

        @pl.when((i == NT - 1) & (s == last_s))
        def _():
            chip_sum(0).wait_send()
            sbuf[1] = acc[...].astype(BF16)
            own.start()

        @pl.when(s == last_s)
        def _():
            nw = nw_ref[...]
            _, r, xn, a = _modulated(_ctx_or_x(i, ctx_ref, x_ref), nw, shift, scale)
            dhx = dhx_sc[i]
            dsh, dsc = _colsum(dhx), _colsum(dhx * a)
            da = dhx * (1.0 + scale)
            gnw_o[...] += _colsum(da * xn)
            dxn = da * nw
            gx_ref[...] = dx1_ref[...] + r * (dxn - xn * jnp.mean(dxn * xn, axis=-1, keepdims=True))

            @pl.when(i == 0)
            def _():
                dmc_o[0:1, :] += dsh
                dmc_o[1:2, :] += dsc

            @pl.when(i > 0)
            def _():
                dmx_o[0:1, :] += dsh
                dmx_o[1:2, :] += dsc

        @pl.when((i == NT - 1) & (s == last_s))
        def _():
            to_sibling.wait_send()
            to_sibling.wait_recv()
            for p in range(3):
                chip_sum(p).wait_recv()
            own.wait()

    grid_spec = pltpu.PrefetchScalarGridSpec(
        num_scalar_prefetch=1, grid=(NDEV, NT),
        in_specs=[VMEM_SPEC, pl.BlockSpec((TM, D), lambda s, i, ix: (jnp.maximum(i - 1, 0), 0)),
                  pl.BlockSpec((TM, D), lambda s, i, ix: (jnp.maximum(i - 1, 0), 0)),
                  pl.BlockSpec((TM, SH_WIN), lambda s, i, ix: (i, ix[0] ^ (last_s - s))), VMEM_SPEC,
                  pl.BlockSpec((1, 2, D), lambda s, i, ix: (jnp.minimum(i, 1), 0, 0)),
                  pl.BlockSpec((D, SH_WIN), lambda s, i, ix: (0, ix[0] ^ (last_s - s)))],
        out_specs=[pl.BlockSpec((TM, D), lambda s, i, ix: (jnp.where(s == last_s, jnp.maximum(i - 1, 0), 0), 0)),
                   HBM_SPEC, VMEM_SPEC, VMEM_SPEC, VMEM_SPEC],
        scratch_shapes=[pltpu.VMEM((NT, TM, D), BF16), pltpu.VMEM((NT, TM, D), F32), pltpu.VMEM((D, SH_WIN), F32),
                        pltpu.VMEM((2, D, SH_WIN), BF16), pltpu.VMEM((3, D, SH_WIN), BF16),
                        pltpu.SemaphoreType.DMA((3,)), pltpu.SemaphoreType.DMA((3,)), pltpu.SemaphoreType.DMA((3,)),
                        pltpu.SemaphoreType.DMA((3,)), pltpu.SemaphoreType.DMA((2,)), pltpu.SemaphoreType.DMA])
    return pl.pallas_call(
        body, name="b1_in_bwd", grid_spec=grid_spec,
        out_shape=(_sds((T, D), F32), _sds((RS_SLOTS, D, SH_WIN), BF16), _sds((2, D), F32), _sds((2, D), F32), _sds((1, D), F32)),
        compiler_params=pltpu.CompilerParams(dimension_semantics=("arbitrary", "arbitrary"), vmem_limit_bytes=VMEM_LIMIT),
    )(idx1, ctx, x, dx1, dg, nw, msel, win)


def _reduce_small(pd, pv, cg, c_ctx, ada_w0):
    n_arr = 3

    def body(pd_r, pv_r, cg_r, cctx_r, ada_r, gada_o, gadab_o, gcctx_o, pvsum_o, loss_o,
             pd_all, pv_all, dsc_all, dsc_mine, ssem, rsem):
        x, y, cc, idx = _mesh_pos()
        srcs = [pd_r, pv_r, dsc_mine]
        dsts = [pd_all.at[idx], pv_all.at[idx], dsc_all.at[idx]]

        def remote(a, k):
            return pltpu.make_async_remote_copy(src_ref=srcs[a], dst_ref=dsts[a], send_sem=ssem.at[a, k], recv_sem=rsem.at[a, k],
                                                device_id=_peer(x, y, cc, k), device_id_type=MESH)

        first = [remote(a, k) for k in range(1, NDEV) for a in (0, 1)]
        for cp in first:
            cp.start()
        pd_all[idx] = pd_r[...]
        pv_all[idx] = pv_r[...]
        for k in range(1, NDEV):
            remote(0, k).wait_recv()
            remote(1, k).wait_recv()
        mine = [pd_all[s, :, pl.ds(idx, 1), :] for s in range(NDEV)]
        dmc = functools.reduce(lambda u, v: u + v, [m[2] for m in mine])
        rows = _stack_rows([cg_r[i] for i in range(NDEV)] + [cctx_r[...]])
        sc = (rows * _sigmoid(rows)).astype(BF16)
        gada_o[0] = _dot_ta(sc, _stack_rows([m[0] for m in mine] + [dmc]))
        gada_o[1] = _dot_ta(sc, _stack_rows([m[1] for m in mine]))
        dsc_mine[...] = _dot_tb(jnp.broadcast_to(dmc, (8, SH_ADA)), ada_r[...])[0:1, :]
        dsc_all[idx] = dsc_mine[...]
        second = [remote(2, k) for k in range(1, NDEV)]
        for cp in second:
            cp.start()
        tot = [functools.reduce(lambda u, v: u + v, [pd_all[s, l] for s in range(NDEV)]) for l in range(3)]
        gadab_o[0] = tot[0] + tot[2]
        gadab_o[1] = tot[1]
        pvs = functools.reduce(lambda u, v: u + v, [pv_all[s] for s in range(NDEV)])
        pvsum_o[...] = pvs
        loss_o[...] = jnp.broadcast_to(jnp.sum(pvs[:, PV_LOSS:PV_LOSS + D], axis=-1, keepdims=True) * (0.5 / D), (1, 128))
        for k in range(1, NDEV):
            remote(2, k).wait_recv()
        dsc = functools.reduce(lambda u, v: u + v, [dsc_all[s] for s in range(NDEV)])
        cx = cctx_r[...]
        sx = _sigmoid(cx)
        gcctx_o[...] = dsc * (sx * (1.0 + cx * (1.0 - sx)))
        for cp in first + second:
            cp.wait_send()

    outs = (_sds((2, D, SH_ADA), F32), _sds((2, NDEV, SH_ADA), F32), _sds((1, D), F32), _sds((1, PV_LEN), F32), _sds((1, 128), F32))
    return pl.pallas_call(
        body, name="reduce_small", out_shape=outs,
        in_specs=[VMEM_SPEC] * 5, out_specs=[VMEM_SPEC] * 5,
        scratch_shapes=[
            pltpu.VMEM((NDEV, 3, NDEV, SH_ADA), F32), pltpu.VMEM((NDEV, 1, PV_LEN), F32), pltpu.VMEM((NDEV, 1, D), F32),
            pltpu.VMEM((1, D), F32),
            pltpu.SemaphoreType.DMA((n_arr, NDEV)), pltpu.SemaphoreType.DMA((n_arr, NDEV)),
        ],
        compiler_params=pltpu.CompilerParams(vmem_limit_bytes=VMEM_LIMIT),
    )(pd, pv, cg, c_ctx, ada_w0)


PV_NW, PV_GNORM, PV_FINAL, PV_LB, PV_PSCALE, PV_LOSS, PV_LEN = 0, 2 * D, 3 * D, 4 * D, 6 * D, 7 * D, 8 * D


def _adamw(w, g, m, v):
    m = ADAM_B1 * m + (1.0 - ADAM_B1) * g
    v = ADAM_B2 * v + (1.0 - ADAM_B2) * (g * g)
    m_hat = m / (1.0 - ADAM_B1 ** ADAM_STEP)
    v_hat = v / (1.0 - ADAM_B2 ** ADAM_STEP)
    delta = -ADAM_LR * (m_hat / (jnp.sqrt(v_hat) + ADAM_EPS) + ADAM_WD * w)
    return delta, m, v


ADAM_STEPS = 8


def _adam_all(sharded, dense, small, lb_idx, lbv):
    ns, nd, nsm = len(sharded), len(dense), len(small)

    def body(*refs):
        it = iter(refs)
        sh_in = [[next(it) for _ in range(4)] for _ in range(ns)]
        de_in = [[next(it) for _ in range(4)] for _ in range(nd)]
        sm_in = [[next(it) for _ in range(4)] for _ in range(nsm)]
        lb_r = next(it)
        sh_out = [[next(it) for _ in range(4)] for _ in range(ns)]
        de_out = [[next(it) for _ in range(3)] for _ in range(nd)]
        sm_out = [[next(it) for _ in range(4)] for _ in range(nsm)]
        for (p, w, m, v), outs in zip(sh_in, sh_out):
            g = p[0].astype(F32)
            for s in range(1, p.shape[0]):
                g = g + p[s].astype(F32)
            d, mn, vn = _adamw(w[...], g, m[...], v[...])
            outs[0][...], outs[1][...], outs[2][...], outs[3][...] = g, d, mn, vn
        for (g, w, m, v), outs in zip(de_in, de_out):
            d, mn, vn = _adamw(w[...], g[...], m[...], v[...])
            outs[0][...], outs[1][...], outs[2][...] = d, mn, vn

        @pl.when(pl.program_id(0) == 0)
        def _():
            for j, ((g, w, m, v), outs) in enumerate(zip(sm_in, sm_out)):
                gj = g[...]
                if j == lb_idx:
                    gj = gj * lb_r[...] * (1.0 - lb_r[...])
                d, mn, vn = _adamw(w[...], gj, m[...], v[...])
                outs[0][...], outs[1][...], outs[2][...], outs[3][...] = gj, d, mn, vn

    def tile(a):
        return pl.BlockSpec((a.shape[0] // ADAM_STEPS, a.shape[1]), lambda i: (i, 0))

    in_specs, out_specs, out_shape, args = [], [], [], []
    for p, w, m, v in sharded:
        in_specs += [pl.BlockSpec((p.shape[0], p.shape[1] // ADAM_STEPS, p.shape[2]), lambda i: (0, i, 0))] + [tile(w)] * 3
        args += [p, w, m, v]
    for g, w, m, v in dense:
        in_specs += [tile(w)] * 4
        args += [g, w, m, v]
    for g, w, m, v in small:
        in_specs += [VMEM_SPEC] * 4
        args += [g, w, m, v]
    in_specs.append(VMEM_SPEC)
    args.append(lbv)
    for _, w, _, _ in sharded:
        out_specs += [tile(w)] * 4
        out_shape += [_sds(w.shape, F32)] * 4
    for _, w, _, _ in dense:
        out_specs += [tile(w)] * 3
        out_shape += [_sds(w.shape, F32)] * 3
    for _, w, _, _ in small:
        out_specs += [VMEM_SPEC] * 4
        out_shape += [_sds(w.shape, F32)] * 4
    res = pl.pallas_call(body, name="adam_all", grid=(ADAM_STEPS,), in_specs=in_specs, out_specs=out_specs, out_shape=tuple(out_shape),
                         compiler_params=pltpu.CompilerParams(dimension_semantics=("arbitrary",), vmem_limit_bytes=VMEM_LIMIT))(*args)
    it = iter(res)
    return ([tuple(next(it) for _ in range(4)) for _ in range(ns)], [tuple(next(it) for _ in range(3)) for _ in range(nd)],
            [tuple(next(it) for _ in range(4)) for _ in range(nsm)])


def kernel(x, c, ctx, c_ctx, ada_w, ada_b, norm_w, hgrn_w_in, hgrn_lb_logits, hgrn_gnorm_w, hgrn_w_out, pool_w_in, pool_w_grp, pool_scale, pool_w_out, final_norm_w, loss_target, m_c_ctx, m_ada_w, m_ada_b, m_norm_w, m_hgrn_w_in, m_hgrn_lb_logits, m_hgrn_gnorm_w, m_hgrn_w_out, m_pool_w_in, m_pool_w_grp, m_pool_scale, m_pool_w_out, m_final_norm_w, v_c_ctx, v_ada_w, v_ada_b, v_norm_w, v_hgrn_w_in, v_hgrn_lb_logits, v_hgrn_gnorm_w, v_hgrn_w_out, v_pool_w_in, v_pool_w_grp, v_pool_scale, v_pool_w_out, v_final_norm_w):
    idx = 4 * lax.axis_index("x") + 2 * lax.axis_index("y") + lax.axis_index("c")
    cctx2 = c_ctx.reshape(1, D)
    cum01, mask01 = _gla_consts()
    pb, pbt, pinv = _pool_consts()

    idx1 = idx.reshape(1).astype(jnp.int32)
    nw0, nw1 = norm_w[0:1], norm_w[1:2]
    fnw = final_norm_w.reshape(1, D)
    g_all, win, s_wout, s_pwin, s_pgrp, s_pwout, lbl_g, ps_g, cg, mod0, mod1, modc = _f1_gather_matmul(
        idx1, ctx[0], x[0], nw0, hgrn_w_in[0], hgrn_w_out[0], pool_w_in[0], pool_w_grp[0], pool_w_out[0], hgrn_lb_logits[0],
        pool_scale, c, cctx2, ada_w, ada_b)
    lb = jax.nn.sigmoid(jnp.transpose(lbl_g, (1, 0, 2)).reshape(2, E))
    pscale = ps_g.reshape(1, E)
    msel = jnp.stack([modc[:2], mod0[:2]])
    p0, p1, v_all, dec, pwin, pgrp = _gla_prep(g_all, lb, cum01, s_pwin, s_pgrp)
    o, wout, pwout = _gla_fwd(p0, p1, v_all, dec, mask01, s_wout, s_pwout)
    x1 = _f3_out(o, g_all, x[0], mod0[2:3], hgrn_gnorm_w, wout)
    dx1, gpwin, gpgrp, gpwout, dmod1, gnw1, gfw, gps, lossv = _pool_layer(
        x1, loss_target[0], mod1, nw1, fnw, pwin, pgrp, pscale, pwout, pb, pbt, pinv)
    do, dz, gwout, dgate0, ggw, rpwout = _b3_out_bwd(dx1, o, g_all, mod0[2:3], hgrn_gnorm_w, wout, gpwout)
    d0, d1, dv, dgl, rpgrp = _gla_bwd(p0, p1, v_all, dec, do, mask01, gpgrp)
    dg, dlb, rwout, rpwin = _gla_post_bwd(g_all, d0, d1, dgl, dv, dz, lb, cum01, gwout, gpwin)
    grad_x, rwin, dmx, dmc, gnw0 = _b1_in_bwd(idx1, ctx[0], x[0], dx1, dg, nw0, msel, win)

    dmod0 = jnp.concatenate([dmx, dgate0], axis=0)
    dmodc = jnp.concatenate([dmc, jnp.zeros((1, D), F32)], axis=0)
    pd = jnp.stack([dmod0, dmod1, dmodc]).reshape(3, NDEV, SH_ADA)
    pv = jnp.concatenate([gnw0, gnw1, ggw, gfw, dlb.reshape(1, 2 * E), gps, lossv], axis=1)
    g_ada, g_adab, g_cctx, pvsum, loss128 = _reduce_small(pd, pv, cg, cctx2, ada_w[0])

    g2 = (4 * SH_GRP, PG)
    sharded_names = ["hgrn_w_in", "hgrn_w_out", "pool_w_in", "pool_w_grp", "pool_w_out"]
    sharded = [(rwin, hgrn_w_in[0], m_hgrn_w_in[0], v_hgrn_w_in[0]),
               (rwout, hgrn_w_out[0], m_hgrn_w_out[0], v_hgrn_w_out[0]),
               (rpwin, pool_w_in[0], m_pool_w_in[0], v_pool_w_in[0]),
               (rpgrp.reshape((NDEV,) + g2), pool_w_grp[0].reshape(g2), m_pool_w_grp[0].reshape(g2), v_pool_w_grp[0].reshape(g2)),
               (rpwout, pool_w_out[0], m_pool_w_out[0], v_pool_w_out[0])]
    a2 = (2 * D, SH_ADA)
    g_ada2 = g_ada.reshape(a2)
    dense = [(g_ada2, ada_w.reshape(a2), m_ada_w.reshape(a2), v_ada_w.reshape(a2))]
    lb_me = lax.dynamic_slice_in_dim(lb, idx * DH, DH, axis=1)
    small_names = ["c_ctx", "ada_b", "norm_w", "hgrn_lb_logits", "hgrn_gnorm_w", "pool_scale", "final_norm_w"]
    small = [(g_cctx, cctx2, m_c_ctx.reshape(1, D), v_c_ctx.reshape(1, D)),
             (g_adab.reshape(2, 3 * D), ada_b, m_ada_b, v_ada_b),
             (pvsum[:, PV_NW:PV_NW + 2 * D].reshape(2, D), norm_w, m_norm_w, v_norm_w),
             (lax.dynamic_slice_in_dim(pvsum[:, PV_LB:PV_LB + 2 * E].reshape(2, E), idx * DH, DH, axis=1),
              hgrn_lb_logits[0], m_hgrn_lb_logits[0], v_hgrn_lb_logits[0]),
             (pvsum[:, PV_GNORM:PV_GNORM + E], hgrn_gnorm_w, m_hgrn_gnorm_w, v_hgrn_gnorm_w),
             (lax.dynamic_slice_in_dim(pvsum[:, PV_PSCALE:PV_PSCALE + E], idx * DH, DH, axis=1), pool_scale, m_pool_scale, v_pool_scale),
             (pvsum[:, PV_FINAL:PV_FINAL + D], fnw, m_final_norm_w.reshape(1, D), v_final_norm_w.reshape(1, D))]
    r_sharded, r_dense, r_small = _adam_all(sharded, dense, small, 3, lb_me)
    out = dict(zip(sharded_names, r_sharded))
    out["ada_w"] = (g_ada2,) + r_dense[0]
    out.update(zip(small_names, r_small))

    shapes = {"c_ctx": (D,), "ada_w": (2, D, SH_ADA), "ada_b": (2, 3 * D), "norm_w": (2, D), "hgrn_w_in": (1, D, SH_WIN),
              "hgrn_lb_logits": (1, 2, DH), "hgrn_gnorm_w": (1, E), "hgrn_w_out": (1, SH_ROWS, D), "pool_w_in": (1, D, SH_PWIN),
              "pool_w_grp": (1, 4, SH_GRP, PG), "pool_scale": (1, DH), "pool_w_out": (1, SH_ROWS, D), "final_norm_w": (D,)}
    order = ["c_ctx", "ada_w", "ada_b", "norm_w", "hgrn_w_in", "hgrn_lb_logits", "hgrn_gnorm_w", "hgrn_w_out", "pool_w_in",
             "pool_w_grp", "pool_scale", "pool_w_out", "final_norm_w"]
    flat = [out[name][q].reshape(shapes[name]) for q in range(4) for name in order]
    return (loss128[0, 0], grad_x[None], *flat)
```

```python
import functools

import numpy as np
import jax
import jax.numpy as jnp
from jax import lax
from jax.experimental import pallas as pl
from jax.experimental.pallas import tpu as pltpu

F32 = jnp.float32
BF16 = jnp.bfloat16

D = 1024
E = 1024
HEADS = 8
DH = 128
CHUNK = 64
T = 2048
TC = 256
TT = T + TC
TM = 256
NT = TT // TM
NTX = T // TM
NDEV = 8
GRID_W = 64
POOL_WINDOWS = (2, 4, 8, 16)
PG = 256
EPS = 1e-6
WIN_COLS = 5 * E
SH_WIN = WIN_COLS // NDEV
SH_PWIN = 2 * E // NDEV
SH_ROWS = E // NDEV
SH_GRP = PG // NDEV
SH_ADA = 3 * D // NDEV
VMEM_LIMIT = 56 * 1024 * 1024

ADAM_LR, ADAM_B1, ADAM_B2, ADAM_EPS, ADAM_WD, ADAM_STEP = 0.001, 0.9, 0.999, 1e-08, 0.01, 10

MESH = pl.DeviceIdType.MESH
VMEM_SPEC = pl.BlockSpec(memory_space=pltpu.VMEM)
HBM_SPEC = pl.BlockSpec(memory_space=pltpu.HBM)
ANY_SPEC = pl.BlockSpec(memory_space=pl.ANY)


def _sds(shape, dtype):
    return jax.ShapeDtypeStruct(shape, dtype)


def _bf(a):
    return a if a.dtype == BF16 else a.astype(BF16)


def _dot(a, b):
    return lax.dot_general(_bf(a), _bf(b), (((1,), (0,)), ((), ())), preferred_element_type=F32)


def _dot_tb(a, b):
    return lax.dot_general(_bf(a), _bf(b), (((1,), (1,)), ((), ())), preferred_element_type=F32)


def _dot_ta(a, b):
    return lax.dot_general(_bf(a), _bf(b), (((0,), (0,)), ((), ())), preferred_element_type=F32)


def _bdot(a, b):
    return lax.dot_general(_bf(a), _bf(b), (((2,), (1,)), ((0,), (0,))), preferred_element_type=F32)


def _bdot_nt(a, b):
    return lax.dot_general(_bf(a), _bf(b), (((2,), (2,)), ((0,), (0,))), preferred_element_type=F32)


def _bdot_tn(a, b):
    return lax.dot_general(_bf(a), _bf(b), (((1,), (1,)), ((0,), (0,))), preferred_element_type=F32)


def _dot01(m01, x):
    hi = x.astype(BF16)
    lo = (x - hi.astype(F32)).astype(BF16)
    return _dot(m01, hi) + _dot(m01, lo)


def _rstd(x):
    return lax.rsqrt(jnp.mean(x * x, axis=-1, keepdims=True) + EPS)


def _sigmoid(x):
    return jax.nn.sigmoid(x)


def _colsum(a):
    return jnp.sum(a, axis=0, keepdims=True)


def _stack_rows(rows):
    n = rows[0].shape[-1]
    rid = lax.broadcasted_iota(jnp.int32, (16, n), 0)
    out = jnp.zeros((16, n), F32)
    for i, r in enumerate(rows):
        out = jnp.where(rid == i, r, out)
    return out


def _head_map(fn, *arrs):
    outs = [fn(*[a[:, h * DH:(h + 1) * DH] for a in arrs]) for h in range(HEADS)]
    return jnp.concatenate(outs, axis=1)


def _gla_consts():
    r = np.arange(TM)[:, None]
    c = np.arange(TM)[None, :]
    same = (r // CHUNK) == (c // CHUNK)
    tril = same & (c <= r)
    triu = same & (c >= r)
    m = np.stack([tril, triu]).astype(np.float32)
    return jnp.asarray(m, BF16), jnp.asarray(m, F32)


def _pool_consts():
    r = np.arange(TM)[:, None]
    c = np.arange(TM)[None, :]
    same = (r // GRID_W) == (c // GRID_W)
    rp, cp = r % GRID_W, c % GRID_W
    bs, inv = [], []
    for w in POOL_WINDOWS:
        lo = np.clip(rp - w // 2, 0, GRID_W)
        hi = np.clip(rp - w // 2 + w, 0, GRID_W)
        bs.append(same & (cp >= lo) & (cp < hi))
        inv.append(1.0 / (hi - lo).astype(np.float32))
    b = np.stack(bs).astype(np.float32)
    bt = np.transpose(b, (0, 2, 1))
    return jnp.asarray(b, BF16), jnp.asarray(bt, BF16), jnp.asarray(np.stack(inv), F32)


def _mesh_pos():
    x, y, c = lax.axis_index("x"), lax.axis_index("y"), lax.axis_index("c")
    return x, y, c, 4 * x + 2 * y + c


def _peer(x, y, c, k):
    return (x ^ ((k >> 2) & 1), y ^ ((k >> 1) & 1), c ^ (k & 1))


def _small_gathers(refs, ssem, rsem):
    lb_r, ps_r, c_r, cctx_r, ada_r, adab_r, lb_o, ps_o, cg_o, mod_o, lb_out, ps_out, cg_out, mod0_o, mod1_o, modc_o = refs
    x, y, cc, idx = _mesh_pos()
    srcs = [lb_r, ps_r, c_r, mod_o.at[idx]]
    mine = [lb_o.at[idx], ps_o.at[idx], cg_o.at[idx], mod_o.at[idx]]

    def remote(a, k):
        return pltpu.make_async_remote_copy(src_ref=srcs[a], dst_ref=mine[a], send_sem=ssem.at[a, k], recv_sem=rsem.at[a, k],
                                            device_id=_peer(x, y, cc, k), device_id_type=MESH)

    first = [remote(a, k) for k in range(1, NDEV) for a in (2, 0, 1)]
    for cp in first:
        cp.start()
    lb_o[idx] = lb_r[...]
    ps_o[idx] = ps_r[...]
    cg_o[idx] = c_r[...]
    for k in range(1, NDEV):
        remote(2, k).wait_recv()
    rows = _stack_rows([cg_o[i] for i in range(NDEV)] + [cctx_r[...]])
    sc = rows * _sigmoid(rows)
    for l in range(2):
        mod_o[idx, l] = _dot(sc, ada_r[l])
    second = [remote(3, k) for k in range(1, NDEV)]
    for cp in second:
        cp.start()
    for k in range(1, NDEV):
        remote(3, k).wait_recv()

    def mod_rows(l, row):
        full = jnp.concatenate([mod_o[s, l, row, :] for s in range(NDEV)], axis=1) + adab_r[l:l + 1, :]
        return [full[:, j * D:(j + 1) * D] for j in range(3)]

    me = pl.ds(idx, 1)
    for out, parts in ((mod0_o, mod_rows(0, me)), (mod1_o, mod_rows(1, me)), (modc_o, mod_rows(0, slice(NDEV, NDEV + 1)))):
        for j in range(3):
            out[j:j + 1, :] = parts[j]
    for cp in first + second:
        cp.wait_send()
    for k in range(1, NDEV):
        for a in (0, 1):
            remote(a, k).wait_recv()
    lb_out[...] = lb_o[...]
    ps_out[...] = ps_o[...]
    cg_out[...] = cg_o[...]


def _gather_order(s):
    if isinstance(s, int):
        return (0, 1, 2, 4, 3, 5, 6, 7)[s]
    return s + (s == 3).astype(jnp.int32) - (s == 4).astype(jnp.int32)


GATHER_ISSUE = (1, 2, 4, 3, 5, 6, 7)
GATHER_ICI = (2, 4, 6)
GATHER_DIRECT = (1,) + GATHER_ICI
GLA_HB = 2
RS_SLOTS = 5


def _shard_of(kind, ref, i):
    if kind == "rows":
        return ref.at[pl.ds(pl.multiple_of(i * SH_ROWS, SH_ROWS), SH_ROWS), :]
    if kind == "major":
        return ref.at[i]
    assert kind == "grp"
    return ref.at[:, pl.ds(pl.multiple_of(i * SH_GRP, SH_GRP), SH_GRP), :]


def _gather_rider(step, n_steps, forward_at, kinds, srcs, outs, ssem, rsem, lsem):
    x, y, cc, idx = _mesh_pos()
    arrays = range(len(kinds))
    mine = [_shard_of(kinds[a], outs[a], idx) for a in arrays]

    def remote(a, k):
        return pltpu.make_async_remote_copy(src_ref=srcs[a], dst_ref=mine[a], send_sem=ssem.at[a, k], recv_sem=rsem.at[a, k],
                                            device_id=_peer(x, y, cc, k), device_id_type=MESH)

    def forward(a, k):
        blk = _shard_of(kinds[a], outs[a], idx ^ k)
        return pltpu.make_async_remote_copy(src_ref=blk, dst_ref=blk, send_sem=ssem.at[a, k ^ 1], recv_sem=rsem.at[a, k ^ 1],
                                            device_id=(x, y, 1 - cc), device_id_type=MESH)

    copies = [remote(a, k) for k in GATHER_DIRECT for a in arrays]
    passed = [forward(a, k) for k in GATHER_ICI for a in arrays]
    local = [pltpu.make_async_copy(srcs[a], mine[a], lsem.at[a]) for a in arrays]

    @pl.when(step == 0)
    def _():
        for cp in copies + local:
            cp.start()

    @pl.when(step == forward_at)
    def _():
        for k in GATHER_ICI:
            for a in arrays:
                remote(a, k).wait_recv()
                forward(a, k).start()

    @pl.when(step == n_steps - 1)
    def _():
        for cp in copies + passed:
            cp.wait_send()
        for a in arrays:
            remote(a, 1).wait_recv()
        for cp in passed:
            cp.wait_recv()
        for cp in local:
            cp.wait()


def _scatter_rider(step, n_steps, kinds, grads, slots, ssem, rsem, lsem):
    x, y, cc, idx = _mesh_pos()
    arrays = range(len(kinds))
    dsts = [slots[a].at[idx] for a in arrays]

    def remote(a, k):
        px, py, pc = _peer(x, y, cc, k)
        return pltpu.make_async_remote_copy(src_ref=_shard_of(kinds[a], grads[a], 4 * px + 2 * py + pc), dst_ref=dsts[a],
                                            send_sem=ssem.at[a, k], recv_sem=rsem.at[a, k], device_id=(px, py, pc), device_id_type=MESH)

    copies = [remote(a, k) for k in GATHER_ISSUE for a in arrays]
    local = [pltpu.make_async_copy(_shard_of(kinds[a], grads[a], idx), dsts[a], lsem.at[a]) for a in arrays]

    @pl.when(step == 0)
    def _():
        for cp in copies + local:
            cp.start()

    @pl.when(step == n_steps - 1)
    def _():
        for cp in copies:
            cp.wait_send()
        for cp in copies:
            cp.wait_recv()
        for cp in local:
            cp.wait()


def _rider_sems(n):
    return [pltpu.SemaphoreType.DMA((n, NDEV)), pltpu.SemaphoreType.DMA((n, NDEV)), pltpu.SemaphoreType.DMA((n,))]


def _scatter_rider2(step, n_steps, add_at, kinds, grads, slots, bufs, sems):
    x, y, cc, idx = _mesh_pos()
    sibling = (x, y, 1 - cc)
    arrays = range(len(kinds))
    psend, precv, isend, irecv, lown, sibsem, lself = sems

    def mine(a, i):
        return _shard_of(kinds[a], grads[a], i)

    def partial(a, p):
        return pltpu.make_async_remote_copy(src_ref=mine(a, idx ^ (2 * (p + 1)) ^ 1), dst_ref=bufs[a][1].at[p], send_sem=psend.at[a, p],
                                            recv_sem=precv.at[a, p], device_id=sibling, device_id_type=MESH)

    def load(a, p):
        return pltpu.make_async_copy(mine(a, idx ^ (2 * (p + 1))), bufs[a][0].at[p], lown.at[a, p])

    def chip_sum(a, p):
        return pltpu.make_async_remote_copy(src_ref=bufs[a][0].at[p], dst_ref=slots[a].at[2 + p], send_sem=isend.at[a, p],
                                            recv_sem=irecv.at[a, p], device_id=_peer(x, y, cc, 2 * (p + 1)), device_id_type=MESH)

    def to_sibling(a):
        return pltpu.make_async_remote_copy(src_ref=mine(a, idx ^ 1), dst_ref=slots[a].at[1], send_sem=sibsem.at[a, 0],
                                            recv_sem=sibsem.at[a, 1], device_id=sibling, device_id_type=MESH)

    def own(a):
        return pltpu.make_async_copy(mine(a, idx), slots[a].at[0], lself.at[a, 0])

    @pl.when(step == 0)
    def _():
        for a in arrays:
            for p in range(3):
                partial(a, p).start()
                load(a, p).start()
            to_sibling(a).start()
            own(a).start()

    @pl.when(step == add_at)
    def _():
        for a in arrays:
            for p in range(3):
                partial(a, p).wait_recv()
                load(a, p).wait()
                bufs[a][0][p] = (bufs[a][0][p].astype(F32) + bufs[a][1][p].astype(F32)).astype(BF16)
                chip_sum(a, p).start()

    @pl.when(step == n_steps - 1)
    def _():
        for a in arrays:
            for p in range(3):
                partial(a, p).wait_send()
                chip_sum(a, p).wait_send()
                chip_sum(a, p).wait_recv()
            to_sibling(a).wait_send()
            to_sibling(a).wait_recv()
            own(a).wait()


def _rider2_scratch(blocks):
    n = len(blocks)
    bufs = [pltpu.VMEM((3,) + tuple(b), BF16) for b in blocks for _ in range(2)]
    return bufs + [pltpu.SemaphoreType.DMA((n, 3)) for _ in range(5)] + [pltpu.SemaphoreType.DMA((n, 2)), pltpu.SemaphoreType.DMA((n, 1))]


def _rider2_split(refs, n):
    refs = list(refs)
    return [tuple(refs[2 * a:2 * a + 2]) for a in range(n)], tuple(refs[2 * n:2 * n + 7])


def _modulated(x, nw, shift, scale):
    r = _rstd(x)
    xn = x * r
    a = xn * nw
    return a * (1.0 + scale) + shift, r, xn, a


def _ctx_or_x(i, ctx_ref, x_ref):
    return jnp.where(i == 0, ctx_ref[...], x_ref[...])


def _f1_gather_matmul(idx1, ctx, x, nw, w_in, w_out, pw_in, pgrp, pw_out, lb_l, pscale, c, c_ctx, ada_w, ada_b):
    def body(idx_ref, ctx_ref, x_ref, nw_ref, win_r, wout_r, pwin_r, pgrp_r, pwout_r, lb_r, ps_r, c_r, cctx_r, ada_r, adab_r,
             g_ref, win_o, s_wout, s_pwin, s_pgrp, s_pwout, lb_o, ps_o, cg_o, mod0_o, mod1_o, modc_o,
             wslot, hx_sc, lb_g, ps_g, cg_g, mod_g, ssem, rsem, osem, sm_ssem, sm_rsem):
        del idx_ref
        s, i = pl.program_id(0), pl.program_id(1)
        x, y, cc, idx = _mesh_pos()
        k = _gather_order(s)
        j = idx ^ k

        def remote(kk):
            return pltpu.make_async_remote_copy(src_ref=wslot.at[idx], dst_ref=wslot.at[idx], send_sem=ssem.at[kk], recv_sem=rsem.at[kk],
                                                device_id=_peer(x, y, cc, kk), device_id_type=MESH)

        def forward(kk):
            jj = idx ^ kk
            return pltpu.make_async_remote_copy(src_ref=wslot.at[jj], dst_ref=wslot.at[jj], send_sem=ssem.at[kk ^ 1],
                                                recv_sem=rsem.at[kk ^ 1], device_id=(x, y, 1 - cc), device_id_type=MESH)

        def to_hbm(jj, kk):
            return pltpu.make_async_copy(wslot.at[jj], win_o.at[:, pl.ds(pl.multiple_of(jj * SH_WIN, 128), SH_WIN)], osem.at[kk])

        @pl.when((s == 0) & (i == 0))
        def _():
            _small_gathers((lb_r, ps_r, c_r, cctx_r, ada_r, adab_r, lb_g, ps_g, cg_g, mod_g, lb_o, ps_o, cg_o, mod0_o, mod1_o, modc_o),
                           sm_ssem, sm_rsem)
            wslot[idx] = win_r[...].astype(BF16)
            for kk in GATHER_DIRECT:
                remote(kk).start()
            s_wout[...] = wout_r[...].astype(BF16)
            s_pwin[...] = pwin_r[...].astype(BF16)
            s_pgrp[...] = pgrp_r[...].astype(BF16)
            s_pwout[...] = pwout_r[...].astype(BF16)

        @pl.when(s == 0)
        def _():
            shift = jnp.where(i == 0, modc_o[0:1, :], mod0_o[0:1, :])
            scale = jnp.where(i == 0, modc_o[1:2, :], mod0_o[1:2, :])
            hx, _, _, _ = _modulated(_ctx_or_x(i, ctx_ref, x_ref), nw_ref[...], shift, scale)
            hx_sc[i] = hx.astype(BF16)

        @pl.when((s > 0) & (i == 0))
        def _():
            remote(k).wait_recv()

            @pl.when((k & 1) == 0)
            def _():
                forward(k).start()

        @pl.when(i == 0)
        def _():
            to_hbm(j, k).start()

        g_ref[...] = jnp.dot(hx_sc[i], wslot[j], preferred_element_type=F32)

        @pl.when((s == NDEV - 1) & (i == NT - 1))
        def _():
            for kk in GATHER_DIRECT:
                remote(kk).wait_send()
            for kk in GATHER_ICI:
                forward(kk).wait_send()
            for kk in range(NDEV):
                to_hbm(idx ^ kk, kk).wait()

    grid_spec = pltpu.PrefetchScalarGridSpec(
        num_scalar_prefetch=1, grid=(NDEV, NT),
        in_specs=[VMEM_SPEC, pl.BlockSpec((TM, D), lambda s, i, ix: (jnp.maximum(i - 1, 0), 0))] + [VMEM_SPEC] * 12,
        out_specs=[pl.BlockSpec((TM, SH_WIN), lambda s, i, ix: (i, ix[0] ^ _gather_order(s))), HBM_SPEC] + [VMEM_SPEC] * 10,
        scratch_shapes=[pltpu.VMEM((NDEV, D, SH_WIN), BF16), pltpu.VMEM((NT, TM, D), BF16),
                        pltpu.VMEM((NDEV, 2, DH), F32), pltpu.VMEM((NDEV, 1, DH), F32), pltpu.VMEM((NDEV, 1, D), F32),
                        pltpu.VMEM((NDEV, 2, 16, SH_ADA), F32),
                        pltpu.SemaphoreType.DMA((NDEV,)), pltpu.SemaphoreType.DMA((NDEV,)), pltpu.SemaphoreType.DMA((NDEV,)),
                        pltpu.SemaphoreType.DMA((4, NDEV)), pltpu.SemaphoreType.DMA((4, NDEV))])
    outs = (_sds((TT, WIN_COLS), F32), _sds((D, WIN_COLS), BF16),
            _sds((SH_ROWS, D), BF16), _sds((D, SH_PWIN), BF16), _sds((4, SH_GRP, PG), BF16), _sds((SH_ROWS, D), BF16),
            _sds((NDEV, 2, DH), F32), _sds((NDEV, 1, DH), F32), _sds((NDEV, 1, D), F32),
            _sds((3, D), F32), _sds((3, D), F32), _sds((3, D), F32))
    return pl.pallas_call(
        body, name="f1_gather_matmul", grid_spec=grid_spec, out_shape=outs,
        compiler_params=pltpu.CompilerParams(dimension_semantics=("arbitrary", "arbitrary"), vmem_limit_bytes=VMEM_LIMIT),
    )(idx1, ctx, x, nw, w_in, w_out, pw_in, pgrp, pw_out, lb_l, pscale, c, c_ctx, ada_w, ada_b)


def _gla_gates(pre, qpre, lbd, cum, rev):
    rows, n = pre.shape
    nch = rows // CHUNK
    sig = _sigmoid(pre)
    f = lbd + (1.0 - lbd) * sig
    k = 1.0 - f
    g = _dot01(cum, jnp.log(f))
    g3 = g.reshape(nch, CHUNK, n)
    last = 0 if rev else CHUNK - 1
    mid = CHUNK // 2 if rev else CHUNK // 2 - 1
    gl1, gm1 = g3[:, last:last + 1, :], g3[:, mid:mid + 1, :]

    def bc(a):
        return jnp.broadcast_to(a, g3.shape).reshape(rows, n)

    gm = bc(gm1)
    e_q, e_k = jnp.exp(g - gm), jnp.exp(gm - g)
    qsig = _sigmoid(qpre)
    qs = qpre * qsig * (DH ** -0.5)
    return dict(sig=sig, f=f, k=k, qsig=qsig, qs=qs, e_q=e_q, e_k=e_k,
                e_mid=[jnp.exp(gm1[ci]) for ci in range(nch)], e_rest=[jnp.exp(gl1[ci] - gm1[ci]) for ci in range(nch)])


def _put_heads(ref, lead, arr):
    for h in range(HEADS):
        ref[lead + (h,)] = arr[:, h * DH:(h + 1) * DH]


def _get_heads(ref, lead=()):
    return jnp.concatenate([ref[lead + (h,)] for h in range(HEADS)], axis=1)


def _gla_prep(g_all, lb, cum01, s_wout, s_pgrp):
    nch = TM // CHUNK

    def body(g_ref, lb_ref, cum_ref, swout_r, spgrp_r, p0_ref, p1_ref, v_ref, dec_ref, wout_o, pgrp_o, ssem, rsem, lsem):
        _gather_rider(pl.program_id(0), NT, NT - 1, ("rows", "grp"), (swout_r, spgrp_r), (wout_o, pgrp_o), ssem, rsem, lsem)
        qpre = g_ref[:, 3 * E:4 * E]
        _put_heads(v_ref, (), g_ref[:, 2 * E:3 * E].astype(BF16))
        for d, p_ref in ((0, p0_ref), (1, p1_ref)):
            t = _gla_gates(g_ref[:, d * E:(d + 1) * E], qpre, lb_ref[d:d + 1, :], cum_ref[d], d == 1)
            _put_heads(p_ref, (0,), (t["qs"] * t["e_q"]).astype(BF16))
            _put_heads(p_ref, (1,), (t["k"] * t["e_k"]).astype(BF16))
            for ci in range(nch):
                dec_ref[d, 0, ci:ci + 1, :] = t["e_mid"][ci]
                dec_ref[d, 0, nch + ci:nch + ci + 1, :] = t["e_rest"][ci]

    quad = pl.BlockSpec((2, HEADS, TM, DH), lambda i: (0, 0, i, 0))
    return pl.pallas_call(
        body, name="gla_prep", grid=(NT,),
        in_specs=[pl.BlockSpec((TM, 4 * E), lambda i: (i, 0)), VMEM_SPEC, VMEM_SPEC, HBM_SPEC, HBM_SPEC],
        out_specs=[quad, quad, pl.BlockSpec((HEADS, TM, DH), lambda i: (0, i, 0)), pl.BlockSpec((2, 1, 2 * nch, E), lambda i: (0, i, 0, 0)),
                   HBM_SPEC, HBM_SPEC],
        out_shape=(_sds((2, HEADS, TT, DH), BF16), _sds((2, HEADS, TT, DH), BF16), _sds((HEADS, TT, DH), BF16), _sds((2, NT, 2 * nch, E), F32),
                   _sds((E, D), BF16), _sds((4, PG, PG), BF16)),
        scratch_shapes=_rider_sems(2),
        compiler_params=pltpu.CompilerParams(dimension_semantics=("arbitrary",), vmem_limit_bytes=VMEM_LIMIT),
    )(g_all, lb, cum01, s_wout, s_pgrp)


def _scan_tile(i, rev):
    t = jnp.where(i == 0, 0, NT - i) if rev else i
    return t, pl.ds(pl.multiple_of(t * TM, TM), TM)


def _chunk_order(rev):
    n = TM // CHUNK
    return tuple(range(n - 1, -1, -1)) if rev else tuple(range(n))


def _chunk_rows(dec_ref, lanes, cis, where):
    nch = TM // CHUNK

    def rows(off):
        return jnp.stack([dec_ref[d, where[d][0], off + ci:off + ci + 1, hh * DH:(hh + 1) * DH] for (d, hh), ci in zip(lanes, cis)])

    return rows(0), rows(nch)


def _gla_fwd(p0, p1, v_all, dec, mask01, s_pwin):
    n_steps = HEADS // GLA_HB

    def body(p0_ref, p1_ref, v_ref, dec_ref, msk_ref, spwin_r, o_ref, pwin_o, ob_sc, ssem, rsem, lsem):
        _gather_rider(pl.program_id(0), n_steps, n_steps - 1, ("major",), (spwin_r,), (pwin_o,), ssem, rsem, lsem)

        lanes = [(d, hh) for d in (0, 1) for hh in range(GLA_HB)]
        nch = TM // CHUNK

        def tile_body(i, st):
            where = [_scan_tile(i, d == 1) for d in (0, 1)]

            def stacked(fn):
                return jnp.stack([fn(d, hh, where[d][1]) for d, hh in lanes])

            qg, kg = [stacked(lambda d, hh, rows, ty=ty: (p1_ref if d else p0_ref)[ty, hh, rows, :]) for ty in range(2)]
            v = stacked(lambda d, hh, rows: v_ref[hh, rows, :])
            a = _bdot_nt(qg, kg) * jnp.stack([msk_ref[d] for d, _ in lanes])
            intra = _bdot(a, v)
            outs = [[None] * nch for _ in lanes]
            for n in range(nch):
                cis = [nch - 1 - n if d else n for d, _ in lanes]

                def chunk(arr):
                    return jnp.stack([arr[l, ci * CHUNK:(ci + 1) * CHUNK] for l, ci in enumerate(cis)])

                e_mid, e_rest = _chunk_rows(dec_ref, lanes, cis, where)
                inter = _bdot_nt(chunk(qg), st * e_mid)
                for l, ci in enumerate(cis):
                    outs[l][ci] = inter[l] + intra[l, ci * CHUNK:(ci + 1) * CHUNK]
                st = st * (e_mid * e_rest) + _bdot_tn(chunk(v), chunk(kg)) * e_rest
            for l, (d, hh) in enumerate(lanes):
                (ob_sc if d else o_ref)[hh, where[d][1], :] = jnp.concatenate(outs[l], axis=0)
            return st

        lax.fori_loop(0, NT, tile_body, jnp.zeros((len(lanes), DH, DH), F32))
        o_ref[...] += ob_sc[...]

    quad = pl.BlockSpec((2, GLA_HB, TT, DH), lambda h: (0, h, 0, 0))
    head = pl.BlockSpec((GLA_HB, TT, DH), lambda h: (h, 0, 0))
    return pl.pallas_call(
        body, name="gla_fwd", grid=(n_steps,),
        in_specs=[quad, quad, head, pl.BlockSpec((2, NT, 8, GLA_HB * DH), lambda h: (0, 0, 0, h)),
                  pl.BlockSpec((2, TM, TM), lambda h: (0, 0, 0)), HBM_SPEC],
        out_specs=[head, HBM_SPEC],
        out_shape=(_sds((HEADS, TT, DH), F32), _sds((NDEV, D, SH_PWIN), BF16)),
        scratch_shapes=[pltpu.VMEM((GLA_HB, TT, DH), F32)] + _rider_sems(1),
        compiler_params=pltpu.CompilerParams(dimension_semantics=("arbitrary",), vmem_limit_bytes=VMEM_LIMIT),
    )(p0, p1, v_all, dec, mask01, s_pwin)


def _gated_norm(o, z, gw):
    r = _head_map(lambda oh: jnp.broadcast_to(_rstd(oh), oh.shape), o)
    on = o * r
    zs = _sigmoid(z)
    sz = z * zs
    return on * gw * sz, r, on, zs, sz


def _f3_out(o, g_all, x, gate, gw, wout, s_pwout):
    def body(o_ref, z_ref, x_ref, gate_ref, gw_ref, w_ref, spwout_r, x1_ref, pwout_o, ssem, rsem, lsem):
        _gather_rider(pl.program_id(0), NTX, NTX - 1, ("rows",), (spwout_r,), (pwout_o,), ssem, rsem, lsem)
        og, _, _, _, _ = _gated_norm(_get_heads(o_ref), z_ref[...], gw_ref[...])
        x1_ref[...] = x_ref[...] + gate_ref[...] * _dot(og, w_ref[...])

    return pl.pallas_call(
        body, name="f3_out", grid=(NTX,),
        in_specs=[pl.BlockSpec((HEADS, TM, DH), lambda i: (0, i + 1, 0)), pl.BlockSpec((TM, E), lambda i: (i + 1, 4)),
                  pl.BlockSpec((TM, D), lambda i: (i, 0)), pl.BlockSpec((1, D), lambda i: (0, 0)),
                  pl.BlockSpec((1, E), lambda i: (0, 0)), pl.BlockSpec((E, D), lambda i: (0, 0)), HBM_SPEC],
        out_specs=[pl.BlockSpec((TM, D), lambda i: (i, 0)), HBM_SPEC],
        out_shape=(_sds((T, D), F32), _sds((E, D), BF16)),
        scratch_shapes=_rider_sems(1),
        compiler_params=pltpu.CompilerParams(dimension_semantics=("arbitrary",)),
    )(o, g_all, x, gate, gw, wout, s_pwout)


def _pool_layer(x1, tgt, mod1, nw1, fnw, pwin, pgrp, pscale, pwout, pb, pbt, pinv):
    def body(x_ref, t_ref, m_ref, nw_ref, fw_ref, pwin_ref, pgrp_ref, ps_ref, pwout_ref, pb_ref, pbt_ref, pinv_ref,
             dx_ref, gpwin_o, gpgrp_o, gpwout_o, dmod_o, gnw_o, gfw_o, gps_o, loss_o,
             a_pwin, a_pgrp, a_pwout):
        i = pl.program_id(0)

        @pl.when(i == 0)
        def _():
            for ref in (a_pwin, a_pgrp, a_pwout, dmod_o, gnw_o, gfw_o, gps_o, loss_o):
                ref[...] = jnp.zeros_like(ref)

        shift, scale, gate = m_ref[0:1, :], m_ref[1:2, :], m_ref[2:3, :]
        nw, fw, ps = nw_ref[...], fw_ref[...], ps_ref[...]
        x1 = x_ref[...]
        hx, r1, xn, a = _modulated(x1, nw, shift, scale)
        hxb = hx.astype(BF16)
        uz = jnp.concatenate([_dot(hxb, pwin_ref[j]) for j in range(NDEV)], axis=1)
        u, z = uz[:, :E], uz[:, E:]
        pooled, ys = [], []
        for g in range(4):
            ug = u[:, g * PG:(g + 1) * PG]
            pg = _dot01(pb_ref[g], ug) * pinv_ref[g] - ug
            pooled.append(pg.astype(BF16))
            ys.append(_dot(pooled[g], pgrp_ref[g]))
        ycat = jnp.concatenate(ys, axis=1)
        y = ycat * ps
        zs = _sigmoid(z)
        sz = z * zs
        p = (y * sz).astype(BF16)
        out = _dot(p, pwout_ref[...])
        x2 = x1 + gate * out
        r2 = _rstd(x2)
        xn2 = x2 * r2
        diff = xn2 * fw - t_ref[...]
        loss_o[...] += _colsum(diff * diff)
        dyf = diff * (1.0 / D)
        gfw_o[...] += _colsum(dyf * xn2)
        dxn2 = dyf * fw
        dx2 = r2 * (dxn2 - xn2 * jnp.mean(dxn2 * xn2, axis=-1, keepdims=True))
        dgate = _colsum(dx2 * out)
        dout = (dx2 * gate).astype(BF16)
        for j in range(4):
            cs = slice(j * PG, (j + 1) * PG)
            a_pwout[:, cs] += _dot_ta(p, dout[:, cs])
        dp = _dot_tb(dout, pwout_ref[...])
        dy = dp * sz
        dz = dp * y * (zs * (1.0 + z * (1.0 - zs)))
        gps_o[...] += _colsum(dy * ycat)
        dycat = dy * ps
        dus = []
        for g in range(4):
            dyg = dycat[:, g * PG:(g + 1) * PG].astype(BF16)
            a_pgrp[g] += _dot_ta(pooled[g], dyg)
            dpg = _dot_tb(dyg, pgrp_ref[g])
            dus.append(_dot01(pbt_ref[g], dpg * pinv_ref[g]) - dpg)
        duz = jnp.concatenate(dus + [dz], axis=1).astype(BF16)
        dhx = None
        for j in range(NDEV):
            dj = duz[:, j * SH_PWIN:(j + 1) * SH_PWIN]
            a_pwin[j] += _dot_ta(hxb, dj)
            part = _dot_tb(dj, pwin_ref[j])
            dhx = part if dhx is None else dhx + part
        dmod_o[0:1, :] += _colsum(dhx)
        dmod_o[1:2, :] += _colsum(dhx * a)
        dmod_o[2:3, :] += dgate
        da = dhx * (1.0 + scale)
        gnw_o[...] += _colsum(da * xn)
        dxn = da * nw
        dx_ref[...] = dx2 + r1 * (dxn - xn * jnp.mean(dxn * xn, axis=-1, keepdims=True))

        @pl.when(i == NTX - 1)
        def _():
            gpwin_o[...] = a_pwin[...].astype(BF16)
            gpgrp_o[...] = a_pgrp[...].astype(BF16)
            gpwout_o[...] = a_pwout[...].astype(BF16)

    tile = pl.BlockSpec((TM, D), lambda i: (i, 0))
    outs = (_sds((T, D), F32), _sds((NDEV, D, SH_PWIN), BF16), _sds((4, PG, PG), BF16), _sds((E, D), BF16),
            _sds((3, D), F32), _sds((1, D), F32), _sds((1, D), F32), _sds((1, E), F32), _sds((1, D), F32))
    return pl.pallas_call(
        body, name="pool_layer", grid=(NTX,),
        in_specs=[tile, tile] + [VMEM_SPEC] * 10,
        out_specs=[tile] + [VMEM_SPEC] * 8,
        out_shape=outs,
        scratch_shapes=[pltpu.VMEM((NDEV, D, SH_PWIN), F32), pltpu.VMEM((4, PG, PG), F32), pltpu.VMEM((E, D), F32)],
        compiler_params=pltpu.CompilerParams(dimension_semantics=("arbitrary",), vmem_limit_bytes=VMEM_LIMIT),
    )(x1, tgt, mod1, nw1, fnw, pwin, pgrp, pscale, pwout, pb, pbt, pinv)


def _b3_out_bwd(dx1, o, g_all, gate, gw, wout, gpwout):
    def body(dx_ref, o_ref, z_ref, gate_ref, gw_ref, w_ref, gpwout_r, do_ref, dz_ref, gw_o, dgate_o, ggw_o, rpwout_o,
             acc, *rider):
        i = pl.program_id(0)
        bufs, sems = _rider2_split(rider, 1)
        _scatter_rider2(i, NT, 2, ("rows",), (gpwout_r,), (rpwout_o,), bufs, sems)

        @pl.when(i == 0)
        def _():
            acc[...] = jnp.zeros_like(acc)
            dgate_o[...] = jnp.zeros_like(dgate_o)
            ggw_o[...] = jnp.zeros_like(ggw_o)
            do_ref[...] = jnp.zeros_like(do_ref)
            dz_ref[...] = jnp.zeros_like(dz_ref)

        @pl.when(i > 0)
        def _():
            gw = gw_ref[...]
            z = z_ref[...]
            og, r, on, zs, sz = _gated_norm(_get_heads(o_ref), z, gw)
            ogb = og.astype(BF16)
            dx = dx_ref[...]
            dgate_o[...] += _colsum(dx * _dot(ogb, w_ref[...]))
            dy = (dx * gate_ref[...]).astype(BF16)
            for j in range(4):
                cs = slice(j * PG, (j + 1) * PG)
                acc[:, cs] += _dot_ta(ogb, dy[:, cs])
            dog = _dot_tb(dy, w_ref[...])
            dz_ref[...] = (dog * (on * gw) * (zs * (1.0 + z * (1.0 - zs)))).astype(BF16)
            dong = dog * sz
            ggw_o[...] += _colsum(dong * on)
            don = dong * gw
            do = _head_map(lambda dh, nh, rh: rh * (dh - nh * jnp.mean(dh * nh, axis=-1, keepdims=True)), don, on, r)
            _put_heads(do_ref, (), do.astype(BF16))

        @pl.when(i == NT - 1)
        def _():
            gw_o[...] = acc[...].astype(BF16)

    prev = lambda i: (jnp.maximum(i - 1, 0), 0)
    heads = pl.BlockSpec((HEADS, TM, DH), lambda i: (0, i, 0))
    return pl.pallas_call(
        body, name="b3_out_bwd", grid=(NT,),
        in_specs=[pl.BlockSpec((TM, D), prev), heads, pl.BlockSpec((TM, E), lambda i: (i, 4)),
                  VMEM_SPEC, VMEM_SPEC, VMEM_SPEC, HBM_SPEC],
        out_specs=[heads, pl.BlockSpec((TM, E), lambda i: (i, 0)), VMEM_SPEC, VMEM_SPEC, VMEM_SPEC, HBM_SPEC],
        out_shape=(_sds((HEADS, TT, DH), BF16), _sds((TT, E), BF16), _sds((E, D), BF16), _sds((1, D), F32), _sds((1, E), F32),
                   _sds((RS_SLOTS, SH_ROWS, D), BF16)),
        scratch_shapes=[pltpu.VMEM((E, D), F32)] + _rider2_scratch([(SH_ROWS, D)]),
        compiler_params=pltpu.CompilerParams(dimension_semantics=("arbitrary",), vmem_limit_bytes=VMEM_LIMIT),
    )(dx1, o, g_all, gate, gw, wout, gpwout)


def _gla_bwd(p0, p1, v_all, dec, do, mask01, gpwin, gpgrp):
    nch = TM // CHUNK
    n_steps = HEADS // GLA_HB

    def body(p0_ref, p1_ref, v_ref, dec_ref, do_ref, msk_ref, gpwin_r, gpgrp_r, d0_ref, d1_ref, dv_ref, dgl_ref, rpwin_o, rpgrp_o,
             ss_sc, dv_sc, ssem, rsem, lsem, *rider):
        _scatter_rider(pl.program_id(0), n_steps, ("grp",), (gpgrp_r,), (rpgrp_o,), ssem, rsem, lsem)
        bufs, sems = _rider2_split(rider, 1)
        _scatter_rider2(pl.program_id(0), n_steps, 1, ("major",), (gpwin_r,), (rpwin_o,), bufs, sems)

        lanes = [(d, hh) for d in (0, 1) for hh in range(GLA_HB)]
        zero = jnp.zeros((len(lanes), DH, DH), F32)
        dgl_ref[...] = jnp.zeros_like(dgl_ref)

        def p_of(d):
            return p1_ref if d else p0_ref

        def scan_step(i, n):
            where = [_scan_tile(i, d == 1) for d in (0, 1)]
            cis = [nch - 1 - n if d else n for d, _ in lanes]
            e_mid, e_rest = _chunk_rows(dec_ref, lanes, cis, where)

            def chunk(arr):
                return jnp.stack([arr[l, ci * CHUNK:(ci + 1) * CHUNK] for l, ci in enumerate(cis)])

            return where, cis, e_mid, e_rest, chunk

        def stacked(i, fn):
            where = [_scan_tile(i, d == 1) for d in (0, 1)]
            return jnp.stack([fn(d, hh, where[d][1]) for d, hh in lanes])

        def fwd_body(i, st):
            v = stacked(i, lambda d, hh, rows: v_ref[hh, rows, :])
            kg = stacked(i, lambda d, hh, rows: p_of(d)[1, hh, rows, :])
            for n in range(nch):
                _, _, e_mid, e_rest, chunk = scan_step(i, n)
                ss_sc[i * nch + n] = st
                st = st * (e_mid * e_rest) + _bdot_tn(chunk(v), chunk(kg)) * e_rest
            return st

        ss_sc[NT * nch] = lax.fori_loop(0, NT, fwd_body, zero)

        def bwd_body(ii, dst):
            i = NT - 1 - ii
            qg, kg = [stacked(i, lambda d, hh, rows, ty=ty: p_of(d)[ty, hh, rows, :]) for ty in range(2)]
            v = stacked(i, lambda d, hh, rows: v_ref[hh, rows, :])
            dob = stacked(i, lambda d, hh, rows: do_ref[hh, rows, :])
            msk = jnp.stack([msk_ref[d] for d, _ in lanes])
            a = (_bdot_nt(qg, kg) * msk).astype(BF16)
            da = (_bdot_nt(dob, v) * msk).astype(BF16)
            dqg = _bdot(da, kg)
            dkg = _bdot_tn(da, qg)
            dv_intra = _bdot_tn(a, dob)
            dv_l, dkg_l, dqg_l = ([[None] * nch for _ in lanes] for _ in range(3))
            for n in range(nch - 1, -1, -1):
                where, cis, e_mid, e_rest, chunk = scan_step(i, n)
                s_c, s_end = ss_sc[i * nch + n], ss_sc[i * nch + n + 1]
                dste = (dst * e_rest).astype(BF16)
                kg_c, v_c, dob_c = chunk(kg), chunk(v), chunk(dob)
                dv_c = chunk(dv_intra) + _bdot_nt(kg_c, dste)
                dkg_c = chunk(dkg) + _bdot(v_c, dste)
                dqg_c = chunk(dqg) + _bdot(dob_c, s_c * e_mid)
                dgl = jnp.sum(s_end * dst, axis=1, keepdims=True)
                for l, ((d, hh), ci) in enumerate(zip(lanes, cis)):
                    dv_l[l][ci], dkg_l[l][ci], dqg_l[l][ci] = dv_c[l], dkg_c[l], dqg_c[l]
                    dgl_ref[d, where[d][0], ci:ci + 1, hh * DH:(hh + 1) * DH] = dgl[l]
                dst = dst * (e_mid * e_rest) + _bdot_tn(dob_c, chunk(qg)) * e_mid
            where = [_scan_tile(i, d == 1) for d in (0, 1)]
            for l, (d, hh) in enumerate(lanes):
                rows = where[d][1]
                d_ref = d1_ref if d else d0_ref
                d_ref[0, hh, rows, :] = jnp.concatenate(dqg_l[l], axis=0).astype(BF16)
                d_ref[1, hh, rows, :] = jnp.concatenate(dkg_l[l], axis=0).astype(BF16)
                dv_sc[d, hh, rows, :] = jnp.concatenate(dv_l[l], axis=0).astype(BF16)
            return dst

        lax.fori_loop(0, NT, bwd_body, zero)
        dv_ref[...] = (dv_sc[0].astype(F32) + dv_sc[1].astype(F32)).astype(BF16)

    quad = pl.BlockSpec((2, GLA_HB, TT, DH), lambda h: (0, h, 0, 0))
    col = pl.BlockSpec((GLA_HB, TT, DH), lambda h: (h, 0, 0))
    chunkv = pl.BlockSpec((2, NT, 8, GLA_HB * DH), lambda h: (0, 0, 0, h))
    outs = (_sds((2, HEADS, TT, DH), BF16), _sds((2, HEADS, TT, DH), BF16), _sds((HEADS, TT, DH), BF16), _sds((2, NT, 8, E), F32),
            _sds((RS_SLOTS, D, SH_PWIN), BF16), _sds((NDEV, 4, SH_GRP, PG), BF16))
    return pl.pallas_call(
        body, name="gla_bwd", grid=(n_steps,),
        in_specs=[quad, quad, col, chunkv, col, pl.BlockSpec((2, TM, TM), lambda h: (0, 0, 0)), HBM_SPEC, HBM_SPEC],
        out_specs=[quad, quad, col, chunkv, HBM_SPEC, HBM_SPEC],
        out_shape=outs,
        scratch_shapes=[pltpu.VMEM((NT * nch + 1, 2 * GLA_HB, DH, DH), F32), pltpu.VMEM((2, GLA_HB, TT, DH), BF16)] + _rider_sems(1)
        + _rider2_scratch([(D, SH_PWIN)]),
        compiler_params=pltpu.CompilerParams(dimension_semantics=("arbitrary",), vmem_limit_bytes=VMEM_LIMIT),
    )(p0, p1, v_all, dec, do, mask01, gpwin, gpgrp)


TMB = 128


def _gla_post_bwd(g_all, d0, d1, dgl, dv, dz, lb, cum01, gwout):
    nch = TMB // CHUNK

    def body(g_ref, d0_ref, d1_ref, dgl_ref, dv_ref, dz_ref, lb_ref, cum_ref, gwout_r, dg_ref, dlb_ref, rwout_o, *rider):
        i = pl.program_id(0)
        bufs, sems = _rider2_split(rider, 1)
        _scatter_rider2(i, TT // TMB, 2, ("rows",), (gwout_r,), (rwout_o,), bufs, sems)

        @pl.when(i == 0)
        def _():
            dlb_ref[...] = jnp.zeros_like(dlb_ref)

        half = i & 1
        qpre = g_ref[:, 3 * E:4 * E]
        dqs_sum = None
        dpre = []
        for d, d_ref in ((0, d0_ref), (1, d1_ref)):
            rev = d == 1
            lbd = lb_ref[d:d + 1, :]
            t = _gla_gates(g_ref[:, d * E:(d + 1) * E], qpre, lbd, cum_ref[d, :TMB, :TMB], rev)
            dqs = _get_heads(d_ref, (0,)).astype(F32) * t["e_q"]
            dk = _get_heads(d_ref, (1,)).astype(F32) * t["e_k"]
            dg = t["qs"] * dqs - t["k"] * dk
            dgl8 = dgl_ref[d, 0]
            dgl_rows = [jnp.where(half == 0, dgl8[ci:ci + 1, :], dgl8[nch + ci:nch + ci + 1, :]) for ci in range(nch)]
            dgl_b = jnp.concatenate([jnp.broadcast_to(dgl_rows[ci], (CHUNK, E)) for ci in range(nch)], axis=0)
            pos = lax.broadcasted_iota(jnp.int32, (TMB, E), 0) & (CHUNK - 1)
            dg = dg + jnp.where(pos == (0 if rev else CHUNK - 1), dgl_b, 0.0)
            dlf = _dot01(cum_ref[1 - d, :TMB, :TMB], dg)
            df = dlf / t["f"] - dk
            sig = t["sig"]
            dpre.append((df * (1.0 - lbd) * sig * (1.0 - sig)).astype(BF16))
            dlb_ref[d:d + 1, :] += _colsum(df * (1.0 - sig))
            dqs_sum = dqs if dqs_sum is None else dqs_sum + dqs
            qsig = t["qsig"]
        dqpre = dqs_sum * (DH ** -0.5) * (qsig * (1.0 + qpre * (1.0 - qsig)))
        dg_ref[...] = jnp.concatenate([dpre[0], dpre[1], _get_heads(dv_ref), dqpre.astype(BF16), dz_ref[...]], axis=1)

    quad = pl.BlockSpec((2, HEADS, TMB, DH), lambda i: (0, 0, i, 0))
    tile = pl.BlockSpec((TMB, E), lambda i: (i, 0))
    return pl.pallas_call(
        body, name="gla_post_bwd", grid=(TT // TMB,),
        in_specs=[pl.BlockSpec((TMB, 4 * E), lambda i: (i, 0)), quad, quad,
                  pl.BlockSpec((2, 1, 8, E), lambda i: (0, i // 2, 0, 0)), pl.BlockSpec((HEADS, TMB, DH), lambda i: (0, i, 0)), tile,
                  VMEM_SPEC, VMEM_SPEC, HBM_SPEC],
        out_specs=[pl.BlockSpec((TMB, WIN_COLS), lambda i: (i, 0)), VMEM_SPEC, HBM_SPEC],
        out_shape=(_sds((TT, WIN_COLS), BF16), _sds((2, E), F32), _sds((RS_SLOTS, SH_ROWS, D), BF16)),
        scratch_shapes=_rider2_scratch([(SH_ROWS, D)]),
        compiler_params=pltpu.CompilerParams(dimension_semantics=("arbitrary",), vmem_limit_bytes=VMEM_LIMIT),
    )(g_all, d0, d1, dgl, dv, dz, lb, cum01, gwout)


def _b1_in_bwd(idx1, ctx, x, dx1, dg, nw, msel, win):
    last_s = NDEV - 1

    def body(idx_ref, ctx_ref, x_ref, dx1_ref, dg_ref, nw_ref, m_ref, w_ref, gx_ref, rwin_o, dmx_o, dmc_o, gnw_o,
             hx_sc, dhx_sc, acc, sbuf, pbuf, psend, precv, isend, irecv, sibsem, lsem):
        del idx_ref
        s, i = pl.program_id(0), pl.program_id(1)
        x, y, cc, idx = _mesh_pos()
        shift, scale = m_ref[0, 0:1, :], m_ref[0, 1:2, :]
        sibling = (x, y, 1 - cc)

        def partial(p):
            return pltpu.make_async_remote_copy(src_ref=sbuf.at[0], dst_ref=pbuf.at[p], send_sem=psend.at[p], recv_sem=precv.at[p],
                                                device_id=sibling, device_id_type=MESH)

        def chip_sum(p):
            return pltpu.make_async_remote_copy(src_ref=sbuf.at[1], dst_ref=rwin_o.at[2 + p], send_sem=isend.at[p], recv_sem=irecv.at[p],
                                                device_id=_peer(x, y, cc, 2 * (p + 1)), device_id_type=MESH)

        to_sibling = pltpu.make_async_remote_copy(src_ref=sbuf.at[0], dst_ref=rwin_o.at[1], send_sem=sibsem.at[0], recv_sem=sibsem.at[1],
                                                  device_id=sibling, device_id_type=MESH)
        own = pltpu.make_async_copy(sbuf.at[1], rwin_o.at[0], lsem)

        @pl.when((s == 0) & (i == 0))
        def _():
            for ref in (dmx_o, dmc_o, gnw_o):
                ref[...] = jnp.zeros_like(ref)

        @pl.when(s == 0)
        def _():
            hx, _, _, _ = _modulated(_ctx_or_x(i, ctx_ref, x_ref), nw_ref[...], shift, scale)
            hx_sc[i] = hx.astype(BF16)

        @pl.when(i == 0)
        def _():
            acc[...] = jnp.zeros_like(acc)

        dgb = dg_ref[...]
        hxb = hx_sc[i]
        for lo, hi in ((0, 256), (256, 512), (512, SH_WIN)):
            acc[:, lo:hi] += _dot_ta(hxb, dgb[:, lo:hi])
        part = _dot_tb(dgb, w_ref[...])

        @pl.when(s == 0)
        def _():
            dhx_sc[i] = part

        @pl.when(s > 0)
        def _():
            dhx_sc[i] += part

        for p in (2, 1, 0):
            @pl.when((i == NT - 1) & (s == 2 * (2 - p)))
            def _(p=p):
                if p < 2:
                    partial(p + 1).wait_send()
                sbuf[0] = acc[...].astype(BF16)
                partial(p).start()

            @pl.when((i == NT - 1) & (s == 2 * (2 - p) + 1))
            def _(p=p):
                if p < 2:
                    chip_sum(p + 1).wait_send()
                partial(p).wait_recv()
                sbuf[1] = (acc[...] + pbuf[p].astype(F32)).astype(BF16)
                chip_sum(p).start()

        @pl.when((i == NT - 1) & (s == last_s - 1))
        def _():
            partial(0).wait_send()
            sbuf[0] = acc[...].astype(BF16)
            to_sibling.start()

        @pl.when((i == NT - 1) & (s == last_s))
        def _():
            chip_sum(0).wait_send()
            sbuf[1] = acc[...].astype(BF16)
            own.start()

        @pl.when(s == last_s)
        def _():
            nw = nw_ref[...]
            _, r, xn, a = _modulated(_ctx_or_x(i, ctx_ref, x_ref), nw, shift, scale)
            dhx = dhx_sc[i]
            dsh, dsc = _colsum(dhx), _colsum(dhx * a)
            da = dhx * (1.0 + scale)
            gnw_o[...] += _colsum(da * xn)
            dxn = da * nw
            gx_ref[...] = dx1_ref[...] + r * (dxn - xn * jnp.mean(dxn * xn, axis=-1, keepdims=True))

            @pl.when(i == 0)
            def _():
                dmc_o[0:1, :] += dsh
                dmc_o[1:2, :] += dsc

            @pl.when(i > 0)
            def _():
                dmx_o[0:1, :] += dsh
                dmx_o[1:2, :] += dsc

        @pl.when((i == NT - 1) & (s == last_s))
        def _():
            to_sibling.wait_send()
            to_sibling.wait_recv()
            for p in range(3):
                chip_sum(p).wait_recv()
            own.wait()

    grid_spec = pltpu.PrefetchScalarGridSpec(
        num_scalar_prefetch=1, grid=(NDEV, NT),
        in_specs=[VMEM_SPEC, pl.BlockSpec((TM, D), lambda s, i, ix: (jnp.maximum(i - 1, 0), 0)),
                  pl.BlockSpec((TM, D), lambda s, i, ix: (jnp.maximum(i - 1, 0), 0)),
                  pl.BlockSpec((TM, SH_WIN), lambda s, i, ix: (i, ix[0] ^ (last_s - s))), VMEM_SPEC,
                  pl.BlockSpec((1, 2, D), lambda s, i, ix: (jnp.minimum(i, 1), 0, 0)),
                  pl.BlockSpec((D, SH_WIN), lambda s, i, ix: (0, ix[0] ^ (last_s - s)))],
        out_specs=[pl.BlockSpec((TM, D), lambda s, i, ix: (jnp.where(s == last_s, jnp.maximum(i - 1, 0), 0), 0)),
                   HBM_SPEC, VMEM_SPEC, VMEM_SPEC, VMEM_SPEC],
        scratch_shapes=[pltpu.VMEM((NT, TM, D), BF16), pltpu.VMEM((NT, TM, D), F32), pltpu.VMEM((D, SH_WIN), F32),
                        pltpu.VMEM((2, D, SH_WIN), BF16), pltpu.VMEM((3, D, SH_WIN), BF16),
                        pltpu.SemaphoreType.DMA((3,)), pltpu.SemaphoreType.DMA((3,)), pltpu.SemaphoreType.DMA((3,)),
                        pltpu.SemaphoreType.DMA((3,)), pltpu.SemaphoreType.DMA((2,)), pltpu.SemaphoreType.DMA])
    return pl.pallas_call(
        body, name="b1_in_bwd", grid_spec=grid_spec,
        out_shape=(_sds((T, D), F32), _sds((RS_SLOTS, D, SH_WIN), BF16), _sds((2, D), F32), _sds((2, D), F32), _sds((1, D), F32)),
        compiler_params=pltpu.CompilerParams(dimension_semantics=("arbitrary", "arbitrary"), vmem_limit_bytes=VMEM_LIMIT),
    )(idx1, ctx, x, dx1, dg, nw, msel, win)


def _reduce_small(pd, pv, cg, c_ctx, ada_w0):
    n_arr = 3

    def body(pd_r, pv_r, cg_r, cctx_r, ada_r, gada_o, gadab_o, gcctx_o, pvsum_o, loss_o,
             pd_all, pv_all, dsc_all, dsc_mine, ssem, rsem):
        x, y, cc, idx = _mesh_pos()
        srcs = [pd_r, pv_r, dsc_mine]
        dsts = [pd_all.at[idx], pv_all.at[idx], dsc_all.at[idx]]

        def remote(a, k):
            return pltpu.make_async_remote_copy(src_ref=srcs[a], dst_ref=dsts[a], send_sem=ssem.at[a, k], recv_sem=rsem.at[a, k],
                                                device_id=_peer(x, y, cc, k), device_id_type=MESH)

        first = [remote(a, k) for k in range(1, NDEV) for a in (0, 1)]
        for cp in first:
            cp.start()
        pd_all[idx] = pd_r[...]
        pv_all[idx] = pv_r[...]
        for k in range(1, NDEV):
            remote(0, k).wait_recv()
            remote(1, k).wait_recv()
        mine = [pd_all[s, :, pl.ds(idx, 1), :] for s in range(NDEV)]
        dmc = functools.reduce(lambda u, v: u + v, [m[2] for m in mine])
        rows = _stack_rows([cg_r[i] for i in range(NDEV)] + [cctx_r[...]])
        sc = (rows * _sigmoid(rows)).astype(BF16)
        gada_o[0] = _dot_ta(sc, _stack_rows([m[0] for m in mine] + [dmc]))
        gada_o[1] = _dot_ta(sc, _stack_rows([m[1] for m in mine]))
        dsc_mine[...] = _dot_tb(jnp.broadcast_to(dmc, (8, SH_ADA)), ada_r[...])[0:1, :]
        dsc_all[idx] = dsc_mine[...]
        second = [remote(2, k) for k in range(1, NDEV)]
        for cp in second:
            cp.start()
        tot = [functools.reduce(lambda u, v: u + v, [pd_all[s, l] for s in range(NDEV)]) for l in range(3)]
        gadab_o[0] = tot[0] + tot[2]
        gadab_o[1] = tot[1]
        pvs = functools.reduce(lambda u, v: u + v, [pv_all[s] for s in range(NDEV)])
        pvsum_o[...] = pvs
        loss_o[...] = jnp.broadcast_to(jnp.sum(pvs[:, PV_LOSS:PV_LOSS + D], axis=-1, keepdims=True) * (0.5 / D), (1, 128))
        for k in range(1, NDEV):
            remote(2, k).wait_recv()
        dsc = functools.reduce(lambda u, v: u + v, [dsc_all[s] for s in range(NDEV)])
        cx = cctx_r[...]
        sx = _sigmoid(cx)
        gcctx_o[...] = dsc * (sx * (1.0 + cx * (1.0 - sx)))
        for cp in first + second:
            cp.wait_send()

    outs = (_sds((2, D, SH_ADA), F32), _sds((2, NDEV, SH_ADA), F32), _sds((1, D), F32), _sds((1, PV_LEN), F32), _sds((1, 128), F32))
    return pl.pallas_call(
        body, name="reduce_small", out_shape=outs,
        in_specs=[VMEM_SPEC] * 5, out_specs=[VMEM_SPEC] * 5,
        scratch_shapes=[
            pltpu.VMEM((NDEV, 3, NDEV, SH_ADA), F32), pltpu.VMEM((NDEV, 1, PV_LEN), F32), pltpu.VMEM((NDEV, 1, D), F32),
            pltpu.VMEM((1, D), F32),
            pltpu.SemaphoreType.DMA((n_arr, NDEV)), pltpu.SemaphoreType.DMA((n_arr, NDEV)),
        ],
        compiler_params=pltpu.CompilerParams(vmem_limit_bytes=VMEM_LIMIT),
    )(pd, pv, cg, c_ctx, ada_w0)


PV_NW, PV_GNORM, PV_FINAL, PV_LB, PV_PSCALE, PV_LOSS, PV_LEN = 0, 2 * D, 3 * D, 4 * D, 6 * D, 7 * D, 8 * D


def _adamw(w, g, m, v):
    m = ADAM_B1 * m + (1.0 - ADAM_B1) * g
    v = ADAM_B2 * v + (1.0 - ADAM_B2) * (g * g)
    m_hat = m / (1.0 - ADAM_B1 ** ADAM_STEP)
    v_hat = v / (1.0 - ADAM_B2 ** ADAM_STEP)
    delta = -ADAM_LR * (m_hat / (jnp.sqrt(v_hat) + ADAM_EPS) + ADAM_WD * w)
    return delta, m, v


ADAM_STEPS = 8


def _adam_all(sharded, dense, small, lb_idx, lbv):
    ns, nd, nsm = len(sharded), len(dense), len(small)

    def body(*refs):
        it = iter(refs)
        sh_in = [[next(it) for _ in range(4)] for _ in range(ns)]
        de_in = [[next(it) for _ in range(4)] for _ in range(nd)]
        sm_in = [[next(it) for _ in range(4)] for _ in range(nsm)]
        lb_r = next(it)
        sh_out = [[next(it) for _ in range(4)] for _ in range(ns)]
        de_out = [[next(it) for _ in range(3)] for _ in range(nd)]
        sm_out = [[next(it) for _ in range(4)] for _ in range(nsm)]
        for (p, w, m, v), outs in zip(sh_in, sh_out):
            g = p[0].astype(F32)
            for s in range(1, p.shape[0]):
                g = g + p[s].astype(F32)
            d, mn, vn = _adamw(w[...], g, m[...], v[...])
            outs[0][...], outs[1][...], outs[2][...], outs[3][...] = g, d, mn, vn
        for (g, w, m, v), outs in zip(de_in, de_out):
            d, mn, vn = _adamw(w[...], g[...], m[...], v[...])
            outs[0][...], outs[1][...], outs[2][...] = d, mn, vn

        @pl.when(pl.program_id(0) == 0)
        def _():
            for j, ((g, w, m, v), outs) in enumerate(zip(sm_in, sm_out)):
                gj = g[...]
                if j == lb_idx:
                    gj = gj * lb_r[...] * (1.0 - lb_r[...])
                d, mn, vn = _adamw(w[...], gj, m[...], v[...])
                outs[0][...], outs[1][...], outs[2][...], outs[3][...] = gj, d, mn, vn

    def tile(a):
        return pl.BlockSpec((a.shape[0] // ADAM_STEPS, a.shape[1]), lambda i: (i, 0))

    in_specs, out_specs, out_shape, args = [], [], [], []
    for p, w, m, v in sharded:
        in_specs += [pl.BlockSpec((p.shape[0], p.shape[1] // ADAM_STEPS, p.shape[2]), lambda i: (0, i, 0))] + [tile(w)] * 3
        args += [p, w, m, v]
    for g, w, m, v in dense:
        in_specs += [tile(w)] * 4
        args += [g, w, m, v]
    for g, w, m, v in small:
        in_specs += [VMEM_SPEC] * 4
        args += [g, w, m, v]
    in_specs.append(VMEM_SPEC)
    args.append(lbv)
    for _, w, _, _ in sharded:
        out_specs += [tile(w)] * 4
        out_shape += [_sds(w.shape, F32)] * 4
    for _, w, _, _ in dense:
        out_specs += [tile(w)] * 3
        out_shape += [_sds(w.shape, F32)] * 3
    for _, w, _, _ in small:
        out_specs += [VMEM_SPEC] * 4
        out_shape += [_sds(w.shape, F32)] * 4
    res = pl.pallas_call(body, name="adam_all", grid=(ADAM_STEPS,), in_specs=in_specs, out_specs=out_specs, out_shape=tuple(out_shape),
                         compiler_params=pltpu.CompilerParams(dimension_semantics=("arbitrary",), vmem_limit_bytes=VMEM_LIMIT))(*args)
    it = iter(res)
    return ([tuple(next(it) for _ in range(4)) for _ in range(ns)], [tuple(next(it) for _ in range(3)) for _ in range(nd)],
            [tuple(next(it) for _ in range(4)) for _ in range(nsm)])


def kernel(x, c, ctx, c_ctx, ada_w, ada_b, norm_w, hgrn_w_in, hgrn_lb_logits, hgrn_gnorm_w, hgrn_w_out, pool_w_in, pool_w_grp, pool_scale, pool_w_out, final_norm_w, loss_target, m_c_ctx, m_ada_w, m_ada_b, m_norm_w, m_hgrn_w_in, m_hgrn_lb_logits, m_hgrn_gnorm_w, m_hgrn_w_out, m_pool_w_in, m_pool_w_grp, m_pool_scale, m_pool_w_out, m_final_norm_w, v_c_ctx, v_ada_w, v_ada_b, v_norm_w, v_hgrn_w_in, v_hgrn_lb_logits, v_hgrn_gnorm_w, v_hgrn_w_out, v_pool_w_in, v_pool_w_grp, v_pool_scale, v_pool_w_out, v_final_norm_w):
    idx = 4 * lax.axis_index("x") + 2 * lax.axis_index("y") + lax.axis_index("c")
    cctx2 = c_ctx.reshape(1, D)
    cum01, mask01 = _gla_consts()
    pb, pbt, pinv = _pool_consts()

    idx1 = idx.reshape(1).astype(jnp.int32)
    nw0, nw1 = norm_w[0:1], norm_w[1:2]
    fnw = final_norm_w.reshape(1, D)
    g_all, win, s_wout, s_pwin, s_pgrp, s_pwout, lbl_g, ps_g, cg, mod0, mod1, modc = _f1_gather_matmul(
        idx1, ctx[0], x[0], nw0, hgrn_w_in[0], hgrn_w_out[0], pool_w_in[0], pool_w_grp[0], pool_w_out[0], hgrn_lb_logits[0],
        pool_scale, c, cctx2, ada_w, ada_b)
    lb = jax.nn.sigmoid(jnp.transpose(lbl_g, (1, 0, 2)).reshape(2, E))
    pscale = ps_g.reshape(1, E)
    msel = jnp.stack([modc[:2], mod0[:2]])
    p0, p1, v_all, dec, wout, pgrp = _gla_prep(g_all, lb, cum01, s_wout, s_pgrp)
    o, pwin = _gla_fwd(p0, p1, v_all, dec, mask01, s_pwin)
    x1, pwout = _f3_out(o, g_all, x[0], mod0[2:3], hgrn_gnorm_w, wout, s_pwout)
    dx1, gpwin, gpgrp, gpwout, dmod1, gnw1, gfw, gps, lossv = _pool_layer(
        x1, loss_target[0], mod1, nw1, fnw, pwin, pgrp, pscale, pwout, pb, pbt, pinv)
    do, dz, gwout, dgate0, ggw, rpwout = _b3_out_bwd(dx1, o, g_all, mod0[2:3], hgrn_gnorm_w, wout, gpwout)
    d0, d1, dv, dgl, rpwin, rpgrp = _gla_bwd(p0, p1, v_all, dec, do, mask01, gpwin, gpgrp)
    dg, dlb, rwout = _gla_post_bwd(g_all, d0, d1, dgl, dv, dz, lb, cum01, gwout)
    grad_x, rwin, dmx, dmc, gnw0 = _b1_in_bwd(idx1, ctx[0], x[0], dx1, dg, nw0, msel, win)

    dmod0 = jnp.concatenate([dmx, dgate0], axis=0)
    dmodc = jnp.concatenate([dmc, jnp.zeros((1, D), F32)], axis=0)
    pd = jnp.stack([dmod0, dmod1, dmodc]).reshape(3, NDEV, SH_ADA)
    pv = jnp.concatenate([gnw0, gnw1, ggw, gfw, dlb.reshape(1, 2 * E), gps, lossv], axis=1)
    g_ada, g_adab, g_cctx, pvsum, loss128 = _reduce_small(pd, pv, cg, cctx2, ada_w[0])

    g2 = (4 * SH_GRP, PG)
    sharded_names = ["hgrn_w_in", "hgrn_w_out", "pool_w_in", "pool_w_grp", "pool_w_out"]
    sharded = [(rwin, hgrn_w_in[0], m_hgrn_w_in[0], v_hgrn_w_in[0]),
               (rwout, hgrn_w_out[0], m_hgrn_w_out[0], v_hgrn_w_out[0]),
               (rpwin, pool_w_in[0], m_pool_w_in[0], v_pool_w_in[0]),
               (rpgrp.reshape((NDEV,) + g2), pool_w_grp[0].reshape(g2), m_pool_w_grp[0].reshape(g2), v_pool_w_grp[0].reshape(g2)),
               (rpwout, pool_w_out[0], m_pool_w_out[0], v_pool_w_out[0])]
    a2 = (2 * D, SH_ADA)
    g_ada2 = g_ada.reshape(a2)
    dense = [(g_ada2, ada_w.reshape(a2), m_ada_w.reshape(a2), v_ada_w.reshape(a2))]
    lb_me = lax.dynamic_slice_in_dim(lb, idx * DH, DH, axis=1)
    small_names = ["c_ctx", "ada_b", "norm_w", "hgrn_lb_logits", "hgrn_gnorm_w", "pool_scale", "final_norm_w"]
    small = [(g_cctx, cctx2, m_c_ctx.reshape(1, D), v_c_ctx.reshape(1, D)),
             (g_adab.reshape(2, 3 * D), ada_b, m_ada_b, v_ada_b),
             (pvsum[:, PV_NW:PV_NW + 2 * D].reshape(2, D), norm_w, m_norm_w, v_norm_w),
             (lax.dynamic_slice_in_dim(pvsum[:, PV_LB:PV_LB + 2 * E].reshape(2, E), idx * DH, DH, axis=1),
              hgrn_lb_logits[0], m_hgrn_lb_logits[0], v_hgrn_lb_logits[0]),
             (pvsum[:, PV_GNORM:PV_GNORM + E], hgrn_gnorm_w, m_hgrn_gnorm_w, v_hgrn_gnorm_w),
             (lax.dynamic_slice_in_dim(pvsum[:, PV_PSCALE:PV_PSCALE + E], idx * DH, DH, axis=1), pool_scale, m_pool_scale, v_pool_scale),
             (pvsum[:, PV_FINAL:PV_FINAL + D], fnw, m_final_norm_w.reshape(1, D), v_final_norm_w.reshape(1, D))]
    r_sharded, r_dense, r_small = _adam_all(sharded, dense, small, 3, lb_me)
    out = dict(zip(sharded_names, r_sharded))
    out["ada_w"] = (g_ada2,) + r_dense[0]
    out.update(zip(small_names, r_small))

    shapes = {"c_ctx": (D,), "ada_w": (2, D, SH_ADA), "ada_b": (2, 3 * D), "norm_w": (2, D), "hgrn_w_in": (1, D, SH_WIN),
              "hgrn_lb_logits": (1, 2, DH), "hgrn_gnorm_w": (1, E), "hgrn_w_out": (1, SH_ROWS, D), "pool_w_in": (1, D, SH_PWIN),
              "pool_w_grp": (1, 4, SH_GRP, PG), "pool_scale": (1, DH), "pool_w_out": (1, SH_ROWS, D), "final_norm_w": (D,)}
    order = ["c_ctx", "ada_w", "ada_b", "norm_w", "hgrn_w_in", "hgrn_lb_logits", "hgrn_gnorm_w", "hgrn_w_out", "pool_w_in",
             "pool_w_grp", "pool_scale", "pool_w_out", "final_norm_w"]
    flat = [out[name][q].reshape(shapes[name]) for q in range(4) for name in order]
    return (loss128[0, 0], grad_x[None], *flat)
```

```python
import functools

import numpy as np
import jax
import jax.numpy as jnp
from jax import lax
from jax.experimental import pallas as pl
from jax.experimental.pallas import tpu as pltpu

F32 = jnp.float32
BF16 = jnp.bfloat16

D = 1024
E = 1024
HEADS = 8
DH = 128
CHUNK = 64
T = 2048
TC = 256
TT = T + TC
TM = 256
NT = TT // TM
NTX = T // TM
NDEV = 8
GRID_W = 64
POOL_WINDOWS = (2, 4, 8, 16)
PG = 256
EPS = 1e-6
WIN_COLS = 5 * E
SH_WIN = WIN_COLS // NDEV
SH_PWIN = 2 * E // NDEV
SH_ROWS = E // NDEV
SH_GRP = PG // NDEV
SH_ADA = 3 * D // NDEV
VMEM_LIMIT = 56 * 1024 * 1024

ADAM_LR, ADAM_B1, ADAM_B2, ADAM_EPS, ADAM_WD, ADAM_STEP = 0.001, 0.9, 0.999, 1e-08, 0.01, 10

MESH = pl.DeviceIdType.MESH
VMEM_SPEC = pl.BlockSpec(memory_space=pltpu.VMEM)
HBM_SPEC = pl.BlockSpec(memory_space=pltpu.HBM)
ANY_SPEC = pl.BlockSpec(memory_space=pl.ANY)


def _sds(shape, dtype):
    return jax.ShapeDtypeStruct(shape, dtype)


def _bf(a):
    return a if a.dtype == BF16 else a.astype(BF16)


def _dot(a, b):
    return lax.dot_general(_bf(a), _bf(b), (((1,), (0,)), ((), ())), preferred_element_type=F32)


def _dot_tb(a, b):
    return lax.dot_general(_bf(a), _bf(b), (((1,), (1,)), ((), ())), preferred_element_type=F32)


def _dot_ta(a, b):
    return lax.dot_general(_bf(a), _bf(b), (((0,), (0,)), ((), ())), preferred_element_type=F32)


def _bdot(a, b):
    return lax.dot_general(_bf(a), _bf(b), (((2,), (1,)), ((0,), (0,))), preferred_element_type=F32)


def _bdot_nt(a, b):
    return lax.dot_general(_bf(a), _bf(b), (((2,), (2,)), ((0,), (0,))), preferred_element_type=F32)


def _bdot_tn(a, b):
    return lax.dot_general(_bf(a), _bf(b), (((1,), (1,)), ((0,), (0,))), preferred_element_type=F32)


def _dot01(m01, x):
    hi = x.astype(BF16)
    lo = (x - hi.astype(F32)).astype(BF16)
    return _dot(m01, hi) + _dot(m01, lo)


def _rstd(x):
    return lax.rsqrt(jnp.mean(x * x, axis=-1, keepdims=True) + EPS)


def _sigmoid(x):
    return jax.nn.sigmoid(x)


def _colsum(a):
    return jnp.sum(a, axis=0, keepdims=True)


def _stack_rows(rows):
    n = rows[0].shape[-1]
    rid = lax.broadcasted_iota(jnp.int32, (16, n), 0)
    out = jnp.zeros((16, n), F32)
    for i, r in enumerate(rows):
        out = jnp.where(rid == i, r, out)
    return out


def _head_map(fn, *arrs):
    outs = [fn(*[a[:, h * DH:(h + 1) * DH] for a in arrs]) for h in range(HEADS)]
    return jnp.concatenate(outs, axis=1)


def _gla_consts():
    r = np.arange(TM)[:, None]
    c = np.arange(TM)[None, :]
    same = (r // CHUNK) == (c // CHUNK)
    tril = same & (c <= r)
    triu = same & (c >= r)
    m = np.stack([tril, triu]).astype(np.float32)
    return jnp.asarray(m, BF16), jnp.asarray(m, F32)


def _pool_consts():
    r = np.arange(TM)[:, None]
    c = np.arange(TM)[None, :]
    same = (r // GRID_W) == (c // GRID_W)
    rp, cp = r % GRID_W, c % GRID_W
    bs, inv = [], []
    for w in POOL_WINDOWS:
        lo = np.clip(rp - w // 2, 0, GRID_W)
        hi = np.clip(rp - w // 2 + w, 0, GRID_W)
        bs.append(same & (cp >= lo) & (cp < hi))
        inv.append(1.0 / (hi - lo).astype(np.float32))
    b = np.stack(bs).astype(np.float32)
    bt = np.transpose(b, (0, 2, 1))
    return jnp.asarray(b, BF16), jnp.asarray(bt, BF16), jnp.asarray(np.stack(inv), F32)


def _mesh_pos():
    x, y, c = lax.axis_index("x"), lax.axis_index("y"), lax.axis_index("c")
    return x, y, c, 4 * x + 2 * y + c


def _peer(x, y, c, k):
    return (x ^ ((k >> 2) & 1), y ^ ((k >> 1) & 1), c ^ (k & 1))


def _small_gathers(refs, ssem, rsem):
    lb_r, ps_r, c_r, cctx_r, ada_r, adab_r, lb_o, ps_o, cg_o, mod_o, lb_out, ps_out, cg_out, mod0_o, mod1_o, modc_o = refs
    x, y, cc, idx = _mesh_pos()
    srcs = [lb_r, ps_r, c_r, mod_o.at[idx]]
    mine = [lb_o.at[idx], ps_o.at[idx], cg_o.at[idx], mod_o.at[idx]]

    def remote(a, k):
        return pltpu.make_async_remote_copy(src_ref=srcs[a], dst_ref=mine[a], send_sem=ssem.at[a, k], recv_sem=rsem.at[a, k],
                                            device_id=_peer(x, y, cc, k), device_id_type=MESH)

    first = [remote(a, k) for k in range(1, NDEV) for a in (2, 0, 1)]
    for cp in first:
        cp.start()
    lb_o[idx] = lb_r[...]
    ps_o[idx] = ps_r[...]
    cg_o[idx] = c_r[...]
    for k in range(1, NDEV):
        remote(2, k).wait_recv()
    rows = _stack_rows([cg_o[i] for i in range(NDEV)] + [cctx_r[...]])
    sc = rows * _sigmoid(rows)
    for l in range(2):
        mod_o[idx, l] = _dot(sc, ada_r[l])
    second = [remote(3, k) for k in range(1, NDEV)]
    for cp in second:
        cp.start()
    for k in range(1, NDEV):
        remote(3, k).wait_recv()

    def mod_rows(l, row):
        full = jnp.concatenate([mod_o[s, l, row, :] for s in range(NDEV)], axis=1) + adab_r[l:l + 1, :]
        return [full[:, j * D:(j + 1) * D] for j in range(3)]

    me = pl.ds(idx, 1)
    for out, parts in ((mod0_o, mod_rows(0, me)), (mod1_o, mod_rows(1, me)), (modc_o, mod_rows(0, slice(NDEV, NDEV + 1)))):
        for j in range(3):
            out[j:j + 1, :] = parts[j]
    for cp in first + second:
        cp.wait_send()
    for k in range(1, NDEV):
        for a in (0, 1):
            remote(a, k).wait_recv()
    lb_out[...] = lb_o[...]
    ps_out[...] = ps_o[...]
    cg_out[...] = cg_o[...]


def _gather_order(s, core):
    k = jnp.where(s == 2, 4, jnp.where(s == 4, 2, s))
    return k ^ jnp.where((core == 1) & (s >= 2) & (s <= 5), 6, 0)


GATHER_ISSUE = (1, 2, 4, 3, 5, 6, 7)
GATHER_ICI = (2, 4, 6)
GATHER_DIRECT = (1,) + GATHER_ICI
GLA_HB = 2
RS_SLOTS = 5


def _shard_of(kind, ref, i):
    if kind == "rows":
        return ref.at[pl.ds(pl.multiple_of(i * SH_ROWS, SH_ROWS), SH_ROWS), :]
    if kind == "major":
        return ref.at[i]
    assert kind == "grp"
    return ref.at[:, pl.ds(pl.multiple_of(i * SH_GRP, SH_GRP), SH_GRP), :]


def _gather_rider(step, n_steps, forward_at, kinds, srcs, outs, ssem, rsem, lsem):
    x, y, cc, idx = _mesh_pos()
    arrays = range(len(kinds))
    mine = [_shard_of(kinds[a], outs[a], idx) for a in arrays]

    def remote(a, k):
        return pltpu.make_async_remote_copy(src_ref=srcs[a], dst_ref=mine[a], send_sem=ssem.at[a, k], recv_sem=rsem.at[a, k],
                                            device_id=_peer(x, y, cc, k), device_id_type=MESH)

    def forward(a, k):
        blk = _shard_of(kinds[a], outs[a], idx ^ k)
        return pltpu.make_async_remote_copy(src_ref=blk, dst_ref=blk, send_sem=ssem.at[a, k ^ 1], recv_sem=rsem.at[a, k ^ 1],
                                            device_id=(x, y, 1 - cc), device_id_type=MESH)

    copies = [remote(a, k) for k in GATHER_DIRECT for a in arrays]
    passed = [forward(a, k) for k in GATHER_ICI for a in arrays]
    local = [pltpu.make_async_copy(srcs[a], mine[a], lsem.at[a]) for a in arrays]

    @pl.when(step == 0)
    def _():
        for cp in copies + local:
            cp.start()

    @pl.when(step == forward_at)
    def _():
        for k in GATHER_ICI:
            for a in arrays:
                remote(a, k).wait_recv()
                forward(a, k).start()

    @pl.when(step == n_steps - 1)
    def _():
        for cp in copies + passed:
            cp.wait_send()
        for a in arrays:
            remote(a, 1).wait_recv()
        for cp in passed:
            cp.wait_recv()
        for cp in local:
            cp.wait()


def _scatter_rider(step, n_steps, kinds, grads, slots, ssem, rsem, lsem):
    x, y, cc, idx = _mesh_pos()
    arrays = range(len(kinds))
    dsts = [slots[a].at[idx] for a in arrays]

    def remote(a, k):
        px, py, pc = _peer(x, y, cc, k)
        return pltpu.make_async_remote_copy(src_ref=_shard_of(kinds[a], grads[a], 4 * px + 2 * py + pc), dst_ref=dsts[a],
                                            send_sem=ssem.at[a, k], recv_sem=rsem.at[a, k], device_id=(px, py, pc), device_id_type=MESH)

    copies = [remote(a, k) for k in GATHER_ISSUE for a in arrays]
    local = [pltpu.make_async_copy(_shard_of(kinds[a], grads[a], idx), dsts[a], lsem.at[a]) for a in arrays]

    @pl.when(step == 0)
    def _():
        for cp in copies + local:
            cp.start()

    @pl.when(step == n_steps - 1)
    def _():
        for cp in copies:
            cp.wait_send()
        for cp in copies:
            cp.wait_recv()
        for cp in local:
            cp.wait()


def _rider_sems(n):
    return [pltpu.SemaphoreType.DMA((n, NDEV)), pltpu.SemaphoreType.DMA((n, NDEV)), pltpu.SemaphoreType.DMA((n,))]


def _scatter_rider2(step, n_steps, add_at, kinds, grads, slots, bufs, sems):
    x, y, cc, idx = _mesh_pos()
    sibling = (x, y, 1 - cc)
    arrays = range(len(kinds))
    psend, precv, isend, irecv, lown, sibsem, lself = sems

    def mine(a, i):
        return _shard_of(kinds[a], grads[a], i)

    def partial(a, p):
        return pltpu.make_async_remote_copy(src_ref=mine(a, idx ^ (2 * (p + 1)) ^ 1), dst_ref=bufs[a][1].at[p], send_sem=psend.at[a, p],
                                            recv_sem=precv.at[a, p], device_id=sibling, device_id_type=MESH)

    def load(a, p):
        return pltpu.make_async_copy(mine(a, idx ^ (2 * (p + 1))), bufs[a][0].at[p], lown.at[a, p])

    def chip_sum(a, p):
        return pltpu.make_async_remote_copy(src_ref=bufs[a][0].at[p], dst_ref=slots[a].at[2 + p], send_sem=isend.at[a, p],
                                            recv_sem=irecv.at[a, p], device_id=_peer(x, y, cc, 2 * (p + 1)), device_id_type=MESH)

    def to_sibling(a):
        return pltpu.make_async_remote_copy(src_ref=mine(a, idx ^ 1), dst_ref=slots[a].at[1], send_sem=sibsem.at[a, 0],
                                            recv_sem=sibsem.at[a, 1], device_id=sibling, device_id_type=MESH)

    def own(a):
        return pltpu.make_async_copy(mine(a, idx), slots[a].at[0], lself.at[a, 0])

    @pl.when(step == 0)
    def _():
        for a in arrays:
            for p in range(3):
                partial(a, p).start()
                load(a, p).start()
            to_sibling(a).start()
            own(a).start()

    @pl.when(step == add_at)
    def _():
        for a in arrays:
            for p in range(3):
                partial(a, p).wait_recv()
                load(a, p).wait()
                bufs[a][0][p] = (bufs[a][0][p].astype(F32) + bufs[a][1][p].astype(F32)).astype(BF16)
                chip_sum(a, p).start()

    @pl.when(step == n_steps - 1)
    def _():
        for a in arrays:
            for p in range(3):
                partial(a, p).wait_send()
                chip_sum(a, p).wait_send()
                chip_sum(a, p).wait_recv()
            to_sibling(a).wait_send()
            to_sibling(a).wait_recv()
            own(a).wait()


def _rider2_scratch(blocks):
    n = len(blocks)
    bufs = [pltpu.VMEM((3,) + tuple(b), BF16) for b in blocks for _ in range(2)]
    return bufs + [pltpu.SemaphoreType.DMA((n, 3)) for _ in range(5)] + [pltpu.SemaphoreType.DMA((n, 2)), pltpu.SemaphoreType.DMA((n, 1))]


def _rider2_split(refs, n):
    refs = list(refs)
    return [tuple(refs[2 * a:2 * a + 2]) for a in range(n)], tuple(refs[2 * n:2 * n + 7])


def _modulated(x, nw, shift, scale):
    r = _rstd(x)
    xn = x * r
    a = xn * nw
    return a * (1.0 + scale) + shift, r, xn, a


def _ctx_or_x(i, ctx_ref, x_ref):
    return jnp.where(i == 0, ctx_ref[...], x_ref[...])


def _f1_gather_matmul(idx1, ctx, x, nw, w_in, w_out, pw_in, pgrp, pw_out, lb_l, pscale, c, c_ctx, ada_w, ada_b):
    def body(idx_ref, ctx_ref, x_ref, nw_ref, win_r, wout_r, pwin_r, pgrp_r, pwout_r, lb_r, ps_r, c_r, cctx_r, ada_r, adab_r,
             g_ref, win_o, s_wout, s_pwin, s_pgrp, s_pwout, lb_o, ps_o, cg_o, mod0_o, mod1_o, modc_o,
             wslot, hx_sc, lb_g, ps_g, cg_g, mod_g, ssem, rsem, osem, dsem, sm_ssem, sm_rsem):
        del idx_ref
        s, i = pl.program_id(0), pl.program_id(1)
        x, y, cc, idx = _mesh_pos()
        k = _gather_order(s, cc)
        j = idx ^ k
        first = 4 - 2 * cc
        half = D // 2

        def remote(kk):
            return pltpu.make_async_remote_copy(src_ref=wslot.at[idx], dst_ref=wslot.at[idx], send_sem=ssem.at[kk], recv_sem=rsem.at[kk],
                                                device_id=_peer(x, y, cc, kk), device_id_type=MESH)

        def forward(kk):
            jj = idx ^ kk
            return pltpu.make_async_remote_copy(src_ref=wslot.at[jj], dst_ref=wslot.at[jj], send_sem=ssem.at[kk ^ 1],
                                                recv_sem=rsem.at[kk ^ 1], device_id=(x, y, 1 - cc), device_id_type=MESH)

        def relay(h):
            blk = wslot.at[idx ^ (4 >> h), pl.ds(h * half, half), :]
            return pltpu.make_async_remote_copy(src_ref=blk, dst_ref=blk, send_sem=dsem.at[2 * h], recv_sem=dsem.at[2 * h + 1],
                                                device_id=_peer(x, y, cc, 2 << h), device_id_type=MESH)

        def to_hbm(jj, kk):
            return pltpu.make_async_copy(wslot.at[jj], win_o.at[:, pl.ds(pl.multiple_of(jj * SH_WIN, 128), SH_WIN)], osem.at[kk])

        @pl.when((s == 0) & (i == 0))
        def _():
            _small_gathers((lb_r, ps_r, c_r, cctx_r, ada_r, adab_r, lb_g, ps_g, cg_g, mod_g, lb_o, ps_o, cg_o, mod0_o, mod1_o, modc_o),
                           sm_ssem, sm_rsem)
            wslot[idx] = win_r[...].astype(BF16)
            remote(1).start()
            remote(first).start()
            s_wout[...] = wout_r[...].astype(BF16)
            s_pwin[...] = pwin_r[...].astype(BF16)
            s_pgrp[...] = pgrp_r[...].astype(BF16)
            s_pwout[...] = pwout_r[...].astype(BF16)

        @pl.when(s == 0)
        def _():
            shift = jnp.where(i == 0, modc_o[0:1, :], mod0_o[0:1, :])
            scale = jnp.where(i == 0, modc_o[1:2, :], mod0_o[1:2, :])
            hx, _, _, _ = _modulated(_ctx_or_x(i, ctx_ref, x_ref), nw_ref[...], shift, scale)
            hx_sc[i] = hx.astype(BF16)

        @pl.when((s == 2) & (i == 0))
        def _():
            remote(6 - first).start()

        @pl.when((s > 0) & (i == 0) & (k != 6))
        def _():
            remote(k).wait_recv()

            @pl.when((k & 1) == 0)
            def _():
                forward(k).start()

            for h in range(2):
                @pl.when(k == 4 >> h)
                def _():
                    relay(h).start()

        @pl.when((i == 0) & (k == 6))
        def _():
            for h in range(2):
                relay(h).wait_recv()
            forward(6).start()

        @pl.when(i == 0)
        def _():
            to_hbm(j, k).start()

        g_ref[...] = jnp.dot(hx_sc[i], wslot[j], preferred_element_type=F32)

        @pl.when((s == NDEV - 1) & (i == NT - 1))
        def _():
            for kk in (1, 2, 4):
                remote(kk).wait_send()
            for kk in GATHER_ICI:
                forward(kk).wait_send()
            for h in range(2):
                relay(h).wait_send()
            for kk in range(NDEV):
                to_hbm(idx ^ kk, kk).wait()

    grid_spec = pltpu.PrefetchScalarGridSpec(
        num_scalar_prefetch=1, grid=(NDEV, NT),
        in_specs=[VMEM_SPEC, pl.BlockSpec((TM, D), lambda s, i, ix: (jnp.maximum(i - 1, 0), 0))] + [VMEM_SPEC] * 12,
        out_specs=[pl.BlockSpec((TM, SH_WIN), lambda s, i, ix: (i, ix[0] ^ _gather_order(s, ix[0] & 1))), HBM_SPEC] + [VMEM_SPEC] * 10,
        scratch_shapes=[pltpu.VMEM((NDEV, D, SH_WIN), BF16), pltpu.VMEM((NT, TM, D), BF16),
                        pltpu.VMEM((NDEV, 2, DH), F32), pltpu.VMEM((NDEV, 1, DH), F32), pltpu.VMEM((NDEV, 1, D), F32),
                        pltpu.VMEM((NDEV, 2, 16, SH_ADA), F32),
                        pltpu.SemaphoreType.DMA((NDEV,)), pltpu.SemaphoreType.DMA((NDEV,)), pltpu.SemaphoreType.DMA((NDEV,)),
                        pltpu.SemaphoreType.DMA((4,)),
                        pltpu.SemaphoreType.DMA((4, NDEV)), pltpu.SemaphoreType.DMA((4, NDEV))])
    outs = (_sds((TT, WIN_COLS), F32), _sds((D, WIN_COLS), BF16),
            _sds((SH_ROWS, D), BF16), _sds((D, SH_PWIN), BF16), _sds((4, SH_GRP, PG), BF16), _sds((SH_ROWS, D), BF16),
            _sds((NDEV, 2, DH), F32), _sds((NDEV, 1, DH), F32), _sds((NDEV, 1, D), F32),
            _sds((3, D), F32), _sds((3, D), F32), _sds((3, D), F32))
    return pl.pallas_call(
        body, name="f1_gather_matmul", grid_spec=grid_spec, out_shape=outs,
        compiler_params=pltpu.CompilerParams(dimension_semantics=("arbitrary", "arbitrary"), vmem_limit_bytes=VMEM_LIMIT),
    )(idx1, ctx, x, nw, w_in, w_out, pw_in, pgrp, pw_out, lb_l, pscale, c, c_ctx, ada_w, ada_b)


def _gla_gates(pre, qpre, lbd, cum, rev):
    rows, n = pre.shape
    nch = rows // CHUNK
    sig = _sigmoid(pre)
    f = lbd + (1.0 - lbd) * sig
    k = 1.0 - f
    g = _dot01(cum, jnp.log(f))
    g3 = g.reshape(nch, CHUNK, n)
    last = 0 if rev else CHUNK - 1
    mid = CHUNK // 2 if rev else CHUNK // 2 - 1
    gl1, gm1 = g3[:, last:last + 1, :], g3[:, mid:mid + 1, :]

    def bc(a):
        return jnp.broadcast_to(a, g3.shape).reshape(rows, n)

    gm = bc(gm1)
    e_q, e_k = jnp.exp(g - gm), jnp.exp(gm - g)
    qsig = _sigmoid(qpre)
    qs = qpre * qsig * (DH ** -0.5)
    return dict(sig=sig, f=f, k=k, qsig=qsig, qs=qs, e_q=e_q, e_k=e_k,
                e_mid=[jnp.exp(gm1[ci]) for ci in range(nch)], e_rest=[jnp.exp(gl1[ci] - gm1[ci]) for ci in range(nch)])


def _put_heads(ref, lead, arr):
    for h in range(HEADS):
        ref[lead + (h,)] = arr[:, h * DH:(h + 1) * DH]


def _get_heads(ref, lead=()):
    return jnp.concatenate([ref[lead + (h,)] for h in range(HEADS)], axis=1)


def _gla_prep(g_all, lb, cum01, s_wout, s_pgrp):
    nch = TM // CHUNK

    def body(g_ref, lb_ref, cum_ref, swout_r, spgrp_r, p0_ref, p1_ref, v_ref, dec_ref, wout_o, pgrp_o, ssem, rsem, lsem):
        _gather_rider(pl.program_id(0), NT, NT - 1, ("rows", "grp"), (swout_r, spgrp_r), (wout_o, pgrp_o), ssem, rsem, lsem)
        qpre = g_ref[:, 3 * E:4 * E]
        _put_heads(v_ref, (), g_ref[:, 2 * E:3 * E].astype(BF16))
        for d, p_ref in ((0, p0_ref), (1, p1_ref)):
            t = _gla_gates(g_ref[:, d * E:(d + 1) * E], qpre, lb_ref[d:d + 1, :], cum_ref[d], d == 1)
            _put_heads(p_ref, (0,), (t["qs"] * t["e_q"]).astype(BF16))
            _put_heads(p_ref, (1,), (t["k"] * t["e_k"]).astype(BF16))
            for ci in range(nch):
                dec_ref[d, 0, ci:ci + 1, :] = t["e_mid"][ci]
                dec_ref[d, 0, nch + ci:nch + ci + 1, :] = t["e_rest"][ci]

    quad = pl.BlockSpec((2, HEADS, TM, DH), lambda i: (0, 0, i, 0))
    return pl.pallas_call(
        body, name="gla_prep", grid=(NT,),
        in_specs=[pl.BlockSpec((TM, 4 * E), lambda i: (i, 0)), VMEM_SPEC, VMEM_SPEC, HBM_SPEC, HBM_SPEC],
        out_specs=[quad, quad, pl.BlockSpec((HEADS, TM, DH), lambda i: (0, i, 0)), pl.BlockSpec((2, 1, 2 * nch, E), lambda i: (0, i, 0, 0)),
                   HBM_SPEC, HBM_SPEC],
        out_shape=(_sds((2, HEADS, TT, DH), BF16), _sds((2, HEADS, TT, DH), BF16), _sds((HEADS, TT, DH), BF16), _sds((2, NT, 2 * nch, E), F32),
                   _sds((E, D), BF16), _sds((4, PG, PG), BF16)),
        scratch_shapes=_rider_sems(2),
        compiler_params=pltpu.CompilerParams(dimension_semantics=("arbitrary",), vmem_limit_bytes=VMEM_LIMIT),
    )(g_all, lb, cum01, s_wout, s_pgrp)


def _scan_tile(i, rev):
    t = jnp.where(i == 0, 0, NT - i) if rev else i
    return t, pl.ds(pl.multiple_of(t * TM, TM), TM)


def _chunk_order(rev):
    n = TM // CHUNK
    return tuple(range(n - 1, -1, -1)) if rev else tuple(range(n))


def _chunk_rows(dec_ref, lanes, cis, where):
    nch = TM // CHUNK

    def rows(off):
        return jnp.stack([dec_ref[d, where[d][0], off + ci:off + ci + 1, hh * DH:(hh + 1) * DH] for (d, hh), ci in zip(lanes, cis)])

    return rows(0), rows(nch)


def _gla_fwd(p0, p1, v_all, dec, mask01, s_pwin):
    n_steps = HEADS // GLA_HB

    def body(p0_ref, p1_ref, v_ref, dec_ref, msk_ref, spwin_r, o_ref, pwin_o, ob_sc, ssem, rsem, lsem):
        _gather_rider(pl.program_id(0), n_steps, n_steps - 1, ("major",), (spwin_r,), (pwin_o,), ssem, rsem, lsem)

        lanes = [(d, hh) for d in (0, 1) for hh in range(GLA_HB)]
        nch = TM // CHUNK

        def tile_body(i, st):
            where = [_scan_tile(i, d == 1) for d in (0, 1)]

            def stacked(fn):
                return jnp.stack([fn(d, hh, where[d][1]) for d, hh in lanes])

            qg, kg = [stacked(lambda d, hh, rows, ty=ty: (p1_ref if d else p0_ref)[ty, hh, rows, :]) for ty in range(2)]
            v = stacked(lambda d, hh, rows: v_ref[hh, rows, :])
            a = _bdot_nt(qg, kg) * jnp.stack([msk_ref[d] for d, _ in lanes])
            intra = _bdot(a, v)
            outs = [[None] * nch for _ in lanes]
            for n in range(nch):
                cis = [nch - 1 - n if d else n for d, _ in lanes]

                def chunk(arr):
                    return jnp.stack([arr[l, ci * CHUNK:(ci + 1) * CHUNK] for l, ci in enumerate(cis)])

                e_mid, e_rest = _chunk_rows(dec_ref, lanes, cis, where)
                inter = _bdot_nt(chunk(qg), st * e_mid)
                for l, ci in enumerate(cis):
                    outs[l][ci] = inter[l] + intra[l, ci * CHUNK:(ci + 1) * CHUNK]
                st = st * (e_mid * e_rest) + _bdot_tn(chunk(v), chunk(kg)) * e_rest
            for l, (d, hh) in enumerate(lanes):
                (ob_sc if d else o_ref)[hh, where[d][1], :] = jnp.concatenate(outs[l], axis=0)
            return st

        lax.fori_loop(0, NT, tile_body, jnp.zeros((len(lanes), DH, DH), F32))
        o_ref[...] += ob_sc[...]

    quad = pl.BlockSpec((2, GLA_HB, TT, DH), lambda h: (0, h, 0, 0))
    head = pl.BlockSpec((GLA_HB, TT, DH), lambda h: (h, 0, 0))
    return pl.pallas_call(
        body, name="gla_fwd", grid=(n_steps,),
        in_specs=[quad, quad, head, pl.BlockSpec((2, NT, 8, GLA_HB * DH), lambda h: (0, 0, 0, h)),
                  pl.BlockSpec((2, TM, TM), lambda h: (0, 0, 0)), HBM_SPEC],
        out_specs=[head, HBM_SPEC],
        out_shape=(_sds((HEADS, TT, DH), F32), _sds((NDEV, D, SH_PWIN), BF16)),
        scratch_shapes=[pltpu.VMEM((GLA_HB, TT, DH), F32)] + _rider_sems(1),
        compiler_params=pltpu.CompilerParams(dimension_semantics=("arbitrary",), vmem_limit_bytes=VMEM_LIMIT),
    )(p0, p1, v_all, dec, mask01, s_pwin)


def _gated_norm(o, z, gw):
    r = _head_map(lambda oh: jnp.broadcast_to(_rstd(oh), oh.shape), o)
    on = o * r
    zs = _sigmoid(z)
    sz = z * zs
    return on * gw * sz, r, on, zs, sz


def _f3_out(o, g_all, x, gate, gw, wout, s_pwout):
    def body(o_ref, z_ref, x_ref, gate_ref, gw_ref, w_ref, spwout_r, x1_ref, pwout_o, ssem, rsem, lsem):
        _gather_rider(pl.program_id(0), NTX, NTX - 1, ("rows",), (spwout_r,), (pwout_o,), ssem, rsem, lsem)
        og, _, _, _, _ = _gated_norm(_get_heads(o_ref), z_ref[...], gw_ref[...])
        x1_ref[...] = x_ref[...] + gate_ref[...] * _dot(og, w_ref[...])

    return pl.pallas_call(
        body, name="f3_out", grid=(NTX,),
        in_specs=[pl.BlockSpec((HEADS, TM, DH), lambda i: (0, i + 1, 0)), pl.BlockSpec((TM, E), lambda i: (i + 1, 4)),
                  pl.BlockSpec((TM, D), lambda i: (i, 0)), pl.BlockSpec((1, D), lambda i: (0, 0)),
                  pl.BlockSpec((1, E), lambda i: (0, 0)), pl.BlockSpec((E, D), lambda i: (0, 0)), HBM_SPEC],
        out_specs=[pl.BlockSpec((TM, D), lambda i: (i, 0)), HBM_SPEC],
        out_shape=(_sds((T, D), F32), _sds((E, D), BF16)),
        scratch_shapes=_rider_sems(1),
        compiler_params=pltpu.CompilerParams(dimension_semantics=("arbitrary",)),
    )(o, g_all, x, gate, gw, wout, s_pwout)


def _pool_layer(x1, tgt, mod1, nw1, fnw, pwin, pgrp, pscale, pwout, pb, pbt, pinv):
    def body(x_ref, t_ref, m_ref, nw_ref, fw_ref, pwin_ref, pgrp_ref, ps_ref, pwout_ref, pb_ref, pbt_ref, pinv_ref,
             dx_ref, gpwin_o, gpgrp_o, gpwout_o, dmod_o, gnw_o, gfw_o, gps_o, loss_o,
             a_pwin, a_pgrp, a_pwout):
        i = pl.program_id(0)

        @pl.when(i == 0)
        def _():
            for ref in (a_pwin, a_pgrp, a_pwout, dmod_o, gnw_o, gfw_o, gps_o, loss_o):
                ref[...] = jnp.zeros_like(ref)

        shift, scale, gate = m_ref[0:1, :], m_ref[1:2, :], m_ref[2:3, :]
        nw, fw, ps = nw_ref[...], fw_ref[...], ps_ref[...]
        x1 = x_ref[...]
        hx, r1, xn, a = _modulated(x1, nw, shift, scale)
        hxb = hx.astype(BF16)
        uz = jnp.concatenate([_dot(hxb, pwin_ref[j]) for j in range(NDEV)], axis=1)
        u, z = uz[:, :E], uz[:, E:]
        pooled, ys = [], []
        for g in range(4):
            ug = u[:, g * PG:(g + 1) * PG]
            pg = _dot01(pb_ref[g], ug) * pinv_ref[g] - ug
            pooled.append(pg.astype(BF16))
            ys.append(_dot(pooled[g], pgrp_ref[g]))
        ycat = jnp.concatenate(ys, axis=1)
        y = ycat * ps
        zs = _sigmoid(z)
        sz = z * zs
        p = (y * sz).astype(BF16)
        out = _dot(p, pwout_ref[...])
        x2 = x1 + gate * out
        r2 = _rstd(x2)
        xn2 = x2 * r2
        diff = xn2 * fw - t_ref[...]
        loss_o[...] += _colsum(diff * diff)
        dyf = diff * (1.0 / D)
        gfw_o[...] += _colsum(dyf * xn2)
        dxn2 = dyf * fw
        dx2 = r2 * (dxn2 - xn2 * jnp.mean(dxn2 * xn2, axis=-1, keepdims=True))
        dgate = _colsum(dx2 * out)
        dout = (dx2 * gate).astype(BF16)
        for j in range(4):
            cs = slice(j * PG, (j + 1) * PG)
            a_pwout[:, cs] += _dot_ta(p, dout[:, cs])
        dp = _dot_tb(dout, pwout_ref[...])
        dy = dp * sz
        dz = dp * y * (zs * (1.0 + z * (1.0 - zs)))
        gps_o[...] += _colsum(dy * ycat)
        dycat = dy * ps
        dus = []
        for g in range(4):
            dyg = dycat[:, g * PG:(g + 1) * PG].astype(BF16)
            a_pgrp[g] += _dot_ta(pooled[g], dyg)
            dpg = _dot_tb(dyg, pgrp_ref[g])
            dus.append(_dot01(pbt_ref[g], dpg * pinv_ref[g]) - dpg)
        duz = jnp.concatenate(dus + [dz], axis=1).astype(BF16)
        dhx = None
        for j in range(NDEV):
            dj = duz[:, j * SH_PWIN:(j + 1) * SH_PWIN]
            a_pwin[j] += _dot_ta(hxb, dj)
            part = _dot_tb(dj, pwin_ref[j])
            dhx = part if dhx is None else dhx + part
        dmod_o[0:1, :] += _colsum(dhx)
        dmod_o[1:2, :] += _colsum(dhx * a)
        dmod_o[2:3, :] += dgate
        da = dhx * (1.0 + scale)
        gnw_o[...] += _colsum(da * xn)
        dxn = da * nw
        dx_ref[...] = dx2 + r1 * (dxn - xn * jnp.mean(dxn * xn, axis=-1, keepdims=True))

        @pl.when(i == NTX - 1)
        def _():
            gpwin_o[...] = a_pwin[...].astype(BF16)
            gpgrp_o[...] = a_pgrp[...].astype(BF16)
            gpwout_o[...] = a_pwout[...].astype(BF16)

    tile = pl.BlockSpec((TM, D), lambda i: (i, 0))
    outs = (_sds((T, D), F32), _sds((NDEV, D, SH_PWIN), BF16), _sds((4, PG, PG), BF16), _sds((E, D), BF16),
            _sds((3, D), F32), _sds((1, D), F32), _sds((1, D), F32), _sds((1, E), F32), _sds((1, D), F32))
    return pl.pallas_call(
        body, name="pool_layer", grid=(NTX,),
        in_specs=[tile, tile] + [VMEM_SPEC] * 10,
        out_specs=[tile] + [VMEM_SPEC] * 8,
        out_shape=outs,
        scratch_shapes=[pltpu.VMEM((NDEV, D, SH_PWIN), F32), pltpu.VMEM((4, PG, PG), F32), pltpu.VMEM((E, D), F32)],
        compiler_params=pltpu.CompilerParams(dimension_semantics=("arbitrary",), vmem_limit_bytes=VMEM_LIMIT),
    )(x1, tgt, mod1, nw1, fnw, pwin, pgrp, pscale, pwout, pb, pbt, pinv)


def _b3_out_bwd(dx1, o, g_all, gate, gw, wout, gpwout):
    def body(dx_ref, o_ref, z_ref, gate_ref, gw_ref, w_ref, gpwout_r, do_ref, dz_ref, gw_o, dgate_o, ggw_o, rpwout_o,
             acc, *rider):
        i = pl.program_id(0)
        bufs, sems = _rider2_split(rider, 1)
        _scatter_rider2(i, NT, 2, ("rows",), (gpwout_r,), (rpwout_o,), bufs, sems)

        @pl.when(i == 0)
        def _():
            acc[...] = jnp.zeros_like(acc)
            dgate_o[...] = jnp.zeros_like(dgate_o)
            ggw_o[...] = jnp.zeros_like(ggw_o)
            do_ref[...] = jnp.zeros_like(do_ref)
            dz_ref[...] = jnp.zeros_like(dz_ref)

        @pl.when(i > 0)
        def _():
            gw = gw_ref[...]
            z = z_ref[...]
            og, r, on, zs, sz = _gated_norm(_get_heads(o_ref), z, gw)
            ogb = og.astype(BF16)
            dx = dx_ref[...]
            dgate_o[...] += _colsum(dx * _dot(ogb, w_ref[...]))
            dy = (dx * gate_ref[...]).astype(BF16)
            for j in range(4):
                cs = slice(j * PG, (j + 1) * PG)
                acc[:, cs] += _dot_ta(ogb, dy[:, cs])
            dog = _dot_tb(dy, w_ref[...])
            dz_ref[...] = (dog * (on * gw) * (zs * (1.0 + z * (1.0 - zs)))).astype(BF16)
            dong = dog * sz
            ggw_o[...] += _colsum(dong * on)
            don = dong * gw
            do = _head_map(lambda dh, nh, rh: rh * (dh - nh * jnp.mean(dh * nh, axis=-1, keepdims=True)), don, on, r)
            _put_heads(do_ref, (), do.astype(BF16))

        @pl.when(i == NT - 1)
        def _():
            gw_o[...] = acc[...].astype(BF16)

    prev = lambda i: (jnp.maximum(i - 1, 0), 0)
    heads = pl.BlockSpec((HEADS, TM, DH), lambda i: (0, i, 0))
    return pl.pallas_call(
        body, name="b3_out_bwd", grid=(NT,),
        in_specs=[pl.BlockSpec((TM, D), prev), heads, pl.BlockSpec((TM, E), lambda i: (i, 4)),
                  VMEM_SPEC, VMEM_SPEC, VMEM_SPEC, HBM_SPEC],
        out_specs=[heads, pl.BlockSpec((TM, E), lambda i: (i, 0)), VMEM_SPEC, VMEM_SPEC, VMEM_SPEC, HBM_SPEC],
        out_shape=(_sds((HEADS, TT, DH), BF16), _sds((TT, E), BF16), _sds((E, D), BF16), _sds((1, D), F32), _sds((1, E), F32),
                   _sds((RS_SLOTS, SH_ROWS, D), BF16)),
        scratch_shapes=[pltpu.VMEM((E, D), F32)] + _rider2_scratch([(SH_ROWS, D)]),
        compiler_params=pltpu.CompilerParams(dimension_semantics=("arbitrary",), vmem_limit_bytes=VMEM_LIMIT),
    )(dx1, o, g_all, gate, gw, wout, gpwout)


def _gla_bwd(p0, p1, v_all, dec, do, mask01, gpwin, gpgrp):
    nch = TM // CHUNK
    n_steps = HEADS // GLA_HB

    def body(p0_ref, p1_ref, v_ref, dec_ref, do_ref, msk_ref, gpwin_r, gpgrp_r, d0_ref, d1_ref, dv_ref, dgl_ref, rpwin_o, rpgrp_o,
             ss_sc, dv_sc, ssem, rsem, lsem, *rider):
        _scatter_rider(pl.program_id(0), n_steps, ("grp",), (gpgrp_r,), (rpgrp_o,), ssem, rsem, lsem)
        bufs, sems = _rider2_split(rider, 1)
        _scatter_rider2(pl.program_id(0), n_steps, 1, ("major",), (gpwin_r,), (rpwin_o,), bufs, sems)

        lanes = [(d, hh) for d in (0, 1) for hh in range(GLA_HB)]
        zero = jnp.zeros((len(lanes), DH, DH), F32)
        dgl_ref[...] = jnp.zeros_like(dgl_ref)

        def p_of(d):
            return p1_ref if d else p0_ref

        def scan_step(i, n):
            where = [_scan_tile(i, d == 1) for d in (0, 1)]
            cis = [nch - 1 - n if d else n for d, _ in lanes]
            e_mid, e_rest = _chunk_rows(dec_ref, lanes, cis, where)

            def chunk(arr):
                return jnp.stack([arr[l, ci * CHUNK:(ci + 1) * CHUNK] for l, ci in enumerate(cis)])

            return where, cis, e_mid, e_rest, chunk

        def stacked(i, fn):
            where = [_scan_tile(i, d == 1) for d in (0, 1)]
            return jnp.stack([fn(d, hh, where[d][1]) for d, hh in lanes])

        def fwd_body(i, st):
            v = stacked(i, lambda d, hh, rows: v_ref[hh, rows, :])
            kg = stacked(i, lambda d, hh, rows: p_of(d)[1, hh, rows, :])
            for n in range(nch):
                _, _, e_mid, e_rest, chunk = scan_step(i, n)
                ss_sc[i * nch + n] = st
                st = st * (e_mid * e_rest) + _bdot_tn(chunk(v), chunk(kg)) * e_rest
            return st

        ss_sc[NT * nch] = lax.fori_loop(0, NT, fwd_body, zero)

        def bwd_body(ii, dst):
            i = NT - 1 - ii
            qg, kg = [stacked(i, lambda d, hh, rows, ty=ty: p_of(d)[ty, hh, rows, :]) for ty in range(2)]
            v = stacked(i, lambda d, hh, rows: v_ref[hh, rows, :])
            dob = stacked(i, lambda d, hh, rows: do_ref[hh, rows, :])
            msk = jnp.stack([msk_ref[d] for d, _ in lanes])
            a = (_bdot_nt(qg, kg) * msk).astype(BF16)
            da = (_bdot_nt(dob, v) * msk).astype(BF16)
            dqg = _bdot(da, kg)
            dkg = _bdot_tn(da, qg)
            dv_intra = _bdot_tn(a, dob)
            dv_l, dkg_l, dqg_l = ([[None] * nch for _ in lanes] for _ in range(3))
            for n in range(nch - 1, -1, -1):
                where, cis, e_mid, e_rest, chunk = scan_step(i, n)
                s_c, s_end = ss_sc[i * nch + n], ss_sc[i * nch + n + 1]
                dste = (dst * e_rest).astype(BF16)
                kg_c, v_c, dob_c = chunk(kg), chunk(v), chunk(dob)
                dv_c = chunk(dv_intra) + _bdot_nt(kg_c, dste)
                dkg_c = chunk(dkg) + _bdot(v_c, dste)
                dqg_c = chunk(dqg) + _bdot(dob_c, s_c * e_mid)
                dgl = jnp.sum(s_end * dst, axis=1, keepdims=True)
                for l, ((d, hh), ci) in enumerate(zip(lanes, cis)):
                    dv_l[l][ci], dkg_l[l][ci], dqg_l[l][ci] = dv_c[l], dkg_c[l], dqg_c[l]
                    dgl_ref[d, where[d][0], ci:ci + 1, hh * DH:(hh + 1) * DH] = dgl[l]
                dst = dst * (e_mid * e_rest) + _bdot_tn(dob_c, chunk(qg)) * e_mid
            where = [_scan_tile(i, d == 1) for d in (0, 1)]
            for l, (d, hh) in enumerate(lanes):
                rows = where[d][1]
                d_ref = d1_ref if d else d0_ref
                d_ref[0, hh, rows, :] = jnp.concatenate(dqg_l[l], axis=0).astype(BF16)
                d_ref[1, hh, rows, :] = jnp.concatenate(dkg_l[l], axis=0).astype(BF16)
                dv_sc[d, hh, rows, :] = jnp.concatenate(dv_l[l], axis=0).astype(BF16)
            return dst

        lax.fori_loop(0, NT, bwd_body, zero)
        dv_ref[...] = (dv_sc[0].astype(F32) + dv_sc[1].astype(F32)).astype(BF16)

    quad = pl.BlockSpec((2, GLA_HB, TT, DH), lambda h: (0, h, 0, 0))
    col = pl.BlockSpec((GLA_HB, TT, DH), lambda h: (h, 0, 0))
    chunkv = pl.BlockSpec((2, NT, 8, GLA_HB * DH), lambda h: (0, 0, 0, h))
    outs = (_sds((2, HEADS, TT, DH), BF16), _sds((2, HEADS, TT, DH), BF16), _sds((HEADS, TT, DH), BF16), _sds((2, NT, 8, E), F32),
            _sds((RS_SLOTS, D, SH_PWIN), BF16), _sds((NDEV, 4, SH_GRP, PG), BF16))
    return pl.pallas_call(
        body, name="gla_bwd", grid=(n_steps,),
        in_specs=[quad, quad, col, chunkv, col, pl.BlockSpec((2, TM, TM), lambda h: (0, 0, 0)), HBM_SPEC, HBM_SPEC],
        out_specs=[quad, quad, col, chunkv, HBM_SPEC, HBM_SPEC],
        out_shape=outs,
        scratch_shapes=[pltpu.VMEM((NT * nch + 1, 2 * GLA_HB, DH, DH), F32), pltpu.VMEM((2, GLA_HB, TT, DH), BF16)] + _rider_sems(1)
        + _rider2_scratch([(D, SH_PWIN)]),
        compiler_params=pltpu.CompilerParams(dimension_semantics=("arbitrary",), vmem_limit_bytes=VMEM_LIMIT),
    )(p0, p1, v_all, dec, do, mask01, gpwin, gpgrp)


TMB = 128


def _gla_post_bwd(g_all, d0, d1, dgl, dv, dz, lb, cum01, gwout):
    nch = TMB // CHUNK

    def body(g_ref, d0_ref, d1_ref, dgl_ref, dv_ref, dz_ref, lb_ref, cum_ref, gwout_r, dg_ref, dlb_ref, rwout_o, *rider):
        i = pl.program_id(0)
        bufs, sems = _rider2_split(rider, 1)
        _scatter_rider2(i, TT // TMB, 2, ("rows",), (gwout_r,), (rwout_o,), bufs, sems)

        @pl.when(i == 0)
        def _():
            dlb_ref[...] = jnp.zeros_like(dlb_ref)

        half = i & 1
        qpre = g_ref[:, 3 * E:4 * E]
        dqs_sum = None
        dpre = []
        for d, d_ref in ((0, d0_ref), (1, d1_ref)):
            rev = d == 1
            lbd = lb_ref[d:d + 1, :]
            t = _gla_gates(g_ref[:, d * E:(d + 1) * E], qpre, lbd, cum_ref[d, :TMB, :TMB], rev)
            dqs = _get_heads(d_ref, (0,)).astype(F32) * t["e_q"]
            dk = _get_heads(d_ref, (1,)).astype(F32) * t["e_k"]
            dg = t["qs"] * dqs - t["k"] * dk
            dgl8 = dgl_ref[d, 0]
            dgl_rows = [jnp.where(half == 0, dgl8[ci:ci + 1, :], dgl8[nch + ci:nch + ci + 1, :]) for ci in range(nch)]
            dgl_b = jnp.concatenate([jnp.broadcast_to(dgl_rows[ci], (CHUNK, E)) for ci in range(nch)], axis=0)
            pos = lax.broadcasted_iota(jnp.int32, (TMB, E), 0) & (CHUNK - 1)
            dg = dg + jnp.where(pos == (0 if rev else CHUNK - 1), dgl_b, 0.0)
            dlf = _dot01(cum_ref[1 - d, :TMB, :TMB], dg)
            df = dlf / t["f"] - dk
            sig = t["sig"]
            dpre.append((df * (1.0 - lbd) * sig * (1.0 - sig)).astype(BF16))
            dlb_ref[d:d + 1, :] += _colsum(df * (1.0 - sig))
            dqs_sum = dqs if dqs_sum is None else dqs_sum + dqs
            qsig = t["qsig"]
        dqpre = dqs_sum * (DH ** -0.5) * (qsig * (1.0 + qpre * (1.0 - qsig)))
        dg_ref[...] = jnp.concatenate([dpre[0], dpre[1], _get_heads(dv_ref), dqpre.astype(BF16), dz_ref[...]], axis=1)

    quad = pl.BlockSpec((2, HEADS, TMB, DH), lambda i: (0, 0, i, 0))
    tile = pl.BlockSpec((TMB, E), lambda i: (i, 0))
    return pl.pallas_call(
        body, name="gla_post_bwd", grid=(TT // TMB,),
        in_specs=[pl.BlockSpec((TMB, 4 * E), lambda i: (i, 0)), quad, quad,
                  pl.BlockSpec((2, 1, 8, E), lambda i: (0, i // 2, 0, 0)), pl.BlockSpec((HEADS, TMB, DH), lambda i: (0, i, 0)), tile,
                  VMEM_SPEC, VMEM_SPEC, HBM_SPEC],
        out_specs=[pl.BlockSpec((TMB, WIN_COLS), lambda i: (i, 0)), VMEM_SPEC, HBM_SPEC],
        out_shape=(_sds((TT, WIN_COLS), BF16), _sds((2, E), F32), _sds((RS_SLOTS, SH_ROWS, D), BF16)),
        scratch_shapes=_rider2_scratch([(SH_ROWS, D)]),
        compiler_params=pltpu.CompilerParams(dimension_semantics=("arbitrary",), vmem_limit_bytes=VMEM_LIMIT),
    )(g_all, d0, d1, dgl, dv, dz, lb, cum01, gwout)


def _b1_in_bwd(idx1, ctx, x, dx1, dg, nw, msel, win):
    last_s = NDEV - 1

    def body(idx_ref, ctx_ref, x_ref, dx1_ref, dg_ref, nw_ref, m_ref, w_ref, gx_ref, rwin_o, dmx_o, dmc_o, gnw_o,
             hx_sc, dhx_sc, acc, sbuf, pbuf, psend, precv, isend, irecv, sibsem, lsem):
        del idx_ref
        s, i = pl.program_id(0), pl.program_id(1)
        x, y, cc, idx = _mesh_pos()
        shift, scale = m_ref[0, 0:1, :], m_ref[0, 1:2, :]
        sibling = (x, y, 1 - cc)

        def partial(p):
            return pltpu.make_async_remote_copy(src_ref=sbuf.at[0], dst_ref=pbuf.at[p], send_sem=psend.at[p], recv_sem=precv.at[p],
                                                device_id=sibling, device_id_type=MESH)

        def chip_sum(p):
            return pltpu.make_async_remote_copy(src_ref=sbuf.at[1], dst_ref=rwin_o.at[2 + p], send_sem=isend.at[p], recv_sem=irecv.at[p],
                                                device_id=_peer(x, y, cc, 2 * (p + 1)), device_id_type=MESH)

        to_sibling = pltpu.make_async_remote_copy(src_ref=sbuf.at[0], dst_ref=rwin_o.at[1], send_sem=sibsem.at[0], recv_sem=sibsem.at[1],
                                                  device_id=sibling, device_id_type=MESH)
        own = pltpu.make_async_copy(sbuf.at[1], rwin_o.at[0], lsem)

        @pl.when((s == 0) & (i == 0))
        def _():
            for ref in (dmx_o, dmc_o, gnw_o):
                ref[...] = jnp.zeros_like(ref)

        @pl.when(s == 0)
        def _():
            hx, _, _, _ = _modulated(_ctx_or_x(i, ctx_ref, x_ref), nw_ref[...], shift, scale)
            hx_sc[i] = hx.astype(BF16)

        @pl.when(i == 0)
        def _():
            acc[...] = jnp.zeros_like(acc)

        dgb = dg_ref[...]
        hxb = hx_sc[i]
        for lo, hi in ((0, 256), (256, 512), (512, SH_WIN)):
            acc[:, lo:hi] += _dot_ta(hxb, dgb[:, lo:hi])
        part = _dot_tb(dgb, w_ref[...])

        @pl.when(s == 0)
        def _():
            dhx_sc[i] = part

        @pl.when(s > 0)
        def _():
            dhx_sc[i] += part

        for p in (2, 1, 0):
            @pl.when((i == NT - 1) & (s == 2 * (2 - p)))
            def _(p=p):
                if p < 2:
                    partial(p + 1).wait_send()
                sbuf[0] = acc[...].astype(BF16)
                partial(p).start()

            @pl.when((i == NT - 1) & (s == 2 * (2 - p) + 1))
            def _(p=p):
                if p < 2:
                    chip_sum(p + 1).wait_send()
                partial(p).wait_recv()
                sbuf[1] = (acc[...] + pbuf[p].astype(F32)).astype(BF16)
                chip_sum(p).start()

        @pl.when((i == NT - 1) & (s == last_s - 1))
        def _():
            partial(0).wait_send()
            sbuf[0] = acc[...].astype(BF16)
            to_sibling.start()

        @pl.when((i == NT - 1) & (s == last_s))
        def _():
            chip_sum(0).wait_send()
            sbuf[1] = acc[...].astype(BF16)
            own.start()

        @pl.when(s == last_s)
        def _():
            nw = nw_ref[...]
            _, r, xn, a = _modulated(_ctx_or_x(i, ctx_ref, x_ref), nw, shift, scale)
            dhx = dhx_sc[i]
            dsh, dsc = _colsum(dhx), _colsum(dhx * a)
            da = dhx * (1.0 + scale)
            gnw_o[...] += _colsum(da * xn)
            dxn = da * nw
            gx_ref[...] = dx1_ref[...] + r * (dxn - xn * jnp.mean(dxn * xn, axis=-1, keepdims=True))

            @pl.when(i == 0)
            def _():
                dmc_o[0:1, :] += dsh
                dmc_o[1:2, :] += dsc

            @pl.when(i > 0)
            def _():
                dmx_o[0:1, :] += dsh
                dmx_o[1:2, :] += dsc

        @pl.when((i == NT - 1) & (s == last_s))
        def _():
            to_sibling.wait_send()
            to_sibling.wait_recv()
            for p in range(3):
                chip_sum(p).wait_recv()
            own.wait()

    grid_spec = pltpu.PrefetchScalarGridSpec(
        num_scalar_prefetch=1, grid=(NDEV, NT),
        in_specs=[VMEM_SPEC, pl.BlockSpec((TM, D), lambda s, i, ix: (jnp.maximum(i - 1, 0), 0)),
                  pl.BlockSpec((TM, D), lambda s, i, ix: (jnp.maximum(i - 1, 0), 0)),
                  pl.BlockSpec((TM, SH_WIN), lambda s, i, ix: (i, ix[0] ^ (last_s - s))), VMEM_SPEC,
                  pl.BlockSpec((1, 2, D), lambda s, i, ix: (jnp.minimum(i, 1), 0, 0)),
                  pl.BlockSpec((D, SH_WIN), lambda s, i, ix: (0, ix[0] ^ (last_s - s)))],
        out_specs=[pl.BlockSpec((TM, D), lambda s, i, ix: (jnp.where(s == last_s, jnp.maximum(i - 1, 0), 0), 0)),
                   HBM_SPEC, VMEM_SPEC, VMEM_SPEC, VMEM_SPEC],
        scratch_shapes=[pltpu.VMEM((NT, TM, D), BF16), pltpu.VMEM((NT, TM, D), F32), pltpu.VMEM((D, SH_WIN), F32),
                        pltpu.VMEM((2, D, SH_WIN), BF16), pltpu.VMEM((3, D, SH_WIN), BF16),
                        pltpu.SemaphoreType.DMA((3,)), pltpu.SemaphoreType.DMA((3,)), pltpu.SemaphoreType.DMA((3,)),
                        pltpu.SemaphoreType.DMA((3,)), pltpu.SemaphoreType.DMA((2,)), pltpu.SemaphoreType.DMA])
    return pl.pallas_call(
        body, name="b1_in_bwd", grid_spec=grid_spec,
        out_shape=(_sds((T, D), F32), _sds((RS_SLOTS, D, SH_WIN), BF16), _sds((2, D), F32), _sds((2, D), F32), _sds((1, D), F32)),
        compiler_params=pltpu.CompilerParams(dimension_semantics=("arbitrary", "arbitrary"), vmem_limit_bytes=VMEM_LIMIT),
    )(idx1, ctx, x, dx1, dg, nw, msel, win)


def _reduce_small(pd, pv, cg, c_ctx, ada_w0):
    n_arr = 3

    def body(pd_r, pv_r, cg_r, cctx_r, ada_r, gada_o, gadab_o, gcctx_o, pvsum_o, loss_o,
             pd_all, pv_all, dsc_all, dsc_mine, ssem, rsem):
        x, y, cc, idx = _mesh_pos()
        srcs = [pd_r, pv_r, dsc_mine]
        dsts = [pd_all.at[idx], pv_all.at[idx], dsc_all.at[idx]]

        def remote(a, k):
            return pltpu.make_async_remote_copy(src_ref=srcs[a], dst_ref=dsts[a], send_sem=ssem.at[a, k], recv_sem=rsem.at[a, k],
                                                device_id=_peer(x, y, cc, k), device_id_type=MESH)

        first = [remote(a, k) for k in range(1, NDEV) for a in (0, 1)]
        for cp in first:
            cp.start()
        pd_all[idx] = pd_r[...]
        pv_all[idx] = pv_r[...]
        for k in range(1, NDEV):
            remote(0, k).wait_recv()
            remote(1, k).wait_recv()
        mine = [pd_all[s, :, pl.ds(idx, 1), :] for s in range(NDEV)]
        dmc = functools.reduce(lambda u, v: u + v, [m[2] for m in mine])
        rows = _stack_rows([cg_r[i] for i in range(NDEV)] + [cctx_r[...]])
        sc = (rows * _sigmoid(rows)).astype(BF16)
        gada_o[0] = _dot_ta(sc, _stack_rows([m[0] for m in mine] + [dmc]))
        gada_o[1] = _dot_ta(sc, _stack_rows([m[1] for m in mine]))
        dsc_mine[...] = _dot_tb(jnp.broadcast_to(dmc, (8, SH_ADA)), ada_r[...])[0:1, :]
        dsc_all[idx] = dsc_mine[...]
        second = [remote(2, k) for k in range(1, NDEV)]
        for cp in second:
            cp.start()
        tot = [functools.reduce(lambda u, v: u + v, [pd_all[s, l] for s in range(NDEV)]) for l in range(3)]
        gadab_o[0] = tot[0] + tot[2]
        gadab_o[1] = tot[1]
        pvs = functools.reduce(lambda u, v: u + v, [pv_all[s] for s in range(NDEV)])
        pvsum_o[...] = pvs
        loss_o[...] = jnp.broadcast_to(jnp.sum(pvs[:, PV_LOSS:PV_LOSS + D], axis=-1, keepdims=True) * (0.5 / D), (1, 128))
        for k in range(1, NDEV):
            remote(2, k).wait_recv()
        dsc = functools.reduce(lambda u, v: u + v, [dsc_all[s] for s in range(NDEV)])
        cx = cctx_r[...]
        sx = _sigmoid(cx)
        gcctx_o[...] = dsc * (sx * (1.0 + cx * (1.0 - sx)))
        for cp in first + second:
            cp.wait_send()

    outs = (_sds((2, D, SH_ADA), F32), _sds((2, NDEV, SH_ADA), F32), _sds((1, D), F32), _sds((1, PV_LEN), F32), _sds((1, 128), F32))
    return pl.pallas_call(
        body, name="reduce_small", out_shape=outs,
        in_specs=[VMEM_SPEC] * 5, out_specs=[VMEM_SPEC] * 5,
        scratch_shapes=[
            pltpu.VMEM((NDEV, 3, NDEV, SH_ADA), F32), pltpu.VMEM((NDEV, 1, PV_LEN), F32), pltpu.VMEM((NDEV, 1, D), F32),
            pltpu.VMEM((1, D), F32),
            pltpu.SemaphoreType.DMA((n_arr, NDEV)), pltpu.SemaphoreType.DMA((n_arr, NDEV)),
        ],
        compiler_params=pltpu.CompilerParams(vmem_limit_bytes=VMEM_LIMIT),
    )(pd, pv, cg, c_ctx, ada_w0)


PV_NW, PV_GNORM, PV_FINAL, PV_LB, PV_PSCALE, PV_LOSS, PV_LEN = 0, 2 * D, 3 * D, 4 * D, 6 * D, 7 * D, 8 * D


def _adamw(w, g, m, v):
    m = ADAM_B1 * m + (1.0 - ADAM_B1) * g
    v = ADAM_B2 * v + (1.0 - ADAM_B2) * (g * g)
    m_hat = m / (1.0 - ADAM_B1 ** ADAM_STEP)
    v_hat = v / (1.0 - ADAM_B2 ** ADAM_STEP)
    delta = -ADAM_LR * (m_hat / (jnp.sqrt(v_hat) + ADAM_EPS) + ADAM_WD * w)
    return delta, m, v


ADAM_STEPS = 8


def _adam_all(sharded, dense, small, lb_idx, lbv):
    ns, nd, nsm = len(sharded), len(dense), len(small)

    def body(*refs):
        it = iter(refs)
        sh_in = [[next(it) for _ in range(4)] for _ in range(ns)]
        de_in = [[next(it) for _ in range(4)] for _ in range(nd)]
        sm_in = [[next(it) for _ in range(4)] for _ in range(nsm)]
        lb_r = next(it)
        sh_out = [[next(it) for _ in range(4)] for _ in range(ns)]
        de_out = [[next(it) for _ in range(3)] for _ in range(nd)]
        sm_out = [[next(it) for _ in range(4)] for _ in range(nsm)]
        for (p, w, m, v), outs in zip(sh_in, sh_out):
            g = p[0].astype(F32)
            for s in range(1, p.shape[0]):
                g = g + p[s].astype(F32)
            d, mn, vn = _adamw(w[...], g, m[...], v[...])
            outs[0][...], outs[1][...], outs[2][...], outs[3][...] = g, d, mn, vn
        for (g, w, m, v), outs in zip(de_in, de_out):
            d, mn, vn = _adamw(w[...], g[...], m[...], v[...])
            outs[0][...], outs[1][...], outs[2][...] = d, mn, vn

        @pl.when(pl.program_id(0) == 0)
        def _():
            for j, ((g, w, m, v), outs) in enumerate(zip(sm_in, sm_out)):
                gj = g[...]
                if j == lb_idx:
                    gj = gj * lb_r[...] * (1.0 - lb_r[...])
                d, mn, vn = _adamw(w[...], gj, m[...], v[...])
                outs[0][...], outs[1][...], outs[2][...], outs[3][...] = gj, d, mn, vn

    def tile(a):
        return pl.BlockSpec((a.shape[0] // ADAM_STEPS, a.shape[1]), lambda i: (i, 0))

    in_specs, out_specs, out_shape, args = [], [], [], []
    for p, w, m, v in sharded:
        in_specs += [pl.BlockSpec((p.shape[0], p.shape[1] // ADAM_STEPS, p.shape[2]), lambda i: (0, i, 0))] + [tile(w)] * 3
        args += [p, w, m, v]
    for g, w, m, v in dense:
        in_specs += [tile(w)] * 4
        args += [g, w, m, v]
    for g, w, m, v in small:
        in_specs += [VMEM_SPEC] * 4
        args += [g, w, m, v]
    in_specs.append(VMEM_SPEC)
    args.append(lbv)
    for _, w, _, _ in sharded:
        out_specs += [tile(w)] * 4
        out_shape += [_sds(w.shape, F32)] * 4
    for _, w, _, _ in dense:
        out_specs += [tile(w)] * 3
        out_shape += [_sds(w.shape, F32)] * 3
    for _, w, _, _ in small:
        out_specs += [VMEM_SPEC] * 4
        out_shape += [_sds(w.shape, F32)] * 4
    res = pl.pallas_call(body, name="adam_all", grid=(ADAM_STEPS,), in_specs=in_specs, out_specs=out_specs, out_shape=tuple(out_shape),
                         compiler_params=pltpu.CompilerParams(dimension_semantics=("arbitrary",), vmem_limit_bytes=VMEM_LIMIT))(*args)
    it = iter(res)
    return ([tuple(next(it) for _ in range(4)) for _ in range(ns)], [tuple(next(it) for _ in range(3)) for _ in range(nd)],
            [tuple(next(it) for _ in range(4)) for _ in range(nsm)])


def kernel(x, c, ctx, c_ctx, ada_w, ada_b, norm_w, hgrn_w_in, hgrn_lb_logits, hgrn_gnorm_w, hgrn_w_out, pool_w_in, pool_w_grp, pool_scale, pool_w_out, final_norm_w, loss_target, m_c_ctx, m_ada_w, m_ada_b, m_norm_w, m_hgrn_w_in, m_hgrn_lb_logits, m_hgrn_gnorm_w, m_hgrn_w_out, m_pool_w_in, m_pool_w_grp, m_pool_scale, m_pool_w_out, m_final_norm_w, v_c_ctx, v_ada_w, v_ada_b, v_norm_w, v_hgrn_w_in, v_hgrn_lb_logits, v_hgrn_gnorm_w, v_hgrn_w_out, v_pool_w_in, v_pool_w_grp, v_pool_scale, v_pool_w_out, v_final_norm_w):
    idx = 4 * lax.axis_index("x") + 2 * lax.axis_index("y") + lax.axis_index("c")
    cctx2 = c_ctx.reshape(1, D)
    cum01, mask01 = _gla_consts()
    pb, pbt, pinv = _pool_consts()

    idx1 = idx.reshape(1).astype(jnp.int32)
    nw0, nw1 = norm_w[0:1], norm_w[1:2]
    fnw = final_norm_w.reshape(1, D)
    g_all, win, s_wout, s_pwin, s_pgrp, s_pwout, lbl_g, ps_g, cg, mod0, mod1, modc = _f1_gather_matmul(
        idx1, ctx[0], x[0], nw0, hgrn_w_in[0], hgrn_w_out[0], pool_w_in[0], pool_w_grp[0], pool_w_out[0], hgrn_lb_logits[0],
        pool_scale, c, cctx2, ada_w, ada_b)
    lb = jax.nn.sigmoid(jnp.transpose(lbl_g, (1, 0, 2)).reshape(2, E))
    pscale = ps_g.reshape(1, E)
    msel = jnp.stack([modc[:2], mod0[:2]])
    p0, p1, v_all, dec, wout, pgrp = _gla_prep(g_all, lb, cum01, s_wout, s_pgrp)
    o, pwin = _gla_fwd(p0, p1, v_all, dec, mask01, s_pwin)
    x1, pwout = _f3_out(o, g_all, x[0], mod0[2:3], hgrn_gnorm_w, wout, s_pwout)
    dx1, gpwin, gpgrp, gpwout, dmod1, gnw1, gfw, gps, lossv = _pool_layer(
        x1, loss_target[0], mod1, nw1, fnw, pwin, pgrp, pscale, pwout, pb, pbt, pinv)
    do, dz, gwout, dgate0, ggw, rpwout = _b3_out_bwd(dx1, o, g_all, mod0[2:3], hgrn_gnorm_w, wout, gpwout)
    d0, d1, dv, dgl, rpwin, rpgrp = _gla_bwd(p0, p1, v_all, dec, do, mask01, gpwin, gpgrp)
    dg, dlb, rwout = _gla_post_bwd(g_all, d0, d1, dgl, dv, dz, lb, cum01, gwout)
    grad_x, rwin, dmx, dmc, gnw0 = _b1_in_bwd(idx1, ctx[0], x[0], dx1, dg, nw0, msel, win)

    dmod0 = jnp.concatenate([dmx, dgate0], axis=0)
    dmodc = jnp.concatenate([dmc, jnp.zeros((1, D), F32)], axis=0)
    pd = jnp.stack([dmod0, dmod1, dmodc]).reshape(3, NDEV, SH_ADA)
    pv = jnp.concatenate([gnw0, gnw1, ggw, gfw, dlb.reshape(1, 2 * E), gps, lossv], axis=1)
    g_ada, g_adab, g_cctx, pvsum, loss128 = _reduce_small(pd, pv, cg, cctx2, ada_w[0])

    g2 = (4 * SH_GRP, PG)
    sharded_names = ["hgrn_w_in", "hgrn_w_out", "pool_w_in", "pool_w_grp", "pool_w_out"]
    sharded = [(rwin, hgrn_w_in[0], m_hgrn_w_in[0], v_hgrn_w_in[0]),
               (rwout, hgrn_w_out[0], m_hgrn_w_out[0], v_hgrn_w_out[0]),
               (rpwin, pool_w_in[0], m_pool_w_in[0], v_pool_w_in[0]),
               (rpgrp.reshape((NDEV,) + g2), pool_w_grp[0].reshape(g2), m_pool_w_grp[0].reshape(g2), v_pool_w_grp[0].reshape(g2)),
               (rpwout, pool_w_out[0], m_pool_w_out[0], v_pool_w_out[0])]
    a2 = (2 * D, SH_ADA)
    g_ada2 = g_ada.reshape(a2)
    dense = [(g_ada2, ada_w.reshape(a2), m_ada_w.reshape(a2), v_ada_w.reshape(a2))]
    lb_me = lax.dynamic_slice_in_dim(lb, idx * DH, DH, axis=1)
    small_names = ["c_ctx", "ada_b", "norm_w", "hgrn_lb_logits", "hgrn_gnorm_w", "pool_scale", "final_norm_w"]
    small = [(g_cctx, cctx2, m_c_ctx.reshape(1, D), v_c_ctx.reshape(1, D)),
             (g_adab.reshape(2, 3 * D), ada_b, m_ada_b, v_ada_b),
             (pvsum[:, PV_NW:PV_NW + 2 * D].reshape(2, D), norm_w, m_norm_w, v_norm_w),
             (lax.dynamic_slice_in_dim(pvsum[:, PV_LB:PV_LB + 2 * E].reshape(2, E), idx * DH, DH, axis=1),
              hgrn_lb_logits[0], m_hgrn_lb_logits[0], v_hgrn_lb_logits[0]),
             (pvsum[:, PV_GNORM:PV_GNORM + E], hgrn_gnorm_w, m_hgrn_gnorm_w, v_hgrn_gnorm_w),
             (lax.dynamic_slice_in_dim(pvsum[:, PV_PSCALE:PV_PSCALE + E], idx * DH, DH, axis=1), pool_scale, m_pool_scale, v_pool_scale),
             (pvsum[:, PV_FINAL:PV_FINAL + D], fnw, m_final_norm_w.reshape(1, D), v_final_norm_w.reshape(1, D))]
    r_sharded, r_dense, r_small = _adam_all(sharded, dense, small, 3, lb_me)
    out = dict(zip(sharded_names, r_sharded))
    out["ada_w"] = (g_ada2,) + r_dense[0]
    out.update(zip(small_names, r_small))

    shapes = {"c_ctx": (D,), "ada_w": (2, D, SH_ADA), "ada_b": (2, 3 * D), "norm_w": (2, D), "hgrn_w_in": (1, D, SH_WIN),
              "hgrn_lb_logits": (1, 2, DH), "hgrn_gnorm_w": (1, E), "hgrn_w_out": (1, SH_ROWS, D), "pool_w_in": (1, D, SH_PWIN),
              "pool_w_grp": (1, 4, SH_GRP, PG), "pool_scale": (1, DH), "pool_w_out": (1, SH_ROWS, D), "final_norm_w": (D,)}
    order = ["c_ctx", "ada_w", "ada_b", "norm_w", "hgrn_w_in", "hgrn_lb_logits", "hgrn_gnorm_w", "hgrn_w_out", "pool_w_in",
             "pool_w_grp", "pool_scale", "pool_w_out", "final_norm_w"]
    flat = [out[name][q].reshape(shapes[name]) for q in range(4) for name in order]
    return (loss128[0, 0], grad_x[None], *flat)
```

```python
import functools

import numpy as np
import jax
import jax.numpy as jnp
from jax import lax
from jax.experimental import pallas as pl
from jax.experimental.pallas import tpu as pltpu

F32 = jnp.float32
BF16 = jnp.bfloat16

D = 1024
E = 1024
HEADS = 8
DH = 128
CHUNK = 64
T = 2048
TC = 256
TT = T + TC
TM = 256
NT = TT // TM
NTX = T // TM
NDEV = 8
GRID_W = 64
POOL_WINDOWS = (2, 4, 8, 16)
PG = 256
EPS = 1e-6
WIN_COLS = 5 * E
SH_WIN = WIN_COLS // NDEV
SH_PWIN = 2 * E // NDEV
SH_ROWS = E // NDEV
SH_GRP = PG // NDEV
SH_ADA = 3 * D // NDEV
VMEM_LIMIT = 56 * 1024 * 1024

ADAM_LR, ADAM_B1, ADAM_B2, ADAM_EPS, ADAM_WD, ADAM_STEP = 0.001, 0.9, 0.999, 1e-08, 0.01, 10

MESH = pl.DeviceIdType.MESH
VMEM_SPEC = pl.BlockSpec(memory_space=pltpu.VMEM)
HBM_SPEC = pl.BlockSpec(memory_space=pltpu.HBM)
ANY_SPEC = pl.BlockSpec(memory_space=pl.ANY)


def _sds(shape, dtype):
    return jax.ShapeDtypeStruct(shape, dtype)


def _bf(a):
    return a if a.dtype == BF16 else a.astype(BF16)


def _dot(a, b):
    return lax.dot_general(_bf(a), _bf(b), (((1,), (0,)), ((), ())), preferred_element_type=F32)


def _dot_tb(a, b):
    return lax.dot_general(_bf(a), _bf(b), (((1,), (1,)), ((), ())), preferred_element_type=F32)


def _dot_ta(a, b):
    return lax.dot_general(_bf(a), _bf(b), (((0,), (0,)), ((), ())), preferred_element_type=F32)


def _bdot(a, b):
    return lax.dot_general(_bf(a), _bf(b), (((2,), (1,)), ((0,), (0,))), preferred_element_type=F32)


def _bdot_nt(a, b):
    return lax.dot_general(_bf(a), _bf(b), (((2,), (2,)), ((0,), (0,))), preferred_element_type=F32)


def _bdot_tn(a, b):
    return lax.dot_general(_bf(a), _bf(b), (((1,), (1,)), ((0,), (0,))), preferred_element_type=F32)


def _dot01(m01, x):
    hi = x.astype(BF16)
    lo = (x - hi.astype(F32)).astype(BF16)
    return _dot(m01, hi) + _dot(m01, lo)


def _rstd(x):
    return lax.rsqrt(jnp.mean(x * x, axis=-1, keepdims=True) + EPS)


def _sigmoid(x):
    return jax.nn.sigmoid(x)


def _colsum(a):
    return jnp.sum(a, axis=0, keepdims=True)


def _stack_rows(rows):
    n = rows[0].shape[-1]
    rid = lax.broadcasted_iota(jnp.int32, (16, n), 0)
    out = jnp.zeros((16, n), F32)
    for i, r in enumerate(rows):
        out = jnp.where(rid == i, r, out)
    return out


def _head_map(fn, *arrs):
    outs = [fn(*[a[:, h * DH:(h + 1) * DH] for a in arrs]) for h in range(HEADS)]
    return jnp.concatenate(outs, axis=1)


def _gla_consts():
    r = np.arange(TM)[:, None]
    c = np.arange(TM)[None, :]
    same = (r // CHUNK) == (c // CHUNK)
    tril = same & (c <= r)
    triu = same & (c >= r)
    m = np.stack([tril, triu]).astype(np.float32)
    return jnp.asarray(m, BF16), jnp.asarray(m, F32)


def _pool_consts():
    r = np.arange(TM)[:, None]
    c = np.arange(TM)[None, :]
    same = (r // GRID_W) == (c // GRID_W)
    rp, cp = r % GRID_W, c % GRID_W
    bs, inv = [], []
    for w in POOL_WINDOWS:
        lo = np.clip(rp - w // 2, 0, GRID_W)
        hi = np.clip(rp - w // 2 + w, 0, GRID_W)
        bs.append(same & (cp >= lo) & (cp < hi))
        inv.append(1.0 / (hi - lo).astype(np.float32))
    b = np.stack(bs).astype(np.float32)
    bt = np.transpose(b, (0, 2, 1))
    return jnp.asarray(b, BF16), jnp.asarray(bt, BF16), jnp.asarray(np.stack(inv), F32)


def _mesh_pos():
    x, y, c = lax.axis_index("x"), lax.axis_index("y"), lax.axis_index("c")
    return x, y, c, 4 * x + 2 * y + c


def _peer(x, y, c, k):
    return (x ^ ((k >> 2) & 1), y ^ ((k >> 1) & 1), c ^ (k & 1))


def _small_gathers(refs, ssem, rsem):
    lb_r, ps_r, c_r, cctx_r, ada_r, adab_r, lb_o, ps_o, cg_o, mod_o, lb_out, ps_out, cg_out, mod0_o, mod1_o, modc_o = refs
    x, y, cc, idx = _mesh_pos()
    srcs = [lb_r, ps_r, c_r, mod_o.at[idx]]
    mine = [lb_o.at[idx], ps_o.at[idx], cg_o.at[idx], mod_o.at[idx]]

    def remote(a, k):
        return pltpu.make_async_remote_copy(src_ref=srcs[a], dst_ref=mine[a], send_sem=ssem.at[a, k], recv_sem=rsem.at[a, k],
                                            device_id=_peer(x, y, cc, k), device_id_type=MESH)

    first = [remote(a, k) for k in range(1, NDEV) for a in (2, 0, 1)]
    for cp in first:
        cp.start()
    lb_o[idx] = lb_r[...]
    ps_o[idx] = ps_r[...]
    cg_o[idx] = c_r[...]
    for k in range(1, NDEV):
        remote(2, k).wait_recv()
    rows = _stack_rows([cg_o[i] for i in range(NDEV)] + [cctx_r[...]])
    sc = rows * _sigmoid(rows)
    for l in range(2):
        mod_o[idx, l] = _dot(sc, ada_r[l])
    second = [remote(3, k) for k in range(1, NDEV)]
    for cp in second:
        cp.start()
    for k in range(1, NDEV):
        remote(3, k).wait_recv()

    def mod_rows(l, row):
        full = jnp.concatenate([mod_o[s, l, row, :] for s in range(NDEV)], axis=1) + adab_r[l:l + 1, :]
        return [full[:, j * D:(j + 1) * D] for j in range(3)]

    me = pl.ds(idx, 1)
    for out, parts in ((mod0_o, mod_rows(0, me)), (mod1_o, mod_rows(1, me)), (modc_o, mod_rows(0, slice(NDEV, NDEV + 1)))):
        for j in range(3):
            out[j:j + 1, :] = parts[j]
    for cp in first + second:
        cp.wait_send()
    for k in range(1, NDEV):
        for a in (0, 1):
            remote(a, k).wait_recv()
    lb_out[...] = lb_o[...]
    ps_out[...] = ps_o[...]
    cg_out[...] = cg_o[...]


def _gather_order(s, core):
    k = jnp.where(s == 2, 4, jnp.where(s == 4, 2, s))
    return k ^ jnp.where((core == 1) & (s >= 2) & (s <= 5), 6, 0)


GATHER_ISSUE = (1, 2, 4, 3, 5, 6, 7)
GATHER_ICI = (2, 4, 6)
GATHER_DIRECT = (1,) + GATHER_ICI
GLA_HB = 2
RS_SLOTS = 5


def _shard_of(kind, ref, i):
    if kind == "rows":
        return ref.at[pl.ds(pl.multiple_of(i * SH_ROWS, SH_ROWS), SH_ROWS), :]
    if kind == "major":
        return ref.at[i]
    assert kind == "grp"
    return ref.at[:, pl.ds(pl.multiple_of(i * SH_GRP, SH_GRP), SH_GRP), :]


def _gather_rider(step, n_steps, forward_at, kinds, srcs, outs, ssem, rsem, lsem):
    x, y, cc, idx = _mesh_pos()
    arrays = range(len(kinds))
    mine = [_shard_of(kinds[a], outs[a], idx) for a in arrays]

    def remote(a, k):
        return pltpu.make_async_remote_copy(src_ref=srcs[a], dst_ref=mine[a], send_sem=ssem.at[a, k], recv_sem=rsem.at[a, k],
                                            device_id=_peer(x, y, cc, k), device_id_type=MESH)

    def forward(a, k):
        blk = _shard_of(kinds[a], outs[a], idx ^ k)
        return pltpu.make_async_remote_copy(src_ref=blk, dst_ref=blk, send_sem=ssem.at[a, k ^ 1], recv_sem=rsem.at[a, k ^ 1],
                                            device_id=(x, y, 1 - cc), device_id_type=MESH)

    copies = [remote(a, k) for k in GATHER_DIRECT for a in arrays]
    passed = [forward(a, k) for k in GATHER_ICI for a in arrays]
    local = [pltpu.make_async_copy(srcs[a], mine[a], lsem.at[a]) for a in arrays]

    @pl.when(step == 0)
    def _():
        for cp in copies + local:
            cp.start()

    @pl.when(step == forward_at)
    def _():
        for k in GATHER_ICI:
            for a in arrays:
                remote(a, k).wait_recv()
                forward(a, k).start()

    @pl.when(step == n_steps - 1)
    def _():
        for cp in copies + passed:
            cp.wait_send()
        for a in arrays:
            remote(a, 1).wait_recv()
        for cp in passed:
            cp.wait_recv()
        for cp in local:
            cp.wait()


def _scatter_rider(step, n_steps, kinds, grads, slots, ssem, rsem, lsem):
    x, y, cc, idx = _mesh_pos()
    arrays = range(len(kinds))
    dsts = [slots[a].at[idx] for a in arrays]

    def remote(a, k):
        px, py, pc = _peer(x, y, cc, k)
        return pltpu.make_async_remote_copy(src_ref=_shard_of(kinds[a], grads[a], 4 * px + 2 * py + pc), dst_ref=dsts[a],
                                            send_sem=ssem.at[a, k], recv_sem=rsem.at[a, k], device_id=(px, py, pc), device_id_type=MESH)

    copies = [remote(a, k) for k in GATHER_ISSUE for a in arrays]
    local = [pltpu.make_async_copy(_shard_of(kinds[a], grads[a], idx), dsts[a], lsem.at[a]) for a in arrays]

    @pl.when(step == 0)
    def _():
        for cp in copies + local:
            cp.start()

    @pl.when(step == n_steps - 1)
    def _():
        for cp in copies:
            cp.wait_send()
        for cp in copies:
            cp.wait_recv()
        for cp in local:
            cp.wait()


def _rider_sems(n):
    return [pltpu.SemaphoreType.DMA((n, NDEV)), pltpu.SemaphoreType.DMA((n, NDEV)), pltpu.SemaphoreType.DMA((n,))]


def _scatter_rider2(step, n_steps, add_at, kinds, grads, slots, bufs, sems):
    x, y, cc, idx = _mesh_pos()
    sibling = (x, y, 1 - cc)
    arrays = range(len(kinds))
    psend, precv, isend, irecv, lown, sibsem, lself = sems

    def mine(a, i):
        return _shard_of(kinds[a], grads[a], i)

    def partial(a, p):
        return pltpu.make_async_remote_copy(src_ref=mine(a, idx ^ (2 * (p + 1)) ^ 1), dst_ref=bufs[a][1].at[p], send_sem=psend.at[a, p],
                                            recv_sem=precv.at[a, p], device_id=sibling, device_id_type=MESH)

    def load(a, p):
        return pltpu.make_async_copy(mine(a, idx ^ (2 * (p + 1))), bufs[a][0].at[p], lown.at[a, p])

    def chip_sum(a, p):
        return pltpu.make_async_remote_copy(src_ref=bufs[a][0].at[p], dst_ref=slots[a].at[2 + p], send_sem=isend.at[a, p],
                                            recv_sem=irecv.at[a, p], device_id=_peer(x, y, cc, 2 * (p + 1)), device_id_type=MESH)

    def to_sibling(a):
        return pltpu.make_async_remote_copy(src_ref=mine(a, idx ^ 1), dst_ref=slots[a].at[1], send_sem=sibsem.at[a, 0],
                                            recv_sem=sibsem.at[a, 1], device_id=sibling, device_id_type=MESH)

    def own(a):
        return pltpu.make_async_copy(mine(a, idx), slots[a].at[0], lself.at[a, 0])

    @pl.when(step == 0)
    def _():
        for a in arrays:
            for p in range(3):
                partial(a, p).start()
                load(a, p).start()
            to_sibling(a).start()
            own(a).start()

    @pl.when(step == add_at)
    def _():
        for a in arrays:
            for p in range(3):
                partial(a, p).wait_recv()
                load(a, p).wait()
                bufs[a][0][p] = (bufs[a][0][p].astype(F32) + bufs[a][1][p].astype(F32)).astype(BF16)
                chip_sum(a, p).start()

    @pl.when(step == n_steps - 1)
    def _():
        for a in arrays:
            for p in range(3):
                partial(a, p).wait_send()
                chip_sum(a, p).wait_send()
                chip_sum(a, p).wait_recv()
            to_sibling(a).wait_send()
            to_sibling(a).wait_recv()
            own(a).wait()


def _rider2_scratch(blocks):
    n = len(blocks)
    bufs = [pltpu.VMEM((3,) + tuple(b), BF16) for b in blocks for _ in range(2)]
    return bufs + [pltpu.SemaphoreType.DMA((n, 3)) for _ in range(5)] + [pltpu.SemaphoreType.DMA((n, 2)), pltpu.SemaphoreType.DMA((n, 1))]


def _rider2_split(refs, n):
    refs = list(refs)
    return [tuple(refs[2 * a:2 * a + 2]) for a in range(n)], tuple(refs[2 * n:2 * n + 7])


def _modulated(x, nw, shift, scale):
    r = _rstd(x)
    xn = x * r
    a = xn * nw
    return a * (1.0 + scale) + shift, r, xn, a


def _ctx_or_x(i, ctx_ref, x_ref):
    return jnp.where(i == 0, ctx_ref[...], x_ref[...])


def _f1_gather_matmul(idx1, ctx, x, nw, w_in, w_out, pw_in, pgrp, pw_out, lb_l, pscale, c, c_ctx, ada_w, ada_b):
    def body(idx_ref, ctx_ref, x_ref, nw_ref, win_r, wout_r, pwin_r, pgrp_r, pwout_r, lb_r, ps_r, c_r, cctx_r, ada_r, adab_r,
             g_ref, win_o, s_wout, s_pwin, s_pgrp, s_pwout, lb_o, ps_o, cg_o, mod0_o, mod1_o, modc_o,
             wslot, hx_sc, lb_g, ps_g, cg_g, mod_g, ssem, rsem, osem, dsem, sm_ssem, sm_rsem):
        del idx_ref
        s, i = pl.program_id(0), pl.program_id(1)
        x, y, cc, idx = _mesh_pos()
        k = _gather_order(s, cc)
        j = idx ^ k
        first = 4 - 2 * cc
        half = D // 2

        def remote(kk):
            return pltpu.make_async_remote_copy(src_ref=wslot.at[idx], dst_ref=wslot.at[idx], send_sem=ssem.at[kk], recv_sem=rsem.at[kk],
                                                device_id=_peer(x, y, cc, kk), device_id_type=MESH)

        def forward(kk):
            jj = idx ^ kk
            return pltpu.make_async_remote_copy(src_ref=wslot.at[jj], dst_ref=wslot.at[jj], send_sem=ssem.at[kk ^ 1],
                                                recv_sem=rsem.at[kk ^ 1], device_id=(x, y, 1 - cc), device_id_type=MESH)

        def relay(h):
            blk = wslot.at[idx ^ (4 >> h), pl.ds(h * half, half), :]
            return pltpu.make_async_remote_copy(src_ref=blk, dst_ref=blk, send_sem=dsem.at[2 * h], recv_sem=dsem.at[2 * h + 1],
                                                device_id=_peer(x, y, cc, 2 << h), device_id_type=MESH)

        def to_hbm(jj, kk):
            return pltpu.make_async_copy(wslot.at[jj], win_o.at[:, pl.ds(pl.multiple_of(jj * SH_WIN, 128), SH_WIN)], osem.at[kk])

        @pl.when((s == 0) & (i == 0))
        def _():
            _small_gathers((lb_r, ps_r, c_r, cctx_r, ada_r, adab_r, lb_g, ps_g, cg_g, mod_g, lb_o, ps_o, cg_o, mod0_o, mod1_o, modc_o),
                           sm_ssem, sm_rsem)
            wslot[idx] = win_r[...].astype(BF16)
            remote(1).start()
            remote(first).start()
            s_wout[...] = wout_r[...].astype(BF16)
            s_pwin[...] = pwin_r[...].astype(BF16)
            s_pgrp[...] = pgrp_r[...].astype(BF16)
            s_pwout[...] = pwout_r[...].astype(BF16)

        @pl.when(s == 0)
        def _():
            shift = jnp.where(i == 0, modc_o[0:1, :], mod0_o[0:1, :])
            scale = jnp.where(i == 0, modc_o[1:2, :], mod0_o[1:2, :])
            hx, _, _, _ = _modulated(_ctx_or_x(i, ctx_ref, x_ref), nw_ref[...], shift, scale)
            hx_sc[i] = hx.astype(BF16)

        @pl.when((s == 2) & (i == 0))
        def _():
            remote(6 - first).start()

        @pl.when((s > 0) & (i == 0) & (k != 6))
        def _():
            remote(k).wait_recv()

            @pl.when((k & 1) == 0)
            def _():
                forward(k).start()

            for h in range(2):
                @pl.when(k == 4 >> h)
                def _():
                    relay(h).start()

        @pl.when((i == 0) & (k == 6))
        def _():
            for h in range(2):
                relay(h).wait_recv()
            forward(6).start()

        @pl.when(i == 0)
        def _():
            to_hbm(j, k).start()

        g_ref[...] = jnp.dot(hx_sc[i], wslot[j], preferred_element_type=F32)

        @pl.when((s == NDEV - 1) & (i == NT - 1))
        def _():
            for kk in (1, 2, 4):
                remote(kk).wait_send()
            for kk in GATHER_ICI:
                forward(kk).wait_send()
            for h in range(2):
                relay(h).wait_send()
            for kk in range(NDEV):
                to_hbm(idx ^ kk, kk).wait()

    grid_spec = pltpu.PrefetchScalarGridSpec(
        num_scalar_prefetch=1, grid=(NDEV, NT),
        in_specs=[VMEM_SPEC, pl.BlockSpec((TM, D), lambda s, i, ix: (jnp.maximum(i - 1, 0), 0))] + [VMEM_SPEC] * 12,
        out_specs=[pl.BlockSpec((TM, SH_WIN), lambda s, i, ix: (i, ix[0] ^ _gather_order(s, ix[0] & 1))), HBM_SPEC] + [VMEM_SPEC] * 10,
        scratch_shapes=[pltpu.VMEM((NDEV, D, SH_WIN), BF16), pltpu.VMEM((NT, TM, D), BF16),
                        pltpu.VMEM((NDEV, 2, DH), F32), pltpu.VMEM((NDEV, 1, DH), F32), pltpu.VMEM((NDEV, 1, D), F32),
                        pltpu.VMEM((NDEV, 2, 16, SH_ADA), F32),
                        pltpu.SemaphoreType.DMA((NDEV,)), pltpu.SemaphoreType.DMA((NDEV,)), pltpu.SemaphoreType.DMA((NDEV,)),
                        pltpu.SemaphoreType.DMA((4,)),
                        pltpu.SemaphoreType.DMA((4, NDEV)), pltpu.SemaphoreType.DMA((4, NDEV))])
    outs = (_sds((TT, WIN_COLS), F32), _sds((D, WIN_COLS), BF16),
            _sds((SH_ROWS, D), BF16), _sds((D, SH_PWIN), BF16), _sds((4, SH_GRP, PG), BF16), _sds((SH_ROWS, D), BF16),
            _sds((NDEV, 2, DH), F32), _sds((NDEV, 1, DH), F32), _sds((NDEV, 1, D), F32),
            _sds((3, D), F32), _sds((3, D), F32), _sds((3, D), F32))
    return pl.pallas_call(
        body, name="f1_gather_matmul", grid_spec=grid_spec, out_shape=outs,
        compiler_params=pltpu.CompilerParams(dimension_semantics=("arbitrary", "arbitrary"), vmem_limit_bytes=VMEM_LIMIT),
    )(idx1, ctx, x, nw, w_in, w_out, pw_in, pgrp, pw_out, lb_l, pscale, c, c_ctx, ada_w, ada_b)


def _gla_gates(pre, qpre, lbd, cum, rev):
    rows, n = pre.shape
    nch = rows // CHUNK
    sig = _sigmoid(pre)
    f = lbd + (1.0 - lbd) * sig
    k = 1.0 - f
    g = _dot01(cum, jnp.log(f))
    g3 = g.reshape(nch, CHUNK, n)
    last = 0 if rev else CHUNK - 1
    mid = CHUNK // 2 if rev else CHUNK // 2 - 1
    gl1, gm1 = g3[:, last:last + 1, :], g3[:, mid:mid + 1, :]

    def bc(a):
        return jnp.broadcast_to(a, g3.shape).reshape(rows, n)

    gm = bc(gm1)
    e_q, e_k = jnp.exp(g - gm), jnp.exp(gm - g)
    qsig = _sigmoid(qpre)
    qs = qpre * qsig * (DH ** -0.5)
    return dict(sig=sig, f=f, k=k, qsig=qsig, qs=qs, e_q=e_q, e_k=e_k,
                e_mid=[jnp.exp(gm1[ci]) for ci in range(nch)], e_rest=[jnp.exp(gl1[ci] - gm1[ci]) for ci in range(nch)])


def _put_heads(ref, lead, arr):
    for h in range(HEADS):
        ref[lead + (h,)] = arr[:, h * DH:(h + 1) * DH]


def _get_heads(ref, lead=()):
    return jnp.concatenate([ref[lead + (h,)] for h in range(HEADS)], axis=1)


def _gla_prep(g_all, lb, cum01, s_wout, s_pgrp):
    nch = TM // CHUNK

    def body(g_ref, lb_ref, cum_ref, swout_r, spgrp_r, p0_ref, p1_ref, v_ref, dec_ref, wout_o, pgrp_o, ssem, rsem, lsem):
        _gather_rider(pl.program_id(0), NT, NT - 1, ("rows", "grp"), (swout_r, spgrp_r), (wout_o, pgrp_o), ssem, rsem, lsem)
        qpre = g_ref[:, 3 * E:4 * E]
        _put_heads(v_ref, (), g_ref[:, 2 * E:3 * E].astype(BF16))
        for d, p_ref in ((0, p0_ref), (1, p1_ref)):
            t = _gla_gates(g_ref[:, d * E:(d + 1) * E], qpre, lb_ref[d:d + 1, :], cum_ref[d], d == 1)
            _put_heads(p_ref, (0,), (t["qs"] * t["e_q"]).astype(BF16))
            _put_heads(p_ref, (1,), (t["k"] * t["e_k"]).astype(BF16))
            for ci in range(nch):
                dec_ref[d, 0, ci:ci + 1, :] = t["e_mid"][ci]
                dec_ref[d, 0, nch + ci:nch + ci + 1, :] = t["e_rest"][ci]

    quad = pl.BlockSpec((2, HEADS, TM, DH), lambda i: (0, 0, i, 0))
    return pl.pallas_call(
        body, name="gla_prep", grid=(NT,),
        in_specs=[pl.BlockSpec((TM, 4 * E), lambda i: (i, 0)), VMEM_SPEC, VMEM_SPEC, HBM_SPEC, HBM_SPEC],
        out_specs=[quad, quad, pl.BlockSpec((HEADS, TM, DH), lambda i: (0, i, 0)), pl.BlockSpec((2, 1, 2 * nch, E), lambda i: (0, i, 0, 0)),
                   HBM_SPEC, HBM_SPEC],
        out_shape=(_sds((2, HEADS, TT, DH), BF16), _sds((2, HEADS, TT, DH), BF16), _sds((HEADS, TT, DH), BF16), _sds((2, NT, 2 * nch, E), F32),
                   _sds((E, D), BF16), _sds((4, PG, PG), BF16)),
        scratch_shapes=_rider_sems(2),
        compiler_params=pltpu.CompilerParams(dimension_semantics=("arbitrary",), vmem_limit_bytes=VMEM_LIMIT),
    )(g_all, lb, cum01, s_wout, s_pgrp)


def _scan_tile(i, rev):
    t = jnp.where(i == 0, 0, NT - i) if rev else i
    return t, pl.ds(pl.multiple_of(t * TM, TM), TM)


def _chunk_order(rev):
    n = TM // CHUNK
    return tuple(range(n - 1, -1, -1)) if rev else tuple(range(n))


def _chunk_rows(dec_ref, lanes, cis, where):
    nch = TM // CHUNK

    def rows(off):
        return jnp.stack([dec_ref[d, where[d][0], off + ci:off + ci + 1, hh * DH:(hh + 1) * DH] for (d, hh), ci in zip(lanes, cis)])

    return rows(0), rows(nch)


def _gla_fwd(p0, p1, v_all, dec, mask01, s_pwin):
    n_steps = HEADS // GLA_HB

    def body(p0_ref, p1_ref, v_ref, dec_ref, msk_ref, spwin_r, o_ref, pwin_o, ob_sc, ssem, rsem, lsem):
        _gather_rider(pl.program_id(0), n_steps, n_steps - 1, ("major",), (spwin_r,), (pwin_o,), ssem, rsem, lsem)

        lanes = [(d, hh) for d in (0, 1) for hh in range(GLA_HB)]
        nch = TM // CHUNK

        def tile_body(i, st):
            where = [_scan_tile(i, d == 1) for d in (0, 1)]

            def stacked(fn):
                return jnp.stack([fn(d, hh, where[d][1]) for d, hh in lanes])

            qg, kg = [stacked(lambda d, hh, rows, ty=ty: (p1_ref if d else p0_ref)[ty, hh, rows, :]) for ty in range(2)]
            v = stacked(lambda d, hh, rows: v_ref[hh, rows, :])
            a = _bdot_nt(qg, kg) * jnp.stack([msk_ref[d] for d, _ in lanes])
            intra = _bdot(a, v)
            outs = [[None] * nch for _ in lanes]
            for n in range(nch):
                cis = [nch - 1 - n if d else n for d, _ in lanes]

                def chunk(arr):
                    return jnp.stack([arr[l, ci * CHUNK:(ci + 1) * CHUNK] for l, ci in enumerate(cis)])

                e_mid, e_rest = _chunk_rows(dec_ref, lanes, cis, where)
                inter = _bdot_nt(chunk(qg), st * e_mid)
                for l, ci in enumerate(cis):
                    outs[l][ci] = inter[l] + intra[l, ci * CHUNK:(ci + 1) * CHUNK]
                st = st * (e_mid * e_rest) + _bdot_tn(chunk(v), chunk(kg)) * e_rest
            for l, (d, hh) in enumerate(lanes):
                (ob_sc if d else o_ref)[hh, where[d][1], :] = jnp.concatenate(outs[l], axis=0)
            return st

        lax.fori_loop(0, NT, tile_body, jnp.zeros((len(lanes), DH, DH), F32))
        o_ref[...] += ob_sc[...]

    quad = pl.BlockSpec((2, GLA_HB, TT, DH), lambda h: (0, h, 0, 0))
    head = pl.BlockSpec((GLA_HB, TT, DH), lambda h: (h, 0, 0))
    return pl.pallas_call(
        body, name="gla_fwd", grid=(n_steps,),
        in_specs=[quad, quad, head, pl.BlockSpec((2, NT, 8, GLA_HB * DH), lambda h: (0, 0, 0, h)),
                  pl.BlockSpec((2, TM, TM), lambda h: (0, 0, 0)), HBM_SPEC],
        out_specs=[head, HBM_SPEC],
        out_shape=(_sds((HEADS, TT, DH), F32), _sds((NDEV, D, SH_PWIN), BF16)),
        scratch_shapes=[pltpu.VMEM((GLA_HB, TT, DH), F32)] + _rider_sems(1),
        compiler_params=pltpu.CompilerParams(dimension_semantics=("arbitrary",), vmem_limit_bytes=VMEM_LIMIT),
    )(p0, p1, v_all, dec, mask01, s_pwin)


def _gated_norm(o, z, gw):
    r = _head_map(lambda oh: jnp.broadcast_to(_rstd(oh), oh.shape), o)
    on = o * r
    zs = _sigmoid(z)
    sz = z * zs
    return on * gw * sz, r, on, zs, sz


def _f3_out(o, g_all, x, gate, gw, wout, s_pwout):
    def body(o_ref, z_ref, x_ref, gate_ref, gw_ref, w_ref, spwout_r, x1_ref, pwout_o, ssem, rsem, lsem):
        _gather_rider(pl.program_id(0), NTX, NTX - 1, ("rows",), (spwout_r,), (pwout_o,), ssem, rsem, lsem)
        og, _, _, _, _ = _gated_norm(_get_heads(o_ref), z_ref[...], gw_ref[...])
        x1_ref[...] = x_ref[...] + gate_ref[...] * _dot(og, w_ref[...])

    return pl.pallas_call(
        body, name="f3_out", grid=(NTX,),
        in_specs=[pl.BlockSpec((HEADS, TM, DH), lambda i: (0, i + 1, 0)), pl.BlockSpec((TM, E), lambda i: (i + 1, 4)),
                  pl.BlockSpec((TM, D), lambda i: (i, 0)), pl.BlockSpec((1, D), lambda i: (0, 0)),
                  pl.BlockSpec((1, E), lambda i: (0, 0)), pl.BlockSpec((E, D), lambda i: (0, 0)), HBM_SPEC],
        out_specs=[pl.BlockSpec((TM, D), lambda i: (i, 0)), HBM_SPEC],
        out_shape=(_sds((T, D), F32), _sds((E, D), BF16)),
        scratch_shapes=_rider_sems(1),
        compiler_params=pltpu.CompilerParams(dimension_semantics=("arbitrary",)),
    )(o, g_all, x, gate, gw, wout, s_pwout)


def _pool_layer(x1, tgt, mod1, nw1, fnw, pwin, pgrp, pscale, pwout, pb, pbt, pinv):
    def body(x_ref, t_ref, m_ref, nw_ref, fw_ref, pwin_ref, pgrp_ref, ps_ref, pwout_ref, pb_ref, pbt_ref, pinv_ref,
             dx_ref, gpwin_o, gpgrp_o, gpwout_o, dmod_o, gnw_o, gfw_o, gps_o, loss_o,
             a_pwin, a_pgrp, a_pwout):
        i = pl.program_id(0)

        @pl.when(i == 0)
        def _():
            for ref in (a_pwin, a_pgrp, a_pwout, dmod_o, gnw_o, gfw_o, gps_o, loss_o):
                ref[...] = jnp.zeros_like(ref)

        shift, scale, gate = m_ref[0:1, :], m_ref[1:2, :], m_ref[2:3, :]
        nw, fw, ps = nw_ref[...], fw_ref[...], ps_ref[...]
        x1 = x_ref[...]
        hx, r1, xn, a = _modulated(x1, nw, shift, scale)
        hxb = hx.astype(BF16)
        uz = jnp.concatenate([_dot(hxb, pwin_ref[j]) for j in range(NDEV)], axis=1)
        u, z = uz[:, :E], uz[:, E:]
        pooled, ys = [], []
        for g in range(4):
            ug = u[:, g * PG:(g + 1) * PG]
            pg = _dot01(pb_ref[g], ug) * pinv_ref[g] - ug
            pooled.append(pg.astype(BF16))
            ys.append(_dot(pooled[g], pgrp_ref[g]))
        ycat = jnp.concatenate(ys, axis=1)
        y = ycat * ps
        zs = _sigmoid(z)
        sz = z * zs
        p = (y * sz).astype(BF16)
        out = _dot(p, pwout_ref[...])
        x2 = x1 + gate * out
        r2 = _rstd(x2)
        xn2 = x2 * r2
        diff = xn2 * fw - t_ref[...]
        loss_o[...] += _colsum(diff * diff)
        dyf = diff * (1.0 / D)
        gfw_o[...] += _colsum(dyf * xn2)
        dxn2 = dyf * fw
        dx2 = r2 * (dxn2 - xn2 * jnp.mean(dxn2 * xn2, axis=-1, keepdims=True))
        dgate = _colsum(dx2 * out)
        dout = (dx2 * gate).astype(BF16)
        for j in range(4):
            cs = slice(j * PG, (j + 1) * PG)
            a_pwout[:, cs] += _dot_ta(p, dout[:, cs])
        dp = _dot_tb(dout, pwout_ref[...])
        dy = dp * sz
        dz = dp * y * (zs * (1.0 + z * (1.0 - zs)))
        gps_o[...] += _colsum(dy * ycat)
        dycat = dy * ps
        dus = []
        for g in range(4):
            dyg = dycat[:, g * PG:(g + 1) * PG].astype(BF16)
            a_pgrp[g] += _dot_ta(pooled[g], dyg)
            dpg = _dot_tb(dyg, pgrp_ref[g])
            dus.append(_dot01(pbt_ref[g], dpg * pinv_ref[g]) - dpg)
        duz = jnp.concatenate(dus + [dz], axis=1).astype(BF16)
        dhx = None
        for j in range(NDEV):
            dj = duz[:, j * SH_PWIN:(j + 1) * SH_PWIN]
            a_pwin[j] += _dot_ta(hxb, dj)
            part = _dot_tb(dj, pwin_ref[j])
            dhx = part if dhx is None else dhx + part
        dmod_o[0:1, :] += _colsum(dhx)
        dmod_o[1:2, :] += _colsum(dhx * a)
        dmod_o[2:3, :] += dgate
        da = dhx * (1.0 + scale)
        gnw_o[...] += _colsum(da * xn)
        dxn = da * nw
        dx_ref[...] = dx2 + r1 * (dxn - xn * jnp.mean(dxn * xn, axis=-1, keepdims=True))

        @pl.when(i == NTX - 1)
        def _():
            gpwin_o[...] = a_pwin[...].astype(BF16)
            gpgrp_o[...] = a_pgrp[...].astype(BF16)
            gpwout_o[...] = a_pwout[...].astype(BF16)

    tile = pl.BlockSpec((TM, D), lambda i: (i, 0))
    outs = (_sds((T, D), F32), _sds((NDEV, D, SH_PWIN), BF16), _sds((4, PG, PG), BF16), _sds((E, D), BF16),
            _sds((3, D), F32), _sds((1, D), F32), _sds((1, D), F32), _sds((1, E), F32), _sds((1, D), F32))
    return pl.pallas_call(
        body, name="pool_layer", grid=(NTX,),
        in_specs=[tile, tile] + [VMEM_SPEC] * 10,
        out_specs=[tile] + [VMEM_SPEC] * 8,
        out_shape=outs,
        scratch_shapes=[pltpu.VMEM((NDEV, D, SH_PWIN), F32), pltpu.VMEM((4, PG, PG), F32), pltpu.VMEM((E, D), F32)],
        compiler_params=pltpu.CompilerParams(dimension_semantics=("arbitrary",), vmem_limit_bytes=VMEM_LIMIT),
    )(x1, tgt, mod1, nw1, fnw, pwin, pgrp, pscale, pwout, pb, pbt, pinv)


def _b3_out_bwd(dx1, o, g_all, gate, gw, wout, gpwout):
    def body(dx_ref, o_ref, z_ref, gate_ref, gw_ref, w_ref, gpwout_r, do_ref, dz_ref, gw_o, dgate_o, ggw_o, rpwout_o,
             acc, *rider):
        i = pl.program_id(0)
        bufs, sems = _rider2_split(rider, 1)
        _scatter_rider2(i, NT, 2, ("rows",), (gpwout_r,), (rpwout_o,), bufs, sems)

        @pl.when(i == 0)
        def _():
            acc[...] = jnp.zeros_like(acc)
            dgate_o[...] = jnp.zeros_like(dgate_o)
            ggw_o[...] = jnp.zeros_like(ggw_o)
            do_ref[...] = jnp.zeros_like(do_ref)
            dz_ref[...] = jnp.zeros_like(dz_ref)

        @pl.when(i > 0)
        def _():
            gw = gw_ref[...]
            z = z_ref[...]
            og, r, on, zs, sz = _gated_norm(_get_heads(o_ref), z, gw)
            ogb = og.astype(BF16)
            dx = dx_ref[...]
            dgate_o[...] += _colsum(dx * _dot(ogb, w_ref[...]))
            dy = (dx * gate_ref[...]).astype(BF16)
            for j in range(4):
                cs = slice(j * PG, (j + 1) * PG)
                acc[:, cs] += _dot_ta(ogb, dy[:, cs])
            dog = _dot_tb(dy, w_ref[...])
            dz_ref[...] = (dog * (on * gw) * (zs * (1.0 + z * (1.0 - zs)))).astype(BF16)
            dong = dog * sz
            ggw_o[...] += _colsum(dong * on)
            don = dong * gw
            do = _head_map(lambda dh, nh, rh: rh * (dh - nh * jnp.mean(dh * nh, axis=-1, keepdims=True)), don, on, r)
            _put_heads(do_ref, (), do.astype(BF16))

        @pl.when(i == NT - 1)
        def _():
            gw_o[...] = acc[...].astype(BF16)

    prev = lambda i: (jnp.maximum(i - 1, 0), 0)
    heads = pl.BlockSpec((HEADS, TM, DH), lambda i: (0, i, 0))
    return pl.pallas_call(
        body, name="b3_out_bwd", grid=(NT,),
        in_specs=[pl.BlockSpec((TM, D), prev), heads, pl.BlockSpec((TM, E), lambda i: (i, 4)),
                  VMEM_SPEC, VMEM_SPEC, VMEM_SPEC, HBM_SPEC],
        out_specs=[heads, pl.BlockSpec((TM, E), lambda i: (i, 0)), VMEM_SPEC, VMEM_SPEC, VMEM_SPEC, HBM_SPEC],
        out_shape=(_sds((HEADS, TT, DH), BF16), _sds((TT, E), BF16), _sds((E, D), BF16), _sds((1, D), F32), _sds((1, E), F32),
                   _sds((RS_SLOTS, SH_ROWS, D), BF16)),
        scratch_shapes=[pltpu.VMEM((E, D), F32)] + _rider2_scratch([(SH_ROWS, D)]),
        compiler_params=pltpu.CompilerParams(dimension_semantics=("arbitrary",), vmem_limit_bytes=VMEM_LIMIT),
    )(dx1, o, g_all, gate, gw, wout, gpwout)


def _gla_bwd(p0, p1, v_all, dec, do, mask01, gpwin, gpgrp):
    nch = TM // CHUNK
    n_steps = HEADS // GLA_HB

    def body(p0_ref, p1_ref, v_ref, dec_ref, do_ref, msk_ref, gpwin_r, gpgrp_r, d0_ref, d1_ref, dv_ref, dgl_ref, rpwin_o, rpgrp_o,
             ss_sc, dv_sc, ssem, rsem, lsem, *rider):
        _scatter_rider(pl.program_id(0), n_steps, ("grp",), (gpgrp_r,), (rpgrp_o,), ssem, rsem, lsem)
        bufs, sems = _rider2_split(rider, 1)
        _scatter_rider2(pl.program_id(0), n_steps, 1, ("major",), (gpwin_r,), (rpwin_o,), bufs, sems)

        lanes = [(d, hh) for d in (0, 1) for hh in range(GLA_HB)]
        zero = jnp.zeros((len(lanes), DH, DH), F32)
        dgl_ref[...] = jnp.zeros_like(dgl_ref)

        def p_of(d):
            return p1_ref if d else p0_ref

        def scan_step(i, n):
            where = [_scan_tile(i, d == 1) for d in (0, 1)]
            cis = [nch - 1 - n if d else n for d, _ in lanes]
            e_mid, e_rest = _chunk_rows(dec_ref, lanes, cis, where)

            def chunk(arr):
                return jnp.stack([arr[l, ci * CHUNK:(ci + 1) * CHUNK] for l, ci in enumerate(cis)])

            return where, cis, e_mid, e_rest, chunk

        def stacked(i, fn):
            where = [_scan_tile(i, d == 1) for d in (0, 1)]
            return jnp.stack([fn(d, hh, where[d][1]) for d, hh in lanes])

        def fwd_body(i, st):
            v = stacked(i, lambda d, hh, rows: v_ref[hh, rows, :])
            kg = stacked(i, lambda d, hh, rows: p_of(d)[1, hh, rows, :])
            for n in range(nch):
                _, _, e_mid, e_rest, chunk = scan_step(i, n)
                ss_sc[i * nch + n] = st
                st = st * (e_mid * e_rest) + _bdot_tn(chunk(v), chunk(kg)) * e_rest
            return st

        ss_sc[NT * nch] = lax.fori_loop(0, NT, fwd_body, zero)

        def bwd_body(ii, dst):
            i = NT - 1 - ii
            qg, kg = [stacked(i, lambda d, hh, rows, ty=ty: p_of(d)[ty, hh, rows, :]) for ty in range(2)]
            v = stacked(i, lambda d, hh, rows: v_ref[hh, rows, :])
            dob = stacked(i, lambda d, hh, rows: do_ref[hh, rows, :])
            msk = jnp.stack([msk_ref[d] for d, _ in lanes])
            a = (_bdot_nt(qg, kg) * msk).astype(BF16)
            da = (_bdot_nt(dob, v) * msk).astype(BF16)
            dqg = _bdot(da, kg)
            dkg = _bdot_tn(da, qg)
            dv_intra = _bdot_tn(a, dob)
            dv_l, dkg_l, dqg_l = ([[None] * nch for _ in lanes] for _ in range(3))
            for n in range(nch - 1, -1, -1):
                where, cis, e_mid, e_rest, chunk = scan_step(i, n)
                s_c, s_end = ss_sc[i * nch + n], ss_sc[i * nch + n + 1]
                dste = (dst * e_rest).astype(BF16)
                kg_c, v_c, dob_c = chunk(kg), chunk(v), chunk(dob)
                dv_c = chunk(dv_intra) + _bdot_nt(kg_c, dste)
                dkg_c = chunk(dkg) + _bdot(v_c, dste)
                dqg_c = chunk(dqg) + _bdot(dob_c, s_c * e_mid)
                dgl = jnp.sum(s_end * dst, axis=1, keepdims=True)
                for l, ((d, hh), ci) in enumerate(zip(lanes, cis)):
                    dv_l[l][ci], dkg_l[l][ci], dqg_l[l][ci] = dv_c[l], dkg_c[l], dqg_c[l]
                    dgl_ref[d, where[d][0], ci:ci + 1, hh * DH:(hh + 1) * DH] = dgl[l]
                dst = dst * (e_mid * e_rest) + _bdot_tn(dob_c, chunk(qg)) * e_mid
            where = [_scan_tile(i, d == 1) for d in (0, 1)]
            for l, (d, hh) in enumerate(lanes):
                rows = where[d][1]
                d_ref = d1_ref if d else d0_ref
                d_ref[0, hh, rows, :] = jnp.concatenate(dqg_l[l], axis=0).astype(BF16)
                d_ref[1, hh, rows, :] = jnp.concatenate(dkg_l[l], axis=0).astype(BF16)
                dv_sc[d, hh, rows, :] = jnp.concatenate(dv_l[l], axis=0).astype(BF16)
            return dst

        lax.fori_loop(0, NT, bwd_body, zero)
        dv_ref[...] = (dv_sc[0].astype(F32) + dv_sc[1].astype(F32)).astype(BF16)

    quad = pl.BlockSpec((2, GLA_HB, TT, DH), lambda h: (0, h, 0, 0))
    col = pl.BlockSpec((GLA_HB, TT, DH), lambda h: (h, 0, 0))
    chunkv = pl.BlockSpec((2, NT, 8, GLA_HB * DH), lambda h: (0, 0, 0, h))
    outs = (_sds((2, HEADS, TT, DH), BF16), _sds((2, HEADS, TT, DH), BF16), _sds((HEADS, TT, DH), BF16), _sds((2, NT, 8, E), F32),
            _sds((RS_SLOTS, D, SH_PWIN), BF16), _sds((NDEV, 4, SH_GRP, PG), BF16))
    return pl.pallas_call(
        body, name="gla_bwd", grid=(n_steps,),
        in_specs=[quad, quad, col, chunkv, col, pl.BlockSpec((2, TM, TM), lambda h: (0, 0, 0)), HBM_SPEC, HBM_SPEC],
        out_specs=[quad, quad, col, chunkv, HBM_SPEC, HBM_SPEC],
        out_shape=outs,
        scratch_shapes=[pltpu.VMEM((NT * nch + 1, 2 * GLA_HB, DH, DH), F32), pltpu.VMEM((2, GLA_HB, TT, DH), BF16)] + _rider_sems(1)
        + _rider2_scratch([(D, SH_PWIN)]),
        compiler_params=pltpu.CompilerParams(dimension_semantics=("arbitrary",), vmem_limit_bytes=VMEM_LIMIT),
    )(p0, p1, v_all, dec, do, mask01, gpwin, gpgrp)


TMB = 128


def _gla_post_bwd(g_all, d0, d1, dgl, dv, dz, lb, cum01, gwout):
    nch = TMB // CHUNK

    def body(g_ref, d0_ref, d1_ref, dgl_ref, dv_ref, dz_ref, lb_ref, cum_ref, gwout_r, dg_ref, dlb_ref, rwout_o, *rider):
        i = pl.program_id(0)
        bufs, sems = _rider2_split(rider, 1)
        _scatter_rider2(i, TT // TMB, 2, ("rows",), (gwout_r,), (rwout_o,), bufs, sems)

        @pl.when(i == 0)
        def _():
            dlb_ref[...] = jnp.zeros_like(dlb_ref)

        half = i & 1
        qpre = g_ref[:, 3 * E:4 * E]
        dqs_sum = None
        dpre = []
        for d, d_ref in ((0, d0_ref), (1, d1_ref)):
            rev = d == 1
            lbd = lb_ref[d:d + 1, :]
            t = _gla_gates(g_ref[:, d * E:(d + 1) * E], qpre, lbd, cum_ref[d, :TMB, :TMB], rev)
            dqs = _get_heads(d_ref, (0,)).astype(F32) * t["e_q"]
            dk = _get_heads(d_ref, (1,)).astype(F32) * t["e_k"]
            dg = t["qs"] * dqs - t["k"] * dk
            dgl8 = dgl_ref[d, 0]
            dgl_rows = [jnp.where(half == 0, dgl8[ci:ci + 1, :], dgl8[nch + ci:nch + ci + 1, :]) for ci in range(nch)]
            dgl_b = jnp.concatenate([jnp.broadcast_to(dgl_rows[ci], (CHUNK, E)) for ci in range(nch)], axis=0)
            pos = lax.broadcasted_iota(jnp.int32, (TMB, E), 0) & (CHUNK - 1)
            dg = dg + jnp.where(pos == (0 if rev else CHUNK - 1), dgl_b, 0.0)
            dlf = _dot01(cum_ref[1 - d, :TMB, :TMB], dg)
            df = dlf / t["f"] - dk
            sig = t["sig"]
            dpre.append((df * (1.0 - lbd) * sig * (1.0 - sig)).astype(BF16))
            dlb_ref[d:d + 1, :] += _colsum(df * (1.0 - sig))
            dqs_sum = dqs if dqs_sum is None else dqs_sum + dqs
            qsig = t["qsig"]
        dqpre = dqs_sum * (DH ** -0.5) * (qsig * (1.0 + qpre * (1.0 - qsig)))
        dg_ref[...] = jnp.concatenate([dpre[0], dpre[1], _get_heads(dv_ref), dqpre.astype(BF16), dz_ref[...]], axis=1)

    quad = pl.BlockSpec((2, HEADS, TMB, DH), lambda i: (0, 0, i, 0))
    tile = pl.BlockSpec((TMB, E), lambda i: (i, 0))
    return pl.pallas_call(
        body, name="gla_post_bwd", grid=(TT // TMB,),
        in_specs=[pl.BlockSpec((TMB, 4 * E), lambda i: (i, 0)), quad, quad,
                  pl.BlockSpec((2, 1, 8, E), lambda i: (0, i // 2, 0, 0)), pl.BlockSpec((HEADS, TMB, DH), lambda i: (0, i, 0)), tile,
                  VMEM_SPEC, VMEM_SPEC, HBM_SPEC],
        out_specs=[pl.BlockSpec((TMB, WIN_COLS), lambda i: (i, 0)), VMEM_SPEC, HBM_SPEC],
        out_shape=(_sds((TT, WIN_COLS), BF16), _sds((2, E), F32), _sds((RS_SLOTS, SH_ROWS, D), BF16)),
        scratch_shapes=_rider2_scratch([(SH_ROWS, D)]),
        compiler_params=pltpu.CompilerParams(dimension_semantics=("arbitrary",), vmem_limit_bytes=VMEM_LIMIT),
    )(g_all, d0, d1, dgl, dv, dz, lb, cum01, gwout)


WIN_SLOTS = 4


def _scatter_order(s, core):
    return (NDEV - 1 - s) ^ jnp.where((s >= 2) & (s <= 5) & ((s & 1) == core), 6, 0)


def _b1_in_bwd(idx1, ctx, x, dx1, dg, nw, msel, win):
    last_s = NDEV - 1
    half = D // 2

    def body(idx_ref, ctx_ref, x_ref, dx1_ref, dg_ref, nw_ref, m_ref, w_ref, gx_ref, rwin_o, dmx_o, dmc_o, gnw_o,
             hx_sc, dhx_sc, acc, sbuf, pbuf, rbuf, psend, precv, isend, irecv, dsend, drecv, sibsem, lsem):
        del idx_ref
        s, i = pl.program_id(0), pl.program_id(1)
        x, y, cc, idx = _mesh_pos()
        shift, scale = m_ref[0, 0:1, :], m_ref[0, 1:2, :]
        sibling = (x, y, 1 - cc)

        def partial(p):
            return pltpu.make_async_remote_copy(src_ref=sbuf.at[0], dst_ref=pbuf.at[p], send_sem=psend.at[p], recv_sem=precv.at[p],
                                                device_id=sibling, device_id_type=MESH)

        def chip_sum(p):
            return pltpu.make_async_remote_copy(src_ref=sbuf.at[1], dst_ref=rwin_o.at[2 + p], send_sem=isend.at[p], recv_sem=irecv.at[p],
                                                device_id=_peer(x, y, cc, 2 * (p + 1)), device_id_type=MESH)

        def relay(h):
            return pltpu.make_async_remote_copy(src_ref=sbuf.at[1, pl.ds(h * half, half), :], dst_ref=rbuf.at[h], send_sem=dsend.at[h],
                                                recv_sem=drecv.at[h], device_id=_peer(x, y, cc, 2 * (h + 1)), device_id_type=MESH)

        to_sibling = pltpu.make_async_remote_copy(src_ref=sbuf.at[0], dst_ref=rwin_o.at[1], send_sem=sibsem.at[0], recv_sem=sibsem.at[1],
                                                  device_id=sibling, device_id_type=MESH)
        own = pltpu.make_async_copy(sbuf.at[1], rwin_o.at[0], lsem)

        @pl.when((s == 0) & (i == 0))
        def _():
            for ref in (dmx_o, dmc_o, gnw_o):
                ref[...] = jnp.zeros_like(ref)

        @pl.when(s == 0)
        def _():
            hx, _, _, _ = _modulated(_ctx_or_x(i, ctx_ref, x_ref), nw_ref[...], shift, scale)
            hx_sc[i] = hx.astype(BF16)

        @pl.when(i == 0)
        def _():
            acc[...] = jnp.zeros_like(acc)

        dgb = dg_ref[...]
        hxb = hx_sc[i]
        for lo, hi in ((0, 256), (256, 512), (512, SH_WIN)):
            acc[:, lo:hi] += _dot_ta(hxb, dgb[:, lo:hi])
        part = _dot_tb(dgb, w_ref[...])

        @pl.when(s == 0)
        def _():
            dhx_sc[i] = part

        @pl.when(s > 0)
        def _():
            dhx_sc[i] += part

        done = i == NT - 1

        def hand_over(p, before):
            before.wait_send()
            sbuf[0] = acc[...].astype(BF16)
            partial(p).start()

        def send_chip_sum(p, before):
            for cp in before:
                cp.wait_send()
            partial(p).wait_recv()
            sbuf[1] = (acc[...] + pbuf[p].astype(F32)).astype(BF16)
            h = 1 - p
            rows = pl.ds(h * half, half)
            relay(h).wait_recv()
            sbuf[1, rows, :] = (acc[rows, :] + pbuf[p, rows, :].astype(F32) + rbuf[h].astype(F32)).astype(BF16)
            chip_sum(p).start()

        @pl.when(done & (s == 0))
        def _():
            sbuf[0] = acc[...].astype(BF16)
            partial(2).start()

        @pl.when(done & (s == 1))
        def _():
            partial(2).wait_recv()
            sbuf[1] = (acc[...] + pbuf[2].astype(F32)).astype(BF16)
            for h in range(2):
                relay(h).start()

        for core in range(2):
            @pl.when(done & (cc == core) & (s == 2))
            def _(core=core):
                hand_over(core, partial(2))

            @pl.when(done & (cc == core) & (s == 3))
            def _(core=core):
                send_chip_sum(1 - core, [relay(0), relay(1)])

            @pl.when(done & (cc == core) & (s == 4))
            def _(core=core):
                hand_over(1 - core, partial(core))

            @pl.when(done & (cc == core) & (s == 5))
            def _(core=core):
                send_chip_sum(core, [chip_sum(1 - core)])

            @pl.when(done & (cc == core) & (s == last_s - 1))
            def _(core=core):
                partial(1 - core).wait_send()
                sbuf[0] = acc[...].astype(BF16)
                to_sibling.start()

            @pl.when(done & (cc == core) & (s == last_s))
            def _(core=core):
                chip_sum(core).wait_send()
                sbuf[1] = acc[...].astype(BF16)
                own.start()

        @pl.when(s == last_s)
        def _():
            nw = nw_ref[...]
            _, r, xn, a = _modulated(_ctx_or_x(i, ctx_ref, x_ref), nw, shift, scale)
            dhx = dhx_sc[i]
            dsh, dsc = _colsum(dhx), _colsum(dhx * a)
            da = dhx * (1.0 + scale)
            gnw_o[...] += _colsum(da * xn)
            dxn = da * nw
            gx_ref[...] = dx1_ref[...] + r * (dxn - xn * jnp.mean(dxn * xn, axis=-1, keepdims=True))

            @pl.when(i == 0)
            def _():
                dmc_o[0:1, :] += dsh
                dmc_o[1:2, :] += dsc

            @pl.when(i > 0)
            def _():
                dmx_o[0:1, :] += dsh
                dmx_o[1:2, :] += dsc

        @pl.when((i == NT - 1) & (s == last_s))
        def _():
            to_sibling.wait_send()
            to_sibling.wait_recv()
            for p in range(2):
                chip_sum(p).wait_recv()
            own.wait()

    grid_spec = pltpu.PrefetchScalarGridSpec(
        num_scalar_prefetch=1, grid=(NDEV, NT),
        in_specs=[VMEM_SPEC, pl.BlockSpec((TM, D), lambda s, i, ix: (jnp.maximum(i - 1, 0), 0)),
                  pl.BlockSpec((TM, D), lambda s, i, ix: (jnp.maximum(i - 1, 0), 0)),
                  pl.BlockSpec((TM, SH_WIN), lambda s, i, ix: (i, ix[0] ^ _scatter_order(s, ix[0] & 1))), VMEM_SPEC,
                  pl.BlockSpec((1, 2, D), lambda s, i, ix: (jnp.minimum(i, 1), 0, 0)),
                  pl.BlockSpec((D, SH_WIN), lambda s, i, ix: (0, ix[0] ^ _scatter_order(s, ix[0] & 1)))],
        out_specs=[pl.BlockSpec((TM, D), lambda s, i, ix: (jnp.where(s == last_s, jnp.maximum(i - 1, 0), 0), 0)),
                   HBM_SPEC, VMEM_SPEC, VMEM_SPEC, VMEM_SPEC],
        scratch_shapes=[pltpu.VMEM((NT, TM, D), BF16), pltpu.VMEM((NT, TM, D), F32), pltpu.VMEM((D, SH_WIN), F32),
                        pltpu.VMEM((2, D, SH_WIN), BF16), pltpu.VMEM((3, D, SH_WIN), BF16), pltpu.VMEM((2, half, SH_WIN), BF16),
                        pltpu.SemaphoreType.DMA((3,)), pltpu.SemaphoreType.DMA((3,)), pltpu.SemaphoreType.DMA((2,)),
                        pltpu.SemaphoreType.DMA((2,)), pltpu.SemaphoreType.DMA((2,)), pltpu.SemaphoreType.DMA((2,)),
                        pltpu.SemaphoreType.DMA((2,)), pltpu.SemaphoreType.DMA])
    return pl.pallas_call(
        body, name="b1_in_bwd", grid_spec=grid_spec,
        out_shape=(_sds((T, D), F32), _sds((WIN_SLOTS, D, SH_WIN), BF16), _sds((2, D), F32), _sds((2, D), F32), _sds((1, D), F32)),
        compiler_params=pltpu.CompilerParams(dimension_semantics=("arbitrary", "arbitrary"), vmem_limit_bytes=VMEM_LIMIT),
    )(idx1, ctx, x, dx1, dg, nw, msel, win)


def _reduce_small(pd, pv, cg, c_ctx, ada_w0):
    n_arr = 3

    def body(pd_r, pv_r, cg_r, cctx_r, ada_r, gada_o, gadab_o, gcctx_o, pvsum_o, loss_o,
             pd_all, pv_all, dsc_all, dsc_mine, ssem, rsem):
        x, y, cc, idx = _mesh_pos()
        srcs = [pd_r, pv_r, dsc_mine]
        dsts = [pd_all.at[idx], pv_all.at[idx], dsc_all.at[idx]]

        def remote(a, k):
            return pltpu.make_async_remote_copy(src_ref=srcs[a], dst_ref=dsts[a], send_sem=ssem.at[a, k], recv_sem=rsem.at[a, k],
                                                device_id=_peer(x, y, cc, k), device_id_type=MESH)

        first = [remote(a, k) for k in range(1, NDEV) for a in (0, 1)]
        for cp in first:
            cp.start()
        pd_all[idx] = pd_r[...]
        pv_all[idx] = pv_r[...]
        for k in range(1, NDEV):
            remote(0, k).wait_recv()
            remote(1, k).wait_recv()
        mine = [pd_all[s, :, pl.ds(idx, 1), :] for s in range(NDEV)]
        dmc = functools.reduce(lambda u, v: u + v, [m[2] for m in mine])
        rows = _stack_rows([cg_r[i] for i in range(NDEV)] + [cctx_r[...]])
        sc = (rows * _sigmoid(rows)).astype(BF16)
        gada_o[0] = _dot_ta(sc, _stack_rows([m[0] for m in mine] + [dmc]))
        gada_o[1] = _dot_ta(sc, _stack_rows([m[1] for m in mine]))
        dsc_mine[...] = _dot_tb(jnp.broadcast_to(dmc, (8, SH_ADA)), ada_r[...])[0:1, :]
        dsc_all[idx] = dsc_mine[...]
        second = [remote(2, k) for k in range(1, NDEV)]
        for cp in second:
            cp.start()
        tot = [functools.reduce(lambda u, v: u + v, [pd_all[s, l] for s in range(NDEV)]) for l in range(3)]
        gadab_o[0] = tot[0] + tot[2]
        gadab_o[1] = tot[1]
        pvs = functools.reduce(lambda u, v: u + v, [pv_all[s] for s in range(NDEV)])
        pvsum_o[...] = pvs
        loss_o[...] = jnp.broadcast_to(jnp.sum(pvs[:, PV_LOSS:PV_LOSS + D], axis=-1, keepdims=True) * (0.5 / D), (1, 128))
        for k in range(1, NDEV):
            remote(2, k).wait_recv()
        dsc = functools.reduce(lambda u, v: u + v, [dsc_all[s] for s in range(NDEV)])
        cx = cctx_r[...]
        sx = _sigmoid(cx)
        gcctx_o[...] = dsc * (sx * (1.0 + cx * (1.0 - sx)))
        for cp in first + second:
            cp.wait_send()

    outs = (_sds((2, D, SH_ADA), F32), _sds((2, NDEV, SH_ADA), F32), _sds((1, D), F32), _sds((1, PV_LEN), F32), _sds((1, 128), F32))
    return pl.pallas_call(
        body, name="reduce_small", out_shape=outs,
        in_specs=[VMEM_SPEC] * 5, out_specs=[VMEM_SPEC] * 5,
        scratch_shapes=[
            pltpu.VMEM((NDEV, 3, NDEV, SH_ADA), F32), pltpu.VMEM((NDEV, 1, PV_LEN), F32), pltpu.VMEM((NDEV, 1, D), F32),
            pltpu.VMEM((1, D), F32),
            pltpu.SemaphoreType.DMA((n_arr, NDEV)), pltpu.SemaphoreType.DMA((n_arr, NDEV)),
        ],
        compiler_params=pltpu.CompilerParams(vmem_limit_bytes=VMEM_LIMIT),
    )(pd, pv, cg, c_ctx, ada_w0)


PV_NW, PV_GNORM, PV_FINAL, PV_LB, PV_PSCALE, PV_LOSS, PV_LEN = 0, 2 * D, 3 * D, 4 * D, 6 * D, 7 * D, 8 * D


def _adamw(w, g, m, v):
    m = ADAM_B1 * m + (1.0 - ADAM_B1) * g
    v = ADAM_B2 * v + (1.0 - ADAM_B2) * (g * g)
    m_hat = m / (1.0 - ADAM_B1 ** ADAM_STEP)
    v_hat = v / (1.0 - ADAM_B2 ** ADAM_STEP)
    delta = -ADAM_LR * (m_hat / (jnp.sqrt(v_hat) + ADAM_EPS) + ADAM_WD * w)
    return delta, m, v


ADAM_STEPS = 8


def _adam_all(sharded, dense, small, lb_idx, lbv):
    ns, nd, nsm = len(sharded), len(dense), len(small)

    def body(*refs):
        it = iter(refs)
        sh_in = [[next(it) for _ in range(4)] for _ in range(ns)]
        de_in = [[next(it) for _ in range(4)] for _ in range(nd)]
        sm_in = [[next(it) for _ in range(4)] for _ in range(nsm)]
        lb_r = next(it)
        sh_out = [[next(it) for _ in range(4)] for _ in range(ns)]
        de_out = [[next(it) for _ in range(3)] for _ in range(nd)]
        sm_out = [[next(it) for _ in range(4)] for _ in range(nsm)]
        for (p, w, m, v), outs in zip(sh_in, sh_out):
            g = p[0].astype(F32)
            for s in range(1, p.shape[0]):
                g = g + p[s].astype(F32)
            d, mn, vn = _adamw(w[...], g, m[...], v[...])
            outs[0][...], outs[1][...], outs[2][...], outs[3][...] = g, d, mn, vn
        for (g, w, m, v), outs in zip(de_in, de_out):
            d, mn, vn = _adamw(w[...], g[...], m[...], v[...])
            outs[0][...], outs[1][...], outs[2][...] = d, mn, vn

        @pl.when(pl.program_id(0) == 0)
        def _():
            for j, ((g, w, m, v), outs) in enumerate(zip(sm_in, sm_out)):
                gj = g[...]
                if j == lb_idx:
                    gj = gj * lb_r[...] * (1.0 - lb_r[...])
                d, mn, vn = _adamw(w[...], gj, m[...], v[...])
                outs[0][...], outs[1][...], outs[2][...], outs[3][...] = gj, d, mn, vn

    def tile(a):
        return pl.BlockSpec((a.shape[0] // ADAM_STEPS, a.shape[1]), lambda i: (i, 0))

    in_specs, out_specs, out_shape, args = [], [], [], []
    for p, w, m, v in sharded:
        in_specs += [pl.BlockSpec((p.shape[0], p.shape[1] // ADAM_STEPS, p.shape[2]), lambda i: (0, i, 0))] + [tile(w)] * 3
        args += [p, w, m, v]
    for g, w, m, v in dense:
        in_specs += [tile(w)] * 4
        args += [g, w, m, v]
    for g, w, m, v in small:
        in_specs += [VMEM_SPEC] * 4
        args += [g, w, m, v]
    in_specs.append(VMEM_SPEC)
    args.append(lbv)
    for _, w, _, _ in sharded:
        out_specs += [tile(w)] * 4
        out_shape += [_sds(w.shape, F32)] * 4
    for _, w, _, _ in dense:
        out_specs += [tile(w)] * 3
        out_shape += [_sds(w.shape, F32)] * 3
    for _, w, _, _ in small:
        out_specs += [VMEM_SPEC] * 4
        out_shape += [_sds(w.shape, F32)] * 4
    res = pl.pallas_call(body, name="adam_all", grid=(ADAM_STEPS,), in_specs=in_specs, out_specs=out_specs, out_shape=tuple(out_shape),
                         compiler_params=pltpu.CompilerParams(dimension_semantics=("arbitrary",), vmem_limit_bytes=VMEM_LIMIT))(*args)
    it = iter(res)
    return ([tuple(next(it) for _ in range(4)) for _ in range(ns)], [tuple(next(it) for _ in range(3)) for _ in range(nd)],
            [tuple(next(it) for _ in range(4)) for _ in range(nsm)])


def kernel(x, c, ctx, c_ctx, ada_w, ada_b, norm_w, hgrn_w_in, hgrn_lb_logits, hgrn_gnorm_w, hgrn_w_out, pool_w_in, pool_w_grp, pool_scale, pool_w_out, final_norm_w, loss_target, m_c_ctx, m_ada_w, m_ada_b, m_norm_w, m_hgrn_w_in, m_hgrn_lb_logits, m_hgrn_gnorm_w, m_hgrn_w_out, m_pool_w_in, m_pool_w_grp, m_pool_scale, m_pool_w_out, m_final_norm_w, v_c_ctx, v_ada_w, v_ada_b, v_norm_w, v_hgrn_w_in, v_hgrn_lb_logits, v_hgrn_gnorm_w, v_hgrn_w_out, v_pool_w_in, v_pool_w_grp, v_pool_scale, v_pool_w_out, v_final_norm_w):
    idx = 4 * lax.axis_index("x") + 2 * lax.axis_index("y") + lax.axis_index("c")
    cctx2 = c_ctx.reshape(1, D)
    cum01, mask01 = _gla_consts()
    pb, pbt, pinv = _pool_consts()

    idx1 = idx.reshape(1).astype(jnp.int32)
    nw0, nw1 = norm_w[0:1], norm_w[1:2]
    fnw = final_norm_w.reshape(1, D)
    g_all, win, s_wout, s_pwin, s_pgrp, s_pwout, lbl_g, ps_g, cg, mod0, mod1, modc = _f1_gather_matmul(
        idx1, ctx[0], x[0], nw0, hgrn_w_in[0], hgrn_w_out[0], pool_w_in[0], pool_w_grp[0], pool_w_out[0], hgrn_lb_logits[0],
        pool_scale, c, cctx2, ada_w, ada_b)
    lb = jax.nn.sigmoid(jnp.transpose(lbl_g, (1, 0, 2)).reshape(2, E))
    pscale = ps_g.reshape(1, E)
    msel = jnp.stack([modc[:2], mod0[:2]])
    p0, p1, v_all, dec, wout, pgrp = _gla_prep(g_all, lb, cum01, s_wout, s_pgrp)
    o, pwin = _gla_fwd(p0, p1, v_all, dec, mask01, s_pwin)
    x1, pwout = _f3_out(o, g_all, x[0], mod0[2:3], hgrn_gnorm_w, wout, s_pwout)
    dx1, gpwin, gpgrp, gpwout, dmod1, gnw1, gfw, gps, lossv = _pool_layer(
        x1, loss_target[0], mod1, nw1, fnw, pwin, pgrp, pscale, pwout, pb, pbt, pinv)
    do, dz, gwout, dgate0, ggw, rpwout = _b3_out_bwd(dx1, o, g_all, mod0[2:3], hgrn_gnorm_w, wout, gpwout)
    d0, d1, dv, dgl, rpwin, rpgrp = _gla_bwd(p0, p1, v_all, dec, do, mask01, gpwin, gpgrp)
    dg, dlb, rwout = _gla_post_bwd(g_all, d0, d1, dgl, dv, dz, lb, cum01, gwout)
    grad_x, rwin, dmx, dmc, gnw0 = _b1_in_bwd(idx1, ctx[0], x[0], dx1, dg, nw0, msel, win)

    dmod0 = jnp.concatenate([dmx, dgate0], axis=0)
    dmodc = jnp.concatenate([dmc, jnp.zeros((1, D), F32)], axis=0)
    pd = jnp.stack([dmod0, dmod1, dmodc]).reshape(3, NDEV, SH_ADA)
    pv = jnp.concatenate([gnw0, gnw1, ggw, gfw, dlb.reshape(1, 2 * E), gps, lossv], axis=1)
    g_ada, g_adab, g_cctx, pvsum, loss128 = _reduce_small(pd, pv, cg, cctx2, ada_w[0])

    g2 = (4 * SH_GRP, PG)
    sharded_names = ["hgrn_w_in", "hgrn_w_out", "pool_w_in", "pool_w_grp", "pool_w_out"]
    sharded = [(rwin, hgrn_w_in[0], m_hgrn_w_in[0], v_hgrn_w_in[0]),
               (rwout, hgrn_w_out[0], m_hgrn_w_out[0], v_hgrn_w_out[0]),
               (rpwin, pool_w_in[0], m_pool_w_in[0], v_pool_w_in[0]),
               (rpgrp.reshape((NDEV,) + g2), pool_w_grp[0].reshape(g2), m_pool_w_grp[0].reshape(g2), v_pool_w_grp[0].reshape(g2)),
               (rpwout, pool_w_out[0], m_pool_w_out[0], v_pool_w_out[0])]
    a2 = (2 * D, SH_ADA)
    g_ada2 = g_ada.reshape(a2)
    dense = [(g_ada2, ada_w.reshape(a2), m_ada_w.reshape(a2), v_ada_w.reshape(a2))]
    lb_me = lax.dynamic_slice_in_dim(lb, idx * DH, DH, axis=1)
    small_names = ["c_ctx", "ada_b", "norm_w", "hgrn_lb_logits", "hgrn_gnorm_w", "pool_scale", "final_norm_w"]
    small = [(g_cctx, cctx2, m_c_ctx.reshape(1, D), v_c_ctx.reshape(1, D)),
             (g_adab.reshape(2, 3 * D), ada_b, m_ada_b, v_ada_b),
             (pvsum[:, PV_NW:PV_NW + 2 * D].reshape(2, D), norm_w, m_norm_w, v_norm_w),
             (lax.dynamic_slice_in_dim(pvsum[:, PV_LB:PV_LB + 2 * E].reshape(2, E), idx * DH, DH, axis=1),
              hgrn_lb_logits[0], m_hgrn_lb_logits[0], v_hgrn_lb_logits[0]),
             (pvsum[:, PV_GNORM:PV_GNORM + E], hgrn_gnorm_w, m_hgrn_gnorm_w, v_hgrn_gnorm_w),
             (lax.dynamic_slice_in_dim(pvsum[:, PV_PSCALE:PV_PSCALE + E], idx * DH, DH, axis=1), pool_scale, m_pool_scale, v_pool_scale),
             (pvsum[:, PV_FINAL:PV_FINAL + D], fnw, m_final_norm_w.reshape(1, D), v_final_norm_w.reshape(1, D))]
    r_sharded, r_dense, r_small = _adam_all(sharded, dense, small, 3, lb_me)
    out = dict(zip(sharded_names, r_sharded))
    out["ada_w"] = (g_ada2,) + r_dense[0]
    out.update(zip(small_names, r_small))

    shapes = {"c_ctx": (D,), "ada_w": (2, D, SH_ADA), "ada_b": (2, 3 * D), "norm_w": (2, D), "hgrn_w_in": (1, D, SH_WIN),
              "hgrn_lb_logits": (1, 2, DH), "hgrn_gnorm_w": (1, E), "hgrn_w_out": (1, SH_ROWS, D), "pool_w_in": (1, D, SH_PWIN),
              "pool_w_grp": (1, 4, SH_GRP, PG), "pool_scale": (1, DH), "pool_w_out": (1, SH_ROWS, D), "final_norm_w": (D,)}
    order = ["c_ctx", "ada_w", "ada_b", "norm_w", "hgrn_w_in", "hgrn_lb_logits", "hgrn_gnorm_w", "hgrn_w_out", "pool_w_in",
             "pool_w_grp", "pool_scale", "pool_w_out", "final_norm_w"]
    flat = [out[name][q].reshape(shapes[name]) for q in range(4) for name in order]
    return (loss128[0, 0], grad_x[None], *flat)
```

```python
import functools

import numpy as np
import jax
import jax.numpy as jnp
from jax import lax
from jax.experimental import pallas as pl
from jax.experimental.pallas import tpu as pltpu

F32 = jnp.float32
BF16 = jnp.bfloat16

D = 1024
E = 1024
HEADS = 8
DH = 128
CHUNK = 64
T = 2048
TC = 256
TT = T + TC
TM = 256
NT = TT // TM
NTX = T // TM
NDEV = 8
GRID_W = 64
POOL_WINDOWS = (2, 4, 8, 16)
PG = 256
EPS = 1e-6
WIN_COLS = 5 * E
SH_WIN = WIN_COLS // NDEV
SH_PWIN = 2 * E // NDEV
SH_ROWS = E // NDEV
SH_GRP = PG // NDEV
SH_ADA = 3 * D // NDEV
VMEM_LIMIT = 56 * 1024 * 1024

ADAM_LR, ADAM_B1, ADAM_B2, ADAM_EPS, ADAM_WD, ADAM_STEP = 0.001, 0.9, 0.999, 1e-08, 0.01, 10

MESH = pl.DeviceIdType.MESH
VMEM_SPEC = pl.BlockSpec(memory_space=pltpu.VMEM)
HBM_SPEC = pl.BlockSpec(memory_space=pltpu.HBM)
ANY_SPEC = pl.BlockSpec(memory_space=pl.ANY)


def _sds(shape, dtype):
    return jax.ShapeDtypeStruct(shape, dtype)


def _bf(a):
    return a if a.dtype == BF16 else a.astype(BF16)


def _dot(a, b):
    return lax.dot_general(_bf(a), _bf(b), (((1,), (0,)), ((), ())), preferred_element_type=F32)


def _dot_tb(a, b):
    return lax.dot_general(_bf(a), _bf(b), (((1,), (1,)), ((), ())), preferred_element_type=F32)


def _dot_ta(a, b):
    return lax.dot_general(_bf(a), _bf(b), (((0,), (0,)), ((), ())), preferred_element_type=F32)


def _bdot(a, b):
    return lax.dot_general(_bf(a), _bf(b), (((2,), (1,)), ((0,), (0,))), preferred_element_type=F32)


def _bdot_nt(a, b):
    return lax.dot_general(_bf(a), _bf(b), (((2,), (2,)), ((0,), (0,))), preferred_element_type=F32)


def _bdot_tn(a, b):
    return lax.dot_general(_bf(a), _bf(b), (((1,), (1,)), ((0,), (0,))), preferred_element_type=F32)


def _dot01(m01, x):
    hi = x.astype(BF16)
    lo = (x - hi.astype(F32)).astype(BF16)
    return _dot(m01, hi) + _dot(m01, lo)


def _rstd(x):
    return lax.rsqrt(jnp.mean(x * x, axis=-1, keepdims=True) + EPS)


def _sigmoid(x):
    return jax.nn.sigmoid(x)


def _colsum(a):
    return jnp.sum(a, axis=0, keepdims=True)


def _stack_rows(rows):
    n = rows[0].shape[-1]
    rid = lax.broadcasted_iota(jnp.int32, (16, n), 0)
    out = jnp.zeros((16, n), F32)
    for i, r in enumerate(rows):
        out = jnp.where(rid == i, r, out)
    return out


def _head_map(fn, *arrs):
    outs = [fn(*[a[:, h * DH:(h + 1) * DH] for a in arrs]) for h in range(HEADS)]
    return jnp.concatenate(outs, axis=1)


def _gla_consts():
    r = np.arange(TM)[:, None]
    c = np.arange(TM)[None, :]
    same = (r // CHUNK) == (c // CHUNK)
    tril = same & (c <= r)
    triu = same & (c >= r)
    m = np.stack([tril, triu]).astype(np.float32)
    return jnp.asarray(m, BF16), jnp.asarray(m, F32)


def _pool_consts():
    r = np.arange(TM)[:, None]
    c = np.arange(TM)[None, :]
    same = (r // GRID_W) == (c // GRID_W)
    rp, cp = r % GRID_W, c % GRID_W
    bs, inv = [], []
    for w in POOL_WINDOWS:
        lo = np.clip(rp - w // 2, 0, GRID_W)
        hi = np.clip(rp - w // 2 + w, 0, GRID_W)
        bs.append(same & (cp >= lo) & (cp < hi))
        inv.append(1.0 / (hi - lo).astype(np.float32))
    b = np.stack(bs).astype(np.float32)
    bt = np.transpose(b, (0, 2, 1))
    return jnp.asarray(b, BF16), jnp.asarray(bt, BF16), jnp.asarray(np.stack(inv), F32)


def _mesh_pos():
    x, y, c = lax.axis_index("x"), lax.axis_index("y"), lax.axis_index("c")
    return x, y, c, 4 * x + 2 * y + c


def _peer(x, y, c, k):
    return (x ^ ((k >> 2) & 1), y ^ ((k >> 1) & 1), c ^ (k & 1))


PARTS = 4


class _Copies:
    def __init__(self, copies):
        self.copies = copies

    def start(self):
        for cp in self.copies:
            cp.start()

    def wait_send(self):
        for cp in self.copies:
            cp.wait_send()

    def wait_recv(self):
        for cp in self.copies:
            cp.wait_recv()


def _part(ref, q):
    n = ref.shape[0] // PARTS
    return ref.at[pl.ds(q * n, n)]


def _small_gathers(refs, ssem, rsem):
    lb_r, ps_r, c_r, cctx_r, ada_r, adab_r, lb_o, ps_o, cg_o, mod_o, lb_out, ps_out, cg_out, mod0_o, mod1_o, modc_o = refs
    x, y, cc, idx = _mesh_pos()
    srcs = [lb_r, ps_r, c_r, mod_o.at[idx]]
    mine = [lb_o.at[idx], ps_o.at[idx], cg_o.at[idx], mod_o.at[idx]]

    def remote(a, k):
        return pltpu.make_async_remote_copy(src_ref=srcs[a], dst_ref=mine[a], send_sem=ssem.at[a, k], recv_sem=rsem.at[a, k],
                                            device_id=_peer(x, y, cc, k), device_id_type=MESH)

    first = [remote(a, k) for k in range(1, NDEV) for a in (2, 0, 1)]
    for cp in first:
        cp.start()
    lb_o[idx] = lb_r[...]
    ps_o[idx] = ps_r[...]
    cg_o[idx] = c_r[...]
    for k in range(1, NDEV):
        remote(2, k).wait_recv()
    rows = _stack_rows([cg_o[i] for i in range(NDEV)] + [cctx_r[...]])
    sc = rows * _sigmoid(rows)
    for l in range(2):
        mod_o[idx, l] = _dot(sc, ada_r[l])
    second = [remote(3, k) for k in range(1, NDEV)]
    for cp in second:
        cp.start()
    for k in range(1, NDEV):
        remote(3, k).wait_recv()

    def mod_rows(l, row):
        full = jnp.concatenate([mod_o[s, l, row, :] for s in range(NDEV)], axis=1) + adab_r[l:l + 1, :]
        return [full[:, j * D:(j + 1) * D] for j in range(3)]

    me = pl.ds(idx, 1)
    for out, parts in ((mod0_o, mod_rows(0, me)), (mod1_o, mod_rows(1, me)), (modc_o, mod_rows(0, slice(NDEV, NDEV + 1)))):
        for j in range(3):
            out[j:j + 1, :] = parts[j]
    for cp in first + second:
        cp.wait_send()
    for k in range(1, NDEV):
        for a in (0, 1):
            remote(a, k).wait_recv()
    lb_out[...] = lb_o[...]
    ps_out[...] = ps_o[...]
    cg_out[...] = cg_o[...]


def _gather_order(s, core):
    k = jnp.where(s == 2, 4, jnp.where(s == 4, 2, s))
    return k ^ jnp.where((core == 1) & (s >= 2) & (s <= 5), 6, 0)


GATHER_ISSUE = (1, 2, 4, 3, 5, 6, 7)
GATHER_ICI = (2, 4, 6)
GATHER_DIRECT = (1,) + GATHER_ICI
GLA_HB = 2
RS_SLOTS = 5


def _shard_of(kind, ref, i):
    if kind == "rows":
        return ref.at[pl.ds(pl.multiple_of(i * SH_ROWS, SH_ROWS), SH_ROWS), :]
    if kind == "major":
        return ref.at[i]
    assert kind == "grp"
    return ref.at[:, pl.ds(pl.multiple_of(i * SH_GRP, SH_GRP), SH_GRP), :]


def _gather_rider(step, n_steps, forward_at, kinds, srcs, outs, ssem, rsem, lsem):
    x, y, cc, idx = _mesh_pos()
    arrays = range(len(kinds))
    mine = [_shard_of(kinds[a], outs[a], idx) for a in arrays]

    def remote(a, k):
        return _Copies([pltpu.make_async_remote_copy(src_ref=_part(srcs[a], q), dst_ref=_part(mine[a], q), send_sem=ssem.at[a, k, q],
                                                     recv_sem=rsem.at[a, k, q], device_id=_peer(x, y, cc, k), device_id_type=MESH)
                        for q in range(PARTS)])

    def forward(a, k):
        blk = _shard_of(kinds[a], outs[a], idx ^ k)
        return _Copies([pltpu.make_async_remote_copy(src_ref=_part(blk, q), dst_ref=_part(blk, q), send_sem=ssem.at[a, k ^ 1, q],
                                                     recv_sem=rsem.at[a, k ^ 1, q], device_id=(x, y, 1 - cc), device_id_type=MESH)
                        for q in range(PARTS)])

    copies = [remote(a, k) for k in GATHER_DIRECT for a in arrays]
    passed = [forward(a, k) for k in GATHER_ICI for a in arrays]
    local = [pltpu.make_async_copy(srcs[a], mine[a], lsem.at[a]) for a in arrays]

    @pl.when(step == 0)
    def _():
        for cp in copies + local:
            cp.start()

    @pl.when(step == forward_at)
    def _():
        for k in GATHER_ICI:
            for a in arrays:
                remote(a, k).wait_recv()
                forward(a, k).start()

    @pl.when(step == n_steps - 1)
    def _():
        for cp in copies + passed:
            cp.wait_send()
        for a in arrays:
            remote(a, 1).wait_recv()
        for cp in passed:
            cp.wait_recv()
        for cp in local:
            cp.wait()


def _scatter_rider(step, n_steps, kinds, grads, slots, ssem, rsem, lsem):
    x, y, cc, idx = _mesh_pos()
    arrays = range(len(kinds))
    dsts = [slots[a].at[idx] for a in arrays]

    def remote(a, k):
        px, py, pc = _peer(x, y, cc, k)
        return pltpu.make_async_remote_copy(src_ref=_shard_of(kinds[a], grads[a], 4 * px + 2 * py + pc), dst_ref=dsts[a],
                                            send_sem=ssem.at[a, k], recv_sem=rsem.at[a, k], device_id=(px, py, pc), device_id_type=MESH)

    copies = [remote(a, k) for k in GATHER_ISSUE for a in arrays]
    local = [pltpu.make_async_copy(_shard_of(kinds[a], grads[a], idx), dsts[a], lsem.at[a]) for a in arrays]

    @pl.when(step == 0)
    def _():
        for cp in copies + local:
            cp.start()

    @pl.when(step == n_steps - 1)
    def _():
        for cp in copies:
            cp.wait_send()
        for cp in copies:
            cp.wait_recv()
        for cp in local:
            cp.wait()


def _rider_sems(n):
    return [pltpu.SemaphoreType.DMA((n, NDEV)), pltpu.SemaphoreType.DMA((n, NDEV)), pltpu.SemaphoreType.DMA((n,))]


def _gather_sems(n):
    return [pltpu.SemaphoreType.DMA((n, NDEV, PARTS)), pltpu.SemaphoreType.DMA((n, NDEV, PARTS)), pltpu.SemaphoreType.DMA((n,))]


def _scatter_rider2(step, n_steps, add_at, kinds, grads, slots, bufs, sems):
    x, y, cc, idx = _mesh_pos()
    sibling = (x, y, 1 - cc)
    arrays = range(len(kinds))
    psend, precv, isend, irecv, lown, sibsem, lself = sems

    def mine(a, i):
        return _shard_of(kinds[a], grads[a], i)

    def partial(a, p):
        return pltpu.make_async_remote_copy(src_ref=mine(a, idx ^ (2 * (p + 1)) ^ 1), dst_ref=bufs[a][1].at[p], send_sem=psend.at[a, p],
                                            recv_sem=precv.at[a, p], device_id=sibling, device_id_type=MESH)

    def load(a, p):
        return pltpu.make_async_copy(mine(a, idx ^ (2 * (p + 1))), bufs[a][0].at[p], lown.at[a, p])

    def chip_sum(a, p):
        return _Copies([pltpu.make_async_remote_copy(src_ref=_part(bufs[a][0].at[p], q), dst_ref=_part(slots[a].at[2 + p], q),
                                                     send_sem=isend.at[a, p, q], recv_sem=irecv.at[a, p, q],
                                                     device_id=_peer(x, y, cc, 2 * (p + 1)), device_id_type=MESH)
                        for q in range(PARTS)])

    def to_sibling(a):
        return pltpu.make_async_remote_copy(src_ref=mine(a, idx ^ 1), dst_ref=slots[a].at[1], send_sem=sibsem.at[a, 0],
                                            recv_sem=sibsem.at[a, 1], device_id=sibling, device_id_type=MESH)

    def own(a):
        return pltpu.make_async_copy(mine(a, idx), slots[a].at[0], lself.at[a, 0])

    @pl.when(step == 0)
    def _():
        for a in arrays:
            for p in range(3):
                partial(a, p).start()
                load(a, p).start()
            to_sibling(a).start()
            own(a).start()

    @pl.when(step == add_at)
    def _():
        for a in arrays:
            for p in range(3):
                partial(a, p).wait_recv()
                load(a, p).wait()
                bufs[a][0][p] = (bufs[a][0][p].astype(F32) + bufs[a][1][p].astype(F32)).astype(BF16)
                chip_sum(a, p).start()

    @pl.when(step == n_steps - 1)
    def _():
        for a in arrays:
            for p in range(3):
                partial(a, p).wait_send()
                chip_sum(a, p).wait_send()
                chip_sum(a, p).wait_recv()
            to_sibling(a).wait_send()
            to_sibling(a).wait_recv()
            own(a).wait()


def _rider2_scratch(blocks):
    n = len(blocks)
    bufs = [pltpu.VMEM((3,) + tuple(b), BF16) for b in blocks for _ in range(2)]
    sems = [pltpu.SemaphoreType.DMA((n, 3)), pltpu.SemaphoreType.DMA((n, 3)), pltpu.SemaphoreType.DMA((n, 3, PARTS)),
            pltpu.SemaphoreType.DMA((n, 3, PARTS)), pltpu.SemaphoreType.DMA((n, 3)), pltpu.SemaphoreType.DMA((n, 2)),
            pltpu.SemaphoreType.DMA((n, 1))]
    return bufs + sems


def _rider2_split(refs, n):
    refs = list(refs)
    return [tuple(refs[2 * a:2 * a + 2]) for a in range(n)], tuple(refs[2 * n:2 * n + 7])


def _modulated(x, nw, shift, scale):
    r = _rstd(x)
    xn = x * r
    a = xn * nw
    return a * (1.0 + scale) + shift, r, xn, a


def _ctx_or_x(i, ctx_ref, x_ref):
    return jnp.where(i == 0, ctx_ref[...], x_ref[...])


def _f1_gather_matmul(idx1, ctx, x, nw, w_in, w_out, pw_in, pgrp, pw_out, lb_l, pscale, c, c_ctx, ada_w, ada_b):
    def body(idx_ref, ctx_ref, x_ref, nw_ref, win_r, wout_r, pwin_r, pgrp_r, pwout_r, lb_r, ps_r, c_r, cctx_r, ada_r, adab_r,
             g_ref, win_o, s_wout, s_pwin, s_pgrp, s_pwout, lb_o, ps_o, cg_o, mod0_o, mod1_o, modc_o,
             wslot, hx_sc, lb_g, ps_g, cg_g, mod_g, ssem, rsem, osem, dsem, sm_ssem, sm_rsem):
        del idx_ref
        s, i = pl.program_id(0), pl.program_id(1)
        x, y, cc, idx = _mesh_pos()
        k = _gather_order(s, cc)
        j = idx ^ k
        first = 4 - 2 * cc

        def rows_of(jj, q):
            return wslot.at[jj, pl.ds(q * (D // PARTS), D // PARTS), :]

        def remote(kk):
            return _Copies([pltpu.make_async_remote_copy(src_ref=rows_of(idx, q), dst_ref=rows_of(idx, q), send_sem=ssem.at[kk, q],
                                                         recv_sem=rsem.at[kk, q], device_id=_peer(x, y, cc, kk), device_id_type=MESH)
                            for q in range(PARTS)])

        def forward(kk):
            jj = idx ^ kk
            return _Copies([pltpu.make_async_remote_copy(src_ref=rows_of(jj, q), dst_ref=rows_of(jj, q), send_sem=ssem.at[kk ^ 1, q],
                                                         recv_sem=rsem.at[kk ^ 1, q], device_id=(x, y, 1 - cc), device_id_type=MESH)
                            for q in range(PARTS)])

        def relay(h):
            jj = idx ^ (4 >> h)
            return _Copies([pltpu.make_async_remote_copy(src_ref=rows_of(jj, q), dst_ref=rows_of(jj, q), send_sem=dsem.at[0, q],
                                                         recv_sem=dsem.at[1, q], device_id=_peer(x, y, cc, 2 << h), device_id_type=MESH)
                            for q in range(h * PARTS // 2, (h + 1) * PARTS // 2)])

        def to_hbm(jj, kk):
            return pltpu.make_async_copy(wslot.at[jj], win_o.at[:, pl.ds(pl.multiple_of(jj * SH_WIN, 128), SH_WIN)], osem.at[kk])

        @pl.when((s == 0) & (i == 0))
        def _():
            _small_gathers((lb_r, ps_r, c_r, cctx_r, ada_r, adab_r, lb_g, ps_g, cg_g, mod_g, lb_o, ps_o, cg_o, mod0_o, mod1_o, modc_o),
                           sm_ssem, sm_rsem)
            wslot[idx] = win_r[...].astype(BF16)
            remote(1).start()
            remote(first).start()
            s_wout[...] = wout_r[...].astype(BF16)
            s_pwin[...] = pwin_r[...].astype(BF16)
            s_pgrp[...] = pgrp_r[...].astype(BF16)
            s_pwout[...] = pwout_r[...].astype(BF16)

        @pl.when(s == 0)
        def _():
            shift = jnp.where(i == 0, modc_o[0:1, :], mod0_o[0:1, :])
            scale = jnp.where(i == 0, modc_o[1:2, :], mod0_o[1:2, :])
            hx, _, _, _ = _modulated(_ctx_or_x(i, ctx_ref, x_ref), nw_ref[...], shift, scale)
            hx_sc[i] = hx.astype(BF16)

        @pl.when((s == 2) & (i == 0))
        def _():
            remote(6 - first).start()

        @pl.when((s > 0) & (i == 0) & (k != 6))
        def _():
            remote(k).wait_recv()

            @pl.when((k & 1) == 0)
            def _():
                forward(k).start()

            for h in range(2):
                @pl.when(k == 4 >> h)
                def _():
                    relay(h).start()

        @pl.when((i == 0) & (k == 6))
        def _():
            for h in range(2):
                relay(h).wait_recv()
            forward(6).start()

        @pl.when(i == 0)
        def _():
            to_hbm(j, k).start()

        g_ref[...] = jnp.dot(hx_sc[i], wslot[j], preferred_element_type=F32)

        @pl.when((s == NDEV - 1) & (i == NT - 1))
        def _():
            for kk in (1, 2, 4):
                remote(kk).wait_send()
            for kk in GATHER_ICI:
                forward(kk).wait_send()
            for h in range(2):
                relay(h).wait_send()
            for kk in range(NDEV):
                to_hbm(idx ^ kk, kk).wait()

    grid_spec = pltpu.PrefetchScalarGridSpec(
        num_scalar_prefetch=1, grid=(NDEV, NT),
        in_specs=[VMEM_SPEC, pl.BlockSpec((TM, D), lambda s, i, ix: (jnp.maximum(i - 1, 0), 0))] + [VMEM_SPEC] * 12,
        out_specs=[pl.BlockSpec((TM, SH_WIN), lambda s, i, ix: (i, ix[0] ^ _gather_order(s, ix[0] & 1))), HBM_SPEC] + [VMEM_SPEC] * 10,
        scratch_shapes=[pltpu.VMEM((NDEV, D, SH_WIN), BF16), pltpu.VMEM((NT, TM, D), BF16),
                        pltpu.VMEM((NDEV, 2, DH), F32), pltpu.VMEM((NDEV, 1, DH), F32), pltpu.VMEM((NDEV, 1, D), F32),
                        pltpu.VMEM((NDEV, 2, 16, SH_ADA), F32),
                        pltpu.SemaphoreType.DMA((NDEV, PARTS)), pltpu.SemaphoreType.DMA((NDEV, PARTS)), pltpu.SemaphoreType.DMA((NDEV,)),
                        pltpu.SemaphoreType.DMA((2, PARTS)),
                        pltpu.SemaphoreType.DMA((4, NDEV)), pltpu.SemaphoreType.DMA((4, NDEV))])
    outs = (_sds((TT, WIN_COLS), F32), _sds((D, WIN_COLS), BF16),
            _sds((SH_ROWS, D), BF16), _sds((D, SH_PWIN), BF16), _sds((4, SH_GRP, PG), BF16), _sds((SH_ROWS, D), BF16),
            _sds((NDEV, 2, DH), F32), _sds((NDEV, 1, DH), F32), _sds((NDEV, 1, D), F32),
            _sds((3, D), F32), _sds((3, D), F32), _sds((3, D), F32))
    return pl.pallas_call(
        body, name="f1_gather_matmul", grid_spec=grid_spec, out_shape=outs,
        compiler_params=pltpu.CompilerParams(dimension_semantics=("arbitrary", "arbitrary"), vmem_limit_bytes=VMEM_LIMIT),
    )(idx1, ctx, x, nw, w_in, w_out, pw_in, pgrp, pw_out, lb_l, pscale, c, c_ctx, ada_w, ada_b)


def _gla_gates(pre, qpre, lbd, cum, rev):
    rows, n = pre.shape
    nch = rows // CHUNK
    sig = _sigmoid(pre)
    f = lbd + (1.0 - lbd) * sig
    k = 1.0 - f
    g = _dot01(cum, jnp.log(f))
    g3 = g.reshape(nch, CHUNK, n)
    last = 0 if rev else CHUNK - 1
    mid = CHUNK // 2 if rev else CHUNK // 2 - 1
    gl1, gm1 = g3[:, last:last + 1, :], g3[:, mid:mid + 1, :]

    def bc(a):
        return jnp.broadcast_to(a, g3.shape).reshape(rows, n)

    gm = bc(gm1)
    e_q, e_k = jnp.exp(g - gm), jnp.exp(gm - g)
    qsig = _sigmoid(qpre)
    qs = qpre * qsig * (DH ** -0.5)
    return dict(sig=sig, f=f, k=k, qsig=qsig, qs=qs, e_q=e_q, e_k=e_k,
                e_mid=[jnp.exp(gm1[ci]) for ci in range(nch)], e_rest=[jnp.exp(gl1[ci] - gm1[ci]) for ci in range(nch)])


def _put_heads(ref, lead, arr):
    for h in range(HEADS):
        ref[lead + (h,)] = arr[:, h * DH:(h + 1) * DH]


def _get_heads(ref, lead=()):
    return jnp.concatenate([ref[lead + (h,)] for h in range(HEADS)], axis=1)


def _gla_prep(g_all, lb, cum01, s_wout, s_pgrp):
    nch = TM // CHUNK

    def body(g_ref, lb_ref, cum_ref, swout_r, spgrp_r, p0_ref, p1_ref, v_ref, dec_ref, wout_o, pgrp_o, ssem, rsem, lsem):
        _gather_rider(pl.program_id(0), NT, NT - 1, ("rows", "grp"), (swout_r, spgrp_r), (wout_o, pgrp_o), ssem, rsem, lsem)
        qpre = g_ref[:, 3 * E:4 * E]
        _put_heads(v_ref, (), g_ref[:, 2 * E:3 * E].astype(BF16))
        for d, p_ref in ((0, p0_ref), (1, p1_ref)):
            t = _gla_gates(g_ref[:, d * E:(d + 1) * E], qpre, lb_ref[d:d + 1, :], cum_ref[d], d == 1)
            _put_heads(p_ref, (0,), (t["qs"] * t["e_q"]).astype(BF16))
            _put_heads(p_ref, (1,), (t["k"] * t["e_k"]).astype(BF16))
            for ci in range(nch):
                dec_ref[d, 0, ci:ci + 1, :] = t["e_mid"][ci]
                dec_ref[d, 0, nch + ci:nch + ci + 1, :] = t["e_rest"][ci]

    quad = pl.BlockSpec((2, HEADS, TM, DH), lambda i: (0, 0, i, 0))
    return pl.pallas_call(
        body, name="gla_prep", grid=(NT,),
        in_specs=[pl.BlockSpec((TM, 4 * E), lambda i: (i, 0)), VMEM_SPEC, VMEM_SPEC, HBM_SPEC, HBM_SPEC],
        out_specs=[quad, quad, pl.BlockSpec((HEADS, TM, DH), lambda i: (0, i, 0)), pl.BlockSpec((2, 1, 2 * nch, E), lambda i: (0, i, 0, 0)),
                   HBM_SPEC, HBM_SPEC],
        out_shape=(_sds((2, HEADS, TT, DH), BF16), _sds((2, HEADS, TT, DH), BF16), _sds((HEADS, TT, DH), BF16), _sds((2, NT, 2 * nch, E), F32),
                   _sds((E, D), BF16), _sds((4, PG, PG), BF16)),
        scratch_shapes=_gather_sems(2),
        compiler_params=pltpu.CompilerParams(dimension_semantics=("arbitrary",), vmem_limit_bytes=VMEM_LIMIT),
    )(g_all, lb, cum01, s_wout, s_pgrp)


def _scan_tile(i, rev):
    t = jnp.where(i == 0, 0, NT - i) if rev else i
    return t, pl.ds(pl.multiple_of(t * TM, TM), TM)


def _chunk_order(rev):
    n = TM // CHUNK
    return tuple(range(n - 1, -1, -1)) if rev else tuple(range(n))


def _chunk_rows(dec_ref, lanes, cis, where):
    nch = TM // CHUNK

    def rows(off):
        return jnp.stack([dec_ref[d, where[d][0], off + ci:off + ci + 1, hh * DH:(hh + 1) * DH] for (d, hh), ci in zip(lanes, cis)])

    return rows(0), rows(nch)


def _gla_fwd(p0, p1, v_all, dec, mask01, s_pwin):
    n_steps = HEADS // GLA_HB

    def body(p0_ref, p1_ref, v_ref, dec_ref, msk_ref, spwin_r, o_ref, pwin_o, ob_sc, ssem, rsem, lsem):
        _gather_rider(pl.program_id(0), n_steps, n_steps - 1, ("major",), (spwin_r,), (pwin_o,), ssem, rsem, lsem)

        lanes = [(d, hh) for d in (0, 1) for hh in range(GLA_HB)]
        nch = TM // CHUNK

        def tile_body(i, st):
            where = [_scan_tile(i, d == 1) for d in (0, 1)]

            def stacked(fn):
                return jnp.stack([fn(d, hh, where[d][1]) for d, hh in lanes])

            qg, kg = [stacked(lambda d, hh, rows, ty=ty: (p1_ref if d else p0_ref)[ty, hh, rows, :]) for ty in range(2)]
            v = stacked(lambda d, hh, rows: v_ref[hh, rows, :])
            a = _bdot_nt(qg, kg) * jnp.stack([msk_ref[d] for d, _ in lanes])
            intra = _bdot(a, v)
            outs = [[None] * nch for _ in lanes]
            for n in range(nch):
                cis = [nch - 1 - n if d else n for d, _ in lanes]

                def chunk(arr):
                    return jnp.stack([arr[l, ci * CHUNK:(ci + 1) * CHUNK] for l, ci in enumerate(cis)])

                e_mid, e_rest = _chunk_rows(dec_ref, lanes, cis, where)
                inter = _bdot_nt(chunk(qg), st * e_mid)
                for l, ci in enumerate(cis):
                    outs[l][ci] = inter[l] + intra[l, ci * CHUNK:(ci + 1) * CHUNK]
                st = st * (e_mid * e_rest) + _bdot_tn(chunk(v), chunk(kg)) * e_rest
            for l, (d, hh) in enumerate(lanes):
                (ob_sc if d else o_ref)[hh, where[d][1], :] = jnp.concatenate(outs[l], axis=0)
            return st

        lax.fori_loop(0, NT, tile_body, jnp.zeros((len(lanes), DH, DH), F32))
        o_ref[...] += ob_sc[...]

    quad = pl.BlockSpec((2, GLA_HB, TT, DH), lambda h: (0, h, 0, 0))
    head = pl.BlockSpec((GLA_HB, TT, DH), lambda h: (h, 0, 0))
    return pl.pallas_call(
        body, name="gla_fwd", grid=(n_steps,),
        in_specs=[quad, quad, head, pl.BlockSpec((2, NT, 8, GLA_HB * DH), lambda h: (0, 0, 0, h)),
                  pl.BlockSpec((2, TM, TM), lambda h: (0, 0, 0)), HBM_SPEC],
        out_specs=[head, HBM_SPEC],
        out_shape=(_sds((HEADS, TT, DH), F32), _sds((NDEV, D, SH_PWIN), BF16)),
        scratch_shapes=[pltpu.VMEM((GLA_HB, TT, DH), F32)] + _gather_sems(1),
        compiler_params=pltpu.CompilerParams(dimension_semantics=("arbitrary",), vmem_limit_bytes=VMEM_LIMIT),
    )(p0, p1, v_all, dec, mask01, s_pwin)


def _gated_norm(o, z, gw):
    r = _head_map(lambda oh: jnp.broadcast_to(_rstd(oh), oh.shape), o)
    on = o * r
    zs = _sigmoid(z)
    sz = z * zs
    return on * gw * sz, r, on, zs, sz


def _f3_out(o, g_all, x, gate, gw, wout, s_pwout):
    def body(o_ref, z_ref, x_ref, gate_ref, gw_ref, w_ref, spwout_r, x1_ref, pwout_o, ssem, rsem, lsem):
        _gather_rider(pl.program_id(0), NTX, NTX - 1, ("rows",), (spwout_r,), (pwout_o,), ssem, rsem, lsem)
        og, _, _, _, _ = _gated_norm(_get_heads(o_ref), z_ref[...], gw_ref[...])
        x1_ref[...] = x_ref[...] + gate_ref[...] * _dot(og, w_ref[...])

    return pl.pallas_call(
        body, name="f3_out", grid=(NTX,),
        in_specs=[pl.BlockSpec((HEADS, TM, DH), lambda i: (0, i + 1, 0)), pl.BlockSpec((TM, E), lambda i: (i + 1, 4)),
                  pl.BlockSpec((TM, D), lambda i: (i, 0)), pl.BlockSpec((1, D), lambda i: (0, 0)),
                  pl.BlockSpec((1, E), lambda i: (0, 0)), pl.BlockSpec((E, D), lambda i: (0, 0)), HBM_SPEC],
        out_specs=[pl.BlockSpec((TM, D), lambda i: (i, 0)), HBM_SPEC],
        out_shape=(_sds((T, D), F32), _sds((E, D), BF16)),
        scratch_shapes=_gather_sems(1),
        compiler_params=pltpu.CompilerParams(dimension_semantics=("arbitrary",)),
    )(o, g_all, x, gate, gw, wout, s_pwout)


def _pool_layer(x1, tgt, mod1, nw1, fnw, pwin, pgrp, pscale, pwout, pb, pbt, pinv):
    def body(x_ref, t_ref, m_ref, nw_ref, fw_ref, pwin_ref, pgrp_ref, ps_ref, pwout_ref, pb_ref, pbt_ref, pinv_ref,
             dx_ref, gpwin_o, gpgrp_o, gpwout_o, dmod_o, gnw_o, gfw_o, gps_o, loss_o,
             a_pwin, a_pgrp, a_pwout):
        i = pl.program_id(0)

        @pl.when(i == 0)
        def _():
            for ref in (a_pwin, a_pgrp, a_pwout, dmod_o, gnw_o, gfw_o, gps_o, loss_o):
                ref[...] = jnp.zeros_like(ref)

        shift, scale, gate = m_ref[0:1, :], m_ref[1:2, :], m_ref[2:3, :]
        nw, fw, ps = nw_ref[...], fw_ref[...], ps_ref[...]
        x1 = x_ref[...]
        hx, r1, xn, a = _modulated(x1, nw, shift, scale)
        hxb = hx.astype(BF16)
        uz = jnp.concatenate([_dot(hxb, pwin_ref[j]) for j in range(NDEV)], axis=1)
        u, z = uz[:, :E], uz[:, E:]
        pooled, ys = [], []
        for g in range(4):
            ug = u[:, g * PG:(g + 1) * PG]
            pg = _dot01(pb_ref[g], ug) * pinv_ref[g] - ug
            pooled.append(pg.astype(BF16))
            ys.append(_dot(pooled[g], pgrp_ref[g]))
        ycat = jnp.concatenate(ys, axis=1)
        y = ycat * ps
        zs = _sigmoid(z)
        sz = z * zs
        p = (y * sz).astype(BF16)
        out = _dot(p, pwout_ref[...])
        x2 = x1 + gate * out
        r2 = _rstd(x2)
        xn2 = x2 * r2
        diff = xn2 * fw - t_ref[...]
        loss_o[...] += _colsum(diff * diff)
        dyf = diff * (1.0 / D)
        gfw_o[...] += _colsum(dyf * xn2)
        dxn2 = dyf * fw
        dx2 = r2 * (dxn2 - xn2 * jnp.mean(dxn2 * xn2, axis=-1, keepdims=True))
        dgate = _colsum(dx2 * out)
        dout = (dx2 * gate).astype(BF16)
        for j in range(4):
            cs = slice(j * PG, (j + 1) * PG)
            a_pwout[:, cs] += _dot_ta(p, dout[:, cs])
        dp = _dot_tb(dout, pwout_ref[...])
        dy = dp * sz
        dz = dp * y * (zs * (1.0 + z * (1.0 - zs)))
        gps_o[...] += _colsum(dy * ycat)
        dycat = dy * ps
        dus = []
        for g in range(4):
            dyg = dycat[:, g * PG:(g + 1) * PG].astype(BF16)
            a_pgrp[g] += _dot_ta(pooled[g], dyg)
            dpg = _dot_tb(dyg, pgrp_ref[g])
            dus.append(_dot01(pbt_ref[g], dpg * pinv_ref[g]) - dpg)
        duz = jnp.concatenate(dus + [dz], axis=1).astype(BF16)
        dhx = None
        for j in range(NDEV):
            dj = duz[:, j * SH_PWIN:(j + 1) * SH_PWIN]
            a_pwin[j] += _dot_ta(hxb, dj)
            part = _dot_tb(dj, pwin_ref[j])
            dhx = part if dhx is None else dhx + part
        dmod_o[0:1, :] += _colsum(dhx)
        dmod_o[1:2, :] += _colsum(dhx * a)
        dmod_o[2:3, :] += dgate
        da = dhx * (1.0 + scale)
        gnw_o[...] += _colsum(da * xn)
        dxn = da * nw
        dx_ref[...] = dx2 + r1 * (dxn - xn * jnp.mean(dxn * xn, axis=-1, keepdims=True))

        @pl.when(i == NTX - 1)
        def _():
            gpwin_o[...] = a_pwin[...].astype(BF16)
            gpgrp_o[...] = a_pgrp[...].astype(BF16)
            gpwout_o[...] = a_pwout[...].astype(BF16)

    tile = pl.BlockSpec((TM, D), lambda i: (i, 0))
    outs = (_sds((T, D), F32), _sds((NDEV, D, SH_PWIN), BF16), _sds((4, PG, PG), BF16), _sds((E, D), BF16),
            _sds((3, D), F32), _sds((1, D), F32), _sds((1, D), F32), _sds((1, E), F32), _sds((1, D), F32))
    return pl.pallas_call(
        body, name="pool_layer", grid=(NTX,),
        in_specs=[tile, tile] + [VMEM_SPEC] * 10,
        out_specs=[tile] + [VMEM_SPEC] * 8,
        out_shape=outs,
        scratch_shapes=[pltpu.VMEM((NDEV, D, SH_PWIN), F32), pltpu.VMEM((4, PG, PG), F32), pltpu.VMEM((E, D), F32)],
        compiler_params=pltpu.CompilerParams(dimension_semantics=("arbitrary",), vmem_limit_bytes=VMEM_LIMIT),
    )(x1, tgt, mod1, nw1, fnw, pwin, pgrp, pscale, pwout, pb, pbt, pinv)


def _b3_out_bwd(dx1, o, g_all, gate, gw, wout, gpwout):
    def body(dx_ref, o_ref, z_ref, gate_ref, gw_ref, w_ref, gpwout_r, do_ref, dz_ref, gw_o, dgate_o, ggw_o, rpwout_o,
             acc, *rider):
        i = pl.program_id(0)
        bufs, sems = _rider2_split(rider, 1)
        _scatter_rider2(i, NT, 2, ("rows",), (gpwout_r,), (rpwout_o,), bufs, sems)

        @pl.when(i == 0)
        def _():
            acc[...] = jnp.zeros_like(acc)
            dgate_o[...] = jnp.zeros_like(dgate_o)
            ggw_o[...] = jnp.zeros_like(ggw_o)
            do_ref[...] = jnp.zeros_like(do_ref)
            dz_ref[...] = jnp.zeros_like(dz_ref)

        @pl.when(i > 0)
        def _():
            gw = gw_ref[...]
            z = z_ref[...]
            og, r, on, zs, sz = _gated_norm(_get_heads(o_ref), z, gw)
            ogb = og.astype(BF16)
            dx = dx_ref[...]
            dgate_o[...] += _colsum(dx * _dot(ogb, w_ref[...]))
            dy = (dx * gate_ref[...]).astype(BF16)
            for j in range(4):
                cs = slice(j * PG, (j + 1) * PG)
                acc[:, cs] += _dot_ta(ogb, dy[:, cs])
            dog = _dot_tb(dy, w_ref[...])
            dz_ref[...] = (dog * (on * gw) * (zs * (1.0 + z * (1.0 - zs)))).astype(BF16)
            dong = dog * sz
            ggw_o[...] += _colsum(dong * on)
            don = dong * gw
            do = _head_map(lambda dh, nh, rh: rh * (dh - nh * jnp.mean(dh * nh, axis=-1, keepdims=True)), don, on, r)
            _put_heads(do_ref, (), do.astype(BF16))

        @pl.when(i == NT - 1)
        def _():
            gw_o[...] = acc[...].astype(BF16)

    prev = lambda i: (jnp.maximum(i - 1, 0), 0)
    heads = pl.BlockSpec((HEADS, TM, DH), lambda i: (0, i, 0))
    return pl.pallas_call(
        body, name="b3_out_bwd", grid=(NT,),
        in_specs=[pl.BlockSpec((TM, D), prev), heads, pl.BlockSpec((TM, E), lambda i: (i, 4)),
                  VMEM_SPEC, VMEM_SPEC, VMEM_SPEC, HBM_SPEC],
        out_specs=[heads, pl.BlockSpec((TM, E), lambda i: (i, 0)), VMEM_SPEC, VMEM_SPEC, VMEM_SPEC, HBM_SPEC],
        out_shape=(_sds((HEADS, TT, DH), BF16), _sds((TT, E), BF16), _sds((E, D), BF16), _sds((1, D), F32), _sds((1, E), F32),
                   _sds((RS_SLOTS, SH_ROWS, D), BF16)),
        scratch_shapes=[pltpu.VMEM((E, D), F32)] + _rider2_scratch([(SH_ROWS, D)]),
        compiler_params=pltpu.CompilerParams(dimension_semantics=("arbitrary",), vmem_limit_bytes=VMEM_LIMIT),
    )(dx1, o, g_all, gate, gw, wout, gpwout)


def _gla_bwd(p0, p1, v_all, dec, do, mask01, gpwin, gpgrp):
    nch = TM // CHUNK
    n_steps = HEADS // GLA_HB

    def body(p0_ref, p1_ref, v_ref, dec_ref, do_ref, msk_ref, gpwin_r, gpgrp_r, d0_ref, d1_ref, dv_ref, dgl_ref, rpwin_o, rpgrp_o,
             ss_sc, dv_sc, ssem, rsem, lsem, *rider):
        _scatter_rider(pl.program_id(0), n_steps, ("grp",), (gpgrp_r,), (rpgrp_o,), ssem, rsem, lsem)
        bufs, sems = _rider2_split(rider, 1)
        _scatter_rider2(pl.program_id(0), n_steps, 1, ("major",), (gpwin_r,), (rpwin_o,), bufs, sems)

        lanes = [(d, hh) for d in (0, 1) for hh in range(GLA_HB)]
        zero = jnp.zeros((len(lanes), DH, DH), F32)
        dgl_ref[...] = jnp.zeros_like(dgl_ref)

        def p_of(d):
            return p1_ref if d else p0_ref

        def scan_step(i, n):
            where = [_scan_tile(i, d == 1) for d in (0, 1)]
            cis = [nch - 1 - n if d else n for d, _ in lanes]
            e_mid, e_rest = _chunk_rows(dec_ref, lanes, cis, where)

            def chunk(arr):
                return jnp.stack([arr[l, ci * CHUNK:(ci + 1) * CHUNK] for l, ci in enumerate(cis)])

            return where, cis, e_mid, e_rest, chunk

        def stacked(i, fn):
            where = [_scan_tile(i, d == 1) for d in (0, 1)]
            return jnp.stack([fn(d, hh, where[d][1]) for d, hh in lanes])

        def fwd_body(i, st):
            v = stacked(i, lambda d, hh, rows: v_ref[hh, rows, :])
            kg = stacked(i, lambda d, hh, rows: p_of(d)[1, hh, rows, :])
            for n in range(nch):
                _, _, e_mid, e_rest, chunk = scan_step(i, n)
                ss_sc[i * nch + n] = st
                st = st * (e_mid * e_rest) + _bdot_tn(chunk(v), chunk(kg)) * e_rest
            return st

        ss_sc[NT * nch] = lax.fori_loop(0, NT, fwd_body, zero)

        def bwd_body(ii, dst):
            i = NT - 1 - ii
            qg, kg = [stacked(i, lambda d, hh, rows, ty=ty: p_of(d)[ty, hh, rows, :]) for ty in range(2)]
            v = stacked(i, lambda d, hh, rows: v_ref[hh, rows, :])
            dob = stacked(i, lambda d, hh, rows: do_ref[hh, rows, :])
            msk = jnp.stack([msk_ref[d] for d, _ in lanes])
            a = (_bdot_nt(qg, kg) * msk).astype(BF16)
            da = (_bdot_nt(dob, v) * msk).astype(BF16)
            dqg = _bdot(da, kg)
            dkg = _bdot_tn(da, qg)
            dv_intra = _bdot_tn(a, dob)
            dv_l, dkg_l, dqg_l = ([[None] * nch for _ in lanes] for _ in range(3))
            for n in range(nch - 1, -1, -1):
                where, cis, e_mid, e_rest, chunk = scan_step(i, n)
                s_c, s_end = ss_sc[i * nch + n], ss_sc[i * nch + n + 1]
                dste = (dst * e_rest).astype(BF16)
                kg_c, v_c, dob_c = chunk(kg), chunk(v), chunk(dob)
                dv_c = chunk(dv_intra) + _bdot_nt(kg_c, dste)
                dkg_c = chunk(dkg) + _bdot(v_c, dste)
                dqg_c = chunk(dqg) + _bdot(dob_c, s_c * e_mid)
                dgl = jnp.sum(s_end * dst, axis=1, keepdims=True)
                for l, ((d, hh), ci) in enumerate(zip(lanes, cis)):
                    dv_l[l][ci], dkg_l[l][ci], dqg_l[l][ci] = dv_c[l], dkg_c[l], dqg_c[l]
                    dgl_ref[d, where[d][0], ci:ci + 1, hh * DH:(hh + 1) * DH] = dgl[l]
                dst = dst * (e_mid * e_rest) + _bdot_tn(dob_c, chunk(qg)) * e_mid
            where = [_scan_tile(i, d == 1) for d in (0, 1)]
            for l, (d, hh) in enumerate(lanes):
                rows = where[d][1]
                d_ref = d1_ref if d else d0_ref
                d_ref[0, hh, rows, :] = jnp.concatenate(dqg_l[l], axis=0).astype(BF16)
                d_ref[1, hh, rows, :] = jnp.concatenate(dkg_l[l], axis=0).astype(BF16)
                dv_sc[d, hh, rows, :] = jnp.concatenate(dv_l[l], axis=0).astype(BF16)
            return dst

        lax.fori_loop(0, NT, bwd_body, zero)
        dv_ref[...] = (dv_sc[0].astype(F32) + dv_sc[1].astype(F32)).astype(BF16)

    quad = pl.BlockSpec((2, GLA_HB, TT, DH), lambda h: (0, h, 0, 0))
    col = pl.BlockSpec((GLA_HB, TT, DH), lambda h: (h, 0, 0))
    chunkv = pl.BlockSpec((2, NT, 8, GLA_HB * DH), lambda h: (0, 0, 0, h))
    outs = (_sds((2, HEADS, TT, DH), BF16), _sds((2, HEADS, TT, DH), BF16), _sds((HEADS, TT, DH), BF16), _sds((2, NT, 8, E), F32),
            _sds((RS_SLOTS, D, SH_PWIN), BF16), _sds((NDEV, 4, SH_GRP, PG), BF16))
    return pl.pallas_call(
        body, name="gla_bwd", grid=(n_steps,),
        in_specs=[quad, quad, col, chunkv, col, pl.BlockSpec((2, TM, TM), lambda h: (0, 0, 0)), HBM_SPEC, HBM_SPEC],
        out_specs=[quad, quad, col, chunkv, HBM_SPEC, HBM_SPEC],
        out_shape=outs,
        scratch_shapes=[pltpu.VMEM((NT * nch + 1, 2 * GLA_HB, DH, DH), F32), pltpu.VMEM((2, GLA_HB, TT, DH), BF16)] + _rider_sems(1)
        + _rider2_scratch([(D, SH_PWIN)]),
        compiler_params=pltpu.CompilerParams(dimension_semantics=("arbitrary",), vmem_limit_bytes=VMEM_LIMIT),
    )(p0, p1, v_all, dec, do, mask01, gpwin, gpgrp)


TMB = 128


def _gla_post_bwd(g_all, d0, d1, dgl, dv, dz, lb, cum01, gwout):
    nch = TMB // CHUNK

    def body(g_ref, d0_ref, d1_ref, dgl_ref, dv_ref, dz_ref, lb_ref, cum_ref, gwout_r, dg_ref, dlb_ref, rwout_o, *rider):
        i = pl.program_id(0)
        bufs, sems = _rider2_split(rider, 1)
        _scatter_rider2(i, TT // TMB, 2, ("rows",), (gwout_r,), (rwout_o,), bufs, sems)

        @pl.when(i == 0)
        def _():
            dlb_ref[...] = jnp.zeros_like(dlb_ref)

        half = i & 1
        qpre = g_ref[:, 3 * E:4 * E]
        dqs_sum = None
        dpre = []
        for d, d_ref in ((0, d0_ref), (1, d1_ref)):
            rev = d == 1
            lbd = lb_ref[d:d + 1, :]
            t = _gla_gates(g_ref[:, d * E:(d + 1) * E], qpre, lbd, cum_ref[d, :TMB, :TMB], rev)
            dqs = _get_heads(d_ref, (0,)).astype(F32) * t["e_q"]
            dk = _get_heads(d_ref, (1,)).astype(F32) * t["e_k"]
            dg = t["qs"] * dqs - t["k"] * dk
            dgl8 = dgl_ref[d, 0]
            dgl_rows = [jnp.where(half == 0, dgl8[ci:ci + 1, :], dgl8[nch + ci:nch + ci + 1, :]) for ci in range(nch)]
            dgl_b = jnp.concatenate([jnp.broadcast_to(dgl_rows[ci], (CHUNK, E)) for ci in range(nch)], axis=0)
            pos = lax.broadcasted_iota(jnp.int32, (TMB, E), 0) & (CHUNK - 1)
            dg = dg + jnp.where(pos == (0 if rev else CHUNK - 1), dgl_b, 0.0)
            dlf = _dot01(cum_ref[1 - d, :TMB, :TMB], dg)
            df = dlf / t["f"] - dk
            sig = t["sig"]
            dpre.append((df * (1.0 - lbd) * sig * (1.0 - sig)).astype(BF16))
            dlb_ref[d:d + 1, :] += _colsum(df * (1.0 - sig))
            dqs_sum = dqs if dqs_sum is None else dqs_sum + dqs
            qsig = t["qsig"]
        dqpre = dqs_sum * (DH ** -0.5) * (qsig * (1.0 + qpre * (1.0 - qsig)))
        dg_ref[...] = jnp.concatenate([dpre[0], dpre[1], _get_heads(dv_ref), dqpre.astype(BF16), dz_ref[...]], axis=1)

    quad = pl.BlockSpec((2, HEADS, TMB, DH), lambda i: (0, 0, i, 0))
    tile = pl.BlockSpec((TMB, E), lambda i: (i, 0))
    return pl.pallas_call(
        body, name="gla_post_bwd", grid=(TT // TMB,),
        in_specs=[pl.BlockSpec((TMB, 4 * E), lambda i: (i, 0)), quad, quad,
                  pl.BlockSpec((2, 1, 8, E), lambda i: (0, i // 2, 0, 0)), pl.BlockSpec((HEADS, TMB, DH), lambda i: (0, i, 0)), tile,
                  VMEM_SPEC, VMEM_SPEC, HBM_SPEC],
        out_specs=[pl.BlockSpec((TMB, WIN_COLS), lambda i: (i, 0)), VMEM_SPEC, HBM_SPEC],
        out_shape=(_sds((TT, WIN_COLS), BF16), _sds((2, E), F32), _sds((RS_SLOTS, SH_ROWS, D), BF16)),
        scratch_shapes=_rider2_scratch([(SH_ROWS, D)]),
        compiler_params=pltpu.CompilerParams(dimension_semantics=("arbitrary",), vmem_limit_bytes=VMEM_LIMIT),
    )(g_all, d0, d1, dgl, dv, dz, lb, cum01, gwout)


WIN_SLOTS = 4


def _scatter_order(s, core):
    return (NDEV - 1 - s) ^ jnp.where((s >= 2) & (s <= 5) & ((s & 1) == core), 6, 0)


def _b1_in_bwd(idx1, ctx, x, dx1, dg, nw, msel, win):
    last_s = NDEV - 1
    half = D // 2
    prows = D // PARTS

    def body(idx_ref, ctx_ref, x_ref, dx1_ref, dg_ref, nw_ref, m_ref, w_ref, gx_ref, rwin_o, dmx_o, dmc_o, gnw_o,
             hx_sc, dhx_sc, acc, sbuf, pbuf, rbuf, psend, precv, isend, irecv, dsend, drecv, sibsem, lsem):
        del idx_ref
        s, i = pl.program_id(0), pl.program_id(1)
        x, y, cc, idx = _mesh_pos()
        shift, scale = m_ref[0, 0:1, :], m_ref[0, 1:2, :]
        sibling = (x, y, 1 - cc)

        def partial(p):
            return pltpu.make_async_remote_copy(src_ref=sbuf.at[0], dst_ref=pbuf.at[p], send_sem=psend.at[p], recv_sem=precv.at[p],
                                                device_id=sibling, device_id_type=MESH)

        def chip_sum(p):
            return _Copies([pltpu.make_async_remote_copy(src_ref=sbuf.at[1, pl.ds(q * prows, prows), :],
                                                         dst_ref=rwin_o.at[2 + p, pl.ds(q * prows, prows), :], send_sem=isend.at[p, q],
                                                         recv_sem=irecv.at[p, q], device_id=_peer(x, y, cc, 2 * (p + 1)), device_id_type=MESH)
                            for q in range(PARTS)])

        def relay(h):
            return _Copies([pltpu.make_async_remote_copy(src_ref=sbuf.at[1, pl.ds(h * half + q * prows, prows), :],
                                                         dst_ref=rbuf.at[h, pl.ds(q * prows, prows), :], send_sem=dsend.at[h, q],
                                                         recv_sem=drecv.at[h, q], device_id=_peer(x, y, cc, 2 * (h + 1)), device_id_type=MESH)
                            for q in range(PARTS // 2)])

        to_sibling = pltpu.make_async_remote_copy(src_ref=sbuf.at[0], dst_ref=rwin_o.at[1], send_sem=sibsem.at[0], recv_sem=sibsem.at[1],
                                                  device_id=sibling, device_id_type=MESH)
        own = pltpu.make_async_copy(sbuf.at[1], rwin_o.at[0], lsem)

        @pl.when((s == 0) & (i == 0))
        def _():
            for ref in (dmx_o, dmc_o, gnw_o):
                ref[...] = jnp.zeros_like(ref)

        @pl.when(s == 0)
        def _():
            hx, _, _, _ = _modulated(_ctx_or_x(i, ctx_ref, x_ref), nw_ref[...], shift, scale)
            hx_sc[i] = hx.astype(BF16)

        @pl.when(i == 0)
        def _():
            acc[...] = jnp.zeros_like(acc)

        dgb = dg_ref[...]
        hxb = hx_sc[i]
        for lo, hi in ((0, 256), (256, 512), (512, SH_WIN)):
            acc[:, lo:hi] += _dot_ta(hxb, dgb[:, lo:hi])
        part = _dot_tb(dgb, w_ref[...])

        @pl.when(s == 0)
        def _():
            dhx_sc[i] = part

        @pl.when(s > 0)
        def _():
            dhx_sc[i] += part

        done = i == NT - 1

        def hand_over(p, before):
            before.wait_send()
            sbuf[0] = acc[...].astype(BF16)
            partial(p).start()

        def send_chip_sum(p, before):
            for cp in before:
                cp.wait_send()
            partial(p).wait_recv()
            sbuf[1] = (acc[...] + pbuf[p].astype(F32)).astype(BF16)
            h = 1 - p
            rows = pl.ds(h * half, half)
            relay(h).wait_recv()
            sbuf[1, rows, :] = (acc[rows, :] + pbuf[p, rows, :].astype(F32) + rbuf[h].astype(F32)).astype(BF16)
            chip_sum(p).start()

        @pl.when(done & (s == 0))
        def _():
            sbuf[0] = acc[...].astype(BF16)
            partial(2).start()

        @pl.when(done & (s == 1))
        def _():
            partial(2).wait_recv()
            sbuf[1] = (acc[...] + pbuf[2].astype(F32)).astype(BF16)
            for h in range(2):
                relay(h).start()

        for core in range(2):
            @pl.when(done & (cc == core) & (s == 2))
            def _(core=core):
                hand_over(core, partial(2))

            @pl.when(done & (cc == core) & (s == 3))
            def _(core=core):
                send_chip_sum(1 - core, [relay(0), relay(1)])

            @pl.when(done & (cc == core) & (s == 4))
            def _(core=core):
                hand_over(1 - core, partial(core))

            @pl.when(done & (cc == core) & (s == 5))
            def _(core=core):
                send_chip_sum(core, [chip_sum(1 - core)])

            @pl.when(done & (cc == core) & (s == last_s - 1))
            def _(core=core):
                partial(1 - core).wait_send()
                sbuf[0] = acc[...].astype(BF16)
                to_sibling.start()

            @pl.when(done & (cc == core) & (s == last_s))
            def _(core=core):
                chip_sum(core).wait_send()
                sbuf[1] = acc[...].astype(BF16)
                own.start()

        @pl.when(s == last_s)
        def _():
            nw = nw_ref[...]
            _, r, xn, a = _modulated(_ctx_or_x(i, ctx_ref, x_ref), nw, shift, scale)
            dhx = dhx_sc[i]
            dsh, dsc = _colsum(dhx), _colsum(dhx * a)
            da = dhx * (1.0 + scale)
            gnw_o[...] += _colsum(da * xn)
            dxn = da * nw
            gx_ref[...] = dx1_ref[...] + r * (dxn - xn * jnp.mean(dxn * xn, axis=-1, keepdims=True))

            @pl.when(i == 0)
            def _():
                dmc_o[0:1, :] += dsh
                dmc_o[1:2, :] += dsc

            @pl.when(i > 0)
            def _():
                dmx_o[0:1, :] += dsh
                dmx_o[1:2, :] += dsc

        @pl.when((i == NT - 1) & (s == last_s))
        def _():
            to_sibling.wait_send()
            to_sibling.wait_recv()
            for p in range(2):
                chip_sum(p).wait_recv()
            own.wait()

    grid_spec = pltpu.PrefetchScalarGridSpec(
        num_scalar_prefetch=1, grid=(NDEV, NT),
        in_specs=[VMEM_SPEC, pl.BlockSpec((TM, D), lambda s, i, ix: (jnp.maximum(i - 1, 0), 0)),
                  pl.BlockSpec((TM, D), lambda s, i, ix: (jnp.maximum(i - 1, 0), 0)),
                  pl.BlockSpec((TM, SH_WIN), lambda s, i, ix: (i, ix[0] ^ _scatter_order(s, ix[0] & 1))), VMEM_SPEC,
                  pl.BlockSpec((1, 2, D), lambda s, i, ix: (jnp.minimum(i, 1), 0, 0)),
                  pl.BlockSpec((D, SH_WIN), lambda s, i, ix: (0, ix[0] ^ _scatter_order(s, ix[0] & 1)))],
        out_specs=[pl.BlockSpec((TM, D), lambda s, i, ix: (jnp.where(s == last_s, jnp.maximum(i - 1, 0), 0), 0)),
                   HBM_SPEC, VMEM_SPEC, VMEM_SPEC, VMEM_SPEC],
        scratch_shapes=[pltpu.VMEM((NT, TM, D), BF16), pltpu.VMEM((NT, TM, D), F32), pltpu.VMEM((D, SH_WIN), F32),
                        pltpu.VMEM((2, D, SH_WIN), BF16), pltpu.VMEM((3, D, SH_WIN), BF16), pltpu.VMEM((2, half, SH_WIN), BF16),
                        pltpu.SemaphoreType.DMA((3,)), pltpu.SemaphoreType.DMA((3,)), pltpu.SemaphoreType.DMA((2, PARTS)),
                        pltpu.SemaphoreType.DMA((2, PARTS)), pltpu.SemaphoreType.DMA((2, PARTS // 2)), pltpu.SemaphoreType.DMA((2, PARTS // 2)),
                        pltpu.SemaphoreType.DMA((2,)), pltpu.SemaphoreType.DMA])
    return pl.pallas_call(
        body, name="b1_in_bwd", grid_spec=grid_spec,
        out_shape=(_sds((T, D), F32), _sds((WIN_SLOTS, D, SH_WIN), BF16), _sds((2, D), F32), _sds((2, D), F32), _sds((1, D), F32)),
        compiler_params=pltpu.CompilerParams(dimension_semantics=("arbitrary", "arbitrary"), vmem_limit_bytes=VMEM_LIMIT),
    )(idx1, ctx, x, dx1, dg, nw, msel, win)


def _reduce_small(pd, pv, cg, c_ctx, ada_w0):
    n_arr = 3

    def body(pd_r, pv_r, cg_r, cctx_r, ada_r, gada_o, gadab_o, gcctx_o, pvsum_o, loss_o,
             pd_all, pv_all, dsc_all, dsc_mine, ssem, rsem):
        x, y, cc, idx = _mesh_pos()
        srcs = [pd_r, pv_r, dsc_mine]
        dsts = [pd_all.at[idx], pv_all.at[idx], dsc_all.at[idx]]

        def remote(a, k):
            return pltpu.make_async_remote_copy(src_ref=srcs[a], dst_ref=dsts[a], send_sem=ssem.at[a, k], recv_sem=rsem.at[a, k],
                                                device_id=_peer(x, y, cc, k), device_id_type=MESH)

        first = [remote(a, k) for k in range(1, NDEV) for a in (0, 1)]
        for cp in first:
            cp.start()
        pd_all[idx] = pd_r[...]
        pv_all[idx] = pv_r[...]
        for k in range(1, NDEV):
            remote(0, k).wait_recv()
            remote(1, k).wait_recv()
        mine = [pd_all[s, :, pl.ds(idx, 1), :] for s in range(NDEV)]
        dmc = functools.reduce(lambda u, v: u + v, [m[2] for m in mine])
        rows = _stack_rows([cg_r[i] for i in range(NDEV)] + [cctx_r[...]])
        sc = (rows * _sigmoid(rows)).astype(BF16)
        gada_o[0] = _dot_ta(sc, _stack_rows([m[0] for m in mine] + [dmc]))
        gada_o[1] = _dot_ta(sc, _stack_rows([m[1] for m in mine]))
        dsc_mine[...] = _dot_tb(jnp.broadcast_to(dmc, (8, SH_ADA)), ada_r[...])[0:1, :]
        dsc_all[idx] = dsc_mine[...]
        second = [remote(2, k) for k in range(1, NDEV)]
        for cp in second:
            cp.start()
        tot = [functools.reduce(lambda u, v: u + v, [pd_all[s, l] for s in range(NDEV)]) for l in range(3)]
        gadab_o[0] = tot[0] + tot[2]
        gadab_o[1] = tot[1]
        pvs = functools.reduce(lambda u, v: u + v, [pv_all[s] for s in range(NDEV)])
        pvsum_o[...] = pvs
        loss_o[...] = jnp.broadcast_to(jnp.sum(pvs[:, PV_LOSS:PV_LOSS + D], axis=-1, keepdims=True) * (0.5 / D), (1, 128))
        for k in range(1, NDEV):
            remote(2, k).wait_recv()
        dsc = functools.reduce(lambda u, v: u + v, [dsc_all[s] for s in range(NDEV)])
        cx = cctx_r[...]
        sx = _sigmoid(cx)
        gcctx_o[...] = dsc * (sx * (1.0 + cx * (1.0 - sx)))
        for cp in first + second:
            cp.wait_send()

    outs = (_sds((2, D, SH_ADA), F32), _sds((2, NDEV, SH_ADA), F32), _sds((1, D), F32), _sds((1, PV_LEN), F32), _sds((1, 128), F32))
    return pl.pallas_call(
        body, name="reduce_small", out_shape=outs,
        in_specs=[VMEM_SPEC] * 5, out_specs=[VMEM_SPEC] * 5,
        scratch_shapes=[
            pltpu.VMEM((NDEV, 3, NDEV, SH_ADA), F32), pltpu.VMEM((NDEV, 1, PV_LEN), F32), pltpu.VMEM((NDEV, 1, D), F32),
            pltpu.VMEM((1, D), F32),
            pltpu.SemaphoreType.DMA((n_arr, NDEV)), pltpu.SemaphoreType.DMA((n_arr, NDEV)),
        ],
        compiler_params=pltpu.CompilerParams(vmem_limit_bytes=VMEM_LIMIT),
    )(pd, pv, cg, c_ctx, ada_w0)


PV_NW, PV_GNORM, PV_FINAL, PV_LB, PV_PSCALE, PV_LOSS, PV_LEN = 0, 2 * D, 3 * D, 4 * D, 6 * D, 7 * D, 8 * D


def _adamw(w, g, m, v):
    m = ADAM_B1 * m + (1.0 - ADAM_B1) * g
    v = ADAM_B2 * v + (1.0 - ADAM_B2) * (g * g)
    m_hat = m / (1.0 - ADAM_B1 ** ADAM_STEP)
    v_hat = v / (1.0 - ADAM_B2 ** ADAM_STEP)
    delta = -ADAM_LR * (m_hat / (jnp.sqrt(v_hat) + ADAM_EPS) + ADAM_WD * w)
    return delta, m, v


ADAM_STEPS = 8


def _adam_all(sharded, dense, small, lb_idx, lbv):
    ns, nd, nsm = len(sharded), len(dense), len(small)

    def body(*refs):
        it = iter(refs)
        sh_in = [[next(it) for _ in range(4)] for _ in range(ns)]
        de_in = [[next(it) for _ in range(4)] for _ in range(nd)]
        sm_in = [[next(it) for _ in range(4)] for _ in range(nsm)]
        lb_r = next(it)
        sh_out = [[next(it) for _ in range(4)] for _ in range(ns)]
        de_out = [[next(it) for _ in range(3)] for _ in range(nd)]
        sm_out = [[next(it) for _ in range(4)] for _ in range(nsm)]
        for (p, w, m, v), outs in zip(sh_in, sh_out):
            g = p[0].astype(F32)
            for s in range(1, p.shape[0]):
                g = g + p[s].astype(F32)
            d, mn, vn = _adamw(w[...], g, m[...], v[...])
            outs[0][...], outs[1][...], outs[2][...], outs[3][...] = g, d, mn, vn
        for (g, w, m, v), outs in zip(de_in, de_out):
            d, mn, vn = _adamw(w[...], g[...], m[...], v[...])
            outs[0][...], outs[1][...], outs[2][...] = d, mn, vn

        @pl.when(pl.program_id(0) == 0)
        def _():
            for j, ((g, w, m, v), outs) in enumerate(zip(sm_in, sm_out)):
                gj = g[...]
                if j == lb_idx:
                    gj = gj * lb_r[...] * (1.0 - lb_r[...])
                d, mn, vn = _adamw(w[...], gj, m[...], v[...])
                outs[0][...], outs[1][...], outs[2][...], outs[3][...] = gj, d, mn, vn

    def tile(a):
        return pl.BlockSpec((a.shape[0] // ADAM_STEPS, a.shape[1]), lambda i: (i, 0))

    in_specs, out_specs, out_shape, args = [], [], [], []
    for p, w, m, v in sharded:
        in_specs += [pl.BlockSpec((p.shape[0], p.shape[1] // ADAM_STEPS, p.shape[2]), lambda i: (0, i, 0))] + [tile(w)] * 3
        args += [p, w, m, v]
    for g, w, m, v in dense:
        in_specs += [tile(w)] * 4
        args += [g, w, m, v]
    for g, w, m, v in small:
        in_specs += [VMEM_SPEC] * 4
        args += [g, w, m, v]
    in_specs.append(VMEM_SPEC)
    args.append(lbv)
    for _, w, _, _ in sharded:
        out_specs += [tile(w)] * 4
        out_shape += [_sds(w.shape, F32)] * 4
    for _, w, _, _ in dense:
        out_specs += [tile(w)] * 3
        out_shape += [_sds(w.shape, F32)] * 3
    for _, w, _, _ in small:
        out_specs += [VMEM_SPEC] * 4
        out_shape += [_sds(w.shape, F32)] * 4
    res = pl.pallas_call(body, name="adam_all", grid=(ADAM_STEPS,), in_specs=in_specs, out_specs=out_specs, out_shape=tuple(out_shape),
                         compiler_params=pltpu.CompilerParams(dimension_semantics=("arbitrary",), vmem_limit_bytes=VMEM_LIMIT))(*args)
    it = iter(res)
    return ([tuple(next(it) for _ in range(4)) for _ in range(ns)], [tuple(next(it) for _ in range(3)) for _ in range(nd)],
            [tuple(next(it) for _ in range(4)) for _ in range(nsm)])


def kernel(x, c, ctx, c_ctx, ada_w, ada_b, norm_w, hgrn_w_in, hgrn_lb_logits, hgrn_gnorm_w, hgrn_w_out, pool_w_in, pool_w_grp, pool_scale, pool_w_out, final_norm_w, loss_target, m_c_ctx, m_ada_w, m_ada_b, m_norm_w, m_hgrn_w_in, m_hgrn_lb_logits, m_hgrn_gnorm_w, m_hgrn_w_out, m_pool_w_in, m_pool_w_grp, m_pool_scale, m_pool_w_out, m_final_norm_w, v_c_ctx, v_ada_w, v_ada_b, v_norm_w, v_hgrn_w_in, v_hgrn_lb_logits, v_hgrn_gnorm_w, v_hgrn_w_out, v_pool_w_in, v_pool_w_grp, v_pool_scale, v_pool_w_out, v_final_norm_w):
    idx = 4 * lax.axis_index("x") + 2 * lax.axis_index("y") + lax.axis_index("c")
    cctx2 = c_ctx.reshape(1, D)
    cum01, mask01 = _gla_consts()
    pb, pbt, pinv = _pool_consts()

    idx1 = idx.reshape(1).astype(jnp.int32)
    nw0, nw1 = norm_w[0:1], norm_w[1:2]
    fnw = final_norm_w.reshape(1, D)
    g_all, win, s_wout, s_pwin, s_pgrp, s_pwout, lbl_g, ps_g, cg, mod0, mod1, modc = _f1_gather_matmul(
        idx1, ctx[0], x[0], nw0, hgrn_w_in[0], hgrn_w_out[0], pool_w_in[0], pool_w_grp[0], pool_w_out[0], hgrn_lb_logits[0],
        pool_scale, c, cctx2, ada_w, ada_b)
    lb = jax.nn.sigmoid(jnp.transpose(lbl_g, (1, 0, 2)).reshape(2, E))
    pscale = ps_g.reshape(1, E)
    msel = jnp.stack([modc[:2], mod0[:2]])
    p0, p1, v_all, dec, wout, pgrp = _gla_prep(g_all, lb, cum01, s_wout, s_pgrp)
    o, pwin = _gla_fwd(p0, p1, v_all, dec, mask01, s_pwin)
    x1, pwout = _f3_out(o, g_all, x[0], mod0[2:3], hgrn_gnorm_w, wout, s_pwout)
    dx1, gpwin, gpgrp, gpwout, dmod1, gnw1, gfw, gps, lossv = _pool_layer(
        x1, loss_target[0], mod1, nw1, fnw, pwin, pgrp, pscale, pwout, pb, pbt, pinv)
    do, dz, gwout, dgate0, ggw, rpwout = _b3_out_bwd(dx1, o, g_all, mod0[2:3], hgrn_gnorm_w, wout, gpwout)
    d0, d1, dv, dgl, rpwin, rpgrp = _gla_bwd(p0, p1, v_all, dec, do, mask01, gpwin, gpgrp)
    dg, dlb, rwout = _gla_post_bwd(g_all, d0, d1, dgl, dv, dz, lb, cum01, gwout)
    grad_x, rwin, dmx, dmc, gnw0 = _b1_in_bwd(idx1, ctx[0], x[0], dx1, dg, nw0, msel, win)

    dmod0 = jnp.concatenate([dmx, dgate0], axis=0)
    dmodc = jnp.concatenate([dmc, jnp.zeros((1, D), F32)], axis=0)
    pd = jnp.stack([dmod0, dmod1, dmodc]).reshape(3, NDEV, SH_ADA)
    pv = jnp.concatenate([gnw0, gnw1, ggw, gfw, dlb.reshape(1, 2 * E), gps, lossv], axis=1)
    g_ada, g_adab, g_cctx, pvsum, loss128 = _reduce_small(pd, pv, cg, cctx2, ada_w[0])

    g2 = (4 * SH_GRP, PG)
    sharded_names = ["hgrn_w_in", "hgrn_w_out", "pool_w_in", "pool_w_grp", "pool_w_out"]
    sharded = [(rwin, hgrn_w_in[0], m_hgrn_w_in[0], v_hgrn_w_in[0]),
               (rwout, hgrn_w_out[0], m_hgrn_w_out[0], v_hgrn_w_out[0]),
               (rpwin, pool_w_in[0], m_pool_w_in[0], v_pool_w_in[0]),
               (rpgrp.reshape((NDEV,) + g2), pool_w_grp[0].reshape(g2), m_pool_w_grp[0].reshape(g2), v_pool_w_grp[0].reshape(g2)),
               (rpwout, pool_w_out[0], m_pool_w_out[0], v_pool_w_out[0])]
    a2 = (2 * D, SH_ADA)
    g_ada2 = g_ada.reshape(a2)
    dense = [(g_ada2, ada_w.reshape(a2), m_ada_w.reshape(a2), v_ada_w.reshape(a2))]
    lb_me = lax.dynamic_slice_in_dim(lb, idx * DH, DH, axis=1)
    small_names = ["c_ctx", "ada_b", "norm_w", "hgrn_lb_logits", "hgrn_gnorm_w", "pool_scale", "final_norm_w"]
    small = [(g_cctx, cctx2, m_c_ctx.reshape(1, D), v_c_ctx.reshape(1, D)),
             (g_adab.reshape(2, 3 * D), ada_b, m_ada_b, v_ada_b),
             (pvsum[:, PV_NW:PV_NW + 2 * D].reshape(2, D), norm_w, m_norm_w, v_norm_w),
             (lax.dynamic_slice_in_dim(pvsum[:, PV_LB:PV_LB + 2 * E].reshape(2, E), idx * DH, DH, axis=1),
              hgrn_lb_logits[0], m_hgrn_lb_logits[0], v_hgrn_lb_logits[0]),
             (pvsum[:, PV_GNORM:PV_GNORM + E], hgrn_gnorm_w, m_hgrn_gnorm_w, v_hgrn_gnorm_w),
             (lax.dynamic_slice_in_dim(pvsum[:, PV_PSCALE:PV_PSCALE + E], idx * DH, DH, axis=1), pool_scale, m_pool_scale, v_pool_scale),
             (pvsum[:, PV_FINAL:PV_FINAL + D], fnw, m_final_norm_w.reshape(1, D), v_final_norm_w.reshape(1, D))]
    r_sharded, r_dense, r_small = _adam_all(sharded, dense, small, 3, lb_me)
    out = dict(zip(sharded_names, r_sharded))
    out["ada_w"] = (g_ada2,) + r_dense[0]
    out.update(zip(small_names, r_small))

    shapes = {"c_ctx": (D,), "ada_w": (2, D, SH_ADA), "ada_b": (2, 3 * D), "norm_w": (2, D), "hgrn_w_in": (1, D, SH_WIN),
              "hgrn_lb_logits": (1, 2, DH), "hgrn_gnorm_w": (1, E), "hgrn_w_out": (1, SH_ROWS, D), "pool_w_in": (1, D, SH_PWIN),
              "pool_w_grp": (1, 4, SH_GRP, PG), "pool_scale": (1, DH), "pool_w_out": (1, SH_ROWS, D), "final_norm_w": (D,)}
    order = ["c_ctx", "ada_w", "ada_b", "norm_w", "hgrn_w_in", "hgrn_lb_logits", "hgrn_gnorm_w", "hgrn_w_out", "pool_w_in",
             "pool_w_grp", "pool_scale", "pool_w_out", "final_norm_w"]
    flat = [out[name][q].reshape(shapes[name]) for q in range(4) for name in order]
    return (loss128[0, 0], grad_x[None], *flat)
```

```python
import functools

import numpy as np
import jax
import jax.numpy as jnp
from jax import lax
from jax.experimental import pallas as pl
from jax.experimental.pallas import tpu as pltpu

F32 = jnp.float32
BF16 = jnp.bfloat16

D = 1024
E = 1024
HEADS = 8
DH = 128
CHUNK = 64
T = 2048
TC = 256
TT = T + TC
TM = 256
NT = TT // TM
NTX = T // TM
NDEV = 8
GRID_W = 64
POOL_WINDOWS = (2, 4, 8, 16)
PG = 256
EPS = 1e-6
WIN_COLS = 5 * E
SH_WIN = WIN_COLS // NDEV
SH_PWIN = 2 * E // NDEV
SH_ROWS = E // NDEV
SH_GRP = PG // NDEV
SH_ADA = 3 * D // NDEV
VMEM_LIMIT = 56 * 1024 * 1024

ADAM_LR, ADAM_B1, ADAM_B2, ADAM_EPS, ADAM_WD, ADAM_STEP = 0.001, 0.9, 0.999, 1e-08, 0.01, 10

MESH = pl.DeviceIdType.MESH
VMEM_SPEC = pl.BlockSpec(memory_space=pltpu.VMEM)
HBM_SPEC = pl.BlockSpec(memory_space=pltpu.HBM)
ANY_SPEC = pl.BlockSpec(memory_space=pl.ANY)


def _sds(shape, dtype):
    return jax.ShapeDtypeStruct(shape, dtype)


def _bf(a):
    return a if a.dtype == BF16 else a.astype(BF16)


def _dot(a, b):
    return lax.dot_general(_bf(a), _bf(b), (((1,), (0,)), ((), ())), preferred_element_type=F32)


def _dot_tb(a, b):
    return lax.dot_general(_bf(a), _bf(b), (((1,), (1,)), ((), ())), preferred_element_type=F32)


def _dot_ta(a, b):
    return lax.dot_general(_bf(a), _bf(b), (((0,), (0,)), ((), ())), preferred_element_type=F32)


def _bdot(a, b):
    return lax.dot_general(_bf(a), _bf(b), (((2,), (1,)), ((0,), (0,))), preferred_element_type=F32)


def _bdot_nt(a, b):
    return lax.dot_general(_bf(a), _bf(b), (((2,), (2,)), ((0,), (0,))), preferred_element_type=F32)


def _bdot_tn(a, b):
    return lax.dot_general(_bf(a), _bf(b), (((1,), (1,)), ((0,), (0,))), preferred_element_type=F32)


def _dot01(m01, x):
    hi = x.astype(BF16)
    lo = (x - hi.astype(F32)).astype(BF16)
    return _dot(m01, hi) + _dot(m01, lo)


def _rstd(x):
    return lax.rsqrt(jnp.mean(x * x, axis=-1, keepdims=True) + EPS)


def _sigmoid(x):
    return jax.nn.sigmoid(x)


def _colsum(a):
    return jnp.sum(a, axis=0, keepdims=True)


def _stack_rows(rows):
    n = rows[0].shape[-1]
    rid = lax.broadcasted_iota(jnp.int32, (16, n), 0)
    out = jnp.zeros((16, n), F32)
    for i, r in enumerate(rows):
        out = jnp.where(rid == i, r, out)
    return out


def _head_map(fn, *arrs):
    outs = [fn(*[a[:, h * DH:(h + 1) * DH] for a in arrs]) for h in range(HEADS)]
    return jnp.concatenate(outs, axis=1)


def _gla_consts():
    r = np.arange(TM)[:, None]
    c = np.arange(TM)[None, :]
    same = (r // CHUNK) == (c // CHUNK)
    tril = same & (c <= r)
    triu = same & (c >= r)
    m = np.stack([tril, triu]).astype(np.float32)
    return jnp.asarray(m, BF16), jnp.asarray(m, F32)


def _pool_consts():
    r = np.arange(TM)[:, None]
    c = np.arange(TM)[None, :]
    same = (r // GRID_W) == (c // GRID_W)
    rp, cp = r % GRID_W, c % GRID_W
    bs, inv = [], []
    for w in POOL_WINDOWS:
        lo = np.clip(rp - w // 2, 0, GRID_W)
        hi = np.clip(rp - w // 2 + w, 0, GRID_W)
        bs.append(same & (cp >= lo) & (cp < hi))
        inv.append(1.0 / (hi - lo).astype(np.float32))
    b = np.stack(bs).astype(np.float32)
    bt = np.transpose(b, (0, 2, 1))
    return jnp.asarray(b, BF16), jnp.asarray(bt, BF16), jnp.asarray(np.stack(inv), F32)


def _mesh_pos():
    x, y, c = lax.axis_index("x"), lax.axis_index("y"), lax.axis_index("c")
    return x, y, c, 4 * x + 2 * y + c


def _peer(x, y, c, k):
    return (x ^ ((k >> 2) & 1), y ^ ((k >> 1) & 1), c ^ (k & 1))


PARTS = 1


class _Copies:
    def __init__(self, copies):
        self.copies = copies

    def start(self):
        for cp in self.copies:
            cp.start()

    def wait_send(self):
        for cp in self.copies:
            cp.wait_send()

    def wait_recv(self):
        for cp in self.copies:
            cp.wait_recv()


def _part(ref, q):
    n = ref.shape[0] // PARTS
    return ref.at[pl.ds(q * n, n)]


def _small_gathers(refs, ssem, rsem):
    lb_r, ps_r, c_r, cctx_r, ada_r, adab_r, lb_o, ps_o, cg_o, mod_o, lb_out, ps_out, cg_out, mod0_o, mod1_o, modc_o = refs
    x, y, cc, idx = _mesh_pos()
    srcs = [lb_r, ps_r, c_r, mod_o.at[idx]]
    mine = [lb_o.at[idx], ps_o.at[idx], cg_o.at[idx], mod_o.at[idx]]

    def remote(a, k):
        return pltpu.make_async_remote_copy(src_ref=srcs[a], dst_ref=mine[a], send_sem=ssem.at[a, k], recv_sem=rsem.at[a, k],
                                            device_id=_peer(x, y, cc, k), device_id_type=MESH)

    first = [remote(a, k) for k in range(1, NDEV) for a in (2, 0, 1)]
    for cp in first:
        cp.start()
    lb_o[idx] = lb_r[...]
    ps_o[idx] = ps_r[...]
    cg_o[idx] = c_r[...]
    for k in range(1, NDEV):
        remote(2, k).wait_recv()
    rows = _stack_rows([cg_o[i] for i in range(NDEV)] + [cctx_r[...]])
    sc = rows * _sigmoid(rows)
    for l in range(2):
        mod_o[idx, l] = _dot(sc, ada_r[l])
    second = [remote(3, k) for k in range(1, NDEV)]
    for cp in second:
        cp.start()
    for k in range(1, NDEV):
        remote(3, k).wait_recv()

    def mod_rows(l, row):
        full = jnp.concatenate([mod_o[s, l, row, :] for s in range(NDEV)], axis=1) + adab_r[l:l + 1, :]
        return [full[:, j * D:(j + 1) * D] for j in range(3)]

    me = pl.ds(idx, 1)
    for out, parts in ((mod0_o, mod_rows(0, me)), (mod1_o, mod_rows(1, me)), (modc_o, mod_rows(0, slice(NDEV, NDEV + 1)))):
        for j in range(3):
            out[j:j + 1, :] = parts[j]
    for cp in first + second:
        cp.wait_send()
    for k in range(1, NDEV):
        for a in (0, 1):
            remote(a, k).wait_recv()
    lb_out[...] = lb_o[...]
    ps_out[...] = ps_o[...]
    cg_out[...] = cg_o[...]


def _gather_order(s, core):
    k = jnp.where(s == 2, 4, jnp.where(s == 4, 2, s))
    return k ^ jnp.where((core == 1) & (s >= 2) & (s <= 5), 6, 0)


GATHER_ISSUE = (1, 2, 4, 3, 5, 6, 7)
GATHER_ICI = (2, 4, 6)
GATHER_DIRECT = (1,) + GATHER_ICI
GLA_HB = 2
RS_SLOTS = 5


def _shard_of(kind, ref, i):
    if kind == "rows":
        return ref.at[pl.ds(pl.multiple_of(i * SH_ROWS, SH_ROWS), SH_ROWS), :]
    if kind == "major":
        return ref.at[i]
    assert kind == "grp"
    return ref.at[:, pl.ds(pl.multiple_of(i * SH_GRP, SH_GRP), SH_GRP), :]


def _gather_rider(step, n_steps, forward_at, kinds, srcs, outs, ssem, rsem, lsem):
    x, y, cc, idx = _mesh_pos()
    arrays = range(len(kinds))
    mine = [_shard_of(kinds[a], outs[a], idx) for a in arrays]

    def remote(a, k):
        return _Copies([pltpu.make_async_remote_copy(src_ref=_part(srcs[a], q), dst_ref=_part(mine[a], q), send_sem=ssem.at[a, k, q],
                                                     recv_sem=rsem.at[a, k, q], device_id=_peer(x, y, cc, k), device_id_type=MESH)
                        for q in range(PARTS)])

    def forward(a, k):
        blk = _shard_of(kinds[a], outs[a], idx ^ k)
        return _Copies([pltpu.make_async_remote_copy(src_ref=_part(blk, q), dst_ref=_part(blk, q), send_sem=ssem.at[a, k ^ 1, q],
                                                     recv_sem=rsem.at[a, k ^ 1, q], device_id=(x, y, 1 - cc), device_id_type=MESH)
                        for q in range(PARTS)])

    copies = [remote(a, k) for k in GATHER_DIRECT for a in arrays]
    passed = [forward(a, k) for k in GATHER_ICI for a in arrays]
    local = [pltpu.make_async_copy(srcs[a], mine[a], lsem.at[a]) for a in arrays]

    @pl.when(step == 0)
    def _():
        for cp in copies + local:
            cp.start()

    @pl.when(step == forward_at)
    def _():
        for k in GATHER_ICI:
            for a in arrays:
                remote(a, k).wait_recv()
                forward(a, k).start()

    @pl.when(step == n_steps - 1)
    def _():
        for cp in copies + passed:
            cp.wait_send()
        for a in arrays:
            remote(a, 1).wait_recv()
        for cp in passed:
            cp.wait_recv()
        for cp in local:
            cp.wait()


def _scatter_rider(step, n_steps, kinds, grads, slots, ssem, rsem, lsem):
    x, y, cc, idx = _mesh_pos()
    arrays = range(len(kinds))
    dsts = [slots[a].at[idx] for a in arrays]

    def remote(a, k):
        px, py, pc = _peer(x, y, cc, k)
        return pltpu.make_async_remote_copy(src_ref=_shard_of(kinds[a], grads[a], 4 * px + 2 * py + pc), dst_ref=dsts[a],
                                            send_sem=ssem.at[a, k], recv_sem=rsem.at[a, k], device_id=(px, py, pc), device_id_type=MESH)

    copies = [remote(a, k) for k in GATHER_ISSUE for a in arrays]
    local = [pltpu.make_async_copy(_shard_of(kinds[a], grads[a], idx), dsts[a], lsem.at[a]) for a in arrays]

    @pl.when(step == 0)
    def _():
        for cp in copies + local:
            cp.start()

    @pl.when(step == n_steps - 1)
    def _():
        for cp in copies:
            cp.wait_send()
        for cp in copies:
            cp.wait_recv()
        for cp in local:
            cp.wait()


def _rider_sems(n):
    return [pltpu.SemaphoreType.DMA((n, NDEV)), pltpu.SemaphoreType.DMA((n, NDEV)), pltpu.SemaphoreType.DMA((n,))]


def _gather_sems(n):
    return [pltpu.SemaphoreType.DMA((n, NDEV, PARTS)), pltpu.SemaphoreType.DMA((n, NDEV, PARTS)), pltpu.SemaphoreType.DMA((n,))]


def _scatter_rider2(step, n_steps, add_at, kinds, grads, slots, bufs, sems):
    x, y, cc, idx = _mesh_pos()
    sibling = (x, y, 1 - cc)
    arrays = range(len(kinds))
    psend, precv, isend, irecv, lown, sibsem, lself = sems

    def mine(a, i):
        return _shard_of(kinds[a], grads[a], i)

    def partial(a, p):
        return pltpu.make_async_remote_copy(src_ref=mine(a, idx ^ (2 * (p + 1)) ^ 1), dst_ref=bufs[a][1].at[p], send_sem=psend.at[a, p],
                                            recv_sem=precv.at[a, p], device_id=sibling, device_id_type=MESH)

    def load(a, p):
        return pltpu.make_async_copy(mine(a, idx ^ (2 * (p + 1))), bufs[a][0].at[p], lown.at[a, p])

    def chip_sum(a, p):
        return _Copies([pltpu.make_async_remote_copy(src_ref=_part(bufs[a][0].at[p], q), dst_ref=_part(slots[a].at[2 + p], q),
                                                     send_sem=isend.at[a, p, q], recv_sem=irecv.at[a, p, q],
                                                     device_id=_peer(x, y, cc, 2 * (p + 1)), device_id_type=MESH)
                        for q in range(PARTS)])

    def to_sibling(a):
        return pltpu.make_async_remote_copy(src_ref=mine(a, idx ^ 1), dst_ref=slots[a].at[1], send_sem=sibsem.at[a, 0],
                                            recv_sem=sibsem.at[a, 1], device_id=sibling, device_id_type=MESH)

    def own(a):
        return pltpu.make_async_copy(mine(a, idx), slots[a].at[0], lself.at[a, 0])

    @pl.when(step == 0)
    def _():
        for a in arrays:
            for p in range(3):
                partial(a, p).start()
                load(a, p).start()
            to_sibling(a).start()
            own(a).start()

    @pl.when(step == add_at)
    def _():
        for a in arrays:
            for p in range(3):
                partial(a, p).wait_recv()
                load(a, p).wait()
                bufs[a][0][p] = (bufs[a][0][p].astype(F32) + bufs[a][1][p].astype(F32)).astype(BF16)
                chip_sum(a, p).start()

    @pl.when(step == n_steps - 1)
    def _():
        for a in arrays:
            for p in range(3):
                partial(a, p).wait_send()
                chip_sum(a, p).wait_send()
                chip_sum(a, p).wait_recv()
            to_sibling(a).wait_send()
            to_sibling(a).wait_recv()
            own(a).wait()


def _rider2_scratch(blocks):
    n = len(blocks)
    bufs = [pltpu.VMEM((3,) + tuple(b), BF16) for b in blocks for _ in range(2)]
    sems = [pltpu.SemaphoreType.DMA((n, 3)), pltpu.SemaphoreType.DMA((n, 3)), pltpu.SemaphoreType.DMA((n, 3, PARTS)),
            pltpu.SemaphoreType.DMA((n, 3, PARTS)), pltpu.SemaphoreType.DMA((n, 3)), pltpu.SemaphoreType.DMA((n, 2)),
            pltpu.SemaphoreType.DMA((n, 1))]
    return bufs + sems


def _rider2_split(refs, n):
    refs = list(refs)
    return [tuple(refs[2 * a:2 * a + 2]) for a in range(n)], tuple(refs[2 * n:2 * n + 7])


def _modulated(x, nw, shift, scale):
    r = _rstd(x)
    xn = x * r
    a = xn * nw
    return a * (1.0 + scale) + shift, r, xn, a


def _ctx_or_x(i, ctx_ref, x_ref):
    return jnp.where(i == 0, ctx_ref[...], x_ref[...])


def _f1_gather_matmul(idx1, ctx, x, nw, w_in, w_out, pw_in, pgrp, pw_out, lb_l, pscale, c, c_ctx, ada_w, ada_b):
    def body(idx_ref, ctx_ref, x_ref, nw_ref, win_r, wout_r, pwin_r, pgrp_r, pwout_r, lb_r, ps_r, c_r, cctx_r, ada_r, adab_r,
             g_ref, win_o, s_wout, s_pwin, s_pgrp, s_pwout, lb_o, ps_o, cg_o, mod0_o, mod1_o, modc_o,
             wslot, hx_sc, lb_g, ps_g, cg_g, mod_g, ssem, rsem, osem, dsem, sm_ssem, sm_rsem):
        del idx_ref
        s, i = pl.program_id(0), pl.program_id(1)
        x, y, cc, idx = _mesh_pos()
        k = _gather_order(s, cc)
        j = idx ^ k
        first = 4 - 2 * cc

        def rows_of(jj, q):
            return wslot.at[jj, pl.ds(q * (D // PARTS), D // PARTS), :]

        def remote(kk):
            return _Copies([pltpu.make_async_remote_copy(src_ref=rows_of(idx, q), dst_ref=rows_of(idx, q), send_sem=ssem.at[kk, q],
                                                         recv_sem=rsem.at[kk, q], device_id=_peer(x, y, cc, kk), device_id_type=MESH)
                            for q in range(PARTS)])

        def forward(kk):
            jj = idx ^ kk
            return _Copies([pltpu.make_async_remote_copy(src_ref=rows_of(jj, q), dst_ref=rows_of(jj, q), send_sem=ssem.at[kk ^ 1, q],
                                                         recv_sem=rsem.at[kk ^ 1, q], device_id=(x, y, 1 - cc), device_id_type=MESH)
                            for q in range(PARTS)])

        def relay(h):
            blk = wslot.at[idx ^ (4 >> h), pl.ds(h * (D // 2), D // 2), :]
            return pltpu.make_async_remote_copy(src_ref=blk, dst_ref=blk, send_sem=dsem.at[0, h], recv_sem=dsem.at[1, h],
                                                device_id=_peer(x, y, cc, 2 << h), device_id_type=MESH)

        def to_hbm(jj, kk):
            return pltpu.make_async_copy(wslot.at[jj], win_o.at[jj], osem.at[kk])

        @pl.when((s == 0) & (i == 0))
        def _():
            _small_gathers((lb_r, ps_r, c_r, cctx_r, ada_r, adab_r, lb_g, ps_g, cg_g, mod_g, lb_o, ps_o, cg_o, mod0_o, mod1_o, modc_o),
                           sm_ssem, sm_rsem)
            wslot[idx] = win_r[...].astype(BF16)
            remote(1).start()
            remote(first).start()
            s_wout[...] = wout_r[...].astype(BF16)
            s_pwin[...] = pwin_r[...].astype(BF16)
            s_pgrp[...] = pgrp_r[...].astype(BF16)
            s_pwout[...] = pwout_r[...].astype(BF16)

        @pl.when(s == 0)
        def _():
            shift = jnp.where(i == 0, modc_o[0:1, :], mod0_o[0:1, :])
            scale = jnp.where(i == 0, modc_o[1:2, :], mod0_o[1:2, :])
            hx, _, _, _ = _modulated(_ctx_or_x(i, ctx_ref, x_ref), nw_ref[...], shift, scale)
            hx_sc[i] = hx.astype(BF16)

        @pl.when((s == 2) & (i == 0))
        def _():
            remote(6 - first).start()

        @pl.when((s > 0) & (i == 0) & (k != 6))
        def _():
            remote(k).wait_recv()

            @pl.when((k & 1) == 0)
            def _():
                forward(k).start()

            for h in range(2):
                @pl.when(k == 4 >> h)
                def _():
                    relay(h).start()

        @pl.when((i == 0) & (k == 6))
        def _():
            for h in range(2):
                relay(h).wait_recv()
            forward(6).start()

        @pl.when(i == 0)
        def _():
            to_hbm(j, k).start()

        g_ref[...] = jnp.dot(hx_sc[i], wslot[j], preferred_element_type=F32)

        @pl.when((s == NDEV - 1) & (i == NT - 1))
        def _():
            for kk in (1, 2, 4):
                remote(kk).wait_send()
            for kk in GATHER_ICI:
                forward(kk).wait_send()
            for h in range(2):
                relay(h).wait_send()
            for kk in range(NDEV):
                to_hbm(idx ^ kk, kk).wait()

    grid_spec = pltpu.PrefetchScalarGridSpec(
        num_scalar_prefetch=1, grid=(NDEV, NT),
        in_specs=[VMEM_SPEC, pl.BlockSpec((TM, D), lambda s, i, ix: (jnp.where(s == 0, jnp.maximum(i - 1, 0), NTX - 1), 0))]
        + [VMEM_SPEC] * 12,
        out_specs=[pl.BlockSpec((TM, SH_WIN), lambda s, i, ix: (i, ix[0] ^ _gather_order(s, ix[0] & 1))), HBM_SPEC] + [VMEM_SPEC] * 10,
        scratch_shapes=[pltpu.VMEM((NDEV, D, SH_WIN), BF16), pltpu.VMEM((NT, TM, D), BF16),
                        pltpu.VMEM((NDEV, 2, DH), F32), pltpu.VMEM((NDEV, 1, DH), F32), pltpu.VMEM((NDEV, 1, D), F32),
                        pltpu.VMEM((NDEV, 2, 16, SH_ADA), F32),
                        pltpu.SemaphoreType.DMA((NDEV, PARTS)), pltpu.SemaphoreType.DMA((NDEV, PARTS)), pltpu.SemaphoreType.DMA((NDEV,)),
                        pltpu.SemaphoreType.DMA((2, 2)),
                        pltpu.SemaphoreType.DMA((4, NDEV)), pltpu.SemaphoreType.DMA((4, NDEV))])
    outs = (_sds((TT, WIN_COLS), F32), _sds((NDEV, D, SH_WIN), BF16),
            _sds((SH_ROWS, D), BF16), _sds((D, SH_PWIN), BF16), _sds((4, SH_GRP, PG), BF16), _sds((SH_ROWS, D), BF16),
            _sds((NDEV, 2, DH), F32), _sds((NDEV, 1, DH), F32), _sds((NDEV, 1, D), F32),
            _sds((3, D), F32), _sds((3, D), F32), _sds((3, D), F32))
    return pl.pallas_call(
        body, name="f1_gather_matmul", grid_spec=grid_spec, out_shape=outs,
        compiler_params=pltpu.CompilerParams(dimension_semantics=("arbitrary", "arbitrary"), vmem_limit_bytes=VMEM_LIMIT),
    )(idx1, ctx, x, nw, w_in, w_out, pw_in, pgrp, pw_out, lb_l, pscale, c, c_ctx, ada_w, ada_b)


def _gla_gates(pre, qpre, lbd, cum, rev):
    rows, n = pre.shape
    nch = rows // CHUNK
    sig = _sigmoid(pre)
    f = lbd + (1.0 - lbd) * sig
    k = 1.0 - f
    g = _dot01(cum, jnp.log(f))
    g3 = g.reshape(nch, CHUNK, n)
    last = 0 if rev else CHUNK - 1
    mid = CHUNK // 2 if rev else CHUNK // 2 - 1
    gl1, gm1 = g3[:, last:last + 1, :], g3[:, mid:mid + 1, :]

    def bc(a):
        return jnp.broadcast_to(a, g3.shape).reshape(rows, n)

    gm = bc(gm1)
    e_q, e_k = jnp.exp(g - gm), jnp.exp(gm - g)
    qsig = _sigmoid(qpre)
    qs = qpre * qsig * (DH ** -0.5)
    return dict(sig=sig, f=f, k=k, qsig=qsig, qs=qs, e_q=e_q, e_k=e_k,
                e_mid=[jnp.exp(gm1[ci]) for ci in range(nch)], e_rest=[jnp.exp(gl1[ci] - gm1[ci]) for ci in range(nch)])


def _put_heads(ref, lead, arr):
    for h in range(HEADS):
        ref[lead + (h,)] = arr[:, h * DH:(h + 1) * DH]


def _get_heads(ref, lead=()):
    return jnp.concatenate([ref[lead + (h,)] for h in range(HEADS)], axis=1)


def _gla_prep(g_all, lb, cum01, s_wout, s_pgrp):
    nch = TM // CHUNK

    def body(g_ref, lb_ref, cum_ref, swout_r, spgrp_r, p0_ref, p1_ref, v_ref, dec_ref, wout_o, pgrp_o, ssem, rsem, lsem):
        _gather_rider(pl.program_id(0), NT, NT - 1, ("rows", "grp"), (swout_r, spgrp_r), (wout_o, pgrp_o), ssem, rsem, lsem)
        qpre = g_ref[:, 3 * E:4 * E]
        _put_heads(v_ref, (), g_ref[:, 2 * E:3 * E].astype(BF16))
        for d, p_ref in ((0, p0_ref), (1, p1_ref)):
            t = _gla_gates(g_ref[:, d * E:(d + 1) * E], qpre, lb_ref[d:d + 1, :], cum_ref[d], d == 1)
            _put_heads(p_ref, (0,), (t["qs"] * t["e_q"]).astype(BF16))
            _put_heads(p_ref, (1,), (t["k"] * t["e_k"]).astype(BF16))
            for ci in range(nch):
                dec_ref[d, 0, ci:ci + 1, :] = t["e_mid"][ci]
                dec_ref[d, 0, nch + ci:nch + ci + 1, :] = t["e_rest"][ci]

    quad = pl.BlockSpec((2, HEADS, TM, DH), lambda i: (0, 0, i, 0))
    return pl.pallas_call(
        body, name="gla_prep", grid=(NT,),
        in_specs=[pl.BlockSpec((TM, 4 * E), lambda i: (i, 0)), VMEM_SPEC, VMEM_SPEC, HBM_SPEC, HBM_SPEC],
        out_specs=[quad, quad, pl.BlockSpec((HEADS, TM, DH), lambda i: (0, i, 0)), pl.BlockSpec((2, 1, 2 * nch, E), lambda i: (0, i, 0, 0)),
                   HBM_SPEC, HBM_SPEC],
        out_shape=(_sds((2, HEADS, TT, DH), BF16), _sds((2, HEADS, TT, DH), BF16), _sds((HEADS, TT, DH), BF16), _sds((2, NT, 2 * nch, E), F32),
                   _sds((E, D), BF16), _sds((4, PG, PG), BF16)),
        scratch_shapes=_gather_sems(2),
        compiler_params=pltpu.CompilerParams(dimension_semantics=("arbitrary",), vmem_limit_bytes=VMEM_LIMIT),
    )(g_all, lb, cum01, s_wout, s_pgrp)


def _scan_tile(i, rev):
    t = jnp.where(i == 0, 0, NT - i) if rev else i
    return t, pl.ds(pl.multiple_of(t * TM, TM), TM)


def _chunk_order(rev):
    n = TM // CHUNK
    return tuple(range(n - 1, -1, -1)) if rev else tuple(range(n))


def _chunk_rows(dec_ref, lanes, cis, where):
    nch = TM // CHUNK

    def rows(off):
        return jnp.stack([dec_ref[d, where[d][0], off + ci:off + ci + 1, hh * DH:(hh + 1) * DH] for (d, hh), ci in zip(lanes, cis)])

    return rows(0), rows(nch)


def _gla_fwd(p0, p1, v_all, dec, mask01, s_pwin, s_pwout):
    n_steps = HEADS // GLA_HB

    def body(p0_ref, p1_ref, v_ref, dec_ref, msk_ref, spwin_r, spwout_r, o_ref, pwin_o, pwout_o, ob_sc, ssem, rsem, lsem):
        _gather_rider(pl.program_id(0), n_steps, n_steps - 1, ("major", "rows"), (spwin_r, spwout_r), (pwin_o, pwout_o), ssem, rsem, lsem)

        lanes = [(d, hh) for d in (0, 1) for hh in range(GLA_HB)]
        nch = TM // CHUNK

        def tile_body(i, st):
            where = [_scan_tile(i, d == 1) for d in (0, 1)]

            def stacked(fn):
                return jnp.stack([fn(d, hh, where[d][1]) for d, hh in lanes])

            qg, kg = [stacked(lambda d, hh, rows, ty=ty: (p1_ref if d else p0_ref)[ty, hh, rows, :]) for ty in range(2)]
            v = stacked(lambda d, hh, rows: v_ref[hh, rows, :])
            a = _bdot_nt(qg, kg) * jnp.stack([msk_ref[d] for d, _ in lanes])
            intra = _bdot(a, v)
            outs = [[None] * nch for _ in lanes]
            for n in range(nch):
                cis = [nch - 1 - n if d else n for d, _ in lanes]

                def chunk(arr):
                    return jnp.stack([arr[l, ci * CHUNK:(ci + 1) * CHUNK] for l, ci in enumerate(cis)])

                e_mid, e_rest = _chunk_rows(dec_ref, lanes, cis, where)
                inter = _bdot_nt(chunk(qg), st * e_mid)
                for l, ci in enumerate(cis):
                    outs[l][ci] = inter[l] + intra[l, ci * CHUNK:(ci + 1) * CHUNK]
                st = st * (e_mid * e_rest) + _bdot_tn(chunk(v), chunk(kg)) * e_rest
            for l, (d, hh) in enumerate(lanes):
                (ob_sc if d else o_ref)[hh, where[d][1], :] = jnp.concatenate(outs[l], axis=0)
            return st

        lax.fori_loop(0, NT, tile_body, jnp.zeros((len(lanes), DH, DH), F32))
        o_ref[...] += ob_sc[...]

    quad = pl.BlockSpec((2, GLA_HB, TT, DH), lambda h: (0, h, 0, 0))
    head = pl.BlockSpec((GLA_HB, TT, DH), lambda h: (h, 0, 0))
    return pl.pallas_call(
        body, name="gla_fwd", grid=(n_steps,),
        in_specs=[quad, quad, head, pl.BlockSpec((2, NT, 8, GLA_HB * DH), lambda h: (0, 0, 0, h)),
                  pl.BlockSpec((2, TM, TM), lambda h: (0, 0, 0)), HBM_SPEC, HBM_SPEC],
        out_specs=[head, HBM_SPEC, HBM_SPEC],
        out_shape=(_sds((HEADS, TT, DH), F32), _sds((NDEV, D, SH_PWIN), BF16), _sds((E, D), BF16)),
        scratch_shapes=[pltpu.VMEM((GLA_HB, TT, DH), F32)] + _gather_sems(2),
        compiler_params=pltpu.CompilerParams(dimension_semantics=("arbitrary",), vmem_limit_bytes=VMEM_LIMIT),
    )(p0, p1, v_all, dec, mask01, s_pwin, s_pwout)


def _gated_norm(o, z, gw):
    r = _head_map(lambda oh: jnp.broadcast_to(_rstd(oh), oh.shape), o)
    on = o * r
    zs = _sigmoid(z)
    sz = z * zs
    return on * gw * sz, r, on, zs, sz


def _f3_out(o, g_all, x, gate, gw, wout):
    def body(o_ref, z_ref, x_ref, gate_ref, gw_ref, w_ref, x1_ref):
        og, _, _, _, _ = _gated_norm(_get_heads(o_ref), z_ref[...], gw_ref[...])
        x1_ref[...] = x_ref[...] + gate_ref[...] * _dot(og, w_ref[...])

    return pl.pallas_call(
        body, name="f3_out", grid=(NTX,),
        in_specs=[pl.BlockSpec((HEADS, TM, DH), lambda i: (0, i + 1, 0)), pl.BlockSpec((TM, E), lambda i: (i + 1, 4)),
                  pl.BlockSpec((TM, D), lambda i: (i, 0)), pl.BlockSpec((1, D), lambda i: (0, 0)),
                  pl.BlockSpec((1, E), lambda i: (0, 0)), pl.BlockSpec((E, D), lambda i: (0, 0))],
        out_specs=pl.BlockSpec((TM, D), lambda i: (i, 0)),
        out_shape=_sds((T, D), F32),
        compiler_params=pltpu.CompilerParams(dimension_semantics=("arbitrary",)),
    )(o, g_all, x, gate, gw, wout)


def _pool_layer(x1, tgt, mod1, nw1, fnw, pwin, pgrp, pscale, pwout, pb, pbt, pinv):
    def body(x_ref, t_ref, m_ref, nw_ref, fw_ref, pwin_ref, pgrp_ref, ps_ref, pwout_ref, pb_ref, pbt_ref, pinv_ref,
             dx_ref, gpwin_o, gpgrp_o, gpwout_o, dmod_o, gnw_o, gfw_o, gps_o, loss_o,
             a_pwin, a_pgrp, a_pwout):
        i = pl.program_id(0)

        @pl.when(i == 0)
        def _():
            for ref in (a_pwin, a_pgrp, a_pwout, dmod_o, gnw_o, gfw_o, gps_o, loss_o):
                ref[...] = jnp.zeros_like(ref)

        shift, scale, gate = m_ref[0:1, :], m_ref[1:2, :], m_ref[2:3, :]
        nw, fw, ps = nw_ref[...], fw_ref[...], ps_ref[...]
        x1 = x_ref[...]
        hx, r1, xn, a = _modulated(x1, nw, shift, scale)
        hxb = hx.astype(BF16)
        uz = jnp.concatenate([_dot(hxb, pwin_ref[j]) for j in range(NDEV)], axis=1)
        u, z = uz[:, :E], uz[:, E:]
        pooled, ys = [], []
        for g in range(4):
            ug = u[:, g * PG:(g + 1) * PG]
            pg = _dot01(pb_ref[g], ug) * pinv_ref[g] - ug
            pooled.append(pg.astype(BF16))
            ys.append(_dot(pooled[g], pgrp_ref[g]))
        ycat = jnp.concatenate(ys, axis=1)
        y = ycat * ps
        zs = _sigmoid(z)
        sz = z * zs
        p = (y * sz).astype(BF16)
        out = _dot(p, pwout_ref[...])
        x2 = x1 + gate * out
        r2 = _rstd(x2)
        xn2 = x2 * r2
        diff = xn2 * fw - t_ref[...]
        loss_o[...] += _colsum(diff * diff)
        dyf = diff * (1.0 / D)
        gfw_o[...] += _colsum(dyf * xn2)
        dxn2 = dyf * fw
        dx2 = r2 * (dxn2 - xn2 * jnp.mean(dxn2 * xn2, axis=-1, keepdims=True))
        dgate = _colsum(dx2 * out)
        dout = (dx2 * gate).astype(BF16)
        for j in range(4):
            cs = slice(j * PG, (j + 1) * PG)
            a_pwout[:, cs] += _dot_ta(p, dout[:, cs])
        dp = _dot_tb(dout, pwout_ref[...])
        dy = dp * sz
        dz = dp * y * (zs * (1.0 + z * (1.0 - zs)))
        gps_o[...] += _colsum(dy * ycat)
        dycat = dy * ps
        dus = []
        for g in range(4):
            dyg = dycat[:, g * PG:(g + 1) * PG].astype(BF16)
            a_pgrp[g] += _dot_ta(pooled[g], dyg)
            dpg = _dot_tb(dyg, pgrp_ref[g])
            dus.append(_dot01(pbt_ref[g], dpg * pinv_ref[g]) - dpg)
        duz = jnp.concatenate(dus + [dz], axis=1).astype(BF16)
        dhx = None
        for j in range(NDEV):
            dj = duz[:, j * SH_PWIN:(j + 1) * SH_PWIN]
            a_pwin[j] += _dot_ta(hxb, dj)
            part = _dot_tb(dj, pwin_ref[j])
            dhx = part if dhx is None else dhx + part
        dmod_o[0:1, :] += _colsum(dhx)
        dmod_o[1:2, :] += _colsum(dhx * a)
        dmod_o[2:3, :] += dgate
        da = dhx * (1.0 + scale)
        gnw_o[...] += _colsum(da * xn)
        dxn = da * nw
        dx_ref[...] = dx2 + r1 * (dxn - xn * jnp.mean(dxn * xn, axis=-1, keepdims=True))

        @pl.when(i == NTX - 1)
        def _():
            gpwin_o[...] = a_pwin[...].astype(BF16)
            gpgrp_o[...] = a_pgrp[...].astype(BF16)
            gpwout_o[...] = a_pwout[...].astype(BF16)

    tile = pl.BlockSpec((TM, D), lambda i: (i, 0))
    outs = (_sds((T, D), F32), _sds((NDEV, D, SH_PWIN), BF16), _sds((4, PG, PG), BF16), _sds((E, D), BF16),
            _sds((3, D), F32), _sds((1, D), F32), _sds((1, D), F32), _sds((1, E), F32), _sds((1, D), F32))
    return pl.pallas_call(
        body, name="pool_layer", grid=(NTX,),
        in_specs=[tile, tile] + [VMEM_SPEC] * 10,
        out_specs=[tile] + [VMEM_SPEC] * 8,
        out_shape=outs,
        scratch_shapes=[pltpu.VMEM((NDEV, D, SH_PWIN), F32), pltpu.VMEM((4, PG, PG), F32), pltpu.VMEM((E, D), F32)],
        compiler_params=pltpu.CompilerParams(dimension_semantics=("arbitrary",), vmem_limit_bytes=VMEM_LIMIT),
    )(x1, tgt, mod1, nw1, fnw, pwin, pgrp, pscale, pwout, pb, pbt, pinv)


def _b3_out_bwd(dx1, o, g_all, gate, gw, wout, gpwout):
    def body(dx_ref, o_ref, z_ref, gate_ref, gw_ref, w_ref, gpwout_r, do_ref, dz_ref, gw_o, dgate_o, ggw_o, rpwout_o,
             acc, *rider):
        i = pl.program_id(0)
        bufs, sems = _rider2_split(rider, 1)
        _scatter_rider2(i, NT, 2, ("rows",), (gpwout_r,), (rpwout_o,), bufs, sems)

        @pl.when(i == 0)
        def _():
            acc[...] = jnp.zeros_like(acc)
            dgate_o[...] = jnp.zeros_like(dgate_o)
            ggw_o[...] = jnp.zeros_like(ggw_o)
            do_ref[...] = jnp.zeros_like(do_ref)
            dz_ref[...] = jnp.zeros_like(dz_ref)

        @pl.when(i > 0)
        def _():
            gw = gw_ref[...]
            z = z_ref[...]
            og, r, on, zs, sz = _gated_norm(_get_heads(o_ref), z, gw)
            ogb = og.astype(BF16)
            dx = dx_ref[...]
            dgate_o[...] += _colsum(dx * _dot(ogb, w_ref[...]))
            dy = (dx * gate_ref[...]).astype(BF16)
            for j in range(4):
                cs = slice(j * PG, (j + 1) * PG)
                acc[:, cs] += _dot_ta(ogb, dy[:, cs])
            dog = _dot_tb(dy, w_ref[...])
            dz_ref[...] = (dog * (on * gw) * (zs * (1.0 + z * (1.0 - zs)))).astype(BF16)
            dong = dog * sz
            ggw_o[...] += _colsum(dong * on)
            don = dong * gw
            do = _head_map(lambda dh, nh, rh: rh * (dh - nh * jnp.mean(dh * nh, axis=-1, keepdims=True)), don, on, r)
            _put_heads(do_ref, (), do.astype(BF16))

        @pl.when(i == NT - 1)
        def _():
            gw_o[...] = acc[...].astype(BF16)

    prev = lambda i: (jnp.maximum(i - 1, 0), 0)
    heads = pl.BlockSpec((HEADS, TM, DH), lambda i: (0, i, 0))
    return pl.pallas_call(
        body, name="b3_out_bwd", grid=(NT,),
        in_specs=[pl.BlockSpec((TM, D), prev), heads, pl.BlockSpec((TM, E), lambda i: (i, 4)),
                  VMEM_SPEC, VMEM_SPEC, VMEM_SPEC, HBM_SPEC],
        out_specs=[heads, pl.BlockSpec((TM, E), lambda i: (i, 0)), VMEM_SPEC, VMEM_SPEC, VMEM_SPEC, HBM_SPEC],
        out_shape=(_sds((HEADS, TT, DH), BF16), _sds((TT, E), BF16), _sds((E, D), BF16), _sds((1, D), F32), _sds((1, E), F32),
                   _sds((RS_SLOTS, SH_ROWS, D), BF16)),
        scratch_shapes=[pltpu.VMEM((E, D), F32)] + _rider2_scratch([(SH_ROWS, D)]),
        compiler_params=pltpu.CompilerParams(dimension_semantics=("arbitrary",), vmem_limit_bytes=VMEM_LIMIT),
    )(dx1, o, g_all, gate, gw, wout, gpwout)


def _gla_bwd(p0, p1, v_all, dec, do, mask01, gpwin, gpgrp):
    nch = TM // CHUNK
    n_steps = HEADS // GLA_HB

    def body(p0_ref, p1_ref, v_ref, dec_ref, do_ref, msk_ref, gpwin_r, gpgrp_r, d0_ref, d1_ref, dv_ref, dgl_ref, rpwin_o, rpgrp_o,
             ss_sc, dv_sc, ssem, rsem, lsem, *rider):
        _scatter_rider(pl.program_id(0), n_steps, ("grp",), (gpgrp_r,), (rpgrp_o,), ssem, rsem, lsem)
        bufs, sems = _rider2_split(rider, 1)
        _scatter_rider2(pl.program_id(0), n_steps, 1, ("major",), (gpwin_r,), (rpwin_o,), bufs, sems)

        lanes = [(d, hh) for d in (0, 1) for hh in range(GLA_HB)]
        zero = jnp.zeros((len(lanes), DH, DH), F32)
        dgl_ref[...] = jnp.zeros_like(dgl_ref)

        def p_of(d):
            return p1_ref if d else p0_ref

        def scan_step(i, n):
            where = [_scan_tile(i, d == 1) for d in (0, 1)]
            cis = [nch - 1 - n if d else n for d, _ in lanes]
            e_mid, e_rest = _chunk_rows(dec_ref, lanes, cis, where)

            def chunk(arr):
                return jnp.stack([arr[l, ci * CHUNK:(ci + 1) * CHUNK] for l, ci in enumerate(cis)])

            return where, cis, e_mid, e_rest, chunk

        def stacked(i, fn):
            where = [_scan_tile(i, d == 1) for d in (0, 1)]
            return jnp.stack([fn(d, hh, where[d][1]) for d, hh in lanes])

        def fwd_body(i, st):
            v = stacked(i, lambda d, hh, rows: v_ref[hh, rows, :])
            kg = stacked(i, lambda d, hh, rows: p_of(d)[1, hh, rows, :])
            for n in range(nch):
                _, _, e_mid, e_rest, chunk = scan_step(i, n)
                ss_sc[i * nch + n] = st
                st = st * (e_mid * e_rest) + _bdot_tn(chunk(v), chunk(kg)) * e_rest
            return st

        ss_sc[NT * nch] = lax.fori_loop(0, NT, fwd_body, zero)

        def bwd_body(ii, dst):
            i = NT - 1 - ii
            qg, kg = [stacked(i, lambda d, hh, rows, ty=ty: p_of(d)[ty, hh, rows, :]) for ty in range(2)]
            v = stacked(i, lambda d, hh, rows: v_ref[hh, rows, :])
            dob = stacked(i, lambda d, hh, rows: do_ref[hh, rows, :])
            msk = jnp.stack([msk_ref[d] for d, _ in lanes])
            a = (_bdot_nt(qg, kg) * msk).astype(BF16)
            da = (_bdot_nt(dob, v) * msk).astype(BF16)
            dqg = _bdot(da, kg)
            dkg = _bdot_tn(da, qg)
            dv_intra = _bdot_tn(a, dob)
            dv_l, dkg_l, dqg_l = ([[None] * nch for _ in lanes] for _ in range(3))
            for n in range(nch - 1, -1, -1):
                where, cis, e_mid, e_rest, chunk = scan_step(i, n)
                s_c, s_end = ss_sc[i * nch + n], ss_sc[i * nch + n + 1]
                dste = (dst * e_rest).astype(BF16)
                kg_c, v_c, dob_c = chunk(kg), chunk(v), chunk(dob)
                dv_c = chunk(dv_intra) + _bdot_nt(kg_c, dste)
                dkg_c = chunk(dkg) + _bdot(v_c, dste)
                dqg_c = chunk(dqg) + _bdot(dob_c, s_c * e_mid)
                dgl = jnp.sum(s_end * dst, axis=1, keepdims=True)
                for l, ((d, hh), ci) in enumerate(zip(lanes, cis)):
                    dv_l[l][ci], dkg_l[l][ci], dqg_l[l][ci] = dv_c[l], dkg_c[l], dqg_c[l]
                    dgl_ref[d, where[d][0], ci:ci + 1, hh * DH:(hh + 1) * DH] = dgl[l]
                dst = dst * (e_mid * e_rest) + _bdot_tn(dob_c, chunk(qg)) * e_mid
            where = [_scan_tile(i, d == 1) for d in (0, 1)]
            for l, (d, hh) in enumerate(lanes):
                rows = where[d][1]
                d_ref = d1_ref if d else d0_ref
                d_ref[0, hh, rows, :] = jnp.concatenate(dqg_l[l], axis=0).astype(BF16)
                d_ref[1, hh, rows, :] = jnp.concatenate(dkg_l[l], axis=0).astype(BF16)
                dv_sc[d, hh, rows, :] = jnp.concatenate(dv_l[l], axis=0).astype(BF16)
            return dst

        lax.fori_loop(0, NT, bwd_body, zero)
        dv_ref[...] = (dv_sc[0].astype(F32) + dv_sc[1].astype(F32)).astype(BF16)

    quad = pl.BlockSpec((2, GLA_HB, TT, DH), lambda h: (0, h, 0, 0))
    col = pl.BlockSpec((GLA_HB, TT, DH), lambda h: (h, 0, 0))
    chunkv = pl.BlockSpec((2, NT, 8, GLA_HB * DH), lambda h: (0, 0, 0, h))
    outs = (_sds((2, HEADS, TT, DH), BF16), _sds((2, HEADS, TT, DH), BF16), _sds((HEADS, TT, DH), BF16), _sds((2, NT, 8, E), F32),
            _sds((RS_SLOTS, D, SH_PWIN), BF16), _sds((NDEV, 4, SH_GRP, PG), BF16))
    return pl.pallas_call(
        body, name="gla_bwd", grid=(n_steps,),
        in_specs=[quad, quad, col, chunkv, col, pl.BlockSpec((2, TM, TM), lambda h: (0, 0, 0)), HBM_SPEC, HBM_SPEC],
        out_specs=[quad, quad, col, chunkv, HBM_SPEC, HBM_SPEC],
        out_shape=outs,
        scratch_shapes=[pltpu.VMEM((NT * nch + 1, 2 * GLA_HB, DH, DH), F32), pltpu.VMEM((2, GLA_HB, TT, DH), BF16)] + _rider_sems(1)
        + _rider2_scratch([(D, SH_PWIN)]),
        compiler_params=pltpu.CompilerParams(dimension_semantics=("arbitrary",), vmem_limit_bytes=VMEM_LIMIT),
    )(p0, p1, v_all, dec, do, mask01, gpwin, gpgrp)


TMB = 128


def _gla_post_bwd(g_all, d0, d1, dgl, dv, dz, lb, cum01, gwout):
    nch = TMB // CHUNK

    def body(g_ref, d0_ref, d1_ref, dgl_ref, dv_ref, dz_ref, lb_ref, cum_ref, gwout_r, dg_ref, dlb_ref, rwout_o, *rider):
        i = pl.program_id(0)
        bufs, sems = _rider2_split(rider, 1)
        _scatter_rider2(i, TT // TMB, 2, ("rows",), (gwout_r,), (rwout_o,), bufs, sems)

        @pl.when(i == 0)
        def _():
            dlb_ref[...] = jnp.zeros_like(dlb_ref)

        half = i & 1
        qpre = g_ref[:, 3 * E:4 * E]
        dqs_sum = None
        dpre = []
        for d, d_ref in ((0, d0_ref), (1, d1_ref)):
            rev = d == 1
            lbd = lb_ref[d:d + 1, :]
            t = _gla_gates(g_ref[:, d * E:(d + 1) * E], qpre, lbd, cum_ref[d, :TMB, :TMB], rev)
            dqs = _get_heads(d_ref, (0,)).astype(F32) * t["e_q"]
            dk = _get_heads(d_ref, (1,)).astype(F32) * t["e_k"]
            dg = t["qs"] * dqs - t["k"] * dk
            dgl8 = dgl_ref[d, 0]
            dgl_rows = [jnp.where(half == 0, dgl8[ci:ci + 1, :], dgl8[nch + ci:nch + ci + 1, :]) for ci in range(nch)]
            dgl_b = jnp.concatenate([jnp.broadcast_to(dgl_rows[ci], (CHUNK, E)) for ci in range(nch)], axis=0)
            pos = lax.broadcasted_iota(jnp.int32, (TMB, E), 0) & (CHUNK - 1)
            dg = dg + jnp.where(pos == (0 if rev else CHUNK - 1), dgl_b, 0.0)
            dlf = _dot01(cum_ref[1 - d, :TMB, :TMB], dg)
            df = dlf / t["f"] - dk
            sig = t["sig"]
            dpre.append((df * (1.0 - lbd) * sig * (1.0 - sig)).astype(BF16))
            dlb_ref[d:d + 1, :] += _colsum(df * (1.0 - sig))
            dqs_sum = dqs if dqs_sum is None else dqs_sum + dqs
            qsig = t["qsig"]
        dqpre = dqs_sum * (DH ** -0.5) * (qsig * (1.0 + qpre * (1.0 - qsig)))
        dg_ref[...] = jnp.concatenate([dpre[0], dpre[1], _get_heads(dv_ref), dqpre.astype(BF16), dz_ref[...]], axis=1)

    quad = pl.BlockSpec((2, HEADS, TMB, DH), lambda i: (0, 0, i, 0))
    tile = pl.BlockSpec((TMB, E), lambda i: (i, 0))
    return pl.pallas_call(
        body, name="gla_post_bwd", grid=(TT // TMB,),
        in_specs=[pl.BlockSpec((TMB, 4 * E), lambda i: (i, 0)), quad, quad,
                  pl.BlockSpec((2, 1, 8, E), lambda i: (0, i // 2, 0, 0)), pl.BlockSpec((HEADS, TMB, DH), lambda i: (0, i, 0)), tile,
                  VMEM_SPEC, VMEM_SPEC, HBM_SPEC],
        out_specs=[pl.BlockSpec((TMB, WIN_COLS), lambda i: (i, 0)), VMEM_SPEC, HBM_SPEC],
        out_shape=(_sds((TT, WIN_COLS), BF16), _sds((2, E), F32), _sds((RS_SLOTS, SH_ROWS, D), BF16)),
        scratch_shapes=_rider2_scratch([(SH_ROWS, D)]),
        compiler_params=pltpu.CompilerParams(dimension_semantics=("arbitrary",), vmem_limit_bytes=VMEM_LIMIT),
    )(g_all, d0, d1, dgl, dv, dz, lb, cum01, gwout)


WIN_SLOTS = 4


def _scatter_order(s, core):
    return (NDEV - 1 - s) ^ jnp.where((s >= 2) & (s <= 5) & ((s & 1) == core), 6, 0)


def _b1_in_bwd(idx1, ctx, x, dx1, dg, nw, msel, win):
    last_s = NDEV - 1
    half = D // 2
    prows = D // PARTS

    def body(idx_ref, ctx_ref, x_ref, dx1_ref, dg_ref, nw_ref, m_ref, w_ref, gx_ref, rwin_o, dmx_o, dmc_o, gnw_o,
             hx_sc, dhx_sc, acc, sbuf, pbuf, rbuf, psend, precv, isend, irecv, dsend, drecv, sibsem, lsem):
        del idx_ref
        s, i = pl.program_id(0), pl.program_id(1)
        x, y, cc, idx = _mesh_pos()
        shift, scale = m_ref[0, 0:1, :], m_ref[0, 1:2, :]
        sibling = (x, y, 1 - cc)

        def partial(p):
            return pltpu.make_async_remote_copy(src_ref=sbuf.at[0], dst_ref=pbuf.at[p], send_sem=psend.at[p], recv_sem=precv.at[p],
                                                device_id=sibling, device_id_type=MESH)

        def chip_sum(p):
            return _Copies([pltpu.make_async_remote_copy(src_ref=sbuf.at[1, pl.ds(q * prows, prows), :],
                                                         dst_ref=rwin_o.at[2 + p, pl.ds(q * prows, prows), :], send_sem=isend.at[p, q],
                                                         recv_sem=irecv.at[p, q], device_id=_peer(x, y, cc, 2 * (p + 1)), device_id_type=MESH)
                            for q in range(PARTS)])

        def relay(h):
            return pltpu.make_async_remote_copy(src_ref=sbuf.at[1, pl.ds(h * half, half), :], dst_ref=rbuf.at[h], send_sem=dsend.at[h],
                                                recv_sem=drecv.at[h], device_id=_peer(x, y, cc, 2 * (h + 1)), device_id_type=MESH)

        to_sibling = pltpu.make_async_remote_copy(src_ref=sbuf.at[0], dst_ref=rwin_o.at[1], send_sem=sibsem.at[0], recv_sem=sibsem.at[1],
                                                  device_id=sibling, device_id_type=MESH)
        own = pltpu.make_async_copy(sbuf.at[1], rwin_o.at[0], lsem)

        @pl.when((s == 0) & (i == 0))
        def _():
            for ref in (dmx_o, dmc_o, gnw_o):
                ref[...] = jnp.zeros_like(ref)

        @pl.when(s == 0)
        def _():
            hx, _, _, _ = _modulated(_ctx_or_x(i, ctx_ref, x_ref), nw_ref[...], shift, scale)
            hx_sc[i] = hx.astype(BF16)

        @pl.when(i == 0)
        def _():
            acc[...] = jnp.zeros_like(acc)

        dgb = dg_ref[...]
        hxb = hx_sc[i]
        for lo, hi in ((0, 256), (256, 512), (512, SH_WIN)):
            acc[:, lo:hi] += _dot_ta(hxb, dgb[:, lo:hi])
        part = _dot_tb(dgb, w_ref[0])

        @pl.when(s == 0)
        def _():
            dhx_sc[i] = part

        @pl.when(s > 0)
        def _():
            dhx_sc[i] += part

        done = i == NT - 1

        def hand_over(p, before):
            before.wait_send()
            sbuf[0] = acc[...].astype(BF16)
            partial(p).start()

        def send_chip_sum(p, before):
            for cp in before:
                cp.wait_send()
            partial(p).wait_recv()
            sbuf[1] = (acc[...] + pbuf[p].astype(F32)).astype(BF16)
            h = 1 - p
            rows = pl.ds(h * half, half)
            relay(h).wait_recv()
            sbuf[1, rows, :] = (acc[rows, :] + pbuf[p, rows, :].astype(F32) + rbuf[h].astype(F32)).astype(BF16)
            chip_sum(p).start()

        @pl.when(done & (s == 0))
        def _():
            sbuf[0] = acc[...].astype(BF16)
            partial(2).start()

        @pl.when(done & (s == 1))
        def _():
            partial(2).wait_recv()
            sbuf[1] = (acc[...] + pbuf[2].astype(F32)).astype(BF16)
            for h in range(2):
                relay(h).start()

        for core in range(2):
            @pl.when(done & (cc == core) & (s == 2))
            def _(core=core):
                hand_over(core, partial(2))

            @pl.when(done & (cc == core) & (s == 3))
            def _(core=core):
                send_chip_sum(1 - core, [relay(0), relay(1)])

            @pl.when(done & (cc == core) & (s == 4))
            def _(core=core):
                hand_over(1 - core, partial(core))

            @pl.when(done & (cc == core) & (s == 5))
            def _(core=core):
                send_chip_sum(core, [chip_sum(1 - core)])

            @pl.when(done & (cc == core) & (s == last_s - 1))
            def _(core=core):
                partial(1 - core).wait_send()
                sbuf[0] = acc[...].astype(BF16)
                to_sibling.start()

            @pl.when(done & (cc == core) & (s == last_s))
            def _(core=core):
                chip_sum(core).wait_send()
                sbuf[1] = acc[...].astype(BF16)
                own.start()

        @pl.when(s == last_s)
        def _():
            nw = nw_ref[...]
            _, r, xn, a = _modulated(_ctx_or_x(i, ctx_ref, x_ref), nw, shift, scale)
            dhx = dhx_sc[i]
            dsh, dsc = _colsum(dhx), _colsum(dhx * a)
            da = dhx * (1.0 + scale)
            gnw_o[...] += _colsum(da * xn)
            dxn = da * nw
            gx_ref[...] = dx1_ref[...] + r * (dxn - xn * jnp.mean(dxn * xn, axis=-1, keepdims=True))

            @pl.when(i == 0)
            def _():
                dmc_o[0:1, :] += dsh
                dmc_o[1:2, :] += dsc

            @pl.when(i > 0)
            def _():
                dmx_o[0:1, :] += dsh
                dmx_o[1:2, :] += dsc

        @pl.when((i == NT - 1) & (s == last_s))
        def _():
            to_sibling.wait_send()
            to_sibling.wait_recv()
            for p in range(2):
                chip_sum(p).wait_recv()
            own.wait()

    grid_spec = pltpu.PrefetchScalarGridSpec(
        num_scalar_prefetch=1, grid=(NDEV, NT),
        in_specs=[VMEM_SPEC,
                  pl.BlockSpec((TM, D), lambda s, i, ix: (jnp.where((s == 0) | (s == last_s), jnp.maximum(i - 1, 0), NTX - 1), 0)),
                  pl.BlockSpec((TM, D), lambda s, i, ix: (jnp.where(s == last_s, jnp.maximum(i - 1, 0), 0), 0)),
                  pl.BlockSpec((TM, SH_WIN), lambda s, i, ix: (i, ix[0] ^ _scatter_order(s, ix[0] & 1))), VMEM_SPEC,
                  pl.BlockSpec((1, 2, D), lambda s, i, ix: (jnp.minimum(i, 1), 0, 0)),
                  pl.BlockSpec((1, D, SH_WIN), lambda s, i, ix: (ix[0] ^ _scatter_order(s, ix[0] & 1), 0, 0))],
        out_specs=[pl.BlockSpec((TM, D), lambda s, i, ix: (jnp.where(s == last_s, jnp.maximum(i - 1, 0), 0), 0)),
                   HBM_SPEC, VMEM_SPEC, VMEM_SPEC, VMEM_SPEC],
        scratch_shapes=[pltpu.VMEM((NT, TM, D), BF16), pltpu.VMEM((NT, TM, D), F32), pltpu.VMEM((D, SH_WIN), F32),
                        pltpu.VMEM((2, D, SH_WIN), BF16), pltpu.VMEM((3, D, SH_WIN), BF16), pltpu.VMEM((2, half, SH_WIN), BF16),
                        pltpu.SemaphoreType.DMA((3,)), pltpu.SemaphoreType.DMA((3,)), pltpu.SemaphoreType.DMA((2, PARTS)),
                        pltpu.SemaphoreType.DMA((2, PARTS)), pltpu.SemaphoreType.DMA((2,)), pltpu.SemaphoreType.DMA((2,)),
                        pltpu.SemaphoreType.DMA((2,)), pltpu.SemaphoreType.DMA])
    return pl.pallas_call(
        body, name="b1_in_bwd", grid_spec=grid_spec,
        out_shape=(_sds((T, D), F32), _sds((WIN_SLOTS, D, SH_WIN), BF16), _sds((2, D), F32), _sds((2, D), F32), _sds((1, D), F32)),
        compiler_params=pltpu.CompilerParams(dimension_semantics=("arbitrary", "arbitrary"), vmem_limit_bytes=VMEM_LIMIT),
    )(idx1, ctx, x, dx1, dg, nw, msel, win)


def _reduce_small(pd, pv, cg, c_ctx, ada_w0):
    n_arr = 3

    def body(pd_r, pv_r, cg_r, cctx_r, ada_r, gada_o, gadab_o, gcctx_o, pvsum_o, loss_o,
             pd_all, pv_all, dsc_all, dsc_mine, ssem, rsem):
        x, y, cc, idx = _mesh_pos()
        srcs = [pd_r, pv_r, dsc_mine]
        dsts = [pd_all.at[idx], pv_all.at[idx], dsc_all.at[idx]]

        def remote(a, k):
            return pltpu.make_async_remote_copy(src_ref=srcs[a], dst_ref=dsts[a], send_sem=ssem.at[a, k], recv_sem=rsem.at[a, k],
                                                device_id=_peer(x, y, cc, k), device_id_type=MESH)

        first = [remote(a, k) for k in range(1, NDEV) for a in (0, 1)]
        for cp in first:
            cp.start()
        pd_all[idx] = pd_r[...]
        pv_all[idx] = pv_r[...]
        for k in range(1, NDEV):
            remote(0, k).wait_recv()
            remote(1, k).wait_recv()
        mine = [pd_all[s, :, pl.ds(idx, 1), :] for s in range(NDEV)]
        dmc = functools.reduce(lambda u, v: u + v, [m[2] for m in mine])
        rows = _stack_rows([cg_r[i] for i in range(NDEV)] + [cctx_r[...]])
        sc = (rows * _sigmoid(rows)).astype(BF16)
        gada_o[0] = _dot_ta(sc, _stack_rows([m[0] for m in mine] + [dmc]))
        gada_o[1] = _dot_ta(sc, _stack_rows([m[1] for m in mine]))
        dsc_mine[...] = _dot_tb(jnp.broadcast_to(dmc, (8, SH_ADA)), ada_r[...])[0:1, :]
        dsc_all[idx] = dsc_mine[...]
        second = [remote(2, k) for k in range(1, NDEV)]
        for cp in second:
            cp.start()
        tot = [functools.reduce(lambda u, v: u + v, [pd_all[s, l] for s in range(NDEV)]) for l in range(3)]
        gadab_o[0] = tot[0] + tot[2]
        gadab_o[1] = tot[1]
        pvs = functools.reduce(lambda u, v: u + v, [pv_all[s] for s in range(NDEV)])
        pvsum_o[...] = pvs
        loss_o[...] = jnp.broadcast_to(jnp.sum(pvs[:, PV_LOSS:PV_LOSS + D], axis=-1, keepdims=True) * (0.5 / D), (1, 128))
        for k in range(1, NDEV):
            remote(2, k).wait_recv()
        dsc = functools.reduce(lambda u, v: u + v, [dsc_all[s] for s in range(NDEV)])
        cx = cctx_r[...]
        sx = _sigmoid(cx)
        gcctx_o[...] = dsc * (sx * (1.0 + cx * (1.0 - sx)))
        for cp in first + second:
            cp.wait_send()

    outs = (_sds((2, D, SH_ADA), F32), _sds((2, NDEV, SH_ADA), F32), _sds((1, D), F32), _sds((1, PV_LEN), F32), _sds((1, 128), F32))
    return pl.pallas_call(
        body, name="reduce_small", out_shape=outs,
        in_specs=[VMEM_SPEC] * 5, out_specs=[VMEM_SPEC] * 5,
        scratch_shapes=[
            pltpu.VMEM((NDEV, 3, NDEV, SH_ADA), F32), pltpu.VMEM((NDEV, 1, PV_LEN), F32), pltpu.VMEM((NDEV, 1, D), F32),
            pltpu.VMEM((1, D), F32),
            pltpu.SemaphoreType.DMA((n_arr, NDEV)), pltpu.SemaphoreType.DMA((n_arr, NDEV)),
        ],
        compiler_params=pltpu.CompilerParams(vmem_limit_bytes=VMEM_LIMIT),
    )(pd, pv, cg, c_ctx, ada_w0)


PV_NW, PV_GNORM, PV_FINAL, PV_LB, PV_PSCALE, PV_LOSS, PV_LEN = 0, 2 * D, 3 * D, 4 * D, 6 * D, 7 * D, 8 * D


def _adamw(w, g, m, v):
    m = ADAM_B1 * m + (1.0 - ADAM_B1) * g
    v = ADAM_B2 * v + (1.0 - ADAM_B2) * (g * g)
    m_hat = m / (1.0 - ADAM_B1 ** ADAM_STEP)
    v_hat = v / (1.0 - ADAM_B2 ** ADAM_STEP)
    delta = -ADAM_LR * (m_hat / (jnp.sqrt(v_hat) + ADAM_EPS) + ADAM_WD * w)
    return delta, m, v


ADAM_STEPS = 8


def _adam_all(sharded, dense, small, lb_idx, lbv):
    ns, nd, nsm = len(sharded), len(dense), len(small)

    def body(*refs):
        it = iter(refs)
        sh_in = [[next(it) for _ in range(4)] for _ in range(ns)]
        de_in = [[next(it) for _ in range(4)] for _ in range(nd)]
        sm_in = [[next(it) for _ in range(4)] for _ in range(nsm)]
        lb_r = next(it)
        sh_out = [[next(it) for _ in range(4)] for _ in range(ns)]
        de_out = [[next(it) for _ in range(3)] for _ in range(nd)]
        sm_out = [[next(it) for _ in range(4)] for _ in range(nsm)]
        for (p, w, m, v), outs in zip(sh_in, sh_out):
            g = p[0].astype(F32)
            for s in range(1, p.shape[0]):
                g = g + p[s].astype(F32)
            d, mn, vn = _adamw(w[...], g, m[...], v[...])
            outs[0][...], outs[1][...], outs[2][...], outs[3][...] = g, d, mn, vn
        for (g, w, m, v), outs in zip(de_in, de_out):
            d, mn, vn = _adamw(w[...], g[...], m[...], v[...])
            outs[0][...], outs[1][...], outs[2][...] = d, mn, vn

        @pl.when(pl.program_id(0) == 0)
        def _():
            for j, ((g, w, m, v), outs) in enumerate(zip(sm_in, sm_out)):
                gj = g[...]
                if j == lb_idx:
                    gj = gj * lb_r[...] * (1.0 - lb_r[...])
                d, mn, vn = _adamw(w[...], gj, m[...], v[...])
                outs[0][...], outs[1][...], outs[2][...], outs[3][...] = gj, d, mn, vn

    def tile(a):
        return pl.BlockSpec((a.shape[0] // ADAM_STEPS, a.shape[1]), lambda i: (i, 0))

    in_specs, out_specs, out_shape, args = [], [], [], []
    for p, w, m, v in sharded:
        in_specs += [pl.BlockSpec((p.shape[0], p.shape[1] // ADAM_STEPS, p.shape[2]), lambda i: (0, i, 0))] + [tile(w)] * 3
        args += [p, w, m, v]
    for g, w, m, v in dense:
        in_specs += [tile(w)] * 4
        args += [g, w, m, v]
    for g, w, m, v in small:
        in_specs += [VMEM_SPEC] * 4
        args += [g, w, m, v]
    in_specs.append(VMEM_SPEC)
    args.append(lbv)
    for _, w, _, _ in sharded:
        out_specs += [tile(w)] * 4
        out_shape += [_sds(w.shape, F32)] * 4
    for _, w, _, _ in dense:
        out_specs += [tile(w)] * 3
        out_shape += [_sds(w.shape, F32)] * 3
    for _, w, _, _ in small:
        out_specs += [VMEM_SPEC] * 4
        out_shape += [_sds(w.shape, F32)] * 4
    res = pl.pallas_call(body, name="adam_all", grid=(ADAM_STEPS,), in_specs=in_specs, out_specs=out_specs, out_shape=tuple(out_shape),
                         compiler_params=pltpu.CompilerParams(dimension_semantics=("arbitrary",), vmem_limit_bytes=VMEM_LIMIT))(*args)
    it = iter(res)
    return ([tuple(next(it) for _ in range(4)) for _ in range(ns)], [tuple(next(it) for _ in range(3)) for _ in range(nd)],
            [tuple(next(it) for _ in range(4)) for _ in range(nsm)])


def kernel(x, c, ctx, c_ctx, ada_w, ada_b, norm_w, hgrn_w_in, hgrn_lb_logits, hgrn_gnorm_w, hgrn_w_out, pool_w_in, pool_w_grp, pool_scale, pool_w_out, final_norm_w, loss_target, m_c_ctx, m_ada_w, m_ada_b, m_norm_w, m_hgrn_w_in, m_hgrn_lb_logits, m_hgrn_gnorm_w, m_hgrn_w_out, m_pool_w_in, m_pool_w_grp, m_pool_scale, m_pool_w_out, m_final_norm_w, v_c_ctx, v_ada_w, v_ada_b, v_norm_w, v_hgrn_w_in, v_hgrn_lb_logits, v_hgrn_gnorm_w, v_hgrn_w_out, v_pool_w_in, v_pool_w_grp, v_pool_scale, v_pool_w_out, v_final_norm_w):
    idx = 4 * lax.axis_index("x") + 2 * lax.axis_index("y") + lax.axis_index("c")
    cctx2 = c_ctx.reshape(1, D)
    cum01, mask01 = _gla_consts()
    pb, pbt, pinv = _pool_consts()

    idx1 = idx.reshape(1).astype(jnp.int32)
    nw0, nw1 = norm_w[0:1], norm_w[1:2]
    fnw = final_norm_w.reshape(1, D)
    g_all, win, s_wout, s_pwin, s_pgrp, s_pwout, lbl_g, ps_g, cg, mod0, mod1, modc = _f1_gather_matmul(
        idx1, ctx[0], x[0], nw0, hgrn_w_in[0], hgrn_w_out[0], pool_w_in[0], pool_w_grp[0], pool_w_out[0], hgrn_lb_logits[0],
        pool_scale, c, cctx2, ada_w, ada_b)
    lb = jax.nn.sigmoid(jnp.transpose(lbl_g, (1, 0, 2)).reshape(2, E))
    pscale = ps_g.reshape(1, E)
    msel = jnp.stack([modc[:2], mod0[:2]])
    p0, p1, v_all, dec, wout, pgrp = _gla_prep(g_all, lb, cum01, s_wout, s_pgrp)
    o, pwin, pwout = _gla_fwd(p0, p1, v_all, dec, mask01, s_pwin, s_pwout)
    x1 = _f3_out(o, g_all, x[0], mod0[2:3], hgrn_gnorm_w, wout)
    dx1, gpwin, gpgrp, gpwout, dmod1, gnw1, gfw, gps, lossv = _pool_layer(
        x1, loss_target[0], mod1, nw1, fnw, pwin, pgrp, pscale, pwout, pb, pbt, pinv)
    do, dz, gwout, dgate0, ggw, rpwout = _b3_out_bwd(dx1, o, g_all, mod0[2:3], hgrn_gnorm_w, wout, gpwout)
    d0, d1, dv, dgl, rpwin, rpgrp = _gla_bwd(p0, p1, v_all, dec, do, mask01, gpwin, gpgrp)
    dg, dlb, rwout = _gla_post_bwd(g_all, d0, d1, dgl, dv, dz, lb, cum01, gwout)
    grad_x, rwin, dmx, dmc, gnw0 = _b1_in_bwd(idx1, ctx[0], x[0], dx1, dg, nw0, msel, win)

    dmod0 = jnp.concatenate([dmx, dgate0], axis=0)
    dmodc = jnp.concatenate([dmc, jnp.zeros((1, D), F32)], axis=0)
    pd = jnp.stack([dmod0, dmod1, dmodc]).reshape(3, NDEV, SH_ADA)
    pv = jnp.concatenate([gnw0, gnw1, ggw, gfw, dlb.reshape(1, 2 * E), gps, lossv], axis=1)
    g_ada, g_adab, g_cctx, pvsum, loss128 = _reduce_small(pd, pv, cg, cctx2, ada_w[0])

    g2 = (4 * SH_GRP, PG)
    sharded_names = ["hgrn_w_in", "hgrn_w_out", "pool_w_in", "pool_w_grp", "pool_w_out"]
    sharded = [(rwin, hgrn_w_in[0], m_hgrn_w_in[0], v_hgrn_w_in[0]),
               (rwout, hgrn_w_out[0], m_hgrn_w_out[0], v_hgrn_w_out[0]),
               (rpwin, pool_w_in[0], m_pool_w_in[0], v_pool_w_in[0]),
               (rpgrp.reshape((NDEV,) + g2), pool_w_grp[0].reshape(g2), m_pool_w_grp[0].reshape(g2), v_pool_w_grp[0].reshape(g2)),
               (rpwout, pool_w_out[0], m_pool_w_out[0], v_pool_w_out[0])]
    a2 = (2 * D, SH_ADA)
    g_ada2 = g_ada.reshape(a2)
    dense = [(g_ada2, ada_w.reshape(a2), m_ada_w.reshape(a2), v_ada_w.reshape(a2))]
    lb_me = lax.dynamic_slice_in_dim(lb, idx * DH, DH, axis=1)
    small_names = ["c_ctx", "ada_b", "norm_w", "hgrn_lb_logits", "hgrn_gnorm_w", "pool_scale", "final_norm_w"]
    small = [(g_cctx, cctx2, m_c_ctx.reshape(1, D), v_c_ctx.reshape(1, D)),
             (g_adab.reshape(2, 3 * D), ada_b, m_ada_b, v_ada_b),
             (pvsum[:, PV_NW:PV_NW + 2 * D].reshape(2, D), norm_w, m_norm_w, v_norm_w),
             (lax.dynamic_slice_in_dim(pvsum[:, PV_LB:PV_LB + 2 * E].reshape(2, E), idx * DH, DH, axis=1),
              hgrn_lb_logits[0], m_hgrn_lb_logits[0], v_hgrn_lb_logits[0]),
             (pvsum[:, PV_GNORM:PV_GNORM + E], hgrn_gnorm_w, m_hgrn_gnorm_w, v_hgrn_gnorm_w),
             (lax.dynamic_slice_in_dim(pvsum[:, PV_PSCALE:PV_PSCALE + E], idx * DH, DH, axis=1), pool_scale, m_pool_scale, v_pool_scale),
             (pvsum[:, PV_FINAL:PV_FINAL + D], fnw, m_final_norm_w.reshape(1, D), v_final_norm_w.reshape(1, D))]
    r_sharded, r_dense, r_small = _adam_all(sharded, dense, small, 3, lb_me)
    out = dict(zip(sharded_names, r_sharded))
    out["ada_w"] = (g_ada2,) + r_dense[0]
    out.update(zip(small_names, r_small))

    shapes = {"c_ctx": (D,), "ada_w": (2, D, SH_ADA), "ada_b": (2, 3 * D), "norm_w": (2, D), "hgrn_w_in": (1, D, SH_WIN),
              "hgrn_lb_logits": (1, 2, DH), "hgrn_gnorm_w": (1, E), "hgrn_w_out": (1, SH_ROWS, D), "pool_w_in": (1, D, SH_PWIN),
              "pool_w_grp": (1, 4, SH_GRP, PG), "pool_scale": (1, DH), "pool_w_out": (1, SH_ROWS, D), "final_norm_w": (D,)}
    order = ["c_ctx", "ada_w", "ada_b", "norm_w", "hgrn_w_in", "hgrn_lb_logits", "hgrn_gnorm_w", "hgrn_w_out", "pool_w_in",
             "pool_w_grp", "pool_scale", "pool_w_out", "final_norm_w"]
    flat = [out[name][q].reshape(shapes[name]) for q in range(4) for name in order]
    return (loss128[0, 0], grad_x[None], *flat)
```

```python
import functools

import numpy as np
import jax
import jax.numpy as jnp
from jax import lax
from jax.experimental import pallas as pl
from jax.experimental.pallas import tpu as pltpu

F32 = jnp.float32
BF16 = jnp.bfloat16

D = 1024
E = 1024
HEADS = 8
DH = 128
CHUNK = 64
T = 2048
TC = 256
TT = T + TC
TM = 256
NT = TT // TM
NTX = T // TM
NDEV = 8
GRID_W = 64
POOL_WINDOWS = (2, 4, 8, 16)
PG = 256
EPS = 1e-6
WIN_COLS = 5 * E
SH_WIN = WIN_COLS // NDEV
SH_PWIN = 2 * E // NDEV
SH_ROWS = E // NDEV
SH_GRP = PG // NDEV
SH_ADA = 3 * D // NDEV
VMEM_LIMIT = 56 * 1024 * 1024

ADAM_LR, ADAM_B1, ADAM_B2, ADAM_EPS, ADAM_WD, ADAM_STEP = 0.001, 0.9, 0.999, 1e-08, 0.01, 10

MESH = pl.DeviceIdType.MESH
VMEM_SPEC = pl.BlockSpec(memory_space=pltpu.VMEM)
HBM_SPEC = pl.BlockSpec(memory_space=pltpu.HBM)
ANY_SPEC = pl.BlockSpec(memory_space=pl.ANY)


def _sds(shape, dtype):
    return jax.ShapeDtypeStruct(shape, dtype)


def _bf(a):
    return a if a.dtype == BF16 else a.astype(BF16)


def _dot(a, b):
    return lax.dot_general(_bf(a), _bf(b), (((1,), (0,)), ((), ())), preferred_element_type=F32)


def _dot_tb(a, b):
    return lax.dot_general(_bf(a), _bf(b), (((1,), (1,)), ((), ())), preferred_element_type=F32)


def _dot_ta(a, b):
    return lax.dot_general(_bf(a), _bf(b), (((0,), (0,)), ((), ())), preferred_element_type=F32)


def _bdot(a, b):
    return lax.dot_general(_bf(a), _bf(b), (((2,), (1,)), ((0,), (0,))), preferred_element_type=F32)


def _bdot_nt(a, b):
    return lax.dot_general(_bf(a), _bf(b), (((2,), (2,)), ((0,), (0,))), preferred_element_type=F32)


def _bdot_tn(a, b):
    return lax.dot_general(_bf(a), _bf(b), (((1,), (1,)), ((0,), (0,))), preferred_element_type=F32)


def _dot01(m01, x):
    hi = x.astype(BF16)
    lo = (x - hi.astype(F32)).astype(BF16)
    return _dot(m01, hi) + _dot(m01, lo)


def _rstd(x):
    return lax.rsqrt(jnp.mean(x * x, axis=-1, keepdims=True) + EPS)


def _sigmoid(x):
    return jax.nn.sigmoid(x)


def _colsum(a):
    return jnp.sum(a, axis=0, keepdims=True)


def _stack_rows(rows):
    n = rows[0].shape[-1]
    rid = lax.broadcasted_iota(jnp.int32, (16, n), 0)
    out = jnp.zeros((16, n), F32)
    for i, r in enumerate(rows):
        out = jnp.where(rid == i, r, out)
    return out


def _head_map(fn, *arrs):
    outs = [fn(*[a[:, h * DH:(h + 1) * DH] for a in arrs]) for h in range(HEADS)]
    return jnp.concatenate(outs, axis=1)


def _gla_consts():
    r = np.arange(TM)[:, None]
    c = np.arange(TM)[None, :]
    same = (r // CHUNK) == (c // CHUNK)
    tril = same & (c <= r)
    triu = same & (c >= r)
    m = np.stack([tril, triu]).astype(np.float32)
    return jnp.asarray(m, BF16), jnp.asarray(m, F32)


def _pool_consts():
    r = np.arange(TM)[:, None]
    c = np.arange(TM)[None, :]
    same = (r // GRID_W) == (c // GRID_W)
    rp, cp = r % GRID_W, c % GRID_W
    bs, inv = [], []
    for w in POOL_WINDOWS:
        lo = np.clip(rp - w // 2, 0, GRID_W)
        hi = np.clip(rp - w // 2 + w, 0, GRID_W)
        bs.append(same & (cp >= lo) & (cp < hi))
        inv.append(1.0 / (hi - lo).astype(np.float32))
    b = np.stack(bs).astype(np.float32)
    bt = np.transpose(b, (0, 2, 1))
    return jnp.asarray(b, BF16), jnp.asarray(bt, BF16), jnp.asarray(np.stack(inv), F32)


def _mesh_pos():
    x, y, c = lax.axis_index("x"), lax.axis_index("y"), lax.axis_index("c")
    return x, y, c, 4 * x + 2 * y + c


def _peer(x, y, c, k):
    return (x ^ ((k >> 2) & 1), y ^ ((k >> 1) & 1), c ^ (k & 1))


def _small_gathers(refs, ssem, rsem):
    lb_r, ps_r, c_r, cctx_r, ada_r, adab_r, lb_o, ps_o, cg_o, mod_o, lb_out, ps_out, cg_out, mod0_o, mod1_o, modc_o = refs
    x, y, cc, idx = _mesh_pos()
    srcs = [lb_r, ps_r, c_r, mod_o.at[idx]]
    mine = [lb_o.at[idx], ps_o.at[idx], cg_o.at[idx], mod_o.at[idx]]

    def remote(a, k):
        return pltpu.make_async_remote_copy(src_ref=srcs[a], dst_ref=mine[a], send_sem=ssem.at[a, k], recv_sem=rsem.at[a, k],
                                            device_id=_peer(x, y, cc, k), device_id_type=MESH)

    first = [remote(a, k) for k in range(1, NDEV) for a in (2, 0, 1)]
    for cp in first:
        cp.start()
    lb_o[idx] = lb_r[...]
    ps_o[idx] = ps_r[...]
    cg_o[idx] = c_r[...]
    for k in range(1, NDEV):
        remote(2, k).wait_recv()
    rows = _stack_rows([cg_o[i] for i in range(NDEV)] + [cctx_r[...]])
    sc = rows * _sigmoid(rows)
    for l in range(2):
        mod_o[idx, l] = _dot(sc, ada_r[l])
    second = [remote(3, k) for k in range(1, NDEV)]
    for cp in second:
        cp.start()
    for k in range(1, NDEV):
        remote(3, k).wait_recv()

    def mod_rows(l, row):
        full = jnp.concatenate([mod_o[s, l, row, :] for s in range(NDEV)], axis=1) + adab_r[l:l + 1, :]
        return [full[:, j * D:(j + 1) * D] for j in range(3)]

    me = pl.ds(idx, 1)
    for out, parts in ((mod0_o, mod_rows(0, me)), (mod1_o, mod_rows(1, me)), (modc_o, mod_rows(0, slice(NDEV, NDEV + 1)))):
        for j in range(3):
            out[j:j + 1, :] = parts[j]
    for cp in first + second:
        cp.wait_send()
    for k in range(1, NDEV):
        for a in (0, 1):
            remote(a, k).wait_recv()
    lb_out[...] = lb_o[...]
    ps_out[...] = ps_o[...]
    cg_out[...] = cg_o[...]


def _gather_order(s, core):
    k = jnp.where(s == 2, 4, jnp.where(s == 4, 2, s))
    return k ^ jnp.where((core == 1) & (s >= 2) & (s <= 5), 6, 0)


GATHER_ISSUE = (1, 2, 4, 3, 5, 6, 7)
GATHER_ICI = (2, 4, 6)
GATHER_DIRECT = (1,) + GATHER_ICI
GLA_HB = 2
RS_SLOTS = 5


def _shard_of(kind, ref, i):
    if kind == "rows":
        return ref.at[pl.ds(pl.multiple_of(i * SH_ROWS, SH_ROWS), SH_ROWS), :]
    if kind == "major":
        return ref.at[i]
    assert kind == "grp"
    return ref.at[:, pl.ds(pl.multiple_of(i * SH_GRP, SH_GRP), SH_GRP), :]


def _gather_rider(step, n_steps, forward_at, kinds, srcs, outs, ssem, rsem, lsem):
    x, y, cc, idx = _mesh_pos()
    arrays = range(len(kinds))
    mine = [_shard_of(kinds[a], outs[a], idx) for a in arrays]

    def remote(a, k):
        return pltpu.make_async_remote_copy(src_ref=srcs[a], dst_ref=mine[a], send_sem=ssem.at[a, k], recv_sem=rsem.at[a, k],
                                            device_id=_peer(x, y, cc, k), device_id_type=MESH)

    def forward(a, k):
        blk = _shard_of(kinds[a], outs[a], idx ^ k)
        return pltpu.make_async_remote_copy(src_ref=blk, dst_ref=blk, send_sem=ssem.at[a, k ^ 1], recv_sem=rsem.at[a, k ^ 1],
                                            device_id=(x, y, 1 - cc), device_id_type=MESH)

    copies = [remote(a, k) for k in GATHER_DIRECT for a in arrays]
    passed = [forward(a, k) for k in GATHER_ICI for a in arrays]
    local = [pltpu.make_async_copy(srcs[a], mine[a], lsem.at[a]) for a in arrays]

    @pl.when(step == 0)
    def _():
        for cp in copies + local:
            cp.start()

    @pl.when(step == forward_at)
    def _():
        for k in GATHER_ICI:
            for a in arrays:
                remote(a, k).wait_recv()
                forward(a, k).start()

    @pl.when(step == n_steps - 1)
    def _():
        for cp in copies + passed:
            cp.wait_send()
        for a in arrays:
            remote(a, 1).wait_recv()
        for cp in passed:
            cp.wait_recv()
        for cp in local:
            cp.wait()


def _scatter_rider(step, n_steps, kinds, grads, slots, ssem, rsem, lsem):
    x, y, cc, idx = _mesh_pos()
    arrays = range(len(kinds))
    dsts = [slots[a].at[idx] for a in arrays]

    def remote(a, k):
        px, py, pc = _peer(x, y, cc, k)
        return pltpu.make_async_remote_copy(src_ref=_shard_of(kinds[a], grads[a], 4 * px + 2 * py + pc), dst_ref=dsts[a],
                                            send_sem=ssem.at[a, k], recv_sem=rsem.at[a, k], device_id=(px, py, pc), device_id_type=MESH)

    copies = [remote(a, k) for k in GATHER_ISSUE for a in arrays]
    local = [pltpu.make_async_copy(_shard_of(kinds[a], grads[a], idx), dsts[a], lsem.at[a]) for a in arrays]

    @pl.when(step == 0)
    def _():
        for cp in copies + local:
            cp.start()

    @pl.when(step == n_steps - 1)
    def _():
        for cp in copies:
            cp.wait_send()
        for cp in copies:
            cp.wait_recv()
        for cp in local:
            cp.wait()


def _rider_sems(n):
    return [pltpu.SemaphoreType.DMA((n, NDEV)), pltpu.SemaphoreType.DMA((n, NDEV)), pltpu.SemaphoreType.DMA((n,))]


def _scatter_rider2(step, n_steps, add_at, kinds, grads, slots, bufs, sems):
    x, y, cc, idx = _mesh_pos()
    sibling = (x, y, 1 - cc)
    arrays = range(len(kinds))
    psend, precv, isend, irecv, lown, sibsem, lself = sems

    def mine(a, i):
        return _shard_of(kinds[a], grads[a], i)

    def partial(a, p):
        return pltpu.make_async_remote_copy(src_ref=mine(a, idx ^ (2 * (p + 1)) ^ 1), dst_ref=bufs[a][1].at[p], send_sem=psend.at[a, p],
                                            recv_sem=precv.at[a, p], device_id=sibling, device_id_type=MESH)

    def load(a, p):
        return pltpu.make_async_copy(mine(a, idx ^ (2 * (p + 1))), bufs[a][0].at[p], lown.at[a, p])

    def chip_sum(a, p):
        return pltpu.make_async_remote_copy(src_ref=bufs[a][0].at[p], dst_ref=slots[a].at[2 + p], send_sem=isend.at[a, p],
                                            recv_sem=irecv.at[a, p], device_id=_peer(x, y, cc, 2 * (p + 1)), device_id_type=MESH)

    def to_sibling(a):
        return pltpu.make_async_remote_copy(src_ref=mine(a, idx ^ 1), dst_ref=slots[a].at[1], send_sem=sibsem.at[a, 0],
                                            recv_sem=sibsem.at[a, 1], device_id=sibling, device_id_type=MESH)

    def own(a):
        return pltpu.make_async_copy(mine(a, idx), slots[a].at[0], lself.at[a, 0])

    @pl.when(step == 0)
    def _():
        for a in arrays:
            for p in range(3):
                partial(a, p).start()
                load(a, p).start()
            to_sibling(a).start()
            own(a).start()

    @pl.when(step == add_at)
    def _():
        for a in arrays:
            for p in range(3):
                partial(a, p).wait_recv()
                load(a, p).wait()
                bufs[a][0][p] = (bufs[a][0][p].astype(F32) + bufs[a][1][p].astype(F32)).astype(BF16)
                chip_sum(a, p).start()

    @pl.when(step == n_steps - 1)
    def _():
        for a in arrays:
            for p in range(3):
                partial(a, p).wait_send()
                chip_sum(a, p).wait_send()
                chip_sum(a, p).wait_recv()
            to_sibling(a).wait_send()
            to_sibling(a).wait_recv()
            own(a).wait()


def _rider2_scratch(blocks):
    n = len(blocks)
    bufs = [pltpu.VMEM((3,) + tuple(b), BF16) for b in blocks for _ in range(2)]
    return bufs + [pltpu.SemaphoreType.DMA((n, 3)) for _ in range(5)] + [pltpu.SemaphoreType.DMA((n, 2)), pltpu.SemaphoreType.DMA((n, 1))]


def _rider2_split(refs, n):
    refs = list(refs)
    return [tuple(refs[2 * a:2 * a + 2]) for a in range(n)], tuple(refs[2 * n:2 * n + 7])


def _modulated(x, nw, shift, scale):
    r = _rstd(x)
    xn = x * r
    a = xn * nw
    return a * (1.0 + scale) + shift, r, xn, a


def _ctx_or_x(i, ctx_ref, x_ref):
    return jnp.where(i == 0, ctx_ref[...], x_ref[...])


def _f1_gather_matmul(idx1, ctx, x, nw, w_in, w_out, pw_in, pgrp, pw_out, lb_l, pscale, c, c_ctx, ada_w, ada_b):
    def body(idx_ref, ctx_ref, x_ref, nw_ref, win_r, wout_r, pwin_r, pgrp_r, pwout_r, lb_r, ps_r, c_r, cctx_r, ada_r, adab_r,
             g_ref, win_o, s_wout, s_pwin, s_pgrp, s_pwout, lb_o, ps_o, cg_o, mod0_o, mod1_o, modc_o,
             wslot, hx_sc, lb_g, ps_g, cg_g, mod_g, ssem, rsem, osem, dsem, sm_ssem, sm_rsem):
        del idx_ref
        s, i = pl.program_id(0), pl.program_id(1)
        x, y, cc, idx = _mesh_pos()
        k = _gather_order(s, cc)
        j = idx ^ k
        first = 4 - 2 * cc

        def remote(kk):
            return pltpu.make_async_remote_copy(src_ref=wslot.at[idx], dst_ref=wslot.at[idx], send_sem=ssem.at[kk], recv_sem=rsem.at[kk],
                                                device_id=_peer(x, y, cc, kk), device_id_type=MESH)

        def forward(kk):
            jj = idx ^ kk
            return pltpu.make_async_remote_copy(src_ref=wslot.at[jj], dst_ref=wslot.at[jj], send_sem=ssem.at[kk ^ 1],
                                                recv_sem=rsem.at[kk ^ 1], device_id=(x, y, 1 - cc), device_id_type=MESH)

        def relay(h):
            blk = wslot.at[idx ^ (4 >> h), pl.ds(h * (D // 2), D // 2), :]
            return pltpu.make_async_remote_copy(src_ref=blk, dst_ref=blk, send_sem=dsem.at[0, h], recv_sem=dsem.at[1, h],
                                                device_id=_peer(x, y, cc, 2 << h), device_id_type=MESH)

        def to_hbm(jj, kk):
            return pltpu.make_async_copy(wslot.at[jj], win_o.at[jj], osem.at[kk])

        @pl.when((s == 0) & (i == 0))
        def _():
            _small_gathers((lb_r, ps_r, c_r, cctx_r, ada_r, adab_r, lb_g, ps_g, cg_g, mod_g, lb_o, ps_o, cg_o, mod0_o, mod1_o, modc_o),
                           sm_ssem, sm_rsem)
            wslot[idx] = win_r[...].astype(BF16)
            remote(1).start()
            remote(first).start()
            s_wout[...] = wout_r[...].astype(BF16)
            s_pwin[...] = pwin_r[...].astype(BF16)
            s_pgrp[...] = pgrp_r[...].astype(BF16)
            s_pwout[...] = pwout_r[...].astype(BF16)

        @pl.when(s == 0)
        def _():
            shift = jnp.where(i == 0, modc_o[0:1, :], mod0_o[0:1, :])
            scale = jnp.where(i == 0, modc_o[1:2, :], mod0_o[1:2, :])
            hx, _, _, _ = _modulated(_ctx_or_x(i, ctx_ref, x_ref), nw_ref[...], shift, scale)
            hx_sc[i] = hx.astype(BF16)

        @pl.when((s == 2) & (i == 0))
        def _():
            remote(6 - first).start()

        @pl.when((s > 0) & (i == 0) & (k != 6))
        def _():
            remote(k).wait_recv()

            @pl.when((k & 1) == 0)
            def _():
                forward(k).start()

            for h in range(2):
                @pl.when(k == 4 >> h)
                def _():
                    relay(h).start()

        @pl.when((i == 0) & (k == 6))
        def _():
            for h in range(2):
                relay(h).wait_recv()
            forward(6).start()

        @pl.when(i == 0)
        def _():
            to_hbm(j, k).start()

        g_ref[...] = jnp.dot(hx_sc[i], wslot[j], preferred_element_type=F32)

        @pl.when((s == NDEV - 1) & (i == NT - 1))
        def _():
            for kk in (1, 2, 4):
                remote(kk).wait_send()
            for kk in GATHER_ICI:
                forward(kk).wait_send()
            for h in range(2):
                relay(h).wait_send()
            for kk in range(NDEV):
                to_hbm(idx ^ kk, kk).wait()

    grid_spec = pltpu.PrefetchScalarGridSpec(
        num_scalar_prefetch=1, grid=(NDEV, NT),
        in_specs=[VMEM_SPEC, pl.BlockSpec((TM, D), lambda s, i, ix: (jnp.where(s == 0, jnp.maximum(i - 1, 0), NTX - 1), 0))]
        + [VMEM_SPEC] * 12,
        out_specs=[pl.BlockSpec((TM, SH_WIN), lambda s, i, ix: (i, ix[0] ^ _gather_order(s, ix[0] & 1))), HBM_SPEC] + [VMEM_SPEC] * 10,
        scratch_shapes=[pltpu.VMEM((NDEV, D, SH_WIN), BF16), pltpu.VMEM((NT, TM, D), BF16),
                        pltpu.VMEM((NDEV, 2, DH), F32), pltpu.VMEM((NDEV, 1, DH), F32), pltpu.VMEM((NDEV, 1, D), F32),
                        pltpu.VMEM((NDEV, 2, 16, SH_ADA), F32),
                        pltpu.SemaphoreType.DMA((NDEV,)), pltpu.SemaphoreType.DMA((NDEV,)), pltpu.SemaphoreType.DMA((NDEV,)),
                        pltpu.SemaphoreType.DMA((2, 2)),
                        pltpu.SemaphoreType.DMA((4, NDEV)), pltpu.SemaphoreType.DMA((4, NDEV))])
    outs = (_sds((TT, WIN_COLS), F32), _sds((NDEV, D, SH_WIN), BF16),
            _sds((SH_ROWS, D), BF16), _sds((D, SH_PWIN), BF16), _sds((4, SH_GRP, PG), BF16), _sds((SH_ROWS, D), BF16),
            _sds((NDEV, 2, DH), F32), _sds((NDEV, 1, DH), F32), _sds((NDEV, 1, D), F32),
            _sds((3, D), F32), _sds((3, D), F32), _sds((3, D), F32))
    return pl.pallas_call(
        body, name="f1_gather_matmul", grid_spec=grid_spec, out_shape=outs,
        compiler_params=pltpu.CompilerParams(dimension_semantics=("arbitrary", "arbitrary"), vmem_limit_bytes=VMEM_LIMIT),
    )(idx1, ctx, x, nw, w_in, w_out, pw_in, pgrp, pw_out, lb_l, pscale, c, c_ctx, ada_w, ada_b)


def _gla_gates(pre, qpre, lbd, cum, rev):
    rows, n = pre.shape
    nch = rows // CHUNK
    sig = _sigmoid(pre)
    f = lbd + (1.0 - lbd) * sig
    k = 1.0 - f
    g = _dot01(cum, jnp.log(f))
    g3 = g.reshape(nch, CHUNK, n)
    last = 0 if rev else CHUNK - 1
    mid = CHUNK // 2 if rev else CHUNK // 2 - 1
    gl1, gm1 = g3[:, last:last + 1, :], g3[:, mid:mid + 1, :]

    def bc(a):
        return jnp.broadcast_to(a, g3.shape).reshape(rows, n)

    gm = bc(gm1)
    e_q, e_k = jnp.exp(g - gm), jnp.exp(gm - g)
    qsig = _sigmoid(qpre)
    qs = qpre * qsig * (DH ** -0.5)
    return dict(sig=sig, f=f, k=k, qsig=qsig, qs=qs, e_q=e_q, e_k=e_k,
                e_mid=[jnp.exp(gm1[ci]) for ci in range(nch)], e_rest=[jnp.exp(gl1[ci] - gm1[ci]) for ci in range(nch)])


def _put_heads(ref, lead, arr):
    for h in range(HEADS):
        ref[lead + (h,)] = arr[:, h * DH:(h + 1) * DH]


def _get_heads(ref, lead=()):
    return jnp.concatenate([ref[lead + (h,)] for h in range(HEADS)], axis=1)


def _gla_prep(g_all, lb, cum01, s_wout, s_pgrp):
    nch = TM // CHUNK

    def body(g_ref, lb_ref, cum_ref, swout_r, spgrp_r, p0_ref, p1_ref, v_ref, dec_ref, wout_o, pgrp_o, ssem, rsem, lsem):
        _gather_rider(pl.program_id(0), NT, NT - 1, ("rows", "grp"), (swout_r, spgrp_r), (wout_o, pgrp_o), ssem, rsem, lsem)
        qpre = g_ref[:, 3 * E:4 * E]
        _put_heads(v_ref, (), g_ref[:, 2 * E:3 * E].astype(BF16))
        for d, p_ref in ((0, p0_ref), (1, p1_ref)):
            t = _gla_gates(g_ref[:, d * E:(d + 1) * E], qpre, lb_ref[d:d + 1, :], cum_ref[d], d == 1)
            _put_heads(p_ref, (0,), (t["qs"] * t["e_q"]).astype(BF16))
            _put_heads(p_ref, (1,), (t["k"] * t["e_k"]).astype(BF16))
            for ci in range(nch):
                dec_ref[d, 0, ci:ci + 1, :] = t["e_mid"][ci]
                dec_ref[d, 0, nch + ci:nch + ci + 1, :] = t["e_rest"][ci]

    quad = pl.BlockSpec((2, HEADS, TM, DH), lambda i: (0, 0, i, 0))
    return pl.pallas_call(
        body, name="gla_prep", grid=(NT,),
        in_specs=[pl.BlockSpec((TM, 4 * E), lambda i: (i, 0)), VMEM_SPEC, VMEM_SPEC, HBM_SPEC, HBM_SPEC],
        out_specs=[quad, quad, pl.BlockSpec((HEADS, TM, DH), lambda i: (0, i, 0)), pl.BlockSpec((2, 1, 2 * nch, E), lambda i: (0, i, 0, 0)),
                   HBM_SPEC, HBM_SPEC],
        out_shape=(_sds((2, HEADS, TT, DH), BF16), _sds((2, HEADS, TT, DH), BF16), _sds((HEADS, TT, DH), BF16), _sds((2, NT, 2 * nch, E), F32),
                   _sds((E, D), BF16), _sds((4, PG, PG), BF16)),
        scratch_shapes=_rider_sems(2),
        compiler_params=pltpu.CompilerParams(dimension_semantics=("arbitrary",), vmem_limit_bytes=VMEM_LIMIT),
    )(g_all, lb, cum01, s_wout, s_pgrp)


def _scan_tile(i, rev):
    t = jnp.where(i == 0, 0, NT - i) if rev else i
    return t, pl.ds(pl.multiple_of(t * TM, TM), TM)


def _chunk_order(rev):
    n = TM // CHUNK
    return tuple(range(n - 1, -1, -1)) if rev else tuple(range(n))


def _chunk_rows(dec_ref, lanes, cis, where):
    nch = TM // CHUNK

    def rows(off):
        return jnp.stack([dec_ref[d, where[d][0], off + ci:off + ci + 1, hh * DH:(hh + 1) * DH] for (d, hh), ci in zip(lanes, cis)])

    return rows(0), rows(nch)


def _gla_fwd(p0, p1, v_all, dec, mask01, s_pwin, s_pwout):
    n_steps = HEADS // GLA_HB

    def body(p0_ref, p1_ref, v_ref, dec_ref, msk_ref, spwin_r, spwout_r, o_ref, pwin_o, pwout_o, ob_sc, ssem, rsem, lsem):
        _gather_rider(pl.program_id(0), n_steps, n_steps - 1, ("major", "rows"), (spwin_r, spwout_r), (pwin_o, pwout_o), ssem, rsem, lsem)

        lanes = [(d, hh) for d in (0, 1) for hh in range(GLA_HB)]
        nch = TM // CHUNK

        def tile_body(i, st):
            where = [_scan_tile(i, d == 1) for d in (0, 1)]

            def stacked(fn):
                return jnp.stack([fn(d, hh, where[d][1]) for d, hh in lanes])

            qg, kg = [stacked(lambda d, hh, rows, ty=ty: (p1_ref if d else p0_ref)[ty, hh, rows, :]) for ty in range(2)]
            v = stacked(lambda d, hh, rows: v_ref[hh, rows, :])
            a = _bdot_nt(qg, kg) * jnp.stack([msk_ref[d] for d, _ in lanes])
            intra = _bdot(a, v)
            outs = [[None] * nch for _ in lanes]
            for n in range(nch):
                cis = [nch - 1 - n if d else n for d, _ in lanes]

                def chunk(arr):
                    return jnp.stack([arr[l, ci * CHUNK:(ci + 1) * CHUNK] for l, ci in enumerate(cis)])

                e_mid, e_rest = _chunk_rows(dec_ref, lanes, cis, where)
                inter = _bdot_nt(chunk(qg), st * e_mid)
                for l, ci in enumerate(cis):
                    outs[l][ci] = inter[l] + intra[l, ci * CHUNK:(ci + 1) * CHUNK]
                st = st * (e_mid * e_rest) + _bdot_tn(chunk(v), chunk(kg)) * e_rest
            for l, (d, hh) in enumerate(lanes):
                (ob_sc if d else o_ref)[hh, where[d][1], :] = jnp.concatenate(outs[l], axis=0)
            return st

        lax.fori_loop(0, NT, tile_body, jnp.zeros((len(lanes), DH, DH), F32))
        o_ref[...] += ob_sc[...]

    quad = pl.BlockSpec((2, GLA_HB, TT, DH), lambda h: (0, h, 0, 0))
    head = pl.BlockSpec((GLA_HB, TT, DH), lambda h: (h, 0, 0))
    return pl.pallas_call(
        body, name="gla_fwd", grid=(n_steps,),
        in_specs=[quad, quad, head, pl.BlockSpec((2, NT, 8, GLA_HB * DH), lambda h: (0, 0, 0, h)),
                  pl.BlockSpec((2, TM, TM), lambda h: (0, 0, 0)), HBM_SPEC, HBM_SPEC],
        out_specs=[head, HBM_SPEC, HBM_SPEC],
        out_shape=(_sds((HEADS, TT, DH), F32), _sds((NDEV, D, SH_PWIN), BF16), _sds((E, D), BF16)),
        scratch_shapes=[pltpu.VMEM((GLA_HB, TT, DH), F32)] + _rider_sems(2),
        compiler_params=pltpu.CompilerParams(dimension_semantics=("arbitrary",), vmem_limit_bytes=VMEM_LIMIT),
    )(p0, p1, v_all, dec, mask01, s_pwin, s_pwout)


def _gated_norm(o, z, gw):
    r = _head_map(lambda oh: jnp.broadcast_to(_rstd(oh), oh.shape), o)
    on = o * r
    zs = _sigmoid(z)
    sz = z * zs
    return on * gw * sz, r, on, zs, sz


def _f3_out(o, g_all, x, gate, gw, wout):
    def body(o_ref, z_ref, x_ref, gate_ref, gw_ref, w_ref, x1_ref):
        og, _, _, _, _ = _gated_norm(_get_heads(o_ref), z_ref[...], gw_ref[...])
        x1_ref[...] = x_ref[...] + gate_ref[...] * _dot(og, w_ref[...])

    return pl.pallas_call(
        body, name="f3_out", grid=(NTX,),
        in_specs=[pl.BlockSpec((HEADS, TM, DH), lambda i: (0, i + 1, 0)), pl.BlockSpec((TM, E), lambda i: (i + 1, 4)),
                  pl.BlockSpec((TM, D), lambda i: (i, 0)), pl.BlockSpec((1, D), lambda i: (0, 0)),
                  pl.BlockSpec((1, E), lambda i: (0, 0)), pl.BlockSpec((E, D), lambda i: (0, 0))],
        out_specs=pl.BlockSpec((TM, D), lambda i: (i, 0)),
        out_shape=_sds((T, D), F32),
        compiler_params=pltpu.CompilerParams(dimension_semantics=("arbitrary",)),
    )(o, g_all, x, gate, gw, wout)


def _pool_layer(x1, tgt, mod1, nw1, fnw, pwin, pgrp, pscale, pwout, pb, pbt, pinv):
    def body(x_ref, t_ref, m_ref, nw_ref, fw_ref, pwin_ref, pgrp_ref, ps_ref, pwout_ref, pb_ref, pbt_ref, pinv_ref,
             dx_ref, gpwin_o, gpgrp_o, gpwout_o, dmod_o, gnw_o, gfw_o, gps_o, loss_o,
             a_pwin, a_pgrp, a_pwout):
        i = pl.program_id(0)

        @pl.when(i == 0)
        def _():
            for ref in (a_pwin, a_pgrp, a_pwout, dmod_o, gnw_o, gfw_o, gps_o, loss_o):
                ref[...] = jnp.zeros_like(ref)

        shift, scale, gate = m_ref[0:1, :], m_ref[1:2, :], m_ref[2:3, :]
        nw, fw, ps = nw_ref[...], fw_ref[...], ps_ref[...]
        x1 = x_ref[...]
        hx, r1, xn, a = _modulated(x1, nw, shift, scale)
        hxb = hx.astype(BF16)
        uz = jnp.concatenate([_dot(hxb, pwin_ref[j]) for j in range(NDEV)], axis=1)
        u, z = uz[:, :E], uz[:, E:]
        pooled, ys = [], []
        for g in range(4):
            ug = u[:, g * PG:(g + 1) * PG]
            pg = _dot01(pb_ref[g], ug) * pinv_ref[g] - ug
            pooled.append(pg.astype(BF16))
            ys.append(_dot(pooled[g], pgrp_ref[g]))
        ycat = jnp.concatenate(ys, axis=1)
        y = ycat * ps
        zs = _sigmoid(z)
        sz = z * zs
        p = (y * sz).astype(BF16)
        out = _dot(p, pwout_ref[...])
        x2 = x1 + gate * out
        r2 = _rstd(x2)
        xn2 = x2 * r2
        diff = xn2 * fw - t_ref[...]
        loss_o[...] += _colsum(diff * diff)
        dyf = diff * (1.0 / D)
        gfw_o[...] += _colsum(dyf * xn2)
        dxn2 = dyf * fw
        dx2 = r2 * (dxn2 - xn2 * jnp.mean(dxn2 * xn2, axis=-1, keepdims=True))
        dgate = _colsum(dx2 * out)
        dout = (dx2 * gate).astype(BF16)
        for j in range(4):
            cs = slice(j * PG, (j + 1) * PG)
            a_pwout[:, cs] += _dot_ta(p, dout[:, cs])
        dp = _dot_tb(dout, pwout_ref[...])
        dy = dp * sz
        dz = dp * y * (zs * (1.0 + z * (1.0 - zs)))
        gps_o[...] += _colsum(dy * ycat)
        dycat = dy * ps
        dus = []
        for g in range(4):
            dyg = dycat[:, g * PG:(g + 1) * PG].astype(BF16)
            a_pgrp[g] += _dot_ta(pooled[g], dyg)
            dpg = _dot_tb(dyg, pgrp_ref[g])
            dus.append(_dot01(pbt_ref[g], dpg * pinv_ref[g]) - dpg)
        duz = jnp.concatenate(dus + [dz], axis=1).astype(BF16)
        dhx = None
        for j in range(NDEV):
            dj = duz[:, j * SH_PWIN:(j + 1) * SH_PWIN]
            a_pwin[j] += _dot_ta(hxb, dj)
            part = _dot_tb(dj, pwin_ref[j])
            dhx = part if dhx is None else dhx + part
        dmod_o[0:1, :] += _colsum(dhx)
        dmod_o[1:2, :] += _colsum(dhx * a)
        dmod_o[2:3, :] += dgate
        da = dhx * (1.0 + scale)
        gnw_o[...] += _colsum(da * xn)
        dxn = da * nw
        dx_ref[...] = dx2 + r1 * (dxn - xn * jnp.mean(dxn * xn, axis=-1, keepdims=True))

        @pl.when(i == NTX - 1)
        def _():
            gpwin_o[...] = a_pwin[...].astype(BF16)
            gpgrp_o[...] = a_pgrp[...].astype(BF16)
            gpwout_o[...] = a_pwout[...].astype(BF16)

    tile = pl.BlockSpec((TM, D), lambda i: (i, 0))
    outs = (_sds((T, D), F32), _sds((NDEV, D, SH_PWIN), BF16), _sds((4, PG, PG), BF16), _sds((E, D), BF16),
            _sds((3, D), F32), _sds((1, D), F32), _sds((1, D), F32), _sds((1, E), F32), _sds((1, D), F32))
    return pl.pallas_call(
        body, name="pool_layer", grid=(NTX,),
        in_specs=[tile, tile] + [VMEM_SPEC] * 10,
        out_specs=[tile] + [VMEM_SPEC] * 8,
        out_shape=outs,
        scratch_shapes=[pltpu.VMEM((NDEV, D, SH_PWIN), F32), pltpu.VMEM((4, PG, PG), F32), pltpu.VMEM((E, D), F32)],
        compiler_params=pltpu.CompilerParams(dimension_semantics=("arbitrary",), vmem_limit_bytes=VMEM_LIMIT),
    )(x1, tgt, mod1, nw1, fnw, pwin, pgrp, pscale, pwout, pb, pbt, pinv)


def _b3_out_bwd(dx1, o, g_all, gate, gw, wout, gpwout):
    def body(dx_ref, o_ref, z_ref, gate_ref, gw_ref, w_ref, gpwout_r, do_ref, dz_ref, gw_o, dgate_o, ggw_o, rpwout_o,
             acc, *rider):
        i = pl.program_id(0)
        bufs, sems = _rider2_split(rider, 1)
        _scatter_rider2(i, NT, 2, ("rows",), (gpwout_r,), (rpwout_o,), bufs, sems)

        @pl.when(i == 0)
        def _():
            acc[...] = jnp.zeros_like(acc)
            dgate_o[...] = jnp.zeros_like(dgate_o)
            ggw_o[...] = jnp.zeros_like(ggw_o)
            do_ref[...] = jnp.zeros_like(do_ref)
            dz_ref[...] = jnp.zeros_like(dz_ref)

        @pl.when(i > 0)
        def _():
            gw = gw_ref[...]
            z = z_ref[...]
            og, r, on, zs, sz = _gated_norm(_get_heads(o_ref), z, gw)
            ogb = og.astype(BF16)
            dx = dx_ref[...]
            dgate_o[...] += _colsum(dx * _dot(ogb, w_ref[...]))
            dy = (dx * gate_ref[...]).astype(BF16)
            for j in range(4):
                cs = slice(j * PG, (j + 1) * PG)
                acc[:, cs] += _dot_ta(ogb, dy[:, cs])
            dog = _dot_tb(dy, w_ref[...])
            dz_ref[...] = (dog * (on * gw) * (zs * (1.0 + z * (1.0 - zs)))).astype(BF16)
            dong = dog * sz
            ggw_o[...] += _colsum(dong * on)
            don = dong * gw
            do = _head_map(lambda dh, nh, rh: rh * (dh - nh * jnp.mean(dh * nh, axis=-1, keepdims=True)), don, on, r)
            _put_heads(do_ref, (), do.astype(BF16))

        @pl.when(i == NT - 1)
        def _():
            gw_o[...] = acc[...].astype(BF16)

    prev = lambda i: (jnp.maximum(i - 1, 0), 0)
    heads = pl.BlockSpec((HEADS, TM, DH), lambda i: (0, i, 0))
    return pl.pallas_call(
        body, name="b3_out_bwd", grid=(NT,),
        in_specs=[pl.BlockSpec((TM, D), prev), heads, pl.BlockSpec((TM, E), lambda i: (i, 4)),
                  VMEM_SPEC, VMEM_SPEC, VMEM_SPEC, HBM_SPEC],
        out_specs=[heads, pl.BlockSpec((TM, E), lambda i: (i, 0)), VMEM_SPEC, VMEM_SPEC, VMEM_SPEC, HBM_SPEC],
        out_shape=(_sds((HEADS, TT, DH), BF16), _sds((TT, E), BF16), _sds((E, D), BF16), _sds((1, D), F32), _sds((1, E), F32),
                   _sds((RS_SLOTS, SH_ROWS, D), BF16)),
        scratch_shapes=[pltpu.VMEM((E, D), F32)] + _rider2_scratch([(SH_ROWS, D)]),
        compiler_params=pltpu.CompilerParams(dimension_semantics=("arbitrary",), vmem_limit_bytes=VMEM_LIMIT),
    )(dx1, o, g_all, gate, gw, wout, gpwout)


def _gla_bwd(p0, p1, v_all, dec, do, mask01, gpwin, gpgrp):
    nch = TM // CHUNK
    n_steps = HEADS // GLA_HB

    def body(p0_ref, p1_ref, v_ref, dec_ref, do_ref, msk_ref, gpwin_r, gpgrp_r, d0_ref, d1_ref, dv_ref, dgl_ref, rpwin_o, rpgrp_o,
             ss_sc, dv_sc, ssem, rsem, lsem, *rider):
        _scatter_rider(pl.program_id(0), n_steps, ("grp",), (gpgrp_r,), (rpgrp_o,), ssem, rsem, lsem)
        bufs, sems = _rider2_split(rider, 1)
        _scatter_rider2(pl.program_id(0), n_steps, 1, ("major",), (gpwin_r,), (rpwin_o,), bufs, sems)

        lanes = [(d, hh) for d in (0, 1) for hh in range(GLA_HB)]
        zero = jnp.zeros((len(lanes), DH, DH), F32)
        dgl_ref[...] = jnp.zeros_like(dgl_ref)

        def p_of(d):
            return p1_ref if d else p0_ref

        def scan_step(i, n):
            where = [_scan_tile(i, d == 1) for d in (0, 1)]
            cis = [nch - 1 - n if d else n for d, _ in lanes]
            e_mid, e_rest = _chunk_rows(dec_ref, lanes, cis, where)

            def chunk(arr):
                return jnp.stack([arr[l, ci * CHUNK:(ci + 1) * CHUNK] for l, ci in enumerate(cis)])

            return where, cis, e_mid, e_rest, chunk

        def stacked(i, fn):
            where = [_scan_tile(i, d == 1) for d in (0, 1)]
            return jnp.stack([fn(d, hh, where[d][1]) for d, hh in lanes])

        def fwd_body(i, st):
            v = stacked(i, lambda d, hh, rows: v_ref[hh, rows, :])
            kg = stacked(i, lambda d, hh, rows: p_of(d)[1, hh, rows, :])
            for n in range(nch):
                _, _, e_mid, e_rest, chunk = scan_step(i, n)
                ss_sc[i * nch + n] = st
                st = st * (e_mid * e_rest) + _bdot_tn(chunk(v), chunk(kg)) * e_rest
            return st

        ss_sc[NT * nch] = lax.fori_loop(0, NT, fwd_body, zero)

        def bwd_body(ii, dst):
            i = NT - 1 - ii
            qg, kg = [stacked(i, lambda d, hh, rows, ty=ty: p_of(d)[ty, hh, rows, :]) for ty in range(2)]
            v = stacked(i, lambda d, hh, rows: v_ref[hh, rows, :])
            dob = stacked(i, lambda d, hh, rows: do_ref[hh, rows, :])
            msk = jnp.stack([msk_ref[d] for d, _ in lanes])
            a = (_bdot_nt(qg, kg) * msk).astype(BF16)
            da = (_bdot_nt(dob, v) * msk).astype(BF16)
            dqg = _bdot(da, kg)
            dkg = _bdot_tn(da, qg)
            dv_intra = _bdot_tn(a, dob)
            dv_l, dkg_l, dqg_l = ([[None] * nch for _ in lanes] for _ in range(3))
            for n in range(nch - 1, -1, -1):
                where, cis, e_mid, e_rest, chunk = scan_step(i, n)
                s_c, s_end = ss_sc[i * nch + n], ss_sc[i * nch + n + 1]
                dste = (dst * e_rest).astype(BF16)
                kg_c, v_c, dob_c = chunk(kg), chunk(v), chunk(dob)
                dv_c = chunk(dv_intra) + _bdot_nt(kg_c, dste)
                dkg_c = chunk(dkg) + _bdot(v_c, dste)
                dqg_c = chunk(dqg) + _bdot(dob_c, s_c * e_mid)
                dgl = jnp.sum(s_end * dst, axis=1, keepdims=True)
                for l, ((d, hh), ci) in enumerate(zip(lanes, cis)):
                    dv_l[l][ci], dkg_l[l][ci], dqg_l[l][ci] = dv_c[l], dkg_c[l], dqg_c[l]
                    dgl_ref[d, where[d][0], ci:ci + 1, hh * DH:(hh + 1) * DH] = dgl[l]
                dst = dst * (e_mid * e_rest) + _bdot_tn(dob_c, chunk(qg)) * e_mid
            where = [_scan_tile(i, d == 1) for d in (0, 1)]
            for l, (d, hh) in enumerate(lanes):
                rows = where[d][1]
                d_ref = d1_ref if d else d0_ref
                d_ref[0, hh, rows, :] = jnp.concatenate(dqg_l[l], axis=0).astype(BF16)
                d_ref[1, hh, rows, :] = jnp.concatenate(dkg_l[l], axis=0).astype(BF16)
                dv_sc[d, hh, rows, :] = jnp.concatenate(dv_l[l], axis=0).astype(BF16)
            return dst

        lax.fori_loop(0, NT, bwd_body, zero)
        dv_ref[...] = (dv_sc[0].astype(F32) + dv_sc[1].astype(F32)).astype(BF16)

    quad = pl.BlockSpec((2, GLA_HB, TT, DH), lambda h: (0, h, 0, 0))
    col = pl.BlockSpec((GLA_HB, TT, DH), lambda h: (h, 0, 0))
    chunkv = pl.BlockSpec((2, NT, 8, GLA_HB * DH), lambda h: (0, 0, 0, h))
    outs = (_sds((2, HEADS, TT, DH), BF16), _sds((2, HEADS, TT, DH), BF16), _sds((HEADS, TT, DH), BF16), _sds((2, NT, 8, E), F32),
            _sds((RS_SLOTS, D, SH_PWIN), BF16), _sds((NDEV, 4, SH_GRP, PG), BF16))
    return pl.pallas_call(
        body, name="gla_bwd", grid=(n_steps,),
        in_specs=[quad, quad, col, chunkv, col, pl.BlockSpec((2, TM, TM), lambda h: (0, 0, 0)), HBM_SPEC, HBM_SPEC],
        out_specs=[quad, quad, col, chunkv, HBM_SPEC, HBM_SPEC],
        out_shape=outs,
        scratch_shapes=[pltpu.VMEM((NT * nch + 1, 2 * GLA_HB, DH, DH), F32), pltpu.VMEM((2, GLA_HB, TT, DH), BF16)] + _rider_sems(1)
        + _rider2_scratch([(D, SH_PWIN)]),
        compiler_params=pltpu.CompilerParams(dimension_semantics=("arbitrary",), vmem_limit_bytes=VMEM_LIMIT),
    )(p0, p1, v_all, dec, do, mask01, gpwin, gpgrp)


TMB = 128


def _gla_post_bwd(g_all, d0, d1, dgl, dv, dz, lb, cum01, gwout):
    nch = TMB // CHUNK

    def body(g_ref, d0_ref, d1_ref, dgl_ref, dv_ref, dz_ref, lb_ref, cum_ref, gwout_r, dg_ref, dlb_ref, rwout_o, *rider):
        i = pl.program_id(0)
        bufs, sems = _rider2_split(rider, 1)
        _scatter_rider2(i, TT // TMB, 2, ("rows",), (gwout_r,), (rwout_o,), bufs, sems)

        @pl.when(i == 0)
        def _():
            dlb_ref[...] = jnp.zeros_like(dlb_ref)

        half = i & 1
        qpre = g_ref[:, 3 * E:4 * E]
        dqs_sum = None
        dpre = []
        for d, d_ref in ((0, d0_ref), (1, d1_ref)):
            rev = d == 1
            lbd = lb_ref[d:d + 1, :]
            t = _gla_gates(g_ref[:, d * E:(d + 1) * E], qpre, lbd, cum_ref[d, :TMB, :TMB], rev)
            dqs = _get_heads(d_ref, (0,)).astype(F32) * t["e_q"]
            dk = _get_heads(d_ref, (1,)).astype(F32) * t["e_k"]
            dg = t["qs"] * dqs - t["k"] * dk
            dgl8 = dgl_ref[d, 0]
            dgl_rows = [jnp.where(half == 0, dgl8[ci:ci + 1, :], dgl8[nch + ci:nch + ci + 1, :]) for ci in range(nch)]
            dgl_b = jnp.concatenate([jnp.broadcast_to(dgl_rows[ci], (CHUNK, E)) for ci in range(nch)], axis=0)
            pos = lax.broadcasted_iota(jnp.int32, (TMB, E), 0) & (CHUNK - 1)
            dg = dg + jnp.where(pos == (0 if rev else CHUNK - 1), dgl_b, 0.0)
            dlf = _dot01(cum_ref[1 - d, :TMB, :TMB], dg)
            df = dlf / t["f"] - dk
            sig = t["sig"]
            dpre.append((df * (1.0 - lbd) * sig * (1.0 - sig)).astype(BF16))
            dlb_ref[d:d + 1, :] += _colsum(df * (1.0 - sig))
            dqs_sum = dqs if dqs_sum is None else dqs_sum + dqs
            qsig = t["qsig"]
        dqpre = dqs_sum * (DH ** -0.5) * (qsig * (1.0 + qpre * (1.0 - qsig)))
        dg_ref[...] = jnp.concatenate([dpre[0], dpre[1], _get_heads(dv_ref), dqpre.astype(BF16), dz_ref[...]], axis=1)

    quad = pl.BlockSpec((2, HEADS, TMB, DH), lambda i: (0, 0, i, 0))
    tile = pl.BlockSpec((TMB, E), lambda i: (i, 0))
    return pl.pallas_call(
        body, name="gla_post_bwd", grid=(TT // TMB,),
        in_specs=[pl.BlockSpec((TMB, 4 * E), lambda i: (i, 0)), quad, quad,
                  pl.BlockSpec((2, 1, 8, E), lambda i: (0, i // 2, 0, 0)), pl.BlockSpec((HEADS, TMB, DH), lambda i: (0, i, 0)), tile,
                  VMEM_SPEC, VMEM_SPEC, HBM_SPEC],
        out_specs=[pl.BlockSpec((TMB, WIN_COLS), lambda i: (i, 0)), VMEM_SPEC, HBM_SPEC],
        out_shape=(_sds((TT, WIN_COLS), BF16), _sds((2, E), F32), _sds((RS_SLOTS, SH_ROWS, D), BF16)),
        scratch_shapes=_rider2_scratch([(SH_ROWS, D)]),
        compiler_params=pltpu.CompilerParams(dimension_semantics=("arbitrary",), vmem_limit_bytes=VMEM_LIMIT),
    )(g_all, d0, d1, dgl, dv, dz, lb, cum01, gwout)


WIN_SLOTS = 4


def _scatter_order(s, core):
    return (NDEV - 1 - s) ^ jnp.where((s >= 2) & (s <= 5) & ((s & 1) == core), 6, 0)


def _b1_in_bwd(idx1, ctx, x, dx1, dg, nw, msel, win):
    last_s = NDEV - 1
    half = D // 2

    def body(idx_ref, ctx_ref, x_ref, dx1_ref, dg_ref, nw_ref, m_ref, w_ref, gx_ref, rwin_o, dmx_o, dmc_o, gnw_o,
             hx_sc, dhx_sc, acc, sbuf, pbuf, rbuf, psend, precv, isend, irecv, dsend, drecv, sibsem, lsem):
        del idx_ref
        s, i = pl.program_id(0), pl.program_id(1)
        x, y, cc, idx = _mesh_pos()
        shift, scale = m_ref[0, 0:1, :], m_ref[0, 1:2, :]
        sibling = (x, y, 1 - cc)

        def partial(p):
            return pltpu.make_async_remote_copy(src_ref=sbuf.at[0], dst_ref=pbuf.at[p], send_sem=psend.at[p], recv_sem=precv.at[p],
                                                device_id=sibling, device_id_type=MESH)

        def chip_sum(p):
            return pltpu.make_async_remote_copy(src_ref=sbuf.at[1], dst_ref=rwin_o.at[2 + p], send_sem=isend.at[p], recv_sem=irecv.at[p],
                                                device_id=_peer(x, y, cc, 2 * (p + 1)), device_id_type=MESH)

        def relay(h):
            return pltpu.make_async_remote_copy(src_ref=sbuf.at[1, pl.ds(h * half, half), :], dst_ref=rbuf.at[h], send_sem=dsend.at[h],
                                                recv_sem=drecv.at[h], device_id=_peer(x, y, cc, 2 * (h + 1)), device_id_type=MESH)

        to_sibling = pltpu.make_async_remote_copy(src_ref=sbuf.at[0], dst_ref=rwin_o.at[1], send_sem=sibsem.at[0], recv_sem=sibsem.at[1],
                                                  device_id=sibling, device_id_type=MESH)
        own = pltpu.make_async_copy(sbuf.at[1], rwin_o.at[0], lsem)

        @pl.when((s == 0) & (i == 0))
        def _():
            for ref in (dmx_o, dmc_o, gnw_o):
                ref[...] = jnp.zeros_like(ref)

        @pl.when(s == 0)
        def _():
            hx, _, _, _ = _modulated(_ctx_or_x(i, ctx_ref, x_ref), nw_ref[...], shift, scale)
            hx_sc[i] = hx.astype(BF16)

        @pl.when(i == 0)
        def _():
            acc[...] = jnp.zeros_like(acc)

        dgb = dg_ref[...]
        hxb = hx_sc[i]
        for lo, hi in ((0, 256), (256, 512), (512, SH_WIN)):
            acc[:, lo:hi] += _dot_ta(hxb, dgb[:, lo:hi])
        part = _dot_tb(dgb, w_ref[0])

        @pl.when(s == 0)
        def _():
            dhx_sc[i] = part

        @pl.when(s > 0)
        def _():
            dhx_sc[i] += part

        done = i == NT - 1

        def hand_over(p, before):
            before.wait_send()
            sbuf[0] = acc[...].astype(BF16)
            partial(p).start()

        def send_chip_sum(p, before):
            for cp in before:
                cp.wait_send()
            partial(p).wait_recv()
            sbuf[1] = (acc[...] + pbuf[p].astype(F32)).astype(BF16)
            h = 1 - p
            rows = pl.ds(h * half, half)
            relay(h).wait_recv()
            sbuf[1, rows, :] = (acc[rows, :] + pbuf[p, rows, :].astype(F32) + rbuf[h].astype(F32)).astype(BF16)
            chip_sum(p).start()

        @pl.when(done & (s == 0))
        def _():
            sbuf[0] = acc[...].astype(BF16)
            partial(2).start()

        @pl.when(done & (s == 1))
        def _():
            partial(2).wait_recv()
            sbuf[1] = (acc[...] + pbuf[2].astype(F32)).astype(BF16)
            for h in range(2):
                relay(h).start()

        for core in range(2):
            @pl.when(done & (cc == core) & (s == 2))
            def _(core=core):
                hand_over(core, partial(2))

            @pl.when(done & (cc == core) & (s == 3))
            def _(core=core):
                send_chip_sum(1 - core, [relay(0), relay(1)])

            @pl.when(done & (cc == core) & (s == 4))
            def _(core=core):
                hand_over(1 - core, partial(core))

            @pl.when(done & (cc == core) & (s == 5))
            def _(core=core):
                send_chip_sum(core, [chip_sum(1 - core)])

            @pl.when(done & (cc == core) & (s == last_s - 1))
            def _(core=core):
                partial(1 - core).wait_send()
                sbuf[0] = acc[...].astype(BF16)
                to_sibling.start()

            @pl.when(done & (cc == core) & (s == last_s))
            def _(core=core):
                chip_sum(core).wait_send()
                sbuf[1] = acc[...].astype(BF16)
                own.start()

        @pl.when(s == last_s)
        def _():
            nw = nw_ref[...]
            _, r, xn, a = _modulated(_ctx_or_x(i, ctx_ref, x_ref), nw, shift, scale)
            dhx = dhx_sc[i]
            dsh, dsc = _colsum(dhx), _colsum(dhx * a)
            da = dhx * (1.0 + scale)
            gnw_o[...] += _colsum(da * xn)
            dxn = da * nw
            gx_ref[...] = dx1_ref[...] + r * (dxn - xn * jnp.mean(dxn * xn, axis=-1, keepdims=True))

            @pl.when(i == 0)
            def _():
                dmc_o[0:1, :] += dsh
                dmc_o[1:2, :] += dsc

            @pl.when(i > 0)
            def _():
                dmx_o[0:1, :] += dsh
                dmx_o[1:2, :] += dsc

        @pl.when((i == NT - 1) & (s == last_s))
        def _():
            to_sibling.wait_send()
            to_sibling.wait_recv()
            for p in range(2):
                chip_sum(p).wait_recv()
            own.wait()

    grid_spec = pltpu.PrefetchScalarGridSpec(
        num_scalar_prefetch=1, grid=(NDEV, NT),
        in_specs=[VMEM_SPEC,
                  pl.BlockSpec((TM, D), lambda s, i, ix: (jnp.where((s == 0) | (s == last_s), jnp.maximum(i - 1, 0), NTX - 1), 0)),
                  pl.BlockSpec((TM, D), lambda s, i, ix: (jnp.where(s == last_s, jnp.maximum(i - 1, 0), 0), 0)),
                  pl.BlockSpec((TM, SH_WIN), lambda s, i, ix: (i, ix[0] ^ _scatter_order(s, ix[0] & 1))), VMEM_SPEC,
                  pl.BlockSpec((1, 2, D), lambda s, i, ix: (jnp.minimum(i, 1), 0, 0)),
                  pl.BlockSpec((1, D, SH_WIN), lambda s, i, ix: (ix[0] ^ _scatter_order(s, ix[0] & 1), 0, 0))],
        out_specs=[pl.BlockSpec((TM, D), lambda s, i, ix: (jnp.where(s == last_s, jnp.maximum(i - 1, 0), 0), 0)),
                   HBM_SPEC, VMEM_SPEC, VMEM_SPEC, VMEM_SPEC],
        scratch_shapes=[pltpu.VMEM((NT, TM, D), BF16), pltpu.VMEM((NT, TM, D), F32), pltpu.VMEM((D, SH_WIN), F32),
                        pltpu.VMEM((2, D, SH_WIN), BF16), pltpu.VMEM((3, D, SH_WIN), BF16), pltpu.VMEM((2, half, SH_WIN), BF16),
                        pltpu.SemaphoreType.DMA((3,)), pltpu.SemaphoreType.DMA((3,)), pltpu.SemaphoreType.DMA((2,)),
                        pltpu.SemaphoreType.DMA((2,)), pltpu.SemaphoreType.DMA((2,)), pltpu.SemaphoreType.DMA((2,)),
                        pltpu.SemaphoreType.DMA((2,)), pltpu.SemaphoreType.DMA])
    return pl.pallas_call(
        body, name="b1_in_bwd", grid_spec=grid_spec,
        out_shape=(_sds((T, D), F32), _sds((WIN_SLOTS, D, SH_WIN), BF16), _sds((2, D), F32), _sds((2, D), F32), _sds((1, D), F32)),
        compiler_params=pltpu.CompilerParams(dimension_semantics=("arbitrary", "arbitrary"), vmem_limit_bytes=VMEM_LIMIT),
    )(idx1, ctx, x, dx1, dg, nw, msel, win)


def _reduce_small(pd, pv, cg, c_ctx, ada_w0):
    n_arr = 3

    def body(pd_r, pv_r, cg_r, cctx_r, ada_r, gada_o, gadab_o, gcctx_o, pvsum_o, loss_o,
             pd_all, pv_all, dsc_all, dsc_mine, ssem, rsem):
        x, y, cc, idx = _mesh_pos()
        srcs = [pd_r, pv_r, dsc_mine]
        dsts = [pd_all.at[idx], pv_all.at[idx], dsc_all.at[idx]]

        def remote(a, k):
            return pltpu.make_async_remote_copy(src_ref=srcs[a], dst_ref=dsts[a], send_sem=ssem.at[a, k], recv_sem=rsem.at[a, k],
                                                device_id=_peer(x, y, cc, k), device_id_type=MESH)

        first = [remote(a, k) for k in range(1, NDEV) for a in (0, 1)]
        for cp in first:
            cp.start()
        pd_all[idx] = pd_r[...]
        pv_all[idx] = pv_r[...]
        for k in range(1, NDEV):
            remote(0, k).wait_recv()
            remote(1, k).wait_recv()
        mine = [pd_all[s, :, pl.ds(idx, 1), :] for s in range(NDEV)]
        dmc = functools.reduce(lambda u, v: u + v, [m[2] for m in mine])
        rows = _stack_rows([cg_r[i] for i in range(NDEV)] + [cctx_r[...]])
        sc = (rows * _sigmoid(rows)).astype(BF16)
        gada_o[0] = _dot_ta(sc, _stack_rows([m[0] for m in mine] + [dmc]))
        gada_o[1] = _dot_ta(sc, _stack_rows([m[1] for m in mine]))
        dsc_mine[...] = _dot_tb(jnp.broadcast_to(dmc, (8, SH_ADA)), ada_r[...])[0:1, :]
        dsc_all[idx] = dsc_mine[...]
        second = [remote(2, k) for k in range(1, NDEV)]
        for cp in second:
            cp.start()
        tot = [functools.reduce(lambda u, v: u + v, [pd_all[s, l] for s in range(NDEV)]) for l in range(3)]
        gadab_o[0] = tot[0] + tot[2]
        gadab_o[1] = tot[1]
        pvs = functools.reduce(lambda u, v: u + v, [pv_all[s] for s in range(NDEV)])
        pvsum_o[...] = pvs
        loss_o[...] = jnp.broadcast_to(jnp.sum(pvs[:, PV_LOSS:PV_LOSS + D], axis=-1, keepdims=True) * (0.5 / D), (1, 128))
        for k in range(1, NDEV):
            remote(2, k).wait_recv()
        dsc = functools.reduce(lambda u, v: u + v, [dsc_all[s] for s in range(NDEV)])
        cx = cctx_r[...]
        sx = _sigmoid(cx)
        gcctx_o[...] = dsc * (sx * (1.0 + cx * (1.0 - sx)))
        for cp in first + second:
            cp.wait_send()

    outs = (_sds((2, D, SH_ADA), F32), _sds((2, NDEV, SH_ADA), F32), _sds((1, D), F32), _sds((1, PV_LEN), F32), _sds((1, 128), F32))
    return pl.pallas_call(
        body, name="reduce_small", out_shape=outs,
        in_specs=[VMEM_SPEC] * 5, out_specs=[VMEM_SPEC] * 5,
        scratch_shapes=[
            pltpu.VMEM((NDEV, 3, NDEV, SH_ADA), F32), pltpu.VMEM((NDEV, 1, PV_LEN), F32), pltpu.VMEM((NDEV, 1, D), F32),
            pltpu.VMEM((1, D), F32),
            pltpu.SemaphoreType.DMA((n_arr, NDEV)), pltpu.SemaphoreType.DMA((n_arr, NDEV)),
        ],
        compiler_params=pltpu.CompilerParams(vmem_limit_bytes=VMEM_LIMIT),
    )(pd, pv, cg, c_ctx, ada_w0)


PV_NW, PV_GNORM, PV_FINAL, PV_LB, PV_PSCALE, PV_LOSS, PV_LEN = 0, 2 * D, 3 * D, 4 * D, 6 * D, 7 * D, 8 * D


def _adamw(w, g, m, v):
    m = ADAM_B1 * m + (1.0 - ADAM_B1) * g
    v = ADAM_B2 * v + (1.0 - ADAM_B2) * (g * g)
    m_hat = m / (1.0 - ADAM_B1 ** ADAM_STEP)
    v_hat = v / (1.0 - ADAM_B2 ** ADAM_STEP)
    delta = -ADAM_LR * (m_hat / (jnp.sqrt(v_hat) + ADAM_EPS) + ADAM_WD * w)
    return delta, m, v


ADAM_STEPS = 8


def _adam_all(sharded, dense, small, lb_idx, lbv, carried):
    ns, nd, nsm = len(sharded), len(dense), len(small)

    def body(*refs):
        it = iter(refs)
        sh_in = [[next(it) for _ in range(4)] for _ in range(ns)]
        de_in = [[next(it) for _ in range(4)] for _ in range(nd)]
        sm_in = [[next(it) for _ in range(4)] for _ in range(nsm)]
        lb_r = next(it)
        for _ in carried:
            next(it)
        sh_out = [[next(it) for _ in range(4)] for _ in range(ns)]
        de_out = [[next(it) for _ in range(3)] for _ in range(nd)]
        sm_out = [[next(it) for _ in range(4)] for _ in range(nsm)]
        for (p, w, m, v), outs in zip(sh_in, sh_out):
            g = p[0].astype(F32)
            for s in range(1, p.shape[0]):
                g = g + p[s].astype(F32)
            d, mn, vn = _adamw(w[...], g, m[...], v[...])
            outs[0][...], outs[1][...], outs[2][...], outs[3][...] = g, d, mn, vn
        for (g, w, m, v), outs in zip(de_in, de_out):
            d, mn, vn = _adamw(w[...], g[...], m[...], v[...])
            outs[0][...], outs[1][...], outs[2][...] = d, mn, vn

        @pl.when(pl.program_id(0) == 0)
        def _():
            for j, ((g, w, m, v), outs) in enumerate(zip(sm_in, sm_out)):
                gj = g[...]
                if j == lb_idx:
                    gj = gj * lb_r[...] * (1.0 - lb_r[...])
                d, mn, vn = _adamw(w[...], gj, m[...], v[...])
                outs[0][...], outs[1][...], outs[2][...], outs[3][...] = gj, d, mn, vn

    def tile(a):
        return pl.BlockSpec((a.shape[0] // ADAM_STEPS, a.shape[1]), lambda i: (i, 0))

    in_specs, out_specs, out_shape, args = [], [], [], []
    for p, w, m, v in sharded:
        in_specs += [pl.BlockSpec((p.shape[0], p.shape[1] // ADAM_STEPS, p.shape[2]), lambda i: (0, i, 0))] + [tile(w)] * 3
        args += [p, w, m, v]
    for g, w, m, v in dense:
        in_specs += [tile(w)] * 4
        args += [g, w, m, v]
    for g, w, m, v in small:
        in_specs += [VMEM_SPEC] * 4
        args += [g, w, m, v]
    in_specs.append(VMEM_SPEC)
    args.append(lbv)
    for _, w, _, _ in sharded:
        out_specs += [tile(w)] * 4
        out_shape += [_sds(w.shape, F32)] * 4
    for _, w, _, _ in dense:
        out_specs += [tile(w)] * 3
        out_shape += [_sds(w.shape, F32)] * 3
    for _, w, _, _ in small:
        out_specs += [VMEM_SPEC] * 4
        out_shape += [_sds(w.shape, F32)] * 4
    aliases = {len(args) + n: len(out_shape) + n for n in range(len(carried))}
    in_specs += [ANY_SPEC] * len(carried)
    out_specs += [ANY_SPEC] * len(carried)
    args += list(carried)
    out_shape += [_sds(a.shape, a.dtype) for a in carried]
    res = pl.pallas_call(body, name="adam_all", grid=(ADAM_STEPS,), in_specs=in_specs, out_specs=out_specs, out_shape=tuple(out_shape),
                         input_output_aliases=aliases,
                         compiler_params=pltpu.CompilerParams(dimension_semantics=("arbitrary",), vmem_limit_bytes=VMEM_LIMIT))(*args)
    it = iter(res)
    return ([tuple(next(it) for _ in range(4)) for _ in range(ns)], [tuple(next(it) for _ in range(3)) for _ in range(nd)],
            [tuple(next(it) for _ in range(4)) for _ in range(nsm)], list(it))


def kernel(x, c, ctx, c_ctx, ada_w, ada_b, norm_w, hgrn_w_in, hgrn_lb_logits, hgrn_gnorm_w, hgrn_w_out, pool_w_in, pool_w_grp, pool_scale, pool_w_out, final_norm_w, loss_target, m_c_ctx, m_ada_w, m_ada_b, m_norm_w, m_hgrn_w_in, m_hgrn_lb_logits, m_hgrn_gnorm_w, m_hgrn_w_out, m_pool_w_in, m_pool_w_grp, m_pool_scale, m_pool_w_out, m_final_norm_w, v_c_ctx, v_ada_w, v_ada_b, v_norm_w, v_hgrn_w_in, v_hgrn_lb_logits, v_hgrn_gnorm_w, v_hgrn_w_out, v_pool_w_in, v_pool_w_grp, v_pool_scale, v_pool_w_out, v_final_norm_w):
    idx = 4 * lax.axis_index("x") + 2 * lax.axis_index("y") + lax.axis_index("c")
    cctx2 = c_ctx.reshape(1, D)
    cum01, mask01 = _gla_consts()
    pb, pbt, pinv = _pool_consts()

    idx1 = idx.reshape(1).astype(jnp.int32)
    nw0, nw1 = norm_w[0:1], norm_w[1:2]
    fnw = final_norm_w.reshape(1, D)
    g_all, win, s_wout, s_pwin, s_pgrp, s_pwout, lbl_g, ps_g, cg, mod0, mod1, modc = _f1_gather_matmul(
        idx1, ctx[0], x[0], nw0, hgrn_w_in[0], hgrn_w_out[0], pool_w_in[0], pool_w_grp[0], pool_w_out[0], hgrn_lb_logits[0],
        pool_scale, c, cctx2, ada_w, ada_b)
    lb = jax.nn.sigmoid(jnp.transpose(lbl_g, (1, 0, 2)).reshape(2, E))
    pscale = ps_g.reshape(1, E)
    msel = jnp.stack([modc[:2], mod0[:2]])
    p0, p1, v_all, dec, wout, pgrp = _gla_prep(g_all, lb, cum01, s_wout, s_pgrp)
    o, pwin, pwout = _gla_fwd(p0, p1, v_all, dec, mask01, s_pwin, s_pwout)
    x1 = _f3_out(o, g_all, x[0], mod0[2:3], hgrn_gnorm_w, wout)
    dx1, gpwin, gpgrp, gpwout, dmod1, gnw1, gfw, gps, lossv = _pool_layer(
        x1, loss_target[0], mod1, nw1, fnw, pwin, pgrp, pscale, pwout, pb, pbt, pinv)
    do, dz, gwout, dgate0, ggw, rpwout = _b3_out_bwd(dx1, o, g_all, mod0[2:3], hgrn_gnorm_w, wout, gpwout)
    d0, d1, dv, dgl, rpwin, rpgrp = _gla_bwd(p0, p1, v_all, dec, do, mask01, gpwin, gpgrp)
    dg, dlb, rwout = _gla_post_bwd(g_all, d0, d1, dgl, dv, dz, lb, cum01, gwout)
    grad_x, rwin, dmx, dmc, gnw0 = _b1_in_bwd(idx1, ctx[0], x[0], dx1, dg, nw0, msel, win)

    dmod0 = jnp.concatenate([dmx, dgate0], axis=0)
    dmodc = jnp.concatenate([dmc, jnp.zeros((1, D), F32)], axis=0)
    pd = jnp.stack([dmod0, dmod1, dmodc]).reshape(3, NDEV, SH_ADA)
    pv = jnp.concatenate([gnw0, gnw1, ggw, gfw, dlb.reshape(1, 2 * E), gps, lossv], axis=1)
    g_ada, g_adab, g_cctx, pvsum, loss128 = _reduce_small(pd, pv, cg, cctx2, ada_w[0])

    g2 = (4 * SH_GRP, PG)
    sharded_names = ["hgrn_w_in", "hgrn_w_out", "pool_w_in", "pool_w_grp", "pool_w_out"]
    sharded = [(rwin, hgrn_w_in[0], m_hgrn_w_in[0], v_hgrn_w_in[0]),
               (rwout, hgrn_w_out[0], m_hgrn_w_out[0], v_hgrn_w_out[0]),
               (rpwin, pool_w_in[0], m_pool_w_in[0], v_pool_w_in[0]),
               (rpgrp.reshape((NDEV,) + g2), pool_w_grp[0].reshape(g2), m_pool_w_grp[0].reshape(g2), v_pool_w_grp[0].reshape(g2)),
               (rpwout, pool_w_out[0], m_pool_w_out[0], v_pool_w_out[0])]
    a2 = (2 * D, SH_ADA)
    g_ada2 = g_ada.reshape(a2)
    dense = [(g_ada2, ada_w.reshape(a2), m_ada_w.reshape(a2), v_ada_w.reshape(a2))]
    lb_me = lax.dynamic_slice_in_dim(lb, idx * DH, DH, axis=1)
    small_names = ["c_ctx", "ada_b", "norm_w", "hgrn_lb_logits", "hgrn_gnorm_w", "pool_scale", "final_norm_w"]
    small = [(g_cctx, cctx2, m_c_ctx.reshape(1, D), v_c_ctx.reshape(1, D)),
             (g_adab.reshape(2, 3 * D), ada_b, m_ada_b, v_ada_b),
             (pvsum[:, PV_NW:PV_NW + 2 * D].reshape(2, D), norm_w, m_norm_w, v_norm_w),
             (lax.dynamic_slice_in_dim(pvsum[:, PV_LB:PV_LB + 2 * E].reshape(2, E), idx * DH, DH, axis=1),
              hgrn_lb_logits[0], m_hgrn_lb_logits[0], v_hgrn_lb_logits[0]),
             (pvsum[:, PV_GNORM:PV_GNORM + E], hgrn_gnorm_w, m_hgrn_gnorm_w, v_hgrn_gnorm_w),
             (lax.dynamic_slice_in_dim(pvsum[:, PV_PSCALE:PV_PSCALE + E], idx * DH, DH, axis=1), pool_scale, m_pool_scale, v_pool_scale),
             (pvsum[:, PV_FINAL:PV_FINAL + D], fnw, m_final_norm_w.reshape(1, D), v_final_norm_w.reshape(1, D))]
    r_sharded, r_dense, r_small, (grad_x,) = _adam_all(sharded, dense, small, 3, lb_me, [grad_x])
    out = dict(zip(sharded_names, r_sharded))
    out["ada_w"] = (g_ada2,) + r_dense[0]
    out.update(zip(small_names, r_small))

    shapes = {"c_ctx": (D,), "ada_w": (2, D, SH_ADA), "ada_b": (2, 3 * D), "norm_w": (2, D), "hgrn_w_in": (1, D, SH_WIN),
              "hgrn_lb_logits": (1, 2, DH), "hgrn_gnorm_w": (1, E), "hgrn_w_out": (1, SH_ROWS, D), "pool_w_in": (1, D, SH_PWIN),
              "pool_w_grp": (1, 4, SH_GRP, PG), "pool_scale": (1, DH), "pool_w_out": (1, SH_ROWS, D), "final_norm_w": (D,)}
    order = ["c_ctx", "ada_w", "ada_b", "norm_w", "hgrn_w_in", "hgrn_lb_logits", "hgrn_gnorm_w", "hgrn_w_out", "pool_w_in",
             "pool_w_grp", "pool_scale", "pool_w_out", "final_norm_w"]
    flat = [out[name][q].reshape(shapes[name]) for q in range(4) for name in order]
    return (loss128[0, 0], grad_x[None], *flat)
```

```python
import functools

import numpy as np
import jax
import jax.numpy as jnp
from jax import lax
from jax.experimental import pallas as pl
from jax.experimental.pallas import tpu as pltpu

F32 = jnp.float32
BF16 = jnp.bfloat16

D = 1024
E = 1024
HEADS = 8
DH = 128
CHUNK = 64
T = 2048
TC = 256
TT = T + TC
TM = 256
NT = TT // TM
NTX = T // TM
NDEV = 8
GRID_W = 64
POOL_WINDOWS = (2, 4, 8, 16)
PG = 256
EPS = 1e-6
WIN_COLS = 5 * E
SH_WIN = WIN_COLS // NDEV
SH_PWIN = 2 * E // NDEV
SH_ROWS = E // NDEV
SH_GRP = PG // NDEV
SH_ADA = 3 * D // NDEV
VMEM_LIMIT = 56 * 1024 * 1024

ADAM_LR, ADAM_B1, ADAM_B2, ADAM_EPS, ADAM_WD, ADAM_STEP = 0.001, 0.9, 0.999, 1e-08, 0.01, 10

MESH = pl.DeviceIdType.MESH
VMEM_SPEC = pl.BlockSpec(memory_space=pltpu.VMEM)
HBM_SPEC = pl.BlockSpec(memory_space=pltpu.HBM)


def _sds(shape, dtype):
    return jax.ShapeDtypeStruct(shape, dtype)


def _bf(a):
    return a if a.dtype == BF16 else a.astype(BF16)


def _dot(a, b):
    return lax.dot_general(_bf(a), _bf(b), (((1,), (0,)), ((), ())), preferred_element_type=F32)


def _dot_tb(a, b):
    return lax.dot_general(_bf(a), _bf(b), (((1,), (1,)), ((), ())), preferred_element_type=F32)


def _dot_ta(a, b):
    return lax.dot_general(_bf(a), _bf(b), (((0,), (0,)), ((), ())), preferred_element_type=F32)


def _bdot(a, b):
    return lax.dot_general(_bf(a), _bf(b), (((2,), (1,)), ((0,), (0,))), preferred_element_type=F32)


def _bdot_nt(a, b):
    return lax.dot_general(_bf(a), _bf(b), (((2,), (2,)), ((0,), (0,))), preferred_element_type=F32)


def _bdot_tn(a, b):
    return lax.dot_general(_bf(a), _bf(b), (((1,), (1,)), ((0,), (0,))), preferred_element_type=F32)


def _dot01(m01, x):
    hi = x.astype(BF16)
    lo = (x - hi.astype(F32)).astype(BF16)
    return _dot(m01, hi) + _dot(m01, lo)


def _rstd(x):
    return lax.rsqrt(jnp.mean(x * x, axis=-1, keepdims=True) + EPS)


def _sigmoid(x):
    return jax.nn.sigmoid(x)


def _colsum(a):
    return jnp.sum(a, axis=0, keepdims=True)


def _stack_rows(rows):
    n = rows[0].shape[-1]
    rid = lax.broadcasted_iota(jnp.int32, (16, n), 0)
    out = jnp.zeros((16, n), F32)
    for i, r in enumerate(rows):
        out = jnp.where(rid == i, r, out)
    return out


def _head_map(fn, *arrs):
    outs = [fn(*[a[:, h * DH:(h + 1) * DH] for a in arrs]) for h in range(HEADS)]
    return jnp.concatenate(outs, axis=1)


def _gla_consts():
    r = np.arange(TM)[:, None]
    c = np.arange(TM)[None, :]
    same = (r // CHUNK) == (c // CHUNK)
    tril = same & (c <= r)
    triu = same & (c >= r)
    m = np.stack([tril, triu]).astype(np.float32)
    return jnp.asarray(m, BF16), jnp.asarray(m, F32)


def _pool_consts():
    r = np.arange(TM)[:, None]
    c = np.arange(TM)[None, :]
    same = (r // GRID_W) == (c // GRID_W)
    rp, cp = r % GRID_W, c % GRID_W
    bs, inv = [], []
    for w in POOL_WINDOWS:
        lo = np.clip(rp - w // 2, 0, GRID_W)
        hi = np.clip(rp - w // 2 + w, 0, GRID_W)
        bs.append(same & (cp >= lo) & (cp < hi))
        inv.append(1.0 / (hi - lo).astype(np.float32))
    b = np.stack(bs).astype(np.float32)
    bt = np.transpose(b, (0, 2, 1))
    return jnp.asarray(b, BF16), jnp.asarray(bt, BF16), jnp.asarray(np.stack(inv), F32)


def _mesh_pos():
    x, y, c = lax.axis_index("x"), lax.axis_index("y"), lax.axis_index("c")
    return x, y, c, 4 * x + 2 * y + c


def _peer(x, y, c, k):
    return (x ^ ((k >> 2) & 1), y ^ ((k >> 1) & 1), c ^ (k & 1))


def _small_gathers(refs, ssem, rsem):
    lb_r, ps_r, c_r, cctx_r, ada_r, adab_r, lb_o, ps_o, cg_o, mod_o, lb_out, ps_out, cg_out, mod0_o, mod1_o, modc_o = refs
    x, y, cc, idx = _mesh_pos()
    srcs = [lb_r, ps_r, c_r, mod_o.at[idx]]
    mine = [lb_o.at[idx], ps_o.at[idx], cg_o.at[idx], mod_o.at[idx]]

    def remote(a, k):
        return pltpu.make_async_remote_copy(src_ref=srcs[a], dst_ref=mine[a], send_sem=ssem.at[a, k], recv_sem=rsem.at[a, k],
                                            device_id=_peer(x, y, cc, k), device_id_type=MESH)

    first = [remote(a, k) for k in range(1, NDEV) for a in (2, 0, 1)]
    for cp in first:
        cp.start()
    lb_o[idx] = lb_r[...]
    ps_o[idx] = ps_r[...]
    cg_o[idx] = c_r[...]
    for k in range(1, NDEV):
        remote(2, k).wait_recv()
    rows = _stack_rows([cg_o[i] for i in range(NDEV)] + [cctx_r[...]])
    sc = rows * _sigmoid(rows)
    for l in range(2):
        mod_o[idx, l] = _dot(sc, ada_r[l])
    second = [remote(3, k) for k in range(1, NDEV)]
    for cp in second:
        cp.start()
    for k in range(1, NDEV):
        remote(3, k).wait_recv()

    def mod_rows(l, row):
        full = jnp.concatenate([mod_o[s, l, row, :] for s in range(NDEV)], axis=1) + adab_r[l:l + 1, :]
        return [full[:, j * D:(j + 1) * D] for j in range(3)]

    me = pl.ds(idx, 1)
    for out, parts in ((mod0_o, mod_rows(0, me)), (mod1_o, mod_rows(1, me)), (modc_o, mod_rows(0, slice(NDEV, NDEV + 1)))):
        for j in range(3):
            out[j:j + 1, :] = parts[j]
    for cp in first + second:
        cp.wait_send()
    for k in range(1, NDEV):
        for a in (0, 1):
            remote(a, k).wait_recv()
    lb_out[...] = lb_o[...]
    ps_out[...] = ps_o[...]
    cg_out[...] = cg_o[...]


def _gather_order(s, core):
    k = jnp.where(s == 2, 4, jnp.where(s == 4, 2, s))
    return k ^ jnp.where((core == 1) & (s >= 2) & (s <= 5), 6, 0)


GATHER_ISSUE = (1, 2, 4, 3, 5, 6, 7)
GATHER_ICI = (2, 4, 6)
GATHER_DIRECT = (1,) + GATHER_ICI
GLA_HB = 2
RS_SLOTS = 5


def _shard_of(kind, ref, i):
    if kind == "rows":
        return ref.at[pl.ds(pl.multiple_of(i * SH_ROWS, SH_ROWS), SH_ROWS), :]
    if kind == "major":
        return ref.at[i]
    assert kind == "grp"
    return ref.at[:, pl.ds(pl.multiple_of(i * SH_GRP, SH_GRP), SH_GRP), :]


def _gather_rider(step, n_steps, forward_at, kinds, srcs, outs, ssem, rsem, lsem):
    x, y, cc, idx = _mesh_pos()
    arrays = range(len(kinds))
    mine = [_shard_of(kinds[a], outs[a], idx) for a in arrays]

    def remote(a, k):
        return pltpu.make_async_remote_copy(src_ref=srcs[a], dst_ref=mine[a], send_sem=ssem.at[a, k], recv_sem=rsem.at[a, k],
                                            device_id=_peer(x, y, cc, k), device_id_type=MESH)

    def forward(a, k):
        blk = _shard_of(kinds[a], outs[a], idx ^ k)
        return pltpu.make_async_remote_copy(src_ref=blk, dst_ref=blk, send_sem=ssem.at[a, k ^ 1], recv_sem=rsem.at[a, k ^ 1],
                                            device_id=(x, y, 1 - cc), device_id_type=MESH)

    copies = [remote(a, k) for k in GATHER_DIRECT for a in arrays]
    passed = [forward(a, k) for k in GATHER_ICI for a in arrays]
    local = [pltpu.make_async_copy(srcs[a], mine[a], lsem.at[a]) for a in arrays]

    @pl.when(step == 0)
    def _():
        for cp in copies + local:
            cp.start()

    @pl.when(step == forward_at)
    def _():
        for k in GATHER_ICI:
            for a in arrays:
                remote(a, k).wait_recv()
                forward(a, k).start()

    @pl.when(step == n_steps - 1)
    def _():
        for cp in copies + passed:
            cp.wait_send()
        for a in arrays:
            remote(a, 1).wait_recv()
        for cp in passed:
            cp.wait_recv()
        for cp in local:
            cp.wait()


def _scatter_rider(step, n_steps, kinds, grads, slots, ssem, rsem, lsem):
    x, y, cc, idx = _mesh_pos()
    arrays = range(len(kinds))
    dsts = [slots[a].at[idx] for a in arrays]

    def remote(a, k):
        px, py, pc = _peer(x, y, cc, k)
        return pltpu.make_async_remote_copy(src_ref=_shard_of(kinds[a], grads[a], 4 * px + 2 * py + pc), dst_ref=dsts[a],
                                            send_sem=ssem.at[a, k], recv_sem=rsem.at[a, k], device_id=(px, py, pc), device_id_type=MESH)

    copies = [remote(a, k) for k in GATHER_ISSUE for a in arrays]
    local = [pltpu.make_async_copy(_shard_of(kinds[a], grads[a], idx), dsts[a], lsem.at[a]) for a in arrays]

    @pl.when(step == 0)
    def _():
        for cp in copies + local:
            cp.start()

    @pl.when(step == n_steps - 1)
    def _():
        for cp in copies:
            cp.wait_send()
        for cp in copies:
            cp.wait_recv()
        for cp in local:
            cp.wait()


def _rider_sems(n):
    return [pltpu.SemaphoreType.DMA((n, NDEV)), pltpu.SemaphoreType.DMA((n, NDEV)), pltpu.SemaphoreType.DMA((n,))]


def _scatter_rider2(step, n_steps, add_at, kinds, grads, slots, bufs, sems):
    x, y, cc, idx = _mesh_pos()
    sibling = (x, y, 1 - cc)
    arrays = range(len(kinds))
    psend, precv, isend, irecv, lown, sibsem, lself = sems

    def mine(a, i):
        return _shard_of(kinds[a], grads[a], i)

    def partial(a, p):
        return pltpu.make_async_remote_copy(src_ref=mine(a, idx ^ (2 * (p + 1)) ^ 1), dst_ref=bufs[a][1].at[p], send_sem=psend.at[a, p],
                                            recv_sem=precv.at[a, p], device_id=sibling, device_id_type=MESH)

    def load(a, p):
        return pltpu.make_async_copy(mine(a, idx ^ (2 * (p + 1))), bufs[a][0].at[p], lown.at[a, p])

    def chip_sum(a, p):
        return pltpu.make_async_remote_copy(src_ref=bufs[a][0].at[p], dst_ref=slots[a].at[2 + p], send_sem=isend.at[a, p],
                                            recv_sem=irecv.at[a, p], device_id=_peer(x, y, cc, 2 * (p + 1)), device_id_type=MESH)

    def to_sibling(a):
        return pltpu.make_async_remote_copy(src_ref=mine(a, idx ^ 1), dst_ref=slots[a].at[1], send_sem=sibsem.at[a, 0],
                                            recv_sem=sibsem.at[a, 1], device_id=sibling, device_id_type=MESH)

    def own(a):
        return pltpu.make_async_copy(mine(a, idx), slots[a].at[0], lself.at[a, 0])

    @pl.when(step == 0)
    def _():
        for a in arrays:
            for p in range(3):
                partial(a, p).start()
                load(a, p).start()
            to_sibling(a).start()
            own(a).start()

    @pl.when(step == add_at)
    def _():
        for a in arrays:
            for p in range(3):
                partial(a, p).wait_recv()
                load(a, p).wait()
                bufs[a][0][p] = (bufs[a][0][p].astype(F32) + bufs[a][1][p].astype(F32)).astype(BF16)
                chip_sum(a, p).start()

    @pl.when(step == n_steps - 1)
    def _():
        for a in arrays:
            for p in range(3):
                partial(a, p).wait_send()
                chip_sum(a, p).wait_send()
                chip_sum(a, p).wait_recv()
            to_sibling(a).wait_send()
            to_sibling(a).wait_recv()
            own(a).wait()


def _rider2_scratch(blocks):
    n = len(blocks)
    bufs = [pltpu.VMEM((3,) + tuple(b), BF16) for b in blocks for _ in range(2)]
    return bufs + [pltpu.SemaphoreType.DMA((n, 3)) for _ in range(5)] + [pltpu.SemaphoreType.DMA((n, 2)), pltpu.SemaphoreType.DMA((n, 1))]


def _rider2_split(refs, n):
    refs = list(refs)
    return [tuple(refs[2 * a:2 * a + 2]) for a in range(n)], tuple(refs[2 * n:2 * n + 7])


def _modulated(x, nw, shift, scale):
    r = _rstd(x)
    xn = x * r
    a = xn * nw
    return a * (1.0 + scale) + shift, r, xn, a


def _ctx_or_x(i, ctx_ref, x_ref):
    return jnp.where(i == 0, ctx_ref[...], x_ref[...])


def _f1_gather_matmul(idx1, ctx, x, nw, w_in, w_out, pw_in, pgrp, pw_out, lb_l, pscale, c, c_ctx, ada_w, ada_b):
    def body(idx_ref, ctx_ref, x_ref, nw_ref, win_r, wout_r, pwin_r, pgrp_r, pwout_r, lb_r, ps_r, c_r, cctx_r, ada_r, adab_r,
             g_ref, win_o, s_wout, s_pwin, s_pgrp, s_pwout, lb_o, ps_o, cg_o, mod0_o, mod1_o, modc_o,
             wslot, hx_sc, lb_g, ps_g, cg_g, mod_g, ssem, rsem, osem, dsem, sm_ssem, sm_rsem):
        del idx_ref
        s, i = pl.program_id(0), pl.program_id(1)
        x, y, cc, idx = _mesh_pos()
        k = _gather_order(s, cc)
        j = idx ^ k
        first = 4 - 2 * cc

        def remote(kk):
            return pltpu.make_async_remote_copy(src_ref=wslot.at[idx], dst_ref=wslot.at[idx], send_sem=ssem.at[kk], recv_sem=rsem.at[kk],
                                                device_id=_peer(x, y, cc, kk), device_id_type=MESH)

        def forward(kk):
            jj = idx ^ kk
            return pltpu.make_async_remote_copy(src_ref=wslot.at[jj], dst_ref=wslot.at[jj], send_sem=ssem.at[kk ^ 1],
                                                recv_sem=rsem.at[kk ^ 1], device_id=(x, y, 1 - cc), device_id_type=MESH)

        def relay(h):
            blk = wslot.at[idx ^ (4 >> h), pl.ds(h * (D // 2), D // 2), :]
            return pltpu.make_async_remote_copy(src_ref=blk, dst_ref=blk, send_sem=dsem.at[0, h], recv_sem=dsem.at[1, h],
                                                device_id=_peer(x, y, cc, 2 << h), device_id_type=MESH)

        def to_hbm(jj, kk):
            return pltpu.make_async_copy(wslot.at[jj], win_o.at[jj], osem.at[kk])

        @pl.when((s == 0) & (i == 0))
        def _():
            _small_gathers((lb_r, ps_r, c_r, cctx_r, ada_r, adab_r, lb_g, ps_g, cg_g, mod_g, lb_o, ps_o, cg_o, mod0_o, mod1_o, modc_o),
                           sm_ssem, sm_rsem)
            wslot[idx] = win_r[...].astype(BF16)
            remote(1).start()
            remote(first).start()
            s_wout[...] = wout_r[...].astype(BF16)
            s_pwin[...] = pwin_r[...].astype(BF16)
            s_pgrp[...] = pgrp_r[...].astype(BF16)
            s_pwout[...] = pwout_r[...].astype(BF16)

        @pl.when(s == 0)
        def _():
            shift = jnp.where(i == 0, modc_o[0:1, :], mod0_o[0:1, :])
            scale = jnp.where(i == 0, modc_o[1:2, :], mod0_o[1:2, :])
            hx, _, _, _ = _modulated(_ctx_or_x(i, ctx_ref, x_ref), nw_ref[...], shift, scale)
            hx_sc[i] = hx.astype(BF16)

        @pl.when((s == 2) & (i == 0))
        def _():
            remote(6 - first).start()

        @pl.when((s > 0) & (i == 0) & (k != 6))
        def _():
            remote(k).wait_recv()

            @pl.when((k & 1) == 0)
            def _():
                forward(k).start()

            for h in range(2):
                @pl.when(k == 4 >> h)
                def _():
                    relay(h).start()

        @pl.when((i == 0) & (k == 6))
        def _():
            for h in range(2):
                relay(h).wait_recv()
            forward(6).start()

        @pl.when(i == 0)
        def _():
            to_hbm(j, k).start()

        g_ref[...] = jnp.dot(hx_sc[i], wslot[j], preferred_element_type=F32)

        @pl.when((s == NDEV - 1) & (i == NT - 1))
        def _():
            for kk in (1, 2, 4):
                remote(kk).wait_send()
            for kk in GATHER_ICI:
                forward(kk).wait_send()
            for h in range(2):
                relay(h).wait_send()
            for kk in range(NDEV):
                to_hbm(idx ^ kk, kk).wait()

    grid_spec = pltpu.PrefetchScalarGridSpec(
        num_scalar_prefetch=1, grid=(NDEV, NT),
        in_specs=[VMEM_SPEC, pl.BlockSpec((TM, D), lambda s, i, ix: (jnp.where(s == 0, jnp.maximum(i - 1, 0), NTX - 1), 0))]
        + [VMEM_SPEC] * 12,
        out_specs=[pl.BlockSpec((TM, SH_WIN), lambda s, i, ix: (i, ix[0] ^ _gather_order(s, ix[0] & 1))), HBM_SPEC] + [VMEM_SPEC] * 10,
        scratch_shapes=[pltpu.VMEM((NDEV, D, SH_WIN), BF16), pltpu.VMEM((NT, TM, D), BF16),
                        pltpu.VMEM((NDEV, 2, DH), F32), pltpu.VMEM((NDEV, 1, DH), F32), pltpu.VMEM((NDEV, 1, D), F32),
                        pltpu.VMEM((NDEV, 2, 16, SH_ADA), F32),
                        pltpu.SemaphoreType.DMA((NDEV,)), pltpu.SemaphoreType.DMA((NDEV,)), pltpu.SemaphoreType.DMA((NDEV,)),
                        pltpu.SemaphoreType.DMA((2, 2)),
                        pltpu.SemaphoreType.DMA((4, NDEV)), pltpu.SemaphoreType.DMA((4, NDEV))])
    outs = (_sds((TT, WIN_COLS), F32), _sds((NDEV, D, SH_WIN), BF16),
            _sds((SH_ROWS, D), BF16), _sds((D, SH_PWIN), BF16), _sds((4, SH_GRP, PG), BF16), _sds((SH_ROWS, D), BF16),
            _sds((NDEV, 2, DH), F32), _sds((NDEV, 1, DH), F32), _sds((NDEV, 1, D), F32),
            _sds((3, D), F32), _sds((3, D), F32), _sds((3, D), F32))
    return pl.pallas_call(
        body, name="f1_gather_matmul", grid_spec=grid_spec, out_shape=outs,
        compiler_params=pltpu.CompilerParams(dimension_semantics=("arbitrary", "arbitrary"), vmem_limit_bytes=VMEM_LIMIT),
    )(idx1, ctx, x, nw, w_in, w_out, pw_in, pgrp, pw_out, lb_l, pscale, c, c_ctx, ada_w, ada_b)


def _gla_gates(pre, qpre, lbd, cum, rev):
    rows, n = pre.shape
    nch = rows // CHUNK
    sig = _sigmoid(pre)
    f = lbd + (1.0 - lbd) * sig
    k = 1.0 - f
    g = _dot01(cum, jnp.log(f))
    g3 = g.reshape(nch, CHUNK, n)
    last = 0 if rev else CHUNK - 1
    mid = CHUNK // 2 if rev else CHUNK // 2 - 1
    gl1, gm1 = g3[:, last:last + 1, :], g3[:, mid:mid + 1, :]

    def bc(a):
        return jnp.broadcast_to(a, g3.shape).reshape(rows, n)

    gm = bc(gm1)
    e_q, e_k = jnp.exp(g - gm), jnp.exp(gm - g)
    qsig = _sigmoid(qpre)
    qs = qpre * qsig * (DH ** -0.5)
    return dict(sig=sig, f=f, k=k, qsig=qsig, qs=qs, e_q=e_q, e_k=e_k,
                e_mid=[jnp.exp(gm1[ci]) for ci in range(nch)], e_rest=[jnp.exp(gl1[ci] - gm1[ci]) for ci in range(nch)])


def _put_heads(ref, lead, arr):
    for h in range(HEADS):
        ref[lead + (h,)] = arr[:, h * DH:(h + 1) * DH]


def _get_heads(ref, lead=()):
    return jnp.concatenate([ref[lead + (h,)] for h in range(HEADS)], axis=1)


def _gla_prep(g_all, lb, cum01, s_wout, s_pgrp):
    nch = TM // CHUNK

    def body(g_ref, lb_ref, cum_ref, swout_r, spgrp_r, p0_ref, p1_ref, v_ref, dec_ref, wout_o, pgrp_o, ssem, rsem, lsem):
        _gather_rider(pl.program_id(0), NT, NT - 1, ("rows", "grp"), (swout_r, spgrp_r), (wout_o, pgrp_o), ssem, rsem, lsem)
        qpre = g_ref[:, 3 * E:4 * E]
        _put_heads(v_ref, (), g_ref[:, 2 * E:3 * E].astype(BF16))
        for d, p_ref in ((0, p0_ref), (1, p1_ref)):
            t = _gla_gates(g_ref[:, d * E:(d + 1) * E], qpre, lb_ref[d:d + 1, :], cum_ref[d], d == 1)
            _put_heads(p_ref, (0,), (t["qs"] * t["e_q"]).astype(BF16))
            _put_heads(p_ref, (1,), (t["k"] * t["e_k"]).astype(BF16))
            for ci in range(nch):
                dec_ref[d, 0, ci:ci + 1, :] = t["e_mid"][ci]
                dec_ref[d, 0, nch + ci:nch + ci + 1, :] = t["e_rest"][ci]

    quad = pl.BlockSpec((2, HEADS, TM, DH), lambda i: (0, 0, i, 0))
    return pl.pallas_call(
        body, name="gla_prep", grid=(NT,),
        in_specs=[pl.BlockSpec((TM, 4 * E), lambda i: (i, 0)), VMEM_SPEC, VMEM_SPEC, HBM_SPEC, HBM_SPEC],
        out_specs=[quad, quad, pl.BlockSpec((HEADS, TM, DH), lambda i: (0, i, 0)), pl.BlockSpec((2, 1, 2 * nch, E), lambda i: (0, i, 0, 0)),
                   HBM_SPEC, HBM_SPEC],
        out_shape=(_sds((2, HEADS, TT, DH), BF16), _sds((2, HEADS, TT, DH), BF16), _sds((HEADS, TT, DH), BF16), _sds((2, NT, 2 * nch, E), F32),
                   _sds((E, D), BF16), _sds((4, PG, PG), BF16)),
        scratch_shapes=_rider_sems(2),
        compiler_params=pltpu.CompilerParams(dimension_semantics=("arbitrary",), vmem_limit_bytes=VMEM_LIMIT),
    )(g_all, lb, cum01, s_wout, s_pgrp)


def _scan_tile(i, rev):
    t = jnp.where(i == 0, 0, NT - i) if rev else i
    return t, pl.ds(pl.multiple_of(t * TM, TM), TM)


def _chunk_rows(dec_ref, lanes, cis, where):
    nch = TM // CHUNK

    def rows(off):
        return jnp.stack([dec_ref[d, where[d][0], off + ci:off + ci + 1, hh * DH:(hh + 1) * DH] for (d, hh), ci in zip(lanes, cis)])

    return rows(0), rows(nch)


def _gla_fwd(p0, p1, v_all, dec, mask01, s_pwin, s_pwout):
    n_steps = HEADS // GLA_HB

    def body(p0_ref, p1_ref, v_ref, dec_ref, msk_ref, spwin_r, spwout_r, o_ref, pwin_o, pwout_o, ob_sc, ssem, rsem, lsem):
        _gather_rider(pl.program_id(0), n_steps, n_steps - 1, ("major", "rows"), (spwin_r, spwout_r), (pwin_o, pwout_o), ssem, rsem, lsem)

        lanes = [(d, hh) for d in (0, 1) for hh in range(GLA_HB)]
        nch = TM // CHUNK

        def tile_body(i, st):
            where = [_scan_tile(i, d == 1) for d in (0, 1)]

            def stacked(fn):
                return jnp.stack([fn(d, hh, where[d][1]) for d, hh in lanes])

            qg, kg = [stacked(lambda d, hh, rows, ty=ty: (p1_ref if d else p0_ref)[ty, hh, rows, :]) for ty in range(2)]
            v = stacked(lambda d, hh, rows: v_ref[hh, rows, :])
            a = _bdot_nt(qg, kg) * jnp.stack([msk_ref[d] for d, _ in lanes])
            intra = _bdot(a, v)
            outs = [[None] * nch for _ in lanes]
            for n in range(nch):
                cis = [nch - 1 - n if d else n for d, _ in lanes]

                def chunk(arr):
                    return jnp.stack([arr[l, ci * CHUNK:(ci + 1) * CHUNK] for l, ci in enumerate(cis)])

                e_mid, e_rest = _chunk_rows(dec_ref, lanes, cis, where)
                inter = _bdot_nt(chunk(qg), st * e_mid)
                for l, ci in enumerate(cis):
                    outs[l][ci] = inter[l] + intra[l, ci * CHUNK:(ci + 1) * CHUNK]
                st = st * (e_mid * e_rest) + _bdot_tn(chunk(v), chunk(kg)) * e_rest
            for l, (d, hh) in enumerate(lanes):
                (ob_sc if d else o_ref)[hh, where[d][1], :] = jnp.concatenate(outs[l], axis=0)
            return st

        lax.fori_loop(0, NT, tile_body, jnp.zeros((len(lanes), DH, DH), F32))
        o_ref[...] += ob_sc[...]

    quad = pl.BlockSpec((2, GLA_HB, TT, DH), lambda h: (0, h, 0, 0))
    head = pl.BlockSpec((GLA_HB, TT, DH), lambda h: (h, 0, 0))
    return pl.pallas_call(
        body, name="gla_fwd", grid=(n_steps,),
        in_specs=[quad, quad, head, pl.BlockSpec((2, NT, 8, GLA_HB * DH), lambda h: (0, 0, 0, h)),
                  pl.BlockSpec((2, TM, TM), lambda h: (0, 0, 0)), HBM_SPEC, HBM_SPEC],
        out_specs=[head, HBM_SPEC, HBM_SPEC],
        out_shape=(_sds((HEADS, TT, DH), F32), _sds((NDEV, D, SH_PWIN), BF16), _sds((E, D), BF16)),
        scratch_shapes=[pltpu.VMEM((GLA_HB, TT, DH), F32)] + _rider_sems(2),
        compiler_params=pltpu.CompilerParams(dimension_semantics=("arbitrary",), vmem_limit_bytes=VMEM_LIMIT),
    )(p0, p1, v_all, dec, mask01, s_pwin, s_pwout)


def _gated_norm(o, z, gw):
    r = _head_map(lambda oh: jnp.broadcast_to(_rstd(oh), oh.shape), o)
    on = o * r
    zs = _sigmoid(z)
    sz = z * zs
    return on * gw * sz, r, on, zs, sz


def _f3_out(o, g_all, x, gate, gw, wout):
    def body(o_ref, z_ref, x_ref, gate_ref, gw_ref, w_ref, x1_ref):
        og, _, _, _, _ = _gated_norm(_get_heads(o_ref), z_ref[...], gw_ref[...])
        x1_ref[...] = x_ref[...] + gate_ref[...] * _dot(og, w_ref[...])

    return pl.pallas_call(
        body, name="f3_out", grid=(NTX,),
        in_specs=[pl.BlockSpec((HEADS, TM, DH), lambda i: (0, i + 1, 0)), pl.BlockSpec((TM, E), lambda i: (i + 1, 4)),
                  pl.BlockSpec((TM, D), lambda i: (i, 0)), pl.BlockSpec((1, D), lambda i: (0, 0)),
                  pl.BlockSpec((1, E), lambda i: (0, 0)), pl.BlockSpec((E, D), lambda i: (0, 0))],
        out_specs=pl.BlockSpec((TM, D), lambda i: (i, 0)),
        out_shape=_sds((T, D), F32),
        compiler_params=pltpu.CompilerParams(dimension_semantics=("arbitrary",)),
    )(o, g_all, x, gate, gw, wout)


def _pool_layer(x1, tgt, mod1, nw1, fnw, pwin, pgrp, pscale, pwout, pb, pbt, pinv):
    def body(x_ref, t_ref, m_ref, nw_ref, fw_ref, pwin_ref, pgrp_ref, ps_ref, pwout_ref, pb_ref, pbt_ref, pinv_ref,
             dx_ref, gpwin_o, gpgrp_o, gpwout_o, dmod_o, gnw_o, gfw_o, gps_o, loss_o,
             a_pwin, a_pgrp, a_pwout):
        i = pl.program_id(0)

        @pl.when(i == 0)
        def _():
            for ref in (a_pwin, a_pgrp, a_pwout, dmod_o, gnw_o, gfw_o, gps_o, loss_o):
                ref[...] = jnp.zeros_like(ref)

        shift, scale, gate = m_ref[0:1, :], m_ref[1:2, :], m_ref[2:3, :]
        nw, fw, ps = nw_ref[...], fw_ref[...], ps_ref[...]
        x1 = x_ref[...]
        hx, r1, xn, a = _modulated(x1, nw, shift, scale)
        hxb = hx.astype(BF16)
        uz = jnp.concatenate([_dot(hxb, pwin_ref[j]) for j in range(NDEV)], axis=1)
        u, z = uz[:, :E], uz[:, E:]
        pooled, ys = [], []
        for g in range(4):
            ug = u[:, g * PG:(g + 1) * PG]
            pg = _dot01(pb_ref[g], ug) * pinv_ref[g] - ug
            pooled.append(pg.astype(BF16))
            ys.append(_dot(pooled[g], pgrp_ref[g]))
        ycat = jnp.concatenate(ys, axis=1)
        y = ycat * ps
        zs = _sigmoid(z)
        sz = z * zs
        p = (y * sz).astype(BF16)
        out = _dot(p, pwout_ref[...])
        x2 = x1 + gate * out
        r2 = _rstd(x2)
        xn2 = x2 * r2
        diff = xn2 * fw - t_ref[...]
        loss_o[...] += _colsum(diff * diff)
        dyf = diff * (1.0 / D)
        gfw_o[...] += _colsum(dyf * xn2)
        dxn2 = dyf * fw
        dx2 = r2 * (dxn2 - xn2 * jnp.mean(dxn2 * xn2, axis=-1, keepdims=True))
        dgate = _colsum(dx2 * out)
        dout = (dx2 * gate).astype(BF16)
        for j in range(4):
            cs = slice(j * PG, (j + 1) * PG)
            a_pwout[:, cs] += _dot_ta(p, dout[:, cs])
        dp = _dot_tb(dout, pwout_ref[...])
        dy = dp * sz
        dz = dp * y * (zs * (1.0 + z * (1.0 - zs)))
        gps_o[...] += _colsum(dy * ycat)
        dycat = dy * ps
        dus = []
        for g in range(4):
            dyg = dycat[:, g * PG:(g + 1) * PG].astype(BF16)
            a_pgrp[g] += _dot_ta(pooled[g], dyg)
            dpg = _dot_tb(dyg, pgrp_ref[g])
            dus.append(_dot01(pbt_ref[g], dpg * pinv_ref[g]) - dpg)
        duz = jnp.concatenate(dus + [dz], axis=1).astype(BF16)
        dhx = None
        for j in range(NDEV):
            dj = duz[:, j * SH_PWIN:(j + 1) * SH_PWIN]
            a_pwin[j] += _dot_ta(hxb, dj)
            part = _dot_tb(dj, pwin_ref[j])
            dhx = part if dhx is None else dhx + part
        dmod_o[0:1, :] += _colsum(dhx)
        dmod_o[1:2, :] += _colsum(dhx * a)
        dmod_o[2:3, :] += dgate
        da = dhx * (1.0 + scale)
        gnw_o[...] += _colsum(da * xn)
        dxn = da * nw
        dx_ref[...] = dx2 + r1 * (dxn - xn * jnp.mean(dxn * xn, axis=-1, keepdims=True))

        @pl.when(i == NTX - 1)
        def _():
            gpwin_o[...] = a_pwin[...].astype(BF16)
            gpgrp_o[...] = a_pgrp[...].astype(BF16)
            gpwout_o[...] = a_pwout[...].astype(BF16)

    tile = pl.BlockSpec((TM, D), lambda i: (i, 0))
    outs = (_sds((T, D), F32), _sds((NDEV, D, SH_PWIN), BF16), _sds((4, PG, PG), BF16), _sds((E, D), BF16),
            _sds((3, D), F32), _sds((1, D), F32), _sds((1, D), F32), _sds((1, E), F32), _sds((1, D), F32))
    return pl.pallas_call(
        body, name="pool_layer", grid=(NTX,),
        in_specs=[tile, tile] + [VMEM_SPEC] * 10,
        out_specs=[tile] + [VMEM_SPEC] * 8,
        out_shape=outs,
        scratch_shapes=[pltpu.VMEM((NDEV, D, SH_PWIN), F32), pltpu.VMEM((4, PG, PG), F32), pltpu.VMEM((E, D), F32)],
        compiler_params=pltpu.CompilerParams(dimension_semantics=("arbitrary",), vmem_limit_bytes=VMEM_LIMIT),
    )(x1, tgt, mod1, nw1, fnw, pwin, pgrp, pscale, pwout, pb, pbt, pinv)


def _b3_out_bwd(dx1, o, g_all, gate, gw, wout, gpwout):
    def body(dx_ref, o_ref, z_ref, gate_ref, gw_ref, w_ref, gpwout_r, do_ref, dz_ref, gw_o, dgate_o, ggw_o, rpwout_o,
             acc, *rider):
        i = pl.program_id(0)
        bufs, sems = _rider2_split(rider, 1)
        _scatter_rider2(i, NT, 2, ("rows",), (gpwout_r,), (rpwout_o,), bufs, sems)

        @pl.when(i == 0)
        def _():
            acc[...] = jnp.zeros_like(acc)
            dgate_o[...] = jnp.zeros_like(dgate_o)
            ggw_o[...] = jnp.zeros_like(ggw_o)
            do_ref[...] = jnp.zeros_like(do_ref)
            dz_ref[...] = jnp.zeros_like(dz_ref)

        @pl.when(i > 0)
        def _():
            gw = gw_ref[...]
            z = z_ref[...]
            og, r, on, zs, sz = _gated_norm(_get_heads(o_ref), z, gw)
            ogb = og.astype(BF16)
            dx = dx_ref[...]
            dgate_o[...] += _colsum(dx * _dot(ogb, w_ref[...]))
            dy = (dx * gate_ref[...]).astype(BF16)
            for j in range(4):
                cs = slice(j * PG, (j + 1) * PG)
                acc[:, cs] += _dot_ta(ogb, dy[:, cs])
            dog = _dot_tb(dy, w_ref[...])
            dz_ref[...] = (dog * (on * gw) * (zs * (1.0 + z * (1.0 - zs)))).astype(BF16)
            dong = dog * sz
            ggw_o[...] += _colsum(dong * on)
            don = dong * gw
            do = _head_map(lambda dh, nh, rh: rh * (dh - nh * jnp.mean(dh * nh, axis=-1, keepdims=True)), don, on, r)
            _put_heads(do_ref, (), do.astype(BF16))

        @pl.when(i == NT - 1)
        def _():
            gw_o[...] = acc[...].astype(BF16)

    prev = lambda i: (jnp.maximum(i - 1, 0), 0)
    heads = pl.BlockSpec((HEADS, TM, DH), lambda i: (0, i, 0))
    return pl.pallas_call(
        body, name="b3_out_bwd", grid=(NT,),
        in_specs=[pl.BlockSpec((TM, D), prev), heads, pl.BlockSpec((TM, E), lambda i: (i, 4)),
                  VMEM_SPEC, VMEM_SPEC, VMEM_SPEC, HBM_SPEC],
        out_specs=[heads, pl.BlockSpec((TM, E), lambda i: (i, 0)), VMEM_SPEC, VMEM_SPEC, VMEM_SPEC, HBM_SPEC],
        out_shape=(_sds((HEADS, TT, DH), BF16), _sds((TT, E), BF16), _sds((E, D), BF16), _sds((1, D), F32), _sds((1, E), F32),
                   _sds((RS_SLOTS, SH_ROWS, D), BF16)),
        scratch_shapes=[pltpu.VMEM((E, D), F32)] + _rider2_scratch([(SH_ROWS, D)]),
        compiler_params=pltpu.CompilerParams(dimension_semantics=("arbitrary",), vmem_limit_bytes=VMEM_LIMIT),
    )(dx1, o, g_all, gate, gw, wout, gpwout)


def _gla_bwd(p0, p1, v_all, dec, do, mask01, gpwin, gpgrp):
    nch = TM // CHUNK
    n_steps = HEADS // GLA_HB

    def body(p0_ref, p1_ref, v_ref, dec_ref, do_ref, msk_ref, gpwin_r, gpgrp_r, d0_ref, d1_ref, dv_ref, dgl_ref, rpwin_o, rpgrp_o,
             ss_sc, dv_sc, ssem, rsem, lsem, *rider):
        _scatter_rider(pl.program_id(0), n_steps, ("grp",), (gpgrp_r,), (rpgrp_o,), ssem, rsem, lsem)
        bufs, sems = _rider2_split(rider, 1)
        _scatter_rider2(pl.program_id(0), n_steps, 1, ("major",), (gpwin_r,), (rpwin_o,), bufs, sems)

        lanes = [(d, hh) for d in (0, 1) for hh in range(GLA_HB)]
        zero = jnp.zeros((len(lanes), DH, DH), F32)
        dgl_ref[...] = jnp.zeros_like(dgl_ref)

        def p_of(d):
            return p1_ref if d else p0_ref

        def scan_step(i, n):
            where = [_scan_tile(i, d == 1) for d in (0, 1)]
            cis = [nch - 1 - n if d else n for d, _ in lanes]
            e_mid, e_rest = _chunk_rows(dec_ref, lanes, cis, where)

            def chunk(arr):
                return jnp.stack([arr[l, ci * CHUNK:(ci + 1) * CHUNK] for l, ci in enumerate(cis)])

            return where, cis, e_mid, e_rest, chunk

        def stacked(i, fn):
            where = [_scan_tile(i, d == 1) for d in (0, 1)]
            return jnp.stack([fn(d, hh, where[d][1]) for d, hh in lanes])

        def fwd_body(i, st):
            v = stacked(i, lambda d, hh, rows: v_ref[hh, rows, :])
            kg = stacked(i, lambda d, hh, rows: p_of(d)[1, hh, rows, :])
            for n in range(nch):
                _, _, e_mid, e_rest, chunk = scan_step(i, n)
                ss_sc[i * nch + n] = st
                st = st * (e_mid * e_rest) + _bdot_tn(chunk(v), chunk(kg)) * e_rest
            return st

        ss_sc[NT * nch] = lax.fori_loop(0, NT, fwd_body, zero)

        def bwd_body(ii, dst):
            i = NT - 1 - ii
            qg, kg = [stacked(i, lambda d, hh, rows, ty=ty: p_of(d)[ty, hh, rows, :]) for ty in range(2)]
            v = stacked(i, lambda d, hh, rows: v_ref[hh, rows, :])
            dob = stacked(i, lambda d, hh, rows: do_ref[hh, rows, :])
            msk = jnp.stack([msk_ref[d] for d, _ in lanes])
            a = (_bdot_nt(qg, kg) * msk).astype(BF16)
            da = (_bdot_nt(dob, v) * msk).astype(BF16)
            dqg = _bdot(da, kg)
            dkg = _bdot_tn(da, qg)
            dv_intra = _bdot_tn(a, dob)
            dv_l, dkg_l, dqg_l = ([[None] * nch for _ in lanes] for _ in range(3))
            for n in range(nch - 1, -1, -1):
                where, cis, e_mid, e_rest, chunk = scan_step(i, n)
                s_c, s_end = ss_sc[i * nch + n], ss_sc[i * nch + n + 1]
                dste = (dst * e_rest).astype(BF16)
                kg_c, v_c, dob_c = chunk(kg), chunk(v), chunk(dob)
                dv_c = chunk(dv_intra) + _bdot_nt(kg_c, dste)
                dkg_c = chunk(dkg) + _bdot(v_c, dste)
                dqg_c = chunk(dqg) + _bdot(dob_c, s_c * e_mid)
                dgl = jnp.sum(s_end * dst, axis=1, keepdims=True)
                for l, ((d, hh), ci) in enumerate(zip(lanes, cis)):
                    dv_l[l][ci], dkg_l[l][ci], dqg_l[l][ci] = dv_c[l], dkg_c[l], dqg_c[l]
                    dgl_ref[d, where[d][0], ci:ci + 1, hh * DH:(hh + 1) * DH] = dgl[l]
                dst = dst * (e_mid * e_rest) + _bdot_tn(dob_c, chunk(qg)) * e_mid
            where = [_scan_tile(i, d == 1) for d in (0, 1)]
            for l, (d, hh) in enumerate(lanes):
                rows = where[d][1]
                d_ref = d1_ref if d else d0_ref
                d_ref[0, hh, rows, :] = jnp.concatenate(dqg_l[l], axis=0).astype(BF16)
                d_ref[1, hh, rows, :] = jnp.concatenate(dkg_l[l], axis=0).astype(BF16)
                dv_sc[d, hh, rows, :] = jnp.concatenate(dv_l[l], axis=0).astype(BF16)
            return dst

        lax.fori_loop(0, NT, bwd_body, zero)
        dv_ref[...] = (dv_sc[0].astype(F32) + dv_sc[1].astype(F32)).astype(BF16)

    quad = pl.BlockSpec((2, GLA_HB, TT, DH), lambda h: (0, h, 0, 0))
    col = pl.BlockSpec((GLA_HB, TT, DH), lambda h: (h, 0, 0))
    chunkv = pl.BlockSpec((2, NT, 8, GLA_HB * DH), lambda h: (0, 0, 0, h))
    outs = (_sds((2, HEADS, TT, DH), BF16), _sds((2, HEADS, TT, DH), BF16), _sds((HEADS, TT, DH), BF16), _sds((2, NT, 8, E), F32),
            _sds((RS_SLOTS, D, SH_PWIN), BF16), _sds((NDEV, 4, SH_GRP, PG), BF16))
    return pl.pallas_call(
        body, name="gla_bwd", grid=(n_steps,),
        in_specs=[quad, quad, col, chunkv, col, pl.BlockSpec((2, TM, TM), lambda h: (0, 0, 0)), HBM_SPEC, HBM_SPEC],
        out_specs=[quad, quad, col, chunkv, HBM_SPEC, HBM_SPEC],
        out_shape=outs,
        scratch_shapes=[pltpu.VMEM((NT * nch + 1, 2 * GLA_HB, DH, DH), F32), pltpu.VMEM((2, GLA_HB, TT, DH), BF16)] + _rider_sems(1)
        + _rider2_scratch([(D, SH_PWIN)]),
        compiler_params=pltpu.CompilerParams(dimension_semantics=("arbitrary",), vmem_limit_bytes=VMEM_LIMIT),
    )(p0, p1, v_all, dec, do, mask01, gpwin, gpgrp)


TMB = 128


def _gla_post_bwd(g_all, d0, d1, dgl, dv, dz, lb, cum01, gwout):
    nch = TMB // CHUNK

    def body(g_ref, d0_ref, d1_ref, dgl_ref, dv_ref, dz_ref, lb_ref, cum_ref, gwout_r, dg_ref, dlb_ref, rwout_o, *rider):
        i = pl.program_id(0)
        bufs, sems = _rider2_split(rider, 1)
        _scatter_rider2(i, TT // TMB, 2, ("rows",), (gwout_r,), (rwout_o,), bufs, sems)

        @pl.when(i == 0)
        def _():
            dlb_ref[...] = jnp.zeros_like(dlb_ref)

        half = i & 1
        qpre = g_ref[:, 3 * E:4 * E]
        dqs_sum = None
        dpre = []
        for d, d_ref in ((0, d0_ref), (1, d1_ref)):
            rev = d == 1
            lbd = lb_ref[d:d + 1, :]
            t = _gla_gates(g_ref[:, d * E:(d + 1) * E], qpre, lbd, cum_ref[d, :TMB, :TMB], rev)
            dqs = _get_heads(d_ref, (0,)).astype(F32) * t["e_q"]
            dk = _get_heads(d_ref, (1,)).astype(F32) * t["e_k"]
            dg = t["qs"] * dqs - t["k"] * dk
            dgl8 = dgl_ref[d, 0]
            dgl_rows = [jnp.where(half == 0, dgl8[ci:ci + 1, :], dgl8[nch + ci:nch + ci + 1, :]) for ci in range(nch)]
            dgl_b = jnp.concatenate([jnp.broadcast_to(dgl_rows[ci], (CHUNK, E)) for ci in range(nch)], axis=0)
            pos = lax.broadcasted_iota(jnp.int32, (TMB, E), 0) & (CHUNK - 1)
            dg = dg + jnp.where(pos == (0 if rev else CHUNK - 1), dgl_b, 0.0)
            dlf = _dot01(cum_ref[1 - d, :TMB, :TMB], dg)
            df = dlf / t["f"] - dk
            sig = t["sig"]
            dpre.append((df * (1.0 - lbd) * sig * (1.0 - sig)).astype(BF16))
            dlb_ref[d:d + 1, :] += _colsum(df * (1.0 - sig))
            dqs_sum = dqs if dqs_sum is None else dqs_sum + dqs
            qsig = t["qsig"]
        dqpre = dqs_sum * (DH ** -0.5) * (qsig * (1.0 + qpre * (1.0 - qsig)))
        dg_ref[...] = jnp.concatenate([dpre[0], dpre[1], _get_heads(dv_ref), dqpre.astype(BF16), dz_ref[...]], axis=1)

    quad = pl.BlockSpec((2, HEADS, TMB, DH), lambda i: (0, 0, i, 0))
    tile = pl.BlockSpec((TMB, E), lambda i: (i, 0))
    return pl.pallas_call(
        body, name="gla_post_bwd", grid=(TT // TMB,),
        in_specs=[pl.BlockSpec((TMB, 4 * E), lambda i: (i, 0)), quad, quad,
                  pl.BlockSpec((2, 1, 8, E), lambda i: (0, i // 2, 0, 0)), pl.BlockSpec((HEADS, TMB, DH), lambda i: (0, i, 0)), tile,
                  VMEM_SPEC, VMEM_SPEC, HBM_SPEC],
        out_specs=[pl.BlockSpec((TMB, WIN_COLS), lambda i: (i, 0)), VMEM_SPEC, HBM_SPEC],
        out_shape=(_sds((TT, WIN_COLS), BF16), _sds((2, E), F32), _sds((RS_SLOTS, SH_ROWS, D), BF16)),
        scratch_shapes=_rider2_scratch([(SH_ROWS, D)]),
        compiler_params=pltpu.CompilerParams(dimension_semantics=("arbitrary",), vmem_limit_bytes=VMEM_LIMIT),
    )(g_all, d0, d1, dgl, dv, dz, lb, cum01, gwout)


WIN_SLOTS = 4


def _scatter_order(s, core):
    return (NDEV - 1 - s) ^ jnp.where((s >= 2) & (s <= 5) & ((s & 1) == core), 6, 0)


def _b1_in_bwd(idx1, ctx, x, dx1, dg, nw, msel, win):
    last_s = NDEV - 1
    half = D // 2

    def body(idx_ref, ctx_ref, x_ref, dx1_ref, dg_ref, nw_ref, m_ref, w_ref, gx_ref, rwin_o, dmx_o, dmc_o, gnw_o,
             hx_sc, dhx_sc, acc, sbuf, pbuf, rbuf, psend, precv, isend, irecv, dsend, drecv, sibsem, lsem):
        del idx_ref
        s, i = pl.program_id(0), pl.program_id(1)
        x, y, cc, idx = _mesh_pos()
        shift, scale = m_ref[0, 0:1, :], m_ref[0, 1:2, :]
        sibling = (x, y, 1 - cc)

        def partial(p):
            return pltpu.make_async_remote_copy(src_ref=sbuf.at[0], dst_ref=pbuf.at[p], send_sem=psend.at[p], recv_sem=precv.at[p],
                                                device_id=sibling, device_id_type=MESH)

        def chip_sum(p):
            return pltpu.make_async_remote_copy(src_ref=sbuf.at[1], dst_ref=rwin_o.at[2 + p], send_sem=isend.at[p], recv_sem=irecv.at[p],
                                                device_id=_peer(x, y, cc, 2 * (p + 1)), device_id_type=MESH)

        def relay(h):
            return pltpu.make_async_remote_copy(src_ref=sbuf.at[1, pl.ds(h * half, half), :], dst_ref=rbuf.at[h], send_sem=dsend.at[h],
                                                recv_sem=drecv.at[h], device_id=_peer(x, y, cc, 2 * (h + 1)), device_id_type=MESH)

        to_sibling = pltpu.make_async_remote_copy(src_ref=sbuf.at[0], dst_ref=rwin_o.at[1], send_sem=sibsem.at[0], recv_sem=sibsem.at[1],
                                                  device_id=sibling, device_id_type=MESH)
        own = pltpu.make_async_copy(sbuf.at[1], rwin_o.at[0], lsem)

        @pl.when((s == 0) & (i == 0))
        def _():
            for ref in (dmx_o, dmc_o, gnw_o):
                ref[...] = jnp.zeros_like(ref)

        @pl.when(s == 0)
        def _():
            hx, _, _, _ = _modulated(_ctx_or_x(i, ctx_ref, x_ref), nw_ref[...], shift, scale)
            hx_sc[i] = hx.astype(BF16)

        @pl.when(i == 0)
        def _():
            acc[...] = jnp.zeros_like(acc)

        dgb = dg_ref[...]
        hxb = hx_sc[i]
        for lo, hi in ((0, 256), (256, 512), (512, SH_WIN)):
            acc[:, lo:hi] += _dot_ta(hxb, dgb[:, lo:hi])
        part = _dot_tb(dgb, w_ref[0])

        @pl.when(s == 0)
        def _():
            dhx_sc[i] = part

        @pl.when(s > 0)
        def _():
            dhx_sc[i] += part

        done = i == NT - 1

        def hand_over(p, before):
            before.wait_send()
            sbuf[0] = acc[...].astype(BF16)
            partial(p).start()

        def send_chip_sum(p, before):
            for cp in before:
                cp.wait_send()
            partial(p).wait_recv()
            sbuf[1] = (acc[...] + pbuf[p].astype(F32)).astype(BF16)
            h = 1 - p
            rows = pl.ds(h * half, half)
            relay(h).wait_recv()
            sbuf[1, rows, :] = (acc[rows, :] + pbuf[p, rows, :].astype(F32) + rbuf[h].astype(F32)).astype(BF16)
            chip_sum(p).start()

        @pl.when(done & (s == 0))
        def _():
            sbuf[0] = acc[...].astype(BF16)
            partial(2).start()

        @pl.when(done & (s == 1))
        def _():
            partial(2).wait_recv()
            sbuf[1] = (acc[...] + pbuf[2].astype(F32)).astype(BF16)
            for h in range(2):
                relay(h).start()

        for core in range(2):
            @pl.when(done & (cc == core) & (s == 2))
            def _(core=core):
                hand_over(core, partial(2))

            @pl.when(done & (cc == core) & (s == 3))
            def _(core=core):
                send_chip_sum(1 - core, [relay(0), relay(1)])

            @pl.when(done & (cc == core) & (s == 4))
            def _(core=core):
                hand_over(1 - core, partial(core))

            @pl.when(done & (cc == core) & (s == 5))
            def _(core=core):
                send_chip_sum(core, [chip_sum(1 - core)])

            @pl.when(done & (cc == core) & (s == last_s - 1))
            def _(core=core):
                partial(1 - core).wait_send()
                sbuf[0] = acc[...].astype(BF16)
                to_sibling.start()

            @pl.when(done & (cc == core) & (s == last_s))
            def _(core=core):
                chip_sum(core).wait_send()
                sbuf[1] = acc[...].astype(BF16)
                own.start()

        @pl.when(s == last_s)
        def _():
            nw = nw_ref[...]
            _, r, xn, a = _modulated(_ctx_or_x(i, ctx_ref, x_ref), nw, shift, scale)
            dhx = dhx_sc[i]
            dsh, dsc = _colsum(dhx), _colsum(dhx * a)
            da = dhx * (1.0 + scale)
            gnw_o[...] += _colsum(da * xn)
            dxn = da * nw
            gx_ref[...] = dx1_ref[...] + r * (dxn - xn * jnp.mean(dxn * xn, axis=-1, keepdims=True))

            @pl.when(i == 0)
            def _():
                dmc_o[0:1, :] += dsh
                dmc_o[1:2, :] += dsc

            @pl.when(i > 0)
            def _():
                dmx_o[0:1, :] += dsh
                dmx_o[1:2, :] += dsc

        @pl.when((i == NT - 1) & (s == last_s))
        def _():
            to_sibling.wait_send()
            to_sibling.wait_recv()
            for p in range(2):
                chip_sum(p).wait_recv()
            own.wait()

    grid_spec = pltpu.PrefetchScalarGridSpec(
        num_scalar_prefetch=1, grid=(NDEV, NT),
        in_specs=[VMEM_SPEC,
                  pl.BlockSpec((TM, D), lambda s, i, ix: (jnp.where((s == 0) | (s == last_s), jnp.maximum(i - 1, 0), NTX - 1), 0)),
                  pl.BlockSpec((TM, D), lambda s, i, ix: (jnp.where(s == last_s, jnp.maximum(i - 1, 0), 0), 0)),
                  pl.BlockSpec((TM, SH_WIN), lambda s, i, ix: (i, ix[0] ^ _scatter_order(s, ix[0] & 1))), VMEM_SPEC,
                  pl.BlockSpec((1, 2, D), lambda s, i, ix: (jnp.minimum(i, 1), 0, 0)),
                  pl.BlockSpec((1, D, SH_WIN), lambda s, i, ix: (ix[0] ^ _scatter_order(s, ix[0] & 1), 0, 0))],
        out_specs=[pl.BlockSpec((TM, D), lambda s, i, ix: (jnp.where(s == last_s, jnp.maximum(i - 1, 0), 0), 0)),
                   HBM_SPEC, VMEM_SPEC, VMEM_SPEC, VMEM_SPEC],
        scratch_shapes=[pltpu.VMEM((NT, TM, D), BF16), pltpu.VMEM((NT, TM, D), F32), pltpu.VMEM((D, SH_WIN), F32),
                        pltpu.VMEM((2, D, SH_WIN), BF16), pltpu.VMEM((3, D, SH_WIN), BF16), pltpu.VMEM((2, half, SH_WIN), BF16),
                        pltpu.SemaphoreType.DMA((3,)), pltpu.SemaphoreType.DMA((3,)), pltpu.SemaphoreType.DMA((2,)),
                        pltpu.SemaphoreType.DMA((2,)), pltpu.SemaphoreType.DMA((2,)), pltpu.SemaphoreType.DMA((2,)),
                        pltpu.SemaphoreType.DMA((2,)), pltpu.SemaphoreType.DMA])
    return pl.pallas_call(
        body, name="b1_in_bwd", grid_spec=grid_spec,
        out_shape=(_sds((T, D), F32), _sds((WIN_SLOTS, D, SH_WIN), BF16), _sds((2, D), F32), _sds((2, D), F32), _sds((1, D), F32)),
        compiler_params=pltpu.CompilerParams(dimension_semantics=("arbitrary", "arbitrary"), vmem_limit_bytes=VMEM_LIMIT),
    )(idx1, ctx, x, dx1, dg, nw, msel, win)


def _reduce_small(pd, pv, cg, c_ctx, ada_w0):
    n_arr = 3

    def body(pd_r, pv_r, cg_r, cctx_r, ada_r, gada_o, gadab_o, gcctx_o, pvsum_o, loss_o,
             pd_all, pv_all, dsc_all, dsc_mine, ssem, rsem):
        x, y, cc, idx = _mesh_pos()
        srcs = [pd_r, pv_r, dsc_mine]
        dsts = [pd_all.at[idx], pv_all.at[idx], dsc_all.at[idx]]

        def remote(a, k):
            return pltpu.make_async_remote_copy(src_ref=srcs[a], dst_ref=dsts[a], send_sem=ssem.at[a, k], recv_sem=rsem.at[a, k],
                                                device_id=_peer(x, y, cc, k), device_id_type=MESH)

        first = [remote(a, k) for k in range(1, NDEV) for a in (0, 1)]
        for cp in first:
            cp.start()
        pd_all[idx] = pd_r[...]
        pv_all[idx] = pv_r[...]
        for k in range(1, NDEV):
            remote(0, k).wait_recv()
            remote(1, k).wait_recv()
        mine = [pd_all[s, :, pl.ds(idx, 1), :] for s in range(NDEV)]
        dmc = functools.reduce(lambda u, v: u + v, [m[2] for m in mine])
        rows = _stack_rows([cg_r[i] for i in range(NDEV)] + [cctx_r[...]])
        sc = (rows * _sigmoid(rows)).astype(BF16)
        gada_o[0] = _dot_ta(sc, _stack_rows([m[0] for m in mine] + [dmc]))
        gada_o[1] = _dot_ta(sc, _stack_rows([m[1] for m in mine]))
        dsc_mine[...] = _dot_tb(jnp.broadcast_to(dmc, (8, SH_ADA)), ada_r[...])[0:1, :]
        dsc_all[idx] = dsc_mine[...]
        second = [remote(2, k) for k in range(1, NDEV)]
        for cp in second:
            cp.start()
        tot = [functools.reduce(lambda u, v: u + v, [pd_all[s, l] for s in range(NDEV)]) for l in range(3)]
        gadab_o[0] = tot[0] + tot[2]
        gadab_o[1] = tot[1]
        pvs = functools.reduce(lambda u, v: u + v, [pv_all[s] for s in range(NDEV)])
        pvsum_o[...] = pvs
        loss_o[...] = jnp.broadcast_to(jnp.sum(pvs[:, PV_LOSS:PV_LOSS + D], axis=-1, keepdims=True) * (0.5 / D), (1, 128))
        for k in range(1, NDEV):
            remote(2, k).wait_recv()
        dsc = functools.reduce(lambda u, v: u + v, [dsc_all[s] for s in range(NDEV)])
        cx = cctx_r[...]
        sx = _sigmoid(cx)
        gcctx_o[...] = dsc * (sx * (1.0 + cx * (1.0 - sx)))
        for cp in first + second:
            cp.wait_send()

    outs = (_sds((2, D, SH_ADA), F32), _sds((2, NDEV, SH_ADA), F32), _sds((1, D), F32), _sds((1, PV_LEN), F32), _sds((1, 128), F32))
    return pl.pallas_call(
        body, name="reduce_small", out_shape=outs,
        in_specs=[VMEM_SPEC] * 5, out_specs=[VMEM_SPEC] * 5,
        scratch_shapes=[
            pltpu.VMEM((NDEV, 3, NDEV, SH_ADA), F32), pltpu.VMEM((NDEV, 1, PV_LEN), F32), pltpu.VMEM((NDEV, 1, D), F32),
            pltpu.VMEM((1, D), F32),
            pltpu.SemaphoreType.DMA((n_arr, NDEV)), pltpu.SemaphoreType.DMA((n_arr, NDEV)),
        ],
        compiler_params=pltpu.CompilerParams(vmem_limit_bytes=VMEM_LIMIT),
    )(pd, pv, cg, c_ctx, ada_w0)


PV_NW, PV_GNORM, PV_FINAL, PV_LB, PV_PSCALE, PV_LOSS, PV_LEN = 0, 2 * D, 3 * D, 4 * D, 6 * D, 7 * D, 8 * D


def _adamw(w, g, m, v):
    m = ADAM_B1 * m + (1.0 - ADAM_B1) * g
    v = ADAM_B2 * v + (1.0 - ADAM_B2) * (g * g)
    m_hat = m / (1.0 - ADAM_B1 ** ADAM_STEP)
    v_hat = v / (1.0 - ADAM_B2 ** ADAM_STEP)
    delta = -ADAM_LR * (m_hat / (jnp.sqrt(v_hat) + ADAM_EPS) + ADAM_WD * w)
    return delta, m, v


ADAM_STEPS = 8


def _adam_all(sharded, dense, small, lb_idx, lbv):
    ns, nd, nsm = len(sharded), len(dense), len(small)

    def body(*refs):
        it = iter(refs)
        sh_in = [[next(it) for _ in range(4)] for _ in range(ns)]
        de_in = [[next(it) for _ in range(4)] for _ in range(nd)]
        sm_in = [[next(it) for _ in range(4)] for _ in range(nsm)]
        lb_r = next(it)
        sh_out = [[next(it) for _ in range(4)] for _ in range(ns)]
        de_out = [[next(it) for _ in range(3)] for _ in range(nd)]
        sm_out = [[next(it) for _ in range(4)] for _ in range(nsm)]
        for (p, w, m, v), outs in zip(sh_in, sh_out):
            g = p[0].astype(F32)
            for s in range(1, p.shape[0]):
                g = g + p[s].astype(F32)
            d, mn, vn = _adamw(w[...], g, m[...], v[...])
            outs[0][...], outs[1][...], outs[2][...], outs[3][...] = g, d, mn, vn
        for (g, w, m, v), outs in zip(de_in, de_out):
            d, mn, vn = _adamw(w[...], g[...], m[...], v[...])
            outs[0][...], outs[1][...], outs[2][...] = d, mn, vn

        @pl.when(pl.program_id(0) == 0)
        def _():
            for j, ((g, w, m, v), outs) in enumerate(zip(sm_in, sm_out)):
                gj = g[...]
                if j == lb_idx:
                    gj = gj * lb_r[...] * (1.0 - lb_r[...])
                d, mn, vn = _adamw(w[...], gj, m[...], v[...])
                outs[0][...], outs[1][...], outs[2][...], outs[3][...] = gj, d, mn, vn

    def tile(a):
        return pl.BlockSpec((a.shape[0] // ADAM_STEPS, a.shape[1]), lambda i: (i, 0))

    in_specs, out_specs, out_shape, args = [], [], [], []
    for p, w, m, v in sharded:
        in_specs += [pl.BlockSpec((p.shape[0], p.shape[1] // ADAM_STEPS, p.shape[2]), lambda i: (0, i, 0))] + [tile(w)] * 3
        args += [p, w, m, v]
    for g, w, m, v in dense:
        in_specs += [tile(w)] * 4
        args += [g, w, m, v]
    for g, w, m, v in small:
        in_specs += [VMEM_SPEC] * 4
        args += [g, w, m, v]
    in_specs.append(VMEM_SPEC)
    args.append(lbv)
    for _, w, _, _ in sharded:
        out_specs += [tile(w)] * 4
        out_shape += [_sds(w.shape, F32)] * 4
    for _, w, _, _ in dense:
        out_specs += [tile(w)] * 3
        out_shape += [_sds(w.shape, F32)] * 3
    for _, w, _, _ in small:
        out_specs += [VMEM_SPEC] * 4
        out_shape += [_sds(w.shape, F32)] * 4
    res = pl.pallas_call(body, name="adam_all", grid=(ADAM_STEPS,), in_specs=in_specs, out_specs=out_specs, out_shape=tuple(out_shape),
                         compiler_params=pltpu.CompilerParams(dimension_semantics=("arbitrary",), vmem_limit_bytes=VMEM_LIMIT))(*args)
    it = iter(res)
    return ([tuple(next(it) for _ in range(4)) for _ in range(ns)], [tuple(next(it) for _ in range(3)) for _ in range(nd)],
            [tuple(next(it) for _ in range(4)) for _ in range(nsm)])


def kernel(x, c, ctx, c_ctx, ada_w, ada_b, norm_w, hgrn_w_in, hgrn_lb_logits, hgrn_gnorm_w, hgrn_w_out, pool_w_in, pool_w_grp, pool_scale, pool_w_out, final_norm_w, loss_target, m_c_ctx, m_ada_w, m_ada_b, m_norm_w, m_hgrn_w_in, m_hgrn_lb_logits, m_hgrn_gnorm_w, m_hgrn_w_out, m_pool_w_in, m_pool_w_grp, m_pool_scale, m_pool_w_out, m_final_norm_w, v_c_ctx, v_ada_w, v_ada_b, v_norm_w, v_hgrn_w_in, v_hgrn_lb_logits, v_hgrn_gnorm_w, v_hgrn_w_out, v_pool_w_in, v_pool_w_grp, v_pool_scale, v_pool_w_out, v_final_norm_w):
    idx = 4 * lax.axis_index("x") + 2 * lax.axis_index("y") + lax.axis_index("c")
    cctx2 = c_ctx.reshape(1, D)
    cum01, mask01 = _gla_consts()
    pb, pbt, pinv = _pool_consts()

    idx1 = idx.reshape(1).astype(jnp.int32)
    nw0, nw1 = norm_w[0:1], norm_w[1:2]
    fnw = final_norm_w.reshape(1, D)
    g_all, win, s_wout, s_pwin, s_pgrp, s_pwout, lbl_g, ps_g, cg, mod0, mod1, modc = _f1_gather_matmul(
        idx1, ctx[0], x[0], nw0, hgrn_w_in[0], hgrn_w_out[0], pool_w_in[0], pool_w_grp[0], pool_w_out[0], hgrn_lb_logits[0],
        pool_scale, c, cctx2, ada_w, ada_b)
    lb = jax.nn.sigmoid(jnp.transpose(lbl_g, (1, 0, 2)).reshape(2, E))
    pscale = ps_g.reshape(1, E)
    msel = jnp.stack([modc[:2], mod0[:2]])
    p0, p1, v_all, dec, wout, pgrp = _gla_prep(g_all, lb, cum01, s_wout, s_pgrp)
    o, pwin, pwout = _gla_fwd(p0, p1, v_all, dec, mask01, s_pwin, s_pwout)
    x1 = _f3_out(o, g_all, x[0], mod0[2:3], hgrn_gnorm_w, wout)
    dx1, gpwin, gpgrp, gpwout, dmod1, gnw1, gfw, gps, lossv = _pool_layer(
        x1, loss_target[0], mod1, nw1, fnw, pwin, pgrp, pscale, pwout, pb, pbt, pinv)
    do, dz, gwout, dgate0, ggw, rpwout = _b3_out_bwd(dx1, o, g_all, mod0[2:3], hgrn_gnorm_w, wout, gpwout)
    d0, d1, dv, dgl, rpwin, rpgrp = _gla_bwd(p0, p1, v_all, dec, do, mask01, gpwin, gpgrp)
    dg, dlb, rwout = _gla_post_bwd(g_all, d0, d1, dgl, dv, dz, lb, cum01, gwout)
    grad_x, rwin, dmx, dmc, gnw0 = _b1_in_bwd(idx1, ctx[0], x[0], dx1, dg, nw0, msel, win)

    dmod0 = jnp.concatenate([dmx, dgate0], axis=0)
    dmodc = jnp.concatenate([dmc, jnp.zeros((1, D), F32)], axis=0)
    pd = jnp.stack([dmod0, dmod1, dmodc]).reshape(3, NDEV, SH_ADA)
    pv = jnp.concatenate([gnw0, gnw1, ggw, gfw, dlb.reshape(1, 2 * E), gps, lossv], axis=1)
    g_ada, g_adab, g_cctx, pvsum, loss128 = _reduce_small(pd, pv, cg, cctx2, ada_w[0])

    g2 = (4 * SH_GRP, PG)
    sharded_names = ["hgrn_w_in", "hgrn_w_out", "pool_w_in", "pool_w_grp", "pool_w_out"]
    sharded = [(rwin, hgrn_w_in[0], m_hgrn_w_in[0], v_hgrn_w_in[0]),
               (rwout, hgrn_w_out[0], m_hgrn_w_out[0], v_hgrn_w_out[0]),
               (rpwin, pool_w_in[0], m_pool_w_in[0], v_pool_w_in[0]),
               (rpgrp.reshape((NDEV,) + g2), pool_w_grp[0].reshape(g2), m_pool_w_grp[0].reshape(g2), v_pool_w_grp[0].reshape(g2)),
               (rpwout, pool_w_out[0], m_pool_w_out[0], v_pool_w_out[0])]
    a2 = (2 * D, SH_ADA)
    g_ada2 = g_ada.reshape(a2)
    dense = [(g_ada2, ada_w.reshape(a2), m_ada_w.reshape(a2), v_ada_w.reshape(a2))]
    lb_me = lax.dynamic_slice_in_dim(lb, idx * DH, DH, axis=1)
    small_names = ["c_ctx", "ada_b", "norm_w", "hgrn_lb_logits", "hgrn_gnorm_w", "pool_scale", "final_norm_w"]
    small = [(g_cctx, cctx2, m_c_ctx.reshape(1, D), v_c_ctx.reshape(1, D)),
             (g_adab.reshape(2, 3 * D), ada_b, m_ada_b, v_ada_b),
             (pvsum[:, PV_NW:PV_NW + 2 * D].reshape(2, D), norm_w, m_norm_w, v_norm_w),
             (lax.dynamic_slice_in_dim(pvsum[:, PV_LB:PV_LB + 2 * E].reshape(2, E), idx * DH, DH, axis=1),
              hgrn_lb_logits[0], m_hgrn_lb_logits[0], v_hgrn_lb_logits[0]),
             (pvsum[:, PV_GNORM:PV_GNORM + E], hgrn_gnorm_w, m_hgrn_gnorm_w, v_hgrn_gnorm_w),
             (lax.dynamic_slice_in_dim(pvsum[:, PV_PSCALE:PV_PSCALE + E], idx * DH, DH, axis=1), pool_scale, m_pool_scale, v_pool_scale),
             (pvsum[:, PV_FINAL:PV_FINAL + D], fnw, m_final_norm_w.reshape(1, D), v_final_norm_w.reshape(1, D))]
    r_sharded, r_dense, r_small = _adam_all(sharded, dense, small, 3, lb_me)
    out = dict(zip(sharded_names, r_sharded))
    out["ada_w"] = (g_ada2,) + r_dense[0]
    out.update(zip(small_names, r_small))

    shapes = {"c_ctx": (D,), "ada_w": (2, D, SH_ADA), "ada_b": (2, 3 * D), "norm_w": (2, D), "hgrn_w_in": (1, D, SH_WIN),
              "hgrn_lb_logits": (1, 2, DH), "hgrn_gnorm_w": (1, E), "hgrn_w_out": (1, SH_ROWS, D), "pool_w_in": (1, D, SH_PWIN),
              "pool_w_grp": (1, 4, SH_GRP, PG), "pool_scale": (1, DH), "pool_w_out": (1, SH_ROWS, D), "final_norm_w": (D,)}
    order = ["c_ctx", "ada_w", "ada_b", "norm_w", "hgrn_w_in", "hgrn_lb_logits", "hgrn_gnorm_w", "hgrn_w_out", "pool_w_in",
             "pool_w_grp", "pool_scale", "pool_w_out", "final_norm_w"]
    flat = [out[name][q].reshape(shapes[name]) for q in range(4) for name in order]
    return (loss128[0, 0], grad_x[None], *flat)
```

```python
import functools

import numpy as np
import jax
import jax.numpy as jnp
from jax import lax
from jax.experimental import pallas as pl
from jax.experimental.pallas import tpu as pltpu

F32 = jnp.float32
BF16 = jnp.bfloat16

D = 1024
E = 1024
HEADS = 8
DH = 128
CHUNK = 64
T = 2048
TC = 256
TT = T + TC
TM = 256
NT = TT // TM
NTX = T // TM
NDEV = 8
GRID_W = 64
POOL_WINDOWS = (2, 4, 8, 16)
PG = 256
EPS = 1e-6
WIN_COLS = 5 * E
SH_WIN = WIN_COLS // NDEV
SH_PWIN = 2 * E // NDEV
SH_ROWS = E // NDEV
SH_GRP = PG // NDEV
SH_ADA = 3 * D // NDEV
VMEM_LIMIT = 56 * 1024 * 1024

ADAM_LR, ADAM_B1, ADAM_B2, ADAM_EPS, ADAM_WD, ADAM_STEP = 0.001, 0.9, 0.999, 1e-08, 0.01, 10

MESH = pl.DeviceIdType.MESH
VMEM_SPEC = pl.BlockSpec(memory_space=pltpu.VMEM)
HBM_SPEC = pl.BlockSpec(memory_space=pltpu.HBM)


def _sds(shape, dtype):
    return jax.ShapeDtypeStruct(shape, dtype)


def _bf(a):
    return a if a.dtype == BF16 else a.astype(BF16)


def _dot(a, b):
    return lax.dot_general(_bf(a), _bf(b), (((1,), (0,)), ((), ())), preferred_element_type=F32)


def _dot_tb(a, b):
    return lax.dot_general(_bf(a), _bf(b), (((1,), (1,)), ((), ())), preferred_element_type=F32)


def _dot_ta(a, b):
    return lax.dot_general(_bf(a), _bf(b), (((0,), (0,)), ((), ())), preferred_element_type=F32)


def _bdot(a, b):
    return lax.dot_general(_bf(a), _bf(b), (((2,), (1,)), ((0,), (0,))), preferred_element_type=F32)


def _bdot_nt(a, b):
    return lax.dot_general(_bf(a), _bf(b), (((2,), (2,)), ((0,), (0,))), preferred_element_type=F32)


def _bdot_tn(a, b):
    return lax.dot_general(_bf(a), _bf(b), (((1,), (1,)), ((0,), (0,))), preferred_element_type=F32)


def _dot01(m01, x):
    hi = x.astype(BF16)
    lo = (x - hi.astype(F32)).astype(BF16)
    return _dot(m01, hi) + _dot(m01, lo)


def _rstd(x):
    return lax.rsqrt(jnp.mean(x * x, axis=-1, keepdims=True) + EPS)


def _sigmoid(x):
    return jax.nn.sigmoid(x)


def _colsum(a):
    return jnp.sum(a, axis=0, keepdims=True)


def _stack_rows(rows):
    n = rows[0].shape[-1]
    rid = lax.broadcasted_iota(jnp.int32, (16, n), 0)
    out = jnp.zeros((16, n), F32)
    for i, r in enumerate(rows):
        out = jnp.where(rid == i, r, out)
    return out


def _head_map(fn, *arrs):
    outs = [fn(*[a[:, h * DH:(h + 1) * DH] for a in arrs]) for h in range(HEADS)]
    return jnp.concatenate(outs, axis=1)


def _gla_consts():
    r = np.arange(TM)[:, None]
    c = np.arange(TM)[None, :]
    same = (r // CHUNK) == (c // CHUNK)
    tril = same & (c <= r)
    triu = same & (c >= r)
    m = np.stack([tril, triu]).astype(np.float32)
    return jnp.asarray(m, BF16), jnp.asarray(m, F32)


def _pool_consts():
    r = np.arange(TM)[:, None]
    c = np.arange(TM)[None, :]
    same = (r // GRID_W) == (c // GRID_W)
    rp, cp = r % GRID_W, c % GRID_W
    bs, inv = [], []
    for w in POOL_WINDOWS:
        lo = np.clip(rp - w // 2, 0, GRID_W)
        hi = np.clip(rp - w // 2 + w, 0, GRID_W)
        bs.append(same & (cp >= lo) & (cp < hi))
        inv.append(1.0 / (hi - lo).astype(np.float32))
    b = np.stack(bs).astype(np.float32)
    bt = np.transpose(b, (0, 2, 1))
    return jnp.asarray(b, BF16), jnp.asarray(bt, BF16), jnp.asarray(np.stack(inv), F32)


def _mesh_pos():
    x, y, c = lax.axis_index("x"), lax.axis_index("y"), lax.axis_index("c")
    return x, y, c, 4 * x + 2 * y + c


def _peer(x, y, c, k):
    return (x ^ ((k >> 2) & 1), y ^ ((k >> 1) & 1), c ^ (k & 1))


def _small_gathers(refs, ssem, rsem):
    lb_r, ps_r, c_r, cctx_r, ada_r, adab_r, lb_o, ps_o, cg_o, mod_o, lb_out, ps_out, cg_out, mod0_o, mod1_o, modc_o = refs
    x, y, cc, idx = _mesh_pos()
    srcs = [lb_r, ps_r, c_r, mod_o.at[idx]]
    mine = [lb_o.at[idx], ps_o.at[idx], cg_o.at[idx], mod_o.at[idx]]

    def remote(a, k):
        return pltpu.make_async_remote_copy(src_ref=srcs[a], dst_ref=mine[a], send_sem=ssem.at[a, k], recv_sem=rsem.at[a, k],
                                            device_id=_peer(x, y, cc, k), device_id_type=MESH)

    first = [remote(a, k) for k in range(1, NDEV) for a in (2, 0, 1)]
    for cp in first:
        cp.start()
    lb_o[idx] = lb_r[...]
    ps_o[idx] = ps_r[...]
    cg_o[idx] = c_r[...]
    for k in range(1, NDEV):
        remote(2, k).wait_recv()
    rows = _stack_rows([cg_o[i] for i in range(NDEV)] + [cctx_r[...]])
    sc = rows * _sigmoid(rows)
    for l in range(2):
        mod_o[idx, l] = _dot(sc, ada_r[l])
    second = [remote(3, k) for k in range(1, NDEV)]
    for cp in second:
        cp.start()
    for k in range(1, NDEV):
        remote(3, k).wait_recv()

    def mod_rows(l, row):
        full = jnp.concatenate([mod_o[s, l, row, :] for s in range(NDEV)], axis=1) + adab_r[l:l + 1, :]
        return [full[:, j * D:(j + 1) * D] for j in range(3)]

    me = pl.ds(idx, 1)
    for out, parts in ((mod0_o, mod_rows(0, me)), (mod1_o, mod_rows(1, me)), (modc_o, mod_rows(0, slice(NDEV, NDEV + 1)))):
        for j in range(3):
            out[j:j + 1, :] = parts[j]
    for cp in first + second:
        cp.wait_send()
    for k in range(1, NDEV):
        for a in (0, 1):
            remote(a, k).wait_recv()
    lb_out[...] = lb_o[...]
    ps_out[...] = ps_o[...]
    cg_out[...] = cg_o[...]


def _gather_order(s, core):
    k = jnp.where(s == 2, 4, jnp.where(s == 4, 2, s))
    return k ^ jnp.where((core == 1) & (s >= 2) & (s <= 5), 6, 0)


GATHER_ISSUE = (1, 2, 4, 3, 5, 6, 7)
GATHER_ICI = (2, 4, 6)
GATHER_DIRECT = (1,) + GATHER_ICI
GLA_HB = 2
RS_SLOTS = 5


def _shard_of(kind, ref, i):
    if kind == "rows":
        return ref.at[pl.ds(pl.multiple_of(i * SH_ROWS, SH_ROWS), SH_ROWS), :]
    if kind == "major":
        return ref.at[i]
    assert kind == "grp"
    return ref.at[:, pl.ds(pl.multiple_of(i * SH_GRP, SH_GRP), SH_GRP), :]


def _gather_rider(step, n_steps, forward_at, kinds, srcs, outs, ssem, rsem, lsem):
    x, y, cc, idx = _mesh_pos()
    arrays = range(len(kinds))
    mine = [_shard_of(kinds[a], outs[a], idx) for a in arrays]

    def remote(a, k):
        return pltpu.make_async_remote_copy(src_ref=srcs[a], dst_ref=mine[a], send_sem=ssem.at[a, k], recv_sem=rsem.at[a, k],
                                            device_id=_peer(x, y, cc, k), device_id_type=MESH)

    def forward(a, k):
        blk = _shard_of(kinds[a], outs[a], idx ^ k)
        return pltpu.make_async_remote_copy(src_ref=blk, dst_ref=blk, send_sem=ssem.at[a, k ^ 1], recv_sem=rsem.at[a, k ^ 1],
                                            device_id=(x, y, 1 - cc), device_id_type=MESH)

    copies = [remote(a, k) for k in GATHER_DIRECT for a in arrays]
    passed = [forward(a, k) for k in GATHER_ICI for a in arrays]
    local = [pltpu.make_async_copy(srcs[a], mine[a], lsem.at[a]) for a in arrays]

    @pl.when(step == 0)
    def _():
        for cp in copies + local:
            cp.start()

    @pl.when(step == forward_at)
    def _():
        for k in GATHER_ICI:
            for a in arrays:
                remote(a, k).wait_recv()
                forward(a, k).start()

    @pl.when(step == n_steps - 1)
    def _():
        for cp in copies + passed:
            cp.wait_send()
        for a in arrays:
            remote(a, 1).wait_recv()
        for cp in passed:
            cp.wait_recv()
        for cp in local:
            cp.wait()


def _scatter_rider(step, n_steps, kinds, grads, slots, ssem, rsem, lsem):
    x, y, cc, idx = _mesh_pos()
    arrays = range(len(kinds))
    dsts = [slots[a].at[idx] for a in arrays]

    def remote(a, k):
        px, py, pc = _peer(x, y, cc, k)
        return pltpu.make_async_remote_copy(src_ref=_shard_of(kinds[a], grads[a], 4 * px + 2 * py + pc), dst_ref=dsts[a],
                                            send_sem=ssem.at[a, k], recv_sem=rsem.at[a, k], device_id=(px, py, pc), device_id_type=MESH)

    copies = [remote(a, k) for k in GATHER_ISSUE for a in arrays]
    local = [pltpu.make_async_copy(_shard_of(kinds[a], grads[a], idx), dsts[a], lsem.at[a]) for a in arrays]

    @pl.when(step == 0)
    def _():
        for cp in copies + local:
            cp.start()

    @pl.when(step == n_steps - 1)
    def _():
        for cp in copies:
            cp.wait_send()
        for cp in copies:
            cp.wait_recv()
        for cp in local:
            cp.wait()


def _rider_sems(n):
    return [pltpu.SemaphoreType.DMA((n, NDEV)), pltpu.SemaphoreType.DMA((n, NDEV)), pltpu.SemaphoreType.DMA((n,))]


def _scatter_rider2(step, n_steps, add_at, kinds, grads, slots, bufs, sems):
    x, y, cc, idx = _mesh_pos()
    sibling = (x, y, 1 - cc)
    arrays = range(len(kinds))
    psend, precv, isend, irecv, lown, sibsem, lself = sems

    def mine(a, i):
        return _shard_of(kinds[a], grads[a], i)

    def partial(a, p):
        return pltpu.make_async_remote_copy(src_ref=mine(a, idx ^ (2 * (p + 1)) ^ 1), dst_ref=bufs[a][1].at[p], send_sem=psend.at[a, p],
                                            recv_sem=precv.at[a, p], device_id=sibling, device_id_type=MESH)

    def load(a, p):
        return pltpu.make_async_copy(mine(a, idx ^ (2 * (p + 1))), bufs[a][0].at[p], lown.at[a, p])

    def chip_sum(a, p):
        return pltpu.make_async_remote_copy(src_ref=bufs[a][0].at[p], dst_ref=slots[a].at[2 + p], send_sem=isend.at[a, p],
                                            recv_sem=irecv.at[a, p], device_id=_peer(x, y, cc, 2 * (p + 1)), device_id_type=MESH)

    def to_sibling(a):
        return pltpu.make_async_remote_copy(src_ref=mine(a, idx ^ 1), dst_ref=slots[a].at[1], send_sem=sibsem.at[a, 0],
                                            recv_sem=sibsem.at[a, 1], device_id=sibling, device_id_type=MESH)

    def own(a):
        return pltpu.make_async_copy(mine(a, idx), slots[a].at[0], lself.at[a, 0])

    @pl.when(step == 0)
    def _():
        for a in arrays:
            for p in range(3):
                partial(a, p).start()
                load(a, p).start()
            to_sibling(a).start()
            own(a).start()

    @pl.when(step == add_at)
    def _():
        for a in arrays:
            for p in range(3):
                partial(a, p).wait_recv()
                load(a, p).wait()
                bufs[a][0][p] = (bufs[a][0][p].astype(F32) + bufs[a][1][p].astype(F32)).astype(BF16)
                chip_sum(a, p).start()

    @pl.when(step == n_steps - 1)
    def _():
        for a in arrays:
            for p in range(3):
                partial(a, p).wait_send()
                chip_sum(a, p).wait_send()
                chip_sum(a, p).wait_recv()
            to_sibling(a).wait_send()
            to_sibling(a).wait_recv()
            own(a).wait()


def _rider2_scratch(blocks):
    n = len(blocks)
    bufs = [pltpu.VMEM((3,) + tuple(b), BF16) for b in blocks for _ in range(2)]
    return bufs + [pltpu.SemaphoreType.DMA((n, 3)) for _ in range(5)] + [pltpu.SemaphoreType.DMA((n, 2)), pltpu.SemaphoreType.DMA((n, 1))]


def _rider2_split(refs, n):
    refs = list(refs)
    return [tuple(refs[2 * a:2 * a + 2]) for a in range(n)], tuple(refs[2 * n:2 * n + 7])


def _modulated(x, nw, shift, scale):
    r = _rstd(x)
    xn = x * r
    a = xn * nw
    return a * (1.0 + scale) + shift, r, xn, a


def _ctx_or_x(i, ctx_ref, x_ref):
    return jnp.where(i == 0, ctx_ref[...], x_ref[...])


def _f1_gather_matmul(idx1, ctx, x, nw, w_in, w_out, pw_in, pgrp, pw_out, lb_l, pscale, c, c_ctx, ada_w, ada_b):
    def body(idx_ref, ctx_ref, x_ref, nw_ref, win_r, wout_r, pwin_r, pgrp_r, pwout_r, lb_r, ps_r, c_r, cctx_r, ada_r, adab_r,
             g_ref, win_o, s_wout, s_pwin, s_pgrp, s_pwout, lb_o, ps_o, cg_o, mod0_o, mod1_o, modc_o,
             wslot, hx_sc, lb_g, ps_g, cg_g, mod_g, ssem, rsem, osem, dsem, sm_ssem, sm_rsem):
        del idx_ref
        s, i = pl.program_id(0), pl.program_id(1)
        x, y, cc, idx = _mesh_pos()
        k = _gather_order(s, cc)
        j = idx ^ k
        first = 4 - 2 * cc

        def remote(kk):
            return pltpu.make_async_remote_copy(src_ref=wslot.at[idx], dst_ref=wslot.at[idx], send_sem=ssem.at[kk], recv_sem=rsem.at[kk],
                                                device_id=_peer(x, y, cc, kk), device_id_type=MESH)

        def forward(kk):
            jj = idx ^ kk
            return pltpu.make_async_remote_copy(src_ref=wslot.at[jj], dst_ref=wslot.at[jj], send_sem=ssem.at[kk ^ 1],
                                                recv_sem=rsem.at[kk ^ 1], device_id=(x, y, 1 - cc), device_id_type=MESH)

        def relay(h):
            blk = wslot.at[idx ^ (4 >> h), pl.ds(h * (D // 2), D // 2), :]
            return pltpu.make_async_remote_copy(src_ref=blk, dst_ref=blk, send_sem=dsem.at[0, h], recv_sem=dsem.at[1, h],
                                                device_id=_peer(x, y, cc, 2 << h), device_id_type=MESH)

        def to_hbm(jj, kk):
            return pltpu.make_async_copy(wslot.at[jj], win_o.at[jj], osem.at[kk])

        @pl.when((s == 0) & (i == 0))
        def _():
            _small_gathers((lb_r, ps_r, c_r, cctx_r, ada_r, adab_r, lb_g, ps_g, cg_g, mod_g, lb_o, ps_o, cg_o, mod0_o, mod1_o, modc_o),
                           sm_ssem, sm_rsem)
            wslot[idx] = win_r[...].astype(BF16)
            remote(1).start()
            remote(first).start()
            remote(6 - first).start()
            s_wout[...] = wout_r[...].astype(BF16)
            s_pwin[...] = pwin_r[...].astype(BF16)
            s_pgrp[...] = pgrp_r[...].astype(BF16)
            s_pwout[...] = pwout_r[...].astype(BF16)

        @pl.when(s == 0)
        def _():
            shift = jnp.where(i == 0, modc_o[0:1, :], mod0_o[0:1, :])
            scale = jnp.where(i == 0, modc_o[1:2, :], mod0_o[1:2, :])
            hx, _, _, _ = _modulated(_ctx_or_x(i, ctx_ref, x_ref), nw_ref[...], shift, scale)
            hx_sc[i] = hx.astype(BF16)

        @pl.when((s > 0) & (i == 0) & (k != 6))
        def _():
            remote(k).wait_recv()

            @pl.when((k & 1) == 0)
            def _():
                forward(k).start()

            for h in range(2):
                @pl.when(k == 4 >> h)
                def _():
                    relay(h).start()

        @pl.when((i == 0) & (k == 6))
        def _():
            for h in range(2):
                relay(h).wait_recv()
            forward(6).start()

        @pl.when(i == 0)
        def _():
            to_hbm(j, k).start()

        g_ref[...] = jnp.dot(hx_sc[i], wslot[j], preferred_element_type=F32)

        @pl.when((s == NDEV - 1) & (i == NT - 1))
        def _():
            for kk in (1, 2, 4):
                remote(kk).wait_send()
            for kk in GATHER_ICI:
                forward(kk).wait_send()
            for h in range(2):
                relay(h).wait_send()
            for kk in range(NDEV):
                to_hbm(idx ^ kk, kk).wait()

    grid_spec = pltpu.PrefetchScalarGridSpec(
        num_scalar_prefetch=1, grid=(NDEV, NT),
        in_specs=[VMEM_SPEC, pl.BlockSpec((TM, D), lambda s, i, ix: (jnp.where(s == 0, jnp.maximum(i - 1, 0), NTX - 1), 0))]
        + [VMEM_SPEC] * 12,
        out_specs=[pl.BlockSpec((TM, SH_WIN), lambda s, i, ix: (i, ix[0] ^ _gather_order(s, ix[0] & 1))), HBM_SPEC] + [VMEM_SPEC] * 10,
        scratch_shapes=[pltpu.VMEM((NDEV, D, SH_WIN), BF16), pltpu.VMEM((NT, TM, D), BF16),
                        pltpu.VMEM((NDEV, 2, DH), F32), pltpu.VMEM((NDEV, 1, DH), F32), pltpu.VMEM((NDEV, 1, D), F32),
                        pltpu.VMEM((NDEV, 2, 16, SH_ADA), F32),
                        pltpu.SemaphoreType.DMA((NDEV,)), pltpu.SemaphoreType.DMA((NDEV,)), pltpu.SemaphoreType.DMA((NDEV,)),
                        pltpu.SemaphoreType.DMA((2, 2)),
                        pltpu.SemaphoreType.DMA((4, NDEV)), pltpu.SemaphoreType.DMA((4, NDEV))])
    outs = (_sds((TT, WIN_COLS), F32), _sds((NDEV, D, SH_WIN), BF16),
            _sds((SH_ROWS, D), BF16), _sds((D, SH_PWIN), BF16), _sds((4, SH_GRP, PG), BF16), _sds((SH_ROWS, D), BF16),
            _sds((NDEV, 2, DH), F32), _sds((NDEV, 1, DH), F32), _sds((NDEV, 1, D), F32),
            _sds((3, D), F32), _sds((3, D), F32), _sds((3, D), F32))
    return pl.pallas_call(
        body, name="f1_gather_matmul", grid_spec=grid_spec, out_shape=outs,
        compiler_params=pltpu.CompilerParams(dimension_semantics=("arbitrary", "arbitrary"), vmem_limit_bytes=VMEM_LIMIT),
    )(idx1, ctx, x, nw, w_in, w_out, pw_in, pgrp, pw_out, lb_l, pscale, c, c_ctx, ada_w, ada_b)


def _gla_gates(pre, qpre, lbd, cum, rev):
    rows, n = pre.shape
    nch = rows // CHUNK
    sig = _sigmoid(pre)
    f = lbd + (1.0 - lbd) * sig
    k = 1.0 - f
    g = _dot01(cum, jnp.log(f))
    g3 = g.reshape(nch, CHUNK, n)
    last = 0 if rev else CHUNK - 1
    mid = CHUNK // 2 if rev else CHUNK // 2 - 1
    gl1, gm1 = g3[:, last:last + 1, :], g3[:, mid:mid + 1, :]

    def bc(a):
        return jnp.broadcast_to(a, g3.shape).reshape(rows, n)

    gm = bc(gm1)
    e_q, e_k = jnp.exp(g - gm), jnp.exp(gm - g)
    qsig = _sigmoid(qpre)
    qs = qpre * qsig * (DH ** -0.5)
    return dict(sig=sig, f=f, k=k, qsig=qsig, qs=qs, e_q=e_q, e_k=e_k,
                e_mid=[jnp.exp(gm1[ci]) for ci in range(nch)], e_rest=[jnp.exp(gl1[ci] - gm1[ci]) for ci in range(nch)])


def _put_heads(ref, lead, arr):
    for h in range(HEADS):
        ref[lead + (h,)] = arr[:, h * DH:(h + 1) * DH]


def _get_heads(ref, lead=()):
    return jnp.concatenate([ref[lead + (h,)] for h in range(HEADS)], axis=1)


def _gla_prep(g_all, lb, cum01, s_wout, s_pgrp):
    nch = TM // CHUNK

    def body(g_ref, lb_ref, cum_ref, swout_r, spgrp_r, p0_ref, p1_ref, v_ref, dec_ref, wout_o, pgrp_o, ssem, rsem, lsem):
        _gather_rider(pl.program_id(0), NT, NT - 1, ("rows", "grp"), (swout_r, spgrp_r), (wout_o, pgrp_o), ssem, rsem, lsem)
        qpre = g_ref[:, 3 * E:4 * E]
        _put_heads(v_ref, (), g_ref[:, 2 * E:3 * E].astype(BF16))
        for d, p_ref in ((0, p0_ref), (1, p1_ref)):
            t = _gla_gates(g_ref[:, d * E:(d + 1) * E], qpre, lb_ref[d:d + 1, :], cum_ref[d], d == 1)
            _put_heads(p_ref, (0,), (t["qs"] * t["e_q"]).astype(BF16))
            _put_heads(p_ref, (1,), (t["k"] * t["e_k"]).astype(BF16))
            for ci in range(nch):
                dec_ref[d, 0, ci:ci + 1, :] = t["e_mid"][ci]
                dec_ref[d, 0, nch + ci:nch + ci + 1, :] = t["e_rest"][ci]

    quad = pl.BlockSpec((2, HEADS, TM, DH), lambda i: (0, 0, i, 0))
    return pl.pallas_call(
        body, name="gla_prep", grid=(NT,),
        in_specs=[pl.BlockSpec((TM, 4 * E), lambda i: (i, 0)), VMEM_SPEC, VMEM_SPEC, HBM_SPEC, HBM_SPEC],
        out_specs=[quad, quad, pl.BlockSpec((HEADS, TM, DH), lambda i: (0, i, 0)), pl.BlockSpec((2, 1, 2 * nch, E), lambda i: (0, i, 0, 0)),
                   HBM_SPEC, HBM_SPEC],
        out_shape=(_sds((2, HEADS, TT, DH), BF16), _sds((2, HEADS, TT, DH), BF16), _sds((HEADS, TT, DH), BF16), _sds((2, NT, 2 * nch, E), F32),
                   _sds((E, D), BF16), _sds((4, PG, PG), BF16)),
        scratch_shapes=_rider_sems(2),
        compiler_params=pltpu.CompilerParams(dimension_semantics=("arbitrary",), vmem_limit_bytes=VMEM_LIMIT),
    )(g_all, lb, cum01, s_wout, s_pgrp)


def _scan_tile(i, rev):
    t = jnp.where(i == 0, 0, NT - i) if rev else i
    return t, pl.ds(pl.multiple_of(t * TM, TM), TM)


def _chunk_rows(dec_ref, lanes, cis, where):
    nch = TM // CHUNK

    def rows(off):
        return jnp.stack([dec_ref[d, where[d][0], off + ci:off + ci + 1, hh * DH:(hh + 1) * DH] for (d, hh), ci in zip(lanes, cis)])

    return rows(0), rows(nch)


def _gla_fwd(p0, p1, v_all, dec, mask01, s_pwin, s_pwout):
    n_steps = HEADS // GLA_HB

    def body(p0_ref, p1_ref, v_ref, dec_ref, msk_ref, spwin_r, spwout_r, o_ref, pwin_o, pwout_o, ob_sc, ssem, rsem, lsem):
        _gather_rider(pl.program_id(0), n_steps, n_steps - 1, ("major", "rows"), (spwin_r, spwout_r), (pwin_o, pwout_o), ssem, rsem, lsem)

        lanes = [(d, hh) for d in (0, 1) for hh in range(GLA_HB)]
        nch = TM // CHUNK

        def tile_body(i, st):
            where = [_scan_tile(i, d == 1) for d in (0, 1)]

            def stacked(fn):
                return jnp.stack([fn(d, hh, where[d][1]) for d, hh in lanes])

            qg, kg = [stacked(lambda d, hh, rows, ty=ty: (p1_ref if d else p0_ref)[ty, hh, rows, :]) for ty in range(2)]
            v = stacked(lambda d, hh, rows: v_ref[hh, rows, :])
            a = _bdot_nt(qg, kg) * jnp.stack([msk_ref[d] for d, _ in lanes])
            intra = _bdot(a, v)
            outs = [[None] * nch for _ in lanes]
            for n in range(nch):
                cis = [nch - 1 - n if d else n for d, _ in lanes]

                def chunk(arr):
                    return jnp.stack([arr[l, ci * CHUNK:(ci + 1) * CHUNK] for l, ci in enumerate(cis)])

                e_mid, e_rest = _chunk_rows(dec_ref, lanes, cis, where)
                inter = _bdot_nt(chunk(qg), st * e_mid)
                for l, ci in enumerate(cis):
                    outs[l][ci] = inter[l] + intra[l, ci * CHUNK:(ci + 1) * CHUNK]
                st = st * (e_mid * e_rest) + _bdot_tn(chunk(v), chunk(kg)) * e_rest
            for l, (d, hh) in enumerate(lanes):
                (ob_sc if d else o_ref)[hh, where[d][1], :] = jnp.concatenate(outs[l], axis=0)
            return st

        lax.fori_loop(0, NT, tile_body, jnp.zeros((len(lanes), DH, DH), F32))
        o_ref[...] += ob_sc[...]

    quad = pl.BlockSpec((2, GLA_HB, TT, DH), lambda h: (0, h, 0, 0))
    head = pl.BlockSpec((GLA_HB, TT, DH), lambda h: (h, 0, 0))
    return pl.pallas_call(
        body, name="gla_fwd", grid=(n_steps,),
        in_specs=[quad, quad, head, pl.BlockSpec((2, NT, 8, GLA_HB * DH), lambda h: (0, 0, 0, h)),
                  pl.BlockSpec((2, TM, TM), lambda h: (0, 0, 0)), HBM_SPEC, HBM_SPEC],
        out_specs=[head, HBM_SPEC, HBM_SPEC],
        out_shape=(_sds((HEADS, TT, DH), F32), _sds((NDEV, D, SH_PWIN), BF16), _sds((E, D), BF16)),
        scratch_shapes=[pltpu.VMEM((GLA_HB, TT, DH), F32)] + _rider_sems(2),
        compiler_params=pltpu.CompilerParams(dimension_semantics=("arbitrary",), vmem_limit_bytes=VMEM_LIMIT),
    )(p0, p1, v_all, dec, mask01, s_pwin, s_pwout)


def _gated_norm(o, z, gw):
    r = _head_map(lambda oh: jnp.broadcast_to(_rstd(oh), oh.shape), o)
    on = o * r
    zs = _sigmoid(z)
    sz = z * zs
    return on * gw * sz, r, on, zs, sz


def _f3_out(o, g_all, x, gate, gw, wout):
    def body(o_ref, z_ref, x_ref, gate_ref, gw_ref, w_ref, x1_ref):
        og, _, _, _, _ = _gated_norm(_get_heads(o_ref), z_ref[...], gw_ref[...])
        x1_ref[...] = x_ref[...] + gate_ref[...] * _dot(og, w_ref[...])

    return pl.pallas_call(
        body, name="f3_out", grid=(NTX,),
        in_specs=[pl.BlockSpec((HEADS, TM, DH), lambda i: (0, i + 1, 0)), pl.BlockSpec((TM, E), lambda i: (i + 1, 4)),
                  pl.BlockSpec((TM, D), lambda i: (i, 0)), pl.BlockSpec((1, D), lambda i: (0, 0)),
                  pl.BlockSpec((1, E), lambda i: (0, 0)), pl.BlockSpec((E, D), lambda i: (0, 0))],
        out_specs=pl.BlockSpec((TM, D), lambda i: (i, 0)),
        out_shape=_sds((T, D), F32),
        compiler_params=pltpu.CompilerParams(dimension_semantics=("arbitrary",)),
    )(o, g_all, x, gate, gw, wout)


def _pool_layer(x1, tgt, mod1, nw1, fnw, pwin, pgrp, pscale, pwout, pb, pbt, pinv):
    def body(x_ref, t_ref, m_ref, nw_ref, fw_ref, pwin_ref, pgrp_ref, ps_ref, pwout_ref, pb_ref, pbt_ref, pinv_ref,
             dx_ref, gpwin_o, gpgrp_o, gpwout_o, dmod_o, gnw_o, gfw_o, gps_o, loss_o,
             a_pwin, a_pgrp, a_pwout):
        i = pl.program_id(0)

        @pl.when(i == 0)
        def _():
            for ref in (a_pwin, a_pgrp, a_pwout, dmod_o, gnw_o, gfw_o, gps_o, loss_o):
                ref[...] = jnp.zeros_like(ref)

        shift, scale, gate = m_ref[0:1, :], m_ref[1:2, :], m_ref[2:3, :]
        nw, fw, ps = nw_ref[...], fw_ref[...], ps_ref[...]
        x1 = x_ref[...]
        hx, r1, xn, a = _modulated(x1, nw, shift, scale)
        hxb = hx.astype(BF16)
        uz = jnp.concatenate([_dot(hxb, pwin_ref[j]) for j in range(NDEV)], axis=1)
        u, z = uz[:, :E], uz[:, E:]
        pooled, ys = [], []
        for g in range(4):
            ug = u[:, g * PG:(g + 1) * PG]
            pg = _dot01(pb_ref[g], ug) * pinv_ref[g] - ug
            pooled.append(pg.astype(BF16))
            ys.append(_dot(pooled[g], pgrp_ref[g]))
        ycat = jnp.concatenate(ys, axis=1)
        y = ycat * ps
        zs = _sigmoid(z)
        sz = z * zs
        p = (y * sz).astype(BF16)
        out = _dot(p, pwout_ref[...])
        x2 = x1 + gate * out
        r2 = _rstd(x2)
        xn2 = x2 * r2
        diff = xn2 * fw - t_ref[...]
        loss_o[...] += _colsum(diff * diff)
        dyf = diff * (1.0 / D)
        gfw_o[...] += _colsum(dyf * xn2)
        dxn2 = dyf * fw
        dx2 = r2 * (dxn2 - xn2 * jnp.mean(dxn2 * xn2, axis=-1, keepdims=True))
        dgate = _colsum(dx2 * out)
        dout = (dx2 * gate).astype(BF16)
        for j in range(4):
            cs = slice(j * PG, (j + 1) * PG)
            a_pwout[:, cs] += _dot_ta(p, dout[:, cs])
        dp = _dot_tb(dout, pwout_ref[...])
        dy = dp * sz
        dz = dp * y * (zs * (1.0 + z * (1.0 - zs)))
        gps_o[...] += _colsum(dy * ycat)
        dycat = dy * ps
        dus = []
        for g in range(4):
            dyg = dycat[:, g * PG:(g + 1) * PG].astype(BF16)
            a_pgrp[g] += _dot_ta(pooled[g], dyg)
            dpg = _dot_tb(dyg, pgrp_ref[g])
            dus.append(_dot01(pbt_ref[g], dpg * pinv_ref[g]) - dpg)
        duz = jnp.concatenate(dus + [dz], axis=1).astype(BF16)
        dhx = None
        for j in range(NDEV):
            dj = duz[:, j * SH_PWIN:(j + 1) * SH_PWIN]
            a_pwin[j] += _dot_ta(hxb, dj)
            part = _dot_tb(dj, pwin_ref[j])
            dhx = part if dhx is None else dhx + part
        dmod_o[0:1, :] += _colsum(dhx)
        dmod_o[1:2, :] += _colsum(dhx * a)
        dmod_o[2:3, :] += dgate
        da = dhx * (1.0 + scale)
        gnw_o[...] += _colsum(da * xn)
        dxn = da * nw
        dx_ref[...] = dx2 + r1 * (dxn - xn * jnp.mean(dxn * xn, axis=-1, keepdims=True))

        @pl.when(i == NTX - 1)
        def _():
            gpwin_o[...] = a_pwin[...].astype(BF16)
            gpgrp_o[...] = a_pgrp[...].astype(BF16)
            gpwout_o[...] = a_pwout[...].astype(BF16)

    tile = pl.BlockSpec((TM, D), lambda i: (i, 0))
    outs = (_sds((T, D), F32), _sds((NDEV, D, SH_PWIN), BF16), _sds((4, PG, PG), BF16), _sds((E, D), BF16),
            _sds((3, D), F32), _sds((1, D), F32), _sds((1, D), F32), _sds((1, E), F32), _sds((1, D), F32))
    return pl.pallas_call(
        body, name="pool_layer", grid=(NTX,),
        in_specs=[tile, tile] + [VMEM_SPEC] * 10,
        out_specs=[tile] + [VMEM_SPEC] * 8,
        out_shape=outs,
        scratch_shapes=[pltpu.VMEM((NDEV, D, SH_PWIN), F32), pltpu.VMEM((4, PG, PG), F32), pltpu.VMEM((E, D), F32)],
        compiler_params=pltpu.CompilerParams(dimension_semantics=("arbitrary",), vmem_limit_bytes=VMEM_LIMIT),
    )(x1, tgt, mod1, nw1, fnw, pwin, pgrp, pscale, pwout, pb, pbt, pinv)


def _b3_out_bwd(dx1, o, g_all, gate, gw, wout, gpwout):
    def body(dx_ref, o_ref, z_ref, gate_ref, gw_ref, w_ref, gpwout_r, do_ref, dz_ref, gw_o, dgate_o, ggw_o, rpwout_o,
             acc, *rider):
        i = pl.program_id(0)
        bufs, sems = _rider2_split(rider, 1)
        _scatter_rider2(i, NT, 2, ("rows",), (gpwout_r,), (rpwout_o,), bufs, sems)

        @pl.when(i == 0)
        def _():
            acc[...] = jnp.zeros_like(acc)
            dgate_o[...] = jnp.zeros_like(dgate_o)
            ggw_o[...] = jnp.zeros_like(ggw_o)
            do_ref[...] = jnp.zeros_like(do_ref)
            dz_ref[...] = jnp.zeros_like(dz_ref)

        @pl.when(i > 0)
        def _():
            gw = gw_ref[...]
            z = z_ref[...]
            og, r, on, zs, sz = _gated_norm(_get_heads(o_ref), z, gw)
            ogb = og.astype(BF16)
            dx = dx_ref[...]
            dgate_o[...] += _colsum(dx * _dot(ogb, w_ref[...]))
            dy = (dx * gate_ref[...]).astype(BF16)
            for j in range(4):
                cs = slice(j * PG, (j + 1) * PG)
                acc[:, cs] += _dot_ta(ogb, dy[:, cs])
            dog = _dot_tb(dy, w_ref[...])
            dz_ref[...] = (dog * (on * gw) * (zs * (1.0 + z * (1.0 - zs)))).astype(BF16)
            dong = dog * sz
            ggw_o[...] += _colsum(dong * on)
            don = dong * gw
            do = _head_map(lambda dh, nh, rh: rh * (dh - nh * jnp.mean(dh * nh, axis=-1, keepdims=True)), don, on, r)
            _put_heads(do_ref, (), do.astype(BF16))

        @pl.when(i == NT - 1)
        def _():
            gw_o[...] = acc[...].astype(BF16)

    prev = lambda i: (jnp.maximum(i - 1, 0), 0)
    heads = pl.BlockSpec((HEADS, TM, DH), lambda i: (0, i, 0))
    return pl.pallas_call(
        body, name="b3_out_bwd", grid=(NT,),
        in_specs=[pl.BlockSpec((TM, D), prev), heads, pl.BlockSpec((TM, E), lambda i: (i, 4)),
                  VMEM_SPEC, VMEM_SPEC, VMEM_SPEC, HBM_SPEC],
        out_specs=[heads, pl.BlockSpec((TM, E), lambda i: (i, 0)), VMEM_SPEC, VMEM_SPEC, VMEM_SPEC, HBM_SPEC],
        out_shape=(_sds((HEADS, TT, DH), BF16), _sds((TT, E), BF16), _sds((E, D), BF16), _sds((1, D), F32), _sds((1, E), F32),
                   _sds((RS_SLOTS, SH_ROWS, D), BF16)),
        scratch_shapes=[pltpu.VMEM((E, D), F32)] + _rider2_scratch([(SH_ROWS, D)]),
        compiler_params=pltpu.CompilerParams(dimension_semantics=("arbitrary",), vmem_limit_bytes=VMEM_LIMIT),
    )(dx1, o, g_all, gate, gw, wout, gpwout)


def _gla_bwd(p0, p1, v_all, dec, do, mask01, gpwin, gpgrp):
    nch = TM // CHUNK
    n_steps = HEADS // GLA_HB

    def body(p0_ref, p1_ref, v_ref, dec_ref, do_ref, msk_ref, gpwin_r, gpgrp_r, d0_ref, d1_ref, dv_ref, dgl_ref, rpwin_o, rpgrp_o,
             ss_sc, dv_sc, ssem, rsem, lsem, *rider):
        _scatter_rider(pl.program_id(0), n_steps, ("grp",), (gpgrp_r,), (rpgrp_o,), ssem, rsem, lsem)
        bufs, sems = _rider2_split(rider, 1)
        _scatter_rider2(pl.program_id(0), n_steps, 1, ("major",), (gpwin_r,), (rpwin_o,), bufs, sems)

        lanes = [(d, hh) for d in (0, 1) for hh in range(GLA_HB)]
        zero = jnp.zeros((len(lanes), DH, DH), F32)
        dgl_ref[...] = jnp.zeros_like(dgl_ref)

        def p_of(d):
            return p1_ref if d else p0_ref

        def scan_step(i, n):
            where = [_scan_tile(i, d == 1) for d in (0, 1)]
            cis = [nch - 1 - n if d else n for d, _ in lanes]
            e_mid, e_rest = _chunk_rows(dec_ref, lanes, cis, where)

            def chunk(arr):
                return jnp.stack([arr[l, ci * CHUNK:(ci + 1) * CHUNK] for l, ci in enumerate(cis)])

            return where, cis, e_mid, e_rest, chunk

        def stacked(i, fn):
            where = [_scan_tile(i, d == 1) for d in (0, 1)]
            return jnp.stack([fn(d, hh, where[d][1]) for d, hh in lanes])

        def fwd_body(i, st):
            v = stacked(i, lambda d, hh, rows: v_ref[hh, rows, :])
            kg = stacked(i, lambda d, hh, rows: p_of(d)[1, hh, rows, :])
            for n in range(nch):
                _, _, e_mid, e_rest, chunk = scan_step(i, n)
                ss_sc[i * nch + n] = st
                st = st * (e_mid * e_rest) + _bdot_tn(chunk(v), chunk(kg)) * e_rest
            return st

        ss_sc[NT * nch] = lax.fori_loop(0, NT, fwd_body, zero)

        def bwd_body(ii, dst):
            i = NT - 1 - ii
            qg, kg = [stacked(i, lambda d, hh, rows, ty=ty: p_of(d)[ty, hh, rows, :]) for ty in range(2)]
            v = stacked(i, lambda d, hh, rows: v_ref[hh, rows, :])
            dob = stacked(i, lambda d, hh, rows: do_ref[hh, rows, :])
            msk = jnp.stack([msk_ref[d] for d, _ in lanes])
            a = (_bdot_nt(qg, kg) * msk).astype(BF16)
            da = (_bdot_nt(dob, v) * msk).astype(BF16)
            dqg = _bdot(da, kg)
            dkg = _bdot_tn(da, qg)
            dv_intra = _bdot_tn(a, dob)
            dv_l, dkg_l, dqg_l = ([[None] * nch for _ in lanes] for _ in range(3))
            for n in range(nch - 1, -1, -1):
                where, cis, e_mid, e_rest, chunk = scan_step(i, n)
                s_c, s_end = ss_sc[i * nch + n], ss_sc[i * nch + n + 1]
                dste = (dst * e_rest).astype(BF16)
                kg_c, v_c, dob_c = chunk(kg), chunk(v), chunk(dob)
                dv_c = chunk(dv_intra) + _bdot_nt(kg_c, dste)
                dkg_c = chunk(dkg) + _bdot(v_c, dste)
                dqg_c = chunk(dqg) + _bdot(dob_c, s_c * e_mid)
                dgl = jnp.sum(s_end * dst, axis=1, keepdims=True)
                for l, ((d, hh), ci) in enumerate(zip(lanes, cis)):
                    dv_l[l][ci], dkg_l[l][ci], dqg_l[l][ci] = dv_c[l], dkg_c[l], dqg_c[l]
                    dgl_ref[d, where[d][0], ci:ci + 1, hh * DH:(hh + 1) * DH] = dgl[l]
                dst = dst * (e_mid * e_rest) + _bdot_tn(dob_c, chunk(qg)) * e_mid
            where = [_scan_tile(i, d == 1) for d in (0, 1)]
            for l, (d, hh) in enumerate(lanes):
                rows = where[d][1]
                d_ref = d1_ref if d else d0_ref
                d_ref[0, hh, rows, :] = jnp.concatenate(dqg_l[l], axis=0).astype(BF16)
                d_ref[1, hh, rows, :] = jnp.concatenate(dkg_l[l], axis=0).astype(BF16)
                dv_sc[d, hh, rows, :] = jnp.concatenate(dv_l[l], axis=0).astype(BF16)
            return dst

        lax.fori_loop(0, NT, bwd_body, zero)
        dv_ref[...] = (dv_sc[0].astype(F32) + dv_sc[1].astype(F32)).astype(BF16)

    quad = pl.BlockSpec((2, GLA_HB, TT, DH), lambda h: (0, h, 0, 0))
    col = pl.BlockSpec((GLA_HB, TT, DH), lambda h: (h, 0, 0))
    chunkv = pl.BlockSpec((2, NT, 8, GLA_HB * DH), lambda h: (0, 0, 0, h))
    outs = (_sds((2, HEADS, TT, DH), BF16), _sds((2, HEADS, TT, DH), BF16), _sds((HEADS, TT, DH), BF16), _sds((2, NT, 8, E), F32),
            _sds((RS_SLOTS, D, SH_PWIN), BF16), _sds((NDEV, 4, SH_GRP, PG), BF16))
    return pl.pallas_call(
        body, name="gla_bwd", grid=(n_steps,),
        in_specs=[quad, quad, col, chunkv, col, pl.BlockSpec((2, TM, TM), lambda h: (0, 0, 0)), HBM_SPEC, HBM_SPEC],
        out_specs=[quad, quad, col, chunkv, HBM_SPEC, HBM_SPEC],
        out_shape=outs,
        scratch_shapes=[pltpu.VMEM((NT * nch + 1, 2 * GLA_HB, DH, DH), F32), pltpu.VMEM((2, GLA_HB, TT, DH), BF16)] + _rider_sems(1)
        + _rider2_scratch([(D, SH_PWIN)]),
        compiler_params=pltpu.CompilerParams(dimension_semantics=("arbitrary",), vmem_limit_bytes=VMEM_LIMIT),
    )(p0, p1, v_all, dec, do, mask01, gpwin, gpgrp)


TMB = 128


def _gla_post_bwd(g_all, d0, d1, dgl, dv, dz, lb, cum01, gwout):
    nch = TMB // CHUNK

    def body(g_ref, d0_ref, d1_ref, dgl_ref, dv_ref, dz_ref, lb_ref, cum_ref, gwout_r, dg_ref, dlb_ref, rwout_o, *rider):
        i = pl.program_id(0)
        bufs, sems = _rider2_split(rider, 1)
        _scatter_rider2(i, TT // TMB, 2, ("rows",), (gwout_r,), (rwout_o,), bufs, sems)

        @pl.when(i == 0)
        def _():
            dlb_ref[...] = jnp.zeros_like(dlb_ref)

        half = i & 1
        qpre = g_ref[:, 3 * E:4 * E]
        dqs_sum = None
        dpre = []
        for d, d_ref in ((0, d0_ref), (1, d1_ref)):
            rev = d == 1
            lbd = lb_ref[d:d + 1, :]
            t = _gla_gates(g_ref[:, d * E:(d + 1) * E], qpre, lbd, cum_ref[d, :TMB, :TMB], rev)
            dqs = _get_heads(d_ref, (0,)).astype(F32) * t["e_q"]
            dk = _get_heads(d_ref, (1,)).astype(F32) * t["e_k"]
            dg = t["qs"] * dqs - t["k"] * dk
            dgl8 = dgl_ref[d, 0]
            dgl_rows = [jnp.where(half == 0, dgl8[ci:ci + 1, :], dgl8[nch + ci:nch + ci + 1, :]) for ci in range(nch)]
            dgl_b = jnp.concatenate([jnp.broadcast_to(dgl_rows[ci], (CHUNK, E)) for ci in range(nch)], axis=0)
            pos = lax.broadcasted_iota(jnp.int32, (TMB, E), 0) & (CHUNK - 1)
            dg = dg + jnp.where(pos == (0 if rev else CHUNK - 1), dgl_b, 0.0)
            dlf = _dot01(cum_ref[1 - d, :TMB, :TMB], dg)
            df = dlf / t["f"] - dk
            sig = t["sig"]
            dpre.append((df * (1.0 - lbd) * sig * (1.0 - sig)).astype(BF16))
            dlb_ref[d:d + 1, :] += _colsum(df * (1.0 - sig))
            dqs_sum = dqs if dqs_sum is None else dqs_sum + dqs
            qsig = t["qsig"]
        dqpre = dqs_sum * (DH ** -0.5) * (qsig * (1.0 + qpre * (1.0 - qsig)))
        dg_ref[...] = jnp.concatenate([dpre[0], dpre[1], _get_heads(dv_ref), dqpre.astype(BF16), dz_ref[...]], axis=1)

    quad = pl.BlockSpec((2, HEADS, TMB, DH), lambda i: (0, 0, i, 0))
    tile = pl.BlockSpec((TMB, E), lambda i: (i, 0))
    return pl.pallas_call(
        body, name="gla_post_bwd", grid=(TT // TMB,),
        in_specs=[pl.BlockSpec((TMB, 4 * E), lambda i: (i, 0)), quad, quad,
                  pl.BlockSpec((2, 1, 8, E), lambda i: (0, i // 2, 0, 0)), pl.BlockSpec((HEADS, TMB, DH), lambda i: (0, i, 0)), tile,
                  VMEM_SPEC, VMEM_SPEC, HBM_SPEC],
        out_specs=[pl.BlockSpec((TMB, WIN_COLS), lambda i: (i, 0)), VMEM_SPEC, HBM_SPEC],
        out_shape=(_sds((TT, WIN_COLS), BF16), _sds((2, E), F32), _sds((RS_SLOTS, SH_ROWS, D), BF16)),
        scratch_shapes=_rider2_scratch([(SH_ROWS, D)]),
        compiler_params=pltpu.CompilerParams(dimension_semantics=("arbitrary",), vmem_limit_bytes=VMEM_LIMIT),
    )(g_all, d0, d1, dgl, dv, dz, lb, cum01, gwout)


WIN_SLOTS = 4


def _scatter_order(s, core):
    return (NDEV - 1 - s) ^ jnp.where((s >= 2) & (s <= 5) & ((s & 1) == core), 6, 0)


def _b1_in_bwd(idx1, ctx, x, dx1, dg, nw, msel, win):
    last_s = NDEV - 1
    half = D // 2

    def body(idx_ref, ctx_ref, x_ref, dx1_ref, dg_ref, nw_ref, m_ref, w_ref, gx_ref, rwin_o, dmx_o, dmc_o, gnw_o,
             hx_sc, dhx_sc, acc, sbuf, pbuf, rbuf, psend, precv, isend, irecv, dsend, drecv, sibsem, lsem):
        del idx_ref
        s, i = pl.program_id(0), pl.program_id(1)
        x, y, cc, idx = _mesh_pos()
        shift, scale = m_ref[0, 0:1, :], m_ref[0, 1:2, :]
        sibling = (x, y, 1 - cc)

        def partial(p):
            return pltpu.make_async_remote_copy(src_ref=sbuf.at[0], dst_ref=pbuf.at[p], send_sem=psend.at[p], recv_sem=precv.at[p],
                                                device_id=sibling, device_id_type=MESH)

        def chip_sum(p):
            return pltpu.make_async_remote_copy(src_ref=sbuf.at[1], dst_ref=rwin_o.at[2 + p], send_sem=isend.at[p], recv_sem=irecv.at[p],
                                                device_id=_peer(x, y, cc, 2 * (p + 1)), device_id_type=MESH)

        def relay(h):
            return pltpu.make_async_remote_copy(src_ref=sbuf.at[1, pl.ds(h * half, half), :], dst_ref=rbuf.at[h], send_sem=dsend.at[h],
                                                recv_sem=drecv.at[h], device_id=_peer(x, y, cc, 2 * (h + 1)), device_id_type=MESH)

        to_sibling = pltpu.make_async_remote_copy(src_ref=sbuf.at[0], dst_ref=rwin_o.at[1], send_sem=sibsem.at[0], recv_sem=sibsem.at[1],
                                                  device_id=sibling, device_id_type=MESH)
        own = pltpu.make_async_copy(sbuf.at[1], rwin_o.at[0], lsem)

        @pl.when((s == 0) & (i == 0))
        def _():
            for ref in (dmx_o, dmc_o, gnw_o):
                ref[...] = jnp.zeros_like(ref)

        @pl.when(s == 0)
        def _():
            hx, _, _, _ = _modulated(_ctx_or_x(i, ctx_ref, x_ref), nw_ref[...], shift, scale)
            hx_sc[i] = hx.astype(BF16)

        @pl.when(i == 0)
        def _():
            acc[...] = jnp.zeros_like(acc)

        dgb = dg_ref[...]
        hxb = hx_sc[i]
        for lo, hi in ((0, 256), (256, 512), (512, SH_WIN)):
            acc[:, lo:hi] += _dot_ta(hxb, dgb[:, lo:hi])
        part = _dot_tb(dgb, w_ref[0])

        @pl.when(s == 0)
        def _():
            dhx_sc[i] = part

        @pl.when(s > 0)
        def _():
            dhx_sc[i] += part

        done = i == NT - 1

        def hand_over(p, before):
            before.wait_send()
            sbuf[0] = acc[...].astype(BF16)
            partial(p).start()

        def send_chip_sum(p, before):
            for cp in before:
                cp.wait_send()
            partial(p).wait_recv()
            sbuf[1] = (acc[...] + pbuf[p].astype(F32)).astype(BF16)
            h = 1 - p
            rows = pl.ds(h * half, half)
            relay(h).wait_recv()
            sbuf[1, rows, :] = (acc[rows, :] + pbuf[p, rows, :].astype(F32) + rbuf[h].astype(F32)).astype(BF16)
            chip_sum(p).start()

        @pl.when(done & (s == 0))
        def _():
            sbuf[0] = acc[...].astype(BF16)
            partial(2).start()

        @pl.when(done & (s == 1))
        def _():
            partial(2).wait_recv()
            sbuf[1] = (acc[...] + pbuf[2].astype(F32)).astype(BF16)
            for h in range(2):
                relay(h).start()

        for core in range(2):
            @pl.when(done & (cc == core) & (s == 2))
            def _(core=core):
                hand_over(core, partial(2))

            @pl.when(done & (cc == core) & (s == 3))
            def _(core=core):
                send_chip_sum(1 - core, [relay(0), relay(1)])

            @pl.when(done & (cc == core) & (s == 4))
            def _(core=core):
                hand_over(1 - core, partial(core))

            @pl.when(done & (cc == core) & (s == 5))
            def _(core=core):
                send_chip_sum(core, [chip_sum(1 - core)])

            @pl.when(done & (cc == core) & (s == last_s - 1))
            def _(core=core):
                partial(1 - core).wait_send()
                sbuf[0] = acc[...].astype(BF16)
                to_sibling.start()

            @pl.when(done & (cc == core) & (s == last_s))
            def _(core=core):
                chip_sum(core).wait_send()
                sbuf[1] = acc[...].astype(BF16)
                own.start()

        @pl.when(s == last_s)
        def _():
            nw = nw_ref[...]
            _, r, xn, a = _modulated(_ctx_or_x(i, ctx_ref, x_ref), nw, shift, scale)
            dhx = dhx_sc[i]
            dsh, dsc = _colsum(dhx), _colsum(dhx * a)
            da = dhx * (1.0 + scale)
            gnw_o[...] += _colsum(da * xn)
            dxn = da * nw
            gx_ref[...] = dx1_ref[...] + r * (dxn - xn * jnp.mean(dxn * xn, axis=-1, keepdims=True))

            @pl.when(i == 0)
            def _():
                dmc_o[0:1, :] += dsh
                dmc_o[1:2, :] += dsc

            @pl.when(i > 0)
            def _():
                dmx_o[0:1, :] += dsh
                dmx_o[1:2, :] += dsc

        @pl.when((i == NT - 1) & (s == last_s))
        def _():
            to_sibling.wait_send()
            to_sibling.wait_recv()
            for p in range(2):
                chip_sum(p).wait_recv()
            own.wait()

    grid_spec = pltpu.PrefetchScalarGridSpec(
        num_scalar_prefetch=1, grid=(NDEV, NT),
        in_specs=[VMEM_SPEC,
                  pl.BlockSpec((TM, D), lambda s, i, ix: (jnp.where((s == 0) | (s == last_s), jnp.maximum(i - 1, 0), NTX - 1), 0)),
                  pl.BlockSpec((TM, D), lambda s, i, ix: (jnp.where(s == last_s, jnp.maximum(i - 1, 0), 0), 0)),
                  pl.BlockSpec((TM, SH_WIN), lambda s, i, ix: (i, ix[0] ^ _scatter_order(s, ix[0] & 1))), VMEM_SPEC,
                  pl.BlockSpec((1, 2, D), lambda s, i, ix: (jnp.minimum(i, 1), 0, 0)),
                  pl.BlockSpec((1, D, SH_WIN), lambda s, i, ix: (ix[0] ^ _scatter_order(s, ix[0] & 1), 0, 0))],
        out_specs=[pl.BlockSpec((TM, D), lambda s, i, ix: (jnp.where(s == last_s, jnp.maximum(i - 1, 0), 0), 0)),
                   HBM_SPEC, VMEM_SPEC, VMEM_SPEC, VMEM_SPEC],
        scratch_shapes=[pltpu.VMEM((NT, TM, D), BF16), pltpu.VMEM((NT, TM, D), F32), pltpu.VMEM((D, SH_WIN), F32),
                        pltpu.VMEM((2, D, SH_WIN), BF16), pltpu.VMEM((3, D, SH_WIN), BF16), pltpu.VMEM((2, half, SH_WIN), BF16),
                        pltpu.SemaphoreType.DMA((3,)), pltpu.SemaphoreType.DMA((3,)), pltpu.SemaphoreType.DMA((2,)),
                        pltpu.SemaphoreType.DMA((2,)), pltpu.SemaphoreType.DMA((2,)), pltpu.SemaphoreType.DMA((2,)),
                        pltpu.SemaphoreType.DMA((2,)), pltpu.SemaphoreType.DMA])
    return pl.pallas_call(
        body, name="b1_in_bwd", grid_spec=grid_spec,
        out_shape=(_sds((T, D), F32), _sds((WIN_SLOTS, D, SH_WIN), BF16), _sds((2, D), F32), _sds((2, D), F32), _sds((1, D), F32)),
        compiler_params=pltpu.CompilerParams(dimension_semantics=("arbitrary", "arbitrary"), vmem_limit_bytes=VMEM_LIMIT),
    )(idx1, ctx, x, dx1, dg, nw, msel, win)


def _reduce_small(pd, pv, cg, c_ctx, ada_w0):
    n_arr = 3

    def body(pd_r, pv_r, cg_r, cctx_r, ada_r, gada_o, gadab_o, gcctx_o, pvsum_o, loss_o,
             pd_all, pv_all, dsc_all, dsc_mine, ssem, rsem):
        x, y, cc, idx = _mesh_pos()
        srcs = [pd_r, pv_r, dsc_mine]
        dsts = [pd_all.at[idx], pv_all.at[idx], dsc_all.at[idx]]

        def remote(a, k):
            return pltpu.make_async_remote_copy(src_ref=srcs[a], dst_ref=dsts[a], send_sem=ssem.at[a, k], recv_sem=rsem.at[a, k],
                                                device_id=_peer(x, y, cc, k), device_id_type=MESH)

        first = [remote(a, k) for k in range(1, NDEV) for a in (0, 1)]
        for cp in first:
            cp.start()
        pd_all[idx] = pd_r[...]
        pv_all[idx] = pv_r[...]
        for k in range(1, NDEV):
            remote(0, k).wait_recv()
            remote(1, k).wait_recv()
        mine = [pd_all[s, :, pl.ds(idx, 1), :] for s in range(NDEV)]
        dmc = functools.reduce(lambda u, v: u + v, [m[2] for m in mine])
        rows = _stack_rows([cg_r[i] for i in range(NDEV)] + [cctx_r[...]])
        sc = (rows * _sigmoid(rows)).astype(BF16)
        gada_o[0] = _dot_ta(sc, _stack_rows([m[0] for m in mine] + [dmc]))
        gada_o[1] = _dot_ta(sc, _stack_rows([m[1] for m in mine]))
        dsc_mine[...] = _dot_tb(jnp.broadcast_to(dmc, (8, SH_ADA)), ada_r[...])[0:1, :]
        dsc_all[idx] = dsc_mine[...]
        second = [remote(2, k) for k in range(1, NDEV)]
        for cp in second:
            cp.start()
        tot = [functools.reduce(lambda u, v: u + v, [pd_all[s, l] for s in range(NDEV)]) for l in range(3)]
        gadab_o[0] = tot[0] + tot[2]
        gadab_o[1] = tot[1]
        pvs = functools.reduce(lambda u, v: u + v, [pv_all[s] for s in range(NDEV)])
        pvsum_o[...] = pvs
        loss_o[...] = jnp.broadcast_to(jnp.sum(pvs[:, PV_LOSS:PV_LOSS + D], axis=-1, keepdims=True) * (0.5 / D), (1, 128))
        for k in range(1, NDEV):
            remote(2, k).wait_recv()
        dsc = functools.reduce(lambda u, v: u + v, [dsc_all[s] for s in range(NDEV)])
        cx = cctx_r[...]
        sx = _sigmoid(cx)
        gcctx_o[...] = dsc * (sx * (1.0 + cx * (1.0 - sx)))
        for cp in first + second:
            cp.wait_send()

    outs = (_sds((2, D, SH_ADA), F32), _sds((2, NDEV, SH_ADA), F32), _sds((1, D), F32), _sds((1, PV_LEN), F32), _sds((1, 128), F32))
    return pl.pallas_call(
        body, name="reduce_small", out_shape=outs,
        in_specs=[VMEM_SPEC] * 5, out_specs=[VMEM_SPEC] * 5,
        scratch_shapes=[
            pltpu.VMEM((NDEV, 3, NDEV, SH_ADA), F32), pltpu.VMEM((NDEV, 1, PV_LEN), F32), pltpu.VMEM((NDEV, 1, D), F32),
            pltpu.VMEM((1, D), F32),
            pltpu.SemaphoreType.DMA((n_arr, NDEV)), pltpu.SemaphoreType.DMA((n_arr, NDEV)),
        ],
        compiler_params=pltpu.CompilerParams(vmem_limit_bytes=VMEM_LIMIT),
    )(pd, pv, cg, c_ctx, ada_w0)


PV_NW, PV_GNORM, PV_FINAL, PV_LB, PV_PSCALE, PV_LOSS, PV_LEN = 0, 2 * D, 3 * D, 4 * D, 6 * D, 7 * D, 8 * D


def _adamw(w, g, m, v):
    m = ADAM_B1 * m + (1.0 - ADAM_B1) * g
    v = ADAM_B2 * v + (1.0 - ADAM_B2) * (g * g)
    m_hat = m / (1.0 - ADAM_B1 ** ADAM_STEP)
    v_hat = v / (1.0 - ADAM_B2 ** ADAM_STEP)
    delta = -ADAM_LR * (m_hat / (jnp.sqrt(v_hat) + ADAM_EPS) + ADAM_WD * w)
    return delta, m, v


ADAM_STEPS = 8


def _adam_all(sharded, dense, small, lb_idx, lbv):
    ns, nd, nsm = len(sharded), len(dense), len(small)

    def body(*refs):
        it = iter(refs)
        sh_in = [[next(it) for _ in range(4)] for _ in range(ns)]
        de_in = [[next(it) for _ in range(4)] for _ in range(nd)]
        sm_in = [[next(it) for _ in range(4)] for _ in range(nsm)]
        lb_r = next(it)
        sh_out = [[next(it) for _ in range(4)] for _ in range(ns)]
        de_out = [[next(it) for _ in range(3)] for _ in range(nd)]
        sm_out = [[next(it) for _ in range(4)] for _ in range(nsm)]
        for (p, w, m, v), outs in zip(sh_in, sh_out):
            g = p[0].astype(F32)
            for s in range(1, p.shape[0]):
                g = g + p[s].astype(F32)
            d, mn, vn = _adamw(w[...], g, m[...], v[...])
            outs[0][...], outs[1][...], outs[2][...], outs[3][...] = g, d, mn, vn
        for (g, w, m, v), outs in zip(de_in, de_out):
            d, mn, vn = _adamw(w[...], g[...], m[...], v[...])
            outs[0][...], outs[1][...], outs[2][...] = d, mn, vn

        @pl.when(pl.program_id(0) == 0)
        def _():
            for j, ((g, w, m, v), outs) in enumerate(zip(sm_in, sm_out)):
                gj = g[...]
                if j == lb_idx:
                    gj = gj * lb_r[...] * (1.0 - lb_r[...])
                d, mn, vn = _adamw(w[...], gj, m[...], v[...])
                outs[0][...], outs[1][...], outs[2][...], outs[3][...] = gj, d, mn, vn

    def tile(a):
        return pl.BlockSpec((a.shape[0] // ADAM_STEPS, a.shape[1]), lambda i: (i, 0))

    in_specs, out_specs, out_shape, args = [], [], [], []
    for p, w, m, v in sharded:
        in_specs += [pl.BlockSpec((p.shape[0], p.shape[1] // ADAM_STEPS, p.shape[2]), lambda i: (0, i, 0))] + [tile(w)] * 3
        args += [p, w, m, v]
    for g, w, m, v in dense:
        in_specs += [tile(w)] * 4
        args += [g, w, m, v]
    for g, w, m, v in small:
        in_specs += [VMEM_SPEC] * 4
        args += [g, w, m, v]
    in_specs.append(VMEM_SPEC)
    args.append(lbv)
    for _, w, _, _ in sharded:
        out_specs += [tile(w)] * 4
        out_shape += [_sds(w.shape, F32)] * 4
    for _, w, _, _ in dense:
        out_specs += [tile(w)] * 3
        out_shape += [_sds(w.shape, F32)] * 3
    for _, w, _, _ in small:
        out_specs += [VMEM_SPEC] * 4
        out_shape += [_sds(w.shape, F32)] * 4
    res = pl.pallas_call(body, name="adam_all", grid=(ADAM_STEPS,), in_specs=in_specs, out_specs=out_specs, out_shape=tuple(out_shape),
                         compiler_params=pltpu.CompilerParams(dimension_semantics=("arbitrary",), vmem_limit_bytes=VMEM_LIMIT))(*args)
    it = iter(res)
    return ([tuple(next(it) for _ in range(4)) for _ in range(ns)], [tuple(next(it) for _ in range(3)) for _ in range(nd)],
            [tuple(next(it) for _ in range(4)) for _ in range(nsm)])


def kernel(x, c, ctx, c_ctx, ada_w, ada_b, norm_w, hgrn_w_in, hgrn_lb_logits, hgrn_gnorm_w, hgrn_w_out, pool_w_in, pool_w_grp, pool_scale, pool_w_out, final_norm_w, loss_target, m_c_ctx, m_ada_w, m_ada_b, m_norm_w, m_hgrn_w_in, m_hgrn_lb_logits, m_hgrn_gnorm_w, m_hgrn_w_out, m_pool_w_in, m_pool_w_grp, m_pool_scale, m_pool_w_out, m_final_norm_w, v_c_ctx, v_ada_w, v_ada_b, v_norm_w, v_hgrn_w_in, v_hgrn_lb_logits, v_hgrn_gnorm_w, v_hgrn_w_out, v_pool_w_in, v_pool_w_grp, v_pool_scale, v_pool_w_out, v_final_norm_w):
    idx = 4 * lax.axis_index("x") + 2 * lax.axis_index("y") + lax.axis_index("c")
    cctx2 = c_ctx.reshape(1, D)
    cum01, mask01 = _gla_consts()
    pb, pbt, pinv = _pool_consts()

    idx1 = idx.reshape(1).astype(jnp.int32)
    nw0, nw1 = norm_w[0:1], norm_w[1:2]
    fnw = final_norm_w.reshape(1, D)
    g_all, win, s_wout, s_pwin, s_pgrp, s_pwout, lbl_g, ps_g, cg, mod0, mod1, modc = _f1_gather_matmul(
        idx1, ctx[0], x[0], nw0, hgrn_w_in[0], hgrn_w_out[0], pool_w_in[0], pool_w_grp[0], pool_w_out[0], hgrn_lb_logits[0],
        pool_scale, c, cctx2, ada_w, ada_b)
    lb = jax.nn.sigmoid(jnp.transpose(lbl_g, (1, 0, 2)).reshape(2, E))
    pscale = ps_g.reshape(1, E)
    msel = jnp.stack([modc[:2], mod0[:2]])
    p0, p1, v_all, dec, wout, pgrp = _gla_prep(g_all, lb, cum01, s_wout, s_pgrp)
    o, pwin, pwout = _gla_fwd(p0, p1, v_all, dec, mask01, s_pwin, s_pwout)
    x1 = _f3_out(o, g_all, x[0], mod0[2:3], hgrn_gnorm_w, wout)
    dx1, gpwin, gpgrp, gpwout, dmod1, gnw1, gfw, gps, lossv = _pool_layer(
        x1, loss_target[0], mod1, nw1, fnw, pwin, pgrp, pscale, pwout, pb, pbt, pinv)
    do, dz, gwout, dgate0, ggw, rpwout = _b3_out_bwd(dx1, o, g_all, mod0[2:3], hgrn_gnorm_w, wout, gpwout)
    d0, d1, dv, dgl, rpwin, rpgrp = _gla_bwd(p0, p1, v_all, dec, do, mask01, gpwin, gpgrp)
    dg, dlb, rwout = _gla_post_bwd(g_all, d0, d1, dgl, dv, dz, lb, cum01, gwout)
    grad_x, rwin, dmx, dmc, gnw0 = _b1_in_bwd(idx1, ctx[0], x[0], dx1, dg, nw0, msel, win)

    dmod0 = jnp.concatenate([dmx, dgate0], axis=0)
    dmodc = jnp.concatenate([dmc, jnp.zeros((1, D), F32)], axis=0)
    pd = jnp.stack([dmod0, dmod1, dmodc]).reshape(3, NDEV, SH_ADA)
    pv = jnp.concatenate([gnw0, gnw1, ggw, gfw, dlb.reshape(1, 2 * E), gps, lossv], axis=1)
    g_ada, g_adab, g_cctx, pvsum, loss128 = _reduce_small(pd, pv, cg, cctx2, ada_w[0])

    g2 = (4 * SH_GRP, PG)
    sharded_names = ["hgrn_w_in", "hgrn_w_out", "pool_w_in", "pool_w_grp", "pool_w_out"]
    sharded = [(rwin, hgrn_w_in[0], m_hgrn_w_in[0], v_hgrn_w_in[0]),
               (rwout, hgrn_w_out[0], m_hgrn_w_out[0], v_hgrn_w_out[0]),
               (rpwin, pool_w_in[0], m_pool_w_in[0], v_pool_w_in[0]),
               (rpgrp.reshape((NDEV,) + g2), pool_w_grp[0].reshape(g2), m_pool_w_grp[0].reshape(g2), v_pool_w_grp[0].reshape(g2)),
               (rpwout, pool_w_out[0], m_pool_w_out[0], v_pool_w_out[0])]
    a2 = (2 * D, SH_ADA)
    g_ada2 = g_ada.reshape(a2)
    dense = [(g_ada2, ada_w.reshape(a2), m_ada_w.reshape(a2), v_ada_w.reshape(a2))]
    lb_me = lax.dynamic_slice_in_dim(lb, idx * DH, DH, axis=1)
    small_names = ["c_ctx", "ada_b", "norm_w", "hgrn_lb_logits", "hgrn_gnorm_w", "pool_scale", "final_norm_w"]
    small = [(g_cctx, cctx2, m_c_ctx.reshape(1, D), v_c_ctx.reshape(1, D)),
             (g_adab.reshape(2, 3 * D), ada_b, m_ada_b, v_ada_b),
             (pvsum[:, PV_NW:PV_NW + 2 * D].reshape(2, D), norm_w, m_norm_w, v_norm_w),
             (lax.dynamic_slice_in_dim(pvsum[:, PV_LB:PV_LB + 2 * E].reshape(2, E), idx * DH, DH, axis=1),
              hgrn_lb_logits[0], m_hgrn_lb_logits[0], v_hgrn_lb_logits[0]),
             (pvsum[:, PV_GNORM:PV_GNORM + E], hgrn_gnorm_w, m_hgrn_gnorm_w, v_hgrn_gnorm_w),
             (lax.dynamic_slice_in_dim(pvsum[:, PV_PSCALE:PV_PSCALE + E], idx * DH, DH, axis=1), pool_scale, m_pool_scale, v_pool_scale),
             (pvsum[:, PV_FINAL:PV_FINAL + D], fnw, m_final_norm_w.reshape(1, D), v_final_norm_w.reshape(1, D))]
    r_sharded, r_dense, r_small = _adam_all(sharded, dense, small, 3, lb_me)
    out = dict(zip(sharded_names, r_sharded))
    out["ada_w"] = (g_ada2,) + r_dense[0]
    out.update(zip(small_names, r_small))

    shapes = {"c_ctx": (D,), "ada_w": (2, D, SH_ADA), "ada_b": (2, 3 * D), "norm_w": (2, D), "hgrn_w_in": (1, D, SH_WIN),
              "hgrn_lb_logits": (1, 2, DH), "hgrn_gnorm_w": (1, E), "hgrn_w_out": (1, SH_ROWS, D), "pool_w_in": (1, D, SH_PWIN),
              "pool_w_grp": (1, 4, SH_GRP, PG), "pool_scale": (1, DH), "pool_w_out": (1, SH_ROWS, D), "final_norm_w": (D,)}
    order = ["c_ctx", "ada_w", "ada_b", "norm_w", "hgrn_w_in", "hgrn_lb_logits", "hgrn_gnorm_w", "hgrn_w_out", "pool_w_in",
             "pool_w_grp", "pool_scale", "pool_w_out", "final_norm_w"]
    flat = [out[name][q].reshape(shapes[name]) for q in range(4) for name in order]
    return (loss128[0, 0], grad_x[None], *flat)
```

```python
import functools

import numpy as np
import jax
import jax.numpy as jnp
from jax import lax
from jax.experimental import pallas as pl
from jax.experimental.pallas import tpu as pltpu

F32 = jnp.float32
BF16 = jnp.bfloat16

D = 1024
E = 1024
HEADS = 8
DH = 128
CHUNK = 64
T = 2048
TC = 256
TT = T + TC
TM = 256
NT = TT // TM
NTX = T // TM
NDEV = 8
GRID_W = 64
POOL_WINDOWS = (2, 4, 8, 16)
PG = 256
EPS = 1e-6
WIN_COLS = 5 * E
SH_WIN = WIN_COLS // NDEV
SH_PWIN = 2 * E // NDEV
SH_ROWS = E // NDEV
SH_GRP = PG // NDEV
SH_ADA = 3 * D // NDEV
VMEM_LIMIT = 56 * 1024 * 1024

ADAM_LR, ADAM_B1, ADAM_B2, ADAM_EPS, ADAM_WD, ADAM_STEP = 0.001, 0.9, 0.999, 1e-08, 0.01, 10

MESH = pl.DeviceIdType.MESH
VMEM_SPEC = pl.BlockSpec(memory_space=pltpu.VMEM)
HBM_SPEC = pl.BlockSpec(memory_space=pltpu.HBM)


def _sds(shape, dtype):
    return jax.ShapeDtypeStruct(shape, dtype)


def _bf(a):
    return a if a.dtype == BF16 else a.astype(BF16)


def _dot(a, b):
    return lax.dot_general(_bf(a), _bf(b), (((1,), (0,)), ((), ())), preferred_element_type=F32)


def _dot_tb(a, b):
    return lax.dot_general(_bf(a), _bf(b), (((1,), (1,)), ((), ())), preferred_element_type=F32)


def _dot_ta(a, b):
    return lax.dot_general(_bf(a), _bf(b), (((0,), (0,)), ((), ())), preferred_element_type=F32)


def _bdot(a, b):
    return lax.dot_general(_bf(a), _bf(b), (((2,), (1,)), ((0,), (0,))), preferred_element_type=F32)


def _bdot_nt(a, b):
    return lax.dot_general(_bf(a), _bf(b), (((2,), (2,)), ((0,), (0,))), preferred_element_type=F32)


def _bdot_tn(a, b):
    return lax.dot_general(_bf(a), _bf(b), (((1,), (1,)), ((0,), (0,))), preferred_element_type=F32)


def _dot01(m01, x):
    hi = x.astype(BF16)
    lo = (x - hi.astype(F32)).astype(BF16)
    return _dot(m01, hi) + _dot(m01, lo)


def _rstd(x):
    return lax.rsqrt(jnp.mean(x * x, axis=-1, keepdims=True) + EPS)


def _sigmoid(x):
    return jax.nn.sigmoid(x)


def _colsum(a):
    return jnp.sum(a, axis=0, keepdims=True)


def _stack_rows(rows):
    n = rows[0].shape[-1]
    rid = lax.broadcasted_iota(jnp.int32, (16, n), 0)
    out = jnp.zeros((16, n), F32)
    for i, r in enumerate(rows):
        out = jnp.where(rid == i, r, out)
    return out


def _head_map(fn, *arrs):
    outs = [fn(*[a[:, h * DH:(h + 1) * DH] for a in arrs]) for h in range(HEADS)]
    return jnp.concatenate(outs, axis=1)


def _gla_consts():
    r = np.arange(TM)[:, None]
    c = np.arange(TM)[None, :]
    same = (r // CHUNK) == (c // CHUNK)
    tril = same & (c <= r)
    triu = same & (c >= r)
    m = np.stack([tril, triu]).astype(np.float32)
    return jnp.asarray(m, BF16), jnp.asarray(m, F32)


def _pool_consts():
    r = np.arange(TM)[:, None]
    c = np.arange(TM)[None, :]
    same = (r // GRID_W) == (c // GRID_W)
    rp, cp = r % GRID_W, c % GRID_W
    bs, inv = [], []
    for w in POOL_WINDOWS:
        lo = np.clip(rp - w // 2, 0, GRID_W)
        hi = np.clip(rp - w // 2 + w, 0, GRID_W)
        bs.append(same & (cp >= lo) & (cp < hi))
        inv.append(1.0 / (hi - lo).astype(np.float32))
    b = np.stack(bs).astype(np.float32)
    bt = np.transpose(b, (0, 2, 1))
    return jnp.asarray(b, BF16), jnp.asarray(bt, BF16), jnp.asarray(np.stack(inv), F32)


def _mesh_pos():
    x, y, c = lax.axis_index("x"), lax.axis_index("y"), lax.axis_index("c")
    return x, y, c, 4 * x + 2 * y + c


def _peer(x, y, c, k):
    return (x ^ ((k >> 2) & 1), y ^ ((k >> 1) & 1), c ^ (k & 1))


def _small_gathers(refs, ssem, rsem):
    lb_r, ps_r, c_r, cctx_r, ada_r, adab_r, lb_o, ps_o, cg_o, mod_o, lb_out, ps_out, cg_out, mod0_o, mod1_o, modc_o = refs
    x, y, cc, idx = _mesh_pos()
    srcs = [lb_r, ps_r, c_r, mod_o.at[idx]]
    mine = [lb_o.at[idx], ps_o.at[idx], cg_o.at[idx], mod_o.at[idx]]

    def remote(a, k):
        return pltpu.make_async_remote_copy(src_ref=srcs[a], dst_ref=mine[a], send_sem=ssem.at[a, k], recv_sem=rsem.at[a, k],
                                            device_id=_peer(x, y, cc, k), device_id_type=MESH)

    first = [remote(a, k) for k in range(1, NDEV) for a in (2, 0, 1)]
    for cp in first:
        cp.start()
    lb_o[idx] = lb_r[...]
    ps_o[idx] = ps_r[...]
    cg_o[idx] = c_r[...]
    for k in range(1, NDEV):
        remote(2, k).wait_recv()
    rows = _stack_rows([cg_o[i] for i in range(NDEV)] + [cctx_r[...]])
    sc = rows * _sigmoid(rows)
    for l in range(2):
        mod_o[idx, l] = _dot(sc, ada_r[l])
    second = [remote(3, k) for k in range(1, NDEV)]
    for cp in second:
        cp.start()
    for k in range(1, NDEV):
        remote(3, k).wait_recv()

    def mod_rows(l, row):
        full = jnp.concatenate([mod_o[s, l, row, :] for s in range(NDEV)], axis=1) + adab_r[l:l + 1, :]
        return [full[:, j * D:(j + 1) * D] for j in range(3)]

    me = pl.ds(idx, 1)
    for out, parts in ((mod0_o, mod_rows(0, me)), (mod1_o, mod_rows(1, me)), (modc_o, mod_rows(0, slice(NDEV, NDEV + 1)))):
        for j in range(3):
            out[j:j + 1, :] = parts[j]
    for cp in first + second:
        cp.wait_send()
    for k in range(1, NDEV):
        for a in (0, 1):
            remote(a, k).wait_recv()
    lb_out[...] = lb_o[...]
    ps_out[...] = ps_o[...]
    cg_out[...] = cg_o[...]


def _gather_order(s, core):
    k = jnp.where(s == 2, 4, jnp.where(s == 4, 2, s))
    return k ^ jnp.where((core == 1) & (s >= 2) & (s <= 5), 6, 0)


GATHER_ISSUE = (1, 2, 4, 3, 5, 6, 7)
GATHER_ICI = (2, 4, 6)
GATHER_DIRECT = (1,) + GATHER_ICI
GLA_HB = 2
RS_SLOTS = 5


def _shard_of(kind, ref, i):
    if kind == "rows":
        return ref.at[pl.ds(pl.multiple_of(i * SH_ROWS, SH_ROWS), SH_ROWS), :]
    if kind == "major":
        return ref.at[i]
    assert kind == "grp"
    return ref.at[:, pl.ds(pl.multiple_of(i * SH_GRP, SH_GRP), SH_GRP), :]


def _gather_rider(step, n_steps, forward_at, kinds, srcs, outs, ssem, rsem, lsem):
    x, y, cc, idx = _mesh_pos()
    arrays = range(len(kinds))
    mine = [_shard_of(kinds[a], outs[a], idx) for a in arrays]

    def remote(a, k):
        return pltpu.make_async_remote_copy(src_ref=srcs[a], dst_ref=mine[a], send_sem=ssem.at[a, k], recv_sem=rsem.at[a, k],
                                            device_id=_peer(x, y, cc, k), device_id_type=MESH)

    def forward(a, k):
        blk = _shard_of(kinds[a], outs[a], idx ^ k)
        return pltpu.make_async_remote_copy(src_ref=blk, dst_ref=blk, send_sem=ssem.at[a, k ^ 1], recv_sem=rsem.at[a, k ^ 1],
                                            device_id=(x, y, 1 - cc), device_id_type=MESH)

    copies = [remote(a, k) for k in GATHER_DIRECT for a in arrays]
    passed = [forward(a, k) for k in GATHER_ICI for a in arrays]
    local = [pltpu.make_async_copy(srcs[a], mine[a], lsem.at[a]) for a in arrays]

    @pl.when(step == 0)
    def _():
        for cp in copies + local:
            cp.start()

    @pl.when(step == forward_at)
    def _():
        for k in GATHER_ICI:
            for a in arrays:
                remote(a, k).wait_recv()
                forward(a, k).start()

    @pl.when(step == n_steps - 1)
    def _():
        for cp in copies + passed:
            cp.wait_send()
        for a in arrays:
            remote(a, 1).wait_recv()
        for cp in passed:
            cp.wait_recv()
        for cp in local:
            cp.wait()


def _scatter_rider(step, n_steps, kinds, grads, slots, ssem, rsem, lsem):
    x, y, cc, idx = _mesh_pos()
    arrays = range(len(kinds))
    dsts = [slots[a].at[idx] for a in arrays]

    def remote(a, k):
        px, py, pc = _peer(x, y, cc, k)
        return pltpu.make_async_remote_copy(src_ref=_shard_of(kinds[a], grads[a], 4 * px + 2 * py + pc), dst_ref=dsts[a],
                                            send_sem=ssem.at[a, k], recv_sem=rsem.at[a, k], device_id=(px, py, pc), device_id_type=MESH)

    copies = [remote(a, k) for k in GATHER_ISSUE for a in arrays]
    local = [pltpu.make_async_copy(_shard_of(kinds[a], grads[a], idx), dsts[a], lsem.at[a]) for a in arrays]

    @pl.when(step == 0)
    def _():
        for cp in copies + local:
            cp.start()

    @pl.when(step == n_steps - 1)
    def _():
        for cp in copies:
            cp.wait_send()
        for cp in copies:
            cp.wait_recv()
        for cp in local:
            cp.wait()


def _rider_sems(n):
    return [pltpu.SemaphoreType.DMA((n, NDEV)), pltpu.SemaphoreType.DMA((n, NDEV)), pltpu.SemaphoreType.DMA((n,))]


def _scatter_rider2(step, n_steps, add_at, kinds, grads, slots, bufs, sems):
    x, y, cc, idx = _mesh_pos()
    sibling = (x, y, 1 - cc)
    arrays = range(len(kinds))
    psend, precv, isend, irecv, lown, sibsem, lself = sems

    def mine(a, i):
        return _shard_of(kinds[a], grads[a], i)

    def partial(a, p):
        return pltpu.make_async_remote_copy(src_ref=mine(a, idx ^ (2 * (p + 1)) ^ 1), dst_ref=bufs[a][1].at[p], send_sem=psend.at[a, p],
                                            recv_sem=precv.at[a, p], device_id=sibling, device_id_type=MESH)

    def load(a, p):
        return pltpu.make_async_copy(mine(a, idx ^ (2 * (p + 1))), bufs[a][0].at[p], lown.at[a, p])

    def chip_sum(a, p):
        return pltpu.make_async_remote_copy(src_ref=bufs[a][0].at[p], dst_ref=slots[a].at[2 + p], send_sem=isend.at[a, p],
                                            recv_sem=irecv.at[a, p], device_id=_peer(x, y, cc, 2 * (p + 1)), device_id_type=MESH)

    def to_sibling(a):
        return pltpu.make_async_remote_copy(src_ref=mine(a, idx ^ 1), dst_ref=slots[a].at[1], send_sem=sibsem.at[a, 0],
                                            recv_sem=sibsem.at[a, 1], device_id=sibling, device_id_type=MESH)

    def own(a):
        return pltpu.make_async_copy(mine(a, idx), slots[a].at[0], lself.at[a, 0])

    @pl.when(step == 0)
    def _():
        for a in arrays:
            for p in range(3):
                partial(a, p).start()
                load(a, p).start()
            to_sibling(a).start()
            own(a).start()

    @pl.when(step == add_at)
    def _():
        for a in arrays:
            for p in range(3):
                partial(a, p).wait_recv()
                load(a, p).wait()
                bufs[a][0][p] = (bufs[a][0][p].astype(F32) + bufs[a][1][p].astype(F32)).astype(BF16)
                chip_sum(a, p).start()

    @pl.when(step == n_steps - 1)
    def _():
        for a in arrays:
            for p in range(3):
                partial(a, p).wait_send()
                chip_sum(a, p).wait_send()
                chip_sum(a, p).wait_recv()
            to_sibling(a).wait_send()
            to_sibling(a).wait_recv()
            own(a).wait()


def _rider2_scratch(blocks):
    n = len(blocks)
    bufs = [pltpu.VMEM((3,) + tuple(b), BF16) for b in blocks for _ in range(2)]
    return bufs + [pltpu.SemaphoreType.DMA((n, 3)) for _ in range(5)] + [pltpu.SemaphoreType.DMA((n, 2)), pltpu.SemaphoreType.DMA((n, 1))]


def _rider2_split(refs, n):
    refs = list(refs)
    return [tuple(refs[2 * a:2 * a + 2]) for a in range(n)], tuple(refs[2 * n:2 * n + 7])


def _modulated(x, nw, shift, scale):
    r = _rstd(x)
    xn = x * r
    a = xn * nw
    return a * (1.0 + scale) + shift, r, xn, a


def _ctx_or_x(i, ctx_ref, x_ref):
    return jnp.where(i == 0, ctx_ref[...], x_ref[...])


def _f1_gather_matmul(idx1, ctx, x, nw, w_in, w_out, pw_in, pgrp, pw_out, lb_l, pscale, c, c_ctx, ada_w, ada_b):
    def body(idx_ref, ctx_ref, x_ref, nw_ref, win_r, wout_r, pwin_r, pgrp_r, pwout_r, lb_r, ps_r, c_r, cctx_r, ada_r, adab_r,
             g_ref, win_o, s_wout, s_pwin, s_pgrp, s_pwout, lb_o, ps_o, cg_o, mod0_o, mod1_o, modc_o,
             wslot, hx_sc, lb_g, ps_g, cg_g, mod_g, ssem, rsem, osem, dsem, sm_ssem, sm_rsem):
        del idx_ref
        s, i = pl.program_id(0), pl.program_id(1)
        x, y, cc, idx = _mesh_pos()
        k = _gather_order(s, cc)
        j = idx ^ k
        first = 4 - 2 * cc

        def remote(kk):
            return pltpu.make_async_remote_copy(src_ref=wslot.at[idx], dst_ref=wslot.at[idx], send_sem=ssem.at[kk], recv_sem=rsem.at[kk],
                                                device_id=_peer(x, y, cc, kk), device_id_type=MESH)

        def forward(kk):
            jj = idx ^ kk
            return pltpu.make_async_remote_copy(src_ref=wslot.at[jj], dst_ref=wslot.at[jj], send_sem=ssem.at[kk ^ 1],
                                                recv_sem=rsem.at[kk ^ 1], device_id=(x, y, 1 - cc), device_id_type=MESH)

        def relay(h):
            blk = wslot.at[idx ^ (4 >> h), pl.ds(h * (D // 2), D // 2), :]
            return pltpu.make_async_remote_copy(src_ref=blk, dst_ref=blk, send_sem=dsem.at[0, h], recv_sem=dsem.at[1, h],
                                                device_id=_peer(x, y, cc, 2 << h), device_id_type=MESH)

        def to_hbm(jj, kk):
            return pltpu.make_async_copy(wslot.at[jj], win_o.at[jj], osem.at[kk])

        @pl.when((s == 0) & (i == 0))
        def _():
            _small_gathers((lb_r, ps_r, c_r, cctx_r, ada_r, adab_r, lb_g, ps_g, cg_g, mod_g, lb_o, ps_o, cg_o, mod0_o, mod1_o, modc_o),
                           sm_ssem, sm_rsem)
            wslot[idx] = win_r[...].astype(BF16)
            remote(1).start()
            remote(first).start()
            s_wout[...] = wout_r[...].astype(BF16)
            s_pwin[...] = pwin_r[...].astype(BF16)
            s_pgrp[...] = pgrp_r[...].astype(BF16)
            s_pwout[...] = pwout_r[...].astype(BF16)

        @pl.when(s == 0)
        def _():
            shift = jnp.where(i == 0, modc_o[0:1, :], mod0_o[0:1, :])
            scale = jnp.where(i == 0, modc_o[1:2, :], mod0_o[1:2, :])
            hx, _, _, _ = _modulated(_ctx_or_x(i, ctx_ref, x_ref), nw_ref[...], shift, scale)
            hx_sc[i] = hx.astype(BF16)

        @pl.when((s == 2) & (i == 0))
        def _():
            remote(6 - first).start()

        @pl.when((s > 0) & (i == 0) & (k != 6))
        def _():
            remote(k).wait_recv()

            @pl.when((k & 1) == 0)
            def _():
                forward(k).start()

            for h in range(2):
                @pl.when(k == 4 >> h)
                def _():
                    relay(h).start()

        @pl.when((i == 0) & (k == 6))
        def _():
            for h in range(2):
                relay(h).wait_recv()
            forward(6).start()

        @pl.when(i == 0)
        def _():
            to_hbm(j, k).start()

        g_ref[...] = jnp.dot(hx_sc[i], wslot[j], preferred_element_type=F32)

        @pl.when((s == NDEV - 1) & (i == NT - 1))
        def _():
            for kk in (1, 2, 4):
                remote(kk).wait_send()
            for kk in GATHER_ICI:
                forward(kk).wait_send()
            for h in range(2):
                relay(h).wait_send()
            for kk in range(NDEV):
                to_hbm(idx ^ kk, kk).wait()

    grid_spec = pltpu.PrefetchScalarGridSpec(
        num_scalar_prefetch=1, grid=(NDEV, NT),
        in_specs=[VMEM_SPEC, pl.BlockSpec((TM, D), lambda s, i, ix: (jnp.where(s == 0, jnp.maximum(i - 1, 0), NTX - 1), 0))]
        + [VMEM_SPEC] * 12,
        out_specs=[pl.BlockSpec((TM, SH_WIN), lambda s, i, ix: (i, ix[0] ^ _gather_order(s, ix[0] & 1))), HBM_SPEC] + [VMEM_SPEC] * 10,
        scratch_shapes=[pltpu.VMEM((NDEV, D, SH_WIN), BF16), pltpu.VMEM((NT, TM, D), BF16),
                        pltpu.VMEM((NDEV, 2, DH), F32), pltpu.VMEM((NDEV, 1, DH), F32), pltpu.VMEM((NDEV, 1, D), F32),
                        pltpu.VMEM((NDEV, 2, 16, SH_ADA), F32),
                        pltpu.SemaphoreType.DMA((NDEV,)), pltpu.SemaphoreType.DMA((NDEV,)), pltpu.SemaphoreType.DMA((NDEV,)),
                        pltpu.SemaphoreType.DMA((2, 2)),
                        pltpu.SemaphoreType.DMA((4, NDEV)), pltpu.SemaphoreType.DMA((4, NDEV))])
    outs = (_sds((TT, WIN_COLS), F32), _sds((NDEV, D, SH_WIN), BF16),
            _sds((SH_ROWS, D), BF16), _sds((D, SH_PWIN), BF16), _sds((4, SH_GRP, PG), BF16), _sds((SH_ROWS, D), BF16),
            _sds((NDEV, 2, DH), F32), _sds((NDEV, 1, DH), F32), _sds((NDEV, 1, D), F32),
            _sds((3, D), F32), _sds((3, D), F32), _sds((3, D), F32))
    return pl.pallas_call(
        body, name="f1_gather_matmul", grid_spec=grid_spec, out_shape=outs,
        compiler_params=pltpu.CompilerParams(dimension_semantics=("arbitrary", "arbitrary"), vmem_limit_bytes=VMEM_LIMIT),
    )(idx1, ctx, x, nw, w_in, w_out, pw_in, pgrp, pw_out, lb_l, pscale, c, c_ctx, ada_w, ada_b)


def _gla_gates(pre, qpre, lbd, cum, rev):
    rows, n = pre.shape
    nch = rows // CHUNK
    sig = _sigmoid(pre)
    f = lbd + (1.0 - lbd) * sig
    k = 1.0 - f
    g = _dot01(cum, jnp.log(f))
    g3 = g.reshape(nch, CHUNK, n)
    last = 0 if rev else CHUNK - 1
    mid = CHUNK // 2 if rev else CHUNK // 2 - 1
    gl1, gm1 = g3[:, last:last + 1, :], g3[:, mid:mid + 1, :]

    def bc(a):
        return jnp.broadcast_to(a, g3.shape).reshape(rows, n)

    gm = bc(gm1)
    e_q, e_k = jnp.exp(g - gm), jnp.exp(gm - g)
    qsig = _sigmoid(qpre)
    qs = qpre * qsig * (DH ** -0.5)
    return dict(sig=sig, f=f, k=k, qsig=qsig, qs=qs, e_q=e_q, e_k=e_k,
                e_mid=[jnp.exp(gm1[ci]) for ci in range(nch)], e_rest=[jnp.exp(gl1[ci] - gm1[ci]) for ci in range(nch)])


def _put_heads(ref, lead, arr):
    for h in range(HEADS):
        ref[lead + (h,)] = arr[:, h * DH:(h + 1) * DH]


def _get_heads(ref, lead=()):
    return jnp.concatenate([ref[lead + (h,)] for h in range(HEADS)], axis=1)


def _gla_prep(g_all, lb, cum01, s_wout, s_pgrp):
    nch = TM // CHUNK

    def body(g_ref, lb_ref, cum_ref, swout_r, spgrp_r, p0_ref, p1_ref, v_ref, dec_ref, wout_o, pgrp_o, ssem, rsem, lsem):
        _gather_rider(pl.program_id(0), NT, NT - 1, ("rows", "grp"), (swout_r, spgrp_r), (wout_o, pgrp_o), ssem, rsem, lsem)
        qpre = g_ref[:, 3 * E:4 * E]
        _put_heads(v_ref, (), g_ref[:, 2 * E:3 * E].astype(BF16))
        for d, p_ref in ((0, p0_ref), (1, p1_ref)):
            t = _gla_gates(g_ref[:, d * E:(d + 1) * E], qpre, lb_ref[d:d + 1, :], cum_ref[d], d == 1)
            _put_heads(p_ref, (0,), (t["qs"] * t["e_q"]).astype(BF16))
            _put_heads(p_ref, (1,), (t["k"] * t["e_k"]).astype(BF16))
            for ci in range(nch):
                dec_ref[d, 0, ci:ci + 1, :] = t["e_mid"][ci]
                dec_ref[d, 0, nch + ci:nch + ci + 1, :] = t["e_rest"][ci]

    quad = pl.BlockSpec((2, HEADS, TM, DH), lambda i: (0, 0, i, 0))
    return pl.pallas_call(
        body, name="gla_prep", grid=(NT,),
        in_specs=[pl.BlockSpec((TM, 4 * E), lambda i: (i, 0)), VMEM_SPEC, VMEM_SPEC, HBM_SPEC, HBM_SPEC],
        out_specs=[quad, quad, pl.BlockSpec((HEADS, TM, DH), lambda i: (0, i, 0)), pl.BlockSpec((2, 1, 2 * nch, E), lambda i: (0, i, 0, 0)),
                   HBM_SPEC, HBM_SPEC],
        out_shape=(_sds((2, HEADS, TT, DH), BF16), _sds((2, HEADS, TT, DH), BF16), _sds((HEADS, TT, DH), BF16), _sds((2, NT, 2 * nch, E), F32),
                   _sds((E, D), BF16), _sds((4, PG, PG), BF16)),
        scratch_shapes=_rider_sems(2),
        compiler_params=pltpu.CompilerParams(dimension_semantics=("arbitrary",), vmem_limit_bytes=VMEM_LIMIT),
    )(g_all, lb, cum01, s_wout, s_pgrp)


def _scan_tile(i, rev):
    t = jnp.where(i == 0, 0, NT - i) if rev else i
    return t, pl.ds(pl.multiple_of(t * TM, TM), TM)


def _chunk_rows(dec_ref, lanes, cis, where):
    nch = TM // CHUNK

    def rows(off):
        return jnp.stack([dec_ref[d, where[d][0], off + ci:off + ci + 1, hh * DH:(hh + 1) * DH] for (d, hh), ci in zip(lanes, cis)])

    return rows(0), rows(nch)


def _gla_fwd(p0, p1, v_all, dec, mask01, s_pwin, s_pwout):
    n_steps = HEADS // GLA_HB

    def body(p0_ref, p1_ref, v_ref, dec_ref, msk_ref, spwin_r, spwout_r, o_ref, pwin_o, pwout_o, ob_sc, ssem, rsem, lsem):
        _gather_rider(pl.program_id(0), n_steps, n_steps - 1, ("major", "rows"), (spwin_r, spwout_r), (pwin_o, pwout_o), ssem, rsem, lsem)

        lanes = [(d, hh) for d in (0, 1) for hh in range(GLA_HB)]
        nch = TM // CHUNK

        def tile_body(i, st):
            where = [_scan_tile(i, d == 1) for d in (0, 1)]

            def stacked(fn):
                return jnp.stack([fn(d, hh, where[d][1]) for d, hh in lanes])

            qg, kg = [stacked(lambda d, hh, rows, ty=ty: (p1_ref if d else p0_ref)[ty, hh, rows, :]) for ty in range(2)]
            v = stacked(lambda d, hh, rows: v_ref[hh, rows, :])
            a = _bdot_nt(qg, kg) * jnp.stack([msk_ref[d] for d, _ in lanes])
            intra = _bdot(a, v)
            outs = [[None] * nch for _ in lanes]
            for n in range(nch):
                cis = [nch - 1 - n if d else n for d, _ in lanes]

                def chunk(arr):
                    return jnp.stack([arr[l, ci * CHUNK:(ci + 1) * CHUNK] for l, ci in enumerate(cis)])

                e_mid, e_rest = _chunk_rows(dec_ref, lanes, cis, where)
                inter = _bdot_nt(chunk(qg), st * e_mid)
                for l, ci in enumerate(cis):
                    outs[l][ci] = inter[l] + intra[l, ci * CHUNK:(ci + 1) * CHUNK]
                st = st * (e_mid * e_rest) + _bdot_tn(chunk(v), chunk(kg)) * e_rest
            for l, (d, hh) in enumerate(lanes):
                (ob_sc if d else o_ref)[hh, where[d][1], :] = jnp.concatenate(outs[l], axis=0)
            return st

        lax.fori_loop(0, NT, tile_body, jnp.zeros((len(lanes), DH, DH), F32))
        o_ref[...] += ob_sc[...]

    quad = pl.BlockSpec((2, GLA_HB, TT, DH), lambda h: (0, h, 0, 0))
    head = pl.BlockSpec((GLA_HB, TT, DH), lambda h: (h, 0, 0))
    return pl.pallas_call(
        body, name="gla_fwd", grid=(n_steps,),
        in_specs=[quad, quad, head, pl.BlockSpec((2, NT, 8, GLA_HB * DH), lambda h: (0, 0, 0, h)),
                  pl.BlockSpec((2, TM, TM), lambda h: (0, 0, 0)), HBM_SPEC, HBM_SPEC],
        out_specs=[head, HBM_SPEC, HBM_SPEC],
        out_shape=(_sds((HEADS, TT, DH), F32), _sds((NDEV, D, SH_PWIN), BF16), _sds((E, D), BF16)),
        scratch_shapes=[pltpu.VMEM((GLA_HB, TT, DH), F32)] + _rider_sems(2),
        compiler_params=pltpu.CompilerParams(dimension_semantics=("arbitrary",), vmem_limit_bytes=VMEM_LIMIT),
    )(p0, p1, v_all, dec, mask01, s_pwin, s_pwout)


def _gated_norm(o, z, gw):
    r = _head_map(lambda oh: jnp.broadcast_to(_rstd(oh), oh.shape), o)
    on = o * r
    zs = _sigmoid(z)
    sz = z * zs
    return on * gw * sz, r, on, zs, sz


def _f3_out(o, g_all, x, gate, gw, wout):
    def body(o_ref, z_ref, x_ref, gate_ref, gw_ref, w_ref, x1_ref):
        og, _, _, _, _ = _gated_norm(_get_heads(o_ref), z_ref[...], gw_ref[...])
        x1_ref[...] = x_ref[...] + gate_ref[...] * _dot(og, w_ref[...])

    return pl.pallas_call(
        body, name="f3_out", grid=(NTX,),
        in_specs=[pl.BlockSpec((HEADS, TM, DH), lambda i: (0, i + 1, 0)), pl.BlockSpec((TM, E), lambda i: (i + 1, 4)),
                  pl.BlockSpec((TM, D), lambda i: (i, 0)), pl.BlockSpec((1, D), lambda i: (0, 0)),
                  pl.BlockSpec((1, E), lambda i: (0, 0)), pl.BlockSpec((E, D), lambda i: (0, 0))],
        out_specs=pl.BlockSpec((TM, D), lambda i: (i, 0)),
        out_shape=_sds((T, D), F32),
        compiler_params=pltpu.CompilerParams(dimension_semantics=("arbitrary",)),
    )(o, g_all, x, gate, gw, wout)


def _pool_layer(x1, tgt, mod1, nw1, fnw, pwin, pgrp, pscale, pwout, pb, pbt, pinv):
    def body(x_ref, t_ref, m_ref, nw_ref, fw_ref, pwin_ref, pgrp_ref, ps_ref, pwout_ref, pb_ref, pbt_ref, pinv_ref,
             dx_ref, gpwin_o, gpgrp_o, gpwout_o, dmod_o, gnw_o, gfw_o, gps_o, loss_o,
             a_pwin, a_pgrp, a_pwout):
        i = pl.program_id(0)

        @pl.when(i == 0)
        def _():
            for ref in (a_pwin, a_pgrp, a_pwout, dmod_o, gnw_o, gfw_o, gps_o, loss_o):
                ref[...] = jnp.zeros_like(ref)

        shift, scale, gate = m_ref[0:1, :], m_ref[1:2, :], m_ref[2:3, :]
        nw, fw, ps = nw_ref[...], fw_ref[...], ps_ref[...]
        x1 = x_ref[...]
        hx, r1, xn, a = _modulated(x1, nw, shift, scale)
        hxb = hx.astype(BF16)
        uz = jnp.concatenate([_dot(hxb, pwin_ref[j]) for j in range(NDEV)], axis=1)
        u, z = uz[:, :E], uz[:, E:]
        pooled, ys = [], []
        for g in range(4):
            ug = u[:, g * PG:(g + 1) * PG]
            pg = _dot01(pb_ref[g], ug) * pinv_ref[g] - ug
            pooled.append(pg.astype(BF16))
            ys.append(_dot(pooled[g], pgrp_ref[g]))
        ycat = jnp.concatenate(ys, axis=1)
        y = ycat * ps
        zs = _sigmoid(z)
        sz = z * zs
        p = (y * sz).astype(BF16)
        out = _dot(p, pwout_ref[...])
        x2 = x1 + gate * out
        r2 = _rstd(x2)
        xn2 = x2 * r2
        diff = xn2 * fw - t_ref[...]
        loss_o[...] += _colsum(diff * diff)
        dyf = diff * (1.0 / D)
        gfw_o[...] += _colsum(dyf * xn2)
        dxn2 = dyf * fw
        dx2 = r2 * (dxn2 - xn2 * jnp.mean(dxn2 * xn2, axis=-1, keepdims=True))
        dgate = _colsum(dx2 * out)
        dout = (dx2 * gate).astype(BF16)
        for j in range(4):
            cs = slice(j * PG, (j + 1) * PG)
            a_pwout[:, cs] += _dot_ta(p, dout[:, cs])
        dp = _dot_tb(dout, pwout_ref[...])
        dy = dp * sz
        dz = dp * y * (zs * (1.0 + z * (1.0 - zs)))
        gps_o[...] += _colsum(dy * ycat)
        dycat = dy * ps
        dus = []
        for g in range(4):
            dyg = dycat[:, g * PG:(g + 1) * PG].astype(BF16)
            a_pgrp[g] += _dot_ta(pooled[g], dyg)
            dpg = _dot_tb(dyg, pgrp_ref[g])
            dus.append(_dot01(pbt_ref[g], dpg * pinv_ref[g]) - dpg)
        duz = jnp.concatenate(dus + [dz], axis=1).astype(BF16)
        dhx = None
        for j in range(NDEV):
            dj = duz[:, j * SH_PWIN:(j + 1) * SH_PWIN]
            a_pwin[j] += _dot_ta(hxb, dj)
            part = _dot_tb(dj, pwin_ref[j])
            dhx = part if dhx is None else dhx + part
        dmod_o[0:1, :] += _colsum(dhx)
        dmod_o[1:2, :] += _colsum(dhx * a)
        dmod_o[2:3, :] += dgate
        da = dhx * (1.0 + scale)
        gnw_o[...] += _colsum(da * xn)
        dxn = da * nw
        dx_ref[...] = dx2 + r1 * (dxn - xn * jnp.mean(dxn * xn, axis=-1, keepdims=True))

        @pl.when(i == NTX - 1)
        def _():
            gpwin_o[...] = a_pwin[...].astype(BF16)
            gpgrp_o[...] = a_pgrp[...].astype(BF16)
            gpwout_o[...] = a_pwout[...].astype(BF16)

    tile = pl.BlockSpec((TM, D), lambda i: (i, 0))
    outs = (_sds((T, D), F32), _sds((NDEV, D, SH_PWIN), BF16), _sds((4, PG, PG), BF16), _sds((E, D), BF16),
            _sds((3, D), F32), _sds((1, D), F32), _sds((1, D), F32), _sds((1, E), F32), _sds((1, D), F32))
    return pl.pallas_call(
        body, name="pool_layer", grid=(NTX,),
        in_specs=[tile, tile] + [VMEM_SPEC] * 10,
        out_specs=[tile] + [VMEM_SPEC] * 8,
        out_shape=outs,
        scratch_shapes=[pltpu.VMEM((NDEV, D, SH_PWIN), F32), pltpu.VMEM((4, PG, PG), F32), pltpu.VMEM((E, D), F32)],
        compiler_params=pltpu.CompilerParams(dimension_semantics=("arbitrary",), vmem_limit_bytes=VMEM_LIMIT),
    )(x1, tgt, mod1, nw1, fnw, pwin, pgrp, pscale, pwout, pb, pbt, pinv)


def _b3_out_bwd(dx1, o, g_all, gate, gw, wout, gpwout):
    def body(dx_ref, o_ref, z_ref, gate_ref, gw_ref, w_ref, gpwout_r, do_ref, dz_ref, gw_o, dgate_o, ggw_o, rpwout_o,
             acc, *rider):
        i = pl.program_id(0)
        bufs, sems = _rider2_split(rider, 1)
        _scatter_rider2(i, NT, 2, ("rows",), (gpwout_r,), (rpwout_o,), bufs, sems)

        @pl.when(i == 0)
        def _():
            acc[...] = jnp.zeros_like(acc)
            dgate_o[...] = jnp.zeros_like(dgate_o)
            ggw_o[...] = jnp.zeros_like(ggw_o)
            do_ref[...] = jnp.zeros_like(do_ref)
            dz_ref[...] = jnp.zeros_like(dz_ref)

        @pl.when(i > 0)
        def _():
            gw = gw_ref[...]
            z = z_ref[...]
            og, r, on, zs, sz = _gated_norm(_get_heads(o_ref), z, gw)
            ogb = og.astype(BF16)
            dx = dx_ref[...]
            dgate_o[...] += _colsum(dx * _dot(ogb, w_ref[...]))
            dy = (dx * gate_ref[...]).astype(BF16)
            for j in range(4):
                cs = slice(j * PG, (j + 1) * PG)
                acc[:, cs] += _dot_ta(ogb, dy[:, cs])
            dog = _dot_tb(dy, w_ref[...])
            dz_ref[...] = (dog * (on * gw) * (zs * (1.0 + z * (1.0 - zs)))).astype(BF16)
            dong = dog * sz
            ggw_o[...] += _colsum(dong * on)
            don = dong * gw
            do = _head_map(lambda dh, nh, rh: rh * (dh - nh * jnp.mean(dh * nh, axis=-1, keepdims=True)), don, on, r)
            _put_heads(do_ref, (), do.astype(BF16))

        @pl.when(i == NT - 1)
        def _():
            gw_o[...] = acc[...].astype(BF16)

    prev = lambda i: (jnp.maximum(i - 1, 0), 0)
    heads = pl.BlockSpec((HEADS, TM, DH), lambda i: (0, i, 0))
    return pl.pallas_call(
        body, name="b3_out_bwd", grid=(NT,),
        in_specs=[pl.BlockSpec((TM, D), prev), heads, pl.BlockSpec((TM, E), lambda i: (i, 4)),
                  VMEM_SPEC, VMEM_SPEC, VMEM_SPEC, HBM_SPEC],
        out_specs=[heads, pl.BlockSpec((TM, E), lambda i: (i, 0)), VMEM_SPEC, VMEM_SPEC, VMEM_SPEC, HBM_SPEC],
        out_shape=(_sds((HEADS, TT, DH), BF16), _sds((TT, E), BF16), _sds((E, D), BF16), _sds((1, D), F32), _sds((1, E), F32),
                   _sds((RS_SLOTS, SH_ROWS, D), BF16)),
        scratch_shapes=[pltpu.VMEM((E, D), F32)] + _rider2_scratch([(SH_ROWS, D)]),
        compiler_params=pltpu.CompilerParams(dimension_semantics=("arbitrary",), vmem_limit_bytes=VMEM_LIMIT),
    )(dx1, o, g_all, gate, gw, wout, gpwout)


def _gla_bwd(p0, p1, v_all, dec, do, mask01, gpwin, gpgrp):
    nch = TM // CHUNK
    n_steps = HEADS // GLA_HB

    def body(p0_ref, p1_ref, v_ref, dec_ref, do_ref, msk_ref, gpwin_r, gpgrp_r, d0_ref, d1_ref, dv_ref, dgl_ref, rpwin_o, rpgrp_o,
             ss_sc, dv_sc, ssem, rsem, lsem, *rider):
        _scatter_rider(pl.program_id(0), n_steps, ("grp",), (gpgrp_r,), (rpgrp_o,), ssem, rsem, lsem)
        bufs, sems = _rider2_split(rider, 1)
        _scatter_rider2(pl.program_id(0), n_steps, 1, ("major",), (gpwin_r,), (rpwin_o,), bufs, sems)

        lanes = [(d, hh) for d in (0, 1) for hh in range(GLA_HB)]
        zero = jnp.zeros((len(lanes), DH, DH), F32)
        dgl_ref[...] = jnp.zeros_like(dgl_ref)

        def p_of(d):
            return p1_ref if d else p0_ref

        def scan_step(i, n):
            where = [_scan_tile(i, d == 1) for d in (0, 1)]
            cis = [nch - 1 - n if d else n for d, _ in lanes]
            e_mid, e_rest = _chunk_rows(dec_ref, lanes, cis, where)

            def chunk(arr):
                return jnp.stack([arr[l, ci * CHUNK:(ci + 1) * CHUNK] for l, ci in enumerate(cis)])

            return where, cis, e_mid, e_rest, chunk

        def stacked(i, fn):
            where = [_scan_tile(i, d == 1) for d in (0, 1)]
            return jnp.stack([fn(d, hh, where[d][1]) for d, hh in lanes])

        def fwd_body(i, st):
            v = stacked(i, lambda d, hh, rows: v_ref[hh, rows, :])
            kg = stacked(i, lambda d, hh, rows: p_of(d)[1, hh, rows, :])
            for n in range(nch):
                _, _, e_mid, e_rest, chunk = scan_step(i, n)
                ss_sc[i * nch + n] = st
                st = st * (e_mid * e_rest) + _bdot_tn(chunk(v), chunk(kg)) * e_rest
            return st

        ss_sc[NT * nch] = lax.fori_loop(0, NT, fwd_body, zero)

        def bwd_body(ii, dst):
            i = NT - 1 - ii
            qg, kg = [stacked(i, lambda d, hh, rows, ty=ty: p_of(d)[ty, hh, rows, :]) for ty in range(2)]
            v = stacked(i, lambda d, hh, rows: v_ref[hh, rows, :])
            dob = stacked(i, lambda d, hh, rows: do_ref[hh, rows, :])
            msk = jnp.stack([msk_ref[d] for d, _ in lanes])
            a = (_bdot_nt(qg, kg) * msk).astype(BF16)
            da = (_bdot_nt(dob, v) * msk).astype(BF16)
            dqg = _bdot(da, kg)
            dkg = _bdot_tn(da, qg)
            dv_intra = _bdot_tn(a, dob)
            dv_l, dkg_l, dqg_l = ([[None] * nch for _ in lanes] for _ in range(3))
            for n in range(nch - 1, -1, -1):
                where, cis, e_mid, e_rest, chunk = scan_step(i, n)
                s_c, s_end = ss_sc[i * nch + n], ss_sc[i * nch + n + 1]
                dste = (dst * e_rest).astype(BF16)
                kg_c, v_c, dob_c = chunk(kg), chunk(v), chunk(dob)
                dv_c = chunk(dv_intra) + _bdot_nt(kg_c, dste)
                dkg_c = chunk(dkg) + _bdot(v_c, dste)
                dqg_c = chunk(dqg) + _bdot(dob_c, s_c * e_mid)
                dgl = jnp.sum(s_end * dst, axis=1, keepdims=True)
                for l, ((d, hh), ci) in enumerate(zip(lanes, cis)):
                    dv_l[l][ci], dkg_l[l][ci], dqg_l[l][ci] = dv_c[l], dkg_c[l], dqg_c[l]
                    dgl_ref[d, where[d][0], ci:ci + 1, hh * DH:(hh + 1) * DH] = dgl[l]
                dst = dst * (e_mid * e_rest) + _bdot_tn(dob_c, chunk(qg)) * e_mid
            where = [_scan_tile(i, d == 1) for d in (0, 1)]
            for l, (d, hh) in enumerate(lanes):
                rows = where[d][1]
                d_ref = d1_ref if d else d0_ref
                d_ref[0, hh, rows, :] = jnp.concatenate(dqg_l[l], axis=0).astype(BF16)
                d_ref[1, hh, rows, :] = jnp.concatenate(dkg_l[l], axis=0).astype(BF16)
                dv_sc[d, hh, rows, :] = jnp.concatenate(dv_l[l], axis=0).astype(BF16)
            return dst

        lax.fori_loop(0, NT, bwd_body, zero)
        dv_ref[...] = (dv_sc[0].astype(F32) + dv_sc[1].astype(F32)).astype(BF16)

    quad = pl.BlockSpec((2, GLA_HB, TT, DH), lambda h: (0, h, 0, 0))
    col = pl.BlockSpec((GLA_HB, TT, DH), lambda h: (h, 0, 0))
    chunkv = pl.BlockSpec((2, NT, 8, GLA_HB * DH), lambda h: (0, 0, 0, h))
    outs = (_sds((2, HEADS, TT, DH), BF16), _sds((2, HEADS, TT, DH), BF16), _sds((HEADS, TT, DH), BF16), _sds((2, NT, 8, E), F32),
            _sds((RS_SLOTS, D, SH_PWIN), BF16), _sds((NDEV, 4, SH_GRP, PG), BF16))
    return pl.pallas_call(
        body, name="gla_bwd", grid=(n_steps,),
        in_specs=[quad, quad, col, chunkv, col, pl.BlockSpec((2, TM, TM), lambda h: (0, 0, 0)), HBM_SPEC, HBM_SPEC],
        out_specs=[quad, quad, col, chunkv, HBM_SPEC, HBM_SPEC],
        out_shape=outs,
        scratch_shapes=[pltpu.VMEM((NT * nch + 1, 2 * GLA_HB, DH, DH), F32), pltpu.VMEM((2, GLA_HB, TT, DH), BF16)] + _rider_sems(1)
        + _rider2_scratch([(D, SH_PWIN)]),
        compiler_params=pltpu.CompilerParams(dimension_semantics=("arbitrary",), vmem_limit_bytes=VMEM_LIMIT),
    )(p0, p1, v_all, dec, do, mask01, gpwin, gpgrp)


TMB = 128


def _gla_post_bwd(g_all, d0, d1, dgl, dv, dz, lb, cum01, gwout):
    nch = TMB // CHUNK

    def body(g_ref, d0_ref, d1_ref, dgl_ref, dv_ref, dz_ref, lb_ref, cum_ref, gwout_r, dg_ref, dlb_ref, rwout_o, *rider):
        i = pl.program_id(0)
        bufs, sems = _rider2_split(rider, 1)
        _scatter_rider2(i, TT // TMB, 2, ("rows",), (gwout_r,), (rwout_o,), bufs, sems)

        @pl.when(i == 0)
        def _():
            dlb_ref[...] = jnp.zeros_like(dlb_ref)

        half = i & 1
        qpre = g_ref[:, 3 * E:4 * E]
        dqs_sum = None
        dpre = []
        for d, d_ref in ((0, d0_ref), (1, d1_ref)):
            rev = d == 1
            lbd = lb_ref[d:d + 1, :]
            t = _gla_gates(g_ref[:, d * E:(d + 1) * E], qpre, lbd, cum_ref[d, :TMB, :TMB], rev)
            dqs = _get_heads(d_ref, (0,)).astype(F32) * t["e_q"]
            dk = _get_heads(d_ref, (1,)).astype(F32) * t["e_k"]
            dg = t["qs"] * dqs - t["k"] * dk
            dgl8 = dgl_ref[d, 0]
            dgl_rows = [jnp.where(half == 0, dgl8[ci:ci + 1, :], dgl8[nch + ci:nch + ci + 1, :]) for ci in range(nch)]
            dgl_b = jnp.concatenate([jnp.broadcast_to(dgl_rows[ci], (CHUNK, E)) for ci in range(nch)], axis=0)
            pos = lax.broadcasted_iota(jnp.int32, (TMB, E), 0) & (CHUNK - 1)
            dg = dg + jnp.where(pos == (0 if rev else CHUNK - 1), dgl_b, 0.0)
            dlf = _dot01(cum_ref[1 - d, :TMB, :TMB], dg)
            df = dlf / t["f"] - dk
            sig = t["sig"]
            dpre.append((df * (1.0 - lbd) * sig * (1.0 - sig)).astype(BF16))
            dlb_ref[d:d + 1, :] += _colsum(df * (1.0 - sig))
            dqs_sum = dqs if dqs_sum is None else dqs_sum + dqs
            qsig = t["qsig"]
        dqpre = dqs_sum * (DH ** -0.5) * (qsig * (1.0 + qpre * (1.0 - qsig)))
        row = jnp.concatenate([dpre[0], dpre[1], _get_heads(dv_ref), dqpre.astype(BF16), dz_ref[...]], axis=1)
        for j in range(NDEV):
            dg_ref[j] = row[:, j * SH_WIN:(j + 1) * SH_WIN]

    quad = pl.BlockSpec((2, HEADS, TMB, DH), lambda i: (0, 0, i, 0))
    tile = pl.BlockSpec((TMB, E), lambda i: (i, 0))
    return pl.pallas_call(
        body, name="gla_post_bwd", grid=(TT // TMB,),
        in_specs=[pl.BlockSpec((TMB, 4 * E), lambda i: (i, 0)), quad, quad,
                  pl.BlockSpec((2, 1, 8, E), lambda i: (0, i // 2, 0, 0)), pl.BlockSpec((HEADS, TMB, DH), lambda i: (0, i, 0)), tile,
                  VMEM_SPEC, VMEM_SPEC, HBM_SPEC],
        out_specs=[pl.BlockSpec((NDEV, TMB, SH_WIN), lambda i: (0, i, 0)), VMEM_SPEC, HBM_SPEC],
        out_shape=(_sds((NDEV, TT, SH_WIN), BF16), _sds((2, E), F32), _sds((RS_SLOTS, SH_ROWS, D), BF16)),
        scratch_shapes=_rider2_scratch([(SH_ROWS, D)]),
        compiler_params=pltpu.CompilerParams(dimension_semantics=("arbitrary",), vmem_limit_bytes=VMEM_LIMIT),
    )(g_all, d0, d1, dgl, dv, dz, lb, cum01, gwout)


WIN_SLOTS = 4


def _scatter_order(s, core):
    return (NDEV - 1 - s) ^ jnp.where((s >= 2) & (s <= 5) & ((s & 1) == core), 6, 0)


def _b1_in_bwd(idx1, ctx, x, dx1, dg, nw, msel, win):
    last_s = NDEV - 1
    half = D // 2

    def body(idx_ref, ctx_ref, x_ref, dx1_ref, dg_ref, nw_ref, m_ref, w_ref, gx_ref, rwin_o, dmx_o, dmc_o, gnw_o,
             hx_sc, dhx_sc, acc, sbuf, pbuf, rbuf, psend, precv, isend, irecv, dsend, drecv, sibsem, lsem):
        del idx_ref
        s, i = pl.program_id(0), pl.program_id(1)
        x, y, cc, idx = _mesh_pos()
        shift, scale = m_ref[0, 0:1, :], m_ref[0, 1:2, :]
        sibling = (x, y, 1 - cc)

        def partial(p):
            return pltpu.make_async_remote_copy(src_ref=sbuf.at[0], dst_ref=pbuf.at[p], send_sem=psend.at[p], recv_sem=precv.at[p],
                                                device_id=sibling, device_id_type=MESH)

        def chip_sum(p):
            return pltpu.make_async_remote_copy(src_ref=sbuf.at[1], dst_ref=rwin_o.at[2 + p], send_sem=isend.at[p], recv_sem=irecv.at[p],
                                                device_id=_peer(x, y, cc, 2 * (p + 1)), device_id_type=MESH)

        def relay(h):
            return pltpu.make_async_remote_copy(src_ref=sbuf.at[1, pl.ds(h * half, half), :], dst_ref=rbuf.at[h], send_sem=dsend.at[h],
                                                recv_sem=drecv.at[h], device_id=_peer(x, y, cc, 2 * (h + 1)), device_id_type=MESH)

        to_sibling = pltpu.make_async_remote_copy(src_ref=sbuf.at[0], dst_ref=rwin_o.at[1], send_sem=sibsem.at[0], recv_sem=sibsem.at[1],
                                                  device_id=sibling, device_id_type=MESH)
        own = pltpu.make_async_copy(sbuf.at[1], rwin_o.at[0], lsem)

        @pl.when((s == 0) & (i == 0))
        def _():
            for ref in (dmx_o, dmc_o, gnw_o):
                ref[...] = jnp.zeros_like(ref)

        @pl.when(s == 0)
        def _():
            hx, _, _, _ = _modulated(_ctx_or_x(i, ctx_ref, x_ref), nw_ref[...], shift, scale)
            hx_sc[i] = hx.astype(BF16)

        @pl.when(i == 0)
        def _():
            acc[...] = jnp.zeros_like(acc)

        dgb = dg_ref[0]
        hxb = hx_sc[i]
        for lo, hi in ((0, 256), (256, 512), (512, SH_WIN)):
            acc[:, lo:hi] += _dot_ta(hxb, dgb[:, lo:hi])
        part = _dot_tb(dgb, w_ref[0])

        @pl.when(s == 0)
        def _():
            dhx_sc[i] = part

        @pl.when(s > 0)
        def _():
            dhx_sc[i] += part

        done = i == NT - 1

        def hand_over(p, before):
            before.wait_send()
            sbuf[0] = acc[...].astype(BF16)
            partial(p).start()

        def send_chip_sum(p, before):
            for cp in before:
                cp.wait_send()
            partial(p).wait_recv()
            sbuf[1] = (acc[...] + pbuf[p].astype(F32)).astype(BF16)
            h = 1 - p
            rows = pl.ds(h * half, half)
            relay(h).wait_recv()
            sbuf[1, rows, :] = (acc[rows, :] + pbuf[p, rows, :].astype(F32) + rbuf[h].astype(F32)).astype(BF16)
            chip_sum(p).start()

        @pl.when(done & (s == 0))
        def _():
            sbuf[0] = acc[...].astype(BF16)
            partial(2).start()

        @pl.when(done & (s == 1))
        def _():
            partial(2).wait_recv()
            sbuf[1] = (acc[...] + pbuf[2].astype(F32)).astype(BF16)
            for h in range(2):
                relay(h).start()

        for core in range(2):
            @pl.when(done & (cc == core) & (s == 2))
            def _(core=core):
                hand_over(core, partial(2))

            @pl.when(done & (cc == core) & (s == 3))
            def _(core=core):
                send_chip_sum(1 - core, [relay(0), relay(1)])

            @pl.when(done & (cc == core) & (s == 4))
            def _(core=core):
                hand_over(1 - core, partial(core))

            @pl.when(done & (cc == core) & (s == 5))
            def _(core=core):
                send_chip_sum(core, [chip_sum(1 - core)])

            @pl.when(done & (cc == core) & (s == last_s - 1))
            def _(core=core):
                partial(1 - core).wait_send()
                sbuf[0] = acc[...].astype(BF16)
                to_sibling.start()

            @pl.when(done & (cc == core) & (s == last_s))
            def _(core=core):
                chip_sum(core).wait_send()
                sbuf[1] = acc[...].astype(BF16)
                own.start()

        @pl.when(s == last_s)
        def _():
            nw = nw_ref[...]
            _, r, xn, a = _modulated(_ctx_or_x(i, ctx_ref, x_ref), nw, shift, scale)
            dhx = dhx_sc[i]
            dsh, dsc = _colsum(dhx), _colsum(dhx * a)
            da = dhx * (1.0 + scale)
            gnw_o[...] += _colsum(da * xn)
            dxn = da * nw
            gx_ref[...] = dx1_ref[...] + r * (dxn - xn * jnp.mean(dxn * xn, axis=-1, keepdims=True))

            @pl.when(i == 0)
            def _():
                dmc_o[0:1, :] += dsh
                dmc_o[1:2, :] += dsc

            @pl.when(i > 0)
            def _():
                dmx_o[0:1, :] += dsh
                dmx_o[1:2, :] += dsc

        @pl.when((i == NT - 1) & (s == last_s))
        def _():
            to_sibling.wait_send()
            to_sibling.wait_recv()
            for p in range(2):
                chip_sum(p).wait_recv()
            own.wait()

    grid_spec = pltpu.PrefetchScalarGridSpec(
        num_scalar_prefetch=1, grid=(NDEV, NT),
        in_specs=[VMEM_SPEC,
                  pl.BlockSpec((TM, D), lambda s, i, ix: (jnp.where((s == 0) | (s == last_s), jnp.maximum(i - 1, 0), NTX - 1), 0)),
                  pl.BlockSpec((TM, D), lambda s, i, ix: (jnp.where(s == last_s, jnp.maximum(i - 1, 0), 0), 0)),
                  pl.BlockSpec((1, TM, SH_WIN), lambda s, i, ix: (ix[0] ^ _scatter_order(s, ix[0] & 1), i, 0)), VMEM_SPEC,
                  pl.BlockSpec((1, 2, D), lambda s, i, ix: (jnp.minimum(i, 1), 0, 0)),
                  pl.BlockSpec((1, D, SH_WIN), lambda s, i, ix: (ix[0] ^ _scatter_order(s, ix[0] & 1), 0, 0))],
        out_specs=[pl.BlockSpec((TM, D), lambda s, i, ix: (jnp.where(s == last_s, jnp.maximum(i - 1, 0), 0), 0)),
                   HBM_SPEC, VMEM_SPEC, VMEM_SPEC, VMEM_SPEC],
        scratch_shapes=[pltpu.VMEM((NT, TM, D), BF16), pltpu.VMEM((NT, TM, D), F32), pltpu.VMEM((D, SH_WIN), F32),
                        pltpu.VMEM((2, D, SH_WIN), BF16), pltpu.VMEM((3, D, SH_WIN), BF16), pltpu.VMEM((2, half, SH_WIN), BF16),
                        pltpu.SemaphoreType.DMA((3,)), pltpu.SemaphoreType.DMA((3,)), pltpu.SemaphoreType.DMA((2,)),
                        pltpu.SemaphoreType.DMA((2,)), pltpu.SemaphoreType.DMA((2,)), pltpu.SemaphoreType.DMA((2,)),
                        pltpu.SemaphoreType.DMA((2,)), pltpu.SemaphoreType.DMA])
    return pl.pallas_call(
        body, name="b1_in_bwd", grid_spec=grid_spec,
        out_shape=(_sds((T, D), F32), _sds((WIN_SLOTS, D, SH_WIN), BF16), _sds((2, D), F32), _sds((2, D), F32), _sds((1, D), F32)),
        compiler_params=pltpu.CompilerParams(dimension_semantics=("arbitrary", "arbitrary"), vmem_limit_bytes=VMEM_LIMIT),
    )(idx1, ctx, x, dx1, dg, nw, msel, win)


def _reduce_small(pd, pv, cg, c_ctx, ada_w0):
    n_arr = 3

    def body(pd_r, pv_r, cg_r, cctx_r, ada_r, gada_o, gadab_o, gcctx_o, pvsum_o, loss_o,
             pd_all, pv_all, dsc_all, dsc_mine, ssem, rsem):
        x, y, cc, idx = _mesh_pos()
        srcs = [pd_r, pv_r, dsc_mine]
        dsts = [pd_all.at[idx], pv_all.at[idx], dsc_all.at[idx]]

        def remote(a, k):
            return pltpu.make_async_remote_copy(src_ref=srcs[a], dst_ref=dsts[a], send_sem=ssem.at[a, k], recv_sem=rsem.at[a, k],
                                                device_id=_peer(x, y, cc, k), device_id_type=MESH)

        first = [remote(a, k) for k in range(1, NDEV) for a in (0, 1)]
        for cp in first:
            cp.start()
        pd_all[idx] = pd_r[...]
        pv_all[idx] = pv_r[...]
        for k in range(1, NDEV):
            remote(0, k).wait_recv()
            remote(1, k).wait_recv()
        mine = [pd_all[s, :, pl.ds(idx, 1), :] for s in range(NDEV)]
        dmc = functools.reduce(lambda u, v: u + v, [m[2] for m in mine])
        rows = _stack_rows([cg_r[i] for i in range(NDEV)] + [cctx_r[...]])
        sc = (rows * _sigmoid(rows)).astype(BF16)
        gada_o[0] = _dot_ta(sc, _stack_rows([m[0] for m in mine] + [dmc]))
        gada_o[1] = _dot_ta(sc, _stack_rows([m[1] for m in mine]))
        dsc_mine[...] = _dot_tb(jnp.broadcast_to(dmc, (8, SH_ADA)), ada_r[...])[0:1, :]
        dsc_all[idx] = dsc_mine[...]
        second = [remote(2, k) for k in range(1, NDEV)]
        for cp in second:
            cp.start()
        tot = [functools.reduce(lambda u, v: u + v, [pd_all[s, l] for s in range(NDEV)]) for l in range(3)]
        gadab_o[0] = tot[0] + tot[2]
        gadab_o[1] = tot[1]
        pvs = functools.reduce(lambda u, v: u + v, [pv_all[s] for s in range(NDEV)])
        pvsum_o[...] = pvs
        loss_o[...] = jnp.broadcast_to(jnp.sum(pvs[:, PV_LOSS:PV_LOSS + D], axis=-1, keepdims=True) * (0.5 / D), (1, 128))
        for k in range(1, NDEV):
            remote(2, k).wait_recv()
        dsc = functools.reduce(lambda u, v: u + v, [dsc_all[s] for s in range(NDEV)])
        cx = cctx_r[...]
        sx = _sigmoid(cx)
        gcctx_o[...] = dsc * (sx * (1.0 + cx * (1.0 - sx)))
        for cp in first + second:
            cp.wait_send()

    outs = (_sds((2, D, SH_ADA), F32), _sds((2, NDEV, SH_ADA), F32), _sds((1, D), F32), _sds((1, PV_LEN), F32), _sds((1, 128), F32))
    return pl.pallas_call(
        body, name="reduce_small", out_shape=outs,
        in_specs=[VMEM_SPEC] * 5, out_specs=[VMEM_SPEC] * 5,
        scratch_shapes=[
            pltpu.VMEM((NDEV, 3, NDEV, SH_ADA), F32), pltpu.VMEM((NDEV, 1, PV_LEN), F32), pltpu.VMEM((NDEV, 1, D), F32),
            pltpu.VMEM((1, D), F32),
            pltpu.SemaphoreType.DMA((n_arr, NDEV)), pltpu.SemaphoreType.DMA((n_arr, NDEV)),
        ],
        compiler_params=pltpu.CompilerParams(vmem_limit_bytes=VMEM_LIMIT),
    )(pd, pv, cg, c_ctx, ada_w0)


PV_NW, PV_GNORM, PV_FINAL, PV_LB, PV_PSCALE, PV_LOSS, PV_LEN = 0, 2 * D, 3 * D, 4 * D, 6 * D, 7 * D, 8 * D


def _adamw(w, g, m, v):
    m = ADAM_B1 * m + (1.0 - ADAM_B1) * g
    v = ADAM_B2 * v + (1.0 - ADAM_B2) * (g * g)
    m_hat = m / (1.0 - ADAM_B1 ** ADAM_STEP)
    v_hat = v / (1.0 - ADAM_B2 ** ADAM_STEP)
    delta = -ADAM_LR * (m_hat / (jnp.sqrt(v_hat) + ADAM_EPS) + ADAM_WD * w)
    return delta, m, v


ADAM_STEPS = 8


def _adam_all(sharded, dense, small, lb_idx, lbv):
    ns, nd, nsm = len(sharded), len(dense), len(small)

    def body(*refs):
        it = iter(refs)
        sh_in = [[next(it) for _ in range(4)] for _ in range(ns)]
        de_in = [[next(it) for _ in range(4)] for _ in range(nd)]
        sm_in = [[next(it) for _ in range(4)] for _ in range(nsm)]
        lb_r = next(it)
        sh_out = [[next(it) for _ in range(4)] for _ in range(ns)]
        de_out = [[next(it) for _ in range(3)] for _ in range(nd)]
        sm_out = [[next(it) for _ in range(4)] for _ in range(nsm)]
        for (p, w, m, v), outs in zip(sh_in, sh_out):
            g = p[0].astype(F32)
            for s in range(1, p.shape[0]):
                g = g + p[s].astype(F32)
            d, mn, vn = _adamw(w[...], g, m[...], v[...])
            outs[0][...], outs[1][...], outs[2][...], outs[3][...] = g, d, mn, vn
        for (g, w, m, v), outs in zip(de_in, de_out):
            d, mn, vn = _adamw(w[...], g[...], m[...], v[...])
            outs[0][...], outs[1][...], outs[2][...] = d, mn, vn

        @pl.when(pl.program_id(0) == 0)
        def _():
            for j, ((g, w, m, v), outs) in enumerate(zip(sm_in, sm_out)):
                gj = g[...]
                if j == lb_idx:
                    gj = gj * lb_r[...] * (1.0 - lb_r[...])
                d, mn, vn = _adamw(w[...], gj, m[...], v[...])
                outs[0][...], outs[1][...], outs[2][...], outs[3][...] = gj, d, mn, vn

    def tile(a):
        return pl.BlockSpec((a.shape[0] // ADAM_STEPS, a.shape[1]), lambda i: (i, 0))

    in_specs, out_specs, out_shape, args = [], [], [], []
    for p, w, m, v in sharded:
        in_specs += [pl.BlockSpec((p.shape[0], p.shape[1] // ADAM_STEPS, p.shape[2]), lambda i: (0, i, 0))] + [tile(w)] * 3
        args += [p, w, m, v]
    for g, w, m, v in dense:
        in_specs += [tile(w)] * 4
        args += [g, w, m, v]
    for g, w, m, v in small:
        in_specs += [VMEM_SPEC] * 4
        args += [g, w, m, v]
    in_specs.append(VMEM_SPEC)
    args.append(lbv)
    for _, w, _, _ in sharded:
        out_specs += [tile(w)] * 4
        out_shape += [_sds(w.shape, F32)] * 4
    for _, w, _, _ in dense:
        out_specs += [tile(w)] * 3
        out_shape += [_sds(w.shape, F32)] * 3
    for _, w, _, _ in small:
        out_specs += [VMEM_SPEC] * 4
        out_shape += [_sds(w.shape, F32)] * 4
    res = pl.pallas_call(body, name="adam_all", grid=(ADAM_STEPS,), in_specs=in_specs, out_specs=out_specs, out_shape=tuple(out_shape),
                         compiler_params=pltpu.CompilerParams(dimension_semantics=("arbitrary",), vmem_limit_bytes=VMEM_LIMIT))(*args)
    it = iter(res)
    return ([tuple(next(it) for _ in range(4)) for _ in range(ns)], [tuple(next(it) for _ in range(3)) for _ in range(nd)],
            [tuple(next(it) for _ in range(4)) for _ in range(nsm)])


def kernel(x, c, ctx, c_ctx, ada_w, ada_b, norm_w, hgrn_w_in, hgrn_lb_logits, hgrn_gnorm_w, hgrn_w_out, pool_w_in, pool_w_grp, pool_scale, pool_w_out, final_norm_w, loss_target, m_c_ctx, m_ada_w, m_ada_b, m_norm_w, m_hgrn_w_in, m_hgrn_lb_logits, m_hgrn_gnorm_w, m_hgrn_w_out, m_pool_w_in, m_pool_w_grp, m_pool_scale, m_pool_w_out, m_final_norm_w, v_c_ctx, v_ada_w, v_ada_b, v_norm_w, v_hgrn_w_in, v_hgrn_lb_logits, v_hgrn_gnorm_w, v_hgrn_w_out, v_pool_w_in, v_pool_w_grp, v_pool_scale, v_pool_w_out, v_final_norm_w):
    idx = 4 * lax.axis_index("x") + 2 * lax.axis_index("y") + lax.axis_index("c")
    cctx2 = c_ctx.reshape(1, D)
    cum01, mask01 = _gla_consts()
    pb, pbt, pinv = _pool_consts()

    idx1 = idx.reshape(1).astype(jnp.int32)
    nw0, nw1 = norm_w[0:1], norm_w[1:2]
    fnw = final_norm_w.reshape(1, D)
    g_all, win, s_wout, s_pwin, s_pgrp, s_pwout, lbl_g, ps_g, cg, mod0, mod1, modc = _f1_gather_matmul(
        idx1, ctx[0], x[0], nw0, hgrn_w_in[0], hgrn_w_out[0], pool_w_in[0], pool_w_grp[0], pool_w_out[0], hgrn_lb_logits[0],
        pool_scale, c, cctx2, ada_w, ada_b)
    lb = jax.nn.sigmoid(jnp.transpose(lbl_g, (1, 0, 2)).reshape(2, E))
    pscale = ps_g.reshape(1, E)
    msel = jnp.stack([modc[:2], mod0[:2]])
    p0, p1, v_all, dec, wout, pgrp = _gla_prep(g_all, lb, cum01, s_wout, s_pgrp)
    o, pwin, pwout = _gla_fwd(p0, p1, v_all, dec, mask01, s_pwin, s_pwout)
    x1 = _f3_out(o, g_all, x[0], mod0[2:3], hgrn_gnorm_w, wout)
    dx1, gpwin, gpgrp, gpwout, dmod1, gnw1, gfw, gps, lossv = _pool_layer(
        x1, loss_target[0], mod1, nw1, fnw, pwin, pgrp, pscale, pwout, pb, pbt, pinv)
    do, dz, gwout, dgate0, ggw, rpwout = _b3_out_bwd(dx1, o, g_all, mod0[2:3], hgrn_gnorm_w, wout, gpwout)
    d0, d1, dv, dgl, rpwin, rpgrp = _gla_bwd(p0, p1, v_all, dec, do, mask01, gpwin, gpgrp)
    dg, dlb, rwout = _gla_post_bwd(g_all, d0, d1, dgl, dv, dz, lb, cum01, gwout)
    grad_x, rwin, dmx, dmc, gnw0 = _b1_in_bwd(idx1, ctx[0], x[0], dx1, dg, nw0, msel, win)

    dmod0 = jnp.concatenate([dmx, dgate0], axis=0)
    dmodc = jnp.concatenate([dmc, jnp.zeros((1, D), F32)], axis=0)
    pd = jnp.stack([dmod0, dmod1, dmodc]).reshape(3, NDEV, SH_ADA)
    pv = jnp.concatenate([gnw0, gnw1, ggw, gfw, dlb.reshape(1, 2 * E), gps, lossv], axis=1)
    g_ada, g_adab, g_cctx, pvsum, loss128 = _reduce_small(pd, pv, cg, cctx2, ada_w[0])

    g2 = (4 * SH_GRP, PG)
    sharded_names = ["hgrn_w_in", "hgrn_w_out", "pool_w_in", "pool_w_grp", "pool_w_out"]
    sharded = [(rwin, hgrn_w_in[0], m_hgrn_w_in[0], v_hgrn_w_in[0]),
               (rwout, hgrn_w_out[0], m_hgrn_w_out[0], v_hgrn_w_out[0]),
               (rpwin, pool_w_in[0], m_pool_w_in[0], v_pool_w_in[0]),
               (rpgrp.reshape((NDEV,) + g2), pool_w_grp[0].reshape(g2), m_pool_w_grp[0].reshape(g2), v_pool_w_grp[0].reshape(g2)),
               (rpwout, pool_w_out[0], m_pool_w_out[0], v_pool_w_out[0])]
    a2 = (2 * D, SH_ADA)
    g_ada2 = g_ada.reshape(a2)
    dense = [(g_ada2, ada_w.reshape(a2), m_ada_w.reshape(a2), v_ada_w.reshape(a2))]
    lb_me = lax.dynamic_slice_in_dim(lb, idx * DH, DH, axis=1)
    small_names = ["c_ctx", "ada_b", "norm_w", "hgrn_lb_logits", "hgrn_gnorm_w", "pool_scale", "final_norm_w"]
    small = [(g_cctx, cctx2, m_c_ctx.reshape(1, D), v_c_ctx.reshape(1, D)),
             (g_adab.reshape(2, 3 * D), ada_b, m_ada_b, v_ada_b),
             (pvsum[:, PV_NW:PV_NW + 2 * D].reshape(2, D), norm_w, m_norm_w, v_norm_w),
             (lax.dynamic_slice_in_dim(pvsum[:, PV_LB:PV_LB + 2 * E].reshape(2, E), idx * DH, DH, axis=1),
              hgrn_lb_logits[0], m_hgrn_lb_logits[0], v_hgrn_lb_logits[0]),
             (pvsum[:, PV_GNORM:PV_GNORM + E], hgrn_gnorm_w, m_hgrn_gnorm_w, v_hgrn_gnorm_w),
             (lax.dynamic_slice_in_dim(pvsum[:, PV_PSCALE:PV_PSCALE + E], idx * DH, DH, axis=1), pool_scale, m_pool_scale, v_pool_scale),
             (pvsum[:, PV_FINAL:PV_FINAL + D], fnw, m_final_norm_w.reshape(1, D), v_final_norm_w.reshape(1, D))]
    r_sharded, r_dense, r_small = _adam_all(sharded, dense, small, 3, lb_me)
    out = dict(zip(sharded_names, r_sharded))
    out["ada_w"] = (g_ada2,) + r_dense[0]
    out.update(zip(small_names, r_small))

    shapes = {"c_ctx": (D,), "ada_w": (2, D, SH_ADA), "ada_b": (2, 3 * D), "norm_w": (2, D), "hgrn_w_in": (1, D, SH_WIN),
              "hgrn_lb_logits": (1, 2, DH), "hgrn_gnorm_w": (1, E), "hgrn_w_out": (1, SH_ROWS, D), "pool_w_in": (1, D, SH_PWIN),
              "pool_w_grp": (1, 4, SH_GRP, PG), "pool_scale": (1, DH), "pool_w_out": (1, SH_ROWS, D), "final_norm_w": (D,)}
    order = ["c_ctx", "ada_w", "ada_b", "norm_w", "hgrn_w_in", "hgrn_lb_logits", "hgrn_gnorm_w", "hgrn_w_out", "pool_w_in",
             "pool_w_grp", "pool_scale", "pool_w_out", "final_norm_w"]
    flat = [out[name][q].reshape(shapes[name]) for q in range(4) for name in order]
    return (loss128[0, 0], grad_x[None], *flat)
```

```python
import functools

import numpy as np
import jax
import jax.numpy as jnp
from jax import lax
from jax.experimental import pallas as pl
from jax.experimental.pallas import tpu as pltpu

F32 = jnp.float32
BF16 = jnp.bfloat16

D = 1024
E = 1024
HEADS = 8
DH = 128
CHUNK = 64
T = 2048
TC = 256
TT = T + TC
TM = 256
NT = TT // TM
NTX = T // TM
NDEV = 8
GRID_W = 64
POOL_WINDOWS = (2, 4, 8, 16)
PG = 256
EPS = 1e-6
WIN_COLS = 5 * E
SH_WIN = WIN_COLS // NDEV
SH_PWIN = 2 * E // NDEV
SH_ROWS = E // NDEV
SH_GRP = PG // NDEV
SH_ADA = 3 * D // NDEV
VMEM_LIMIT = 56 * 1024 * 1024

ADAM_LR, ADAM_B1, ADAM_B2, ADAM_EPS, ADAM_WD, ADAM_STEP = 0.001, 0.9, 0.999, 1e-08, 0.01, 10

MESH = pl.DeviceIdType.MESH
VMEM_SPEC = pl.BlockSpec(memory_space=pltpu.VMEM)
HBM_SPEC = pl.BlockSpec(memory_space=pltpu.HBM)


def _sds(shape, dtype):
    return jax.ShapeDtypeStruct(shape, dtype)


def _bf(a):
    return a if a.dtype == BF16 else a.astype(BF16)


def _dot(a, b):
    return lax.dot_general(_bf(a), _bf(b), (((1,), (0,)), ((), ())), preferred_element_type=F32)


def _dot_tb(a, b):
    return lax.dot_general(_bf(a), _bf(b), (((1,), (1,)), ((), ())), preferred_element_type=F32)


def _dot_ta(a, b):
    return lax.dot_general(_bf(a), _bf(b), (((0,), (0,)), ((), ())), preferred_element_type=F32)


def _bdot(a, b):
    return lax.dot_general(_bf(a), _bf(b), (((2,), (1,)), ((0,), (0,))), preferred_element_type=F32)


def _bdot_nt(a, b):
    return lax.dot_general(_bf(a), _bf(b), (((2,), (2,)), ((0,), (0,))), preferred_element_type=F32)


def _bdot_tn(a, b):
    return lax.dot_general(_bf(a), _bf(b), (((1,), (1,)), ((0,), (0,))), preferred_element_type=F32)


def _dot01(m01, x):
    hi = x.astype(BF16)
    lo = (x - hi.astype(F32)).astype(BF16)
    return _dot(m01, hi) + _dot(m01, lo)


def _rstd(x):
    return lax.rsqrt(jnp.mean(x * x, axis=-1, keepdims=True) + EPS)


def _sigmoid(x):
    return jax.nn.sigmoid(x)


def _colsum(a):
    return jnp.sum(a, axis=0, keepdims=True)


def _stack_rows(rows):
    n = rows[0].shape[-1]
    rid = lax.broadcasted_iota(jnp.int32, (16, n), 0)
    out = jnp.zeros((16, n), F32)
    for i, r in enumerate(rows):
        out = jnp.where(rid == i, r, out)
    return out


def _head_map(fn, *arrs):
    outs = [fn(*[a[:, h * DH:(h + 1) * DH] for a in arrs]) for h in range(HEADS)]
    return jnp.concatenate(outs, axis=1)


def _gla_consts():
    r = np.arange(TM)[:, None]
    c = np.arange(TM)[None, :]
    same = (r // CHUNK) == (c // CHUNK)
    tril = same & (c <= r)
    triu = same & (c >= r)
    m = np.stack([tril, triu]).astype(np.float32)
    return jnp.asarray(m, BF16), jnp.asarray(m, F32)


def _pool_consts():
    r = np.arange(TM)[:, None]
    c = np.arange(TM)[None, :]
    same = (r // GRID_W) == (c // GRID_W)
    rp, cp = r % GRID_W, c % GRID_W
    bs, inv = [], []
    for w in POOL_WINDOWS:
        lo = np.clip(rp - w // 2, 0, GRID_W)
        hi = np.clip(rp - w // 2 + w, 0, GRID_W)
        bs.append(same & (cp >= lo) & (cp < hi))
        inv.append(1.0 / (hi - lo).astype(np.float32))
    b = np.stack(bs).astype(np.float32)
    bt = np.transpose(b, (0, 2, 1))
    return jnp.asarray(b, BF16), jnp.asarray(bt, BF16), jnp.asarray(np.stack(inv), F32)


def _mesh_pos():
    x, y, c = lax.axis_index("x"), lax.axis_index("y"), lax.axis_index("c")
    return x, y, c, 4 * x + 2 * y + c


def _peer(x, y, c, k):
    return (x ^ ((k >> 2) & 1), y ^ ((k >> 1) & 1), c ^ (k & 1))


def _small_gathers(refs, ssem, rsem):
    lb_r, ps_r, c_r, cctx_r, ada_r, adab_r, lb_o, ps_o, cg_o, mod_o, lb_out, ps_out, cg_out, mod0_o, mod1_o, modc_o = refs
    x, y, cc, idx = _mesh_pos()
    srcs = [lb_r, ps_r, c_r, mod_o.at[idx]]
    mine = [lb_o.at[idx], ps_o.at[idx], cg_o.at[idx], mod_o.at[idx]]

    def remote(a, k):
        return pltpu.make_async_remote_copy(src_ref=srcs[a], dst_ref=mine[a], send_sem=ssem.at[a, k], recv_sem=rsem.at[a, k],
                                            device_id=_peer(x, y, cc, k), device_id_type=MESH)

    first = [remote(a, k) for k in range(1, NDEV) for a in (2, 0, 1)]
    for cp in first:
        cp.start()
    lb_o[idx] = lb_r[...]
    ps_o[idx] = ps_r[...]
    cg_o[idx] = c_r[...]
    for k in range(1, NDEV):
        remote(2, k).wait_recv()
    rows = _stack_rows([cg_o[i] for i in range(NDEV)] + [cctx_r[...]])
    sc = rows * _sigmoid(rows)
    for l in range(2):
        mod_o[idx, l] = _dot(sc, ada_r[l])
    second = [remote(3, k) for k in range(1, NDEV)]
    for cp in second:
        cp.start()
    for k in range(1, NDEV):
        remote(3, k).wait_recv()

    def mod_rows(l, row):
        full = jnp.concatenate([mod_o[s, l, row, :] for s in range(NDEV)], axis=1) + adab_r[l:l + 1, :]
        return [full[:, j * D:(j + 1) * D] for j in range(3)]

    me = pl.ds(idx, 1)
    for out, parts in ((mod0_o, mod_rows(0, me)), (mod1_o, mod_rows(1, me)), (modc_o, mod_rows(0, slice(NDEV, NDEV + 1)))):
        for j in range(3):
            out[j:j + 1, :] = parts[j]
    for cp in first + second:
        cp.wait_send()
    for k in range(1, NDEV):
        for a in (0, 1):
            remote(a, k).wait_recv()
    lb_out[...] = lb_o[...]
    ps_out[...] = ps_o[...]
    cg_out[...] = cg_o[...]


def _gather_order(s, core):
    k = jnp.where(s == 2, 4, jnp.where(s == 4, 2, s))
    return k ^ jnp.where((core == 1) & (s >= 2) & (s <= 5), 6, 0)


GATHER_ISSUE = (1, 2, 4, 3, 5, 6, 7)
GATHER_ICI = (2, 4, 6)
GATHER_DIRECT = (1,) + GATHER_ICI
GLA_HB = 2
RS_SLOTS = 5


def _shard_of(kind, ref, i):
    if kind == "rows":
        return ref.at[pl.ds(pl.multiple_of(i * SH_ROWS, SH_ROWS), SH_ROWS), :]
    if kind == "major":
        return ref.at[i]
    assert kind == "grp"
    return ref.at[:, pl.ds(pl.multiple_of(i * SH_GRP, SH_GRP), SH_GRP), :]


def _gather_rider(step, n_steps, forward_at, kinds, srcs, outs, ssem, rsem, lsem):
    x, y, cc, idx = _mesh_pos()
    arrays = range(len(kinds))
    mine = [_shard_of(kinds[a], outs[a], idx) for a in arrays]

    def remote(a, k):
        return pltpu.make_async_remote_copy(src_ref=srcs[a], dst_ref=mine[a], send_sem=ssem.at[a, k], recv_sem=rsem.at[a, k],
                                            device_id=_peer(x, y, cc, k), device_id_type=MESH)

    def forward(a, k):
        blk = _shard_of(kinds[a], outs[a], idx ^ k)
        return pltpu.make_async_remote_copy(src_ref=blk, dst_ref=blk, send_sem=ssem.at[a, k ^ 1], recv_sem=rsem.at[a, k ^ 1],
                                            device_id=(x, y, 1 - cc), device_id_type=MESH)

    copies = [remote(a, k) for k in GATHER_DIRECT for a in arrays]
    passed = [forward(a, k) for k in GATHER_ICI for a in arrays]
    local = [pltpu.make_async_copy(srcs[a], mine[a], lsem.at[a]) for a in arrays]

    @pl.when(step == 0)
    def _():
        for cp in copies + local:
            cp.start()

    @pl.when(step == forward_at)
    def _():
        for k in GATHER_ICI:
            for a in arrays:
                remote(a, k).wait_recv()
                forward(a, k).start()

    def finish():
        @pl.when(step == n_steps - 1)
        def _():
            for cp in copies + passed:
                cp.wait_send()
            for a in arrays:
                remote(a, 1).wait_recv()
            for cp in passed:
                cp.wait_recv()
            for cp in local:
                cp.wait()

    return finish


def _scatter_rider(step, n_steps, kinds, grads, slots, ssem, rsem, lsem):
    x, y, cc, idx = _mesh_pos()
    arrays = range(len(kinds))
    dsts = [slots[a].at[idx] for a in arrays]

    def remote(a, k):
        px, py, pc = _peer(x, y, cc, k)
        return pltpu.make_async_remote_copy(src_ref=_shard_of(kinds[a], grads[a], 4 * px + 2 * py + pc), dst_ref=dsts[a],
                                            send_sem=ssem.at[a, k], recv_sem=rsem.at[a, k], device_id=(px, py, pc), device_id_type=MESH)

    copies = [remote(a, k) for k in GATHER_ISSUE for a in arrays]
    local = [pltpu.make_async_copy(_shard_of(kinds[a], grads[a], idx), dsts[a], lsem.at[a]) for a in arrays]

    @pl.when(step == 0)
    def _():
        for cp in copies + local:
            cp.start()

    def finish():
        @pl.when(step == n_steps - 1)
        def _():
            for cp in copies:
                cp.wait_send()
            for cp in copies:
                cp.wait_recv()
            for cp in local:
                cp.wait()

    return finish


def _rider_sems(n):
    return [pltpu.SemaphoreType.DMA((n, NDEV)), pltpu.SemaphoreType.DMA((n, NDEV)), pltpu.SemaphoreType.DMA((n,))]


def _scatter_rider2(step, n_steps, add_at, kinds, grads, slots, bufs, sems):
    x, y, cc, idx = _mesh_pos()
    sibling = (x, y, 1 - cc)
    arrays = range(len(kinds))
    psend, precv, isend, irecv, lown, sibsem, lself = sems

    def mine(a, i):
        return _shard_of(kinds[a], grads[a], i)

    def partial(a, p):
        return pltpu.make_async_remote_copy(src_ref=mine(a, idx ^ (2 * (p + 1)) ^ 1), dst_ref=bufs[a][1].at[p], send_sem=psend.at[a, p],
                                            recv_sem=precv.at[a, p], device_id=sibling, device_id_type=MESH)

    def load(a, p):
        return pltpu.make_async_copy(mine(a, idx ^ (2 * (p + 1))), bufs[a][0].at[p], lown.at[a, p])

    def chip_sum(a, p):
        return pltpu.make_async_remote_copy(src_ref=bufs[a][0].at[p], dst_ref=slots[a].at[2 + p], send_sem=isend.at[a, p],
                                            recv_sem=irecv.at[a, p], device_id=_peer(x, y, cc, 2 * (p + 1)), device_id_type=MESH)

    def to_sibling(a):
        return pltpu.make_async_remote_copy(src_ref=mine(a, idx ^ 1), dst_ref=slots[a].at[1], send_sem=sibsem.at[a, 0],
                                            recv_sem=sibsem.at[a, 1], device_id=sibling, device_id_type=MESH)

    def own(a):
        return pltpu.make_async_copy(mine(a, idx), slots[a].at[0], lself.at[a, 0])

    @pl.when(step == 0)
    def _():
        for a in arrays:
            for p in range(3):
                partial(a, p).start()
                load(a, p).start()
            to_sibling(a).start()
            own(a).start()

    @pl.when(step == add_at)
    def _():
        for a in arrays:
            for p in range(3):
                partial(a, p).wait_recv()
                load(a, p).wait()
                bufs[a][0][p] = (bufs[a][0][p].astype(F32) + bufs[a][1][p].astype(F32)).astype(BF16)
                chip_sum(a, p).start()

    def finish():
        @pl.when(step == n_steps - 1)
        def _():
            for a in arrays:
                for p in range(3):
                    partial(a, p).wait_send()
                    chip_sum(a, p).wait_send()
                    chip_sum(a, p).wait_recv()
                to_sibling(a).wait_send()
                to_sibling(a).wait_recv()
                own(a).wait()

    return finish


def _rider2_scratch(blocks):
    n = len(blocks)
    bufs = [pltpu.VMEM((3,) + tuple(b), BF16) for b in blocks for _ in range(2)]
    return bufs + [pltpu.SemaphoreType.DMA((n, 3)) for _ in range(5)] + [pltpu.SemaphoreType.DMA((n, 2)), pltpu.SemaphoreType.DMA((n, 1))]


def _rider2_split(refs, n):
    refs = list(refs)
    return [tuple(refs[2 * a:2 * a + 2]) for a in range(n)], tuple(refs[2 * n:2 * n + 7])


def _modulated(x, nw, shift, scale):
    r = _rstd(x)
    xn = x * r
    a = xn * nw
    return a * (1.0 + scale) + shift, r, xn, a


def _ctx_or_x(i, ctx_ref, x_ref):
    return jnp.where(i == 0, ctx_ref[...], x_ref[...])


def _f1_gather_matmul(idx1, ctx, x, nw, w_in, w_out, pw_in, pgrp, pw_out, lb_l, pscale, c, c_ctx, ada_w, ada_b):
    def body(idx_ref, ctx_ref, x_ref, nw_ref, win_r, wout_r, pwin_r, pgrp_r, pwout_r, lb_r, ps_r, c_r, cctx_r, ada_r, adab_r,
             g_ref, win_o, s_wout, s_pwin, s_pgrp, s_pwout, lb_o, ps_o, cg_o, mod0_o, mod1_o, modc_o,
             wslot, hx_sc, lb_g, ps_g, cg_g, mod_g, ssem, rsem, osem, dsem, sm_ssem, sm_rsem):
        del idx_ref
        s, i = pl.program_id(0), pl.program_id(1)
        x, y, cc, idx = _mesh_pos()
        k = _gather_order(s, cc)
        j = idx ^ k
        first = 4 - 2 * cc

        def remote(kk):
            return pltpu.make_async_remote_copy(src_ref=wslot.at[idx], dst_ref=wslot.at[idx], send_sem=ssem.at[kk], recv_sem=rsem.at[kk],
                                                device_id=_peer(x, y, cc, kk), device_id_type=MESH)

        def forward(kk):
            jj = idx ^ kk
            return pltpu.make_async_remote_copy(src_ref=wslot.at[jj], dst_ref=wslot.at[jj], send_sem=ssem.at[kk ^ 1],
                                                recv_sem=rsem.at[kk ^ 1], device_id=(x, y, 1 - cc), device_id_type=MESH)

        def relay(h):
            blk = wslot.at[idx ^ (4 >> h), pl.ds(h * (D // 2), D // 2), :]
            return pltpu.make_async_remote_copy(src_ref=blk, dst_ref=blk, send_sem=dsem.at[0, h], recv_sem=dsem.at[1, h],
                                                device_id=_peer(x, y, cc, 2 << h), device_id_type=MESH)

        def to_hbm(jj, kk):
            return pltpu.make_async_copy(wslot.at[jj], win_o.at[jj], osem.at[kk])

        @pl.when((s == 0) & (i == 0))
        def _():
            _small_gathers((lb_r, ps_r, c_r, cctx_r, ada_r, adab_r, lb_g, ps_g, cg_g, mod_g, lb_o, ps_o, cg_o, mod0_o, mod1_o, modc_o),
                           sm_ssem, sm_rsem)
            wslot[idx] = win_r[...].astype(BF16)
            remote(1).start()
            remote(first).start()
            s_wout[...] = wout_r[...].astype(BF16)
            s_pwin[...] = pwin_r[...].astype(BF16)
            s_pgrp[...] = pgrp_r[...].astype(BF16)
            s_pwout[...] = pwout_r[...].astype(BF16)

        @pl.when(s == 0)
        def _():
            shift = jnp.where(i == 0, modc_o[0:1, :], mod0_o[0:1, :])
            scale = jnp.where(i == 0, modc_o[1:2, :], mod0_o[1:2, :])
            hx, _, _, _ = _modulated(_ctx_or_x(i, ctx_ref, x_ref), nw_ref[...], shift, scale)
            hx_sc[i] = hx.astype(BF16)

        @pl.when((s == 2) & (i == 0))
        def _():
            remote(6 - first).start()

        @pl.when((s > 0) & (i == 0) & (k != 6))
        def _():
            remote(k).wait_recv()

            @pl.when((k & 1) == 0)
            def _():
                forward(k).start()

            for h in range(2):
                @pl.when(k == 4 >> h)
                def _():
                    relay(h).start()

        @pl.when((i == 0) & (k == 6))
        def _():
            for h in range(2):
                relay(h).wait_recv()
            forward(6).start()

        @pl.when(i == 0)
        def _():
            to_hbm(j, k).start()

        g_ref[...] = jnp.dot(hx_sc[i], wslot[j], preferred_element_type=F32)

        @pl.when((s == NDEV - 1) & (i == NT - 1))
        def _():
            for kk in (1, 2, 4):
                remote(kk).wait_send()
            for kk in GATHER_ICI:
                forward(kk).wait_send()
            for h in range(2):
                relay(h).wait_send()
            for kk in range(NDEV):
                to_hbm(idx ^ kk, kk).wait()

    grid_spec = pltpu.PrefetchScalarGridSpec(
        num_scalar_prefetch=1, grid=(NDEV, NT),
        in_specs=[VMEM_SPEC, pl.BlockSpec((TM, D), lambda s, i, ix: (jnp.where(s == 0, jnp.maximum(i - 1, 0), NTX - 1), 0))]
        + [VMEM_SPEC] * 12,
        out_specs=[pl.BlockSpec((TM, SH_WIN), lambda s, i, ix: (i, ix[0] ^ _gather_order(s, ix[0] & 1))), HBM_SPEC] + [VMEM_SPEC] * 10,
        scratch_shapes=[pltpu.VMEM((NDEV, D, SH_WIN), BF16), pltpu.VMEM((NT, TM, D), BF16),
                        pltpu.VMEM((NDEV, 2, DH), F32), pltpu.VMEM((NDEV, 1, DH), F32), pltpu.VMEM((NDEV, 1, D), F32),
                        pltpu.VMEM((NDEV, 2, 16, SH_ADA), F32),
                        pltpu.SemaphoreType.DMA((NDEV,)), pltpu.SemaphoreType.DMA((NDEV,)), pltpu.SemaphoreType.DMA((NDEV,)),
                        pltpu.SemaphoreType.DMA((2, 2)),
                        pltpu.SemaphoreType.DMA((4, NDEV)), pltpu.SemaphoreType.DMA((4, NDEV))])
    outs = (_sds((TT, WIN_COLS), F32), _sds((NDEV, D, SH_WIN), BF16),
            _sds((SH_ROWS, D), BF16), _sds((D, SH_PWIN), BF16), _sds((4, SH_GRP, PG), BF16), _sds((SH_ROWS, D), BF16),
            _sds((NDEV, 2, DH), F32), _sds((NDEV, 1, DH), F32), _sds((NDEV, 1, D), F32),
            _sds((3, D), F32), _sds((3, D), F32), _sds((3, D), F32))
    return pl.pallas_call(
        body, name="f1_gather_matmul", grid_spec=grid_spec, out_shape=outs,
        compiler_params=pltpu.CompilerParams(dimension_semantics=("arbitrary", "arbitrary"), vmem_limit_bytes=VMEM_LIMIT),
    )(idx1, ctx, x, nw, w_in, w_out, pw_in, pgrp, pw_out, lb_l, pscale, c, c_ctx, ada_w, ada_b)


def _gla_gates(pre, qpre, lbd, cum, rev):
    rows, n = pre.shape
    nch = rows // CHUNK
    sig = _sigmoid(pre)
    f = lbd + (1.0 - lbd) * sig
    k = 1.0 - f
    g = _dot01(cum, jnp.log(f))
    g3 = g.reshape(nch, CHUNK, n)
    last = 0 if rev else CHUNK - 1
    mid = CHUNK // 2 if rev else CHUNK // 2 - 1
    gl1, gm1 = g3[:, last:last + 1, :], g3[:, mid:mid + 1, :]

    def bc(a):
        return jnp.broadcast_to(a, g3.shape).reshape(rows, n)

    gm = bc(gm1)
    e_q, e_k = jnp.exp(g - gm), jnp.exp(gm - g)
    qsig = _sigmoid(qpre)
    qs = qpre * qsig * (DH ** -0.5)
    return dict(sig=sig, f=f, k=k, qsig=qsig, qs=qs, e_q=e_q, e_k=e_k,
                e_mid=[jnp.exp(gm1[ci]) for ci in range(nch)], e_rest=[jnp.exp(gl1[ci] - gm1[ci]) for ci in range(nch)])


def _put_heads(ref, lead, arr):
    for h in range(HEADS):
        ref[lead + (h,)] = arr[:, h * DH:(h + 1) * DH]


def _get_heads(ref, lead=()):
    return jnp.concatenate([ref[lead + (h,)] for h in range(HEADS)], axis=1)


def _gla_prep(g_all, lb, cum01, s_wout, s_pgrp):
    nch = TM // CHUNK

    def body(g_ref, lb_ref, cum_ref, swout_r, spgrp_r, p0_ref, p1_ref, v_ref, dec_ref, wout_o, pgrp_o, ssem, rsem, lsem):
        finish = _gather_rider(pl.program_id(0), NT, NT - 1, ("rows", "grp"), (swout_r, spgrp_r), (wout_o, pgrp_o), ssem, rsem, lsem)
        qpre = g_ref[:, 3 * E:4 * E]
        _put_heads(v_ref, (), g_ref[:, 2 * E:3 * E].astype(BF16))
        for d, p_ref in ((0, p0_ref), (1, p1_ref)):
            t = _gla_gates(g_ref[:, d * E:(d + 1) * E], qpre, lb_ref[d:d + 1, :], cum_ref[d], d == 1)
            _put_heads(p_ref, (0,), (t["qs"] * t["e_q"]).astype(BF16))
            _put_heads(p_ref, (1,), (t["k"] * t["e_k"]).astype(BF16))
            for ci in range(nch):
                dec_ref[d, 0, ci:ci + 1, :] = t["e_mid"][ci]
                dec_ref[d, 0, nch + ci:nch + ci + 1, :] = t["e_rest"][ci]
        finish()

    quad = pl.BlockSpec((2, HEADS, TM, DH), lambda i: (0, 0, i, 0))
    return pl.pallas_call(
        body, name="gla_prep", grid=(NT,),
        in_specs=[pl.BlockSpec((TM, 4 * E), lambda i: (i, 0)), VMEM_SPEC, VMEM_SPEC, HBM_SPEC, HBM_SPEC],
        out_specs=[quad, quad, pl.BlockSpec((HEADS, TM, DH), lambda i: (0, i, 0)), pl.BlockSpec((2, 1, 2 * nch, E), lambda i: (0, i, 0, 0)),
                   HBM_SPEC, HBM_SPEC],
        out_shape=(_sds((2, HEADS, TT, DH), BF16), _sds((2, HEADS, TT, DH), BF16), _sds((HEADS, TT, DH), BF16), _sds((2, NT, 2 * nch, E), F32),
                   _sds((E, D), BF16), _sds((4, PG, PG), BF16)),
        scratch_shapes=_rider_sems(2),
        compiler_params=pltpu.CompilerParams(dimension_semantics=("arbitrary",), vmem_limit_bytes=VMEM_LIMIT),
    )(g_all, lb, cum01, s_wout, s_pgrp)


def _scan_tile(i, rev):
    t = jnp.where(i == 0, 0, NT - i) if rev else i
    return t, pl.ds(pl.multiple_of(t * TM, TM), TM)


def _chunk_rows(dec_ref, lanes, cis, where):
    nch = TM // CHUNK

    def rows(off):
        return jnp.stack([dec_ref[d, where[d][0], off + ci:off + ci + 1, hh * DH:(hh + 1) * DH] for (d, hh), ci in zip(lanes, cis)])

    return rows(0), rows(nch)


def _gla_fwd(p0, p1, v_all, dec, mask01, s_pwin, s_pwout):
    n_steps = HEADS // GLA_HB

    def body(p0_ref, p1_ref, v_ref, dec_ref, msk_ref, spwin_r, spwout_r, o_ref, pwin_o, pwout_o, ob_sc, ssem, rsem, lsem):
        finish = _gather_rider(pl.program_id(0), n_steps, n_steps - 1, ("major", "rows"), (spwin_r, spwout_r), (pwin_o, pwout_o),
                               ssem, rsem, lsem)

        lanes = [(d, hh) for d in (0, 1) for hh in range(GLA_HB)]
        nch = TM // CHUNK

        def tile_body(i, st):
            where = [_scan_tile(i, d == 1) for d in (0, 1)]

            def stacked(fn):
                return jnp.stack([fn(d, hh, where[d][1]) for d, hh in lanes])

            qg, kg = [stacked(lambda d, hh, rows, ty=ty: (p1_ref if d else p0_ref)[ty, hh, rows, :]) for ty in range(2)]
            v = stacked(lambda d, hh, rows: v_ref[hh, rows, :])
            a = _bdot_nt(qg, kg) * jnp.stack([msk_ref[d] for d, _ in lanes])
            intra = _bdot(a, v)
            outs = [[None] * nch for _ in lanes]
            for n in range(nch):
                cis = [nch - 1 - n if d else n for d, _ in lanes]

                def chunk(arr):
                    return jnp.stack([arr[l, ci * CHUNK:(ci + 1) * CHUNK] for l, ci in enumerate(cis)])

                e_mid, e_rest = _chunk_rows(dec_ref, lanes, cis, where)
                inter = _bdot_nt(chunk(qg), st * e_mid)
                for l, ci in enumerate(cis):
                    outs[l][ci] = inter[l] + intra[l, ci * CHUNK:(ci + 1) * CHUNK]
                st = st * (e_mid * e_rest) + _bdot_tn(chunk(v), chunk(kg)) * e_rest
            for l, (d, hh) in enumerate(lanes):
                (ob_sc if d else o_ref)[hh, where[d][1], :] = jnp.concatenate(outs[l], axis=0)
            return st

        lax.fori_loop(0, NT, tile_body, jnp.zeros((len(lanes), DH, DH), F32))
        o_ref[...] += ob_sc[...]
        finish()

    quad = pl.BlockSpec((2, GLA_HB, TT, DH), lambda h: (0, h, 0, 0))
    head = pl.BlockSpec((GLA_HB, TT, DH), lambda h: (h, 0, 0))
    return pl.pallas_call(
        body, name="gla_fwd", grid=(n_steps,),
        in_specs=[quad, quad, head, pl.BlockSpec((2, NT, 8, GLA_HB * DH), lambda h: (0, 0, 0, h)),
                  pl.BlockSpec((2, TM, TM), lambda h: (0, 0, 0)), HBM_SPEC, HBM_SPEC],
        out_specs=[head, HBM_SPEC, HBM_SPEC],
        out_shape=(_sds((HEADS, TT, DH), F32), _sds((NDEV, D, SH_PWIN), BF16), _sds((E, D), BF16)),
        scratch_shapes=[pltpu.VMEM((GLA_HB, TT, DH), F32)] + _rider_sems(2),
        compiler_params=pltpu.CompilerParams(dimension_semantics=("arbitrary",), vmem_limit_bytes=VMEM_LIMIT),
    )(p0, p1, v_all, dec, mask01, s_pwin, s_pwout)


def _gated_norm(o, z, gw):
    r = _head_map(lambda oh: jnp.broadcast_to(_rstd(oh), oh.shape), o)
    on = o * r
    zs = _sigmoid(z)
    sz = z * zs
    return on * gw * sz, r, on, zs, sz


def _f3_out(o, g_all, x, gate, gw, wout):
    def body(o_ref, z_ref, x_ref, gate_ref, gw_ref, w_ref, x1_ref):
        og, _, _, _, _ = _gated_norm(_get_heads(o_ref), z_ref[...], gw_ref[...])
        x1_ref[...] = x_ref[...] + gate_ref[...] * _dot(og, w_ref[...])

    return pl.pallas_call(
        body, name="f3_out", grid=(NTX,),
        in_specs=[pl.BlockSpec((HEADS, TM, DH), lambda i: (0, i + 1, 0)), pl.BlockSpec((TM, E), lambda i: (i + 1, 4)),
                  pl.BlockSpec((TM, D), lambda i: (i, 0)), pl.BlockSpec((1, D), lambda i: (0, 0)),
                  pl.BlockSpec((1, E), lambda i: (0, 0)), pl.BlockSpec((E, D), lambda i: (0, 0))],
        out_specs=pl.BlockSpec((TM, D), lambda i: (i, 0)),
        out_shape=_sds((T, D), F32),
        compiler_params=pltpu.CompilerParams(dimension_semantics=("arbitrary",)),
    )(o, g_all, x, gate, gw, wout)


def _pool_layer(x1, tgt, mod1, nw1, fnw, pwin, pgrp, pscale, pwout, pb, pbt, pinv):
    def body(x_ref, t_ref, m_ref, nw_ref, fw_ref, pwin_ref, pgrp_ref, ps_ref, pwout_ref, pb_ref, pbt_ref, pinv_ref,
             dx_ref, gpwin_o, gpgrp_o, gpwout_o, dmod_o, gnw_o, gfw_o, gps_o, loss_o,
             a_pwin, a_pgrp, a_pwout):
        i = pl.program_id(0)

        @pl.when(i == 0)
        def _():
            for ref in (a_pwin, a_pgrp, a_pwout, dmod_o, gnw_o, gfw_o, gps_o, loss_o):
                ref[...] = jnp.zeros_like(ref)

        shift, scale, gate = m_ref[0:1, :], m_ref[1:2, :], m_ref[2:3, :]
        nw, fw, ps = nw_ref[...], fw_ref[...], ps_ref[...]
        x1 = x_ref[...]
        hx, r1, xn, a = _modulated(x1, nw, shift, scale)
        hxb = hx.astype(BF16)
        uz = jnp.concatenate([_dot(hxb, pwin_ref[j]) for j in range(NDEV)], axis=1)
        u, z = uz[:, :E], uz[:, E:]
        pooled, ys = [], []
        for g in range(4):
            ug = u[:, g * PG:(g + 1) * PG]
            pg = _dot01(pb_ref[g], ug) * pinv_ref[g] - ug
            pooled.append(pg.astype(BF16))
            ys.append(_dot(pooled[g], pgrp_ref[g]))
        ycat = jnp.concatenate(ys, axis=1)
        y = ycat * ps
        zs = _sigmoid(z)
        sz = z * zs
        p = (y * sz).astype(BF16)
        out = _dot(p, pwout_ref[...])
        x2 = x1 + gate * out
        r2 = _rstd(x2)
        xn2 = x2 * r2
        diff = xn2 * fw - t_ref[...]
        loss_o[...] += _colsum(diff * diff)
        dyf = diff * (1.0 / D)
        gfw_o[...] += _colsum(dyf * xn2)
        dxn2 = dyf * fw
        dx2 = r2 * (dxn2 - xn2 * jnp.mean(dxn2 * xn2, axis=-1, keepdims=True))
        dgate = _colsum(dx2 * out)
        dout = (dx2 * gate).astype(BF16)
        for j in range(4):
            cs = slice(j * PG, (j + 1) * PG)
            a_pwout[:, cs] += _dot_ta(p, dout[:, cs])
        dp = _dot_tb(dout, pwout_ref[...])
        dy = dp * sz
        dz = dp * y * (zs * (1.0 + z * (1.0 - zs)))
        gps_o[...] += _colsum(dy * ycat)
        dycat = dy * ps
        dus = []
        for g in range(4):
            dyg = dycat[:, g * PG:(g + 1) * PG].astype(BF16)
            a_pgrp[g] += _dot_ta(pooled[g], dyg)
            dpg = _dot_tb(dyg, pgrp_ref[g])
            dus.append(_dot01(pbt_ref[g], dpg * pinv_ref[g]) - dpg)
        duz = jnp.concatenate(dus + [dz], axis=1).astype(BF16)
        dhx = None
        for j in range(NDEV):
            dj = duz[:, j * SH_PWIN:(j + 1) * SH_PWIN]
            a_pwin[j] += _dot_ta(hxb, dj)
            part = _dot_tb(dj, pwin_ref[j])
            dhx = part if dhx is None else dhx + part
        dmod_o[0:1, :] += _colsum(dhx)
        dmod_o[1:2, :] += _colsum(dhx * a)
        dmod_o[2:3, :] += dgate
        da = dhx * (1.0 + scale)
        gnw_o[...] += _colsum(da * xn)
        dxn = da * nw
        dx_ref[...] = dx2 + r1 * (dxn - xn * jnp.mean(dxn * xn, axis=-1, keepdims=True))

        @pl.when(i == NTX - 1)
        def _():
            gpwin_o[...] = a_pwin[...].astype(BF16)
            gpgrp_o[...] = a_pgrp[...].astype(BF16)
            gpwout_o[...] = a_pwout[...].astype(BF16)

    tile = pl.BlockSpec((TM, D), lambda i: (i, 0))
    outs = (_sds((T, D), F32), _sds((NDEV, D, SH_PWIN), BF16), _sds((4, PG, PG), BF16), _sds((E, D), BF16),
            _sds((3, D), F32), _sds((1, D), F32), _sds((1, D), F32), _sds((1, E), F32), _sds((1, D), F32))
    return pl.pallas_call(
        body, name="pool_layer", grid=(NTX,),
        in_specs=[tile, tile] + [VMEM_SPEC] * 10,
        out_specs=[tile] + [VMEM_SPEC] * 8,
        out_shape=outs,
        scratch_shapes=[pltpu.VMEM((NDEV, D, SH_PWIN), F32), pltpu.VMEM((4, PG, PG), F32), pltpu.VMEM((E, D), F32)],
        compiler_params=pltpu.CompilerParams(dimension_semantics=("arbitrary",), vmem_limit_bytes=VMEM_LIMIT),
    )(x1, tgt, mod1, nw1, fnw, pwin, pgrp, pscale, pwout, pb, pbt, pinv)


def _b3_out_bwd(dx1, o, g_all, gate, gw, wout, gpwout):
    def body(dx_ref, o_ref, z_ref, gate_ref, gw_ref, w_ref, gpwout_r, do_ref, dz_ref, gw_o, dgate_o, ggw_o, rpwout_o,
             acc, *rider):
        i = pl.program_id(0)
        bufs, sems = _rider2_split(rider, 1)
        finish = _scatter_rider2(i, NT, 2, ("rows",), (gpwout_r,), (rpwout_o,), bufs, sems)

        @pl.when(i == 0)
        def _():
            acc[...] = jnp.zeros_like(acc)
            dgate_o[...] = jnp.zeros_like(dgate_o)
            ggw_o[...] = jnp.zeros_like(ggw_o)
            do_ref[...] = jnp.zeros_like(do_ref)
            dz_ref[...] = jnp.zeros_like(dz_ref)

        @pl.when(i > 0)
        def _():
            gw = gw_ref[...]
            z = z_ref[...]
            og, r, on, zs, sz = _gated_norm(_get_heads(o_ref), z, gw)
            ogb = og.astype(BF16)
            dx = dx_ref[...]
            dgate_o[...] += _colsum(dx * _dot(ogb, w_ref[...]))
            dy = (dx * gate_ref[...]).astype(BF16)
            for j in range(4):
                cs = slice(j * PG, (j + 1) * PG)
                acc[:, cs] += _dot_ta(ogb, dy[:, cs])
            dog = _dot_tb(dy, w_ref[...])
            dz_ref[...] = (dog * (on * gw) * (zs * (1.0 + z * (1.0 - zs)))).astype(BF16)
            dong = dog * sz
            ggw_o[...] += _colsum(dong * on)
            don = dong * gw
            do = _head_map(lambda dh, nh, rh: rh * (dh - nh * jnp.mean(dh * nh, axis=-1, keepdims=True)), don, on, r)
            _put_heads(do_ref, (), do.astype(BF16))

        @pl.when(i == NT - 1)
        def _():
            gw_o[...] = acc[...].astype(BF16)

        finish()

    prev = lambda i: (jnp.maximum(i - 1, 0), 0)
    heads = pl.BlockSpec((HEADS, TM, DH), lambda i: (0, i, 0))
    return pl.pallas_call(
        body, name="b3_out_bwd", grid=(NT,),
        in_specs=[pl.BlockSpec((TM, D), prev), heads, pl.BlockSpec((TM, E), lambda i: (i, 4)),
                  VMEM_SPEC, VMEM_SPEC, VMEM_SPEC, HBM_SPEC],
        out_specs=[heads, pl.BlockSpec((TM, E), lambda i: (i, 0)), VMEM_SPEC, VMEM_SPEC, VMEM_SPEC, HBM_SPEC],
        out_shape=(_sds((HEADS, TT, DH), BF16), _sds((TT, E), BF16), _sds((E, D), BF16), _sds((1, D), F32), _sds((1, E), F32),
                   _sds((RS_SLOTS, SH_ROWS, D), BF16)),
        scratch_shapes=[pltpu.VMEM((E, D), F32)] + _rider2_scratch([(SH_ROWS, D)]),
        compiler_params=pltpu.CompilerParams(dimension_semantics=("arbitrary",), vmem_limit_bytes=VMEM_LIMIT),
    )(dx1, o, g_all, gate, gw, wout, gpwout)


def _gla_bwd(p0, p1, v_all, dec, do, mask01, gpwin, gpgrp):
    nch = TM // CHUNK
    n_steps = HEADS // GLA_HB

    def body(p0_ref, p1_ref, v_ref, dec_ref, do_ref, msk_ref, gpwin_r, gpgrp_r, d0_ref, d1_ref, dv_ref, dgl_ref, rpwin_o, rpgrp_o,
             ss_sc, dv_sc, ssem, rsem, lsem, *rider):
        finish_grp = _scatter_rider(pl.program_id(0), n_steps, ("grp",), (gpgrp_r,), (rpgrp_o,), ssem, rsem, lsem)
        bufs, sems = _rider2_split(rider, 1)
        finish_win = _scatter_rider2(pl.program_id(0), n_steps, 1, ("major",), (gpwin_r,), (rpwin_o,), bufs, sems)

        lanes = [(d, hh) for d in (0, 1) for hh in range(GLA_HB)]
        zero = jnp.zeros((len(lanes), DH, DH), F32)
        dgl_ref[...] = jnp.zeros_like(dgl_ref)

        def p_of(d):
            return p1_ref if d else p0_ref

        def scan_step(i, n):
            where = [_scan_tile(i, d == 1) for d in (0, 1)]
            cis = [nch - 1 - n if d else n for d, _ in lanes]
            e_mid, e_rest = _chunk_rows(dec_ref, lanes, cis, where)

            def chunk(arr):
                return jnp.stack([arr[l, ci * CHUNK:(ci + 1) * CHUNK] for l, ci in enumerate(cis)])

            return where, cis, e_mid, e_rest, chunk

        def stacked(i, fn):
            where = [_scan_tile(i, d == 1) for d in (0, 1)]
            return jnp.stack([fn(d, hh, where[d][1]) for d, hh in lanes])

        def fwd_body(i, st):
            v = stacked(i, lambda d, hh, rows: v_ref[hh, rows, :])
            kg = stacked(i, lambda d, hh, rows: p_of(d)[1, hh, rows, :])
            for n in range(nch):
                _, _, e_mid, e_rest, chunk = scan_step(i, n)
                ss_sc[i * nch + n] = st
                st = st * (e_mid * e_rest) + _bdot_tn(chunk(v), chunk(kg)) * e_rest
            return st

        ss_sc[NT * nch] = lax.fori_loop(0, NT, fwd_body, zero)

        def bwd_body(ii, dst):
            i = NT - 1 - ii
            qg, kg = [stacked(i, lambda d, hh, rows, ty=ty: p_of(d)[ty, hh, rows, :]) for ty in range(2)]
            v = stacked(i, lambda d, hh, rows: v_ref[hh, rows, :])
            dob = stacked(i, lambda d, hh, rows: do_ref[hh, rows, :])
            msk = jnp.stack([msk_ref[d] for d, _ in lanes])
            a = (_bdot_nt(qg, kg) * msk).astype(BF16)
            da = (_bdot_nt(dob, v) * msk).astype(BF16)
            dqg = _bdot(da, kg)
            dkg = _bdot_tn(da, qg)
            dv_intra = _bdot_tn(a, dob)
            dv_l, dkg_l, dqg_l = ([[None] * nch for _ in lanes] for _ in range(3))
            for n in range(nch - 1, -1, -1):
                where, cis, e_mid, e_rest, chunk = scan_step(i, n)
                s_c, s_end = ss_sc[i * nch + n], ss_sc[i * nch + n + 1]
                dste = (dst * e_rest).astype(BF16)
                kg_c, v_c, dob_c = chunk(kg), chunk(v), chunk(dob)
                dv_c = chunk(dv_intra) + _bdot_nt(kg_c, dste)
                dkg_c = chunk(dkg) + _bdot(v_c, dste)
                dqg_c = chunk(dqg) + _bdot(dob_c, s_c * e_mid)
                dgl = jnp.sum(s_end * dst, axis=1, keepdims=True)
                for l, ((d, hh), ci) in enumerate(zip(lanes, cis)):
                    dv_l[l][ci], dkg_l[l][ci], dqg_l[l][ci] = dv_c[l], dkg_c[l], dqg_c[l]
                    dgl_ref[d, where[d][0], ci:ci + 1, hh * DH:(hh + 1) * DH] = dgl[l]
                dst = dst * (e_mid * e_rest) + _bdot_tn(dob_c, chunk(qg)) * e_mid
            where = [_scan_tile(i, d == 1) for d in (0, 1)]
            for l, (d, hh) in enumerate(lanes):
                rows = where[d][1]
                d_ref = d1_ref if d else d0_ref
                d_ref[0, hh, rows, :] = jnp.concatenate(dqg_l[l], axis=0).astype(BF16)
                d_ref[1, hh, rows, :] = jnp.concatenate(dkg_l[l], axis=0).astype(BF16)
                dv_sc[d, hh, rows, :] = jnp.concatenate(dv_l[l], axis=0).astype(BF16)
            return dst

        lax.fori_loop(0, NT, bwd_body, zero)
        dv_ref[...] = (dv_sc[0].astype(F32) + dv_sc[1].astype(F32)).astype(BF16)
        finish_grp()
        finish_win()

    quad = pl.BlockSpec((2, GLA_HB, TT, DH), lambda h: (0, h, 0, 0))
    col = pl.BlockSpec((GLA_HB, TT, DH), lambda h: (h, 0, 0))
    chunkv = pl.BlockSpec((2, NT, 8, GLA_HB * DH), lambda h: (0, 0, 0, h))
    outs = (_sds((2, HEADS, TT, DH), BF16), _sds((2, HEADS, TT, DH), BF16), _sds((HEADS, TT, DH), BF16), _sds((2, NT, 8, E), F32),
            _sds((RS_SLOTS, D, SH_PWIN), BF16), _sds((NDEV, 4, SH_GRP, PG), BF16))
    return pl.pallas_call(
        body, name="gla_bwd", grid=(n_steps,),
        in_specs=[quad, quad, col, chunkv, col, pl.BlockSpec((2, TM, TM), lambda h: (0, 0, 0)), HBM_SPEC, HBM_SPEC],
        out_specs=[quad, quad, col, chunkv, HBM_SPEC, HBM_SPEC],
        out_shape=outs,
        scratch_shapes=[pltpu.VMEM((NT * nch + 1, 2 * GLA_HB, DH, DH), F32), pltpu.VMEM((2, GLA_HB, TT, DH), BF16)] + _rider_sems(1)
        + _rider2_scratch([(D, SH_PWIN)]),
        compiler_params=pltpu.CompilerParams(dimension_semantics=("arbitrary",), vmem_limit_bytes=VMEM_LIMIT),
    )(p0, p1, v_all, dec, do, mask01, gpwin, gpgrp)


TMB = 128


def _gla_post_bwd(g_all, d0, d1, dgl, dv, dz, lb, cum01, gwout):
    nch = TMB // CHUNK

    def body(g_ref, d0_ref, d1_ref, dgl_ref, dv_ref, dz_ref, lb_ref, cum_ref, gwout_r, dg_ref, dlb_ref, rwout_o, *rider):
        i = pl.program_id(0)
        bufs, sems = _rider2_split(rider, 1)
        finish = _scatter_rider2(i, TT // TMB, 2, ("rows",), (gwout_r,), (rwout_o,), bufs, sems)

        @pl.when(i == 0)
        def _():
            dlb_ref[...] = jnp.zeros_like(dlb_ref)

        half = i & 1
        qpre = g_ref[:, 3 * E:4 * E]
        dqs_sum = None
        dpre = []
        for d, d_ref in ((0, d0_ref), (1, d1_ref)):
            rev = d == 1
            lbd = lb_ref[d:d + 1, :]
            t = _gla_gates(g_ref[:, d * E:(d + 1) * E], qpre, lbd, cum_ref[d, :TMB, :TMB], rev)
            dqs = _get_heads(d_ref, (0,)).astype(F32) * t["e_q"]
            dk = _get_heads(d_ref, (1,)).astype(F32) * t["e_k"]
            dg = t["qs"] * dqs - t["k"] * dk
            dgl8 = dgl_ref[d, 0]
            dgl_rows = [jnp.where(half == 0, dgl8[ci:ci + 1, :], dgl8[nch + ci:nch + ci + 1, :]) for ci in range(nch)]
            dgl_b = jnp.concatenate([jnp.broadcast_to(dgl_rows[ci], (CHUNK, E)) for ci in range(nch)], axis=0)
            pos = lax.broadcasted_iota(jnp.int32, (TMB, E), 0) & (CHUNK - 1)
            dg = dg + jnp.where(pos == (0 if rev else CHUNK - 1), dgl_b, 0.0)
            dlf = _dot01(cum_ref[1 - d, :TMB, :TMB], dg)
            df = dlf / t["f"] - dk
            sig = t["sig"]
            dpre.append((df * (1.0 - lbd) * sig * (1.0 - sig)).astype(BF16))
            dlb_ref[d:d + 1, :] += _colsum(df * (1.0 - sig))
            dqs_sum = dqs if dqs_sum is None else dqs_sum + dqs
            qsig = t["qsig"]
        dqpre = dqs_sum * (DH ** -0.5) * (qsig * (1.0 + qpre * (1.0 - qsig)))
        dg_ref[...] = jnp.concatenate([dpre[0], dpre[1], _get_heads(dv_ref), dqpre.astype(BF16), dz_ref[...]], axis=1)
        finish()

    quad = pl.BlockSpec((2, HEADS, TMB, DH), lambda i: (0, 0, i, 0))
    tile = pl.BlockSpec((TMB, E), lambda i: (i, 0))
    return pl.pallas_call(
        body, name="gla_post_bwd", grid=(TT // TMB,),
        in_specs=[pl.BlockSpec((TMB, 4 * E), lambda i: (i, 0)), quad, quad,
                  pl.BlockSpec((2, 1, 8, E), lambda i: (0, i // 2, 0, 0)), pl.BlockSpec((HEADS, TMB, DH), lambda i: (0, i, 0)), tile,
                  VMEM_SPEC, VMEM_SPEC, HBM_SPEC],
        out_specs=[pl.BlockSpec((TMB, WIN_COLS), lambda i: (i, 0)), VMEM_SPEC, HBM_SPEC],
        out_shape=(_sds((TT, WIN_COLS), BF16), _sds((2, E), F32), _sds((RS_SLOTS, SH_ROWS, D), BF16)),
        scratch_shapes=_rider2_scratch([(SH_ROWS, D)]),
        compiler_params=pltpu.CompilerParams(dimension_semantics=("arbitrary",), vmem_limit_bytes=VMEM_LIMIT),
    )(g_all, d0, d1, dgl, dv, dz, lb, cum01, gwout)


WIN_SLOTS = 4


def _scatter_order(s, core):
    return (NDEV - 1 - s) ^ jnp.where((s >= 2) & (s <= 5) & ((s & 1) == core), 6, 0)


def _b1_in_bwd(idx1, ctx, x, dx1, dg, nw, msel, win):
    last_s = NDEV - 1
    half = D // 2

    def body(idx_ref, ctx_ref, x_ref, dx1_ref, dg_ref, nw_ref, m_ref, w_ref, gx_ref, rwin_o, dmx_o, dmc_o, gnw_o,
             hx_sc, dhx_sc, acc, sbuf, pbuf, rbuf, psend, precv, isend, irecv, dsend, drecv, sibsem, lsem):
        del idx_ref
        s, i = pl.program_id(0), pl.program_id(1)
        x, y, cc, idx = _mesh_pos()
        shift, scale = m_ref[0, 0:1, :], m_ref[0, 1:2, :]
        sibling = (x, y, 1 - cc)

        def partial(p):
            return pltpu.make_async_remote_copy(src_ref=sbuf.at[0], dst_ref=pbuf.at[p], send_sem=psend.at[p], recv_sem=precv.at[p],
                                                device_id=sibling, device_id_type=MESH)

        def chip_sum(p):
            return pltpu.make_async_remote_copy(src_ref=sbuf.at[1], dst_ref=rwin_o.at[2 + p], send_sem=isend.at[p], recv_sem=irecv.at[p],
                                                device_id=_peer(x, y, cc, 2 * (p + 1)), device_id_type=MESH)

        def relay(h):
            return pltpu.make_async_remote_copy(src_ref=sbuf.at[1, pl.ds(h * half, half), :], dst_ref=rbuf.at[h], send_sem=dsend.at[h],
                                                recv_sem=drecv.at[h], device_id=_peer(x, y, cc, 2 * (h + 1)), device_id_type=MESH)

        to_sibling = pltpu.make_async_remote_copy(src_ref=sbuf.at[0], dst_ref=rwin_o.at[1], send_sem=sibsem.at[0], recv_sem=sibsem.at[1],
                                                  device_id=sibling, device_id_type=MESH)
        own = pltpu.make_async_copy(sbuf.at[1], rwin_o.at[0], lsem)

        @pl.when((s == 0) & (i == 0))
        def _():
            for ref in (dmx_o, dmc_o, gnw_o):
                ref[...] = jnp.zeros_like(ref)

        @pl.when(s == 0)
        def _():
            hx, _, _, _ = _modulated(_ctx_or_x(i, ctx_ref, x_ref), nw_ref[...], shift, scale)
            hx_sc[i] = hx.astype(BF16)

        @pl.when(i == 0)
        def _():
            acc[...] = jnp.zeros_like(acc)

        dgb = dg_ref[...]
        hxb = hx_sc[i]
        for lo, hi in ((0, 256), (256, 512), (512, SH_WIN)):
            acc[:, lo:hi] += _dot_ta(hxb, dgb[:, lo:hi])
        part = _dot_tb(dgb, w_ref[0])

        @pl.when(s == 0)
        def _():
            dhx_sc[i] = part

        @pl.when(s > 0)
        def _():
            dhx_sc[i] += part

        done = i == NT - 1

        def hand_over(p, before):
            before.wait_send()
            sbuf[0] = acc[...].astype(BF16)
            partial(p).start()

        def send_chip_sum(p, before):
            for cp in before:
                cp.wait_send()
            partial(p).wait_recv()
            sbuf[1] = (acc[...] + pbuf[p].astype(F32)).astype(BF16)
            h = 1 - p
            rows = pl.ds(h * half, half)
            relay(h).wait_recv()
            sbuf[1, rows, :] = (acc[rows, :] + pbuf[p, rows, :].astype(F32) + rbuf[h].astype(F32)).astype(BF16)
            chip_sum(p).start()

        @pl.when(done & (s == 0))
        def _():
            sbuf[0] = acc[...].astype(BF16)
            partial(2).start()

        @pl.when(done & (s == 1))
        def _():
            partial(2).wait_recv()
            sbuf[1] = (acc[...] + pbuf[2].astype(F32)).astype(BF16)
            for h in range(2):
                relay(h).start()

        for core in range(2):
            @pl.when(done & (cc == core) & (s == 2))
            def _(core=core):
                hand_over(core, partial(2))

            @pl.when(done & (cc == core) & (s == 3))
            def _(core=core):
                send_chip_sum(1 - core, [relay(0), relay(1)])

            @pl.when(done & (cc == core) & (s == 4))
            def _(core=core):
                hand_over(1 - core, partial(core))

            @pl.when(done & (cc == core) & (s == 5))
            def _(core=core):
                send_chip_sum(core, [chip_sum(1 - core)])

            @pl.when(done & (cc == core) & (s == last_s - 1))
            def _(core=core):
                partial(1 - core).wait_send()
                sbuf[0] = acc[...].astype(BF16)
                to_sibling.start()

            @pl.when(done & (cc == core) & (s == last_s))
            def _(core=core):
                chip_sum(core).wait_send()
                sbuf[1] = acc[...].astype(BF16)
                own.start()

        @pl.when(s == last_s)
        def _():
            nw = nw_ref[...]
            _, r, xn, a = _modulated(_ctx_or_x(i, ctx_ref, x_ref), nw, shift, scale)
            dhx = dhx_sc[i]
            dsh, dsc = _colsum(dhx), _colsum(dhx * a)
            da = dhx * (1.0 + scale)
            gnw_o[...] += _colsum(da * xn)
            dxn = da * nw
            gx_ref[...] = dx1_ref[...] + r * (dxn - xn * jnp.mean(dxn * xn, axis=-1, keepdims=True))

            @pl.when(i == 0)
            def _():
                dmc_o[0:1, :] += dsh
                dmc_o[1:2, :] += dsc

            @pl.when(i > 0)
            def _():
                dmx_o[0:1, :] += dsh
                dmx_o[1:2, :] += dsc

        @pl.when((i == NT - 1) & (s == last_s))
        def _():
            to_sibling.wait_send()
            to_sibling.wait_recv()
            for p in range(2):
                chip_sum(p).wait_recv()
            own.wait()

    grid_spec = pltpu.PrefetchScalarGridSpec(
        num_scalar_prefetch=1, grid=(NDEV, NT),
        in_specs=[VMEM_SPEC,
                  pl.BlockSpec((TM, D), lambda s, i, ix: (jnp.where((s == 0) | (s == last_s), jnp.maximum(i - 1, 0), NTX - 1), 0)),
                  pl.BlockSpec((TM, D), lambda s, i, ix: (jnp.where(s == last_s, jnp.maximum(i - 1, 0), 0), 0)),
                  pl.BlockSpec((TM, SH_WIN), lambda s, i, ix: (i, ix[0] ^ _scatter_order(s, ix[0] & 1))), VMEM_SPEC,
                  pl.BlockSpec((1, 2, D), lambda s, i, ix: (jnp.minimum(i, 1), 0, 0)),
                  pl.BlockSpec((1, D, SH_WIN), lambda s, i, ix: (ix[0] ^ _scatter_order(s, ix[0] & 1), 0, 0))],
        out_specs=[pl.BlockSpec((TM, D), lambda s, i, ix: (jnp.where(s == last_s, jnp.maximum(i - 1, 0), 0), 0)),
                   HBM_SPEC, VMEM_SPEC, VMEM_SPEC, VMEM_SPEC],
        scratch_shapes=[pltpu.VMEM((NT, TM, D), BF16), pltpu.VMEM((NT, TM, D), F32), pltpu.VMEM((D, SH_WIN), F32),
                        pltpu.VMEM((2, D, SH_WIN), BF16), pltpu.VMEM((3, D, SH_WIN), BF16), pltpu.VMEM((2, half, SH_WIN), BF16),
                        pltpu.SemaphoreType.DMA((3,)), pltpu.SemaphoreType.DMA((3,)), pltpu.SemaphoreType.DMA((2,)),
                        pltpu.SemaphoreType.DMA((2,)), pltpu.SemaphoreType.DMA((2,)), pltpu.SemaphoreType.DMA((2,)),
                        pltpu.SemaphoreType.DMA((2,)), pltpu.SemaphoreType.DMA])
    return pl.pallas_call(
        body, name="b1_in_bwd", grid_spec=grid_spec,
        out_shape=(_sds((T, D), F32), _sds((WIN_SLOTS, D, SH_WIN), BF16), _sds((2, D), F32), _sds((2, D), F32), _sds((1, D), F32)),
        compiler_params=pltpu.CompilerParams(dimension_semantics=("arbitrary", "arbitrary"), vmem_limit_bytes=VMEM_LIMIT),
    )(idx1, ctx, x, dx1, dg, nw, msel, win)


def _reduce_small(pd, pv, cg, c_ctx, ada_w0):
    n_arr = 3

    def body(pd_r, pv_r, cg_r, cctx_r, ada_r, gada_o, gadab_o, gcctx_o, pvsum_o, loss_o,
             pd_all, pv_all, dsc_all, dsc_mine, ssem, rsem):
        x, y, cc, idx = _mesh_pos()
        srcs = [pd_r, pv_r, dsc_mine]
        dsts = [pd_all.at[idx], pv_all.at[idx], dsc_all.at[idx]]

        def remote(a, k):
            return pltpu.make_async_remote_copy(src_ref=srcs[a], dst_ref=dsts[a], send_sem=ssem.at[a, k], recv_sem=rsem.at[a, k],
                                                device_id=_peer(x, y, cc, k), device_id_type=MESH)

        first = [remote(a, k) for k in range(1, NDEV) for a in (0, 1)]
        for cp in first:
            cp.start()
        pd_all[idx] = pd_r[...]
        pv_all[idx] = pv_r[...]
        for k in range(1, NDEV):
            remote(0, k).wait_recv()
            remote(1, k).wait_recv()
        mine = [pd_all[s, :, pl.ds(idx, 1), :] for s in range(NDEV)]
        dmc = functools.reduce(lambda u, v: u + v, [m[2] for m in mine])
        rows = _stack_rows([cg_r[i] for i in range(NDEV)] + [cctx_r[...]])
        sc = (rows * _sigmoid(rows)).astype(BF16)
        gada_o[0] = _dot_ta(sc, _stack_rows([m[0] for m in mine] + [dmc]))
        gada_o[1] = _dot_ta(sc, _stack_rows([m[1] for m in mine]))
        dsc_mine[...] = _dot_tb(jnp.broadcast_to(dmc, (8, SH_ADA)), ada_r[...])[0:1, :]
        dsc_all[idx] = dsc_mine[...]
        second = [remote(2, k) for k in range(1, NDEV)]
        for cp in second:
            cp.start()
        tot = [functools.reduce(lambda u, v: u + v, [pd_all[s, l] for s in range(NDEV)]) for l in range(3)]
        gadab_o[0] = tot[0] + tot[2]
        gadab_o[1] = tot[1]
        pvs = functools.reduce(lambda u, v: u + v, [pv_all[s] for s in range(NDEV)])
        pvsum_o[...] = pvs
        loss_o[...] = jnp.broadcast_to(jnp.sum(pvs[:, PV_LOSS:PV_LOSS + D], axis=-1, keepdims=True) * (0.5 / D), (1, 128))
        for k in range(1, NDEV):
            remote(2, k).wait_recv()
        dsc = functools.reduce(lambda u, v: u + v, [dsc_all[s] for s in range(NDEV)])
        cx = cctx_r[...]
        sx = _sigmoid(cx)
        gcctx_o[...] = dsc * (sx * (1.0 + cx * (1.0 - sx)))
        for cp in first + second:
            cp.wait_send()

    outs = (_sds((2, D, SH_ADA), F32), _sds((2, NDEV, SH_ADA), F32), _sds((1, D), F32), _sds((1, PV_LEN), F32), _sds((1, 128), F32))
    return pl.pallas_call(
        body, name="reduce_small", out_shape=outs,
        in_specs=[VMEM_SPEC] * 5, out_specs=[VMEM_SPEC] * 5,
        scratch_shapes=[
            pltpu.VMEM((NDEV, 3, NDEV, SH_ADA), F32), pltpu.VMEM((NDEV, 1, PV_LEN), F32), pltpu.VMEM((NDEV, 1, D), F32),
            pltpu.VMEM((1, D), F32),
            pltpu.SemaphoreType.DMA((n_arr, NDEV)), pltpu.SemaphoreType.DMA((n_arr, NDEV)),
        ],
        compiler_params=pltpu.CompilerParams(vmem_limit_bytes=VMEM_LIMIT),
    )(pd, pv, cg, c_ctx, ada_w0)


PV_NW, PV_GNORM, PV_FINAL, PV_LB, PV_PSCALE, PV_LOSS, PV_LEN = 0, 2 * D, 3 * D, 4 * D, 6 * D, 7 * D, 8 * D


def _adamw(w, g, m, v):
    m = ADAM_B1 * m + (1.0 - ADAM_B1) * g
    v = ADAM_B2 * v + (1.0 - ADAM_B2) * (g * g)
    m_hat = m / (1.0 - ADAM_B1 ** ADAM_STEP)
    v_hat = v / (1.0 - ADAM_B2 ** ADAM_STEP)
    delta = -ADAM_LR * (m_hat / (jnp.sqrt(v_hat) + ADAM_EPS) + ADAM_WD * w)
    return delta, m, v


ADAM_STEPS = 8


def _adam_all(sharded, dense, small, lb_idx, lbv):
    ns, nd, nsm = len(sharded), len(dense), len(small)

    def body(*refs):
        it = iter(refs)
        sh_in = [[next(it) for _ in range(4)] for _ in range(ns)]
        de_in = [[next(it) for _ in range(4)] for _ in range(nd)]
        sm_in = [[next(it) for _ in range(4)] for _ in range(nsm)]
        lb_r = next(it)
        sh_out = [[next(it) for _ in range(4)] for _ in range(ns)]
        de_out = [[next(it) for _ in range(3)] for _ in range(nd)]
        sm_out = [[next(it) for _ in range(4)] for _ in range(nsm)]
        for (p, w, m, v), outs in zip(sh_in, sh_out):
            g = p[0].astype(F32)
            for s in range(1, p.shape[0]):
                g = g + p[s].astype(F32)
            d, mn, vn = _adamw(w[...], g, m[...], v[...])
            outs[0][...], outs[1][...], outs[2][...], outs[3][...] = g, d, mn, vn
        for (g, w, m, v), outs in zip(de_in, de_out):
            d, mn, vn = _adamw(w[...], g[...], m[...], v[...])
            outs[0][...], outs[1][...], outs[2][...] = d, mn, vn

        @pl.when(pl.program_id(0) == 0)
        def _():
            for j, ((g, w, m, v), outs) in enumerate(zip(sm_in, sm_out)):
                gj = g[...]
                if j == lb_idx:
                    gj = gj * lb_r[...] * (1.0 - lb_r[...])
                d, mn, vn = _adamw(w[...], gj, m[...], v[...])
                outs[0][...], outs[1][...], outs[2][...], outs[3][...] = gj, d, mn, vn

    def tile(a):
        return pl.BlockSpec((a.shape[0] // ADAM_STEPS, a.shape[1]), lambda i: (i, 0))

    in_specs, out_specs, out_shape, args = [], [], [], []
    for p, w, m, v in sharded:
        in_specs += [pl.BlockSpec((p.shape[0], p.shape[1] // ADAM_STEPS, p.shape[2]), lambda i: (0, i, 0))] + [tile(w)] * 3
        args += [p, w, m, v]
    for g, w, m, v in dense:
        in_specs += [tile(w)] * 4
        args += [g, w, m, v]
    for g, w, m, v in small:
        in_specs += [VMEM_SPEC] * 4
        args += [g, w, m, v]
    in_specs.append(VMEM_SPEC)
    args.append(lbv)
    for _, w, _, _ in sharded:
        out_specs += [tile(w)] * 4
        out_shape += [_sds(w.shape, F32)] * 4
    for _, w, _, _ in dense:
        out_specs += [tile(w)] * 3
        out_shape += [_sds(w.shape, F32)] * 3
    for _, w, _, _ in small:
        out_specs += [VMEM_SPEC] * 4
        out_shape += [_sds(w.shape, F32)] * 4
    res = pl.pallas_call(body, name="adam_all", grid=(ADAM_STEPS,), in_specs=in_specs, out_specs=out_specs, out_shape=tuple(out_shape),
                         compiler_params=pltpu.CompilerParams(dimension_semantics=("arbitrary",), vmem_limit_bytes=VMEM_LIMIT))(*args)
    it = iter(res)
    return ([tuple(next(it) for _ in range(4)) for _ in range(ns)], [tuple(next(it) for _ in range(3)) for _ in range(nd)],
            [tuple(next(it) for _ in range(4)) for _ in range(nsm)])


def kernel(x, c, ctx, c_ctx, ada_w, ada_b, norm_w, hgrn_w_in, hgrn_lb_logits, hgrn_gnorm_w, hgrn_w_out, pool_w_in, pool_w_grp, pool_scale, pool_w_out, final_norm_w, loss_target, m_c_ctx, m_ada_w, m_ada_b, m_norm_w, m_hgrn_w_in, m_hgrn_lb_logits, m_hgrn_gnorm_w, m_hgrn_w_out, m_pool_w_in, m_pool_w_grp, m_pool_scale, m_pool_w_out, m_final_norm_w, v_c_ctx, v_ada_w, v_ada_b, v_norm_w, v_hgrn_w_in, v_hgrn_lb_logits, v_hgrn_gnorm_w, v_hgrn_w_out, v_pool_w_in, v_pool_w_grp, v_pool_scale, v_pool_w_out, v_final_norm_w):
    idx = 4 * lax.axis_index("x") + 2 * lax.axis_index("y") + lax.axis_index("c")
    cctx2 = c_ctx.reshape(1, D)
    cum01, mask01 = _gla_consts()
    pb, pbt, pinv = _pool_consts()

    idx1 = idx.reshape(1).astype(jnp.int32)
    nw0, nw1 = norm_w[0:1], norm_w[1:2]
    fnw = final_norm_w.reshape(1, D)
    g_all, win, s_wout, s_pwin, s_pgrp, s_pwout, lbl_g, ps_g, cg, mod0, mod1, modc = _f1_gather_matmul(
        idx1, ctx[0], x[0], nw0, hgrn_w_in[0], hgrn_w_out[0], pool_w_in[0], pool_w_grp[0], pool_w_out[0], hgrn_lb_logits[0],
        pool_scale, c, cctx2, ada_w, ada_b)
    lb = jax.nn.sigmoid(jnp.transpose(lbl_g, (1, 0, 2)).reshape(2, E))
    pscale = ps_g.reshape(1, E)
    msel = jnp.stack([modc[:2], mod0[:2]])
    p0, p1, v_all, dec, wout, pgrp = _gla_prep(g_all, lb, cum01, s_wout, s_pgrp)
    o, pwin, pwout = _gla_fwd(p0, p1, v_all, dec, mask01, s_pwin, s_pwout)
    x1 = _f3_out(o, g_all, x[0], mod0[2:3], hgrn_gnorm_w, wout)
    dx1, gpwin, gpgrp, gpwout, dmod1, gnw1, gfw, gps, lossv = _pool_layer(
        x1, loss_target[0], mod1, nw1, fnw, pwin, pgrp, pscale, pwout, pb, pbt, pinv)
    do, dz, gwout, dgate0, ggw, rpwout = _b3_out_bwd(dx1, o, g_all, mod0[2:3], hgrn_gnorm_w, wout, gpwout)
    d0, d1, dv, dgl, rpwin, rpgrp = _gla_bwd(p0, p1, v_all, dec, do, mask01, gpwin, gpgrp)
    dg, dlb, rwout = _gla_post_bwd(g_all, d0, d1, dgl, dv, dz, lb, cum01, gwout)
    grad_x, rwin, dmx, dmc, gnw0 = _b1_in_bwd(idx1, ctx[0], x[0], dx1, dg, nw0, msel, win)

    dmod0 = jnp.concatenate([dmx, dgate0], axis=0)
    dmodc = jnp.concatenate([dmc, jnp.zeros((1, D), F32)], axis=0)
    pd = jnp.stack([dmod0, dmod1, dmodc]).reshape(3, NDEV, SH_ADA)
    pv = jnp.concatenate([gnw0, gnw1, ggw, gfw, dlb.reshape(1, 2 * E), gps, lossv], axis=1)
    g_ada, g_adab, g_cctx, pvsum, loss128 = _reduce_small(pd, pv, cg, cctx2, ada_w[0])

    g2 = (4 * SH_GRP, PG)
    sharded_names = ["hgrn_w_in", "hgrn_w_out", "pool_w_in", "pool_w_grp", "pool_w_out"]
    sharded = [(rwin, hgrn_w_in[0], m_hgrn_w_in[0], v_hgrn_w_in[0]),
               (rwout, hgrn_w_out[0], m_hgrn_w_out[0], v_hgrn_w_out[0]),
               (rpwin, pool_w_in[0], m_pool_w_in[0], v_pool_w_in[0]),
               (rpgrp.reshape((NDEV,) + g2), pool_w_grp[0].reshape(g2), m_pool_w_grp[0].reshape(g2), v_pool_w_grp[0].reshape(g2)),
               (rpwout, pool_w_out[0], m_pool_w_out[0], v_pool_w_out[0])]
    a2 = (2 * D, SH_ADA)
    g_ada2 = g_ada.reshape(a2)
    dense = [(g_ada2, ada_w.reshape(a2), m_ada_w.reshape(a2), v_ada_w.reshape(a2))]
    lb_me = lax.dynamic_slice_in_dim(lb, idx * DH, DH, axis=1)
    small_names = ["c_ctx", "ada_b", "norm_w", "hgrn_lb_logits", "hgrn_gnorm_w", "pool_scale", "final_norm_w"]
    small = [(g_cctx, cctx2, m_c_ctx.reshape(1, D), v_c_ctx.reshape(1, D)),
             (g_adab.reshape(2, 3 * D), ada_b, m_ada_b, v_ada_b),
             (pvsum[:, PV_NW:PV_NW + 2 * D].reshape(2, D), norm_w, m_norm_w, v_norm_w),
             (lax.dynamic_slice_in_dim(pvsum[:, PV_LB:PV_LB + 2 * E].reshape(2, E), idx * DH, DH, axis=1),
              hgrn_lb_logits[0], m_hgrn_lb_logits[0], v_hgrn_lb_logits[0]),
             (pvsum[:, PV_GNORM:PV_GNORM + E], hgrn_gnorm_w, m_hgrn_gnorm_w, v_hgrn_gnorm_w),
             (lax.dynamic_slice_in_dim(pvsum[:, PV_PSCALE:PV_PSCALE + E], idx * DH, DH, axis=1), pool_scale, m_pool_scale, v_pool_scale),
             (pvsum[:, PV_FINAL:PV_FINAL + D], fnw, m_final_norm_w.reshape(1, D), v_final_norm_w.reshape(1, D))]
    r_sharded, r_dense, r_small = _adam_all(sharded, dense, small, 3, lb_me)
    out = dict(zip(sharded_names, r_sharded))
    out["ada_w"] = (g_ada2,) + r_dense[0]
    out.update(zip(small_names, r_small))

    shapes = {"c_ctx": (D,), "ada_w": (2, D, SH_ADA), "ada_b": (2, 3 * D), "norm_w": (2, D), "hgrn_w_in": (1, D, SH_WIN),
              "hgrn_lb_logits": (1, 2, DH), "hgrn_gnorm_w": (1, E), "hgrn_w_out": (1, SH_ROWS, D), "pool_w_in": (1, D, SH_PWIN),
              "pool_w_grp": (1, 4, SH_GRP, PG), "pool_scale": (1, DH), "pool_w_out": (1, SH_ROWS, D), "final_norm_w": (D,)}
    order = ["c_ctx", "ada_w", "ada_b", "norm_w", "hgrn_w_in", "hgrn_lb_logits", "hgrn_gnorm_w", "hgrn_w_out", "pool_w_in",
             "pool_w_grp", "pool_scale", "pool_w_out", "final_norm_w"]
    flat = [out[name][q].reshape(shapes[name]) for q in range(4) for name in order]
    return (loss128[0, 0], grad_x[None], *flat)
```

```python
import functools

import numpy as np
import jax
import jax.numpy as jnp
from jax import lax
from jax.experimental import pallas as pl
from jax.experimental.pallas import tpu as pltpu

F32 = jnp.float32
BF16 = jnp.bfloat16

D = 1024
E = 1024
HEADS = 8
DH = 128
CHUNK = 64
T = 2048
TC = 256
TT = T + TC
TM = 256
NT = TT // TM
NTX = T // TM
NDEV = 8
GRID_W = 64
POOL_WINDOWS = (2, 4, 8, 16)
PG = 256
EPS = 1e-6
WIN_COLS = 5 * E
SH_WIN = WIN_COLS // NDEV
SH_PWIN = 2 * E // NDEV
SH_ROWS = E // NDEV
SH_GRP = PG // NDEV
SH_ADA = 3 * D // NDEV
VMEM_LIMIT = 56 * 1024 * 1024

ADAM_LR, ADAM_B1, ADAM_B2, ADAM_EPS, ADAM_WD, ADAM_STEP = 0.001, 0.9, 0.999, 1e-08, 0.01, 10

MESH = pl.DeviceIdType.MESH
VMEM_SPEC = pl.BlockSpec(memory_space=pltpu.VMEM)
HBM_SPEC = pl.BlockSpec(memory_space=pltpu.HBM)


def _sds(shape, dtype):
    return jax.ShapeDtypeStruct(shape, dtype)


def _bf(a):
    return a if a.dtype == BF16 else a.astype(BF16)


def _dot(a, b):
    return lax.dot_general(_bf(a), _bf(b), (((1,), (0,)), ((), ())), preferred_element_type=F32)


def _dot_tb(a, b):
    return lax.dot_general(_bf(a), _bf(b), (((1,), (1,)), ((), ())), preferred_element_type=F32)


def _dot_ta(a, b):
    return lax.dot_general(_bf(a), _bf(b), (((0,), (0,)), ((), ())), preferred_element_type=F32)


def _bdot(a, b):
    return lax.dot_general(_bf(a), _bf(b), (((2,), (1,)), ((0,), (0,))), preferred_element_type=F32)


def _bdot_nt(a, b):
    return lax.dot_general(_bf(a), _bf(b), (((2,), (2,)), ((0,), (0,))), preferred_element_type=F32)


def _bdot_tn(a, b):
    return lax.dot_general(_bf(a), _bf(b), (((1,), (1,)), ((0,), (0,))), preferred_element_type=F32)


def _dot01(m01, x):
    hi = x.astype(BF16)
    lo = (x - hi.astype(F32)).astype(BF16)
    return _dot(m01, hi) + _dot(m01, lo)


def _rstd(x):
    return lax.rsqrt(jnp.mean(x * x, axis=-1, keepdims=True) + EPS)


def _sigmoid(x):
    return jax.nn.sigmoid(x)


def _colsum(a):
    return jnp.sum(a, axis=0, keepdims=True)


def _stack_rows(rows):
    n = rows[0].shape[-1]
    rid = lax.broadcasted_iota(jnp.int32, (16, n), 0)
    out = jnp.zeros((16, n), F32)
    for i, r in enumerate(rows):
        out = jnp.where(rid == i, r, out)
    return out


def _head_map(fn, *arrs):
    outs = [fn(*[a[:, h * DH:(h + 1) * DH] for a in arrs]) for h in range(HEADS)]
    return jnp.concatenate(outs, axis=1)


def _gla_consts():
    r = np.arange(TM)[:, None]
    c = np.arange(TM)[None, :]
    same = (r // CHUNK) == (c // CHUNK)
    tril = same & (c <= r)
    triu = same & (c >= r)
    m = np.stack([tril, triu]).astype(np.float32)
    return jnp.asarray(m, BF16), jnp.asarray(m, F32)


def _pool_consts():
    r = np.arange(TM)[:, None]
    c = np.arange(TM)[None, :]
    same = (r // GRID_W) == (c // GRID_W)
    rp, cp = r % GRID_W, c % GRID_W
    bs, inv = [], []
    for w in POOL_WINDOWS:
        lo = np.clip(rp - w // 2, 0, GRID_W)
        hi = np.clip(rp - w // 2 + w, 0, GRID_W)
        bs.append(same & (cp >= lo) & (cp < hi))
        inv.append(1.0 / (hi - lo).astype(np.float32))
    b = np.stack(bs).astype(np.float32)
    bt = np.transpose(b, (0, 2, 1))
    return jnp.asarray(b, BF16), jnp.asarray(bt, BF16), jnp.asarray(np.stack(inv), F32)


def _mesh_pos():
    x, y, c = lax.axis_index("x"), lax.axis_index("y"), lax.axis_index("c")
    return x, y, c, 4 * x + 2 * y + c


def _peer(x, y, c, k):
    return (x ^ ((k >> 2) & 1), y ^ ((k >> 1) & 1), c ^ (k & 1))


def _small_gathers(refs, ssem, rsem):
    lb_r, ps_r, c_r, cctx_r, ada_r, adab_r, lb_o, ps_o, cg_o, mod_o, lb_out, ps_out, cg_out, mod0_o, mod1_o, modc_o = refs
    x, y, cc, idx = _mesh_pos()
    srcs = [lb_r, ps_r, c_r, mod_o.at[idx]]
    mine = [lb_o.at[idx], ps_o.at[idx], cg_o.at[idx], mod_o.at[idx]]

    def remote(a, k):
        return pltpu.make_async_remote_copy(src_ref=srcs[a], dst_ref=mine[a], send_sem=ssem.at[a, k], recv_sem=rsem.at[a, k],
                                            device_id=_peer(x, y, cc, k), device_id_type=MESH)

    first = [remote(a, k) for k in range(1, NDEV) for a in (2, 0, 1)]
    for cp in first:
        cp.start()
    lb_o[idx] = lb_r[...]
    ps_o[idx] = ps_r[...]
    cg_o[idx] = c_r[...]
    for k in range(1, NDEV):
        remote(2, k).wait_recv()
    rows = _stack_rows([cg_o[i] for i in range(NDEV)] + [cctx_r[...]])
    sc = rows * _sigmoid(rows)
    for l in range(2):
        mod_o[idx, l] = _dot(sc, ada_r[l])
    second = [remote(3, k) for k in range(1, NDEV)]
    for cp in second:
        cp.start()
    for k in range(1, NDEV):
        remote(3, k).wait_recv()

    def mod_rows(l, row):
        full = jnp.concatenate([mod_o[s, l, row, :] for s in range(NDEV)], axis=1) + adab_r[l:l + 1, :]
        return [full[:, j * D:(j + 1) * D] for j in range(3)]

    me = pl.ds(idx, 1)
    for out, parts in ((mod0_o, mod_rows(0, me)), (mod1_o, mod_rows(1, me)), (modc_o, mod_rows(0, slice(NDEV, NDEV + 1)))):
        for j in range(3):
            out[j:j + 1, :] = parts[j]
    for cp in first + second:
        cp.wait_send()
    for k in range(1, NDEV):
        for a in (0, 1):
            remote(a, k).wait_recv()
    lb_out[...] = lb_o[...]
    ps_out[...] = ps_o[...]
    cg_out[...] = cg_o[...]


def _gather_order(s, core):
    k = jnp.where(s == 2, 4, jnp.where(s == 4, 2, s))
    return k ^ jnp.where((core == 1) & (s >= 2) & (s <= 5), 6, 0)


GATHER_ISSUE = (1, 2, 4, 3, 5, 6, 7)
GATHER_ICI = (2, 4, 6)
GATHER_DIRECT = (1,) + GATHER_ICI
GLA_HB = 2
RS_SLOTS = 5


def _shard_of(kind, ref, i):
    if kind == "rows":
        return ref.at[pl.ds(pl.multiple_of(i * SH_ROWS, SH_ROWS), SH_ROWS), :]
    if kind == "major":
        return ref.at[i]
    assert kind == "grp"
    return ref.at[:, pl.ds(pl.multiple_of(i * SH_GRP, SH_GRP), SH_GRP), :]


def _gather_rider(step, n_steps, forward_at, kinds, srcs, outs, ssem, rsem, lsem):
    x, y, cc, idx = _mesh_pos()
    arrays = range(len(kinds))
    mine = [_shard_of(kinds[a], outs[a], idx) for a in arrays]

    def remote(a, k):
        return pltpu.make_async_remote_copy(src_ref=srcs[a], dst_ref=mine[a], send_sem=ssem.at[a, k], recv_sem=rsem.at[a, k],
                                            device_id=_peer(x, y, cc, k), device_id_type=MESH)

    def forward(a, k):
        blk = _shard_of(kinds[a], outs[a], idx ^ k)
        return pltpu.make_async_remote_copy(src_ref=blk, dst_ref=blk, send_sem=ssem.at[a, k ^ 1], recv_sem=rsem.at[a, k ^ 1],
                                            device_id=(x, y, 1 - cc), device_id_type=MESH)

    copies = [remote(a, k) for k in GATHER_DIRECT for a in arrays]
    passed = [forward(a, k) for k in GATHER_ICI for a in arrays]
    local = [pltpu.make_async_copy(srcs[a], mine[a], lsem.at[a]) for a in arrays]

    @pl.when(step == 0)
    def _():
        for cp in copies + local:
            cp.start()

    def pass_on():
        for k in GATHER_ICI:
            for a in arrays:
                remote(a, k).wait_recv()
                forward(a, k).start()

    if forward_at < n_steps:
        @pl.when(step == forward_at)
        def _():
            pass_on()

    def finish():
        @pl.when(step == n_steps - 1)
        def _():
            if forward_at == n_steps:
                pass_on()
            for cp in copies + passed:
                cp.wait_send()
            for a in arrays:
                remote(a, 1).wait_recv()
            for cp in passed:
                cp.wait_recv()
            for cp in local:
                cp.wait()

    return finish


def _scatter_rider(step, n_steps, kinds, grads, slots, ssem, rsem, lsem):
    x, y, cc, idx = _mesh_pos()
    arrays = range(len(kinds))
    dsts = [slots[a].at[idx] for a in arrays]

    def remote(a, k):
        px, py, pc = _peer(x, y, cc, k)
        return pltpu.make_async_remote_copy(src_ref=_shard_of(kinds[a], grads[a], 4 * px + 2 * py + pc), dst_ref=dsts[a],
                                            send_sem=ssem.at[a, k], recv_sem=rsem.at[a, k], device_id=(px, py, pc), device_id_type=MESH)

    copies = [remote(a, k) for k in GATHER_ISSUE for a in arrays]
    local = [pltpu.make_async_copy(_shard_of(kinds[a], grads[a], idx), dsts[a], lsem.at[a]) for a in arrays]

    @pl.when(step == 0)
    def _():
        for cp in copies + local:
            cp.start()

    def finish():
        @pl.when(step == n_steps - 1)
        def _():
            for cp in copies:
                cp.wait_send()
            for cp in copies:
                cp.wait_recv()
            for cp in local:
                cp.wait()

    return finish


def _rider_sems(n):
    return [pltpu.SemaphoreType.DMA((n, NDEV)), pltpu.SemaphoreType.DMA((n, NDEV)), pltpu.SemaphoreType.DMA((n,))]


def _scatter_rider2(step, n_steps, add_at, kinds, grads, slots, bufs, sems):
    x, y, cc, idx = _mesh_pos()
    sibling = (x, y, 1 - cc)
    arrays = range(len(kinds))
    psend, precv, isend, irecv, lown, sibsem, lself = sems

    def mine(a, i):
        return _shard_of(kinds[a], grads[a], i)

    def partial(a, p):
        return pltpu.make_async_remote_copy(src_ref=mine(a, idx ^ (2 * (p + 1)) ^ 1), dst_ref=bufs[a][1].at[p], send_sem=psend.at[a, p],
                                            recv_sem=precv.at[a, p], device_id=sibling, device_id_type=MESH)

    def load(a, p):
        return pltpu.make_async_copy(mine(a, idx ^ (2 * (p + 1))), bufs[a][0].at[p], lown.at[a, p])

    def chip_sum(a, p):
        return pltpu.make_async_remote_copy(src_ref=bufs[a][0].at[p], dst_ref=slots[a].at[2 + p], send_sem=isend.at[a, p],
                                            recv_sem=irecv.at[a, p], device_id=_peer(x, y, cc, 2 * (p + 1)), device_id_type=MESH)

    def to_sibling(a):
        return pltpu.make_async_remote_copy(src_ref=mine(a, idx ^ 1), dst_ref=slots[a].at[1], send_sem=sibsem.at[a, 0],
                                            recv_sem=sibsem.at[a, 1], device_id=sibling, device_id_type=MESH)

    def own(a):
        return pltpu.make_async_copy(mine(a, idx), slots[a].at[0], lself.at[a, 0])

    @pl.when(step == 0)
    def _():
        for a in arrays:
            for p in range(3):
                partial(a, p).start()
                load(a, p).start()
            to_sibling(a).start()
            own(a).start()

    @pl.when(step == add_at)
    def _():
        for a in arrays:
            for p in range(3):
                partial(a, p).wait_recv()
                load(a, p).wait()
                bufs[a][0][p] = (bufs[a][0][p].astype(F32) + bufs[a][1][p].astype(F32)).astype(BF16)
                chip_sum(a, p).start()

    def finish():
        @pl.when(step == n_steps - 1)
        def _():
            for a in arrays:
                for p in range(3):
                    partial(a, p).wait_send()
                    chip_sum(a, p).wait_send()
                    chip_sum(a, p).wait_recv()
                to_sibling(a).wait_send()
                to_sibling(a).wait_recv()
                own(a).wait()

    return finish


def _rider2_scratch(blocks):
    n = len(blocks)
    bufs = [pltpu.VMEM((3,) + tuple(b), BF16) for b in blocks for _ in range(2)]
    return bufs + [pltpu.SemaphoreType.DMA((n, 3)) for _ in range(5)] + [pltpu.SemaphoreType.DMA((n, 2)), pltpu.SemaphoreType.DMA((n, 1))]


def _rider2_split(refs, n):
    refs = list(refs)
    return [tuple(refs[2 * a:2 * a + 2]) for a in range(n)], tuple(refs[2 * n:2 * n + 7])


def _modulated(x, nw, shift, scale):
    r = _rstd(x)
    xn = x * r
    a = xn * nw
    return a * (1.0 + scale) + shift, r, xn, a


def _ctx_or_x(i, ctx_ref, x_ref):
    return jnp.where(i == 0, ctx_ref[...], x_ref[...])


def _f1_gather_matmul(idx1, ctx, x, nw, w_in, w_out, pw_in, pgrp, pw_out, lb_l, pscale, c, c_ctx, ada_w, ada_b):
    def body(idx_ref, ctx_ref, x_ref, nw_ref, win_r, wout_r, pwin_r, pgrp_r, pwout_r, lb_r, ps_r, c_r, cctx_r, ada_r, adab_r,
             g_ref, win_o, s_wout, s_pwin, s_pgrp, s_pwout, lb_o, ps_o, cg_o, mod0_o, mod1_o, modc_o,
             wslot, hx_sc, lb_g, ps_g, cg_g, mod_g, ssem, rsem, osem, dsem, sm_ssem, sm_rsem):
        del idx_ref
        s, i = pl.program_id(0), pl.program_id(1)
        x, y, cc, idx = _mesh_pos()
        k = _gather_order(s, cc)
        j = idx ^ k
        first = 4 - 2 * cc

        def remote(kk):
            return pltpu.make_async_remote_copy(src_ref=wslot.at[idx], dst_ref=wslot.at[idx], send_sem=ssem.at[kk], recv_sem=rsem.at[kk],
                                                device_id=_peer(x, y, cc, kk), device_id_type=MESH)

        def forward(kk):
            jj = idx ^ kk
            return pltpu.make_async_remote_copy(src_ref=wslot.at[jj], dst_ref=wslot.at[jj], send_sem=ssem.at[kk ^ 1],
                                                recv_sem=rsem.at[kk ^ 1], device_id=(x, y, 1 - cc), device_id_type=MESH)

        def relay(h):
            blk = wslot.at[idx ^ (4 >> h), pl.ds(h * (D // 2), D // 2), :]
            return pltpu.make_async_remote_copy(src_ref=blk, dst_ref=blk, send_sem=dsem.at[0, h], recv_sem=dsem.at[1, h],
                                                device_id=_peer(x, y, cc, 2 << h), device_id_type=MESH)

        def to_hbm(jj, kk):
            return pltpu.make_async_copy(wslot.at[jj], win_o.at[jj], osem.at[kk])

        @pl.when((s == 0) & (i == 0))
        def _():
            _small_gathers((lb_r, ps_r, c_r, cctx_r, ada_r, adab_r, lb_g, ps_g, cg_g, mod_g, lb_o, ps_o, cg_o, mod0_o, mod1_o, modc_o),
                           sm_ssem, sm_rsem)
            wslot[idx] = win_r[...].astype(BF16)
            remote(1).start()
            remote(first).start()
            s_wout[...] = wout_r[...].astype(BF16)
            s_pwin[...] = pwin_r[...].astype(BF16)
            s_pgrp[...] = pgrp_r[...].astype(BF16)
            s_pwout[...] = pwout_r[...].astype(BF16)

        @pl.when(s == 0)
        def _():
            shift = jnp.where(i == 0, modc_o[0:1, :], mod0_o[0:1, :])
            scale = jnp.where(i == 0, modc_o[1:2, :], mod0_o[1:2, :])
            hx, _, _, _ = _modulated(_ctx_or_x(i, ctx_ref, x_ref), nw_ref[...], shift, scale)
            hx_sc[i] = hx.astype(BF16)

        @pl.when((s == 2) & (i == 0))
        def _():
            remote(6 - first).start()

        @pl.when((s > 0) & (i == 0) & (k != 6))
        def _():
            remote(k).wait_recv()

            @pl.when((k & 1) == 0)
            def _():
                forward(k).start()

            for h in range(2):
                @pl.when(k == 4 >> h)
                def _():
                    relay(h).start()

        @pl.when((i == 0) & (k == 6))
        def _():
            for h in range(2):
                relay(h).wait_recv()
            forward(6).start()

        @pl.when(i == 0)
        def _():
            to_hbm(j, k).start()

        g_ref[...] = jnp.dot(hx_sc[i], wslot[j], preferred_element_type=F32)

        @pl.when((s == NDEV - 1) & (i == NT - 1))
        def _():
            for kk in (1, 2, 4):
                remote(kk).wait_send()
            for kk in GATHER_ICI:
                forward(kk).wait_send()
            for h in range(2):
                relay(h).wait_send()
            for kk in range(NDEV):
                to_hbm(idx ^ kk, kk).wait()

    grid_spec = pltpu.PrefetchScalarGridSpec(
        num_scalar_prefetch=1, grid=(NDEV, NT),
        in_specs=[VMEM_SPEC, pl.BlockSpec((TM, D), lambda s, i, ix: (jnp.where(s == 0, jnp.maximum(i - 1, 0), NTX - 1), 0))]
        + [VMEM_SPEC] * 12,
        out_specs=[pl.BlockSpec((TM, SH_WIN), lambda s, i, ix: (i, ix[0] ^ _gather_order(s, ix[0] & 1))), HBM_SPEC] + [VMEM_SPEC] * 10,
        scratch_shapes=[pltpu.VMEM((NDEV, D, SH_WIN), BF16), pltpu.VMEM((NT, TM, D), BF16),
                        pltpu.VMEM((NDEV, 2, DH), F32), pltpu.VMEM((NDEV, 1, DH), F32), pltpu.VMEM((NDEV, 1, D), F32),
                        pltpu.VMEM((NDEV, 2, 16, SH_ADA), F32),
                        pltpu.SemaphoreType.DMA((NDEV,)), pltpu.SemaphoreType.DMA((NDEV,)), pltpu.SemaphoreType.DMA((NDEV,)),
                        pltpu.SemaphoreType.DMA((2, 2)),
                        pltpu.SemaphoreType.DMA((4, NDEV)), pltpu.SemaphoreType.DMA((4, NDEV))])
    outs = (_sds((TT, WIN_COLS), F32), _sds((NDEV, D, SH_WIN), BF16),
            _sds((SH_ROWS, D), BF16), _sds((D, SH_PWIN), BF16), _sds((4, SH_GRP, PG), BF16), _sds((SH_ROWS, D), BF16),
            _sds((NDEV, 2, DH), F32), _sds((NDEV, 1, DH), F32), _sds((NDEV, 1, D), F32),
            _sds((3, D), F32), _sds((3, D), F32), _sds((3, D), F32))
    return pl.pallas_call(
        body, name="f1_gather_matmul", grid_spec=grid_spec, out_shape=outs,
        compiler_params=pltpu.CompilerParams(dimension_semantics=("arbitrary", "arbitrary"), vmem_limit_bytes=VMEM_LIMIT),
    )(idx1, ctx, x, nw, w_in, w_out, pw_in, pgrp, pw_out, lb_l, pscale, c, c_ctx, ada_w, ada_b)


def _gla_gates(pre, qpre, lbd, cum, rev):
    rows, n = pre.shape
    nch = rows // CHUNK
    sig = _sigmoid(pre)
    f = lbd + (1.0 - lbd) * sig
    k = 1.0 - f
    g = _dot01(cum, jnp.log(f))
    g3 = g.reshape(nch, CHUNK, n)
    last = 0 if rev else CHUNK - 1
    mid = CHUNK // 2 if rev else CHUNK // 2 - 1
    gl1, gm1 = g3[:, last:last + 1, :], g3[:, mid:mid + 1, :]

    def bc(a):
        return jnp.broadcast_to(a, g3.shape).reshape(rows, n)

    gm = bc(gm1)
    e_q, e_k = jnp.exp(g - gm), jnp.exp(gm - g)
    qsig = _sigmoid(qpre)
    qs = qpre * qsig * (DH ** -0.5)
    return dict(sig=sig, f=f, k=k, qsig=qsig, qs=qs, e_q=e_q, e_k=e_k,
                e_mid=[jnp.exp(gm1[ci]) for ci in range(nch)], e_rest=[jnp.exp(gl1[ci] - gm1[ci]) for ci in range(nch)])


def _put_heads(ref, lead, arr):
    for h in range(HEADS):
        ref[lead + (h,)] = arr[:, h * DH:(h + 1) * DH]


def _get_heads(ref, lead=()):
    return jnp.concatenate([ref[lead + (h,)] for h in range(HEADS)], axis=1)


def _gla_prep(g_all, lb, cum01, s_wout, s_pgrp):
    nch = TM // CHUNK

    def body(g_ref, lb_ref, cum_ref, swout_r, spgrp_r, p0_ref, p1_ref, v_ref, dec_ref, wout_o, pgrp_o, ssem, rsem, lsem):
        finish = _gather_rider(pl.program_id(0), NT, NT - 1, ("rows", "grp"), (swout_r, spgrp_r), (wout_o, pgrp_o), ssem, rsem, lsem)
        qpre = g_ref[:, 3 * E:4 * E]
        _put_heads(v_ref, (), g_ref[:, 2 * E:3 * E].astype(BF16))
        for d, p_ref in ((0, p0_ref), (1, p1_ref)):
            t = _gla_gates(g_ref[:, d * E:(d + 1) * E], qpre, lb_ref[d:d + 1, :], cum_ref[d], d == 1)
            _put_heads(p_ref, (0,), (t["qs"] * t["e_q"]).astype(BF16))
            _put_heads(p_ref, (1,), (t["k"] * t["e_k"]).astype(BF16))
            for ci in range(nch):
                dec_ref[d, 0, ci:ci + 1, :] = t["e_mid"][ci]
                dec_ref[d, 0, nch + ci:nch + ci + 1, :] = t["e_rest"][ci]
        finish()

    quad = pl.BlockSpec((2, HEADS, TM, DH), lambda i: (0, 0, i, 0))
    return pl.pallas_call(
        body, name="gla_prep", grid=(NT,),
        in_specs=[pl.BlockSpec((TM, 4 * E), lambda i: (i, 0)), VMEM_SPEC, VMEM_SPEC, HBM_SPEC, HBM_SPEC],
        out_specs=[quad, quad, pl.BlockSpec((HEADS, TM, DH), lambda i: (0, i, 0)), pl.BlockSpec((2, 1, 2 * nch, E), lambda i: (0, i, 0, 0)),
                   HBM_SPEC, HBM_SPEC],
        out_shape=(_sds((2, HEADS, TT, DH), BF16), _sds((2, HEADS, TT, DH), BF16), _sds((HEADS, TT, DH), BF16), _sds((2, NT, 2 * nch, E), F32),
                   _sds((E, D), BF16), _sds((4, PG, PG), BF16)),
        scratch_shapes=_rider_sems(2),
        compiler_params=pltpu.CompilerParams(dimension_semantics=("arbitrary",), vmem_limit_bytes=VMEM_LIMIT),
    )(g_all, lb, cum01, s_wout, s_pgrp)


def _scan_tile(i, rev):
    t = jnp.where(i == 0, 0, NT - i) if rev else i
    return t, pl.ds(pl.multiple_of(t * TM, TM), TM)


def _chunk_rows(dec_ref, lanes, cis, where):
    nch = TM // CHUNK

    def rows(off):
        return jnp.stack([dec_ref[d, where[d][0], off + ci:off + ci + 1, hh * DH:(hh + 1) * DH] for (d, hh), ci in zip(lanes, cis)])

    return rows(0), rows(nch)


def _gla_fwd(p0, p1, v_all, dec, mask01, s_pwin, s_pwout):
    n_steps = HEADS // GLA_HB

    def body(p0_ref, p1_ref, v_ref, dec_ref, msk_ref, spwin_r, spwout_r, o_ref, pwin_o, pwout_o, ob_sc, ssem, rsem, lsem):
        finish = _gather_rider(pl.program_id(0), n_steps, n_steps, ("major", "rows"), (spwin_r, spwout_r), (pwin_o, pwout_o),
                               ssem, rsem, lsem)

        lanes = [(d, hh) for d in (0, 1) for hh in range(GLA_HB)]
        nch = TM // CHUNK

        def tile_body(i, st):
            where = [_scan_tile(i, d == 1) for d in (0, 1)]

            def stacked(fn):
                return jnp.stack([fn(d, hh, where[d][1]) for d, hh in lanes])

            qg, kg = [stacked(lambda d, hh, rows, ty=ty: (p1_ref if d else p0_ref)[ty, hh, rows, :]) for ty in range(2)]
            v = stacked(lambda d, hh, rows: v_ref[hh, rows, :])
            a = _bdot_nt(qg, kg) * jnp.stack([msk_ref[d] for d, _ in lanes])
            intra = _bdot(a, v)
            outs = [[None] * nch for _ in lanes]
            for n in range(nch):
                cis = [nch - 1 - n if d else n for d, _ in lanes]

                def chunk(arr):
                    return jnp.stack([arr[l, ci * CHUNK:(ci + 1) * CHUNK] for l, ci in enumerate(cis)])

                e_mid, e_rest = _chunk_rows(dec_ref, lanes, cis, where)
                inter = _bdot_nt(chunk(qg), st * e_mid)
                for l, ci in enumerate(cis):
                    outs[l][ci] = inter[l] + intra[l, ci * CHUNK:(ci + 1) * CHUNK]
                st = st * (e_mid * e_rest) + _bdot_tn(chunk(v), chunk(kg)) * e_rest
            for l, (d, hh) in enumerate(lanes):
                (ob_sc if d else o_ref)[hh, where[d][1], :] = jnp.concatenate(outs[l], axis=0)
            return st

        lax.fori_loop(0, NT, tile_body, jnp.zeros((len(lanes), DH, DH), F32))
        o_ref[...] += ob_sc[...]
        finish()

    quad = pl.BlockSpec((2, GLA_HB, TT, DH), lambda h: (0, h, 0, 0))
    head = pl.BlockSpec((GLA_HB, TT, DH), lambda h: (h, 0, 0))
    return pl.pallas_call(
        body, name="gla_fwd", grid=(n_steps,),
        in_specs=[quad, quad, head, pl.BlockSpec((2, NT, 8, GLA_HB * DH), lambda h: (0, 0, 0, h)),
                  pl.BlockSpec((2, TM, TM), lambda h: (0, 0, 0)), HBM_SPEC, HBM_SPEC],
        out_specs=[head, HBM_SPEC, HBM_SPEC],
        out_shape=(_sds((HEADS, TT, DH), F32), _sds((NDEV, D, SH_PWIN), BF16), _sds((E, D), BF16)),
        scratch_shapes=[pltpu.VMEM((GLA_HB, TT, DH), F32)] + _rider_sems(2),
        compiler_params=pltpu.CompilerParams(dimension_semantics=("arbitrary",), vmem_limit_bytes=VMEM_LIMIT),
    )(p0, p1, v_all, dec, mask01, s_pwin, s_pwout)


def _gated_norm(o, z, gw):
    r = _head_map(lambda oh: jnp.broadcast_to(_rstd(oh), oh.shape), o)
    on = o * r
    zs = _sigmoid(z)
    sz = z * zs
    return on * gw * sz, r, on, zs, sz


def _f3_out(o, g_all, x, gate, gw, wout):
    def body(o_ref, z_ref, x_ref, gate_ref, gw_ref, w_ref, x1_ref):
        og, _, _, _, _ = _gated_norm(_get_heads(o_ref), z_ref[...], gw_ref[...])
        x1_ref[...] = x_ref[...] + gate_ref[...] * _dot(og, w_ref[...])

    return pl.pallas_call(
        body, name="f3_out", grid=(NTX,),
        in_specs=[pl.BlockSpec((HEADS, TM, DH), lambda i: (0, i + 1, 0)), pl.BlockSpec((TM, E), lambda i: (i + 1, 4)),
                  pl.BlockSpec((TM, D), lambda i: (i, 0)), pl.BlockSpec((1, D), lambda i: (0, 0)),
                  pl.BlockSpec((1, E), lambda i: (0, 0)), pl.BlockSpec((E, D), lambda i: (0, 0))],
        out_specs=pl.BlockSpec((TM, D), lambda i: (i, 0)),
        out_shape=_sds((T, D), F32),
        compiler_params=pltpu.CompilerParams(dimension_semantics=("arbitrary",)),
    )(o, g_all, x, gate, gw, wout)


def _pool_layer(x1, tgt, mod1, nw1, fnw, pwin, pgrp, pscale, pwout, pb, pbt, pinv):
    def body(x_ref, t_ref, m_ref, nw_ref, fw_ref, pwin_ref, pgrp_ref, ps_ref, pwout_ref, pb_ref, pbt_ref, pinv_ref,
             dx_ref, gpwin_o, gpgrp_o, gpwout_o, dmod_o, gnw_o, gfw_o, gps_o, loss_o,
             a_pwin, a_pgrp, a_pwout):
        i = pl.program_id(0)

        @pl.when(i == 0)
        def _():
            for ref in (a_pwin, a_pgrp, a_pwout, dmod_o, gnw_o, gfw_o, gps_o, loss_o):
                ref[...] = jnp.zeros_like(ref)

        shift, scale, gate = m_ref[0:1, :], m_ref[1:2, :], m_ref[2:3, :]
        nw, fw, ps = nw_ref[...], fw_ref[...], ps_ref[...]
        x1 = x_ref[...]
        hx, r1, xn, a = _modulated(x1, nw, shift, scale)
        hxb = hx.astype(BF16)
        uz = jnp.concatenate([_dot(hxb, pwin_ref[j]) for j in range(NDEV)], axis=1)
        u, z = uz[:, :E], uz[:, E:]
        pooled, ys = [], []
        for g in range(4):
            ug = u[:, g * PG:(g + 1) * PG]
            pg = _dot01(pb_ref[g], ug) * pinv_ref[g] - ug
            pooled.append(pg.astype(BF16))
            ys.append(_dot(pooled[g], pgrp_ref[g]))
        ycat = jnp.concatenate(ys, axis=1)
        y = ycat * ps
        zs = _sigmoid(z)
        sz = z * zs
        p = (y * sz).astype(BF16)
        out = _dot(p, pwout_ref[...])
        x2 = x1 + gate * out
        r2 = _rstd(x2)
        xn2 = x2 * r2
        diff = xn2 * fw - t_ref[...]
        loss_o[...] += _colsum(diff * diff)
        dyf = diff * (1.0 / D)
        gfw_o[...] += _colsum(dyf * xn2)
        dxn2 = dyf * fw
        dx2 = r2 * (dxn2 - xn2 * jnp.mean(dxn2 * xn2, axis=-1, keepdims=True))
        dgate = _colsum(dx2 * out)
        dout = (dx2 * gate).astype(BF16)
        for j in range(4):
            cs = slice(j * PG, (j + 1) * PG)
            a_pwout[:, cs] += _dot_ta(p, dout[:, cs])
        dp = _dot_tb(dout, pwout_ref[...])
        dy = dp * sz
        dz = dp * y * (zs * (1.0 + z * (1.0 - zs)))
        gps_o[...] += _colsum(dy * ycat)
        dycat = dy * ps
        dus = []
        for g in range(4):
            dyg = dycat[:, g * PG:(g + 1) * PG].astype(BF16)
            a_pgrp[g] += _dot_ta(pooled[g], dyg)
            dpg = _dot_tb(dyg, pgrp_ref[g])
            dus.append(_dot01(pbt_ref[g], dpg * pinv_ref[g]) - dpg)
        duz = jnp.concatenate(dus + [dz], axis=1).astype(BF16)
        dhx = None
        for j in range(NDEV):
            dj = duz[:, j * SH_PWIN:(j + 1) * SH_PWIN]
            a_pwin[j] += _dot_ta(hxb, dj)
            part = _dot_tb(dj, pwin_ref[j])
            dhx = part if dhx is None else dhx + part
        dmod_o[0:1, :] += _colsum(dhx)
        dmod_o[1:2, :] += _colsum(dhx * a)
        dmod_o[2:3, :] += dgate
        da = dhx * (1.0 + scale)
        gnw_o[...] += _colsum(da * xn)
        dxn = da * nw
        dx_ref[...] = dx2 + r1 * (dxn - xn * jnp.mean(dxn * xn, axis=-1, keepdims=True))

        @pl.when(i == NTX - 1)
        def _():
            gpwin_o[...] = a_pwin[...].astype(BF16)
            gpgrp_o[...] = a_pgrp[...].astype(BF16)
            gpwout_o[...] = a_pwout[...].astype(BF16)

    tile = pl.BlockSpec((TM, D), lambda i: (i, 0))
    outs = (_sds((T, D), F32), _sds((NDEV, D, SH_PWIN), BF16), _sds((4, PG, PG), BF16), _sds((E, D), BF16),
            _sds((3, D), F32), _sds((1, D), F32), _sds((1, D), F32), _sds((1, E), F32), _sds((1, D), F32))
    return pl.pallas_call(
        body, name="pool_layer", grid=(NTX,),
        in_specs=[tile, tile] + [VMEM_SPEC] * 10,
        out_specs=[tile] + [VMEM_SPEC] * 8,
        out_shape=outs,
        scratch_shapes=[pltpu.VMEM((NDEV, D, SH_PWIN), F32), pltpu.VMEM((4, PG, PG), F32), pltpu.VMEM((E, D), F32)],
        compiler_params=pltpu.CompilerParams(dimension_semantics=("arbitrary",), vmem_limit_bytes=VMEM_LIMIT),
    )(x1, tgt, mod1, nw1, fnw, pwin, pgrp, pscale, pwout, pb, pbt, pinv)


def _b3_out_bwd(dx1, o, g_all, gate, gw, wout, gpwout):
    def body(dx_ref, o_ref, z_ref, gate_ref, gw_ref, w_ref, gpwout_r, do_ref, dz_ref, gw_o, dgate_o, ggw_o, rpwout_o,
             acc, *rider):
        i = pl.program_id(0)
        bufs, sems = _rider2_split(rider, 1)
        finish = _scatter_rider2(i, NT, 2, ("rows",), (gpwout_r,), (rpwout_o,), bufs, sems)

        @pl.when(i == 0)
        def _():
            acc[...] = jnp.zeros_like(acc)
            dgate_o[...] = jnp.zeros_like(dgate_o)
            ggw_o[...] = jnp.zeros_like(ggw_o)
            do_ref[...] = jnp.zeros_like(do_ref)
            dz_ref[...] = jnp.zeros_like(dz_ref)

        @pl.when(i > 0)
        def _():
            gw = gw_ref[...]
            z = z_ref[...]
            og, r, on, zs, sz = _gated_norm(_get_heads(o_ref), z, gw)
            ogb = og.astype(BF16)
            dx = dx_ref[...]
            dgate_o[...] += _colsum(dx * _dot(ogb, w_ref[...]))
            dy = (dx * gate_ref[...]).astype(BF16)
            for j in range(4):
                cs = slice(j * PG, (j + 1) * PG)
                acc[:, cs] += _dot_ta(ogb, dy[:, cs])
            dog = _dot_tb(dy, w_ref[...])
            dz_ref[...] = (dog * (on * gw) * (zs * (1.0 + z * (1.0 - zs)))).astype(BF16)
            dong = dog * sz
            ggw_o[...] += _colsum(dong * on)
            don = dong * gw
            do = _head_map(lambda dh, nh, rh: rh * (dh - nh * jnp.mean(dh * nh, axis=-1, keepdims=True)), don, on, r)
            _put_heads(do_ref, (), do.astype(BF16))

        @pl.when(i == NT - 1)
        def _():
            gw_o[...] = acc[...].astype(BF16)

        finish()

    prev = lambda i: (jnp.maximum(i - 1, 0), 0)
    heads = pl.BlockSpec((HEADS, TM, DH), lambda i: (0, i, 0))
    return pl.pallas_call(
        body, name="b3_out_bwd", grid=(NT,),
        in_specs=[pl.BlockSpec((TM, D), prev), heads, pl.BlockSpec((TM, E), lambda i: (i, 4)),
                  VMEM_SPEC, VMEM_SPEC, VMEM_SPEC, HBM_SPEC],
        out_specs=[heads, pl.BlockSpec((TM, E), lambda i: (i, 0)), VMEM_SPEC, VMEM_SPEC, VMEM_SPEC, HBM_SPEC],
        out_shape=(_sds((HEADS, TT, DH), BF16), _sds((TT, E), BF16), _sds((E, D), BF16), _sds((1, D), F32), _sds((1, E), F32),
                   _sds((RS_SLOTS, SH_ROWS, D), BF16)),
        scratch_shapes=[pltpu.VMEM((E, D), F32)] + _rider2_scratch([(SH_ROWS, D)]),
        compiler_params=pltpu.CompilerParams(dimension_semantics=("arbitrary",), vmem_limit_bytes=VMEM_LIMIT),
    )(dx1, o, g_all, gate, gw, wout, gpwout)


def _gla_bwd(p0, p1, v_all, dec, do, mask01, gpwin, gpgrp):
    nch = TM // CHUNK
    n_steps = HEADS // GLA_HB

    def body(p0_ref, p1_ref, v_ref, dec_ref, do_ref, msk_ref, gpwin_r, gpgrp_r, d0_ref, d1_ref, dv_ref, dgl_ref, rpwin_o, rpgrp_o,
             ss_sc, dv_sc, ssem, rsem, lsem, *rider):
        finish_grp = _scatter_rider(pl.program_id(0), n_steps, ("grp",), (gpgrp_r,), (rpgrp_o,), ssem, rsem, lsem)
        bufs, sems = _rider2_split(rider, 1)
        finish_win = _scatter_rider2(pl.program_id(0), n_steps, 1, ("major",), (gpwin_r,), (rpwin_o,), bufs, sems)

        lanes = [(d, hh) for d in (0, 1) for hh in range(GLA_HB)]
        zero = jnp.zeros((len(lanes), DH, DH), F32)
        dgl_ref[...] = jnp.zeros_like(dgl_ref)

        def p_of(d):
            return p1_ref if d else p0_ref

        def scan_step(i, n):
            where = [_scan_tile(i, d == 1) for d in (0, 1)]
            cis = [nch - 1 - n if d else n for d, _ in lanes]
            e_mid, e_rest = _chunk_rows(dec_ref, lanes, cis, where)

            def chunk(arr):
                return jnp.stack([arr[l, ci * CHUNK:(ci + 1) * CHUNK] for l, ci in enumerate(cis)])

            return where, cis, e_mid, e_rest, chunk

        def stacked(i, fn):
            where = [_scan_tile(i, d == 1) for d in (0, 1)]
            return jnp.stack([fn(d, hh, where[d][1]) for d, hh in lanes])

        def fwd_body(i, st):
            v = stacked(i, lambda d, hh, rows: v_ref[hh, rows, :])
            kg = stacked(i, lambda d, hh, rows: p_of(d)[1, hh, rows, :])
            for n in range(nch):
                _, _, e_mid, e_rest, chunk = scan_step(i, n)
                ss_sc[i * nch + n] = st
                st = st * (e_mid * e_rest) + _bdot_tn(chunk(v), chunk(kg)) * e_rest
            return st

        ss_sc[NT * nch] = lax.fori_loop(0, NT, fwd_body, zero)

        def bwd_body(ii, dst):
            i = NT - 1 - ii
            qg, kg = [stacked(i, lambda d, hh, rows, ty=ty: p_of(d)[ty, hh, rows, :]) for ty in range(2)]
            v = stacked(i, lambda d, hh, rows: v_ref[hh, rows, :])
            dob = stacked(i, lambda d, hh, rows: do_ref[hh, rows, :])
            msk = jnp.stack([msk_ref[d] for d, _ in lanes])
            a = (_bdot_nt(qg, kg) * msk).astype(BF16)
            da = (_bdot_nt(dob, v) * msk).astype(BF16)
            dqg = _bdot(da, kg)
            dkg = _bdot_tn(da, qg)
            dv_intra = _bdot_tn(a, dob)
            dv_l, dkg_l, dqg_l = ([[None] * nch for _ in lanes] for _ in range(3))
            for n in range(nch - 1, -1, -1):
                where, cis, e_mid, e_rest, chunk = scan_step(i, n)
                s_c, s_end = ss_sc[i * nch + n], ss_sc[i * nch + n + 1]
                dste = (dst * e_rest).astype(BF16)
                kg_c, v_c, dob_c = chunk(kg), chunk(v), chunk(dob)
                dv_c = chunk(dv_intra) + _bdot_nt(kg_c, dste)
                dkg_c = chunk(dkg) + _bdot(v_c, dste)
                dqg_c = chunk(dqg) + _bdot(dob_c, s_c * e_mid)
                dgl = jnp.sum(s_end * dst, axis=1, keepdims=True)
                for l, ((d, hh), ci) in enumerate(zip(lanes, cis)):
                    dv_l[l][ci], dkg_l[l][ci], dqg_l[l][ci] = dv_c[l], dkg_c[l], dqg_c[l]
                    dgl_ref[d, where[d][0], ci:ci + 1, hh * DH:(hh + 1) * DH] = dgl[l]
                dst = dst * (e_mid * e_rest) + _bdot_tn(dob_c, chunk(qg)) * e_mid
            where = [_scan_tile(i, d == 1) for d in (0, 1)]
            for l, (d, hh) in enumerate(lanes):
                rows = where[d][1]
                d_ref = d1_ref if d else d0_ref
                d_ref[0, hh, rows, :] = jnp.concatenate(dqg_l[l], axis=0).astype(BF16)
                d_ref[1, hh, rows, :] = jnp.concatenate(dkg_l[l], axis=0).astype(BF16)
                dv_sc[d, hh, rows, :] = jnp.concatenate(dv_l[l], axis=0).astype(BF16)
            return dst

        lax.fori_loop(0, NT, bwd_body, zero)
        dv_ref[...] = (dv_sc[0].astype(F32) + dv_sc[1].astype(F32)).astype(BF16)
        finish_grp()
        finish_win()

    quad = pl.BlockSpec((2, GLA_HB, TT, DH), lambda h: (0, h, 0, 0))
    col = pl.BlockSpec((GLA_HB, TT, DH), lambda h: (h, 0, 0))
    chunkv = pl.BlockSpec((2, NT, 8, GLA_HB * DH), lambda h: (0, 0, 0, h))
    outs = (_sds((2, HEADS, TT, DH), BF16), _sds((2, HEADS, TT, DH), BF16), _sds((HEADS, TT, DH), BF16), _sds((2, NT, 8, E), F32),
            _sds((RS_SLOTS, D, SH_PWIN), BF16), _sds((NDEV, 4, SH_GRP, PG), BF16))
    return pl.pallas_call(
        body, name="gla_bwd", grid=(n_steps,),
        in_specs=[quad, quad, col, chunkv, col, pl.BlockSpec((2, TM, TM), lambda h: (0, 0, 0)), HBM_SPEC, HBM_SPEC],
        out_specs=[quad, quad, col, chunkv, HBM_SPEC, HBM_SPEC],
        out_shape=outs,
        scratch_shapes=[pltpu.VMEM((NT * nch + 1, 2 * GLA_HB, DH, DH), F32), pltpu.VMEM((2, GLA_HB, TT, DH), BF16)] + _rider_sems(1)
        + _rider2_scratch([(D, SH_PWIN)]),
        compiler_params=pltpu.CompilerParams(dimension_semantics=("arbitrary",), vmem_limit_bytes=VMEM_LIMIT),
    )(p0, p1, v_all, dec, do, mask01, gpwin, gpgrp)


TMB = 128


def _gla_post_bwd(g_all, d0, d1, dgl, dv, dz, lb, cum01, gwout):
    nch = TMB // CHUNK

    def body(g_ref, d0_ref, d1_ref, dgl_ref, dv_ref, dz_ref, lb_ref, cum_ref, gwout_r, dg_ref, dlb_ref, rwout_o, *rider):
        i = pl.program_id(0)
        bufs, sems = _rider2_split(rider, 1)
        finish = _scatter_rider2(i, TT // TMB, 2, ("rows",), (gwout_r,), (rwout_o,), bufs, sems)

        @pl.when(i == 0)
        def _():
            dlb_ref[...] = jnp.zeros_like(dlb_ref)

        half = i & 1
        qpre = g_ref[:, 3 * E:4 * E]
        dqs_sum = None
        dpre = []
        for d, d_ref in ((0, d0_ref), (1, d1_ref)):
            rev = d == 1
            lbd = lb_ref[d:d + 1, :]
            t = _gla_gates(g_ref[:, d * E:(d + 1) * E], qpre, lbd, cum_ref[d, :TMB, :TMB], rev)
            dqs = _get_heads(d_ref, (0,)).astype(F32) * t["e_q"]
            dk = _get_heads(d_ref, (1,)).astype(F32) * t["e_k"]
            dg = t["qs"] * dqs - t["k"] * dk
            dgl8 = dgl_ref[d, 0]
            dgl_rows = [jnp.where(half == 0, dgl8[ci:ci + 1, :], dgl8[nch + ci:nch + ci + 1, :]) for ci in range(nch)]
            dgl_b = jnp.concatenate([jnp.broadcast_to(dgl_rows[ci], (CHUNK, E)) for ci in range(nch)], axis=0)
            pos = lax.broadcasted_iota(jnp.int32, (TMB, E), 0) & (CHUNK - 1)
            dg = dg + jnp.where(pos == (0 if rev else CHUNK - 1), dgl_b, 0.0)
            dlf = _dot01(cum_ref[1 - d, :TMB, :TMB], dg)
            df = dlf / t["f"] - dk
            sig = t["sig"]
            dpre.append((df * (1.0 - lbd) * sig * (1.0 - sig)).astype(BF16))
            dlb_ref[d:d + 1, :] += _colsum(df * (1.0 - sig))
            dqs_sum = dqs if dqs_sum is None else dqs_sum + dqs
            qsig = t["qsig"]
        dqpre = dqs_sum * (DH ** -0.5) * (qsig * (1.0 + qpre * (1.0 - qsig)))
        dg_ref[...] = jnp.concatenate([dpre[0], dpre[1], _get_heads(dv_ref), dqpre.astype(BF16), dz_ref[...]], axis=1)
        finish()

    quad = pl.BlockSpec((2, HEADS, TMB, DH), lambda i: (0, 0, i, 0))
    tile = pl.BlockSpec((TMB, E), lambda i: (i, 0))
    return pl.pallas_call(
        body, name="gla_post_bwd", grid=(TT // TMB,),
        in_specs=[pl.BlockSpec((TMB, 4 * E), lambda i: (i, 0)), quad, quad,
                  pl.BlockSpec((2, 1, 8, E), lambda i: (0, i // 2, 0, 0)), pl.BlockSpec((HEADS, TMB, DH), lambda i: (0, i, 0)), tile,
                  VMEM_SPEC, VMEM_SPEC, HBM_SPEC],
        out_specs=[pl.BlockSpec((TMB, WIN_COLS), lambda i: (i, 0)), VMEM_SPEC, HBM_SPEC],
        out_shape=(_sds((TT, WIN_COLS), BF16), _sds((2, E), F32), _sds((RS_SLOTS, SH_ROWS, D), BF16)),
        scratch_shapes=_rider2_scratch([(SH_ROWS, D)]),
        compiler_params=pltpu.CompilerParams(dimension_semantics=("arbitrary",), vmem_limit_bytes=VMEM_LIMIT),
    )(g_all, d0, d1, dgl, dv, dz, lb, cum01, gwout)


WIN_SLOTS = 4


def _scatter_order(s, core):
    return (NDEV - 1 - s) ^ jnp.where((s >= 2) & (s <= 5) & ((s & 1) == core), 6, 0)


def _b1_in_bwd(idx1, ctx, x, dx1, dg, nw, msel, win):
    last_s = NDEV - 1
    half = D // 2

    def body(idx_ref, ctx_ref, x_ref, dx1_ref, dg_ref, nw_ref, m_ref, w_ref, gx_ref, rwin_o, dmx_o, dmc_o, gnw_o,
             hx_sc, dhx_sc, acc, sbuf, pbuf, rbuf, psend, precv, isend, irecv, dsend, drecv, sibsem, lsem):
        del idx_ref
        s, i = pl.program_id(0), pl.program_id(1)
        x, y, cc, idx = _mesh_pos()
        shift, scale = m_ref[0, 0:1, :], m_ref[0, 1:2, :]
        sibling = (x, y, 1 - cc)

        def partial(p):
            return pltpu.make_async_remote_copy(src_ref=sbuf.at[0], dst_ref=pbuf.at[p], send_sem=psend.at[p], recv_sem=precv.at[p],
                                                device_id=sibling, device_id_type=MESH)

        def chip_sum(p):
            return pltpu.make_async_remote_copy(src_ref=sbuf.at[1], dst_ref=rwin_o.at[2 + p], send_sem=isend.at[p], recv_sem=irecv.at[p],
                                                device_id=_peer(x, y, cc, 2 * (p + 1)), device_id_type=MESH)

        def relay(h):
            return pltpu.make_async_remote_copy(src_ref=sbuf.at[1, pl.ds(h * half, half), :], dst_ref=rbuf.at[h], send_sem=dsend.at[h],
                                                recv_sem=drecv.at[h], device_id=_peer(x, y, cc, 2 * (h + 1)), device_id_type=MESH)

        to_sibling = pltpu.make_async_remote_copy(src_ref=sbuf.at[0], dst_ref=rwin_o.at[1], send_sem=sibsem.at[0], recv_sem=sibsem.at[1],
                                                  device_id=sibling, device_id_type=MESH)
        own = pltpu.make_async_copy(sbuf.at[1], rwin_o.at[0], lsem)

        @pl.when((s == 0) & (i == 0))
        def _():
            for ref in (dmx_o, dmc_o, gnw_o):
                ref[...] = jnp.zeros_like(ref)

        @pl.when(s == 0)
        def _():
            hx, _, _, _ = _modulated(_ctx_or_x(i, ctx_ref, x_ref), nw_ref[...], shift, scale)
            hx_sc[i] = hx.astype(BF16)

        @pl.when(i == 0)
        def _():
            acc[...] = jnp.zeros_like(acc)

        dgb = dg_ref[...]
        hxb = hx_sc[i]
        for lo, hi in ((0, 256), (256, 512), (512, SH_WIN)):
            acc[:, lo:hi] += _dot_ta(hxb, dgb[:, lo:hi])
        part = _dot_tb(dgb, w_ref[0])

        @pl.when(s == 0)
        def _():
            dhx_sc[i] = part

        @pl.when(s > 0)
        def _():
            dhx_sc[i] += part

        done = i == NT - 1

        def hand_over(p, before):
            before.wait_send()
            sbuf[0] = acc[...].astype(BF16)
            partial(p).start()

        def send_chip_sum(p, before):
            for cp in before:
                cp.wait_send()
            partial(p).wait_recv()
            sbuf[1] = (acc[...] + pbuf[p].astype(F32)).astype(BF16)
            h = 1 - p
            rows = pl.ds(h * half, half)
            relay(h).wait_recv()
            sbuf[1, rows, :] = (acc[rows, :] + pbuf[p, rows, :].astype(F32) + rbuf[h].astype(F32)).astype(BF16)
            chip_sum(p).start()

        @pl.when(done & (s == 0))
        def _():
            sbuf[0] = acc[...].astype(BF16)
            partial(2).start()

        @pl.when(done & (s == 1))
        def _():
            partial(2).wait_recv()
            sbuf[1] = (acc[...] + pbuf[2].astype(F32)).astype(BF16)
            for h in range(2):
                relay(h).start()

        for core in range(2):
            @pl.when(done & (cc == core) & (s == 2))
            def _(core=core):
                hand_over(core, partial(2))

            @pl.when(done & (cc == core) & (s == 3))
            def _(core=core):
                send_chip_sum(1 - core, [relay(0), relay(1)])

            @pl.when(done & (cc == core) & (s == 4))
            def _(core=core):
                hand_over(1 - core, partial(core))

            @pl.when(done & (cc == core) & (s == 5))
            def _(core=core):
                send_chip_sum(core, [chip_sum(1 - core)])

            @pl.when(done & (cc == core) & (s == last_s - 1))
            def _(core=core):
                partial(1 - core).wait_send()
                sbuf[0] = acc[...].astype(BF16)
                to_sibling.start()

            @pl.when(done & (cc == core) & (s == last_s))
            def _(core=core):
                chip_sum(core).wait_send()
                sbuf[1] = acc[...].astype(BF16)
                own.start()

        @pl.when(s == last_s)
        def _():
            nw = nw_ref[...]
            _, r, xn, a = _modulated(_ctx_or_x(i, ctx_ref, x_ref), nw, shift, scale)
            dhx = dhx_sc[i]
            dsh, dsc = _colsum(dhx), _colsum(dhx * a)
            da = dhx * (1.0 + scale)
            gnw_o[...] += _colsum(da * xn)
            dxn = da * nw
            gx_ref[...] = dx1_ref[...] + r * (dxn - xn * jnp.mean(dxn * xn, axis=-1, keepdims=True))

            @pl.when(i == 0)
            def _():
                dmc_o[0:1, :] += dsh
                dmc_o[1:2, :] += dsc

            @pl.when(i > 0)
            def _():
                dmx_o[0:1, :] += dsh
                dmx_o[1:2, :] += dsc

        @pl.when((i == NT - 1) & (s == last_s))
        def _():
            to_sibling.wait_send()
            to_sibling.wait_recv()
            for p in range(2):
                chip_sum(p).wait_recv()
            own.wait()

    grid_spec = pltpu.PrefetchScalarGridSpec(
        num_scalar_prefetch=1, grid=(NDEV, NT),
        in_specs=[VMEM_SPEC,
                  pl.BlockSpec((TM, D), lambda s, i, ix: (jnp.where((s == 0) | (s == last_s), jnp.maximum(i - 1, 0), NTX - 1), 0)),
                  pl.BlockSpec((TM, D), lambda s, i, ix: (jnp.where(s == last_s, jnp.maximum(i - 1, 0), 0), 0)),
                  pl.BlockSpec((TM, SH_WIN), lambda s, i, ix: (i, ix[0] ^ _scatter_order(s, ix[0] & 1))), VMEM_SPEC,
                  pl.BlockSpec((1, 2, D), lambda s, i, ix: (jnp.minimum(i, 1), 0, 0)),
                  pl.BlockSpec((1, D, SH_WIN), lambda s, i, ix: (ix[0] ^ _scatter_order(s, ix[0] & 1), 0, 0))],
        out_specs=[pl.BlockSpec((TM, D), lambda s, i, ix: (jnp.where(s == last_s, jnp.maximum(i - 1, 0), 0), 0)),
                   HBM_SPEC, VMEM_SPEC, VMEM_SPEC, VMEM_SPEC],
        scratch_shapes=[pltpu.VMEM((NT, TM, D), BF16), pltpu.VMEM((NT, TM, D), F32), pltpu.VMEM((D, SH_WIN), F32),
                        pltpu.VMEM((2, D, SH_WIN), BF16), pltpu.VMEM((3, D, SH_WIN), BF16), pltpu.VMEM((2, half, SH_WIN), BF16),
                        pltpu.SemaphoreType.DMA((3,)), pltpu.SemaphoreType.DMA((3,)), pltpu.SemaphoreType.DMA((2,)),
                        pltpu.SemaphoreType.DMA((2,)), pltpu.SemaphoreType.DMA((2,)), pltpu.SemaphoreType.DMA((2,)),
                        pltpu.SemaphoreType.DMA((2,)), pltpu.SemaphoreType.DMA])
    return pl.pallas_call(
        body, name="b1_in_bwd", grid_spec=grid_spec,
        out_shape=(_sds((T, D), F32), _sds((WIN_SLOTS, D, SH_WIN), BF16), _sds((2, D), F32), _sds((2, D), F32), _sds((1, D), F32)),
        compiler_params=pltpu.CompilerParams(dimension_semantics=("arbitrary", "arbitrary"), vmem_limit_bytes=VMEM_LIMIT),
    )(idx1, ctx, x, dx1, dg, nw, msel, win)


def _reduce_small(pd, pv, cg, c_ctx, ada_w0):
    n_arr = 3

    def body(pd_r, pv_r, cg_r, cctx_r, ada_r, gada_o, gadab_o, gcctx_o, pvsum_o, loss_o,
             pd_all, pv_all, dsc_all, dsc_mine, ssem, rsem):
        x, y, cc, idx = _mesh_pos()
        srcs = [pd_r, pv_r, dsc_mine]
        dsts = [pd_all.at[idx], pv_all.at[idx], dsc_all.at[idx]]

        def remote(a, k):
            return pltpu.make_async_remote_copy(src_ref=srcs[a], dst_ref=dsts[a], send_sem=ssem.at[a, k], recv_sem=rsem.at[a, k],
                                                device_id=_peer(x, y, cc, k), device_id_type=MESH)

        first = [remote(a, k) for k in range(1, NDEV) for a in (0, 1)]
        for cp in first:
            cp.start()
        pd_all[idx] = pd_r[...]
        pv_all[idx] = pv_r[...]
        for k in range(1, NDEV):
            remote(0, k).wait_recv()
            remote(1, k).wait_recv()
        mine = [pd_all[s, :, pl.ds(idx, 1), :] for s in range(NDEV)]
        dmc = functools.reduce(lambda u, v: u + v, [m[2] for m in mine])
        rows = _stack_rows([cg_r[i] for i in range(NDEV)] + [cctx_r[...]])
        sc = (rows * _sigmoid(rows)).astype(BF16)
        gada_o[0] = _dot_ta(sc, _stack_rows([m[0] for m in mine] + [dmc]))
        gada_o[1] = _dot_ta(sc, _stack_rows([m[1] for m in mine]))
        dsc_mine[...] = _dot_tb(jnp.broadcast_to(dmc, (8, SH_ADA)), ada_r[...])[0:1, :]
        dsc_all[idx] = dsc_mine[...]
        second = [remote(2, k) for k in range(1, NDEV)]
        for cp in second:
            cp.start()
        tot = [functools.reduce(lambda u, v: u + v, [pd_all[s, l] for s in range(NDEV)]) for l in range(3)]
        gadab_o[0] = tot[0] + tot[2]
        gadab_o[1] = tot[1]
        pvs = functools.reduce(lambda u, v: u + v, [pv_all[s] for s in range(NDEV)])
        pvsum_o[...] = pvs
        loss_o[...] = jnp.broadcast_to(jnp.sum(pvs[:, PV_LOSS:PV_LOSS + D], axis=-1, keepdims=True) * (0.5 / D), (1, 128))
        for k in range(1, NDEV):
            remote(2, k).wait_recv()
        dsc = functools.reduce(lambda u, v: u + v, [dsc_all[s] for s in range(NDEV)])
        cx = cctx_r[...]
        sx = _sigmoid(cx)
        gcctx_o[...] = dsc * (sx * (1.0 + cx * (1.0 - sx)))
        for cp in first + second:
            cp.wait_send()

    outs = (_sds((2, D, SH_ADA), F32), _sds((2, NDEV, SH_ADA), F32), _sds((1, D), F32), _sds((1, PV_LEN), F32), _sds((1, 128), F32))
    return pl.pallas_call(
        body, name="reduce_small", out_shape=outs,
        in_specs=[VMEM_SPEC] * 5, out_specs=[VMEM_SPEC] * 5,
        scratch_shapes=[
            pltpu.VMEM((NDEV, 3, NDEV, SH_ADA), F32), pltpu.VMEM((NDEV, 1, PV_LEN), F32), pltpu.VMEM((NDEV, 1, D), F32),
            pltpu.VMEM((1, D), F32),
            pltpu.SemaphoreType.DMA((n_arr, NDEV)), pltpu.SemaphoreType.DMA((n_arr, NDEV)),
        ],
        compiler_params=pltpu.CompilerParams(vmem_limit_bytes=VMEM_LIMIT),
    )(pd, pv, cg, c_ctx, ada_w0)


PV_NW, PV_GNORM, PV_FINAL, PV_LB, PV_PSCALE, PV_LOSS, PV_LEN = 0, 2 * D, 3 * D, 4 * D, 6 * D, 7 * D, 8 * D


def _adamw(w, g, m, v):
    m = ADAM_B1 * m + (1.0 - ADAM_B1) * g
    v = ADAM_B2 * v + (1.0 - ADAM_B2) * (g * g)
    m_hat = m / (1.0 - ADAM_B1 ** ADAM_STEP)
    v_hat = v / (1.0 - ADAM_B2 ** ADAM_STEP)
    delta = -ADAM_LR * (m_hat / (jnp.sqrt(v_hat) + ADAM_EPS) + ADAM_WD * w)
    return delta, m, v


ADAM_STEPS = 8


def _adam_all(sharded, dense, small, lb_idx, lbv):
    ns, nd, nsm = len(sharded), len(dense), len(small)

    def body(*refs):
        it = iter(refs)
        sh_in = [[next(it) for _ in range(4)] for _ in range(ns)]
        de_in = [[next(it) for _ in range(4)] for _ in range(nd)]
        sm_in = [[next(it) for _ in range(4)] for _ in range(nsm)]
        lb_r = next(it)
        sh_out = [[next(it) for _ in range(4)] for _ in range(ns)]
        de_out = [[next(it) for _ in range(3)] for _ in range(nd)]
        sm_out = [[next(it) for _ in range(4)] for _ in range(nsm)]
        for (p, w, m, v), outs in zip(sh_in, sh_out):
            g = p[0].astype(F32)
            for s in range(1, p.shape[0]):
                g = g + p[s].astype(F32)
            d, mn, vn = _adamw(w[...], g, m[...], v[...])
            outs[0][...], outs[1][...], outs[2][...], outs[3][...] = g, d, mn, vn
        for (g, w, m, v), outs in zip(de_in, de_out):
            d, mn, vn = _adamw(w[...], g[...], m[...], v[...])
            outs[0][...], outs[1][...], outs[2][...] = d, mn, vn

        @pl.when(pl.program_id(0) == 0)
        def _():
            for j, ((g, w, m, v), outs) in enumerate(zip(sm_in, sm_out)):
                gj = g[...]
                if j == lb_idx:
                    gj = gj * lb_r[...] * (1.0 - lb_r[...])
                d, mn, vn = _adamw(w[...], gj, m[...], v[...])
                outs[0][...], outs[1][...], outs[2][...], outs[3][...] = gj, d, mn, vn

    def tile(a):
        return pl.BlockSpec((a.shape[0] // ADAM_STEPS, a.shape[1]), lambda i: (i, 0))

    in_specs, out_specs, out_shape, args = [], [], [], []
    for p, w, m, v in sharded:
        in_specs += [pl.BlockSpec((p.shape[0], p.shape[1] // ADAM_STEPS, p.shape[2]), lambda i: (0, i, 0))] + [tile(w)] * 3
        args += [p, w, m, v]
    for g, w, m, v in dense:
        in_specs += [tile(w)] * 4
        args += [g, w, m, v]
    for g, w, m, v in small:
        in_specs += [VMEM_SPEC] * 4
        args += [g, w, m, v]
    in_specs.append(VMEM_SPEC)
    args.append(lbv)
    for _, w, _, _ in sharded:
        out_specs += [tile(w)] * 4
        out_shape += [_sds(w.shape, F32)] * 4
    for _, w, _, _ in dense:
        out_specs += [tile(w)] * 3
        out_shape += [_sds(w.shape, F32)] * 3
    for _, w, _, _ in small:
        out_specs += [VMEM_SPEC] * 4
        out_shape += [_sds(w.shape, F32)] * 4
    res = pl.pallas_call(body, name="adam_all", grid=(ADAM_STEPS,), in_specs=in_specs, out_specs=out_specs, out_shape=tuple(out_shape),
                         compiler_params=pltpu.CompilerParams(dimension_semantics=("arbitrary",), vmem_limit_bytes=VMEM_LIMIT))(*args)
    it = iter(res)
    return ([tuple(next(it) for _ in range(4)) for _ in range(ns)], [tuple(next(it) for _ in range(3)) for _ in range(nd)],
            [tuple(next(it) for _ in range(4)) for _ in range(nsm)])


def kernel(x, c, ctx, c_ctx, ada_w, ada_b, norm_w, hgrn_w_in, hgrn_lb_logits, hgrn_gnorm_w, hgrn_w_out, pool_w_in, pool_w_grp, pool_scale, pool_w_out, final_norm_w, loss_target, m_c_ctx, m_ada_w, m_ada_b, m_norm_w, m_hgrn_w_in, m_hgrn_lb_logits, m_hgrn_gnorm_w, m_hgrn_w_out, m_pool_w_in, m_pool_w_grp, m_pool_scale, m_pool_w_out, m_final_norm_w, v_c_ctx, v_ada_w, v_ada_b, v_norm_w, v_hgrn_w_in, v_hgrn_lb_logits, v_hgrn_gnorm_w, v_hgrn_w_out, v_pool_w_in, v_pool_w_grp, v_pool_scale, v_pool_w_out, v_final_norm_w):
    idx = 4 * lax.axis_index("x") + 2 * lax.axis_index("y") + lax.axis_index("c")
    cctx2 = c_ctx.reshape(1, D)
    cum01, mask01 = _gla_consts()
    pb, pbt, pinv = _pool_consts()

    idx1 = idx.reshape(1).astype(jnp.int32)
    nw0, nw1 = norm_w[0:1], norm_w[1:2]
    fnw = final_norm_w.reshape(1, D)
    g_all, win, s_wout, s_pwin, s_pgrp, s_pwout, lbl_g, ps_g, cg, mod0, mod1, modc = _f1_gather_matmul(
        idx1, ctx[0], x[0], nw0, hgrn_w_in[0], hgrn_w_out[0], pool_w_in[0], pool_w_grp[0], pool_w_out[0], hgrn_lb_logits[0],
        pool_scale, c, cctx2, ada_w, ada_b)
    lb = jax.nn.sigmoid(jnp.transpose(lbl_g, (1, 0, 2)).reshape(2, E))
    pscale = ps_g.reshape(1, E)
    msel = jnp.stack([modc[:2], mod0[:2]])
    p0, p1, v_all, dec, wout, pgrp = _gla_prep(g_all, lb, cum01, s_wout, s_pgrp)
    o, pwin, pwout = _gla_fwd(p0, p1, v_all, dec, mask01, s_pwin, s_pwout)
    x1 = _f3_out(o, g_all, x[0], mod0[2:3], hgrn_gnorm_w, wout)
    dx1, gpwin, gpgrp, gpwout, dmod1, gnw1, gfw, gps, lossv = _pool_layer(
        x1, loss_target[0], mod1, nw1, fnw, pwin, pgrp, pscale, pwout, pb, pbt, pinv)
    do, dz, gwout, dgate0, ggw, rpwout = _b3_out_bwd(dx1, o, g_all, mod0[2:3], hgrn_gnorm_w, wout, gpwout)
    d0, d1, dv, dgl, rpwin, rpgrp = _gla_bwd(p0, p1, v_all, dec, do, mask01, gpwin, gpgrp)
    dg, dlb, rwout = _gla_post_bwd(g_all, d0, d1, dgl, dv, dz, lb, cum01, gwout)
    grad_x, rwin, dmx, dmc, gnw0 = _b1_in_bwd(idx1, ctx[0], x[0], dx1, dg, nw0, msel, win)

    dmod0 = jnp.concatenate([dmx, dgate0], axis=0)
    dmodc = jnp.concatenate([dmc, jnp.zeros((1, D), F32)], axis=0)
    pd = jnp.stack([dmod0, dmod1, dmodc]).reshape(3, NDEV, SH_ADA)
    pv = jnp.concatenate([gnw0, gnw1, ggw, gfw, dlb.reshape(1, 2 * E), gps, lossv], axis=1)
    g_ada, g_adab, g_cctx, pvsum, loss128 = _reduce_small(pd, pv, cg, cctx2, ada_w[0])

    g2 = (4 * SH_GRP, PG)
    sharded_names = ["hgrn_w_in", "hgrn_w_out", "pool_w_in", "pool_w_grp", "pool_w_out"]
    sharded = [(rwin, hgrn_w_in[0], m_hgrn_w_in[0], v_hgrn_w_in[0]),
               (rwout, hgrn_w_out[0], m_hgrn_w_out[0], v_hgrn_w_out[0]),
               (rpwin, pool_w_in[0], m_pool_w_in[0], v_pool_w_in[0]),
               (rpgrp.reshape((NDEV,) + g2), pool_w_grp[0].reshape(g2), m_pool_w_grp[0].reshape(g2), v_pool_w_grp[0].reshape(g2)),
               (rpwout, pool_w_out[0], m_pool_w_out[0], v_pool_w_out[0])]
    a2 = (2 * D, SH_ADA)
    g_ada2 = g_ada.reshape(a2)
    dense = [(g_ada2, ada_w.reshape(a2), m_ada_w.reshape(a2), v_ada_w.reshape(a2))]
    lb_me = lax.dynamic_slice_in_dim(lb, idx * DH, DH, axis=1)
    small_names = ["c_ctx", "ada_b", "norm_w", "hgrn_lb_logits", "hgrn_gnorm_w", "pool_scale", "final_norm_w"]
    small = [(g_cctx, cctx2, m_c_ctx.reshape(1, D), v_c_ctx.reshape(1, D)),
             (g_adab.reshape(2, 3 * D), ada_b, m_ada_b, v_ada_b),
             (pvsum[:, PV_NW:PV_NW + 2 * D].reshape(2, D), norm_w, m_norm_w, v_norm_w),
             (lax.dynamic_slice_in_dim(pvsum[:, PV_LB:PV_LB + 2 * E].reshape(2, E), idx * DH, DH, axis=1),
              hgrn_lb_logits[0], m_hgrn_lb_logits[0], v_hgrn_lb_logits[0]),
             (pvsum[:, PV_GNORM:PV_GNORM + E], hgrn_gnorm_w, m_hgrn_gnorm_w, v_hgrn_gnorm_w),
             (lax.dynamic_slice_in_dim(pvsum[:, PV_PSCALE:PV_PSCALE + E], idx * DH, DH, axis=1), pool_scale, m_pool_scale, v_pool_scale),
             (pvsum[:, PV_FINAL:PV_FINAL + D], fnw, m_final_norm_w.reshape(1, D), v_final_norm_w.reshape(1, D))]
    r_sharded, r_dense, r_small = _adam_all(sharded, dense, small, 3, lb_me)
    out = dict(zip(sharded_names, r_sharded))
    out["ada_w"] = (g_ada2,) + r_dense[0]
    out.update(zip(small_names, r_small))

    shapes = {"c_ctx": (D,), "ada_w": (2, D, SH_ADA), "ada_b": (2, 3 * D), "norm_w": (2, D), "hgrn_w_in": (1, D, SH_WIN),
              "hgrn_lb_logits": (1, 2, DH), "hgrn_gnorm_w": (1, E), "hgrn_w_out": (1, SH_ROWS, D), "pool_w_in": (1, D, SH_PWIN),
              "pool_w_grp": (1, 4, SH_GRP, PG), "pool_scale": (1, DH), "pool_w_out": (1, SH_ROWS, D), "final_norm_w": (D,)}
    order = ["c_ctx", "ada_w", "ada_b", "norm_w", "hgrn_w_in", "hgrn_lb_logits", "hgrn_gnorm_w", "hgrn_w_out", "pool_w_in",
             "pool_w_grp", "pool_scale", "pool_w_out", "final_norm_w"]
    flat = [out[name][q].reshape(shapes[name]) for q in range(4) for name in order]
    return (loss128[0, 0], grad_x[None], *flat)
```

```python
import functools

import numpy as np
import jax
import jax.numpy as jnp
from jax import lax
from jax.experimental import pallas as pl
from jax.experimental.pallas import tpu as pltpu

F32 = jnp.float32
BF16 = jnp.bfloat16

D = 1024
E = 1024
HEADS = 8
DH = 128
CHUNK = 64
T = 2048
TC = 256
TT = T + TC
TM = 256
NT = TT // TM
NTX = T // TM
NDEV = 8
GRID_W = 64
POOL_WINDOWS = (2, 4, 8, 16)
PG = 256
EPS = 1e-6
WIN_COLS = 5 * E
SH_WIN = WIN_COLS // NDEV
SH_PWIN = 2 * E // NDEV
SH_ROWS = E // NDEV
SH_GRP = PG // NDEV
SH_ADA = 3 * D // NDEV
VMEM_LIMIT = 56 * 1024 * 1024

ADAM_LR, ADAM_B1, ADAM_B2, ADAM_EPS, ADAM_WD, ADAM_STEP = 0.001, 0.9, 0.999, 1e-08, 0.01, 10

MESH = pl.DeviceIdType.MESH
VMEM_SPEC = pl.BlockSpec(memory_space=pltpu.VMEM)
HBM_SPEC = pl.BlockSpec(memory_space=pltpu.HBM)


def _sds(shape, dtype):
    return jax.ShapeDtypeStruct(shape, dtype)


def _bf(a):
    return a if a.dtype == BF16 else a.astype(BF16)


def _dot(a, b):
    return lax.dot_general(_bf(a), _bf(b), (((1,), (0,)), ((), ())), preferred_element_type=F32)


def _dot_tb(a, b):
    return lax.dot_general(_bf(a), _bf(b), (((1,), (1,)), ((), ())), preferred_element_type=F32)


def _dot_ta(a, b):
    return lax.dot_general(_bf(a), _bf(b), (((0,), (0,)), ((), ())), preferred_element_type=F32)


def _bdot(a, b):
    return lax.dot_general(_bf(a), _bf(b), (((2,), (1,)), ((0,), (0,))), preferred_element_type=F32)


def _bdot_nt(a, b):
    return lax.dot_general(_bf(a), _bf(b), (((2,), (2,)), ((0,), (0,))), preferred_element_type=F32)


def _bdot_tn(a, b):
    return lax.dot_general(_bf(a), _bf(b), (((1,), (1,)), ((0,), (0,))), preferred_element_type=F32)


def _dot01(m01, x):
    hi = x.astype(BF16)
    lo = (x - hi.astype(F32)).astype(BF16)
    return _dot(m01, hi) + _dot(m01, lo)


def _rstd(x):
    return lax.rsqrt(jnp.mean(x * x, axis=-1, keepdims=True) + EPS)


def _sigmoid(x):
    return jax.nn.sigmoid(x)


def _colsum(a):
    return jnp.sum(a, axis=0, keepdims=True)


def _stack_rows(rows):
    n = rows[0].shape[-1]
    rid = lax.broadcasted_iota(jnp.int32, (16, n), 0)
    out = jnp.zeros((16, n), F32)
    for i, r in enumerate(rows):
        out = jnp.where(rid == i, r, out)
    return out


def _head_map(fn, *arrs):
    outs = [fn(*[a[:, h * DH:(h + 1) * DH] for a in arrs]) for h in range(HEADS)]
    return jnp.concatenate(outs, axis=1)


def _gla_consts():
    r = np.arange(TM)[:, None]
    c = np.arange(TM)[None, :]
    same = (r // CHUNK) == (c // CHUNK)
    tril = same & (c <= r)
    triu = same & (c >= r)
    m = np.stack([tril, triu]).astype(np.float32)
    return jnp.asarray(m, BF16), jnp.asarray(m, F32)


def _pool_consts():
    r = np.arange(TM)[:, None]
    c = np.arange(TM)[None, :]
    same = (r // GRID_W) == (c // GRID_W)
    rp, cp = r % GRID_W, c % GRID_W
    bs, inv = [], []
    for w in POOL_WINDOWS:
        lo = np.clip(rp - w // 2, 0, GRID_W)
        hi = np.clip(rp - w // 2 + w, 0, GRID_W)
        bs.append(same & (cp >= lo) & (cp < hi))
        inv.append(1.0 / (hi - lo).astype(np.float32))
    b = np.stack(bs).astype(np.float32)
    bt = np.transpose(b, (0, 2, 1))
    return jnp.asarray(b, BF16), jnp.asarray(bt, BF16), jnp.asarray(np.stack(inv), F32)


def _mesh_pos():
    x, y, c = lax.axis_index("x"), lax.axis_index("y"), lax.axis_index("c")
    return x, y, c, 4 * x + 2 * y + c


def _peer(x, y, c, k):
    return (x ^ ((k >> 2) & 1), y ^ ((k >> 1) & 1), c ^ (k & 1))


def _small_gathers(refs, ssem, rsem):
    lb_r, ps_r, c_r, cctx_r, ada_r, adab_r, lb_o, ps_o, cg_o, mod_o, lb_out, ps_out, cg_out, mod0_o, mod1_o, modc_o = refs
    x, y, cc, idx = _mesh_pos()
    srcs = [lb_r, ps_r, c_r, mod_o.at[idx]]
    mine = [lb_o.at[idx], ps_o.at[idx], cg_o.at[idx], mod_o.at[idx]]

    def remote(a, k):
        return pltpu.make_async_remote_copy(src_ref=srcs[a], dst_ref=mine[a], send_sem=ssem.at[a, k], recv_sem=rsem.at[a, k],
                                            device_id=_peer(x, y, cc, k), device_id_type=MESH)

    first = [remote(a, k) for k in range(1, NDEV) for a in (2, 0, 1)]
    for cp in first:
        cp.start()
    lb_o[idx] = lb_r[...]
    ps_o[idx] = ps_r[...]
    cg_o[idx] = c_r[...]
    for k in range(1, NDEV):
        remote(2, k).wait_recv()
    rows = _stack_rows([cg_o[i] for i in range(NDEV)] + [cctx_r[...]])
    sc = rows * _sigmoid(rows)
    for l in range(2):
        mod_o[idx, l] = _dot(sc, ada_r[l])
    second = [remote(3, k) for k in range(1, NDEV)]
    for cp in second:
        cp.start()
    for k in range(1, NDEV):
        remote(3, k).wait_recv()

    def mod_rows(l, row):
        full = jnp.concatenate([mod_o[s, l, row, :] for s in range(NDEV)], axis=1) + adab_r[l:l + 1, :]
        return [full[:, j * D:(j + 1) * D] for j in range(3)]

    me = pl.ds(idx, 1)
    for out, parts in ((mod0_o, mod_rows(0, me)), (mod1_o, mod_rows(1, me)), (modc_o, mod_rows(0, slice(NDEV, NDEV + 1)))):
        for j in range(3):
            out[j:j + 1, :] = parts[j]
    for cp in first + second:
        cp.wait_send()
    for k in range(1, NDEV):
        for a in (0, 1):
            remote(a, k).wait_recv()
    lb_out[...] = lb_o[...]
    ps_out[...] = ps_o[...]
    cg_out[...] = cg_o[...]


def _gather_order(s, core):
    k = jnp.where(s == 2, 4, jnp.where(s == 4, 2, s))
    return k ^ jnp.where((core == 1) & (s >= 2) & (s <= 5), 6, 0)


GATHER_ISSUE = (1, 2, 4, 3, 5, 6, 7)
GATHER_ICI = (2, 4, 6)
GATHER_DIRECT = (1,) + GATHER_ICI
GLA_HB = 2
RS_SLOTS = 5


def _shard_of(kind, ref, i):
    if kind == "rows":
        return ref.at[pl.ds(pl.multiple_of(i * SH_ROWS, SH_ROWS), SH_ROWS), :]
    if kind == "major":
        return ref.at[i]
    assert kind == "grp"
    return ref.at[:, pl.ds(pl.multiple_of(i * SH_GRP, SH_GRP), SH_GRP), :]


def _gather_rider(step, n_steps, forward_at, kinds, srcs, outs, ssem, rsem, lsem):
    x, y, cc, idx = _mesh_pos()
    arrays = range(len(kinds))
    mine = [_shard_of(kinds[a], outs[a], idx) for a in arrays]

    def remote(a, k):
        return pltpu.make_async_remote_copy(src_ref=srcs[a], dst_ref=mine[a], send_sem=ssem.at[a, k], recv_sem=rsem.at[a, k],
                                            device_id=_peer(x, y, cc, k), device_id_type=MESH)

    def forward(a, k):
        blk = _shard_of(kinds[a], outs[a], idx ^ k)
        return pltpu.make_async_remote_copy(src_ref=blk, dst_ref=blk, send_sem=ssem.at[a, k ^ 1], recv_sem=rsem.at[a, k ^ 1],
                                            device_id=(x, y, 1 - cc), device_id_type=MESH)

    copies = [remote(a, k) for k in GATHER_DIRECT for a in arrays]
    passed = [forward(a, k) for k in GATHER_ICI for a in arrays]
    local = [pltpu.make_async_copy(srcs[a], mine[a], lsem.at[a]) for a in arrays]

    @pl.when(step == 0)
    def _():
        for cp in copies + local:
            cp.start()

    @pl.when(step == forward_at)
    def _():
        for k in GATHER_ICI:
            for a in arrays:
                remote(a, k).wait_recv()
                forward(a, k).start()

    def finish():
        @pl.when(step == n_steps - 1)
        def _():
            for cp in copies + passed:
                cp.wait_send()
            for a in arrays:
                remote(a, 1).wait_recv()
            for cp in passed:
                cp.wait_recv()
            for cp in local:
                cp.wait()

    return finish


def _scatter_rider(step, n_steps, kinds, grads, slots, ssem, rsem, lsem):
    x, y, cc, idx = _mesh_pos()
    arrays = range(len(kinds))
    dsts = [slots[a].at[idx] for a in arrays]

    def remote(a, k):
        px, py, pc = _peer(x, y, cc, k)
        return pltpu.make_async_remote_copy(src_ref=_shard_of(kinds[a], grads[a], 4 * px + 2 * py + pc), dst_ref=dsts[a],
                                            send_sem=ssem.at[a, k], recv_sem=rsem.at[a, k], device_id=(px, py, pc), device_id_type=MESH)

    copies = [remote(a, k) for k in GATHER_ISSUE for a in arrays]
    local = [pltpu.make_async_copy(_shard_of(kinds[a], grads[a], idx), dsts[a], lsem.at[a]) for a in arrays]

    @pl.when(step == 0)
    def _():
        for cp in copies + local:
            cp.start()

    def finish():
        @pl.when(step == n_steps - 1)
        def _():
            for cp in copies:
                cp.wait_send()
            for cp in copies:
                cp.wait_recv()
            for cp in local:
                cp.wait()

    return finish


def _rider_sems(n):
    return [pltpu.SemaphoreType.DMA((n, NDEV)), pltpu.SemaphoreType.DMA((n, NDEV)), pltpu.SemaphoreType.DMA((n,))]


def _scatter_rider2(step, n_steps, add_at, kinds, grads, slots, bufs, sems):
    x, y, cc, idx = _mesh_pos()
    sibling = (x, y, 1 - cc)
    arrays = range(len(kinds))
    psend, precv, isend, irecv, lown, sibsem, lself = sems

    def mine(a, i):
        return _shard_of(kinds[a], grads[a], i)

    def partial(a, p):
        return pltpu.make_async_remote_copy(src_ref=mine(a, idx ^ (2 * (p + 1)) ^ 1), dst_ref=bufs[a][1].at[p], send_sem=psend.at[a, p],
                                            recv_sem=precv.at[a, p], device_id=sibling, device_id_type=MESH)

    def load(a, p):
        return pltpu.make_async_copy(mine(a, idx ^ (2 * (p + 1))), bufs[a][0].at[p], lown.at[a, p])

    def chip_sum(a, p):
        return pltpu.make_async_remote_copy(src_ref=bufs[a][0].at[p], dst_ref=slots[a].at[2 + p], send_sem=isend.at[a, p],
                                            recv_sem=irecv.at[a, p], device_id=_peer(x, y, cc, 2 * (p + 1)), device_id_type=MESH)

    def to_sibling(a):
        return pltpu.make_async_remote_copy(src_ref=mine(a, idx ^ 1), dst_ref=slots[a].at[1], send_sem=sibsem.at[a, 0],
                                            recv_sem=sibsem.at[a, 1], device_id=sibling, device_id_type=MESH)

    def own(a):
        return pltpu.make_async_copy(mine(a, idx), slots[a].at[0], lself.at[a, 0])

    @pl.when(step == 0)
    def _():
        for a in arrays:
            for p in range(3):
                partial(a, p).start()
                load(a, p).start()
            to_sibling(a).start()
            own(a).start()

    @pl.when(step == add_at)
    def _():
        for a in arrays:
            for p in range(3):
                partial(a, p).wait_recv()
                load(a, p).wait()
                bufs[a][0][p] = (bufs[a][0][p].astype(F32) + bufs[a][1][p].astype(F32)).astype(BF16)
                chip_sum(a, p).start()

    def finish():
        @pl.when(step == n_steps - 1)
        def _():
            for a in arrays:
                for p in range(3):
                    partial(a, p).wait_send()
                    chip_sum(a, p).wait_send()
                    chip_sum(a, p).wait_recv()
                to_sibling(a).wait_send()
                to_sibling(a).wait_recv()
                own(a).wait()

    return finish


def _rider2_scratch(blocks):
    n = len(blocks)
    bufs = [pltpu.VMEM((3,) + tuple(b), BF16) for b in blocks for _ in range(2)]
    return bufs + [pltpu.SemaphoreType.DMA((n, 3)) for _ in range(5)] + [pltpu.SemaphoreType.DMA((n, 2)), pltpu.SemaphoreType.DMA((n, 1))]


def _rider2_split(refs, n):
    refs = list(refs)
    return [tuple(refs[2 * a:2 * a + 2]) for a in range(n)], tuple(refs[2 * n:2 * n + 7])


def _modulated(x, nw, shift, scale):
    r = _rstd(x)
    xn = x * r
    a = xn * nw
    return a * (1.0 + scale) + shift, r, xn, a


def _ctx_or_x(i, ctx_ref, x_ref):
    return jnp.where(i == 0, ctx_ref[...], x_ref[...])


def _f1_gather_matmul(idx1, ctx, x, nw, w_in, w_out, pw_in, pgrp, pw_out, lb_l, pscale, c, c_ctx, ada_w, ada_b):
    def body(idx_ref, ctx_ref, x_ref, nw_ref, win_r, wout_r, pwin_r, pgrp_r, pwout_r, lb_r, ps_r, c_r, cctx_r, ada_r, adab_r,
             g_ref, win_o, s_wout, s_pwin, s_pgrp, s_pwout, lb_o, ps_o, cg_o, mod0_o, mod1_o, modc_o,
             wslot, hx_sc, lb_g, ps_g, cg_g, mod_g, ssem, rsem, osem, dsem, sm_ssem, sm_rsem):
        del idx_ref
        s, i = pl.program_id(0), pl.program_id(1)
        x, y, cc, idx = _mesh_pos()
        k = _gather_order(s, cc)
        j = idx ^ k
        first = 4 - 2 * cc

        def remote(kk):
            return pltpu.make_async_remote_copy(src_ref=wslot.at[idx], dst_ref=wslot.at[idx], send_sem=ssem.at[kk], recv_sem=rsem.at[kk],
                                                device_id=_peer(x, y, cc, kk), device_id_type=MESH)

        def forward(kk):
            jj = idx ^ kk
            return pltpu.make_async_remote_copy(src_ref=wslot.at[jj], dst_ref=wslot.at[jj], send_sem=ssem.at[kk ^ 1],
                                                recv_sem=rsem.at[kk ^ 1], device_id=(x, y, 1 - cc), device_id_type=MESH)

        def relay(h):
            blk = wslot.at[idx ^ (4 >> h), pl.ds(h * (D // 2), D // 2), :]
            return pltpu.make_async_remote_copy(src_ref=blk, dst_ref=blk, send_sem=dsem.at[0, h], recv_sem=dsem.at[1, h],
                                                device_id=_peer(x, y, cc, 2 << h), device_id_type=MESH)

        def to_hbm(jj, kk):
            return pltpu.make_async_copy(wslot.at[jj], win_o.at[jj], osem.at[kk])

        @pl.when((s == 0) & (i == 0))
        def _():
            _small_gathers((lb_r, ps_r, c_r, cctx_r, ada_r, adab_r, lb_g, ps_g, cg_g, mod_g, lb_o, ps_o, cg_o, mod0_o, mod1_o, modc_o),
                           sm_ssem, sm_rsem)
            wslot[idx] = win_r[...].astype(BF16)
            remote(1).start()
            remote(first).start()
            s_wout[...] = wout_r[...].astype(BF16)
            s_pwin[...] = pwin_r[...].astype(BF16)
            s_pgrp[...] = pgrp_r[...].astype(BF16)
            s_pwout[...] = pwout_r[...].astype(BF16)

        @pl.when(s == 0)
        def _():
            shift = jnp.where(i == 0, modc_o[0:1, :], mod0_o[0:1, :])
            scale = jnp.where(i == 0, modc_o[1:2, :], mod0_o[1:2, :])
            hx, _, _, _ = _modulated(_ctx_or_x(i, ctx_ref, x_ref), nw_ref[...], shift, scale)
            hx_sc[i] = hx.astype(BF16)

        @pl.when((s == 2) & (i == 0))
        def _():
            remote(6 - first).start()

        @pl.when((s > 0) & (i == 0) & (k != 6))
        def _():
            remote(k).wait_recv()

            @pl.when((k & 1) == 0)
            def _():
                forward(k).start()

            for h in range(2):
                @pl.when(k == 4 >> h)
                def _():
                    relay(h).start()

        @pl.when((i == 0) & (k == 6))
        def _():
            for h in range(2):
                relay(h).wait_recv()
            forward(6).start()

        @pl.when(i == 0)
        def _():
            to_hbm(j, k).start()

        g_ref[...] = jnp.dot(hx_sc[i], wslot[j], preferred_element_type=F32)

        @pl.when((s == NDEV - 1) & (i == NT - 1))
        def _():
            for kk in (1, 2, 4):
                remote(kk).wait_send()
            for kk in GATHER_ICI:
                forward(kk).wait_send()
            for h in range(2):
                relay(h).wait_send()
            for kk in range(NDEV):
                to_hbm(idx ^ kk, kk).wait()

    grid_spec = pltpu.PrefetchScalarGridSpec(
        num_scalar_prefetch=1, grid=(NDEV, NT),
        in_specs=[VMEM_SPEC, pl.BlockSpec((TM, D), lambda s, i, ix: (jnp.where(s == 0, jnp.maximum(i - 1, 0), NTX - 1), 0))]
        + [VMEM_SPEC] * 12,
        out_specs=[pl.BlockSpec((TM, SH_WIN), lambda s, i, ix: (i, ix[0] ^ _gather_order(s, ix[0] & 1))), HBM_SPEC] + [VMEM_SPEC] * 10,
        scratch_shapes=[pltpu.VMEM((NDEV, D, SH_WIN), BF16), pltpu.VMEM((NT, TM, D), BF16),
                        pltpu.VMEM((NDEV, 2, DH), F32), pltpu.VMEM((NDEV, 1, DH), F32), pltpu.VMEM((NDEV, 1, D), F32),
                        pltpu.VMEM((NDEV, 2, 16, SH_ADA), F32),
                        pltpu.SemaphoreType.DMA((NDEV,)), pltpu.SemaphoreType.DMA((NDEV,)), pltpu.SemaphoreType.DMA((NDEV,)),
                        pltpu.SemaphoreType.DMA((2, 2)),
                        pltpu.SemaphoreType.DMA((4, NDEV)), pltpu.SemaphoreType.DMA((4, NDEV))])
    outs = (_sds((TT, WIN_COLS), F32), _sds((NDEV, D, SH_WIN), BF16),
            _sds((SH_ROWS, D), BF16), _sds((D, SH_PWIN), BF16), _sds((4, SH_GRP, PG), BF16), _sds((SH_ROWS, D), BF16),
            _sds((NDEV, 2, DH), F32), _sds((NDEV, 1, DH), F32), _sds((NDEV, 1, D), F32),
            _sds((3, D), F32), _sds((3, D), F32), _sds((3, D), F32))
    return pl.pallas_call(
        body, name="f1_gather_matmul", grid_spec=grid_spec, out_shape=outs,
        compiler_params=pltpu.CompilerParams(dimension_semantics=("arbitrary", "arbitrary"), vmem_limit_bytes=VMEM_LIMIT),
    )(idx1, ctx, x, nw, w_in, w_out, pw_in, pgrp, pw_out, lb_l, pscale, c, c_ctx, ada_w, ada_b)


def _gla_gates(pre, qpre, lbd, cum, rev):
    rows, n = pre.shape
    nch = rows // CHUNK
    sig = _sigmoid(pre)
    f = lbd + (1.0 - lbd) * sig
    k = 1.0 - f
    g = _dot01(cum, jnp.log(f))
    g3 = g.reshape(nch, CHUNK, n)
    last = 0 if rev else CHUNK - 1
    mid = CHUNK // 2 if rev else CHUNK // 2 - 1
    gl1, gm1 = g3[:, last:last + 1, :], g3[:, mid:mid + 1, :]

    def bc(a):
        return jnp.broadcast_to(a, g3.shape).reshape(rows, n)

    gm = bc(gm1)
    e_q, e_k = jnp.exp(g - gm), jnp.exp(gm - g)
    qsig = _sigmoid(qpre)
    qs = qpre * qsig * (DH ** -0.5)
    return dict(sig=sig, f=f, k=k, qsig=qsig, qs=qs, e_q=e_q, e_k=e_k,
                e_mid=[jnp.exp(gm1[ci]) for ci in range(nch)], e_rest=[jnp.exp(gl1[ci] - gm1[ci]) for ci in range(nch)])


def _put_heads(ref, lead, arr):
    for h in range(HEADS):
        ref[lead + (h,)] = arr[:, h * DH:(h + 1) * DH]


def _get_heads(ref, lead=()):
    return jnp.concatenate([ref[lead + (h,)] for h in range(HEADS)], axis=1)


def _gla_prep(g_all, lb, cum01, s_wout, s_pgrp):
    nch = TM // CHUNK

    def body(g_ref, lb_ref, cum_ref, swout_r, spgrp_r, p0_ref, p1_ref, v_ref, dec_ref, wout_o, pgrp_o, ssem, rsem, lsem):
        finish = _gather_rider(pl.program_id(0), NT, NT - 1, ("rows", "grp"), (swout_r, spgrp_r), (wout_o, pgrp_o), ssem, rsem, lsem)
        qpre = g_ref[:, 3 * E:4 * E]
        _put_heads(v_ref, (), g_ref[:, 2 * E:3 * E].astype(BF16))
        for d, p_ref in ((0, p0_ref), (1, p1_ref)):
            t = _gla_gates(g_ref[:, d * E:(d + 1) * E], qpre, lb_ref[d:d + 1, :], cum_ref[d], d == 1)
            _put_heads(p_ref, (0,), (t["qs"] * t["e_q"]).astype(BF16))
            _put_heads(p_ref, (1,), (t["k"] * t["e_k"]).astype(BF16))
            for ci in range(nch):
                dec_ref[d, 0, ci:ci + 1, :] = t["e_mid"][ci]
                dec_ref[d, 0, nch + ci:nch + ci + 1, :] = t["e_rest"][ci]
        finish()

    quad = pl.BlockSpec((2, HEADS, TM, DH), lambda i: (0, 0, i, 0))
    return pl.pallas_call(
        body, name="gla_prep", grid=(NT,),
        in_specs=[pl.BlockSpec((TM, 4 * E), lambda i: (i, 0)), VMEM_SPEC, VMEM_SPEC, HBM_SPEC, HBM_SPEC],
        out_specs=[quad, quad, pl.BlockSpec((HEADS, TM, DH), lambda i: (0, i, 0)), pl.BlockSpec((2, 1, 2 * nch, E), lambda i: (0, i, 0, 0)),
                   HBM_SPEC, HBM_SPEC],
        out_shape=(_sds((2, HEADS, TT, DH), BF16), _sds((2, HEADS, TT, DH), BF16), _sds((HEADS, TT, DH), BF16), _sds((2, NT, 2 * nch, E), F32),
                   _sds((E, D), BF16), _sds((4, PG, PG), BF16)),
        scratch_shapes=_rider_sems(2),
        compiler_params=pltpu.CompilerParams(dimension_semantics=("arbitrary",), vmem_limit_bytes=VMEM_LIMIT),
    )(g_all, lb, cum01, s_wout, s_pgrp)


def _scan_tile(i, rev):
    t = jnp.where(i == 0, 0, NT - i) if rev else i
    return t, pl.ds(pl.multiple_of(t * TM, TM), TM)


def _chunk_rows(dec_ref, lanes, cis, where):
    nch = TM // CHUNK

    def rows(off):
        return jnp.stack([dec_ref[d, where[d][0], off + ci:off + ci + 1, hh * DH:(hh + 1) * DH] for (d, hh), ci in zip(lanes, cis)])

    return rows(0), rows(nch)


def _gla_fwd(p0, p1, v_all, dec, mask01, s_pwin, s_pwout):
    n_steps = HEADS // GLA_HB

    def body(p0_ref, p1_ref, v_ref, dec_ref, msk_ref, spwin_r, spwout_r, o_ref, pwin_o, pwout_o, ob_sc, ssem, rsem, lsem):
        finish = _gather_rider(pl.program_id(0), n_steps, n_steps - 1, ("major", "rows"), (spwin_r, spwout_r), (pwin_o, pwout_o),
                               ssem, rsem, lsem)

        lanes = [(d, hh) for d in (0, 1) for hh in range(GLA_HB)]
        nch = TM // CHUNK

        def tile_body(i, st):
            where = [_scan_tile(i, d == 1) for d in (0, 1)]

            def stacked(fn):
                return jnp.stack([fn(d, hh, where[d][1]) for d, hh in lanes])

            qg, kg = [stacked(lambda d, hh, rows, ty=ty: (p1_ref if d else p0_ref)[ty, hh, rows, :]) for ty in range(2)]
            v = stacked(lambda d, hh, rows: v_ref[hh, rows, :])
            a = _bdot_nt(qg, kg) * jnp.stack([msk_ref[d] for d, _ in lanes])
            intra = _bdot(a, v)
            outs = [[None] * nch for _ in lanes]
            for n in range(nch):
                cis = [nch - 1 - n if d else n for d, _ in lanes]

                def chunk(arr):
                    return jnp.stack([arr[l, ci * CHUNK:(ci + 1) * CHUNK] for l, ci in enumerate(cis)])

                e_mid, e_rest = _chunk_rows(dec_ref, lanes, cis, where)
                inter = _bdot_nt(chunk(qg), st * e_mid)
                for l, ci in enumerate(cis):
                    outs[l][ci] = inter[l] + intra[l, ci * CHUNK:(ci + 1) * CHUNK]
                st = st * (e_mid * e_rest) + _bdot_tn(chunk(v), chunk(kg)) * e_rest
            for l, (d, hh) in enumerate(lanes):
                (ob_sc if d else o_ref)[hh, where[d][1], :] = jnp.concatenate(outs[l], axis=0)
            return st

        lax.fori_loop(0, NT, tile_body, jnp.zeros((len(lanes), DH, DH), F32))
        o_ref[...] += ob_sc[...]
        finish()

    quad = pl.BlockSpec((2, GLA_HB, TT, DH), lambda h: (0, h, 0, 0))
    head = pl.BlockSpec((GLA_HB, TT, DH), lambda h: (h, 0, 0))
    return pl.pallas_call(
        body, name="gla_fwd", grid=(n_steps,),
        in_specs=[quad, quad, head, pl.BlockSpec((2, NT, 8, GLA_HB * DH), lambda h: (0, 0, 0, h)),
                  pl.BlockSpec((2, TM, TM), lambda h: (0, 0, 0)), HBM_SPEC, HBM_SPEC],
        out_specs=[head, HBM_SPEC, HBM_SPEC],
        out_shape=(_sds((HEADS, TT, DH), F32), _sds((NDEV, D, SH_PWIN), BF16), _sds((E, D), BF16)),
        scratch_shapes=[pltpu.VMEM((GLA_HB, TT, DH), F32)] + _rider_sems(2),
        compiler_params=pltpu.CompilerParams(dimension_semantics=("arbitrary",), vmem_limit_bytes=VMEM_LIMIT),
    )(p0, p1, v_all, dec, mask01, s_pwin, s_pwout)


def _gated_norm(o, z, gw):
    r = _head_map(lambda oh: jnp.broadcast_to(_rstd(oh), oh.shape), o)
    on = o * r
    zs = _sigmoid(z)
    sz = z * zs
    return on * gw * sz, r, on, zs, sz


def _f3_out(o, g_all, x, gate, gw, wout):
    def body(o_ref, z_ref, x_ref, gate_ref, gw_ref, w_ref, x1_ref):
        og, _, _, _, _ = _gated_norm(_get_heads(o_ref), z_ref[...], gw_ref[...])
        x1_ref[...] = x_ref[...] + gate_ref[...] * _dot(og, w_ref[...])

    return pl.pallas_call(
        body, name="f3_out", grid=(NTX,),
        in_specs=[pl.BlockSpec((HEADS, TM, DH), lambda i: (0, i + 1, 0)), pl.BlockSpec((TM, E), lambda i: (i + 1, 4)),
                  pl.BlockSpec((TM, D), lambda i: (i, 0)), pl.BlockSpec((1, D), lambda i: (0, 0)),
                  pl.BlockSpec((1, E), lambda i: (0, 0)), pl.BlockSpec((E, D), lambda i: (0, 0))],
        out_specs=pl.BlockSpec((TM, D), lambda i: (i, 0)),
        out_shape=_sds((T, D), F32),
        compiler_params=pltpu.CompilerParams(dimension_semantics=("arbitrary",)),
    )(o, g_all, x, gate, gw, wout)


def _pool_layer(x1, tgt, mod1, nw1, fnw, pwin, pgrp, pscale, pwout, pb, pbt, pinv):
    def body(x_ref, t_ref, m_ref, nw_ref, fw_ref, pwin_ref, pgrp_ref, ps_ref, pwout_ref, pb_ref, pbt_ref, pinv_ref,
             dx_ref, gpwin_o, gpgrp_o, gpwout_o, dmod_o, gnw_o, gfw_o, gps_o, loss_o,
             a_pwin, a_pgrp, a_pwout):
        i = pl.program_id(0)

        @pl.when(i == 0)
        def _():
            for ref in (a_pwin, a_pgrp, a_pwout, dmod_o, gnw_o, gfw_o, gps_o, loss_o):
                ref[...] = jnp.zeros_like(ref)

        shift, scale, gate = m_ref[0:1, :], m_ref[1:2, :], m_ref[2:3, :]
        nw, fw, ps = nw_ref[...], fw_ref[...], ps_ref[...]
        x1 = x_ref[...]
        hx, r1, xn, a = _modulated(x1, nw, shift, scale)
        hxb = hx.astype(BF16)
        uz = jnp.concatenate([_dot(hxb, pwin_ref[j]) for j in range(NDEV)], axis=1)
        u, z = uz[:, :E], uz[:, E:]
        pooled, ys = [], []
        for g in range(4):
            ug = u[:, g * PG:(g + 1) * PG]
            pg = _dot01(pb_ref[g], ug) * pinv_ref[g] - ug
            pooled.append(pg.astype(BF16))
            ys.append(_dot(pooled[g], pgrp_ref[g]))
        ycat = jnp.concatenate(ys, axis=1)
        y = ycat * ps
        zs = _sigmoid(z)
        sz = z * zs
        p = (y * sz).astype(BF16)
        out = _dot(p, pwout_ref[...])
        x2 = x1 + gate * out
        r2 = _rstd(x2)
        xn2 = x2 * r2
        diff = xn2 * fw - t_ref[...]
        loss_o[...] += _colsum(diff * diff)
        dyf = diff * (1.0 / D)
        gfw_o[...] += _colsum(dyf * xn2)
        dxn2 = dyf * fw
        dx2 = r2 * (dxn2 - xn2 * jnp.mean(dxn2 * xn2, axis=-1, keepdims=True))
        dgate = _colsum(dx2 * out)
        dout = (dx2 * gate).astype(BF16)
        for j in range(4):
            cs = slice(j * PG, (j + 1) * PG)
            a_pwout[:, cs] += _dot_ta(p, dout[:, cs])
        dp = _dot_tb(dout, pwout_ref[...])
        dy = dp * sz
        dz = dp * y * (zs * (1.0 + z * (1.0 - zs)))
        gps_o[...] += _colsum(dy * ycat)
        dycat = dy * ps
        dus = []
        for g in range(4):
            dyg = dycat[:, g * PG:(g + 1) * PG].astype(BF16)
            a_pgrp[g] += _dot_ta(pooled[g], dyg)
            dpg = _dot_tb(dyg, pgrp_ref[g])
            dus.append(_dot01(pbt_ref[g], dpg * pinv_ref[g]) - dpg)
        duz = jnp.concatenate(dus + [dz], axis=1).astype(BF16)
        dhx = None
        for j in range(NDEV):
            dj = duz[:, j * SH_PWIN:(j + 1) * SH_PWIN]
            a_pwin[j] += _dot_ta(hxb, dj)
            part = _dot_tb(dj, pwin_ref[j])
            dhx = part if dhx is None else dhx + part
        dmod_o[0:1, :] += _colsum(dhx)
        dmod_o[1:2, :] += _colsum(dhx * a)
        dmod_o[2:3, :] += dgate
        da = dhx * (1.0 + scale)
        gnw_o[...] += _colsum(da * xn)
        dxn = da * nw
        dx_ref[...] = dx2 + r1 * (dxn - xn * jnp.mean(dxn * xn, axis=-1, keepdims=True))

        @pl.when(i == NTX - 1)
        def _():
            gpwin_o[...] = a_pwin[...].astype(BF16)
            gpgrp_o[...] = a_pgrp[...].astype(BF16)
            gpwout_o[...] = a_pwout[...].astype(BF16)

    tile = pl.BlockSpec((TM, D), lambda i: (i, 0))
    outs = (_sds((T, D), F32), _sds((NDEV, D, SH_PWIN), BF16), _sds((4, PG, PG), BF16), _sds((E, D), BF16),
            _sds((3, D), F32), _sds((1, D), F32), _sds((1, D), F32), _sds((1, E), F32), _sds((1, D), F32))
    return pl.pallas_call(
        body, name="pool_layer", grid=(NTX,),
        in_specs=[tile, tile] + [VMEM_SPEC] * 10,
        out_specs=[tile] + [VMEM_SPEC] * 8,
        out_shape=outs,
        scratch_shapes=[pltpu.VMEM((NDEV, D, SH_PWIN), F32), pltpu.VMEM((4, PG, PG), F32), pltpu.VMEM((E, D), F32)],
        compiler_params=pltpu.CompilerParams(dimension_semantics=("arbitrary",), vmem_limit_bytes=VMEM_LIMIT),
    )(x1, tgt, mod1, nw1, fnw, pwin, pgrp, pscale, pwout, pb, pbt, pinv)


def _b3_out_bwd(dx1, o, g_all, gate, gw, wout, gpwout):
    def body(dx_ref, o_ref, z_ref, gate_ref, gw_ref, w_ref, gpwout_r, do_ref, dz_ref, gw_o, dgate_o, ggw_o, rpwout_o,
             acc, *rider):
        i = pl.program_id(0)
        bufs, sems = _rider2_split(rider, 1)
        finish = _scatter_rider2(i, NT, 2, ("rows",), (gpwout_r,), (rpwout_o,), bufs, sems)

        @pl.when(i == 0)
        def _():
            acc[...] = jnp.zeros_like(acc)
            dgate_o[...] = jnp.zeros_like(dgate_o)
            ggw_o[...] = jnp.zeros_like(ggw_o)
            do_ref[...] = jnp.zeros_like(do_ref)
            dz_ref[...] = jnp.zeros_like(dz_ref)

        @pl.when(i > 0)
        def _():
            gw = gw_ref[...]
            z = z_ref[...]
            og, r, on, zs, sz = _gated_norm(_get_heads(o_ref), z, gw)
            ogb = og.astype(BF16)
            dx = dx_ref[...]
            dgate_o[...] += _colsum(dx * _dot(ogb, w_ref[...]))
            dy = (dx * gate_ref[...]).astype(BF16)
            for j in range(4):
                cs = slice(j * PG, (j + 1) * PG)
                acc[:, cs] += _dot_ta(ogb, dy[:, cs])
            dog = _dot_tb(dy, w_ref[...])
            dz_ref[...] = (dog * (on * gw) * (zs * (1.0 + z * (1.0 - zs)))).astype(BF16)
            dong = dog * sz
            ggw_o[...] += _colsum(dong * on)
            don = dong * gw
            do = _head_map(lambda dh, nh, rh: rh * (dh - nh * jnp.mean(dh * nh, axis=-1, keepdims=True)), don, on, r)
            _put_heads(do_ref, (), do.astype(BF16))

        @pl.when(i == NT - 1)
        def _():
            gw_o[...] = acc[...].astype(BF16)

        finish()

    prev = lambda i: (jnp.maximum(i - 1, 0), 0)
    heads = pl.BlockSpec((HEADS, TM, DH), lambda i: (0, i, 0))
    return pl.pallas_call(
        body, name="b3_out_bwd", grid=(NT,),
        in_specs=[pl.BlockSpec((TM, D), prev), heads, pl.BlockSpec((TM, E), lambda i: (i, 4)),
                  VMEM_SPEC, VMEM_SPEC, VMEM_SPEC, HBM_SPEC],
        out_specs=[heads, pl.BlockSpec((TM, E), lambda i: (i, 0)), VMEM_SPEC, VMEM_SPEC, VMEM_SPEC, HBM_SPEC],
        out_shape=(_sds((HEADS, TT, DH), BF16), _sds((TT, E), BF16), _sds((E, D), BF16), _sds((1, D), F32), _sds((1, E), F32),
                   _sds((RS_SLOTS, SH_ROWS, D), BF16)),
        scratch_shapes=[pltpu.VMEM((E, D), F32)] + _rider2_scratch([(SH_ROWS, D)]),
        compiler_params=pltpu.CompilerParams(dimension_semantics=("arbitrary",), vmem_limit_bytes=VMEM_LIMIT),
    )(dx1, o, g_all, gate, gw, wout, gpwout)


def _gla_bwd(p0, p1, v_all, dec, do, mask01, gpwin, gpgrp):
    nch = TM // CHUNK
    n_steps = HEADS // GLA_HB

    def body(p0_ref, p1_ref, v_ref, dec_ref, do_ref, msk_ref, gpwin_r, gpgrp_r, d0_ref, d1_ref, dv_ref, dgl_ref, rpwin_o, rpgrp_o,
             ss_sc, dv_sc, ssem, rsem, lsem, *rider):
        finish_grp = _scatter_rider(pl.program_id(0), n_steps, ("grp",), (gpgrp_r,), (rpgrp_o,), ssem, rsem, lsem)
        bufs, sems = _rider2_split(rider, 1)
        finish_win = _scatter_rider2(pl.program_id(0), n_steps, 1, ("major",), (gpwin_r,), (rpwin_o,), bufs, sems)

        lanes = [(d, hh) for d in (0, 1) for hh in range(GLA_HB)]
        zero = jnp.zeros((len(lanes), DH, DH), F32)
        dgl_ref[...] = jnp.zeros_like(dgl_ref)

        def p_of(d):
            return p1_ref if d else p0_ref

        def scan_step(i, n):
            where = [_scan_tile(i, d == 1) for d in (0, 1)]
            cis = [nch - 1 - n if d else n for d, _ in lanes]
            e_mid, e_rest = _chunk_rows(dec_ref, lanes, cis, where)

            def chunk(arr):
                return jnp.stack([arr[l, ci * CHUNK:(ci + 1) * CHUNK] for l, ci in enumerate(cis)])

            return where, cis, e_mid, e_rest, chunk

        def stacked(i, fn):
            where = [_scan_tile(i, d == 1) for d in (0, 1)]
            return jnp.stack([fn(d, hh, where[d][1]) for d, hh in lanes])

        def fwd_body(i, st):
            v = stacked(i, lambda d, hh, rows: v_ref[hh, rows, :])
            kg = stacked(i, lambda d, hh, rows: p_of(d)[1, hh, rows, :])
            for n in range(nch):
                _, _, e_mid, e_rest, chunk = scan_step(i, n)
                ss_sc[i * nch + n] = st
                st = st * (e_mid * e_rest) + _bdot_tn(chunk(v), chunk(kg)) * e_rest
            return st

        ss_sc[NT * nch] = lax.fori_loop(0, NT, fwd_body, zero)

        def bwd_body(ii, dst):
            i = NT - 1 - ii
            qg, kg = [stacked(i, lambda d, hh, rows, ty=ty: p_of(d)[ty, hh, rows, :]) for ty in range(2)]
            v = stacked(i, lambda d, hh, rows: v_ref[hh, rows, :])
            dob = stacked(i, lambda d, hh, rows: do_ref[hh, rows, :])
            msk = jnp.stack([msk_ref[d] for d, _ in lanes])
            a = (_bdot_nt(qg, kg) * msk).astype(BF16)
            da = (_bdot_nt(dob, v) * msk).astype(BF16)
            dqg = _bdot(da, kg)
            dkg = _bdot_tn(da, qg)
            dv_intra = _bdot_tn(a, dob)
            dv_l, dkg_l, dqg_l = ([[None] * nch for _ in lanes] for _ in range(3))
            for n in range(nch - 1, -1, -1):
                where, cis, e_mid, e_rest, chunk = scan_step(i, n)
                s_c, s_end = ss_sc[i * nch + n], ss_sc[i * nch + n + 1]
                dste = (dst * e_rest).astype(BF16)
                kg_c, v_c, dob_c = chunk(kg), chunk(v), chunk(dob)
                dv_c = chunk(dv_intra) + _bdot_nt(kg_c, dste)
                dkg_c = chunk(dkg) + _bdot(v_c, dste)
                dqg_c = chunk(dqg) + _bdot(dob_c, s_c * e_mid)
                dgl = jnp.sum(s_end * dst, axis=1, keepdims=True)
                for l, ((d, hh), ci) in enumerate(zip(lanes, cis)):
                    dv_l[l][ci], dkg_l[l][ci], dqg_l[l][ci] = dv_c[l], dkg_c[l], dqg_c[l]
                    dgl_ref[d, where[d][0], ci:ci + 1, hh * DH:(hh + 1) * DH] = dgl[l]
                dst = dst * (e_mid * e_rest) + _bdot_tn(dob_c, chunk(qg)) * e_mid
            where = [_scan_tile(i, d == 1) for d in (0, 1)]
            for l, (d, hh) in enumerate(lanes):
                rows = where[d][1]
                d_ref = d1_ref if d else d0_ref
                d_ref[0, hh, rows, :] = jnp.concatenate(dqg_l[l], axis=0).astype(BF16)
                d_ref[1, hh, rows, :] = jnp.concatenate(dkg_l[l], axis=0).astype(BF16)
                dv_sc[d, hh, rows, :] = jnp.concatenate(dv_l[l], axis=0).astype(BF16)
            return dst

        lax.fori_loop(0, NT, bwd_body, zero)
        dv_ref[...] = (dv_sc[0].astype(F32) + dv_sc[1].astype(F32)).astype(BF16)
        finish_grp()
        finish_win()

    quad = pl.BlockSpec((2, GLA_HB, TT, DH), lambda h: (0, h, 0, 0))
    col = pl.BlockSpec((GLA_HB, TT, DH), lambda h: (h, 0, 0))
    chunkv = pl.BlockSpec((2, NT, 8, GLA_HB * DH), lambda h: (0, 0, 0, h))
    outs = (_sds((2, HEADS, TT, DH), BF16), _sds((2, HEADS, TT, DH), BF16), _sds((HEADS, TT, DH), BF16), _sds((2, NT, 8, E), F32),
            _sds((RS_SLOTS, D, SH_PWIN), BF16), _sds((NDEV, 4, SH_GRP, PG), BF16))
    return pl.pallas_call(
        body, name="gla_bwd", grid=(n_steps,),
        in_specs=[quad, quad, col, chunkv, col, pl.BlockSpec((2, TM, TM), lambda h: (0, 0, 0)), HBM_SPEC, HBM_SPEC],
        out_specs=[quad, quad, col, chunkv, HBM_SPEC, HBM_SPEC],
        out_shape=outs,
        scratch_shapes=[pltpu.VMEM((NT * nch + 1, 2 * GLA_HB, DH, DH), F32), pltpu.VMEM((2, GLA_HB, TT, DH), BF16)] + _rider_sems(1)
        + _rider2_scratch([(D, SH_PWIN)]),
        compiler_params=pltpu.CompilerParams(dimension_semantics=("arbitrary",), vmem_limit_bytes=VMEM_LIMIT),
    )(p0, p1, v_all, dec, do, mask01, gpwin, gpgrp)


TMB = 128


def _gla_post_bwd(g_all, d0, d1, dgl, dv, dz, lb, cum01, gwout):
    nch = TMB // CHUNK

    def body(g_ref, d0_ref, d1_ref, dgl_ref, dv_ref, dz_ref, lb_ref, cum_ref, gwout_r, dg_ref, dlb_ref, rwout_o, *rider):
        i = pl.program_id(0)
        bufs, sems = _rider2_split(rider, 1)
        finish = _scatter_rider2(i, TT // TMB, 2, ("rows",), (gwout_r,), (rwout_o,), bufs, sems)

        @pl.when(i == 0)
        def _():
            dlb_ref[...] = jnp.zeros_like(dlb_ref)

        half = i & 1
        qpre = g_ref[:, 3 * E:4 * E]
        dqs_sum = None
        dpre = []
        for d, d_ref in ((0, d0_ref), (1, d1_ref)):
            rev = d == 1
            lbd = lb_ref[d:d + 1, :]
            t = _gla_gates(g_ref[:, d * E:(d + 1) * E], qpre, lbd, cum_ref[d, :TMB, :TMB], rev)
            dqs = _get_heads(d_ref, (0,)).astype(F32) * t["e_q"]
            dk = _get_heads(d_ref, (1,)).astype(F32) * t["e_k"]
            dg = t["qs"] * dqs - t["k"] * dk
            dgl8 = dgl_ref[d, 0]
            dgl_rows = [jnp.where(half == 0, dgl8[ci:ci + 1, :], dgl8[nch + ci:nch + ci + 1, :]) for ci in range(nch)]
            dgl_b = jnp.concatenate([jnp.broadcast_to(dgl_rows[ci], (CHUNK, E)) for ci in range(nch)], axis=0)
            pos = lax.broadcasted_iota(jnp.int32, (TMB, E), 0) & (CHUNK - 1)
            dg = dg + jnp.where(pos == (0 if rev else CHUNK - 1), dgl_b, 0.0)
            dlf = _dot01(cum_ref[1 - d, :TMB, :TMB], dg)
            df = dlf / t["f"] - dk
            sig = t["sig"]
            dpre.append((df * (1.0 - lbd) * sig * (1.0 - sig)).astype(BF16))
            dlb_ref[d:d + 1, :] += _colsum(df * (1.0 - sig))
            dqs_sum = dqs if dqs_sum is None else dqs_sum + dqs
            qsig = t["qsig"]
        dqpre = dqs_sum * (DH ** -0.5) * (qsig * (1.0 + qpre * (1.0 - qsig)))
        dg_ref[...] = jnp.concatenate([dpre[0], dpre[1], _get_heads(dv_ref), dqpre.astype(BF16), dz_ref[...]], axis=1)
        finish()

    quad = pl.BlockSpec((2, HEADS, TMB, DH), lambda i: (0, 0, i, 0))
    tile = pl.BlockSpec((TMB, E), lambda i: (i, 0))
    return pl.pallas_call(
        body, name="gla_post_bwd", grid=(TT // TMB,),
        in_specs=[pl.BlockSpec((TMB, 4 * E), lambda i: (i, 0)), quad, quad,
                  pl.BlockSpec((2, 1, 8, E), lambda i: (0, i // 2, 0, 0)), pl.BlockSpec((HEADS, TMB, DH), lambda i: (0, i, 0)), tile,
                  VMEM_SPEC, VMEM_SPEC, HBM_SPEC],
        out_specs=[pl.BlockSpec((TMB, WIN_COLS), lambda i: (i, 0)), VMEM_SPEC, HBM_SPEC],
        out_shape=(_sds((TT, WIN_COLS), BF16), _sds((2, E), F32), _sds((RS_SLOTS, SH_ROWS, D), BF16)),
        scratch_shapes=_rider2_scratch([(SH_ROWS, D)]),
        compiler_params=pltpu.CompilerParams(dimension_semantics=("arbitrary",), vmem_limit_bytes=VMEM_LIMIT),
    )(g_all, d0, d1, dgl, dv, dz, lb, cum01, gwout)


DG_RING = 3
WIN_SLOTS = 4


def _scatter_order(s, core):
    return (NDEV - 1 - s) ^ jnp.where((s >= 2) & (s <= 5) & ((s & 1) == core), 6, 0)


def _b1_in_bwd(idx1, ctx, x, dx1, dg, nw, msel, win):
    last_s = NDEV - 1
    half = D // 2

    def body(idx_ref, ctx_ref, x_ref, dx1_ref, dg_ref, nw_ref, m_ref, w_ref, gx_ref, rwin_o, dmx_o, dmc_o, gnw_o,
             hx_sc, dhx_sc, acc, sbuf, pbuf, rbuf, psend, precv, isend, irecv, dsend, drecv, sibsem, lsem, dgbuf, gsem):
        del idx_ref
        s, i = pl.program_id(0), pl.program_id(1)
        x, y, cc, idx = _mesh_pos()
        shift, scale = m_ref[0, 0:1, :], m_ref[0, 1:2, :]
        sibling = (x, y, 1 - cc)

        def partial(p):
            return pltpu.make_async_remote_copy(src_ref=sbuf.at[0], dst_ref=pbuf.at[p], send_sem=psend.at[p], recv_sem=precv.at[p],
                                                device_id=sibling, device_id_type=MESH)

        def chip_sum(p):
            return pltpu.make_async_remote_copy(src_ref=sbuf.at[1], dst_ref=rwin_o.at[2 + p], send_sem=isend.at[p], recv_sem=irecv.at[p],
                                                device_id=_peer(x, y, cc, 2 * (p + 1)), device_id_type=MESH)

        def relay(h):
            return pltpu.make_async_remote_copy(src_ref=sbuf.at[1, pl.ds(h * half, half), :], dst_ref=rbuf.at[h], send_sem=dsend.at[h],
                                                recv_sem=drecv.at[h], device_id=_peer(x, y, cc, 2 * (h + 1)), device_id_type=MESH)

        to_sibling = pltpu.make_async_remote_copy(src_ref=sbuf.at[0], dst_ref=rwin_o.at[1], send_sem=sibsem.at[0], recv_sem=sibsem.at[1],
                                                  device_id=sibling, device_id_type=MESH)
        own = pltpu.make_async_copy(sbuf.at[1], rwin_o.at[0], lsem)

        @pl.when((s == 0) & (i == 0))
        def _():
            for ref in (dmx_o, dmc_o, gnw_o):
                ref[...] = jnp.zeros_like(ref)

        @pl.when(s == 0)
        def _():
            hx, _, _, _ = _modulated(_ctx_or_x(i, ctx_ref, x_ref), nw_ref[...], shift, scale)
            hx_sc[i] = hx.astype(BF16)

        @pl.when(i == 0)
        def _():
            acc[...] = jnp.zeros_like(acc)

        n = s * NT + i

        def fetch(m):
            sm = m // NT
            jm = idx ^ _scatter_order(sm, cc)
            src = dg_ref.at[pl.ds(pl.multiple_of((m - sm * NT) * TM, TM), TM), pl.ds(pl.multiple_of(jm * SH_WIN, 128), SH_WIN)]
            return pltpu.make_async_copy(src, dgbuf.at[m % DG_RING], gsem.at[m % DG_RING])

        @pl.when(n == 0)
        def _():
            for m in range(DG_RING - 1):
                fetch(m).start()

        @pl.when(n + DG_RING - 1 < NDEV * NT)
        def _():
            fetch(n + DG_RING - 1).start()

        fetch(n).wait()
        dgb = dgbuf[n % DG_RING]
        hxb = hx_sc[i]
        for lo, hi in ((0, 256), (256, 512), (512, SH_WIN)):
            acc[:, lo:hi] += _dot_ta(hxb, dgb[:, lo:hi])
        part = _dot_tb(dgb, w_ref[0])

        @pl.when(s == 0)
        def _():
            dhx_sc[i] = part

        @pl.when(s > 0)
        def _():
            dhx_sc[i] += part

        done = i == NT - 1

        def hand_over(p, before):
            before.wait_send()
            sbuf[0] = acc[...].astype(BF16)
            partial(p).start()

        def send_chip_sum(p, before):
            for cp in before:
                cp.wait_send()
            partial(p).wait_recv()
            sbuf[1] = (acc[...] + pbuf[p].astype(F32)).astype(BF16)
            h = 1 - p
            rows = pl.ds(h * half, half)
            relay(h).wait_recv()
            sbuf[1, rows, :] = (acc[rows, :] + pbuf[p, rows, :].astype(F32) + rbuf[h].astype(F32)).astype(BF16)
            chip_sum(p).start()

        @pl.when(done & (s == 0))
        def _():
            sbuf[0] = acc[...].astype(BF16)
            partial(2).start()

        @pl.when(done & (s == 1))
        def _():
            partial(2).wait_recv()
            sbuf[1] = (acc[...] + pbuf[2].astype(F32)).astype(BF16)
            for h in range(2):
                relay(h).start()

        for core in range(2):
            @pl.when(done & (cc == core) & (s == 2))
            def _(core=core):
                hand_over(core, partial(2))

            @pl.when(done & (cc == core) & (s == 3))
            def _(core=core):
                send_chip_sum(1 - core, [relay(0), relay(1)])

            @pl.when(done & (cc == core) & (s == 4))
            def _(core=core):
                hand_over(1 - core, partial(core))

            @pl.when(done & (cc == core) & (s == 5))
            def _(core=core):
                send_chip_sum(core, [chip_sum(1 - core)])

            @pl.when(done & (cc == core) & (s == last_s - 1))
            def _(core=core):
                partial(1 - core).wait_send()
                sbuf[0] = acc[...].astype(BF16)
                to_sibling.start()

            @pl.when(done & (cc == core) & (s == last_s))
            def _(core=core):
                chip_sum(core).wait_send()
                sbuf[1] = acc[...].astype(BF16)
                own.start()

        @pl.when(s == last_s)
        def _():
            nw = nw_ref[...]
            _, r, xn, a = _modulated(_ctx_or_x(i, ctx_ref, x_ref), nw, shift, scale)
            dhx = dhx_sc[i]
            dsh, dsc = _colsum(dhx), _colsum(dhx * a)
            da = dhx * (1.0 + scale)
            gnw_o[...] += _colsum(da * xn)
            dxn = da * nw
            gx_ref[...] = dx1_ref[...] + r * (dxn - xn * jnp.mean(dxn * xn, axis=-1, keepdims=True))

            @pl.when(i == 0)
            def _():
                dmc_o[0:1, :] += dsh
                dmc_o[1:2, :] += dsc

            @pl.when(i > 0)
            def _():
                dmx_o[0:1, :] += dsh
                dmx_o[1:2, :] += dsc

        @pl.when((i == NT - 1) & (s == last_s))
        def _():
            to_sibling.wait_send()
            to_sibling.wait_recv()
            for p in range(2):
                chip_sum(p).wait_recv()
            own.wait()

    grid_spec = pltpu.PrefetchScalarGridSpec(
        num_scalar_prefetch=1, grid=(NDEV, NT),
        in_specs=[VMEM_SPEC,
                  pl.BlockSpec((TM, D), lambda s, i, ix: (jnp.where((s == 0) | (s == last_s), jnp.maximum(i - 1, 0), NTX - 1), 0)),
                  pl.BlockSpec((TM, D), lambda s, i, ix: (jnp.where(s == last_s, jnp.maximum(i - 1, 0), 0), 0)),
                  HBM_SPEC, VMEM_SPEC,
                  pl.BlockSpec((1, 2, D), lambda s, i, ix: (jnp.minimum(i, 1), 0, 0)),
                  pl.BlockSpec((1, D, SH_WIN), lambda s, i, ix: (ix[0] ^ _scatter_order(s, ix[0] & 1), 0, 0))],
        out_specs=[pl.BlockSpec((TM, D), lambda s, i, ix: (jnp.where(s == last_s, jnp.maximum(i - 1, 0), 0), 0)),
                   HBM_SPEC, VMEM_SPEC, VMEM_SPEC, VMEM_SPEC],
        scratch_shapes=[pltpu.VMEM((NT, TM, D), BF16), pltpu.VMEM((NT, TM, D), F32), pltpu.VMEM((D, SH_WIN), F32),
                        pltpu.VMEM((2, D, SH_WIN), BF16), pltpu.VMEM((3, D, SH_WIN), BF16), pltpu.VMEM((2, half, SH_WIN), BF16),
                        pltpu.SemaphoreType.DMA((3,)), pltpu.SemaphoreType.DMA((3,)), pltpu.SemaphoreType.DMA((2,)),
                        pltpu.SemaphoreType.DMA((2,)), pltpu.SemaphoreType.DMA((2,)), pltpu.SemaphoreType.DMA((2,)),
                        pltpu.SemaphoreType.DMA((2,)), pltpu.SemaphoreType.DMA,
                        pltpu.VMEM((DG_RING, TM, SH_WIN), BF16), pltpu.SemaphoreType.DMA((DG_RING,))])
    return pl.pallas_call(
        body, name="b1_in_bwd", grid_spec=grid_spec,
        out_shape=(_sds((T, D), F32), _sds((WIN_SLOTS, D, SH_WIN), BF16), _sds((2, D), F32), _sds((2, D), F32), _sds((1, D), F32)),
        compiler_params=pltpu.CompilerParams(dimension_semantics=("arbitrary", "arbitrary"), vmem_limit_bytes=VMEM_LIMIT),
    )(idx1, ctx, x, dx1, dg, nw, msel, win)


def _reduce_small(pd, pv, cg, c_ctx, ada_w0):
    n_arr = 3

    def body(pd_r, pv_r, cg_r, cctx_r, ada_r, gada_o, gadab_o, gcctx_o, pvsum_o, loss_o,
             pd_all, pv_all, dsc_all, dsc_mine, ssem, rsem):
        x, y, cc, idx = _mesh_pos()
        srcs = [pd_r, pv_r, dsc_mine]
        dsts = [pd_all.at[idx], pv_all.at[idx], dsc_all.at[idx]]

        def remote(a, k):
            return pltpu.make_async_remote_copy(src_ref=srcs[a], dst_ref=dsts[a], send_sem=ssem.at[a, k], recv_sem=rsem.at[a, k],
                                                device_id=_peer(x, y, cc, k), device_id_type=MESH)

        first = [remote(a, k) for k in range(1, NDEV) for a in (0, 1)]
        for cp in first:
            cp.start()
        pd_all[idx] = pd_r[...]
        pv_all[idx] = pv_r[...]
        for k in range(1, NDEV):
            remote(0, k).wait_recv()
            remote(1, k).wait_recv()
        mine = [pd_all[s, :, pl.ds(idx, 1), :] for s in range(NDEV)]
        dmc = functools.reduce(lambda u, v: u + v, [m[2] for m in mine])
        rows = _stack_rows([cg_r[i] for i in range(NDEV)] + [cctx_r[...]])
        sc = (rows * _sigmoid(rows)).astype(BF16)
        gada_o[0] = _dot_ta(sc, _stack_rows([m[0] for m in mine] + [dmc]))
        gada_o[1] = _dot_ta(sc, _stack_rows([m[1] for m in mine]))
        dsc_mine[...] = _dot_tb(jnp.broadcast_to(dmc, (8, SH_ADA)), ada_r[...])[0:1, :]
        dsc_all[idx] = dsc_mine[...]
        second = [remote(2, k) for k in range(1, NDEV)]
        for cp in second:
            cp.start()
        tot = [functools.reduce(lambda u, v: u + v, [pd_all[s, l] for s in range(NDEV)]) for l in range(3)]
        gadab_o[0] = tot[0] + tot[2]
        gadab_o[1] = tot[1]
        pvs = functools.reduce(lambda u, v: u + v, [pv_all[s] for s in range(NDEV)])
        pvsum_o[...] = pvs
        loss_o[...] = jnp.broadcast_to(jnp.sum(pvs[:, PV_LOSS:PV_LOSS + D], axis=-1, keepdims=True) * (0.5 / D), (1, 128))
        for k in range(1, NDEV):
            remote(2, k).wait_recv()
        dsc = functools.reduce(lambda u, v: u + v, [dsc_all[s] for s in range(NDEV)])
        cx = cctx_r[...]
        sx = _sigmoid(cx)
        gcctx_o[...] = dsc * (sx * (1.0 + cx * (1.0 - sx)))
        for cp in first + second:
            cp.wait_send()

    outs = (_sds((2, D, SH_ADA), F32), _sds((2, NDEV, SH_ADA), F32), _sds((1, D), F32), _sds((1, PV_LEN), F32), _sds((1, 128), F32))
    return pl.pallas_call(
        body, name="reduce_small", out_shape=outs,
        in_specs=[VMEM_SPEC] * 5, out_specs=[VMEM_SPEC] * 5,
        scratch_shapes=[
            pltpu.VMEM((NDEV, 3, NDEV, SH_ADA), F32), pltpu.VMEM((NDEV, 1, PV_LEN), F32), pltpu.VMEM((NDEV, 1, D), F32),
            pltpu.VMEM((1, D), F32),
            pltpu.SemaphoreType.DMA((n_arr, NDEV)), pltpu.SemaphoreType.DMA((n_arr, NDEV)),
        ],
        compiler_params=pltpu.CompilerParams(vmem_limit_bytes=VMEM_LIMIT),
    )(pd, pv, cg, c_ctx, ada_w0)


PV_NW, PV_GNORM, PV_FINAL, PV_LB, PV_PSCALE, PV_LOSS, PV_LEN = 0, 2 * D, 3 * D, 4 * D, 6 * D, 7 * D, 8 * D


def _adamw(w, g, m, v):
    m = ADAM_B1 * m + (1.0 - ADAM_B1) * g
    v = ADAM_B2 * v + (1.0 - ADAM_B2) * (g * g)
    m_hat = m / (1.0 - ADAM_B1 ** ADAM_STEP)
    v_hat = v / (1.0 - ADAM_B2 ** ADAM_STEP)
    delta = -ADAM_LR * (m_hat / (jnp.sqrt(v_hat) + ADAM_EPS) + ADAM_WD * w)
    return delta, m, v


ADAM_STEPS = 8


def _adam_all(sharded, dense, small, lb_idx, lbv):
    ns, nd, nsm = len(sharded), len(dense), len(small)

    def body(*refs):
        it = iter(refs)
        sh_in = [[next(it) for _ in range(4)] for _ in range(ns)]
        de_in = [[next(it) for _ in range(4)] for _ in range(nd)]
        sm_in = [[next(it) for _ in range(4)] for _ in range(nsm)]
        lb_r = next(it)
        sh_out = [[next(it) for _ in range(4)] for _ in range(ns)]
        de_out = [[next(it) for _ in range(3)] for _ in range(nd)]
        sm_out = [[next(it) for _ in range(4)] for _ in range(nsm)]
        for (p, w, m, v), outs in zip(sh_in, sh_out):
            g = p[0].astype(F32)
            for s in range(1, p.shape[0]):
                g = g + p[s].astype(F32)
            d, mn, vn = _adamw(w[...], g, m[...], v[...])
            outs[0][...], outs[1][...], outs[2][...], outs[3][...] = g, d, mn, vn
        for (g, w, m, v), outs in zip(de_in, de_out):
            d, mn, vn = _adamw(w[...], g[...], m[...], v[...])
            outs[0][...], outs[1][...], outs[2][...] = d, mn, vn

        @pl.when(pl.program_id(0) == 0)
        def _():
            for j, ((g, w, m, v), outs) in enumerate(zip(sm_in, sm_out)):
                gj = g[...]
                if j == lb_idx:
                    gj = gj * lb_r[...] * (1.0 - lb_r[...])
                d, mn, vn = _adamw(w[...], gj, m[...], v[...])
                outs[0][...], outs[1][...], outs[2][...], outs[3][...] = gj, d, mn, vn

    def tile(a):
        return pl.BlockSpec((a.shape[0] // ADAM_STEPS, a.shape[1]), lambda i: (i, 0))

    in_specs, out_specs, out_shape, args = [], [], [], []
    for p, w, m, v in sharded:
        in_specs += [pl.BlockSpec((p.shape[0], p.shape[1] // ADAM_STEPS, p.shape[2]), lambda i: (0, i, 0))] + [tile(w)] * 3
        args += [p, w, m, v]
    for g, w, m, v in dense:
        in_specs += [tile(w)] * 4
        args += [g, w, m, v]
    for g, w, m, v in small:
        in_specs += [VMEM_SPEC] * 4
        args += [g, w, m, v]
    in_specs.append(VMEM_SPEC)
    args.append(lbv)
    for _, w, _, _ in sharded:
        out_specs += [tile(w)] * 4
        out_shape += [_sds(w.shape, F32)] * 4
    for _, w, _, _ in dense:
        out_specs += [tile(w)] * 3
        out_shape += [_sds(w.shape, F32)] * 3
    for _, w, _, _ in small:
        out_specs += [VMEM_SPEC] * 4
        out_shape += [_sds(w.shape, F32)] * 4
    res = pl.pallas_call(body, name="adam_all", grid=(ADAM_STEPS,), in_specs=in_specs, out_specs=out_specs, out_shape=tuple(out_shape),
                         compiler_params=pltpu.CompilerParams(dimension_semantics=("arbitrary",), vmem_limit_bytes=VMEM_LIMIT))(*args)
    it = iter(res)
    return ([tuple(next(it) for _ in range(4)) for _ in range(ns)], [tuple(next(it) for _ in range(3)) for _ in range(nd)],
            [tuple(next(it) for _ in range(4)) for _ in range(nsm)])


def kernel(x, c, ctx, c_ctx, ada_w, ada_b, norm_w, hgrn_w_in, hgrn_lb_logits, hgrn_gnorm_w, hgrn_w_out, pool_w_in, pool_w_grp, pool_scale, pool_w_out, final_norm_w, loss_target, m_c_ctx, m_ada_w, m_ada_b, m_norm_w, m_hgrn_w_in, m_hgrn_lb_logits, m_hgrn_gnorm_w, m_hgrn_w_out, m_pool_w_in, m_pool_w_grp, m_pool_scale, m_pool_w_out, m_final_norm_w, v_c_ctx, v_ada_w, v_ada_b, v_norm_w, v_hgrn_w_in, v_hgrn_lb_logits, v_hgrn_gnorm_w, v_hgrn_w_out, v_pool_w_in, v_pool_w_grp, v_pool_scale, v_pool_w_out, v_final_norm_w):
    idx = 4 * lax.axis_index("x") + 2 * lax.axis_index("y") + lax.axis_index("c")
    cctx2 = c_ctx.reshape(1, D)
    cum01, mask01 = _gla_consts()
    pb, pbt, pinv = _pool_consts()

    idx1 = idx.reshape(1).astype(jnp.int32)
    nw0, nw1 = norm_w[0:1], norm_w[1:2]
    fnw = final_norm_w.reshape(1, D)
    g_all, win, s_wout, s_pwin, s_pgrp, s_pwout, lbl_g, ps_g, cg, mod0, mod1, modc = _f1_gather_matmul(
        idx1, ctx[0], x[0], nw0, hgrn_w_in[0], hgrn_w_out[0], pool_w_in[0], pool_w_grp[0], pool_w_out[0], hgrn_lb_logits[0],
        pool_scale, c, cctx2, ada_w, ada_b)
    lb = jax.nn.sigmoid(jnp.transpose(lbl_g, (1, 0, 2)).reshape(2, E))
    pscale = ps_g.reshape(1, E)
    msel = jnp.stack([modc[:2], mod0[:2]])
    p0, p1, v_all, dec, wout, pgrp = _gla_prep(g_all, lb, cum01, s_wout, s_pgrp)
    o, pwin, pwout = _gla_fwd(p0, p1, v_all, dec, mask01, s_pwin, s_pwout)
    x1 = _f3_out(o, g_all, x[0], mod0[2:3], hgrn_gnorm_w, wout)
    dx1, gpwin, gpgrp, gpwout, dmod1, gnw1, gfw, gps, lossv = _pool_layer(
        x1, loss_target[0], mod1, nw1, fnw, pwin, pgrp, pscale, pwout, pb, pbt, pinv)
    do, dz, gwout, dgate0, ggw, rpwout = _b3_out_bwd(dx1, o, g_all, mod0[2:3], hgrn_gnorm_w, wout, gpwout)
    d0, d1, dv, dgl, rpwin, rpgrp = _gla_bwd(p0, p1, v_all, dec, do, mask01, gpwin, gpgrp)
    dg, dlb, rwout = _gla_post_bwd(g_all, d0, d1, dgl, dv, dz, lb, cum01, gwout)
    grad_x, rwin, dmx, dmc, gnw0 = _b1_in_bwd(idx1, ctx[0], x[0], dx1, dg, nw0, msel, win)

    dmod0 = jnp.concatenate([dmx, dgate0], axis=0)
    dmodc = jnp.concatenate([dmc, jnp.zeros((1, D), F32)], axis=0)
    pd = jnp.stack([dmod0, dmod1, dmodc]).reshape(3, NDEV, SH_ADA)
    pv = jnp.concatenate([gnw0, gnw1, ggw, gfw, dlb.reshape(1, 2 * E), gps, lossv], axis=1)
    g_ada, g_adab, g_cctx, pvsum, loss128 = _reduce_small(pd, pv, cg, cctx2, ada_w[0])

    g2 = (4 * SH_GRP, PG)
    sharded_names = ["hgrn_w_in", "hgrn_w_out", "pool_w_in", "pool_w_grp", "pool_w_out"]
    sharded = [(rwin, hgrn_w_in[0], m_hgrn_w_in[0], v_hgrn_w_in[0]),
               (rwout, hgrn_w_out[0], m_hgrn_w_out[0], v_hgrn_w_out[0]),
               (rpwin, pool_w_in[0], m_pool_w_in[0], v_pool_w_in[0]),
               (rpgrp.reshape((NDEV,) + g2), pool_w_grp[0].reshape(g2), m_pool_w_grp[0].reshape(g2), v_pool_w_grp[0].reshape(g2)),
               (rpwout, pool_w_out[0], m_pool_w_out[0], v_pool_w_out[0])]
    a2 = (2 * D, SH_ADA)
    g_ada2 = g_ada.reshape(a2)
    dense = [(g_ada2, ada_w.reshape(a2), m_ada_w.reshape(a2), v_ada_w.reshape(a2))]
    lb_me = lax.dynamic_slice_in_dim(lb, idx * DH, DH, axis=1)
    small_names = ["c_ctx", "ada_b", "norm_w", "hgrn_lb_logits", "hgrn_gnorm_w", "pool_scale", "final_norm_w"]
    small = [(g_cctx, cctx2, m_c_ctx.reshape(1, D), v_c_ctx.reshape(1, D)),
             (g_adab.reshape(2, 3 * D), ada_b, m_ada_b, v_ada_b),
             (pvsum[:, PV_NW:PV_NW + 2 * D].reshape(2, D), norm_w, m_norm_w, v_norm_w),
             (lax.dynamic_slice_in_dim(pvsum[:, PV_LB:PV_LB + 2 * E].reshape(2, E), idx * DH, DH, axis=1),
              hgrn_lb_logits[0], m_hgrn_lb_logits[0], v_hgrn_lb_logits[0]),
             (pvsum[:, PV_GNORM:PV_GNORM + E], hgrn_gnorm_w, m_hgrn_gnorm_w, v_hgrn_gnorm_w),
             (lax.dynamic_slice_in_dim(pvsum[:, PV_PSCALE:PV_PSCALE + E], idx * DH, DH, axis=1), pool_scale, m_pool_scale, v_pool_scale),
             (pvsum[:, PV_FINAL:PV_FINAL + D], fnw, m_final_norm_w.reshape(1, D), v_final_norm_w.reshape(1, D))]
    r_sharded, r_dense, r_small = _adam_all(sharded, dense, small, 3, lb_me)
    out = dict(zip(sharded_names, r_sharded))
    out["ada_w"] = (g_ada2,) + r_dense[0]
    out.update(zip(small_names, r_small))

    shapes = {"c_ctx": (D,), "ada_w": (2, D, SH_ADA), "ada_b": (2, 3 * D), "norm_w": (2, D), "hgrn_w_in": (1, D, SH_WIN),
              "hgrn_lb_logits": (1, 2, DH), "hgrn_gnorm_w": (1, E), "hgrn_w_out": (1, SH_ROWS, D), "pool_w_in": (1, D, SH_PWIN),
              "pool_w_grp": (1, 4, SH_GRP, PG), "pool_scale": (1, DH), "pool_w_out": (1, SH_ROWS, D), "final_norm_w": (D,)}
    order = ["c_ctx", "ada_w", "ada_b", "norm_w", "hgrn_w_in", "hgrn_lb_logits", "hgrn_gnorm_w", "hgrn_w_out", "pool_w_in",
             "pool_w_grp", "pool_scale", "pool_w_out", "final_norm_w"]
    flat = [out[name][q].reshape(shapes[name]) for q in range(4) for name in order]
    return (loss128[0, 0], grad_x[None], *flat)
```

```python
import functools

import numpy as np
import jax
import jax.numpy as jnp
from jax import lax
from jax.experimental import pallas as pl
from jax.experimental.pallas import tpu as pltpu

F32 = jnp.float32
BF16 = jnp.bfloat16

D = 1024
E = 1024
HEADS = 8
DH = 128
CHUNK = 64
T = 2048
TC = 256
TT = T + TC
TM = 256
NT = TT // TM
NTX = T // TM
NDEV = 8
GRID_W = 64
POOL_WINDOWS = (2, 4, 8, 16)
PG = 256
EPS = 1e-6
WIN_COLS = 5 * E
SH_WIN = WIN_COLS // NDEV
SH_PWIN = 2 * E // NDEV
SH_ROWS = E // NDEV
SH_GRP = PG // NDEV
SH_ADA = 3 * D // NDEV
VMEM_LIMIT = 56 * 1024 * 1024

ADAM_LR, ADAM_B1, ADAM_B2, ADAM_EPS, ADAM_WD, ADAM_STEP = 0.001, 0.9, 0.999, 1e-08, 0.01, 10

MESH = pl.DeviceIdType.MESH
VMEM_SPEC = pl.BlockSpec(memory_space=pltpu.VMEM)
HBM_SPEC = pl.BlockSpec(memory_space=pltpu.HBM)


def _sds(shape, dtype):
    return jax.ShapeDtypeStruct(shape, dtype)


def _bf(a):
    return a if a.dtype == BF16 else a.astype(BF16)


def _dot(a, b):
    return lax.dot_general(_bf(a), _bf(b), (((1,), (0,)), ((), ())), preferred_element_type=F32)


def _dot_tb(a, b):
    return lax.dot_general(_bf(a), _bf(b), (((1,), (1,)), ((), ())), preferred_element_type=F32)


def _dot_ta(a, b):
    return lax.dot_general(_bf(a), _bf(b), (((0,), (0,)), ((), ())), preferred_element_type=F32)


def _bdot(a, b):
    return lax.dot_general(_bf(a), _bf(b), (((2,), (1,)), ((0,), (0,))), preferred_element_type=F32)


def _bdot_nt(a, b):
    return lax.dot_general(_bf(a), _bf(b), (((2,), (2,)), ((0,), (0,))), preferred_element_type=F32)


def _bdot_tn(a, b):
    return lax.dot_general(_bf(a), _bf(b), (((1,), (1,)), ((0,), (0,))), preferred_element_type=F32)


def _dot01(m01, x):
    hi = x.astype(BF16)
    lo = (x - hi.astype(F32)).astype(BF16)
    return _dot(m01, hi) + _dot(m01, lo)


def _rstd(x):
    return lax.rsqrt(jnp.mean(x * x, axis=-1, keepdims=True) + EPS)


def _sigmoid(x):
    return jax.nn.sigmoid(x)


def _colsum(a):
    return jnp.sum(a, axis=0, keepdims=True)


def _stack_rows(rows):
    n = rows[0].shape[-1]
    rid = lax.broadcasted_iota(jnp.int32, (16, n), 0)
    out = jnp.zeros((16, n), F32)
    for i, r in enumerate(rows):
        out = jnp.where(rid == i, r, out)
    return out


def _head_map(fn, *arrs):
    outs = [fn(*[a[:, h * DH:(h + 1) * DH] for a in arrs]) for h in range(HEADS)]
    return jnp.concatenate(outs, axis=1)


def _gla_consts():
    r = np.arange(TM)[:, None]
    c = np.arange(TM)[None, :]
    same = (r // CHUNK) == (c // CHUNK)
    tril = same & (c <= r)
    triu = same & (c >= r)
    m = np.stack([tril, triu]).astype(np.float32)
    return jnp.asarray(m, BF16), jnp.asarray(m, F32)


def _pool_consts():
    r = np.arange(TM)[:, None]
    c = np.arange(TM)[None, :]
    same = (r // GRID_W) == (c // GRID_W)
    rp, cp = r % GRID_W, c % GRID_W
    bs, inv = [], []
    for w in POOL_WINDOWS:
        lo = np.clip(rp - w // 2, 0, GRID_W)
        hi = np.clip(rp - w // 2 + w, 0, GRID_W)
        bs.append(same & (cp >= lo) & (cp < hi))
        inv.append(1.0 / (hi - lo).astype(np.float32))
    b = np.stack(bs).astype(np.float32)
    bt = np.transpose(b, (0, 2, 1))
    return jnp.asarray(b, BF16), jnp.asarray(bt, BF16), jnp.asarray(np.stack(inv), F32)


def _mesh_pos():
    x, y, c = lax.axis_index("x"), lax.axis_index("y"), lax.axis_index("c")
    return x, y, c, 4 * x + 2 * y + c


def _peer(x, y, c, k):
    return (x ^ ((k >> 2) & 1), y ^ ((k >> 1) & 1), c ^ (k & 1))


def _small_gathers(refs, ssem, rsem):
    lb_r, ps_r, c_r, cctx_r, ada_r, adab_r, lb_o, ps_o, cg_o, mod_o, lb_out, ps_out, cg_out, mod0_o, mod1_o, modc_o = refs
    x, y, cc, idx = _mesh_pos()
    srcs = [lb_r, ps_r, c_r, mod_o.at[idx]]
    mine = [lb_o.at[idx], ps_o.at[idx], cg_o.at[idx], mod_o.at[idx]]

    def remote(a, k):
        return pltpu.make_async_remote_copy(src_ref=srcs[a], dst_ref=mine[a], send_sem=ssem.at[a, k], recv_sem=rsem.at[a, k],
                                            device_id=_peer(x, y, cc, k), device_id_type=MESH)

    first = [remote(a, k) for k in range(1, NDEV) for a in (2, 0, 1)]
    for cp in first:
        cp.start()
    lb_o[idx] = lb_r[...]
    ps_o[idx] = ps_r[...]
    cg_o[idx] = c_r[...]
    for k in range(1, NDEV):
        remote(2, k).wait_recv()
    rows = _stack_rows([cg_o[i] for i in range(NDEV)] + [cctx_r[...]])
    sc = rows * _sigmoid(rows)
    for l in range(2):
        mod_o[idx, l] = _dot(sc, ada_r[l])
    second = [remote(3, k) for k in range(1, NDEV)]
    for cp in second:
        cp.start()
    for k in range(1, NDEV):
        remote(3, k).wait_recv()

    def mod_rows(l, row):
        full = jnp.concatenate([mod_o[s, l, row, :] for s in range(NDEV)], axis=1) + adab_r[l:l + 1, :]
        return [full[:, j * D:(j + 1) * D] for j in range(3)]

    me = pl.ds(idx, 1)
    for out, parts in ((mod0_o, mod_rows(0, me)), (mod1_o, mod_rows(1, me)), (modc_o, mod_rows(0, slice(NDEV, NDEV + 1)))):
        for j in range(3):
            out[j:j + 1, :] = parts[j]
    for cp in first + second:
        cp.wait_send()
    for k in range(1, NDEV):
        for a in (0, 1):
            remote(a, k).wait_recv()
    lb_out[...] = lb_o[...]
    ps_out[...] = ps_o[...]
    cg_out[...] = cg_o[...]


def _gather_order(s, core):
    k = jnp.where(s == 2, 4, jnp.where(s == 4, 2, s))
    return k ^ jnp.where((core == 1) & (s >= 2) & (s <= 5), 6, 0)


GATHER_ISSUE = (1, 2, 4, 3, 5, 6, 7)
GATHER_ICI = (2, 4, 6)
GATHER_DIRECT = (1,) + GATHER_ICI
GLA_HB = 2
RS_SLOTS = 5


def _shard_of(kind, ref, i):
    if kind == "rows":
        return ref.at[pl.ds(pl.multiple_of(i * SH_ROWS, SH_ROWS), SH_ROWS), :]
    if kind == "major":
        return ref.at[i]
    assert kind == "grp"
    return ref.at[:, pl.ds(pl.multiple_of(i * SH_GRP, SH_GRP), SH_GRP), :]


def _gather_rider(step, n_steps, forward_at, kinds, srcs, outs, ssem, rsem, lsem):
    x, y, cc, idx = _mesh_pos()
    arrays = range(len(kinds))
    mine = [_shard_of(kinds[a], outs[a], idx) for a in arrays]

    def remote(a, k):
        return pltpu.make_async_remote_copy(src_ref=srcs[a], dst_ref=mine[a], send_sem=ssem.at[a, k], recv_sem=rsem.at[a, k],
                                            device_id=_peer(x, y, cc, k), device_id_type=MESH)

    def forward(a, k):
        blk = _shard_of(kinds[a], outs[a], idx ^ k)
        return pltpu.make_async_remote_copy(src_ref=blk, dst_ref=blk, send_sem=ssem.at[a, k ^ 1], recv_sem=rsem.at[a, k ^ 1],
                                            device_id=(x, y, 1 - cc), device_id_type=MESH)

    copies = [remote(a, k) for k in GATHER_DIRECT for a in arrays]
    passed = [forward(a, k) for k in GATHER_ICI for a in arrays]
    local = [pltpu.make_async_copy(srcs[a], mine[a], lsem.at[a]) for a in arrays]

    @pl.when(step == 0)
    def _():
        for cp in copies + local:
            cp.start()

    @pl.when(step == forward_at)
    def _():
        for k in GATHER_ICI:
            for a in arrays:
                remote(a, k).wait_recv()
                forward(a, k).start()

    def finish():
        @pl.when(step == n_steps - 1)
        def _():
            for cp in copies + passed:
                cp.wait_send()
            for a in arrays:
                remote(a, 1).wait_recv()
            for cp in passed:
                cp.wait_recv()
            for cp in local:
                cp.wait()

    return finish


def _scatter_rider(step, n_steps, kinds, grads, slots, ssem, rsem, lsem):
    x, y, cc, idx = _mesh_pos()
    arrays = range(len(kinds))
    dsts = [slots[a].at[idx] for a in arrays]

    def remote(a, k):
        px, py, pc = _peer(x, y, cc, k)
        return pltpu.make_async_remote_copy(src_ref=_shard_of(kinds[a], grads[a], 4 * px + 2 * py + pc), dst_ref=dsts[a],
                                            send_sem=ssem.at[a, k], recv_sem=rsem.at[a, k], device_id=(px, py, pc), device_id_type=MESH)

    copies = [remote(a, k) for k in GATHER_ISSUE for a in arrays]
    local = [pltpu.make_async_copy(_shard_of(kinds[a], grads[a], idx), dsts[a], lsem.at[a]) for a in arrays]

    @pl.when(step == 0)
    def _():
        for cp in copies + local:
            cp.start()

    def finish():
        @pl.when(step == n_steps - 1)
        def _():
            for cp in copies:
                cp.wait_send()
            for cp in copies:
                cp.wait_recv()
            for cp in local:
                cp.wait()

    return finish


def _rider_sems(n):
    return [pltpu.SemaphoreType.DMA((n, NDEV)), pltpu.SemaphoreType.DMA((n, NDEV)), pltpu.SemaphoreType.DMA((n,))]


def _scatter_rider2(step, n_steps, add_at, kinds, grads, slots, bufs, sems):
    x, y, cc, idx = _mesh_pos()
    sibling = (x, y, 1 - cc)
    arrays = range(len(kinds))
    psend, precv, isend, irecv, lown, sibsem, lself = sems

    def mine(a, i):
        return _shard_of(kinds[a], grads[a], i)

    def partial(a, p):
        return pltpu.make_async_remote_copy(src_ref=mine(a, idx ^ (2 * (p + 1)) ^ 1), dst_ref=bufs[a][1].at[p], send_sem=psend.at[a, p],
                                            recv_sem=precv.at[a, p], device_id=sibling, device_id_type=MESH)

    def load(a, p):
        return pltpu.make_async_copy(mine(a, idx ^ (2 * (p + 1))), bufs[a][0].at[p], lown.at[a, p])

    def chip_sum(a, p):
        return pltpu.make_async_remote_copy(src_ref=bufs[a][0].at[p], dst_ref=slots[a].at[2 + p], send_sem=isend.at[a, p],
                                            recv_sem=irecv.at[a, p], device_id=_peer(x, y, cc, 2 * (p + 1)), device_id_type=MESH)

    def to_sibling(a):
        return pltpu.make_async_remote_copy(src_ref=mine(a, idx ^ 1), dst_ref=slots[a].at[1], send_sem=sibsem.at[a, 0],
                                            recv_sem=sibsem.at[a, 1], device_id=sibling, device_id_type=MESH)

    def own(a):
        return pltpu.make_async_copy(mine(a, idx), slots[a].at[0], lself.at[a, 0])

    @pl.when(step == 0)
    def _():
        for a in arrays:
            for p in range(3):
                partial(a, p).start()
                load(a, p).start()
            to_sibling(a).start()
            own(a).start()

    @pl.when(step == add_at)
    def _():
        for a in arrays:
            for p in range(3):
                partial(a, p).wait_recv()
                load(a, p).wait()
                bufs[a][0][p] = (bufs[a][0][p].astype(F32) + bufs[a][1][p].astype(F32)).astype(BF16)
                chip_sum(a, p).start()

    def finish():
        @pl.when(step == n_steps - 1)
        def _():
            for a in arrays:
                for p in range(3):
                    partial(a, p).wait_send()
                    chip_sum(a, p).wait_send()
                    chip_sum(a, p).wait_recv()
                to_sibling(a).wait_send()
                to_sibling(a).wait_recv()
                own(a).wait()

    return finish


def _rider2_scratch(blocks):
    n = len(blocks)
    bufs = [pltpu.VMEM((3,) + tuple(b), BF16) for b in blocks for _ in range(2)]
    return bufs + [pltpu.SemaphoreType.DMA((n, 3)) for _ in range(5)] + [pltpu.SemaphoreType.DMA((n, 2)), pltpu.SemaphoreType.DMA((n, 1))]


def _rider2_split(refs, n):
    refs = list(refs)
    return [tuple(refs[2 * a:2 * a + 2]) for a in range(n)], tuple(refs[2 * n:2 * n + 7])


def _modulated(x, nw, shift, scale):
    r = _rstd(x)
    xn = x * r
    a = xn * nw
    return a * (1.0 + scale) + shift, r, xn, a


def _ctx_or_x(i, ctx_ref, x_ref):
    return jnp.where(i == 0, ctx_ref[...], x_ref[...])


def _f1_gather_matmul(idx1, ctx, x, nw, w_in, w_out, pw_in, pgrp, pw_out, lb_l, pscale, c, c_ctx, ada_w, ada_b):
    def body(idx_ref, ctx_ref, x_ref, nw_ref, win_r, wout_r, pwin_r, pgrp_r, pwout_r, lb_r, ps_r, c_r, cctx_r, ada_r, adab_r,
             g_ref, win_o, s_wout, s_pwin, s_pgrp, s_pwout, lb_o, ps_o, cg_o, mod0_o, mod1_o, modc_o,
             wslot, hx_sc, lb_g, ps_g, cg_g, mod_g, ssem, rsem, osem, dsem, sm_ssem, sm_rsem):
        del idx_ref
        s, i = pl.program_id(0), pl.program_id(1)
        x, y, cc, idx = _mesh_pos()
        k = _gather_order(s, cc)
        j = idx ^ k
        first = 4 - 2 * cc

        def remote(kk):
            return pltpu.make_async_remote_copy(src_ref=wslot.at[idx], dst_ref=wslot.at[idx], send_sem=ssem.at[kk], recv_sem=rsem.at[kk],
                                                device_id=_peer(x, y, cc, kk), device_id_type=MESH)

        def forward(kk):
            jj = idx ^ kk
            return pltpu.make_async_remote_copy(src_ref=wslot.at[jj], dst_ref=wslot.at[jj], send_sem=ssem.at[kk ^ 1],
                                                recv_sem=rsem.at[kk ^ 1], device_id=(x, y, 1 - cc), device_id_type=MESH)

        def relay(h):
            blk = wslot.at[idx ^ (4 >> h), pl.ds(h * (D // 2), D // 2), :]
            return pltpu.make_async_remote_copy(src_ref=blk, dst_ref=blk, send_sem=dsem.at[0, h], recv_sem=dsem.at[1, h],
                                                device_id=_peer(x, y, cc, 2 << h), device_id_type=MESH)

        def to_hbm(jj, kk):
            return pltpu.make_async_copy(wslot.at[jj], win_o.at[jj], osem.at[kk])

        @pl.when((s == 0) & (i == 0))
        def _():
            _small_gathers((lb_r, ps_r, c_r, cctx_r, ada_r, adab_r, lb_g, ps_g, cg_g, mod_g, lb_o, ps_o, cg_o, mod0_o, mod1_o, modc_o),
                           sm_ssem, sm_rsem)
            wslot[idx] = win_r[...].astype(BF16)
            remote(1).start()
            remote(first).start()
            s_wout[...] = wout_r[...].astype(BF16)
            s_pwin[...] = pwin_r[...].astype(BF16)
            s_pgrp[...] = pgrp_r[...].astype(BF16)
            s_pwout[...] = pwout_r[...].astype(BF16)

        @pl.when(s == 0)
        def _():
            shift = jnp.where(i == 0, modc_o[0:1, :], mod0_o[0:1, :])
            scale = jnp.where(i == 0, modc_o[1:2, :], mod0_o[1:2, :])
            hx, _, _, _ = _modulated(_ctx_or_x(i, ctx_ref, x_ref), nw_ref[...], shift, scale)
            hx_sc[i] = hx.astype(BF16)

        @pl.when((s == 2) & (i == 0))
        def _():
            remote(6 - first).start()

        @pl.when((s > 0) & (i == 0) & (k != 6))
        def _():
            remote(k).wait_recv()

            @pl.when((k & 1) == 0)
            def _():
                forward(k).start()

            for h in range(2):
                @pl.when(k == 4 >> h)
                def _():
                    relay(h).start()

        @pl.when((i == 0) & (k == 6))
        def _():
            for h in range(2):
                relay(h).wait_recv()
            forward(6).start()

        @pl.when(i == 0)
        def _():
            to_hbm(j, k).start()

        g_ref[...] = jnp.dot(hx_sc[i], wslot[j], preferred_element_type=F32)

        @pl.when((s == NDEV - 1) & (i == NT - 1))
        def _():
            for kk in (1, 2, 4):
                remote(kk).wait_send()
            for kk in GATHER_ICI:
                forward(kk).wait_send()
            for h in range(2):
                relay(h).wait_send()
            for kk in range(NDEV):
                to_hbm(idx ^ kk, kk).wait()

    grid_spec = pltpu.PrefetchScalarGridSpec(
        num_scalar_prefetch=1, grid=(NDEV, NT),
        in_specs=[VMEM_SPEC, pl.BlockSpec((TM, D), lambda s, i, ix: (jnp.where(s == 0, jnp.maximum(i - 1, 0), NTX - 1), 0))]
        + [VMEM_SPEC] * 12,
        out_specs=[pl.BlockSpec((TM, SH_WIN), lambda s, i, ix: (i, ix[0] ^ _gather_order(s, ix[0] & 1))), HBM_SPEC] + [VMEM_SPEC] * 10,
        scratch_shapes=[pltpu.VMEM((NDEV, D, SH_WIN), BF16), pltpu.VMEM((NT, TM, D), BF16),
                        pltpu.VMEM((NDEV, 2, DH), F32), pltpu.VMEM((NDEV, 1, DH), F32), pltpu.VMEM((NDEV, 1, D), F32),
                        pltpu.VMEM((NDEV, 2, 16, SH_ADA), F32),
                        pltpu.SemaphoreType.DMA((NDEV,)), pltpu.SemaphoreType.DMA((NDEV,)), pltpu.SemaphoreType.DMA((NDEV,)),
                        pltpu.SemaphoreType.DMA((2, 2)),
                        pltpu.SemaphoreType.DMA((4, NDEV)), pltpu.SemaphoreType.DMA((4, NDEV))])
    outs = (_sds((TT, WIN_COLS), F32), _sds((NDEV, D, SH_WIN), BF16),
            _sds((SH_ROWS, D), BF16), _sds((D, SH_PWIN), BF16), _sds((4, SH_GRP, PG), BF16), _sds((SH_ROWS, D), BF16),
            _sds((NDEV, 2, DH), F32), _sds((NDEV, 1, DH), F32), _sds((NDEV, 1, D), F32),
            _sds((3, D), F32), _sds((3, D), F32), _sds((3, D), F32))
    return pl.pallas_call(
        body, name="f1_gather_matmul", grid_spec=grid_spec, out_shape=outs,
        compiler_params=pltpu.CompilerParams(dimension_semantics=("arbitrary", "arbitrary"), vmem_limit_bytes=VMEM_LIMIT),
    )(idx1, ctx, x, nw, w_in, w_out, pw_in, pgrp, pw_out, lb_l, pscale, c, c_ctx, ada_w, ada_b)


def _gla_gates(pre, qpre, lbd, cum, rev):
    rows, n = pre.shape
    nch = rows // CHUNK
    sig = _sigmoid(pre)
    f = lbd + (1.0 - lbd) * sig
    k = 1.0 - f
    g = _dot01(cum, jnp.log(f))
    g3 = g.reshape(nch, CHUNK, n)
    last = 0 if rev else CHUNK - 1
    mid = CHUNK // 2 if rev else CHUNK // 2 - 1
    gl1, gm1 = g3[:, last:last + 1, :], g3[:, mid:mid + 1, :]

    def bc(a):
        return jnp.broadcast_to(a, g3.shape).reshape(rows, n)

    gm = bc(gm1)
    e_q, e_k = jnp.exp(g - gm), jnp.exp(gm - g)
    qsig = _sigmoid(qpre)
    qs = qpre * qsig * (DH ** -0.5)
    return dict(sig=sig, f=f, k=k, qsig=qsig, qs=qs, e_q=e_q, e_k=e_k,
                e_mid=[jnp.exp(gm1[ci]) for ci in range(nch)], e_rest=[jnp.exp(gl1[ci] - gm1[ci]) for ci in range(nch)])


def _put_heads(ref, lead, arr):
    for h in range(HEADS):
        ref[lead + (h,)] = arr[:, h * DH:(h + 1) * DH]


def _get_heads(ref, lead=()):
    return jnp.concatenate([ref[lead + (h,)] for h in range(HEADS)], axis=1)


def _gla_prep(g_all, lb, cum01, s_wout, s_pgrp):
    nch = TM // CHUNK

    def body(g_ref, lb_ref, cum_ref, swout_r, spgrp_r, p0_ref, p1_ref, v_ref, dec_ref, wout_o, pgrp_o, ssem, rsem, lsem):
        finish = _gather_rider(pl.program_id(0), NT, NT - 1, ("rows", "grp"), (swout_r, spgrp_r), (wout_o, pgrp_o), ssem, rsem, lsem)
        qpre = g_ref[:, 3 * E:4 * E]
        _put_heads(v_ref, (), g_ref[:, 2 * E:3 * E].astype(BF16))
        for d, p_ref in ((0, p0_ref), (1, p1_ref)):
            t = _gla_gates(g_ref[:, d * E:(d + 1) * E], qpre, lb_ref[d:d + 1, :], cum_ref[d], d == 1)
            _put_heads(p_ref, (0,), (t["qs"] * t["e_q"]).astype(BF16))
            _put_heads(p_ref, (1,), (t["k"] * t["e_k"]).astype(BF16))
            for ci in range(nch):
                dec_ref[d, 0, ci:ci + 1, :] = t["e_mid"][ci]
                dec_ref[d, 0, nch + ci:nch + ci + 1, :] = t["e_rest"][ci]
        finish()

    quad = pl.BlockSpec((2, HEADS, TM, DH), lambda i: (0, 0, i, 0))
    return pl.pallas_call(
        body, name="gla_prep", grid=(NT,),
        in_specs=[pl.BlockSpec((TM, 4 * E), lambda i: (i, 0)), VMEM_SPEC, VMEM_SPEC, HBM_SPEC, HBM_SPEC],
        out_specs=[quad, quad, pl.BlockSpec((HEADS, TM, DH), lambda i: (0, i, 0)), pl.BlockSpec((2, 1, 2 * nch, E), lambda i: (0, i, 0, 0)),
                   HBM_SPEC, HBM_SPEC],
        out_shape=(_sds((2, HEADS, TT, DH), BF16), _sds((2, HEADS, TT, DH), BF16), _sds((HEADS, TT, DH), BF16), _sds((2, NT, 2 * nch, E), F32),
                   _sds((E, D), BF16), _sds((4, PG, PG), BF16)),
        scratch_shapes=_rider_sems(2),
        compiler_params=pltpu.CompilerParams(dimension_semantics=("arbitrary",), vmem_limit_bytes=VMEM_LIMIT),
    )(g_all, lb, cum01, s_wout, s_pgrp)


def _scan_tile(i, rev):
    t = jnp.where(i == 0, 0, NT - i) if rev else i
    return t, pl.ds(pl.multiple_of(t * TM, TM), TM)


def _chunk_rows(dec_ref, lanes, cis, where):
    nch = TM // CHUNK

    def rows(off):
        return jnp.stack([dec_ref[d, where[d][0], off + ci:off + ci + 1, hh * DH:(hh + 1) * DH] for (d, hh), ci in zip(lanes, cis)])

    return rows(0), rows(nch)


def _gla_fwd(p0, p1, v_all, dec, mask01, s_pwin, s_pwout):
    n_steps = HEADS // GLA_HB

    def body(p0_ref, p1_ref, v_ref, dec_ref, msk_ref, spwin_r, spwout_r, o_ref, pwin_o, pwout_o, ob_sc, ssem, rsem, lsem):
        finish = _gather_rider(pl.program_id(0), n_steps, n_steps - 1, ("major", "rows"), (spwin_r, spwout_r), (pwin_o, pwout_o),
                               ssem, rsem, lsem)

        lanes = [(d, hh) for d in (0, 1) for hh in range(GLA_HB)]
        nch = TM // CHUNK

        def tile_body(i, st):
            where = [_scan_tile(i, d == 1) for d in (0, 1)]

            def stacked(fn):
                return jnp.stack([fn(d, hh, where[d][1]) for d, hh in lanes])

            qg, kg = [stacked(lambda d, hh, rows, ty=ty: (p1_ref if d else p0_ref)[ty, hh, rows, :]) for ty in range(2)]
            v = stacked(lambda d, hh, rows: v_ref[hh, rows, :])
            a = _bdot_nt(qg, kg) * jnp.stack([msk_ref[d] for d, _ in lanes])
            intra = _bdot(a, v)
            outs = [[None] * nch for _ in lanes]
            for n in range(nch):
                cis = [nch - 1 - n if d else n for d, _ in lanes]

                def chunk(arr):
                    return jnp.stack([arr[l, ci * CHUNK:(ci + 1) * CHUNK] for l, ci in enumerate(cis)])

                e_mid, e_rest = _chunk_rows(dec_ref, lanes, cis, where)
                inter = _bdot_nt(chunk(qg), st * e_mid)
                for l, ci in enumerate(cis):
                    outs[l][ci] = inter[l] + intra[l, ci * CHUNK:(ci + 1) * CHUNK]
                st = st * (e_mid * e_rest) + _bdot_tn(chunk(v), chunk(kg)) * e_rest
            for l, (d, hh) in enumerate(lanes):
                (ob_sc if d else o_ref)[hh, where[d][1], :] = jnp.concatenate(outs[l], axis=0)
            return st

        lax.fori_loop(0, NT, tile_body, jnp.zeros((len(lanes), DH, DH), F32))
        o_ref[...] += ob_sc[...]
        finish()

    quad = pl.BlockSpec((2, GLA_HB, TT, DH), lambda h: (0, h, 0, 0))
    head = pl.BlockSpec((GLA_HB, TT, DH), lambda h: (h, 0, 0))
    return pl.pallas_call(
        body, name="gla_fwd", grid=(n_steps,),
        in_specs=[quad, quad, head, pl.BlockSpec((2, NT, 8, GLA_HB * DH), lambda h: (0, 0, 0, h)),
                  pl.BlockSpec((2, TM, TM), lambda h: (0, 0, 0)), HBM_SPEC, HBM_SPEC],
        out_specs=[head, HBM_SPEC, HBM_SPEC],
        out_shape=(_sds((HEADS, TT, DH), F32), _sds((NDEV, D, SH_PWIN), BF16), _sds((E, D), BF16)),
        scratch_shapes=[pltpu.VMEM((GLA_HB, TT, DH), F32)] + _rider_sems(2),
        compiler_params=pltpu.CompilerParams(dimension_semantics=("arbitrary",), vmem_limit_bytes=VMEM_LIMIT),
    )(p0, p1, v_all, dec, mask01, s_pwin, s_pwout)


def _gated_norm(o, z, gw):
    r = _head_map(lambda oh: jnp.broadcast_to(_rstd(oh), oh.shape), o)
    on = o * r
    zs = _sigmoid(z)
    sz = z * zs
    return on * gw * sz, r, on, zs, sz


def _f3_out(o, g_all, x, gate, gw, wout):
    def body(o_ref, z_ref, x_ref, gate_ref, gw_ref, w_ref, x1_ref):
        og, _, _, _, _ = _gated_norm(_get_heads(o_ref), z_ref[...], gw_ref[...])
        x1_ref[...] = x_ref[...] + gate_ref[...] * _dot(og, w_ref[...])

    return pl.pallas_call(
        body, name="f3_out", grid=(NTX,),
        in_specs=[pl.BlockSpec((HEADS, TM, DH), lambda i: (0, i + 1, 0)), pl.BlockSpec((TM, E), lambda i: (i + 1, 4)),
                  pl.BlockSpec((TM, D), lambda i: (i, 0)), pl.BlockSpec((1, D), lambda i: (0, 0)),
                  pl.BlockSpec((1, E), lambda i: (0, 0)), pl.BlockSpec((E, D), lambda i: (0, 0))],
        out_specs=pl.BlockSpec((TM, D), lambda i: (i, 0)),
        out_shape=_sds((T, D), F32),
        compiler_params=pltpu.CompilerParams(dimension_semantics=("arbitrary",)),
    )(o, g_all, x, gate, gw, wout)


def _pool_layer(x1, tgt, mod1, nw1, fnw, pwin, pgrp, pscale, pwout, pb, pbt, pinv):
    def body(x_ref, t_ref, m_ref, nw_ref, fw_ref, pwin_ref, pgrp_ref, ps_ref, pwout_ref, pb_ref, pbt_ref, pinv_ref,
             dx_ref, gpwin_o, gpgrp_o, gpwout_o, dmod_o, gnw_o, gfw_o, gps_o, loss_o,
             a_pwin, a_pgrp, a_pwout):
        i = pl.program_id(0)

        @pl.when(i == 0)
        def _():
            for ref in (a_pwin, a_pgrp, a_pwout, dmod_o, gnw_o, gfw_o, gps_o, loss_o):
                ref[...] = jnp.zeros_like(ref)

        shift, scale, gate = m_ref[0:1, :], m_ref[1:2, :], m_ref[2:3, :]
        nw, fw, ps = nw_ref[...], fw_ref[...], ps_ref[...]
        x1 = x_ref[...]
        hx, r1, xn, a = _modulated(x1, nw, shift, scale)
        hxb = hx.astype(BF16)
        uz = jnp.concatenate([_dot(hxb, pwin_ref[j]) for j in range(NDEV)], axis=1)
        u, z = uz[:, :E], uz[:, E:]
        pooled, ys = [], []
        for g in range(4):
            ug = u[:, g * PG:(g + 1) * PG]
            pg = _dot01(pb_ref[g], ug) * pinv_ref[g] - ug
            pooled.append(pg.astype(BF16))
            ys.append(_dot(pooled[g], pgrp_ref[g]))
        ycat = jnp.concatenate(ys, axis=1)
        y = ycat * ps
        zs = _sigmoid(z)
        sz = z * zs
        p = (y * sz).astype(BF16)
        out = _dot(p, pwout_ref[...])
        x2 = x1 + gate * out
        r2 = _rstd(x2)
        xn2 = x2 * r2
        diff = xn2 * fw - t_ref[...]
        loss_o[...] += _colsum(diff * diff)
        dyf = diff * (1.0 / D)
        gfw_o[...] += _colsum(dyf * xn2)
        dxn2 = dyf * fw
        dx2 = r2 * (dxn2 - xn2 * jnp.mean(dxn2 * xn2, axis=-1, keepdims=True))
        dgate = _colsum(dx2 * out)
        dout = (dx2 * gate).astype(BF16)
        for j in range(4):
            cs = slice(j * PG, (j + 1) * PG)
            a_pwout[:, cs] += _dot_ta(p, dout[:, cs])
        dp = _dot_tb(dout, pwout_ref[...])
        dy = dp * sz
        dz = dp * y * (zs * (1.0 + z * (1.0 - zs)))
        gps_o[...] += _colsum(dy * ycat)
        dycat = dy * ps
        dus = []
        for g in range(4):
            dyg = dycat[:, g * PG:(g + 1) * PG].astype(BF16)
            a_pgrp[g] += _dot_ta(pooled[g], dyg)
            dpg = _dot_tb(dyg, pgrp_ref[g])
            dus.append(_dot01(pbt_ref[g], dpg * pinv_ref[g]) - dpg)
        duz = jnp.concatenate(dus + [dz], axis=1).astype(BF16)
        dhx = None
        for j in range(NDEV):
            dj = duz[:, j * SH_PWIN:(j + 1) * SH_PWIN]
            a_pwin[j] += _dot_ta(hxb, dj)
            part = _dot_tb(dj, pwin_ref[j])
            dhx = part if dhx is None else dhx + part
        dmod_o[0:1, :] += _colsum(dhx)
        dmod_o[1:2, :] += _colsum(dhx * a)
        dmod_o[2:3, :] += dgate
        da = dhx * (1.0 + scale)
        gnw_o[...] += _colsum(da * xn)
        dxn = da * nw
        dx_ref[...] = dx2 + r1 * (dxn - xn * jnp.mean(dxn * xn, axis=-1, keepdims=True))

        @pl.when(i == NTX - 1)
        def _():
            gpwin_o[...] = a_pwin[...].astype(BF16)
            gpgrp_o[...] = a_pgrp[...].astype(BF16)
            gpwout_o[...] = a_pwout[...].astype(BF16)

    tile = pl.BlockSpec((TM, D), lambda i: (i, 0))
    outs = (_sds((T, D), F32), _sds((NDEV, D, SH_PWIN), BF16), _sds((4, PG, PG), BF16), _sds((E, D), BF16),
            _sds((3, D), F32), _sds((1, D), F32), _sds((1, D), F32), _sds((1, E), F32), _sds((1, D), F32))
    return pl.pallas_call(
        body, name="pool_layer", grid=(NTX,),
        in_specs=[tile, tile] + [VMEM_SPEC] * 10,
        out_specs=[tile] + [VMEM_SPEC] * 8,
        out_shape=outs,
        scratch_shapes=[pltpu.VMEM((NDEV, D, SH_PWIN), F32), pltpu.VMEM((4, PG, PG), F32), pltpu.VMEM((E, D), F32)],
        compiler_params=pltpu.CompilerParams(dimension_semantics=("arbitrary",), vmem_limit_bytes=VMEM_LIMIT),
    )(x1, tgt, mod1, nw1, fnw, pwin, pgrp, pscale, pwout, pb, pbt, pinv)


def _b3_out_bwd(dx1, o, g_all, gate, gw, wout, gpwout):
    def body(dx_ref, o_ref, z_ref, gate_ref, gw_ref, w_ref, gpwout_r, do_ref, dz_ref, gw_o, dgate_o, ggw_o, rpwout_o,
             acc, *rider):
        i = pl.program_id(0)
        bufs, sems = _rider2_split(rider, 1)
        finish = _scatter_rider2(i, NT, 2, ("rows",), (gpwout_r,), (rpwout_o,), bufs, sems)

        @pl.when(i == 0)
        def _():
            acc[...] = jnp.zeros_like(acc)
            dgate_o[...] = jnp.zeros_like(dgate_o)
            ggw_o[...] = jnp.zeros_like(ggw_o)
            do_ref[...] = jnp.zeros_like(do_ref)
            dz_ref[...] = jnp.zeros_like(dz_ref)

        @pl.when(i > 0)
        def _():
            gw = gw_ref[...]
            z = z_ref[...]
            og, r, on, zs, sz = _gated_norm(_get_heads(o_ref), z, gw)
            ogb = og.astype(BF16)
            dx = dx_ref[...]
            dgate_o[...] += _colsum(dx * _dot(ogb, w_ref[...]))
            dy = (dx * gate_ref[...]).astype(BF16)
            for j in range(4):
                cs = slice(j * PG, (j + 1) * PG)
                acc[:, cs] += _dot_ta(ogb, dy[:, cs])
            dog = _dot_tb(dy, w_ref[...])
            dz_ref[...] = (dog * (on * gw) * (zs * (1.0 + z * (1.0 - zs)))).astype(BF16)
            dong = dog * sz
            ggw_o[...] += _colsum(dong * on)
            don = dong * gw
            do = _head_map(lambda dh, nh, rh: rh * (dh - nh * jnp.mean(dh * nh, axis=-1, keepdims=True)), don, on, r)
            _put_heads(do_ref, (), do.astype(BF16))

        @pl.when(i == NT - 1)
        def _():
            gw_o[...] = acc[...].astype(BF16)

        finish()

    prev = lambda i: (jnp.maximum(i - 1, 0), 0)
    heads = pl.BlockSpec((HEADS, TM, DH), lambda i: (0, i, 0))
    return pl.pallas_call(
        body, name="b3_out_bwd", grid=(NT,),
        in_specs=[pl.BlockSpec((TM, D), prev), heads, pl.BlockSpec((TM, E), lambda i: (i, 4)),
                  VMEM_SPEC, VMEM_SPEC, VMEM_SPEC, HBM_SPEC],
        out_specs=[heads, pl.BlockSpec((TM, E), lambda i: (i, 0)), VMEM_SPEC, VMEM_SPEC, VMEM_SPEC, HBM_SPEC],
        out_shape=(_sds((HEADS, TT, DH), BF16), _sds((TT, E), BF16), _sds((E, D), BF16), _sds((1, D), F32), _sds((1, E), F32),
                   _sds((RS_SLOTS, SH_ROWS, D), BF16)),
        scratch_shapes=[pltpu.VMEM((E, D), F32)] + _rider2_scratch([(SH_ROWS, D)]),
        compiler_params=pltpu.CompilerParams(dimension_semantics=("arbitrary",), vmem_limit_bytes=VMEM_LIMIT),
    )(dx1, o, g_all, gate, gw, wout, gpwout)


def _gla_bwd(p0, p1, v_all, dec, do, mask01, gpwin, gpgrp):
    nch = TM // CHUNK
    n_steps = HEADS // GLA_HB

    def body(p0_ref, p1_ref, v_ref, dec_ref, do_ref, msk_ref, gpwin_r, gpgrp_r, d0_ref, d1_ref, dv_ref, dgl_ref, rpwin_o, rpgrp_o,
             ss_sc, dv_sc, ssem, rsem, lsem, *rider):
        finish_grp = _scatter_rider(pl.program_id(0), n_steps, ("grp",), (gpgrp_r,), (rpgrp_o,), ssem, rsem, lsem)
        bufs, sems = _rider2_split(rider, 1)
        finish_win = _scatter_rider2(pl.program_id(0), n_steps, 1, ("major",), (gpwin_r,), (rpwin_o,), bufs, sems)

        lanes = [(d, hh) for d in (0, 1) for hh in range(GLA_HB)]
        zero = jnp.zeros((len(lanes), DH, DH), F32)
        dgl_ref[...] = jnp.zeros_like(dgl_ref)

        def p_of(d):
            return p1_ref if d else p0_ref

        def scan_step(i, n):
            where = [_scan_tile(i, d == 1) for d in (0, 1)]
            cis = [nch - 1 - n if d else n for d, _ in lanes]
            e_mid, e_rest = _chunk_rows(dec_ref, lanes, cis, where)

            def chunk(arr):
                return jnp.stack([arr[l, ci * CHUNK:(ci + 1) * CHUNK] for l, ci in enumerate(cis)])

            return where, cis, e_mid, e_rest, chunk

        def stacked(i, fn):
            where = [_scan_tile(i, d == 1) for d in (0, 1)]
            return jnp.stack([fn(d, hh, where[d][1]) for d, hh in lanes])

        def fwd_body(i, st):
            v = stacked(i, lambda d, hh, rows: v_ref[hh, rows, :])
            kg = stacked(i, lambda d, hh, rows: p_of(d)[1, hh, rows, :])
            for n in range(nch):
                _, _, e_mid, e_rest, chunk = scan_step(i, n)
                ss_sc[i * nch + n] = st
                st = st * (e_mid * e_rest) + _bdot_tn(chunk(v), chunk(kg)) * e_rest
            return st

        ss_sc[NT * nch] = lax.fori_loop(0, NT, fwd_body, zero)

        def bwd_body(ii, dst):
            i = NT - 1 - ii
            qg, kg = [stacked(i, lambda d, hh, rows, ty=ty: p_of(d)[ty, hh, rows, :]) for ty in range(2)]
            v = stacked(i, lambda d, hh, rows: v_ref[hh, rows, :])
            dob = stacked(i, lambda d, hh, rows: do_ref[hh, rows, :])
            msk = jnp.stack([msk_ref[d] for d, _ in lanes])
            a = (_bdot_nt(qg, kg) * msk).astype(BF16)
            da = (_bdot_nt(dob, v) * msk).astype(BF16)
            dqg = _bdot(da, kg)
            dkg = _bdot_tn(da, qg)
            dv_intra = _bdot_tn(a, dob)
            dv_l, dkg_l, dqg_l = ([[None] * nch for _ in lanes] for _ in range(3))
            for n in range(nch - 1, -1, -1):
                where, cis, e_mid, e_rest, chunk = scan_step(i, n)
                s_c, s_end = ss_sc[i * nch + n], ss_sc[i * nch + n + 1]
                dste = (dst * e_rest).astype(BF16)
                kg_c, v_c, dob_c = chunk(kg), chunk(v), chunk(dob)
                dv_c = chunk(dv_intra) + _bdot_nt(kg_c, dste)
                dkg_c = chunk(dkg) + _bdot(v_c, dste)
                dqg_c = chunk(dqg) + _bdot(dob_c, s_c * e_mid)
                dgl = jnp.sum(s_end * dst, axis=1, keepdims=True)
                for l, ((d, hh), ci) in enumerate(zip(lanes, cis)):
                    dv_l[l][ci], dkg_l[l][ci], dqg_l[l][ci] = dv_c[l], dkg_c[l], dqg_c[l]
                    dgl_ref[d, where[d][0], ci:ci + 1, hh * DH:(hh + 1) * DH] = dgl[l]
                dst = dst * (e_mid * e_rest) + _bdot_tn(dob_c, chunk(qg)) * e_mid
            where = [_scan_tile(i, d == 1) for d in (0, 1)]
            for l, (d, hh) in enumerate(lanes):
                rows = where[d][1]
                d_ref = d1_ref if d else d0_ref
                d_ref[0, hh, rows, :] = jnp.concatenate(dqg_l[l], axis=0).astype(BF16)
                d_ref[1, hh, rows, :] = jnp.concatenate(dkg_l[l], axis=0).astype(BF16)
                dv_sc[d, hh, rows, :] = jnp.concatenate(dv_l[l], axis=0).astype(BF16)
            return dst

        lax.fori_loop(0, NT, bwd_body, zero)
        dv_ref[...] = (dv_sc[0].astype(F32) + dv_sc[1].astype(F32)).astype(BF16)
        finish_grp()
        finish_win()

    quad = pl.BlockSpec((2, GLA_HB, TT, DH), lambda h: (0, h, 0, 0))
    col = pl.BlockSpec((GLA_HB, TT, DH), lambda h: (h, 0, 0))
    chunkv = pl.BlockSpec((2, NT, 8, GLA_HB * DH), lambda h: (0, 0, 0, h))
    outs = (_sds((2, HEADS, TT, DH), BF16), _sds((2, HEADS, TT, DH), BF16), _sds((HEADS, TT, DH), BF16), _sds((2, NT, 8, E), F32),
            _sds((RS_SLOTS, D, SH_PWIN), BF16), _sds((NDEV, 4, SH_GRP, PG), BF16))
    return pl.pallas_call(
        body, name="gla_bwd", grid=(n_steps,),
        in_specs=[quad, quad, col, chunkv, col, pl.BlockSpec((2, TM, TM), lambda h: (0, 0, 0)), HBM_SPEC, HBM_SPEC],
        out_specs=[quad, quad, col, chunkv, HBM_SPEC, HBM_SPEC],
        out_shape=outs,
        scratch_shapes=[pltpu.VMEM((NT * nch + 1, 2 * GLA_HB, DH, DH), F32), pltpu.VMEM((2, GLA_HB, TT, DH), BF16)] + _rider_sems(1)
        + _rider2_scratch([(D, SH_PWIN)]),
        compiler_params=pltpu.CompilerParams(dimension_semantics=("arbitrary",), vmem_limit_bytes=VMEM_LIMIT),
    )(p0, p1, v_all, dec, do, mask01, gpwin, gpgrp)


TMB = 128


def _gla_post_bwd(g_all, d0, d1, dgl, dv, dz, lb, cum01, gwout):
    nch = TMB // CHUNK

    def body(g_ref, d0_ref, d1_ref, dgl_ref, dv_ref, dz_ref, lb_ref, cum_ref, gwout_r, dg_ref, dlb_ref, rwout_o, *rider):
        i = pl.program_id(0)
        bufs, sems = _rider2_split(rider, 1)
        finish = _scatter_rider2(i, TT // TMB, 2, ("rows",), (gwout_r,), (rwout_o,), bufs, sems)

        @pl.when(i == 0)
        def _():
            dlb_ref[...] = jnp.zeros_like(dlb_ref)

        half = i & 1
        qpre = g_ref[:, 3 * E:4 * E]
        dqs_sum = None
        dpre = []
        for d, d_ref in ((0, d0_ref), (1, d1_ref)):
            rev = d == 1
            lbd = lb_ref[d:d + 1, :]
            t = _gla_gates(g_ref[:, d * E:(d + 1) * E], qpre, lbd, cum_ref[d, :TMB, :TMB], rev)
            dqs = _get_heads(d_ref, (0,)).astype(F32) * t["e_q"]
            dk = _get_heads(d_ref, (1,)).astype(F32) * t["e_k"]
            dg = t["qs"] * dqs - t["k"] * dk
            dgl8 = dgl_ref[d, 0]
            dgl_rows = [jnp.where(half == 0, dgl8[ci:ci + 1, :], dgl8[nch + ci:nch + ci + 1, :]) for ci in range(nch)]
            dgl_b = jnp.concatenate([jnp.broadcast_to(dgl_rows[ci], (CHUNK, E)) for ci in range(nch)], axis=0)
            pos = lax.broadcasted_iota(jnp.int32, (TMB, E), 0) & (CHUNK - 1)
            dg = dg + jnp.where(pos == (0 if rev else CHUNK - 1), dgl_b, 0.0)
            dlf = _dot01(cum_ref[1 - d, :TMB, :TMB], dg)
            df = dlf / t["f"] - dk
            sig = t["sig"]
            dpre.append((df * (1.0 - lbd) * sig * (1.0 - sig)).astype(BF16))
            dlb_ref[d:d + 1, :] += _colsum(df * (1.0 - sig))
            dqs_sum = dqs if dqs_sum is None else dqs_sum + dqs
            qsig = t["qsig"]
        dqpre = dqs_sum * (DH ** -0.5) * (qsig * (1.0 + qpre * (1.0 - qsig)))
        dg_ref[...] = jnp.concatenate([dpre[0], dpre[1], _get_heads(dv_ref), dqpre.astype(BF16), dz_ref[...]], axis=1)
        finish()

    quad = pl.BlockSpec((2, HEADS, TMB, DH), lambda i: (0, 0, i, 0))
    tile = pl.BlockSpec((TMB, E), lambda i: (i, 0))
    return pl.pallas_call(
        body, name="gla_post_bwd", grid=(TT // TMB,),
        in_specs=[pl.BlockSpec((TMB, 4 * E), lambda i: (i, 0)), quad, quad,
                  pl.BlockSpec((2, 1, 8, E), lambda i: (0, i // 2, 0, 0)), pl.BlockSpec((HEADS, TMB, DH), lambda i: (0, i, 0)), tile,
                  VMEM_SPEC, VMEM_SPEC, HBM_SPEC],
        out_specs=[pl.BlockSpec((TMB, WIN_COLS), lambda i: (i, 0)), VMEM_SPEC, HBM_SPEC],
        out_shape=(_sds((TT, WIN_COLS), BF16), _sds((2, E), F32), _sds((RS_SLOTS, SH_ROWS, D), BF16)),
        scratch_shapes=_rider2_scratch([(SH_ROWS, D)]),
        compiler_params=pltpu.CompilerParams(dimension_semantics=("arbitrary",), vmem_limit_bytes=VMEM_LIMIT),
    )(g_all, d0, d1, dgl, dv, dz, lb, cum01, gwout)


WIN_SLOTS = 4


def _scatter_order(s, core):
    return (NDEV - 1 - s) ^ jnp.where((s >= 2) & (s <= 5) & ((s & 1) == core), 6, 0)


def _b1_in_bwd(idx1, ctx, x, dx1, dg, nw, msel, win):
    last_s = NDEV - 1
    half = D // 2

    def body(idx_ref, ctx_ref, x_ref, dx1_ref, dg_ref, nw_ref, m_ref, w_ref, gx_ref, rwin_o, dmx_o, dmc_o, gnw_o,
             hx_sc, dhx_sc, acc, sbuf, pbuf, rbuf, psend, precv, isend, irecv, dsend, drecv, sibsem, lsem):
        del idx_ref
        s, i = pl.program_id(0), pl.program_id(1)
        x, y, cc, idx = _mesh_pos()
        shift, scale = m_ref[0, 0:1, :], m_ref[0, 1:2, :]
        sibling = (x, y, 1 - cc)

        def partial(p):
            return pltpu.make_async_remote_copy(src_ref=sbuf.at[0], dst_ref=pbuf.at[p], send_sem=psend.at[p], recv_sem=precv.at[p],
                                                device_id=sibling, device_id_type=MESH)

        def chip_sum(p):
            return pltpu.make_async_remote_copy(src_ref=sbuf.at[1], dst_ref=rwin_o.at[2 + p], send_sem=isend.at[p], recv_sem=irecv.at[p],
                                                device_id=_peer(x, y, cc, 2 * (p + 1)), device_id_type=MESH)

        def relay(h):
            return pltpu.make_async_remote_copy(src_ref=sbuf.at[1, pl.ds(h * half, half), :], dst_ref=rbuf.at[h], send_sem=dsend.at[h],
                                                recv_sem=drecv.at[h], device_id=_peer(x, y, cc, 2 * (h + 1)), device_id_type=MESH)

        to_sibling = pltpu.make_async_remote_copy(src_ref=sbuf.at[0], dst_ref=rwin_o.at[1], send_sem=sibsem.at[0], recv_sem=sibsem.at[1],
                                                  device_id=sibling, device_id_type=MESH)
        own = pltpu.make_async_copy(sbuf.at[1], rwin_o.at[0], lsem)

        @pl.when((s == 0) & (i == 0))
        def _():
            for ref in (dmx_o, dmc_o, gnw_o):
                ref[...] = jnp.zeros_like(ref)

        @pl.when(s == 0)
        def _():
            hx, _, _, _ = _modulated(_ctx_or_x(i, ctx_ref, x_ref), nw_ref[...], shift, scale)
            hx_sc[i] = hx.astype(BF16)

        @pl.when(i == 0)
        def _():
            acc[...] = jnp.zeros_like(acc)

        dgb = dg_ref[...]
        hxb = hx_sc[i]
        for r in range(0, D, 256):
            acc[r:r + 256, :] += _dot_ta(hxb[:, r:r + 256], dgb)
        part = _dot_tb(dgb, w_ref[0])

        @pl.when(s == 0)
        def _():
            dhx_sc[i] = part

        @pl.when(s > 0)
        def _():
            dhx_sc[i] += part

        done = i == NT - 1

        def hand_over(p, before):
            before.wait_send()
            sbuf[0] = acc[...].astype(BF16)
            partial(p).start()

        def send_chip_sum(p, before):
            for cp in before:
                cp.wait_send()
            partial(p).wait_recv()
            sbuf[1] = (acc[...] + pbuf[p].astype(F32)).astype(BF16)
            h = 1 - p
            rows = pl.ds(h * half, half)
            relay(h).wait_recv()
            sbuf[1, rows, :] = (acc[rows, :] + pbuf[p, rows, :].astype(F32) + rbuf[h].astype(F32)).astype(BF16)
            chip_sum(p).start()

        @pl.when(done & (s == 0))
        def _():
            sbuf[0] = acc[...].astype(BF16)
            partial(2).start()

        @pl.when(done & (s == 1))
        def _():
            partial(2).wait_recv()
            sbuf[1] = (acc[...] + pbuf[2].astype(F32)).astype(BF16)
            for h in range(2):
                relay(h).start()

        for core in range(2):
            @pl.when(done & (cc == core) & (s == 2))
            def _(core=core):
                hand_over(core, partial(2))

            @pl.when(done & (cc == core) & (s == 3))
            def _(core=core):
                send_chip_sum(1 - core, [relay(0), relay(1)])

            @pl.when(done & (cc == core) & (s == 4))
            def _(core=core):
                hand_over(1 - core, partial(core))

            @pl.when(done & (cc == core) & (s == 5))
            def _(core=core):
                send_chip_sum(core, [chip_sum(1 - core)])

            @pl.when(done & (cc == core) & (s == last_s - 1))
            def _(core=core):
                partial(1 - core).wait_send()
                sbuf[0] = acc[...].astype(BF16)
                to_sibling.start()

            @pl.when(done & (cc == core) & (s == last_s))
            def _(core=core):
                chip_sum(core).wait_send()
                sbuf[1] = acc[...].astype(BF16)
                own.start()

        @pl.when(s == last_s)
        def _():
            nw = nw_ref[...]
            _, r, xn, a = _modulated(_ctx_or_x(i, ctx_ref, x_ref), nw, shift, scale)
            dhx = dhx_sc[i]
            dsh, dsc = _colsum(dhx), _colsum(dhx * a)
            da = dhx * (1.0 + scale)
            gnw_o[...] += _colsum(da * xn)
            dxn = da * nw
            gx_ref[...] = dx1_ref[...] + r * (dxn - xn * jnp.mean(dxn * xn, axis=-1, keepdims=True))

            @pl.when(i == 0)
            def _():
                dmc_o[0:1, :] += dsh
                dmc_o[1:2, :] += dsc

            @pl.when(i > 0)
            def _():
                dmx_o[0:1, :] += dsh
                dmx_o[1:2, :] += dsc

        @pl.when((i == NT - 1) & (s == last_s))
        def _():
            to_sibling.wait_send()
            to_sibling.wait_recv()
            for p in range(2):
                chip_sum(p).wait_recv()
            own.wait()

    grid_spec = pltpu.PrefetchScalarGridSpec(
        num_scalar_prefetch=1, grid=(NDEV, NT),
        in_specs=[VMEM_SPEC,
                  pl.BlockSpec((TM, D), lambda s, i, ix: (jnp.where((s == 0) | (s == last_s), jnp.maximum(i - 1, 0), NTX - 1), 0)),
                  pl.BlockSpec((TM, D), lambda s, i, ix: (jnp.where(s == last_s, jnp.maximum(i - 1, 0), 0), 0)),
                  pl.BlockSpec((TM, SH_WIN), lambda s, i, ix: (i, ix[0] ^ _scatter_order(s, ix[0] & 1))), VMEM_SPEC,
                  pl.BlockSpec((1, 2, D), lambda s, i, ix: (jnp.minimum(i, 1), 0, 0)),
                  pl.BlockSpec((1, D, SH_WIN), lambda s, i, ix: (ix[0] ^ _scatter_order(s, ix[0] & 1), 0, 0))],
        out_specs=[pl.BlockSpec((TM, D), lambda s, i, ix: (jnp.where(s == last_s, jnp.maximum(i - 1, 0), 0), 0)),
                   HBM_SPEC, VMEM_SPEC, VMEM_SPEC, VMEM_SPEC],
        scratch_shapes=[pltpu.VMEM((NT, TM, D), BF16), pltpu.VMEM((NT, TM, D), F32), pltpu.VMEM((D, SH_WIN), F32),
                        pltpu.VMEM((2, D, SH_WIN), BF16), pltpu.VMEM((3, D, SH_WIN), BF16), pltpu.VMEM((2, half, SH_WIN), BF16),
                        pltpu.SemaphoreType.DMA((3,)), pltpu.SemaphoreType.DMA((3,)), pltpu.SemaphoreType.DMA((2,)),
                        pltpu.SemaphoreType.DMA((2,)), pltpu.SemaphoreType.DMA((2,)), pltpu.SemaphoreType.DMA((2,)),
                        pltpu.SemaphoreType.DMA((2,)), pltpu.SemaphoreType.DMA])
    return pl.pallas_call(
        body, name="b1_in_bwd", grid_spec=grid_spec,
        out_shape=(_sds((T, D), F32), _sds((WIN_SLOTS, D, SH_WIN), BF16), _sds((2, D), F32), _sds((2, D), F32), _sds((1, D), F32)),
        compiler_params=pltpu.CompilerParams(dimension_semantics=("arbitrary", "arbitrary"), vmem_limit_bytes=VMEM_LIMIT),
    )(idx1, ctx, x, dx1, dg, nw, msel, win)


def _reduce_small(pd, pv, cg, c_ctx, ada_w0):
    n_arr = 3

    def body(pd_r, pv_r, cg_r, cctx_r, ada_r, gada_o, gadab_o, gcctx_o, pvsum_o, loss_o,
             pd_all, pv_all, dsc_all, dsc_mine, ssem, rsem):
        x, y, cc, idx = _mesh_pos()
        srcs = [pd_r, pv_r, dsc_mine]
        dsts = [pd_all.at[idx], pv_all.at[idx], dsc_all.at[idx]]

        def remote(a, k):
            return pltpu.make_async_remote_copy(src_ref=srcs[a], dst_ref=dsts[a], send_sem=ssem.at[a, k], recv_sem=rsem.at[a, k],
                                                device_id=_peer(x, y, cc, k), device_id_type=MESH)

        first = [remote(a, k) for k in range(1, NDEV) for a in (0, 1)]
        for cp in first:
            cp.start()
        pd_all[idx] = pd_r[...]
        pv_all[idx] = pv_r[...]
        for k in range(1, NDEV):
            remote(0, k).wait_recv()
            remote(1, k).wait_recv()
        mine = [pd_all[s, :, pl.ds(idx, 1), :] for s in range(NDEV)]
        dmc = functools.reduce(lambda u, v: u + v, [m[2] for m in mine])
        rows = _stack_rows([cg_r[i] for i in range(NDEV)] + [cctx_r[...]])
        sc = (rows * _sigmoid(rows)).astype(BF16)
        gada_o[0] = _dot_ta(sc, _stack_rows([m[0] for m in mine] + [dmc]))
        gada_o[1] = _dot_ta(sc, _stack_rows([m[1] for m in mine]))
        dsc_mine[...] = _dot_tb(jnp.broadcast_to(dmc, (8, SH_ADA)), ada_r[...])[0:1, :]
        dsc_all[idx] = dsc_mine[...]
        second = [remote(2, k) for k in range(1, NDEV)]
        for cp in second:
            cp.start()
        tot = [functools.reduce(lambda u, v: u + v, [pd_all[s, l] for s in range(NDEV)]) for l in range(3)]
        gadab_o[0] = tot[0] + tot[2]
        gadab_o[1] = tot[1]
        pvs = functools.reduce(lambda u, v: u + v, [pv_all[s] for s in range(NDEV)])
        pvsum_o[...] = pvs
        loss_o[...] = jnp.broadcast_to(jnp.sum(pvs[:, PV_LOSS:PV_LOSS + D], axis=-1, keepdims=True) * (0.5 / D), (1, 128))
        for k in range(1, NDEV):
            remote(2, k).wait_recv()
        dsc = functools.reduce(lambda u, v: u + v, [dsc_all[s] for s in range(NDEV)])
        cx = cctx_r[...]
        sx = _sigmoid(cx)
        gcctx_o[...] = dsc * (sx * (1.0 + cx * (1.0 - sx)))
        for cp in first + second:
            cp.wait_send()

    outs = (_sds((2, D, SH_ADA), F32), _sds((2, NDEV, SH_ADA), F32), _sds((1, D), F32), _sds((1, PV_LEN), F32), _sds((1, 128), F32))
    return pl.pallas_call(
        body, name="reduce_small", out_shape=outs,
        in_specs=[VMEM_SPEC] * 5, out_specs=[VMEM_SPEC] * 5,
        scratch_shapes=[
            pltpu.VMEM((NDEV, 3, NDEV, SH_ADA), F32), pltpu.VMEM((NDEV, 1, PV_LEN), F32), pltpu.VMEM((NDEV, 1, D), F32),
            pltpu.VMEM((1, D), F32),
            pltpu.SemaphoreType.DMA((n_arr, NDEV)), pltpu.SemaphoreType.DMA((n_arr, NDEV)),
        ],
        compiler_params=pltpu.CompilerParams(vmem_limit_bytes=VMEM_LIMIT),
    )(pd, pv, cg, c_ctx, ada_w0)


PV_NW, PV_GNORM, PV_FINAL, PV_LB, PV_PSCALE, PV_LOSS, PV_LEN = 0, 2 * D, 3 * D, 4 * D, 6 * D, 7 * D, 8 * D


def _adamw(w, g, m, v):
    m = ADAM_B1 * m + (1.0 - ADAM_B1) * g
    v = ADAM_B2 * v + (1.0 - ADAM_B2) * (g * g)
    m_hat = m / (1.0 - ADAM_B1 ** ADAM_STEP)
    v_hat = v / (1.0 - ADAM_B2 ** ADAM_STEP)
    delta = -ADAM_LR * (m_hat / (jnp.sqrt(v_hat) + ADAM_EPS) + ADAM_WD * w)
    return delta, m, v


ADAM_STEPS = 8


def _adam_all(sharded, dense, small, lb_idx, lbv):
    ns, nd, nsm = len(sharded), len(dense), len(small)

    def body(*refs):
        it = iter(refs)
        sh_in = [[next(it) for _ in range(4)] for _ in range(ns)]
        de_in = [[next(it) for _ in range(4)] for _ in range(nd)]
        sm_in = [[next(it) for _ in range(4)] for _ in range(nsm)]
        lb_r = next(it)
        sh_out = [[next(it) for _ in range(4)] for _ in range(ns)]
        de_out = [[next(it) for _ in range(3)] for _ in range(nd)]
        sm_out = [[next(it) for _ in range(4)] for _ in range(nsm)]
        for (p, w, m, v), outs in zip(sh_in, sh_out):
            g = p[0].astype(F32)
            for s in range(1, p.shape[0]):
                g = g + p[s].astype(F32)
            d, mn, vn = _adamw(w[...], g, m[...], v[...])
            outs[0][...], outs[1][...], outs[2][...], outs[3][...] = g, d, mn, vn
        for (g, w, m, v), outs in zip(de_in, de_out):
            d, mn, vn = _adamw(w[...], g[...], m[...], v[...])
            outs[0][...], outs[1][...], outs[2][...] = d, mn, vn

        @pl.when(pl.program_id(0) == 0)
        def _():
            for j, ((g, w, m, v), outs) in enumerate(zip(sm_in, sm_out)):
                gj = g[...]
                if j == lb_idx:
                    gj = gj * lb_r[...] * (1.0 - lb_r[...])
                d, mn, vn = _adamw(w[...], gj, m[...], v[...])
                outs[0][...], outs[1][...], outs[2][...], outs[3][...] = gj, d, mn, vn

    def tile(a):
        return pl.BlockSpec((a.shape[0] // ADAM_STEPS, a.shape[1]), lambda i: (i, 0))

    in_specs, out_specs, out_shape, args = [], [], [], []
    for p, w, m, v in sharded:
        in_specs += [pl.BlockSpec((p.shape[0], p.shape[1] // ADAM_STEPS, p.shape[2]), lambda i: (0, i, 0))] + [tile(w)] * 3
        args += [p, w, m, v]
    for g, w, m, v in dense:
        in_specs += [tile(w)] * 4
        args += [g, w, m, v]
    for g, w, m, v in small:
        in_specs += [VMEM_SPEC] * 4
        args += [g, w, m, v]
    in_specs.append(VMEM_SPEC)
    args.append(lbv)
    for _, w, _, _ in sharded:
        out_specs += [tile(w)] * 4
        out_shape += [_sds(w.shape, F32)] * 4
    for _, w, _, _ in dense:
        out_specs += [tile(w)] * 3
        out_shape += [_sds(w.shape, F32)] * 3
    for _, w, _, _ in small:
        out_specs += [VMEM_SPEC] * 4
        out_shape += [_sds(w.shape, F32)] * 4
    res = pl.pallas_call(body, name="adam_all", grid=(ADAM_STEPS,), in_specs=in_specs, out_specs=out_specs, out_shape=tuple(out_shape),
                         compiler_params=pltpu.CompilerParams(dimension_semantics=("arbitrary",), vmem_limit_bytes=VMEM_LIMIT))(*args)
    it = iter(res)
    return ([tuple(next(it) for _ in range(4)) for _ in range(ns)], [tuple(next(it) for _ in range(3)) for _ in range(nd)],
            [tuple(next(it) for _ in range(4)) for _ in range(nsm)])


def kernel(x, c, ctx, c_ctx, ada_w, ada_b, norm_w, hgrn_w_in, hgrn_lb_logits, hgrn_gnorm_w, hgrn_w_out, pool_w_in, pool_w_grp, pool_scale, pool_w_out, final_norm_w, loss_target, m_c_ctx, m_ada_w, m_ada_b, m_norm_w, m_hgrn_w_in, m_hgrn_lb_logits, m_hgrn_gnorm_w, m_hgrn_w_out, m_pool_w_in, m_pool_w_grp, m_pool_scale, m_pool_w_out, m_final_norm_w, v_c_ctx, v_ada_w, v_ada_b, v_norm_w, v_hgrn_w_in, v_hgrn_lb_logits, v_hgrn_gnorm_w, v_hgrn_w_out, v_pool_w_in, v_pool_w_grp, v_pool_scale, v_pool_w_out, v_final_norm_w):
    idx = 4 * lax.axis_index("x") + 2 * lax.axis_index("y") + lax.axis_index("c")
    cctx2 = c_ctx.reshape(1, D)
    cum01, mask01 = _gla_consts()
    pb, pbt, pinv = _pool_consts()

    idx1 = idx.reshape(1).astype(jnp.int32)
    nw0, nw1 = norm_w[0:1], norm_w[1:2]
    fnw = final_norm_w.reshape(1, D)
    g_all, win, s_wout, s_pwin, s_pgrp, s_pwout, lbl_g, ps_g, cg, mod0, mod1, modc = _f1_gather_matmul(
        idx1, ctx[0], x[0], nw0, hgrn_w_in[0], hgrn_w_out[0], pool_w_in[0], pool_w_grp[0], pool_w_out[0], hgrn_lb_logits[0],
        pool_scale, c, cctx2, ada_w, ada_b)
    lb = jax.nn.sigmoid(jnp.transpose(lbl_g, (1, 0, 2)).reshape(2, E))
    pscale = ps_g.reshape(1, E)
    msel = jnp.stack([modc[:2], mod0[:2]])
    p0, p1, v_all, dec, wout, pgrp = _gla_prep(g_all, lb, cum01, s_wout, s_pgrp)
    o, pwin, pwout = _gla_fwd(p0, p1, v_all, dec, mask01, s_pwin, s_pwout)
    x1 = _f3_out(o, g_all, x[0], mod0[2:3], hgrn_gnorm_w, wout)
    dx1, gpwin, gpgrp, gpwout, dmod1, gnw1, gfw, gps, lossv = _pool_layer(
        x1, loss_target[0], mod1, nw1, fnw, pwin, pgrp, pscale, pwout, pb, pbt, pinv)
    do, dz, gwout, dgate0, ggw, rpwout = _b3_out_bwd(dx1, o, g_all, mod0[2:3], hgrn_gnorm_w, wout, gpwout)
    d0, d1, dv, dgl, rpwin, rpgrp = _gla_bwd(p0, p1, v_all, dec, do, mask01, gpwin, gpgrp)
    dg, dlb, rwout = _gla_post_bwd(g_all, d0, d1, dgl, dv, dz, lb, cum01, gwout)
    grad_x, rwin, dmx, dmc, gnw0 = _b1_in_bwd(idx1, ctx[0], x[0], dx1, dg, nw0, msel, win)

    dmod0 = jnp.concatenate([dmx, dgate0], axis=0)
    dmodc = jnp.concatenate([dmc, jnp.zeros((1, D), F32)], axis=0)
    pd = jnp.stack([dmod0, dmod1, dmodc]).reshape(3, NDEV, SH_ADA)
    pv = jnp.concatenate([gnw0, gnw1, ggw, gfw, dlb.reshape(1, 2 * E), gps, lossv], axis=1)
    g_ada, g_adab, g_cctx, pvsum, loss128 = _reduce_small(pd, pv, cg, cctx2, ada_w[0])

    g2 = (4 * SH_GRP, PG)
    sharded_names = ["hgrn_w_in", "hgrn_w_out", "pool_w_in", "pool_w_grp", "pool_w_out"]
    sharded = [(rwin, hgrn_w_in[0], m_hgrn_w_in[0], v_hgrn_w_in[0]),
               (rwout, hgrn_w_out[0], m_hgrn_w_out[0], v_hgrn_w_out[0]),
               (rpwin, pool_w_in[0], m_pool_w_in[0], v_pool_w_in[0]),
               (rpgrp.reshape((NDEV,) + g2), pool_w_grp[0].reshape(g2), m_pool_w_grp[0].reshape(g2), v_pool_w_grp[0].reshape(g2)),
               (rpwout, pool_w_out[0], m_pool_w_out[0], v_pool_w_out[0])]
    a2 = (2 * D, SH_ADA)
    g_ada2 = g_ada.reshape(a2)
    dense = [(g_ada2, ada_w.reshape(a2), m_ada_w.reshape(a2), v_ada_w.reshape(a2))]
    lb_me = lax.dynamic_slice_in_dim(lb, idx * DH, DH, axis=1)
    small_names = ["c_ctx", "ada_b", "norm_w", "hgrn_lb_logits", "hgrn_gnorm_w", "pool_scale", "final_norm_w"]
    small = [(g_cctx, cctx2, m_c_ctx.reshape(1, D), v_c_ctx.reshape(1, D)),
             (g_adab.reshape(2, 3 * D), ada_b, m_ada_b, v_ada_b),
             (pvsum[:, PV_NW:PV_NW + 2 * D].reshape(2, D), norm_w, m_norm_w, v_norm_w),
             (lax.dynamic_slice_in_dim(pvsum[:, PV_LB:PV_LB + 2 * E].reshape(2, E), idx * DH, DH, axis=1),
              hgrn_lb_logits[0], m_hgrn_lb_logits[0], v_hgrn_lb_logits[0]),
             (pvsum[:, PV_GNORM:PV_GNORM + E], hgrn_gnorm_w, m_hgrn_gnorm_w, v_hgrn_gnorm_w),
             (lax.dynamic_slice_in_dim(pvsum[:, PV_PSCALE:PV_PSCALE + E], idx * DH, DH, axis=1), pool_scale, m_pool_scale, v_pool_scale),
             (pvsum[:, PV_FINAL:PV_FINAL + D], fnw, m_final_norm_w.reshape(1, D), v_final_norm_w.reshape(1, D))]
    r_sharded, r_dense, r_small = _adam_all(sharded, dense, small, 3, lb_me)
    out = dict(zip(sharded_names, r_sharded))
    out["ada_w"] = (g_ada2,) + r_dense[0]
    out.update(zip(small_names, r_small))

    shapes = {"c_ctx": (D,), "ada_w": (2, D, SH_ADA), "ada_b": (2, 3 * D), "norm_w": (2, D), "hgrn_w_in": (1, D, SH_WIN),
              "hgrn_lb_logits": (1, 2, DH), "hgrn_gnorm_w": (1, E), "hgrn_w_out": (1, SH_ROWS, D), "pool_w_in": (1, D, SH_PWIN),
              "pool_w_grp": (1, 4, SH_GRP, PG), "pool_scale": (1, DH), "pool_w_out": (1, SH_ROWS, D), "final_norm_w": (D,)}
    order = ["c_ctx", "ada_w", "ada_b", "norm_w", "hgrn_w_in", "hgrn_lb_logits", "hgrn_gnorm_w", "hgrn_w_out", "pool_w_in",
             "pool_w_grp", "pool_scale", "pool_w_out", "final_norm_w"]
    flat = [out[name][q].reshape(shapes[name]) for q in range(4) for name in order]
    return (loss128[0, 0], grad_x[None], *flat)
```

```python
import functools

import numpy as np
import jax
import jax.numpy as jnp
from jax import lax
from jax.experimental import pallas as pl
from jax.experimental.pallas import tpu as pltpu

F32 = jnp.float32
BF16 = jnp.bfloat16

D = 1024
E = 1024
HEADS = 8
DH = 128
CHUNK = 64
T = 2048
TC = 256
TT = T + TC
TM = 256
NT = TT // TM
NTX = T // TM
NDEV = 8
GRID_W = 64
POOL_WINDOWS = (2, 4, 8, 16)
PG = 256
EPS = 1e-6
WIN_COLS = 5 * E
SH_WIN = WIN_COLS // NDEV
SH_PWIN = 2 * E // NDEV
SH_ROWS = E // NDEV
SH_GRP = PG // NDEV
SH_ADA = 3 * D // NDEV
VMEM_LIMIT = 56 * 1024 * 1024

ADAM_LR, ADAM_B1, ADAM_B2, ADAM_EPS, ADAM_WD, ADAM_STEP = 0.001, 0.9, 0.999, 1e-08, 0.01, 10

MESH = pl.DeviceIdType.MESH
VMEM_SPEC = pl.BlockSpec(memory_space=pltpu.VMEM)
HBM_SPEC = pl.BlockSpec(memory_space=pltpu.HBM)


def _sds(shape, dtype):
    return jax.ShapeDtypeStruct(shape, dtype)


def _bf(a):
    return a if a.dtype == BF16 else a.astype(BF16)


def _dot(a, b):
    return lax.dot_general(_bf(a), _bf(b), (((1,), (0,)), ((), ())), preferred_element_type=F32)


def _dot_tb(a, b):
    return lax.dot_general(_bf(a), _bf(b), (((1,), (1,)), ((), ())), preferred_element_type=F32)


def _dot_ta(a, b):
    return lax.dot_general(_bf(a), _bf(b), (((0,), (0,)), ((), ())), preferred_element_type=F32)


def _bdot(a, b):
    return lax.dot_general(_bf(a), _bf(b), (((2,), (1,)), ((0,), (0,))), preferred_element_type=F32)


def _bdot_nt(a, b):
    return lax.dot_general(_bf(a), _bf(b), (((2,), (2,)), ((0,), (0,))), preferred_element_type=F32)


def _bdot_tn(a, b):
    return lax.dot_general(_bf(a), _bf(b), (((1,), (1,)), ((0,), (0,))), preferred_element_type=F32)


def _dot01(m01, x):
    hi = x.astype(BF16)
    lo = (x - hi.astype(F32)).astype(BF16)
    return _dot(m01, hi) + _dot(m01, lo)


def _rstd(x):
    return lax.rsqrt(jnp.mean(x * x, axis=-1, keepdims=True) + EPS)


def _sigmoid(x):
    return jax.nn.sigmoid(x)


def _colsum(a):
    return jnp.sum(a, axis=0, keepdims=True)


def _stack_rows(rows):
    n = rows[0].shape[-1]
    rid = lax.broadcasted_iota(jnp.int32, (16, n), 0)
    out = jnp.zeros((16, n), F32)
    for i, r in enumerate(rows):
        out = jnp.where(rid == i, r, out)
    return out


def _head_map(fn, *arrs):
    outs = [fn(*[a[:, h * DH:(h + 1) * DH] for a in arrs]) for h in range(HEADS)]
    return jnp.concatenate(outs, axis=1)


def _gla_consts():
    r = np.arange(TM)[:, None]
    c = np.arange(TM)[None, :]
    same = (r // CHUNK) == (c // CHUNK)
    tril = same & (c <= r)
    triu = same & (c >= r)
    m = np.stack([tril, triu]).astype(np.float32)
    return jnp.asarray(m, BF16), jnp.asarray(m, F32)


def _pool_consts():
    r = np.arange(TM)[:, None]
    c = np.arange(TM)[None, :]
    same = (r // GRID_W) == (c // GRID_W)
    rp, cp = r % GRID_W, c % GRID_W
    bs, inv = [], []
    for w in POOL_WINDOWS:
        lo = np.clip(rp - w // 2, 0, GRID_W)
        hi = np.clip(rp - w // 2 + w, 0, GRID_W)
        bs.append(same & (cp >= lo) & (cp < hi))
        inv.append(1.0 / (hi - lo).astype(np.float32))
    b = np.stack(bs).astype(np.float32)
    bt = np.transpose(b, (0, 2, 1))
    return jnp.asarray(b, BF16), jnp.asarray(bt, BF16), jnp.asarray(np.stack(inv), F32)


def _mesh_pos():
    x, y, c = lax.axis_index("x"), lax.axis_index("y"), lax.axis_index("c")
    return x, y, c, 4 * x + 2 * y + c


def _peer(x, y, c, k):
    return (x ^ ((k >> 2) & 1), y ^ ((k >> 1) & 1), c ^ (k & 1))


def _small_gathers(refs, ssem, rsem):
    lb_r, ps_r, c_r, cctx_r, ada_r, adab_r, lb_o, ps_o, cg_o, mod_o, lb_out, ps_out, cg_out, mod0_o, mod1_o, modc_o = refs
    x, y, cc, idx = _mesh_pos()
    srcs = [lb_r, ps_r, c_r, mod_o.at[idx]]
    mine = [lb_o.at[idx], ps_o.at[idx], cg_o.at[idx], mod_o.at[idx]]

    def remote(a, k):
        return pltpu.make_async_remote_copy(src_ref=srcs[a], dst_ref=mine[a], send_sem=ssem.at[a, k], recv_sem=rsem.at[a, k],
                                            device_id=_peer(x, y, cc, k), device_id_type=MESH)

    first = [remote(a, k) for k in range(1, NDEV) for a in (2, 0, 1)]
    for cp in first:
        cp.start()
    lb_o[idx] = lb_r[...]
    ps_o[idx] = ps_r[...]
    cg_o[idx] = c_r[...]
    for k in range(1, NDEV):
        remote(2, k).wait_recv()
    rows = _stack_rows([cg_o[i] for i in range(NDEV)] + [cctx_r[...]])
    sc = rows * _sigmoid(rows)
    for l in range(2):
        mod_o[idx, l] = _dot(sc, ada_r[l])
    second = [remote(3, k) for k in range(1, NDEV)]
    for cp in second:
        cp.start()
    for k in range(1, NDEV):
        remote(3, k).wait_recv()

    def mod_rows(l, row):
        full = jnp.concatenate([mod_o[s, l, row, :] for s in range(NDEV)], axis=1) + adab_r[l:l + 1, :]
        return [full[:, j * D:(j + 1) * D] for j in range(3)]

    me = pl.ds(idx, 1)
    for out, parts in ((mod0_o, mod_rows(0, me)), (mod1_o, mod_rows(1, me)), (modc_o, mod_rows(0, slice(NDEV, NDEV + 1)))):
        for j in range(3):
            out[j:j + 1, :] = parts[j]
    for cp in first + second:
        cp.wait_send()
    for k in range(1, NDEV):
        for a in (0, 1):
            remote(a, k).wait_recv()
    lb_out[...] = lb_o[...]
    ps_out[...] = ps_o[...]
    cg_out[...] = cg_o[...]


def _gather_order(s, core):
    k = jnp.where(s == 2, 4, jnp.where(s == 4, 2, s))
    return k ^ jnp.where((core == 1) & (s >= 2) & (s <= 5), 6, 0)


GATHER_ISSUE = (1, 2, 4, 3, 5, 6, 7)
GATHER_ICI = (2, 4, 6)
GATHER_DIRECT = (1,) + GATHER_ICI
GLA_HB = 2
RS_SLOTS = 5


def _shard_of(kind, ref, i):
    if kind == "rows":
        return ref.at[pl.ds(pl.multiple_of(i * SH_ROWS, SH_ROWS), SH_ROWS), :]
    if kind == "major":
        return ref.at[i]
    assert kind == "grp"
    return ref.at[:, pl.ds(pl.multiple_of(i * SH_GRP, SH_GRP), SH_GRP), :]


def _gather_rider(step, n_steps, forward_at, kinds, srcs, outs, ssem, rsem, lsem):
    x, y, cc, idx = _mesh_pos()
    arrays = range(len(kinds))
    mine = [_shard_of(kinds[a], outs[a], idx) for a in arrays]

    def remote(a, k):
        return pltpu.make_async_remote_copy(src_ref=srcs[a], dst_ref=mine[a], send_sem=ssem.at[a, k], recv_sem=rsem.at[a, k],
                                            device_id=_peer(x, y, cc, k), device_id_type=MESH)

    def forward(a, k):
        blk = _shard_of(kinds[a], outs[a], idx ^ k)
        return pltpu.make_async_remote_copy(src_ref=blk, dst_ref=blk, send_sem=ssem.at[a, k ^ 1], recv_sem=rsem.at[a, k ^ 1],
                                            device_id=(x, y, 1 - cc), device_id_type=MESH)

    copies = [remote(a, k) for k in GATHER_DIRECT for a in arrays]
    passed = [forward(a, k) for k in GATHER_ICI for a in arrays]
    local = [pltpu.make_async_copy(srcs[a], mine[a], lsem.at[a]) for a in arrays]

    @pl.when(step == 0)
    def _():
        for cp in copies + local:
            cp.start()

    @pl.when(step == forward_at)
    def _():
        for k in GATHER_ICI:
            for a in arrays:
                remote(a, k).wait_recv()
                forward(a, k).start()

    def finish():
        @pl.when(step == n_steps - 1)
        def _():
            for cp in copies + passed:
                cp.wait_send()
            for a in arrays:
                remote(a, 1).wait_recv()
            for cp in passed:
                cp.wait_recv()
            for cp in local:
                cp.wait()

    return finish


def _scatter_rider(step, n_steps, kinds, grads, slots, ssem, rsem, lsem):
    x, y, cc, idx = _mesh_pos()
    arrays = range(len(kinds))
    dsts = [slots[a].at[idx] for a in arrays]

    def remote(a, k):
        px, py, pc = _peer(x, y, cc, k)
        return pltpu.make_async_remote_copy(src_ref=_shard_of(kinds[a], grads[a], 4 * px + 2 * py + pc), dst_ref=dsts[a],
                                            send_sem=ssem.at[a, k], recv_sem=rsem.at[a, k], device_id=(px, py, pc), device_id_type=MESH)

    copies = [remote(a, k) for k in GATHER_ISSUE for a in arrays]
    local = [pltpu.make_async_copy(_shard_of(kinds[a], grads[a], idx), dsts[a], lsem.at[a]) for a in arrays]

    @pl.when(step == 0)
    def _():
        for cp in copies + local:
            cp.start()

    def finish():
        @pl.when(step == n_steps - 1)
        def _():
            for cp in copies:
                cp.wait_send()
            for cp in copies:
                cp.wait_recv()
            for cp in local:
                cp.wait()

    return finish


def _rider_sems(n):
    return [pltpu.SemaphoreType.DMA((n, NDEV)), pltpu.SemaphoreType.DMA((n, NDEV)), pltpu.SemaphoreType.DMA((n,))]


def _scatter_rider2(step, n_steps, add_at, kinds, grads, slots, bufs, sems):
    x, y, cc, idx = _mesh_pos()
    sibling = (x, y, 1 - cc)
    arrays = range(len(kinds))
    psend, precv, isend, irecv, lown, sibsem, lself = sems

    def mine(a, i):
        return _shard_of(kinds[a], grads[a], i)

    def partial(a, p):
        return pltpu.make_async_remote_copy(src_ref=mine(a, idx ^ (2 * (p + 1)) ^ 1), dst_ref=bufs[a][1].at[p], send_sem=psend.at[a, p],
                                            recv_sem=precv.at[a, p], device_id=sibling, device_id_type=MESH)

    def load(a, p):
        return pltpu.make_async_copy(mine(a, idx ^ (2 * (p + 1))), bufs[a][0].at[p], lown.at[a, p])

    def chip_sum(a, p):
        return pltpu.make_async_remote_copy(src_ref=bufs[a][0].at[p], dst_ref=slots[a].at[2 + p], send_sem=isend.at[a, p],
                                            recv_sem=irecv.at[a, p], device_id=_peer(x, y, cc, 2 * (p + 1)), device_id_type=MESH)

    def to_sibling(a):
        return pltpu.make_async_remote_copy(src_ref=mine(a, idx ^ 1), dst_ref=slots[a].at[1], send_sem=sibsem.at[a, 0],
                                            recv_sem=sibsem.at[a, 1], device_id=sibling, device_id_type=MESH)

    def own(a):
        return pltpu.make_async_copy(mine(a, idx), slots[a].at[0], lself.at[a, 0])

    @pl.when(step == 0)
    def _():
        for a in arrays:
            for p in range(3):
                partial(a, p).start()
                load(a, p).start()
            to_sibling(a).start()
            own(a).start()

    @pl.when(step == add_at)
    def _():
        for a in arrays:
            for p in range(3):
                partial(a, p).wait_recv()
                load(a, p).wait()
                bufs[a][0][p] = (bufs[a][0][p].astype(F32) + bufs[a][1][p].astype(F32)).astype(BF16)
                chip_sum(a, p).start()

    def finish():
        @pl.when(step == n_steps - 1)
        def _():
            for a in arrays:
                for p in range(3):
                    partial(a, p).wait_send()
                    chip_sum(a, p).wait_send()
                    chip_sum(a, p).wait_recv()
                to_sibling(a).wait_send()
                to_sibling(a).wait_recv()
                own(a).wait()

    return finish


def _rider2_scratch(blocks):
    n = len(blocks)
    bufs = [pltpu.VMEM((3,) + tuple(b), BF16) for b in blocks for _ in range(2)]
    return bufs + [pltpu.SemaphoreType.DMA((n, 3)) for _ in range(5)] + [pltpu.SemaphoreType.DMA((n, 2)), pltpu.SemaphoreType.DMA((n, 1))]


def _rider2_split(refs, n):
    refs = list(refs)
    return [tuple(refs[2 * a:2 * a + 2]) for a in range(n)], tuple(refs[2 * n:2 * n + 7])


def _modulated(x, nw, shift, scale):
    r = _rstd(x)
    xn = x * r
    a = xn * nw
    return a * (1.0 + scale) + shift, r, xn, a


def _ctx_or_x(i, ctx_ref, x_ref):
    return jnp.where(i == 0, ctx_ref[...], x_ref[...])


def _f1_gather_matmul(idx1, ctx, x, nw, w_in, w_out, pw_in, pgrp, pw_out, lb_l, pscale, c, c_ctx, ada_w, ada_b):
    def body(idx_ref, ctx_ref, x_ref, nw_ref, win_r, wout_r, pwin_r, pgrp_r, pwout_r, lb_r, ps_r, c_r, cctx_r, ada_r, adab_r,
             g_ref, win_o, s_wout, s_pwin, s_pgrp, s_pwout, lb_o, ps_o, cg_o, mod0_o, mod1_o, modc_o,
             wslot, hx_sc, lb_g, ps_g, cg_g, mod_g, ssem, rsem, osem, dsem, sm_ssem, sm_rsem):
        del idx_ref
        s, i = pl.program_id(0), pl.program_id(1)
        x, y, cc, idx = _mesh_pos()
        k = _gather_order(s, cc)
        j = idx ^ k
        first = 4 - 2 * cc

        def remote(kk):
            return pltpu.make_async_remote_copy(src_ref=wslot.at[idx], dst_ref=wslot.at[idx], send_sem=ssem.at[kk], recv_sem=rsem.at[kk],
                                                device_id=_peer(x, y, cc, kk), device_id_type=MESH)

        def forward(kk):
            jj = idx ^ kk
            return pltpu.make_async_remote_copy(src_ref=wslot.at[jj], dst_ref=wslot.at[jj], send_sem=ssem.at[kk ^ 1],
                                                recv_sem=rsem.at[kk ^ 1], device_id=(x, y, 1 - cc), device_id_type=MESH)

        def relay(h):
            blk = wslot.at[idx ^ (4 >> h), pl.ds(h * (D // 2), D // 2), :]
            return pltpu.make_async_remote_copy(src_ref=blk, dst_ref=blk, send_sem=dsem.at[0, h], recv_sem=dsem.at[1, h],
                                                device_id=_peer(x, y, cc, 2 << h), device_id_type=MESH)

        def to_hbm(jj, kk):
            return pltpu.make_async_copy(wslot.at[jj], win_o.at[jj], osem.at[kk])

        @pl.when((s == 0) & (i == 0))
        def _():
            _small_gathers((lb_r, ps_r, c_r, cctx_r, ada_r, adab_r, lb_g, ps_g, cg_g, mod_g, lb_o, ps_o, cg_o, mod0_o, mod1_o, modc_o),
                           sm_ssem, sm_rsem)
            wslot[idx] = win_r[...].astype(BF16)
            remote(1).start()
            remote(first).start()
            s_wout[...] = wout_r[...].astype(BF16)
            s_pwin[...] = pwin_r[...].astype(BF16)
            s_pgrp[...] = pgrp_r[...].astype(BF16)
            s_pwout[...] = pwout_r[...].astype(BF16)

        @pl.when(s == 0)
        def _():
            shift = jnp.where(i == 0, modc_o[0:1, :], mod0_o[0:1, :])
            scale = jnp.where(i == 0, modc_o[1:2, :], mod0_o[1:2, :])
            hx, _, _, _ = _modulated(_ctx_or_x(i, ctx_ref, x_ref), nw_ref[...], shift, scale)
            hx_sc[i] = hx.astype(BF16)

        @pl.when((s == 2) & (i == 0))
        def _():
            remote(6 - first).start()

        @pl.when((s > 0) & (i == 0) & (k != 6))
        def _():
            remote(k).wait_recv()

            @pl.when((k & 1) == 0)
            def _():
                forward(k).start()

            for h in range(2):
                @pl.when(k == 4 >> h)
                def _():
                    relay(h).start()

        @pl.when((i == 0) & (k == 6))
        def _():
            for h in range(2):
                relay(h).wait_recv()
            forward(6).start()

        @pl.when(i == 0)
        def _():
            to_hbm(j, k).start()

        g_ref[...] = jnp.dot(hx_sc[i], wslot[j], preferred_element_type=F32)

        @pl.when((s == NDEV - 1) & (i == NT - 1))
        def _():
            for kk in (1, 2, 4):
                remote(kk).wait_send()
            for kk in GATHER_ICI:
                forward(kk).wait_send()
            for h in range(2):
                relay(h).wait_send()
            for kk in range(NDEV):
                to_hbm(idx ^ kk, kk).wait()

    grid_spec = pltpu.PrefetchScalarGridSpec(
        num_scalar_prefetch=1, grid=(NDEV, NT),
        in_specs=[VMEM_SPEC, pl.BlockSpec((TM, D), lambda s, i, ix: (jnp.where(s == 0, jnp.maximum(i - 1, 0), NTX - 1), 0))]
        + [VMEM_SPEC] * 12,
        out_specs=[pl.BlockSpec((TM, SH_WIN), lambda s, i, ix: (i, ix[0] ^ _gather_order(s, ix[0] & 1))), HBM_SPEC] + [VMEM_SPEC] * 10,
        scratch_shapes=[pltpu.VMEM((NDEV, D, SH_WIN), BF16), pltpu.VMEM((NT, TM, D), BF16),
                        pltpu.VMEM((NDEV, 2, DH), F32), pltpu.VMEM((NDEV, 1, DH), F32), pltpu.VMEM((NDEV, 1, D), F32),
                        pltpu.VMEM((NDEV, 2, 16, SH_ADA), F32),
                        pltpu.SemaphoreType.DMA((NDEV,)), pltpu.SemaphoreType.DMA((NDEV,)), pltpu.SemaphoreType.DMA((NDEV,)),
                        pltpu.SemaphoreType.DMA((2, 2)),
                        pltpu.SemaphoreType.DMA((4, NDEV)), pltpu.SemaphoreType.DMA((4, NDEV))])
    outs = (_sds((TT, WIN_COLS), F32), _sds((NDEV, D, SH_WIN), BF16),
            _sds((SH_ROWS, D), BF16), _sds((D, SH_PWIN), BF16), _sds((4, SH_GRP, PG), BF16), _sds((SH_ROWS, D), BF16),
            _sds((NDEV, 2, DH), F32), _sds((NDEV, 1, DH), F32), _sds((NDEV, 1, D), F32),
            _sds((3, D), F32), _sds((3, D), F32), _sds((3, D), F32))
    return pl.pallas_call(
        body, name="f1_gather_matmul", grid_spec=grid_spec, out_shape=outs,
        compiler_params=pltpu.CompilerParams(dimension_semantics=("arbitrary", "arbitrary"), vmem_limit_bytes=VMEM_LIMIT),
    )(idx1, ctx, x, nw, w_in, w_out, pw_in, pgrp, pw_out, lb_l, pscale, c, c_ctx, ada_w, ada_b)


def _gla_gates(pre, qpre, lbd, cum, rev):
    rows, n = pre.shape
    nch = rows // CHUNK
    sig = _sigmoid(pre)
    f = lbd + (1.0 - lbd) * sig
    k = 1.0 - f
    g = _dot01(cum, jnp.log(f))
    g3 = g.reshape(nch, CHUNK, n)
    last = 0 if rev else CHUNK - 1
    mid = CHUNK // 2 if rev else CHUNK // 2 - 1
    gl1, gm1 = g3[:, last:last + 1, :], g3[:, mid:mid + 1, :]

    def bc(a):
        return jnp.broadcast_to(a, g3.shape).reshape(rows, n)

    gm = bc(gm1)
    e_q, e_k = jnp.exp(g - gm), jnp.exp(gm - g)
    qsig = _sigmoid(qpre)
    qs = qpre * qsig * (DH ** -0.5)
    return dict(sig=sig, f=f, k=k, qsig=qsig, qs=qs, e_q=e_q, e_k=e_k,
                e_mid=[jnp.exp(gm1[ci]) for ci in range(nch)], e_rest=[jnp.exp(gl1[ci] - gm1[ci]) for ci in range(nch)])


def _put_heads(ref, lead, arr):
    for h in range(HEADS):
        ref[lead + (h,)] = arr[:, h * DH:(h + 1) * DH]


def _get_heads(ref, lead=()):
    return jnp.concatenate([ref[lead + (h,)] for h in range(HEADS)], axis=1)


def _gla_prep(g_all, lb, cum01, s_wout, s_pgrp):
    nch = TM // CHUNK

    def body(g_ref, lb_ref, cum_ref, swout_r, spgrp_r, p0_ref, p1_ref, v_ref, dec_ref, wout_o, pgrp_o, ssem, rsem, lsem):
        finish = _gather_rider(pl.program_id(0), NT, NT - 1, ("rows", "grp"), (swout_r, spgrp_r), (wout_o, pgrp_o), ssem, rsem, lsem)
        qpre = g_ref[:, 3 * E:4 * E]
        _put_heads(v_ref, (), g_ref[:, 2 * E:3 * E].astype(BF16))
        for d, p_ref in ((0, p0_ref), (1, p1_ref)):
            t = _gla_gates(g_ref[:, d * E:(d + 1) * E], qpre, lb_ref[d:d + 1, :], cum_ref[d], d == 1)
            _put_heads(p_ref, (0,), (t["qs"] * t["e_q"]).astype(BF16))
            _put_heads(p_ref, (1,), (t["k"] * t["e_k"]).astype(BF16))
            for ci in range(nch):
                dec_ref[d, 0, ci:ci + 1, :] = t["e_mid"][ci]
                dec_ref[d, 0, nch + ci:nch + ci + 1, :] = t["e_rest"][ci]
        finish()

    quad = pl.BlockSpec((2, HEADS, TM, DH), lambda i: (0, 0, i, 0))
    return pl.pallas_call(
        body, name="gla_prep", grid=(NT,),
        in_specs=[pl.BlockSpec((TM, 4 * E), lambda i: (i, 0)), VMEM_SPEC, VMEM_SPEC, HBM_SPEC, HBM_SPEC],
        out_specs=[quad, quad, pl.BlockSpec((HEADS, TM, DH), lambda i: (0, i, 0)), pl.BlockSpec((2, 1, 2 * nch, E), lambda i: (0, i, 0, 0)),
                   HBM_SPEC, HBM_SPEC],
        out_shape=(_sds((2, HEADS, TT, DH), BF16), _sds((2, HEADS, TT, DH), BF16), _sds((HEADS, TT, DH), BF16), _sds((2, NT, 2 * nch, E), F32),
                   _sds((E, D), BF16), _sds((4, PG, PG), BF16)),
        scratch_shapes=_rider_sems(2),
        compiler_params=pltpu.CompilerParams(dimension_semantics=("arbitrary",), vmem_limit_bytes=VMEM_LIMIT),
    )(g_all, lb, cum01, s_wout, s_pgrp)


def _scan_tile(i, rev):
    t = jnp.where(i == 0, 0, NT - i) if rev else i
    return t, pl.ds(pl.multiple_of(t * TM, TM), TM)


def _chunk_rows(dec_ref, lanes, cis, where):
    nch = TM // CHUNK

    def rows(off):
        return jnp.stack([dec_ref[d, where[d][0], off + ci:off + ci + 1, hh * DH:(hh + 1) * DH] for (d, hh), ci in zip(lanes, cis)])

    return rows(0), rows(nch)


def _gla_fwd(p0, p1, v_all, dec, mask01, s_pwin, s_pwout):
    n_steps = HEADS // GLA_HB

    def body(p0_ref, p1_ref, v_ref, dec_ref, msk_ref, spwin_r, spwout_r, o_ref, pwin_o, pwout_o, ob_sc, ssem, rsem, lsem):
        finish = _gather_rider(pl.program_id(0), n_steps, n_steps - 1, ("major", "rows"), (spwin_r, spwout_r), (pwin_o, pwout_o),
                               ssem, rsem, lsem)

        lanes = [(d, hh) for d in (0, 1) for hh in range(GLA_HB)]
        nch = TM // CHUNK

        def tile_body(i, st):
            where = [_scan_tile(i, d == 1) for d in (0, 1)]

            def stacked(fn):
                return jnp.stack([fn(d, hh, where[d][1]) for d, hh in lanes])

            qg, kg = [stacked(lambda d, hh, rows, ty=ty: (p1_ref if d else p0_ref)[ty, hh, rows, :]) for ty in range(2)]
            v = stacked(lambda d, hh, rows: v_ref[hh, rows, :])
            a = _bdot_nt(qg, kg) * jnp.stack([msk_ref[d] for d, _ in lanes])
            intra = _bdot(a, v)
            outs = [[None] * nch for _ in lanes]
            for n in range(nch):
                cis = [nch - 1 - n if d else n for d, _ in lanes]

                def chunk(arr):
                    return jnp.stack([arr[l, ci * CHUNK:(ci + 1) * CHUNK] for l, ci in enumerate(cis)])

                e_mid, e_rest = _chunk_rows(dec_ref, lanes, cis, where)
                inter = _bdot_nt(chunk(qg), st * e_mid)
                for l, ci in enumerate(cis):
                    outs[l][ci] = inter[l] + intra[l, ci * CHUNK:(ci + 1) * CHUNK]
                st = st * (e_mid * e_rest) + _bdot_tn(chunk(v), chunk(kg)) * e_rest
            for l, (d, hh) in enumerate(lanes):
                (ob_sc if d else o_ref)[hh, where[d][1], :] = jnp.concatenate(outs[l], axis=0)
            return st

        lax.fori_loop(0, NT, tile_body, jnp.zeros((len(lanes), DH, DH), F32))
        o_ref[...] += ob_sc[...]
        finish()

    quad = pl.BlockSpec((2, GLA_HB, TT, DH), lambda h: (0, h, 0, 0))
    head = pl.BlockSpec((GLA_HB, TT, DH), lambda h: (h, 0, 0))
    return pl.pallas_call(
        body, name="gla_fwd", grid=(n_steps,),
        in_specs=[quad, quad, head, pl.BlockSpec((2, NT, 8, GLA_HB * DH), lambda h: (0, 0, 0, h)),
                  pl.BlockSpec((2, TM, TM), lambda h: (0, 0, 0)), HBM_SPEC, HBM_SPEC],
        out_specs=[head, HBM_SPEC, HBM_SPEC],
        out_shape=(_sds((HEADS, TT, DH), F32), _sds((NDEV, D, SH_PWIN), BF16), _sds((E, D), BF16)),
        scratch_shapes=[pltpu.VMEM((GLA_HB, TT, DH), F32)] + _rider_sems(2),
        compiler_params=pltpu.CompilerParams(dimension_semantics=("arbitrary",), vmem_limit_bytes=VMEM_LIMIT),
    )(p0, p1, v_all, dec, mask01, s_pwin, s_pwout)


def _gated_norm(o, z, gw):
    r = _head_map(lambda oh: jnp.broadcast_to(_rstd(oh), oh.shape), o)
    on = o * r
    zs = _sigmoid(z)
    sz = z * zs
    return on * gw * sz, r, on, zs, sz


def _f3_out(o, g_all, x, gate, gw, wout):
    def body(o_ref, z_ref, x_ref, gate_ref, gw_ref, w_ref, x1_ref):
        og, _, _, _, _ = _gated_norm(_get_heads(o_ref), z_ref[...], gw_ref[...])
        x1_ref[...] = x_ref[...] + gate_ref[...] * _dot(og, w_ref[...])

    return pl.pallas_call(
        body, name="f3_out", grid=(NTX,),
        in_specs=[pl.BlockSpec((HEADS, TM, DH), lambda i: (0, i + 1, 0)), pl.BlockSpec((TM, E), lambda i: (i + 1, 4)),
                  pl.BlockSpec((TM, D), lambda i: (i, 0)), pl.BlockSpec((1, D), lambda i: (0, 0)),
                  pl.BlockSpec((1, E), lambda i: (0, 0)), pl.BlockSpec((E, D), lambda i: (0, 0))],
        out_specs=pl.BlockSpec((TM, D), lambda i: (i, 0)),
        out_shape=_sds((T, D), F32),
        compiler_params=pltpu.CompilerParams(dimension_semantics=("arbitrary",)),
    )(o, g_all, x, gate, gw, wout)


def _pool_layer(x1, tgt, mod1, nw1, fnw, pwin, pgrp, pscale, pwout, pb, pbt, pinv):
    def body(x_ref, t_ref, m_ref, nw_ref, fw_ref, pwin_ref, pgrp_ref, ps_ref, pwout_ref, pb_ref, pbt_ref, pinv_ref,
             dx_ref, gpwin_o, gpgrp_o, gpwout_o, dmod_o, gnw_o, gfw_o, gps_o, loss_o,
             a_pwin, a_pgrp, a_pwout):
        i = pl.program_id(0)

        @pl.when(i == 0)
        def _():
            for ref in (a_pwin, a_pgrp, a_pwout, dmod_o, gnw_o, gfw_o, gps_o, loss_o):
                ref[...] = jnp.zeros_like(ref)

        shift, scale, gate = m_ref[0:1, :], m_ref[1:2, :], m_ref[2:3, :]
        nw, fw, ps = nw_ref[...], fw_ref[...], ps_ref[...]
        x1 = x_ref[...]
        hx, r1, xn, a = _modulated(x1, nw, shift, scale)
        hxb = hx.astype(BF16)
        uz = jnp.concatenate([_dot(hxb, pwin_ref[j]) for j in range(NDEV)], axis=1)
        u, z = uz[:, :E], uz[:, E:]
        pooled, ys = [], []
        for g in range(4):
            ug = u[:, g * PG:(g + 1) * PG]
            pg = _dot01(pb_ref[g], ug) * pinv_ref[g] - ug
            pooled.append(pg.astype(BF16))
            ys.append(_dot(pooled[g], pgrp_ref[g]))
        ycat = jnp.concatenate(ys, axis=1)
        y = ycat * ps
        zs = _sigmoid(z)
        sz = z * zs
        p = (y * sz).astype(BF16)
        out = _dot(p, pwout_ref[...])
        x2 = x1 + gate * out
        r2 = _rstd(x2)
        xn2 = x2 * r2
        diff = xn2 * fw - t_ref[...]
        loss_o[...] += _colsum(diff * diff)
        dyf = diff * (1.0 / D)
        gfw_o[...] += _colsum(dyf * xn2)
        dxn2 = dyf * fw
        dx2 = r2 * (dxn2 - xn2 * jnp.mean(dxn2 * xn2, axis=-1, keepdims=True))
        dgate = _colsum(dx2 * out)
        dout = (dx2 * gate).astype(BF16)
        for j in range(4):
            cs = slice(j * PG, (j + 1) * PG)
            a_pwout[:, cs] += _dot_ta(p, dout[:, cs])
        dp = _dot_tb(dout, pwout_ref[...])
        dy = dp * sz
        dz = dp * y * (zs * (1.0 + z * (1.0 - zs)))
        gps_o[...] += _colsum(dy * ycat)
        dycat = dy * ps
        dus = []
        for g in range(4):
            dyg = dycat[:, g * PG:(g + 1) * PG].astype(BF16)
            a_pgrp[g] += _dot_ta(pooled[g], dyg)
            dpg = _dot_tb(dyg, pgrp_ref[g])
            dus.append(_dot01(pbt_ref[g], dpg * pinv_ref[g]) - dpg)
        duz = jnp.concatenate(dus + [dz], axis=1).astype(BF16)
        dhx = None
        for j in range(NDEV):
            dj = duz[:, j * SH_PWIN:(j + 1) * SH_PWIN]
            a_pwin[j] += _dot_ta(hxb, dj)
            part = _dot_tb(dj, pwin_ref[j])
            dhx = part if dhx is None else dhx + part
        dmod_o[0:1, :] += _colsum(dhx)
        dmod_o[1:2, :] += _colsum(dhx * a)
        dmod_o[2:3, :] += dgate
        da = dhx * (1.0 + scale)
        gnw_o[...] += _colsum(da * xn)
        dxn = da * nw
        dx_ref[...] = dx2 + r1 * (dxn - xn * jnp.mean(dxn * xn, axis=-1, keepdims=True))

        @pl.when(i == NTX - 1)
        def _():
            gpwin_o[...] = a_pwin[...].astype(BF16)
            gpgrp_o[...] = a_pgrp[...].astype(BF16)
            gpwout_o[...] = a_pwout[...].astype(BF16)

    tile = pl.BlockSpec((TM, D), lambda i: (i, 0))
    outs = (_sds((T, D), F32), _sds((NDEV, D, SH_PWIN), BF16), _sds((4, PG, PG), BF16), _sds((E, D), BF16),
            _sds((3, D), F32), _sds((1, D), F32), _sds((1, D), F32), _sds((1, E), F32), _sds((1, D), F32))
    return pl.pallas_call(
        body, name="pool_layer", grid=(NTX,),
        in_specs=[tile, tile] + [VMEM_SPEC] * 10,
        out_specs=[tile] + [VMEM_SPEC] * 8,
        out_shape=outs,
        scratch_shapes=[pltpu.VMEM((NDEV, D, SH_PWIN), F32), pltpu.VMEM((4, PG, PG), F32), pltpu.VMEM((E, D), F32)],
        compiler_params=pltpu.CompilerParams(dimension_semantics=("arbitrary",), vmem_limit_bytes=VMEM_LIMIT),
    )(x1, tgt, mod1, nw1, fnw, pwin, pgrp, pscale, pwout, pb, pbt, pinv)


def _b3_out_bwd(dx1, o, g_all, gate, gw, wout, gpwout):
    def body(dx_ref, o_ref, z_ref, gate_ref, gw_ref, w_ref, gpwout_r, do_ref, dz_ref, gw_o, dgate_o, ggw_o, rpwout_o,
             acc, *rider):
        i = pl.program_id(0)
        bufs, sems = _rider2_split(rider, 1)
        finish = _scatter_rider2(i, NT, 2, ("rows",), (gpwout_r,), (rpwout_o,), bufs, sems)

        @pl.when(i == 0)
        def _():
            acc[...] = jnp.zeros_like(acc)
            dgate_o[...] = jnp.zeros_like(dgate_o)
            ggw_o[...] = jnp.zeros_like(ggw_o)
            do_ref[...] = jnp.zeros_like(do_ref)
            dz_ref[...] = jnp.zeros_like(dz_ref)

        @pl.when(i > 0)
        def _():
            gw = gw_ref[...]
            z = z_ref[...]
            og, r, on, zs, sz = _gated_norm(_get_heads(o_ref), z, gw)
            ogb = og.astype(BF16)
            dx = dx_ref[...]
            dgate_o[...] += _colsum(dx * _dot(ogb, w_ref[...]))
            dy = (dx * gate_ref[...]).astype(BF16)
            for j in range(4):
                cs = slice(j * PG, (j + 1) * PG)
                acc[:, cs] += _dot_ta(ogb, dy[:, cs])
            dog = _dot_tb(dy, w_ref[...])
            dz_ref[...] = (dog * (on * gw) * (zs * (1.0 + z * (1.0 - zs)))).astype(BF16)
            dong = dog * sz
            ggw_o[...] += _colsum(dong * on)
            don = dong * gw
            do = _head_map(lambda dh, nh, rh: rh * (dh - nh * jnp.mean(dh * nh, axis=-1, keepdims=True)), don, on, r)
            _put_heads(do_ref, (), do.astype(BF16))

        @pl.when(i == NT - 1)
        def _():
            gw_o[...] = acc[...].astype(BF16)

        finish()

    prev = lambda i: (jnp.maximum(i - 1, 0), 0)
    heads = pl.BlockSpec((HEADS, TM, DH), lambda i: (0, i, 0))
    return pl.pallas_call(
        body, name="b3_out_bwd", grid=(NT,),
        in_specs=[pl.BlockSpec((TM, D), prev), heads, pl.BlockSpec((TM, E), lambda i: (i, 4)),
                  VMEM_SPEC, VMEM_SPEC, VMEM_SPEC, HBM_SPEC],
        out_specs=[heads, pl.BlockSpec((TM, E), lambda i: (i, 0)), VMEM_SPEC, VMEM_SPEC, VMEM_SPEC, HBM_SPEC],
        out_shape=(_sds((HEADS, TT, DH), BF16), _sds((TT, E), BF16), _sds((E, D), BF16), _sds((1, D), F32), _sds((1, E), F32),
                   _sds((RS_SLOTS, SH_ROWS, D), BF16)),
        scratch_shapes=[pltpu.VMEM((E, D), F32)] + _rider2_scratch([(SH_ROWS, D)]),
        compiler_params=pltpu.CompilerParams(dimension_semantics=("arbitrary",), vmem_limit_bytes=VMEM_LIMIT),
    )(dx1, o, g_all, gate, gw, wout, gpwout)


def _gla_bwd(p0, p1, v_all, dec, do, mask01, gpwin, gpgrp):
    nch = TM // CHUNK
    n_steps = HEADS // GLA_HB

    def body(p0_ref, p1_ref, v_ref, dec_ref, do_ref, msk_ref, gpwin_r, gpgrp_r, d0_ref, d1_ref, dv_ref, dgl_ref, rpwin_o, rpgrp_o,
             ss_sc, dv_sc, ssem, rsem, lsem, *rider):
        finish_grp = _scatter_rider(pl.program_id(0), n_steps, ("grp",), (gpgrp_r,), (rpgrp_o,), ssem, rsem, lsem)
        bufs, sems = _rider2_split(rider, 1)
        finish_win = _scatter_rider2(pl.program_id(0), n_steps, 1, ("major",), (gpwin_r,), (rpwin_o,), bufs, sems)

        lanes = [(d, hh) for d in (0, 1) for hh in range(GLA_HB)]
        zero = jnp.zeros((len(lanes), DH, DH), F32)
        dgl_ref[...] = jnp.zeros_like(dgl_ref)

        def p_of(d):
            return p1_ref if d else p0_ref

        def scan_step(i, n):
            where = [_scan_tile(i, d == 1) for d in (0, 1)]
            cis = [nch - 1 - n if d else n for d, _ in lanes]
            e_mid, e_rest = _chunk_rows(dec_ref, lanes, cis, where)

            def chunk(arr):
                return jnp.stack([arr[l, ci * CHUNK:(ci + 1) * CHUNK] for l, ci in enumerate(cis)])

            return where, cis, e_mid, e_rest, chunk

        def stacked(i, fn):
            where = [_scan_tile(i, d == 1) for d in (0, 1)]
            return jnp.stack([fn(d, hh, where[d][1]) for d, hh in lanes])

        def fwd_body(i, st):
            v = stacked(i, lambda d, hh, rows: v_ref[hh, rows, :])
            kg = stacked(i, lambda d, hh, rows: p_of(d)[1, hh, rows, :])
            for n in range(nch):
                _, _, e_mid, e_rest, chunk = scan_step(i, n)
                ss_sc[i * nch + n] = st
                st = st * (e_mid * e_rest) + _bdot_tn(chunk(v), chunk(kg)) * e_rest
            return st

        ss_sc[NT * nch] = lax.fori_loop(0, NT, fwd_body, zero)

        def bwd_body(ii, dst):
            i = NT - 1 - ii
            qg, kg = [stacked(i, lambda d, hh, rows, ty=ty: p_of(d)[ty, hh, rows, :]) for ty in range(2)]
            v = stacked(i, lambda d, hh, rows: v_ref[hh, rows, :])
            dob = stacked(i, lambda d, hh, rows: do_ref[hh, rows, :])
            msk = jnp.stack([msk_ref[d] for d, _ in lanes])
            a = (_bdot_nt(qg, kg) * msk).astype(BF16)
            da = (_bdot_nt(dob, v) * msk).astype(BF16)
            dqg = _bdot(da, kg)
            dkg = _bdot_tn(da, qg)
            dv_intra = _bdot_tn(a, dob)
            dv_l, dkg_l, dqg_l = ([[None] * nch for _ in lanes] for _ in range(3))
            for n in range(nch - 1, -1, -1):
                where, cis, e_mid, e_rest, chunk = scan_step(i, n)
                s_c, s_end = ss_sc[i * nch + n], ss_sc[i * nch + n + 1]
                dste = (dst * e_rest).astype(BF16)
                kg_c, v_c, dob_c = chunk(kg), chunk(v), chunk(dob)
                dv_c = chunk(dv_intra) + _bdot_nt(kg_c, dste)
                dkg_c = chunk(dkg) + _bdot(v_c, dste)
                dqg_c = chunk(dqg) + _bdot(dob_c, s_c * e_mid)
                dgl = jnp.sum(s_end * dst, axis=1, keepdims=True)
                for l, ((d, hh), ci) in enumerate(zip(lanes, cis)):
                    dv_l[l][ci], dkg_l[l][ci], dqg_l[l][ci] = dv_c[l], dkg_c[l], dqg_c[l]
                    dgl_ref[d, where[d][0], ci:ci + 1, hh * DH:(hh + 1) * DH] = dgl[l]
                dst = dst * (e_mid * e_rest) + _bdot_tn(dob_c, chunk(qg)) * e_mid
            where = [_scan_tile(i, d == 1) for d in (0, 1)]
            for l, (d, hh) in enumerate(lanes):
                rows = where[d][1]
                d_ref = d1_ref if d else d0_ref
                d_ref[0, hh, rows, :] = jnp.concatenate(dqg_l[l], axis=0).astype(BF16)
                d_ref[1, hh, rows, :] = jnp.concatenate(dkg_l[l], axis=0).astype(BF16)
                dv_sc[d, hh, rows, :] = jnp.concatenate(dv_l[l], axis=0).astype(BF16)
            return dst

        lax.fori_loop(0, NT, bwd_body, zero)
        dv_ref[...] = (dv_sc[0].astype(F32) + dv_sc[1].astype(F32)).astype(BF16)
        finish_grp()
        finish_win()

    quad = pl.BlockSpec((2, GLA_HB, TT, DH), lambda h: (0, h, 0, 0))
    col = pl.BlockSpec((GLA_HB, TT, DH), lambda h: (h, 0, 0))
    chunkv = pl.BlockSpec((2, NT, 8, GLA_HB * DH), lambda h: (0, 0, 0, h))
    outs = (_sds((2, HEADS, TT, DH), BF16), _sds((2, HEADS, TT, DH), BF16), _sds((HEADS, TT, DH), BF16), _sds((2, NT, 8, E), F32),
            _sds((RS_SLOTS, D, SH_PWIN), BF16), _sds((NDEV, 4, SH_GRP, PG), BF16))
    return pl.pallas_call(
        body, name="gla_bwd", grid=(n_steps,),
        in_specs=[quad, quad, col, chunkv, col, pl.BlockSpec((2, TM, TM), lambda h: (0, 0, 0)), HBM_SPEC, HBM_SPEC],
        out_specs=[quad, quad, col, chunkv, HBM_SPEC, HBM_SPEC],
        out_shape=outs,
        scratch_shapes=[pltpu.VMEM((NT * nch + 1, 2 * GLA_HB, DH, DH), F32), pltpu.VMEM((2, GLA_HB, TT, DH), BF16)] + _rider_sems(1)
        + _rider2_scratch([(D, SH_PWIN)]),
        compiler_params=pltpu.CompilerParams(dimension_semantics=("arbitrary",), vmem_limit_bytes=VMEM_LIMIT),
    )(p0, p1, v_all, dec, do, mask01, gpwin, gpgrp)


TMB = 128


def _gla_post_bwd(g_all, d0, d1, dgl, dv, dz, lb, cum01, gwout):
    nch = TMB // CHUNK

    def body(g_ref, d0_ref, d1_ref, dgl_ref, dv_ref, dz_ref, lb_ref, cum_ref, gwout_r, dg_ref, dlb_ref, rwout_o, *rider):
        i = pl.program_id(0)
        bufs, sems = _rider2_split(rider, 1)
        finish = _scatter_rider2(i, TT // TMB, 2, ("rows",), (gwout_r,), (rwout_o,), bufs, sems)

        @pl.when(i == 0)
        def _():
            dlb_ref[...] = jnp.zeros_like(dlb_ref)

        half = i & 1
        qpre = g_ref[:, 3 * E:4 * E]
        dqs_sum = None
        dpre = []
        for d, d_ref in ((0, d0_ref), (1, d1_ref)):
            rev = d == 1
            lbd = lb_ref[d:d + 1, :]
            t = _gla_gates(g_ref[:, d * E:(d + 1) * E], qpre, lbd, cum_ref[d, :TMB, :TMB], rev)
            dqs = _get_heads(d_ref, (0,)).astype(F32) * t["e_q"]
            dk = _get_heads(d_ref, (1,)).astype(F32) * t["e_k"]
            dg = t["qs"] * dqs - t["k"] * dk
            dgl8 = dgl_ref[d, 0]
            dgl_rows = [jnp.where(half == 0, dgl8[ci:ci + 1, :], dgl8[nch + ci:nch + ci + 1, :]) for ci in range(nch)]
            dgl_b = jnp.concatenate([jnp.broadcast_to(dgl_rows[ci], (CHUNK, E)) for ci in range(nch)], axis=0)
            pos = lax.broadcasted_iota(jnp.int32, (TMB, E), 0) & (CHUNK - 1)
            dg = dg + jnp.where(pos == (0 if rev else CHUNK - 1), dgl_b, 0.0)
            dlf = _dot01(cum_ref[1 - d, :TMB, :TMB], dg)
            df = dlf / t["f"] - dk
            sig = t["sig"]
            dpre.append((df * (1.0 - lbd) * sig * (1.0 - sig)).astype(BF16))
            dlb_ref[d:d + 1, :] += _colsum(df * (1.0 - sig))
            dqs_sum = dqs if dqs_sum is None else dqs_sum + dqs
            qsig = t["qsig"]
        dqpre = dqs_sum * (DH ** -0.5) * (qsig * (1.0 + qpre * (1.0 - qsig)))
        dg_ref[...] = jnp.concatenate([dpre[0], dpre[1], _get_heads(dv_ref), dqpre.astype(BF16), dz_ref[...]], axis=1)
        finish()

    quad = pl.BlockSpec((2, HEADS, TMB, DH), lambda i: (0, 0, i, 0))
    tile = pl.BlockSpec((TMB, E), lambda i: (i, 0))
    return pl.pallas_call(
        body, name="gla_post_bwd", grid=(TT // TMB,),
        in_specs=[pl.BlockSpec((TMB, 4 * E), lambda i: (i, 0)), quad, quad,
                  pl.BlockSpec((2, 1, 8, E), lambda i: (0, i // 2, 0, 0)), pl.BlockSpec((HEADS, TMB, DH), lambda i: (0, i, 0)), tile,
                  VMEM_SPEC, VMEM_SPEC, HBM_SPEC],
        out_specs=[pl.BlockSpec((TMB, WIN_COLS), lambda i: (i, 0)), VMEM_SPEC, HBM_SPEC],
        out_shape=(_sds((TT, WIN_COLS), BF16), _sds((2, E), F32), _sds((RS_SLOTS, SH_ROWS, D), BF16)),
        scratch_shapes=_rider2_scratch([(SH_ROWS, D)]),
        compiler_params=pltpu.CompilerParams(dimension_semantics=("arbitrary",), vmem_limit_bytes=VMEM_LIMIT),
    )(g_all, d0, d1, dgl, dv, dz, lb, cum01, gwout)


WIN_SLOTS = 4


def _scatter_order(s, core):
    return (NDEV - 1 - s) ^ jnp.where((s >= 2) & (s <= 5) & ((s & 1) == core), 6, 0)


def _b1_in_bwd(idx1, ctx, x, dx1, dg, nw, msel, win):
    last_s = NDEV - 1
    half = D // 2

    def body(idx_ref, ctx_ref, x_ref, dx1_ref, dg_ref, nw_ref, m_ref, w_ref, gx_ref, rwin_o, dmx_o, dmc_o, gnw_o,
             hx_sc, dhx_sc, acc, sbuf, pbuf, rbuf, psend, precv, isend, irecv, dsend, drecv, sibsem, lsem):
        del idx_ref
        s, i = pl.program_id(0), pl.program_id(1)
        x, y, cc, idx = _mesh_pos()
        shift, scale = m_ref[0, 0:1, :], m_ref[0, 1:2, :]
        sibling = (x, y, 1 - cc)

        def partial(p):
            return pltpu.make_async_remote_copy(src_ref=sbuf.at[0], dst_ref=pbuf.at[p], send_sem=psend.at[p], recv_sem=precv.at[p],
                                                device_id=sibling, device_id_type=MESH)

        def chip_sum(p):
            return pltpu.make_async_remote_copy(src_ref=sbuf.at[1], dst_ref=rwin_o.at[2 + p], send_sem=isend.at[p], recv_sem=irecv.at[p],
                                                device_id=_peer(x, y, cc, 2 * (p + 1)), device_id_type=MESH)

        def relay(h):
            return pltpu.make_async_remote_copy(src_ref=sbuf.at[1, pl.ds(h * half, half), :], dst_ref=rbuf.at[h], send_sem=dsend.at[h],
                                                recv_sem=drecv.at[h], device_id=_peer(x, y, cc, 2 * (h + 1)), device_id_type=MESH)

        to_sibling = pltpu.make_async_remote_copy(src_ref=sbuf.at[0], dst_ref=rwin_o.at[1], send_sem=sibsem.at[0], recv_sem=sibsem.at[1],
                                                  device_id=sibling, device_id_type=MESH)
        own = pltpu.make_async_copy(sbuf.at[1], rwin_o.at[0], lsem)

        @pl.when((s == 0) & (i == 0))
        def _():
            for ref in (dmx_o, dmc_o, gnw_o):
                ref[...] = jnp.zeros_like(ref)

        @pl.when(s == 0)
        def _():
            hx, _, _, _ = _modulated(_ctx_or_x(i, ctx_ref, x_ref), nw_ref[...], shift, scale)
            hx_sc[i] = hx.astype(BF16)

        @pl.when(i == 0)
        def _():
            acc[...] = jnp.zeros_like(acc)

        dgb = dg_ref[...]
        hxb = hx_sc[i]
        for r in range(0, D, 256):
            acc[r:r + 256, :] += _dot_ta(hxb[:, r:r + 256], dgb)
        for r in range(0, D, 512):
            part = _dot_tb(dgb, w_ref[0, r:r + 512, :])

            @pl.when(s == 0)
            def _():
                dhx_sc[i, :, r:r + 512] = part

            @pl.when(s > 0)
            def _():
                dhx_sc[i, :, r:r + 512] += part

        done = i == NT - 1

        def hand_over(p, before):
            before.wait_send()
            sbuf[0] = acc[...].astype(BF16)
            partial(p).start()

        def send_chip_sum(p, before):
            for cp in before:
                cp.wait_send()
            partial(p).wait_recv()
            sbuf[1] = (acc[...] + pbuf[p].astype(F32)).astype(BF16)
            h = 1 - p
            rows = pl.ds(h * half, half)
            relay(h).wait_recv()
            sbuf[1, rows, :] = (acc[rows, :] + pbuf[p, rows, :].astype(F32) + rbuf[h].astype(F32)).astype(BF16)
            chip_sum(p).start()

        @pl.when(done & (s == 0))
        def _():
            sbuf[0] = acc[...].astype(BF16)
            partial(2).start()

        @pl.when(done & (s == 1))
        def _():
            partial(2).wait_recv()
            sbuf[1] = (acc[...] + pbuf[2].astype(F32)).astype(BF16)
            for h in range(2):
                relay(h).start()

        for core in range(2):
            @pl.when(done & (cc == core) & (s == 2))
            def _(core=core):
                hand_over(core, partial(2))

            @pl.when(done & (cc == core) & (s == 3))
            def _(core=core):
                send_chip_sum(1 - core, [relay(0), relay(1)])

            @pl.when(done & (cc == core) & (s == 4))
            def _(core=core):
                hand_over(1 - core, partial(core))

            @pl.when(done & (cc == core) & (s == 5))
            def _(core=core):
                send_chip_sum(core, [chip_sum(1 - core)])

            @pl.when(done & (cc == core) & (s == last_s - 1))
            def _(core=core):
                partial(1 - core).wait_send()
                sbuf[0] = acc[...].astype(BF16)
                to_sibling.start()

            @pl.when(done & (cc == core) & (s == last_s))
            def _(core=core):
                chip_sum(core).wait_send()
                sbuf[1] = acc[...].astype(BF16)
                own.start()

        @pl.when(s == last_s)
        def _():
            nw = nw_ref[...]
            _, r, xn, a = _modulated(_ctx_or_x(i, ctx_ref, x_ref), nw, shift, scale)
            dhx = dhx_sc[i]
            dsh, dsc = _colsum(dhx), _colsum(dhx * a)
            da = dhx * (1.0 + scale)
            gnw_o[...] += _colsum(da * xn)
            dxn = da * nw
            gx_ref[...] = dx1_ref[...] + r * (dxn - xn * jnp.mean(dxn * xn, axis=-1, keepdims=True))

            @pl.when(i == 0)
            def _():
                dmc_o[0:1, :] += dsh
                dmc_o[1:2, :] += dsc

            @pl.when(i > 0)
            def _():
                dmx_o[0:1, :] += dsh
                dmx_o[1:2, :] += dsc

        @pl.when((i == NT - 1) & (s == last_s))
        def _():
            to_sibling.wait_send()
            to_sibling.wait_recv()
            for p in range(2):
                chip_sum(p).wait_recv()
            own.wait()

    grid_spec = pltpu.PrefetchScalarGridSpec(
        num_scalar_prefetch=1, grid=(NDEV, NT),
        in_specs=[VMEM_SPEC,
                  pl.BlockSpec((TM, D), lambda s, i, ix: (jnp.where((s == 0) | (s == last_s), jnp.maximum(i - 1, 0), NTX - 1), 0)),
                  pl.BlockSpec((TM, D), lambda s, i, ix: (jnp.where(s == last_s, jnp.maximum(i - 1, 0), 0), 0)),
                  pl.BlockSpec((TM, SH_WIN), lambda s, i, ix: (i, ix[0] ^ _scatter_order(s, ix[0] & 1))), VMEM_SPEC,
                  pl.BlockSpec((1, 2, D), lambda s, i, ix: (jnp.minimum(i, 1), 0, 0)),
                  pl.BlockSpec((1, D, SH_WIN), lambda s, i, ix: (ix[0] ^ _scatter_order(s, ix[0] & 1), 0, 0))],
        out_specs=[pl.BlockSpec((TM, D), lambda s, i, ix: (jnp.where(s == last_s, jnp.maximum(i - 1, 0), 0), 0)),
                   HBM_SPEC, VMEM_SPEC, VMEM_SPEC, VMEM_SPEC],
        scratch_shapes=[pltpu.VMEM((NT, TM, D), BF16), pltpu.VMEM((NT, TM, D), F32), pltpu.VMEM((D, SH_WIN), F32),
                        pltpu.VMEM((2, D, SH_WIN), BF16), pltpu.VMEM((3, D, SH_WIN), BF16), pltpu.VMEM((2, half, SH_WIN), BF16),
                        pltpu.SemaphoreType.DMA((3,)), pltpu.SemaphoreType.DMA((3,)), pltpu.SemaphoreType.DMA((2,)),
                        pltpu.SemaphoreType.DMA((2,)), pltpu.SemaphoreType.DMA((2,)), pltpu.SemaphoreType.DMA((2,)),
                        pltpu.SemaphoreType.DMA((2,)), pltpu.SemaphoreType.DMA])
    return pl.pallas_call(
        body, name="b1_in_bwd", grid_spec=grid_spec,
        out_shape=(_sds((T, D), F32), _sds((WIN_SLOTS, D, SH_WIN), BF16), _sds((2, D), F32), _sds((2, D), F32), _sds((1, D), F32)),
        compiler_params=pltpu.CompilerParams(dimension_semantics=("arbitrary", "arbitrary"), vmem_limit_bytes=VMEM_LIMIT),
    )(idx1, ctx, x, dx1, dg, nw, msel, win)


def _reduce_small(pd, pv, cg, c_ctx, ada_w0):
    n_arr = 3

    def body(pd_r, pv_r, cg_r, cctx_r, ada_r, gada_o, gadab_o, gcctx_o, pvsum_o, loss_o,
             pd_all, pv_all, dsc_all, dsc_mine, ssem, rsem):
        x, y, cc, idx = _mesh_pos()
        srcs = [pd_r, pv_r, dsc_mine]
        dsts = [pd_all.at[idx], pv_all.at[idx], dsc_all.at[idx]]

        def remote(a, k):
            return pltpu.make_async_remote_copy(src_ref=srcs[a], dst_ref=dsts[a], send_sem=ssem.at[a, k], recv_sem=rsem.at[a, k],
                                                device_id=_peer(x, y, cc, k), device_id_type=MESH)

        first = [remote(a, k) for k in range(1, NDEV) for a in (0, 1)]
        for cp in first:
            cp.start()
        pd_all[idx] = pd_r[...]
        pv_all[idx] = pv_r[...]
        for k in range(1, NDEV):
            remote(0, k).wait_recv()
            remote(1, k).wait_recv()
        mine = [pd_all[s, :, pl.ds(idx, 1), :] for s in range(NDEV)]
        dmc = functools.reduce(lambda u, v: u + v, [m[2] for m in mine])
        rows = _stack_rows([cg_r[i] for i in range(NDEV)] + [cctx_r[...]])
        sc = (rows * _sigmoid(rows)).astype(BF16)
        gada_o[0] = _dot_ta(sc, _stack_rows([m[0] for m in mine] + [dmc]))
        gada_o[1] = _dot_ta(sc, _stack_rows([m[1] for m in mine]))
        dsc_mine[...] = _dot_tb(jnp.broadcast_to(dmc, (8, SH_ADA)), ada_r[...])[0:1, :]
        dsc_all[idx] = dsc_mine[...]
        second = [remote(2, k) for k in range(1, NDEV)]
        for cp in second:
            cp.start()
        tot = [functools.reduce(lambda u, v: u + v, [pd_all[s, l] for s in range(NDEV)]) for l in range(3)]
        gadab_o[0] = tot[0] + tot[2]
        gadab_o[1] = tot[1]
        pvs = functools.reduce(lambda u, v: u + v, [pv_all[s] for s in range(NDEV)])
        pvsum_o[...] = pvs
        loss_o[...] = jnp.broadcast_to(jnp.sum(pvs[:, PV_LOSS:PV_LOSS + D], axis=-1, keepdims=True) * (0.5 / D), (1, 128))
        for k in range(1, NDEV):
            remote(2, k).wait_recv()
        dsc = functools.reduce(lambda u, v: u + v, [dsc_all[s] for s in range(NDEV)])
        cx = cctx_r[...]
        sx = _sigmoid(cx)
        gcctx_o[...] = dsc * (sx * (1.0 + cx * (1.0 - sx)))
        for cp in first + second:
            cp.wait_send()

    outs = (_sds((2, D, SH_ADA), F32), _sds((2, NDEV, SH_ADA), F32), _sds((1, D), F32), _sds((1, PV_LEN), F32), _sds((1, 128), F32))
    return pl.pallas_call(
        body, name="reduce_small", out_shape=outs,
        in_specs=[VMEM_SPEC] * 5, out_specs=[VMEM_SPEC] * 5,
        scratch_shapes=[
            pltpu.VMEM((NDEV, 3, NDEV, SH_ADA), F32), pltpu.VMEM((NDEV, 1, PV_LEN), F32), pltpu.VMEM((NDEV, 1, D), F32),
            pltpu.VMEM((1, D), F32),
            pltpu.SemaphoreType.DMA((n_arr, NDEV)), pltpu.SemaphoreType.DMA((n_arr, NDEV)),
        ],
        compiler_params=pltpu.CompilerParams(vmem_limit_bytes=VMEM_LIMIT),
    )(pd, pv, cg, c_ctx, ada_w0)


PV_NW, PV_GNORM, PV_FINAL, PV_LB, PV_PSCALE, PV_LOSS, PV_LEN = 0, 2 * D, 3 * D, 4 * D, 6 * D, 7 * D, 8 * D


def _adamw(w, g, m, v):
    m = ADAM_B1 * m + (1.0 - ADAM_B1) * g
    v = ADAM_B2 * v + (1.0 - ADAM_B2) * (g * g)
    m_hat = m / (1.0 - ADAM_B1 ** ADAM_STEP)
    v_hat = v / (1.0 - ADAM_B2 ** ADAM_STEP)
    delta = -ADAM_LR * (m_hat / (jnp.sqrt(v_hat) + ADAM_EPS) + ADAM_WD * w)
    return delta, m, v


ADAM_STEPS = 8


def _adam_all(sharded, dense, small, lb_idx, lbv):
    ns, nd, nsm = len(sharded), len(dense), len(small)

    def body(*refs):
        it = iter(refs)
        sh_in = [[next(it) for _ in range(4)] for _ in range(ns)]
        de_in = [[next(it) for _ in range(4)] for _ in range(nd)]
        sm_in = [[next(it) for _ in range(4)] for _ in range(nsm)]
        lb_r = next(it)
        sh_out = [[next(it) for _ in range(4)] for _ in range(ns)]
        de_out = [[next(it) for _ in range(3)] for _ in range(nd)]
        sm_out = [[next(it) for _ in range(4)] for _ in range(nsm)]
        for (p, w, m, v), outs in zip(sh_in, sh_out):
            g = p[0].astype(F32)
            for s in range(1, p.shape[0]):
                g = g + p[s].astype(F32)
            d, mn, vn = _adamw(w[...], g, m[...], v[...])
            outs[0][...], outs[1][...], outs[2][...], outs[3][...] = g, d, mn, vn
        for (g, w, m, v), outs in zip(de_in, de_out):
            d, mn, vn = _adamw(w[...], g[...], m[...], v[...])
            outs[0][...], outs[1][...], outs[2][...] = d, mn, vn

        @pl.when(pl.program_id(0) == 0)
        def _():
            for j, ((g, w, m, v), outs) in enumerate(zip(sm_in, sm_out)):
                gj = g[...]
                if j == lb_idx:
                    gj = gj * lb_r[...] * (1.0 - lb_r[...])
                d, mn, vn = _adamw(w[...], gj, m[...], v[...])
                outs[0][...], outs[1][...], outs[2][...], outs[3][...] = gj, d, mn, vn

    def tile(a):
        return pl.BlockSpec((a.shape[0] // ADAM_STEPS, a.shape[1]), lambda i: (i, 0))

    in_specs, out_specs, out_shape, args = [], [], [], []
    for p, w, m, v in sharded:
        in_specs += [pl.BlockSpec((p.shape[0], p.shape[1] // ADAM_STEPS, p.shape[2]), lambda i: (0, i, 0))] + [tile(w)] * 3
        args += [p, w, m, v]
    for g, w, m, v in dense:
        in_specs += [tile(w)] * 4
        args += [g, w, m, v]
    for g, w, m, v in small:
        in_specs += [VMEM_SPEC] * 4
        args += [g, w, m, v]
    in_specs.append(VMEM_SPEC)
    args.append(lbv)
    for _, w, _, _ in sharded:
        out_specs += [tile(w)] * 4
        out_shape += [_sds(w.shape, F32)] * 4
    for _, w, _, _ in dense:
        out_specs += [tile(w)] * 3
        out_shape += [_sds(w.shape, F32)] * 3
    for _, w, _, _ in small:
        out_specs += [VMEM_SPEC] * 4
        out_shape += [_sds(w.shape, F32)] * 4
    res = pl.pallas_call(body, name="adam_all", grid=(ADAM_STEPS,), in_specs=in_specs, out_specs=out_specs, out_shape=tuple(out_shape),
                         compiler_params=pltpu.CompilerParams(dimension_semantics=("arbitrary",), vmem_limit_bytes=VMEM_LIMIT))(*args)
    it = iter(res)
    return ([tuple(next(it) for _ in range(4)) for _ in range(ns)], [tuple(next(it) for _ in range(3)) for _ in range(nd)],
            [tuple(next(it) for _ in range(4)) for _ in range(nsm)])


def kernel(x, c, ctx, c_ctx, ada_w, ada_b, norm_w, hgrn_w_in, hgrn_lb_logits, hgrn_gnorm_w, hgrn_w_out, pool_w_in, pool_w_grp, pool_scale, pool_w_out, final_norm_w, loss_target, m_c_ctx, m_ada_w, m_ada_b, m_norm_w, m_hgrn_w_in, m_hgrn_lb_logits, m_hgrn_gnorm_w, m_hgrn_w_out, m_pool_w_in, m_pool_w_grp, m_pool_scale, m_pool_w_out, m_final_norm_w, v_c_ctx, v_ada_w, v_ada_b, v_norm_w, v_hgrn_w_in, v_hgrn_lb_logits, v_hgrn_gnorm_w, v_hgrn_w_out, v_pool_w_in, v_pool_w_grp, v_pool_scale, v_pool_w_out, v_final_norm_w):
    idx = 4 * lax.axis_index("x") + 2 * lax.axis_index("y") + lax.axis_index("c")
    cctx2 = c_ctx.reshape(1, D)
    cum01, mask01 = _gla_consts()
    pb, pbt, pinv = _pool_consts()

    idx1 = idx.reshape(1).astype(jnp.int32)
    nw0, nw1 = norm_w[0:1], norm_w[1:2]
    fnw = final_norm_w.reshape(1, D)
    g_all, win, s_wout, s_pwin, s_pgrp, s_pwout, lbl_g, ps_g, cg, mod0, mod1, modc = _f1_gather_matmul(
        idx1, ctx[0], x[0], nw0, hgrn_w_in[0], hgrn_w_out[0], pool_w_in[0], pool_w_grp[0], pool_w_out[0], hgrn_lb_logits[0],
        pool_scale, c, cctx2, ada_w, ada_b)
    lb = jax.nn.sigmoid(jnp.transpose(lbl_g, (1, 0, 2)).reshape(2, E))
    pscale = ps_g.reshape(1, E)
    msel = jnp.stack([modc[:2], mod0[:2]])
    p0, p1, v_all, dec, wout, pgrp = _gla_prep(g_all, lb, cum01, s_wout, s_pgrp)
    o, pwin, pwout = _gla_fwd(p0, p1, v_all, dec, mask01, s_pwin, s_pwout)
    x1 = _f3_out(o, g_all, x[0], mod0[2:3], hgrn_gnorm_w, wout)
    dx1, gpwin, gpgrp, gpwout, dmod1, gnw1, gfw, gps, lossv = _pool_layer(
        x1, loss_target[0], mod1, nw1, fnw, pwin, pgrp, pscale, pwout, pb, pbt, pinv)
    do, dz, gwout, dgate0, ggw, rpwout = _b3_out_bwd(dx1, o, g_all, mod0[2:3], hgrn_gnorm_w, wout, gpwout)
    d0, d1, dv, dgl, rpwin, rpgrp = _gla_bwd(p0, p1, v_all, dec, do, mask01, gpwin, gpgrp)
    dg, dlb, rwout = _gla_post_bwd(g_all, d0, d1, dgl, dv, dz, lb, cum01, gwout)
    grad_x, rwin, dmx, dmc, gnw0 = _b1_in_bwd(idx1, ctx[0], x[0], dx1, dg, nw0, msel, win)

    dmod0 = jnp.concatenate([dmx, dgate0], axis=0)
    dmodc = jnp.concatenate([dmc, jnp.zeros((1, D), F32)], axis=0)
    pd = jnp.stack([dmod0, dmod1, dmodc]).reshape(3, NDEV, SH_ADA)
    pv = jnp.concatenate([gnw0, gnw1, ggw, gfw, dlb.reshape(1, 2 * E), gps, lossv], axis=1)
    g_ada, g_adab, g_cctx, pvsum, loss128 = _reduce_small(pd, pv, cg, cctx2, ada_w[0])

    g2 = (4 * SH_GRP, PG)
    sharded_names = ["hgrn_w_in", "hgrn_w_out", "pool_w_in", "pool_w_grp", "pool_w_out"]
    sharded = [(rwin, hgrn_w_in[0], m_hgrn_w_in[0], v_hgrn_w_in[0]),
               (rwout, hgrn_w_out[0], m_hgrn_w_out[0], v_hgrn_w_out[0]),
               (rpwin, pool_w_in[0], m_pool_w_in[0], v_pool_w_in[0]),
               (rpgrp.reshape((NDEV,) + g2), pool_w_grp[0].reshape(g2), m_pool_w_grp[0].reshape(g2), v_pool_w_grp[0].reshape(g2)),
               (rpwout, pool_w_out[0], m_pool_w_out[0], v_pool_w_out[0])]
    a2 = (2 * D, SH_ADA)
    g_ada2 = g_ada.reshape(a2)
    dense = [(g_ada2, ada_w.reshape(a2), m_ada_w.reshape(a2), v_ada_w.reshape(a2))]
    lb_me = lax.dynamic_slice_in_dim(lb, idx * DH, DH, axis=1)
    small_names = ["c_ctx", "ada_b", "norm_w", "hgrn_lb_logits", "hgrn_gnorm_w", "pool_scale", "final_norm_w"]
    small = [(g_cctx, cctx2, m_c_ctx.reshape(1, D), v_c_ctx.reshape(1, D)),
             (g_adab.reshape(2, 3 * D), ada_b, m_ada_b, v_ada_b),
             (pvsum[:, PV_NW:PV_NW + 2 * D].reshape(2, D), norm_w, m_norm_w, v_norm_w),
             (lax.dynamic_slice_in_dim(pvsum[:, PV_LB:PV_LB + 2 * E].reshape(2, E), idx * DH, DH, axis=1),
              hgrn_lb_logits[0], m_hgrn_lb_logits[0], v_hgrn_lb_logits[0]),
             (pvsum[:, PV_GNORM:PV_GNORM + E], hgrn_gnorm_w, m_hgrn_gnorm_w, v_hgrn_gnorm_w),
             (lax.dynamic_slice_in_dim(pvsum[:, PV_PSCALE:PV_PSCALE + E], idx * DH, DH, axis=1), pool_scale, m_pool_scale, v_pool_scale),
             (pvsum[:, PV_FINAL:PV_FINAL + D], fnw, m_final_norm_w.reshape(1, D), v_final_norm_w.reshape(1, D))]
    r_sharded, r_dense, r_small = _adam_all(sharded, dense, small, 3, lb_me)
    out = dict(zip(sharded_names, r_sharded))
    out["ada_w"] = (g_ada2,) + r_dense[0]
    out.update(zip(small_names, r_small))

    shapes = {"c_ctx": (D,), "ada_w": (2, D, SH_ADA), "ada_b": (2, 3 * D), "norm_w": (2, D), "hgrn_w_in": (1, D, SH_WIN),
              "hgrn_lb_logits": (1, 2, DH), "hgrn_gnorm_w": (1, E), "hgrn_w_out": (1, SH_ROWS, D), "pool_w_in": (1, D, SH_PWIN),
              "pool_w_grp": (1, 4, SH_GRP, PG), "pool_scale": (1, DH), "pool_w_out": (1, SH_ROWS, D), "final_norm_w": (D,)}
    order = ["c_ctx", "ada_w", "ada_b", "norm_w", "hgrn_w_in", "hgrn_lb_logits", "hgrn_gnorm_w", "hgrn_w_out", "pool_w_in",
             "pool_w_grp", "pool_scale", "pool_w_out", "final_norm_w"]
    flat = [out[name][q].reshape(shapes[name]) for q in range(4) for name in order]
    return (loss128[0, 0], grad_x[None], *flat)
```
